```python
import math
import jax, jax.numpy as jnp
from jax import lax
import numpy as np

D_MODEL = 1024
BATCH = 8
SEQ = 4096
DEPTH = 2

MIX_WIDTH = D_MODEL
A_WIDTH = MIX_WIDTH // 2
A_GROUPS = 4
A_GROUP_DIM = A_WIDTH // A_GROUPS
CHUNK = 128
B_WIDTH = MIX_WIDTH - A_WIDTH
HEAD_DIM = 64
B_HEADS = B_WIDTH // HEAD_DIM
ROT_DIM = HEAD_DIM // 4
ROPE_THETA = 500000.0
DILATED_PATTERNS = ((128, 1), (512, 4), (2048, 16))
IN_COLS = 2 * A_WIDTH + 3 * B_WIDTH
D_FF = 4 * D_MODEL
CONV_WIDTH = 3
EPS = 1e-6
NEG_INF = -1e30

kernel_name = 'hybrid_gmlp_dilated_attn_convffn'


def rmsnorm(x, g):
    xf = x.astype(jnp.float32)
    y = xf * lax.rsqrt(jnp.mean(xf * xf, axis=-1, keepdims=True) + EPS)
    return (y * g.astype(jnp.float32)).astype(x.dtype)


def layernorm(x, g, b):
    xf = x.astype(jnp.float32)
    mu = jnp.mean(xf, axis=-1, keepdims=True)
    xc = xf - mu
    y = xc * lax.rsqrt(jnp.mean(xc * xc, axis=-1, keepdims=True) + EPS)
    return (y * g.astype(jnp.float32) + b.astype(jnp.float32)).astype(x.dtype)


def partial_rope(x):
    s = x.shape[1]
    half = ROT_DIM // 2
    inv = ROPE_THETA ** (-jnp.arange(0, ROT_DIM, 2, dtype=jnp.float32) / ROT_DIM)
    ang = jnp.arange(s, dtype=jnp.float32)[:, None] * inv[None, :]
    cos = jnp.cos(ang)[None, :, None, :]
    sin = jnp.sin(ang)[None, :, None, :]
    xf = x.astype(jnp.float32)
    x1, x2 = xf[..., :half], xf[..., half:ROT_DIM]
    out = jnp.concatenate([x1 * cos - x2 * sin, x2 * cos + x1 * sin, xf[..., ROT_DIM:]], axis=-1)
    return out.astype(x.dtype)


def dilated_branch(q, k, v, window, dilation):
    bsz, s, h, dh = q.shape
    band = window // dilation
    n = s // dilation
    nb = -(-n // band)
    pad = nb * band - n

    def to_blocks(t):
        t = t.reshape(bsz, n, dilation, h, dh).transpose(0, 2, 1, 3, 4)
        t = jnp.pad(t, ((0, 0), (0, 0), (0, pad), (0, 0), (0, 0)))
        return t.reshape(bsz, dilation, nb, band, h, dh)

    def with_prev(t):
        prev = jnp.pad(t[:, :, :-1], ((0, 0), (0, 0), (1, 0), (0, 0), (0, 0), (0, 0)))
        return jnp.concatenate([prev, t], axis=3)

    qb = to_blocks(q)
    kc = with_prev(to_blocks(k))
    vc = with_prev(to_blocks(v))
    scores = jnp.einsum('brnqhd,brnkhd->brnhqk', qb, kc).astype(jnp.float32) * (dh ** -0.5)
    qi = jnp.arange(band)[:, None]
    kj = jnp.arange(2 * band)[None, :]
    dist = qi + band - kj
    blk = jnp.arange(nb)[:, None, None]
    valid = (dist >= 0) & (dist <= band) & (blk * band + kj - band >= 0)
    scores = jnp.where(valid[None, None, :, None], scores, NEG_INF)
    m = jnp.max(scores, axis=-1, keepdims=True)
    p = jnp.exp(scores - m)
    l = jnp.sum(p, axis=-1, keepdims=True)
    o = jnp.einsum('brnhqk,brnkhd->brnqhd', p.astype(v.dtype), vc).astype(jnp.float32)
    l_t = l[..., 0].transpose(0, 1, 2, 4, 3)
    lse = (m[..., 0] + jnp.log(l[..., 0])).transpose(0, 1, 2, 4, 3)
    o = o / l_t[..., None]
    o = o.reshape(bsz, dilation, nb * band, h, dh)[:, :, :n]
    o = o.transpose(0, 2, 1, 3, 4).reshape(bsz, s, h, dh)
    lse = lse.reshape(bsz, dilation, nb * band, h)[:, :, :n]
    lse = lse.transpose(0, 2, 1, 3).reshape(bsz, s, h)
    return o, lse


def mixer_spatial_gating(za, v_norm_g, v_norm_b, w_spatial, b_spatial):
    bsz, s, _ = za.shape
    za = jax.nn.gelu(za, approximate=False)
    u, va = za[..., :A_WIDTH], za[..., A_WIDTH:]
    va = layernorm(va, v_norm_g, v_norm_b)
    vch = va.reshape(bsz, s // CHUNK, CHUNK, A_GROUPS, A_GROUP_DIM)
    ws = w_spatial * jnp.tril(jnp.ones((CHUNK, CHUNK), dtype=w_spatial.dtype))
    sg = jnp.einsum('gpq,bnqgc->bnpgc', ws, vch) + b_spatial.T[None, None, :, :, None]
    return u * sg.reshape(bsz, s, A_WIDTH)


def mixer_dilated_attention(zb):
    bsz, s, _ = zb.shape
    qkv = zb.reshape(bsz, s, 3, B_HEADS, HEAD_DIM)
    q = partial_rope(qkv[:, :, 0])
    k = partial_rope(qkv[:, :, 1])
    v = qkv[:, :, 2]
    outs, lses = zip(*[dilated_branch(q, k, v, w, d) for (w, d) in DILATED_PATTERNS])
    alpha = jax.nn.softmax(jnp.stack(lses, axis=0), axis=0)
    o = jnp.sum(alpha[..., None] * jnp.stack(outs, axis=0), axis=0)
    return o.reshape(bsz, s, B_WIDTH).astype(zb.dtype)


def conv_ffn(h, w_up, conv_w, conv_b, w_down):
    s = h.shape[1]
    up = jnp.einsum('bsd,df->bsf', h, w_up)
    up_pad = jnp.pad(up, ((0, 0), (CONV_WIDTH - 1, 0), (0, 0)))
    conv = conv_b + sum(conv_w[i] * up_pad[:, i:i + s] for i in range(CONV_WIDTH))
    gate, val = conv[..., :D_FF], conv[..., D_FF:]
    y = jax.nn.gelu(gate, approximate=True) * val
    return jnp.einsum('bsf,fd->bsd', y, w_down)


def _fwd_setup_inputs(seed: int = 0) -> dict:
    key = jax.random.key(seed)
    ks = jax.random.split(key, 18)
    f32 = jnp.float32

    def nrm(k, shape, scale):
        return jax.random.normal(k, shape, f32) * scale

    def gain(k, shape):
        return 1.0 + 0.05 * jax.random.normal(k, shape, f32)

    L = DEPTH
    return {
        'x': jax.random.normal(ks[0], (BATCH, SEQ, D_MODEL), f32),
        'pre_mix_norm': gain(ks[1], (L, D_MODEL)),
        'w_in': nrm(ks[2], (L, D_MODEL, IN_COLS), D_MODEL ** -0.5),
        'v_norm_g': gain(ks[3], (L, A_WIDTH)),
        'v_norm_b': nrm(ks[4], (L, A_WIDTH), 0.02),
        'w_spatial': nrm(ks[5], (L, A_GROUPS, CHUNK, CHUNK), CHUNK ** -0.5),
        'b_spatial': gain(ks[6], (L, A_GROUPS, CHUNK)),
        'out_norm_a': gain(ks[7], (L, A_WIDTH)),
        'out_norm_b': gain(ks[8], (L, B_WIDTH)),
        'w_out': nrm(ks[9], (L, MIX_WIDTH, D_MODEL), MIX_WIDTH ** -0.5),
        'post_mix_norm': gain(ks[10], (L, D_MODEL)),
        'pre_ffn_norm': gain(ks[11], (L, D_MODEL)),
        'w_up': nrm(ks[12], (L, D_MODEL, 2 * D_FF), D_MODEL ** -0.5),
        'conv_w': nrm(ks[13], (L, CONV_WIDTH, 2 * D_FF), CONV_WIDTH ** -0.5),
        'conv_b': nrm(ks[14], (L, 2 * D_FF), 0.02),
        'w_down': nrm(ks[15], (L, D_FF, D_MODEL), D_FF ** -0.5),
        'post_ffn_norm': gain(ks[16], (L, D_MODEL)),
    }


def _fwd_reference(x, pre_mix_norm, w_in, v_norm_g, v_norm_b, w_spatial, b_spatial,
              out_norm_a, out_norm_b, w_out, post_mix_norm, pre_ffn_norm,
              w_up, conv_w, conv_b, w_down, post_ffn_norm):
    for l in range(DEPTH):
        h = rmsnorm(x, pre_mix_norm[l])
        proj = jnp.einsum('bsd,de->bse', h, w_in[l])
        o_a = mixer_spatial_gating(proj[..., :2 * A_WIDTH], v_norm_g[l], v_norm_b[l],
                                   w_spatial[l], b_spatial[l])
        o_b = mixer_dilated_attention(proj[..., 2 * A_WIDTH:])
        mixed = jnp.concatenate([rmsnorm(o_a, out_norm_a[l]), rmsnorm(o_b, out_norm_b[l])], axis=-1)
        y = jnp.einsum('bse,ed->bsd', mixed, w_out[l])
        x = x + rmsnorm(y, post_mix_norm[l])
        h = rmsnorm(x, pre_ffn_norm[l])
        f = conv_ffn(h, w_up[l], conv_w[l], conv_b[l], w_down[l])
        x = x + rmsnorm(f, post_ffn_norm[l])
    return x


import jax as _jax
import jax.numpy as _jnp

TWIN_FORMAT = 'train_step'
FWD_PARAMS = ['x', 'pre_mix_norm', 'w_in', 'v_norm_g', 'v_norm_b', 'w_spatial', 'b_spatial', 'out_norm_a', 'out_norm_b', 'w_out', 'post_mix_norm', 'pre_ffn_norm', 'w_up', 'conv_w', 'conv_b', 'w_down', 'post_ffn_norm']
TWIN_WEIGHTS = ['pre_mix_norm', 'w_in', 'v_norm_g', 'v_norm_b', 'w_spatial', 'b_spatial', 'out_norm_a', 'out_norm_b', 'w_out', 'post_mix_norm', 'pre_ffn_norm', 'w_up', 'conv_w', 'conv_b', 'w_down', 'post_ffn_norm']
TWIN_DIFF_INPUT = 'x'
TWIN_INPUTS = ['x', 'pre_mix_norm', 'w_in', 'v_norm_g', 'v_norm_b', 'w_spatial', 'b_spatial', 'out_norm_a', 'out_norm_b', 'w_out', 'post_mix_norm', 'pre_ffn_norm', 'w_up', 'conv_w', 'conv_b', 'w_down', 'post_ffn_norm', 'loss_target', 'm_pre_mix_norm', 'm_w_in', 'm_v_norm_g', 'm_v_norm_b', 'm_w_spatial', 'm_b_spatial', 'm_out_norm_a', 'm_out_norm_b', 'm_w_out', 'm_post_mix_norm', 'm_pre_ffn_norm', 'm_w_up', 'm_conv_w', 'm_conv_b', 'm_w_down', 'm_post_ffn_norm', 'v_pre_mix_norm', 'v_w_in', 'v_v_norm_g', 'v_v_norm_b', 'v_w_spatial', 'v_b_spatial', 'v_out_norm_a', 'v_out_norm_b', 'v_w_out', 'v_post_mix_norm', 'v_pre_ffn_norm', 'v_w_up', 'v_conv_w', 'v_conv_b', 'v_w_down', 'v_post_ffn_norm']
TWIN_OUTPUTS = ['loss', 'grad_x', 'grad_pre_mix_norm', 'grad_w_in', 'grad_v_norm_g', 'grad_v_norm_b', 'grad_w_spatial', 'grad_b_spatial', 'grad_out_norm_a', 'grad_out_norm_b', 'grad_w_out', 'grad_post_mix_norm', 'grad_pre_ffn_norm', 'grad_w_up', 'grad_conv_w', 'grad_conv_b', 'grad_w_down', 'grad_post_ffn_norm', 'delta_pre_mix_norm', 'delta_w_in', 'delta_v_norm_g', 'delta_v_norm_b', 'delta_w_spatial', 'delta_b_spatial', 'delta_out_norm_a', 'delta_out_norm_b', 'delta_w_out', 'delta_post_mix_norm', 'delta_pre_ffn_norm', 'delta_w_up', 'delta_conv_w', 'delta_conv_b', 'delta_w_down', 'delta_post_ffn_norm', 'new_m_pre_mix_norm', 'new_m_w_in', 'new_m_v_norm_g', 'new_m_v_norm_b', 'new_m_w_spatial', 'new_m_b_spatial', 'new_m_out_norm_a', 'new_m_out_norm_b', 'new_m_w_out', 'new_m_post_mix_norm', 'new_m_pre_ffn_norm', 'new_m_w_up', 'new_m_conv_w', 'new_m_conv_b', 'new_m_w_down', 'new_m_post_ffn_norm', 'new_v_pre_mix_norm', 'new_v_w_in', 'new_v_v_norm_g', 'new_v_v_norm_b', 'new_v_w_spatial', 'new_v_b_spatial', 'new_v_out_norm_a', 'new_v_out_norm_b', 'new_v_w_out', 'new_v_post_mix_norm', 'new_v_pre_ffn_norm', 'new_v_w_up', 'new_v_conv_w', 'new_v_conv_b', 'new_v_w_down', 'new_v_post_ffn_norm']
TWIN_LEAF_KINDS = {'loss': 'loss', 'grad_x': 'grad_x', 'grad_pre_mix_norm': 'grad_w', 'grad_w_in': 'grad_w', 'grad_v_norm_g': 'grad_w', 'grad_v_norm_b': 'grad_w', 'grad_w_spatial': 'grad_w', 'grad_b_spatial': 'grad_w', 'grad_out_norm_a': 'grad_w', 'grad_out_norm_b': 'grad_w', 'grad_w_out': 'grad_w', 'grad_post_mix_norm': 'grad_w', 'grad_pre_ffn_norm': 'grad_w', 'grad_w_up': 'grad_w', 'grad_conv_w': 'grad_w', 'grad_conv_b': 'grad_w', 'grad_w_down': 'grad_w', 'grad_post_ffn_norm': 'grad_w', 'delta_pre_mix_norm': 'delta_w', 'delta_w_in': 'delta_w', 'delta_v_norm_g': 'delta_w', 'delta_v_norm_b': 'delta_w', 'delta_w_spatial': 'delta_w', 'delta_b_spatial': 'delta_w', 'delta_out_norm_a': 'delta_w', 'delta_out_norm_b': 'delta_w', 'delta_w_out': 'delta_w', 'delta_post_mix_norm': 'delta_w', 'delta_pre_ffn_norm': 'delta_w', 'delta_w_up': 'delta_w', 'delta_conv_w': 'delta_w', 'delta_conv_b': 'delta_w', 'delta_w_down': 'delta_w', 'delta_post_ffn_norm': 'delta_w', 'new_m_pre_mix_norm': 'new_m', 'new_m_w_in': 'new_m', 'new_m_v_norm_g': 'new_m', 'new_m_v_norm_b': 'new_m', 'new_m_w_spatial': 'new_m', 'new_m_b_spatial': 'new_m', 'new_m_out_norm_a': 'new_m', 'new_m_out_norm_b': 'new_m', 'new_m_w_out': 'new_m', 'new_m_post_mix_norm': 'new_m', 'new_m_pre_ffn_norm': 'new_m', 'new_m_w_up': 'new_m', 'new_m_conv_w': 'new_m', 'new_m_conv_b': 'new_m', 'new_m_w_down': 'new_m', 'new_m_post_ffn_norm': 'new_m', 'new_v_pre_mix_norm': 'new_v', 'new_v_w_in': 'new_v', 'new_v_v_norm_g': 'new_v', 'new_v_v_norm_b': 'new_v', 'new_v_w_spatial': 'new_v', 'new_v_b_spatial': 'new_v', 'new_v_out_norm_a': 'new_v', 'new_v_out_norm_b': 'new_v', 'new_v_w_out': 'new_v', 'new_v_post_mix_norm': 'new_v', 'new_v_pre_ffn_norm': 'new_v', 'new_v_w_up': 'new_v', 'new_v_conv_w': 'new_v', 'new_v_conv_b': 'new_v', 'new_v_w_down': 'new_v', 'new_v_post_ffn_norm': 'new_v'}


def _forward(args):
    return _fwd_reference(*[args[k] for k in FWD_PARAMS])


def _output_shape():
    out = _jax.eval_shape(lambda: _forward(_fwd_setup_inputs(0)))
    return out.shape, out.dtype

N_MICROBATCH = 1
ADAM_LR = 0.001
ADAM_B1 = 0.9
ADAM_B2 = 0.999
ADAM_EPS = 1e-08
ADAM_WD = 0.01
ADAM_STEP = 10
PER_EXAMPLE_BATCH_AXIS = {'x': 0, 'loss_target': 0}
SHARED_INPUTS = []
_WEIGHT_DTYPES = {'pre_mix_norm': _jnp.float32, 'w_in': _jnp.float32, 'v_norm_g': _jnp.float32, 'v_norm_b': _jnp.float32, 'w_spatial': _jnp.float32, 'b_spatial': _jnp.float32, 'out_norm_a': _jnp.float32, 'out_norm_b': _jnp.float32, 'w_out': _jnp.float32, 'post_mix_norm': _jnp.float32, 'pre_ffn_norm': _jnp.float32, 'w_up': _jnp.float32, 'conv_w': _jnp.float32, 'conv_b': _jnp.float32, 'w_down': _jnp.float32, 'post_ffn_norm': _jnp.float32}
MOMENT_SCALE = {'pre_mix_norm': 2.954746e+00, 'w_in': 1.924652e+00, 'v_norm_g': 3.686430e-01, 'v_norm_b': 5.782955e-01, 'w_spatial': 3.819400e-01, 'b_spatial': 5.393302e-01, 'out_norm_a': 7.170602e+00, 'out_norm_b': 3.601295e+00, 'w_out': 5.689194e+00, 'post_mix_norm': 3.409313e+01, 'pre_ffn_norm': 2.144910e+00, 'w_up': 8.029241e-01, 'conv_w': 9.124442e-01, 'conv_b': 4.104104e+00, 'w_down': 1.982027e+00, 'post_ffn_norm': 3.215235e+01}


def _to_microbatches(a, axis):
    t = _jnp.moveaxis(a, axis, 0)
    t = t.reshape((N_MICROBATCH, t.shape[0] // N_MICROBATCH) + t.shape[1:])
    return _jnp.moveaxis(t, 1, axis + 1)


def setup_inputs(seed: int = 0) -> dict:
    inp = _fwd_setup_inputs(seed)
    key = _jax.random.fold_in(_jax.random.key(seed), 7919)
    shape, _ = _output_shape()
    out = dict(inp)
    out["loss_target"] = _jax.random.normal(_jax.random.fold_in(key, 0), shape, _jnp.float32)
    for i, name in enumerate(TWIN_WEIGHTS):
        w = inp[name].astype(_jnp.float32)
        if MOMENT_SCALE is None:
            s = _jnp.sqrt(_jnp.mean(_jnp.square(w)) + 1e-30)
        else:
            s = MOMENT_SCALE[name]
        km, kv = _jax.random.split(_jax.random.fold_in(key, i + 1))
        out[name] = w
        out["m_" + name] = s * _jax.random.normal(km, w.shape, _jnp.float32)
        out["v_" + name] = (s * s) * _jax.random.uniform(kv, w.shape, _jnp.float32, 0.5, 1.5)
    if N_MICROBATCH > 1:
        for name, axis in PER_EXAMPLE_BATCH_AXIS.items():
            out[name] = _to_microbatches(out[name], axis)
    return {'x': out['x'], 'pre_mix_norm': out['pre_mix_norm'], 'w_in': out['w_in'], 'v_norm_g': out['v_norm_g'], 'v_norm_b': out['v_norm_b'], 'w_spatial': out['w_spatial'], 'b_spatial': out['b_spatial'], 'out_norm_a': out['out_norm_a'], 'out_norm_b': out['out_norm_b'], 'w_out': out['w_out'], 'post_mix_norm': out['post_mix_norm'], 'pre_ffn_norm': out['pre_ffn_norm'], 'w_up': out['w_up'], 'conv_w': out['conv_w'], 'conv_b': out['conv_b'], 'w_down': out['w_down'], 'post_ffn_norm': out['post_ffn_norm'], 'loss_target': out['loss_target'], 'm_pre_mix_norm': out['m_pre_mix_norm'], 'm_w_in': out['m_w_in'], 'm_v_norm_g': out['m_v_norm_g'], 'm_v_norm_b': out['m_v_norm_b'], 'm_w_spatial': out['m_w_spatial'], 'm_b_spatial': out['m_b_spatial'], 'm_out_norm_a': out['m_out_norm_a'], 'm_out_norm_b': out['m_out_norm_b'], 'm_w_out': out['m_w_out'], 'm_post_mix_norm': out['m_post_mix_norm'], 'm_pre_ffn_norm': out['m_pre_ffn_norm'], 'm_w_up': out['m_w_up'], 'm_conv_w': out['m_conv_w'], 'm_conv_b': out['m_conv_b'], 'm_w_down': out['m_w_down'], 'm_post_ffn_norm': out['m_post_ffn_norm'], 'v_pre_mix_norm': out['v_pre_mix_norm'], 'v_w_in': out['v_w_in'], 'v_v_norm_g': out['v_v_norm_g'], 'v_v_norm_b': out['v_v_norm_b'], 'v_w_spatial': out['v_w_spatial'], 'v_b_spatial': out['v_b_spatial'], 'v_out_norm_a': out['v_out_norm_a'], 'v_out_norm_b': out['v_out_norm_b'], 'v_w_out': out['v_w_out'], 'v_post_mix_norm': out['v_post_mix_norm'], 'v_pre_ffn_norm': out['v_pre_ffn_norm'], 'v_w_up': out['v_w_up'], 'v_conv_w': out['v_conv_w'], 'v_conv_b': out['v_conv_b'], 'v_w_down': out['v_w_down'], 'v_post_ffn_norm': out['v_post_ffn_norm']}


def _loss(weights, diff, rest, loss_target):
    with _jax.named_scope("forward"):
        args = {**rest, TWIN_DIFF_INPUT: diff, **{k: w.astype(_WEIGHT_DTYPES[k]) for k, w in weights.items()}}
        y = _forward(args)
    with _jax.named_scope("loss_head"):
        err = _jnp.square(y.astype(_jnp.float32) - loss_target)
        return 0.5 * _jnp.sum(_jnp.mean(err, axis=-1)) if err.ndim else 0.5 * err


def _adamw(w, g, m, v):
    m = ADAM_B1 * m + (1.0 - ADAM_B1) * g
    v = ADAM_B2 * v + (1.0 - ADAM_B2) * _jnp.square(g)
    m_hat = m / (1.0 - ADAM_B1 ** ADAM_STEP)
    v_hat = v / (1.0 - ADAM_B2 ** ADAM_STEP)
    delta = -ADAM_LR * (m_hat / (_jnp.sqrt(v_hat) + ADAM_EPS) + ADAM_WD * w)
    return delta, m, v


def reference(x, pre_mix_norm, w_in, v_norm_g, v_norm_b, w_spatial, b_spatial, out_norm_a, out_norm_b, w_out, post_mix_norm, pre_ffn_norm, w_up, conv_w, conv_b, w_down, post_ffn_norm, loss_target, m_pre_mix_norm, m_w_in, m_v_norm_g, m_v_norm_b, m_w_spatial, m_b_spatial, m_out_norm_a, m_out_norm_b, m_w_out, m_post_mix_norm, m_pre_ffn_norm, m_w_up, m_conv_w, m_conv_b, m_w_down, m_post_ffn_norm, v_pre_mix_norm, v_w_in, v_v_norm_g, v_v_norm_b, v_w_spatial, v_b_spatial, v_out_norm_a, v_out_norm_b, v_w_out, v_post_mix_norm, v_pre_ffn_norm, v_w_up, v_conv_w, v_conv_b, v_w_down, v_post_ffn_norm):
    given = dict(x=x, pre_mix_norm=pre_mix_norm, w_in=w_in, v_norm_g=v_norm_g, v_norm_b=v_norm_b, w_spatial=w_spatial, b_spatial=b_spatial, out_norm_a=out_norm_a, out_norm_b=out_norm_b, w_out=w_out, post_mix_norm=post_mix_norm, pre_ffn_norm=pre_ffn_norm, w_up=w_up, conv_w=conv_w, conv_b=conv_b, w_down=w_down, post_ffn_norm=post_ffn_norm, loss_target=loss_target, m_pre_mix_norm=m_pre_mix_norm, m_w_in=m_w_in, m_v_norm_g=m_v_norm_g, m_v_norm_b=m_v_norm_b, m_w_spatial=m_w_spatial, m_b_spatial=m_b_spatial, m_out_norm_a=m_out_norm_a, m_out_norm_b=m_out_norm_b, m_w_out=m_w_out, m_post_mix_norm=m_post_mix_norm, m_pre_ffn_norm=m_pre_ffn_norm, m_w_up=m_w_up, m_conv_w=m_conv_w, m_conv_b=m_conv_b, m_w_down=m_w_down, m_post_ffn_norm=m_post_ffn_norm, v_pre_mix_norm=v_pre_mix_norm, v_w_in=v_w_in, v_v_norm_g=v_v_norm_g, v_v_norm_b=v_v_norm_b, v_w_spatial=v_w_spatial, v_b_spatial=v_b_spatial, v_out_norm_a=v_out_norm_a, v_out_norm_b=v_out_norm_b, v_w_out=v_w_out, v_post_mix_norm=v_post_mix_norm, v_pre_ffn_norm=v_pre_ffn_norm, v_w_up=v_w_up, v_conv_w=v_conv_w, v_conv_b=v_conv_b, v_w_down=v_w_down, v_post_ffn_norm=v_post_ffn_norm)
    weights = {n: given[n] for n in TWIN_WEIGHTS}
    shared = {n: given[n] for n in SHARED_INPUTS}
    per_example = {n: given[n] for n in ['x']}
    grad_fn = _jax.value_and_grad(_loss, argnums=(0, 1))

    def one_microbatch(ex, loss_target):
        ex = dict(ex)
        diff = ex.pop(TWIN_DIFF_INPUT)
        return grad_fn(weights, diff, {**shared, **ex}, loss_target)

    if N_MICROBATCH == 1:
        loss, (grad_w, grad_x) = one_microbatch(per_example, given["loss_target"])
    else:
        def body(carry, xs):
            loss_sum, grad_sum = carry
            l_k, (gw_k, gx_k) = one_microbatch(xs[0], xs[1])
            with _jax.named_scope("update"):
                return (loss_sum + l_k, _jax.tree.map(_jnp.add, grad_sum, gw_k)), gx_k

        init = (_jnp.zeros((), _jnp.float32), _jax.tree.map(_jnp.zeros_like, weights))
        (loss, grad_w), grad_x = _jax.lax.scan(body, init, (per_example, given["loss_target"]))
    with _jax.named_scope("update"):
        delta_w, new_m, new_v = {}, {}, {}
        for n in TWIN_WEIGHTS:
            delta_w[n], new_m[n], new_v[n] = _adamw(weights[n], grad_w[n], given["m_" + n], given["v_" + n])
    return (loss, grad_x, *[grad_w[n] for n in TWIN_WEIGHTS], *[delta_w[n] for n in TWIN_WEIGHTS],
            *[new_m[n] for n in TWIN_WEIGHTS], *[new_v[n] for n in TWIN_WEIGHTS])
```

```python
import functools
import math

import jax
import jax.numpy as jnp
import numpy as np
from jax import lax
from jax.experimental import pallas as pl
from jax.experimental.pallas import tpu as pltpu

F32 = jnp.float32
BF16 = jnp.bfloat16
MESH = pl.DeviceIdType.MESH

D_MODEL = 1024
A_WIDTH = 512
A_GROUPS = 4
GROUP_DIM = 128
CHUNK = 128
B_WIDTH = 512
HEAD_DIM = 64
ROT_DIM = 16
ROPE_THETA = 500000.0
DILATIONS = (1, 4, 16)
BAND = 128
IN_COLS = 2560
D_FF = 4096
EPS = 1e-6
NEG_INF = -1e30
N_CHIPS = 4
LANES = 128

ADAM_LR = 0.001
ADAM_B1 = 0.9
ADAM_B2 = 0.999
ADAM_EPS = 1e-08
ADAM_WD = 0.01
ADAM_STEP = 10

VMEM_LIMIT_BYTES = 56 * 1024 * 1024
RSQRT2 = 0.7071067811865476
INV_SQRT_2PI = 0.3989422804014327
GELU_C = 0.7978845608028654
GELU_A = 0.044715

NN = ((1,), (0,))
NT = ((1,), (1,))
TN = ((0,), (0,))


def _cparams(*sem):
    return pltpu.CompilerParams(dimension_semantics=sem, vmem_limit_bytes=VMEM_LIMIT_BYTES)


def _dot(a, b, dims):
    return lax.dot_general(a, b, (dims, ((), ())), preferred_element_type=F32)


def _rsq_mean(a):
    return lax.rsqrt(jnp.mean(a * a, axis=-1, keepdims=True) + EPS)


def _rms_bwd(a, r, g, dz):
    t = dz * g
    da = r * t - a * (r * r * r) * jnp.mean(t * a, axis=-1, keepdims=True)
    return da, dz * a * r


def _colsum(a):
    return jnp.sum(a, axis=0, keepdims=True)


def _gelu_tanh(x):
    t = jnp.tanh(GELU_C * (x + GELU_A * x * x * x))
    return 0.5 * x * (1.0 + t), t


def _gelu_tanh_grad(x, t):
    return 0.5 * (1.0 + t) + 0.5 * x * (1.0 - t * t) * GELU_C * (1.0 + 3.0 * GELU_A * x * x)


def _matmul(a, b, *, grid, a_spec, b_spec, o_spec, o_shape, o_dtype, dims, nk, kaxis, acc_shape, name):
    def body(a_ref, b_ref, o_ref, *scratch):
        part = _dot(a_ref[...], b_ref[...], dims)
        if nk == 1:
            o_ref[...] = part.astype(o_dtype)
        else:
            acc = scratch[0]
            k = pl.program_id(kaxis)

            @pl.when(k == 0)
            def _():
                acc[...] = part

            @pl.when(k > 0)
            def _():
                acc[...] += part

            @pl.when(k == nk - 1)
            def _():
                o_ref[...] = acc[...].astype(o_dtype)

    sem = tuple("arbitrary" if (nk > 1 and ax == kaxis) else "parallel" for ax in range(len(grid)))
    return pl.pallas_call(
        body, grid=grid, in_specs=[a_spec, b_spec], out_specs=o_spec,
        out_shape=jax.ShapeDtypeStruct(o_shape, o_dtype),
        scratch_shapes=[pltpu.VMEM(acc_shape, F32)] if nk > 1 else [],
        compiler_params=_cparams(*sem), name=name)(a, b)


TM = 512


TR = 256


def _row_spec(width, col=0):
    return pl.BlockSpec((TR, width), lambda i, col=col: (i, col))


def _vec_spec(width):
    return pl.BlockSpec((1, width), lambda i: (0, 0))


def _rms_cast(x, g, name):
    s, d = x.shape

    def body(x_ref, g_ref, h_ref):
        a = x_ref[...]
        h_ref[...] = (a * _rsq_mean(a) * g_ref[...]).astype(BF16)

    return pl.pallas_call(
        body, grid=(s // TR,), in_specs=[_row_spec(d), _vec_spec(d)], out_specs=_row_spec(d),
        out_shape=jax.ShapeDtypeStruct((s, d), BF16), compiler_params=_cparams("parallel"), name=name)(x, g)


def _residual_norm(x0, y, g_post, g_next, name):
    s, d = x0.shape

    def body(x_ref, y_ref, gp_ref, gn_ref, x1_ref, h_ref):
        yv = y_ref[...]
        x1 = x_ref[...] + yv * _rsq_mean(yv) * gp_ref[...]
        x1_ref[...] = x1
        h_ref[...] = (x1 * _rsq_mean(x1) * gn_ref[...]).astype(BF16)

    return pl.pallas_call(
        body, grid=(s // TR,), in_specs=[_row_spec(d), _row_spec(d), _vec_spec(d), _vec_spec(d)],
        out_specs=[_row_spec(d), _row_spec(d)],
        out_shape=[jax.ShapeDtypeStruct((s, d), F32), jax.ShapeDtypeStruct((s, d), BF16)],
        compiler_params=_cparams("parallel"), name=name)(x0, y, g_post, g_next)


def _residual_loss(x1, f, g_post, target, name):
    s, d = x1.shape

    def body(x_ref, f_ref, gp_ref, t_ref, loss_ref, dx_ref):
        fv = f_ref[...]
        err = x_ref[...] + fv * _rsq_mean(fv) * gp_ref[...] - t_ref[...]
        dx_ref[...] = err * (1.0 / d)
        part = 0.5 * jnp.sum(jnp.mean(err * err, axis=-1, keepdims=True), axis=0, keepdims=True)

        @pl.when(pl.program_id(0) == 0)
        def _():
            loss_ref[...] = jnp.zeros_like(loss_ref)

        loss_ref[...] += jnp.broadcast_to(part, loss_ref.shape)

    return pl.pallas_call(
        body, grid=(s // TR,), in_specs=[_row_spec(d), _row_spec(d), _vec_spec(d), _row_spec(d)],
        out_specs=[pl.BlockSpec((8, LANES), lambda i: (0, 0)), _row_spec(d)],
        out_shape=[jax.ShapeDtypeStruct((8, LANES), F32), jax.ShapeDtypeStruct((s, d), F32)],
        compiler_params=_cparams("arbitrary"), name=name)(x1, f, g_post, target)


def _acc_init(refs):
    @pl.when(pl.program_id(0) == 0)
    def _():
        for r in refs:
            r[...] = jnp.zeros_like(r)


def _norm_bwd_out(dx, f, g_post, name):
    s, d = dx.shape

    def body(dx_ref, f_ref, g_ref, df_ref, dg_ref):
        _acc_init([dg_ref])
        fv = f_ref[...]
        dz = dx_ref[...]
        da, dgt = _rms_bwd(fv, _rsq_mean(fv), g_ref[...], dz)
        df_ref[...] = da.astype(BF16)
        dg_ref[...] += _colsum(dgt)

    return pl.pallas_call(
        body, grid=(s // TR,), in_specs=[_row_spec(d), _row_spec(d), _vec_spec(d)],
        out_specs=[_row_spec(d), _vec_spec(d)],
        out_shape=[jax.ShapeDtypeStruct((s, d), BF16), jax.ShapeDtypeStruct((1, d), F32)],
        compiler_params=_cparams("arbitrary"), name=name)(dx, f, g_post)


def _norm_bwd_mid(dx2, dh2, x1, g_pf, y1, g_pm, name):
    s, d = dx2.shape

    def body(dx2_ref, dh_ref, x1_ref, gpf_ref, y1_ref, gpm_ref, dx1_ref, dy1_ref, dgpf_ref, dgpm_ref):
        _acc_init([dgpf_ref, dgpm_ref])
        x1 = x1_ref[...]
        da, dgt = _rms_bwd(x1, _rsq_mean(x1), gpf_ref[...], dh_ref[...])
        dx1 = dx2_ref[...] + da
        dx1_ref[...] = dx1
        dgpf_ref[...] += _colsum(dgt)
        y1 = y1_ref[...]
        dy, dgt2 = _rms_bwd(y1, _rsq_mean(y1), gpm_ref[...], dx1)
        dy1_ref[...] = dy.astype(BF16)
        dgpm_ref[...] += _colsum(dgt2)

    return pl.pallas_call(
        body, grid=(s // TR,),
        in_specs=[_row_spec(d), _row_spec(d), _row_spec(d), _vec_spec(d), _row_spec(d), _vec_spec(d)],
        out_specs=[_row_spec(d), _row_spec(d), _vec_spec(d), _vec_spec(d)],
        out_shape=[jax.ShapeDtypeStruct((s, d), F32), jax.ShapeDtypeStruct((s, d), BF16),
                   jax.ShapeDtypeStruct((1, d), F32), jax.ShapeDtypeStruct((1, d), F32)],
        compiler_params=_cparams("arbitrary"), name=name)(dx2, dh2, x1, g_pf, y1, g_pm)


def _norm_bwd_in(dx1, dh1, x0, g1, name):
    s, d = dx1.shape

    def body(dx1_ref, dh_ref, x0_ref, g_ref, dx0_ref, dg_ref):
        _acc_init([dg_ref])
        x0 = x0_ref[...]
        da, dgt = _rms_bwd(x0, _rsq_mean(x0), g_ref[...], dh_ref[...])
        dx0_ref[...] = dx1_ref[...] + da
        dg_ref[...] += _colsum(dgt)

    return pl.pallas_call(
        body, grid=(s // TR,), in_specs=[_row_spec(d), _row_spec(d), _row_spec(d), _vec_spec(d)],
        out_specs=[_row_spec(d), _vec_spec(d)],
        out_shape=[jax.ShapeDtypeStruct((s, d), F32), jax.ShapeDtypeStruct((1, d), F32)],
        compiler_params=_cparams("arbitrary"), name=name)(dx1, dh1, x0, g1)


def _tril_mask():
    row = lax.broadcasted_iota(jnp.int32, (CHUNK, CHUNK), 0)
    col = lax.broadcasted_iota(jnp.int32, (CHUNK, CHUNK), 1)
    return row >= col


def _gating_forward(pa, gv, bv, wt, bsf):
    er = lax.erf(pa * RSQRT2)
    za = 0.5 * pa * (1.0 + er)
    u = za[:, :A_WIDTH]
    va = za[:, A_WIDTH:]
    xc = va - jnp.mean(va, axis=-1, keepdims=True)
    rs = lax.rsqrt(jnp.mean(xc * xc, axis=-1, keepdims=True) + EPS)
    vn = xc * rs
    vlb = (vn * gv + bv).astype(BF16)
    sg = jnp.concatenate(
        [_dot(wt[g], vlb[:, g * GROUP_DIM:(g + 1) * GROUP_DIM], NN) for g in range(A_GROUPS)], axis=1) + bsf
    return er, u, rs, vn, vlb, sg


def _masked_ws(ws_ref):
    mask = _tril_mask()
    return [jnp.where(mask, ws_ref[g], 0.0).astype(BF16) for g in range(A_GROUPS)]


def _mixer_a_fwd(proj, gv, bv, ws, bsf, ga, name):
    s = proj.shape[0]

    def body(p_ref, gv_ref, bv_ref, ws_ref, bs_ref, ga_ref, o_ref):
        wt = _masked_ws(ws_ref)
        for ch in range(TR // CHUNK):
            rows = slice(ch * CHUNK, (ch + 1) * CHUNK)
            _, u, _, _, _, sg = _gating_forward(p_ref[rows, :], gv_ref[...], bv_ref[...], wt, bs_ref[...])
            oa = u * sg
            o_ref[rows, :] = (oa * _rsq_mean(oa) * ga_ref[...]).astype(BF16)

    return pl.pallas_call(
        body, grid=(s // TR,),
        in_specs=[_row_spec(2 * A_WIDTH), _vec_spec(A_WIDTH), _vec_spec(A_WIDTH),
                  pl.BlockSpec((A_GROUPS, CHUNK, CHUNK), lambda i: (0, 0, 0)),
                  pl.BlockSpec((CHUNK, A_WIDTH), lambda i: (0, 0)), _vec_spec(A_WIDTH)],
        out_specs=_row_spec(A_WIDTH), out_shape=jax.ShapeDtypeStruct((s, A_WIDTH), BF16),
        compiler_params=_cparams("parallel"), name=name)(proj, gv, bv, ws, bsf, ga)


def _mixer_a_bwd(proj, dmixed, gv, bv, ws, bsf, ga, name):
    s = proj.shape[0]
    nsteps = s // TR

    def body(p_ref, dm_ref, gv_ref, bv_ref, ws_ref, bs_ref, ga_ref,
             dp_ref, dga_ref, dgv_ref, dbv_ref, dbs_ref, dws_ref):
        _acc_init([dga_ref, dgv_ref, dbv_ref, dbs_ref, dws_ref])
        mask = _tril_mask()
        wt = _masked_ws(ws_ref)
        gvv = gv_ref[...]
        gav = ga_ref[...]
        for ch in range(TR // CHUNK):
            rows = slice(ch * CHUNK, (ch + 1) * CHUNK)
            pa = p_ref[rows, :]
            er, u, rs, vn, vlb, sg = _gating_forward(pa, gvv, bv_ref[...], wt, bs_ref[...])
            oa = u * sg
            doa, dgt = _rms_bwd(oa, _rsq_mean(oa), gav, dm_ref[rows, :])
            dga_ref[...] += _colsum(dgt)
            du = doa * sg
            dsg = doa * u
            dbs_ref[...] += dsg
            dsgb = dsg.astype(BF16)
            dvl = []
            for g in range(A_GROUPS):
                cols = slice(g * GROUP_DIM, (g + 1) * GROUP_DIM)
                dws_ref[g] += jnp.where(mask, _dot(dsgb[:, cols], vlb[:, cols], NT), 0.0)
                dvl.append(_dot(wt[g], dsgb[:, cols], TN))
            dvl = jnp.concatenate(dvl, axis=1)
            dgv_ref[...] += _colsum(dvl * vn)
            dbv_ref[...] += _colsum(dvl)
            dvn = dvl * gvv
            dva = rs * (dvn - jnp.mean(dvn, axis=-1, keepdims=True)
                        - vn * jnp.mean(dvn * vn, axis=-1, keepdims=True))
            gp = 0.5 * (1.0 + er) + pa * jnp.exp(-0.5 * pa * pa) * INV_SQRT_2PI
            dp_ref[rows, :] = (jnp.concatenate([du, dva], axis=1) * gp).astype(BF16)

        @pl.when(pl.program_id(0) == nsteps - 1)
        def _():
            for g in range(A_GROUPS):
                cols = slice(g * GROUP_DIM, (g + 1) * GROUP_DIM)
                tot = jnp.sum(dbs_ref[:, cols], axis=1, keepdims=True)
                dbs_ref[:, cols] = jnp.broadcast_to(tot, (CHUNK, GROUP_DIM))

    full = lambda *shape: pl.BlockSpec(shape, lambda i: (0,) * len(shape))
    return pl.pallas_call(
        body, grid=(nsteps,),
        in_specs=[_row_spec(2 * A_WIDTH), _row_spec(A_WIDTH), _vec_spec(A_WIDTH), _vec_spec(A_WIDTH),
                  full(A_GROUPS, CHUNK, CHUNK), full(CHUNK, A_WIDTH), _vec_spec(A_WIDTH)],
        out_specs=[_row_spec(2 * A_WIDTH), _vec_spec(A_WIDTH), _vec_spec(A_WIDTH), _vec_spec(A_WIDTH),
                   full(CHUNK, A_WIDTH), full(A_GROUPS, CHUNK, CHUNK)],
        out_shape=[jax.ShapeDtypeStruct((s, 2 * A_WIDTH), BF16), jax.ShapeDtypeStruct((1, A_WIDTH), F32),
                   jax.ShapeDtypeStruct((1, A_WIDTH), F32), jax.ShapeDtypeStruct((1, A_WIDTH), F32),
                   jax.ShapeDtypeStruct((CHUNK, A_WIDTH), F32),
                   jax.ShapeDtypeStruct((A_GROUPS, CHUNK, CHUNK), F32)],
        compiler_params=_cparams("arbitrary"), name=name)(proj, dmixed, gv, bv, ws, bsf, ga)


def _rope_tables(s):
    half = ROT_DIM // 2
    inv = ROPE_THETA ** (-jnp.arange(0, ROT_DIM, 2, dtype=F32) / ROT_DIM)
    ang = jnp.arange(s, dtype=F32)[:, None] * inv[None, :]
    cos, sin = jnp.cos(ang), jnp.sin(ang)
    zeros = jnp.zeros((s, HEAD_DIM - ROT_DIM), F32)
    zh = jnp.zeros((s, half), F32)
    c = jnp.concatenate([cos, cos, zeros + 1.0], axis=1)
    s1 = jnp.concatenate([-sin, zh, zeros], axis=1)
    s2 = jnp.concatenate([zh, sin, zeros], axis=1)
    return tuple(jnp.concatenate([t, t], axis=1) for t in (c, s1, s2))


def _lane_blocks(width):
    return [slice(b * LANES, (b + 1) * LANES) for b in range(width // LANES)]


def _rope_fwd(proj, tabs, name):
    s = proj.shape[0]
    half = ROT_DIM // 2
    scale = HEAD_DIM ** -0.5

    def body(q_ref, k_ref, v_ref, c_ref, s1_ref, s2_ref, qo_ref, ko_ref, vo_ref):
        c, s1, s2 = c_ref[...], s1_ref[...], s2_ref[...]
        for sl in _lane_blocks(B_WIDTH):
            for src, dst, mul in ((q_ref, qo_ref, scale), (k_ref, ko_ref, 1.0)):
                a = src[:, sl]
                r = a * c + pltpu.roll(a, LANES - half, 1) * s1 + pltpu.roll(a, half, 1) * s2
                dst[:, sl] = (r * mul).astype(BF16)
        vo_ref[...] = v_ref[...].astype(BF16)

    tab = pl.BlockSpec((TR, LANES), lambda i: (i, 0))
    return pl.pallas_call(
        body, grid=(s // TR,),
        in_specs=[_row_spec(B_WIDTH, 2), _row_spec(B_WIDTH, 3), _row_spec(B_WIDTH, 4), tab, tab, tab],
        out_specs=[_row_spec(B_WIDTH)] * 3, out_shape=[jax.ShapeDtypeStruct((s, B_WIDTH), BF16)] * 3,
        compiler_params=_cparams("parallel"), name=name)(proj, proj, proj, *tabs)


def _band_mask(i):
    qi = lax.broadcasted_iota(jnp.int32, (BAND, 2 * BAND), 0)
    kj = lax.broadcasted_iota(jnp.int32, (BAND, 2 * BAND), 1)
    return (kj >= qi) & (kj <= qi + BAND) & ((kj >= BAND) | (i > 0))


def _head_masks():
    lane = lax.broadcasted_iota(jnp.int32, (1, LANES), 1)
    return lane < HEAD_DIM, lane >= HEAD_DIM


def _attn_specs(nb, last):
    cur = pl.BlockSpec((BAND, B_WIDTH), lambda r, i: (jnp.minimum(i, last), r))
    prev = pl.BlockSpec((BAND, B_WIDTH), lambda r, i: (jnp.maximum(jnp.minimum(i, last) - 1, 0), r))
    return cur, prev


def _attn_fwd(q, k, v, dil, name):
    s = q.shape[0]
    n = s // dil
    nb = n // BAND
    q2, k2, v2 = (t.reshape(n, dil * B_WIDTH) for t in (q, k, v))

    def body(q_ref, kc_ref, kp_ref, vc_ref, vp_ref, o_ref, l_ref):
        valid = _band_mask(pl.program_id(1))
        lo, hi = _head_masks()
        for sl in _lane_blocks(B_WIDTH):
            qb = q_ref[:, sl]
            kk = jnp.concatenate([kp_ref[:, sl], kc_ref[:, sl]], axis=0)
            vv = jnp.concatenate([vp_ref[:, sl], vc_ref[:, sl]], axis=0)
            outs, lses = [], []
            for hm in (lo, hi):
                sc = _dot(jnp.where(hm, qb, jnp.zeros_like(qb)), kk, NT)
                sc = jnp.where(valid, sc, NEG_INF)
                mx = jnp.max(sc, axis=1, keepdims=True)
                p = jnp.exp(sc - mx)
                den = jnp.sum(p, axis=1, keepdims=True)
                outs.append(_dot(p.astype(BF16), vv, NN) / den)
                lses.append(mx + jnp.log(den))
            o_ref[:, sl] = jnp.where(lo, outs[0], outs[1])
            l_ref[:, sl] = jnp.where(lo, lses[0], lses[1])

    cur, prev = _attn_specs(nb, nb - 1)
    o, lse = pl.pallas_call(
        body, grid=(dil, nb), in_specs=[cur, cur, prev, cur, prev], out_specs=[cur, cur],
        out_shape=[jax.ShapeDtypeStruct((n, dil * B_WIDTH), F32)] * 2,
        compiler_params=_cparams("parallel", "parallel"), name=name)(q2, k2, k2, v2, v2)
    return o.reshape(s, B_WIDTH), lse.reshape(s, B_WIDTH)


def _attn_combine(outs, lses, gb, name):
    s = outs[0].shape[0]
    npat = len(outs)

    def body(*refs):
        o_refs, l_refs = refs[:npat], refs[npat:2 * npat]
        g_ref, ob_ref, lse_ref, mb_ref = refs[2 * npat:]
        ls = [r[...] for r in l_refs]
        mx = functools.reduce(jnp.maximum, ls)
        ws = [jnp.exp(l - mx) for l in ls]
        tot = functools.reduce(lambda a, b: a + b, ws)
        ob = functools.reduce(lambda a, b: a + b, [w / tot * r[...] for w, r in zip(ws, o_refs)])
        ob_ref[...] = ob
        lse_ref[...] = mx + jnp.log(tot)
        mb_ref[...] = (ob * _rsq_mean(ob) * g_ref[...]).astype(BF16)

    w = B_WIDTH
    return pl.pallas_call(
        body, grid=(s // TR,), in_specs=[_row_spec(w)] * (2 * npat) + [_vec_spec(w)],
        out_specs=[_row_spec(w)] * 3,
        out_shape=[jax.ShapeDtypeStruct((s, w), F32), jax.ShapeDtypeStruct((s, w), F32),
                   jax.ShapeDtypeStruct((s, w), BF16)],
        compiler_params=_cparams("parallel"), name=name)(*outs, *lses, gb)


def _attn_bwd_prep(dmixed, ob, gb, name):
    s = ob.shape[0]
    w = B_WIDTH

    def body(dm_ref, ob_ref, g_ref, do_ref, dl_ref, dg_ref):
        _acc_init([dg_ref])
        ob = ob_ref[...]
        dob, dgt = _rms_bwd(ob, _rsq_mean(ob), g_ref[...], dm_ref[...])
        dg_ref[...] += _colsum(dgt)
        do_ref[...] = dob.astype(BF16)
        lo, hi = _head_masks()
        t = dob * ob
        for sl in _lane_blocks(w):
            tb = t[:, sl]
            s0 = jnp.sum(jnp.where(lo, tb, 0.0), axis=1, keepdims=True)
            s1 = jnp.sum(jnp.where(hi, tb, 0.0), axis=1, keepdims=True)
            dl_ref[:, sl] = jnp.where(lo, s0, s1)

    return pl.pallas_call(
        body, grid=(s // TR,), in_specs=[_row_spec(w, 1), _row_spec(w), _vec_spec(w)],
        out_specs=[_row_spec(w), _row_spec(w), _vec_spec(w)],
        out_shape=[jax.ShapeDtypeStruct((s, w), BF16), jax.ShapeDtypeStruct((s, w), F32),
                   jax.ShapeDtypeStruct((1, w), F32)],
        compiler_params=_cparams("arbitrary"), name=name)(dmixed, ob, gb)


def _attn_bwd(q, k, v, do, lse, delta, dil, name):
    s = q.shape[0]
    n = s // dil
    nb = n // BAND
    q2, k2, v2, do2, lse2, dl2 = (t.reshape(n, dil * B_WIDTH) for t in (q, k, v, do, lse, delta))

    def body(q_ref, kc_ref, kp_ref, vc_ref, vp_ref, do_ref, lse_ref, dl_ref,
             dq_ref, dk_ref, dv_ref, ck_ref, cv_ref):
        i = pl.program_id(1)

        @pl.when(i == 0)
        def _():
            ck_ref[...] = jnp.zeros_like(ck_ref)
            cv_ref[...] = jnp.zeros_like(cv_ref)

        @pl.when(i < nb)
        def _():
            valid = _band_mask(i)
            lo, hi = _head_masks()
            lane = lax.broadcasted_iota(jnp.int32, (1, LANES), 1)
            for sl in _lane_blocks(B_WIDTH):
                qb = q_ref[:, sl]
                dob = do_ref[:, sl]
                kk = jnp.concatenate([kp_ref[:, sl], kc_ref[:, sl]], axis=0)
                vv = jnp.concatenate([vp_ref[:, sl], vc_ref[:, sl]], axis=0)
                lseb = lse_ref[:, sl]
                dlb = dl_ref[:, sl]
                dq = jnp.zeros((BAND, LANES), F32)
                dkk = jnp.zeros((2 * BAND, LANES), F32)
                dvv = jnp.zeros((2 * BAND, LANES), F32)
                for hm, first in ((lo, 0), (hi, HEAD_DIM)):
                    pick = lane == first
                    lse_h = jnp.sum(jnp.where(pick, lseb, 0.0), axis=1, keepdims=True)
                    dl_h = jnp.sum(jnp.where(pick, dlb, 0.0), axis=1, keepdims=True)
                    qm = jnp.where(hm, qb, jnp.zeros_like(qb))
                    dom = jnp.where(hm, dob, jnp.zeros_like(dob))
                    sc = _dot(qm, kk, NT)
                    p = jnp.where(valid, jnp.exp(sc - lse_h), 0.0)
                    dp = _dot(dom, vv, NT)
                    ds = (p * (dp - dl_h)).astype(BF16)
                    dq += _dot(ds, jnp.where(hm, kk, jnp.zeros_like(kk)), NN)
                    dkk += _dot(ds, qm, TN)
                    dvv += _dot(p.astype(BF16), dom, TN)
                dq_ref[:, sl] = dq
                dk_ref[:, sl] = ck_ref[:, sl] + dkk[:BAND]
                dv_ref[:, sl] = cv_ref[:, sl] + dvv[:BAND]
                ck_ref[:, sl] = dkk[BAND:]
                cv_ref[:, sl] = dvv[BAND:]

        @pl.when(i == nb)
        def _():
            dk_ref[...] = ck_ref[...]
            dv_ref[...] = cv_ref[...]

    cur, prev = _attn_specs(nb, nb - 1)
    lag = pl.BlockSpec((BAND, B_WIDTH), lambda r, i: (jnp.maximum(i - 1, 0), r))
    shape = jax.ShapeDtypeStruct((n, dil * B_WIDTH), F32)
    dq, dk, dv = pl.pallas_call(
        body, grid=(dil, nb + 1), in_specs=[cur, cur, prev, cur, prev, cur, cur, cur],
        out_specs=[cur, lag, lag], out_shape=[shape] * 3,
        scratch_shapes=[pltpu.VMEM((BAND, B_WIDTH), F32)] * 2,
        compiler_params=_cparams("arbitrary", "arbitrary"), name=name)(q2, k2, k2, v2, v2, do2, lse2, dl2)
    return tuple(t.reshape(s, B_WIDTH) for t in (dq, dk, dv))


def _rope_bwd(dqs, dks, dvs, tabs, name):
    s = dqs[0].shape[0]
    half = ROT_DIM // 2
    scale = HEAD_DIM ** -0.5
    npat = len(dqs)
    w = B_WIDTH

    def body(*refs):
        dq_refs, dk_refs, dv_refs = refs[:npat], refs[npat:2 * npat], refs[2 * npat:3 * npat]
        c_ref, s1_ref, s2_ref, o_ref = refs[3 * npat:]
        c, s1, s2 = c_ref[...], s1_ref[...], s2_ref[...]
        add = lambda rs, sl: functools.reduce(lambda a, b: a + b, [r[:, sl] for r in rs])
        for b, sl in enumerate(_lane_blocks(w)):
            for which, rs, mul in ((0, dq_refs, scale), (1, dk_refs, 1.0)):
                g = add(rs, sl) * mul
                o = g * c + pltpu.roll(g * s1, half, 1) + pltpu.roll(g * s2, LANES - half, 1)
                o_ref[:, which * w + b * LANES:which * w + (b + 1) * LANES] = o.astype(BF16)
            o_ref[:, 2 * w + b * LANES:2 * w + (b + 1) * LANES] = add(dv_refs, sl).astype(BF16)

    tab = pl.BlockSpec((TR, LANES), lambda i: (i, 0))
    return pl.pallas_call(
        body, grid=(s // TR,), in_specs=[_row_spec(w)] * (3 * npat) + [tab] * 3,
        out_specs=_row_spec(3 * w), out_shape=jax.ShapeDtypeStruct((s, 3 * w), BF16),
        compiler_params=_cparams("parallel"), name=name)(*dqs, *dks, *dvs, *tabs)


TK = 512
HALO = 8


def _shift_down(x, halo_ref, h, first, n):
    row = lax.broadcasted_iota(jnp.int32, (x.shape[0], 1), 0)
    out = pltpu.roll(x, n, 0)
    for j in range(n):
        edge = jnp.where(first, 0.0, halo_ref[h, HALO - n + j:HALO - n + j + 1, :])
        out = jnp.where(row == j, edge, out)
    return out


def _shift_up(x, halo_ref, h, last, n):
    rows = x.shape[0]
    row = lax.broadcasted_iota(jnp.int32, (rows, 1), 0)
    out = pltpu.roll(x, rows - n, 0)
    for j in range(n):
        edge = jnp.where(last, 0.0, halo_ref[h, j:j + 1, :])
        out = jnp.where(row == rows - n + j, edge, out)
    return out


def _conv_taps(up_ref, halo_ref, h, first):
    x = up_ref[h]
    return _shift_down(x, halo_ref, h, first, 2), _shift_down(x, halo_ref, h, first, 1), x


def _conv_value(taps, cw_ref, cb_ref, h):
    return (cb_ref[h] + cw_ref[h, 0:1, :] * taps[0] + cw_ref[h, 1:2, :] * taps[1]
            + cw_ref[h, 2:3, :] * taps[2])


def _ffn_specs(tm, ncol):
    up = pl.BlockSpec((2, tm, TK), lambda i, j: (0, i, j))
    halo = pl.BlockSpec((2, HALO, TK), lambda i, j: (0, jnp.maximum(i * (tm // HALO) - 1, 0), j))
    cw = pl.BlockSpec((2, 3, TK), lambda i, j: (0, 0, j))
    cb = pl.BlockSpec((2, 1, TK), lambda i, j: (0, 0, j))
    return up, halo, cw, cb


def _conv_glu(up3, cw3, cb3, name):
    s = up3.shape[1]
    ncol = D_FF // TK

    def body(up_ref, halo_ref, cw_ref, cb_ref, y_ref):
        first = pl.program_id(0) == 0
        gate = _conv_value(_conv_taps(up_ref, halo_ref, 0, first), cw_ref, cb_ref, 0)
        val = _conv_value(_conv_taps(up_ref, halo_ref, 1, first), cw_ref, cb_ref, 1)
        y_ref[...] = (_gelu_tanh(gate)[0] * val).astype(BF16)

    return pl.pallas_call(
        body, grid=(s // TM, ncol), in_specs=list(_ffn_specs(TM, ncol)),
        out_specs=pl.BlockSpec((TM, TK), lambda i, j: (i, j)),
        out_shape=jax.ShapeDtypeStruct((s, D_FF), BF16),
        compiler_params=_cparams("parallel", "parallel"), name=name)(up3, up3, cw3, cb3)


def _ffn_bwd_conv(df, w_down, layer, up3, cw3, cb3, name):
    s = df.shape[0]
    ncol = D_FF // TK
    per_shard = (D_FF // N_CHIPS) // TK

    def body(df_ref, wd_ref, up_ref, halo_ref, cw_ref, cb_ref, dc_ref, acc_ref):
        i, j = pl.program_id(0), pl.program_id(1)

        @pl.when((i == 0) & (j == 0))
        def _():
            acc_ref[...] = jnp.zeros_like(acc_ref)

        first = i == 0
        dy = _dot(df_ref[...], wd_ref[...], NT)
        taps_g = _conv_taps(up_ref, halo_ref, 0, first)
        taps_v = _conv_taps(up_ref, halo_ref, 1, first)
        gate = _conv_value(taps_g, cw_ref, cb_ref, 0)
        val = _conv_value(taps_v, cw_ref, cb_ref, 1)
        act, t = _gelu_tanh(gate)
        dgate = dy * val * _gelu_tanh_grad(gate, t)
        dval = dy * act
        dc_ref[0] = dgate
        dc_ref[1] = dval
        row = lax.broadcasted_iota(jnp.int32, (8, 1), 0)
        for h, dc, taps in ((0, dgate, taps_g), (1, dval, taps_v)):
            sums = [_colsum(dc * taps[0]), _colsum(dc * taps[1]), _colsum(dc * taps[2]), _colsum(dc)]
            upd = jnp.zeros((8, TK), F32)
            for ridx, sm in enumerate(sums):
                upd = jnp.where(row == ridx, sm, upd)
            acc_ref[j, h] += upd

    up, halo, cw, cb = _ffn_specs(TM, ncol)
    return pl.pallas_call(
        body, grid=(s // TM, ncol),
        in_specs=[pl.BlockSpec((TM, D_MODEL), lambda i, j: (i, 0)),
                  pl.BlockSpec((None, None, TK, D_MODEL), lambda i, j: (j // per_shard, layer, j % per_shard, 0)),
                  up, halo, cw, cb],
        out_specs=[pl.BlockSpec((2, TM, TK), lambda i, j: (0, i, j)),
                   pl.BlockSpec((ncol, 2, 8, TK), lambda i, j: (0, 0, 0, 0))],
        out_shape=[jax.ShapeDtypeStruct((2, s, D_FF), F32), jax.ShapeDtypeStruct((ncol, 2, 8, TK), F32)],
        compiler_params=_cparams("arbitrary", "arbitrary"), name=name)(df, w_down, up3, up3, cw3, cb3)


def _conv_transpose(dconv, cw3, name):
    s = dconv.shape[1]
    ncol = D_FF // TK
    nrow = s // TM

    def body(dc_ref, halo_ref, cw_ref, o_ref):
        last = pl.program_id(0) == nrow - 1
        for h in range(2):
            x = dc_ref[h]
            o_ref[h] = (cw_ref[h, 2:3, :] * x + cw_ref[h, 1:2, :] * _shift_up(x, halo_ref, h, last, 1)
                        + cw_ref[h, 0:1, :] * _shift_up(x, halo_ref, h, last, 2)).astype(BF16)

    blk = pl.BlockSpec((2, TM, TK), lambda i, j: (0, i, j))
    halo = pl.BlockSpec((2, HALO, TK), lambda i, j: (0, jnp.minimum((i + 1) * (TM // HALO), s // HALO - 1), j))
    return pl.pallas_call(
        body, grid=(nrow, ncol), in_specs=[blk, halo, pl.BlockSpec((2, 3, TK), lambda i, j: (0, 0, j))],
        out_specs=blk, out_shape=jax.ShapeDtypeStruct((2, s, D_FF), BF16),
        compiler_params=_cparams("parallel", "parallel"), name=name)(dconv, dconv, cw3)


def _wspec(rows, cols, index_map):
    return pl.BlockSpec((None, None, rows, cols), index_map)


def _layer_forward(l, x0, h1, p, wg, tabs):
    s = x0.shape[0]
    nm = s // TM
    tag = f"_l{l}"
    proj = _matmul(
        h1, wg["w_in"], grid=(nm, N_CHIPS), a_spec=pl.BlockSpec((TM, D_MODEL), lambda i, j: (i, 0)),
        b_spec=_wspec(D_MODEL, IN_COLS // N_CHIPS, lambda i, j: (j, l, 0, 0)),
        o_spec=pl.BlockSpec((TM, IN_COLS // N_CHIPS), lambda i, j: (i, j)), o_shape=(s, IN_COLS), o_dtype=F32,
        dims=NN, nk=1, kaxis=None, acc_shape=None, name="proj" + tag)
    ma = _mixer_a_fwd(proj, p["v_norm_g"], p["v_norm_b"], p["w_spatial"], p["bs_full"], p["out_norm_a"],
                      "mixer_a_fwd" + tag)
    q, k, v = _rope_fwd(proj, tabs, "rope_fwd" + tag)
    outs, lses = zip(*[_attn_fwd(q, k, v, d, f"attn_fwd_d{d}" + tag) for d in DILATIONS])
    ob, lse, mb = _attn_combine(outs, lses, p["out_norm_b"], "attn_combine" + tag)
    mixed = jnp.concatenate([ma, mb], axis=1)
    kt = D_MODEL // N_CHIPS
    y1 = _matmul(
        mixed, wg["w_out"], grid=(nm, N_CHIPS), a_spec=pl.BlockSpec((TM, kt), lambda i, k: (i, k)),
        b_spec=_wspec(kt, D_MODEL, lambda i, k: (k, l, 0, 0)),
        o_spec=pl.BlockSpec((TM, D_MODEL), lambda i, k: (i, 0)), o_shape=(s, D_MODEL), o_dtype=F32,
        dims=NN, nk=N_CHIPS, kaxis=1, acc_shape=(TM, D_MODEL), name="mix_out" + tag)
    x1, h2 = _residual_norm(x0, y1, p["post_mix_norm"], p["pre_ffn_norm"], "post_mix" + tag)
    nup = 2 * D_FF // TK
    per = (2 * D_FF // N_CHIPS) // TK
    half = D_FF // TK
    up3 = _matmul(
        h2, wg["w_up"], grid=(nm, nup), a_spec=pl.BlockSpec((TM, D_MODEL), lambda i, j: (i, 0)),
        b_spec=_wspec(D_MODEL, TK, lambda i, j: (j // per, l, 0, j % per)),
        o_spec=pl.BlockSpec((None, TM, TK), lambda i, j: (j // half, i, j % half)), o_shape=(2, s, D_FF),
        o_dtype=F32, dims=NN, nk=1, kaxis=None, acc_shape=None, name="ffn_up" + tag)
    y = _conv_glu(up3, p["cw3"], p["cb3"], "conv_glu" + tag)
    kd = D_FF // N_CHIPS
    f = _matmul(
        y, wg["w_down"], grid=(nm, N_CHIPS), a_spec=pl.BlockSpec((TM, kd), lambda i, k: (i, k)),
        b_spec=_wspec(kd, D_MODEL, lambda i, k: (k, l, 0, 0)),
        o_spec=pl.BlockSpec((TM, D_MODEL), lambda i, k: (i, 0)), o_shape=(s, D_MODEL), o_dtype=F32,
        dims=NN, nk=N_CHIPS, kaxis=1, acc_shape=(TM, D_MODEL), name="ffn_down" + tag)
    saved = dict(x0=x0, h1=h1, proj=proj, q=q, k=k, v=v, ob=ob, lse=lse, mixed=mixed, y1=y1, x1=x1, h2=h2,
                 up3=up3, y=y, f=f)
    return saved


def _layer_backward(l, dx2, sv, p, wg, tabs):
    s = dx2.shape[0]
    nm = s // TM
    tag = f"_l{l}"
    g = {}
    df, g["post_ffn_norm"] = _norm_bwd_out(dx2, sv["f"], p["post_ffn_norm"], "norm_bwd_out" + tag)
    dconv, conv_sums = _ffn_bwd_conv(df, wg["w_down"], l, sv["up3"], p["cw3"], p["cb3"], "ffn_bwd_conv" + tag)
    dup3 = _conv_transpose(dconv, p["cw3"], "conv_transpose" + tag)
    sums = conv_sums.transpose(1, 2, 0, 3).reshape(2, 8, D_FF)
    g["conv_w"] = jnp.concatenate([sums[0, :3], sums[1, :3]], axis=1)
    g["conv_b"] = jnp.concatenate([sums[0, 3:4], sums[1, 3:4]], axis=1)
    nup = 2 * D_FF // TK
    per = (2 * D_FF // N_CHIPS) // TK
    half = D_FF // TK
    dh2 = _matmul(
        dup3, wg["w_up"], grid=(nm, nup), a_spec=pl.BlockSpec((None, TM, TK), lambda i, n: (n // half, i, n % half)),
        b_spec=_wspec(D_MODEL, TK, lambda i, n: (n // per, l, 0, n % per)),
        o_spec=pl.BlockSpec((TM, D_MODEL), lambda i, n: (i, 0)), o_shape=(s, D_MODEL), o_dtype=F32,
        dims=NT, nk=nup, kaxis=1, acc_shape=(TM, D_MODEL), name="ffn_up_bwd" + tag)
    tn = 1024
    gw_up = _matmul(
        sv["h2"], dup3, grid=(2 * D_FF // tn, nm), a_spec=pl.BlockSpec((TM, D_MODEL), lambda n, m: (m, 0)),
        b_spec=pl.BlockSpec((None, TM, tn), lambda n, m: (n // (D_FF // tn), m, n % (D_FF // tn))),
        o_spec=pl.BlockSpec((None, D_MODEL, tn), lambda n, m: (n // 2, 0, n % 2)),
        o_shape=(N_CHIPS, D_MODEL, 2 * D_FF // N_CHIPS), o_dtype=BF16,
        dims=TN, nk=nm, kaxis=1, acc_shape=(D_MODEL, tn), name="w_up_grad" + tag)
    gw_down = _matmul(
        sv["y"], df, grid=(D_FF // tn, 2, nm), a_spec=pl.BlockSpec((TM, tn), lambda k, h, m: (m, k)),
        b_spec=pl.BlockSpec((TM, D_MODEL // 2), lambda k, h, m: (m, h)),
        o_spec=pl.BlockSpec((None, tn, D_MODEL // 2), lambda k, h, m: (h, k, 0)),
        o_shape=(2, D_FF, D_MODEL // 2), o_dtype=BF16,
        dims=TN, nk=nm, kaxis=2, acc_shape=(tn, D_MODEL // 2), name="w_down_grad" + tag)
    dx1, dy1, g["pre_ffn_norm"], g["post_mix_norm"] = _norm_bwd_mid(
        dx2, dh2, sv["x1"], p["pre_ffn_norm"], sv["y1"], p["post_mix_norm"], "norm_bwd_mid" + tag)
    kt = D_MODEL // N_CHIPS
    dmixed = _matmul(
        dy1, wg["w_out"], grid=(nm, N_CHIPS), a_spec=pl.BlockSpec((TM, D_MODEL), lambda i, j: (i, 0)),
        b_spec=_wspec(kt, D_MODEL, lambda i, j: (j, l, 0, 0)),
        o_spec=pl.BlockSpec((TM, kt), lambda i, j: (i, j)), o_shape=(s, D_MODEL), o_dtype=F32,
        dims=NT, nk=1, kaxis=None, acc_shape=None, name="mix_out_bwd" + tag)
    gw_out = _matmul(
        sv["mixed"], dy1, grid=(2, nm), a_spec=pl.BlockSpec((TM, D_MODEL), lambda h, m: (m, 0)),
        b_spec=pl.BlockSpec((TM, D_MODEL // 2), lambda h, m: (m, h)),
        o_spec=pl.BlockSpec((None, D_MODEL, D_MODEL // 2), lambda h, m: (h, 0, 0)),
        o_shape=(2, D_MODEL, D_MODEL // 2), o_dtype=BF16,
        dims=TN, nk=nm, kaxis=1, acc_shape=(D_MODEL, D_MODEL // 2), name="w_out_grad" + tag)
    dpa, g["out_norm_a"], g["v_norm_g"], g["v_norm_b"], dbs, g["w_spatial"] = _mixer_a_bwd(
        sv["proj"], dmixed, p["v_norm_g"], p["v_norm_b"], p["w_spatial"], p["bs_full"], p["out_norm_a"],
        "mixer_a_bwd" + tag)
    g["b_spatial"] = dbs[:, ::GROUP_DIM].T
    dob, delta, g["out_norm_b"] = _attn_bwd_prep(dmixed, sv["ob"], p["out_norm_b"], "attn_bwd_prep" + tag)
    dqs, dks, dvs = zip(*[
        _attn_bwd(sv["q"], sv["k"], sv["v"], dob, sv["lse"], delta, d, f"attn_bwd_d{d}" + tag) for d in DILATIONS])
    dqkv = _rope_bwd(dqs, dks, dvs, tabs, "rope_bwd" + tag)
    dproj = jnp.concatenate([dpa, dqkv], axis=1)
    wcol = IN_COLS // N_CHIPS
    dh1 = _matmul(
        dproj, wg["w_in"], grid=(nm, N_CHIPS), a_spec=pl.BlockSpec((TM, wcol), lambda i, n: (i, n)),
        b_spec=_wspec(D_MODEL, wcol, lambda i, n: (n, l, 0, 0)),
        o_spec=pl.BlockSpec((TM, D_MODEL), lambda i, n: (i, 0)), o_shape=(s, D_MODEL), o_dtype=F32,
        dims=NT, nk=N_CHIPS, kaxis=1, acc_shape=(TM, D_MODEL), name="proj_bwd" + tag)
    gw_in = _matmul(
        sv["h1"], dproj, grid=(N_CHIPS, nm), a_spec=pl.BlockSpec((TM, D_MODEL), lambda n, m: (m, 0)),
        b_spec=pl.BlockSpec((TM, wcol), lambda n, m: (m, n)),
        o_spec=pl.BlockSpec((None, D_MODEL, wcol), lambda n, m: (n, 0, 0)),
        o_shape=(N_CHIPS, D_MODEL, wcol), o_dtype=BF16,
        dims=TN, nk=nm, kaxis=1, acc_shape=(D_MODEL, wcol), name="w_in_grad" + tag)
    dx0, g["pre_mix_norm"] = _norm_bwd_in(dx1, dh1, sv["x0"], p["pre_mix_norm"], "norm_bwd_in" + tag)
    big = dict(w_in=gw_in, w_up=gw_up, w_out=gw_out, w_down=gw_down)
    return dx0, big, g


SMALL = ("pre_mix_norm", "v_norm_g", "v_norm_b", "w_spatial", "b_spatial", "out_norm_a", "out_norm_b",
         "post_mix_norm", "pre_ffn_norm", "conv_b", "post_ffn_norm")
BIG = ("w_in", "w_out", "w_up", "w_down")
DEPTH = 2


def _layer_params(l, small, conv_w_full):
    p = {n: small[n][l].reshape(1, -1) for n in SMALL if n not in ("w_spatial", "b_spatial")}
    p["w_spatial"] = small["w_spatial"][l]
    p["bs_full"] = jnp.repeat(small["b_spatial"][l].T, GROUP_DIM, axis=1)
    p["cw3"] = conv_w_full[l].reshape(3, 2, D_FF).transpose(1, 0, 2)
    p["cb3"] = small["conv_b"][l].reshape(2, 1, D_FF)
    return p


def _local_step(x, target, small, conv_w_full, wg):
    s = x.shape[0]
    tabs = _rope_tables(s)
    params = [_layer_params(l, small, conv_w_full) for l in range(DEPTH)]
    saved = []
    xin = x
    h = _rms_cast(xin, params[0]["pre_mix_norm"], "pre_mix_l0")
    for l in range(DEPTH):
        sv = _layer_forward(l, xin, h, params[l], wg, tabs)
        saved.append(sv)
        if l + 1 < DEPTH:
            xin, h = _residual_norm(sv["x1"], sv["f"], params[l]["post_ffn_norm"], params[l + 1]["pre_mix_norm"],
                                    f"post_ffn_l{l}")
    loss, dx = _residual_loss(saved[-1]["x1"], saved[-1]["f"], params[-1]["post_ffn_norm"], target, "loss")
    bigs, smalls = [None] * DEPTH, [None] * DEPTH
    for l in reversed(range(DEPTH)):
        dx, bigs[l], smalls[l] = _layer_backward(l, dx, saved[l], params[l], wg, tabs)
    return loss, dx, bigs, smalls


ANY = pl.BlockSpec(memory_space=pl.ANY)


def _mesh_pos():
    return lax.axis_index("x"), lax.axis_index("y"), lax.axis_index("c")


def _other_chips(x, y):
    return [(1 - x, y), (x, 1 - y), (1 - x, 1 - y)]


def _gather_weights(shards):
    nw = len(shards)
    nrel = N_CHIPS - 1

    def body(*refs):
        ins, outs = refs[:nw], refs[nw:2 * nw]
        send, recv, local_sem = refs[2 * nw:]
        x, y, c = _mesh_pos()
        mine = 2 * x + y
        sibling = (x, y, 1 - c)
        chips = _other_chips(x, y)

        def copy(src, dst, slot, to):
            return pltpu.make_async_remote_copy(src_ref=src, dst_ref=dst, send_sem=send.at[slot],
                                                recv_sem=recv.at[slot], device_id=to, device_id_type=MESH)

        local = [pltpu.make_async_copy(ins[t], outs[t].at[mine], local_sem.at[t]) for t in range(nw)]
        for cp in local:
            cp.start()
        sends = []
        for t in range(nw):
            for r, (px, py) in enumerate(chips):
                cp = copy(ins[t].at[c], outs[t].at[mine, c], t * nrel + r, (px, py, c))
                cp.start()
                sends.append(cp)
        for t in range(nw):
            for r, (px, py) in enumerate(chips):
                landed = outs[t].at[2 * px + py, c]
                copy(landed, landed, t * nrel + r, (px, py, c)).wait_recv()
                cp = copy(landed, landed, nw * nrel + t * nrel + r, sibling)
                cp.start()
                sends.append(cp)
        for t in range(nw):
            for r, (px, py) in enumerate(chips):
                passed = outs[t].at[2 * px + py, 1 - c]
                copy(passed, passed, nw * nrel + t * nrel + r, sibling).wait_recv()
        for cp in sends:
            cp.wait_send()
        for cp in local:
            cp.wait()

    nsem = 2 * nw * nrel
    return pl.pallas_call(
        body, in_specs=[ANY] * nw, out_specs=[ANY] * nw,
        out_shape=[jax.ShapeDtypeStruct((N_CHIPS,) + a.shape, a.dtype) for a in shards],
        scratch_shapes=[pltpu.SemaphoreType.DMA((nsem,)), pltpu.SemaphoreType.DMA((nsem,)),
                        pltpu.SemaphoreType.DMA((nw,))],
        name="gather_weights")(*shards)


HALF_ROWS = D_MODEL // 2


def _pair_exchange(g, name):
    shapes = [(N_CHIPS, HALF_ROWS, IN_COLS // N_CHIPS), (N_CHIPS, HALF_ROWS, 2 * D_FF // N_CHIPS),
              (D_MODEL, D_MODEL // 2), (D_FF, D_MODEL // 2)]

    def body(gin, gup, gout, gdn, rin, rup, rout, rdn, send, recv):
        x, y, c = _mesh_pos()
        o = 1 - c
        rows = pl.ds(pl.multiple_of(o * HALF_ROWS, HALF_ROWS), HALF_ROWS)
        pairs = [(gin.at[:, rows, :], rin), (gup.at[:, rows, :], rup), (gout.at[o], rout), (gdn.at[o], rdn)]
        cps = [pltpu.make_async_remote_copy(src_ref=src, dst_ref=dst, send_sem=send.at[t], recv_sem=recv.at[t],
                                            device_id=(x, y, o), device_id_type=MESH)
               for t, (src, dst) in enumerate(pairs)]
        for cp in cps:
            cp.start()
        for cp in cps:
            cp.wait()

    return pl.pallas_call(
        body, in_specs=[ANY] * 4, out_specs=[ANY] * 4,
        out_shape=[jax.ShapeDtypeStruct(sh, BF16) for sh in shapes],
        scratch_shapes=[pltpu.SemaphoreType.DMA((4,)), pltpu.SemaphoreType.DMA((4,))],
        name=name)(g["w_in"], g["w_up"], g["w_out"], g["w_down"])


def _pair_sum(g, recv, pos, name_prefix):
    def add(a, b, grid, a_spec, b_spec, shape, name):
        def body(pos_ref, a_ref, b_ref, o_ref):
            o_ref[...] = (a_ref[...].astype(F32) + b_ref[...].astype(F32)).astype(BF16)

        return pl.pallas_call(
            body, grid_spec=pltpu.PrefetchScalarGridSpec(
                num_scalar_prefetch=1, grid=grid, in_specs=[a_spec, b_spec], out_specs=b_spec),
            out_shape=jax.ShapeDtypeStruct(shape, BF16), compiler_params=_cparams(*["parallel"] * len(grid)),
            name=name)(pos, a, b)

    rin, rup, rout, rdn = recv
    wi, wu = IN_COLS // N_CHIPS, 2 * D_FF // N_CHIPS
    s_in = add(g["w_in"], rin, (N_CHIPS,), pl.BlockSpec((None, HALF_ROWS, wi), lambda j, pos: (j, pos[2], 0)),
               pl.BlockSpec((None, HALF_ROWS, wi), lambda j, pos: (j, 0, 0)), rin.shape, name_prefix + "_in")
    s_up = add(g["w_up"], rup, (N_CHIPS,), pl.BlockSpec((None, HALF_ROWS, wu), lambda j, pos: (j, pos[2], 0)),
               pl.BlockSpec((None, HALF_ROWS, wu), lambda j, pos: (j, 0, 0)), rup.shape, name_prefix + "_up")
    hc = D_MODEL // 2
    s_out = add(g["w_out"], rout, (1,), pl.BlockSpec((None, D_MODEL, hc), lambda j, pos: (pos[2], 0, 0)),
                pl.BlockSpec((D_MODEL, hc), lambda j, pos: (0, 0)), rout.shape, name_prefix + "_out")
    s_dn = add(g["w_down"], rdn, (N_CHIPS,), pl.BlockSpec((None, D_FF // N_CHIPS, hc), lambda j, pos: (pos[2], j, 0)),
               pl.BlockSpec((D_FF // N_CHIPS, hc), lambda j, pos: (j, 0)), rdn.shape, name_prefix + "_down")
    return s_in, s_up, s_out, s_dn


OUT_ROWS = D_MODEL // N_CHIPS
DOWN_ROWS = D_FF // N_CHIPS


def _chip_scatter(sums, name):
    nrel = N_CHIPS - 1
    shapes = [(nrel, HALF_ROWS, IN_COLS // N_CHIPS), (nrel, HALF_ROWS, 2 * D_FF // N_CHIPS),
              (nrel, OUT_ROWS, D_MODEL // 2), (nrel, DOWN_ROWS, D_MODEL // 2)]

    def body(sin, sup, sout, sdn, rin, rup, rout, rdn, send, recv):
        x, y, c = _mesh_pos()
        cps = []
        for r, (px, py) in enumerate(_other_chips(x, y)):
            j = 2 * px + py
            pieces = [(sin.at[j], rin), (sup.at[j], rup),
                      (sout.at[pl.ds(pl.multiple_of(j * OUT_ROWS, OUT_ROWS), OUT_ROWS), :], rout),
                      (sdn.at[pl.ds(pl.multiple_of(j * DOWN_ROWS, DOWN_ROWS), DOWN_ROWS), :], rdn)]
            for t, (src, dst) in enumerate(pieces):
                cp = pltpu.make_async_remote_copy(src_ref=src, dst_ref=dst.at[r], send_sem=send.at[t * nrel + r],
                                                  recv_sem=recv.at[t * nrel + r], device_id=(px, py, c),
                                                  device_id_type=MESH)
                cp.start()
                cps.append(cp)
        for cp in cps:
            cp.wait()

    return pl.pallas_call(
        body, in_specs=[ANY] * 4, out_specs=[ANY] * 4,
        out_shape=[jax.ShapeDtypeStruct(sh, BF16) for sh in shapes],
        scratch_shapes=[pltpu.SemaphoreType.DMA((4 * nrel,)), pltpu.SemaphoreType.DMA((4 * nrel,))],
        name=name)(*sums)


def _chip_sum(sums, recv, pos, name_prefix):
    def add(a, b, a_spec, shape, name):
        def body(pos_ref, a_ref, b_ref, o_ref):
            tot = a_ref[...].astype(F32)
            for r in range(N_CHIPS - 1):
                tot = tot + b_ref[r].astype(F32)
            o_ref[...] = tot

        return pl.pallas_call(
            body, grid_spec=pltpu.PrefetchScalarGridSpec(
                num_scalar_prefetch=1, grid=(1,), in_specs=[a_spec, pl.BlockSpec(b.shape, lambda i, pos: (0, 0, 0))],
                out_specs=pl.BlockSpec(shape, lambda i, pos: (0, 0))),
            out_shape=jax.ShapeDtypeStruct(shape, F32), compiler_params=_cparams("arbitrary"), name=name)(pos, a, b)

    s_in, s_up, s_out, s_dn = sums
    rin, rup, rout, rdn = recv
    wi, wu, hc = IN_COLS // N_CHIPS, 2 * D_FF // N_CHIPS, D_MODEL // 2
    chip = lambda pos: 2 * pos[0] + pos[1]
    t_in = add(s_in, rin, pl.BlockSpec((None, HALF_ROWS, wi), lambda i, pos: (chip(pos), 0, 0)), (HALF_ROWS, wi),
               name_prefix + "_in")
    t_up = add(s_up, rup, pl.BlockSpec((None, HALF_ROWS, wu), lambda i, pos: (chip(pos), 0, 0)), (HALF_ROWS, wu),
               name_prefix + "_up")
    t_out = add(s_out, rout, pl.BlockSpec((OUT_ROWS, hc), lambda i, pos: (chip(pos), 0)), (OUT_ROWS, hc),
                name_prefix + "_out")
    t_dn = add(s_dn, rdn, pl.BlockSpec((DOWN_ROWS, hc), lambda i, pos: (chip(pos), 0)), (DOWN_ROWS, hc),
               name_prefix + "_down")
    return t_in, t_up, t_out, t_dn


def _pair_share(totals, name):
    hc = D_MODEL // 2
    shapes = [(D_MODEL, IN_COLS // N_CHIPS), (D_MODEL, 2 * D_FF // N_CHIPS), (2, OUT_ROWS, hc), (2, DOWN_ROWS, hc)]

    def body(tin, tup, tout, tdn, fin, fup, fout, fdn, send, recv, local_sem):
        x, y, c = _mesh_pos()
        rows = pl.ds(pl.multiple_of(c * HALF_ROWS, HALF_ROWS), HALF_ROWS)
        pairs = [(tin, fin.at[rows, :]), (tup, fup.at[rows, :]), (tout, fout.at[c]), (tdn, fdn.at[c])]
        local = [pltpu.make_async_copy(src, dst, local_sem.at[t]) for t, (src, dst) in enumerate(pairs)]
        cps = [pltpu.make_async_remote_copy(src_ref=src, dst_ref=dst, send_sem=send.at[t], recv_sem=recv.at[t],
                                            device_id=(x, y, 1 - c), device_id_type=MESH)
               for t, (src, dst) in enumerate(pairs)]
        for cp in local + cps:
            cp.start()
        o = 1 - c
        orows = pl.ds(pl.multiple_of(o * HALF_ROWS, HALF_ROWS), HALF_ROWS)
        theirs = [fin.at[orows, :], fup.at[orows, :], fout.at[o], fdn.at[o]]
        for t, dst in enumerate(theirs):
            pltpu.make_async_remote_copy(src_ref=dst, dst_ref=dst, send_sem=send.at[t], recv_sem=recv.at[t],
                                         device_id=(x, y, o), device_id_type=MESH).wait_recv()
        for cp in cps:
            cp.wait_send()
        for cp in local:
            cp.wait()

    return pl.pallas_call(
        body, in_specs=[ANY] * 4, out_specs=[ANY] * 4,
        out_shape=[jax.ShapeDtypeStruct(sh, F32) for sh in shapes],
        scratch_shapes=[pltpu.SemaphoreType.DMA((4,)), pltpu.SemaphoreType.DMA((4,)), pltpu.SemaphoreType.DMA((4,))],
        name=name)(*totals)


def _reduce_scatter_layer(l, g, pos):
    recv = _pair_exchange(g, f"pair_exchange_l{l}")
    sums = _pair_sum(g, recv, pos, f"pair_sum_l{l}")
    recv2 = _chip_scatter(sums, f"chip_scatter_l{l}")
    totals = _chip_sum(sums, recv2, pos, f"chip_sum_l{l}")
    f_in, f_up, f_out, f_dn = _pair_share(totals, f"pair_share_l{l}")
    f_out = f_out.transpose(1, 0, 2).reshape(OUT_ROWS, D_MODEL)
    f_dn = f_dn.transpose(1, 0, 2).reshape(DOWN_ROWS, D_MODEL)
    return dict(w_in=f_in, w_up=f_up, w_out=f_out, w_down=f_dn)


N_DEV = 8


def _allreduce_small(packed, name):
    rows = packed.shape[0]

    def body(x_ref, out_ref, gath, send_sems, recv_sems, local_sem):
        x, y, c = _mesh_pos()
        me, sibling = (x, y, c), (x, y, 1 - c)
        chips = _other_chips(x, y)

        def blk(px, py, pc):
            return gath.at[pl.ds(pl.multiple_of((4 * px + 2 * py + pc) * rows, 8), rows), :]

        def copy(k, block, to, src=None):
            return pltpu.make_async_remote_copy(
                src_ref=blk(*block) if src is None else src, dst_ref=blk(*block), send_sem=send_sems.at[k],
                recv_sem=recv_sems.at[k], device_id=to, device_id_type=MESH)

        mine = pltpu.make_async_copy(x_ref, blk(*me), local_sem)
        mine.start()
        first = [copy(0, me, sibling, src=x_ref)]
        first += [copy(1 + j, me, (*chip, c), src=x_ref) for j, chip in enumerate(chips)]
        for cp in first:
            cp.start()
        passed = [copy(4 + j, (*chip, c), sibling) for j, chip in enumerate(chips)]
        for j, chip in enumerate(chips):
            copy(1 + j, (*chip, c), me).wait_recv()
            passed[j].start()
        copy(0, sibling, me).wait_recv()
        for j, chip in enumerate(chips):
            copy(4 + j, (*chip, 1 - c), me).wait_recv()
        for cp in first + passed:
            cp.wait_send()
        mine.wait()
        tot = gath[0:rows, :]
        for d in range(1, N_DEV):
            tot = tot + gath[d * rows:(d + 1) * rows, :]
        out_ref[...] = tot

    vmem = pl.BlockSpec(memory_space=pltpu.VMEM)
    return pl.pallas_call(
        body, in_specs=[vmem], out_specs=vmem, out_shape=jax.ShapeDtypeStruct((rows, LANES), F32),
        scratch_shapes=[pltpu.VMEM((N_DEV * rows, LANES), F32), pltpu.SemaphoreType.DMA((7,)),
                        pltpu.SemaphoreType.DMA((7,)), pltpu.SemaphoreType.DMA],
        compiler_params=pltpu.CompilerParams(vmem_limit_bytes=VMEM_LIMIT_BYTES),
        name=name)(packed)


def _adamw(w, g, m, v, name):
    rows, cols = w.shape
    tr = 256 if rows % 256 == 0 else rows

    def body(w_ref, g_ref, m_ref, v_ref, d_ref, mo_ref, vo_ref):
        gv = g_ref[...]
        mn = ADAM_B1 * m_ref[...] + (1.0 - ADAM_B1) * gv
        vn = ADAM_B2 * v_ref[...] + (1.0 - ADAM_B2) * (gv * gv)
        m_hat = mn / (1.0 - ADAM_B1 ** ADAM_STEP)
        v_hat = vn / (1.0 - ADAM_B2 ** ADAM_STEP)
        d_ref[...] = -ADAM_LR * (m_hat / (jnp.sqrt(v_hat) + ADAM_EPS) + ADAM_WD * w_ref[...])
        mo_ref[...] = mn
        vo_ref[...] = vn

    spec = pl.BlockSpec((tr, cols), lambda i: (i, 0))
    return pl.pallas_call(
        body, grid=(rows // tr,), in_specs=[spec] * 4, out_specs=[spec] * 3,
        out_shape=[jax.ShapeDtypeStruct((rows, cols), F32)] * 3, compiler_params=_cparams("parallel"),
        name=name)(w, g, m, v)


def _adamw_nd(w, g, m, v, name):
    cols = w.shape[-1] if w.shape[-1] % LANES == 0 else LANES
    outs = _adamw(*(t.reshape(-1, cols) for t in (w, g, m, v)), name)
    return tuple(t.reshape(w.shape) for t in outs)


def _pack(arrays):
    return jnp.concatenate([a.reshape(-1, LANES) for a in arrays], axis=0)


def _unpack(packed, shapes):
    out, row = [], 0
    for sh in shapes:
        n = math.prod(sh) // LANES
        out.append(packed[row:row + n].reshape(sh))
        row += n
    return out


WEIGHTS = ("pre_mix_norm", "w_in", "v_norm_g", "v_norm_b", "w_spatial", "b_spatial", "out_norm_a", "out_norm_b",
           "w_out", "post_mix_norm", "pre_ffn_norm", "w_up", "conv_w", "conv_b", "w_down", "post_ffn_norm")


def kernel(x, pre_mix_norm, w_in, v_norm_g, v_norm_b, w_spatial, b_spatial, out_norm_a, out_norm_b, w_out, post_mix_norm, pre_ffn_norm, w_up, conv_w, conv_b, w_down, post_ffn_norm, loss_target, m_pre_mix_norm, m_w_in, m_v_norm_g, m_v_norm_b, m_w_spatial, m_b_spatial, m_out_norm_a, m_out_norm_b, m_w_out, m_post_mix_norm, m_pre_ffn_norm, m_w_up, m_conv_w, m_conv_b, m_w_down, m_post_ffn_norm, v_pre_mix_norm, v_w_in, v_v_norm_g, v_v_norm_b, v_w_spatial, v_b_spatial, v_out_norm_a, v_out_norm_b, v_w_out, v_post_mix_norm, v_pre_ffn_norm, v_w_up, v_conv_w, v_conv_b, v_w_down, v_post_ffn_norm):
    w = dict(pre_mix_norm=pre_mix_norm, w_in=w_in, v_norm_g=v_norm_g, v_norm_b=v_norm_b, w_spatial=w_spatial,
             b_spatial=b_spatial, out_norm_a=out_norm_a, out_norm_b=out_norm_b, w_out=w_out,
             post_mix_norm=post_mix_norm, pre_ffn_norm=pre_ffn_norm, w_up=w_up, conv_w=conv_w, conv_b=conv_b,
             w_down=w_down, post_ffn_norm=post_ffn_norm)
    m = dict(pre_mix_norm=m_pre_mix_norm, w_in=m_w_in, v_norm_g=m_v_norm_g, v_norm_b=m_v_norm_b,
             w_spatial=m_w_spatial, b_spatial=m_b_spatial, out_norm_a=m_out_norm_a, out_norm_b=m_out_norm_b,
             w_out=m_w_out, post_mix_norm=m_post_mix_norm, pre_ffn_norm=m_pre_ffn_norm, w_up=m_w_up,
             conv_w=m_conv_w, conv_b=m_conv_b, w_down=m_w_down, post_ffn_norm=m_post_ffn_norm)
    v = dict(pre_mix_norm=v_pre_mix_norm, w_in=v_w_in, v_norm_g=v_v_norm_g, v_norm_b=v_v_norm_b,
             w_spatial=v_w_spatial, b_spatial=v_b_spatial, out_norm_a=v_out_norm_a, out_norm_b=v_out_norm_b,
             w_out=v_w_out, post_mix_norm=v_post_mix_norm, pre_ffn_norm=v_pre_ffn_norm, w_up=v_w_up,
             conv_w=v_conv_w, conv_b=v_conv_b, w_down=v_w_down, post_ffn_norm=v_post_ffn_norm)
    pos = jnp.stack([lax.axis_index("x"), lax.axis_index("y"), lax.axis_index("c")]).astype(jnp.int32)
    chip = 2 * lax.axis_index("x") + lax.axis_index("y")

    gathered = _gather_weights([w[n].astype(BF16) for n in BIG])
    wg = dict(zip(BIG, gathered))
    cw_cols = conv_w.shape[-1]
    cw_slab = lax.dynamic_update_slice(jnp.zeros((DEPTH, 3, 2 * D_FF), F32), conv_w, (0, 0, chip * cw_cols))
    conv_w_full = _allreduce_small(cw_slab.reshape(-1, LANES), "gather_conv_w").reshape(DEPTH, 3, 2 * D_FF)
    conv_w_full = conv_w_full * 0.5

    small = {n: w[n] for n in SMALL}
    loss_part, dx, bigs, smalls = _local_step(x[0], loss_target[0], small, conv_w_full, wg)

    small_shapes = [w[n].shape for n in SMALL]
    stacked = [jnp.stack([smalls[l][n].reshape(w[n].shape[1:]) for l in range(DEPTH)]) for n in SMALL]
    cw_grad = jnp.stack([smalls[l]["conv_w"] for l in range(DEPTH)])
    packed = _pack(stacked + [cw_grad, loss_part])
    total = _allreduce_small(packed, "allreduce_small")
    parts = _unpack(total, small_shapes + [cw_grad.shape, (8, LANES)])
    g_small = dict(zip(SMALL, parts[:len(SMALL)]))
    loss = parts[-1][0, 0]
    g_conv_w = lax.dynamic_slice(parts[-2], (0, 0, chip * cw_cols), conv_w.shape)

    shards = [_reduce_scatter_layer(l, bigs[l], pos) for l in range(DEPTH)]
    grads = {n: jnp.stack([shards[l][n] for l in range(DEPTH)]) for n in BIG}
    grads.update(g_small)
    grads["conv_w"] = g_conv_w

    dp, mp, vp = _adamw(_pack([w[n] for n in SMALL]), _pack([g_small[n] for n in SMALL]),
                        _pack([m[n] for n in SMALL]), _pack([v[n] for n in SMALL]), "adamw_small")
    delta = dict(zip(SMALL, _unpack(dp, small_shapes)))
    new_m = dict(zip(SMALL, _unpack(mp, small_shapes)))
    new_v = dict(zip(SMALL, _unpack(vp, small_shapes)))
    for n in BIG + ("conv_w",):
        delta[n], new_m[n], new_v[n] = _adamw_nd(w[n], grads[n], m[n], v[n], "adamw_" + n)

    return (loss, dx[None], *[grads[n] for n in WEIGHTS], *[delta[n] for n in WEIGHTS],
            *[new_m[n] for n in WEIGHTS], *[new_v[n] for n in WEIGHTS])
```

```python
import functools
import math

import jax
import jax.numpy as jnp
import numpy as np
from jax import lax
from jax.experimental import pallas as pl
from jax.experimental.pallas import tpu as pltpu

F32 = jnp.float32
BF16 = jnp.bfloat16
MESH = pl.DeviceIdType.MESH

D_MODEL = 1024
A_WIDTH = 512
A_GROUPS = 4
GROUP_DIM = 128
CHUNK = 128
B_WIDTH = 512
HEAD_DIM = 64
ROT_DIM = 16
ROPE_THETA = 500000.0
DILATIONS = (1, 4, 16)
BAND = 128
IN_COLS = 2560
D_FF = 4096
EPS = 1e-6
NEG_INF = -1e30
N_CHIPS = 4
LANES = 128

ADAM_LR = 0.001
ADAM_B1 = 0.9
ADAM_B2 = 0.999
ADAM_EPS = 1e-08
ADAM_WD = 0.01
ADAM_STEP = 10

VMEM_LIMIT_BYTES = 56 * 1024 * 1024
RSQRT2 = 0.7071067811865476
INV_SQRT_2PI = 0.3989422804014327
GELU_C = 0.7978845608028654
GELU_A = 0.044715

NN = ((1,), (0,))
NT = ((1,), (1,))
TN = ((0,), (0,))


def _cparams(*sem):
    return pltpu.CompilerParams(dimension_semantics=sem, vmem_limit_bytes=VMEM_LIMIT_BYTES)


def _dot(a, b, dims):
    return lax.dot_general(a, b, (dims, ((), ())), preferred_element_type=F32)


def _rsq_mean(a):
    return lax.rsqrt(jnp.mean(a * a, axis=-1, keepdims=True) + EPS)


def _rms_bwd(a, r, g, dz):
    t = dz * g
    da = r * t - a * (r * r * r) * jnp.mean(t * a, axis=-1, keepdims=True)
    return da, dz * a * r


def _colsum(a):
    return jnp.sum(a, axis=0, keepdims=True)


def _gelu_tanh(x):
    t = jnp.tanh(GELU_C * (x + GELU_A * x * x * x))
    return 0.5 * x * (1.0 + t), t


def _gelu_tanh_grad(x, t):
    return 0.5 * (1.0 + t) + 0.5 * x * (1.0 - t * t) * GELU_C * (1.0 + 3.0 * GELU_A * x * x)


def _matmul(a, b, *, grid, a_spec, b_spec, o_spec, o_shape, o_dtype, dims, nk, kaxis, acc_shape, name, b_2d=None):
    def body(a_ref, b_ref, o_ref, *scratch):
        bv = b_ref[...] if b_2d is None else b_ref[...].reshape(b_2d)
        part = _dot(a_ref[...], bv, dims)
        if nk == 1:
            o_ref[...] = part.astype(o_dtype)
        else:
            acc = scratch[0]
            k = pl.program_id(kaxis)

            @pl.when(k == 0)
            def _():
                acc[...] = part

            @pl.when(k > 0)
            def _():
                acc[...] += part

            @pl.when(k == nk - 1)
            def _():
                o_ref[...] = acc[...].astype(o_dtype)

    sem = tuple("arbitrary" if (nk > 1 and ax == kaxis) else "parallel" for ax in range(len(grid)))
    return pl.pallas_call(
        body, grid=grid, in_specs=[a_spec, b_spec], out_specs=o_spec,
        out_shape=jax.ShapeDtypeStruct(o_shape, o_dtype),
        scratch_shapes=[pltpu.VMEM(acc_shape, F32)] if nk > 1 else [],
        compiler_params=_cparams(*sem), name=name)(a, b)


TM = 512
TMM = 1024


TR = 256


def _row_spec(width, col=0):
    return pl.BlockSpec((TR, width), lambda i, col=col: (i, col))


def _vec_spec(width):
    return pl.BlockSpec((1, width), lambda i: (0, 0))


def _rms_cast(x, g, name):
    s, d = x.shape

    def body(x_ref, g_ref, h_ref):
        a = x_ref[...]
        h_ref[...] = (a * _rsq_mean(a) * g_ref[...]).astype(BF16)

    return pl.pallas_call(
        body, grid=(s // TR,), in_specs=[_row_spec(d), _vec_spec(d)], out_specs=_row_spec(d),
        out_shape=jax.ShapeDtypeStruct((s, d), BF16), compiler_params=_cparams("parallel"), name=name)(x, g)


def _residual_norm(x0, y, g_post, g_next, name):
    s, d = x0.shape

    def body(x_ref, y_ref, gp_ref, gn_ref, x1_ref, h_ref):
        yv = y_ref[...]
        x1 = x_ref[...] + yv * _rsq_mean(yv) * gp_ref[...]
        x1_ref[...] = x1
        h_ref[...] = (x1 * _rsq_mean(x1) * gn_ref[...]).astype(BF16)

    return pl.pallas_call(
        body, grid=(s // TR,), in_specs=[_row_spec(d), _row_spec(d), _vec_spec(d), _vec_spec(d)],
        out_specs=[_row_spec(d), _row_spec(d)],
        out_shape=[jax.ShapeDtypeStruct((s, d), F32), jax.ShapeDtypeStruct((s, d), BF16)],
        compiler_params=_cparams("parallel"), name=name)(x0, y, g_post, g_next)


def _residual_loss(x1, f, g_post, target, name):
    s, d = x1.shape

    def body(x_ref, f_ref, gp_ref, t_ref, loss_ref, dx_ref):
        fv = f_ref[...]
        err = x_ref[...] + fv * _rsq_mean(fv) * gp_ref[...] - t_ref[...]
        dx_ref[...] = err * (1.0 / d)
        part = 0.5 * jnp.sum(jnp.mean(err * err, axis=-1, keepdims=True), axis=0, keepdims=True)

        @pl.when(pl.program_id(0) == 0)
        def _():
            loss_ref[...] = jnp.zeros_like(loss_ref)

        loss_ref[...] += jnp.broadcast_to(part, loss_ref.shape)

    return pl.pallas_call(
        body, grid=(s // TR,), in_specs=[_row_spec(d), _row_spec(d), _vec_spec(d), _row_spec(d)],
        out_specs=[pl.BlockSpec((8, LANES), lambda i: (0, 0)), _row_spec(d)],
        out_shape=[jax.ShapeDtypeStruct((8, LANES), F32), jax.ShapeDtypeStruct((s, d), F32)],
        compiler_params=_cparams("arbitrary"), name=name)(x1, f, g_post, target)


def _acc_init(refs):
    @pl.when(pl.program_id(0) == 0)
    def _():
        for r in refs:
            r[...] = jnp.zeros_like(r)


def _norm_bwd_out(dx, f, g_post, name):
    s, d = dx.shape

    def body(dx_ref, f_ref, g_ref, df_ref, dg_ref):
        _acc_init([dg_ref])
        fv = f_ref[...]
        dz = dx_ref[...]
        da, dgt = _rms_bwd(fv, _rsq_mean(fv), g_ref[...], dz)
        df_ref[...] = da.astype(BF16)
        dg_ref[...] += _colsum(dgt)

    return pl.pallas_call(
        body, grid=(s // TR,), in_specs=[_row_spec(d), _row_spec(d), _vec_spec(d)],
        out_specs=[_row_spec(d), _vec_spec(d)],
        out_shape=[jax.ShapeDtypeStruct((s, d), BF16), jax.ShapeDtypeStruct((1, d), F32)],
        compiler_params=_cparams("arbitrary"), name=name)(dx, f, g_post)


def _norm_bwd_mid(dx2, dh2, x1, g_pf, y1, g_pm, name):
    s, d = dx2.shape

    def body(dx2_ref, dh_ref, x1_ref, gpf_ref, y1_ref, gpm_ref, dx1_ref, dy1_ref, dgpf_ref, dgpm_ref):
        _acc_init([dgpf_ref, dgpm_ref])
        x1 = x1_ref[...]
        da, dgt = _rms_bwd(x1, _rsq_mean(x1), gpf_ref[...], dh_ref[...])
        dx1 = dx2_ref[...] + da
        dx1_ref[...] = dx1
        dgpf_ref[...] += _colsum(dgt)
        y1 = y1_ref[...]
        dy, dgt2 = _rms_bwd(y1, _rsq_mean(y1), gpm_ref[...], dx1)
        dy1_ref[...] = dy.astype(BF16)
        dgpm_ref[...] += _colsum(dgt2)

    return pl.pallas_call(
        body, grid=(s // TR,),
        in_specs=[_row_spec(d), _row_spec(d), _row_spec(d), _vec_spec(d), _row_spec(d), _vec_spec(d)],
        out_specs=[_row_spec(d), _row_spec(d), _vec_spec(d), _vec_spec(d)],
        out_shape=[jax.ShapeDtypeStruct((s, d), F32), jax.ShapeDtypeStruct((s, d), BF16),
                   jax.ShapeDtypeStruct((1, d), F32), jax.ShapeDtypeStruct((1, d), F32)],
        compiler_params=_cparams("arbitrary"), name=name)(dx2, dh2, x1, g_pf, y1, g_pm)


def _norm_bwd_in(dx1, dh1, x0, g1, name):
    s, d = dx1.shape

    def body(dx1_ref, dh_ref, x0_ref, g_ref, dx0_ref, dg_ref):
        _acc_init([dg_ref])
        x0 = x0_ref[...]
        da, dgt = _rms_bwd(x0, _rsq_mean(x0), g_ref[...], dh_ref[...])
        dx0_ref[...] = dx1_ref[...] + da
        dg_ref[...] += _colsum(dgt)

    return pl.pallas_call(
        body, grid=(s // TR,), in_specs=[_row_spec(d), _row_spec(d), _row_spec(d), _vec_spec(d)],
        out_specs=[_row_spec(d), _vec_spec(d)],
        out_shape=[jax.ShapeDtypeStruct((s, d), F32), jax.ShapeDtypeStruct((1, d), F32)],
        compiler_params=_cparams("arbitrary"), name=name)(dx1, dh1, x0, g1)


def _tril_mask():
    row = lax.broadcasted_iota(jnp.int32, (CHUNK, CHUNK), 0)
    col = lax.broadcasted_iota(jnp.int32, (CHUNK, CHUNK), 1)
    return row >= col


def _gating_forward(pa, gv, bv, wt, bsf):
    er = lax.erf(pa * RSQRT2)
    za = 0.5 * pa * (1.0 + er)
    u = za[:, :A_WIDTH]
    va = za[:, A_WIDTH:]
    xc = va - jnp.mean(va, axis=-1, keepdims=True)
    rs = lax.rsqrt(jnp.mean(xc * xc, axis=-1, keepdims=True) + EPS)
    vn = xc * rs
    vlb = (vn * gv + bv).astype(BF16)
    sg = jnp.concatenate(
        [_dot(wt[g], vlb[:, g * GROUP_DIM:(g + 1) * GROUP_DIM], NN) for g in range(A_GROUPS)], axis=1) + bsf
    return er, u, rs, vn, vlb, sg


def _masked_ws(ws_ref):
    mask = _tril_mask()
    return [jnp.where(mask, ws_ref[g], 0.0).astype(BF16) for g in range(A_GROUPS)]


def _mixer_a_fwd(proj, gv, bv, ws, bsf, ga, name):
    s = proj.shape[0]

    def body(p_ref, gv_ref, bv_ref, ws_ref, bs_ref, ga_ref, o_ref):
        wt = _masked_ws(ws_ref)
        for ch in range(TR // CHUNK):
            rows = slice(ch * CHUNK, (ch + 1) * CHUNK)
            _, u, _, _, _, sg = _gating_forward(p_ref[rows, :], gv_ref[...], bv_ref[...], wt, bs_ref[...])
            oa = u * sg
            o_ref[rows, :] = (oa * _rsq_mean(oa) * ga_ref[...]).astype(BF16)

    return pl.pallas_call(
        body, grid=(s // TR,),
        in_specs=[_row_spec(2 * A_WIDTH), _vec_spec(A_WIDTH), _vec_spec(A_WIDTH),
                  pl.BlockSpec((A_GROUPS, CHUNK, CHUNK), lambda i: (0, 0, 0)),
                  pl.BlockSpec((CHUNK, A_WIDTH), lambda i: (0, 0)), _vec_spec(A_WIDTH)],
        out_specs=_row_spec(A_WIDTH), out_shape=jax.ShapeDtypeStruct((s, A_WIDTH), BF16),
        compiler_params=_cparams("parallel"), name=name)(proj, gv, bv, ws, bsf, ga)


def _mixer_a_bwd(proj, dmixed, gv, bv, ws, bsf, ga, name):
    s = proj.shape[0]
    nsteps = s // TR

    def body(p_ref, dm_ref, gv_ref, bv_ref, ws_ref, bs_ref, ga_ref,
             dp_ref, dga_ref, dgv_ref, dbv_ref, dbs_ref, dws_ref):
        _acc_init([dga_ref, dgv_ref, dbv_ref, dbs_ref, dws_ref])
        mask = _tril_mask()
        wt = _masked_ws(ws_ref)
        gvv = gv_ref[...]
        gav = ga_ref[...]
        for ch in range(TR // CHUNK):
            rows = slice(ch * CHUNK, (ch + 1) * CHUNK)
            pa = p_ref[rows, :]
            er, u, rs, vn, vlb, sg = _gating_forward(pa, gvv, bv_ref[...], wt, bs_ref[...])
            oa = u * sg
            doa, dgt = _rms_bwd(oa, _rsq_mean(oa), gav, dm_ref[rows, :])
            dga_ref[...] += _colsum(dgt)
            du = doa * sg
            dsg = doa * u
            dbs_ref[...] += dsg
            dsgb = dsg.astype(BF16)
            dvl = []
            for g in range(A_GROUPS):
                cols = slice(g * GROUP_DIM, (g + 1) * GROUP_DIM)
                dws_ref[g] += jnp.where(mask, _dot(dsgb[:, cols], vlb[:, cols], NT), 0.0)
                dvl.append(_dot(wt[g], dsgb[:, cols], TN))
            dvl = jnp.concatenate(dvl, axis=1)
            dgv_ref[...] += _colsum(dvl * vn)
            dbv_ref[...] += _colsum(dvl)
            dvn = dvl * gvv
            dva = rs * (dvn - jnp.mean(dvn, axis=-1, keepdims=True)
                        - vn * jnp.mean(dvn * vn, axis=-1, keepdims=True))
            gp = 0.5 * (1.0 + er) + pa * jnp.exp(-0.5 * pa * pa) * INV_SQRT_2PI
            dp_ref[rows, :] = (jnp.concatenate([du, dva], axis=1) * gp).astype(BF16)

        @pl.when(pl.program_id(0) == nsteps - 1)
        def _():
            for g in range(A_GROUPS):
                cols = slice(g * GROUP_DIM, (g + 1) * GROUP_DIM)
                tot = jnp.sum(dbs_ref[:, cols], axis=1, keepdims=True)
                dbs_ref[:, cols] = jnp.broadcast_to(tot, (CHUNK, GROUP_DIM))

    full = lambda *shape: pl.BlockSpec(shape, lambda i: (0,) * len(shape))
    return pl.pallas_call(
        body, grid=(nsteps,),
        in_specs=[_row_spec(2 * A_WIDTH), _row_spec(A_WIDTH), _vec_spec(A_WIDTH), _vec_spec(A_WIDTH),
                  full(A_GROUPS, CHUNK, CHUNK), full(CHUNK, A_WIDTH), _vec_spec(A_WIDTH)],
        out_specs=[_row_spec(2 * A_WIDTH), _vec_spec(A_WIDTH), _vec_spec(A_WIDTH), _vec_spec(A_WIDTH),
                   full(CHUNK, A_WIDTH), full(A_GROUPS, CHUNK, CHUNK)],
        out_shape=[jax.ShapeDtypeStruct((s, 2 * A_WIDTH), BF16), jax.ShapeDtypeStruct((1, A_WIDTH), F32),
                   jax.ShapeDtypeStruct((1, A_WIDTH), F32), jax.ShapeDtypeStruct((1, A_WIDTH), F32),
                   jax.ShapeDtypeStruct((CHUNK, A_WIDTH), F32),
                   jax.ShapeDtypeStruct((A_GROUPS, CHUNK, CHUNK), F32)],
        compiler_params=_cparams("arbitrary"), name=name)(proj, dmixed, gv, bv, ws, bsf, ga)


def _rope_tables(s):
    half = ROT_DIM // 2
    inv = ROPE_THETA ** (-jnp.arange(0, ROT_DIM, 2, dtype=F32) / ROT_DIM)
    ang = jnp.arange(s, dtype=F32)[:, None] * inv[None, :]
    cos, sin = jnp.cos(ang), jnp.sin(ang)
    zeros = jnp.zeros((s, HEAD_DIM - ROT_DIM), F32)
    zh = jnp.zeros((s, half), F32)
    c = jnp.concatenate([cos, cos, zeros + 1.0], axis=1)
    s1 = jnp.concatenate([-sin, zh, zeros], axis=1)
    s2 = jnp.concatenate([zh, sin, zeros], axis=1)
    return tuple(jnp.concatenate([t, t], axis=1) for t in (c, s1, s2))


def _lane_blocks(width):
    return [slice(b * LANES, (b + 1) * LANES) for b in range(width // LANES)]


def _rope_fwd(proj, tabs, name):
    s = proj.shape[0]
    half = ROT_DIM // 2
    scale = HEAD_DIM ** -0.5

    def body(q_ref, k_ref, v_ref, c_ref, s1_ref, s2_ref, qo_ref, ko_ref, vo_ref):
        c, s1, s2 = c_ref[...], s1_ref[...], s2_ref[...]
        for sl in _lane_blocks(B_WIDTH):
            for src, dst, mul in ((q_ref, qo_ref, scale), (k_ref, ko_ref, 1.0)):
                a = src[:, sl]
                r = a * c + pltpu.roll(a, LANES - half, 1) * s1 + pltpu.roll(a, half, 1) * s2
                dst[:, sl] = (r * mul).astype(BF16)
        vo_ref[...] = v_ref[...].astype(BF16)

    tab = pl.BlockSpec((TR, LANES), lambda i: (i, 0))
    return pl.pallas_call(
        body, grid=(s // TR,),
        in_specs=[_row_spec(B_WIDTH, 2), _row_spec(B_WIDTH, 3), _row_spec(B_WIDTH, 4), tab, tab, tab],
        out_specs=[_row_spec(B_WIDTH)] * 3, out_shape=[jax.ShapeDtypeStruct((s, B_WIDTH), BF16)] * 3,
        compiler_params=_cparams("parallel"), name=name)(proj, proj, proj, *tabs)


def _band_mask(i):
    qi = lax.broadcasted_iota(jnp.int32, (BAND, 2 * BAND), 0)
    kj = lax.broadcasted_iota(jnp.int32, (BAND, 2 * BAND), 1)
    return (kj >= qi) & (kj <= qi + BAND) & ((kj >= BAND) | (i > 0))


def _head_masks():
    lane = lax.broadcasted_iota(jnp.int32, (1, LANES), 1)
    return lane < HEAD_DIM, lane >= HEAD_DIM


def _attn_specs(nb, last):
    cur = pl.BlockSpec((BAND, B_WIDTH), lambda r, i: (jnp.minimum(i, last), r))
    prev = pl.BlockSpec((BAND, B_WIDTH), lambda r, i: (jnp.maximum(jnp.minimum(i, last) - 1, 0), r))
    return cur, prev


def _attn_fwd(q, k, v, dil, name):
    s = q.shape[0]
    n = s // dil
    nb = n // BAND
    q2, k2, v2 = (t.reshape(n, dil * B_WIDTH) for t in (q, k, v))

    def body(q_ref, kc_ref, kp_ref, vc_ref, vp_ref, o_ref, l_ref):
        valid = _band_mask(pl.program_id(1))
        lo, hi = _head_masks()
        for sl in _lane_blocks(B_WIDTH):
            qb = q_ref[:, sl]
            kk = jnp.concatenate([kp_ref[:, sl], kc_ref[:, sl]], axis=0)
            vv = jnp.concatenate([vp_ref[:, sl], vc_ref[:, sl]], axis=0)
            outs, lses = [], []
            for hm in (lo, hi):
                sc = _dot(jnp.where(hm, qb, jnp.zeros_like(qb)), kk, NT)
                sc = jnp.where(valid, sc, NEG_INF)
                mx = jnp.max(sc, axis=1, keepdims=True)
                p = jnp.exp(sc - mx)
                den = jnp.sum(p, axis=1, keepdims=True)
                outs.append(_dot(p.astype(BF16), vv, NN) / den)
                lses.append(mx + jnp.log(den))
            o_ref[:, sl] = jnp.where(lo, outs[0], outs[1])
            l_ref[:, sl] = jnp.where(lo, lses[0], lses[1])

    cur, prev = _attn_specs(nb, nb - 1)
    o, lse = pl.pallas_call(
        body, grid=(dil, nb), in_specs=[cur, cur, prev, cur, prev], out_specs=[cur, cur],
        out_shape=[jax.ShapeDtypeStruct((n, dil * B_WIDTH), F32)] * 2,
        compiler_params=_cparams("parallel", "parallel"), name=name)(q2, k2, k2, v2, v2)
    return o.reshape(s, B_WIDTH), lse.reshape(s, B_WIDTH)


def _attn_combine(outs, lses, gb, name):
    s = outs[0].shape[0]
    npat = len(outs)

    def body(*refs):
        o_refs, l_refs = refs[:npat], refs[npat:2 * npat]
        g_ref, ob_ref, lse_ref, mb_ref = refs[2 * npat:]
        ls = [r[...] for r in l_refs]
        mx = functools.reduce(jnp.maximum, ls)
        ws = [jnp.exp(l - mx) for l in ls]
        tot = functools.reduce(lambda a, b: a + b, ws)
        ob = functools.reduce(lambda a, b: a + b, [w / tot * r[...] for w, r in zip(ws, o_refs)])
        ob_ref[...] = ob
        lse_ref[...] = mx + jnp.log(tot)
        mb_ref[...] = (ob * _rsq_mean(ob) * g_ref[...]).astype(BF16)

    w = B_WIDTH
    return pl.pallas_call(
        body, grid=(s // TR,), in_specs=[_row_spec(w)] * (2 * npat) + [_vec_spec(w)],
        out_specs=[_row_spec(w)] * 3,
        out_shape=[jax.ShapeDtypeStruct((s, w), F32), jax.ShapeDtypeStruct((s, w), F32),
                   jax.ShapeDtypeStruct((s, w), BF16)],
        compiler_params=_cparams("parallel"), name=name)(*outs, *lses, gb)


def _attn_bwd_prep(dmixed, ob, gb, name):
    s = ob.shape[0]
    w = B_WIDTH

    def body(dm_ref, ob_ref, g_ref, do_ref, dl_ref, dg_ref):
        _acc_init([dg_ref])
        ob = ob_ref[...]
        dob, dgt = _rms_bwd(ob, _rsq_mean(ob), g_ref[...], dm_ref[...])
        dg_ref[...] += _colsum(dgt)
        do_ref[...] = dob.astype(BF16)
        lo, hi = _head_masks()
        t = dob * ob
        for sl in _lane_blocks(w):
            tb = t[:, sl]
            s0 = jnp.sum(jnp.where(lo, tb, 0.0), axis=1, keepdims=True)
            s1 = jnp.sum(jnp.where(hi, tb, 0.0), axis=1, keepdims=True)
            dl_ref[:, sl] = jnp.where(lo, s0, s1)

    return pl.pallas_call(
        body, grid=(s // TR,), in_specs=[_row_spec(w, 1), _row_spec(w), _vec_spec(w)],
        out_specs=[_row_spec(w), _row_spec(w), _vec_spec(w)],
        out_shape=[jax.ShapeDtypeStruct((s, w), BF16), jax.ShapeDtypeStruct((s, w), F32),
                   jax.ShapeDtypeStruct((1, w), F32)],
        compiler_params=_cparams("arbitrary"), name=name)(dmixed, ob, gb)


def _attn_bwd(q, k, v, do, lse, delta, dil, name):
    s = q.shape[0]
    n = s // dil
    nb = n // BAND
    q2, k2, v2, do2, lse2, dl2 = (t.reshape(n, dil * B_WIDTH) for t in (q, k, v, do, lse, delta))

    def body(q_ref, kc_ref, kp_ref, vc_ref, vp_ref, do_ref, lse_ref, dl_ref,
             dq_ref, dk_ref, dv_ref, ck_ref, cv_ref):
        i = pl.program_id(1)

        @pl.when(i == 0)
        def _():
            ck_ref[...] = jnp.zeros_like(ck_ref)
            cv_ref[...] = jnp.zeros_like(cv_ref)

        @pl.when(i < nb)
        def _():
            valid = _band_mask(i)
            lo, hi = _head_masks()
            lane = lax.broadcasted_iota(jnp.int32, (1, LANES), 1)
            for sl in _lane_blocks(B_WIDTH):
                qb = q_ref[:, sl]
                dob = do_ref[:, sl]
                kk = jnp.concatenate([kp_ref[:, sl], kc_ref[:, sl]], axis=0)
                vv = jnp.concatenate([vp_ref[:, sl], vc_ref[:, sl]], axis=0)
                lseb = lse_ref[:, sl]
                dlb = dl_ref[:, sl]
                dq = jnp.zeros((BAND, LANES), F32)
                dkk = jnp.zeros((2 * BAND, LANES), F32)
                dvv = jnp.zeros((2 * BAND, LANES), F32)
                for hm, first in ((lo, 0), (hi, HEAD_DIM)):
                    pick = lane == first
                    lse_h = jnp.sum(jnp.where(pick, lseb, 0.0), axis=1, keepdims=True)
                    dl_h = jnp.sum(jnp.where(pick, dlb, 0.0), axis=1, keepdims=True)
                    qm = jnp.where(hm, qb, jnp.zeros_like(qb))
                    dom = jnp.where(hm, dob, jnp.zeros_like(dob))
                    sc = _dot(qm, kk, NT)
                    p = jnp.where(valid, jnp.exp(sc - lse_h), 0.0)
                    dp = _dot(dom, vv, NT)
                    ds = (p * (dp - dl_h)).astype(BF16)
                    dq += _dot(ds, jnp.where(hm, kk, jnp.zeros_like(kk)), NN)
                    dkk += _dot(ds, qm, TN)
                    dvv += _dot(p.astype(BF16), dom, TN)
                dq_ref[:, sl] = dq
                dk_ref[:, sl] = ck_ref[:, sl] + dkk[:BAND]
                dv_ref[:, sl] = cv_ref[:, sl] + dvv[:BAND]
                ck_ref[:, sl] = dkk[BAND:]
                cv_ref[:, sl] = dvv[BAND:]

        @pl.when(i == nb)
        def _():
            dk_ref[...] = ck_ref[...]
            dv_ref[...] = cv_ref[...]

    cur, prev = _attn_specs(nb, nb - 1)
    lag = pl.BlockSpec((BAND, B_WIDTH), lambda r, i: (jnp.maximum(i - 1, 0), r))
    shape = jax.ShapeDtypeStruct((n, dil * B_WIDTH), F32)
    dq, dk, dv = pl.pallas_call(
        body, grid=(dil, nb + 1), in_specs=[cur, cur, prev, cur, prev, cur, cur, cur],
        out_specs=[cur, lag, lag], out_shape=[shape] * 3,
        scratch_shapes=[pltpu.VMEM((BAND, B_WIDTH), F32)] * 2,
        compiler_params=_cparams("arbitrary", "arbitrary"), name=name)(q2, k2, k2, v2, v2, do2, lse2, dl2)
    return tuple(t.reshape(s, B_WIDTH) for t in (dq, dk, dv))


def _rope_bwd(dqs, dks, dvs, tabs, name):
    s = dqs[0].shape[0]
    half = ROT_DIM // 2
    scale = HEAD_DIM ** -0.5
    npat = len(dqs)
    w = B_WIDTH

    def body(*refs):
        dq_refs, dk_refs, dv_refs = refs[:npat], refs[npat:2 * npat], refs[2 * npat:3 * npat]
        c_ref, s1_ref, s2_ref, o_ref = refs[3 * npat:]
        c, s1, s2 = c_ref[...], s1_ref[...], s2_ref[...]
        add = lambda rs, sl: functools.reduce(lambda a, b: a + b, [r[:, sl] for r in rs])
        for b, sl in enumerate(_lane_blocks(w)):
            for which, rs, mul in ((0, dq_refs, scale), (1, dk_refs, 1.0)):
                g = add(rs, sl) * mul
                o = g * c + pltpu.roll(g * s1, half, 1) + pltpu.roll(g * s2, LANES - half, 1)
                o_ref[:, which * w + b * LANES:which * w + (b + 1) * LANES] = o.astype(BF16)
            o_ref[:, 2 * w + b * LANES:2 * w + (b + 1) * LANES] = add(dv_refs, sl).astype(BF16)

    tab = pl.BlockSpec((TR, LANES), lambda i: (i, 0))
    return pl.pallas_call(
        body, grid=(s // TR,), in_specs=[_row_spec(w)] * (3 * npat) + [tab] * 3,
        out_specs=_row_spec(3 * w), out_shape=jax.ShapeDtypeStruct((s, 3 * w), BF16),
        compiler_params=_cparams("parallel"), name=name)(*dqs, *dks, *dvs, *tabs)


TK = 512
HALO = 16


def _row_of(v, r):
    rows = lax.broadcasted_iota(jnp.int32, (v.shape[0], 1), 0)
    return jnp.sum(jnp.where(rows == r, v, 0.0), axis=0, keepdims=True)


def _shift_down(x, halo, n):
    row = lax.broadcasted_iota(jnp.int32, (x.shape[0], 1), 0)
    out = pltpu.roll(x, n, 0)
    for j in range(n):
        out = jnp.where(row == j, _row_of(halo, HALO - n + j), out)
    return out


def _shift_up(x, halo, n):
    rows = x.shape[0]
    row = lax.broadcasted_iota(jnp.int32, (rows, 1), 0)
    out = pltpu.roll(x, rows - n, 0)
    for j in range(n):
        out = jnp.where(row == rows - n + j, _row_of(halo, j), out)
    return out


def _conv_value(x, halo, cw_ref, cb_ref, h):
    taps = (_shift_down(x, halo, 2), _shift_down(x, halo, 1), x)
    conv = cb_ref[h] + cw_ref[h, 0:1, :] * taps[0] + cw_ref[h, 1:2, :] * taps[1] + cw_ref[h, 2:3, :] * taps[2]
    return conv, taps


def _ffn_weight_specs(layer, ncol):
    per_up = (2 * D_FF // N_CHIPS) // TK
    per_dn = (D_FF // N_CHIPS) // TK
    wg = pl.BlockSpec((None, None, D_MODEL, TK), lambda i, j: (j // per_up, layer, 0, j % per_up))
    wv = pl.BlockSpec((None, None, D_MODEL, TK), lambda i, j: ((j + ncol) // per_up, layer, 0, (j + ncol) % per_up))
    wd = pl.BlockSpec((None, None, TK, D_MODEL), lambda i, j: (j // per_dn, layer, j % per_dn, 0))
    cw = pl.BlockSpec((2, 3, TK), lambda i, j: (0, 0, j))
    cb = pl.BlockSpec((2, 1, TK), lambda i, j: (0, 0, j))
    return wg, wv, wd, cw, cb


def _ffn_forward(h2, w_up, w_down, layer, cw3, cb3, name):
    s = h2.shape[0]
    nm, ncol = s // TM, D_FF // TK

    def body(h_ref, wg_ref, wv_ref, wd_ref, cw_ref, cb_ref, y_ref, up_ref, f_ref, carry, acc):
        i, j = pl.program_id(0), pl.program_id(1)

        @pl.when((i == 0) & (j == 0))
        def _():
            carry[...] = jnp.zeros_like(carry)

        h = h_ref[...]
        conv = []
        for hh, w_ref in ((0, wg_ref), (1, wv_ref)):
            up = _dot(h, w_ref[...], NN).astype(BF16)
            up_ref[hh] = up
            x = up.astype(F32)
            conv.append(_conv_value(x, carry[j, hh], cw_ref, cb_ref, hh)[0])
            carry[j, hh] = x[TM - HALO:, :]
        y = (_gelu_tanh(conv[0])[0] * conv[1]).astype(BF16)
        y_ref[...] = y
        part = _dot(y, wd_ref[...], NN)

        @pl.when(j == 0)
        def _():
            acc[...] = part

        @pl.when(j > 0)
        def _():
            acc[...] += part

        @pl.when(j == ncol - 1)
        def _():
            f_ref[...] = acc[...]

    wg, wv, wd, cw, cb = _ffn_weight_specs(layer, ncol)
    return pl.pallas_call(
        body, grid=(nm, ncol),
        in_specs=[pl.BlockSpec((TM, D_MODEL), lambda i, j: (i, 0)), wg, wv, wd, cw, cb],
        out_specs=[pl.BlockSpec((TM, TK), lambda i, j: (i, j)), pl.BlockSpec((2, TM, TK), lambda i, j: (0, i, j)),
                   pl.BlockSpec((TM, D_MODEL), lambda i, j: (i, 0))],
        out_shape=[jax.ShapeDtypeStruct((s, D_FF), BF16), jax.ShapeDtypeStruct((2, s, D_FF), BF16),
                   jax.ShapeDtypeStruct((s, D_MODEL), F32)],
        scratch_shapes=[pltpu.VMEM((ncol, 2, HALO, TK), F32), pltpu.VMEM((TM, D_MODEL), F32)],
        compiler_params=_cparams("arbitrary", "arbitrary"), name=name)(h2, w_up, w_up, w_down, cw3, cb3)


def _ffn_backward(df, w_up, w_down, layer, up3, cw3, cb3, name):
    s = df.shape[0]
    nm, ncol = s // TM, D_FF // TK

    def body(df_ref, wg_ref, wv_ref, wd_ref, cw_ref, cb_ref, up_ref, halo_ref, dup_ref, dh_ref, sums_ref, carry, acc):
        i, j = pl.program_id(0), pl.program_id(1)

        @pl.when((i == 0) & (j == 0))
        def _():
            carry[...] = jnp.zeros_like(carry)
            sums_ref[...] = jnp.zeros_like(sums_ref)

        seq_first = i == nm - 1
        dy = _dot(df_ref[...], wd_ref[...], NT)
        conv, taps = [], []
        for hh in range(2):
            halo = jnp.where(seq_first, 0.0, halo_ref[hh].astype(F32))
            cv, tp = _conv_value(up_ref[hh].astype(F32), halo, cw_ref, cb_ref, hh)
            conv.append(cv)
            taps.append(tp)
        act, t = _gelu_tanh(conv[0])
        dcs = (dy * conv[1] * _gelu_tanh_grad(conv[0], t), dy * act)
        row = lax.broadcasted_iota(jnp.int32, (8, 1), 0)
        part = None
        for hh, w_ref in ((0, wg_ref), (1, wv_ref)):
            dc, tp = dcs[hh], taps[hh]
            upd = jnp.zeros((8, TK), F32)
            for ridx, sm in enumerate((_colsum(dc * tp[0]), _colsum(dc * tp[1]), _colsum(dc * tp[2]), _colsum(dc))):
                upd = jnp.where(row == ridx, sm, upd)
            sums_ref[j, hh] += upd
            nxt = carry[j, hh]
            dup = (cw_ref[hh, 2:3, :] * dc + cw_ref[hh, 1:2, :] * _shift_up(dc, nxt, 1)
                   + cw_ref[hh, 0:1, :] * _shift_up(dc, nxt, 2)).astype(BF16)
            carry[j, hh] = dc[:HALO, :]
            dup_ref[hh] = dup
            d = _dot(dup, w_ref[...], NT)
            part = d if part is None else part + d

        @pl.when(j == 0)
        def _():
            acc[...] = part

        @pl.when(j > 0)
        def _():
            acc[...] += part

        @pl.when(j == ncol - 1)
        def _():
            dh_ref[...] = acc[...]

    wg, wv, wd, cw, cb = _ffn_weight_specs(layer, ncol)
    rev = lambda i: nm - 1 - i
    return pl.pallas_call(
        body, grid=(nm, ncol),
        in_specs=[pl.BlockSpec((TM, D_MODEL), lambda i, j: (rev(i), 0)), wg, wv, wd, cw, cb,
                  pl.BlockSpec((2, TM, TK), lambda i, j: (0, rev(i), j)),
                  pl.BlockSpec((2, HALO, TK), lambda i, j: (0, jnp.maximum(rev(i) * (TM // HALO) - 1, 0), j))],
        out_specs=[pl.BlockSpec((2, TM, TK), lambda i, j: (0, rev(i), j)),
                   pl.BlockSpec((TM, D_MODEL), lambda i, j: (rev(i), 0)),
                   pl.BlockSpec((ncol, 2, 8, TK), lambda i, j: (0, 0, 0, 0))],
        out_shape=[jax.ShapeDtypeStruct((2, s, D_FF), BF16), jax.ShapeDtypeStruct((s, D_MODEL), F32),
                   jax.ShapeDtypeStruct((ncol, 2, 8, TK), F32)],
        scratch_shapes=[pltpu.VMEM((ncol, 2, HALO, TK), F32), pltpu.VMEM((TM, D_MODEL), F32)],
        compiler_params=_cparams("arbitrary", "arbitrary"), name=name)(df, w_up, w_up, w_down, cw3, cb3, up3, up3)


def _wspec(rows, cols, index_map):
    return pl.BlockSpec((None, None, rows, cols), index_map)


def _layer_forward(l, x0, h1, p, wg, tabs):
    s = x0.shape[0]
    nm = s // TMM
    tag = f"_l{l}"
    proj = _matmul(
        h1, wg["w_in"], grid=(nm, N_CHIPS), a_spec=pl.BlockSpec((TMM, D_MODEL), lambda i, j: (i, 0)),
        b_spec=_wspec(D_MODEL, IN_COLS // N_CHIPS, lambda i, j: (j, l, 0, 0)),
        o_spec=pl.BlockSpec((TMM, IN_COLS // N_CHIPS), lambda i, j: (i, j)), o_shape=(s, IN_COLS), o_dtype=F32,
        dims=NN, nk=1, kaxis=None, acc_shape=None, name="proj" + tag)
    ma = _mixer_a_fwd(proj, p["v_norm_g"], p["v_norm_b"], p["w_spatial"], p["bs_full"], p["out_norm_a"],
                      "mixer_a_fwd" + tag)
    q, k, v = _rope_fwd(proj, tabs, "rope_fwd" + tag)
    outs, lses = zip(*[_attn_fwd(q, k, v, d, f"attn_fwd_d{d}" + tag) for d in DILATIONS])
    ob, lse, mb = _attn_combine(outs, lses, p["out_norm_b"], "attn_combine" + tag)
    mixed = jnp.concatenate([ma, mb], axis=1)
    w_out_all = pl.BlockSpec((N_CHIPS, None, D_MODEL // N_CHIPS, D_MODEL), lambda i: (0, l, 0, 0))
    y1 = _matmul(
        mixed, wg["w_out"], grid=(nm,), a_spec=pl.BlockSpec((TMM, D_MODEL), lambda i: (i, 0)), b_spec=w_out_all,
        o_spec=pl.BlockSpec((TMM, D_MODEL), lambda i: (i, 0)), o_shape=(s, D_MODEL), o_dtype=F32,
        dims=NN, nk=1, kaxis=None, acc_shape=None, name="mix_out" + tag, b_2d=(D_MODEL, D_MODEL))
    x1, h2 = _residual_norm(x0, y1, p["post_mix_norm"], p["pre_ffn_norm"], "post_mix" + tag)
    y, up3, f = _ffn_forward(h2, wg["w_up"], wg["w_down"], l, p["cw3"], p["cb3"], "ffn_fwd" + tag)
    saved = dict(x0=x0, h1=h1, proj=proj, q=q, k=k, v=v, ob=ob, lse=lse, mixed=mixed, y1=y1, x1=x1, h2=h2,
                 up3=up3, y=y, f=f)
    return saved


def _layer_backward(l, dx2, sv, p, wg, tabs):
    s = dx2.shape[0]
    nm = s // TMM
    tag = f"_l{l}"
    g = {}
    df, g["post_ffn_norm"] = _norm_bwd_out(dx2, sv["f"], p["post_ffn_norm"], "norm_bwd_out" + tag)
    dup3, dh2, conv_sums = _ffn_backward(df, wg["w_up"], wg["w_down"], l, sv["up3"], p["cw3"], p["cb3"],
                                         "ffn_bwd" + tag)
    sums = conv_sums.transpose(1, 2, 0, 3).reshape(2, 8, D_FF)
    g["conv_w"] = jnp.concatenate([sums[0, :3], sums[1, :3]], axis=1)
    g["conv_b"] = jnp.concatenate([sums[0, 3:4], sums[1, 3:4]], axis=1)
    tn = 1024
    gw_up = _matmul(
        sv["h2"], dup3, grid=(2 * D_FF // tn, nm), a_spec=pl.BlockSpec((TMM, D_MODEL), lambda n, m: (m, 0)),
        b_spec=pl.BlockSpec((None, TMM, tn), lambda n, m: (n // (D_FF // tn), m, n % (D_FF // tn))),
        o_spec=pl.BlockSpec((None, D_MODEL, tn), lambda n, m: (n // 2, 0, n % 2)),
        o_shape=(N_CHIPS, D_MODEL, 2 * D_FF // N_CHIPS), o_dtype=BF16,
        dims=TN, nk=nm, kaxis=1, acc_shape=(D_MODEL, tn), name="w_up_grad" + tag)
    gw_down = _matmul(
        sv["y"], df, grid=(D_FF // tn, 2, nm), a_spec=pl.BlockSpec((TMM, tn), lambda k, h, m: (m, k)),
        b_spec=pl.BlockSpec((TMM, D_MODEL // 2), lambda k, h, m: (m, h)),
        o_spec=pl.BlockSpec((None, tn, D_MODEL // 2), lambda k, h, m: (h, k, 0)),
        o_shape=(2, D_FF, D_MODEL // 2), o_dtype=BF16,
        dims=TN, nk=nm, kaxis=2, acc_shape=(tn, D_MODEL // 2), name="w_down_grad" + tag)
    dx1, dy1, g["pre_ffn_norm"], g["post_mix_norm"] = _norm_bwd_mid(
        dx2, dh2, sv["x1"], p["pre_ffn_norm"], sv["y1"], p["post_mix_norm"], "norm_bwd_mid" + tag)
    w_out_all = pl.BlockSpec((N_CHIPS, None, D_MODEL // N_CHIPS, D_MODEL), lambda i: (0, l, 0, 0))
    dmixed = _matmul(
        dy1, wg["w_out"], grid=(nm,), a_spec=pl.BlockSpec((TMM, D_MODEL), lambda i: (i, 0)), b_spec=w_out_all,
        o_spec=pl.BlockSpec((TMM, D_MODEL), lambda i: (i, 0)), o_shape=(s, D_MODEL), o_dtype=F32,
        dims=NT, nk=1, kaxis=None, acc_shape=None, name="mix_out_bwd" + tag, b_2d=(D_MODEL, D_MODEL))
    gw_out = _matmul(
        sv["mixed"], dy1, grid=(2, nm), a_spec=pl.BlockSpec((TMM, D_MODEL), lambda h, m: (m, 0)),
        b_spec=pl.BlockSpec((TMM, D_MODEL // 2), lambda h, m: (m, h)),
        o_spec=pl.BlockSpec((None, D_MODEL, D_MODEL // 2), lambda h, m: (h, 0, 0)),
        o_shape=(2, D_MODEL, D_MODEL // 2), o_dtype=BF16,
        dims=TN, nk=nm, kaxis=1, acc_shape=(D_MODEL, D_MODEL // 2), name="w_out_grad" + tag)
    dpa, g["out_norm_a"], g["v_norm_g"], g["v_norm_b"], dbs, g["w_spatial"] = _mixer_a_bwd(
        sv["proj"], dmixed, p["v_norm_g"], p["v_norm_b"], p["w_spatial"], p["bs_full"], p["out_norm_a"],
        "mixer_a_bwd" + tag)
    g["b_spatial"] = dbs[:, ::GROUP_DIM].T
    dob, delta, g["out_norm_b"] = _attn_bwd_prep(dmixed, sv["ob"], p["out_norm_b"], "attn_bwd_prep" + tag)
    dqs, dks, dvs = zip(*[
        _attn_bwd(sv["q"], sv["k"], sv["v"], dob, sv["lse"], delta, d, f"attn_bwd_d{d}" + tag) for d in DILATIONS])
    dqkv = _rope_bwd(dqs, dks, dvs, tabs, "rope_bwd" + tag)
    dproj = jnp.concatenate([dpa, dqkv], axis=1)
    wcol = IN_COLS // N_CHIPS
    dh1 = _matmul(
        dproj, wg["w_in"], grid=(nm, N_CHIPS), a_spec=pl.BlockSpec((TMM, wcol), lambda i, n: (i, n)),
        b_spec=_wspec(D_MODEL, wcol, lambda i, n: (n, l, 0, 0)),
        o_spec=pl.BlockSpec((TMM, D_MODEL), lambda i, n: (i, 0)), o_shape=(s, D_MODEL), o_dtype=F32,
        dims=NT, nk=N_CHIPS, kaxis=1, acc_shape=(TMM, D_MODEL), name="proj_bwd" + tag)
    gw_in = _matmul(
        sv["h1"], dproj, grid=(N_CHIPS, nm), a_spec=pl.BlockSpec((TMM, D_MODEL), lambda n, m: (m, 0)),
        b_spec=pl.BlockSpec((TMM, wcol), lambda n, m: (m, n)),
        o_spec=pl.BlockSpec((None, D_MODEL, wcol), lambda n, m: (n, 0, 0)),
        o_shape=(N_CHIPS, D_MODEL, wcol), o_dtype=BF16,
        dims=TN, nk=nm, kaxis=1, acc_shape=(D_MODEL, wcol), name="w_in_grad" + tag)
    dx0, g["pre_mix_norm"] = _norm_bwd_in(dx1, dh1, sv["x0"], p["pre_mix_norm"], "norm_bwd_in" + tag)
    big = dict(w_in=gw_in, w_up=gw_up, w_out=gw_out, w_down=gw_down)
    return dx0, big, g


SMALL = ("pre_mix_norm", "v_norm_g", "v_norm_b", "w_spatial", "b_spatial", "out_norm_a", "out_norm_b",
         "post_mix_norm", "pre_ffn_norm", "conv_b", "post_ffn_norm")
BIG = ("w_in", "w_out", "w_up", "w_down")
DEPTH = 2


def _layer_params(l, small, conv_w_full):
    p = {n: small[n][l].reshape(1, -1) for n in SMALL if n not in ("w_spatial", "b_spatial")}
    p["w_spatial"] = small["w_spatial"][l]
    p["bs_full"] = jnp.repeat(small["b_spatial"][l].T, GROUP_DIM, axis=1)
    p["cw3"] = conv_w_full[l].reshape(3, 2, D_FF).transpose(1, 0, 2)
    p["cb3"] = small["conv_b"][l].reshape(2, 1, D_FF)
    return p


def _local_step(x, target, small, conv_w_full, wg):
    s = x.shape[0]
    tabs = _rope_tables(s)
    params = [_layer_params(l, small, conv_w_full) for l in range(DEPTH)]
    saved = []
    xin = x
    h = _rms_cast(xin, params[0]["pre_mix_norm"], "pre_mix_l0")
    for l in range(DEPTH):
        sv = _layer_forward(l, xin, h, params[l], wg, tabs)
        saved.append(sv)
        if l + 1 < DEPTH:
            xin, h = _residual_norm(sv["x1"], sv["f"], params[l]["post_ffn_norm"], params[l + 1]["pre_mix_norm"],
                                    f"post_ffn_l{l}")
    loss, dx = _residual_loss(saved[-1]["x1"], saved[-1]["f"], params[-1]["post_ffn_norm"], target, "loss")
    bigs, smalls = [None] * DEPTH, [None] * DEPTH
    for l in reversed(range(DEPTH)):
        dx, bigs[l], smalls[l] = _layer_backward(l, dx, saved[l], params[l], wg, tabs)
    return loss, dx, bigs, smalls


ANY = pl.BlockSpec(memory_space=pl.ANY)


def _mesh_pos():
    return lax.axis_index("x"), lax.axis_index("y"), lax.axis_index("c")


def _other_chips(x, y):
    return [(1 - x, y), (x, 1 - y), (1 - x, 1 - y)]


def _gather_weights(shards):
    nw = len(shards)
    nrel = N_CHIPS - 1

    def body(*refs):
        ins, outs = refs[:nw], refs[nw:2 * nw]
        send, recv, local_sem = refs[2 * nw:]
        x, y, c = _mesh_pos()
        mine = 2 * x + y
        sibling = (x, y, 1 - c)
        chips = _other_chips(x, y)

        def copy(src, dst, slot, to):
            return pltpu.make_async_remote_copy(src_ref=src, dst_ref=dst, send_sem=send.at[slot],
                                                recv_sem=recv.at[slot], device_id=to, device_id_type=MESH)

        local = [pltpu.make_async_copy(ins[t], outs[t].at[mine], local_sem.at[t]) for t in range(nw)]
        for cp in local:
            cp.start()
        sends = []
        for t in range(nw):
            for r, (px, py) in enumerate(chips):
                cp = copy(ins[t].at[c], outs[t].at[mine, c], t * nrel + r, (px, py, c))
                cp.start()
                sends.append(cp)
        for t in range(nw):
            for r, (px, py) in enumerate(chips):
                landed = outs[t].at[2 * px + py, c]
                copy(landed, landed, t * nrel + r, (px, py, c)).wait_recv()
                cp = copy(landed, landed, nw * nrel + t * nrel + r, sibling)
                cp.start()
                sends.append(cp)
        for t in range(nw):
            for r, (px, py) in enumerate(chips):
                passed = outs[t].at[2 * px + py, 1 - c]
                copy(passed, passed, nw * nrel + t * nrel + r, sibling).wait_recv()
        for cp in sends:
            cp.wait_send()
        for cp in local:
            cp.wait()

    nsem = 2 * nw * nrel
    return pl.pallas_call(
        body, in_specs=[ANY] * nw, out_specs=[ANY] * nw,
        out_shape=[jax.ShapeDtypeStruct((N_CHIPS,) + a.shape, a.dtype) for a in shards],
        scratch_shapes=[pltpu.SemaphoreType.DMA((nsem,)), pltpu.SemaphoreType.DMA((nsem,)),
                        pltpu.SemaphoreType.DMA((nw,))],
        name="gather_weights")(*shards)


HALF_ROWS = D_MODEL // 2


def _pair_exchange(g, name):
    shapes = [(N_CHIPS, HALF_ROWS, IN_COLS // N_CHIPS), (N_CHIPS, HALF_ROWS, 2 * D_FF // N_CHIPS),
              (D_MODEL, D_MODEL // 2), (D_FF, D_MODEL // 2)]

    def body(gin, gup, gout, gdn, rin, rup, rout, rdn, send, recv):
        x, y, c = _mesh_pos()
        o = 1 - c
        rows = pl.ds(pl.multiple_of(o * HALF_ROWS, HALF_ROWS), HALF_ROWS)
        pairs = [(gin.at[:, rows, :], rin), (gup.at[:, rows, :], rup), (gout.at[o], rout), (gdn.at[o], rdn)]
        cps = [pltpu.make_async_remote_copy(src_ref=src, dst_ref=dst, send_sem=send.at[t], recv_sem=recv.at[t],
                                            device_id=(x, y, o), device_id_type=MESH)
               for t, (src, dst) in enumerate(pairs)]
        for cp in cps:
            cp.start()
        for cp in cps:
            cp.wait()

    return pl.pallas_call(
        body, in_specs=[ANY] * 4, out_specs=[ANY] * 4,
        out_shape=[jax.ShapeDtypeStruct(sh, BF16) for sh in shapes],
        scratch_shapes=[pltpu.SemaphoreType.DMA((4,)), pltpu.SemaphoreType.DMA((4,))],
        name=name)(g["w_in"], g["w_up"], g["w_out"], g["w_down"])


def _pair_sum(g, recv, pos, name_prefix):
    def add(a, b, grid, a_spec, b_spec, shape, name):
        def body(pos_ref, a_ref, b_ref, o_ref):
            o_ref[...] = (a_ref[...].astype(F32) + b_ref[...].astype(F32)).astype(BF16)

        return pl.pallas_call(
            body, grid_spec=pltpu.PrefetchScalarGridSpec(
                num_scalar_prefetch=1, grid=grid, in_specs=[a_spec, b_spec], out_specs=b_spec),
            out_shape=jax.ShapeDtypeStruct(shape, BF16), compiler_params=_cparams(*["parallel"] * len(grid)),
            name=name)(pos, a, b)

    rin, rup, rout, rdn = recv
    wi, wu = IN_COLS // N_CHIPS, 2 * D_FF // N_CHIPS
    s_in = add(g["w_in"], rin, (N_CHIPS,), pl.BlockSpec((None, HALF_ROWS, wi), lambda j, pos: (j, pos[2], 0)),
               pl.BlockSpec((None, HALF_ROWS, wi), lambda j, pos: (j, 0, 0)), rin.shape, name_prefix + "_in")
    s_up = add(g["w_up"], rup, (N_CHIPS,), pl.BlockSpec((None, HALF_ROWS, wu), lambda j, pos: (j, pos[2], 0)),
               pl.BlockSpec((None, HALF_ROWS, wu), lambda j, pos: (j, 0, 0)), rup.shape, name_prefix + "_up")
    hc = D_MODEL // 2
    s_out = add(g["w_out"], rout, (1,), pl.BlockSpec((None, D_MODEL, hc), lambda j, pos: (pos[2], 0, 0)),
                pl.BlockSpec((D_MODEL, hc), lambda j, pos: (0, 0)), rout.shape, name_prefix + "_out")
    s_dn = add(g["w_down"], rdn, (N_CHIPS,), pl.BlockSpec((None, D_FF // N_CHIPS, hc), lambda j, pos: (pos[2], j, 0)),
               pl.BlockSpec((D_FF // N_CHIPS, hc), lambda j, pos: (j, 0)), rdn.shape, name_prefix + "_down")
    return s_in, s_up, s_out, s_dn


OUT_ROWS = D_MODEL // N_CHIPS
DOWN_ROWS = D_FF // N_CHIPS


def _chip_scatter(sums, name):
    nrel = N_CHIPS - 1
    shapes = [(nrel, HALF_ROWS, IN_COLS // N_CHIPS), (nrel, HALF_ROWS, 2 * D_FF // N_CHIPS),
              (nrel, OUT_ROWS, D_MODEL // 2), (nrel, DOWN_ROWS, D_MODEL // 2)]

    def body(sin, sup, sout, sdn, rin, rup, rout, rdn, send, recv):
        x, y, c = _mesh_pos()
        cps = []
        for r, (px, py) in enumerate(_other_chips(x, y)):
            j = 2 * px + py
            pieces = [(sin.at[j], rin), (sup.at[j], rup),
                      (sout.at[pl.ds(pl.multiple_of(j * OUT_ROWS, OUT_ROWS), OUT_ROWS), :], rout),
                      (sdn.at[pl.ds(pl.multiple_of(j * DOWN_ROWS, DOWN_ROWS), DOWN_ROWS), :], rdn)]
            for t, (src, dst) in enumerate(pieces):
                cp = pltpu.make_async_remote_copy(src_ref=src, dst_ref=dst.at[r], send_sem=send.at[t * nrel + r],
                                                  recv_sem=recv.at[t * nrel + r], device_id=(px, py, c),
                                                  device_id_type=MESH)
                cp.start()
                cps.append(cp)
        for cp in cps:
            cp.wait()

    return pl.pallas_call(
        body, in_specs=[ANY] * 4, out_specs=[ANY] * 4,
        out_shape=[jax.ShapeDtypeStruct(sh, BF16) for sh in shapes],
        scratch_shapes=[pltpu.SemaphoreType.DMA((4 * nrel,)), pltpu.SemaphoreType.DMA((4 * nrel,))],
        name=name)(*sums)


def _chip_sum(sums, recv, pos, name_prefix):
    def add(a, b, a_spec, shape, name):
        def body(pos_ref, a_ref, b_ref, o_ref):
            tot = a_ref[...].astype(F32)
            for r in range(N_CHIPS - 1):
                tot = tot + b_ref[r].astype(F32)
            o_ref[...] = tot

        return pl.pallas_call(
            body, grid_spec=pltpu.PrefetchScalarGridSpec(
                num_scalar_prefetch=1, grid=(1,), in_specs=[a_spec, pl.BlockSpec(b.shape, lambda i, pos: (0, 0, 0))],
                out_specs=pl.BlockSpec(shape, lambda i, pos: (0, 0))),
            out_shape=jax.ShapeDtypeStruct(shape, F32), compiler_params=_cparams("arbitrary"), name=name)(pos, a, b)

    s_in, s_up, s_out, s_dn = sums
    rin, rup, rout, rdn = recv
    wi, wu, hc = IN_COLS // N_CHIPS, 2 * D_FF // N_CHIPS, D_MODEL // 2
    chip = lambda pos: 2 * pos[0] + pos[1]
    t_in = add(s_in, rin, pl.BlockSpec((None, HALF_ROWS, wi), lambda i, pos: (chip(pos), 0, 0)), (HALF_ROWS, wi),
               name_prefix + "_in")
    t_up = add(s_up, rup, pl.BlockSpec((None, HALF_ROWS, wu), lambda i, pos: (chip(pos), 0, 0)), (HALF_ROWS, wu),
               name_prefix + "_up")
    t_out = add(s_out, rout, pl.BlockSpec((OUT_ROWS, hc), lambda i, pos: (chip(pos), 0)), (OUT_ROWS, hc),
                name_prefix + "_out")
    t_dn = add(s_dn, rdn, pl.BlockSpec((DOWN_ROWS, hc), lambda i, pos: (chip(pos), 0)), (DOWN_ROWS, hc),
               name_prefix + "_down")
    return t_in, t_up, t_out, t_dn


def _pair_share(totals, name):
    hc = D_MODEL // 2
    shapes = [(D_MODEL, IN_COLS // N_CHIPS), (D_MODEL, 2 * D_FF // N_CHIPS), (2, OUT_ROWS, hc), (2, DOWN_ROWS, hc)]

    def body(tin, tup, tout, tdn, fin, fup, fout, fdn, send, recv, local_sem):
        x, y, c = _mesh_pos()
        rows = pl.ds(pl.multiple_of(c * HALF_ROWS, HALF_ROWS), HALF_ROWS)
        pairs = [(tin, fin.at[rows, :]), (tup, fup.at[rows, :]), (tout, fout.at[c]), (tdn, fdn.at[c])]
        local = [pltpu.make_async_copy(src, dst, local_sem.at[t]) for t, (src, dst) in enumerate(pairs)]
        cps = [pltpu.make_async_remote_copy(src_ref=src, dst_ref=dst, send_sem=send.at[t], recv_sem=recv.at[t],
                                            device_id=(x, y, 1 - c), device_id_type=MESH)
               for t, (src, dst) in enumerate(pairs)]
        for cp in local + cps:
            cp.start()
        o = 1 - c
        orows = pl.ds(pl.multiple_of(o * HALF_ROWS, HALF_ROWS), HALF_ROWS)
        theirs = [fin.at[orows, :], fup.at[orows, :], fout.at[o], fdn.at[o]]
        for t, dst in enumerate(theirs):
            pltpu.make_async_remote_copy(src_ref=dst, dst_ref=dst, send_sem=send.at[t], recv_sem=recv.at[t],
                                         device_id=(x, y, o), device_id_type=MESH).wait_recv()
        for cp in cps:
            cp.wait_send()
        for cp in local:
            cp.wait()

    return pl.pallas_call(
        body, in_specs=[ANY] * 4, out_specs=[ANY] * 4,
        out_shape=[jax.ShapeDtypeStruct(sh, F32) for sh in shapes],
        scratch_shapes=[pltpu.SemaphoreType.DMA((4,)), pltpu.SemaphoreType.DMA((4,)), pltpu.SemaphoreType.DMA((4,))],
        name=name)(*totals)


def _reduce_scatter_layer(l, g, pos):
    recv = _pair_exchange(g, f"pair_exchange_l{l}")
    sums = _pair_sum(g, recv, pos, f"pair_sum_l{l}")
    recv2 = _chip_scatter(sums, f"chip_scatter_l{l}")
    totals = _chip_sum(sums, recv2, pos, f"chip_sum_l{l}")
    f_in, f_up, f_out, f_dn = _pair_share(totals, f"pair_share_l{l}")
    f_out = f_out.transpose(1, 0, 2).reshape(OUT_ROWS, D_MODEL)
    f_dn = f_dn.transpose(1, 0, 2).reshape(DOWN_ROWS, D_MODEL)
    return dict(w_in=f_in, w_up=f_up, w_out=f_out, w_down=f_dn)


N_DEV = 8


def _allreduce_small(packed, name):
    rows = packed.shape[0]

    def body(x_ref, out_ref, gath, send_sems, recv_sems, local_sem):
        x, y, c = _mesh_pos()
        me, sibling = (x, y, c), (x, y, 1 - c)
        chips = _other_chips(x, y)

        def blk(px, py, pc):
            return gath.at[pl.ds(pl.multiple_of((4 * px + 2 * py + pc) * rows, 8), rows), :]

        def copy(k, block, to, src=None):
            return pltpu.make_async_remote_copy(
                src_ref=blk(*block) if src is None else src, dst_ref=blk(*block), send_sem=send_sems.at[k],
                recv_sem=recv_sems.at[k], device_id=to, device_id_type=MESH)

        mine = pltpu.make_async_copy(x_ref, blk(*me), local_sem)
        mine.start()
        first = [copy(0, me, sibling, src=x_ref)]
        first += [copy(1 + j, me, (*chip, c), src=x_ref) for j, chip in enumerate(chips)]
        for cp in first:
            cp.start()
        passed = [copy(4 + j, (*chip, c), sibling) for j, chip in enumerate(chips)]
        for j, chip in enumerate(chips):
            copy(1 + j, (*chip, c), me).wait_recv()
            passed[j].start()
        copy(0, sibling, me).wait_recv()
        for j, chip in enumerate(chips):
            copy(4 + j, (*chip, 1 - c), me).wait_recv()
        for cp in first + passed:
            cp.wait_send()
        mine.wait()
        tot = gath[0:rows, :]
        for d in range(1, N_DEV):
            tot = tot + gath[d * rows:(d + 1) * rows, :]
        out_ref[...] = tot

    vmem = pl.BlockSpec(memory_space=pltpu.VMEM)
    return pl.pallas_call(
        body, in_specs=[vmem], out_specs=vmem, out_shape=jax.ShapeDtypeStruct((rows, LANES), F32),
        scratch_shapes=[pltpu.VMEM((N_DEV * rows, LANES), F32), pltpu.SemaphoreType.DMA((7,)),
                        pltpu.SemaphoreType.DMA((7,)), pltpu.SemaphoreType.DMA],
        compiler_params=pltpu.CompilerParams(vmem_limit_bytes=VMEM_LIMIT_BYTES),
        name=name)(packed)


def _adamw(w, g, m, v, name):
    rows, cols = w.shape
    tr = 256 if rows % 256 == 0 else rows

    def body(w_ref, g_ref, m_ref, v_ref, d_ref, mo_ref, vo_ref):
        gv = g_ref[...]
        mn = ADAM_B1 * m_ref[...] + (1.0 - ADAM_B1) * gv
        vn = ADAM_B2 * v_ref[...] + (1.0 - ADAM_B2) * (gv * gv)
        m_hat = mn / (1.0 - ADAM_B1 ** ADAM_STEP)
        v_hat = vn / (1.0 - ADAM_B2 ** ADAM_STEP)
        d_ref[...] = -ADAM_LR * (m_hat / (jnp.sqrt(v_hat) + ADAM_EPS) + ADAM_WD * w_ref[...])
        mo_ref[...] = mn
        vo_ref[...] = vn

    spec = pl.BlockSpec((tr, cols), lambda i: (i, 0))
    return pl.pallas_call(
        body, grid=(rows // tr,), in_specs=[spec] * 4, out_specs=[spec] * 3,
        out_shape=[jax.ShapeDtypeStruct((rows, cols), F32)] * 3, compiler_params=_cparams("parallel"),
        name=name)(w, g, m, v)


def _adamw_nd(w, g, m, v, name):
    cols = w.shape[-1] if w.shape[-1] % LANES == 0 else LANES
    outs = _adamw(*(t.reshape(-1, cols) for t in (w, g, m, v)), name)
    return tuple(t.reshape(w.shape) for t in outs)


def _pack(arrays):
    return jnp.concatenate([a.reshape(-1, LANES) for a in arrays], axis=0)


def _unpack(packed, shapes):
    out, row = [], 0
    for sh in shapes:
        n = math.prod(sh) // LANES
        out.append(packed[row:row + n].reshape(sh))
        row += n
    return out


WEIGHTS = ("pre_mix_norm", "w_in", "v_norm_g", "v_norm_b", "w_spatial", "b_spatial", "out_norm_a", "out_norm_b",
           "w_out", "post_mix_norm", "pre_ffn_norm", "w_up", "conv_w", "conv_b", "w_down", "post_ffn_norm")


def kernel(x, pre_mix_norm, w_in, v_norm_g, v_norm_b, w_spatial, b_spatial, out_norm_a, out_norm_b, w_out, post_mix_norm, pre_ffn_norm, w_up, conv_w, conv_b, w_down, post_ffn_norm, loss_target, m_pre_mix_norm, m_w_in, m_v_norm_g, m_v_norm_b, m_w_spatial, m_b_spatial, m_out_norm_a, m_out_norm_b, m_w_out, m_post_mix_norm, m_pre_ffn_norm, m_w_up, m_conv_w, m_conv_b, m_w_down, m_post_ffn_norm, v_pre_mix_norm, v_w_in, v_v_norm_g, v_v_norm_b, v_w_spatial, v_b_spatial, v_out_norm_a, v_out_norm_b, v_w_out, v_post_mix_norm, v_pre_ffn_norm, v_w_up, v_conv_w, v_conv_b, v_w_down, v_post_ffn_norm):
    w = dict(pre_mix_norm=pre_mix_norm, w_in=w_in, v_norm_g=v_norm_g, v_norm_b=v_norm_b, w_spatial=w_spatial,
             b_spatial=b_spatial, out_norm_a=out_norm_a, out_norm_b=out_norm_b, w_out=w_out,
             post_mix_norm=post_mix_norm, pre_ffn_norm=pre_ffn_norm, w_up=w_up, conv_w=conv_w, conv_b=conv_b,
             w_down=w_down, post_ffn_norm=post_ffn_norm)
    m = dict(pre_mix_norm=m_pre_mix_norm, w_in=m_w_in, v_norm_g=m_v_norm_g, v_norm_b=m_v_norm_b,
             w_spatial=m_w_spatial, b_spatial=m_b_spatial, out_norm_a=m_out_norm_a, out_norm_b=m_out_norm_b,
             w_out=m_w_out, post_mix_norm=m_post_mix_norm, pre_ffn_norm=m_pre_ffn_norm, w_up=m_w_up,
             conv_w=m_conv_w, conv_b=m_conv_b, w_down=m_w_down, post_ffn_norm=m_post_ffn_norm)
    v = dict(pre_mix_norm=v_pre_mix_norm, w_in=v_w_in, v_norm_g=v_v_norm_g, v_norm_b=v_v_norm_b,
             w_spatial=v_w_spatial, b_spatial=v_b_spatial, out_norm_a=v_out_norm_a, out_norm_b=v_out_norm_b,
             w_out=v_w_out, post_mix_norm=v_post_mix_norm, pre_ffn_norm=v_pre_ffn_norm, w_up=v_w_up,
             conv_w=v_conv_w, conv_b=v_conv_b, w_down=v_w_down, post_ffn_norm=v_post_ffn_norm)
    pos = jnp.stack([lax.axis_index("x"), lax.axis_index("y"), lax.axis_index("c")]).astype(jnp.int32)
    chip = 2 * lax.axis_index("x") + lax.axis_index("y")

    gathered = _gather_weights([w[n].astype(BF16) for n in BIG])
    wg = dict(zip(BIG, gathered))
    cw_cols = conv_w.shape[-1]
    cw_slab = lax.dynamic_update_slice(jnp.zeros((DEPTH, 3, 2 * D_FF), F32), conv_w, (0, 0, chip * cw_cols))
    conv_w_full = _allreduce_small(cw_slab.reshape(-1, LANES), "gather_conv_w").reshape(DEPTH, 3, 2 * D_FF)
    conv_w_full = conv_w_full * 0.5

    small = {n: w[n] for n in SMALL}
    loss_part, dx, bigs, smalls = _local_step(x[0], loss_target[0], small, conv_w_full, wg)

    small_shapes = [w[n].shape for n in SMALL]
    stacked = [jnp.stack([smalls[l][n].reshape(w[n].shape[1:]) for l in range(DEPTH)]) for n in SMALL]
    cw_grad = jnp.stack([smalls[l]["conv_w"] for l in range(DEPTH)])
    packed = _pack(stacked + [cw_grad, loss_part])
    total = _allreduce_small(packed, "allreduce_small")
    parts = _unpack(total, small_shapes + [cw_grad.shape, (8, LANES)])
    g_small = dict(zip(SMALL, parts[:len(SMALL)]))
    loss = parts[-1][0, 0]
    g_conv_w = lax.dynamic_slice(parts[-2], (0, 0, chip * cw_cols), conv_w.shape)

    shards = [_reduce_scatter_layer(l, bigs[l], pos) for l in range(DEPTH)]
    grads = {n: jnp.stack([shards[l][n] for l in range(DEPTH)]) for n in BIG}
    grads.update(g_small)
    grads["conv_w"] = g_conv_w

    dp, mp, vp = _adamw(_pack([w[n] for n in SMALL]), _pack([g_small[n] for n in SMALL]),
                        _pack([m[n] for n in SMALL]), _pack([v[n] for n in SMALL]), "adamw_small")
    delta = dict(zip(SMALL, _unpack(dp, small_shapes)))
    new_m = dict(zip(SMALL, _unpack(mp, small_shapes)))
    new_v = dict(zip(SMALL, _unpack(vp, small_shapes)))
    for n in BIG + ("conv_w",):
        delta[n], new_m[n], new_v[n] = _adamw_nd(w[n], grads[n], m[n], v[n], "adamw_" + n)

    return (loss, dx[None], *[grads[n] for n in WEIGHTS], *[delta[n] for n in WEIGHTS],
            *[new_m[n] for n in WEIGHTS], *[new_v[n] for n in WEIGHTS])
```

```python
import functools
import math

import jax
import jax.numpy as jnp
import numpy as np
from jax import lax
from jax.experimental import pallas as pl
from jax.experimental.pallas import tpu as pltpu

F32 = jnp.float32
BF16 = jnp.bfloat16
MESH = pl.DeviceIdType.MESH

D_MODEL = 1024
A_WIDTH = 512
A_GROUPS = 4
GROUP_DIM = 128
CHUNK = 128
B_WIDTH = 512
HEAD_DIM = 64
ROT_DIM = 16
ROPE_THETA = 500000.0
DILATIONS = (1, 4, 16)
BAND = 128
IN_COLS = 2560
D_FF = 4096
EPS = 1e-6
NEG_INF = -1e30
N_CHIPS = 4
LANES = 128

ADAM_LR = 0.001
ADAM_B1 = 0.9
ADAM_B2 = 0.999
ADAM_EPS = 1e-08
ADAM_WD = 0.01
ADAM_STEP = 10

VMEM_LIMIT_BYTES = 56 * 1024 * 1024
RSQRT2 = 0.7071067811865476
INV_SQRT_2PI = 0.3989422804014327
GELU_C = 0.7978845608028654
GELU_A = 0.044715

ANY = pl.BlockSpec(memory_space=pl.ANY)
NN = ((1,), (0,))
NT = ((1,), (1,))
TN = ((0,), (0,))


def _cparams(*sem):
    return pltpu.CompilerParams(dimension_semantics=sem, vmem_limit_bytes=VMEM_LIMIT_BYTES)


def _dot(a, b, dims):
    return lax.dot_general(a, b, (dims, ((), ())), preferred_element_type=F32)


def _rsq_mean(a):
    return lax.rsqrt(jnp.mean(a * a, axis=-1, keepdims=True) + EPS)


def _rms_bwd(a, r, g, dz):
    t = dz * g
    da = r * t - a * (r * r * r) * jnp.mean(t * a, axis=-1, keepdims=True)
    return da, dz * a * r


def _colsum(a):
    return jnp.sum(a, axis=0, keepdims=True)


def _gelu_tanh(x):
    t = jnp.tanh(GELU_C * (x + GELU_A * x * x * x))
    return 0.5 * x * (1.0 + t), t


def _gelu_tanh_grad(x, t):
    return 0.5 * (1.0 + t) + 0.5 * x * (1.0 - t * t) * GELU_C * (1.0 + 3.0 * GELU_A * x * x)


def _matmul(a, b, *, grid, a_spec, b_spec, o_spec, o_shape, o_dtype, dims, nk, kaxis, acc_shape, name, b_2d=None):
    def body(a_ref, b_ref, o_ref, *scratch):
        bv = b_ref[...] if b_2d is None else b_ref[...].reshape(b_2d)
        part = _dot(a_ref[...], bv, dims)
        if nk == 1:
            o_ref[...] = part.astype(o_dtype)
        else:
            acc = scratch[0]
            k = pl.program_id(kaxis)

            @pl.when(k == 0)
            def _():
                acc[...] = part

            @pl.when(k > 0)
            def _():
                acc[...] += part

            @pl.when(k == nk - 1)
            def _():
                o_ref[...] = acc[...].astype(o_dtype)

    sem = tuple("arbitrary" if (nk > 1 and ax == kaxis) else "parallel" for ax in range(len(grid)))
    return pl.pallas_call(
        body, grid=grid, in_specs=[a_spec, b_spec], out_specs=o_spec,
        out_shape=jax.ShapeDtypeStruct(o_shape, o_dtype),
        scratch_shapes=[pltpu.VMEM(acc_shape, F32)] if nk > 1 else [],
        compiler_params=_cparams(*sem), name=name)(a, b)


TM = 512
TMM = 1024


TR = 256


def _row_spec(width, col=0):
    return pl.BlockSpec((TR, width), lambda i, col=col: (i, col))


def _vec_spec(width):
    return pl.BlockSpec((1, width), lambda i: (0, 0))


def _rms_cast(x, g, name):
    s, d = x.shape

    def body(x_ref, g_ref, h_ref):
        a = x_ref[...]
        h_ref[...] = (a * _rsq_mean(a) * g_ref[...]).astype(BF16)

    return pl.pallas_call(
        body, grid=(s // TR,), in_specs=[_row_spec(d), _vec_spec(d)], out_specs=_row_spec(d),
        out_shape=jax.ShapeDtypeStruct((s, d), BF16), compiler_params=_cparams("parallel"), name=name)(x, g)


def _residual_norm(x0, y, g_post, g_next, name):
    s, d = x0.shape

    def body(x_ref, y_ref, gp_ref, gn_ref, x1_ref, h_ref):
        yv = y_ref[...]
        x1 = x_ref[...] + yv * _rsq_mean(yv) * gp_ref[...]
        x1_ref[...] = x1
        h_ref[...] = (x1 * _rsq_mean(x1) * gn_ref[...]).astype(BF16)

    return pl.pallas_call(
        body, grid=(s // TR,), in_specs=[_row_spec(d), _row_spec(d), _vec_spec(d), _vec_spec(d)],
        out_specs=[_row_spec(d), _row_spec(d)],
        out_shape=[jax.ShapeDtypeStruct((s, d), F32), jax.ShapeDtypeStruct((s, d), BF16)],
        compiler_params=_cparams("parallel"), name=name)(x0, y, g_post, g_next)


def _residual_loss(x1, f, g_post, target, name):
    s, d = x1.shape

    def body(x_ref, f_ref, gp_ref, t_ref, loss_ref, dx_ref):
        fv = f_ref[...]
        err = x_ref[...] + fv * _rsq_mean(fv) * gp_ref[...] - t_ref[...]
        dx_ref[...] = err * (1.0 / d)
        part = 0.5 * jnp.sum(jnp.mean(err * err, axis=-1, keepdims=True), axis=0, keepdims=True)

        @pl.when(pl.program_id(0) == 0)
        def _():
            loss_ref[...] = jnp.zeros_like(loss_ref)

        loss_ref[...] += jnp.broadcast_to(part, loss_ref.shape)

    return pl.pallas_call(
        body, grid=(s // TR,), in_specs=[_row_spec(d), _row_spec(d), _vec_spec(d), _row_spec(d)],
        out_specs=[pl.BlockSpec((8, LANES), lambda i: (0, 0)), _row_spec(d)],
        out_shape=[jax.ShapeDtypeStruct((8, LANES), F32), jax.ShapeDtypeStruct((s, d), F32)],
        compiler_params=_cparams("arbitrary"), name=name)(x1, f, g_post, target)


def _acc_init(refs):
    @pl.when(pl.program_id(0) == 0)
    def _():
        for r in refs:
            r[...] = jnp.zeros_like(r)


def _norm_bwd_out(dx, f, g_post, name):
    s, d = dx.shape

    def body(dx_ref, f_ref, g_ref, df_ref, dg_ref):
        _acc_init([dg_ref])
        fv = f_ref[...]
        dz = dx_ref[...]
        da, dgt = _rms_bwd(fv, _rsq_mean(fv), g_ref[...], dz)
        df_ref[...] = da.astype(BF16)
        dg_ref[...] += _colsum(dgt)

    return pl.pallas_call(
        body, grid=(s // TR,), in_specs=[_row_spec(d), _row_spec(d), _vec_spec(d)],
        out_specs=[_row_spec(d), _vec_spec(d)],
        out_shape=[jax.ShapeDtypeStruct((s, d), BF16), jax.ShapeDtypeStruct((1, d), F32)],
        compiler_params=_cparams("arbitrary"), name=name)(dx, f, g_post)


def _norm_bwd_mid(dx2, dh2, x1, g_pf, y1, g_pm, name):
    s, d = dx2.shape

    def body(dx2_ref, dh_ref, x1_ref, gpf_ref, y1_ref, gpm_ref, dx1_ref, dy1_ref, dgpf_ref, dgpm_ref):
        _acc_init([dgpf_ref, dgpm_ref])
        x1 = x1_ref[...]
        da, dgt = _rms_bwd(x1, _rsq_mean(x1), gpf_ref[...], dh_ref[...])
        dx1 = dx2_ref[...] + da
        dx1_ref[...] = dx1
        dgpf_ref[...] += _colsum(dgt)
        y1 = y1_ref[...]
        dy, dgt2 = _rms_bwd(y1, _rsq_mean(y1), gpm_ref[...], dx1)
        dy1_ref[...] = dy.astype(BF16)
        dgpm_ref[...] += _colsum(dgt2)

    return pl.pallas_call(
        body, grid=(s // TR,),
        in_specs=[_row_spec(d), _row_spec(d), _row_spec(d), _vec_spec(d), _row_spec(d), _vec_spec(d)],
        out_specs=[_row_spec(d), _row_spec(d), _vec_spec(d), _vec_spec(d)],
        out_shape=[jax.ShapeDtypeStruct((s, d), F32), jax.ShapeDtypeStruct((s, d), BF16),
                   jax.ShapeDtypeStruct((1, d), F32), jax.ShapeDtypeStruct((1, d), F32)],
        compiler_params=_cparams("arbitrary"), name=name)(dx2, dh2, x1, g_pf, y1, g_pm)


def _norm_bwd_in(dx1, dh1, x0, g1, name):
    s, d = dx1.shape

    def body(dx1_ref, dh_ref, x0_ref, g_ref, dx0_ref, dg_ref):
        _acc_init([dg_ref])
        x0 = x0_ref[...]
        da, dgt = _rms_bwd(x0, _rsq_mean(x0), g_ref[...], dh_ref[...])
        dx0_ref[...] = dx1_ref[...] + da
        dg_ref[...] += _colsum(dgt)

    return pl.pallas_call(
        body, grid=(s // TR,), in_specs=[_row_spec(d), _row_spec(d), _row_spec(d), _vec_spec(d)],
        out_specs=[_row_spec(d), _vec_spec(d)],
        out_shape=[jax.ShapeDtypeStruct((s, d), F32), jax.ShapeDtypeStruct((1, d), F32)],
        compiler_params=_cparams("arbitrary"), name=name)(dx1, dh1, x0, g1)


def _tril_mask():
    row = lax.broadcasted_iota(jnp.int32, (CHUNK, CHUNK), 0)
    col = lax.broadcasted_iota(jnp.int32, (CHUNK, CHUNK), 1)
    return row >= col


def _gating_forward(pa, gv, bv, wt, bsf):
    er = lax.erf(pa * RSQRT2)
    za = 0.5 * pa * (1.0 + er)
    u = za[:, :A_WIDTH]
    va = za[:, A_WIDTH:]
    xc = va - jnp.mean(va, axis=-1, keepdims=True)
    rs = lax.rsqrt(jnp.mean(xc * xc, axis=-1, keepdims=True) + EPS)
    vn = xc * rs
    vlb = (vn * gv + bv).astype(BF16)
    sg = jnp.concatenate(
        [_dot(wt[g], vlb[:, g * GROUP_DIM:(g + 1) * GROUP_DIM], NN) for g in range(A_GROUPS)], axis=1) + bsf
    return er, u, rs, vn, vlb, sg


def _masked_ws(ws_ref):
    mask = _tril_mask()
    return [jnp.where(mask, ws_ref[g], 0.0).astype(BF16) for g in range(A_GROUPS)]


def _mixer_a_fwd(proj, gv, bv, ws, bsf, ga, name):
    s = proj.shape[0]

    def body(p_ref, gv_ref, bv_ref, ws_ref, bs_ref, ga_ref, o_ref):
        wt = _masked_ws(ws_ref)
        for ch in range(TR // CHUNK):
            rows = slice(ch * CHUNK, (ch + 1) * CHUNK)
            _, u, _, _, _, sg = _gating_forward(p_ref[rows, :], gv_ref[...], bv_ref[...], wt, bs_ref[...])
            oa = u * sg
            o_ref[rows, :] = (oa * _rsq_mean(oa) * ga_ref[...]).astype(BF16)

    return pl.pallas_call(
        body, grid=(s // TR,),
        in_specs=[_row_spec(2 * A_WIDTH), _vec_spec(A_WIDTH), _vec_spec(A_WIDTH),
                  pl.BlockSpec((A_GROUPS, CHUNK, CHUNK), lambda i: (0, 0, 0)),
                  pl.BlockSpec((CHUNK, A_WIDTH), lambda i: (0, 0)), _vec_spec(A_WIDTH)],
        out_specs=_row_spec(A_WIDTH), out_shape=jax.ShapeDtypeStruct((s, A_WIDTH + B_WIDTH), BF16),
        compiler_params=_cparams("parallel"), name=name)(proj, gv, bv, ws, bsf, ga)


def _mixer_a_bwd(proj, dmixed, gv, bv, ws, bsf, ga, name):
    s = proj.shape[0]
    nsteps = s // TR

    def body(p_ref, dm_ref, gv_ref, bv_ref, ws_ref, bs_ref, ga_ref,
             dp_ref, dga_ref, dgv_ref, dbv_ref, dbs_ref, dws_ref):
        _acc_init([dga_ref, dgv_ref, dbv_ref, dbs_ref, dws_ref])
        mask = _tril_mask()
        wt = _masked_ws(ws_ref)
        gvv = gv_ref[...]
        gav = ga_ref[...]
        for ch in range(TR // CHUNK):
            rows = slice(ch * CHUNK, (ch + 1) * CHUNK)
            pa = p_ref[rows, :]
            er, u, rs, vn, vlb, sg = _gating_forward(pa, gvv, bv_ref[...], wt, bs_ref[...])
            oa = u * sg
            doa, dgt = _rms_bwd(oa, _rsq_mean(oa), gav, dm_ref[rows, :])
            dga_ref[...] += _colsum(dgt)
            du = doa * sg
            dsg = doa * u
            dbs_ref[...] += dsg
            dsgb = dsg.astype(BF16)
            dvl = []
            for g in range(A_GROUPS):
                cols = slice(g * GROUP_DIM, (g + 1) * GROUP_DIM)
                dws_ref[g] += jnp.where(mask, _dot(dsgb[:, cols], vlb[:, cols], NT), 0.0)
                dvl.append(_dot(wt[g], dsgb[:, cols], TN))
            dvl = jnp.concatenate(dvl, axis=1)
            dgv_ref[...] += _colsum(dvl * vn)
            dbv_ref[...] += _colsum(dvl)
            dvn = dvl * gvv
            dva = rs * (dvn - jnp.mean(dvn, axis=-1, keepdims=True)
                        - vn * jnp.mean(dvn * vn, axis=-1, keepdims=True))
            gp = 0.5 * (1.0 + er) + pa * jnp.exp(-0.5 * pa * pa) * INV_SQRT_2PI
            dp_ref[rows, :] = (jnp.concatenate([du, dva], axis=1) * gp).astype(BF16)

        @pl.when(pl.program_id(0) == nsteps - 1)
        def _():
            for g in range(A_GROUPS):
                cols = slice(g * GROUP_DIM, (g + 1) * GROUP_DIM)
                tot = jnp.sum(dbs_ref[:, cols], axis=1, keepdims=True)
                dbs_ref[:, cols] = jnp.broadcast_to(tot, (CHUNK, GROUP_DIM))

    full = lambda *shape: pl.BlockSpec(shape, lambda i: (0,) * len(shape))
    return pl.pallas_call(
        body, grid=(nsteps,),
        in_specs=[_row_spec(2 * A_WIDTH), _row_spec(A_WIDTH), _vec_spec(A_WIDTH), _vec_spec(A_WIDTH),
                  full(A_GROUPS, CHUNK, CHUNK), full(CHUNK, A_WIDTH), _vec_spec(A_WIDTH)],
        out_specs=[_row_spec(2 * A_WIDTH), _vec_spec(A_WIDTH), _vec_spec(A_WIDTH), _vec_spec(A_WIDTH),
                   full(CHUNK, A_WIDTH), full(A_GROUPS, CHUNK, CHUNK)],
        out_shape=[jax.ShapeDtypeStruct((s, IN_COLS), BF16), jax.ShapeDtypeStruct((1, A_WIDTH), F32),
                   jax.ShapeDtypeStruct((1, A_WIDTH), F32), jax.ShapeDtypeStruct((1, A_WIDTH), F32),
                   jax.ShapeDtypeStruct((CHUNK, A_WIDTH), F32),
                   jax.ShapeDtypeStruct((A_GROUPS, CHUNK, CHUNK), F32)],
        compiler_params=_cparams("arbitrary"), name=name)(proj, dmixed, gv, bv, ws, bsf, ga)


def _rope_tables(s):
    half = ROT_DIM // 2
    inv = ROPE_THETA ** (-jnp.arange(0, ROT_DIM, 2, dtype=F32) / ROT_DIM)
    ang = jnp.arange(s, dtype=F32)[:, None] * inv[None, :]
    cos, sin = jnp.cos(ang), jnp.sin(ang)
    zeros = jnp.zeros((s, HEAD_DIM - ROT_DIM), F32)
    zh = jnp.zeros((s, half), F32)
    c = jnp.concatenate([cos, cos, zeros + 1.0], axis=1)
    s1 = jnp.concatenate([-sin, zh, zeros], axis=1)
    s2 = jnp.concatenate([zh, sin, zeros], axis=1)
    return tuple(jnp.concatenate([t, t], axis=1) for t in (c, s1, s2))


def _lane_blocks(width):
    return [slice(b * LANES, (b + 1) * LANES) for b in range(width // LANES)]


CLASS_DILS = tuple(d for d in DILATIONS if d > 1)


def _class_shape(s, dil, dtype):
    return jax.ShapeDtypeStruct((dil, s // dil, B_WIDTH), dtype)


def _class_spec(dil):
    return pl.BlockSpec((dil, TR // dil, B_WIDTH), lambda i, *_: (0, i, 0))


NBLK = B_WIDTH // LANES
STAGE = pltpu.VMEM((NBLK, TR, LANES), F32)


def _stage_put(stage, value):
    for b, sl in enumerate(_lane_blocks(B_WIDTH)):
        stage[b] = value[:, sl]


def _stage_get(stage):
    return jnp.concatenate([stage[b] for b in range(NBLK)], axis=1)


def _store_classes(stage, dst_ref, dil):
    for b, sl in enumerate(_lane_blocks(B_WIDTH)):
        for r in range(dil):
            dst_ref[r, :, sl] = stage[b, pl.ds(r, TR // dil, stride=dil), :].astype(dst_ref.dtype)


def _load_classes(src_ref, stage, dil):
    for b, sl in enumerate(_lane_blocks(B_WIDTH)):
        for r in range(dil):
            stage[b, pl.ds(r, TR // dil, stride=dil), :] = src_ref[r, :, sl].astype(F32)
    return _stage_get(stage)


def _rope_fwd(proj, tabs, name):
    s = proj.shape[0]
    half = ROT_DIM // 2
    scale = HEAD_DIM ** -0.5
    nlay = 1 + len(CLASS_DILS)

    def body(q_ref, k_ref, v_ref, c_ref, s1_ref, s2_ref, *rest):
        outs, stage = rest[:3 * nlay], rest[3 * nlay]
        c, s1, s2 = c_ref[...], s1_ref[...], s2_ref[...]
        for which, (src, mul) in enumerate(((q_ref, scale), (k_ref, 1.0), (v_ref, None))):
            if mul is None:
                _stage_put(stage, src[...])
            else:
                for b, sl in enumerate(_lane_blocks(B_WIDTH)):
                    a = src[:, sl]
                    r = a * c + pltpu.roll(a, LANES - half, 1) * s1 + pltpu.roll(a, half, 1) * s2
                    stage[b] = r * mul
            dst = outs[which * nlay:(which + 1) * nlay]
            dst[0][...] = _stage_get(stage).astype(BF16)
            for ref, d in zip(dst[1:], CLASS_DILS):
                _store_classes(stage, ref, d)

    tab = pl.BlockSpec((TR, LANES), lambda i: (i, 0))
    lay_specs = [_row_spec(B_WIDTH)] + [_class_spec(d) for d in CLASS_DILS]
    lay_shapes = [jax.ShapeDtypeStruct((s, B_WIDTH), BF16)] + [_class_shape(s, d, BF16) for d in CLASS_DILS]
    outs = pl.pallas_call(
        body, grid=(s // TR,),
        in_specs=[_row_spec(B_WIDTH, 2), _row_spec(B_WIDTH, 3), _row_spec(B_WIDTH, 4), tab, tab, tab],
        out_specs=lay_specs * 3, out_shape=lay_shapes * 3, scratch_shapes=[STAGE],
        compiler_params=_cparams("parallel"), name=name)(proj, proj, proj, *tabs)
    q, k, v = (dict(zip(DILATIONS, outs[w * nlay:(w + 1) * nlay])) for w in range(3))
    return q, k, v


def _as_classes(t):
    return t if t.ndim == 3 else t[None]


def _band_mask(i):
    qi = lax.broadcasted_iota(jnp.int32, (BAND, 2 * BAND), 0)
    kj = lax.broadcasted_iota(jnp.int32, (BAND, 2 * BAND), 1)
    return (kj >= qi) & (kj <= qi + BAND) & ((kj >= BAND) | (i > 0))


def _head_masks():
    lane = lax.broadcasted_iota(jnp.int32, (1, LANES), 1)
    return lane < HEAD_DIM, lane >= HEAD_DIM


def _attn_specs(last):
    cur = pl.BlockSpec((None, BAND, B_WIDTH), lambda r, i: (r, jnp.minimum(i, last), 0))
    prev = pl.BlockSpec((None, BAND, B_WIDTH), lambda r, i: (r, jnp.maximum(jnp.minimum(i, last) - 1, 0), 0))
    return cur, prev


def _attn_fwd(q, k, v, name):
    dil, n, _ = q.shape
    nb = n // BAND

    def body(q_ref, kc_ref, kp_ref, vc_ref, vp_ref, o_ref, l_ref):
        valid = _band_mask(pl.program_id(1))
        lo, hi = _head_masks()
        for sl in _lane_blocks(B_WIDTH):
            qb = q_ref[:, sl]
            kk = jnp.concatenate([kp_ref[:, sl], kc_ref[:, sl]], axis=0)
            vv = jnp.concatenate([vp_ref[:, sl], vc_ref[:, sl]], axis=0)
            outs, lses = [], []
            for hm in (lo, hi):
                sc = _dot(jnp.where(hm, qb, jnp.zeros_like(qb)), kk, NT)
                sc = jnp.where(valid, sc, NEG_INF)
                mx = jnp.max(sc, axis=1, keepdims=True)
                p = jnp.exp(sc - mx)
                den = jnp.sum(p, axis=1, keepdims=True)
                outs.append(_dot(p.astype(BF16), vv, NN) / den)
                lses.append(mx + jnp.log(den))
            o_ref[:, sl] = jnp.where(lo, outs[0], outs[1])
            l_ref[:, sl] = jnp.where(lo, lses[0], lses[1])

    cur, prev = _attn_specs(nb - 1)
    return pl.pallas_call(
        body, grid=(dil, nb), in_specs=[cur, cur, prev, cur, prev], out_specs=[cur, cur],
        out_shape=[jax.ShapeDtypeStruct((dil, n, B_WIDTH), F32)] * 2,
        compiler_params=_cparams("parallel", "parallel"), name=name)(q, k, k, v, v)


def _attn_combine(outs, lses, gb, mixed, name):
    s = mixed.shape[0]
    npat = len(DILATIONS)
    w = B_WIDTH

    def body(*refs):
        o_refs, l_refs = refs[:npat], refs[npat:2 * npat]
        g_ref, _, ob_ref = refs[2 * npat:2 * npat + 3]
        lse_refs = refs[2 * npat + 3:3 * npat + 3]
        mb_ref, stage = refs[3 * npat + 3:]
        os_ = [o_refs[0][...]] + [_load_classes(r, stage, d) for r, d in zip(o_refs[1:], CLASS_DILS)]
        ls = [l_refs[0][...]] + [_load_classes(r, stage, d) for r, d in zip(l_refs[1:], CLASS_DILS)]
        mx = functools.reduce(jnp.maximum, ls)
        ws = [jnp.exp(l - mx) for l in ls]
        tot = functools.reduce(lambda a, b: a + b, ws)
        ob = functools.reduce(lambda a, b: a + b, [wt / tot * o for wt, o in zip(ws, os_)])
        ob_ref[...] = ob
        lse = mx + jnp.log(tot)
        _stage_put(stage, lse)
        lse_refs[0][...] = lse
        for ref, d in zip(lse_refs[1:], CLASS_DILS):
            _store_classes(stage, ref, d)
        mb_ref[...] = (ob * _rsq_mean(ob) * g_ref[...]).astype(BF16)

    lay_specs = [_row_spec(w)] + [_class_spec(d) for d in CLASS_DILS]
    res = pl.pallas_call(
        body, grid=(s // TR,), in_specs=lay_specs * 2 + [_vec_spec(w), ANY],
        out_specs=[_row_spec(w)] + lay_specs + [_row_spec(w, 1)],
        out_shape=[jax.ShapeDtypeStruct((s, w), F32), jax.ShapeDtypeStruct((s, w), F32)]
        + [_class_shape(s, d, F32) for d in CLASS_DILS] + [jax.ShapeDtypeStruct(mixed.shape, mixed.dtype)],
        scratch_shapes=[STAGE], input_output_aliases={2 * npat + 1: npat + 1},
        compiler_params=_cparams("parallel"), name=name)(*outs, *lses, gb, mixed)
    return res[0], dict(zip(DILATIONS, res[1:npat + 1])), res[npat + 1]


def _attn_bwd_prep(dmixed, ob, gb, name):
    s = ob.shape[0]
    w = B_WIDTH
    nlay = len(DILATIONS)

    def body(dm_ref, ob_ref, g_ref, *rest):
        do_refs, dl_refs = rest[:nlay], rest[nlay:2 * nlay]
        dg_ref, stage = rest[2 * nlay:]
        _acc_init([dg_ref])
        ob = ob_ref[...]
        dob, dgt = _rms_bwd(ob, _rsq_mean(ob), g_ref[...], dm_ref[...])
        dg_ref[...] += _colsum(dgt)
        _stage_put(stage, dob)
        do_refs[0][...] = dob.astype(BF16)
        for ref, d in zip(do_refs[1:], CLASS_DILS):
            _store_classes(stage, ref, d)
        lo, hi = _head_masks()
        t = dob * ob
        for b, sl in enumerate(_lane_blocks(w)):
            tb = t[:, sl]
            s0 = jnp.sum(jnp.where(lo, tb, 0.0), axis=1, keepdims=True)
            s1 = jnp.sum(jnp.where(hi, tb, 0.0), axis=1, keepdims=True)
            stage[b] = jnp.where(lo, s0, s1)
        dl_refs[0][...] = _stage_get(stage)
        for ref, d in zip(dl_refs[1:], CLASS_DILS):
            _store_classes(stage, ref, d)

    lay_specs = [_row_spec(w)] + [_class_spec(d) for d in CLASS_DILS]
    shapes = lambda dt: [jax.ShapeDtypeStruct((s, w), dt)] + [_class_shape(s, d, dt) for d in CLASS_DILS]
    res = pl.pallas_call(
        body, grid=(s // TR,), in_specs=[_row_spec(w, 1), _row_spec(w), _vec_spec(w)],
        out_specs=lay_specs * 2 + [_vec_spec(w)],
        out_shape=shapes(BF16) + shapes(F32) + [jax.ShapeDtypeStruct((1, w), F32)],
        scratch_shapes=[STAGE],
        compiler_params=_cparams("arbitrary"), name=name)(dmixed, ob, gb)
    return dict(zip(DILATIONS, res[:nlay])), dict(zip(DILATIONS, res[nlay:2 * nlay])), res[2 * nlay]


def _attn_bwd(q, k, v, do, lse, delta, name):
    dil, n, _ = q.shape
    nb = n // BAND

    def body(q_ref, kc_ref, kp_ref, vc_ref, vp_ref, do_ref, lse_ref, dl_ref,
             dq_ref, dk_ref, dv_ref, ck_ref, cv_ref):
        i = pl.program_id(1)

        @pl.when(i == 0)
        def _():
            ck_ref[...] = jnp.zeros_like(ck_ref)
            cv_ref[...] = jnp.zeros_like(cv_ref)

        @pl.when(i < nb)
        def _():
            valid = _band_mask(i)
            lo, hi = _head_masks()
            lane = lax.broadcasted_iota(jnp.int32, (1, LANES), 1)
            for sl in _lane_blocks(B_WIDTH):
                qb = q_ref[:, sl]
                dob = do_ref[:, sl]
                kk = jnp.concatenate([kp_ref[:, sl], kc_ref[:, sl]], axis=0)
                vv = jnp.concatenate([vp_ref[:, sl], vc_ref[:, sl]], axis=0)
                lseb = lse_ref[:, sl]
                dlb = dl_ref[:, sl]
                dq = jnp.zeros((BAND, LANES), F32)
                dkk = jnp.zeros((2 * BAND, LANES), F32)
                dvv = jnp.zeros((2 * BAND, LANES), F32)
                for hm, first in ((lo, 0), (hi, HEAD_DIM)):
                    pick = lane == first
                    lse_h = jnp.sum(jnp.where(pick, lseb, 0.0), axis=1, keepdims=True)
                    dl_h = jnp.sum(jnp.where(pick, dlb, 0.0), axis=1, keepdims=True)
                    qm = jnp.where(hm, qb, jnp.zeros_like(qb))
                    dom = jnp.where(hm, dob, jnp.zeros_like(dob))
                    sc = _dot(qm, kk, NT)
                    p = jnp.where(valid, jnp.exp(sc - lse_h), 0.0)
                    dp = _dot(dom, vv, NT)
                    ds = (p * (dp - dl_h)).astype(BF16)
                    dq += _dot(ds, jnp.where(hm, kk, jnp.zeros_like(kk)), NN)
                    dkk += _dot(ds, qm, TN)
                    dvv += _dot(p.astype(BF16), dom, TN)
                dq_ref[:, sl] = dq
                dk_ref[:, sl] = ck_ref[:, sl] + dkk[:BAND]
                dv_ref[:, sl] = cv_ref[:, sl] + dvv[:BAND]
                ck_ref[:, sl] = dkk[BAND:]
                cv_ref[:, sl] = dvv[BAND:]

        @pl.when(i == nb)
        def _():
            dk_ref[...] = ck_ref[...]
            dv_ref[...] = cv_ref[...]

    cur, prev = _attn_specs(nb - 1)
    lag = pl.BlockSpec((None, BAND, B_WIDTH), lambda r, i: (r, jnp.maximum(i - 1, 0), 0))
    shape = jax.ShapeDtypeStruct((dil, n, B_WIDTH), F32)
    return pl.pallas_call(
        body, grid=(dil, nb + 1), in_specs=[cur, cur, prev, cur, prev, cur, cur, cur],
        out_specs=[cur, lag, lag], out_shape=[shape] * 3,
        scratch_shapes=[pltpu.VMEM((BAND, B_WIDTH), F32)] * 2,
        compiler_params=_cparams("arbitrary", "arbitrary"), name=name)(q, k, k, v, v, do, lse, delta)


def _rope_bwd(dqs, dks, dvs, tabs, dproj, name):
    s = dproj.shape[0]
    half = ROT_DIM // 2
    scale = HEAD_DIM ** -0.5
    npat = len(DILATIONS)
    w = B_WIDTH

    def body(*refs):
        groups = [refs[g * npat:(g + 1) * npat] for g in range(3)]
        c_ref, s1_ref, s2_ref, _, o_ref, stage = refs[3 * npat:]

        def total(rs):
            acc = rs[0][...]
            for ref, d in zip(rs[1:], CLASS_DILS):
                acc = acc + _load_classes(ref, stage, d)
            return acc

        def unrope(g):
            c, s1, s2 = c_ref[...], s1_ref[...], s2_ref[...]
            for sl in _lane_blocks(w):
                gb = g[:, sl]
                o = gb * c + pltpu.roll(gb * s1, half, 1) + pltpu.roll(gb * s2, LANES - half, 1)
                o_ref[:, sl] = o.astype(BF16)

        which = pl.program_id(1)

        @pl.when(which == 0)
        def _():
            unrope(total(groups[0]) * scale)

        @pl.when(which == 1)
        def _():
            unrope(total(groups[1]))

        @pl.when(which == 2)
        def _():
            o_ref[...] = total(groups[2]).astype(BF16)

    tab = pl.BlockSpec((TR, LANES), lambda i, j: (i, 0))
    nat = pl.BlockSpec((TR, w), lambda i, j: (i, 0))
    lay_specs = [nat] + [_class_spec(d) for d in CLASS_DILS]
    first_col = 2 * A_WIDTH // w
    return pl.pallas_call(
        body, grid=(s // TR, 3), in_specs=lay_specs * 3 + [tab] * 3 + [ANY],
        out_specs=pl.BlockSpec((TR, w), lambda i, j: (i, first_col + j)),
        out_shape=jax.ShapeDtypeStruct(dproj.shape, dproj.dtype), scratch_shapes=[STAGE],
        input_output_aliases={3 * npat + 3: 0},
        compiler_params=_cparams("parallel", "arbitrary"), name=name)(*dqs, *dks, *dvs, *tabs, dproj)


TK = 512
HALO = 16


def _row_of(v, r):
    rows = lax.broadcasted_iota(jnp.int32, (v.shape[0], 1), 0)
    return jnp.sum(jnp.where(rows == r, v, 0.0), axis=0, keepdims=True)


def _shift_down(x, halo, n):
    row = lax.broadcasted_iota(jnp.int32, (x.shape[0], 1), 0)
    out = pltpu.roll(x, n, 0)
    for j in range(n):
        out = jnp.where(row == j, _row_of(halo, HALO - n + j), out)
    return out


def _shift_up(x, halo, n):
    rows = x.shape[0]
    row = lax.broadcasted_iota(jnp.int32, (rows, 1), 0)
    out = pltpu.roll(x, rows - n, 0)
    for j in range(n):
        out = jnp.where(row == rows - n + j, _row_of(halo, j), out)
    return out


def _conv_value(x, halo, cw_ref, cb_ref, h):
    taps = (_shift_down(x, halo, 2), _shift_down(x, halo, 1), x)
    conv = cb_ref[h] + cw_ref[h, 0:1, :] * taps[0] + cw_ref[h, 1:2, :] * taps[1] + cw_ref[h, 2:3, :] * taps[2]
    return conv, taps


def _ffn_weight_specs(layer, ncol):
    per_up = (2 * D_FF // N_CHIPS) // TK
    per_dn = (D_FF // N_CHIPS) // TK
    wg = pl.BlockSpec((None, None, D_MODEL, TK), lambda i, j: (j // per_up, layer, 0, j % per_up))
    wv = pl.BlockSpec((None, None, D_MODEL, TK), lambda i, j: ((j + ncol) // per_up, layer, 0, (j + ncol) % per_up))
    wd = pl.BlockSpec((None, None, TK, D_MODEL), lambda i, j: (j // per_dn, layer, j % per_dn, 0))
    cw = pl.BlockSpec((2, 3, TK), lambda i, j: (0, 0, j))
    cb = pl.BlockSpec((2, 1, TK), lambda i, j: (0, 0, j))
    return wg, wv, wd, cw, cb


def _ffn_forward(h2, w_up, w_down, layer, cw3, cb3, name):
    s = h2.shape[0]
    nm, ncol = s // TM, D_FF // TK

    def body(h_ref, wg_ref, wv_ref, wd_ref, cw_ref, cb_ref, y_ref, up_ref, f_ref, carry, acc):
        i, j = pl.program_id(0), pl.program_id(1)

        @pl.when((i == 0) & (j == 0))
        def _():
            carry[...] = jnp.zeros_like(carry)

        h = h_ref[...]
        conv = []
        for hh, w_ref in ((0, wg_ref), (1, wv_ref)):
            up = _dot(h, w_ref[...], NN).astype(BF16)
            up_ref[hh] = up
            x = up.astype(F32)
            conv.append(_conv_value(x, carry[j, hh], cw_ref, cb_ref, hh)[0])
            carry[j, hh] = x[TM - HALO:, :]
        y = (_gelu_tanh(conv[0])[0] * conv[1]).astype(BF16)
        y_ref[...] = y
        part = _dot(y, wd_ref[...], NN)

        @pl.when(j == 0)
        def _():
            acc[...] = part

        @pl.when(j > 0)
        def _():
            acc[...] += part

        @pl.when(j == ncol - 1)
        def _():
            f_ref[...] = acc[...]

    wg, wv, wd, cw, cb = _ffn_weight_specs(layer, ncol)
    return pl.pallas_call(
        body, grid=(nm, ncol),
        in_specs=[pl.BlockSpec((TM, D_MODEL), lambda i, j: (i, 0)), wg, wv, wd, cw, cb],
        out_specs=[pl.BlockSpec((TM, TK), lambda i, j: (i, j)), pl.BlockSpec((2, TM, TK), lambda i, j: (0, i, j)),
                   pl.BlockSpec((TM, D_MODEL), lambda i, j: (i, 0))],
        out_shape=[jax.ShapeDtypeStruct((s, D_FF), BF16), jax.ShapeDtypeStruct((2, s, D_FF), BF16),
                   jax.ShapeDtypeStruct((s, D_MODEL), F32)],
        scratch_shapes=[pltpu.VMEM((ncol, 2, HALO, TK), F32), pltpu.VMEM((TM, D_MODEL), F32)],
        compiler_params=_cparams("arbitrary", "arbitrary"), name=name)(h2, w_up, w_up, w_down, cw3, cb3)


def _ffn_backward(df, w_up, w_down, layer, up3, cw3, cb3, name):
    s = df.shape[0]
    nm, ncol = s // TM, D_FF // TK

    def body(df_ref, wg_ref, wv_ref, wd_ref, cw_ref, cb_ref, up_ref, halo_ref, dup_ref, dh_ref, sums_ref, carry, acc):
        i, j = pl.program_id(0), pl.program_id(1)

        @pl.when((i == 0) & (j == 0))
        def _():
            carry[...] = jnp.zeros_like(carry)
            sums_ref[...] = jnp.zeros_like(sums_ref)

        seq_first = i == nm - 1
        dy = _dot(df_ref[...], wd_ref[...], NT)
        conv, taps = [], []
        for hh in range(2):
            halo = jnp.where(seq_first, 0.0, halo_ref[hh].astype(F32))
            cv, tp = _conv_value(up_ref[hh].astype(F32), halo, cw_ref, cb_ref, hh)
            conv.append(cv)
            taps.append(tp)
        act, t = _gelu_tanh(conv[0])
        dcs = (dy * conv[1] * _gelu_tanh_grad(conv[0], t), dy * act)
        row = lax.broadcasted_iota(jnp.int32, (8, 1), 0)
        part = None
        for hh, w_ref in ((0, wg_ref), (1, wv_ref)):
            dc, tp = dcs[hh], taps[hh]
            upd = jnp.zeros((8, TK), F32)
            for ridx, sm in enumerate((_colsum(dc * tp[0]), _colsum(dc * tp[1]), _colsum(dc * tp[2]), _colsum(dc))):
                upd = jnp.where(row == ridx, sm, upd)
            sums_ref[j, hh] += upd
            nxt = carry[j, hh]
            dup = (cw_ref[hh, 2:3, :] * dc + cw_ref[hh, 1:2, :] * _shift_up(dc, nxt, 1)
                   + cw_ref[hh, 0:1, :] * _shift_up(dc, nxt, 2)).astype(BF16)
            carry[j, hh] = dc[:HALO, :]
            dup_ref[hh] = dup
            d = _dot(dup, w_ref[...], NT)
            part = d if part is None else part + d

        @pl.when(j == 0)
        def _():
            acc[...] = part

        @pl.when(j > 0)
        def _():
            acc[...] += part

        @pl.when(j == ncol - 1)
        def _():
            dh_ref[...] = acc[...]

    wg, wv, wd, cw, cb = _ffn_weight_specs(layer, ncol)
    rev = lambda i: nm - 1 - i
    return pl.pallas_call(
        body, grid=(nm, ncol),
        in_specs=[pl.BlockSpec((TM, D_MODEL), lambda i, j: (rev(i), 0)), wg, wv, wd, cw, cb,
                  pl.BlockSpec((2, TM, TK), lambda i, j: (0, rev(i), j)),
                  pl.BlockSpec((2, HALO, TK), lambda i, j: (0, jnp.maximum(rev(i) * (TM // HALO) - 1, 0), j))],
        out_specs=[pl.BlockSpec((2, TM, TK), lambda i, j: (0, rev(i), j)),
                   pl.BlockSpec((TM, D_MODEL), lambda i, j: (rev(i), 0)),
                   pl.BlockSpec((ncol, 2, 8, TK), lambda i, j: (0, 0, 0, 0))],
        out_shape=[jax.ShapeDtypeStruct((2, s, D_FF), BF16), jax.ShapeDtypeStruct((s, D_MODEL), F32),
                   jax.ShapeDtypeStruct((ncol, 2, 8, TK), F32)],
        scratch_shapes=[pltpu.VMEM((ncol, 2, HALO, TK), F32), pltpu.VMEM((TM, D_MODEL), F32)],
        compiler_params=_cparams("arbitrary", "arbitrary"), name=name)(df, w_up, w_up, w_down, cw3, cb3, up3, up3)


def _wspec(rows, cols, index_map):
    return pl.BlockSpec((None, None, rows, cols), index_map)


def _layer_forward(l, x0, h1, p, wg, tabs):
    s = x0.shape[0]
    nm = s // TMM
    tag = f"_l{l}"
    proj = _matmul(
        h1, wg["w_in"], grid=(nm, N_CHIPS), a_spec=pl.BlockSpec((TMM, D_MODEL), lambda i, j: (i, 0)),
        b_spec=_wspec(D_MODEL, IN_COLS // N_CHIPS, lambda i, j: (j, l, 0, 0)),
        o_spec=pl.BlockSpec((TMM, IN_COLS // N_CHIPS), lambda i, j: (i, j)), o_shape=(s, IN_COLS), o_dtype=F32,
        dims=NN, nk=1, kaxis=None, acc_shape=None, name="proj" + tag)
    ma = _mixer_a_fwd(proj, p["v_norm_g"], p["v_norm_b"], p["w_spatial"], p["bs_full"], p["out_norm_a"],
                      "mixer_a_fwd" + tag)
    q, k, v = _rope_fwd(proj, tabs, "rope_fwd" + tag)
    outs, lses = zip(*[_attn_fwd(_as_classes(q[d]), _as_classes(k[d]), _as_classes(v[d]), f"attn_fwd_d{d}" + tag)
                       for d in DILATIONS])
    outs = [o.reshape(s, B_WIDTH) if d == 1 else o for o, d in zip(outs, DILATIONS)]
    lses = [t.reshape(s, B_WIDTH) if d == 1 else t for t, d in zip(lses, DILATIONS)]
    ob, lse, mixed = _attn_combine(outs, lses, p["out_norm_b"], ma, "attn_combine" + tag)
    w_out_all = pl.BlockSpec((N_CHIPS, None, D_MODEL // N_CHIPS, D_MODEL), lambda i: (0, l, 0, 0))
    y1 = _matmul(
        mixed, wg["w_out"], grid=(nm,), a_spec=pl.BlockSpec((TMM, D_MODEL), lambda i: (i, 0)), b_spec=w_out_all,
        o_spec=pl.BlockSpec((TMM, D_MODEL), lambda i: (i, 0)), o_shape=(s, D_MODEL), o_dtype=F32,
        dims=NN, nk=1, kaxis=None, acc_shape=None, name="mix_out" + tag, b_2d=(D_MODEL, D_MODEL))
    x1, h2 = _residual_norm(x0, y1, p["post_mix_norm"], p["pre_ffn_norm"], "post_mix" + tag)
    y, up3, f = _ffn_forward(h2, wg["w_up"], wg["w_down"], l, p["cw3"], p["cb3"], "ffn_fwd" + tag)
    saved = dict(x0=x0, h1=h1, proj=proj, q=q, k=k, v=v, ob=ob, lse=lse, mixed=mixed, y1=y1, x1=x1, h2=h2,
                 up3=up3, y=y, f=f)
    return saved


def _layer_backward(l, dx2, sv, p, wg, tabs):
    s = dx2.shape[0]
    nm = s // TMM
    tag = f"_l{l}"
    g = {}
    df, g["post_ffn_norm"] = _norm_bwd_out(dx2, sv["f"], p["post_ffn_norm"], "norm_bwd_out" + tag)
    dup3, dh2, conv_sums = _ffn_backward(df, wg["w_up"], wg["w_down"], l, sv["up3"], p["cw3"], p["cb3"],
                                         "ffn_bwd" + tag)
    sums = conv_sums.transpose(1, 2, 0, 3).reshape(2, 8, D_FF)
    g["conv_w"] = jnp.concatenate([sums[0, :3], sums[1, :3]], axis=1)
    g["conv_b"] = jnp.concatenate([sums[0, 3:4], sums[1, 3:4]], axis=1)
    tn = 1024
    gw_up = _matmul(
        sv["h2"], dup3, grid=(2 * D_FF // tn, nm), a_spec=pl.BlockSpec((TMM, D_MODEL), lambda n, m: (m, 0)),
        b_spec=pl.BlockSpec((None, TMM, tn), lambda n, m: (n // (D_FF // tn), m, n % (D_FF // tn))),
        o_spec=pl.BlockSpec((None, D_MODEL, tn), lambda n, m: (n // 2, 0, n % 2)),
        o_shape=(N_CHIPS, D_MODEL, 2 * D_FF // N_CHIPS), o_dtype=BF16,
        dims=TN, nk=nm, kaxis=1, acc_shape=(D_MODEL, tn), name="w_up_grad" + tag)
    gw_down = _matmul(
        sv["y"], df, grid=(D_FF // tn, 2, nm), a_spec=pl.BlockSpec((TMM, tn), lambda k, h, m: (m, k)),
        b_spec=pl.BlockSpec((TMM, D_MODEL // 2), lambda k, h, m: (m, h)),
        o_spec=pl.BlockSpec((None, tn, D_MODEL // 2), lambda k, h, m: (h, k, 0)),
        o_shape=(2, D_FF, D_MODEL // 2), o_dtype=BF16,
        dims=TN, nk=nm, kaxis=2, acc_shape=(tn, D_MODEL // 2), name="w_down_grad" + tag)
    dx1, dy1, g["pre_ffn_norm"], g["post_mix_norm"] = _norm_bwd_mid(
        dx2, dh2, sv["x1"], p["pre_ffn_norm"], sv["y1"], p["post_mix_norm"], "norm_bwd_mid" + tag)
    w_out_all = pl.BlockSpec((N_CHIPS, None, D_MODEL // N_CHIPS, D_MODEL), lambda i: (0, l, 0, 0))
    dmixed = _matmul(
        dy1, wg["w_out"], grid=(nm,), a_spec=pl.BlockSpec((TMM, D_MODEL), lambda i: (i, 0)), b_spec=w_out_all,
        o_spec=pl.BlockSpec((TMM, D_MODEL), lambda i: (i, 0)), o_shape=(s, D_MODEL), o_dtype=F32,
        dims=NT, nk=1, kaxis=None, acc_shape=None, name="mix_out_bwd" + tag, b_2d=(D_MODEL, D_MODEL))
    gw_out = _matmul(
        sv["mixed"], dy1, grid=(2, nm), a_spec=pl.BlockSpec((TMM, D_MODEL), lambda h, m: (m, 0)),
        b_spec=pl.BlockSpec((TMM, D_MODEL // 2), lambda h, m: (m, h)),
        o_spec=pl.BlockSpec((None, D_MODEL, D_MODEL // 2), lambda h, m: (h, 0, 0)),
        o_shape=(2, D_MODEL, D_MODEL // 2), o_dtype=BF16,
        dims=TN, nk=nm, kaxis=1, acc_shape=(D_MODEL, D_MODEL // 2), name="w_out_grad" + tag)
    dpa, g["out_norm_a"], g["v_norm_g"], g["v_norm_b"], dbs, g["w_spatial"] = _mixer_a_bwd(
        sv["proj"], dmixed, p["v_norm_g"], p["v_norm_b"], p["w_spatial"], p["bs_full"], p["out_norm_a"],
        "mixer_a_bwd" + tag)
    g["b_spatial"] = dbs[:, ::GROUP_DIM].T
    dob, delta, g["out_norm_b"] = _attn_bwd_prep(dmixed, sv["ob"], p["out_norm_b"], "attn_bwd_prep" + tag)
    dqs, dks, dvs = zip(*[
        _attn_bwd(*(_as_classes(t[d]) for t in (sv["q"], sv["k"], sv["v"], dob, sv["lse"], delta)),
                  f"attn_bwd_d{d}" + tag) for d in DILATIONS])
    nat = lambda ts: [t.reshape(s, B_WIDTH) if d == 1 else t for t, d in zip(ts, DILATIONS)]
    dproj = _rope_bwd(nat(dqs), nat(dks), nat(dvs), tabs, dpa, "rope_bwd" + tag)
    wcol = IN_COLS // N_CHIPS
    dh1 = _matmul(
        dproj, wg["w_in"], grid=(nm, N_CHIPS), a_spec=pl.BlockSpec((TMM, wcol), lambda i, n: (i, n)),
        b_spec=_wspec(D_MODEL, wcol, lambda i, n: (n, l, 0, 0)),
        o_spec=pl.BlockSpec((TMM, D_MODEL), lambda i, n: (i, 0)), o_shape=(s, D_MODEL), o_dtype=F32,
        dims=NT, nk=N_CHIPS, kaxis=1, acc_shape=(TMM, D_MODEL), name="proj_bwd" + tag)
    gw_in = _matmul(
        sv["h1"], dproj, grid=(N_CHIPS, nm), a_spec=pl.BlockSpec((TMM, D_MODEL), lambda n, m: (m, 0)),
        b_spec=pl.BlockSpec((TMM, wcol), lambda n, m: (m, n)),
        o_spec=pl.BlockSpec((None, D_MODEL, wcol), lambda n, m: (n, 0, 0)),
        o_shape=(N_CHIPS, D_MODEL, wcol), o_dtype=BF16,
        dims=TN, nk=nm, kaxis=1, acc_shape=(D_MODEL, wcol), name="w_in_grad" + tag)
    dx0, g["pre_mix_norm"] = _norm_bwd_in(dx1, dh1, sv["x0"], p["pre_mix_norm"], "norm_bwd_in" + tag)
    big = dict(w_in=gw_in, w_up=gw_up, w_out=gw_out, w_down=gw_down)
    return dx0, big, g


SMALL = ("pre_mix_norm", "v_norm_g", "v_norm_b", "w_spatial", "b_spatial", "out_norm_a", "out_norm_b",
         "post_mix_norm", "pre_ffn_norm", "conv_b", "post_ffn_norm")
BIG = ("w_in", "w_out", "w_up", "w_down")
DEPTH = 2


def _layer_params(l, small, conv_w_full):
    p = {n: small[n][l].reshape(1, -1) for n in SMALL if n not in ("w_spatial", "b_spatial")}
    p["w_spatial"] = small["w_spatial"][l]
    p["bs_full"] = jnp.repeat(small["b_spatial"][l].T, GROUP_DIM, axis=1)
    p["cw3"] = conv_w_full[l].reshape(3, 2, D_FF).transpose(1, 0, 2)
    p["cb3"] = small["conv_b"][l].reshape(2, 1, D_FF)
    return p


def _local_step(x, target, small, conv_w_full, wg):
    s = x.shape[0]
    tabs = _rope_tables(s)
    params = [_layer_params(l, small, conv_w_full) for l in range(DEPTH)]
    saved = []
    xin = x
    h = _rms_cast(xin, params[0]["pre_mix_norm"], "pre_mix_l0")
    for l in range(DEPTH):
        sv = _layer_forward(l, xin, h, params[l], wg, tabs)
        saved.append(sv)
        if l + 1 < DEPTH:
            xin, h = _residual_norm(sv["x1"], sv["f"], params[l]["post_ffn_norm"], params[l + 1]["pre_mix_norm"],
                                    f"post_ffn_l{l}")
    loss, dx = _residual_loss(saved[-1]["x1"], saved[-1]["f"], params[-1]["post_ffn_norm"], target, "loss")
    bigs, smalls = [None] * DEPTH, [None] * DEPTH
    for l in reversed(range(DEPTH)):
        dx, bigs[l], smalls[l] = _layer_backward(l, dx, saved[l], params[l], wg, tabs)
    return loss, dx, bigs, smalls


def _mesh_pos():
    return lax.axis_index("x"), lax.axis_index("y"), lax.axis_index("c")


def _other_chips(x, y):
    return [(1 - x, y), (x, 1 - y), (1 - x, 1 - y)]


def _gather_weights(shards):
    nw = len(shards)
    nrel = N_CHIPS - 1

    def body(*refs):
        ins, outs = refs[:nw], refs[nw:2 * nw]
        send, recv = refs[2 * nw:]
        x, y, c = _mesh_pos()
        mine = 2 * x + y
        sibling = (x, y, 1 - c)
        chips = _other_chips(x, y)

        def copy(src, dst, slot, to):
            return pltpu.make_async_remote_copy(src_ref=src, dst_ref=dst, send_sem=send.at[slot],
                                                recv_sem=recv.at[slot], device_id=to, device_id_type=MESH)

        own = [copy(ins[t], outs[t].at[mine], 2 * nw * nrel + t, sibling) for t in range(nw)]
        for cp in own:
            cp.start()
        sends = []
        for t in range(nw):
            for r, (px, py) in enumerate(chips):
                cp = copy(ins[t].at[c], outs[t].at[mine, c], t * nrel + r, (px, py, c))
                cp.start()
                sends.append(cp)
        for t in range(nw):
            for r, (px, py) in enumerate(chips):
                landed = outs[t].at[2 * px + py, c]
                copy(landed, landed, t * nrel + r, (px, py, c)).wait_recv()
                cp = copy(landed, landed, nw * nrel + t * nrel + r, sibling)
                cp.start()
                sends.append(cp)
        for t in range(nw):
            for r, (px, py) in enumerate(chips):
                passed = outs[t].at[2 * px + py, 1 - c]
                copy(passed, passed, nw * nrel + t * nrel + r, sibling).wait_recv()
        for cp in sends:
            cp.wait_send()
        for cp in own:
            cp.wait()

    nsem = 2 * nw * nrel + nw
    return pl.pallas_call(
        body, in_specs=[ANY] * nw, out_specs=[ANY] * nw,
        out_shape=[jax.ShapeDtypeStruct((N_CHIPS,) + a.shape, a.dtype) for a in shards],
        scratch_shapes=[pltpu.SemaphoreType.DMA((nsem,)), pltpu.SemaphoreType.DMA((nsem,))],
        name="gather_weights")(*shards)


HALF_ROWS = D_MODEL // 2


def _pair_exchange(g, name):
    shapes = [(N_CHIPS, HALF_ROWS, IN_COLS // N_CHIPS), (N_CHIPS, HALF_ROWS, 2 * D_FF // N_CHIPS),
              (D_MODEL, D_MODEL // 2), (D_FF, D_MODEL // 2)]

    def body(gin, gup, gout, gdn, rin, rup, rout, rdn, send, recv):
        x, y, c = _mesh_pos()
        o = 1 - c
        rows = pl.ds(pl.multiple_of(o * HALF_ROWS, HALF_ROWS), HALF_ROWS)
        pairs = [(gin.at[:, rows, :], rin), (gup.at[:, rows, :], rup), (gout.at[o], rout), (gdn.at[o], rdn)]
        cps = [pltpu.make_async_remote_copy(src_ref=src, dst_ref=dst, send_sem=send.at[t], recv_sem=recv.at[t],
                                            device_id=(x, y, o), device_id_type=MESH)
               for t, (src, dst) in enumerate(pairs)]
        for cp in cps:
            cp.start()
        for cp in cps:
            cp.wait()

    return pl.pallas_call(
        body, in_specs=[ANY] * 4, out_specs=[ANY] * 4,
        out_shape=[jax.ShapeDtypeStruct(sh, BF16) for sh in shapes],
        scratch_shapes=[pltpu.SemaphoreType.DMA((4,)), pltpu.SemaphoreType.DMA((4,))],
        name=name)(g["w_in"], g["w_up"], g["w_out"], g["w_down"])


def _pair_sum(g, recv, pos, name_prefix):
    def add(a, b, grid, a_spec, b_spec, shape, name):
        def body(pos_ref, a_ref, b_ref, o_ref):
            o_ref[...] = (a_ref[...].astype(F32) + b_ref[...].astype(F32)).astype(BF16)

        return pl.pallas_call(
            body, grid_spec=pltpu.PrefetchScalarGridSpec(
                num_scalar_prefetch=1, grid=grid, in_specs=[a_spec, b_spec], out_specs=b_spec),
            out_shape=jax.ShapeDtypeStruct(shape, BF16), compiler_params=_cparams(*["parallel"] * len(grid)),
            name=name)(pos, a, b)

    rin, rup, rout, rdn = recv
    wi, wu = IN_COLS // N_CHIPS, 2 * D_FF // N_CHIPS
    s_in = add(g["w_in"], rin, (N_CHIPS,), pl.BlockSpec((None, HALF_ROWS, wi), lambda j, pos: (j, pos[2], 0)),
               pl.BlockSpec((None, HALF_ROWS, wi), lambda j, pos: (j, 0, 0)), rin.shape, name_prefix + "_in")
    s_up = add(g["w_up"], rup, (N_CHIPS,), pl.BlockSpec((None, HALF_ROWS, wu), lambda j, pos: (j, pos[2], 0)),
               pl.BlockSpec((None, HALF_ROWS, wu), lambda j, pos: (j, 0, 0)), rup.shape, name_prefix + "_up")
    hc = D_MODEL // 2
    s_out = add(g["w_out"], rout, (1,), pl.BlockSpec((None, D_MODEL, hc), lambda j, pos: (pos[2], 0, 0)),
                pl.BlockSpec((D_MODEL, hc), lambda j, pos: (0, 0)), rout.shape, name_prefix + "_out")
    s_dn = add(g["w_down"], rdn, (N_CHIPS,), pl.BlockSpec((None, D_FF // N_CHIPS, hc), lambda j, pos: (pos[2], j, 0)),
               pl.BlockSpec((D_FF // N_CHIPS, hc), lambda j, pos: (j, 0)), rdn.shape, name_prefix + "_down")
    return s_in, s_up, s_out, s_dn


OUT_ROWS = D_MODEL // N_CHIPS
DOWN_ROWS = D_FF // N_CHIPS


def _chip_scatter(sums, name):
    nrel = N_CHIPS - 1
    shapes = [(nrel, HALF_ROWS, IN_COLS // N_CHIPS), (nrel, HALF_ROWS, 2 * D_FF // N_CHIPS),
              (nrel, OUT_ROWS, D_MODEL // 2), (nrel, DOWN_ROWS, D_MODEL // 2)]

    def body(sin, sup, sout, sdn, rin, rup, rout, rdn, send, recv):
        x, y, c = _mesh_pos()
        cps = []
        for r, (px, py) in enumerate(_other_chips(x, y)):
            j = 2 * px + py
            pieces = [(sin.at[j], rin), (sup.at[j], rup),
                      (sout.at[pl.ds(pl.multiple_of(j * OUT_ROWS, OUT_ROWS), OUT_ROWS), :], rout),
                      (sdn.at[pl.ds(pl.multiple_of(j * DOWN_ROWS, DOWN_ROWS), DOWN_ROWS), :], rdn)]
            for t, (src, dst) in enumerate(pieces):
                cp = pltpu.make_async_remote_copy(src_ref=src, dst_ref=dst.at[r], send_sem=send.at[t * nrel + r],
                                                  recv_sem=recv.at[t * nrel + r], device_id=(px, py, c),
                                                  device_id_type=MESH)
                cp.start()
                cps.append(cp)
        for cp in cps:
            cp.wait()

    return pl.pallas_call(
        body, in_specs=[ANY] * 4, out_specs=[ANY] * 4,
        out_shape=[jax.ShapeDtypeStruct(sh, BF16) for sh in shapes],
        scratch_shapes=[pltpu.SemaphoreType.DMA((4 * nrel,)), pltpu.SemaphoreType.DMA((4 * nrel,))],
        name=name)(*sums)


def _chip_sum(sums, recv, pos, name_prefix):
    def add(a, b, a_spec, shape, name):
        def body(pos_ref, a_ref, b_ref, o_ref):
            tot = a_ref[...].astype(F32)
            for r in range(N_CHIPS - 1):
                tot = tot + b_ref[r].astype(F32)
            o_ref[...] = tot

        return pl.pallas_call(
            body, grid_spec=pltpu.PrefetchScalarGridSpec(
                num_scalar_prefetch=1, grid=(1,), in_specs=[a_spec, pl.BlockSpec(b.shape, lambda i, pos: (0, 0, 0))],
                out_specs=pl.BlockSpec((None,) + shape, lambda i, pos: (pos[2], 0, 0))),
            out_shape=jax.ShapeDtypeStruct((2,) + shape, F32), compiler_params=_cparams("arbitrary"),
            name=name)(pos, a, b)

    s_in, s_up, s_out, s_dn = sums
    rin, rup, rout, rdn = recv
    wi, wu, hc = IN_COLS // N_CHIPS, 2 * D_FF // N_CHIPS, D_MODEL // 2
    chip = lambda pos: 2 * pos[0] + pos[1]
    t_in = add(s_in, rin, pl.BlockSpec((None, HALF_ROWS, wi), lambda i, pos: (chip(pos), 0, 0)), (HALF_ROWS, wi),
               name_prefix + "_in")
    t_up = add(s_up, rup, pl.BlockSpec((None, HALF_ROWS, wu), lambda i, pos: (chip(pos), 0, 0)), (HALF_ROWS, wu),
               name_prefix + "_up")
    t_out = add(s_out, rout, pl.BlockSpec((OUT_ROWS, hc), lambda i, pos: (chip(pos), 0)), (OUT_ROWS, hc),
                name_prefix + "_out")
    t_dn = add(s_dn, rdn, pl.BlockSpec((DOWN_ROWS, hc), lambda i, pos: (chip(pos), 0)), (DOWN_ROWS, hc),
               name_prefix + "_down")
    return t_in, t_up, t_out, t_dn


def _pair_share(totals, name):
    n = len(totals)

    def body(*refs):
        ins, outs = refs[:n], refs[n:2 * n]
        send, recv = refs[2 * n:]
        x, y, c = _mesh_pos()
        o = 1 - c
        cps = [pltpu.make_async_remote_copy(src_ref=ins[t].at[c], dst_ref=outs[t].at[c], send_sem=send.at[t],
                                            recv_sem=recv.at[t], device_id=(x, y, o), device_id_type=MESH)
               for t in range(n)]
        for cp in cps:
            cp.start()
        for t in range(n):
            pltpu.make_async_remote_copy(src_ref=ins[t].at[o], dst_ref=outs[t].at[o], send_sem=send.at[t],
                                         recv_sem=recv.at[t], device_id=(x, y, o), device_id_type=MESH).wait_recv()
        for cp in cps:
            cp.wait_send()

    return pl.pallas_call(
        body, in_specs=[ANY] * n, out_specs=[ANY] * n,
        out_shape=[jax.ShapeDtypeStruct(t.shape, t.dtype) for t in totals],
        scratch_shapes=[pltpu.SemaphoreType.DMA((n,)), pltpu.SemaphoreType.DMA((n,))],
        input_output_aliases={t: t for t in range(n)}, name=name)(*totals)


def _reduce_scatter_layer(l, g, pos):
    recv = _pair_exchange(g, f"pair_exchange_l{l}")
    sums = _pair_sum(g, recv, pos, f"pair_sum_l{l}")
    recv2 = _chip_scatter(sums, f"chip_scatter_l{l}")
    totals = _chip_sum(sums, recv2, pos, f"chip_sum_l{l}")
    f_in, f_up, f_out, f_dn = _pair_share(totals, f"pair_share_l{l}")
    f_in = f_in.reshape(D_MODEL, IN_COLS // N_CHIPS)
    f_up = f_up.reshape(D_MODEL, 2 * D_FF // N_CHIPS)
    f_out = f_out.transpose(1, 0, 2).reshape(OUT_ROWS, D_MODEL)
    f_dn = f_dn.transpose(1, 0, 2).reshape(DOWN_ROWS, D_MODEL)
    return dict(w_in=f_in, w_up=f_up, w_out=f_out, w_down=f_dn)


N_DEV = 8


def _allreduce_small(packed, name):
    rows = packed.shape[0]

    def body(x_ref, out_ref, gath, send_sems, recv_sems, local_sem):
        x, y, c = _mesh_pos()
        me, sibling = (x, y, c), (x, y, 1 - c)
        chips = _other_chips(x, y)

        def blk(px, py, pc):
            return gath.at[pl.ds(pl.multiple_of((4 * px + 2 * py + pc) * rows, 8), rows), :]

        def copy(k, block, to, src=None):
            return pltpu.make_async_remote_copy(
                src_ref=blk(*block) if src is None else src, dst_ref=blk(*block), send_sem=send_sems.at[k],
                recv_sem=recv_sems.at[k], device_id=to, device_id_type=MESH)

        mine = pltpu.make_async_copy(x_ref, blk(*me), local_sem)
        mine.start()
        first = [copy(0, me, sibling, src=x_ref)]
        first += [copy(1 + j, me, (*chip, c), src=x_ref) for j, chip in enumerate(chips)]
        for cp in first:
            cp.start()
        passed = [copy(4 + j, (*chip, c), sibling) for j, chip in enumerate(chips)]
        for j, chip in enumerate(chips):
            copy(1 + j, (*chip, c), me).wait_recv()
            passed[j].start()
        copy(0, sibling, me).wait_recv()
        for j, chip in enumerate(chips):
            copy(4 + j, (*chip, 1 - c), me).wait_recv()
        for cp in first + passed:
            cp.wait_send()
        mine.wait()
        tot = gath[0:rows, :]
        for d in range(1, N_DEV):
            tot = tot + gath[d * rows:(d + 1) * rows, :]
        out_ref[...] = tot

    vmem = pl.BlockSpec(memory_space=pltpu.VMEM)
    return pl.pallas_call(
        body, in_specs=[vmem], out_specs=vmem, out_shape=jax.ShapeDtypeStruct((rows, LANES), F32),
        scratch_shapes=[pltpu.VMEM((N_DEV * rows, LANES), F32), pltpu.SemaphoreType.DMA((7,)),
                        pltpu.SemaphoreType.DMA((7,)), pltpu.SemaphoreType.DMA],
        compiler_params=pltpu.CompilerParams(vmem_limit_bytes=VMEM_LIMIT_BYTES),
        name=name)(packed)


def _adamw(w, g, m, v, name):
    rows, cols = w.shape
    tr = 256 if rows % 256 == 0 else rows

    def body(w_ref, g_ref, m_ref, v_ref, d_ref, mo_ref, vo_ref):
        gv = g_ref[...]
        mn = ADAM_B1 * m_ref[...] + (1.0 - ADAM_B1) * gv
        vn = ADAM_B2 * v_ref[...] + (1.0 - ADAM_B2) * (gv * gv)
        m_hat = mn / (1.0 - ADAM_B1 ** ADAM_STEP)
        v_hat = vn / (1.0 - ADAM_B2 ** ADAM_STEP)
        d_ref[...] = -ADAM_LR * (m_hat / (jnp.sqrt(v_hat) + ADAM_EPS) + ADAM_WD * w_ref[...])
        mo_ref[...] = mn
        vo_ref[...] = vn

    spec = pl.BlockSpec((tr, cols), lambda i: (i, 0))
    return pl.pallas_call(
        body, grid=(rows // tr,), in_specs=[spec] * 4, out_specs=[spec] * 3,
        out_shape=[jax.ShapeDtypeStruct((rows, cols), F32)] * 3, compiler_params=_cparams("parallel"),
        name=name)(w, g, m, v)


def _adamw_nd(w, g, m, v, name):
    cols = w.shape[-1] if w.shape[-1] % LANES == 0 else LANES
    outs = _adamw(*(t.reshape(-1, cols) for t in (w, g, m, v)), name)
    return tuple(t.reshape(w.shape) for t in outs)


def _pack(arrays):
    return jnp.concatenate([a.reshape(-1, LANES) for a in arrays], axis=0)


def _unpack(packed, shapes):
    out, row = [], 0
    for sh in shapes:
        n = math.prod(sh) // LANES
        out.append(packed[row:row + n].reshape(sh))
        row += n
    return out


WEIGHTS = ("pre_mix_norm", "w_in", "v_norm_g", "v_norm_b", "w_spatial", "b_spatial", "out_norm_a", "out_norm_b",
           "w_out", "post_mix_norm", "pre_ffn_norm", "w_up", "conv_w", "conv_b", "w_down", "post_ffn_norm")


def kernel(x, pre_mix_norm, w_in, v_norm_g, v_norm_b, w_spatial, b_spatial, out_norm_a, out_norm_b, w_out, post_mix_norm, pre_ffn_norm, w_up, conv_w, conv_b, w_down, post_ffn_norm, loss_target, m_pre_mix_norm, m_w_in, m_v_norm_g, m_v_norm_b, m_w_spatial, m_b_spatial, m_out_norm_a, m_out_norm_b, m_w_out, m_post_mix_norm, m_pre_ffn_norm, m_w_up, m_conv_w, m_conv_b, m_w_down, m_post_ffn_norm, v_pre_mix_norm, v_w_in, v_v_norm_g, v_v_norm_b, v_w_spatial, v_b_spatial, v_out_norm_a, v_out_norm_b, v_w_out, v_post_mix_norm, v_pre_ffn_norm, v_w_up, v_conv_w, v_conv_b, v_w_down, v_post_ffn_norm):
    w = dict(pre_mix_norm=pre_mix_norm, w_in=w_in, v_norm_g=v_norm_g, v_norm_b=v_norm_b, w_spatial=w_spatial,
             b_spatial=b_spatial, out_norm_a=out_norm_a, out_norm_b=out_norm_b, w_out=w_out,
             post_mix_norm=post_mix_norm, pre_ffn_norm=pre_ffn_norm, w_up=w_up, conv_w=conv_w, conv_b=conv_b,
             w_down=w_down, post_ffn_norm=post_ffn_norm)
    m = dict(pre_mix_norm=m_pre_mix_norm, w_in=m_w_in, v_norm_g=m_v_norm_g, v_norm_b=m_v_norm_b,
             w_spatial=m_w_spatial, b_spatial=m_b_spatial, out_norm_a=m_out_norm_a, out_norm_b=m_out_norm_b,
             w_out=m_w_out, post_mix_norm=m_post_mix_norm, pre_ffn_norm=m_pre_ffn_norm, w_up=m_w_up,
             conv_w=m_conv_w, conv_b=m_conv_b, w_down=m_w_down, post_ffn_norm=m_post_ffn_norm)
    v = dict(pre_mix_norm=v_pre_mix_norm, w_in=v_w_in, v_norm_g=v_v_norm_g, v_norm_b=v_v_norm_b,
             w_spatial=v_w_spatial, b_spatial=v_b_spatial, out_norm_a=v_out_norm_a, out_norm_b=v_out_norm_b,
             w_out=v_w_out, post_mix_norm=v_post_mix_norm, pre_ffn_norm=v_pre_ffn_norm, w_up=v_w_up,
             conv_w=v_conv_w, conv_b=v_conv_b, w_down=v_w_down, post_ffn_norm=v_post_ffn_norm)
    pos = jnp.stack([lax.axis_index("x"), lax.axis_index("y"), lax.axis_index("c")]).astype(jnp.int32)
    chip = 2 * lax.axis_index("x") + lax.axis_index("y")

    gathered = _gather_weights([w[n].astype(BF16) for n in BIG])
    wg = dict(zip(BIG, gathered))
    cw_cols = conv_w.shape[-1]
    cw_slab = lax.dynamic_update_slice(jnp.zeros((DEPTH, 3, 2 * D_FF), F32), conv_w, (0, 0, chip * cw_cols))
    conv_w_full = _allreduce_small(cw_slab.reshape(-1, LANES), "gather_conv_w").reshape(DEPTH, 3, 2 * D_FF)
    conv_w_full = conv_w_full * 0.5

    small = {n: w[n] for n in SMALL}
    loss_part, dx, bigs, smalls = _local_step(x[0], loss_target[0], small, conv_w_full, wg)

    small_shapes = [w[n].shape for n in SMALL]
    stacked = [jnp.stack([smalls[l][n].reshape(w[n].shape[1:]) for l in range(DEPTH)]) for n in SMALL]
    cw_grad = jnp.stack([smalls[l]["conv_w"] for l in range(DEPTH)])
    packed = _pack(stacked + [cw_grad, loss_part])
    total = _allreduce_small(packed, "allreduce_small")
    parts = _unpack(total, small_shapes + [cw_grad.shape, (8, LANES)])
    g_small = dict(zip(SMALL, parts[:len(SMALL)]))
    loss = parts[-1][0, 0]
    g_conv_w = lax.dynamic_slice(parts[-2], (0, 0, chip * cw_cols), conv_w.shape)

    shards = [_reduce_scatter_layer(l, bigs[l], pos) for l in range(DEPTH)]
    grads = {n: jnp.stack([shards[l][n] for l in range(DEPTH)]) for n in BIG}
    grads.update(g_small)
    grads["conv_w"] = g_conv_w

    dp, mp, vp = _adamw(_pack([w[n] for n in SMALL]), _pack([g_small[n] for n in SMALL]),
                        _pack([m[n] for n in SMALL]), _pack([v[n] for n in SMALL]), "adamw_small")
    delta = dict(zip(SMALL, _unpack(dp, small_shapes)))
    new_m = dict(zip(SMALL, _unpack(mp, small_shapes)))
    new_v = dict(zip(SMALL, _unpack(vp, small_shapes)))
    for n in BIG + ("conv_w",):
        delta[n], new_m[n], new_v[n] = _adamw_nd(w[n], grads[n], m[n], v[n], "adamw_" + n)

    return (loss, dx[None], *[grads[n] for n in WEIGHTS], *[delta[n] for n in WEIGHTS],
            *[new_m[n] for n in WEIGHTS], *[new_v[n] for n in WEIGHTS])
```

```python
import functools
import math

import jax
import jax.numpy as jnp
import numpy as np
from jax import lax
from jax.experimental import pallas as pl
from jax.experimental.pallas import tpu as pltpu

F32 = jnp.float32
BF16 = jnp.bfloat16
MESH = pl.DeviceIdType.MESH

D_MODEL = 1024
A_WIDTH = 512
A_GROUPS = 4
GROUP_DIM = 128
CHUNK = 128
B_WIDTH = 512
HEAD_DIM = 64
ROT_DIM = 16
ROPE_THETA = 500000.0
DILATIONS = (1, 4, 16)
BAND = 128
IN_COLS = 2560
D_FF = 4096
EPS = 1e-6
NEG_INF = -1e30
N_CHIPS = 4
LANES = 128

ADAM_LR = 0.001
ADAM_B1 = 0.9
ADAM_B2 = 0.999
ADAM_EPS = 1e-08
ADAM_WD = 0.01
ADAM_STEP = 10

VMEM_LIMIT_BYTES = 56 * 1024 * 1024
RSQRT2 = 0.7071067811865476
INV_SQRT_2PI = 0.3989422804014327
GELU_C = 0.7978845608028654
GELU_A = 0.044715

ANY = pl.BlockSpec(memory_space=pl.ANY)
NN = ((1,), (0,))
NT = ((1,), (1,))
TN = ((0,), (0,))


def _cparams(*sem):
    return pltpu.CompilerParams(dimension_semantics=sem, vmem_limit_bytes=VMEM_LIMIT_BYTES)


def _dot(a, b, dims):
    return lax.dot_general(a, b, (dims, ((), ())), preferred_element_type=F32)


def _rsq_mean(a):
    return lax.rsqrt(jnp.mean(a * a, axis=-1, keepdims=True) + EPS)


def _rms_bwd(a, r, g, dz):
    t = dz * g
    da = r * t - a * (r * r * r) * jnp.mean(t * a, axis=-1, keepdims=True)
    return da, dz * a * r


def _colsum(a):
    return jnp.sum(a, axis=0, keepdims=True)


def _gelu_tanh(x):
    t = jnp.tanh(GELU_C * (x + GELU_A * x * x * x))
    return 0.5 * x * (1.0 + t), t


def _gelu_tanh_grad(x, t):
    return 0.5 * (1.0 + t) + 0.5 * x * (1.0 - t * t) * GELU_C * (1.0 + 3.0 * GELU_A * x * x)


def _matmul(a, b, *, grid, a_spec, b_spec, o_spec, o_shape, o_dtype, dims, nk, kaxis, acc_shape, name, b_2d=None):
    def body(a_ref, b_ref, o_ref, *scratch):
        bv = b_ref[...] if b_2d is None else b_ref[...].reshape(b_2d)
        part = _dot(a_ref[...], bv, dims)
        if nk == 1:
            o_ref[...] = part.astype(o_dtype)
        else:
            acc = scratch[0]
            k = pl.program_id(kaxis)

            @pl.when(k == 0)
            def _():
                acc[...] = part

            @pl.when(k > 0)
            def _():
                acc[...] += part

            @pl.when(k == nk - 1)
            def _():
                o_ref[...] = acc[...].astype(o_dtype)

    sem = tuple("arbitrary" if (nk > 1 and ax == kaxis) else "parallel" for ax in range(len(grid)))
    return pl.pallas_call(
        body, grid=grid, in_specs=[a_spec, b_spec], out_specs=o_spec,
        out_shape=jax.ShapeDtypeStruct(o_shape, o_dtype),
        scratch_shapes=[pltpu.VMEM(acc_shape, F32)] if nk > 1 else [],
        compiler_params=_cparams(*sem), name=name)(a, b)


TM = 512
TMM = 1024


TR = 256


def _row_spec(width, col=0):
    return pl.BlockSpec((TR, width), lambda i, col=col: (i, col))


def _vec_spec(width):
    return pl.BlockSpec((1, width), lambda i: (0, 0))


def _rms_cast(x, g, name):
    s, d = x.shape

    def body(x_ref, g_ref, h_ref):
        a = x_ref[...]
        h_ref[...] = (a * _rsq_mean(a) * g_ref[...]).astype(BF16)

    return pl.pallas_call(
        body, grid=(s // TR,), in_specs=[_row_spec(d), _vec_spec(d)], out_specs=_row_spec(d),
        out_shape=jax.ShapeDtypeStruct((s, d), BF16), compiler_params=_cparams("parallel"), name=name)(x, g)


def _residual_norm(x0, y, g_post, g_next, name):
    s, d = x0.shape

    def body(x_ref, y_ref, gp_ref, gn_ref, x1_ref, h_ref):
        yv = y_ref[...]
        x1 = x_ref[...] + yv * _rsq_mean(yv) * gp_ref[...]
        x1_ref[...] = x1
        h_ref[...] = (x1 * _rsq_mean(x1) * gn_ref[...]).astype(BF16)

    return pl.pallas_call(
        body, grid=(s // TR,), in_specs=[_row_spec(d), _row_spec(d), _vec_spec(d), _vec_spec(d)],
        out_specs=[_row_spec(d), _row_spec(d)],
        out_shape=[jax.ShapeDtypeStruct((s, d), F32), jax.ShapeDtypeStruct((s, d), BF16)],
        compiler_params=_cparams("parallel"), name=name)(x0, y, g_post, g_next)


def _residual_loss(x1, f, g_post, target, name):
    s, d = x1.shape

    def body(x_ref, f_ref, gp_ref, t_ref, loss_ref, dx_ref):
        fv = f_ref[...]
        err = x_ref[...] + fv * _rsq_mean(fv) * gp_ref[...] - t_ref[...]
        dx_ref[...] = err * (1.0 / d)
        part = 0.5 * jnp.sum(jnp.mean(err * err, axis=-1, keepdims=True), axis=0, keepdims=True)

        @pl.when(pl.program_id(0) == 0)
        def _():
            loss_ref[...] = jnp.zeros_like(loss_ref)

        loss_ref[...] += jnp.broadcast_to(part, loss_ref.shape)

    return pl.pallas_call(
        body, grid=(s // TR,), in_specs=[_row_spec(d), _row_spec(d), _vec_spec(d), _row_spec(d)],
        out_specs=[pl.BlockSpec((8, LANES), lambda i: (0, 0)), _row_spec(d)],
        out_shape=[jax.ShapeDtypeStruct((8, LANES), F32), jax.ShapeDtypeStruct((s, d), F32)],
        compiler_params=_cparams("arbitrary"), name=name)(x1, f, g_post, target)


def _acc_init(refs):
    @pl.when(pl.program_id(0) == 0)
    def _():
        for r in refs:
            r[...] = jnp.zeros_like(r)


def _norm_bwd_out(dx, f, g_post, name):
    s, d = dx.shape

    def body(dx_ref, f_ref, g_ref, df_ref, dg_ref):
        _acc_init([dg_ref])
        fv = f_ref[...]
        dz = dx_ref[...]
        da, dgt = _rms_bwd(fv, _rsq_mean(fv), g_ref[...], dz)
        df_ref[...] = da.astype(BF16)
        dg_ref[...] += _colsum(dgt)

    return pl.pallas_call(
        body, grid=(s // TR,), in_specs=[_row_spec(d), _row_spec(d), _vec_spec(d)],
        out_specs=[_row_spec(d), _vec_spec(d)],
        out_shape=[jax.ShapeDtypeStruct((s, d), BF16), jax.ShapeDtypeStruct((1, d), F32)],
        compiler_params=_cparams("arbitrary"), name=name)(dx, f, g_post)


def _norm_bwd_mid(dx2, dh2, x1, g_pf, y1, g_pm, name):
    s, d = dx2.shape

    def body(dx2_ref, dh_ref, x1_ref, gpf_ref, y1_ref, gpm_ref, dx1_ref, dy1_ref, dgpf_ref, dgpm_ref):
        _acc_init([dgpf_ref, dgpm_ref])
        x1 = x1_ref[...]
        da, dgt = _rms_bwd(x1, _rsq_mean(x1), gpf_ref[...], dh_ref[...])
        dx1 = dx2_ref[...] + da
        dx1_ref[...] = dx1
        dgpf_ref[...] += _colsum(dgt)
        y1 = y1_ref[...]
        dy, dgt2 = _rms_bwd(y1, _rsq_mean(y1), gpm_ref[...], dx1)
        dy1_ref[...] = dy.astype(BF16)
        dgpm_ref[...] += _colsum(dgt2)

    return pl.pallas_call(
        body, grid=(s // TR,),
        in_specs=[_row_spec(d), _row_spec(d), _row_spec(d), _vec_spec(d), _row_spec(d), _vec_spec(d)],
        out_specs=[_row_spec(d), _row_spec(d), _vec_spec(d), _vec_spec(d)],
        out_shape=[jax.ShapeDtypeStruct((s, d), F32), jax.ShapeDtypeStruct((s, d), BF16),
                   jax.ShapeDtypeStruct((1, d), F32), jax.ShapeDtypeStruct((1, d), F32)],
        compiler_params=_cparams("arbitrary"), name=name)(dx2, dh2, x1, g_pf, y1, g_pm)


def _norm_bwd_in(dx1, dh1, x0, g1, name):
    s, d = dx1.shape

    def body(dx1_ref, dh_ref, x0_ref, g_ref, dx0_ref, dg_ref):
        _acc_init([dg_ref])
        x0 = x0_ref[...]
        da, dgt = _rms_bwd(x0, _rsq_mean(x0), g_ref[...], dh_ref[...])
        dx0_ref[...] = dx1_ref[...] + da
        dg_ref[...] += _colsum(dgt)

    return pl.pallas_call(
        body, grid=(s // TR,), in_specs=[_row_spec(d), _row_spec(d), _row_spec(d), _vec_spec(d)],
        out_specs=[_row_spec(d), _vec_spec(d)],
        out_shape=[jax.ShapeDtypeStruct((s, d), F32), jax.ShapeDtypeStruct((1, d), F32)],
        compiler_params=_cparams("arbitrary"), name=name)(dx1, dh1, x0, g1)


def _tril_mask():
    row = lax.broadcasted_iota(jnp.int32, (CHUNK, CHUNK), 0)
    col = lax.broadcasted_iota(jnp.int32, (CHUNK, CHUNK), 1)
    return row >= col


def _gating_forward(pa, gv, bv, wt, bsf):
    er = lax.erf(pa * RSQRT2)
    za = 0.5 * pa * (1.0 + er)
    u = za[:, :A_WIDTH]
    va = za[:, A_WIDTH:]
    xc = va - jnp.mean(va, axis=-1, keepdims=True)
    rs = lax.rsqrt(jnp.mean(xc * xc, axis=-1, keepdims=True) + EPS)
    vn = xc * rs
    vlb = (vn * gv + bv).astype(BF16)
    sg = jnp.concatenate(
        [_dot(wt[g], vlb[:, g * GROUP_DIM:(g + 1) * GROUP_DIM], NN) for g in range(A_GROUPS)], axis=1) + bsf
    return er, u, rs, vn, vlb, sg


def _masked_ws(ws_ref):
    mask = _tril_mask()
    return [jnp.where(mask, ws_ref[g], 0.0).astype(BF16) for g in range(A_GROUPS)]


def _mixer_a_fwd(proj, gv, bv, ws, bsf, ga, name):
    s = proj.shape[0]

    def body(p_ref, gv_ref, bv_ref, ws_ref, bs_ref, ga_ref, o_ref):
        wt = _masked_ws(ws_ref)
        for ch in range(TR // CHUNK):
            rows = slice(ch * CHUNK, (ch + 1) * CHUNK)
            _, u, _, _, _, sg = _gating_forward(p_ref[rows, :], gv_ref[...], bv_ref[...], wt, bs_ref[...])
            oa = u * sg
            o_ref[rows, :] = (oa * _rsq_mean(oa) * ga_ref[...]).astype(BF16)

    return pl.pallas_call(
        body, grid=(s // TR,),
        in_specs=[_row_spec(2 * A_WIDTH), _vec_spec(A_WIDTH), _vec_spec(A_WIDTH),
                  pl.BlockSpec((A_GROUPS, CHUNK, CHUNK), lambda i: (0, 0, 0)),
                  pl.BlockSpec((CHUNK, A_WIDTH), lambda i: (0, 0)), _vec_spec(A_WIDTH)],
        out_specs=_row_spec(A_WIDTH), out_shape=jax.ShapeDtypeStruct((s, A_WIDTH + B_WIDTH), BF16),
        compiler_params=_cparams("parallel"), name=name)(proj, gv, bv, ws, bsf, ga)


def _mixer_a_bwd(proj, dmixed, gv, bv, ws, bsf, ga, name):
    s = proj.shape[0]
    nsteps = s // TR

    def body(p_ref, dm_ref, gv_ref, bv_ref, ws_ref, bs_ref, ga_ref,
             dp_ref, dga_ref, dgv_ref, dbv_ref, dbs_ref, dws_ref):
        _acc_init([dga_ref, dgv_ref, dbv_ref, dbs_ref, dws_ref])
        mask = _tril_mask()
        wt = _masked_ws(ws_ref)
        gvv = gv_ref[...]
        gav = ga_ref[...]
        for ch in range(TR // CHUNK):
            rows = slice(ch * CHUNK, (ch + 1) * CHUNK)
            pa = p_ref[rows, :]
            er, u, rs, vn, vlb, sg = _gating_forward(pa, gvv, bv_ref[...], wt, bs_ref[...])
            oa = u * sg
            doa, dgt = _rms_bwd(oa, _rsq_mean(oa), gav, dm_ref[rows, :])
            dga_ref[...] += _colsum(dgt)
            du = doa * sg
            dsg = doa * u
            dbs_ref[...] += dsg
            dsgb = dsg.astype(BF16)
            dvl = []
            for g in range(A_GROUPS):
                cols = slice(g * GROUP_DIM, (g + 1) * GROUP_DIM)
                dws_ref[g] += jnp.where(mask, _dot(dsgb[:, cols], vlb[:, cols], NT), 0.0)
                dvl.append(_dot(wt[g], dsgb[:, cols], TN))
            dvl = jnp.concatenate(dvl, axis=1)
            dgv_ref[...] += _colsum(dvl * vn)
            dbv_ref[...] += _colsum(dvl)
            dvn = dvl * gvv
            dva = rs * (dvn - jnp.mean(dvn, axis=-1, keepdims=True)
                        - vn * jnp.mean(dvn * vn, axis=-1, keepdims=True))
            gp = 0.5 * (1.0 + er) + pa * jnp.exp(-0.5 * pa * pa) * INV_SQRT_2PI
            dp_ref[rows, :] = (jnp.concatenate([du, dva], axis=1) * gp).astype(BF16)

        @pl.when(pl.program_id(0) == nsteps - 1)
        def _():
            for g in range(A_GROUPS):
                cols = slice(g * GROUP_DIM, (g + 1) * GROUP_DIM)
                tot = jnp.sum(dbs_ref[:, cols], axis=1, keepdims=True)
                dbs_ref[:, cols] = jnp.broadcast_to(tot, (CHUNK, GROUP_DIM))

    full = lambda *shape: pl.BlockSpec(shape, lambda i: (0,) * len(shape))
    return pl.pallas_call(
        body, grid=(nsteps,),
        in_specs=[_row_spec(2 * A_WIDTH), _row_spec(A_WIDTH), _vec_spec(A_WIDTH), _vec_spec(A_WIDTH),
                  full(A_GROUPS, CHUNK, CHUNK), full(CHUNK, A_WIDTH), _vec_spec(A_WIDTH)],
        out_specs=[_row_spec(2 * A_WIDTH), _vec_spec(A_WIDTH), _vec_spec(A_WIDTH), _vec_spec(A_WIDTH),
                   full(CHUNK, A_WIDTH), full(A_GROUPS, CHUNK, CHUNK)],
        out_shape=[jax.ShapeDtypeStruct((s, IN_COLS), BF16), jax.ShapeDtypeStruct((1, A_WIDTH), F32),
                   jax.ShapeDtypeStruct((1, A_WIDTH), F32), jax.ShapeDtypeStruct((1, A_WIDTH), F32),
                   jax.ShapeDtypeStruct((CHUNK, A_WIDTH), F32),
                   jax.ShapeDtypeStruct((A_GROUPS, CHUNK, CHUNK), F32)],
        compiler_params=_cparams("arbitrary"), name=name)(proj, dmixed, gv, bv, ws, bsf, ga)


def _rope_tables(s):
    half = ROT_DIM // 2
    inv = ROPE_THETA ** (-jnp.arange(0, ROT_DIM, 2, dtype=F32) / ROT_DIM)
    ang = jnp.arange(s, dtype=F32)[:, None] * inv[None, :]
    cos, sin = jnp.cos(ang), jnp.sin(ang)
    zeros = jnp.zeros((s, HEAD_DIM - ROT_DIM), F32)
    zh = jnp.zeros((s, half), F32)
    c = jnp.concatenate([cos, cos, zeros + 1.0], axis=1)
    s1 = jnp.concatenate([-sin, zh, zeros], axis=1)
    s2 = jnp.concatenate([zh, sin, zeros], axis=1)
    return tuple(jnp.concatenate([t, t], axis=1) for t in (c, s1, s2))


def _lane_blocks(width):
    return [slice(b * LANES, (b + 1) * LANES) for b in range(width // LANES)]


CLASS_DILS = tuple(d for d in DILATIONS if d > 1)


def _class_shape(s, dil, dtype):
    return jax.ShapeDtypeStruct((dil, s // dil, B_WIDTH), dtype)


def _class_spec(dil):
    return pl.BlockSpec((dil, TR // dil, B_WIDTH), lambda i, *_: (0, i, 0))


NBLK = B_WIDTH // LANES
STAGE = pltpu.VMEM((NBLK, TR, LANES), F32)


def _stage_put(stage, value):
    for b, sl in enumerate(_lane_blocks(B_WIDTH)):
        stage[b] = value[:, sl]


def _stage_get(stage):
    return jnp.concatenate([stage[b] for b in range(NBLK)], axis=1)


def _store_classes(stage, dst_ref, dil):
    for b, sl in enumerate(_lane_blocks(B_WIDTH)):
        for r in range(dil):
            dst_ref[r, :, sl] = stage[b, pl.ds(r, TR // dil, stride=dil), :].astype(dst_ref.dtype)


def _load_classes(src_ref, stage, dil):
    for b, sl in enumerate(_lane_blocks(B_WIDTH)):
        for r in range(dil):
            stage[b, pl.ds(r, TR // dil, stride=dil), :] = src_ref[r, :, sl].astype(F32)
    return _stage_get(stage)


def _rope_fwd(proj, tabs, name):
    s = proj.shape[0]
    half = ROT_DIM // 2
    scale = HEAD_DIM ** -0.5
    nlay = 1 + len(CLASS_DILS)

    def body(q_ref, k_ref, v_ref, c_ref, s1_ref, s2_ref, *rest):
        outs, stage = rest[:3 * nlay], rest[3 * nlay]
        c, s1, s2 = c_ref[...], s1_ref[...], s2_ref[...]
        for which, (src, mul) in enumerate(((q_ref, scale), (k_ref, 1.0), (v_ref, None))):
            if mul is None:
                _stage_put(stage, src[...])
            else:
                for b, sl in enumerate(_lane_blocks(B_WIDTH)):
                    a = src[:, sl]
                    r = a * c + pltpu.roll(a, LANES - half, 1) * s1 + pltpu.roll(a, half, 1) * s2
                    stage[b] = r * mul
            dst = outs[which * nlay:(which + 1) * nlay]
            dst[0][...] = _stage_get(stage).astype(BF16)
            for ref, d in zip(dst[1:], CLASS_DILS):
                _store_classes(stage, ref, d)

    tab = pl.BlockSpec((TR, LANES), lambda i: (i, 0))
    lay_specs = [_row_spec(B_WIDTH)] + [_class_spec(d) for d in CLASS_DILS]
    lay_shapes = [jax.ShapeDtypeStruct((s, B_WIDTH), BF16)] + [_class_shape(s, d, BF16) for d in CLASS_DILS]
    outs = pl.pallas_call(
        body, grid=(s // TR,),
        in_specs=[_row_spec(B_WIDTH, 2), _row_spec(B_WIDTH, 3), _row_spec(B_WIDTH, 4), tab, tab, tab],
        out_specs=lay_specs * 3, out_shape=lay_shapes * 3, scratch_shapes=[STAGE],
        compiler_params=_cparams("parallel"), name=name)(proj, proj, proj, *tabs)
    q, k, v = (dict(zip(DILATIONS, outs[w * nlay:(w + 1) * nlay])) for w in range(3))
    return q, k, v


def _as_classes(t):
    return t if t.ndim == 3 else t[None]


def _band_mask(i):
    qi = lax.broadcasted_iota(jnp.int32, (BAND, 2 * BAND), 0)
    kj = lax.broadcasted_iota(jnp.int32, (BAND, 2 * BAND), 1)
    return (kj >= qi) & (kj <= qi + BAND) & ((kj >= BAND) | (i > 0))


def _head_masks():
    lane = lax.broadcasted_iota(jnp.int32, (1, LANES), 1)
    return lane < HEAD_DIM, lane >= HEAD_DIM


def _attn_specs(last):
    cur = pl.BlockSpec((None, BAND, B_WIDTH), lambda r, i: (r, jnp.minimum(i, last), 0))
    prev = pl.BlockSpec((None, BAND, B_WIDTH), lambda r, i: (r, jnp.maximum(jnp.minimum(i, last) - 1, 0), 0))
    return cur, prev


def _attn_fwd(q, k, v, name):
    dil, n, _ = q.shape
    nb = n // BAND

    def body(q_ref, kc_ref, kp_ref, vc_ref, vp_ref, o_ref, l_ref):
        valid = _band_mask(pl.program_id(1))
        lo, hi = _head_masks()
        for sl in _lane_blocks(B_WIDTH):
            qb = q_ref[:, sl]
            kk = jnp.concatenate([kp_ref[:, sl], kc_ref[:, sl]], axis=0)
            vv = jnp.concatenate([vp_ref[:, sl], vc_ref[:, sl]], axis=0)
            outs, lses = [], []
            for hm in (lo, hi):
                sc = _dot(jnp.where(hm, qb, jnp.zeros_like(qb)), kk, NT)
                sc = jnp.where(valid, sc, NEG_INF)
                mx = jnp.max(sc, axis=1, keepdims=True)
                p = jnp.exp(sc - mx)
                den = jnp.sum(p, axis=1, keepdims=True)
                outs.append(_dot(p.astype(BF16), vv, NN) / den)
                lses.append(mx + jnp.log(den))
            o_ref[:, sl] = jnp.where(lo, outs[0], outs[1])
            l_ref[:, sl] = jnp.where(lo, lses[0], lses[1])

    cur, prev = _attn_specs(nb - 1)
    return pl.pallas_call(
        body, grid=(dil, nb), in_specs=[cur, cur, prev, cur, prev], out_specs=[cur, cur],
        out_shape=[jax.ShapeDtypeStruct((dil, n, B_WIDTH), F32)] * 2,
        compiler_params=_cparams("parallel", "parallel"), name=name)(q, k, k, v, v)


def _attn_combine(outs, lses, gb, mixed, name):
    s = mixed.shape[0]
    npat = len(DILATIONS)
    w = B_WIDTH

    def body(*refs):
        o_refs, l_refs = refs[:npat], refs[npat:2 * npat]
        g_ref, _, ob_ref = refs[2 * npat:2 * npat + 3]
        lse_refs = refs[2 * npat + 3:3 * npat + 3]
        mb_ref, stage = refs[3 * npat + 3:]
        os_ = [o_refs[0][...]] + [_load_classes(r, stage, d) for r, d in zip(o_refs[1:], CLASS_DILS)]
        ls = [l_refs[0][...]] + [_load_classes(r, stage, d) for r, d in zip(l_refs[1:], CLASS_DILS)]
        mx = functools.reduce(jnp.maximum, ls)
        ws = [jnp.exp(l - mx) for l in ls]
        tot = functools.reduce(lambda a, b: a + b, ws)
        ob = functools.reduce(lambda a, b: a + b, [wt / tot * o for wt, o in zip(ws, os_)])
        ob_ref[...] = ob
        lse = mx + jnp.log(tot)
        _stage_put(stage, lse)
        lse_refs[0][...] = lse
        for ref, d in zip(lse_refs[1:], CLASS_DILS):
            _store_classes(stage, ref, d)
        mb_ref[...] = (ob * _rsq_mean(ob) * g_ref[...]).astype(BF16)

    lay_specs = [_row_spec(w)] + [_class_spec(d) for d in CLASS_DILS]
    res = pl.pallas_call(
        body, grid=(s // TR,), in_specs=lay_specs * 2 + [_vec_spec(w), ANY],
        out_specs=[_row_spec(w)] + lay_specs + [_row_spec(w, 1)],
        out_shape=[jax.ShapeDtypeStruct((s, w), F32), jax.ShapeDtypeStruct((s, w), F32)]
        + [_class_shape(s, d, F32) for d in CLASS_DILS] + [jax.ShapeDtypeStruct(mixed.shape, mixed.dtype)],
        scratch_shapes=[STAGE], input_output_aliases={2 * npat + 1: npat + 1},
        compiler_params=_cparams("parallel"), name=name)(*outs, *lses, gb, mixed)
    return res[0], dict(zip(DILATIONS, res[1:npat + 1])), res[npat + 1]


def _attn_bwd_prep(dmixed, ob, gb, name):
    s = ob.shape[0]
    w = B_WIDTH
    nlay = len(DILATIONS)

    def body(dm_ref, ob_ref, g_ref, *rest):
        do_refs, dl_refs = rest[:nlay], rest[nlay:2 * nlay]
        dg_ref, stage = rest[2 * nlay:]
        _acc_init([dg_ref])
        ob = ob_ref[...]
        dob, dgt = _rms_bwd(ob, _rsq_mean(ob), g_ref[...], dm_ref[...])
        dg_ref[...] += _colsum(dgt)
        _stage_put(stage, dob)
        do_refs[0][...] = dob.astype(BF16)
        for ref, d in zip(do_refs[1:], CLASS_DILS):
            _store_classes(stage, ref, d)
        lo, hi = _head_masks()
        t = dob * ob
        for b, sl in enumerate(_lane_blocks(w)):
            tb = t[:, sl]
            s0 = jnp.sum(jnp.where(lo, tb, 0.0), axis=1, keepdims=True)
            s1 = jnp.sum(jnp.where(hi, tb, 0.0), axis=1, keepdims=True)
            stage[b] = jnp.where(lo, s0, s1)
        dl_refs[0][...] = _stage_get(stage)
        for ref, d in zip(dl_refs[1:], CLASS_DILS):
            _store_classes(stage, ref, d)

    lay_specs = [_row_spec(w)] + [_class_spec(d) for d in CLASS_DILS]
    shapes = lambda dt: [jax.ShapeDtypeStruct((s, w), dt)] + [_class_shape(s, d, dt) for d in CLASS_DILS]
    res = pl.pallas_call(
        body, grid=(s // TR,), in_specs=[_row_spec(w, 1), _row_spec(w), _vec_spec(w)],
        out_specs=lay_specs * 2 + [_vec_spec(w)],
        out_shape=shapes(BF16) + shapes(F32) + [jax.ShapeDtypeStruct((1, w), F32)],
        scratch_shapes=[STAGE],
        compiler_params=_cparams("arbitrary"), name=name)(dmixed, ob, gb)
    return dict(zip(DILATIONS, res[:nlay])), dict(zip(DILATIONS, res[nlay:2 * nlay])), res[2 * nlay]


def _attn_bwd(q, k, v, do, lse, delta, name):
    dil, n, _ = q.shape
    nb = n // BAND

    def body(q_ref, kc_ref, kp_ref, vc_ref, vp_ref, do_ref, lse_ref, dl_ref,
             dq_ref, dk_ref, dv_ref, ck_ref, cv_ref):
        i = pl.program_id(1)

        @pl.when(i == 0)
        def _():
            ck_ref[...] = jnp.zeros_like(ck_ref)
            cv_ref[...] = jnp.zeros_like(cv_ref)

        @pl.when(i < nb)
        def _():
            valid = _band_mask(i)
            lo, hi = _head_masks()
            lane = lax.broadcasted_iota(jnp.int32, (1, LANES), 1)
            for sl in _lane_blocks(B_WIDTH):
                qb = q_ref[:, sl]
                dob = do_ref[:, sl]
                kk = jnp.concatenate([kp_ref[:, sl], kc_ref[:, sl]], axis=0)
                vv = jnp.concatenate([vp_ref[:, sl], vc_ref[:, sl]], axis=0)
                lseb = lse_ref[:, sl]
                dlb = dl_ref[:, sl]
                dq = jnp.zeros((BAND, LANES), F32)
                dkk = jnp.zeros((2 * BAND, LANES), F32)
                dvv = jnp.zeros((2 * BAND, LANES), F32)
                for hm, first in ((lo, 0), (hi, HEAD_DIM)):
                    pick = lane == first
                    lse_h = jnp.sum(jnp.where(pick, lseb, 0.0), axis=1, keepdims=True)
                    dl_h = jnp.sum(jnp.where(pick, dlb, 0.0), axis=1, keepdims=True)
                    qm = jnp.where(hm, qb, jnp.zeros_like(qb))
                    dom = jnp.where(hm, dob, jnp.zeros_like(dob))
                    sc = _dot(qm, kk, NT)
                    p = jnp.where(valid, jnp.exp(sc - lse_h), 0.0)
                    dp = _dot(dom, vv, NT)
                    ds = (p * (dp - dl_h)).astype(BF16)
                    dq += _dot(ds, jnp.where(hm, kk, jnp.zeros_like(kk)), NN)
                    dkk += _dot(ds, qm, TN)
                    dvv += _dot(p.astype(BF16), dom, TN)
                dq_ref[:, sl] = dq
                dk_ref[:, sl] = ck_ref[:, sl] + dkk[:BAND]
                dv_ref[:, sl] = cv_ref[:, sl] + dvv[:BAND]
                ck_ref[:, sl] = dkk[BAND:]
                cv_ref[:, sl] = dvv[BAND:]

        @pl.when(i == nb)
        def _():
            dk_ref[...] = ck_ref[...]
            dv_ref[...] = cv_ref[...]

    cur, prev = _attn_specs(nb - 1)
    lag = pl.BlockSpec((None, BAND, B_WIDTH), lambda r, i: (r, jnp.maximum(i - 1, 0), 0))
    shape = jax.ShapeDtypeStruct((dil, n, B_WIDTH), F32)
    return pl.pallas_call(
        body, grid=(dil, nb + 1), in_specs=[cur, cur, prev, cur, prev, cur, cur, cur],
        out_specs=[cur, lag, lag], out_shape=[shape] * 3,
        scratch_shapes=[pltpu.VMEM((BAND, B_WIDTH), F32)] * 2,
        compiler_params=_cparams("arbitrary", "arbitrary"), name=name)(q, k, k, v, v, do, lse, delta)


def _rope_bwd(dqs, dks, dvs, tabs, dproj, name):
    s = dproj.shape[0]
    half = ROT_DIM // 2
    scale = HEAD_DIM ** -0.5
    npat = len(DILATIONS)
    w = B_WIDTH

    def body(*refs):
        groups = [refs[g * npat:(g + 1) * npat] for g in range(3)]
        c_ref, s1_ref, s2_ref, _, o_ref, stage = refs[3 * npat:]

        def total(rs):
            acc = rs[0][...]
            for ref, d in zip(rs[1:], CLASS_DILS):
                acc = acc + _load_classes(ref, stage, d)
            return acc

        def unrope(g):
            c, s1, s2 = c_ref[...], s1_ref[...], s2_ref[...]
            for sl in _lane_blocks(w):
                gb = g[:, sl]
                o = gb * c + pltpu.roll(gb * s1, half, 1) + pltpu.roll(gb * s2, LANES - half, 1)
                o_ref[:, sl] = o.astype(BF16)

        which = pl.program_id(1)

        @pl.when(which == 0)
        def _():
            unrope(total(groups[0]) * scale)

        @pl.when(which == 1)
        def _():
            unrope(total(groups[1]))

        @pl.when(which == 2)
        def _():
            o_ref[...] = total(groups[2]).astype(BF16)

    tab = pl.BlockSpec((TR, LANES), lambda i, j: (i, 0))
    nat = pl.BlockSpec((TR, w), lambda i, j: (i, 0))
    lay_specs = [nat] + [_class_spec(d) for d in CLASS_DILS]
    first_col = 2 * A_WIDTH // w
    return pl.pallas_call(
        body, grid=(s // TR, 3), in_specs=lay_specs * 3 + [tab] * 3 + [ANY],
        out_specs=pl.BlockSpec((TR, w), lambda i, j: (i, first_col + j)),
        out_shape=jax.ShapeDtypeStruct(dproj.shape, dproj.dtype), scratch_shapes=[STAGE],
        input_output_aliases={3 * npat + 3: 0},
        compiler_params=_cparams("parallel", "arbitrary"), name=name)(*dqs, *dks, *dvs, *tabs, dproj)


TK = 512
HALO = 16


def _taps_before(buf, x, halo):
    buf[0:HALO, :] = halo
    buf[HALO:HALO + TM, :] = x
    return buf[pl.ds(HALO - 2, TM), :], buf[pl.ds(HALO - 1, TM), :], x


def _taps_after(buf, x, halo):
    buf[0:TM, :] = x
    buf[TM:TM + HALO, :] = halo
    return buf[pl.ds(1, TM), :], buf[pl.ds(2, TM), :]


def _conv_value(taps, cw_ref, cb_ref, h):
    return cb_ref[h] + cw_ref[h, 0:1, :] * taps[0] + cw_ref[h, 1:2, :] * taps[1] + cw_ref[h, 2:3, :] * taps[2]


SHIFT_BUF = pltpu.VMEM((HALO + TM, TK), F32)


def _ffn_weight_specs(ncol):
    per_up = (2 * D_FF // N_CHIPS) // TK
    per_dn = (D_FF // N_CHIPS) // TK
    wg = pl.BlockSpec((None, None, D_MODEL, TK), lambda i, j: (j // per_up, 0, 0, j % per_up))
    wv = pl.BlockSpec((None, None, D_MODEL, TK), lambda i, j: ((j + ncol) // per_up, 0, 0, (j + ncol) % per_up))
    wd = pl.BlockSpec((None, None, TK, D_MODEL), lambda i, j: (j // per_dn, 0, j % per_dn, 0))
    cw = pl.BlockSpec((2, 3, TK), lambda i, j: (0, 0, j))
    cb = pl.BlockSpec((2, 1, TK), lambda i, j: (0, 0, j))
    return wg, wv, wd, cw, cb


def _ffn_forward(h2, w_up, w_down, cw3, cb3, name, gather=None):
    s = h2.shape[0]
    nm, ncol = s // TM, D_FF // TK
    ng = 0 if gather is None else len(gather)

    def body(*refs):
        h_ref, wg_ref, wv_ref, wd_ref, cw_ref, cb_ref = refs[:6]
        g_in = refs[6:6 + ng]
        y_ref, up_ref, f_ref = refs[6 + ng:9 + ng]
        g_out = refs[9 + ng:9 + 2 * ng]
        carry, acc, buf_g, buf_v = refs[9 + 2 * ng:13 + 2 * ng]
        i, j = pl.program_id(0), pl.program_id(1)
        if ng:
            start, relay, finish = _gather_steps(g_in, g_out, *refs[13 + 2 * ng:])
            pl.when((i == 0) & (j == 0))(start)
            pl.when((i == nm - 1) & (j == 0))(relay)

        @pl.when((i == 0) & (j == 0))
        def _():
            carry[...] = jnp.zeros_like(carry)

        h = h_ref[...]
        conv = []
        for hh, w_ref, buf in ((0, wg_ref, buf_g), (1, wv_ref, buf_v)):
            up = _dot(h, w_ref[...], NN).astype(BF16)
            up_ref[hh] = up
            x = up.astype(F32)
            conv.append(_conv_value(_taps_before(buf, x, carry[j, hh]), cw_ref, cb_ref, hh))
            carry[j, hh] = x[TM - HALO:, :]
        y = (_gelu_tanh(conv[0])[0] * conv[1]).astype(BF16)
        y_ref[...] = y
        part = _dot(y, wd_ref[...], NN)

        @pl.when(j == 0)
        def _():
            acc[...] = part

        @pl.when(j > 0)
        def _():
            acc[...] += part

        @pl.when(j == ncol - 1)
        def _():
            f_ref[...] = acc[...]

        if ng:
            pl.when((i == nm - 1) & (j == ncol - 1))(finish)

    wg, wv, wd, cw, cb = _ffn_weight_specs(ncol)
    res = pl.pallas_call(
        body, grid=(nm, ncol),
        in_specs=[pl.BlockSpec((TM, D_MODEL), lambda i, j: (i, 0)), wg, wv, wd, cw, cb] + [ANY] * ng,
        out_specs=[pl.BlockSpec((TM, TK), lambda i, j: (i, j)), pl.BlockSpec((2, TM, TK), lambda i, j: (0, i, j)),
                   pl.BlockSpec((TM, D_MODEL), lambda i, j: (i, 0))] + [ANY] * ng,
        out_shape=[jax.ShapeDtypeStruct((s, D_FF), BF16), jax.ShapeDtypeStruct((2, s, D_FF), BF16),
                   jax.ShapeDtypeStruct((s, D_MODEL), F32)] + _gathered_shapes(gather or []),
        scratch_shapes=[pltpu.VMEM((ncol, 2, HALO, TK), F32), pltpu.VMEM((TM, D_MODEL), F32), SHIFT_BUF, SHIFT_BUF]
        + (_gather_sems(ng) if ng else []),
        compiler_params=_cparams("arbitrary", "arbitrary"), name=name)(h2, w_up, w_up, w_down, cw3, cb3,
                                                                      *(gather or []))
    return res[:3], list(res[3:])


def _ffn_backward(df, w_up, w_down, up3, cw3, cb3, name, scatter=None):
    s = df.shape[0]
    nm, ncol = s // TM, D_FF // TK
    ns = 0 if scatter is None else len(scatter)

    def body(*refs):
        df_ref, wg_ref, wv_ref, wd_ref, cw_ref, cb_ref, up_ref, halo_ref = refs[:8]
        s_in = refs[8:8 + ns]
        dup_ref, dh_ref, sums_ref = refs[8 + ns:11 + ns]
        s_out = refs[11 + ns:11 + 2 * ns]
        carry, acc, buf_g, buf_v, buf_dg, buf_dv = refs[11 + 2 * ns:17 + 2 * ns]
        i, j = pl.program_id(0), pl.program_id(1)
        if ns:
            start, finish = _scatter_steps(s_in, s_out, *refs[17 + 2 * ns:])
            pl.when((i == 0) & (j == 0))(start)

        @pl.when((i == 0) & (j == 0))
        def _():
            carry[...] = jnp.zeros_like(carry)
            sums_ref[...] = jnp.zeros_like(sums_ref)

        seq_first = i == nm - 1
        dy = _dot(df_ref[...], wd_ref[...], NT)
        conv, taps = [], []
        for hh, buf in ((0, buf_g), (1, buf_v)):
            halo = jnp.where(seq_first, 0.0, halo_ref[hh].astype(F32))
            tp = _taps_before(buf, up_ref[hh].astype(F32), halo)
            conv.append(_conv_value(tp, cw_ref, cb_ref, hh))
            taps.append(tp)
        act, t = _gelu_tanh(conv[0])
        dcs = (dy * conv[1] * _gelu_tanh_grad(conv[0], t), dy * act)
        row = lax.broadcasted_iota(jnp.int32, (8, 1), 0)
        part = None
        for hh, w_ref, buf in ((0, wg_ref, buf_dg), (1, wv_ref, buf_dv)):
            dc, tp = dcs[hh], taps[hh]
            upd = jnp.zeros((8, TK), F32)
            for ridx, sm in enumerate((_colsum(dc * tp[0]), _colsum(dc * tp[1]), _colsum(dc * tp[2]), _colsum(dc))):
                upd = jnp.where(row == ridx, sm, upd)
            sums_ref[j, hh] += upd
            after1, after2 = _taps_after(buf, dc, carry[j, hh])
            dup = (cw_ref[hh, 2:3, :] * dc + cw_ref[hh, 1:2, :] * after1 + cw_ref[hh, 0:1, :] * after2).astype(BF16)
            carry[j, hh] = dc[:HALO, :]
            dup_ref[hh] = dup
            d = _dot(dup, w_ref[...], NT)
            part = d if part is None else part + d

        @pl.when(j == 0)
        def _():
            acc[...] = part

        @pl.when(j > 0)
        def _():
            acc[...] += part

        @pl.when(j == ncol - 1)
        def _():
            dh_ref[...] = acc[...]

        if ns:
            pl.when((i == nm - 1) & (j == ncol - 1))(finish)

    wg, wv, wd, cw, cb = _ffn_weight_specs(ncol)
    rev = lambda i: nm - 1 - i
    res = pl.pallas_call(
        body, grid=(nm, ncol),
        in_specs=[pl.BlockSpec((TM, D_MODEL), lambda i, j: (rev(i), 0)), wg, wv, wd, cw, cb,
                  pl.BlockSpec((2, TM, TK), lambda i, j: (0, rev(i), j)),
                  pl.BlockSpec((2, HALO, TK), lambda i, j: (0, jnp.maximum(rev(i) * (TM // HALO) - 1, 0), j))]
        + [ANY] * ns,
        out_specs=[pl.BlockSpec((2, TM, TK), lambda i, j: (0, rev(i), j)),
                   pl.BlockSpec((TM, D_MODEL), lambda i, j: (rev(i), 0)),
                   pl.BlockSpec((ncol, 2, 8, TK), lambda i, j: (0, 0, 0, 0))] + [ANY] * ns,
        out_shape=[jax.ShapeDtypeStruct((2, s, D_FF), BF16), jax.ShapeDtypeStruct((s, D_MODEL), F32),
                   jax.ShapeDtypeStruct((ncol, 2, 8, TK), F32)] + (_scattered_shapes() if ns else []),
        scratch_shapes=[pltpu.VMEM((ncol, 2, HALO, TK), F32), pltpu.VMEM((TM, D_MODEL), F32)] + [SHIFT_BUF] * 4
        + (_scatter_sems() if ns else []),
        compiler_params=_cparams("arbitrary", "arbitrary"), name=name)(df, w_up, w_up, w_down, cw3, cb3, up3, up3,
                                                                      *(scatter or []))
    return res[:3], list(res[3:])


def _wspec(rows, cols, index_map):
    return pl.BlockSpec((None, None, rows, cols), index_map)


def _layer_forward(l, x0, h1, p, wg, tabs, gather=None):
    s = x0.shape[0]
    nm = s // TMM
    tag = f"_l{l}"
    proj = _matmul(
        h1, wg["w_in"], grid=(nm, N_CHIPS), a_spec=pl.BlockSpec((TMM, D_MODEL), lambda i, j: (i, 0)),
        b_spec=_wspec(D_MODEL, IN_COLS // N_CHIPS, lambda i, j: (j, 0, 0, 0)),
        o_spec=pl.BlockSpec((TMM, IN_COLS // N_CHIPS), lambda i, j: (i, j)), o_shape=(s, IN_COLS), o_dtype=F32,
        dims=NN, nk=1, kaxis=None, acc_shape=None, name="proj" + tag)
    ma = _mixer_a_fwd(proj, p["v_norm_g"], p["v_norm_b"], p["w_spatial"], p["bs_full"], p["out_norm_a"],
                      "mixer_a_fwd" + tag)
    q, k, v = _rope_fwd(proj, tabs, "rope_fwd" + tag)
    outs, lses = zip(*[_attn_fwd(_as_classes(q[d]), _as_classes(k[d]), _as_classes(v[d]), f"attn_fwd_d{d}" + tag)
                       for d in DILATIONS])
    outs = [o.reshape(s, B_WIDTH) if d == 1 else o for o, d in zip(outs, DILATIONS)]
    lses = [t.reshape(s, B_WIDTH) if d == 1 else t for t, d in zip(lses, DILATIONS)]
    ob, lse, mixed = _attn_combine(outs, lses, p["out_norm_b"], ma, "attn_combine" + tag)
    w_out_all = pl.BlockSpec((N_CHIPS, None, D_MODEL // N_CHIPS, D_MODEL), lambda i: (0, 0, 0, 0))
    y1 = _matmul(
        mixed, wg["w_out"], grid=(nm,), a_spec=pl.BlockSpec((TMM, D_MODEL), lambda i: (i, 0)), b_spec=w_out_all,
        o_spec=pl.BlockSpec((TMM, D_MODEL), lambda i: (i, 0)), o_shape=(s, D_MODEL), o_dtype=F32,
        dims=NN, nk=1, kaxis=None, acc_shape=None, name="mix_out" + tag, b_2d=(D_MODEL, D_MODEL))
    x1, h2 = _residual_norm(x0, y1, p["post_mix_norm"], p["pre_ffn_norm"], "post_mix" + tag)
    (y, up3, f), gathered = _ffn_forward(h2, wg["w_up"], wg["w_down"], p["cw3"], p["cb3"], "ffn_fwd" + tag, gather)
    saved = dict(x0=x0, h1=h1, proj=proj, q=q, k=k, v=v, ob=ob, lse=lse, mixed=mixed, y1=y1, x1=x1, h2=h2,
                 up3=up3, y=y, f=f)
    return saved, gathered


def _layer_backward(l, dx2, sv, p, wg, tabs, scatter=None):
    s = dx2.shape[0]
    nm = s // TMM
    tag = f"_l{l}"
    g = {}
    df, g["post_ffn_norm"] = _norm_bwd_out(dx2, sv["f"], p["post_ffn_norm"], "norm_bwd_out" + tag)
    (dup3, dh2, conv_sums), scattered = _ffn_backward(df, wg["w_up"], wg["w_down"], sv["up3"], p["cw3"], p["cb3"],
                                                      "ffn_bwd" + tag, scatter)
    sums = conv_sums.transpose(1, 2, 0, 3).reshape(2, 8, D_FF)
    g["conv_w"] = jnp.concatenate([sums[0, :3], sums[1, :3]], axis=1)
    g["conv_b"] = jnp.concatenate([sums[0, 3:4], sums[1, 3:4]], axis=1)
    tn = 1024
    gw_up = _matmul(
        sv["h2"], dup3, grid=(2 * D_FF // tn, nm), a_spec=pl.BlockSpec((TMM, D_MODEL), lambda n, m: (m, 0)),
        b_spec=pl.BlockSpec((None, TMM, tn), lambda n, m: (n // (D_FF // tn), m, n % (D_FF // tn))),
        o_spec=pl.BlockSpec((None, D_MODEL, tn), lambda n, m: (n // 2, 0, n % 2)),
        o_shape=(N_CHIPS, D_MODEL, 2 * D_FF // N_CHIPS), o_dtype=BF16,
        dims=TN, nk=nm, kaxis=1, acc_shape=(D_MODEL, tn), name="w_up_grad" + tag)
    gw_down = _matmul(
        sv["y"], df, grid=(D_FF // tn, 2, nm), a_spec=pl.BlockSpec((TMM, tn), lambda k, h, m: (m, k)),
        b_spec=pl.BlockSpec((TMM, D_MODEL // 2), lambda k, h, m: (m, h)),
        o_spec=pl.BlockSpec((None, tn, D_MODEL // 2), lambda k, h, m: (h, k, 0)),
        o_shape=(2, D_FF, D_MODEL // 2), o_dtype=BF16,
        dims=TN, nk=nm, kaxis=2, acc_shape=(tn, D_MODEL // 2), name="w_down_grad" + tag)
    dx1, dy1, g["pre_ffn_norm"], g["post_mix_norm"] = _norm_bwd_mid(
        dx2, dh2, sv["x1"], p["pre_ffn_norm"], sv["y1"], p["post_mix_norm"], "norm_bwd_mid" + tag)
    w_out_all = pl.BlockSpec((N_CHIPS, None, D_MODEL // N_CHIPS, D_MODEL), lambda i: (0, 0, 0, 0))
    dmixed = _matmul(
        dy1, wg["w_out"], grid=(nm,), a_spec=pl.BlockSpec((TMM, D_MODEL), lambda i: (i, 0)), b_spec=w_out_all,
        o_spec=pl.BlockSpec((TMM, D_MODEL), lambda i: (i, 0)), o_shape=(s, D_MODEL), o_dtype=F32,
        dims=NT, nk=1, kaxis=None, acc_shape=None, name="mix_out_bwd" + tag, b_2d=(D_MODEL, D_MODEL))
    gw_out = _matmul(
        sv["mixed"], dy1, grid=(2, nm), a_spec=pl.BlockSpec((TMM, D_MODEL), lambda h, m: (m, 0)),
        b_spec=pl.BlockSpec((TMM, D_MODEL // 2), lambda h, m: (m, h)),
        o_spec=pl.BlockSpec((None, D_MODEL, D_MODEL // 2), lambda h, m: (h, 0, 0)),
        o_shape=(2, D_MODEL, D_MODEL // 2), o_dtype=BF16,
        dims=TN, nk=nm, kaxis=1, acc_shape=(D_MODEL, D_MODEL // 2), name="w_out_grad" + tag)
    dpa, g["out_norm_a"], g["v_norm_g"], g["v_norm_b"], dbs, g["w_spatial"] = _mixer_a_bwd(
        sv["proj"], dmixed, p["v_norm_g"], p["v_norm_b"], p["w_spatial"], p["bs_full"], p["out_norm_a"],
        "mixer_a_bwd" + tag)
    g["b_spatial"] = dbs[:, ::GROUP_DIM].T
    dob, delta, g["out_norm_b"] = _attn_bwd_prep(dmixed, sv["ob"], p["out_norm_b"], "attn_bwd_prep" + tag)
    dqs, dks, dvs = zip(*[
        _attn_bwd(*(_as_classes(t[d]) for t in (sv["q"], sv["k"], sv["v"], dob, sv["lse"], delta)),
                  f"attn_bwd_d{d}" + tag) for d in DILATIONS])
    nat = lambda ts: [t.reshape(s, B_WIDTH) if d == 1 else t for t, d in zip(ts, DILATIONS)]
    dproj = _rope_bwd(nat(dqs), nat(dks), nat(dvs), tabs, dpa, "rope_bwd" + tag)
    wcol = IN_COLS // N_CHIPS
    dh1 = _matmul(
        dproj, wg["w_in"], grid=(nm, N_CHIPS), a_spec=pl.BlockSpec((TMM, wcol), lambda i, n: (i, n)),
        b_spec=_wspec(D_MODEL, wcol, lambda i, n: (n, 0, 0, 0)),
        o_spec=pl.BlockSpec((TMM, D_MODEL), lambda i, n: (i, 0)), o_shape=(s, D_MODEL), o_dtype=F32,
        dims=NT, nk=N_CHIPS, kaxis=1, acc_shape=(TMM, D_MODEL), name="proj_bwd" + tag)
    gw_in = _matmul(
        sv["h1"], dproj, grid=(N_CHIPS, nm), a_spec=pl.BlockSpec((TMM, D_MODEL), lambda n, m: (m, 0)),
        b_spec=pl.BlockSpec((TMM, wcol), lambda n, m: (m, n)),
        o_spec=pl.BlockSpec((None, D_MODEL, wcol), lambda n, m: (n, 0, 0)),
        o_shape=(N_CHIPS, D_MODEL, wcol), o_dtype=BF16,
        dims=TN, nk=nm, kaxis=1, acc_shape=(D_MODEL, wcol), name="w_in_grad" + tag)
    dx0, g["pre_mix_norm"] = _norm_bwd_in(dx1, dh1, sv["x0"], p["pre_mix_norm"], "norm_bwd_in" + tag)
    big = dict(w_in=gw_in, w_up=gw_up, w_out=gw_out, w_down=gw_down)
    return dx0, big, g, scattered


SMALL = ("pre_mix_norm", "v_norm_g", "v_norm_b", "w_spatial", "b_spatial", "out_norm_a", "out_norm_b",
         "post_mix_norm", "pre_ffn_norm", "conv_b", "post_ffn_norm")
BIG = ("w_in", "w_out", "w_up", "w_down")
DEPTH = 2


def _layer_params(l, small, conv_w_full):
    p = {n: small[n][l].reshape(1, -1) for n in SMALL if n not in ("w_spatial", "b_spatial")}
    p["w_spatial"] = small["w_spatial"][l]
    p["bs_full"] = jnp.repeat(small["b_spatial"][l].T, GROUP_DIM, axis=1)
    p["cw3"] = conv_w_full[l].reshape(3, 2, D_FF).transpose(1, 0, 2)
    p["cb3"] = small["conv_b"][l].reshape(2, 1, D_FF)
    return p


def _mesh_pos():
    return lax.axis_index("x"), lax.axis_index("y"), lax.axis_index("c")


def _other_chips(x, y):
    return [(1 - x, y), (x, 1 - y), (1 - x, 1 - y)]


def _gathered_shapes(blocks):
    return [jax.ShapeDtypeStruct((N_CHIPS, 1) + a.shape, a.dtype) for a in blocks]


def _gather_sems(nw):
    n = 2 * nw * (N_CHIPS - 1) + nw
    return [pltpu.SemaphoreType.DMA((n,)), pltpu.SemaphoreType.DMA((n,))]


def _gather_steps(ins, outs, send, recv):
    nw, nrel = len(ins), N_CHIPS - 1
    x, y, c = _mesh_pos()
    mine, sibling, chips = 2 * x + y, (x, y, 1 - c), _other_chips(x, y)

    def copy(src, dst, slot, to):
        return pltpu.make_async_remote_copy(src_ref=src, dst_ref=dst, send_sem=send.at[slot],
                                            recv_sem=recv.at[slot], device_id=to, device_id_type=MESH)

    def half_rows(t, core):
        rows = ins[t].shape[0] // 2
        return pl.ds(pl.multiple_of(core * rows, rows), rows)

    def landing(t, chip, core):
        return outs[t].at[chip, 0, half_rows(t, core), :]

    slots = [(t, r, chip) for t in range(nw) for r, chip in enumerate(chips)]
    own = [copy(ins[t], outs[t].at[mine, 0], 2 * nw * nrel + t, sibling) for t in range(nw)]
    first = [copy(ins[t].at[half_rows(t, c), :], landing(t, mine, c), t * nrel + r, (px, py, c))
             for t, r, (px, py) in slots]
    relays = [copy(landing(t, 2 * px + py, c), landing(t, 2 * px + py, c), nw * nrel + t * nrel + r, sibling)
              for t, r, (px, py) in slots]

    def start():
        for cp in own + first:
            cp.start()

    def relay():
        for (t, r, (px, py)), cp in zip(slots, relays):
            copy(landing(t, 2 * px + py, c), landing(t, 2 * px + py, c), t * nrel + r, (px, py, c)).wait_recv()
            cp.start()

    def finish():
        for t, r, (px, py) in slots:
            passed = landing(t, 2 * px + py, 1 - c)
            copy(passed, passed, nw * nrel + t * nrel + r, sibling).wait_recv()
        for cp in first + relays:
            cp.wait_send()
        for cp in own:
            cp.wait()

    return start, relay, finish


def _gather_weights(blocks, name):
    nw = len(blocks)

    def body(*refs):
        start, relay, finish = _gather_steps(refs[:nw], refs[nw:2 * nw], *refs[2 * nw:])
        start()
        relay()
        finish()

    return pl.pallas_call(
        body, in_specs=[ANY] * nw, out_specs=[ANY] * nw, out_shape=_gathered_shapes(blocks),
        scratch_shapes=_gather_sems(nw), name=name)(*blocks)


HALF_ROWS = D_MODEL // 2


def _pair_exchange(g, name):
    shapes = [(N_CHIPS, HALF_ROWS, IN_COLS // N_CHIPS), (N_CHIPS, HALF_ROWS, 2 * D_FF // N_CHIPS),
              (D_MODEL, D_MODEL // 2), (D_FF, D_MODEL // 2)]

    def body(gin, gup, gout, gdn, rin, rup, rout, rdn, send, recv):
        x, y, c = _mesh_pos()
        o = 1 - c
        rows = pl.ds(pl.multiple_of(o * HALF_ROWS, HALF_ROWS), HALF_ROWS)
        pairs = [(gin.at[:, rows, :], rin), (gup.at[:, rows, :], rup), (gout.at[o], rout), (gdn.at[o], rdn)]
        cps = [pltpu.make_async_remote_copy(src_ref=src, dst_ref=dst, send_sem=send.at[t], recv_sem=recv.at[t],
                                            device_id=(x, y, o), device_id_type=MESH)
               for t, (src, dst) in enumerate(pairs)]
        for cp in cps:
            cp.start()
        for cp in cps:
            cp.wait()

    return pl.pallas_call(
        body, in_specs=[ANY] * 4, out_specs=[ANY] * 4,
        out_shape=[jax.ShapeDtypeStruct(sh, BF16) for sh in shapes],
        scratch_shapes=[pltpu.SemaphoreType.DMA((4,)), pltpu.SemaphoreType.DMA((4,))],
        name=name)(g["w_in"], g["w_up"], g["w_out"], g["w_down"])


def _pair_sum(g, recv, pos, name_prefix):
    def add(a, b, grid, a_spec, b_spec, shape, name):
        def body(pos_ref, a_ref, b_ref, o_ref):
            o_ref[...] = (a_ref[...].astype(F32) + b_ref[...].astype(F32)).astype(BF16)

        return pl.pallas_call(
            body, grid_spec=pltpu.PrefetchScalarGridSpec(
                num_scalar_prefetch=1, grid=grid, in_specs=[a_spec, b_spec], out_specs=b_spec),
            out_shape=jax.ShapeDtypeStruct(shape, BF16), compiler_params=_cparams(*["parallel"] * len(grid)),
            name=name)(pos, a, b)

    rin, rup, rout, rdn = recv
    wi, wu = IN_COLS // N_CHIPS, 2 * D_FF // N_CHIPS
    s_in = add(g["w_in"], rin, (N_CHIPS,), pl.BlockSpec((None, HALF_ROWS, wi), lambda j, pos: (j, pos[2], 0)),
               pl.BlockSpec((None, HALF_ROWS, wi), lambda j, pos: (j, 0, 0)), rin.shape, name_prefix + "_in")
    s_up = add(g["w_up"], rup, (N_CHIPS,), pl.BlockSpec((None, HALF_ROWS, wu), lambda j, pos: (j, pos[2], 0)),
               pl.BlockSpec((None, HALF_ROWS, wu), lambda j, pos: (j, 0, 0)), rup.shape, name_prefix + "_up")
    hc = D_MODEL // 2
    s_out = add(g["w_out"], rout, (1,), pl.BlockSpec((None, D_MODEL, hc), lambda j, pos: (pos[2], 0, 0)),
                pl.BlockSpec((D_MODEL, hc), lambda j, pos: (0, 0)), rout.shape, name_prefix + "_out")
    s_dn = add(g["w_down"], rdn, (N_CHIPS,), pl.BlockSpec((None, D_FF // N_CHIPS, hc), lambda j, pos: (pos[2], j, 0)),
               pl.BlockSpec((D_FF // N_CHIPS, hc), lambda j, pos: (j, 0)), rdn.shape, name_prefix + "_down")
    return s_in, s_up, s_out, s_dn


OUT_ROWS = D_MODEL // N_CHIPS
DOWN_ROWS = D_FF // N_CHIPS


def _scattered_shapes():
    nrel = N_CHIPS - 1
    shapes = [(nrel, HALF_ROWS, IN_COLS // N_CHIPS), (nrel, HALF_ROWS, 2 * D_FF // N_CHIPS),
              (nrel, OUT_ROWS, D_MODEL // 2), (nrel, DOWN_ROWS, D_MODEL // 2)]
    return [jax.ShapeDtypeStruct(sh, BF16) for sh in shapes]


def _scatter_sems():
    n = 4 * (N_CHIPS - 1)
    return [pltpu.SemaphoreType.DMA((n,)), pltpu.SemaphoreType.DMA((n,))]


def _scatter_steps(sums, outs, send, recv):
    nrel = N_CHIPS - 1
    sin, sup, sout, sdn = sums
    x, y, c = _mesh_pos()
    cps = []
    for r, (px, py) in enumerate(_other_chips(x, y)):
        j = 2 * px + py
        pieces = [sin.at[j], sup.at[j], sout.at[pl.ds(pl.multiple_of(j * OUT_ROWS, OUT_ROWS), OUT_ROWS), :],
                  sdn.at[pl.ds(pl.multiple_of(j * DOWN_ROWS, DOWN_ROWS), DOWN_ROWS), :]]
        for t, src in enumerate(pieces):
            cps.append(pltpu.make_async_remote_copy(
                src_ref=src, dst_ref=outs[t].at[r], send_sem=send.at[t * nrel + r], recv_sem=recv.at[t * nrel + r],
                device_id=(px, py, c), device_id_type=MESH))

    def start():
        for cp in cps:
            cp.start()

    def finish():
        for cp in cps:
            cp.wait()

    return start, finish


def _chip_scatter(sums, name):
    def body(*refs):
        start, finish = _scatter_steps(refs[:4], refs[4:8], *refs[8:])
        start()
        finish()

    return pl.pallas_call(
        body, in_specs=[ANY] * 4, out_specs=[ANY] * 4, out_shape=_scattered_shapes(),
        scratch_shapes=_scatter_sems(), name=name)(*sums)


def _chip_sum(sums, recv, pos, name_prefix):
    def add(a, b, a_spec, shape, name):
        def body(pos_ref, a_ref, b_ref, o_ref):
            tot = a_ref[...].astype(F32)
            for r in range(N_CHIPS - 1):
                tot = tot + b_ref[r].astype(F32)
            o_ref[...] = tot

        return pl.pallas_call(
            body, grid_spec=pltpu.PrefetchScalarGridSpec(
                num_scalar_prefetch=1, grid=(1,), in_specs=[a_spec, pl.BlockSpec(b.shape, lambda i, pos: (0, 0, 0))],
                out_specs=pl.BlockSpec((None,) + shape, lambda i, pos: (pos[2], 0, 0))),
            out_shape=jax.ShapeDtypeStruct((2,) + shape, F32), compiler_params=_cparams("arbitrary"),
            name=name)(pos, a, b)

    s_in, s_up, s_out, s_dn = sums
    rin, rup, rout, rdn = recv
    wi, wu, hc = IN_COLS // N_CHIPS, 2 * D_FF // N_CHIPS, D_MODEL // 2
    chip = lambda pos: 2 * pos[0] + pos[1]
    t_in = add(s_in, rin, pl.BlockSpec((None, HALF_ROWS, wi), lambda i, pos: (chip(pos), 0, 0)), (HALF_ROWS, wi),
               name_prefix + "_in")
    t_up = add(s_up, rup, pl.BlockSpec((None, HALF_ROWS, wu), lambda i, pos: (chip(pos), 0, 0)), (HALF_ROWS, wu),
               name_prefix + "_up")
    t_out = add(s_out, rout, pl.BlockSpec((OUT_ROWS, hc), lambda i, pos: (chip(pos), 0)), (OUT_ROWS, hc),
                name_prefix + "_out")
    t_dn = add(s_dn, rdn, pl.BlockSpec((DOWN_ROWS, hc), lambda i, pos: (chip(pos), 0)), (DOWN_ROWS, hc),
               name_prefix + "_down")
    return t_in, t_up, t_out, t_dn


def _pair_share(totals, name):
    n = len(totals)

    def body(*refs):
        ins, outs = refs[:n], refs[n:2 * n]
        send, recv = refs[2 * n:]
        x, y, c = _mesh_pos()
        o = 1 - c
        cps = [pltpu.make_async_remote_copy(src_ref=ins[t].at[c], dst_ref=outs[t].at[c], send_sem=send.at[t],
                                            recv_sem=recv.at[t], device_id=(x, y, o), device_id_type=MESH)
               for t in range(n)]
        for cp in cps:
            cp.start()
        for t in range(n):
            pltpu.make_async_remote_copy(src_ref=ins[t].at[o], dst_ref=outs[t].at[o], send_sem=send.at[t],
                                         recv_sem=recv.at[t], device_id=(x, y, o), device_id_type=MESH).wait_recv()
        for cp in cps:
            cp.wait_send()

    return pl.pallas_call(
        body, in_specs=[ANY] * n, out_specs=[ANY] * n,
        out_shape=[jax.ShapeDtypeStruct(t.shape, t.dtype) for t in totals],
        scratch_shapes=[pltpu.SemaphoreType.DMA((n,)), pltpu.SemaphoreType.DMA((n,))],
        input_output_aliases={t: t for t in range(n)}, name=name)(*totals)


def _chip_sums(l, g, pos):
    recv = _pair_exchange(g, f"pair_exchange_l{l}")
    return _pair_sum(g, recv, pos, f"pair_sum_l{l}")


def _gradient_shards(l, sums, scattered, pos):
    totals = _chip_sum(sums, scattered, pos, f"chip_sum_l{l}")
    f_in, f_up, f_out, f_dn = _pair_share(totals, f"pair_share_l{l}")
    f_in = f_in.reshape(D_MODEL, IN_COLS // N_CHIPS)
    f_up = f_up.reshape(D_MODEL, 2 * D_FF // N_CHIPS)
    f_out = f_out.transpose(1, 0, 2).reshape(OUT_ROWS, D_MODEL)
    f_dn = f_dn.transpose(1, 0, 2).reshape(DOWN_ROWS, D_MODEL)
    return dict(w_in=f_in, w_up=f_up, w_out=f_out, w_down=f_dn)


N_DEV = 8


def _allreduce_small(packed, name):
    rows = packed.shape[0]

    def body(x_ref, out_ref, gath, send_sems, recv_sems, local_sem):
        x, y, c = _mesh_pos()
        me, sibling = (x, y, c), (x, y, 1 - c)
        chips = _other_chips(x, y)

        def blk(px, py, pc):
            return gath.at[pl.ds(pl.multiple_of((4 * px + 2 * py + pc) * rows, 8), rows), :]

        def copy(k, block, to, src=None):
            return pltpu.make_async_remote_copy(
                src_ref=blk(*block) if src is None else src, dst_ref=blk(*block), send_sem=send_sems.at[k],
                recv_sem=recv_sems.at[k], device_id=to, device_id_type=MESH)

        mine = pltpu.make_async_copy(x_ref, blk(*me), local_sem)
        mine.start()
        first = [copy(0, me, sibling, src=x_ref)]
        first += [copy(1 + j, me, (*chip, c), src=x_ref) for j, chip in enumerate(chips)]
        for cp in first:
            cp.start()
        passed = [copy(4 + j, (*chip, c), sibling) for j, chip in enumerate(chips)]
        for j, chip in enumerate(chips):
            copy(1 + j, (*chip, c), me).wait_recv()
            passed[j].start()
        copy(0, sibling, me).wait_recv()
        for j, chip in enumerate(chips):
            copy(4 + j, (*chip, 1 - c), me).wait_recv()
        for cp in first + passed:
            cp.wait_send()
        mine.wait()
        tot = gath[0:rows, :]
        for d in range(1, N_DEV):
            tot = tot + gath[d * rows:(d + 1) * rows, :]
        out_ref[...] = tot

    vmem = pl.BlockSpec(memory_space=pltpu.VMEM)
    return pl.pallas_call(
        body, in_specs=[vmem], out_specs=vmem, out_shape=jax.ShapeDtypeStruct((rows, LANES), F32),
        scratch_shapes=[pltpu.VMEM((N_DEV * rows, LANES), F32), pltpu.SemaphoreType.DMA((7,)),
                        pltpu.SemaphoreType.DMA((7,)), pltpu.SemaphoreType.DMA],
        compiler_params=pltpu.CompilerParams(vmem_limit_bytes=VMEM_LIMIT_BYTES),
        name=name)(packed)


def _adamw(w, g, m, v, name):
    rows, cols = w.shape
    tr = 256 if rows % 256 == 0 else rows

    def body(w_ref, g_ref, m_ref, v_ref, d_ref, mo_ref, vo_ref):
        gv = g_ref[...]
        mn = ADAM_B1 * m_ref[...] + (1.0 - ADAM_B1) * gv
        vn = ADAM_B2 * v_ref[...] + (1.0 - ADAM_B2) * (gv * gv)
        m_hat = mn / (1.0 - ADAM_B1 ** ADAM_STEP)
        v_hat = vn / (1.0 - ADAM_B2 ** ADAM_STEP)
        d_ref[...] = -ADAM_LR * (m_hat / (jnp.sqrt(v_hat) + ADAM_EPS) + ADAM_WD * w_ref[...])
        mo_ref[...] = mn
        vo_ref[...] = vn

    spec = pl.BlockSpec((tr, cols), lambda i: (i, 0))
    return pl.pallas_call(
        body, grid=(rows // tr,), in_specs=[spec] * 4, out_specs=[spec] * 3,
        out_shape=[jax.ShapeDtypeStruct((rows, cols), F32)] * 3, compiler_params=_cparams("parallel"),
        name=name)(w, g, m, v)


def _adamw_nd(w, g, m, v, name):
    cols = w.shape[-1] if w.shape[-1] % LANES == 0 else LANES
    outs = _adamw(*(t.reshape(-1, cols) for t in (w, g, m, v)), name)
    return tuple(t.reshape(w.shape) for t in outs)


def _pack(arrays):
    return jnp.concatenate([a.reshape(-1, LANES) for a in arrays], axis=0)


def _unpack(packed, shapes):
    out, row = [], 0
    for sh in shapes:
        n = math.prod(sh) // LANES
        out.append(packed[row:row + n].reshape(sh))
        row += n
    return out


WEIGHTS = ("pre_mix_norm", "w_in", "v_norm_g", "v_norm_b", "w_spatial", "b_spatial", "out_norm_a", "out_norm_b",
           "w_out", "post_mix_norm", "pre_ffn_norm", "w_up", "conv_w", "conv_b", "w_down", "post_ffn_norm")


def kernel(x, pre_mix_norm, w_in, v_norm_g, v_norm_b, w_spatial, b_spatial, out_norm_a, out_norm_b, w_out, post_mix_norm, pre_ffn_norm, w_up, conv_w, conv_b, w_down, post_ffn_norm, loss_target, m_pre_mix_norm, m_w_in, m_v_norm_g, m_v_norm_b, m_w_spatial, m_b_spatial, m_out_norm_a, m_out_norm_b, m_w_out, m_post_mix_norm, m_pre_ffn_norm, m_w_up, m_conv_w, m_conv_b, m_w_down, m_post_ffn_norm, v_pre_mix_norm, v_w_in, v_v_norm_g, v_v_norm_b, v_w_spatial, v_b_spatial, v_out_norm_a, v_out_norm_b, v_w_out, v_post_mix_norm, v_pre_ffn_norm, v_w_up, v_conv_w, v_conv_b, v_w_down, v_post_ffn_norm):
    w = dict(pre_mix_norm=pre_mix_norm, w_in=w_in, v_norm_g=v_norm_g, v_norm_b=v_norm_b, w_spatial=w_spatial,
             b_spatial=b_spatial, out_norm_a=out_norm_a, out_norm_b=out_norm_b, w_out=w_out,
             post_mix_norm=post_mix_norm, pre_ffn_norm=pre_ffn_norm, w_up=w_up, conv_w=conv_w, conv_b=conv_b,
             w_down=w_down, post_ffn_norm=post_ffn_norm)
    m = dict(pre_mix_norm=m_pre_mix_norm, w_in=m_w_in, v_norm_g=m_v_norm_g, v_norm_b=m_v_norm_b,
             w_spatial=m_w_spatial, b_spatial=m_b_spatial, out_norm_a=m_out_norm_a, out_norm_b=m_out_norm_b,
             w_out=m_w_out, post_mix_norm=m_post_mix_norm, pre_ffn_norm=m_pre_ffn_norm, w_up=m_w_up,
             conv_w=m_conv_w, conv_b=m_conv_b, w_down=m_w_down, post_ffn_norm=m_post_ffn_norm)
    v = dict(pre_mix_norm=v_pre_mix_norm, w_in=v_w_in, v_norm_g=v_v_norm_g, v_norm_b=v_v_norm_b,
             w_spatial=v_w_spatial, b_spatial=v_b_spatial, out_norm_a=v_out_norm_a, out_norm_b=v_out_norm_b,
             w_out=v_w_out, post_mix_norm=v_post_mix_norm, pre_ffn_norm=v_pre_ffn_norm, w_up=v_w_up,
             conv_w=v_conv_w, conv_b=v_conv_b, w_down=v_w_down, post_ffn_norm=v_post_ffn_norm)
    pos = jnp.stack([lax.axis_index("x"), lax.axis_index("y"), lax.axis_index("c")]).astype(jnp.int32)
    chip = 2 * lax.axis_index("x") + lax.axis_index("y")

    cw_cols = conv_w.shape[-1]
    cw_slab = lax.dynamic_update_slice(jnp.zeros((DEPTH, 3, 2 * D_FF), F32), conv_w, (0, 0, chip * cw_cols))
    conv_w_full = _allreduce_small(cw_slab.reshape(-1, LANES), "gather_conv_w").reshape(DEPTH, 3, 2 * D_FF)
    conv_w_full = conv_w_full * 0.5
    blocks = [[w[n][l].astype(BF16) for n in BIG] for l in range(DEPTH)]
    wg = dict(zip(BIG, _gather_weights(blocks[0], "gather_weights_l0")))

    small = {n: w[n] for n in SMALL}
    xs, target = x[0], loss_target[0]
    tabs = _rope_tables(xs.shape[0])
    params = [_layer_params(l, small, conv_w_full) for l in range(DEPTH)]
    saved, wgs = [], []
    xin = xs
    h = _rms_cast(xin, params[0]["pre_mix_norm"], "pre_mix_l0")
    for l in range(DEPTH):
        sv, gathered = _layer_forward(l, xin, h, params[l], wg, tabs, blocks[l + 1] if l + 1 < DEPTH else None)
        saved.append(sv)
        wgs.append(wg)
        if l + 1 < DEPTH:
            wg = dict(zip(BIG, gathered))
            xin, h = _residual_norm(sv["x1"], sv["f"], params[l]["post_ffn_norm"], params[l + 1]["pre_mix_norm"],
                                    f"post_ffn_l{l}")
    loss_part, dx = _residual_loss(saved[-1]["x1"], saved[-1]["f"], params[-1]["post_ffn_norm"], target, "loss")
    smalls, shards = [None] * DEPTH, [None] * DEPTH
    pending = None
    for l in reversed(range(DEPTH)):
        dx, big, smalls[l], scattered = _layer_backward(l, dx, saved[l], params[l], wgs[l], tabs,
                                                        pending[1] if pending else None)
        if pending:
            shards[pending[0]] = _gradient_shards(pending[0], pending[1], scattered, pos)
        pending = (l, _chip_sums(l, big, pos))
    shards[pending[0]] = _gradient_shards(pending[0], pending[1],
                                          _chip_scatter(pending[1], f"chip_scatter_l{pending[0]}"), pos)

    small_shapes = [w[n].shape for n in SMALL]
    stacked = [jnp.stack([smalls[l][n].reshape(w[n].shape[1:]) for l in range(DEPTH)]) for n in SMALL]
    cw_grad = jnp.stack([smalls[l]["conv_w"] for l in range(DEPTH)])
    packed = _pack(stacked + [cw_grad, loss_part])
    total = _allreduce_small(packed, "allreduce_small")
    parts = _unpack(total, small_shapes + [cw_grad.shape, (8, LANES)])
    g_small = dict(zip(SMALL, parts[:len(SMALL)]))
    loss = parts[-1][0, 0]
    g_conv_w = lax.dynamic_slice(parts[-2], (0, 0, chip * cw_cols), conv_w.shape)

    grads = {n: jnp.stack([shards[l][n] for l in range(DEPTH)]) for n in BIG}
    grads.update(g_small)
    grads["conv_w"] = g_conv_w

    dp, mp, vp = _adamw(_pack([w[n] for n in SMALL]), _pack([g_small[n] for n in SMALL]),
                        _pack([m[n] for n in SMALL]), _pack([v[n] for n in SMALL]), "adamw_small")
    delta = dict(zip(SMALL, _unpack(dp, small_shapes)))
    new_m = dict(zip(SMALL, _unpack(mp, small_shapes)))
    new_v = dict(zip(SMALL, _unpack(vp, small_shapes)))
    for n in BIG + ("conv_w",):
        delta[n], new_m[n], new_v[n] = _adamw_nd(w[n], grads[n], m[n], v[n], "adamw_" + n)

    return (loss, dx[None], *[grads[n] for n in WEIGHTS], *[delta[n] for n in WEIGHTS],
            *[new_m[n] for n in WEIGHTS], *[new_v[n] for n in WEIGHTS])
```

```python
import functools
import math

import jax
import jax.numpy as jnp
import numpy as np
from jax import lax
from jax.experimental import pallas as pl
from jax.experimental.pallas import tpu as pltpu

F32 = jnp.float32
BF16 = jnp.bfloat16
MESH = pl.DeviceIdType.MESH

D_MODEL = 1024
A_WIDTH = 512
A_GROUPS = 4
GROUP_DIM = 128
CHUNK = 128
B_WIDTH = 512
HEAD_DIM = 64
ROT_DIM = 16
ROPE_THETA = 500000.0
DILATIONS = (1, 4, 16)
BAND = 128
IN_COLS = 2560
D_FF = 4096
EPS = 1e-6
NEG_INF = -1e30
N_CHIPS = 4
LANES = 128

ADAM_LR = 0.001
ADAM_B1 = 0.9
ADAM_B2 = 0.999
ADAM_EPS = 1e-08
ADAM_WD = 0.01
ADAM_STEP = 10

VMEM_LIMIT_BYTES = 56 * 1024 * 1024
RSQRT2 = 0.7071067811865476
INV_SQRT_2PI = 0.3989422804014327
GELU_C = 0.7978845608028654
GELU_A = 0.044715

ANY = pl.BlockSpec(memory_space=pl.ANY)
NN = ((1,), (0,))
NT = ((1,), (1,))
TN = ((0,), (0,))


def _cparams(*sem):
    return pltpu.CompilerParams(dimension_semantics=sem, vmem_limit_bytes=VMEM_LIMIT_BYTES)


def _dot(a, b, dims):
    return lax.dot_general(a, b, (dims, ((), ())), preferred_element_type=F32)


def _rsq_mean(a):
    return lax.rsqrt(jnp.mean(a * a, axis=-1, keepdims=True) + EPS)


def _rms_bwd(a, r, g, dz):
    t = dz * g
    da = r * t - a * (r * r * r) * jnp.mean(t * a, axis=-1, keepdims=True)
    return da, dz * a * r


def _colsum(a):
    return jnp.sum(a, axis=0, keepdims=True)


def _gelu_tanh(x):
    t = jnp.tanh(GELU_C * (x + GELU_A * x * x * x))
    return 0.5 * x * (1.0 + t), t


def _gelu_tanh_grad(x, t):
    return 0.5 * (1.0 + t) + 0.5 * x * (1.0 - t * t) * GELU_C * (1.0 + 3.0 * GELU_A * x * x)


def _matmul(a, b, *, grid, a_spec, b_spec, o_spec, o_shape, o_dtype, dims, nk, kaxis, acc_shape, name, b_2d=None):
    def body(a_ref, b_ref, o_ref, *scratch):
        bv = b_ref[...] if b_2d is None else b_ref[...].reshape(b_2d)
        part = _dot(a_ref[...], bv, dims)
        if nk == 1:
            o_ref[...] = part.astype(o_dtype)
        else:
            acc = scratch[0]
            k = pl.program_id(kaxis)

            @pl.when(k == 0)
            def _():
                acc[...] = part

            @pl.when(k > 0)
            def _():
                acc[...] += part

            @pl.when(k == nk - 1)
            def _():
                o_ref[...] = acc[...].astype(o_dtype)

    sem = tuple("arbitrary" if (nk > 1 and ax == kaxis) else "parallel" for ax in range(len(grid)))
    return pl.pallas_call(
        body, grid=grid, in_specs=[a_spec, b_spec], out_specs=o_spec,
        out_shape=jax.ShapeDtypeStruct(o_shape, o_dtype),
        scratch_shapes=[pltpu.VMEM(acc_shape, F32)] if nk > 1 else [],
        compiler_params=_cparams(*sem), name=name)(a, b)


TM = 512
TMM = 1024


TR = 256


def _row_spec(width, col=0):
    return pl.BlockSpec((TR, width), lambda i, col=col: (i, col))


def _vec_spec(width):
    return pl.BlockSpec((1, width), lambda i: (0, 0))


def _rms_cast(x, g, name):
    s, d = x.shape

    def body(x_ref, g_ref, h_ref):
        a = x_ref[...]
        h_ref[...] = (a * _rsq_mean(a) * g_ref[...]).astype(BF16)

    return pl.pallas_call(
        body, grid=(s // TR,), in_specs=[_row_spec(d), _vec_spec(d)], out_specs=_row_spec(d),
        out_shape=jax.ShapeDtypeStruct((s, d), BF16), compiler_params=_cparams("parallel"), name=name)(x, g)


def _residual_norm(x0, y, g_post, g_next, name):
    s, d = x0.shape

    def body(x_ref, y_ref, gp_ref, gn_ref, x1_ref, h_ref):
        yv = y_ref[...]
        x1 = x_ref[...] + yv * _rsq_mean(yv) * gp_ref[...]
        x1_ref[...] = x1
        h_ref[...] = (x1 * _rsq_mean(x1) * gn_ref[...]).astype(BF16)

    return pl.pallas_call(
        body, grid=(s // TR,), in_specs=[_row_spec(d), _row_spec(d), _vec_spec(d), _vec_spec(d)],
        out_specs=[_row_spec(d), _row_spec(d)],
        out_shape=[jax.ShapeDtypeStruct((s, d), F32), jax.ShapeDtypeStruct((s, d), BF16)],
        compiler_params=_cparams("parallel"), name=name)(x0, y, g_post, g_next)


def _residual_loss(x1, f, g_post, target, name):
    s, d = x1.shape

    def body(x_ref, f_ref, gp_ref, t_ref, loss_ref, dx_ref):
        fv = f_ref[...]
        err = x_ref[...] + fv * _rsq_mean(fv) * gp_ref[...] - t_ref[...]
        dx_ref[...] = err * (1.0 / d)
        part = 0.5 * jnp.sum(jnp.mean(err * err, axis=-1, keepdims=True), axis=0, keepdims=True)

        @pl.when(pl.program_id(0) == 0)
        def _():
            loss_ref[...] = jnp.zeros_like(loss_ref)

        loss_ref[...] += jnp.broadcast_to(part, loss_ref.shape)

    return pl.pallas_call(
        body, grid=(s // TR,), in_specs=[_row_spec(d), _row_spec(d), _vec_spec(d), _row_spec(d)],
        out_specs=[pl.BlockSpec((8, LANES), lambda i: (0, 0)), _row_spec(d)],
        out_shape=[jax.ShapeDtypeStruct((8, LANES), F32), jax.ShapeDtypeStruct((s, d), F32)],
        compiler_params=_cparams("arbitrary"), name=name)(x1, f, g_post, target)


def _acc_init(refs):
    @pl.when(pl.program_id(0) == 0)
    def _():
        for r in refs:
            r[...] = jnp.zeros_like(r)


def _norm_bwd_out(dx, f, g_post, name):
    s, d = dx.shape

    def body(dx_ref, f_ref, g_ref, df_ref, dg_ref):
        _acc_init([dg_ref])
        fv = f_ref[...]
        dz = dx_ref[...]
        da, dgt = _rms_bwd(fv, _rsq_mean(fv), g_ref[...], dz)
        df_ref[...] = da.astype(BF16)
        dg_ref[...] += _colsum(dgt)

    return pl.pallas_call(
        body, grid=(s // TR,), in_specs=[_row_spec(d), _row_spec(d), _vec_spec(d)],
        out_specs=[_row_spec(d), _vec_spec(d)],
        out_shape=[jax.ShapeDtypeStruct((s, d), BF16), jax.ShapeDtypeStruct((1, d), F32)],
        compiler_params=_cparams("arbitrary"), name=name)(dx, f, g_post)


def _norm_bwd_mid(dx2, dh2, x1, g_pf, y1, g_pm, name):
    s, d = dx2.shape

    def body(dx2_ref, dh_ref, x1_ref, gpf_ref, y1_ref, gpm_ref, dx1_ref, dy1_ref, dgpf_ref, dgpm_ref):
        _acc_init([dgpf_ref, dgpm_ref])
        x1 = x1_ref[...]
        da, dgt = _rms_bwd(x1, _rsq_mean(x1), gpf_ref[...], dh_ref[...])
        dx1 = dx2_ref[...] + da
        dx1_ref[...] = dx1
        dgpf_ref[...] += _colsum(dgt)
        y1 = y1_ref[...]
        dy, dgt2 = _rms_bwd(y1, _rsq_mean(y1), gpm_ref[...], dx1)
        dy1_ref[...] = dy.astype(BF16)
        dgpm_ref[...] += _colsum(dgt2)

    return pl.pallas_call(
        body, grid=(s // TR,),
        in_specs=[_row_spec(d), _row_spec(d), _row_spec(d), _vec_spec(d), _row_spec(d), _vec_spec(d)],
        out_specs=[_row_spec(d), _row_spec(d), _vec_spec(d), _vec_spec(d)],
        out_shape=[jax.ShapeDtypeStruct((s, d), F32), jax.ShapeDtypeStruct((s, d), BF16),
                   jax.ShapeDtypeStruct((1, d), F32), jax.ShapeDtypeStruct((1, d), F32)],
        compiler_params=_cparams("arbitrary"), name=name)(dx2, dh2, x1, g_pf, y1, g_pm)


def _norm_bwd_in(dx1, dh1, x0, g1, name):
    s, d = dx1.shape

    def body(dx1_ref, dh_ref, x0_ref, g_ref, dx0_ref, dg_ref):
        _acc_init([dg_ref])
        x0 = x0_ref[...]
        da, dgt = _rms_bwd(x0, _rsq_mean(x0), g_ref[...], dh_ref[...])
        dx0_ref[...] = dx1_ref[...] + da
        dg_ref[...] += _colsum(dgt)

    return pl.pallas_call(
        body, grid=(s // TR,), in_specs=[_row_spec(d), _row_spec(d), _row_spec(d), _vec_spec(d)],
        out_specs=[_row_spec(d), _vec_spec(d)],
        out_shape=[jax.ShapeDtypeStruct((s, d), F32), jax.ShapeDtypeStruct((1, d), F32)],
        compiler_params=_cparams("arbitrary"), name=name)(dx1, dh1, x0, g1)


def _tril_mask():
    row = lax.broadcasted_iota(jnp.int32, (CHUNK, CHUNK), 0)
    col = lax.broadcasted_iota(jnp.int32, (CHUNK, CHUNK), 1)
    return row >= col


def _gating_forward(pa, gv, bv, wt, bsf):
    er = lax.erf(pa * RSQRT2)
    za = 0.5 * pa * (1.0 + er)
    u = za[:, :A_WIDTH]
    va = za[:, A_WIDTH:]
    xc = va - jnp.mean(va, axis=-1, keepdims=True)
    rs = lax.rsqrt(jnp.mean(xc * xc, axis=-1, keepdims=True) + EPS)
    vn = xc * rs
    vlb = (vn * gv + bv).astype(BF16)
    sg = jnp.concatenate(
        [_dot(wt[g], vlb[:, g * GROUP_DIM:(g + 1) * GROUP_DIM], NN) for g in range(A_GROUPS)], axis=1) + bsf
    return er, u, rs, vn, vlb, sg


def _masked_ws(ws_ref):
    mask = _tril_mask()
    return [jnp.where(mask, ws_ref[g], 0.0).astype(BF16) for g in range(A_GROUPS)]


def _mixer_a_fwd(proj, gv, bv, ws, bsf, ga, name):
    s = proj.shape[0]

    def body(p_ref, gv_ref, bv_ref, ws_ref, bs_ref, ga_ref, o_ref):
        wt = _masked_ws(ws_ref)
        for ch in range(TR // CHUNK):
            rows = slice(ch * CHUNK, (ch + 1) * CHUNK)
            _, u, _, _, _, sg = _gating_forward(p_ref[rows, :], gv_ref[...], bv_ref[...], wt, bs_ref[...])
            oa = u * sg
            o_ref[rows, :] = (oa * _rsq_mean(oa) * ga_ref[...]).astype(BF16)

    return pl.pallas_call(
        body, grid=(s // TR,),
        in_specs=[_row_spec(2 * A_WIDTH), _vec_spec(A_WIDTH), _vec_spec(A_WIDTH),
                  pl.BlockSpec((A_GROUPS, CHUNK, CHUNK), lambda i: (0, 0, 0)),
                  pl.BlockSpec((CHUNK, A_WIDTH), lambda i: (0, 0)), _vec_spec(A_WIDTH)],
        out_specs=_row_spec(A_WIDTH), out_shape=jax.ShapeDtypeStruct((s, A_WIDTH + B_WIDTH), BF16),
        compiler_params=_cparams("parallel"), name=name)(proj, gv, bv, ws, bsf, ga)


def _mixer_a_bwd(proj, dmixed, gv, bv, ws, bsf, ga, name):
    s = proj.shape[0]
    nsteps = s // TR

    def body(p_ref, dm_ref, gv_ref, bv_ref, ws_ref, bs_ref, ga_ref,
             dp_ref, dga_ref, dgv_ref, dbv_ref, dbs_ref, dws_ref):
        _acc_init([dga_ref, dgv_ref, dbv_ref, dbs_ref, dws_ref])
        mask = _tril_mask()
        wt = _masked_ws(ws_ref)
        gvv = gv_ref[...]
        gav = ga_ref[...]
        for ch in range(TR // CHUNK):
            rows = slice(ch * CHUNK, (ch + 1) * CHUNK)
            pa = p_ref[rows, :]
            er, u, rs, vn, vlb, sg = _gating_forward(pa, gvv, bv_ref[...], wt, bs_ref[...])
            oa = u * sg
            doa, dgt = _rms_bwd(oa, _rsq_mean(oa), gav, dm_ref[rows, :])
            dga_ref[...] += _colsum(dgt)
            du = doa * sg
            dsg = doa * u
            dbs_ref[...] += dsg
            dsgb = dsg.astype(BF16)
            dvl = []
            for g in range(A_GROUPS):
                cols = slice(g * GROUP_DIM, (g + 1) * GROUP_DIM)
                dws_ref[g] += jnp.where(mask, _dot(dsgb[:, cols], vlb[:, cols], NT), 0.0)
                dvl.append(_dot(wt[g], dsgb[:, cols], TN))
            dvl = jnp.concatenate(dvl, axis=1)
            dgv_ref[...] += _colsum(dvl * vn)
            dbv_ref[...] += _colsum(dvl)
            dvn = dvl * gvv
            dva = rs * (dvn - jnp.mean(dvn, axis=-1, keepdims=True)
                        - vn * jnp.mean(dvn * vn, axis=-1, keepdims=True))
            gp = 0.5 * (1.0 + er) + pa * jnp.exp(-0.5 * pa * pa) * INV_SQRT_2PI
            dp_ref[rows, :] = (jnp.concatenate([du, dva], axis=1) * gp).astype(BF16)

        @pl.when(pl.program_id(0) == nsteps - 1)
        def _():
            for g in range(A_GROUPS):
                cols = slice(g * GROUP_DIM, (g + 1) * GROUP_DIM)
                tot = jnp.sum(dbs_ref[:, cols], axis=1, keepdims=True)
                dbs_ref[:, cols] = jnp.broadcast_to(tot, (CHUNK, GROUP_DIM))

    full = lambda *shape: pl.BlockSpec(shape, lambda i: (0,) * len(shape))
    return pl.pallas_call(
        body, grid=(nsteps,),
        in_specs=[_row_spec(2 * A_WIDTH), _row_spec(A_WIDTH), _vec_spec(A_WIDTH), _vec_spec(A_WIDTH),
                  full(A_GROUPS, CHUNK, CHUNK), full(CHUNK, A_WIDTH), _vec_spec(A_WIDTH)],
        out_specs=[_row_spec(2 * A_WIDTH), _vec_spec(A_WIDTH), _vec_spec(A_WIDTH), _vec_spec(A_WIDTH),
                   full(CHUNK, A_WIDTH), full(A_GROUPS, CHUNK, CHUNK)],
        out_shape=[jax.ShapeDtypeStruct((s, IN_COLS), BF16), jax.ShapeDtypeStruct((1, A_WIDTH), F32),
                   jax.ShapeDtypeStruct((1, A_WIDTH), F32), jax.ShapeDtypeStruct((1, A_WIDTH), F32),
                   jax.ShapeDtypeStruct((CHUNK, A_WIDTH), F32),
                   jax.ShapeDtypeStruct((A_GROUPS, CHUNK, CHUNK), F32)],
        compiler_params=_cparams("arbitrary"), name=name)(proj, dmixed, gv, bv, ws, bsf, ga)


def _rope_tables(s):
    half = ROT_DIM // 2
    inv = ROPE_THETA ** (-jnp.arange(0, ROT_DIM, 2, dtype=F32) / ROT_DIM)
    ang = jnp.arange(s, dtype=F32)[:, None] * inv[None, :]
    cos, sin = jnp.cos(ang), jnp.sin(ang)
    zeros = jnp.zeros((s, HEAD_DIM - ROT_DIM), F32)
    zh = jnp.zeros((s, half), F32)
    c = jnp.concatenate([cos, cos, zeros + 1.0], axis=1)
    s1 = jnp.concatenate([-sin, zh, zeros], axis=1)
    s2 = jnp.concatenate([zh, sin, zeros], axis=1)
    return tuple(jnp.concatenate([t, t], axis=1) for t in (c, s1, s2))


def _lane_blocks(width):
    return [slice(b * LANES, (b + 1) * LANES) for b in range(width // LANES)]


CLASS_DILS = tuple(d for d in DILATIONS if d > 1)


def _class_shape(s, dil, dtype):
    return jax.ShapeDtypeStruct((dil, s // dil, B_WIDTH), dtype)


def _class_spec(dil):
    return pl.BlockSpec((dil, TR // dil, B_WIDTH), lambda i, *_: (0, i, 0))


NBLK = B_WIDTH // LANES
STAGE = pltpu.VMEM((NBLK, TR, LANES), F32)


def _stage_put(stage, value):
    for b, sl in enumerate(_lane_blocks(B_WIDTH)):
        stage[b] = value[:, sl]


def _stage_get(stage):
    return jnp.concatenate([stage[b] for b in range(NBLK)], axis=1)


def _store_classes(stage, dst_ref, dil):
    for b, sl in enumerate(_lane_blocks(B_WIDTH)):
        for r in range(dil):
            dst_ref[r, :, sl] = stage[b, pl.ds(r, TR // dil, stride=dil), :].astype(dst_ref.dtype)


def _load_classes(src_ref, stage, dil):
    for b, sl in enumerate(_lane_blocks(B_WIDTH)):
        for r in range(dil):
            stage[b, pl.ds(r, TR // dil, stride=dil), :] = src_ref[r, :, sl].astype(F32)
    return _stage_get(stage)


def _rope_fwd(proj, tabs, name):
    s = proj.shape[0]
    half = ROT_DIM // 2
    scale = HEAD_DIM ** -0.5
    nlay = 1 + len(CLASS_DILS)

    def body(q_ref, k_ref, v_ref, c_ref, s1_ref, s2_ref, *rest):
        outs, stage = rest[:3 * nlay], rest[3 * nlay]
        c, s1, s2 = c_ref[...], s1_ref[...], s2_ref[...]
        for which, (src, mul) in enumerate(((q_ref, scale), (k_ref, 1.0), (v_ref, None))):
            if mul is None:
                _stage_put(stage, src[...])
            else:
                for b, sl in enumerate(_lane_blocks(B_WIDTH)):
                    a = src[:, sl]
                    r = a * c + pltpu.roll(a, LANES - half, 1) * s1 + pltpu.roll(a, half, 1) * s2
                    stage[b] = r * mul
            dst = outs[which * nlay:(which + 1) * nlay]
            dst[0][...] = _stage_get(stage).astype(BF16)
            for ref, d in zip(dst[1:], CLASS_DILS):
                _store_classes(stage, ref, d)

    tab = pl.BlockSpec((TR, LANES), lambda i: (i, 0))
    lay_specs = [_row_spec(B_WIDTH)] + [_class_spec(d) for d in CLASS_DILS]
    lay_shapes = [jax.ShapeDtypeStruct((s, B_WIDTH), BF16)] + [_class_shape(s, d, BF16) for d in CLASS_DILS]
    outs = pl.pallas_call(
        body, grid=(s // TR,),
        in_specs=[_row_spec(B_WIDTH, 2), _row_spec(B_WIDTH, 3), _row_spec(B_WIDTH, 4), tab, tab, tab],
        out_specs=lay_specs * 3, out_shape=lay_shapes * 3, scratch_shapes=[STAGE],
        compiler_params=_cparams("parallel"), name=name)(proj, proj, proj, *tabs)
    q, k, v = (dict(zip(DILATIONS, outs[w * nlay:(w + 1) * nlay])) for w in range(3))
    return q, k, v


def _as_classes(t):
    return t if t.ndim == 3 else t[None]


def _band_mask(i):
    qi = lax.broadcasted_iota(jnp.int32, (BAND, 2 * BAND), 0)
    kj = lax.broadcasted_iota(jnp.int32, (BAND, 2 * BAND), 1)
    return (kj >= qi) & (kj <= qi + BAND) & ((kj >= BAND) | (i > 0))


def _head_masks():
    lane = lax.broadcasted_iota(jnp.int32, (1, LANES), 1)
    return lane < HEAD_DIM, lane >= HEAD_DIM


def _stack_heads(t):
    lo, hi = _head_masks()
    zero = jnp.zeros_like(t)
    return jnp.concatenate([jnp.where(lo, t, zero), jnp.where(hi, t, zero)], axis=0)


def _attn_specs(last):
    cur = pl.BlockSpec((None, BAND, B_WIDTH), lambda r, i: (r, jnp.minimum(i, last), 0))
    prev = pl.BlockSpec((None, BAND, B_WIDTH), lambda r, i: (r, jnp.maximum(jnp.minimum(i, last) - 1, 0), 0))
    return cur, prev


def _attn_fwd(q, k, v, name):
    dil, n, _ = q.shape
    nb = n // BAND

    def body(q_ref, kc_ref, kp_ref, vc_ref, vp_ref, o_ref, l_ref):
        valid = _band_mask(pl.program_id(1))
        valid = jnp.concatenate([valid, valid], axis=0)
        lo, _ = _head_masks()
        for sl in _lane_blocks(B_WIDTH):
            kk = jnp.concatenate([kp_ref[:, sl], kc_ref[:, sl]], axis=0)
            vv = jnp.concatenate([vp_ref[:, sl], vc_ref[:, sl]], axis=0)
            sc = jnp.where(valid, _dot(_stack_heads(q_ref[:, sl]), kk, NT), NEG_INF)
            mx = jnp.max(sc, axis=1, keepdims=True)
            p = jnp.exp(sc - mx)
            den = jnp.sum(p, axis=1, keepdims=True)
            out = _dot(p.astype(BF16), vv, NN) / den
            lse = mx + jnp.log(den)
            o_ref[:, sl] = jnp.where(lo, out[:BAND], out[BAND:])
            l_ref[:, sl] = jnp.where(lo, lse[:BAND], lse[BAND:])

    cur, prev = _attn_specs(nb - 1)
    return pl.pallas_call(
        body, grid=(dil, nb), in_specs=[cur, cur, prev, cur, prev], out_specs=[cur, cur],
        out_shape=[jax.ShapeDtypeStruct((dil, n, B_WIDTH), F32)] * 2,
        compiler_params=_cparams("parallel", "parallel"), name=name)(q, k, k, v, v)


def _attn_combine(outs, lses, gb, mixed, name):
    s = mixed.shape[0]
    npat = len(DILATIONS)
    w = B_WIDTH

    def body(*refs):
        o_refs, l_refs = refs[:npat], refs[npat:2 * npat]
        g_ref, _, ob_ref = refs[2 * npat:2 * npat + 3]
        lse_refs = refs[2 * npat + 3:3 * npat + 3]
        mb_ref, stage = refs[3 * npat + 3:]
        os_ = [o_refs[0][...]] + [_load_classes(r, stage, d) for r, d in zip(o_refs[1:], CLASS_DILS)]
        ls = [l_refs[0][...]] + [_load_classes(r, stage, d) for r, d in zip(l_refs[1:], CLASS_DILS)]
        mx = functools.reduce(jnp.maximum, ls)
        ws = [jnp.exp(l - mx) for l in ls]
        tot = functools.reduce(lambda a, b: a + b, ws)
        ob = functools.reduce(lambda a, b: a + b, [wt / tot * o for wt, o in zip(ws, os_)])
        ob_ref[...] = ob
        lse = mx + jnp.log(tot)
        _stage_put(stage, lse)
        lse_refs[0][...] = lse
        for ref, d in zip(lse_refs[1:], CLASS_DILS):
            _store_classes(stage, ref, d)
        mb_ref[...] = (ob * _rsq_mean(ob) * g_ref[...]).astype(BF16)

    lay_specs = [_row_spec(w)] + [_class_spec(d) for d in CLASS_DILS]
    res = pl.pallas_call(
        body, grid=(s // TR,), in_specs=lay_specs * 2 + [_vec_spec(w), ANY],
        out_specs=[_row_spec(w)] + lay_specs + [_row_spec(w, 1)],
        out_shape=[jax.ShapeDtypeStruct((s, w), F32), jax.ShapeDtypeStruct((s, w), F32)]
        + [_class_shape(s, d, F32) for d in CLASS_DILS] + [jax.ShapeDtypeStruct(mixed.shape, mixed.dtype)],
        scratch_shapes=[STAGE], input_output_aliases={2 * npat + 1: npat + 1},
        compiler_params=_cparams("parallel"), name=name)(*outs, *lses, gb, mixed)
    return res[0], dict(zip(DILATIONS, res[1:npat + 1])), res[npat + 1]


def _attn_bwd_prep(dmixed, ob, gb, name):
    s = ob.shape[0]
    w = B_WIDTH
    nlay = len(DILATIONS)

    def body(dm_ref, ob_ref, g_ref, *rest):
        do_refs, dl_refs = rest[:nlay], rest[nlay:2 * nlay]
        dg_ref, stage = rest[2 * nlay:]
        _acc_init([dg_ref])
        ob = ob_ref[...]
        dob, dgt = _rms_bwd(ob, _rsq_mean(ob), g_ref[...], dm_ref[...])
        dg_ref[...] += _colsum(dgt)
        _stage_put(stage, dob)
        do_refs[0][...] = dob.astype(BF16)
        for ref, d in zip(do_refs[1:], CLASS_DILS):
            _store_classes(stage, ref, d)
        lo, hi = _head_masks()
        t = dob * ob
        for b, sl in enumerate(_lane_blocks(w)):
            tb = t[:, sl]
            s0 = jnp.sum(jnp.where(lo, tb, 0.0), axis=1, keepdims=True)
            s1 = jnp.sum(jnp.where(hi, tb, 0.0), axis=1, keepdims=True)
            stage[b] = jnp.where(lo, s0, s1)
        dl_refs[0][...] = _stage_get(stage)
        for ref, d in zip(dl_refs[1:], CLASS_DILS):
            _store_classes(stage, ref, d)

    lay_specs = [_row_spec(w)] + [_class_spec(d) for d in CLASS_DILS]
    shapes = lambda dt: [jax.ShapeDtypeStruct((s, w), dt)] + [_class_shape(s, d, dt) for d in CLASS_DILS]
    res = pl.pallas_call(
        body, grid=(s // TR,), in_specs=[_row_spec(w, 1), _row_spec(w), _vec_spec(w)],
        out_specs=lay_specs * 2 + [_vec_spec(w)],
        out_shape=shapes(BF16) + shapes(F32) + [jax.ShapeDtypeStruct((1, w), F32)],
        scratch_shapes=[STAGE],
        compiler_params=_cparams("arbitrary"), name=name)(dmixed, ob, gb)
    return dict(zip(DILATIONS, res[:nlay])), dict(zip(DILATIONS, res[nlay:2 * nlay])), res[2 * nlay]


def _attn_bwd(q, k, v, do, lse, delta, name):
    dil, n, _ = q.shape
    nb = n // BAND

    def body(q_ref, kc_ref, kp_ref, vc_ref, vp_ref, do_ref, lse_ref, dl_ref,
             dq_ref, dk_ref, dv_ref, ck_ref, cv_ref):
        i = pl.program_id(1)

        @pl.when(i == 0)
        def _():
            ck_ref[...] = jnp.zeros_like(ck_ref)
            cv_ref[...] = jnp.zeros_like(cv_ref)

        @pl.when(i < nb)
        def _():
            valid = _band_mask(i)
            valid = jnp.concatenate([valid, valid], axis=0)
            lo, _ = _head_masks()
            lane = lax.broadcasted_iota(jnp.int32, (1, LANES), 1)

            def per_head(t):
                return jnp.concatenate(
                    [jnp.sum(jnp.where(lane == first, t, 0.0), axis=1, keepdims=True) for first in (0, HEAD_DIM)], axis=0)

            for sl in _lane_blocks(B_WIDTH):
                q2 = _stack_heads(q_ref[:, sl])
                do2 = _stack_heads(do_ref[:, sl])
                kk = jnp.concatenate([kp_ref[:, sl], kc_ref[:, sl]], axis=0)
                vv = jnp.concatenate([vp_ref[:, sl], vc_ref[:, sl]], axis=0)
                p = jnp.where(valid, jnp.exp(_dot(q2, kk, NT) - per_head(lse_ref[:, sl])), 0.0)
                ds = (p * (_dot(do2, vv, NT) - per_head(dl_ref[:, sl]))).astype(BF16)
                dq = _dot(ds, kk, NN)
                dkk = _dot(ds, q2, TN)
                dvv = _dot(p.astype(BF16), do2, TN)
                dq_ref[:, sl] = jnp.where(lo, dq[:BAND], dq[BAND:])
                dk_ref[:, sl] = ck_ref[:, sl] + dkk[:BAND]
                dv_ref[:, sl] = cv_ref[:, sl] + dvv[:BAND]
                ck_ref[:, sl] = dkk[BAND:]
                cv_ref[:, sl] = dvv[BAND:]

        @pl.when(i == nb)
        def _():
            dk_ref[...] = ck_ref[...]
            dv_ref[...] = cv_ref[...]

    cur, prev = _attn_specs(nb - 1)
    lag = pl.BlockSpec((None, BAND, B_WIDTH), lambda r, i: (r, jnp.maximum(i - 1, 0), 0))
    shape = jax.ShapeDtypeStruct((dil, n, B_WIDTH), F32)
    return pl.pallas_call(
        body, grid=(dil, nb + 1), in_specs=[cur, cur, prev, cur, prev, cur, cur, cur],
        out_specs=[cur, lag, lag], out_shape=[shape] * 3,
        scratch_shapes=[pltpu.VMEM((BAND, B_WIDTH), F32)] * 2,
        compiler_params=_cparams("arbitrary", "arbitrary"), name=name)(q, k, k, v, v, do, lse, delta)


def _rope_bwd(dqs, dks, dvs, tabs, dproj, name):
    s = dproj.shape[0]
    half = ROT_DIM // 2
    scale = HEAD_DIM ** -0.5
    npat = len(DILATIONS)
    w = B_WIDTH

    def body(*refs):
        groups = [refs[g * npat:(g + 1) * npat] for g in range(3)]
        c_ref, s1_ref, s2_ref, _, o_ref, stage = refs[3 * npat:]

        def total(rs):
            acc = rs[0][...]
            for ref, d in zip(rs[1:], CLASS_DILS):
                acc = acc + _load_classes(ref, stage, d)
            return acc

        def unrope(g):
            c, s1, s2 = c_ref[...], s1_ref[...], s2_ref[...]
            for sl in _lane_blocks(w):
                gb = g[:, sl]
                o = gb * c + pltpu.roll(gb * s1, half, 1) + pltpu.roll(gb * s2, LANES - half, 1)
                o_ref[:, sl] = o.astype(BF16)

        which = pl.program_id(1)

        @pl.when(which == 0)
        def _():
            unrope(total(groups[0]) * scale)

        @pl.when(which == 1)
        def _():
            unrope(total(groups[1]))

        @pl.when(which == 2)
        def _():
            o_ref[...] = total(groups[2]).astype(BF16)

    tab = pl.BlockSpec((TR, LANES), lambda i, j: (i, 0))
    nat = pl.BlockSpec((TR, w), lambda i, j: (i, 0))
    lay_specs = [nat] + [_class_spec(d) for d in CLASS_DILS]
    first_col = 2 * A_WIDTH // w
    return pl.pallas_call(
        body, grid=(s // TR, 3), in_specs=lay_specs * 3 + [tab] * 3 + [ANY],
        out_specs=pl.BlockSpec((TR, w), lambda i, j: (i, first_col + j)),
        out_shape=jax.ShapeDtypeStruct(dproj.shape, dproj.dtype), scratch_shapes=[STAGE],
        input_output_aliases={3 * npat + 3: 0},
        compiler_params=_cparams("parallel", "arbitrary"), name=name)(*dqs, *dks, *dvs, *tabs, dproj)


TK = 512
HALO = 16


def _row_of(v, r):
    rows = lax.broadcasted_iota(jnp.int32, (v.shape[0], 1), 0)
    return jnp.sum(jnp.where(rows == r, v, 0.0), axis=0, keepdims=True)


def _taps_before(x, halo):
    row = lax.broadcasted_iota(jnp.int32, (x.shape[0], 1), 0)
    m1 = jnp.where(row == 0, _row_of(halo, HALO - 1), pltpu.roll(x, 1, 0))
    m2 = jnp.where(row == 0, _row_of(halo, HALO - 2), jnp.where(row == 1, _row_of(halo, HALO - 1), pltpu.roll(x, 2, 0)))
    return m2, m1, x


def _taps_after(x, halo):
    rows = x.shape[0]
    row = lax.broadcasted_iota(jnp.int32, (rows, 1), 0)
    p1 = jnp.where(row == rows - 1, _row_of(halo, 0), pltpu.roll(x, rows - 1, 0))
    p2 = jnp.where(row == rows - 2, _row_of(halo, 0), jnp.where(row == rows - 1, _row_of(halo, 1), pltpu.roll(x, rows - 2, 0)))
    return p1, p2


def _conv_value(taps, cw_ref, cb_ref, h):
    return cb_ref[h] + cw_ref[h, 0:1, :] * taps[0] + cw_ref[h, 1:2, :] * taps[1] + cw_ref[h, 2:3, :] * taps[2]


def _ffn_weight_specs(ncol):
    per_up = (2 * D_FF // N_CHIPS) // TK
    per_dn = (D_FF // N_CHIPS) // TK
    wg = pl.BlockSpec((None, None, D_MODEL, TK), lambda i, j: (j // per_up, 0, 0, j % per_up))
    wv = pl.BlockSpec((None, None, D_MODEL, TK), lambda i, j: ((j + ncol) // per_up, 0, 0, (j + ncol) % per_up))
    wd = pl.BlockSpec((None, None, TK, D_MODEL), lambda i, j: (j // per_dn, 0, j % per_dn, 0))
    cw = pl.BlockSpec((2, 3, TK), lambda i, j: (0, 0, j))
    cb = pl.BlockSpec((2, 1, TK), lambda i, j: (0, 0, j))
    return wg, wv, wd, cw, cb


def _ffn_forward(h2, w_up, w_down, cw3, cb3, name, gather=None):
    s = h2.shape[0]
    nm, ncol = s // TM, D_FF // TK
    ng = 0 if gather is None else len(gather)

    def body(*refs):
        h_ref, wg_ref, wv_ref, wd_ref, cw_ref, cb_ref = refs[:6]
        g_in = refs[6:6 + ng]
        y_ref, up_ref, f_ref = refs[6 + ng:9 + ng]
        g_out = refs[9 + ng:9 + 2 * ng]
        carry, acc = refs[9 + 2 * ng:11 + 2 * ng]
        i, j = pl.program_id(0), pl.program_id(1)
        if ng:
            start, relay, finish = _gather_steps(g_in, g_out, *refs[11 + 2 * ng:])
            pl.when((i == 0) & (j == 0))(start)
            pl.when((i == nm - 1) & (j == 0))(relay)

        @pl.when((i == 0) & (j == 0))
        def _():
            carry[...] = jnp.zeros_like(carry)

        h = h_ref[...]
        conv = []
        for hh, w_ref in ((0, wg_ref), (1, wv_ref)):
            up = _dot(h, w_ref[...], NN).astype(BF16)
            up_ref[hh] = up
            x = up.astype(F32)
            conv.append(_conv_value(_taps_before(x, carry[j, hh]), cw_ref, cb_ref, hh))
            carry[j, hh] = x[TM - HALO:, :]
        y = (_gelu_tanh(conv[0])[0] * conv[1]).astype(BF16)
        y_ref[...] = y
        part = _dot(y, wd_ref[...], NN)

        @pl.when(j == 0)
        def _():
            acc[...] = part

        @pl.when(j > 0)
        def _():
            acc[...] += part

        @pl.when(j == ncol - 1)
        def _():
            f_ref[...] = acc[...]

        if ng:
            pl.when((i == nm - 1) & (j == ncol - 1))(finish)

    wg, wv, wd, cw, cb = _ffn_weight_specs(ncol)
    res = pl.pallas_call(
        body, grid=(nm, ncol),
        in_specs=[pl.BlockSpec((TM, D_MODEL), lambda i, j: (i, 0)), wg, wv, wd, cw, cb] + [ANY] * ng,
        out_specs=[pl.BlockSpec((TM, TK), lambda i, j: (i, j)), pl.BlockSpec((2, TM, TK), lambda i, j: (0, i, j)),
                   pl.BlockSpec((TM, D_MODEL), lambda i, j: (i, 0))] + [ANY] * ng,
        out_shape=[jax.ShapeDtypeStruct((s, D_FF), BF16), jax.ShapeDtypeStruct((2, s, D_FF), BF16),
                   jax.ShapeDtypeStruct((s, D_MODEL), F32)] + _gathered_shapes(gather or []),
        scratch_shapes=[pltpu.VMEM((ncol, 2, HALO, TK), F32), pltpu.VMEM((TM, D_MODEL), F32)]
        + (_gather_sems(ng) if ng else []),
        compiler_params=_cparams("arbitrary", "arbitrary"), name=name)(h2, w_up, w_up, w_down, cw3, cb3,
                                                                      *(gather or []))
    return res[:3], list(res[3:])


def _ffn_backward(df, w_up, w_down, up3, cw3, cb3, name, scatter=None):
    s = df.shape[0]
    nm, ncol = s // TM, D_FF // TK
    ns = 0 if scatter is None else len(scatter)

    def body(*refs):
        df_ref, wg_ref, wv_ref, wd_ref, cw_ref, cb_ref, up_ref, halo_ref = refs[:8]
        s_in = refs[8:8 + ns]
        dup_ref, dh_ref, sums_ref = refs[8 + ns:11 + ns]
        s_out = refs[11 + ns:11 + 2 * ns]
        carry, acc = refs[11 + 2 * ns:13 + 2 * ns]
        i, j = pl.program_id(0), pl.program_id(1)
        if ns:
            start, finish = _scatter_steps(s_in, s_out, *refs[13 + 2 * ns:])
            pl.when((i == 0) & (j == 0))(start)

        @pl.when((i == 0) & (j == 0))
        def _():
            carry[...] = jnp.zeros_like(carry)
            sums_ref[...] = jnp.zeros_like(sums_ref)

        seq_first = i == nm - 1
        dy = _dot(df_ref[...], wd_ref[...], NT)
        conv, taps = [], []
        for hh in range(2):
            halo = jnp.where(seq_first, 0.0, halo_ref[hh].astype(F32))
            tp = _taps_before(up_ref[hh].astype(F32), halo)
            conv.append(_conv_value(tp, cw_ref, cb_ref, hh))
            taps.append(tp)
        act, t = _gelu_tanh(conv[0])
        dcs = (dy * conv[1] * _gelu_tanh_grad(conv[0], t), dy * act)
        row = lax.broadcasted_iota(jnp.int32, (8, 1), 0)
        part = None
        for hh, w_ref in ((0, wg_ref), (1, wv_ref)):
            dc, tp = dcs[hh], taps[hh]
            upd = jnp.zeros((8, TK), F32)
            for ridx, sm in enumerate((_colsum(dc * tp[0]), _colsum(dc * tp[1]), _colsum(dc * tp[2]), _colsum(dc))):
                upd = jnp.where(row == ridx, sm, upd)
            sums_ref[j, hh] += upd
            after1, after2 = _taps_after(dc, carry[j, hh])
            dup = (cw_ref[hh, 2:3, :] * dc + cw_ref[hh, 1:2, :] * after1 + cw_ref[hh, 0:1, :] * after2).astype(BF16)
            carry[j, hh] = dc[:HALO, :]
            dup_ref[hh] = dup
            d = _dot(dup, w_ref[...], NT)
            part = d if part is None else part + d

        @pl.when(j == 0)
        def _():
            acc[...] = part

        @pl.when(j > 0)
        def _():
            acc[...] += part

        @pl.when(j == ncol - 1)
        def _():
            dh_ref[...] = acc[...]

        if ns:
            pl.when((i == nm - 1) & (j == ncol - 1))(finish)

    wg, wv, wd, cw, cb = _ffn_weight_specs(ncol)
    rev = lambda i: nm - 1 - i
    res = pl.pallas_call(
        body, grid=(nm, ncol),
        in_specs=[pl.BlockSpec((TM, D_MODEL), lambda i, j: (rev(i), 0)), wg, wv, wd, cw, cb,
                  pl.BlockSpec((2, TM, TK), lambda i, j: (0, rev(i), j)),
                  pl.BlockSpec((2, HALO, TK), lambda i, j: (0, jnp.maximum(rev(i) * (TM // HALO) - 1, 0), j))]
        + [ANY] * ns,
        out_specs=[pl.BlockSpec((2, TM, TK), lambda i, j: (0, rev(i), j)),
                   pl.BlockSpec((TM, D_MODEL), lambda i, j: (rev(i), 0)),
                   pl.BlockSpec((ncol, 2, 8, TK), lambda i, j: (0, 0, 0, 0))] + [ANY] * ns,
        out_shape=[jax.ShapeDtypeStruct((2, s, D_FF), BF16), jax.ShapeDtypeStruct((s, D_MODEL), F32),
                   jax.ShapeDtypeStruct((ncol, 2, 8, TK), F32)] + (_scattered_shapes() if ns else []),
        scratch_shapes=[pltpu.VMEM((ncol, 2, HALO, TK), F32), pltpu.VMEM((TM, D_MODEL), F32)]
        + (_scatter_sems() if ns else []),
        compiler_params=_cparams("arbitrary", "arbitrary"), name=name)(df, w_up, w_up, w_down, cw3, cb3, up3, up3,
                                                                      *(scatter or []))
    return res[:3], list(res[3:])


def _wspec(rows, cols, index_map):
    return pl.BlockSpec((None, None, rows, cols), index_map)


def _layer_forward(l, x0, h1, p, wg, tabs, gather=None):
    s = x0.shape[0]
    nm = s // TMM
    tag = f"_l{l}"
    proj = _matmul(
        h1, wg["w_in"], grid=(nm, N_CHIPS), a_spec=pl.BlockSpec((TMM, D_MODEL), lambda i, j: (i, 0)),
        b_spec=_wspec(D_MODEL, IN_COLS // N_CHIPS, lambda i, j: (j, 0, 0, 0)),
        o_spec=pl.BlockSpec((TMM, IN_COLS // N_CHIPS), lambda i, j: (i, j)), o_shape=(s, IN_COLS), o_dtype=F32,
        dims=NN, nk=1, kaxis=None, acc_shape=None, name="proj" + tag)
    ma = _mixer_a_fwd(proj, p["v_norm_g"], p["v_norm_b"], p["w_spatial"], p["bs_full"], p["out_norm_a"],
                      "mixer_a_fwd" + tag)
    q, k, v = _rope_fwd(proj, tabs, "rope_fwd" + tag)
    outs, lses = zip(*[_attn_fwd(_as_classes(q[d]), _as_classes(k[d]), _as_classes(v[d]), f"attn_fwd_d{d}" + tag)
                       for d in DILATIONS])
    outs = [o.reshape(s, B_WIDTH) if d == 1 else o for o, d in zip(outs, DILATIONS)]
    lses = [t.reshape(s, B_WIDTH) if d == 1 else t for t, d in zip(lses, DILATIONS)]
    ob, lse, mixed = _attn_combine(outs, lses, p["out_norm_b"], ma, "attn_combine" + tag)
    w_out_all = pl.BlockSpec((N_CHIPS, None, D_MODEL // N_CHIPS, D_MODEL), lambda i: (0, 0, 0, 0))
    y1 = _matmul(
        mixed, wg["w_out"], grid=(nm,), a_spec=pl.BlockSpec((TMM, D_MODEL), lambda i: (i, 0)), b_spec=w_out_all,
        o_spec=pl.BlockSpec((TMM, D_MODEL), lambda i: (i, 0)), o_shape=(s, D_MODEL), o_dtype=F32,
        dims=NN, nk=1, kaxis=None, acc_shape=None, name="mix_out" + tag, b_2d=(D_MODEL, D_MODEL))
    x1, h2 = _residual_norm(x0, y1, p["post_mix_norm"], p["pre_ffn_norm"], "post_mix" + tag)
    (y, up3, f), gathered = _ffn_forward(h2, wg["w_up"], wg["w_down"], p["cw3"], p["cb3"], "ffn_fwd" + tag, gather)
    saved = dict(x0=x0, h1=h1, proj=proj, q=q, k=k, v=v, ob=ob, lse=lse, mixed=mixed, y1=y1, x1=x1, h2=h2,
                 up3=up3, y=y, f=f)
    return saved, gathered


def _layer_backward(l, dx2, sv, p, wg, tabs, scatter=None):
    s = dx2.shape[0]
    nm = s // TMM
    tag = f"_l{l}"
    g = {}
    df, g["post_ffn_norm"] = _norm_bwd_out(dx2, sv["f"], p["post_ffn_norm"], "norm_bwd_out" + tag)
    (dup3, dh2, conv_sums), scattered = _ffn_backward(df, wg["w_up"], wg["w_down"], sv["up3"], p["cw3"], p["cb3"],
                                                      "ffn_bwd" + tag, scatter)
    sums = conv_sums.transpose(1, 2, 0, 3).reshape(2, 8, D_FF)
    g["conv_w"] = jnp.concatenate([sums[0, :3], sums[1, :3]], axis=1)
    g["conv_b"] = jnp.concatenate([sums[0, 3:4], sums[1, 3:4]], axis=1)
    tn = 1024
    gw_up = _matmul(
        sv["h2"], dup3, grid=(2 * D_FF // tn, nm), a_spec=pl.BlockSpec((TMM, D_MODEL), lambda n, m: (m, 0)),
        b_spec=pl.BlockSpec((None, TMM, tn), lambda n, m: (n // (D_FF // tn), m, n % (D_FF // tn))),
        o_spec=pl.BlockSpec((None, D_MODEL, tn), lambda n, m: (n // 2, 0, n % 2)),
        o_shape=(N_CHIPS, D_MODEL, 2 * D_FF // N_CHIPS), o_dtype=BF16,
        dims=TN, nk=nm, kaxis=1, acc_shape=(D_MODEL, tn), name="w_up_grad" + tag)
    gw_down = _matmul(
        sv["y"], df, grid=(D_FF // tn, 2, nm), a_spec=pl.BlockSpec((TMM, tn), lambda k, h, m: (m, k)),
        b_spec=pl.BlockSpec((TMM, D_MODEL // 2), lambda k, h, m: (m, h)),
        o_spec=pl.BlockSpec((None, tn, D_MODEL // 2), lambda k, h, m: (h, k, 0)),
        o_shape=(2, D_FF, D_MODEL // 2), o_dtype=BF16,
        dims=TN, nk=nm, kaxis=2, acc_shape=(tn, D_MODEL // 2), name="w_down_grad" + tag)
    dx1, dy1, g["pre_ffn_norm"], g["post_mix_norm"] = _norm_bwd_mid(
        dx2, dh2, sv["x1"], p["pre_ffn_norm"], sv["y1"], p["post_mix_norm"], "norm_bwd_mid" + tag)
    w_out_all = pl.BlockSpec((N_CHIPS, None, D_MODEL // N_CHIPS, D_MODEL), lambda i: (0, 0, 0, 0))
    dmixed = _matmul(
        dy1, wg["w_out"], grid=(nm,), a_spec=pl.BlockSpec((TMM, D_MODEL), lambda i: (i, 0)), b_spec=w_out_all,
        o_spec=pl.BlockSpec((TMM, D_MODEL), lambda i: (i, 0)), o_shape=(s, D_MODEL), o_dtype=F32,
        dims=NT, nk=1, kaxis=None, acc_shape=None, name="mix_out_bwd" + tag, b_2d=(D_MODEL, D_MODEL))
    gw_out = _matmul(
        sv["mixed"], dy1, grid=(2, nm), a_spec=pl.BlockSpec((TMM, D_MODEL), lambda h, m: (m, 0)),
        b_spec=pl.BlockSpec((TMM, D_MODEL // 2), lambda h, m: (m, h)),
        o_spec=pl.BlockSpec((None, D_MODEL, D_MODEL // 2), lambda h, m: (h, 0, 0)),
        o_shape=(2, D_MODEL, D_MODEL // 2), o_dtype=BF16,
        dims=TN, nk=nm, kaxis=1, acc_shape=(D_MODEL, D_MODEL // 2), name="w_out_grad" + tag)
    dpa, g["out_norm_a"], g["v_norm_g"], g["v_norm_b"], dbs, g["w_spatial"] = _mixer_a_bwd(
        sv["proj"], dmixed, p["v_norm_g"], p["v_norm_b"], p["w_spatial"], p["bs_full"], p["out_norm_a"],
        "mixer_a_bwd" + tag)
    g["b_spatial"] = dbs[:, ::GROUP_DIM].T
    dob, delta, g["out_norm_b"] = _attn_bwd_prep(dmixed, sv["ob"], p["out_norm_b"], "attn_bwd_prep" + tag)
    dqs, dks, dvs = zip(*[
        _attn_bwd(*(_as_classes(t[d]) for t in (sv["q"], sv["k"], sv["v"], dob, sv["lse"], delta)),
                  f"attn_bwd_d{d}" + tag) for d in DILATIONS])
    nat = lambda ts: [t.reshape(s, B_WIDTH) if d == 1 else t for t, d in zip(ts, DILATIONS)]
    dproj = _rope_bwd(nat(dqs), nat(dks), nat(dvs), tabs, dpa, "rope_bwd" + tag)
    wcol = IN_COLS // N_CHIPS
    dh1 = _matmul(
        dproj, wg["w_in"], grid=(nm, N_CHIPS), a_spec=pl.BlockSpec((TMM, wcol), lambda i, n: (i, n)),
        b_spec=_wspec(D_MODEL, wcol, lambda i, n: (n, 0, 0, 0)),
        o_spec=pl.BlockSpec((TMM, D_MODEL), lambda i, n: (i, 0)), o_shape=(s, D_MODEL), o_dtype=F32,
        dims=NT, nk=N_CHIPS, kaxis=1, acc_shape=(TMM, D_MODEL), name="proj_bwd" + tag)
    gw_in = _matmul(
        sv["h1"], dproj, grid=(N_CHIPS, nm), a_spec=pl.BlockSpec((TMM, D_MODEL), lambda n, m: (m, 0)),
        b_spec=pl.BlockSpec((TMM, wcol), lambda n, m: (m, n)),
        o_spec=pl.BlockSpec((None, D_MODEL, wcol), lambda n, m: (n, 0, 0)),
        o_shape=(N_CHIPS, D_MODEL, wcol), o_dtype=BF16,
        dims=TN, nk=nm, kaxis=1, acc_shape=(D_MODEL, wcol), name="w_in_grad" + tag)
    dx0, g["pre_mix_norm"] = _norm_bwd_in(dx1, dh1, sv["x0"], p["pre_mix_norm"], "norm_bwd_in" + tag)
    big = dict(w_in=gw_in, w_up=gw_up, w_out=gw_out, w_down=gw_down)
    return dx0, big, g, scattered


SMALL = ("pre_mix_norm", "v_norm_g", "v_norm_b", "w_spatial", "b_spatial", "out_norm_a", "out_norm_b",
         "post_mix_norm", "pre_ffn_norm", "conv_b", "post_ffn_norm")
BIG = ("w_in", "w_out", "w_up", "w_down")
DEPTH = 2


def _layer_params(l, small, conv_w_full):
    p = {n: small[n][l].reshape(1, -1) for n in SMALL if n not in ("w_spatial", "b_spatial")}
    p["w_spatial"] = small["w_spatial"][l]
    p["bs_full"] = jnp.repeat(small["b_spatial"][l].T, GROUP_DIM, axis=1)
    p["cw3"] = conv_w_full[l].reshape(3, 2, D_FF).transpose(1, 0, 2)
    p["cb3"] = small["conv_b"][l].reshape(2, 1, D_FF)
    return p


def _mesh_pos():
    return lax.axis_index("x"), lax.axis_index("y"), lax.axis_index("c")


def _other_chips(x, y):
    return [(1 - x, y), (x, 1 - y), (1 - x, 1 - y)]


def _gathered_shapes(blocks):
    return [jax.ShapeDtypeStruct((N_CHIPS, 1) + a.shape, a.dtype) for a in blocks]


def _gather_sems(nw):
    n = 2 * nw * (N_CHIPS - 1) + nw
    return [pltpu.SemaphoreType.DMA((n,)), pltpu.SemaphoreType.DMA((n,))]


def _gather_steps(ins, outs, send, recv):
    nw, nrel = len(ins), N_CHIPS - 1
    x, y, c = _mesh_pos()
    mine, sibling, chips = 2 * x + y, (x, y, 1 - c), _other_chips(x, y)

    def copy(src, dst, slot, to):
        return pltpu.make_async_remote_copy(src_ref=src, dst_ref=dst, send_sem=send.at[slot],
                                            recv_sem=recv.at[slot], device_id=to, device_id_type=MESH)

    def half_rows(t, core):
        rows = ins[t].shape[0] // 2
        return pl.ds(pl.multiple_of(core * rows, rows), rows)

    def landing(t, chip, core):
        return outs[t].at[chip, 0, half_rows(t, core), :]

    slots = [(t, r, chip) for t in range(nw) for r, chip in enumerate(chips)]
    own = [copy(ins[t], outs[t].at[mine, 0], 2 * nw * nrel + t, sibling) for t in range(nw)]
    first = [copy(ins[t].at[half_rows(t, c), :], landing(t, mine, c), t * nrel + r, (px, py, c))
             for t, r, (px, py) in slots]
    relays = [copy(landing(t, 2 * px + py, c), landing(t, 2 * px + py, c), nw * nrel + t * nrel + r, sibling)
              for t, r, (px, py) in slots]

    def start():
        for cp in own + first:
            cp.start()

    def relay():
        for (t, r, (px, py)), cp in zip(slots, relays):
            copy(landing(t, 2 * px + py, c), landing(t, 2 * px + py, c), t * nrel + r, (px, py, c)).wait_recv()
            cp.start()

    def finish():
        for t, r, (px, py) in slots:
            passed = landing(t, 2 * px + py, 1 - c)
            copy(passed, passed, nw * nrel + t * nrel + r, sibling).wait_recv()
        for cp in first + relays:
            cp.wait_send()
        for cp in own:
            cp.wait()

    return start, relay, finish


def _gather_weights(blocks, name):
    nw = len(blocks)

    def body(*refs):
        start, relay, finish = _gather_steps(refs[:nw], refs[nw:2 * nw], *refs[2 * nw:])
        start()
        relay()
        finish()

    return pl.pallas_call(
        body, in_specs=[ANY] * nw, out_specs=[ANY] * nw, out_shape=_gathered_shapes(blocks),
        scratch_shapes=_gather_sems(nw), name=name)(*blocks)


HALF_ROWS = D_MODEL // 2


def _pair_exchange(g, name):
    shapes = [(N_CHIPS, HALF_ROWS, IN_COLS // N_CHIPS), (N_CHIPS, HALF_ROWS, 2 * D_FF // N_CHIPS),
              (D_MODEL, D_MODEL // 2), (D_FF, D_MODEL // 2)]

    def body(gin, gup, gout, gdn, rin, rup, rout, rdn, send, recv):
        x, y, c = _mesh_pos()
        o = 1 - c
        rows = pl.ds(pl.multiple_of(o * HALF_ROWS, HALF_ROWS), HALF_ROWS)
        pairs = [(gin.at[:, rows, :], rin), (gup.at[:, rows, :], rup), (gout.at[o], rout), (gdn.at[o], rdn)]
        cps = [pltpu.make_async_remote_copy(src_ref=src, dst_ref=dst, send_sem=send.at[t], recv_sem=recv.at[t],
                                            device_id=(x, y, o), device_id_type=MESH)
               for t, (src, dst) in enumerate(pairs)]
        for cp in cps:
            cp.start()
        for cp in cps:
            cp.wait()

    return pl.pallas_call(
        body, in_specs=[ANY] * 4, out_specs=[ANY] * 4,
        out_shape=[jax.ShapeDtypeStruct(sh, BF16) for sh in shapes],
        scratch_shapes=[pltpu.SemaphoreType.DMA((4,)), pltpu.SemaphoreType.DMA((4,))],
        name=name)(g["w_in"], g["w_up"], g["w_out"], g["w_down"])


def _pair_sum(g, recv, pos, name_prefix):
    def add(a, b, grid, a_spec, b_spec, shape, name):
        def body(pos_ref, a_ref, b_ref, o_ref):
            o_ref[...] = (a_ref[...].astype(F32) + b_ref[...].astype(F32)).astype(BF16)

        return pl.pallas_call(
            body, grid_spec=pltpu.PrefetchScalarGridSpec(
                num_scalar_prefetch=1, grid=grid, in_specs=[a_spec, b_spec], out_specs=b_spec),
            out_shape=jax.ShapeDtypeStruct(shape, BF16), compiler_params=_cparams(*["parallel"] * len(grid)),
            name=name)(pos, a, b)

    rin, rup, rout, rdn = recv
    wi, wu = IN_COLS // N_CHIPS, 2 * D_FF // N_CHIPS
    s_in = add(g["w_in"], rin, (N_CHIPS,), pl.BlockSpec((None, HALF_ROWS, wi), lambda j, pos: (j, pos[2], 0)),
               pl.BlockSpec((None, HALF_ROWS, wi), lambda j, pos: (j, 0, 0)), rin.shape, name_prefix + "_in")
    s_up = add(g["w_up"], rup, (N_CHIPS,), pl.BlockSpec((None, HALF_ROWS, wu), lambda j, pos: (j, pos[2], 0)),
               pl.BlockSpec((None, HALF_ROWS, wu), lambda j, pos: (j, 0, 0)), rup.shape, name_prefix + "_up")
    hc = D_MODEL // 2
    s_out = add(g["w_out"], rout, (1,), pl.BlockSpec((None, D_MODEL, hc), lambda j, pos: (pos[2], 0, 0)),
                pl.BlockSpec((D_MODEL, hc), lambda j, pos: (0, 0)), rout.shape, name_prefix + "_out")
    s_dn = add(g["w_down"], rdn, (N_CHIPS,), pl.BlockSpec((None, D_FF // N_CHIPS, hc), lambda j, pos: (pos[2], j, 0)),
               pl.BlockSpec((D_FF // N_CHIPS, hc), lambda j, pos: (j, 0)), rdn.shape, name_prefix + "_down")
    return s_in, s_up, s_out, s_dn


OUT_ROWS = D_MODEL // N_CHIPS
DOWN_ROWS = D_FF // N_CHIPS


def _scattered_shapes():
    nrel = N_CHIPS - 1
    shapes = [(nrel, HALF_ROWS, IN_COLS // N_CHIPS), (nrel, HALF_ROWS, 2 * D_FF // N_CHIPS),
              (nrel, OUT_ROWS, D_MODEL // 2), (nrel, DOWN_ROWS, D_MODEL // 2)]
    return [jax.ShapeDtypeStruct(sh, BF16) for sh in shapes]


def _scatter_sems():
    n = 4 * (N_CHIPS - 1)
    return [pltpu.SemaphoreType.DMA((n,)), pltpu.SemaphoreType.DMA((n,))]


def _scatter_steps(sums, outs, send, recv):
    nrel = N_CHIPS - 1
    sin, sup, sout, sdn = sums
    x, y, c = _mesh_pos()
    cps = []
    for r, (px, py) in enumerate(_other_chips(x, y)):
        j = 2 * px + py
        pieces = [sin.at[j], sup.at[j], sout.at[pl.ds(pl.multiple_of(j * OUT_ROWS, OUT_ROWS), OUT_ROWS), :],
                  sdn.at[pl.ds(pl.multiple_of(j * DOWN_ROWS, DOWN_ROWS), DOWN_ROWS), :]]
        for t, src in enumerate(pieces):
            cps.append(pltpu.make_async_remote_copy(
                src_ref=src, dst_ref=outs[t].at[r], send_sem=send.at[t * nrel + r], recv_sem=recv.at[t * nrel + r],
                device_id=(px, py, c), device_id_type=MESH))

    def start():
        for cp in cps:
            cp.start()

    def finish():
        for cp in cps:
            cp.wait()

    return start, finish


def _chip_scatter(sums, name):
    def body(*refs):
        start, finish = _scatter_steps(refs[:4], refs[4:8], *refs[8:])
        start()
        finish()

    return pl.pallas_call(
        body, in_specs=[ANY] * 4, out_specs=[ANY] * 4, out_shape=_scattered_shapes(),
        scratch_shapes=_scatter_sems(), name=name)(*sums)


def _chip_sum(sums, recv, pos, name_prefix):
    def add(a, b, a_spec, shape, name):
        def body(pos_ref, a_ref, b_ref, o_ref):
            tot = a_ref[...].astype(F32)
            for r in range(N_CHIPS - 1):
                tot = tot + b_ref[r].astype(F32)
            o_ref[...] = tot

        return pl.pallas_call(
            body, grid_spec=pltpu.PrefetchScalarGridSpec(
                num_scalar_prefetch=1, grid=(1,), in_specs=[a_spec, pl.BlockSpec(b.shape, lambda i, pos: (0, 0, 0))],
                out_specs=pl.BlockSpec((None,) + shape, lambda i, pos: (pos[2], 0, 0))),
            out_shape=jax.ShapeDtypeStruct((2,) + shape, F32), compiler_params=_cparams("arbitrary"),
            name=name)(pos, a, b)

    s_in, s_up, s_out, s_dn = sums
    rin, rup, rout, rdn = recv
    wi, wu, hc = IN_COLS // N_CHIPS, 2 * D_FF // N_CHIPS, D_MODEL // 2
    chip = lambda pos: 2 * pos[0] + pos[1]
    t_in = add(s_in, rin, pl.BlockSpec((None, HALF_ROWS, wi), lambda i, pos: (chip(pos), 0, 0)), (HALF_ROWS, wi),
               name_prefix + "_in")
    t_up = add(s_up, rup, pl.BlockSpec((None, HALF_ROWS, wu), lambda i, pos: (chip(pos), 0, 0)), (HALF_ROWS, wu),
               name_prefix + "_up")
    t_out = add(s_out, rout, pl.BlockSpec((OUT_ROWS, hc), lambda i, pos: (chip(pos), 0)), (OUT_ROWS, hc),
                name_prefix + "_out")
    t_dn = add(s_dn, rdn, pl.BlockSpec((DOWN_ROWS, hc), lambda i, pos: (chip(pos), 0)), (DOWN_ROWS, hc),
               name_prefix + "_down")
    return t_in, t_up, t_out, t_dn


def _pair_share(totals, name):
    n = len(totals)

    def body(*refs):
        ins, outs = refs[:n], refs[n:2 * n]
        send, recv = refs[2 * n:]
        x, y, c = _mesh_pos()
        o = 1 - c
        cps = [pltpu.make_async_remote_copy(src_ref=ins[t].at[c], dst_ref=outs[t].at[c], send_sem=send.at[t],
                                            recv_sem=recv.at[t], device_id=(x, y, o), device_id_type=MESH)
               for t in range(n)]
        for cp in cps:
            cp.start()
        for t in range(n):
            pltpu.make_async_remote_copy(src_ref=ins[t].at[o], dst_ref=outs[t].at[o], send_sem=send.at[t],
                                         recv_sem=recv.at[t], device_id=(x, y, o), device_id_type=MESH).wait_recv()
        for cp in cps:
            cp.wait_send()

    return pl.pallas_call(
        body, in_specs=[ANY] * n, out_specs=[ANY] * n,
        out_shape=[jax.ShapeDtypeStruct(t.shape, t.dtype) for t in totals],
        scratch_shapes=[pltpu.SemaphoreType.DMA((n,)), pltpu.SemaphoreType.DMA((n,))],
        input_output_aliases={t: t for t in range(n)}, name=name)(*totals)


def _chip_sums(l, g, pos):
    recv = _pair_exchange(g, f"pair_exchange_l{l}")
    return _pair_sum(g, recv, pos, f"pair_sum_l{l}")


def _gradient_shards(l, sums, scattered, pos):
    totals = _chip_sum(sums, scattered, pos, f"chip_sum_l{l}")
    f_in, f_up, f_out, f_dn = _pair_share(totals, f"pair_share_l{l}")
    f_in = f_in.reshape(D_MODEL, IN_COLS // N_CHIPS)
    f_up = f_up.reshape(D_MODEL, 2 * D_FF // N_CHIPS)
    f_out = f_out.transpose(1, 0, 2).reshape(OUT_ROWS, D_MODEL)
    f_dn = f_dn.transpose(1, 0, 2).reshape(DOWN_ROWS, D_MODEL)
    return dict(w_in=f_in, w_up=f_up, w_out=f_out, w_down=f_dn)


N_DEV = 8


def _allreduce_small(packed, name):
    rows = packed.shape[0]

    def body(x_ref, out_ref, gath, send_sems, recv_sems, local_sem):
        x, y, c = _mesh_pos()
        me, sibling = (x, y, c), (x, y, 1 - c)
        chips = _other_chips(x, y)

        def blk(px, py, pc):
            return gath.at[pl.ds(pl.multiple_of((4 * px + 2 * py + pc) * rows, 8), rows), :]

        def copy(k, block, to, src=None):
            return pltpu.make_async_remote_copy(
                src_ref=blk(*block) if src is None else src, dst_ref=blk(*block), send_sem=send_sems.at[k],
                recv_sem=recv_sems.at[k], device_id=to, device_id_type=MESH)

        mine = pltpu.make_async_copy(x_ref, blk(*me), local_sem)
        mine.start()
        first = [copy(0, me, sibling, src=x_ref)]
        first += [copy(1 + j, me, (*chip, c), src=x_ref) for j, chip in enumerate(chips)]
        for cp in first:
            cp.start()
        passed = [copy(4 + j, (*chip, c), sibling) for j, chip in enumerate(chips)]
        for j, chip in enumerate(chips):
            copy(1 + j, (*chip, c), me).wait_recv()
            passed[j].start()
        copy(0, sibling, me).wait_recv()
        for j, chip in enumerate(chips):
            copy(4 + j, (*chip, 1 - c), me).wait_recv()
        for cp in first + passed:
            cp.wait_send()
        mine.wait()
        tot = gath[0:rows, :]
        for d in range(1, N_DEV):
            tot = tot + gath[d * rows:(d + 1) * rows, :]
        out_ref[...] = tot

    vmem = pl.BlockSpec(memory_space=pltpu.VMEM)
    return pl.pallas_call(
        body, in_specs=[vmem], out_specs=vmem, out_shape=jax.ShapeDtypeStruct((rows, LANES), F32),
        scratch_shapes=[pltpu.VMEM((N_DEV * rows, LANES), F32), pltpu.SemaphoreType.DMA((7,)),
                        pltpu.SemaphoreType.DMA((7,)), pltpu.SemaphoreType.DMA],
        compiler_params=pltpu.CompilerParams(vmem_limit_bytes=VMEM_LIMIT_BYTES),
        name=name)(packed)


def _adamw(w, g, m, v, name):
    rows, cols = w.shape
    tr = 256 if rows % 256 == 0 else rows

    def body(w_ref, g_ref, m_ref, v_ref, d_ref, mo_ref, vo_ref):
        gv = g_ref[...]
        mn = ADAM_B1 * m_ref[...] + (1.0 - ADAM_B1) * gv
        vn = ADAM_B2 * v_ref[...] + (1.0 - ADAM_B2) * (gv * gv)
        m_hat = mn / (1.0 - ADAM_B1 ** ADAM_STEP)
        v_hat = vn / (1.0 - ADAM_B2 ** ADAM_STEP)
        d_ref[...] = -ADAM_LR * (m_hat / (jnp.sqrt(v_hat) + ADAM_EPS) + ADAM_WD * w_ref[...])
        mo_ref[...] = mn
        vo_ref[...] = vn

    spec = pl.BlockSpec((tr, cols), lambda i: (i, 0))
    return pl.pallas_call(
        body, grid=(rows // tr,), in_specs=[spec] * 4, out_specs=[spec] * 3,
        out_shape=[jax.ShapeDtypeStruct((rows, cols), F32)] * 3, compiler_params=_cparams("parallel"),
        name=name)(w, g, m, v)


def _adamw_nd(w, g, m, v, name):
    cols = w.shape[-1] if w.shape[-1] % LANES == 0 else LANES
    outs = _adamw(*(t.reshape(-1, cols) for t in (w, g, m, v)), name)
    return tuple(t.reshape(w.shape) for t in outs)


def _pack(arrays):
    return jnp.concatenate([a.reshape(-1, LANES) for a in arrays], axis=0)


def _unpack(packed, shapes):
    out, row = [], 0
    for sh in shapes:
        n = math.prod(sh) // LANES
        out.append(packed[row:row + n].reshape(sh))
        row += n
    return out


WEIGHTS = ("pre_mix_norm", "w_in", "v_norm_g", "v_norm_b", "w_spatial", "b_spatial", "out_norm_a", "out_norm_b",
           "w_out", "post_mix_norm", "pre_ffn_norm", "w_up", "conv_w", "conv_b", "w_down", "post_ffn_norm")


def kernel(x, pre_mix_norm, w_in, v_norm_g, v_norm_b, w_spatial, b_spatial, out_norm_a, out_norm_b, w_out, post_mix_norm, pre_ffn_norm, w_up, conv_w, conv_b, w_down, post_ffn_norm, loss_target, m_pre_mix_norm, m_w_in, m_v_norm_g, m_v_norm_b, m_w_spatial, m_b_spatial, m_out_norm_a, m_out_norm_b, m_w_out, m_post_mix_norm, m_pre_ffn_norm, m_w_up, m_conv_w, m_conv_b, m_w_down, m_post_ffn_norm, v_pre_mix_norm, v_w_in, v_v_norm_g, v_v_norm_b, v_w_spatial, v_b_spatial, v_out_norm_a, v_out_norm_b, v_w_out, v_post_mix_norm, v_pre_ffn_norm, v_w_up, v_conv_w, v_conv_b, v_w_down, v_post_ffn_norm):
    w = dict(pre_mix_norm=pre_mix_norm, w_in=w_in, v_norm_g=v_norm_g, v_norm_b=v_norm_b, w_spatial=w_spatial,
             b_spatial=b_spatial, out_norm_a=out_norm_a, out_norm_b=out_norm_b, w_out=w_out,
             post_mix_norm=post_mix_norm, pre_ffn_norm=pre_ffn_norm, w_up=w_up, conv_w=conv_w, conv_b=conv_b,
             w_down=w_down, post_ffn_norm=post_ffn_norm)
    m = dict(pre_mix_norm=m_pre_mix_norm, w_in=m_w_in, v_norm_g=m_v_norm_g, v_norm_b=m_v_norm_b,
             w_spatial=m_w_spatial, b_spatial=m_b_spatial, out_norm_a=m_out_norm_a, out_norm_b=m_out_norm_b,
             w_out=m_w_out, post_mix_norm=m_post_mix_norm, pre_ffn_norm=m_pre_ffn_norm, w_up=m_w_up,
             conv_w=m_conv_w, conv_b=m_conv_b, w_down=m_w_down, post_ffn_norm=m_post_ffn_norm)
    v = dict(pre_mix_norm=v_pre_mix_norm, w_in=v_w_in, v_norm_g=v_v_norm_g, v_norm_b=v_v_norm_b,
             w_spatial=v_w_spatial, b_spatial=v_b_spatial, out_norm_a=v_out_norm_a, out_norm_b=v_out_norm_b,
             w_out=v_w_out, post_mix_norm=v_post_mix_norm, pre_ffn_norm=v_pre_ffn_norm, w_up=v_w_up,
             conv_w=v_conv_w, conv_b=v_conv_b, w_down=v_w_down, post_ffn_norm=v_post_ffn_norm)
    pos = jnp.stack([lax.axis_index("x"), lax.axis_index("y"), lax.axis_index("c")]).astype(jnp.int32)
    chip = 2 * lax.axis_index("x") + lax.axis_index("y")

    cw_cols = conv_w.shape[-1]
    cw_slab = lax.dynamic_update_slice(jnp.zeros((DEPTH, 3, 2 * D_FF), F32), conv_w, (0, 0, chip * cw_cols))
    conv_w_full = _allreduce_small(cw_slab.reshape(-1, LANES), "gather_conv_w").reshape(DEPTH, 3, 2 * D_FF)
    conv_w_full = conv_w_full * 0.5
    blocks = [[w[n][l].astype(BF16) for n in BIG] for l in range(DEPTH)]
    wg = dict(zip(BIG, _gather_weights(blocks[0], "gather_weights_l0")))

    small = {n: w[n] for n in SMALL}
    xs, target = x[0], loss_target[0]
    tabs = _rope_tables(xs.shape[0])
    params = [_layer_params(l, small, conv_w_full) for l in range(DEPTH)]
    saved, wgs = [], []
    xin = xs
    h = _rms_cast(xin, params[0]["pre_mix_norm"], "pre_mix_l0")
    for l in range(DEPTH):
        sv, gathered = _layer_forward(l, xin, h, params[l], wg, tabs, blocks[l + 1] if l + 1 < DEPTH else None)
        saved.append(sv)
        wgs.append(wg)
        if l + 1 < DEPTH:
            wg = dict(zip(BIG, gathered))
            xin, h = _residual_norm(sv["x1"], sv["f"], params[l]["post_ffn_norm"], params[l + 1]["pre_mix_norm"],
                                    f"post_ffn_l{l}")
    loss_part, dx = _residual_loss(saved[-1]["x1"], saved[-1]["f"], params[-1]["post_ffn_norm"], target, "loss")
    smalls, shards = [None] * DEPTH, [None] * DEPTH
    pending = None
    for l in reversed(range(DEPTH)):
        dx, big, smalls[l], scattered = _layer_backward(l, dx, saved[l], params[l], wgs[l], tabs,
                                                        pending[1] if pending else None)
        if pending:
            shards[pending[0]] = _gradient_shards(pending[0], pending[1], scattered, pos)
        pending = (l, _chip_sums(l, big, pos))
    shards[pending[0]] = _gradient_shards(pending[0], pending[1],
                                          _chip_scatter(pending[1], f"chip_scatter_l{pending[0]}"), pos)

    small_shapes = [w[n].shape for n in SMALL]
    stacked = [jnp.stack([smalls[l][n].reshape(w[n].shape[1:]) for l in range(DEPTH)]) for n in SMALL]
    cw_grad = jnp.stack([smalls[l]["conv_w"] for l in range(DEPTH)])
    packed = _pack(stacked + [cw_grad, loss_part])
    total = _allreduce_small(packed, "allreduce_small")
    parts = _unpack(total, small_shapes + [cw_grad.shape, (8, LANES)])
    g_small = dict(zip(SMALL, parts[:len(SMALL)]))
    loss = parts[-1][0, 0]
    g_conv_w = lax.dynamic_slice(parts[-2], (0, 0, chip * cw_cols), conv_w.shape)

    grads = {n: jnp.stack([shards[l][n] for l in range(DEPTH)]) for n in BIG}
    grads.update(g_small)
    grads["conv_w"] = g_conv_w

    dp, mp, vp = _adamw(_pack([w[n] for n in SMALL]), _pack([g_small[n] for n in SMALL]),
                        _pack([m[n] for n in SMALL]), _pack([v[n] for n in SMALL]), "adamw_small")
    delta = dict(zip(SMALL, _unpack(dp, small_shapes)))
    new_m = dict(zip(SMALL, _unpack(mp, small_shapes)))
    new_v = dict(zip(SMALL, _unpack(vp, small_shapes)))
    for n in BIG + ("conv_w",):
        delta[n], new_m[n], new_v[n] = _adamw_nd(w[n], grads[n], m[n], v[n], "adamw_" + n)

    return (loss, dx[None], *[grads[n] for n in WEIGHTS], *[delta[n] for n in WEIGHTS],
            *[new_m[n] for n in WEIGHTS], *[new_v[n] for n in WEIGHTS])
```

```python
import functools
import math

import jax
import jax.numpy as jnp
import numpy as np
from jax import lax
from jax.experimental import pallas as pl
from jax.experimental.pallas import tpu as pltpu

F32 = jnp.float32
BF16 = jnp.bfloat16
MESH = pl.DeviceIdType.MESH

D_MODEL = 1024
A_WIDTH = 512
A_GROUPS = 4
GROUP_DIM = 128
CHUNK = 128
B_WIDTH = 512
HEAD_DIM = 64
ROT_DIM = 16
ROPE_THETA = 500000.0
DILATIONS = (1, 4, 16)
BAND = 128
IN_COLS = 2560
D_FF = 4096
EPS = 1e-6
NEG_INF = -1e30
N_CHIPS = 4
LANES = 128

ADAM_LR = 0.001
ADAM_B1 = 0.9
ADAM_B2 = 0.999
ADAM_EPS = 1e-08
ADAM_WD = 0.01
ADAM_STEP = 10

VMEM_LIMIT_BYTES = 56 * 1024 * 1024
RSQRT2 = 0.7071067811865476
INV_SQRT_2PI = 0.3989422804014327
GELU_C = 0.7978845608028654
GELU_A = 0.044715

ANY = pl.BlockSpec(memory_space=pl.ANY)
NN = ((1,), (0,))
NT = ((1,), (1,))
TN = ((0,), (0,))


def _cparams(*sem):
    return pltpu.CompilerParams(dimension_semantics=sem, vmem_limit_bytes=VMEM_LIMIT_BYTES)


def _dot(a, b, dims):
    return lax.dot_general(a, b, (dims, ((), ())), preferred_element_type=F32)


def _rsq_mean(a):
    return lax.rsqrt(jnp.mean(a * a, axis=-1, keepdims=True) + EPS)


def _rms_bwd(a, r, g, dz):
    t = dz * g
    da = r * t - a * (r * r * r) * jnp.mean(t * a, axis=-1, keepdims=True)
    return da, dz * a * r


def _colsum(a):
    return jnp.sum(a, axis=0, keepdims=True)


def _gelu_tanh(x):
    t = jnp.tanh(GELU_C * (x + GELU_A * x * x * x))
    return 0.5 * x * (1.0 + t), t


def _gelu_tanh_grad(x, t):
    return 0.5 * (1.0 + t) + 0.5 * x * (1.0 - t * t) * GELU_C * (1.0 + 3.0 * GELU_A * x * x)


def _matmul(a, b, *, grid, a_spec, b_spec, o_spec, o_shape, o_dtype, dims, nk, kaxis, acc_shape, name, b_2d=None):
    def body(a_ref, b_ref, o_ref, *scratch):
        bv = b_ref[...] if b_2d is None else b_ref[...].reshape(b_2d)
        part = _dot(a_ref[...], bv, dims)
        if nk == 1:
            o_ref[...] = part.astype(o_dtype)
        else:
            acc = scratch[0]
            k = pl.program_id(kaxis)

            @pl.when(k == 0)
            def _():
                acc[...] = part

            @pl.when(k > 0)
            def _():
                acc[...] += part

            @pl.when(k == nk - 1)
            def _():
                o_ref[...] = acc[...].astype(o_dtype)

    sem = tuple("arbitrary" if (nk > 1 and ax == kaxis) else "parallel" for ax in range(len(grid)))
    return pl.pallas_call(
        body, grid=grid, in_specs=[a_spec, b_spec], out_specs=o_spec,
        out_shape=jax.ShapeDtypeStruct(o_shape, o_dtype),
        scratch_shapes=[pltpu.VMEM(acc_shape, F32)] if nk > 1 else [],
        compiler_params=_cparams(*sem), name=name)(a, b)


TM = 512
TMM = 1024


TR = 256


def _row_spec(width, col=0):
    return pl.BlockSpec((TR, width), lambda i, col=col: (i, col))


def _vec_spec(width):
    return pl.BlockSpec((1, width), lambda i: (0, 0))


def _rms_cast(x, g, name):
    s, d = x.shape

    def body(x_ref, g_ref, h_ref):
        a = x_ref[...]
        h_ref[...] = (a * _rsq_mean(a) * g_ref[...]).astype(BF16)

    return pl.pallas_call(
        body, grid=(s // TR,), in_specs=[_row_spec(d), _vec_spec(d)], out_specs=_row_spec(d),
        out_shape=jax.ShapeDtypeStruct((s, d), BF16), compiler_params=_cparams("parallel"), name=name)(x, g)


def _residual_norm(x0, y, g_post, g_next, name):
    s, d = x0.shape

    def body(x_ref, y_ref, gp_ref, gn_ref, x1_ref, h_ref):
        yv = y_ref[...]
        x1 = x_ref[...] + yv * _rsq_mean(yv) * gp_ref[...]
        x1_ref[...] = x1
        h_ref[...] = (x1 * _rsq_mean(x1) * gn_ref[...]).astype(BF16)

    return pl.pallas_call(
        body, grid=(s // TR,), in_specs=[_row_spec(d), _row_spec(d), _vec_spec(d), _vec_spec(d)],
        out_specs=[_row_spec(d), _row_spec(d)],
        out_shape=[jax.ShapeDtypeStruct((s, d), F32), jax.ShapeDtypeStruct((s, d), BF16)],
        compiler_params=_cparams("parallel"), name=name)(x0, y, g_post, g_next)


def _residual_loss(x1, f, g_post, target, name):
    s, d = x1.shape

    def body(x_ref, f_ref, gp_ref, t_ref, loss_ref, dx_ref):
        fv = f_ref[...]
        err = x_ref[...] + fv * _rsq_mean(fv) * gp_ref[...] - t_ref[...]
        dx_ref[...] = err * (1.0 / d)
        part = 0.5 * jnp.sum(jnp.mean(err * err, axis=-1, keepdims=True), axis=0, keepdims=True)

        @pl.when(pl.program_id(0) == 0)
        def _():
            loss_ref[...] = jnp.zeros_like(loss_ref)

        loss_ref[...] += jnp.broadcast_to(part, loss_ref.shape)

    return pl.pallas_call(
        body, grid=(s // TR,), in_specs=[_row_spec(d), _row_spec(d), _vec_spec(d), _row_spec(d)],
        out_specs=[pl.BlockSpec((8, LANES), lambda i: (0, 0)), _row_spec(d)],
        out_shape=[jax.ShapeDtypeStruct((8, LANES), F32), jax.ShapeDtypeStruct((s, d), F32)],
        compiler_params=_cparams("arbitrary"), name=name)(x1, f, g_post, target)


def _acc_init(refs):
    @pl.when(pl.program_id(0) == 0)
    def _():
        for r in refs:
            r[...] = jnp.zeros_like(r)


def _norm_bwd_out(dx, f, g_post, name):
    s, d = dx.shape

    def body(dx_ref, f_ref, g_ref, df_ref, dg_ref):
        _acc_init([dg_ref])
        fv = f_ref[...]
        dz = dx_ref[...]
        da, dgt = _rms_bwd(fv, _rsq_mean(fv), g_ref[...], dz)
        df_ref[...] = da.astype(BF16)
        dg_ref[...] += _colsum(dgt)

    return pl.pallas_call(
        body, grid=(s // TR,), in_specs=[_row_spec(d), _row_spec(d), _vec_spec(d)],
        out_specs=[_row_spec(d), _vec_spec(d)],
        out_shape=[jax.ShapeDtypeStruct((s, d), BF16), jax.ShapeDtypeStruct((1, d), F32)],
        compiler_params=_cparams("arbitrary"), name=name)(dx, f, g_post)


def _norm_bwd_mid(dx2, dh2, x1, g_pf, y1, g_pm, name):
    s, d = dx2.shape

    def body(dx2_ref, dh_ref, x1_ref, gpf_ref, y1_ref, gpm_ref, dx1_ref, dy1_ref, dgpf_ref, dgpm_ref):
        _acc_init([dgpf_ref, dgpm_ref])
        x1 = x1_ref[...]
        da, dgt = _rms_bwd(x1, _rsq_mean(x1), gpf_ref[...], dh_ref[...])
        dx1 = dx2_ref[...] + da
        dx1_ref[...] = dx1
        dgpf_ref[...] += _colsum(dgt)
        y1 = y1_ref[...]
        dy, dgt2 = _rms_bwd(y1, _rsq_mean(y1), gpm_ref[...], dx1)
        dy1_ref[...] = dy.astype(BF16)
        dgpm_ref[...] += _colsum(dgt2)

    return pl.pallas_call(
        body, grid=(s // TR,),
        in_specs=[_row_spec(d), _row_spec(d), _row_spec(d), _vec_spec(d), _row_spec(d), _vec_spec(d)],
        out_specs=[_row_spec(d), _row_spec(d), _vec_spec(d), _vec_spec(d)],
        out_shape=[jax.ShapeDtypeStruct((s, d), F32), jax.ShapeDtypeStruct((s, d), BF16),
                   jax.ShapeDtypeStruct((1, d), F32), jax.ShapeDtypeStruct((1, d), F32)],
        compiler_params=_cparams("arbitrary"), name=name)(dx2, dh2, x1, g_pf, y1, g_pm)


def _norm_bwd_in(dx1, dh1, x0, g1, name):
    s, d = dx1.shape

    def body(dx1_ref, dh_ref, x0_ref, g_ref, dx0_ref, dg_ref):
        _acc_init([dg_ref])
        x0 = x0_ref[...]
        da, dgt = _rms_bwd(x0, _rsq_mean(x0), g_ref[...], dh_ref[...])
        dx0_ref[...] = dx1_ref[...] + da
        dg_ref[...] += _colsum(dgt)

    return pl.pallas_call(
        body, grid=(s // TR,), in_specs=[_row_spec(d), _row_spec(d), _row_spec(d), _vec_spec(d)],
        out_specs=[_row_spec(d), _vec_spec(d)],
        out_shape=[jax.ShapeDtypeStruct((s, d), F32), jax.ShapeDtypeStruct((1, d), F32)],
        compiler_params=_cparams("arbitrary"), name=name)(dx1, dh1, x0, g1)


def _tril_mask():
    row = lax.broadcasted_iota(jnp.int32, (CHUNK, CHUNK), 0)
    col = lax.broadcasted_iota(jnp.int32, (CHUNK, CHUNK), 1)
    return row >= col


def _gating_forward(pa, gv, bv, wt, bsf):
    er = lax.erf(pa * RSQRT2)
    za = 0.5 * pa * (1.0 + er)
    u = za[:, :A_WIDTH]
    va = za[:, A_WIDTH:]
    xc = va - jnp.mean(va, axis=-1, keepdims=True)
    rs = lax.rsqrt(jnp.mean(xc * xc, axis=-1, keepdims=True) + EPS)
    vn = xc * rs
    vlb = (vn * gv + bv).astype(BF16)
    sg = jnp.concatenate(
        [_dot(wt[g], vlb[:, g * GROUP_DIM:(g + 1) * GROUP_DIM], NN) for g in range(A_GROUPS)], axis=1) + bsf
    return er, u, rs, vn, vlb, sg


def _masked_ws(ws_ref):
    mask = _tril_mask()
    return [jnp.where(mask, ws_ref[g], 0.0).astype(BF16) for g in range(A_GROUPS)]


def _mixer_a_fwd(proj, gv, bv, ws, bsf, ga, name):
    s = proj.shape[0]

    def body(p_ref, gv_ref, bv_ref, ws_ref, bs_ref, ga_ref, o_ref):
        wt = _masked_ws(ws_ref)
        for ch in range(TR // CHUNK):
            rows = slice(ch * CHUNK, (ch + 1) * CHUNK)
            _, u, _, _, _, sg = _gating_forward(p_ref[rows, :], gv_ref[...], bv_ref[...], wt, bs_ref[...])
            oa = u * sg
            o_ref[rows, :] = (oa * _rsq_mean(oa) * ga_ref[...]).astype(BF16)

    return pl.pallas_call(
        body, grid=(s // TR,),
        in_specs=[_row_spec(2 * A_WIDTH), _vec_spec(A_WIDTH), _vec_spec(A_WIDTH),
                  pl.BlockSpec((A_GROUPS, CHUNK, CHUNK), lambda i: (0, 0, 0)),
                  pl.BlockSpec((CHUNK, A_WIDTH), lambda i: (0, 0)), _vec_spec(A_WIDTH)],
        out_specs=_row_spec(A_WIDTH), out_shape=jax.ShapeDtypeStruct((s, A_WIDTH + B_WIDTH), BF16),
        compiler_params=_cparams("parallel"), name=name)(proj, gv, bv, ws, bsf, ga)


def _mixer_a_bwd(proj, dmixed, gv, bv, ws, bsf, ga, name):
    s = proj.shape[0]
    nsteps = s // TR

    def body(p_ref, dm_ref, gv_ref, bv_ref, ws_ref, bs_ref, ga_ref,
             dp_ref, dga_ref, dgv_ref, dbv_ref, dbs_ref, dws_ref):
        _acc_init([dga_ref, dgv_ref, dbv_ref, dbs_ref, dws_ref])
        mask = _tril_mask()
        wt = _masked_ws(ws_ref)
        gvv = gv_ref[...]
        gav = ga_ref[...]
        for ch in range(TR // CHUNK):
            rows = slice(ch * CHUNK, (ch + 1) * CHUNK)
            pa = p_ref[rows, :]
            er, u, rs, vn, vlb, sg = _gating_forward(pa, gvv, bv_ref[...], wt, bs_ref[...])
            oa = u * sg
            doa, dgt = _rms_bwd(oa, _rsq_mean(oa), gav, dm_ref[rows, :])
            dga_ref[...] += _colsum(dgt)
            du = doa * sg
            dsg = doa * u
            dbs_ref[...] += dsg
            dsgb = dsg.astype(BF16)
            dvl = []
            for g in range(A_GROUPS):
                cols = slice(g * GROUP_DIM, (g + 1) * GROUP_DIM)
                dws_ref[g] += jnp.where(mask, _dot(dsgb[:, cols], vlb[:, cols], NT), 0.0)
                dvl.append(_dot(wt[g], dsgb[:, cols], TN))
            dvl = jnp.concatenate(dvl, axis=1)
            dgv_ref[...] += _colsum(dvl * vn)
            dbv_ref[...] += _colsum(dvl)
            dvn = dvl * gvv
            dva = rs * (dvn - jnp.mean(dvn, axis=-1, keepdims=True)
                        - vn * jnp.mean(dvn * vn, axis=-1, keepdims=True))
            gp = 0.5 * (1.0 + er) + pa * jnp.exp(-0.5 * pa * pa) * INV_SQRT_2PI
            dp_ref[rows, :] = (jnp.concatenate([du, dva], axis=1) * gp).astype(BF16)

        @pl.when(pl.program_id(0) == nsteps - 1)
        def _():
            for g in range(A_GROUPS):
                cols = slice(g * GROUP_DIM, (g + 1) * GROUP_DIM)
                tot = jnp.sum(dbs_ref[:, cols], axis=1, keepdims=True)
                dbs_ref[:, cols] = jnp.broadcast_to(tot, (CHUNK, GROUP_DIM))

    full = lambda *shape: pl.BlockSpec(shape, lambda i: (0,) * len(shape))
    return pl.pallas_call(
        body, grid=(nsteps,),
        in_specs=[_row_spec(2 * A_WIDTH), _row_spec(A_WIDTH), _vec_spec(A_WIDTH), _vec_spec(A_WIDTH),
                  full(A_GROUPS, CHUNK, CHUNK), full(CHUNK, A_WIDTH), _vec_spec(A_WIDTH)],
        out_specs=[_row_spec(2 * A_WIDTH), _vec_spec(A_WIDTH), _vec_spec(A_WIDTH), _vec_spec(A_WIDTH),
                   full(CHUNK, A_WIDTH), full(A_GROUPS, CHUNK, CHUNK)],
        out_shape=[jax.ShapeDtypeStruct((s, IN_COLS), BF16), jax.ShapeDtypeStruct((1, A_WIDTH), F32),
                   jax.ShapeDtypeStruct((1, A_WIDTH), F32), jax.ShapeDtypeStruct((1, A_WIDTH), F32),
                   jax.ShapeDtypeStruct((CHUNK, A_WIDTH), F32),
                   jax.ShapeDtypeStruct((A_GROUPS, CHUNK, CHUNK), F32)],
        compiler_params=_cparams("arbitrary"), name=name)(proj, dmixed, gv, bv, ws, bsf, ga)


def _rope_tables(s):
    half = ROT_DIM // 2
    inv = ROPE_THETA ** (-jnp.arange(0, ROT_DIM, 2, dtype=F32) / ROT_DIM)
    ang = jnp.arange(s, dtype=F32)[:, None] * inv[None, :]
    cos, sin = jnp.cos(ang), jnp.sin(ang)
    zeros = jnp.zeros((s, HEAD_DIM - ROT_DIM), F32)
    zh = jnp.zeros((s, half), F32)
    c = jnp.concatenate([cos, cos, zeros + 1.0], axis=1)
    s1 = jnp.concatenate([-sin, zh, zeros], axis=1)
    s2 = jnp.concatenate([zh, sin, zeros], axis=1)
    return tuple(jnp.concatenate([t, t], axis=1) for t in (c, s1, s2))


def _lane_blocks(width):
    return [slice(b * LANES, (b + 1) * LANES) for b in range(width // LANES)]


CLASS_DILS = tuple(d for d in DILATIONS if d > 1)


def _class_shape(s, dil, dtype):
    return jax.ShapeDtypeStruct((dil, s // dil, B_WIDTH), dtype)


def _class_spec(dil):
    return pl.BlockSpec((dil, TR // dil, B_WIDTH), lambda i, *_: (0, i, 0))


NBLK = B_WIDTH // LANES
STAGE = pltpu.VMEM((NBLK, TR, LANES), F32)


def _stage_put(stage, value):
    for b, sl in enumerate(_lane_blocks(B_WIDTH)):
        stage[b] = value[:, sl]


def _stage_get(stage):
    return jnp.concatenate([stage[b] for b in range(NBLK)], axis=1)


def _store_classes(stage, dst_ref, dil):
    for b, sl in enumerate(_lane_blocks(B_WIDTH)):
        for r in range(dil):
            dst_ref[r, :, sl] = stage[b, pl.ds(r, TR // dil, stride=dil), :].astype(dst_ref.dtype)


def _load_classes(src_ref, stage, dil):
    for b, sl in enumerate(_lane_blocks(B_WIDTH)):
        for r in range(dil):
            stage[b, pl.ds(r, TR // dil, stride=dil), :] = src_ref[r, :, sl].astype(F32)
    return _stage_get(stage)


def _rope_fwd(proj, tabs, name):
    s = proj.shape[0]
    half = ROT_DIM // 2
    scale = HEAD_DIM ** -0.5
    nlay = 1 + len(CLASS_DILS)

    def body(q_ref, k_ref, v_ref, c_ref, s1_ref, s2_ref, *rest):
        outs, stage = rest[:3 * nlay], rest[3 * nlay]
        c, s1, s2 = c_ref[...], s1_ref[...], s2_ref[...]
        for which, (src, mul) in enumerate(((q_ref, scale), (k_ref, 1.0), (v_ref, None))):
            if mul is None:
                _stage_put(stage, src[...])
            else:
                for b, sl in enumerate(_lane_blocks(B_WIDTH)):
                    a = src[:, sl]
                    r = a * c + pltpu.roll(a, LANES - half, 1) * s1 + pltpu.roll(a, half, 1) * s2
                    stage[b] = r * mul
            dst = outs[which * nlay:(which + 1) * nlay]
            dst[0][...] = _stage_get(stage).astype(BF16)
            for ref, d in zip(dst[1:], CLASS_DILS):
                _store_classes(stage, ref, d)

    tab = pl.BlockSpec((TR, LANES), lambda i: (i, 0))
    lay_specs = [_row_spec(B_WIDTH)] + [_class_spec(d) for d in CLASS_DILS]
    lay_shapes = [jax.ShapeDtypeStruct((s, B_WIDTH), BF16)] + [_class_shape(s, d, BF16) for d in CLASS_DILS]
    outs = pl.pallas_call(
        body, grid=(s // TR,),
        in_specs=[_row_spec(B_WIDTH, 2), _row_spec(B_WIDTH, 3), _row_spec(B_WIDTH, 4), tab, tab, tab],
        out_specs=lay_specs * 3, out_shape=lay_shapes * 3, scratch_shapes=[STAGE],
        compiler_params=_cparams("parallel"), name=name)(proj, proj, proj, *tabs)
    q, k, v = (dict(zip(DILATIONS, outs[w * nlay:(w + 1) * nlay])) for w in range(3))
    return q, k, v


def _as_classes(t):
    return t if t.ndim == 3 else t[None]


def _band_mask(i):
    qi = lax.broadcasted_iota(jnp.int32, (BAND, 2 * BAND), 0)
    kj = lax.broadcasted_iota(jnp.int32, (BAND, 2 * BAND), 1)
    return (kj >= qi) & (kj <= qi + BAND) & ((kj >= BAND) | (i > 0))


def _head_masks():
    lane = lax.broadcasted_iota(jnp.int32, (1, LANES), 1)
    return lane < HEAD_DIM, lane >= HEAD_DIM


def _stack_heads(t):
    lo, hi = _head_masks()
    zero = jnp.zeros_like(t)
    return jnp.concatenate([jnp.where(lo, t, zero), jnp.where(hi, t, zero)], axis=0)


def _attn_specs(last):
    cur = pl.BlockSpec((None, BAND, B_WIDTH), lambda r, i: (r, jnp.minimum(i, last), 0))
    prev = pl.BlockSpec((None, BAND, B_WIDTH), lambda r, i: (r, jnp.maximum(jnp.minimum(i, last) - 1, 0), 0))
    return cur, prev


def _attn_fwd(q, k, v, name):
    dil, n, _ = q.shape
    nb = n // BAND

    def body(q_ref, kc_ref, kp_ref, vc_ref, vp_ref, o_ref, l_ref):
        valid = _band_mask(pl.program_id(1))
        valid = jnp.concatenate([valid, valid], axis=0)
        lo, _ = _head_masks()
        for sl in _lane_blocks(B_WIDTH):
            kk = jnp.concatenate([kp_ref[:, sl], kc_ref[:, sl]], axis=0)
            vv = jnp.concatenate([vp_ref[:, sl], vc_ref[:, sl]], axis=0)
            sc = jnp.where(valid, _dot(_stack_heads(q_ref[:, sl]), kk, NT), NEG_INF)
            mx = jnp.max(sc, axis=1, keepdims=True)
            p = jnp.exp(sc - mx)
            den = jnp.sum(p, axis=1, keepdims=True)
            out = _dot(p.astype(BF16), vv, NN) / den
            lse = mx + jnp.log(den)
            o_ref[:, sl] = jnp.where(lo, out[:BAND], out[BAND:])
            l_ref[:, sl] = jnp.where(lo, lse[:BAND], lse[BAND:])

    cur, prev = _attn_specs(nb - 1)
    return pl.pallas_call(
        body, grid=(dil, nb), in_specs=[cur, cur, prev, cur, prev], out_specs=[cur, cur],
        out_shape=[jax.ShapeDtypeStruct((dil, n, B_WIDTH), F32)] * 2,
        compiler_params=_cparams("parallel", "parallel"), name=name)(q, k, k, v, v)


def _attn_combine(outs, lses, gb, mixed, name):
    s = mixed.shape[0]
    npat = len(DILATIONS)
    w = B_WIDTH

    def body(*refs):
        o_refs, l_refs = refs[:npat], refs[npat:2 * npat]
        g_ref, _, ob_ref = refs[2 * npat:2 * npat + 3]
        lse_refs = refs[2 * npat + 3:3 * npat + 3]
        mb_ref, stage = refs[3 * npat + 3:]
        os_ = [o_refs[0][...]] + [_load_classes(r, stage, d) for r, d in zip(o_refs[1:], CLASS_DILS)]
        ls = [l_refs[0][...]] + [_load_classes(r, stage, d) for r, d in zip(l_refs[1:], CLASS_DILS)]
        mx = functools.reduce(jnp.maximum, ls)
        ws = [jnp.exp(l - mx) for l in ls]
        tot = functools.reduce(lambda a, b: a + b, ws)
        ob = functools.reduce(lambda a, b: a + b, [wt / tot * o for wt, o in zip(ws, os_)])
        ob_ref[...] = ob
        lse = mx + jnp.log(tot)
        _stage_put(stage, lse)
        lse_refs[0][...] = lse
        for ref, d in zip(lse_refs[1:], CLASS_DILS):
            _store_classes(stage, ref, d)
        mb_ref[...] = (ob * _rsq_mean(ob) * g_ref[...]).astype(BF16)

    lay_specs = [_row_spec(w)] + [_class_spec(d) for d in CLASS_DILS]
    res = pl.pallas_call(
        body, grid=(s // TR,), in_specs=lay_specs * 2 + [_vec_spec(w), ANY],
        out_specs=[_row_spec(w)] + lay_specs + [_row_spec(w, 1)],
        out_shape=[jax.ShapeDtypeStruct((s, w), F32), jax.ShapeDtypeStruct((s, w), F32)]
        + [_class_shape(s, d, F32) for d in CLASS_DILS] + [jax.ShapeDtypeStruct(mixed.shape, mixed.dtype)],
        scratch_shapes=[STAGE], input_output_aliases={2 * npat + 1: npat + 1},
        compiler_params=_cparams("parallel"), name=name)(*outs, *lses, gb, mixed)
    return res[0], dict(zip(DILATIONS, res[1:npat + 1])), res[npat + 1]


def _attn_bwd_prep(dmixed, ob, gb, name):
    s = ob.shape[0]
    w = B_WIDTH
    nlay = len(DILATIONS)

    def body(dm_ref, ob_ref, g_ref, *rest):
        do_refs, dl_refs = rest[:nlay], rest[nlay:2 * nlay]
        dg_ref, stage = rest[2 * nlay:]
        _acc_init([dg_ref])
        ob = ob_ref[...]
        dob, dgt = _rms_bwd(ob, _rsq_mean(ob), g_ref[...], dm_ref[...])
        dg_ref[...] += _colsum(dgt)
        _stage_put(stage, dob)
        do_refs[0][...] = dob.astype(BF16)
        for ref, d in zip(do_refs[1:], CLASS_DILS):
            _store_classes(stage, ref, d)
        lo, hi = _head_masks()
        t = dob * ob
        for b, sl in enumerate(_lane_blocks(w)):
            tb = t[:, sl]
            s0 = jnp.sum(jnp.where(lo, tb, 0.0), axis=1, keepdims=True)
            s1 = jnp.sum(jnp.where(hi, tb, 0.0), axis=1, keepdims=True)
            stage[b] = jnp.where(lo, s0, s1)
        dl_refs[0][...] = _stage_get(stage)
        for ref, d in zip(dl_refs[1:], CLASS_DILS):
            _store_classes(stage, ref, d)

    lay_specs = [_row_spec(w)] + [_class_spec(d) for d in CLASS_DILS]
    shapes = lambda dt: [jax.ShapeDtypeStruct((s, w), dt)] + [_class_shape(s, d, dt) for d in CLASS_DILS]
    res = pl.pallas_call(
        body, grid=(s // TR,), in_specs=[_row_spec(w, 1), _row_spec(w), _vec_spec(w)],
        out_specs=lay_specs * 2 + [_vec_spec(w)],
        out_shape=shapes(BF16) + shapes(F32) + [jax.ShapeDtypeStruct((1, w), F32)],
        scratch_shapes=[STAGE],
        compiler_params=_cparams("arbitrary"), name=name)(dmixed, ob, gb)
    return dict(zip(DILATIONS, res[:nlay])), dict(zip(DILATIONS, res[nlay:2 * nlay])), res[2 * nlay]


def _attn_bwd(q, k, v, do, lse, delta, name):
    dil, n, _ = q.shape
    nb = n // BAND

    def body(q_ref, kc_ref, kp_ref, vc_ref, vp_ref, do_ref, lse_ref, dl_ref,
             dq_ref, dk_ref, dv_ref, ck_ref, cv_ref):
        i = pl.program_id(1)

        @pl.when(i == 0)
        def _():
            ck_ref[...] = jnp.zeros_like(ck_ref)
            cv_ref[...] = jnp.zeros_like(cv_ref)

        @pl.when(i < nb)
        def _():
            valid = _band_mask(i)
            valid = jnp.concatenate([valid, valid], axis=0)
            lo, _ = _head_masks()
            lane = lax.broadcasted_iota(jnp.int32, (1, LANES), 1)

            def per_head(t):
                return jnp.concatenate(
                    [jnp.sum(jnp.where(lane == first, t, 0.0), axis=1, keepdims=True) for first in (0, HEAD_DIM)], axis=0)

            for sl in _lane_blocks(B_WIDTH):
                q2 = _stack_heads(q_ref[:, sl])
                do2 = _stack_heads(do_ref[:, sl])
                kk = jnp.concatenate([kp_ref[:, sl], kc_ref[:, sl]], axis=0)
                vv = jnp.concatenate([vp_ref[:, sl], vc_ref[:, sl]], axis=0)
                p = jnp.where(valid, jnp.exp(_dot(q2, kk, NT) - per_head(lse_ref[:, sl])), 0.0)
                ds = (p * (_dot(do2, vv, NT) - per_head(dl_ref[:, sl]))).astype(BF16)
                dq = _dot(ds, kk, NN)
                dkk = _dot(ds, q2, TN)
                dvv = _dot(p.astype(BF16), do2, TN)
                dq_ref[:, sl] = jnp.where(lo, dq[:BAND], dq[BAND:])
                dk_ref[:, sl] = ck_ref[:, sl] + dkk[:BAND]
                dv_ref[:, sl] = cv_ref[:, sl] + dvv[:BAND]
                ck_ref[:, sl] = dkk[BAND:]
                cv_ref[:, sl] = dvv[BAND:]

        @pl.when(i == nb)
        def _():
            dk_ref[...] = ck_ref[...]
            dv_ref[...] = cv_ref[...]

    cur, prev = _attn_specs(nb - 1)
    lag = pl.BlockSpec((None, BAND, B_WIDTH), lambda r, i: (r, jnp.maximum(i - 1, 0), 0))
    shape = jax.ShapeDtypeStruct((dil, n, B_WIDTH), F32)
    return pl.pallas_call(
        body, grid=(dil, nb + 1), in_specs=[cur, cur, prev, cur, prev, cur, cur, cur],
        out_specs=[cur, lag, lag], out_shape=[shape] * 3,
        scratch_shapes=[pltpu.VMEM((BAND, B_WIDTH), F32)] * 2,
        compiler_params=_cparams("arbitrary", "arbitrary"), name=name)(q, k, k, v, v, do, lse, delta)


def _rope_bwd(dqs, dks, dvs, tabs, dproj, name):
    s = dproj.shape[0]
    half = ROT_DIM // 2
    scale = HEAD_DIM ** -0.5
    npat = len(DILATIONS)
    w = B_WIDTH

    def body(*refs):
        groups = [refs[g * npat:(g + 1) * npat] for g in range(3)]
        c_ref, s1_ref, s2_ref, _, o_ref, stage = refs[3 * npat:]

        def total(rs):
            acc = rs[0][...]
            for ref, d in zip(rs[1:], CLASS_DILS):
                acc = acc + _load_classes(ref, stage, d)
            return acc

        def unrope(g):
            c, s1, s2 = c_ref[...], s1_ref[...], s2_ref[...]
            for sl in _lane_blocks(w):
                gb = g[:, sl]
                o = gb * c + pltpu.roll(gb * s1, half, 1) + pltpu.roll(gb * s2, LANES - half, 1)
                o_ref[:, sl] = o.astype(BF16)

        which = pl.program_id(1)

        @pl.when(which == 0)
        def _():
            unrope(total(groups[0]) * scale)

        @pl.when(which == 1)
        def _():
            unrope(total(groups[1]))

        @pl.when(which == 2)
        def _():
            o_ref[...] = total(groups[2]).astype(BF16)

    tab = pl.BlockSpec((TR, LANES), lambda i, j: (i, 0))
    nat = pl.BlockSpec((TR, w), lambda i, j: (i, 0))
    lay_specs = [nat] + [_class_spec(d) for d in CLASS_DILS]
    first_col = 2 * A_WIDTH // w
    return pl.pallas_call(
        body, grid=(s // TR, 3), in_specs=lay_specs * 3 + [tab] * 3 + [ANY],
        out_specs=pl.BlockSpec((TR, w), lambda i, j: (i, first_col + j)),
        out_shape=jax.ShapeDtypeStruct(dproj.shape, dproj.dtype), scratch_shapes=[STAGE],
        input_output_aliases={3 * npat + 3: 0},
        compiler_params=_cparams("parallel", "arbitrary"), name=name)(*dqs, *dks, *dvs, *tabs, dproj)


TK = 512
HALO = 16


def _row_of(v, r):
    rows = lax.broadcasted_iota(jnp.int32, (v.shape[0], 1), 0)
    return jnp.sum(jnp.where(rows == r, v, 0.0), axis=0, keepdims=True)


def _taps_before(x, halo):
    row = lax.broadcasted_iota(jnp.int32, (x.shape[0], 1), 0)
    m1 = jnp.where(row == 0, _row_of(halo, HALO - 1), pltpu.roll(x, 1, 0))
    m2 = jnp.where(row == 0, _row_of(halo, HALO - 2), jnp.where(row == 1, _row_of(halo, HALO - 1), pltpu.roll(x, 2, 0)))
    return m2, m1, x


def _taps_after(x, halo):
    rows = x.shape[0]
    row = lax.broadcasted_iota(jnp.int32, (rows, 1), 0)
    p1 = jnp.where(row == rows - 1, _row_of(halo, 0), pltpu.roll(x, rows - 1, 0))
    p2 = jnp.where(row == rows - 2, _row_of(halo, 0), jnp.where(row == rows - 1, _row_of(halo, 1), pltpu.roll(x, rows - 2, 0)))
    return p1, p2


def _conv_value(taps, cw_ref, cb_ref, h):
    return cb_ref[h] + cw_ref[h, 0:1, :] * taps[0] + cw_ref[h, 1:2, :] * taps[1] + cw_ref[h, 2:3, :] * taps[2]


def _ffn_tiles(nm, ncol, reverse):
    last = nm * ncol - 1

    def split(t):
        row = t // ncol
        return (nm - 1 - row if reverse else row), t % ncol

    return (lambda t: split(jnp.minimum(t, last))), (lambda t: split(jnp.maximum(t - 1, 0)))


def _ffn_forward(h2, w_up, w_down, cw3, cb3, name, gather=None):
    s = h2.shape[0]
    nm, ncol = s // TM, D_FF // TK
    nsteps = nm * ncol + 1
    ng = 0 if gather is None else len(gather)
    ahead, behind = _ffn_tiles(nm, ncol, False)

    def body(*refs):
        h_ref, wg_ref, wv_ref, wd_ref, cw_ref, cb_ref = refs[:6]
        g_in = refs[6:6 + ng]
        y_ref, up_ref, f_ref = refs[6 + ng:9 + ng]
        g_out = refs[9 + ng:9 + 2 * ng]
        carry, acc, up_s = refs[9 + 2 * ng:12 + 2 * ng]
        t = pl.program_id(0)
        jb = behind(t)[1]
        if ng:
            start, relay, finish = _gather_steps(g_in, g_out, *refs[12 + 2 * ng:])
            pl.when(t == 0)(start)
            pl.when(t == (nm - 1) * ncol)(relay)

        @pl.when(t == 0)
        def _():
            carry[...] = jnp.zeros_like(carry)
            up_s[...] = jnp.zeros_like(up_s)

        xs = [up_s[hh].astype(F32) for hh in range(2)]
        h = h_ref[...]
        for hh, w_ref in ((0, wg_ref), (1, wv_ref)):
            up = _dot(h, w_ref[...], NN).astype(BF16)
            up_ref[hh] = up
            up_s[hh] = up
        conv = [_conv_value(_taps_before(xs[hh], carry[jb, hh]), cw_ref, cb_ref, hh) for hh in range(2)]
        y = (_gelu_tanh(conv[0])[0] * conv[1]).astype(BF16)
        y_ref[...] = y
        part = _dot(y, wd_ref[...], NN)

        @pl.when(t > 0)
        def _():
            for hh in range(2):
                carry[jb, hh] = xs[hh][TM - HALO:, :]

        @pl.when((t > 0) & (jb == 0))
        def _():
            acc[...] = part

        @pl.when((t > 0) & (jb > 0))
        def _():
            acc[...] += part

        @pl.when((t > 0) & (jb == ncol - 1))
        def _():
            f_ref[...] = acc[...]

        if ng:
            pl.when(t == nsteps - 1)(finish)

    per_up = (2 * D_FF // N_CHIPS) // TK
    per_dn = (D_FF // N_CHIPS) // TK

    def up_w(t, first):
        j = ahead(t)[1] + first
        return (j // per_up, 0, 0, j % per_up)

    def down_w(t):
        j = behind(t)[1]
        return (j // per_dn, 0, j % per_dn, 0)

    res = pl.pallas_call(
        body, grid=(nsteps,),
        in_specs=[pl.BlockSpec((TM, D_MODEL), lambda t: (ahead(t)[0], 0)),
                  pl.BlockSpec((None, None, D_MODEL, TK), lambda t: up_w(t, 0)),
                  pl.BlockSpec((None, None, D_MODEL, TK), lambda t: up_w(t, ncol)),
                  pl.BlockSpec((None, None, TK, D_MODEL), down_w),
                  pl.BlockSpec((2, 3, TK), lambda t: (0, 0, behind(t)[1])),
                  pl.BlockSpec((2, 1, TK), lambda t: (0, 0, behind(t)[1]))] + [ANY] * ng,
        out_specs=[pl.BlockSpec((TM, TK), lambda t: behind(t)),
                   pl.BlockSpec((2, TM, TK), lambda t: (0,) + ahead(t)),
                   pl.BlockSpec((TM, D_MODEL), lambda t: (behind(t)[0], 0))] + [ANY] * ng,
        out_shape=[jax.ShapeDtypeStruct((s, D_FF), BF16), jax.ShapeDtypeStruct((2, s, D_FF), BF16),
                   jax.ShapeDtypeStruct((s, D_MODEL), F32)] + _gathered_shapes(gather or []),
        scratch_shapes=[pltpu.VMEM((ncol, 2, HALO, TK), F32), pltpu.VMEM((TM, D_MODEL), F32),
                        pltpu.VMEM((2, TM, TK), BF16)] + (_gather_sems(ng) if ng else []),
        compiler_params=_cparams("arbitrary"), name=name)(h2, w_up, w_up, w_down, cw3, cb3, *(gather or []))
    return res[:3], list(res[3:])


def _ffn_backward(df, w_up, w_down, up3, cw3, cb3, name, scatter=None):
    s = df.shape[0]
    nm, ncol = s // TM, D_FF // TK
    nsteps = nm * ncol + 1
    ns = 0 if scatter is None else len(scatter)
    ahead, behind = _ffn_tiles(nm, ncol, True)

    def body(*refs):
        df_ref, wg_ref, wv_ref, wd_ref, cw_ref, cb_ref, up_ref, halo_ref = refs[:8]
        s_in = refs[8:8 + ns]
        dup_ref, dh_ref, sums_ref = refs[8 + ns:11 + ns]
        s_out = refs[11 + ns:11 + 2 * ns]
        carry, acc, dy_s = refs[11 + 2 * ns:14 + 2 * ns]
        t = pl.program_id(0)
        ib, jb = behind(t)
        if ns:
            start, finish = _scatter_steps(s_in, s_out, *refs[14 + 2 * ns:])
            pl.when(t == 0)(start)

        @pl.when(t == 0)
        def _():
            carry[...] = jnp.zeros_like(carry)
            sums_ref[...] = jnp.zeros_like(sums_ref)
            dy_s[...] = jnp.zeros_like(dy_s)

        dy = dy_s[...]
        dy_s[...] = _dot(df_ref[...], wd_ref[...], NT)
        conv, taps = [], []
        for hh in range(2):
            halo = jnp.where(ib == 0, 0.0, halo_ref[hh].astype(F32))
            tp = _taps_before(up_ref[hh].astype(F32), halo)
            conv.append(_conv_value(tp, cw_ref, cb_ref, hh))
            taps.append(tp)
        act, th = _gelu_tanh(conv[0])
        dcs = (dy * conv[1] * _gelu_tanh_grad(conv[0], th), dy * act)
        row = lax.broadcasted_iota(jnp.int32, (8, 1), 0)
        part, upds = None, []
        for hh, w_ref in ((0, wg_ref), (1, wv_ref)):
            dc, tp = dcs[hh], taps[hh]
            upd = jnp.zeros((8, TK), F32)
            for ridx, sm in enumerate((_colsum(dc * tp[0]), _colsum(dc * tp[1]), _colsum(dc * tp[2]), _colsum(dc))):
                upd = jnp.where(row == ridx, sm, upd)
            upds.append(upd)
            after1, after2 = _taps_after(dc, carry[jb, hh])
            dup = (cw_ref[hh, 2:3, :] * dc + cw_ref[hh, 1:2, :] * after1 + cw_ref[hh, 0:1, :] * after2).astype(BF16)
            dup_ref[hh] = dup
            d = _dot(dup, w_ref[...], NT)
            part = d if part is None else part + d

        @pl.when(t > 0)
        def _():
            for hh in range(2):
                sums_ref[jb, hh] += upds[hh]
                carry[jb, hh] = dcs[hh][:HALO, :]

        @pl.when((t > 0) & (jb == 0))
        def _():
            acc[...] = part

        @pl.when((t > 0) & (jb > 0))
        def _():
            acc[...] += part

        @pl.when((t > 0) & (jb == ncol - 1))
        def _():
            dh_ref[...] = acc[...]

        if ns:
            pl.when(t == nsteps - 1)(finish)

    per_up = (2 * D_FF // N_CHIPS) // TK
    per_dn = (D_FF // N_CHIPS) // TK

    def up_w(t, first):
        j = behind(t)[1] + first
        return (j // per_up, 0, 0, j % per_up)

    def down_w(t):
        j = ahead(t)[1]
        return (j // per_dn, 0, j % per_dn, 0)

    def halo_rows(t):
        i, j = behind(t)
        return (0, jnp.maximum(i * (TM // HALO) - 1, 0), j)

    res = pl.pallas_call(
        body, grid=(nsteps,),
        in_specs=[pl.BlockSpec((TM, D_MODEL), lambda t: (ahead(t)[0], 0)),
                  pl.BlockSpec((None, None, D_MODEL, TK), lambda t: up_w(t, 0)),
                  pl.BlockSpec((None, None, D_MODEL, TK), lambda t: up_w(t, ncol)),
                  pl.BlockSpec((None, None, TK, D_MODEL), down_w),
                  pl.BlockSpec((2, 3, TK), lambda t: (0, 0, behind(t)[1])),
                  pl.BlockSpec((2, 1, TK), lambda t: (0, 0, behind(t)[1])),
                  pl.BlockSpec((2, TM, TK), lambda t: (0,) + behind(t)),
                  pl.BlockSpec((2, HALO, TK), halo_rows)] + [ANY] * ns,
        out_specs=[pl.BlockSpec((2, TM, TK), lambda t: (0,) + behind(t)),
                   pl.BlockSpec((TM, D_MODEL), lambda t: (behind(t)[0], 0)),
                   pl.BlockSpec((ncol, 2, 8, TK), lambda t: (0, 0, 0, 0))] + [ANY] * ns,
        out_shape=[jax.ShapeDtypeStruct((2, s, D_FF), BF16), jax.ShapeDtypeStruct((s, D_MODEL), F32),
                   jax.ShapeDtypeStruct((ncol, 2, 8, TK), F32)] + (_scattered_shapes() if ns else []),
        scratch_shapes=[pltpu.VMEM((ncol, 2, HALO, TK), F32), pltpu.VMEM((TM, D_MODEL), F32),
                        pltpu.VMEM((TM, TK), F32)] + (_scatter_sems() if ns else []),
        compiler_params=_cparams("arbitrary"), name=name)(df, w_up, w_up, w_down, cw3, cb3, up3, up3,
                                                          *(scatter or []))
    return res[:3], list(res[3:])


def _wspec(rows, cols, index_map):
    return pl.BlockSpec((None, None, rows, cols), index_map)


def _layer_forward(l, x0, h1, p, wg, tabs, gather=None):
    s = x0.shape[0]
    nm = s // TMM
    tag = f"_l{l}"
    proj = _matmul(
        h1, wg["w_in"], grid=(nm, N_CHIPS), a_spec=pl.BlockSpec((TMM, D_MODEL), lambda i, j: (i, 0)),
        b_spec=_wspec(D_MODEL, IN_COLS // N_CHIPS, lambda i, j: (j, 0, 0, 0)),
        o_spec=pl.BlockSpec((TMM, IN_COLS // N_CHIPS), lambda i, j: (i, j)), o_shape=(s, IN_COLS), o_dtype=F32,
        dims=NN, nk=1, kaxis=None, acc_shape=None, name="proj" + tag)
    ma = _mixer_a_fwd(proj, p["v_norm_g"], p["v_norm_b"], p["w_spatial"], p["bs_full"], p["out_norm_a"],
                      "mixer_a_fwd" + tag)
    q, k, v = _rope_fwd(proj, tabs, "rope_fwd" + tag)
    outs, lses = zip(*[_attn_fwd(_as_classes(q[d]), _as_classes(k[d]), _as_classes(v[d]), f"attn_fwd_d{d}" + tag)
                       for d in DILATIONS])
    outs = [o.reshape(s, B_WIDTH) if d == 1 else o for o, d in zip(outs, DILATIONS)]
    lses = [t.reshape(s, B_WIDTH) if d == 1 else t for t, d in zip(lses, DILATIONS)]
    ob, lse, mixed = _attn_combine(outs, lses, p["out_norm_b"], ma, "attn_combine" + tag)
    w_out_all = pl.BlockSpec((N_CHIPS, None, D_MODEL // N_CHIPS, D_MODEL), lambda i: (0, 0, 0, 0))
    y1 = _matmul(
        mixed, wg["w_out"], grid=(nm,), a_spec=pl.BlockSpec((TMM, D_MODEL), lambda i: (i, 0)), b_spec=w_out_all,
        o_spec=pl.BlockSpec((TMM, D_MODEL), lambda i: (i, 0)), o_shape=(s, D_MODEL), o_dtype=F32,
        dims=NN, nk=1, kaxis=None, acc_shape=None, name="mix_out" + tag, b_2d=(D_MODEL, D_MODEL))
    x1, h2 = _residual_norm(x0, y1, p["post_mix_norm"], p["pre_ffn_norm"], "post_mix" + tag)
    (y, up3, f), gathered = _ffn_forward(h2, wg["w_up"], wg["w_down"], p["cw3"], p["cb3"], "ffn_fwd" + tag, gather)
    saved = dict(x0=x0, h1=h1, proj=proj, q=q, k=k, v=v, ob=ob, lse=lse, mixed=mixed, y1=y1, x1=x1, h2=h2,
                 up3=up3, y=y, f=f)
    return saved, gathered


def _layer_backward(l, dx2, sv, p, wg, tabs, scatter=None):
    s = dx2.shape[0]
    nm = s // TMM
    tag = f"_l{l}"
    g = {}
    df, g["post_ffn_norm"] = _norm_bwd_out(dx2, sv["f"], p["post_ffn_norm"], "norm_bwd_out" + tag)
    (dup3, dh2, conv_sums), scattered = _ffn_backward(df, wg["w_up"], wg["w_down"], sv["up3"], p["cw3"], p["cb3"],
                                                      "ffn_bwd" + tag, scatter)
    sums = conv_sums.transpose(1, 2, 0, 3).reshape(2, 8, D_FF)
    g["conv_w"] = jnp.concatenate([sums[0, :3], sums[1, :3]], axis=1)
    g["conv_b"] = jnp.concatenate([sums[0, 3:4], sums[1, 3:4]], axis=1)
    tn = 1024
    gw_up = _matmul(
        sv["h2"], dup3, grid=(2 * D_FF // tn, nm), a_spec=pl.BlockSpec((TMM, D_MODEL), lambda n, m: (m, 0)),
        b_spec=pl.BlockSpec((None, TMM, tn), lambda n, m: (n // (D_FF // tn), m, n % (D_FF // tn))),
        o_spec=pl.BlockSpec((None, D_MODEL, tn), lambda n, m: (n // 2, 0, n % 2)),
        o_shape=(N_CHIPS, D_MODEL, 2 * D_FF // N_CHIPS), o_dtype=BF16,
        dims=TN, nk=nm, kaxis=1, acc_shape=(D_MODEL, tn), name="w_up_grad" + tag)
    gw_down = _matmul(
        sv["y"], df, grid=(D_FF // tn, 2, nm), a_spec=pl.BlockSpec((TMM, tn), lambda k, h, m: (m, k)),
        b_spec=pl.BlockSpec((TMM, D_MODEL // 2), lambda k, h, m: (m, h)),
        o_spec=pl.BlockSpec((None, tn, D_MODEL // 2), lambda k, h, m: (h, k, 0)),
        o_shape=(2, D_FF, D_MODEL // 2), o_dtype=BF16,
        dims=TN, nk=nm, kaxis=2, acc_shape=(tn, D_MODEL // 2), name="w_down_grad" + tag)
    dx1, dy1, g["pre_ffn_norm"], g["post_mix_norm"] = _norm_bwd_mid(
        dx2, dh2, sv["x1"], p["pre_ffn_norm"], sv["y1"], p["post_mix_norm"], "norm_bwd_mid" + tag)
    w_out_all = pl.BlockSpec((N_CHIPS, None, D_MODEL // N_CHIPS, D_MODEL), lambda i: (0, 0, 0, 0))
    dmixed = _matmul(
        dy1, wg["w_out"], grid=(nm,), a_spec=pl.BlockSpec((TMM, D_MODEL), lambda i: (i, 0)), b_spec=w_out_all,
        o_spec=pl.BlockSpec((TMM, D_MODEL), lambda i: (i, 0)), o_shape=(s, D_MODEL), o_dtype=F32,
        dims=NT, nk=1, kaxis=None, acc_shape=None, name="mix_out_bwd" + tag, b_2d=(D_MODEL, D_MODEL))
    gw_out = _matmul(
        sv["mixed"], dy1, grid=(2, nm), a_spec=pl.BlockSpec((TMM, D_MODEL), lambda h, m: (m, 0)),
        b_spec=pl.BlockSpec((TMM, D_MODEL // 2), lambda h, m: (m, h)),
        o_spec=pl.BlockSpec((None, D_MODEL, D_MODEL // 2), lambda h, m: (h, 0, 0)),
        o_shape=(2, D_MODEL, D_MODEL // 2), o_dtype=BF16,
        dims=TN, nk=nm, kaxis=1, acc_shape=(D_MODEL, D_MODEL // 2), name="w_out_grad" + tag)
    dpa, g["out_norm_a"], g["v_norm_g"], g["v_norm_b"], dbs, g["w_spatial"] = _mixer_a_bwd(
        sv["proj"], dmixed, p["v_norm_g"], p["v_norm_b"], p["w_spatial"], p["bs_full"], p["out_norm_a"],
        "mixer_a_bwd" + tag)
    g["b_spatial"] = dbs[:, ::GROUP_DIM].T
    dob, delta, g["out_norm_b"] = _attn_bwd_prep(dmixed, sv["ob"], p["out_norm_b"], "attn_bwd_prep" + tag)
    dqs, dks, dvs = zip(*[
        _attn_bwd(*(_as_classes(t[d]) for t in (sv["q"], sv["k"], sv["v"], dob, sv["lse"], delta)),
                  f"attn_bwd_d{d}" + tag) for d in DILATIONS])
    nat = lambda ts: [t.reshape(s, B_WIDTH) if d == 1 else t for t, d in zip(ts, DILATIONS)]
    dproj = _rope_bwd(nat(dqs), nat(dks), nat(dvs), tabs, dpa, "rope_bwd" + tag)
    wcol = IN_COLS // N_CHIPS
    dh1 = _matmul(
        dproj, wg["w_in"], grid=(nm, N_CHIPS), a_spec=pl.BlockSpec((TMM, wcol), lambda i, n: (i, n)),
        b_spec=_wspec(D_MODEL, wcol, lambda i, n: (n, 0, 0, 0)),
        o_spec=pl.BlockSpec((TMM, D_MODEL), lambda i, n: (i, 0)), o_shape=(s, D_MODEL), o_dtype=F32,
        dims=NT, nk=N_CHIPS, kaxis=1, acc_shape=(TMM, D_MODEL), name="proj_bwd" + tag)
    gw_in = _matmul(
        sv["h1"], dproj, grid=(N_CHIPS, nm), a_spec=pl.BlockSpec((TMM, D_MODEL), lambda n, m: (m, 0)),
        b_spec=pl.BlockSpec((TMM, wcol), lambda n, m: (m, n)),
        o_spec=pl.BlockSpec((None, D_MODEL, wcol), lambda n, m: (n, 0, 0)),
        o_shape=(N_CHIPS, D_MODEL, wcol), o_dtype=BF16,
        dims=TN, nk=nm, kaxis=1, acc_shape=(D_MODEL, wcol), name="w_in_grad" + tag)
    dx0, g["pre_mix_norm"] = _norm_bwd_in(dx1, dh1, sv["x0"], p["pre_mix_norm"], "norm_bwd_in" + tag)
    big = dict(w_in=gw_in, w_up=gw_up, w_out=gw_out, w_down=gw_down)
    return dx0, big, g, scattered


SMALL = ("pre_mix_norm", "v_norm_g", "v_norm_b", "w_spatial", "b_spatial", "out_norm_a", "out_norm_b",
         "post_mix_norm", "pre_ffn_norm", "conv_b", "post_ffn_norm")
BIG = ("w_in", "w_out", "w_up", "w_down")
DEPTH = 2


def _layer_params(l, small, conv_w_full):
    p = {n: small[n][l].reshape(1, -1) for n in SMALL if n not in ("w_spatial", "b_spatial")}
    p["w_spatial"] = small["w_spatial"][l]
    p["bs_full"] = jnp.repeat(small["b_spatial"][l].T, GROUP_DIM, axis=1)
    p["cw3"] = conv_w_full[l].reshape(3, 2, D_FF).transpose(1, 0, 2)
    p["cb3"] = small["conv_b"][l].reshape(2, 1, D_FF)
    return p


def _mesh_pos():
    return lax.axis_index("x"), lax.axis_index("y"), lax.axis_index("c")


def _other_chips(x, y):
    return [(1 - x, y), (x, 1 - y), (1 - x, 1 - y)]


def _gathered_shapes(blocks):
    return [jax.ShapeDtypeStruct((N_CHIPS, 1) + a.shape, a.dtype) for a in blocks]


def _gather_sems(nw):
    n = 2 * nw * (N_CHIPS - 1) + nw
    return [pltpu.SemaphoreType.DMA((n,)), pltpu.SemaphoreType.DMA((n,))]


def _gather_steps(ins, outs, send, recv):
    nw, nrel = len(ins), N_CHIPS - 1
    x, y, c = _mesh_pos()
    mine, sibling, chips = 2 * x + y, (x, y, 1 - c), _other_chips(x, y)

    def copy(src, dst, slot, to):
        return pltpu.make_async_remote_copy(src_ref=src, dst_ref=dst, send_sem=send.at[slot],
                                            recv_sem=recv.at[slot], device_id=to, device_id_type=MESH)

    def half_rows(t, core):
        rows = ins[t].shape[0] // 2
        return pl.ds(pl.multiple_of(core * rows, rows), rows)

    def landing(t, chip, core):
        return outs[t].at[chip, 0, half_rows(t, core), :]

    slots = [(t, r, chip) for t in range(nw) for r, chip in enumerate(chips)]
    own = [copy(ins[t], outs[t].at[mine, 0], 2 * nw * nrel + t, sibling) for t in range(nw)]
    first = [copy(ins[t].at[half_rows(t, c), :], landing(t, mine, c), t * nrel + r, (px, py, c))
             for t, r, (px, py) in slots]
    relays = [copy(landing(t, 2 * px + py, c), landing(t, 2 * px + py, c), nw * nrel + t * nrel + r, sibling)
              for t, r, (px, py) in slots]

    def start():
        for cp in own + first:
            cp.start()

    def relay():
        for (t, r, (px, py)), cp in zip(slots, relays):
            copy(landing(t, 2 * px + py, c), landing(t, 2 * px + py, c), t * nrel + r, (px, py, c)).wait_recv()
            cp.start()

    def finish():
        for t, r, (px, py) in slots:
            passed = landing(t, 2 * px + py, 1 - c)
            copy(passed, passed, nw * nrel + t * nrel + r, sibling).wait_recv()
        for cp in first + relays:
            cp.wait_send()
        for cp in own:
            cp.wait()

    return start, relay, finish


def _gather_weights(blocks, name):
    nw = len(blocks)

    def body(*refs):
        start, relay, finish = _gather_steps(refs[:nw], refs[nw:2 * nw], *refs[2 * nw:])
        start()
        relay()
        finish()

    return pl.pallas_call(
        body, in_specs=[ANY] * nw, out_specs=[ANY] * nw, out_shape=_gathered_shapes(blocks),
        scratch_shapes=_gather_sems(nw), name=name)(*blocks)


HALF_ROWS = D_MODEL // 2


def _pair_exchange(g, name):
    shapes = [(N_CHIPS, HALF_ROWS, IN_COLS // N_CHIPS), (N_CHIPS, HALF_ROWS, 2 * D_FF // N_CHIPS),
              (D_MODEL, D_MODEL // 2), (D_FF, D_MODEL // 2)]

    def body(gin, gup, gout, gdn, rin, rup, rout, rdn, send, recv):
        x, y, c = _mesh_pos()
        o = 1 - c
        rows = pl.ds(pl.multiple_of(o * HALF_ROWS, HALF_ROWS), HALF_ROWS)
        pairs = [(gin.at[:, rows, :], rin), (gup.at[:, rows, :], rup), (gout.at[o], rout), (gdn.at[o], rdn)]
        cps = [pltpu.make_async_remote_copy(src_ref=src, dst_ref=dst, send_sem=send.at[t], recv_sem=recv.at[t],
                                            device_id=(x, y, o), device_id_type=MESH)
               for t, (src, dst) in enumerate(pairs)]
        for cp in cps:
            cp.start()
        for cp in cps:
            cp.wait()

    return pl.pallas_call(
        body, in_specs=[ANY] * 4, out_specs=[ANY] * 4,
        out_shape=[jax.ShapeDtypeStruct(sh, BF16) for sh in shapes],
        scratch_shapes=[pltpu.SemaphoreType.DMA((4,)), pltpu.SemaphoreType.DMA((4,))],
        name=name)(g["w_in"], g["w_up"], g["w_out"], g["w_down"])


def _pair_sum(g, recv, pos, name_prefix):
    def add(a, b, grid, a_spec, b_spec, shape, name):
        def body(pos_ref, a_ref, b_ref, o_ref):
            o_ref[...] = (a_ref[...].astype(F32) + b_ref[...].astype(F32)).astype(BF16)

        return pl.pallas_call(
            body, grid_spec=pltpu.PrefetchScalarGridSpec(
                num_scalar_prefetch=1, grid=grid, in_specs=[a_spec, b_spec], out_specs=b_spec),
            out_shape=jax.ShapeDtypeStruct(shape, BF16), compiler_params=_cparams(*["parallel"] * len(grid)),
            name=name)(pos, a, b)

    rin, rup, rout, rdn = recv
    wi, wu = IN_COLS // N_CHIPS, 2 * D_FF // N_CHIPS
    s_in = add(g["w_in"], rin, (N_CHIPS,), pl.BlockSpec((None, HALF_ROWS, wi), lambda j, pos: (j, pos[2], 0)),
               pl.BlockSpec((None, HALF_ROWS, wi), lambda j, pos: (j, 0, 0)), rin.shape, name_prefix + "_in")
    s_up = add(g["w_up"], rup, (N_CHIPS,), pl.BlockSpec((None, HALF_ROWS, wu), lambda j, pos: (j, pos[2], 0)),
               pl.BlockSpec((None, HALF_ROWS, wu), lambda j, pos: (j, 0, 0)), rup.shape, name_prefix + "_up")
    hc = D_MODEL // 2
    s_out = add(g["w_out"], rout, (1,), pl.BlockSpec((None, D_MODEL, hc), lambda j, pos: (pos[2], 0, 0)),
                pl.BlockSpec((D_MODEL, hc), lambda j, pos: (0, 0)), rout.shape, name_prefix + "_out")
    s_dn = add(g["w_down"], rdn, (N_CHIPS,), pl.BlockSpec((None, D_FF // N_CHIPS, hc), lambda j, pos: (pos[2], j, 0)),
               pl.BlockSpec((D_FF // N_CHIPS, hc), lambda j, pos: (j, 0)), rdn.shape, name_prefix + "_down")
    return s_in, s_up, s_out, s_dn


OUT_ROWS = D_MODEL // N_CHIPS
DOWN_ROWS = D_FF // N_CHIPS


def _scattered_shapes():
    nrel = N_CHIPS - 1
    shapes = [(nrel, HALF_ROWS, IN_COLS // N_CHIPS), (nrel, HALF_ROWS, 2 * D_FF // N_CHIPS),
              (nrel, OUT_ROWS, D_MODEL // 2), (nrel, DOWN_ROWS, D_MODEL // 2)]
    return [jax.ShapeDtypeStruct(sh, BF16) for sh in shapes]


def _scatter_sems():
    n = 4 * (N_CHIPS - 1)
    return [pltpu.SemaphoreType.DMA((n,)), pltpu.SemaphoreType.DMA((n,))]


def _scatter_steps(sums, outs, send, recv):
    nrel = N_CHIPS - 1
    sin, sup, sout, sdn = sums
    x, y, c = _mesh_pos()
    cps = []
    for r, (px, py) in enumerate(_other_chips(x, y)):
        j = 2 * px + py
        pieces = [sin.at[j], sup.at[j], sout.at[pl.ds(pl.multiple_of(j * OUT_ROWS, OUT_ROWS), OUT_ROWS), :],
                  sdn.at[pl.ds(pl.multiple_of(j * DOWN_ROWS, DOWN_ROWS), DOWN_ROWS), :]]
        for t, src in enumerate(pieces):
            cps.append(pltpu.make_async_remote_copy(
                src_ref=src, dst_ref=outs[t].at[r], send_sem=send.at[t * nrel + r], recv_sem=recv.at[t * nrel + r],
                device_id=(px, py, c), device_id_type=MESH))

    def start():
        for cp in cps:
            cp.start()

    def finish():
        for cp in cps:
            cp.wait()

    return start, finish


def _chip_scatter(sums, name):
    def body(*refs):
        start, finish = _scatter_steps(refs[:4], refs[4:8], *refs[8:])
        start()
        finish()

    return pl.pallas_call(
        body, in_specs=[ANY] * 4, out_specs=[ANY] * 4, out_shape=_scattered_shapes(),
        scratch_shapes=_scatter_sems(), name=name)(*sums)


def _chip_sum(sums, recv, pos, name_prefix):
    def add(a, b, a_spec, shape, name):
        def body(pos_ref, a_ref, b_ref, o_ref):
            tot = a_ref[...].astype(F32)
            for r in range(N_CHIPS - 1):
                tot = tot + b_ref[r].astype(F32)
            o_ref[...] = tot

        return pl.pallas_call(
            body, grid_spec=pltpu.PrefetchScalarGridSpec(
                num_scalar_prefetch=1, grid=(1,), in_specs=[a_spec, pl.BlockSpec(b.shape, lambda i, pos: (0, 0, 0))],
                out_specs=pl.BlockSpec((None,) + shape, lambda i, pos: (pos[2], 0, 0))),
            out_shape=jax.ShapeDtypeStruct((2,) + shape, F32), compiler_params=_cparams("arbitrary"),
            name=name)(pos, a, b)

    s_in, s_up, s_out, s_dn = sums
    rin, rup, rout, rdn = recv
    wi, wu, hc = IN_COLS // N_CHIPS, 2 * D_FF // N_CHIPS, D_MODEL // 2
    chip = lambda pos: 2 * pos[0] + pos[1]
    t_in = add(s_in, rin, pl.BlockSpec((None, HALF_ROWS, wi), lambda i, pos: (chip(pos), 0, 0)), (HALF_ROWS, wi),
               name_prefix + "_in")
    t_up = add(s_up, rup, pl.BlockSpec((None, HALF_ROWS, wu), lambda i, pos: (chip(pos), 0, 0)), (HALF_ROWS, wu),
               name_prefix + "_up")
    t_out = add(s_out, rout, pl.BlockSpec((OUT_ROWS, hc), lambda i, pos: (chip(pos), 0)), (OUT_ROWS, hc),
                name_prefix + "_out")
    t_dn = add(s_dn, rdn, pl.BlockSpec((DOWN_ROWS, hc), lambda i, pos: (chip(pos), 0)), (DOWN_ROWS, hc),
               name_prefix + "_down")
    return t_in, t_up, t_out, t_dn


def _pair_share(totals, name):
    n = len(totals)

    def body(*refs):
        ins, outs = refs[:n], refs[n:2 * n]
        send, recv = refs[2 * n:]
        x, y, c = _mesh_pos()
        o = 1 - c
        cps = [pltpu.make_async_remote_copy(src_ref=ins[t].at[c], dst_ref=outs[t].at[c], send_sem=send.at[t],
                                            recv_sem=recv.at[t], device_id=(x, y, o), device_id_type=MESH)
               for t in range(n)]
        for cp in cps:
            cp.start()
        for t in range(n):
            pltpu.make_async_remote_copy(src_ref=ins[t].at[o], dst_ref=outs[t].at[o], send_sem=send.at[t],
                                         recv_sem=recv.at[t], device_id=(x, y, o), device_id_type=MESH).wait_recv()
        for cp in cps:
            cp.wait_send()

    return pl.pallas_call(
        body, in_specs=[ANY] * n, out_specs=[ANY] * n,
        out_shape=[jax.ShapeDtypeStruct(t.shape, t.dtype) for t in totals],
        scratch_shapes=[pltpu.SemaphoreType.DMA((n,)), pltpu.SemaphoreType.DMA((n,))],
        input_output_aliases={t: t for t in range(n)}, name=name)(*totals)


def _chip_sums(l, g, pos):
    recv = _pair_exchange(g, f"pair_exchange_l{l}")
    return _pair_sum(g, recv, pos, f"pair_sum_l{l}")


def _gradient_shards(l, sums, scattered, pos):
    totals = _chip_sum(sums, scattered, pos, f"chip_sum_l{l}")
    f_in, f_up, f_out, f_dn = _pair_share(totals, f"pair_share_l{l}")
    f_in = f_in.reshape(D_MODEL, IN_COLS // N_CHIPS)
    f_up = f_up.reshape(D_MODEL, 2 * D_FF // N_CHIPS)
    f_out = f_out.transpose(1, 0, 2).reshape(OUT_ROWS, D_MODEL)
    f_dn = f_dn.transpose(1, 0, 2).reshape(DOWN_ROWS, D_MODEL)
    return dict(w_in=f_in, w_up=f_up, w_out=f_out, w_down=f_dn)


N_DEV = 8


def _allreduce_small(packed, name):
    rows = packed.shape[0]

    def body(x_ref, out_ref, gath, send_sems, recv_sems, local_sem):
        x, y, c = _mesh_pos()
        me, sibling = (x, y, c), (x, y, 1 - c)
        chips = _other_chips(x, y)

        def blk(px, py, pc):
            return gath.at[pl.ds(pl.multiple_of((4 * px + 2 * py + pc) * rows, 8), rows), :]

        def copy(k, block, to, src=None):
            return pltpu.make_async_remote_copy(
                src_ref=blk(*block) if src is None else src, dst_ref=blk(*block), send_sem=send_sems.at[k],
                recv_sem=recv_sems.at[k], device_id=to, device_id_type=MESH)

        mine = pltpu.make_async_copy(x_ref, blk(*me), local_sem)
        mine.start()
        first = [copy(0, me, sibling, src=x_ref)]
        first += [copy(1 + j, me, (*chip, c), src=x_ref) for j, chip in enumerate(chips)]
        for cp in first:
            cp.start()
        passed = [copy(4 + j, (*chip, c), sibling) for j, chip in enumerate(chips)]
        for j, chip in enumerate(chips):
            copy(1 + j, (*chip, c), me).wait_recv()
            passed[j].start()
        copy(0, sibling, me).wait_recv()
        for j, chip in enumerate(chips):
            copy(4 + j, (*chip, 1 - c), me).wait_recv()
        for cp in first + passed:
            cp.wait_send()
        mine.wait()
        tot = gath[0:rows, :]
        for d in range(1, N_DEV):
            tot = tot + gath[d * rows:(d + 1) * rows, :]
        out_ref[...] = tot

    vmem = pl.BlockSpec(memory_space=pltpu.VMEM)
    return pl.pallas_call(
        body, in_specs=[vmem], out_specs=vmem, out_shape=jax.ShapeDtypeStruct((rows, LANES), F32),
        scratch_shapes=[pltpu.VMEM((N_DEV * rows, LANES), F32), pltpu.SemaphoreType.DMA((7,)),
                        pltpu.SemaphoreType.DMA((7,)), pltpu.SemaphoreType.DMA],
        compiler_params=pltpu.CompilerParams(vmem_limit_bytes=VMEM_LIMIT_BYTES),
        name=name)(packed)


def _adamw(w, g, m, v, name):
    rows, cols = w.shape
    tr = 256 if rows % 256 == 0 else rows

    def body(w_ref, g_ref, m_ref, v_ref, d_ref, mo_ref, vo_ref):
        gv = g_ref[...]
        mn = ADAM_B1 * m_ref[...] + (1.0 - ADAM_B1) * gv
        vn = ADAM_B2 * v_ref[...] + (1.0 - ADAM_B2) * (gv * gv)
        m_hat = mn / (1.0 - ADAM_B1 ** ADAM_STEP)
        v_hat = vn / (1.0 - ADAM_B2 ** ADAM_STEP)
        d_ref[...] = -ADAM_LR * (m_hat / (jnp.sqrt(v_hat) + ADAM_EPS) + ADAM_WD * w_ref[...])
        mo_ref[...] = mn
        vo_ref[...] = vn

    spec = pl.BlockSpec((tr, cols), lambda i: (i, 0))
    return pl.pallas_call(
        body, grid=(rows // tr,), in_specs=[spec] * 4, out_specs=[spec] * 3,
        out_shape=[jax.ShapeDtypeStruct((rows, cols), F32)] * 3, compiler_params=_cparams("parallel"),
        name=name)(w, g, m, v)


def _adamw_nd(w, g, m, v, name):
    cols = w.shape[-1] if w.shape[-1] % LANES == 0 else LANES
    outs = _adamw(*(t.reshape(-1, cols) for t in (w, g, m, v)), name)
    return tuple(t.reshape(w.shape) for t in outs)


def _pack(arrays):
    return jnp.concatenate([a.reshape(-1, LANES) for a in arrays], axis=0)


def _unpack(packed, shapes):
    out, row = [], 0
    for sh in shapes:
        n = math.prod(sh) // LANES
        out.append(packed[row:row + n].reshape(sh))
        row += n
    return out


WEIGHTS = ("pre_mix_norm", "w_in", "v_norm_g", "v_norm_b", "w_spatial", "b_spatial", "out_norm_a", "out_norm_b",
           "w_out", "post_mix_norm", "pre_ffn_norm", "w_up", "conv_w", "conv_b", "w_down", "post_ffn_norm")


def kernel(x, pre_mix_norm, w_in, v_norm_g, v_norm_b, w_spatial, b_spatial, out_norm_a, out_norm_b, w_out, post_mix_norm, pre_ffn_norm, w_up, conv_w, conv_b, w_down, post_ffn_norm, loss_target, m_pre_mix_norm, m_w_in, m_v_norm_g, m_v_norm_b, m_w_spatial, m_b_spatial, m_out_norm_a, m_out_norm_b, m_w_out, m_post_mix_norm, m_pre_ffn_norm, m_w_up, m_conv_w, m_conv_b, m_w_down, m_post_ffn_norm, v_pre_mix_norm, v_w_in, v_v_norm_g, v_v_norm_b, v_w_spatial, v_b_spatial, v_out_norm_a, v_out_norm_b, v_w_out, v_post_mix_norm, v_pre_ffn_norm, v_w_up, v_conv_w, v_conv_b, v_w_down, v_post_ffn_norm):
    w = dict(pre_mix_norm=pre_mix_norm, w_in=w_in, v_norm_g=v_norm_g, v_norm_b=v_norm_b, w_spatial=w_spatial,
             b_spatial=b_spatial, out_norm_a=out_norm_a, out_norm_b=out_norm_b, w_out=w_out,
             post_mix_norm=post_mix_norm, pre_ffn_norm=pre_ffn_norm, w_up=w_up, conv_w=conv_w, conv_b=conv_b,
             w_down=w_down, post_ffn_norm=post_ffn_norm)
    m = dict(pre_mix_norm=m_pre_mix_norm, w_in=m_w_in, v_norm_g=m_v_norm_g, v_norm_b=m_v_norm_b,
             w_spatial=m_w_spatial, b_spatial=m_b_spatial, out_norm_a=m_out_norm_a, out_norm_b=m_out_norm_b,
             w_out=m_w_out, post_mix_norm=m_post_mix_norm, pre_ffn_norm=m_pre_ffn_norm, w_up=m_w_up,
             conv_w=m_conv_w, conv_b=m_conv_b, w_down=m_w_down, post_ffn_norm=m_post_ffn_norm)
    v = dict(pre_mix_norm=v_pre_mix_norm, w_in=v_w_in, v_norm_g=v_v_norm_g, v_norm_b=v_v_norm_b,
             w_spatial=v_w_spatial, b_spatial=v_b_spatial, out_norm_a=v_out_norm_a, out_norm_b=v_out_norm_b,
             w_out=v_w_out, post_mix_norm=v_post_mix_norm, pre_ffn_norm=v_pre_ffn_norm, w_up=v_w_up,
             conv_w=v_conv_w, conv_b=v_conv_b, w_down=v_w_down, post_ffn_norm=v_post_ffn_norm)
    pos = jnp.stack([lax.axis_index("x"), lax.axis_index("y"), lax.axis_index("c")]).astype(jnp.int32)
    chip = 2 * lax.axis_index("x") + lax.axis_index("y")

    cw_cols = conv_w.shape[-1]
    cw_slab = lax.dynamic_update_slice(jnp.zeros((DEPTH, 3, 2 * D_FF), F32), conv_w, (0, 0, chip * cw_cols))
    conv_w_full = _allreduce_small(cw_slab.reshape(-1, LANES), "gather_conv_w").reshape(DEPTH, 3, 2 * D_FF)
    conv_w_full = conv_w_full * 0.5
    blocks = [[w[n][l].astype(BF16) for n in BIG] for l in range(DEPTH)]
    wg = dict(zip(BIG, _gather_weights(blocks[0], "gather_weights_l0")))

    small = {n: w[n] for n in SMALL}
    xs, target = x[0], loss_target[0]
    tabs = _rope_tables(xs.shape[0])
    params = [_layer_params(l, small, conv_w_full) for l in range(DEPTH)]
    saved, wgs = [], []
    xin = xs
    h = _rms_cast(xin, params[0]["pre_mix_norm"], "pre_mix_l0")
    for l in range(DEPTH):
        sv, gathered = _layer_forward(l, xin, h, params[l], wg, tabs, blocks[l + 1] if l + 1 < DEPTH else None)
        saved.append(sv)
        wgs.append(wg)
        if l + 1 < DEPTH:
            wg = dict(zip(BIG, gathered))
            xin, h = _residual_norm(sv["x1"], sv["f"], params[l]["post_ffn_norm"], params[l + 1]["pre_mix_norm"],
                                    f"post_ffn_l{l}")
    loss_part, dx = _residual_loss(saved[-1]["x1"], saved[-1]["f"], params[-1]["post_ffn_norm"], target, "loss")
    smalls, shards = [None] * DEPTH, [None] * DEPTH
    pending = None
    for l in reversed(range(DEPTH)):
        dx, big, smalls[l], scattered = _layer_backward(l, dx, saved[l], params[l], wgs[l], tabs,
                                                        pending[1] if pending else None)
        if pending:
            shards[pending[0]] = _gradient_shards(pending[0], pending[1], scattered, pos)
        pending = (l, _chip_sums(l, big, pos))
    shards[pending[0]] = _gradient_shards(pending[0], pending[1],
                                          _chip_scatter(pending[1], f"chip_scatter_l{pending[0]}"), pos)

    small_shapes = [w[n].shape for n in SMALL]
    stacked = [jnp.stack([smalls[l][n].reshape(w[n].shape[1:]) for l in range(DEPTH)]) for n in SMALL]
    cw_grad = jnp.stack([smalls[l]["conv_w"] for l in range(DEPTH)])
    packed = _pack(stacked + [cw_grad, loss_part])
    total = _allreduce_small(packed, "allreduce_small")
    parts = _unpack(total, small_shapes + [cw_grad.shape, (8, LANES)])
    g_small = dict(zip(SMALL, parts[:len(SMALL)]))
    loss = parts[-1][0, 0]
    g_conv_w = lax.dynamic_slice(parts[-2], (0, 0, chip * cw_cols), conv_w.shape)

    grads = {n: jnp.stack([shards[l][n] for l in range(DEPTH)]) for n in BIG}
    grads.update(g_small)
    grads["conv_w"] = g_conv_w

    dp, mp, vp = _adamw(_pack([w[n] for n in SMALL]), _pack([g_small[n] for n in SMALL]),
                        _pack([m[n] for n in SMALL]), _pack([v[n] for n in SMALL]), "adamw_small")
    delta = dict(zip(SMALL, _unpack(dp, small_shapes)))
    new_m = dict(zip(SMALL, _unpack(mp, small_shapes)))
    new_v = dict(zip(SMALL, _unpack(vp, small_shapes)))
    for n in BIG + ("conv_w",):
        delta[n], new_m[n], new_v[n] = _adamw_nd(w[n], grads[n], m[n], v[n], "adamw_" + n)

    return (loss, dx[None], *[grads[n] for n in WEIGHTS], *[delta[n] for n in WEIGHTS],
            *[new_m[n] for n in WEIGHTS], *[new_v[n] for n in WEIGHTS])
```

```python
import functools
import math

import jax
import jax.numpy as jnp
import numpy as np
from jax import lax
from jax.experimental import pallas as pl
from jax.experimental.pallas import tpu as pltpu

F32 = jnp.float32
BF16 = jnp.bfloat16
MESH = pl.DeviceIdType.MESH

D_MODEL = 1024
A_WIDTH = 512
A_GROUPS = 4
GROUP_DIM = 128
CHUNK = 128
B_WIDTH = 512
HEAD_DIM = 64
ROT_DIM = 16
ROPE_THETA = 500000.0
DILATIONS = (1, 4, 16)
BAND = 128
IN_COLS = 2560
D_FF = 4096
EPS = 1e-6
NEG_INF = -1e30
N_CHIPS = 4
LANES = 128

ADAM_LR = 0.001
ADAM_B1 = 0.9
ADAM_B2 = 0.999
ADAM_EPS = 1e-08
ADAM_WD = 0.01
ADAM_STEP = 10

VMEM_LIMIT_BYTES = 56 * 1024 * 1024
RSQRT2 = 0.7071067811865476
INV_SQRT_2PI = 0.3989422804014327
GELU_C = 0.7978845608028654
GELU_A = 0.044715

ANY = pl.BlockSpec(memory_space=pl.ANY)
NN = ((1,), (0,))
NT = ((1,), (1,))
TN = ((0,), (0,))


def _cparams(*sem):
    return pltpu.CompilerParams(dimension_semantics=sem, vmem_limit_bytes=VMEM_LIMIT_BYTES)


def _dot(a, b, dims):
    return lax.dot_general(a, b, (dims, ((), ())), preferred_element_type=F32)


def _rsq_mean(a):
    return lax.rsqrt(jnp.mean(a * a, axis=-1, keepdims=True) + EPS)


def _rms_bwd(a, r, g, dz):
    t = dz * g
    da = r * t - a * (r * r * r) * jnp.mean(t * a, axis=-1, keepdims=True)
    return da, dz * a * r


def _colsum(a):
    return jnp.sum(a, axis=0, keepdims=True)


def _gelu_tanh(x):
    u = x * x
    t = jnp.tanh(x * (GELU_C + (GELU_C * GELU_A) * u))
    hx = 0.5 * x
    act = hx + hx * t
    grad = 0.5 + 0.5 * t + (hx - hx * t * t) * (GELU_C + (3.0 * GELU_C * GELU_A) * u)
    return act, grad


def _matmul(a, b, *, grid, a_spec, b_spec, o_spec, o_shape, o_dtype, dims, nk, kaxis, acc_shape, name, b_2d=None):
    def body(a_ref, b_ref, o_ref, *scratch):
        bv = b_ref[...] if b_2d is None else b_ref[...].reshape(b_2d)
        part = _dot(a_ref[...], bv, dims)
        if nk == 1:
            o_ref[...] = part.astype(o_dtype)
        else:
            acc = scratch[0]
            k = pl.program_id(kaxis)

            @pl.when(k == 0)
            def _():
                acc[...] = part

            @pl.when(k > 0)
            def _():
                acc[...] += part

            @pl.when(k == nk - 1)
            def _():
                o_ref[...] = acc[...].astype(o_dtype)

    sem = tuple("arbitrary" if (nk > 1 and ax == kaxis) else "parallel" for ax in range(len(grid)))
    return pl.pallas_call(
        body, grid=grid, in_specs=[a_spec, b_spec], out_specs=o_spec,
        out_shape=jax.ShapeDtypeStruct(o_shape, o_dtype),
        scratch_shapes=[pltpu.VMEM(acc_shape, F32)] if nk > 1 else [],
        compiler_params=_cparams(*sem), name=name)(a, b)


TM = 512
TMM = 1024


TR = 256


def _row_spec(width, col=0):
    return pl.BlockSpec((TR, width), lambda i, col=col: (i, col))


def _vec_spec(width):
    return pl.BlockSpec((1, width), lambda i: (0, 0))


def _rms_cast(x, g, name):
    s, d = x.shape

    def body(x_ref, g_ref, h_ref):
        a = x_ref[...]
        h_ref[...] = (a * _rsq_mean(a) * g_ref[...]).astype(BF16)

    return pl.pallas_call(
        body, grid=(s // TR,), in_specs=[_row_spec(d), _vec_spec(d)], out_specs=_row_spec(d),
        out_shape=jax.ShapeDtypeStruct((s, d), BF16), compiler_params=_cparams("parallel"), name=name)(x, g)


def _residual_norm(x0, y, g_post, g_next, name):
    s, d = x0.shape

    def body(x_ref, y_ref, gp_ref, gn_ref, x1_ref, h_ref):
        yv = y_ref[...]
        x1 = x_ref[...] + yv * _rsq_mean(yv) * gp_ref[...]
        x1_ref[...] = x1
        h_ref[...] = (x1 * _rsq_mean(x1) * gn_ref[...]).astype(BF16)

    return pl.pallas_call(
        body, grid=(s // TR,), in_specs=[_row_spec(d), _row_spec(d), _vec_spec(d), _vec_spec(d)],
        out_specs=[_row_spec(d), _row_spec(d)],
        out_shape=[jax.ShapeDtypeStruct((s, d), F32), jax.ShapeDtypeStruct((s, d), BF16)],
        compiler_params=_cparams("parallel"), name=name)(x0, y, g_post, g_next)


def _residual_loss(x1, f, g_post, target, name):
    s, d = x1.shape

    def body(x_ref, f_ref, gp_ref, t_ref, loss_ref, dx_ref):
        fv = f_ref[...]
        err = x_ref[...] + fv * _rsq_mean(fv) * gp_ref[...] - t_ref[...]
        dx_ref[...] = err * (1.0 / d)
        part = 0.5 * jnp.sum(jnp.mean(err * err, axis=-1, keepdims=True), axis=0, keepdims=True)

        @pl.when(pl.program_id(0) == 0)
        def _():
            loss_ref[...] = jnp.zeros_like(loss_ref)

        loss_ref[...] += jnp.broadcast_to(part, loss_ref.shape)

    return pl.pallas_call(
        body, grid=(s // TR,), in_specs=[_row_spec(d), _row_spec(d), _vec_spec(d), _row_spec(d)],
        out_specs=[pl.BlockSpec((8, LANES), lambda i: (0, 0)), _row_spec(d)],
        out_shape=[jax.ShapeDtypeStruct((8, LANES), F32), jax.ShapeDtypeStruct((s, d), F32)],
        compiler_params=_cparams("arbitrary"), name=name)(x1, f, g_post, target)


def _acc_init(refs):
    @pl.when(pl.program_id(0) == 0)
    def _():
        for r in refs:
            r[...] = jnp.zeros_like(r)


def _norm_bwd_out(dx, f, g_post, name):
    s, d = dx.shape

    def body(dx_ref, f_ref, g_ref, df_ref, dg_ref):
        _acc_init([dg_ref])
        fv = f_ref[...]
        dz = dx_ref[...]
        da, dgt = _rms_bwd(fv, _rsq_mean(fv), g_ref[...], dz)
        df_ref[...] = da.astype(BF16)
        dg_ref[...] += _colsum(dgt)

    return pl.pallas_call(
        body, grid=(s // TR,), in_specs=[_row_spec(d), _row_spec(d), _vec_spec(d)],
        out_specs=[_row_spec(d), _vec_spec(d)],
        out_shape=[jax.ShapeDtypeStruct((s, d), BF16), jax.ShapeDtypeStruct((1, d), F32)],
        compiler_params=_cparams("arbitrary"), name=name)(dx, f, g_post)


def _norm_bwd_mid(dx2, dh2, x1, g_pf, y1, g_pm, name):
    s, d = dx2.shape

    def body(dx2_ref, dh_ref, x1_ref, gpf_ref, y1_ref, gpm_ref, dx1_ref, dy1_ref, dgpf_ref, dgpm_ref):
        _acc_init([dgpf_ref, dgpm_ref])
        x1 = x1_ref[...]
        da, dgt = _rms_bwd(x1, _rsq_mean(x1), gpf_ref[...], dh_ref[...])
        dx1 = dx2_ref[...] + da
        dx1_ref[...] = dx1
        dgpf_ref[...] += _colsum(dgt)
        y1 = y1_ref[...]
        dy, dgt2 = _rms_bwd(y1, _rsq_mean(y1), gpm_ref[...], dx1)
        dy1_ref[...] = dy.astype(BF16)
        dgpm_ref[...] += _colsum(dgt2)

    return pl.pallas_call(
        body, grid=(s // TR,),
        in_specs=[_row_spec(d), _row_spec(d), _row_spec(d), _vec_spec(d), _row_spec(d), _vec_spec(d)],
        out_specs=[_row_spec(d), _row_spec(d), _vec_spec(d), _vec_spec(d)],
        out_shape=[jax.ShapeDtypeStruct((s, d), F32), jax.ShapeDtypeStruct((s, d), BF16),
                   jax.ShapeDtypeStruct((1, d), F32), jax.ShapeDtypeStruct((1, d), F32)],
        compiler_params=_cparams("arbitrary"), name=name)(dx2, dh2, x1, g_pf, y1, g_pm)


def _norm_bwd_in(dx1, dh1, x0, g1, name):
    s, d = dx1.shape

    def body(dx1_ref, dh_ref, x0_ref, g_ref, dx0_ref, dg_ref):
        _acc_init([dg_ref])
        x0 = x0_ref[...]
        da, dgt = _rms_bwd(x0, _rsq_mean(x0), g_ref[...], dh_ref[...])
        dx0_ref[...] = dx1_ref[...] + da
        dg_ref[...] += _colsum(dgt)

    return pl.pallas_call(
        body, grid=(s // TR,), in_specs=[_row_spec(d), _row_spec(d), _row_spec(d), _vec_spec(d)],
        out_specs=[_row_spec(d), _vec_spec(d)],
        out_shape=[jax.ShapeDtypeStruct((s, d), F32), jax.ShapeDtypeStruct((1, d), F32)],
        compiler_params=_cparams("arbitrary"), name=name)(dx1, dh1, x0, g1)


def _tril_mask():
    row = lax.broadcasted_iota(jnp.int32, (CHUNK, CHUNK), 0)
    col = lax.broadcasted_iota(jnp.int32, (CHUNK, CHUNK), 1)
    return row >= col


def _gating_forward(pa, gv, bv, wt, bsf):
    er = lax.erf(pa * RSQRT2)
    za = 0.5 * pa * (1.0 + er)
    u = za[:, :A_WIDTH]
    va = za[:, A_WIDTH:]
    xc = va - jnp.mean(va, axis=-1, keepdims=True)
    rs = lax.rsqrt(jnp.mean(xc * xc, axis=-1, keepdims=True) + EPS)
    vn = xc * rs
    vlb = (vn * gv + bv).astype(BF16)
    sg = jnp.concatenate(
        [_dot(wt[g], vlb[:, g * GROUP_DIM:(g + 1) * GROUP_DIM], NN) for g in range(A_GROUPS)], axis=1) + bsf
    return er, u, rs, vn, vlb, sg


def _masked_ws(ws_ref):
    mask = _tril_mask()
    return [jnp.where(mask, ws_ref[g], 0.0).astype(BF16) for g in range(A_GROUPS)]


def _mixer_a_fwd(proj, gv, bv, ws, bsf, ga, name):
    s = proj.shape[0]

    def body(p_ref, gv_ref, bv_ref, ws_ref, bs_ref, ga_ref, o_ref):
        wt = _masked_ws(ws_ref)
        for ch in range(TR // CHUNK):
            rows = slice(ch * CHUNK, (ch + 1) * CHUNK)
            _, u, _, _, _, sg = _gating_forward(p_ref[rows, :], gv_ref[...], bv_ref[...], wt, bs_ref[...])
            oa = u * sg
            o_ref[rows, :] = (oa * _rsq_mean(oa) * ga_ref[...]).astype(BF16)

    return pl.pallas_call(
        body, grid=(s // TR,),
        in_specs=[_row_spec(2 * A_WIDTH), _vec_spec(A_WIDTH), _vec_spec(A_WIDTH),
                  pl.BlockSpec((A_GROUPS, CHUNK, CHUNK), lambda i: (0, 0, 0)),
                  pl.BlockSpec((CHUNK, A_WIDTH), lambda i: (0, 0)), _vec_spec(A_WIDTH)],
        out_specs=_row_spec(A_WIDTH), out_shape=jax.ShapeDtypeStruct((s, A_WIDTH + B_WIDTH), BF16),
        compiler_params=_cparams("parallel"), name=name)(proj, gv, bv, ws, bsf, ga)


def _mixer_a_bwd(proj, dmixed, gv, bv, ws, bsf, ga, name):
    s = proj.shape[0]
    nsteps = s // TR

    def body(p_ref, dm_ref, gv_ref, bv_ref, ws_ref, bs_ref, ga_ref,
             dp_ref, dga_ref, dgv_ref, dbv_ref, dbs_ref, dws_ref):
        _acc_init([dga_ref, dgv_ref, dbv_ref, dbs_ref, dws_ref])
        mask = _tril_mask()
        wt = _masked_ws(ws_ref)
        gvv = gv_ref[...]
        gav = ga_ref[...]
        for ch in range(TR // CHUNK):
            rows = slice(ch * CHUNK, (ch + 1) * CHUNK)
            pa = p_ref[rows, :]
            er, u, rs, vn, vlb, sg = _gating_forward(pa, gvv, bv_ref[...], wt, bs_ref[...])
            oa = u * sg
            doa, dgt = _rms_bwd(oa, _rsq_mean(oa), gav, dm_ref[rows, :])
            dga_ref[...] += _colsum(dgt)
            du = doa * sg
            dsg = doa * u
            dbs_ref[...] += dsg
            dsgb = dsg.astype(BF16)
            dvl = []
            for g in range(A_GROUPS):
                cols = slice(g * GROUP_DIM, (g + 1) * GROUP_DIM)
                dws_ref[g] += jnp.where(mask, _dot(dsgb[:, cols], vlb[:, cols], NT), 0.0)
                dvl.append(_dot(wt[g], dsgb[:, cols], TN))
            dvl = jnp.concatenate(dvl, axis=1)
            dgv_ref[...] += _colsum(dvl * vn)
            dbv_ref[...] += _colsum(dvl)
            dvn = dvl * gvv
            dva = rs * (dvn - jnp.mean(dvn, axis=-1, keepdims=True)
                        - vn * jnp.mean(dvn * vn, axis=-1, keepdims=True))
            gp = 0.5 * (1.0 + er) + pa * jnp.exp(-0.5 * pa * pa) * INV_SQRT_2PI
            dp_ref[rows, :] = (jnp.concatenate([du, dva], axis=1) * gp).astype(BF16)

        @pl.when(pl.program_id(0) == nsteps - 1)
        def _():
            for g in range(A_GROUPS):
                cols = slice(g * GROUP_DIM, (g + 1) * GROUP_DIM)
                tot = jnp.sum(dbs_ref[:, cols], axis=1, keepdims=True)
                dbs_ref[:, cols] = jnp.broadcast_to(tot, (CHUNK, GROUP_DIM))

    full = lambda *shape: pl.BlockSpec(shape, lambda i: (0,) * len(shape))
    return pl.pallas_call(
        body, grid=(nsteps,),
        in_specs=[_row_spec(2 * A_WIDTH), _row_spec(A_WIDTH), _vec_spec(A_WIDTH), _vec_spec(A_WIDTH),
                  full(A_GROUPS, CHUNK, CHUNK), full(CHUNK, A_WIDTH), _vec_spec(A_WIDTH)],
        out_specs=[_row_spec(2 * A_WIDTH), _vec_spec(A_WIDTH), _vec_spec(A_WIDTH), _vec_spec(A_WIDTH),
                   full(CHUNK, A_WIDTH), full(A_GROUPS, CHUNK, CHUNK)],
        out_shape=[jax.ShapeDtypeStruct((s, IN_COLS), BF16), jax.ShapeDtypeStruct((1, A_WIDTH), F32),
                   jax.ShapeDtypeStruct((1, A_WIDTH), F32), jax.ShapeDtypeStruct((1, A_WIDTH), F32),
                   jax.ShapeDtypeStruct((CHUNK, A_WIDTH), F32),
                   jax.ShapeDtypeStruct((A_GROUPS, CHUNK, CHUNK), F32)],
        compiler_params=_cparams("arbitrary"), name=name)(proj, dmixed, gv, bv, ws, bsf, ga)


def _rope_tables(s):
    half = ROT_DIM // 2
    inv = ROPE_THETA ** (-jnp.arange(0, ROT_DIM, 2, dtype=F32) / ROT_DIM)
    ang = jnp.arange(s, dtype=F32)[:, None] * inv[None, :]
    cos, sin = jnp.cos(ang), jnp.sin(ang)
    zeros = jnp.zeros((s, HEAD_DIM - ROT_DIM), F32)
    zh = jnp.zeros((s, half), F32)
    c = jnp.concatenate([cos, cos, zeros + 1.0], axis=1)
    s1 = jnp.concatenate([-sin, zh, zeros], axis=1)
    s2 = jnp.concatenate([zh, sin, zeros], axis=1)
    return tuple(jnp.concatenate([t, t], axis=1) for t in (c, s1, s2))


def _lane_blocks(width):
    return [slice(b * LANES, (b + 1) * LANES) for b in range(width // LANES)]


CLASS_DILS = tuple(d for d in DILATIONS if d > 1)


def _class_shape(s, dil, dtype):
    return jax.ShapeDtypeStruct((dil, s // dil, B_WIDTH), dtype)


def _class_spec(dil):
    return pl.BlockSpec((dil, TR // dil, B_WIDTH), lambda i, *_: (0, i, 0))


NBLK = B_WIDTH // LANES
STAGE = pltpu.VMEM((NBLK, TR, LANES), F32)


def _stage_put(stage, value):
    for b, sl in enumerate(_lane_blocks(B_WIDTH)):
        stage[b] = value[:, sl]


def _stage_get(stage):
    return jnp.concatenate([stage[b] for b in range(NBLK)], axis=1)


def _store_classes(stage, dst_ref, dil):
    for b, sl in enumerate(_lane_blocks(B_WIDTH)):
        for r in range(dil):
            dst_ref[r, :, sl] = stage[b, pl.ds(r, TR // dil, stride=dil), :].astype(dst_ref.dtype)


def _load_classes(src_ref, stage, dil):
    for b, sl in enumerate(_lane_blocks(B_WIDTH)):
        for r in range(dil):
            stage[b, pl.ds(r, TR // dil, stride=dil), :] = src_ref[r, :, sl].astype(F32)
    return _stage_get(stage)


def _rope_fwd(proj, tabs, name):
    s = proj.shape[0]
    half = ROT_DIM // 2
    scale = HEAD_DIM ** -0.5
    nlay = 1 + len(CLASS_DILS)

    def body(q_ref, k_ref, v_ref, c_ref, s1_ref, s2_ref, *rest):
        outs, stage = rest[:3 * nlay], rest[3 * nlay]
        c, s1, s2 = c_ref[...], s1_ref[...], s2_ref[...]
        for which, (src, mul) in enumerate(((q_ref, scale), (k_ref, 1.0), (v_ref, None))):
            if mul is None:
                _stage_put(stage, src[...])
            else:
                for b, sl in enumerate(_lane_blocks(B_WIDTH)):
                    a = src[:, sl]
                    r = a * c + pltpu.roll(a, LANES - half, 1) * s1 + pltpu.roll(a, half, 1) * s2
                    stage[b] = r * mul
            dst = outs[which * nlay:(which + 1) * nlay]
            dst[0][...] = _stage_get(stage).astype(BF16)
            for ref, d in zip(dst[1:], CLASS_DILS):
                _store_classes(stage, ref, d)

    tab = pl.BlockSpec((TR, LANES), lambda i: (i, 0))
    lay_specs = [_row_spec(B_WIDTH)] + [_class_spec(d) for d in CLASS_DILS]
    lay_shapes = [jax.ShapeDtypeStruct((s, B_WIDTH), BF16)] + [_class_shape(s, d, BF16) for d in CLASS_DILS]
    outs = pl.pallas_call(
        body, grid=(s // TR,),
        in_specs=[_row_spec(B_WIDTH, 2), _row_spec(B_WIDTH, 3), _row_spec(B_WIDTH, 4), tab, tab, tab],
        out_specs=lay_specs * 3, out_shape=lay_shapes * 3, scratch_shapes=[STAGE],
        compiler_params=_cparams("parallel"), name=name)(proj, proj, proj, *tabs)
    q, k, v = (dict(zip(DILATIONS, outs[w * nlay:(w + 1) * nlay])) for w in range(3))
    return q, k, v


def _as_classes(t):
    return t if t.ndim == 3 else t[None]


def _band_mask(i):
    qi = lax.broadcasted_iota(jnp.int32, (BAND, 2 * BAND), 0)
    kj = lax.broadcasted_iota(jnp.int32, (BAND, 2 * BAND), 1)
    return (kj >= qi) & (kj <= qi + BAND) & ((kj >= BAND) | (i > 0))


def _head_masks():
    lane = lax.broadcasted_iota(jnp.int32, (1, LANES), 1)
    return lane < HEAD_DIM, lane >= HEAD_DIM


def _stack_heads(t):
    lo, hi = _head_masks()
    zero = jnp.zeros_like(t)
    return jnp.concatenate([jnp.where(lo, t, zero), jnp.where(hi, t, zero)], axis=0)


def _attn_specs(last):
    cur = pl.BlockSpec((None, BAND, B_WIDTH), lambda r, i: (r, jnp.minimum(i, last), 0))
    prev = pl.BlockSpec((None, BAND, B_WIDTH), lambda r, i: (r, jnp.maximum(jnp.minimum(i, last) - 1, 0), 0))
    return cur, prev


def _attn_fwd(q, k, v, name):
    dil, n, _ = q.shape
    nb = n // BAND

    def body(q_ref, kc_ref, kp_ref, vc_ref, vp_ref, o_ref, l_ref):
        valid = _band_mask(pl.program_id(1))
        valid = jnp.concatenate([valid, valid], axis=0)
        lo, _ = _head_masks()
        for sl in _lane_blocks(B_WIDTH):
            kk = jnp.concatenate([kp_ref[:, sl], kc_ref[:, sl]], axis=0)
            vv = jnp.concatenate([vp_ref[:, sl], vc_ref[:, sl]], axis=0)
            sc = jnp.where(valid, _dot(_stack_heads(q_ref[:, sl]), kk, NT), NEG_INF)
            mx = jnp.max(sc, axis=1, keepdims=True)
            p = jnp.exp(sc - mx)
            den = jnp.sum(p, axis=1, keepdims=True)
            out = _dot(p.astype(BF16), vv, NN) / den
            lse = mx + jnp.log(den)
            o_ref[:, sl] = jnp.where(lo, out[:BAND], out[BAND:])
            l_ref[:, sl] = jnp.where(lo, lse[:BAND], lse[BAND:])

    cur, prev = _attn_specs(nb - 1)
    return pl.pallas_call(
        body, grid=(dil, nb), in_specs=[cur, cur, prev, cur, prev], out_specs=[cur, cur],
        out_shape=[jax.ShapeDtypeStruct((dil, n, B_WIDTH), F32)] * 2,
        compiler_params=_cparams("parallel", "parallel"), name=name)(q, k, k, v, v)


def _attn_combine(outs, lses, gb, mixed, name):
    s = mixed.shape[0]
    npat = len(DILATIONS)
    w = B_WIDTH

    def body(*refs):
        o_refs, l_refs = refs[:npat], refs[npat:2 * npat]
        g_ref, _, ob_ref = refs[2 * npat:2 * npat + 3]
        lse_refs = refs[2 * npat + 3:3 * npat + 3]
        mb_ref, stage = refs[3 * npat + 3:]
        os_ = [o_refs[0][...]] + [_load_classes(r, stage, d) for r, d in zip(o_refs[1:], CLASS_DILS)]
        ls = [l_refs[0][...]] + [_load_classes(r, stage, d) for r, d in zip(l_refs[1:], CLASS_DILS)]
        mx = functools.reduce(jnp.maximum, ls)
        ws = [jnp.exp(l - mx) for l in ls]
        tot = functools.reduce(lambda a, b: a + b, ws)
        ob = functools.reduce(lambda a, b: a + b, [wt / tot * o for wt, o in zip(ws, os_)])
        ob_ref[...] = ob
        lse = mx + jnp.log(tot)
        _stage_put(stage, lse)
        lse_refs[0][...] = lse
        for ref, d in zip(lse_refs[1:], CLASS_DILS):
            _store_classes(stage, ref, d)
        mb_ref[...] = (ob * _rsq_mean(ob) * g_ref[...]).astype(BF16)

    lay_specs = [_row_spec(w)] + [_class_spec(d) for d in CLASS_DILS]
    res = pl.pallas_call(
        body, grid=(s // TR,), in_specs=lay_specs * 2 + [_vec_spec(w), ANY],
        out_specs=[_row_spec(w)] + lay_specs + [_row_spec(w, 1)],
        out_shape=[jax.ShapeDtypeStruct((s, w), F32), jax.ShapeDtypeStruct((s, w), F32)]
        + [_class_shape(s, d, F32) for d in CLASS_DILS] + [jax.ShapeDtypeStruct(mixed.shape, mixed.dtype)],
        scratch_shapes=[STAGE], input_output_aliases={2 * npat + 1: npat + 1},
        compiler_params=_cparams("parallel"), name=name)(*outs, *lses, gb, mixed)
    return res[0], dict(zip(DILATIONS, res[1:npat + 1])), res[npat + 1]


def _attn_bwd_prep(dmixed, ob, gb, name):
    s = ob.shape[0]
    w = B_WIDTH
    nlay = len(DILATIONS)

    def body(dm_ref, ob_ref, g_ref, *rest):
        do_refs, dl_refs = rest[:nlay], rest[nlay:2 * nlay]
        dg_ref, stage = rest[2 * nlay:]
        _acc_init([dg_ref])
        ob = ob_ref[...]
        dob, dgt = _rms_bwd(ob, _rsq_mean(ob), g_ref[...], dm_ref[...])
        dg_ref[...] += _colsum(dgt)
        _stage_put(stage, dob)
        do_refs[0][...] = dob.astype(BF16)
        for ref, d in zip(do_refs[1:], CLASS_DILS):
            _store_classes(stage, ref, d)
        lo, hi = _head_masks()
        t = dob * ob
        for b, sl in enumerate(_lane_blocks(w)):
            tb = t[:, sl]
            s0 = jnp.sum(jnp.where(lo, tb, 0.0), axis=1, keepdims=True)
            s1 = jnp.sum(jnp.where(hi, tb, 0.0), axis=1, keepdims=True)
            stage[b] = jnp.where(lo, s0, s1)
        dl_refs[0][...] = _stage_get(stage)
        for ref, d in zip(dl_refs[1:], CLASS_DILS):
            _store_classes(stage, ref, d)

    lay_specs = [_row_spec(w)] + [_class_spec(d) for d in CLASS_DILS]
    shapes = lambda dt: [jax.ShapeDtypeStruct((s, w), dt)] + [_class_shape(s, d, dt) for d in CLASS_DILS]
    res = pl.pallas_call(
        body, grid=(s // TR,), in_specs=[_row_spec(w, 1), _row_spec(w), _vec_spec(w)],
        out_specs=lay_specs * 2 + [_vec_spec(w)],
        out_shape=shapes(BF16) + shapes(F32) + [jax.ShapeDtypeStruct((1, w), F32)],
        scratch_shapes=[STAGE],
        compiler_params=_cparams("arbitrary"), name=name)(dmixed, ob, gb)
    return dict(zip(DILATIONS, res[:nlay])), dict(zip(DILATIONS, res[nlay:2 * nlay])), res[2 * nlay]


def _attn_bwd(q, k, v, do, lse, delta, name):
    dil, n, _ = q.shape
    nb = n // BAND

    def body(q_ref, kc_ref, kp_ref, vc_ref, vp_ref, do_ref, lse_ref, dl_ref,
             dq_ref, dk_ref, dv_ref, ck_ref, cv_ref):
        i = pl.program_id(1)

        @pl.when(i == 0)
        def _():
            ck_ref[...] = jnp.zeros_like(ck_ref)
            cv_ref[...] = jnp.zeros_like(cv_ref)

        @pl.when(i < nb)
        def _():
            valid = _band_mask(i)
            valid = jnp.concatenate([valid, valid], axis=0)
            lo, _ = _head_masks()
            lane = lax.broadcasted_iota(jnp.int32, (1, LANES), 1)

            def per_head(t):
                return jnp.concatenate(
                    [jnp.sum(jnp.where(lane == first, t, 0.0), axis=1, keepdims=True) for first in (0, HEAD_DIM)], axis=0)

            for sl in _lane_blocks(B_WIDTH):
                q2 = _stack_heads(q_ref[:, sl])
                do2 = _stack_heads(do_ref[:, sl])
                kk = jnp.concatenate([kp_ref[:, sl], kc_ref[:, sl]], axis=0)
                vv = jnp.concatenate([vp_ref[:, sl], vc_ref[:, sl]], axis=0)
                p = jnp.where(valid, jnp.exp(_dot(q2, kk, NT) - per_head(lse_ref[:, sl])), 0.0)
                ds = (p * (_dot(do2, vv, NT) - per_head(dl_ref[:, sl]))).astype(BF16)
                dq = _dot(ds, kk, NN)
                dkk = _dot(ds, q2, TN)
                dvv = _dot(p.astype(BF16), do2, TN)
                dq_ref[:, sl] = jnp.where(lo, dq[:BAND], dq[BAND:])
                dk_ref[:, sl] = ck_ref[:, sl] + dkk[:BAND]
                dv_ref[:, sl] = cv_ref[:, sl] + dvv[:BAND]
                ck_ref[:, sl] = dkk[BAND:]
                cv_ref[:, sl] = dvv[BAND:]

        @pl.when(i == nb)
        def _():
            dk_ref[...] = ck_ref[...]
            dv_ref[...] = cv_ref[...]

    cur, prev = _attn_specs(nb - 1)
    lag = pl.BlockSpec((None, BAND, B_WIDTH), lambda r, i: (r, jnp.maximum(i - 1, 0), 0))
    shape = jax.ShapeDtypeStruct((dil, n, B_WIDTH), F32)
    return pl.pallas_call(
        body, grid=(dil, nb + 1), in_specs=[cur, cur, prev, cur, prev, cur, cur, cur],
        out_specs=[cur, lag, lag], out_shape=[shape] * 3,
        scratch_shapes=[pltpu.VMEM((BAND, B_WIDTH), F32)] * 2,
        compiler_params=_cparams("arbitrary", "arbitrary"), name=name)(q, k, k, v, v, do, lse, delta)


def _rope_bwd(dqs, dks, dvs, tabs, dproj, name):
    s = dproj.shape[0]
    half = ROT_DIM // 2
    scale = HEAD_DIM ** -0.5
    npat = len(DILATIONS)
    w = B_WIDTH

    def body(*refs):
        groups = [refs[g * npat:(g + 1) * npat] for g in range(3)]
        c_ref, s1_ref, s2_ref, _, o_ref, stage = refs[3 * npat:]

        def total(rs):
            acc = rs[0][...]
            for ref, d in zip(rs[1:], CLASS_DILS):
                acc = acc + _load_classes(ref, stage, d)
            return acc

        def unrope(g):
            c, s1, s2 = c_ref[...], s1_ref[...], s2_ref[...]
            for sl in _lane_blocks(w):
                gb = g[:, sl]
                o = gb * c + pltpu.roll(gb * s1, half, 1) + pltpu.roll(gb * s2, LANES - half, 1)
                o_ref[:, sl] = o.astype(BF16)

        which = pl.program_id(1)

        @pl.when(which == 0)
        def _():
            unrope(total(groups[0]) * scale)

        @pl.when(which == 1)
        def _():
            unrope(total(groups[1]))

        @pl.when(which == 2)
        def _():
            o_ref[...] = total(groups[2]).astype(BF16)

    tab = pl.BlockSpec((TR, LANES), lambda i, j: (i, 0))
    nat = pl.BlockSpec((TR, w), lambda i, j: (i, 0))
    lay_specs = [nat] + [_class_spec(d) for d in CLASS_DILS]
    first_col = 2 * A_WIDTH // w
    return pl.pallas_call(
        body, grid=(s // TR, 3), in_specs=lay_specs * 3 + [tab] * 3 + [ANY],
        out_specs=pl.BlockSpec((TR, w), lambda i, j: (i, first_col + j)),
        out_shape=jax.ShapeDtypeStruct(dproj.shape, dproj.dtype), scratch_shapes=[STAGE],
        input_output_aliases={3 * npat + 3: 0},
        compiler_params=_cparams("parallel", "arbitrary"), name=name)(*dqs, *dks, *dvs, *tabs, dproj)


TK = 512
HALO = 16


def _row_of(v, r):
    rows = lax.broadcasted_iota(jnp.int32, (v.shape[0], 1), 0)
    return jnp.sum(jnp.where(rows == r, v, 0.0), axis=0, keepdims=True)


def _taps_before(x, halo):
    row = lax.broadcasted_iota(jnp.int32, (x.shape[0], 1), 0)
    m1 = jnp.where(row == 0, _row_of(halo, HALO - 1), pltpu.roll(x, 1, 0))
    m2 = jnp.where(row == 0, _row_of(halo, HALO - 2), jnp.where(row == 1, _row_of(halo, HALO - 1), pltpu.roll(x, 2, 0)))
    return m2, m1, x


def _taps_after(x, halo):
    rows = x.shape[0]
    row = lax.broadcasted_iota(jnp.int32, (rows, 1), 0)
    p1 = jnp.where(row == rows - 1, _row_of(halo, 0), pltpu.roll(x, rows - 1, 0))
    p2 = jnp.where(row == rows - 2, _row_of(halo, 0), jnp.where(row == rows - 1, _row_of(halo, 1), pltpu.roll(x, rows - 2, 0)))
    return p1, p2


def _conv_value(taps, cw_ref, cb_ref, h):
    return cb_ref[h] + cw_ref[h, 0:1, :] * taps[0] + cw_ref[h, 1:2, :] * taps[1] + cw_ref[h, 2:3, :] * taps[2]


def _ffn_weight_specs(ncol):
    per_up = (2 * D_FF // N_CHIPS) // TK
    per_dn = (D_FF // N_CHIPS) // TK
    wg = pl.BlockSpec((None, None, D_MODEL, TK), lambda i, j: (j // per_up, 0, 0, j % per_up))
    wv = pl.BlockSpec((None, None, D_MODEL, TK), lambda i, j: ((j + ncol) // per_up, 0, 0, (j + ncol) % per_up))
    wd = pl.BlockSpec((None, None, TK, D_MODEL), lambda i, j: (j // per_dn, 0, j % per_dn, 0))
    cw = pl.BlockSpec((2, 3, TK), lambda i, j: (0, 0, j))
    cb = pl.BlockSpec((2, 1, TK), lambda i, j: (0, 0, j))
    return wg, wv, wd, cw, cb


def _ffn_forward(h2, w_up, w_down, cw3, cb3, name, gather=None):
    s = h2.shape[0]
    nm, ncol = s // TM, D_FF // TK
    ng = 0 if gather is None else len(gather)

    def body(*refs):
        h_ref, wg_ref, wv_ref, wd_ref, cw_ref, cb_ref = refs[:6]
        g_in = refs[6:6 + ng]
        y_ref, up_ref, cv_ref, f_ref = refs[6 + ng:10 + ng]
        g_out = refs[10 + ng:10 + 2 * ng]
        carry, acc = refs[10 + 2 * ng:12 + 2 * ng]
        i, j = pl.program_id(0), pl.program_id(1)
        if ng:
            start, relay, finish = _gather_steps(g_in, g_out, *refs[12 + 2 * ng:])
            pl.when((i == 0) & (j == 0))(start)
            pl.when((i == nm - 1) & (j == 0))(relay)

        @pl.when((i == 0) & (j == 0))
        def _():
            carry[...] = jnp.zeros_like(carry)

        h = h_ref[...]
        conv = []
        for hh, w_ref in ((0, wg_ref), (1, wv_ref)):
            up = _dot(h, w_ref[...], NN).astype(BF16)
            up_ref[hh] = up
            x = up.astype(F32)
            conv.append(_conv_value(_taps_before(x, carry[j, hh]), cw_ref, cb_ref, hh))
            cv_ref[hh] = conv[hh].astype(BF16)
            carry[j, hh] = x[TM - HALO:, :]
        y = (_gelu_tanh(conv[0])[0] * conv[1]).astype(BF16)
        y_ref[...] = y
        part = _dot(y, wd_ref[...], NN)

        @pl.when(j == 0)
        def _():
            acc[...] = part

        @pl.when(j > 0)
        def _():
            acc[...] += part

        @pl.when(j == ncol - 1)
        def _():
            f_ref[...] = acc[...]

        if ng:
            pl.when((i == nm - 1) & (j == ncol - 1))(finish)

    wg, wv, wd, cw, cb = _ffn_weight_specs(ncol)
    res = pl.pallas_call(
        body, grid=(nm, ncol),
        in_specs=[pl.BlockSpec((TM, D_MODEL), lambda i, j: (i, 0)), wg, wv, wd, cw, cb] + [ANY] * ng,
        out_specs=[pl.BlockSpec((TM, TK), lambda i, j: (i, j)), pl.BlockSpec((2, TM, TK), lambda i, j: (0, i, j)),
                   pl.BlockSpec((2, TM, TK), lambda i, j: (0, i, j)),
                   pl.BlockSpec((TM, D_MODEL), lambda i, j: (i, 0))] + [ANY] * ng,
        out_shape=[jax.ShapeDtypeStruct((s, D_FF), BF16), jax.ShapeDtypeStruct((2, s, D_FF), BF16),
                   jax.ShapeDtypeStruct((2, s, D_FF), BF16),
                   jax.ShapeDtypeStruct((s, D_MODEL), F32)] + _gathered_shapes(gather or []),
        scratch_shapes=[pltpu.VMEM((ncol, 2, HALO, TK), F32), pltpu.VMEM((TM, D_MODEL), F32)]
        + (_gather_sems(ng) if ng else []),
        compiler_params=_cparams("arbitrary", "arbitrary"), name=name)(h2, w_up, w_up, w_down, cw3, cb3,
                                                                      *(gather or []))
    return res[:4], list(res[4:])


def _ffn_backward(df, w_up, w_down, up3, cv3, cw3, name, scatter=None):
    s = df.shape[0]
    nm, ncol = s // TM, D_FF // TK
    ns = 0 if scatter is None else len(scatter)

    def body(*refs):
        df_ref, wg_ref, wv_ref, wd_ref, cw_ref, up_ref, cv_ref = refs[:7]
        s_in = refs[7:7 + ns]
        dup_ref, dh_ref, sums_ref = refs[7 + ns:10 + ns]
        s_out = refs[10 + ns:10 + 2 * ns]
        carry, acc = refs[10 + 2 * ns:12 + 2 * ns]
        i, j = pl.program_id(0), pl.program_id(1)
        if ns:
            start, finish = _scatter_steps(s_in, s_out, *refs[12 + 2 * ns:])
            pl.when((i == 0) & (j == 0))(start)

        @pl.when((i == 0) & (j == 0))
        def _():
            carry[...] = jnp.zeros_like(carry)
            sums_ref[...] = jnp.zeros_like(sums_ref)

        dy = _dot(df_ref[...], wd_ref[...], NT)
        act, grad = _gelu_tanh(cv_ref[0].astype(F32))
        dcs = (dy * cv_ref[1].astype(F32) * grad, dy * act)
        row = lax.broadcasted_iota(jnp.int32, (8, 1), 0)
        part = None
        for hh, w_ref in ((0, wg_ref), (1, wv_ref)):
            dc = dcs[hh]
            x = up_ref[hh].astype(F32)
            after1, after2 = _taps_after(dc, carry[j, hh])
            upd = jnp.zeros((8, TK), F32)
            for ridx, sm in enumerate((_colsum(after2 * x), _colsum(after1 * x), _colsum(dc * x), _colsum(dc))):
                upd = jnp.where(row == ridx, sm, upd)
            sums_ref[j, hh] += upd
            dup = (cw_ref[hh, 2:3, :] * dc + cw_ref[hh, 1:2, :] * after1 + cw_ref[hh, 0:1, :] * after2).astype(BF16)
            carry[j, hh] = dc[:HALO, :]
            dup_ref[hh] = dup
            d = _dot(dup, w_ref[...], NT)
            part = d if part is None else part + d

        @pl.when(j == 0)
        def _():
            acc[...] = part

        @pl.when(j > 0)
        def _():
            acc[...] += part

        @pl.when(j == ncol - 1)
        def _():
            dh_ref[...] = acc[...]

        if ns:
            pl.when((i == nm - 1) & (j == ncol - 1))(finish)

    wg, wv, wd, cw, _ = _ffn_weight_specs(ncol)
    rev = lambda i: nm - 1 - i
    res = pl.pallas_call(
        body, grid=(nm, ncol),
        in_specs=[pl.BlockSpec((TM, D_MODEL), lambda i, j: (rev(i), 0)), wg, wv, wd, cw,
                  pl.BlockSpec((2, TM, TK), lambda i, j: (0, rev(i), j)),
                  pl.BlockSpec((2, TM, TK), lambda i, j: (0, rev(i), j))] + [ANY] * ns,
        out_specs=[pl.BlockSpec((2, TM, TK), lambda i, j: (0, rev(i), j)),
                   pl.BlockSpec((TM, D_MODEL), lambda i, j: (rev(i), 0)),
                   pl.BlockSpec((ncol, 2, 8, TK), lambda i, j: (0, 0, 0, 0))] + [ANY] * ns,
        out_shape=[jax.ShapeDtypeStruct((2, s, D_FF), BF16), jax.ShapeDtypeStruct((s, D_MODEL), F32),
                   jax.ShapeDtypeStruct((ncol, 2, 8, TK), F32)] + (_scattered_shapes() if ns else []),
        scratch_shapes=[pltpu.VMEM((ncol, 2, HALO, TK), F32), pltpu.VMEM((TM, D_MODEL), F32)]
        + (_scatter_sems() if ns else []),
        compiler_params=_cparams("arbitrary", "arbitrary"), name=name)(df, w_up, w_up, w_down, cw3, up3, cv3,
                                                                      *(scatter or []))
    return res[:3], list(res[3:])


def _wspec(rows, cols, index_map):
    return pl.BlockSpec((None, None, rows, cols), index_map)


def _layer_forward(l, x0, h1, p, wg, tabs, gather=None):
    s = x0.shape[0]
    nm = s // TMM
    tag = f"_l{l}"
    proj = _matmul(
        h1, wg["w_in"], grid=(nm, N_CHIPS), a_spec=pl.BlockSpec((TMM, D_MODEL), lambda i, j: (i, 0)),
        b_spec=_wspec(D_MODEL, IN_COLS // N_CHIPS, lambda i, j: (j, 0, 0, 0)),
        o_spec=pl.BlockSpec((TMM, IN_COLS // N_CHIPS), lambda i, j: (i, j)), o_shape=(s, IN_COLS), o_dtype=F32,
        dims=NN, nk=1, kaxis=None, acc_shape=None, name="proj" + tag)
    ma = _mixer_a_fwd(proj, p["v_norm_g"], p["v_norm_b"], p["w_spatial"], p["bs_full"], p["out_norm_a"],
                      "mixer_a_fwd" + tag)
    q, k, v = _rope_fwd(proj, tabs, "rope_fwd" + tag)
    outs, lses = zip(*[_attn_fwd(_as_classes(q[d]), _as_classes(k[d]), _as_classes(v[d]), f"attn_fwd_d{d}" + tag)
                       for d in DILATIONS])
    outs = [o.reshape(s, B_WIDTH) if d == 1 else o for o, d in zip(outs, DILATIONS)]
    lses = [t.reshape(s, B_WIDTH) if d == 1 else t for t, d in zip(lses, DILATIONS)]
    ob, lse, mixed = _attn_combine(outs, lses, p["out_norm_b"], ma, "attn_combine" + tag)
    w_out_all = pl.BlockSpec((N_CHIPS, None, D_MODEL // N_CHIPS, D_MODEL), lambda i: (0, 0, 0, 0))
    y1 = _matmul(
        mixed, wg["w_out"], grid=(nm,), a_spec=pl.BlockSpec((TMM, D_MODEL), lambda i: (i, 0)), b_spec=w_out_all,
        o_spec=pl.BlockSpec((TMM, D_MODEL), lambda i: (i, 0)), o_shape=(s, D_MODEL), o_dtype=F32,
        dims=NN, nk=1, kaxis=None, acc_shape=None, name="mix_out" + tag, b_2d=(D_MODEL, D_MODEL))
    x1, h2 = _residual_norm(x0, y1, p["post_mix_norm"], p["pre_ffn_norm"], "post_mix" + tag)
    (y, up3, cv3, f), gathered = _ffn_forward(h2, wg["w_up"], wg["w_down"], p["cw3"], p["cb3"], "ffn_fwd" + tag,
                                              gather)
    saved = dict(x0=x0, h1=h1, proj=proj, q=q, k=k, v=v, ob=ob, lse=lse, mixed=mixed, y1=y1, x1=x1, h2=h2,
                 up3=up3, cv3=cv3, y=y, f=f)
    return saved, gathered


def _layer_backward(l, dx2, sv, p, wg, tabs, scatter=None):
    s = dx2.shape[0]
    nm = s // TMM
    tag = f"_l{l}"
    g = {}
    df, g["post_ffn_norm"] = _norm_bwd_out(dx2, sv["f"], p["post_ffn_norm"], "norm_bwd_out" + tag)
    (dup3, dh2, conv_sums), scattered = _ffn_backward(df, wg["w_up"], wg["w_down"], sv["up3"], sv["cv3"], p["cw3"],
                                                      "ffn_bwd" + tag, scatter)
    sums = conv_sums.transpose(1, 2, 0, 3).reshape(2, 8, D_FF)
    g["conv_w"] = jnp.concatenate([sums[0, :3], sums[1, :3]], axis=1)
    g["conv_b"] = jnp.concatenate([sums[0, 3:4], sums[1, 3:4]], axis=1)
    tn = 1024
    gw_up = _matmul(
        sv["h2"], dup3, grid=(2 * D_FF // tn, nm), a_spec=pl.BlockSpec((TMM, D_MODEL), lambda n, m: (m, 0)),
        b_spec=pl.BlockSpec((None, TMM, tn), lambda n, m: (n // (D_FF // tn), m, n % (D_FF // tn))),
        o_spec=pl.BlockSpec((None, D_MODEL, tn), lambda n, m: (n // 2, 0, n % 2)),
        o_shape=(N_CHIPS, D_MODEL, 2 * D_FF // N_CHIPS), o_dtype=BF16,
        dims=TN, nk=nm, kaxis=1, acc_shape=(D_MODEL, tn), name="w_up_grad" + tag)
    gw_down = _matmul(
        sv["y"], df, grid=(D_FF // tn, 2, nm), a_spec=pl.BlockSpec((TMM, tn), lambda k, h, m: (m, k)),
        b_spec=pl.BlockSpec((TMM, D_MODEL // 2), lambda k, h, m: (m, h)),
        o_spec=pl.BlockSpec((None, tn, D_MODEL // 2), lambda k, h, m: (h, k, 0)),
        o_shape=(2, D_FF, D_MODEL // 2), o_dtype=BF16,
        dims=TN, nk=nm, kaxis=2, acc_shape=(tn, D_MODEL // 2), name="w_down_grad" + tag)
    dx1, dy1, g["pre_ffn_norm"], g["post_mix_norm"] = _norm_bwd_mid(
        dx2, dh2, sv["x1"], p["pre_ffn_norm"], sv["y1"], p["post_mix_norm"], "norm_bwd_mid" + tag)
    w_out_all = pl.BlockSpec((N_CHIPS, None, D_MODEL // N_CHIPS, D_MODEL), lambda i: (0, 0, 0, 0))
    dmixed = _matmul(
        dy1, wg["w_out"], grid=(nm,), a_spec=pl.BlockSpec((TMM, D_MODEL), lambda i: (i, 0)), b_spec=w_out_all,
        o_spec=pl.BlockSpec((TMM, D_MODEL), lambda i: (i, 0)), o_shape=(s, D_MODEL), o_dtype=F32,
        dims=NT, nk=1, kaxis=None, acc_shape=None, name="mix_out_bwd" + tag, b_2d=(D_MODEL, D_MODEL))
    gw_out = _matmul(
        sv["mixed"], dy1, grid=(2, nm), a_spec=pl.BlockSpec((TMM, D_MODEL), lambda h, m: (m, 0)),
        b_spec=pl.BlockSpec((TMM, D_MODEL // 2), lambda h, m: (m, h)),
        o_spec=pl.BlockSpec((None, D_MODEL, D_MODEL // 2), lambda h, m: (h, 0, 0)),
        o_shape=(2, D_MODEL, D_MODEL // 2), o_dtype=BF16,
        dims=TN, nk=nm, kaxis=1, acc_shape=(D_MODEL, D_MODEL // 2), name="w_out_grad" + tag)
    dpa, g["out_norm_a"], g["v_norm_g"], g["v_norm_b"], dbs, g["w_spatial"] = _mixer_a_bwd(
        sv["proj"], dmixed, p["v_norm_g"], p["v_norm_b"], p["w_spatial"], p["bs_full"], p["out_norm_a"],
        "mixer_a_bwd" + tag)
    g["b_spatial"] = dbs[:, ::GROUP_DIM].T
    dob, delta, g["out_norm_b"] = _attn_bwd_prep(dmixed, sv["ob"], p["out_norm_b"], "attn_bwd_prep" + tag)
    dqs, dks, dvs = zip(*[
        _attn_bwd(*(_as_classes(t[d]) for t in (sv["q"], sv["k"], sv["v"], dob, sv["lse"], delta)),
                  f"attn_bwd_d{d}" + tag) for d in DILATIONS])
    nat = lambda ts: [t.reshape(s, B_WIDTH) if d == 1 else t for t, d in zip(ts, DILATIONS)]
    dproj = _rope_bwd(nat(dqs), nat(dks), nat(dvs), tabs, dpa, "rope_bwd" + tag)
    wcol = IN_COLS // N_CHIPS
    dh1 = _matmul(
        dproj, wg["w_in"], grid=(nm, N_CHIPS), a_spec=pl.BlockSpec((TMM, wcol), lambda i, n: (i, n)),
        b_spec=_wspec(D_MODEL, wcol, lambda i, n: (n, 0, 0, 0)),
        o_spec=pl.BlockSpec((TMM, D_MODEL), lambda i, n: (i, 0)), o_shape=(s, D_MODEL), o_dtype=F32,
        dims=NT, nk=N_CHIPS, kaxis=1, acc_shape=(TMM, D_MODEL), name="proj_bwd" + tag)
    gw_in = _matmul(
        sv["h1"], dproj, grid=(N_CHIPS, nm), a_spec=pl.BlockSpec((TMM, D_MODEL), lambda n, m: (m, 0)),
        b_spec=pl.BlockSpec((TMM, wcol), lambda n, m: (m, n)),
        o_spec=pl.BlockSpec((None, D_MODEL, wcol), lambda n, m: (n, 0, 0)),
        o_shape=(N_CHIPS, D_MODEL, wcol), o_dtype=BF16,
        dims=TN, nk=nm, kaxis=1, acc_shape=(D_MODEL, wcol), name="w_in_grad" + tag)
    dx0, g["pre_mix_norm"] = _norm_bwd_in(dx1, dh1, sv["x0"], p["pre_mix_norm"], "norm_bwd_in" + tag)
    big = dict(w_in=gw_in, w_up=gw_up, w_out=gw_out, w_down=gw_down)
    return dx0, big, g, scattered


SMALL = ("pre_mix_norm", "v_norm_g", "v_norm_b", "w_spatial", "b_spatial", "out_norm_a", "out_norm_b",
         "post_mix_norm", "pre_ffn_norm", "conv_b", "post_ffn_norm")
BIG = ("w_in", "w_out", "w_up", "w_down")
DEPTH = 2


def _layer_params(l, small, conv_w_full):
    p = {n: small[n][l].reshape(1, -1) for n in SMALL if n not in ("w_spatial", "b_spatial")}
    p["w_spatial"] = small["w_spatial"][l]
    p["bs_full"] = jnp.repeat(small["b_spatial"][l].T, GROUP_DIM, axis=1)
    p["cw3"] = conv_w_full[l].reshape(3, 2, D_FF).transpose(1, 0, 2)
    p["cb3"] = small["conv_b"][l].reshape(2, 1, D_FF)
    return p


def _mesh_pos():
    return lax.axis_index("x"), lax.axis_index("y"), lax.axis_index("c")


def _other_chips(x, y):
    return [(1 - x, y), (x, 1 - y), (1 - x, 1 - y)]


def _gathered_shapes(blocks):
    return [jax.ShapeDtypeStruct((N_CHIPS, 1) + a.shape, a.dtype) for a in blocks]


def _gather_sems(nw):
    n = 2 * nw * (N_CHIPS - 1) + nw
    return [pltpu.SemaphoreType.DMA((n,)), pltpu.SemaphoreType.DMA((n,))]


def _gather_steps(ins, outs, send, recv):
    nw, nrel = len(ins), N_CHIPS - 1
    x, y, c = _mesh_pos()
    mine, sibling, chips = 2 * x + y, (x, y, 1 - c), _other_chips(x, y)

    def copy(src, dst, slot, to):
        return pltpu.make_async_remote_copy(src_ref=src, dst_ref=dst, send_sem=send.at[slot],
                                            recv_sem=recv.at[slot], device_id=to, device_id_type=MESH)

    def half_rows(t, core):
        rows = ins[t].shape[0] // 2
        return pl.ds(pl.multiple_of(core * rows, rows), rows)

    def landing(t, chip, core):
        return outs[t].at[chip, 0, half_rows(t, core), :]

    slots = [(t, r, chip) for t in range(nw) for r, chip in enumerate(chips)]
    own = [copy(ins[t], outs[t].at[mine, 0], 2 * nw * nrel + t, sibling) for t in range(nw)]
    first = [copy(ins[t].at[half_rows(t, c), :], landing(t, mine, c), t * nrel + r, (px, py, c))
             for t, r, (px, py) in slots]
    relays = [copy(landing(t, 2 * px + py, c), landing(t, 2 * px + py, c), nw * nrel + t * nrel + r, sibling)
              for t, r, (px, py) in slots]

    def start():
        for cp in own + first:
            cp.start()

    def relay():
        for (t, r, (px, py)), cp in zip(slots, relays):
            copy(landing(t, 2 * px + py, c), landing(t, 2 * px + py, c), t * nrel + r, (px, py, c)).wait_recv()
            cp.start()

    def finish():
        for t, r, (px, py) in slots:
            passed = landing(t, 2 * px + py, 1 - c)
            copy(passed, passed, nw * nrel + t * nrel + r, sibling).wait_recv()
        for cp in first + relays:
            cp.wait_send()
        for cp in own:
            cp.wait()

    return start, relay, finish


def _gather_weights(blocks, name):
    nw = len(blocks)

    def body(*refs):
        start, relay, finish = _gather_steps(refs[:nw], refs[nw:2 * nw], *refs[2 * nw:])
        start()
        relay()
        finish()

    return pl.pallas_call(
        body, in_specs=[ANY] * nw, out_specs=[ANY] * nw, out_shape=_gathered_shapes(blocks),
        scratch_shapes=_gather_sems(nw), name=name)(*blocks)


HALF_ROWS = D_MODEL // 2


def _pair_exchange(g, name):
    shapes = [(N_CHIPS, HALF_ROWS, IN_COLS // N_CHIPS), (N_CHIPS, HALF_ROWS, 2 * D_FF // N_CHIPS),
              (D_MODEL, D_MODEL // 2), (D_FF, D_MODEL // 2)]

    def body(gin, gup, gout, gdn, rin, rup, rout, rdn, send, recv):
        x, y, c = _mesh_pos()
        o = 1 - c
        rows = pl.ds(pl.multiple_of(o * HALF_ROWS, HALF_ROWS), HALF_ROWS)
        pairs = [(gin.at[:, rows, :], rin), (gup.at[:, rows, :], rup), (gout.at[o], rout), (gdn.at[o], rdn)]
        cps = [pltpu.make_async_remote_copy(src_ref=src, dst_ref=dst, send_sem=send.at[t], recv_sem=recv.at[t],
                                            device_id=(x, y, o), device_id_type=MESH)
               for t, (src, dst) in enumerate(pairs)]
        for cp in cps:
            cp.start()
        for cp in cps:
            cp.wait()

    return pl.pallas_call(
        body, in_specs=[ANY] * 4, out_specs=[ANY] * 4,
        out_shape=[jax.ShapeDtypeStruct(sh, BF16) for sh in shapes],
        scratch_shapes=[pltpu.SemaphoreType.DMA((4,)), pltpu.SemaphoreType.DMA((4,))],
        name=name)(g["w_in"], g["w_up"], g["w_out"], g["w_down"])


def _pair_sum(g, recv, pos, name_prefix):
    def add(a, b, grid, a_spec, b_spec, shape, name):
        def body(pos_ref, a_ref, b_ref, o_ref):
            o_ref[...] = (a_ref[...].astype(F32) + b_ref[...].astype(F32)).astype(BF16)

        return pl.pallas_call(
            body, grid_spec=pltpu.PrefetchScalarGridSpec(
                num_scalar_prefetch=1, grid=grid, in_specs=[a_spec, b_spec], out_specs=b_spec),
            out_shape=jax.ShapeDtypeStruct(shape, BF16), compiler_params=_cparams(*["parallel"] * len(grid)),
            name=name)(pos, a, b)

    rin, rup, rout, rdn = recv
    wi, wu = IN_COLS // N_CHIPS, 2 * D_FF // N_CHIPS
    s_in = add(g["w_in"], rin, (N_CHIPS,), pl.BlockSpec((None, HALF_ROWS, wi), lambda j, pos: (j, pos[2], 0)),
               pl.BlockSpec((None, HALF_ROWS, wi), lambda j, pos: (j, 0, 0)), rin.shape, name_prefix + "_in")
    s_up = add(g["w_up"], rup, (N_CHIPS,), pl.BlockSpec((None, HALF_ROWS, wu), lambda j, pos: (j, pos[2], 0)),
               pl.BlockSpec((None, HALF_ROWS, wu), lambda j, pos: (j, 0, 0)), rup.shape, name_prefix + "_up")
    hc = D_MODEL // 2
    s_out = add(g["w_out"], rout, (1,), pl.BlockSpec((None, D_MODEL, hc), lambda j, pos: (pos[2], 0, 0)),
                pl.BlockSpec((D_MODEL, hc), lambda j, pos: (0, 0)), rout.shape, name_prefix + "_out")
    s_dn = add(g["w_down"], rdn, (N_CHIPS,), pl.BlockSpec((None, D_FF // N_CHIPS, hc), lambda j, pos: (pos[2], j, 0)),
               pl.BlockSpec((D_FF // N_CHIPS, hc), lambda j, pos: (j, 0)), rdn.shape, name_prefix + "_down")
    return s_in, s_up, s_out, s_dn


OUT_ROWS = D_MODEL // N_CHIPS
DOWN_ROWS = D_FF // N_CHIPS


def _scattered_shapes():
    nrel = N_CHIPS - 1
    shapes = [(nrel, HALF_ROWS, IN_COLS // N_CHIPS), (nrel, HALF_ROWS, 2 * D_FF // N_CHIPS),
              (nrel, OUT_ROWS, D_MODEL // 2), (nrel, DOWN_ROWS, D_MODEL // 2)]
    return [jax.ShapeDtypeStruct(sh, BF16) for sh in shapes]


def _scatter_sems():
    n = 4 * (N_CHIPS - 1)
    return [pltpu.SemaphoreType.DMA((n,)), pltpu.SemaphoreType.DMA((n,))]


def _scatter_steps(sums, outs, send, recv):
    nrel = N_CHIPS - 1
    sin, sup, sout, sdn = sums
    x, y, c = _mesh_pos()
    cps = []
    for r, (px, py) in enumerate(_other_chips(x, y)):
        j = 2 * px + py
        pieces = [sin.at[j], sup.at[j], sout.at[pl.ds(pl.multiple_of(j * OUT_ROWS, OUT_ROWS), OUT_ROWS), :],
                  sdn.at[pl.ds(pl.multiple_of(j * DOWN_ROWS, DOWN_ROWS), DOWN_ROWS), :]]
        for t, src in enumerate(pieces):
            cps.append(pltpu.make_async_remote_copy(
                src_ref=src, dst_ref=outs[t].at[r], send_sem=send.at[t * nrel + r], recv_sem=recv.at[t * nrel + r],
                device_id=(px, py, c), device_id_type=MESH))

    def start():
        for cp in cps:
            cp.start()

    def finish():
        for cp in cps:
            cp.wait()

    return start, finish


def _chip_scatter(sums, name):
    def body(*refs):
        start, finish = _scatter_steps(refs[:4], refs[4:8], *refs[8:])
        start()
        finish()

    return pl.pallas_call(
        body, in_specs=[ANY] * 4, out_specs=[ANY] * 4, out_shape=_scattered_shapes(),
        scratch_shapes=_scatter_sems(), name=name)(*sums)


def _chip_sum(sums, recv, pos, name_prefix):
    def add(a, b, a_spec, shape, name):
        def body(pos_ref, a_ref, b_ref, o_ref):
            tot = a_ref[...].astype(F32)
            for r in range(N_CHIPS - 1):
                tot = tot + b_ref[r].astype(F32)
            o_ref[...] = tot

        return pl.pallas_call(
            body, grid_spec=pltpu.PrefetchScalarGridSpec(
                num_scalar_prefetch=1, grid=(1,), in_specs=[a_spec, pl.BlockSpec(b.shape, lambda i, pos: (0, 0, 0))],
                out_specs=pl.BlockSpec((None,) + shape, lambda i, pos: (pos[2], 0, 0))),
            out_shape=jax.ShapeDtypeStruct((2,) + shape, F32), compiler_params=_cparams("arbitrary"),
            name=name)(pos, a, b)

    s_in, s_up, s_out, s_dn = sums
    rin, rup, rout, rdn = recv
    wi, wu, hc = IN_COLS // N_CHIPS, 2 * D_FF // N_CHIPS, D_MODEL // 2
    chip = lambda pos: 2 * pos[0] + pos[1]
    t_in = add(s_in, rin, pl.BlockSpec((None, HALF_ROWS, wi), lambda i, pos: (chip(pos), 0, 0)), (HALF_ROWS, wi),
               name_prefix + "_in")
    t_up = add(s_up, rup, pl.BlockSpec((None, HALF_ROWS, wu), lambda i, pos: (chip(pos), 0, 0)), (HALF_ROWS, wu),
               name_prefix + "_up")
    t_out = add(s_out, rout, pl.BlockSpec((OUT_ROWS, hc), lambda i, pos: (chip(pos), 0)), (OUT_ROWS, hc),
                name_prefix + "_out")
    t_dn = add(s_dn, rdn, pl.BlockSpec((DOWN_ROWS, hc), lambda i, pos: (chip(pos), 0)), (DOWN_ROWS, hc),
               name_prefix + "_down")
    return t_in, t_up, t_out, t_dn


def _pair_share(totals, name):
    n = len(totals)

    def body(*refs):
        ins, outs = refs[:n], refs[n:2 * n]
        send, recv = refs[2 * n:]
        x, y, c = _mesh_pos()
        o = 1 - c
        cps = [pltpu.make_async_remote_copy(src_ref=ins[t].at[c], dst_ref=outs[t].at[c], send_sem=send.at[t],
                                            recv_sem=recv.at[t], device_id=(x, y, o), device_id_type=MESH)
               for t in range(n)]
        for cp in cps:
            cp.start()
        for t in range(n):
            pltpu.make_async_remote_copy(src_ref=ins[t].at[o], dst_ref=outs[t].at[o], send_sem=send.at[t],
                                         recv_sem=recv.at[t], device_id=(x, y, o), device_id_type=MESH).wait_recv()
        for cp in cps:
            cp.wait_send()

    return pl.pallas_call(
        body, in_specs=[ANY] * n, out_specs=[ANY] * n,
        out_shape=[jax.ShapeDtypeStruct(t.shape, t.dtype) for t in totals],
        scratch_shapes=[pltpu.SemaphoreType.DMA((n,)), pltpu.SemaphoreType.DMA((n,))],
        input_output_aliases={t: t for t in range(n)}, name=name)(*totals)


def _chip_sums(l, g, pos):
    recv = _pair_exchange(g, f"pair_exchange_l{l}")
    return _pair_sum(g, recv, pos, f"pair_sum_l{l}")


def _gradient_shards(l, sums, scattered, pos):
    totals = _chip_sum(sums, scattered, pos, f"chip_sum_l{l}")
    f_in, f_up, f_out, f_dn = _pair_share(totals, f"pair_share_l{l}")
    f_in = f_in.reshape(D_MODEL, IN_COLS // N_CHIPS)
    f_up = f_up.reshape(D_MODEL, 2 * D_FF // N_CHIPS)
    f_out = f_out.transpose(1, 0, 2).reshape(OUT_ROWS, D_MODEL)
    f_dn = f_dn.transpose(1, 0, 2).reshape(DOWN_ROWS, D_MODEL)
    return dict(w_in=f_in, w_up=f_up, w_out=f_out, w_down=f_dn)


N_DEV = 8


def _allreduce_small(packed, name):
    rows = packed.shape[0]

    def body(x_ref, out_ref, gath, send_sems, recv_sems, local_sem):
        x, y, c = _mesh_pos()
        me, sibling = (x, y, c), (x, y, 1 - c)
        chips = _other_chips(x, y)

        def blk(px, py, pc):
            return gath.at[pl.ds(pl.multiple_of((4 * px + 2 * py + pc) * rows, 8), rows), :]

        def copy(k, block, to, src=None):
            return pltpu.make_async_remote_copy(
                src_ref=blk(*block) if src is None else src, dst_ref=blk(*block), send_sem=send_sems.at[k],
                recv_sem=recv_sems.at[k], device_id=to, device_id_type=MESH)

        mine = pltpu.make_async_copy(x_ref, blk(*me), local_sem)
        mine.start()
        first = [copy(0, me, sibling, src=x_ref)]
        first += [copy(1 + j, me, (*chip, c), src=x_ref) for j, chip in enumerate(chips)]
        for cp in first:
            cp.start()
        passed = [copy(4 + j, (*chip, c), sibling) for j, chip in enumerate(chips)]
        for j, chip in enumerate(chips):
            copy(1 + j, (*chip, c), me).wait_recv()
            passed[j].start()
        copy(0, sibling, me).wait_recv()
        for j, chip in enumerate(chips):
            copy(4 + j, (*chip, 1 - c), me).wait_recv()
        for cp in first + passed:
            cp.wait_send()
        mine.wait()
        tot = gath[0:rows, :]
        for d in range(1, N_DEV):
            tot = tot + gath[d * rows:(d + 1) * rows, :]
        out_ref[...] = tot

    vmem = pl.BlockSpec(memory_space=pltpu.VMEM)
    return pl.pallas_call(
        body, in_specs=[vmem], out_specs=vmem, out_shape=jax.ShapeDtypeStruct((rows, LANES), F32),
        scratch_shapes=[pltpu.VMEM((N_DEV * rows, LANES), F32), pltpu.SemaphoreType.DMA((7,)),
                        pltpu.SemaphoreType.DMA((7,)), pltpu.SemaphoreType.DMA],
        compiler_params=pltpu.CompilerParams(vmem_limit_bytes=VMEM_LIMIT_BYTES),
        name=name)(packed)


def _adamw(w, g, m, v, name):
    rows, cols = w.shape
    tr = 256 if rows % 256 == 0 else rows

    def body(w_ref, g_ref, m_ref, v_ref, d_ref, mo_ref, vo_ref):
        gv = g_ref[...]
        mn = ADAM_B1 * m_ref[...] + (1.0 - ADAM_B1) * gv
        vn = ADAM_B2 * v_ref[...] + (1.0 - ADAM_B2) * (gv * gv)
        m_hat = mn / (1.0 - ADAM_B1 ** ADAM_STEP)
        v_hat = vn / (1.0 - ADAM_B2 ** ADAM_STEP)
        d_ref[...] = -ADAM_LR * (m_hat / (jnp.sqrt(v_hat) + ADAM_EPS) + ADAM_WD * w_ref[...])
        mo_ref[...] = mn
        vo_ref[...] = vn

    spec = pl.BlockSpec((tr, cols), lambda i: (i, 0))
    return pl.pallas_call(
        body, grid=(rows // tr,), in_specs=[spec] * 4, out_specs=[spec] * 3,
        out_shape=[jax.ShapeDtypeStruct((rows, cols), F32)] * 3, compiler_params=_cparams("parallel"),
        name=name)(w, g, m, v)


def _adamw_nd(w, g, m, v, name):
    cols = w.shape[-1] if w.shape[-1] % LANES == 0 else LANES
    outs = _adamw(*(t.reshape(-1, cols) for t in (w, g, m, v)), name)
    return tuple(t.reshape(w.shape) for t in outs)


def _pack(arrays):
    return jnp.concatenate([a.reshape(-1, LANES) for a in arrays], axis=0)


def _unpack(packed, shapes):
    out, row = [], 0
    for sh in shapes:
        n = math.prod(sh) // LANES
        out.append(packed[row:row + n].reshape(sh))
        row += n
    return out


WEIGHTS = ("pre_mix_norm", "w_in", "v_norm_g", "v_norm_b", "w_spatial", "b_spatial", "out_norm_a", "out_norm_b",
           "w_out", "post_mix_norm", "pre_ffn_norm", "w_up", "conv_w", "conv_b", "w_down", "post_ffn_norm")


def kernel(x, pre_mix_norm, w_in, v_norm_g, v_norm_b, w_spatial, b_spatial, out_norm_a, out_norm_b, w_out, post_mix_norm, pre_ffn_norm, w_up, conv_w, conv_b, w_down, post_ffn_norm, loss_target, m_pre_mix_norm, m_w_in, m_v_norm_g, m_v_norm_b, m_w_spatial, m_b_spatial, m_out_norm_a, m_out_norm_b, m_w_out, m_post_mix_norm, m_pre_ffn_norm, m_w_up, m_conv_w, m_conv_b, m_w_down, m_post_ffn_norm, v_pre_mix_norm, v_w_in, v_v_norm_g, v_v_norm_b, v_w_spatial, v_b_spatial, v_out_norm_a, v_out_norm_b, v_w_out, v_post_mix_norm, v_pre_ffn_norm, v_w_up, v_conv_w, v_conv_b, v_w_down, v_post_ffn_norm):
    w = dict(pre_mix_norm=pre_mix_norm, w_in=w_in, v_norm_g=v_norm_g, v_norm_b=v_norm_b, w_spatial=w_spatial,
             b_spatial=b_spatial, out_norm_a=out_norm_a, out_norm_b=out_norm_b, w_out=w_out,
             post_mix_norm=post_mix_norm, pre_ffn_norm=pre_ffn_norm, w_up=w_up, conv_w=conv_w, conv_b=conv_b,
             w_down=w_down, post_ffn_norm=post_ffn_norm)
    m = dict(pre_mix_norm=m_pre_mix_norm, w_in=m_w_in, v_norm_g=m_v_norm_g, v_norm_b=m_v_norm_b,
             w_spatial=m_w_spatial, b_spatial=m_b_spatial, out_norm_a=m_out_norm_a, out_norm_b=m_out_norm_b,
             w_out=m_w_out, post_mix_norm=m_post_mix_norm, pre_ffn_norm=m_pre_ffn_norm, w_up=m_w_up,
             conv_w=m_conv_w, conv_b=m_conv_b, w_down=m_w_down, post_ffn_norm=m_post_ffn_norm)
    v = dict(pre_mix_norm=v_pre_mix_norm, w_in=v_w_in, v_norm_g=v_v_norm_g, v_norm_b=v_v_norm_b,
             w_spatial=v_w_spatial, b_spatial=v_b_spatial, out_norm_a=v_out_norm_a, out_norm_b=v_out_norm_b,
             w_out=v_w_out, post_mix_norm=v_post_mix_norm, pre_ffn_norm=v_pre_ffn_norm, w_up=v_w_up,
             conv_w=v_conv_w, conv_b=v_conv_b, w_down=v_w_down, post_ffn_norm=v_post_ffn_norm)
    pos = jnp.stack([lax.axis_index("x"), lax.axis_index("y"), lax.axis_index("c")]).astype(jnp.int32)
    chip = 2 * lax.axis_index("x") + lax.axis_index("y")

    cw_cols = conv_w.shape[-1]
    cw_slab = lax.dynamic_update_slice(jnp.zeros((DEPTH, 3, 2 * D_FF), F32), conv_w, (0, 0, chip * cw_cols))
    conv_w_full = _allreduce_small(cw_slab.reshape(-1, LANES), "gather_conv_w").reshape(DEPTH, 3, 2 * D_FF)
    conv_w_full = conv_w_full * 0.5
    blocks = [[w[n][l].astype(BF16) for n in BIG] for l in range(DEPTH)]
    wg = dict(zip(BIG, _gather_weights(blocks[0], "gather_weights_l0")))

    small = {n: w[n] for n in SMALL}
    xs, target = x[0], loss_target[0]
    tabs = _rope_tables(xs.shape[0])
    params = [_layer_params(l, small, conv_w_full) for l in range(DEPTH)]
    saved, wgs = [], []
    xin = xs
    h = _rms_cast(xin, params[0]["pre_mix_norm"], "pre_mix_l0")
    for l in range(DEPTH):
        sv, gathered = _layer_forward(l, xin, h, params[l], wg, tabs, blocks[l + 1] if l + 1 < DEPTH else None)
        saved.append(sv)
        wgs.append(wg)
        if l + 1 < DEPTH:
            wg = dict(zip(BIG, gathered))
            xin, h = _residual_norm(sv["x1"], sv["f"], params[l]["post_ffn_norm"], params[l + 1]["pre_mix_norm"],
                                    f"post_ffn_l{l}")
    loss_part, dx = _residual_loss(saved[-1]["x1"], saved[-1]["f"], params[-1]["post_ffn_norm"], target, "loss")
    smalls, shards = [None] * DEPTH, [None] * DEPTH
    pending = None
    for l in reversed(range(DEPTH)):
        dx, big, smalls[l], scattered = _layer_backward(l, dx, saved[l], params[l], wgs[l], tabs,
                                                        pending[1] if pending else None)
        if pending:
            shards[pending[0]] = _gradient_shards(pending[0], pending[1], scattered, pos)
        pending = (l, _chip_sums(l, big, pos))
    shards[pending[0]] = _gradient_shards(pending[0], pending[1],
                                          _chip_scatter(pending[1], f"chip_scatter_l{pending[0]}"), pos)

    small_shapes = [w[n].shape for n in SMALL]
    stacked = [jnp.stack([smalls[l][n].reshape(w[n].shape[1:]) for l in range(DEPTH)]) for n in SMALL]
    cw_grad = jnp.stack([smalls[l]["conv_w"] for l in range(DEPTH)])
    packed = _pack(stacked + [cw_grad, loss_part])
    total = _allreduce_small(packed, "allreduce_small")
    parts = _unpack(total, small_shapes + [cw_grad.shape, (8, LANES)])
    g_small = dict(zip(SMALL, parts[:len(SMALL)]))
    loss = parts[-1][0, 0]
    g_conv_w = lax.dynamic_slice(parts[-2], (0, 0, chip * cw_cols), conv_w.shape)

    grads = {n: jnp.stack([shards[l][n] for l in range(DEPTH)]) for n in BIG}
    grads.update(g_small)
    grads["conv_w"] = g_conv_w

    dp, mp, vp = _adamw(_pack([w[n] for n in SMALL]), _pack([g_small[n] for n in SMALL]),
                        _pack([m[n] for n in SMALL]), _pack([v[n] for n in SMALL]), "adamw_small")
    delta = dict(zip(SMALL, _unpack(dp, small_shapes)))
    new_m = dict(zip(SMALL, _unpack(mp, small_shapes)))
    new_v = dict(zip(SMALL, _unpack(vp, small_shapes)))
    for n in BIG + ("conv_w",):
        delta[n], new_m[n], new_v[n] = _adamw_nd(w[n], grads[n], m[n], v[n], "adamw_" + n)

    return (loss, dx[None], *[grads[n] for n in WEIGHTS], *[delta[n] for n in WEIGHTS],
            *[new_m[n] for n in WEIGHTS], *[new_v[n] for n in WEIGHTS])
```

```python
import functools
import math

import jax
import jax.numpy as jnp
import numpy as np
from jax import lax
from jax.experimental import pallas as pl
from jax.experimental.pallas import tpu as pltpu

F32 = jnp.float32
BF16 = jnp.bfloat16
MESH = pl.DeviceIdType.MESH

D_MODEL = 1024
A_WIDTH = 512
A_GROUPS = 4
GROUP_DIM = 128
CHUNK = 128
B_WIDTH = 512
HEAD_DIM = 64
ROT_DIM = 16
ROPE_THETA = 500000.0
DILATIONS = (1, 4, 16)
BAND = 128
IN_COLS = 2560
D_FF = 4096
EPS = 1e-6
NEG_INF = -1e30
N_CHIPS = 4
LANES = 128

ADAM_LR = 0.001
ADAM_B1 = 0.9
ADAM_B2 = 0.999
ADAM_EPS = 1e-08
ADAM_WD = 0.01
ADAM_STEP = 10

VMEM_LIMIT_BYTES = 56 * 1024 * 1024
RSQRT2 = 0.7071067811865476
INV_SQRT_2PI = 0.3989422804014327
GELU_C = 0.7978845608028654
GELU_A = 0.044715

ANY = pl.BlockSpec(memory_space=pl.ANY)
NN = ((1,), (0,))
NT = ((1,), (1,))
TN = ((0,), (0,))


def _cparams(*sem):
    return pltpu.CompilerParams(dimension_semantics=sem, vmem_limit_bytes=VMEM_LIMIT_BYTES)


def _dot(a, b, dims):
    return lax.dot_general(a, b, (dims, ((), ())), preferred_element_type=F32)


def _rsq_mean(a):
    return lax.rsqrt(jnp.mean(a * a, axis=-1, keepdims=True) + EPS)


def _rms_bwd(a, r, g, dz):
    t = dz * g
    da = r * t - a * (r * r * r) * jnp.mean(t * a, axis=-1, keepdims=True)
    return da, dz * a * r


def _colsum(a):
    return jnp.sum(a, axis=0, keepdims=True)


def _gelu_tanh(x):
    u = x * x
    t = jnp.tanh(x * (GELU_C + (GELU_C * GELU_A) * u))
    hx = 0.5 * x
    act = hx + hx * t
    grad = 0.5 + 0.5 * t + (hx - hx * t * t) * (GELU_C + (3.0 * GELU_C * GELU_A) * u)
    return act, grad


def _grid_edges(grid):
    ids = [pl.program_id(ax) for ax in range(len(grid))]
    first = functools.reduce(jnp.logical_and, [i == 0 for i in ids])
    last = functools.reduce(jnp.logical_and, [i == n - 1 for i, n in zip(ids, grid)])
    return first, last


def _matmul(a, b, *, grid, a_spec, b_spec, o_spec, o_shape, o_dtype, dims, nk, kaxis, acc_shape, name, b_2d=None,
            scatter=None):
    ns = 0 if scatter is None else len(scatter[0])

    def body(*refs):
        a_ref, b_ref = refs[:2]
        o_ref = refs[2 + ns]
        scratch = refs[3 + 2 * ns:]
        if ns:
            start, finish = _scatter_steps(refs[2:2 + ns], refs[3 + ns:3 + 2 * ns], scratch[-2], scratch[-1],
                                           scatter[1])
            first, last = _grid_edges(grid)
            pl.when(first)(start)
        bv = b_ref[...] if b_2d is None else b_ref[...].reshape(b_2d)
        part = _dot(a_ref[...], bv, dims)
        if nk == 1:
            o_ref[...] = part.astype(o_dtype)
        else:
            acc = scratch[0]
            k = pl.program_id(kaxis)

            @pl.when(k == 0)
            def _():
                acc[...] = part

            @pl.when(k > 0)
            def _():
                acc[...] += part

            @pl.when(k == nk - 1)
            def _():
                o_ref[...] = acc[...].astype(o_dtype)

        if ns:
            pl.when(last)(finish)

    sem = tuple("arbitrary" if (ns or (nk > 1 and ax == kaxis)) else "parallel" for ax in range(len(grid)))
    res = pl.pallas_call(
        body, grid=grid, in_specs=[a_spec, b_spec] + [ANY] * ns, out_specs=[o_spec] + [ANY] * ns,
        out_shape=[jax.ShapeDtypeStruct(o_shape, o_dtype)] + (_scattered_shapes(scatter[1]) if ns else []),
        scratch_shapes=([pltpu.VMEM(acc_shape, F32)] if nk > 1 else []) + (_scatter_sems(ns) if ns else []),
        compiler_params=_cparams(*sem), name=name)(a, b, *(scatter[0] if ns else []))
    return (res[0], list(res[1:])) if ns else res[0]


TM = 512
TMM = 1024


TR = 256


def _row_spec(width, col=0):
    return pl.BlockSpec((TR, width), lambda i, col=col: (i, col))


def _vec_spec(width):
    return pl.BlockSpec((1, width), lambda i: (0, 0))


def _rms_cast(x, g, name):
    s, d = x.shape

    def body(x_ref, g_ref, h_ref):
        a = x_ref[...]
        h_ref[...] = (a * _rsq_mean(a) * g_ref[...]).astype(BF16)

    return pl.pallas_call(
        body, grid=(s // TR,), in_specs=[_row_spec(d), _vec_spec(d)], out_specs=_row_spec(d),
        out_shape=jax.ShapeDtypeStruct((s, d), BF16), compiler_params=_cparams("parallel"), name=name)(x, g)


def _residual_norm(x0, y, g_post, g_next, name):
    s, d = x0.shape

    def body(x_ref, y_ref, gp_ref, gn_ref, x1_ref, h_ref):
        yv = y_ref[...]
        x1 = x_ref[...] + yv * _rsq_mean(yv) * gp_ref[...]
        x1_ref[...] = x1
        h_ref[...] = (x1 * _rsq_mean(x1) * gn_ref[...]).astype(BF16)

    return pl.pallas_call(
        body, grid=(s // TR,), in_specs=[_row_spec(d), _row_spec(d), _vec_spec(d), _vec_spec(d)],
        out_specs=[_row_spec(d), _row_spec(d)],
        out_shape=[jax.ShapeDtypeStruct((s, d), F32), jax.ShapeDtypeStruct((s, d), BF16)],
        compiler_params=_cparams("parallel"), name=name)(x0, y, g_post, g_next)


def _residual_loss(x1, f, g_post, target, name):
    s, d = x1.shape

    def body(x_ref, f_ref, gp_ref, t_ref, loss_ref, dx_ref):
        fv = f_ref[...]
        err = x_ref[...] + fv * _rsq_mean(fv) * gp_ref[...] - t_ref[...]
        dx_ref[...] = err * (1.0 / d)
        part = 0.5 * jnp.sum(jnp.mean(err * err, axis=-1, keepdims=True), axis=0, keepdims=True)

        @pl.when(pl.program_id(0) == 0)
        def _():
            loss_ref[...] = jnp.zeros_like(loss_ref)

        loss_ref[...] += jnp.broadcast_to(part, loss_ref.shape)

    return pl.pallas_call(
        body, grid=(s // TR,), in_specs=[_row_spec(d), _row_spec(d), _vec_spec(d), _row_spec(d)],
        out_specs=[pl.BlockSpec((8, LANES), lambda i: (0, 0)), _row_spec(d)],
        out_shape=[jax.ShapeDtypeStruct((8, LANES), F32), jax.ShapeDtypeStruct((s, d), F32)],
        compiler_params=_cparams("arbitrary"), name=name)(x1, f, g_post, target)


def _acc_init(refs):
    @pl.when(pl.program_id(0) == 0)
    def _():
        for r in refs:
            r[...] = jnp.zeros_like(r)


def _norm_bwd_out(dx, f, g_post, name):
    s, d = dx.shape

    def body(dx_ref, f_ref, g_ref, df_ref, dg_ref):
        _acc_init([dg_ref])
        fv = f_ref[...]
        dz = dx_ref[...]
        da, dgt = _rms_bwd(fv, _rsq_mean(fv), g_ref[...], dz)
        df_ref[...] = da.astype(BF16)
        dg_ref[...] += _colsum(dgt)

    return pl.pallas_call(
        body, grid=(s // TR,), in_specs=[_row_spec(d), _row_spec(d), _vec_spec(d)],
        out_specs=[_row_spec(d), _vec_spec(d)],
        out_shape=[jax.ShapeDtypeStruct((s, d), BF16), jax.ShapeDtypeStruct((1, d), F32)],
        compiler_params=_cparams("arbitrary"), name=name)(dx, f, g_post)


def _norm_bwd_mid(dx2, dh2, x1, g_pf, y1, g_pm, name):
    s, d = dx2.shape

    def body(dx2_ref, dh_ref, x1_ref, gpf_ref, y1_ref, gpm_ref, dx1_ref, dy1_ref, dgpf_ref, dgpm_ref):
        _acc_init([dgpf_ref, dgpm_ref])
        x1 = x1_ref[...]
        da, dgt = _rms_bwd(x1, _rsq_mean(x1), gpf_ref[...], dh_ref[...])
        dx1 = dx2_ref[...] + da
        dx1_ref[...] = dx1
        dgpf_ref[...] += _colsum(dgt)
        y1 = y1_ref[...]
        dy, dgt2 = _rms_bwd(y1, _rsq_mean(y1), gpm_ref[...], dx1)
        dy1_ref[...] = dy.astype(BF16)
        dgpm_ref[...] += _colsum(dgt2)

    return pl.pallas_call(
        body, grid=(s // TR,),
        in_specs=[_row_spec(d), _row_spec(d), _row_spec(d), _vec_spec(d), _row_spec(d), _vec_spec(d)],
        out_specs=[_row_spec(d), _row_spec(d), _vec_spec(d), _vec_spec(d)],
        out_shape=[jax.ShapeDtypeStruct((s, d), F32), jax.ShapeDtypeStruct((s, d), BF16),
                   jax.ShapeDtypeStruct((1, d), F32), jax.ShapeDtypeStruct((1, d), F32)],
        compiler_params=_cparams("arbitrary"), name=name)(dx2, dh2, x1, g_pf, y1, g_pm)


def _norm_bwd_in(dx1, dh1, x0, g1, name):
    s, d = dx1.shape

    def body(dx1_ref, dh_ref, x0_ref, g_ref, dx0_ref, dg_ref):
        _acc_init([dg_ref])
        x0 = x0_ref[...]
        da, dgt = _rms_bwd(x0, _rsq_mean(x0), g_ref[...], dh_ref[...])
        dx0_ref[...] = dx1_ref[...] + da
        dg_ref[...] += _colsum(dgt)

    return pl.pallas_call(
        body, grid=(s // TR,), in_specs=[_row_spec(d), _row_spec(d), _row_spec(d), _vec_spec(d)],
        out_specs=[_row_spec(d), _vec_spec(d)],
        out_shape=[jax.ShapeDtypeStruct((s, d), F32), jax.ShapeDtypeStruct((1, d), F32)],
        compiler_params=_cparams("arbitrary"), name=name)(dx1, dh1, x0, g1)


def _tril_mask():
    row = lax.broadcasted_iota(jnp.int32, (CHUNK, CHUNK), 0)
    col = lax.broadcasted_iota(jnp.int32, (CHUNK, CHUNK), 1)
    return row >= col


def _gating_forward(pa, gv, bv, wt, bsf):
    er = lax.erf(pa * RSQRT2)
    za = 0.5 * pa * (1.0 + er)
    u = za[:, :A_WIDTH]
    va = za[:, A_WIDTH:]
    xc = va - jnp.mean(va, axis=-1, keepdims=True)
    rs = lax.rsqrt(jnp.mean(xc * xc, axis=-1, keepdims=True) + EPS)
    vn = xc * rs
    vlb = (vn * gv + bv).astype(BF16)
    sg = jnp.concatenate(
        [_dot(wt[g], vlb[:, g * GROUP_DIM:(g + 1) * GROUP_DIM], NN) for g in range(A_GROUPS)], axis=1) + bsf
    return er, u, rs, vn, vlb, sg


def _masked_ws(ws_ref):
    mask = _tril_mask()
    return [jnp.where(mask, ws_ref[g], 0.0).astype(BF16) for g in range(A_GROUPS)]


def _mixer_a_fwd(proj, gv, bv, ws, bsf, ga, name):
    s = proj.shape[0]

    def body(p_ref, gv_ref, bv_ref, ws_ref, bs_ref, ga_ref, o_ref):
        wt = _masked_ws(ws_ref)
        for ch in range(TR // CHUNK):
            rows = slice(ch * CHUNK, (ch + 1) * CHUNK)
            _, u, _, _, _, sg = _gating_forward(p_ref[rows, :], gv_ref[...], bv_ref[...], wt, bs_ref[...])
            oa = u * sg
            o_ref[rows, :] = (oa * _rsq_mean(oa) * ga_ref[...]).astype(BF16)

    return pl.pallas_call(
        body, grid=(s // TR,),
        in_specs=[_row_spec(2 * A_WIDTH), _vec_spec(A_WIDTH), _vec_spec(A_WIDTH),
                  pl.BlockSpec((A_GROUPS, CHUNK, CHUNK), lambda i: (0, 0, 0)),
                  pl.BlockSpec((CHUNK, A_WIDTH), lambda i: (0, 0)), _vec_spec(A_WIDTH)],
        out_specs=_row_spec(A_WIDTH), out_shape=jax.ShapeDtypeStruct((s, A_WIDTH + B_WIDTH), BF16),
        compiler_params=_cparams("parallel"), name=name)(proj, gv, bv, ws, bsf, ga)


def _mixer_a_bwd(proj, dmixed, gv, bv, ws, bsf, ga, name):
    s = proj.shape[0]
    nsteps = s // TR

    def body(p_ref, dm_ref, gv_ref, bv_ref, ws_ref, bs_ref, ga_ref,
             dp_ref, dga_ref, dgv_ref, dbv_ref, dbs_ref, dws_ref):
        _acc_init([dga_ref, dgv_ref, dbv_ref, dbs_ref, dws_ref])
        mask = _tril_mask()
        wt = _masked_ws(ws_ref)
        gvv = gv_ref[...]
        gav = ga_ref[...]
        for ch in range(TR // CHUNK):
            rows = slice(ch * CHUNK, (ch + 1) * CHUNK)
            pa = p_ref[rows, :]
            er, u, rs, vn, vlb, sg = _gating_forward(pa, gvv, bv_ref[...], wt, bs_ref[...])
            oa = u * sg
            doa, dgt = _rms_bwd(oa, _rsq_mean(oa), gav, dm_ref[rows, :])
            dga_ref[...] += _colsum(dgt)
            du = doa * sg
            dsg = doa * u
            dbs_ref[...] += dsg
            dsgb = dsg.astype(BF16)
            dvl = []
            for g in range(A_GROUPS):
                cols = slice(g * GROUP_DIM, (g + 1) * GROUP_DIM)
                dws_ref[g] += jnp.where(mask, _dot(dsgb[:, cols], vlb[:, cols], NT), 0.0)
                dvl.append(_dot(wt[g], dsgb[:, cols], TN))
            dvl = jnp.concatenate(dvl, axis=1)
            dgv_ref[...] += _colsum(dvl * vn)
            dbv_ref[...] += _colsum(dvl)
            dvn = dvl * gvv
            dva = rs * (dvn - jnp.mean(dvn, axis=-1, keepdims=True)
                        - vn * jnp.mean(dvn * vn, axis=-1, keepdims=True))
            gp = 0.5 * (1.0 + er) + pa * jnp.exp(-0.5 * pa * pa) * INV_SQRT_2PI
            dp_ref[rows, :] = (jnp.concatenate([du, dva], axis=1) * gp).astype(BF16)

        @pl.when(pl.program_id(0) == nsteps - 1)
        def _():
            for g in range(A_GROUPS):
                cols = slice(g * GROUP_DIM, (g + 1) * GROUP_DIM)
                tot = jnp.sum(dbs_ref[:, cols], axis=1, keepdims=True)
                dbs_ref[:, cols] = jnp.broadcast_to(tot, (CHUNK, GROUP_DIM))

    full = lambda *shape: pl.BlockSpec(shape, lambda i: (0,) * len(shape))
    return pl.pallas_call(
        body, grid=(nsteps,),
        in_specs=[_row_spec(2 * A_WIDTH), _row_spec(A_WIDTH), _vec_spec(A_WIDTH), _vec_spec(A_WIDTH),
                  full(A_GROUPS, CHUNK, CHUNK), full(CHUNK, A_WIDTH), _vec_spec(A_WIDTH)],
        out_specs=[_row_spec(2 * A_WIDTH), _vec_spec(A_WIDTH), _vec_spec(A_WIDTH), _vec_spec(A_WIDTH),
                   full(CHUNK, A_WIDTH), full(A_GROUPS, CHUNK, CHUNK)],
        out_shape=[jax.ShapeDtypeStruct((s, IN_COLS), BF16), jax.ShapeDtypeStruct((1, A_WIDTH), F32),
                   jax.ShapeDtypeStruct((1, A_WIDTH), F32), jax.ShapeDtypeStruct((1, A_WIDTH), F32),
                   jax.ShapeDtypeStruct((CHUNK, A_WIDTH), F32),
                   jax.ShapeDtypeStruct((A_GROUPS, CHUNK, CHUNK), F32)],
        compiler_params=_cparams("arbitrary"), name=name)(proj, dmixed, gv, bv, ws, bsf, ga)


def _rope_tables(s):
    half = ROT_DIM // 2
    inv = ROPE_THETA ** (-jnp.arange(0, ROT_DIM, 2, dtype=F32) / ROT_DIM)
    ang = jnp.arange(s, dtype=F32)[:, None] * inv[None, :]
    cos, sin = jnp.cos(ang), jnp.sin(ang)
    zeros = jnp.zeros((s, HEAD_DIM - ROT_DIM), F32)
    zh = jnp.zeros((s, half), F32)
    c = jnp.concatenate([cos, cos, zeros + 1.0], axis=1)
    s1 = jnp.concatenate([-sin, zh, zeros], axis=1)
    s2 = jnp.concatenate([zh, sin, zeros], axis=1)
    return tuple(jnp.concatenate([t, t], axis=1) for t in (c, s1, s2))


def _lane_blocks(width):
    return [slice(b * LANES, (b + 1) * LANES) for b in range(width // LANES)]


CLASS_DILS = tuple(d for d in DILATIONS if d > 1)


def _class_shape(s, dil, dtype):
    return jax.ShapeDtypeStruct((dil, s // dil, B_WIDTH), dtype)


def _class_spec(dil):
    return pl.BlockSpec((dil, TR // dil, B_WIDTH), lambda i, *_: (0, i, 0))


NBLK = B_WIDTH // LANES
STAGE = pltpu.VMEM((NBLK, TR, LANES), F32)


def _stage_put(stage, value):
    for b, sl in enumerate(_lane_blocks(B_WIDTH)):
        stage[b] = value[:, sl]


def _stage_get(stage):
    return jnp.concatenate([stage[b] for b in range(NBLK)], axis=1)


def _store_classes(stage, dst_ref, dil):
    for b, sl in enumerate(_lane_blocks(B_WIDTH)):
        for r in range(dil):
            dst_ref[r, :, sl] = stage[b, pl.ds(r, TR // dil, stride=dil), :].astype(dst_ref.dtype)


def _load_classes(src_ref, stage, dil):
    for b, sl in enumerate(_lane_blocks(B_WIDTH)):
        for r in range(dil):
            stage[b, pl.ds(r, TR // dil, stride=dil), :] = src_ref[r, :, sl].astype(F32)
    return _stage_get(stage)


def _rope_fwd(proj, tabs, name):
    s = proj.shape[0]
    half = ROT_DIM // 2
    scale = HEAD_DIM ** -0.5
    nlay = 1 + len(CLASS_DILS)

    def body(q_ref, k_ref, v_ref, c_ref, s1_ref, s2_ref, *rest):
        outs, stage = rest[:3 * nlay], rest[3 * nlay]
        c, s1, s2 = c_ref[...], s1_ref[...], s2_ref[...]
        for which, (src, mul) in enumerate(((q_ref, scale), (k_ref, 1.0), (v_ref, None))):
            if mul is None:
                _stage_put(stage, src[...])
            else:
                for b, sl in enumerate(_lane_blocks(B_WIDTH)):
                    a = src[:, sl]
                    r = a * c + pltpu.roll(a, LANES - half, 1) * s1 + pltpu.roll(a, half, 1) * s2
                    stage[b] = r * mul
            dst = outs[which * nlay:(which + 1) * nlay]
            dst[0][...] = _stage_get(stage).astype(BF16)
            for ref, d in zip(dst[1:], CLASS_DILS):
                _store_classes(stage, ref, d)

    tab = pl.BlockSpec((TR, LANES), lambda i: (i, 0))
    lay_specs = [_row_spec(B_WIDTH)] + [_class_spec(d) for d in CLASS_DILS]
    lay_shapes = [jax.ShapeDtypeStruct((s, B_WIDTH), BF16)] + [_class_shape(s, d, BF16) for d in CLASS_DILS]
    outs = pl.pallas_call(
        body, grid=(s // TR,),
        in_specs=[_row_spec(B_WIDTH, 2), _row_spec(B_WIDTH, 3), _row_spec(B_WIDTH, 4), tab, tab, tab],
        out_specs=lay_specs * 3, out_shape=lay_shapes * 3, scratch_shapes=[STAGE],
        compiler_params=_cparams("parallel"), name=name)(proj, proj, proj, *tabs)
    q, k, v = (dict(zip(DILATIONS, outs[w * nlay:(w + 1) * nlay])) for w in range(3))
    return q, k, v


def _as_classes(t):
    return t if t.ndim == 3 else t[None]


def _band_mask(i):
    qi = lax.broadcasted_iota(jnp.int32, (BAND, 2 * BAND), 0)
    kj = lax.broadcasted_iota(jnp.int32, (BAND, 2 * BAND), 1)
    return (kj >= qi) & (kj <= qi + BAND) & ((kj >= BAND) | (i > 0))


def _head_masks():
    lane = lax.broadcasted_iota(jnp.int32, (1, LANES), 1)
    return lane < HEAD_DIM, lane >= HEAD_DIM


def _stack_heads(t):
    lo, hi = _head_masks()
    zero = jnp.zeros_like(t)
    return jnp.concatenate([jnp.where(lo, t, zero), jnp.where(hi, t, zero)], axis=0)


def _attn_specs(last):
    cur = pl.BlockSpec((None, BAND, B_WIDTH), lambda r, i: (r, jnp.minimum(i, last), 0))
    prev = pl.BlockSpec((None, BAND, B_WIDTH), lambda r, i: (r, jnp.maximum(jnp.minimum(i, last) - 1, 0), 0))
    return cur, prev


def _attn_fwd(q, k, v, name, gather=None):
    dil, n, _ = q.shape
    nb = n // BAND
    ng = 0 if gather is None else len(gather)

    def body(*refs):
        q_ref, kc_ref, kp_ref, vc_ref, vp_ref = refs[:5]
        o_ref, l_ref = refs[5 + ng:7 + ng]
        if ng:
            start, relay, finish = _gather_steps(refs[5:5 + ng], refs[7 + ng:7 + 2 * ng], *refs[7 + 2 * ng:])
            first, last = _grid_edges((dil, nb))
            pl.when(first)(start)
        valid = _band_mask(pl.program_id(1))
        valid = jnp.concatenate([valid, valid], axis=0)
        lo, _ = _head_masks()
        for sl in _lane_blocks(B_WIDTH):
            kk = jnp.concatenate([kp_ref[:, sl], kc_ref[:, sl]], axis=0)
            vv = jnp.concatenate([vp_ref[:, sl], vc_ref[:, sl]], axis=0)
            sc = jnp.where(valid, _dot(_stack_heads(q_ref[:, sl]), kk, NT), NEG_INF)
            mx = jnp.max(sc, axis=1, keepdims=True)
            p = jnp.exp(sc - mx)
            den = jnp.sum(p, axis=1, keepdims=True)
            out = _dot(p.astype(BF16), vv, NN) / den
            lse = mx + jnp.log(den)
            o_ref[:, sl] = jnp.where(lo, out[:BAND], out[BAND:])
            l_ref[:, sl] = jnp.where(lo, lse[:BAND], lse[BAND:])

        if ng:
            @pl.when(last)
            def _():
                relay()
                finish()

    cur, prev = _attn_specs(nb - 1)
    sem = ("arbitrary", "arbitrary") if ng else ("parallel", "parallel")
    res = pl.pallas_call(
        body, grid=(dil, nb), in_specs=[cur, cur, prev, cur, prev] + [ANY] * ng, out_specs=[cur, cur] + [ANY] * ng,
        out_shape=[jax.ShapeDtypeStruct((dil, n, B_WIDTH), F32)] * 2 + _gathered_shapes(gather or []),
        scratch_shapes=_gather_sems(ng) if ng else [],
        compiler_params=_cparams(*sem), name=name)(q, k, k, v, v, *(gather or []))
    return res[0], res[1], list(res[2:])


def _attn_combine(outs, lses, gb, mixed, name):
    s = mixed.shape[0]
    npat = len(DILATIONS)
    w = B_WIDTH

    def body(*refs):
        o_refs, l_refs = refs[:npat], refs[npat:2 * npat]
        g_ref, _, ob_ref = refs[2 * npat:2 * npat + 3]
        lse_refs = refs[2 * npat + 3:3 * npat + 3]
        mb_ref, stage = refs[3 * npat + 3:]
        os_ = [o_refs[0][...]] + [_load_classes(r, stage, d) for r, d in zip(o_refs[1:], CLASS_DILS)]
        ls = [l_refs[0][...]] + [_load_classes(r, stage, d) for r, d in zip(l_refs[1:], CLASS_DILS)]
        mx = functools.reduce(jnp.maximum, ls)
        ws = [jnp.exp(l - mx) for l in ls]
        tot = functools.reduce(lambda a, b: a + b, ws)
        ob = functools.reduce(lambda a, b: a + b, [wt / tot * o for wt, o in zip(ws, os_)])
        ob_ref[...] = ob
        lse = mx + jnp.log(tot)
        _stage_put(stage, lse)
        lse_refs[0][...] = lse
        for ref, d in zip(lse_refs[1:], CLASS_DILS):
            _store_classes(stage, ref, d)
        mb_ref[...] = (ob * _rsq_mean(ob) * g_ref[...]).astype(BF16)

    lay_specs = [_row_spec(w)] + [_class_spec(d) for d in CLASS_DILS]
    res = pl.pallas_call(
        body, grid=(s // TR,), in_specs=lay_specs * 2 + [_vec_spec(w), ANY],
        out_specs=[_row_spec(w)] + lay_specs + [_row_spec(w, 1)],
        out_shape=[jax.ShapeDtypeStruct((s, w), F32), jax.ShapeDtypeStruct((s, w), F32)]
        + [_class_shape(s, d, F32) for d in CLASS_DILS] + [jax.ShapeDtypeStruct(mixed.shape, mixed.dtype)],
        scratch_shapes=[STAGE], input_output_aliases={2 * npat + 1: npat + 1},
        compiler_params=_cparams("parallel"), name=name)(*outs, *lses, gb, mixed)
    return res[0], dict(zip(DILATIONS, res[1:npat + 1])), res[npat + 1]


def _attn_bwd_prep(dmixed, ob, gb, name):
    s = ob.shape[0]
    w = B_WIDTH
    nlay = len(DILATIONS)

    def body(dm_ref, ob_ref, g_ref, *rest):
        do_refs, dl_refs = rest[:nlay], rest[nlay:2 * nlay]
        dg_ref, stage = rest[2 * nlay:]
        _acc_init([dg_ref])
        ob = ob_ref[...]
        dob, dgt = _rms_bwd(ob, _rsq_mean(ob), g_ref[...], dm_ref[...])
        dg_ref[...] += _colsum(dgt)
        _stage_put(stage, dob)
        do_refs[0][...] = dob.astype(BF16)
        for ref, d in zip(do_refs[1:], CLASS_DILS):
            _store_classes(stage, ref, d)
        lo, hi = _head_masks()
        t = dob * ob
        for b, sl in enumerate(_lane_blocks(w)):
            tb = t[:, sl]
            s0 = jnp.sum(jnp.where(lo, tb, 0.0), axis=1, keepdims=True)
            s1 = jnp.sum(jnp.where(hi, tb, 0.0), axis=1, keepdims=True)
            stage[b] = jnp.where(lo, s0, s1)
        dl_refs[0][...] = _stage_get(stage)
        for ref, d in zip(dl_refs[1:], CLASS_DILS):
            _store_classes(stage, ref, d)

    lay_specs = [_row_spec(w)] + [_class_spec(d) for d in CLASS_DILS]
    shapes = lambda dt: [jax.ShapeDtypeStruct((s, w), dt)] + [_class_shape(s, d, dt) for d in CLASS_DILS]
    res = pl.pallas_call(
        body, grid=(s // TR,), in_specs=[_row_spec(w, 1), _row_spec(w), _vec_spec(w)],
        out_specs=lay_specs * 2 + [_vec_spec(w)],
        out_shape=shapes(BF16) + shapes(F32) + [jax.ShapeDtypeStruct((1, w), F32)],
        scratch_shapes=[STAGE],
        compiler_params=_cparams("arbitrary"), name=name)(dmixed, ob, gb)
    return dict(zip(DILATIONS, res[:nlay])), dict(zip(DILATIONS, res[nlay:2 * nlay])), res[2 * nlay]


def _attn_bwd(q, k, v, do, lse, delta, name, scatter=None):
    dil, n, _ = q.shape
    nb = n // BAND
    ns = 0 if scatter is None else len(scatter[0])

    def body(*refs):
        q_ref, kc_ref, kp_ref, vc_ref, vp_ref, do_ref, lse_ref, dl_ref = refs[:8]
        dq_ref, dk_ref, dv_ref = refs[8 + ns:11 + ns]
        ck_ref, cv_ref = refs[11 + 2 * ns:13 + 2 * ns]
        i = pl.program_id(1)
        if ns:
            start, finish = _scatter_steps(refs[8:8 + ns], refs[11 + ns:11 + 2 * ns], *refs[13 + 2 * ns:],
                                           scatter[1])
            first, last = _grid_edges((dil, nb + 1))
            pl.when(first)(start)

        @pl.when(i == 0)
        def _():
            ck_ref[...] = jnp.zeros_like(ck_ref)
            cv_ref[...] = jnp.zeros_like(cv_ref)

        @pl.when(i < nb)
        def _():
            valid = _band_mask(i)
            valid = jnp.concatenate([valid, valid], axis=0)
            lo, _ = _head_masks()
            lane = lax.broadcasted_iota(jnp.int32, (1, LANES), 1)

            def per_head(t):
                return jnp.concatenate(
                    [jnp.sum(jnp.where(lane == first, t, 0.0), axis=1, keepdims=True) for first in (0, HEAD_DIM)], axis=0)

            for sl in _lane_blocks(B_WIDTH):
                q2 = _stack_heads(q_ref[:, sl])
                do2 = _stack_heads(do_ref[:, sl])
                kk = jnp.concatenate([kp_ref[:, sl], kc_ref[:, sl]], axis=0)
                vv = jnp.concatenate([vp_ref[:, sl], vc_ref[:, sl]], axis=0)
                p = jnp.where(valid, jnp.exp(_dot(q2, kk, NT) - per_head(lse_ref[:, sl])), 0.0)
                ds = (p * (_dot(do2, vv, NT) - per_head(dl_ref[:, sl]))).astype(BF16)
                dq = _dot(ds, kk, NN)
                dkk = _dot(ds, q2, TN)
                dvv = _dot(p.astype(BF16), do2, TN)
                dq_ref[:, sl] = jnp.where(lo, dq[:BAND], dq[BAND:])
                dk_ref[:, sl] = ck_ref[:, sl] + dkk[:BAND]
                dv_ref[:, sl] = cv_ref[:, sl] + dvv[:BAND]
                ck_ref[:, sl] = dkk[BAND:]
                cv_ref[:, sl] = dvv[BAND:]

        @pl.when(i == nb)
        def _():
            dk_ref[...] = ck_ref[...]
            dv_ref[...] = cv_ref[...]

        if ns:
            pl.when(last)(finish)

    cur, prev = _attn_specs(nb - 1)
    lag = pl.BlockSpec((None, BAND, B_WIDTH), lambda r, i: (r, jnp.maximum(i - 1, 0), 0))
    shape = jax.ShapeDtypeStruct((dil, n, B_WIDTH), F32)
    res = pl.pallas_call(
        body, grid=(dil, nb + 1), in_specs=[cur, cur, prev, cur, prev, cur, cur, cur] + [ANY] * ns,
        out_specs=[cur, lag, lag] + [ANY] * ns,
        out_shape=[shape] * 3 + (_scattered_shapes(scatter[1]) if ns else []),
        scratch_shapes=[pltpu.VMEM((BAND, B_WIDTH), F32)] * 2 + (_scatter_sems(ns) if ns else []),
        compiler_params=_cparams("arbitrary", "arbitrary"), name=name)(q, k, k, v, v, do, lse, delta,
                                                                      *(scatter[0] if ns else []))
    return res[0], res[1], res[2], list(res[3:])


def _rope_bwd(dqs, dks, dvs, tabs, dproj, name):
    s = dproj.shape[0]
    half = ROT_DIM // 2
    scale = HEAD_DIM ** -0.5
    npat = len(DILATIONS)
    w = B_WIDTH

    def body(*refs):
        groups = [refs[g * npat:(g + 1) * npat] for g in range(3)]
        c_ref, s1_ref, s2_ref, _, o_ref, stage = refs[3 * npat:]

        def total(rs):
            acc = rs[0][...]
            for ref, d in zip(rs[1:], CLASS_DILS):
                acc = acc + _load_classes(ref, stage, d)
            return acc

        def unrope(g):
            c, s1, s2 = c_ref[...], s1_ref[...], s2_ref[...]
            for sl in _lane_blocks(w):
                gb = g[:, sl]
                o = gb * c + pltpu.roll(gb * s1, half, 1) + pltpu.roll(gb * s2, LANES - half, 1)
                o_ref[:, sl] = o.astype(BF16)

        which = pl.program_id(1)

        @pl.when(which == 0)
        def _():
            unrope(total(groups[0]) * scale)

        @pl.when(which == 1)
        def _():
            unrope(total(groups[1]))

        @pl.when(which == 2)
        def _():
            o_ref[...] = total(groups[2]).astype(BF16)

    tab = pl.BlockSpec((TR, LANES), lambda i, j: (i, 0))
    nat = pl.BlockSpec((TR, w), lambda i, j: (i, 0))
    lay_specs = [nat] + [_class_spec(d) for d in CLASS_DILS]
    first_col = 2 * A_WIDTH // w
    return pl.pallas_call(
        body, grid=(s // TR, 3), in_specs=lay_specs * 3 + [tab] * 3 + [ANY],
        out_specs=pl.BlockSpec((TR, w), lambda i, j: (i, first_col + j)),
        out_shape=jax.ShapeDtypeStruct(dproj.shape, dproj.dtype), scratch_shapes=[STAGE],
        input_output_aliases={3 * npat + 3: 0},
        compiler_params=_cparams("parallel", "arbitrary"), name=name)(*dqs, *dks, *dvs, *tabs, dproj)


TK = 512
HALO = 16


def _row_of(v, r):
    rows = lax.broadcasted_iota(jnp.int32, (v.shape[0], 1), 0)
    return jnp.sum(jnp.where(rows == r, v, 0.0), axis=0, keepdims=True)


def _taps_before(x, halo):
    row = lax.broadcasted_iota(jnp.int32, (x.shape[0], 1), 0)
    m1 = jnp.where(row == 0, _row_of(halo, HALO - 1), pltpu.roll(x, 1, 0))
    m2 = jnp.where(row == 0, _row_of(halo, HALO - 2), jnp.where(row == 1, _row_of(halo, HALO - 1), pltpu.roll(x, 2, 0)))
    return m2, m1, x


def _taps_after(x, halo):
    rows = x.shape[0]
    row = lax.broadcasted_iota(jnp.int32, (rows, 1), 0)
    p1 = jnp.where(row == rows - 1, _row_of(halo, 0), pltpu.roll(x, rows - 1, 0))
    p2 = jnp.where(row == rows - 2, _row_of(halo, 0), jnp.where(row == rows - 1, _row_of(halo, 1), pltpu.roll(x, rows - 2, 0)))
    return p1, p2


def _conv_value(taps, cw_ref, cb_ref, h):
    return cb_ref[h] + cw_ref[h, 0:1, :] * taps[0] + cw_ref[h, 1:2, :] * taps[1] + cw_ref[h, 2:3, :] * taps[2]


def _ffn_weight_specs(ncol):
    per_up = (2 * D_FF // N_CHIPS) // TK
    per_dn = (D_FF // N_CHIPS) // TK
    wg = pl.BlockSpec((None, None, D_MODEL, TK), lambda i, j: (j // per_up, 0, 0, j % per_up))
    wv = pl.BlockSpec((None, None, D_MODEL, TK), lambda i, j: ((j + ncol) // per_up, 0, 0, (j + ncol) % per_up))
    wd = pl.BlockSpec((None, None, TK, D_MODEL), lambda i, j: (j // per_dn, 0, j % per_dn, 0))
    cw = pl.BlockSpec((2, 3, TK), lambda i, j: (0, 0, j))
    cb = pl.BlockSpec((2, 1, TK), lambda i, j: (0, 0, j))
    return wg, wv, wd, cw, cb


def _ffn_forward(h2, w_up, w_down, cw3, cb3, name, gather=None):
    s = h2.shape[0]
    nm, ncol = s // TM, D_FF // TK
    ng = 0 if gather is None else len(gather)

    def body(*refs):
        h_ref, wg_ref, wv_ref, wd_ref, cw_ref, cb_ref = refs[:6]
        g_in = refs[6:6 + ng]
        y_ref, up_ref, cv_ref, f_ref = refs[6 + ng:10 + ng]
        g_out = refs[10 + ng:10 + 2 * ng]
        carry, acc = refs[10 + 2 * ng:12 + 2 * ng]
        i, j = pl.program_id(0), pl.program_id(1)
        if ng:
            start, relay, finish = _gather_steps(g_in, g_out, *refs[12 + 2 * ng:])
            pl.when((i == 0) & (j == 0))(start)
            pl.when((i == nm - 1) & (j == 0))(relay)

        @pl.when((i == 0) & (j == 0))
        def _():
            carry[...] = jnp.zeros_like(carry)

        h = h_ref[...]
        conv = []
        for hh, w_ref in ((0, wg_ref), (1, wv_ref)):
            up = _dot(h, w_ref[...], NN).astype(BF16)
            up_ref[hh] = up
            x = up.astype(F32)
            conv.append(_conv_value(_taps_before(x, carry[j, hh]), cw_ref, cb_ref, hh))
            cv_ref[hh] = conv[hh].astype(BF16)
            carry[j, hh] = x[TM - HALO:, :]
        y = (_gelu_tanh(conv[0])[0] * conv[1]).astype(BF16)
        y_ref[...] = y
        part = _dot(y, wd_ref[...], NN)

        @pl.when(j == 0)
        def _():
            acc[...] = part

        @pl.when(j > 0)
        def _():
            acc[...] += part

        @pl.when(j == ncol - 1)
        def _():
            f_ref[...] = acc[...]

        if ng:
            pl.when((i == nm - 1) & (j == ncol - 1))(finish)

    wg, wv, wd, cw, cb = _ffn_weight_specs(ncol)
    res = pl.pallas_call(
        body, grid=(nm, ncol),
        in_specs=[pl.BlockSpec((TM, D_MODEL), lambda i, j: (i, 0)), wg, wv, wd, cw, cb] + [ANY] * ng,
        out_specs=[pl.BlockSpec((TM, TK), lambda i, j: (i, j)), pl.BlockSpec((2, TM, TK), lambda i, j: (0, i, j)),
                   pl.BlockSpec((2, TM, TK), lambda i, j: (0, i, j)),
                   pl.BlockSpec((TM, D_MODEL), lambda i, j: (i, 0))] + [ANY] * ng,
        out_shape=[jax.ShapeDtypeStruct((s, D_FF), BF16), jax.ShapeDtypeStruct((2, s, D_FF), BF16),
                   jax.ShapeDtypeStruct((2, s, D_FF), BF16),
                   jax.ShapeDtypeStruct((s, D_MODEL), F32)] + _gathered_shapes(gather or []),
        scratch_shapes=[pltpu.VMEM((ncol, 2, HALO, TK), F32), pltpu.VMEM((TM, D_MODEL), F32)]
        + (_gather_sems(ng) if ng else []),
        compiler_params=_cparams("arbitrary", "arbitrary"), name=name)(h2, w_up, w_up, w_down, cw3, cb3,
                                                                      *(gather or []))
    return res[:4], list(res[4:])


def _ffn_backward(df, w_up, w_down, up3, cv3, cw3, name, scatter=None):
    s = df.shape[0]
    nm, ncol = s // TM, D_FF // TK
    ns = 0 if scatter is None else len(scatter[0])

    def body(*refs):
        df_ref, wg_ref, wv_ref, wd_ref, cw_ref, up_ref, cv_ref = refs[:7]
        s_in = refs[7:7 + ns]
        dup_ref, dh_ref, sums_ref = refs[7 + ns:10 + ns]
        s_out = refs[10 + ns:10 + 2 * ns]
        carry, acc = refs[10 + 2 * ns:12 + 2 * ns]
        i, j = pl.program_id(0), pl.program_id(1)
        if ns:
            start, finish = _scatter_steps(s_in, s_out, *refs[12 + 2 * ns:], scatter[1])
            pl.when((i == 0) & (j == 0))(start)

        @pl.when((i == 0) & (j == 0))
        def _():
            carry[...] = jnp.zeros_like(carry)
            sums_ref[...] = jnp.zeros_like(sums_ref)

        dy = _dot(df_ref[...], wd_ref[...], NT)
        act, grad = _gelu_tanh(cv_ref[0].astype(F32))
        dcs = (dy * cv_ref[1].astype(F32) * grad, dy * act)
        row = lax.broadcasted_iota(jnp.int32, (8, 1), 0)
        part = None
        for hh, w_ref in ((0, wg_ref), (1, wv_ref)):
            dc = dcs[hh]
            x = up_ref[hh].astype(F32)
            after1, after2 = _taps_after(dc, carry[j, hh])
            upd = jnp.zeros((8, TK), F32)
            for ridx, sm in enumerate((_colsum(after2 * x), _colsum(after1 * x), _colsum(dc * x), _colsum(dc))):
                upd = jnp.where(row == ridx, sm, upd)
            sums_ref[j, hh] += upd
            dup = (cw_ref[hh, 2:3, :] * dc + cw_ref[hh, 1:2, :] * after1 + cw_ref[hh, 0:1, :] * after2).astype(BF16)
            carry[j, hh] = dc[:HALO, :]
            dup_ref[hh] = dup
            d = _dot(dup, w_ref[...], NT)
            part = d if part is None else part + d

        @pl.when(j == 0)
        def _():
            acc[...] = part

        @pl.when(j > 0)
        def _():
            acc[...] += part

        @pl.when(j == ncol - 1)
        def _():
            dh_ref[...] = acc[...]

        if ns:
            pl.when((i == nm - 1) & (j == ncol - 1))(finish)

    wg, wv, wd, cw, _ = _ffn_weight_specs(ncol)
    rev = lambda i: nm - 1 - i
    res = pl.pallas_call(
        body, grid=(nm, ncol),
        in_specs=[pl.BlockSpec((TM, D_MODEL), lambda i, j: (rev(i), 0)), wg, wv, wd, cw,
                  pl.BlockSpec((2, TM, TK), lambda i, j: (0, rev(i), j)),
                  pl.BlockSpec((2, TM, TK), lambda i, j: (0, rev(i), j))] + [ANY] * ns,
        out_specs=[pl.BlockSpec((2, TM, TK), lambda i, j: (0, rev(i), j)),
                   pl.BlockSpec((TM, D_MODEL), lambda i, j: (rev(i), 0)),
                   pl.BlockSpec((ncol, 2, 8, TK), lambda i, j: (0, 0, 0, 0))] + [ANY] * ns,
        out_shape=[jax.ShapeDtypeStruct((2, s, D_FF), BF16), jax.ShapeDtypeStruct((s, D_MODEL), F32),
                   jax.ShapeDtypeStruct((ncol, 2, 8, TK), F32)] + (_scattered_shapes(scatter[1]) if ns else []),
        scratch_shapes=[pltpu.VMEM((ncol, 2, HALO, TK), F32), pltpu.VMEM((TM, D_MODEL), F32)]
        + (_scatter_sems(ns) if ns else []),
        compiler_params=_cparams("arbitrary", "arbitrary"), name=name)(df, w_up, w_up, w_down, cw3, up3, cv3,
                                                                      *(scatter[0] if ns else []))
    return res[:3], list(res[3:])


def _wspec(rows, cols, index_map):
    return pl.BlockSpec((None, None, rows, cols), index_map)


def _layer_forward(l, x0, h1, p, wg, tabs, gather=None, late=None):
    s = x0.shape[0]
    nm = s // TMM
    tag = f"_l{l}"
    proj = _matmul(
        h1, wg["w_in"], grid=(nm, N_CHIPS), a_spec=pl.BlockSpec((TMM, D_MODEL), lambda i, j: (i, 0)),
        b_spec=_wspec(D_MODEL, IN_COLS // N_CHIPS, lambda i, j: (j, 0, 0, 0)),
        o_spec=pl.BlockSpec((TMM, IN_COLS // N_CHIPS), lambda i, j: (i, j)), o_shape=(s, IN_COLS), o_dtype=F32,
        dims=NN, nk=1, kaxis=None, acc_shape=None, name="proj" + tag)
    ma = _mixer_a_fwd(proj, p["v_norm_g"], p["v_norm_b"], p["w_spatial"], p["bs_full"], p["out_norm_a"],
                      "mixer_a_fwd" + tag)
    q, k, v = _rope_fwd(proj, tabs, "rope_fwd" + tag)
    riders = dict.fromkeys(DILATIONS)
    if late is not None:
        half = late["w_up"].shape[1] // 2
        riders = dict(zip(DILATIONS, ([late["w_out"], late["w_down"]], [late["w_up"][:, :half]],
                                      [late["w_up"][:, half:]])))
    outs, lses, landed = zip(*[
        _attn_fwd(_as_classes(q[d]), _as_classes(k[d]), _as_classes(v[d]), f"attn_fwd_d{d}" + tag, riders[d])
        for d in DILATIONS])
    if late is not None:
        wg = dict(wg, w_out=landed[0][0], w_down=landed[0][1],
                  w_up=jnp.concatenate([landed[1][0], landed[2][0]], axis=-1))
    outs = [o.reshape(s, B_WIDTH) if d == 1 else o for o, d in zip(outs, DILATIONS)]
    lses = [t.reshape(s, B_WIDTH) if d == 1 else t for t, d in zip(lses, DILATIONS)]
    ob, lse, mixed = _attn_combine(outs, lses, p["out_norm_b"], ma, "attn_combine" + tag)
    w_out_all = pl.BlockSpec((N_CHIPS, None, D_MODEL // N_CHIPS, D_MODEL), lambda i: (0, 0, 0, 0))
    y1 = _matmul(
        mixed, wg["w_out"], grid=(nm,), a_spec=pl.BlockSpec((TMM, D_MODEL), lambda i: (i, 0)), b_spec=w_out_all,
        o_spec=pl.BlockSpec((TMM, D_MODEL), lambda i: (i, 0)), o_shape=(s, D_MODEL), o_dtype=F32,
        dims=NN, nk=1, kaxis=None, acc_shape=None, name="mix_out" + tag, b_2d=(D_MODEL, D_MODEL))
    x1, h2 = _residual_norm(x0, y1, p["post_mix_norm"], p["pre_ffn_norm"], "post_mix" + tag)
    (y, up3, cv3, f), gathered = _ffn_forward(h2, wg["w_up"], wg["w_down"], p["cw3"], p["cb3"], "ffn_fwd" + tag,
                                              gather)
    saved = dict(x0=x0, h1=h1, proj=proj, q=q, k=k, v=v, ob=ob, lse=lse, mixed=mixed, y1=y1, x1=x1, h2=h2,
                 up3=up3, cv3=cv3, y=y, f=f)
    return saved, gathered, wg


def _layer_backward(l, dx2, sv, p, wg, tabs, pos, scatter=None, hide=False):
    s = dx2.shape[0]
    nm = s // TMM
    tag = f"_l{l}"
    g = {}
    df, g["post_ffn_norm"] = _norm_bwd_out(dx2, sv["f"], p["post_ffn_norm"], "norm_bwd_out" + tag)
    (dup3, dh2, conv_sums), scattered = _ffn_backward(df, wg["w_up"], wg["w_down"], sv["up3"], sv["cv3"], p["cw3"],
                                                      "ffn_bwd" + tag, scatter)
    sums = conv_sums.transpose(1, 2, 0, 3).reshape(2, 8, D_FF)
    g["conv_w"] = jnp.concatenate([sums[0, :3], sums[1, :3]], axis=1)
    g["conv_b"] = jnp.concatenate([sums[0, 3:4], sums[1, 3:4]], axis=1)
    tn = 1024
    done = {}
    gw_down = _matmul(
        sv["y"], df, grid=(D_FF // tn, 2, nm), a_spec=pl.BlockSpec((TMM, tn), lambda k, h, m: (m, k)),
        b_spec=pl.BlockSpec((TMM, D_MODEL // 2), lambda k, h, m: (m, h)),
        o_spec=pl.BlockSpec((None, tn, D_MODEL // 2), lambda k, h, m: (h, k, 0)),
        o_shape=(2, D_FF, D_MODEL // 2), o_dtype=BF16,
        dims=TN, nk=nm, kaxis=2, acc_shape=(tn, D_MODEL // 2), name="w_down_grad" + tag)
    down_sums = _chip_sums(l, dict(w_down=gw_down), pos, ("w_down",)) if hide else None
    gw_up = _matmul(
        sv["h2"], dup3, grid=(2 * D_FF // tn, nm), a_spec=pl.BlockSpec((TMM, D_MODEL), lambda n, m: (m, 0)),
        b_spec=pl.BlockSpec((None, TMM, tn), lambda n, m: (n // (D_FF // tn), m, n % (D_FF // tn))),
        o_spec=pl.BlockSpec((None, D_MODEL, tn), lambda n, m: (n // 2, 0, n % 2)),
        o_shape=(N_CHIPS, D_MODEL, 2 * D_FF // N_CHIPS), o_dtype=BF16,
        dims=TN, nk=nm, kaxis=1, acc_shape=(D_MODEL, tn), name="w_up_grad" + tag,
        scatter=(down_sums, ("w_down",)) if hide else None)
    up_sums = None
    if hide:
        gw_up, received = gw_up
        done[("w_down",)] = (down_sums, received)
        up_sums = _chip_sums(l, dict(w_up=gw_up), pos, ("w_up",))
    dx1, dy1, g["pre_ffn_norm"], g["post_mix_norm"] = _norm_bwd_mid(
        dx2, dh2, sv["x1"], p["pre_ffn_norm"], sv["y1"], p["post_mix_norm"], "norm_bwd_mid" + tag)
    w_out_all = pl.BlockSpec((N_CHIPS, None, D_MODEL // N_CHIPS, D_MODEL), lambda i: (0, 0, 0, 0))
    dmixed = _matmul(
        dy1, wg["w_out"], grid=(nm,), a_spec=pl.BlockSpec((TMM, D_MODEL), lambda i: (i, 0)), b_spec=w_out_all,
        o_spec=pl.BlockSpec((TMM, D_MODEL), lambda i: (i, 0)), o_shape=(s, D_MODEL), o_dtype=F32,
        dims=NT, nk=1, kaxis=None, acc_shape=None, name="mix_out_bwd" + tag, b_2d=(D_MODEL, D_MODEL))
    gw_out = _matmul(
        sv["mixed"], dy1, grid=(2, nm), a_spec=pl.BlockSpec((TMM, D_MODEL), lambda h, m: (m, 0)),
        b_spec=pl.BlockSpec((TMM, D_MODEL // 2), lambda h, m: (m, h)),
        o_spec=pl.BlockSpec((None, D_MODEL, D_MODEL // 2), lambda h, m: (h, 0, 0)),
        o_shape=(2, D_MODEL, D_MODEL // 2), o_dtype=BF16,
        dims=TN, nk=nm, kaxis=1, acc_shape=(D_MODEL, D_MODEL // 2), name="w_out_grad" + tag)
    dpa, g["out_norm_a"], g["v_norm_g"], g["v_norm_b"], dbs, g["w_spatial"] = _mixer_a_bwd(
        sv["proj"], dmixed, p["v_norm_g"], p["v_norm_b"], p["w_spatial"], p["bs_full"], p["out_norm_a"],
        "mixer_a_bwd" + tag)
    g["b_spatial"] = dbs[:, ::GROUP_DIM].T
    dob, delta, g["out_norm_b"] = _attn_bwd_prep(dmixed, sv["ob"], p["out_norm_b"], "attn_bwd_prep" + tag)
    last_dil = DILATIONS[-1]
    dqs, dks, dvs, received = zip(*[
        _attn_bwd(*(_as_classes(t[d]) for t in (sv["q"], sv["k"], sv["v"], dob, sv["lse"], delta)),
                  f"attn_bwd_d{d}" + tag, (up_sums, ("w_up",)) if hide and d == last_dil else None)
        for d in DILATIONS])
    if hide:
        done[("w_up",)] = (up_sums, received[-1])
    nat = lambda ts: [t.reshape(s, B_WIDTH) if d == 1 else t for t, d in zip(ts, DILATIONS)]
    dproj = _rope_bwd(nat(dqs), nat(dks), nat(dvs), tabs, dpa, "rope_bwd" + tag)
    wcol = IN_COLS // N_CHIPS
    dh1 = _matmul(
        dproj, wg["w_in"], grid=(nm, N_CHIPS), a_spec=pl.BlockSpec((TMM, wcol), lambda i, n: (i, n)),
        b_spec=_wspec(D_MODEL, wcol, lambda i, n: (n, 0, 0, 0)),
        o_spec=pl.BlockSpec((TMM, D_MODEL), lambda i, n: (i, 0)), o_shape=(s, D_MODEL), o_dtype=F32,
        dims=NT, nk=N_CHIPS, kaxis=1, acc_shape=(TMM, D_MODEL), name="proj_bwd" + tag)
    gw_in = _matmul(
        sv["h1"], dproj, grid=(N_CHIPS, nm), a_spec=pl.BlockSpec((TMM, D_MODEL), lambda n, m: (m, 0)),
        b_spec=pl.BlockSpec((TMM, wcol), lambda n, m: (m, n)),
        o_spec=pl.BlockSpec((None, D_MODEL, wcol), lambda n, m: (n, 0, 0)),
        o_shape=(N_CHIPS, D_MODEL, wcol), o_dtype=BF16,
        dims=TN, nk=nm, kaxis=1, acc_shape=(D_MODEL, wcol), name="w_in_grad" + tag)
    dx0, g["pre_mix_norm"] = _norm_bwd_in(dx1, dh1, sv["x0"], p["pre_mix_norm"], "norm_bwd_in" + tag)
    big = dict(w_in=gw_in, w_out=gw_out) if hide else dict(w_in=gw_in, w_up=gw_up, w_out=gw_out, w_down=gw_down)
    return dx0, big, g, scattered, done


SMALL = ("pre_mix_norm", "v_norm_g", "v_norm_b", "w_spatial", "b_spatial", "out_norm_a", "out_norm_b",
         "post_mix_norm", "pre_ffn_norm", "conv_b", "post_ffn_norm")
BIG = ("w_in", "w_out", "w_up", "w_down")
DEPTH = 2


def _layer_params(l, small, conv_w_full):
    p = {n: small[n][l].reshape(1, -1) for n in SMALL if n not in ("w_spatial", "b_spatial")}
    p["w_spatial"] = small["w_spatial"][l]
    p["bs_full"] = jnp.repeat(small["b_spatial"][l].T, GROUP_DIM, axis=1)
    p["cw3"] = conv_w_full[l].reshape(3, 2, D_FF).transpose(1, 0, 2)
    p["cb3"] = small["conv_b"][l].reshape(2, 1, D_FF)
    return p


def _mesh_pos():
    return lax.axis_index("x"), lax.axis_index("y"), lax.axis_index("c")


def _other_chips(x, y):
    return [(1 - x, y), (x, 1 - y), (1 - x, 1 - y)]


def _gathered_shapes(blocks):
    return [jax.ShapeDtypeStruct((N_CHIPS, 1) + a.shape, a.dtype) for a in blocks]


def _gather_sems(nw):
    n = 2 * nw * (N_CHIPS - 1) + nw
    return [pltpu.SemaphoreType.DMA((n,)), pltpu.SemaphoreType.DMA((n,))]


def _gather_steps(ins, outs, send, recv):
    nw, nrel = len(ins), N_CHIPS - 1
    x, y, c = _mesh_pos()
    mine, sibling, chips = 2 * x + y, (x, y, 1 - c), _other_chips(x, y)

    def copy(src, dst, slot, to):
        return pltpu.make_async_remote_copy(src_ref=src, dst_ref=dst, send_sem=send.at[slot],
                                            recv_sem=recv.at[slot], device_id=to, device_id_type=MESH)

    def half_rows(t, core):
        rows = ins[t].shape[0] // 2
        return pl.ds(pl.multiple_of(core * rows, rows), rows)

    def landing(t, chip, core):
        return outs[t].at[chip, 0, half_rows(t, core), :]

    slots = [(t, r, chip) for t in range(nw) for r, chip in enumerate(chips)]
    own = [copy(ins[t], outs[t].at[mine, 0], 2 * nw * nrel + t, sibling) for t in range(nw)]
    first = [copy(ins[t].at[half_rows(t, c), :], landing(t, mine, c), t * nrel + r, (px, py, c))
             for t, r, (px, py) in slots]
    relays = [copy(landing(t, 2 * px + py, c), landing(t, 2 * px + py, c), nw * nrel + t * nrel + r, sibling)
              for t, r, (px, py) in slots]

    def start():
        for cp in own + first:
            cp.start()

    def relay():
        for (t, r, (px, py)), cp in zip(slots, relays):
            copy(landing(t, 2 * px + py, c), landing(t, 2 * px + py, c), t * nrel + r, (px, py, c)).wait_recv()
            cp.start()

    def finish():
        for t, r, (px, py) in slots:
            passed = landing(t, 2 * px + py, 1 - c)
            copy(passed, passed, nw * nrel + t * nrel + r, sibling).wait_recv()
        for cp in first + relays:
            cp.wait_send()
        for cp in own:
            cp.wait()

    return start, relay, finish


def _gather_weights(blocks, name):
    nw = len(blocks)

    def body(*refs):
        start, relay, finish = _gather_steps(refs[:nw], refs[nw:2 * nw], *refs[2 * nw:])
        start()
        relay()
        finish()

    return pl.pallas_call(
        body, in_specs=[ANY] * nw, out_specs=[ANY] * nw, out_shape=_gathered_shapes(blocks),
        scratch_shapes=_gather_sems(nw), name=name)(*blocks)


HALF = 512

GRAD_GEOM = {"w_in": ("rows", D_MODEL, IN_COLS // N_CHIPS), "w_up": ("rows", D_MODEL, 2 * D_FF // N_CHIPS),
             "w_out": ("cols", D_MODEL, D_MODEL // N_CHIPS), "w_down": ("cols", D_FF, D_FF // N_CHIPS)}


def _exchange_shape(n):
    kind, a, b = GRAD_GEOM[n]
    return (N_CHIPS, HALF, b) if kind == "rows" else (a, HALF)


def _piece_shape(n):
    kind, _, b = GRAD_GEOM[n]
    return (HALF, b) if kind == "rows" else (b, HALF)


def _half_of(ref, n, core):
    if GRAD_GEOM[n][0] == "rows":
        return ref.at[:, pl.ds(pl.multiple_of(core * HALF, HALF), HALF), :]
    return ref.at[core]


def _piece_of(ref, n, chip):
    kind, _, b = GRAD_GEOM[n]
    return ref.at[chip] if kind == "rows" else ref.at[pl.ds(pl.multiple_of(chip * b, b), b), :]


def _pair_exchange(g, names, name):
    n = len(names)

    def body(*refs):
        send, recv = refs[2 * n:]
        x, y, c = _mesh_pos()
        o = 1 - c
        cps = [pltpu.make_async_remote_copy(src_ref=_half_of(refs[t], nm, o), dst_ref=refs[n + t], send_sem=send.at[t],
                                            recv_sem=recv.at[t], device_id=(x, y, o), device_id_type=MESH)
               for t, nm in enumerate(names)]
        for cp in cps:
            cp.start()
        for cp in cps:
            cp.wait()

    return pl.pallas_call(
        body, in_specs=[ANY] * n, out_specs=[ANY] * n,
        out_shape=[jax.ShapeDtypeStruct(_exchange_shape(nm), BF16) for nm in names],
        scratch_shapes=[pltpu.SemaphoreType.DMA((n,)), pltpu.SemaphoreType.DMA((n,))],
        name=name)(*[g[nm] for nm in names])


def _pair_sum(g, recv, pos, names, name_prefix):
    def add(a, b, grid, a_spec, b_spec, name):
        def body(pos_ref, a_ref, b_ref, o_ref):
            o_ref[...] = (a_ref[...].astype(F32) + b_ref[...].astype(F32)).astype(BF16)

        return pl.pallas_call(
            body, grid_spec=pltpu.PrefetchScalarGridSpec(
                num_scalar_prefetch=1, grid=grid, in_specs=[a_spec, b_spec], out_specs=b_spec),
            out_shape=jax.ShapeDtypeStruct(b.shape, BF16), compiler_params=_cparams("parallel"), name=name)(pos, a, b)

    out = []
    for nm, r in zip(names, recv):
        kind, rows, width = GRAD_GEOM[nm]
        if kind == "rows":
            out.append(add(g[nm], r, (N_CHIPS,), pl.BlockSpec((None, HALF, width), lambda j, pos: (j, pos[2], 0)),
                           pl.BlockSpec((None, HALF, width), lambda j, pos: (j, 0, 0)), f"{name_prefix}_{nm}"))
        else:
            out.append(add(g[nm], r, (rows // D_MODEL,), pl.BlockSpec((None, D_MODEL, HALF), lambda j, pos: (pos[2], j, 0)),
                           pl.BlockSpec((D_MODEL, HALF), lambda j, pos: (j, 0)), f"{name_prefix}_{nm}"))
    return out


def _scattered_shapes(names):
    return [jax.ShapeDtypeStruct((N_CHIPS - 1,) + _piece_shape(nm), BF16) for nm in names]


def _scatter_sems(n):
    return [pltpu.SemaphoreType.DMA((n * (N_CHIPS - 1),)), pltpu.SemaphoreType.DMA((n * (N_CHIPS - 1),))]


def _scatter_steps(sums, outs, send, recv, names):
    nrel = N_CHIPS - 1
    x, y, c = _mesh_pos()
    cps = []
    for r, (px, py) in enumerate(_other_chips(x, y)):
        for t, nm in enumerate(names):
            cps.append(pltpu.make_async_remote_copy(
                src_ref=_piece_of(sums[t], nm, 2 * px + py), dst_ref=outs[t].at[r], send_sem=send.at[t * nrel + r],
                recv_sem=recv.at[t * nrel + r], device_id=(px, py, c), device_id_type=MESH))

    def start():
        for cp in cps:
            cp.start()

    def finish():
        for cp in cps:
            cp.wait()

    return start, finish


def _chip_scatter(sums, names, name):
    n = len(names)

    def body(*refs):
        start, finish = _scatter_steps(refs[:n], refs[n:2 * n], *refs[2 * n:], names)
        start()
        finish()

    return pl.pallas_call(
        body, in_specs=[ANY] * n, out_specs=[ANY] * n, out_shape=_scattered_shapes(names),
        scratch_shapes=_scatter_sems(n), name=name)(*sums)


def _chip_sum(sums, recv, pos, names, name_prefix):
    def add(a, b, a_spec, shape, name):
        def body(pos_ref, a_ref, b_ref, o_ref):
            tot = a_ref[...].astype(F32)
            for r in range(N_CHIPS - 1):
                tot = tot + b_ref[r].astype(F32)
            o_ref[...] = tot

        return pl.pallas_call(
            body, grid_spec=pltpu.PrefetchScalarGridSpec(
                num_scalar_prefetch=1, grid=(1,), in_specs=[a_spec, pl.BlockSpec(b.shape, lambda i, pos: (0, 0, 0))],
                out_specs=pl.BlockSpec((None,) + shape, lambda i, pos: (pos[2], 0, 0))),
            out_shape=jax.ShapeDtypeStruct((2,) + shape, F32), compiler_params=_cparams("arbitrary"),
            name=name)(pos, a, b)

    chip = lambda pos: 2 * pos[0] + pos[1]
    out = []
    for nm, a, b in zip(names, sums, recv):
        shape = _piece_shape(nm)
        if GRAD_GEOM[nm][0] == "rows":
            spec = pl.BlockSpec((None,) + shape, lambda i, pos: (chip(pos), 0, 0))
        else:
            spec = pl.BlockSpec(shape, lambda i, pos: (chip(pos), 0))
        out.append(add(a, b, spec, shape, f"{name_prefix}_{nm}"))
    return out


def _pair_share(totals, name):
    n = len(totals)

    def body(*refs):
        ins, outs = refs[:n], refs[n:2 * n]
        send, recv = refs[2 * n:]
        x, y, c = _mesh_pos()
        o = 1 - c
        cps = [pltpu.make_async_remote_copy(src_ref=ins[t].at[c], dst_ref=outs[t].at[c], send_sem=send.at[t],
                                            recv_sem=recv.at[t], device_id=(x, y, o), device_id_type=MESH)
               for t in range(n)]
        for cp in cps:
            cp.start()
        for t in range(n):
            pltpu.make_async_remote_copy(src_ref=ins[t].at[o], dst_ref=outs[t].at[o], send_sem=send.at[t],
                                         recv_sem=recv.at[t], device_id=(x, y, o), device_id_type=MESH).wait_recv()
        for cp in cps:
            cp.wait_send()

    return pl.pallas_call(
        body, in_specs=[ANY] * n, out_specs=[ANY] * n,
        out_shape=[jax.ShapeDtypeStruct(t.shape, t.dtype) for t in totals],
        scratch_shapes=[pltpu.SemaphoreType.DMA((n,)), pltpu.SemaphoreType.DMA((n,))],
        input_output_aliases={t: t for t in range(n)}, name=name)(*totals)


def _chip_sums(l, g, pos, names):
    tag = f"l{l}_" + "_".join(names)
    recv = _pair_exchange(g, names, "pair_exchange_" + tag)
    return _pair_sum(g, recv, pos, names, "pair_sum_" + tag)


def _gradient_shards(l, sums, scattered, pos, names):
    tag = f"l{l}_" + "_".join(names)
    halves = _pair_share(_chip_sum(sums, scattered, pos, names, "chip_sum_" + tag), "pair_share_" + tag)
    out = {}
    for nm, t in zip(names, halves):
        rows, cols = _piece_shape(nm)
        out[nm] = t.reshape(2 * rows, cols) if GRAD_GEOM[nm][0] == "rows" else t.transpose(1, 0, 2).reshape(rows, 2 * cols)
    return out


N_DEV = 8


def _allreduce_small(packed, name):
    rows = packed.shape[0]

    def body(x_ref, out_ref, gath, send_sems, recv_sems, local_sem):
        x, y, c = _mesh_pos()
        me, sibling = (x, y, c), (x, y, 1 - c)
        chips = _other_chips(x, y)

        def blk(px, py, pc):
            return gath.at[pl.ds(pl.multiple_of((4 * px + 2 * py + pc) * rows, 8), rows), :]

        def copy(k, block, to, src=None):
            return pltpu.make_async_remote_copy(
                src_ref=blk(*block) if src is None else src, dst_ref=blk(*block), send_sem=send_sems.at[k],
                recv_sem=recv_sems.at[k], device_id=to, device_id_type=MESH)

        mine = pltpu.make_async_copy(x_ref, blk(*me), local_sem)
        mine.start()
        first = [copy(0, me, sibling, src=x_ref)]
        first += [copy(1 + j, me, (*chip, c), src=x_ref) for j, chip in enumerate(chips)]
        for cp in first:
            cp.start()
        passed = [copy(4 + j, (*chip, c), sibling) for j, chip in enumerate(chips)]
        for j, chip in enumerate(chips):
            copy(1 + j, (*chip, c), me).wait_recv()
            passed[j].start()
        copy(0, sibling, me).wait_recv()
        for j, chip in enumerate(chips):
            copy(4 + j, (*chip, 1 - c), me).wait_recv()
        for cp in first + passed:
            cp.wait_send()
        mine.wait()
        tot = gath[0:rows, :]
        for d in range(1, N_DEV):
            tot = tot + gath[d * rows:(d + 1) * rows, :]
        out_ref[...] = tot

    vmem = pl.BlockSpec(memory_space=pltpu.VMEM)
    return pl.pallas_call(
        body, in_specs=[vmem], out_specs=vmem, out_shape=jax.ShapeDtypeStruct((rows, LANES), F32),
        scratch_shapes=[pltpu.VMEM((N_DEV * rows, LANES), F32), pltpu.SemaphoreType.DMA((7,)),
                        pltpu.SemaphoreType.DMA((7,)), pltpu.SemaphoreType.DMA],
        compiler_params=pltpu.CompilerParams(vmem_limit_bytes=VMEM_LIMIT_BYTES),
        name=name)(packed)


def _adamw(w, g, m, v, name):
    rows, cols = w.shape
    tr = 256 if rows % 256 == 0 else rows

    def body(w_ref, g_ref, m_ref, v_ref, d_ref, mo_ref, vo_ref):
        gv = g_ref[...]
        mn = ADAM_B1 * m_ref[...] + (1.0 - ADAM_B1) * gv
        vn = ADAM_B2 * v_ref[...] + (1.0 - ADAM_B2) * (gv * gv)
        m_hat = mn / (1.0 - ADAM_B1 ** ADAM_STEP)
        v_hat = vn / (1.0 - ADAM_B2 ** ADAM_STEP)
        d_ref[...] = -ADAM_LR * (m_hat / (jnp.sqrt(v_hat) + ADAM_EPS) + ADAM_WD * w_ref[...])
        mo_ref[...] = mn
        vo_ref[...] = vn

    spec = pl.BlockSpec((tr, cols), lambda i: (i, 0))
    return pl.pallas_call(
        body, grid=(rows // tr,), in_specs=[spec] * 4, out_specs=[spec] * 3,
        out_shape=[jax.ShapeDtypeStruct((rows, cols), F32)] * 3, compiler_params=_cparams("parallel"),
        name=name)(w, g, m, v)


def _adamw_nd(w, g, m, v, name):
    cols = w.shape[-1] if w.shape[-1] % LANES == 0 else LANES
    outs = _adamw(*(t.reshape(-1, cols) for t in (w, g, m, v)), name)
    return tuple(t.reshape(w.shape) for t in outs)


def _pack(arrays):
    return jnp.concatenate([a.reshape(-1, LANES) for a in arrays], axis=0)


def _unpack(packed, shapes):
    out, row = [], 0
    for sh in shapes:
        n = math.prod(sh) // LANES
        out.append(packed[row:row + n].reshape(sh))
        row += n
    return out


WEIGHTS = ("pre_mix_norm", "w_in", "v_norm_g", "v_norm_b", "w_spatial", "b_spatial", "out_norm_a", "out_norm_b",
           "w_out", "post_mix_norm", "pre_ffn_norm", "w_up", "conv_w", "conv_b", "w_down", "post_ffn_norm")


def kernel(x, pre_mix_norm, w_in, v_norm_g, v_norm_b, w_spatial, b_spatial, out_norm_a, out_norm_b, w_out, post_mix_norm, pre_ffn_norm, w_up, conv_w, conv_b, w_down, post_ffn_norm, loss_target, m_pre_mix_norm, m_w_in, m_v_norm_g, m_v_norm_b, m_w_spatial, m_b_spatial, m_out_norm_a, m_out_norm_b, m_w_out, m_post_mix_norm, m_pre_ffn_norm, m_w_up, m_conv_w, m_conv_b, m_w_down, m_post_ffn_norm, v_pre_mix_norm, v_w_in, v_v_norm_g, v_v_norm_b, v_w_spatial, v_b_spatial, v_out_norm_a, v_out_norm_b, v_w_out, v_post_mix_norm, v_pre_ffn_norm, v_w_up, v_conv_w, v_conv_b, v_w_down, v_post_ffn_norm):
    w = dict(pre_mix_norm=pre_mix_norm, w_in=w_in, v_norm_g=v_norm_g, v_norm_b=v_norm_b, w_spatial=w_spatial,
             b_spatial=b_spatial, out_norm_a=out_norm_a, out_norm_b=out_norm_b, w_out=w_out,
             post_mix_norm=post_mix_norm, pre_ffn_norm=pre_ffn_norm, w_up=w_up, conv_w=conv_w, conv_b=conv_b,
             w_down=w_down, post_ffn_norm=post_ffn_norm)
    m = dict(pre_mix_norm=m_pre_mix_norm, w_in=m_w_in, v_norm_g=m_v_norm_g, v_norm_b=m_v_norm_b,
             w_spatial=m_w_spatial, b_spatial=m_b_spatial, out_norm_a=m_out_norm_a, out_norm_b=m_out_norm_b,
             w_out=m_w_out, post_mix_norm=m_post_mix_norm, pre_ffn_norm=m_pre_ffn_norm, w_up=m_w_up,
             conv_w=m_conv_w, conv_b=m_conv_b, w_down=m_w_down, post_ffn_norm=m_post_ffn_norm)
    v = dict(pre_mix_norm=v_pre_mix_norm, w_in=v_w_in, v_norm_g=v_v_norm_g, v_norm_b=v_v_norm_b,
             w_spatial=v_w_spatial, b_spatial=v_b_spatial, out_norm_a=v_out_norm_a, out_norm_b=v_out_norm_b,
             w_out=v_w_out, post_mix_norm=v_post_mix_norm, pre_ffn_norm=v_pre_ffn_norm, w_up=v_w_up,
             conv_w=v_conv_w, conv_b=v_conv_b, w_down=v_w_down, post_ffn_norm=v_post_ffn_norm)
    pos = jnp.stack([lax.axis_index("x"), lax.axis_index("y"), lax.axis_index("c")]).astype(jnp.int32)
    chip = 2 * lax.axis_index("x") + lax.axis_index("y")

    cw_cols = conv_w.shape[-1]
    cw_slab = lax.dynamic_update_slice(jnp.zeros((DEPTH, 3, 2 * D_FF), F32), conv_w, (0, 0, chip * cw_cols))
    conv_w_full = _allreduce_small(cw_slab.reshape(-1, LANES), "gather_conv_w").reshape(DEPTH, 3, 2 * D_FF)
    conv_w_full = conv_w_full * 0.5
    blocks = [{n: w[n][l].astype(BF16) for n in BIG} for l in range(DEPTH)]
    wg = dict(w_in=_gather_weights([blocks[0]["w_in"]], "gather_w_in_l0")[0])

    small = {n: w[n] for n in SMALL}
    xs, target = x[0], loss_target[0]
    tabs = _rope_tables(xs.shape[0])
    params = [_layer_params(l, small, conv_w_full) for l in range(DEPTH)]
    saved, wgs = [], []
    xin = xs
    h = _rms_cast(xin, params[0]["pre_mix_norm"], "pre_mix_l0")
    for l in range(DEPTH):
        sv, gathered, wg = _layer_forward(l, xin, h, params[l], wg, tabs,
                                          [blocks[l + 1][n] for n in BIG] if l + 1 < DEPTH else None,
                                          blocks[0] if l == 0 else None)
        saved.append(sv)
        wgs.append(wg)
        if l + 1 < DEPTH:
            wg = dict(zip(BIG, gathered))
            xin, h = _residual_norm(sv["x1"], sv["f"], params[l]["post_ffn_norm"], params[l + 1]["pre_mix_norm"],
                                    f"post_ffn_l{l}")
    loss_part, dx = _residual_loss(saved[-1]["x1"], saved[-1]["f"], params[-1]["post_ffn_norm"], target, "loss")
    smalls, shards = [None] * DEPTH, [{} for _ in range(DEPTH)]
    pending = None
    for l in reversed(range(DEPTH)):
        dx, big, smalls[l], scattered, done = _layer_backward(l, dx, saved[l], params[l], wgs[l], tabs, pos,
                                                              pending[1:] if pending else None, hide=l == 0)
        if pending:
            shards[pending[0]].update(_gradient_shards(pending[0], pending[1], scattered, pos, pending[2]))
        for names, (sums, received) in done.items():
            shards[l].update(_gradient_shards(l, sums, received, pos, names))
        names = tuple(big)
        pending = (l, _chip_sums(l, big, pos, names), names)
    shards[pending[0]].update(_gradient_shards(
        pending[0], pending[1], _chip_scatter(pending[1], pending[2], f"chip_scatter_l{pending[0]}"), pos, pending[2]))

    small_shapes = [w[n].shape for n in SMALL]
    stacked = [jnp.stack([smalls[l][n].reshape(w[n].shape[1:]) for l in range(DEPTH)]) for n in SMALL]
    cw_grad = jnp.stack([smalls[l]["conv_w"] for l in range(DEPTH)])
    packed = _pack(stacked + [cw_grad, loss_part])
    total = _allreduce_small(packed, "allreduce_small")
    parts = _unpack(total, small_shapes + [cw_grad.shape, (8, LANES)])
    g_small = dict(zip(SMALL, parts[:len(SMALL)]))
    loss = parts[-1][0, 0]
    g_conv_w = lax.dynamic_slice(parts[-2], (0, 0, chip * cw_cols), conv_w.shape)

    grads = {n: jnp.stack([shards[l][n] for l in range(DEPTH)]) for n in BIG}
    grads.update(g_small)
    grads["conv_w"] = g_conv_w

    dp, mp, vp = _adamw(_pack([w[n] for n in SMALL]), _pack([g_small[n] for n in SMALL]),
                        _pack([m[n] for n in SMALL]), _pack([v[n] for n in SMALL]), "adamw_small")
    delta = dict(zip(SMALL, _unpack(dp, small_shapes)))
    new_m = dict(zip(SMALL, _unpack(mp, small_shapes)))
    new_v = dict(zip(SMALL, _unpack(vp, small_shapes)))
    for n in BIG + ("conv_w",):
        delta[n], new_m[n], new_v[n] = _adamw_nd(w[n], grads[n], m[n], v[n], "adamw_" + n)

    return (loss, dx[None], *[grads[n] for n in WEIGHTS], *[delta[n] for n in WEIGHTS],
            *[new_m[n] for n in WEIGHTS], *[new_v[n] for n in WEIGHTS])
```

```python
import functools
import math

import jax
import jax.numpy as jnp
import numpy as np
from jax import lax
from jax.experimental import pallas as pl
from jax.experimental.pallas import tpu as pltpu

F32 = jnp.float32
BF16 = jnp.bfloat16
MESH = pl.DeviceIdType.MESH

D_MODEL = 1024
A_WIDTH = 512
A_GROUPS = 4
GROUP_DIM = 128
CHUNK = 128
B_WIDTH = 512
HEAD_DIM = 64
ROT_DIM = 16
ROPE_THETA = 500000.0
DILATIONS = (1, 4, 16)
BAND = 128
IN_COLS = 2560
D_FF = 4096
EPS = 1e-6
NEG_INF = -1e30
N_CHIPS = 4
LANES = 128

ADAM_LR = 0.001
ADAM_B1 = 0.9
ADAM_B2 = 0.999
ADAM_EPS = 1e-08
ADAM_WD = 0.01
ADAM_STEP = 10

VMEM_LIMIT_BYTES = 56 * 1024 * 1024
RSQRT2 = 0.7071067811865476
INV_SQRT_2PI = 0.3989422804014327
GELU_C = 0.7978845608028654
GELU_A = 0.044715

ANY = pl.BlockSpec(memory_space=pl.ANY)
NN = ((1,), (0,))
NT = ((1,), (1,))
TN = ((0,), (0,))


def _cparams(*sem):
    return pltpu.CompilerParams(dimension_semantics=sem, vmem_limit_bytes=VMEM_LIMIT_BYTES)


def _dot(a, b, dims):
    return lax.dot_general(a, b, (dims, ((), ())), preferred_element_type=F32)


def _rsq_mean(a):
    return lax.rsqrt(jnp.mean(a * a, axis=-1, keepdims=True) + EPS)


def _rms_bwd(a, r, g, dz):
    t = dz * g
    da = r * t - a * (r * r * r) * jnp.mean(t * a, axis=-1, keepdims=True)
    return da, dz * a * r


def _colsum(a):
    return jnp.sum(a, axis=0, keepdims=True)


def _gelu_tanh(x):
    u = x * x
    t = jnp.tanh(x * (GELU_C + (GELU_C * GELU_A) * u))
    hx = 0.5 * x
    act = hx + hx * t
    grad = 0.5 + 0.5 * t + (hx - hx * t * t) * (GELU_C + (3.0 * GELU_C * GELU_A) * u)
    return act, grad


def _grid_edges(grid):
    ids = [pl.program_id(ax) for ax in range(len(grid))]
    first = functools.reduce(jnp.logical_and, [i == 0 for i in ids])
    last = functools.reduce(jnp.logical_and, [i == n - 1 for i, n in zip(ids, grid)])
    return first, last


def _matmul(a, b, *, grid, a_spec, b_spec, o_spec, o_shape, o_dtype, dims, nk, kaxis, acc_shape, name, b_2d=None,
            scatter=None, gather=None):
    assert scatter is None or gather is None
    ns = len(scatter[0]) if scatter else len(gather) if gather else 0

    def body(*refs):
        a_ref, b_ref = refs[:2]
        o_ref = refs[2 + ns]
        scratch = refs[3 + 2 * ns:]
        if ns:
            first, last = _grid_edges(grid)
            if scatter:
                start, finish = _scatter_steps(refs[2:2 + ns], refs[3 + ns:3 + 2 * ns], scratch[-2], scratch[-1],
                                               scatter[1])
            else:
                start, relay, last_wait = _gather_steps(refs[2:2 + ns], refs[3 + ns:3 + 2 * ns], scratch[-2],
                                                        scratch[-1])

                def finish():
                    relay()
                    last_wait()
            pl.when(first)(start)
        bv = b_ref[...] if b_2d is None else b_ref[...].reshape(b_2d)
        part = _dot(a_ref[...], bv, dims)
        if nk == 1:
            o_ref[...] = part.astype(o_dtype)
        else:
            acc = scratch[0]
            k = pl.program_id(kaxis)

            @pl.when(k == 0)
            def _():
                acc[...] = part

            @pl.when(k > 0)
            def _():
                acc[...] += part

            @pl.when(k == nk - 1)
            def _():
                o_ref[...] = acc[...].astype(o_dtype)

        if ns:
            pl.when(last)(finish)

    sem = tuple("arbitrary" if (ns or (nk > 1 and ax == kaxis)) else "parallel" for ax in range(len(grid)))
    riding = list(scatter[0]) if scatter else list(gather or [])
    rider_shapes = _scattered_shapes(scatter[1]) if scatter else _gathered_shapes(riding)
    rider_sems = _scatter_sems(ns) if scatter else _gather_sems(ns) if gather else []
    res = pl.pallas_call(
        body, grid=grid, in_specs=[a_spec, b_spec] + [ANY] * ns, out_specs=[o_spec] + [ANY] * ns,
        out_shape=[jax.ShapeDtypeStruct(o_shape, o_dtype)] + rider_shapes,
        scratch_shapes=([pltpu.VMEM(acc_shape, F32)] if nk > 1 else []) + rider_sems,
        compiler_params=_cparams(*sem), name=name)(a, b, *riding)
    return (res[0], list(res[1:])) if ns else res[0]


TM = 512
TMM = 1024


TR = 256


def _row_spec(width, col=0):
    return pl.BlockSpec((TR, width), lambda i, col=col: (i, col))


def _vec_spec(width):
    return pl.BlockSpec((1, width), lambda i: (0, 0))


def _rms_cast(x, g, name):
    s, d = x.shape

    def body(x_ref, g_ref, h_ref):
        a = x_ref[...]
        h_ref[...] = (a * _rsq_mean(a) * g_ref[...]).astype(BF16)

    return pl.pallas_call(
        body, grid=(s // TR,), in_specs=[_row_spec(d), _vec_spec(d)], out_specs=_row_spec(d),
        out_shape=jax.ShapeDtypeStruct((s, d), BF16), compiler_params=_cparams("parallel"), name=name)(x, g)


def _residual_norm(x0, y, g_post, g_next, name):
    s, d = x0.shape

    def body(x_ref, y_ref, gp_ref, gn_ref, x1_ref, h_ref):
        yv = y_ref[...]
        x1 = x_ref[...] + yv * _rsq_mean(yv) * gp_ref[...]
        x1_ref[...] = x1
        h_ref[...] = (x1 * _rsq_mean(x1) * gn_ref[...]).astype(BF16)

    return pl.pallas_call(
        body, grid=(s // TR,), in_specs=[_row_spec(d), _row_spec(d), _vec_spec(d), _vec_spec(d)],
        out_specs=[_row_spec(d), _row_spec(d)],
        out_shape=[jax.ShapeDtypeStruct((s, d), F32), jax.ShapeDtypeStruct((s, d), BF16)],
        compiler_params=_cparams("parallel"), name=name)(x0, y, g_post, g_next)


def _acc_init(refs):
    @pl.when(pl.program_id(0) == 0)
    def _():
        for r in refs:
            r[...] = jnp.zeros_like(r)


def _loss_norm_bwd(x1, f, g_post, target, name):
    s, d = x1.shape

    def body(x_ref, f_ref, gp_ref, t_ref, loss_ref, dx_ref, df_ref, dg_ref):
        _acc_init([loss_ref, dg_ref])
        fv = f_ref[...]
        r = _rsq_mean(fv)
        err = x_ref[...] + fv * r * gp_ref[...] - t_ref[...]
        dx = err * (1.0 / d)
        dx_ref[...] = dx
        part = 0.5 * jnp.sum(jnp.mean(err * err, axis=-1, keepdims=True), axis=0, keepdims=True)
        loss_ref[...] += jnp.broadcast_to(part, loss_ref.shape)
        da, dgt = _rms_bwd(fv, r, gp_ref[...], dx)
        df_ref[...] = da.astype(BF16)
        dg_ref[...] += _colsum(dgt)

    return pl.pallas_call(
        body, grid=(s // TR,), in_specs=[_row_spec(d), _row_spec(d), _vec_spec(d), _row_spec(d)],
        out_specs=[pl.BlockSpec((8, LANES), lambda i: (0, 0)), _row_spec(d), _row_spec(d), _vec_spec(d)],
        out_shape=[jax.ShapeDtypeStruct((8, LANES), F32), jax.ShapeDtypeStruct((s, d), F32),
                   jax.ShapeDtypeStruct((s, d), BF16), jax.ShapeDtypeStruct((1, d), F32)],
        compiler_params=_cparams("arbitrary"), name=name)(x1, f, g_post, target)


def _norm_bwd_mid(dx2, dh2, x1, g_pf, y1, g_pm, name):
    s, d = dx2.shape

    def body(dx2_ref, dh_ref, x1_ref, gpf_ref, y1_ref, gpm_ref, dx1_ref, dy1_ref, dgpf_ref, dgpm_ref):
        _acc_init([dgpf_ref, dgpm_ref])
        x1 = x1_ref[...]
        da, dgt = _rms_bwd(x1, _rsq_mean(x1), gpf_ref[...], dh_ref[...])
        dx1 = dx2_ref[...] + da
        dx1_ref[...] = dx1
        dgpf_ref[...] += _colsum(dgt)
        y1 = y1_ref[...]
        dy, dgt2 = _rms_bwd(y1, _rsq_mean(y1), gpm_ref[...], dx1)
        dy1_ref[...] = dy.astype(BF16)
        dgpm_ref[...] += _colsum(dgt2)

    return pl.pallas_call(
        body, grid=(s // TR,),
        in_specs=[_row_spec(d), _row_spec(d), _row_spec(d), _vec_spec(d), _row_spec(d), _vec_spec(d)],
        out_specs=[_row_spec(d), _row_spec(d), _vec_spec(d), _vec_spec(d)],
        out_shape=[jax.ShapeDtypeStruct((s, d), F32), jax.ShapeDtypeStruct((s, d), BF16),
                   jax.ShapeDtypeStruct((1, d), F32), jax.ShapeDtypeStruct((1, d), F32)],
        compiler_params=_cparams("arbitrary"), name=name)(dx2, dh2, x1, g_pf, y1, g_pm)


def _norm_bwd_in_out(dx1, dh1, x0, g1, f_below, g_post_below, name):
    s, d = dx1.shape

    def body(dx1_ref, dh_ref, x0_ref, g_ref, f_ref, gp_ref, dx0_ref, dg_ref, df_ref, dgp_ref):
        _acc_init([dg_ref, dgp_ref])
        x0 = x0_ref[...]
        da, dgt = _rms_bwd(x0, _rsq_mean(x0), g_ref[...], dh_ref[...])
        dx0 = dx1_ref[...] + da
        dx0_ref[...] = dx0
        dg_ref[...] += _colsum(dgt)
        fv = f_ref[...]
        db, dgt2 = _rms_bwd(fv, _rsq_mean(fv), gp_ref[...], dx0)
        df_ref[...] = db.astype(BF16)
        dgp_ref[...] += _colsum(dgt2)

    return pl.pallas_call(
        body, grid=(s // TR,),
        in_specs=[_row_spec(d), _row_spec(d), _row_spec(d), _vec_spec(d), _row_spec(d), _vec_spec(d)],
        out_specs=[_row_spec(d), _vec_spec(d), _row_spec(d), _vec_spec(d)],
        out_shape=[jax.ShapeDtypeStruct((s, d), F32), jax.ShapeDtypeStruct((1, d), F32),
                   jax.ShapeDtypeStruct((s, d), BF16), jax.ShapeDtypeStruct((1, d), F32)],
        compiler_params=_cparams("arbitrary"), name=name)(dx1, dh1, x0, g1, f_below, g_post_below)


def _norm_bwd_in(dx1, dh1, x0, g1, name):
    s, d = dx1.shape

    def body(dx1_ref, dh_ref, x0_ref, g_ref, dx0_ref, dg_ref):
        _acc_init([dg_ref])
        x0 = x0_ref[...]
        da, dgt = _rms_bwd(x0, _rsq_mean(x0), g_ref[...], dh_ref[...])
        dx0_ref[...] = dx1_ref[...] + da
        dg_ref[...] += _colsum(dgt)

    return pl.pallas_call(
        body, grid=(s // TR,), in_specs=[_row_spec(d), _row_spec(d), _row_spec(d), _vec_spec(d)],
        out_specs=[_row_spec(d), _vec_spec(d)],
        out_shape=[jax.ShapeDtypeStruct((s, d), F32), jax.ShapeDtypeStruct((1, d), F32)],
        compiler_params=_cparams("arbitrary"), name=name)(dx1, dh1, x0, g1)


def _tril_mask():
    row = lax.broadcasted_iota(jnp.int32, (CHUNK, CHUNK), 0)
    col = lax.broadcasted_iota(jnp.int32, (CHUNK, CHUNK), 1)
    return row >= col


def _gating_forward(pa, gv, bv, wt, bsf):
    er = lax.erf(pa * RSQRT2)
    za = 0.5 * pa * (1.0 + er)
    u = za[:, :A_WIDTH]
    va = za[:, A_WIDTH:]
    xc = va - jnp.mean(va, axis=-1, keepdims=True)
    rs = lax.rsqrt(jnp.mean(xc * xc, axis=-1, keepdims=True) + EPS)
    vn = xc * rs
    vlb = (vn * gv + bv).astype(BF16)
    sg = jnp.concatenate(
        [_dot(wt[g], vlb[:, g * GROUP_DIM:(g + 1) * GROUP_DIM], NN) for g in range(A_GROUPS)], axis=1) + bsf
    return er, u, rs, vn, vlb, sg


def _masked_ws(ws_ref):
    mask = _tril_mask()
    return [jnp.where(mask, ws_ref[g], 0.0).astype(BF16) for g in range(A_GROUPS)]


def _mixer_a_fwd(proj, gv, bv, ws, bsf, ga, name):
    s = proj.shape[0]

    def body(p_ref, gv_ref, bv_ref, ws_ref, bs_ref, ga_ref, o_ref):
        wt = _masked_ws(ws_ref)
        for ch in range(TR // CHUNK):
            rows = slice(ch * CHUNK, (ch + 1) * CHUNK)
            _, u, _, _, _, sg = _gating_forward(p_ref[rows, :].astype(F32), gv_ref[...], bv_ref[...], wt, bs_ref[...])
            oa = u * sg
            o_ref[rows, :] = (oa * _rsq_mean(oa) * ga_ref[...]).astype(BF16)

    return pl.pallas_call(
        body, grid=(s // TR,),
        in_specs=[_row_spec(2 * A_WIDTH), _vec_spec(A_WIDTH), _vec_spec(A_WIDTH),
                  pl.BlockSpec((A_GROUPS, CHUNK, CHUNK), lambda i: (0, 0, 0)),
                  pl.BlockSpec((CHUNK, A_WIDTH), lambda i: (0, 0)), _vec_spec(A_WIDTH)],
        out_specs=_row_spec(A_WIDTH), out_shape=jax.ShapeDtypeStruct((s, A_WIDTH + B_WIDTH), BF16),
        compiler_params=_cparams("parallel"), name=name)(proj, gv, bv, ws, bsf, ga)


def _mixer_a_bwd(proj, dmixed, gv, bv, ws, bsf, ga, name):
    s = proj.shape[0]
    nsteps = s // TR

    def body(p_ref, dm_ref, gv_ref, bv_ref, ws_ref, bs_ref, ga_ref,
             dp_ref, dga_ref, dgv_ref, dbv_ref, dbs_ref, dws_ref):
        _acc_init([dga_ref, dgv_ref, dbv_ref, dbs_ref, dws_ref])
        mask = _tril_mask()
        wt = _masked_ws(ws_ref)
        gvv = gv_ref[...]
        gav = ga_ref[...]
        for ch in range(TR // CHUNK):
            rows = slice(ch * CHUNK, (ch + 1) * CHUNK)
            pa = p_ref[rows, :].astype(F32)
            er, u, rs, vn, vlb, sg = _gating_forward(pa, gvv, bv_ref[...], wt, bs_ref[...])
            oa = u * sg
            doa, dgt = _rms_bwd(oa, _rsq_mean(oa), gav, dm_ref[rows, :])
            dga_ref[...] += _colsum(dgt)
            du = doa * sg
            dsg = doa * u
            dbs_ref[...] += dsg
            dsgb = dsg.astype(BF16)
            dvl = []
            for g in range(A_GROUPS):
                cols = slice(g * GROUP_DIM, (g + 1) * GROUP_DIM)
                dws_ref[g] += jnp.where(mask, _dot(dsgb[:, cols], vlb[:, cols], NT), 0.0)
                dvl.append(_dot(wt[g], dsgb[:, cols], TN))
            dvl = jnp.concatenate(dvl, axis=1)
            dgv_ref[...] += _colsum(dvl * vn)
            dbv_ref[...] += _colsum(dvl)
            dvn = dvl * gvv
            dva = rs * (dvn - jnp.mean(dvn, axis=-1, keepdims=True)
                        - vn * jnp.mean(dvn * vn, axis=-1, keepdims=True))
            gp = 0.5 * (1.0 + er) + pa * jnp.exp(-0.5 * pa * pa) * INV_SQRT_2PI
            dp_ref[rows, :] = (jnp.concatenate([du, dva], axis=1) * gp).astype(BF16)

        @pl.when(pl.program_id(0) == nsteps - 1)
        def _():
            for g in range(A_GROUPS):
                cols = slice(g * GROUP_DIM, (g + 1) * GROUP_DIM)
                tot = jnp.sum(dbs_ref[:, cols], axis=1, keepdims=True)
                dbs_ref[:, cols] = jnp.broadcast_to(tot, (CHUNK, GROUP_DIM))

    full = lambda *shape: pl.BlockSpec(shape, lambda i: (0,) * len(shape))
    return pl.pallas_call(
        body, grid=(nsteps,),
        in_specs=[_row_spec(2 * A_WIDTH), _row_spec(A_WIDTH), _vec_spec(A_WIDTH), _vec_spec(A_WIDTH),
                  full(A_GROUPS, CHUNK, CHUNK), full(CHUNK, A_WIDTH), _vec_spec(A_WIDTH)],
        out_specs=[_row_spec(2 * A_WIDTH), _vec_spec(A_WIDTH), _vec_spec(A_WIDTH), _vec_spec(A_WIDTH),
                   full(CHUNK, A_WIDTH), full(A_GROUPS, CHUNK, CHUNK)],
        out_shape=[jax.ShapeDtypeStruct((s, IN_COLS), BF16), jax.ShapeDtypeStruct((1, A_WIDTH), F32),
                   jax.ShapeDtypeStruct((1, A_WIDTH), F32), jax.ShapeDtypeStruct((1, A_WIDTH), F32),
                   jax.ShapeDtypeStruct((CHUNK, A_WIDTH), F32),
                   jax.ShapeDtypeStruct((A_GROUPS, CHUNK, CHUNK), F32)],
        compiler_params=_cparams("arbitrary"), name=name)(proj, dmixed, gv, bv, ws, bsf, ga)


def _rope_tables(s):
    half = ROT_DIM // 2
    inv = ROPE_THETA ** (-jnp.arange(0, ROT_DIM, 2, dtype=F32) / ROT_DIM)
    ang = jnp.arange(s, dtype=F32)[:, None] * inv[None, :]
    cos, sin = jnp.cos(ang), jnp.sin(ang)
    zeros = jnp.zeros((s, HEAD_DIM - ROT_DIM), F32)
    zh = jnp.zeros((s, half), F32)
    c = jnp.concatenate([cos, cos, zeros + 1.0], axis=1)
    s1 = jnp.concatenate([-sin, zh, zeros], axis=1)
    s2 = jnp.concatenate([zh, sin, zeros], axis=1)
    return tuple(jnp.concatenate([t, t], axis=1) for t in (c, s1, s2))


def _lane_blocks(width):
    return [slice(b * LANES, (b + 1) * LANES) for b in range(width // LANES)]


CLASS_DILS = tuple(d for d in DILATIONS if d > 1)


def _class_shape(s, dil, dtype):
    return jax.ShapeDtypeStruct((dil, s // dil, B_WIDTH), dtype)


def _class_spec(dil):
    return pl.BlockSpec((dil, TR // dil, B_WIDTH), lambda i, *_: (0, i, 0))


NBLK = B_WIDTH // LANES
STAGE = pltpu.VMEM((NBLK, TR, LANES), F32)


def _stage_put(stage, value):
    for b, sl in enumerate(_lane_blocks(B_WIDTH)):
        stage[b] = value[:, sl]


def _stage_get(stage):
    return jnp.concatenate([stage[b] for b in range(NBLK)], axis=1)


def _store_classes(stage, dst_ref, dil):
    for b, sl in enumerate(_lane_blocks(B_WIDTH)):
        for r in range(dil):
            dst_ref[r, :, sl] = stage[b, pl.ds(r, TR // dil, stride=dil), :].astype(dst_ref.dtype)


def _load_classes(src_ref, stage, dil):
    for b, sl in enumerate(_lane_blocks(B_WIDTH)):
        for r in range(dil):
            stage[b, pl.ds(r, TR // dil, stride=dil), :] = src_ref[r, :, sl].astype(F32)
    return _stage_get(stage)


def _rope_fwd(proj, tabs, name):
    s = proj.shape[0]
    half = ROT_DIM // 2
    scale = HEAD_DIM ** -0.5
    nlay = 1 + len(CLASS_DILS)

    def body(q_ref, k_ref, v_ref, c_ref, s1_ref, s2_ref, *rest):
        outs, stage = rest[:3 * nlay], rest[3 * nlay]
        c, s1, s2 = c_ref[...], s1_ref[...], s2_ref[...]
        for which, (src, mul) in enumerate(((q_ref, scale), (k_ref, 1.0), (v_ref, None))):
            if mul is None:
                _stage_put(stage, src[...].astype(F32))
            else:
                for b, sl in enumerate(_lane_blocks(B_WIDTH)):
                    a = src[:, sl].astype(F32)
                    r = a * c + pltpu.roll(a, LANES - half, 1) * s1 + pltpu.roll(a, half, 1) * s2
                    stage[b] = r * mul
            dst = outs[which * nlay:(which + 1) * nlay]
            dst[0][...] = _stage_get(stage).astype(BF16)
            for ref, d in zip(dst[1:], CLASS_DILS):
                _store_classes(stage, ref, d)

    tab = pl.BlockSpec((TR, LANES), lambda i: (i, 0))
    lay_specs = [_row_spec(B_WIDTH)] + [_class_spec(d) for d in CLASS_DILS]
    lay_shapes = [jax.ShapeDtypeStruct((s, B_WIDTH), BF16)] + [_class_shape(s, d, BF16) for d in CLASS_DILS]
    outs = pl.pallas_call(
        body, grid=(s // TR,),
        in_specs=[_row_spec(B_WIDTH, 2), _row_spec(B_WIDTH, 3), _row_spec(B_WIDTH, 4), tab, tab, tab],
        out_specs=lay_specs * 3, out_shape=lay_shapes * 3, scratch_shapes=[STAGE],
        compiler_params=_cparams("parallel"), name=name)(proj, proj, proj, *tabs)
    q, k, v = (dict(zip(DILATIONS, outs[w * nlay:(w + 1) * nlay])) for w in range(3))
    return q, k, v


def _as_classes(t):
    return t if t.ndim == 3 else t[None]


def _band_mask(i):
    qi = lax.broadcasted_iota(jnp.int32, (BAND, 2 * BAND), 0)
    kj = lax.broadcasted_iota(jnp.int32, (BAND, 2 * BAND), 1)
    return (kj >= qi) & (kj <= qi + BAND) & ((kj >= BAND) | (i > 0))


def _head_masks():
    lane = lax.broadcasted_iota(jnp.int32, (1, LANES), 1)
    return lane < HEAD_DIM, lane >= HEAD_DIM


def _stack_heads(t):
    lo, hi = _head_masks()
    zero = jnp.zeros_like(t)
    return jnp.concatenate([jnp.where(lo, t, zero), jnp.where(hi, t, zero)], axis=0)


def _attn_specs(last):
    cur = pl.BlockSpec((None, BAND, B_WIDTH), lambda r, i: (r, jnp.minimum(i, last), 0))
    prev = pl.BlockSpec((None, BAND, B_WIDTH), lambda r, i: (r, jnp.maximum(jnp.minimum(i, last) - 1, 0), 0))
    return cur, prev


def _attn_fwd(q, k, v, name, gather=None):
    dil, n, _ = q.shape
    nb = n // BAND
    ng = 0 if gather is None else len(gather)

    def body(*refs):
        q_ref, kc_ref, kp_ref, vc_ref, vp_ref = refs[:5]
        o_ref, l_ref = refs[5 + ng:7 + ng]
        if ng:
            start, relay, finish = _gather_steps(refs[5:5 + ng], refs[7 + ng:7 + 2 * ng], *refs[7 + 2 * ng:])
            first, last = _grid_edges((dil, nb))
            pl.when(first)(start)
        valid = _band_mask(pl.program_id(1))
        valid = jnp.concatenate([valid, valid], axis=0)
        lo, _ = _head_masks()
        for sl in _lane_blocks(B_WIDTH):
            kk = jnp.concatenate([kp_ref[:, sl], kc_ref[:, sl]], axis=0)
            vv = jnp.concatenate([vp_ref[:, sl], vc_ref[:, sl]], axis=0)
            sc = jnp.where(valid, _dot(_stack_heads(q_ref[:, sl]), kk, NT), NEG_INF)
            mx = jnp.max(sc, axis=1, keepdims=True)
            p = jnp.exp(sc - mx)
            den = jnp.sum(p, axis=1, keepdims=True)
            out = _dot(p.astype(BF16), vv, NN) / den
            lse = mx + jnp.log(den)
            o_ref[:, sl] = jnp.where(lo, out[:BAND], out[BAND:]).astype(BF16)
            l_ref[:, sl] = jnp.where(lo, lse[:BAND], lse[BAND:])

        if ng:
            @pl.when(last)
            def _():
                relay()
                finish()

    cur, prev = _attn_specs(nb - 1)
    sem = ("arbitrary", "arbitrary") if ng else ("parallel", "parallel")
    res = pl.pallas_call(
        body, grid=(dil, nb), in_specs=[cur, cur, prev, cur, prev] + [ANY] * ng, out_specs=[cur, cur] + [ANY] * ng,
        out_shape=[jax.ShapeDtypeStruct((dil, n, B_WIDTH), BF16), jax.ShapeDtypeStruct((dil, n, B_WIDTH), F32)]
        + _gathered_shapes(gather or []),
        scratch_shapes=_gather_sems(ng) if ng else [],
        compiler_params=_cparams(*sem), name=name)(q, k, k, v, v, *(gather or []))
    return res[0], res[1], list(res[2:])


def _attn_combine(outs, lses, gb, mixed, name):
    s = mixed.shape[0]
    npat = len(DILATIONS)
    w = B_WIDTH

    def body(*refs):
        o_refs, l_refs = refs[:npat], refs[npat:2 * npat]
        g_ref, _, ob_ref = refs[2 * npat:2 * npat + 3]
        lse_refs = refs[2 * npat + 3:3 * npat + 3]
        mb_ref, stage = refs[3 * npat + 3:]
        os_ = [o_refs[0][...].astype(F32)] + [_load_classes(r, stage, d) for r, d in zip(o_refs[1:], CLASS_DILS)]
        ls = [l_refs[0][...]] + [_load_classes(r, stage, d) for r, d in zip(l_refs[1:], CLASS_DILS)]
        mx = functools.reduce(jnp.maximum, ls)
        ws = [jnp.exp(l - mx) for l in ls]
        tot = functools.reduce(lambda a, b: a + b, ws)
        ob = functools.reduce(lambda a, b: a + b, [wt / tot * o for wt, o in zip(ws, os_)])
        ob_ref[...] = ob
        lse = mx + jnp.log(tot)
        _stage_put(stage, lse)
        lse_refs[0][...] = lse
        for ref, d in zip(lse_refs[1:], CLASS_DILS):
            _store_classes(stage, ref, d)
        mb_ref[...] = (ob * _rsq_mean(ob) * g_ref[...]).astype(BF16)

    lay_specs = [_row_spec(w)] + [_class_spec(d) for d in CLASS_DILS]
    res = pl.pallas_call(
        body, grid=(s // TR,), in_specs=lay_specs * 2 + [_vec_spec(w), ANY],
        out_specs=[_row_spec(w)] + lay_specs + [_row_spec(w, 1)],
        out_shape=[jax.ShapeDtypeStruct((s, w), F32), jax.ShapeDtypeStruct((s, w), F32)]
        + [_class_shape(s, d, F32) for d in CLASS_DILS] + [jax.ShapeDtypeStruct(mixed.shape, mixed.dtype)],
        scratch_shapes=[STAGE], input_output_aliases={2 * npat + 1: npat + 1},
        compiler_params=_cparams("parallel"), name=name)(*outs, *lses, gb, mixed)
    return res[0], dict(zip(DILATIONS, res[1:npat + 1])), res[npat + 1]


def _attn_bwd_prep(dmixed, ob, gb, name):
    s = ob.shape[0]
    w = B_WIDTH
    nlay = len(DILATIONS)

    def body(dm_ref, ob_ref, g_ref, *rest):
        do_refs, dl_refs = rest[:nlay], rest[nlay:2 * nlay]
        dg_ref, stage = rest[2 * nlay:]
        _acc_init([dg_ref])
        ob = ob_ref[...]
        dob, dgt = _rms_bwd(ob, _rsq_mean(ob), g_ref[...], dm_ref[...])
        dg_ref[...] += _colsum(dgt)
        _stage_put(stage, dob)
        do_refs[0][...] = dob.astype(BF16)
        for ref, d in zip(do_refs[1:], CLASS_DILS):
            _store_classes(stage, ref, d)
        lo, hi = _head_masks()
        t = dob * ob
        for b, sl in enumerate(_lane_blocks(w)):
            tb = t[:, sl]
            s0 = jnp.sum(jnp.where(lo, tb, 0.0), axis=1, keepdims=True)
            s1 = jnp.sum(jnp.where(hi, tb, 0.0), axis=1, keepdims=True)
            stage[b] = jnp.where(lo, s0, s1)
        dl_refs[0][...] = _stage_get(stage)
        for ref, d in zip(dl_refs[1:], CLASS_DILS):
            _store_classes(stage, ref, d)

    lay_specs = [_row_spec(w)] + [_class_spec(d) for d in CLASS_DILS]
    shapes = lambda dt: [jax.ShapeDtypeStruct((s, w), dt)] + [_class_shape(s, d, dt) for d in CLASS_DILS]
    res = pl.pallas_call(
        body, grid=(s // TR,), in_specs=[_row_spec(w, 1), _row_spec(w), _vec_spec(w)],
        out_specs=lay_specs * 2 + [_vec_spec(w)],
        out_shape=shapes(BF16) + shapes(F32) + [jax.ShapeDtypeStruct((1, w), F32)],
        scratch_shapes=[STAGE],
        compiler_params=_cparams("arbitrary"), name=name)(dmixed, ob, gb)
    return dict(zip(DILATIONS, res[:nlay])), dict(zip(DILATIONS, res[nlay:2 * nlay])), res[2 * nlay]


def _attn_bwd(q, k, v, do, lse, delta, name, scatter=None):
    dil, n, _ = q.shape
    nb = n // BAND
    ns = 0 if scatter is None else len(scatter[0])

    def body(*refs):
        q_ref, kc_ref, kp_ref, vc_ref, vp_ref, do_ref, lse_ref, dl_ref = refs[:8]
        dq_ref, dk_ref, dv_ref = refs[8 + ns:11 + ns]
        ck_ref, cv_ref = refs[11 + 2 * ns:13 + 2 * ns]
        i = pl.program_id(1)
        if ns:
            start, finish = _scatter_steps(refs[8:8 + ns], refs[11 + ns:11 + 2 * ns], *refs[13 + 2 * ns:],
                                           scatter[1])
            first, last = _grid_edges((dil, nb + 1))
            pl.when(first)(start)

        @pl.when(i == 0)
        def _():
            ck_ref[...] = jnp.zeros_like(ck_ref)
            cv_ref[...] = jnp.zeros_like(cv_ref)

        @pl.when(i < nb)
        def _():
            valid = _band_mask(i)
            valid = jnp.concatenate([valid, valid], axis=0)
            lo, _ = _head_masks()
            lane = lax.broadcasted_iota(jnp.int32, (1, LANES), 1)

            def per_head(t):
                return jnp.concatenate(
                    [jnp.sum(jnp.where(lane == first, t, 0.0), axis=1, keepdims=True) for first in (0, HEAD_DIM)], axis=0)

            for sl in _lane_blocks(B_WIDTH):
                q2 = _stack_heads(q_ref[:, sl])
                do2 = _stack_heads(do_ref[:, sl])
                kk = jnp.concatenate([kp_ref[:, sl], kc_ref[:, sl]], axis=0)
                vv = jnp.concatenate([vp_ref[:, sl], vc_ref[:, sl]], axis=0)
                p = jnp.where(valid, jnp.exp(_dot(q2, kk, NT) - per_head(lse_ref[:, sl])), 0.0)
                ds = (p * (_dot(do2, vv, NT) - per_head(dl_ref[:, sl]))).astype(BF16)
                dq = _dot(ds, kk, NN)
                dkk = _dot(ds, q2, TN)
                dvv = _dot(p.astype(BF16), do2, TN)
                dq_ref[:, sl] = jnp.where(lo, dq[:BAND], dq[BAND:]).astype(BF16)
                dk_ref[:, sl] = (ck_ref[:, sl] + dkk[:BAND]).astype(BF16)
                dv_ref[:, sl] = (cv_ref[:, sl] + dvv[:BAND]).astype(BF16)
                ck_ref[:, sl] = dkk[BAND:]
                cv_ref[:, sl] = dvv[BAND:]

        @pl.when(i == nb)
        def _():
            dk_ref[...] = ck_ref[...].astype(BF16)
            dv_ref[...] = cv_ref[...].astype(BF16)

        if ns:
            pl.when(last)(finish)

    cur, prev = _attn_specs(nb - 1)
    lag = pl.BlockSpec((None, BAND, B_WIDTH), lambda r, i: (r, jnp.maximum(i - 1, 0), 0))
    shape = jax.ShapeDtypeStruct((dil, n, B_WIDTH), BF16)
    res = pl.pallas_call(
        body, grid=(dil, nb + 1), in_specs=[cur, cur, prev, cur, prev, cur, cur, cur] + [ANY] * ns,
        out_specs=[cur, lag, lag] + [ANY] * ns,
        out_shape=[shape] * 3 + (_scattered_shapes(scatter[1]) if ns else []),
        scratch_shapes=[pltpu.VMEM((BAND, B_WIDTH), F32)] * 2 + (_scatter_sems(ns) if ns else []),
        compiler_params=_cparams("arbitrary", "arbitrary"), name=name)(q, k, k, v, v, do, lse, delta,
                                                                      *(scatter[0] if ns else []))
    return res[0], res[1], res[2], list(res[3:])


def _rope_bwd(dqs, dks, dvs, tabs, dproj, name):
    s = dproj.shape[0]
    half = ROT_DIM // 2
    scale = HEAD_DIM ** -0.5
    npat = len(DILATIONS)
    w = B_WIDTH

    def body(*refs):
        groups = [refs[g * npat:(g + 1) * npat] for g in range(3)]
        c_ref, s1_ref, s2_ref, _, o_ref, stage = refs[3 * npat:]

        def total(rs):
            acc = rs[0][...].astype(F32)
            for ref, d in zip(rs[1:], CLASS_DILS):
                acc = acc + _load_classes(ref, stage, d)
            return acc

        def unrope(g):
            c, s1, s2 = c_ref[...], s1_ref[...], s2_ref[...]
            for sl in _lane_blocks(w):
                gb = g[:, sl]
                o = gb * c + pltpu.roll(gb * s1, half, 1) + pltpu.roll(gb * s2, LANES - half, 1)
                o_ref[:, sl] = o.astype(BF16)

        which = pl.program_id(1)

        @pl.when(which == 0)
        def _():
            unrope(total(groups[0]) * scale)

        @pl.when(which == 1)
        def _():
            unrope(total(groups[1]))

        @pl.when(which == 2)
        def _():
            o_ref[...] = total(groups[2]).astype(BF16)

    tab = pl.BlockSpec((TR, LANES), lambda i, j: (i, 0))
    nat = pl.BlockSpec((TR, w), lambda i, j: (i, 0))
    lay_specs = [nat] + [_class_spec(d) for d in CLASS_DILS]
    first_col = 2 * A_WIDTH // w
    return pl.pallas_call(
        body, grid=(s // TR, 3), in_specs=lay_specs * 3 + [tab] * 3 + [ANY],
        out_specs=pl.BlockSpec((TR, w), lambda i, j: (i, first_col + j)),
        out_shape=jax.ShapeDtypeStruct(dproj.shape, dproj.dtype), scratch_shapes=[STAGE],
        input_output_aliases={3 * npat + 3: 0},
        compiler_params=_cparams("parallel", "arbitrary"), name=name)(*dqs, *dks, *dvs, *tabs, dproj)


TK = 512
HALO = 16


def _row_of(v, r):
    rows = lax.broadcasted_iota(jnp.int32, (v.shape[0], 1), 0)
    return jnp.sum(jnp.where(rows == r, v, 0.0), axis=0, keepdims=True)


def _taps_before(x, halo):
    row = lax.broadcasted_iota(jnp.int32, (x.shape[0], 1), 0)
    m1 = jnp.where(row == 0, _row_of(halo, HALO - 1), pltpu.roll(x, 1, 0))
    m2 = jnp.where(row == 0, _row_of(halo, HALO - 2), jnp.where(row == 1, _row_of(halo, HALO - 1), pltpu.roll(x, 2, 0)))
    return m2, m1, x


def _taps_after(x, halo):
    rows = x.shape[0]
    row = lax.broadcasted_iota(jnp.int32, (rows, 1), 0)
    p1 = jnp.where(row == rows - 1, _row_of(halo, 0), pltpu.roll(x, rows - 1, 0))
    p2 = jnp.where(row == rows - 2, _row_of(halo, 0), jnp.where(row == rows - 1, _row_of(halo, 1), pltpu.roll(x, rows - 2, 0)))
    return p1, p2


def _conv_value(taps, cw_ref, cb_ref, h):
    return cb_ref[h] + cw_ref[h, 0:1, :] * taps[0] + cw_ref[h, 1:2, :] * taps[1] + cw_ref[h, 2:3, :] * taps[2]


def _ffn_weight_specs(ncol):
    per_up = (2 * D_FF // N_CHIPS) // TK
    per_dn = (D_FF // N_CHIPS) // TK
    wg = pl.BlockSpec((None, None, D_MODEL, TK), lambda i, j: (j // per_up, 0, 0, j % per_up))
    wv = pl.BlockSpec((None, None, D_MODEL, TK), lambda i, j: ((j + ncol) // per_up, 0, 0, (j + ncol) % per_up))
    wd = pl.BlockSpec((None, None, TK, D_MODEL), lambda i, j: (j // per_dn, 0, j % per_dn, 0))
    cw = pl.BlockSpec((2, 3, TK), lambda i, j: (0, 0, j))
    cb = pl.BlockSpec((2, 1, TK), lambda i, j: (0, 0, j))
    return wg, wv, wd, cw, cb


def _ffn_forward(h2, w_up, w_down, cw3, cb3, name, gather=None):
    s = h2.shape[0]
    nm, ncol = s // TM, D_FF // TK
    ng = 0 if gather is None else len(gather)

    def body(*refs):
        h_ref, wg_ref, wv_ref, wd_ref, cw_ref, cb_ref = refs[:6]
        g_in = refs[6:6 + ng]
        y_ref, up_ref, cv_ref, f_ref = refs[6 + ng:10 + ng]
        g_out = refs[10 + ng:10 + 2 * ng]
        carry, acc = refs[10 + 2 * ng:12 + 2 * ng]
        i, j = pl.program_id(0), pl.program_id(1)
        if ng:
            start, relay, finish = _gather_steps(g_in, g_out, *refs[12 + 2 * ng:])
            pl.when((i == 0) & (j == 0))(start)
            pl.when((i == nm - 1) & (j == 0))(relay)

        @pl.when((i == 0) & (j == 0))
        def _():
            carry[...] = jnp.zeros_like(carry)

        h = h_ref[...]
        conv = []
        for hh, w_ref in ((0, wg_ref), (1, wv_ref)):
            up = _dot(h, w_ref[...], NN).astype(BF16)
            up_ref[hh] = up
            x = up.astype(F32)
            conv.append(_conv_value(_taps_before(x, carry[j, hh]), cw_ref, cb_ref, hh))
            cv_ref[hh] = conv[hh].astype(BF16)
            carry[j, hh] = x[TM - HALO:, :]
        y = (_gelu_tanh(conv[0])[0] * conv[1]).astype(BF16)
        y_ref[...] = y
        part = _dot(y, wd_ref[...], NN)

        @pl.when(j == 0)
        def _():
            acc[...] = part

        @pl.when(j > 0)
        def _():
            acc[...] += part

        @pl.when(j == ncol - 1)
        def _():
            f_ref[...] = acc[...]

        if ng:
            pl.when((i == nm - 1) & (j == ncol - 1))(finish)

    wg, wv, wd, cw, cb = _ffn_weight_specs(ncol)
    res = pl.pallas_call(
        body, grid=(nm, ncol),
        in_specs=[pl.BlockSpec((TM, D_MODEL), lambda i, j: (i, 0)), wg, wv, wd, cw, cb] + [ANY] * ng,
        out_specs=[pl.BlockSpec((TM, TK), lambda i, j: (i, j)), pl.BlockSpec((2, TM, TK), lambda i, j: (0, i, j)),
                   pl.BlockSpec((2, TM, TK), lambda i, j: (0, i, j)),
                   pl.BlockSpec((TM, D_MODEL), lambda i, j: (i, 0))] + [ANY] * ng,
        out_shape=[jax.ShapeDtypeStruct((s, D_FF), BF16), jax.ShapeDtypeStruct((2, s, D_FF), BF16),
                   jax.ShapeDtypeStruct((2, s, D_FF), BF16),
                   jax.ShapeDtypeStruct((s, D_MODEL), F32)] + _gathered_shapes(gather or []),
        scratch_shapes=[pltpu.VMEM((ncol, 2, HALO, TK), F32), pltpu.VMEM((TM, D_MODEL), F32)]
        + (_gather_sems(ng) if ng else []),
        compiler_params=_cparams("arbitrary", "arbitrary"), name=name)(h2, w_up, w_up, w_down, cw3, cb3,
                                                                      *(gather or []))
    return res[:4], list(res[4:])


def _ffn_backward(df, w_up, w_down, up3, cv3, cw3, name, scatter=None):
    s = df.shape[0]
    nm, ncol = s // TM, D_FF // TK
    ns = 0 if scatter is None else len(scatter[0])

    def body(*refs):
        df_ref, wg_ref, wv_ref, wd_ref, cw_ref, up_ref, cv_ref = refs[:7]
        s_in = refs[7:7 + ns]
        dup_ref, dh_ref, sums_ref = refs[7 + ns:10 + ns]
        s_out = refs[10 + ns:10 + 2 * ns]
        carry, acc = refs[10 + 2 * ns:12 + 2 * ns]
        i, j = pl.program_id(0), pl.program_id(1)
        if ns:
            start, finish = _scatter_steps(s_in, s_out, *refs[12 + 2 * ns:], scatter[1])
            pl.when((i == 0) & (j == 0))(start)

        @pl.when((i == 0) & (j == 0))
        def _():
            carry[...] = jnp.zeros_like(carry)
            sums_ref[...] = jnp.zeros_like(sums_ref)

        dy = _dot(df_ref[...], wd_ref[...], NT)
        act, grad = _gelu_tanh(cv_ref[0].astype(F32))
        dcs = (dy * cv_ref[1].astype(F32) * grad, dy * act)
        row = lax.broadcasted_iota(jnp.int32, (8, 1), 0)
        part = None
        for hh, w_ref in ((0, wg_ref), (1, wv_ref)):
            dc = dcs[hh]
            x = up_ref[hh].astype(F32)
            after1, after2 = _taps_after(dc, carry[j, hh])
            upd = jnp.zeros((8, TK), F32)
            for ridx, sm in enumerate((_colsum(after2 * x), _colsum(after1 * x), _colsum(dc * x), _colsum(dc))):
                upd = jnp.where(row == ridx, sm, upd)
            sums_ref[j, hh] += upd
            dup = (cw_ref[hh, 2:3, :] * dc + cw_ref[hh, 1:2, :] * after1 + cw_ref[hh, 0:1, :] * after2).astype(BF16)
            carry[j, hh] = dc[:HALO, :]
            dup_ref[hh] = dup
            d = _dot(dup, w_ref[...], NT)
            part = d if part is None else part + d

        @pl.when(j == 0)
        def _():
            acc[...] = part

        @pl.when(j > 0)
        def _():
            acc[...] += part

        @pl.when(j == ncol - 1)
        def _():
            dh_ref[...] = acc[...]

        if ns:
            pl.when((i == nm - 1) & (j == ncol - 1))(finish)

    wg, wv, wd, cw, _ = _ffn_weight_specs(ncol)
    rev = lambda i: nm - 1 - i
    res = pl.pallas_call(
        body, grid=(nm, ncol),
        in_specs=[pl.BlockSpec((TM, D_MODEL), lambda i, j: (rev(i), 0)), wg, wv, wd, cw,
                  pl.BlockSpec((2, TM, TK), lambda i, j: (0, rev(i), j)),
                  pl.BlockSpec((2, TM, TK), lambda i, j: (0, rev(i), j))] + [ANY] * ns,
        out_specs=[pl.BlockSpec((2, TM, TK), lambda i, j: (0, rev(i), j)),
                   pl.BlockSpec((TM, D_MODEL), lambda i, j: (rev(i), 0)),
                   pl.BlockSpec((ncol, 2, 8, TK), lambda i, j: (0, 0, 0, 0))] + [ANY] * ns,
        out_shape=[jax.ShapeDtypeStruct((2, s, D_FF), BF16), jax.ShapeDtypeStruct((s, D_MODEL), F32),
                   jax.ShapeDtypeStruct((ncol, 2, 8, TK), F32)] + (_scattered_shapes(scatter[1]) if ns else []),
        scratch_shapes=[pltpu.VMEM((ncol, 2, HALO, TK), F32), pltpu.VMEM((TM, D_MODEL), F32)]
        + (_scatter_sems(ns) if ns else []),
        compiler_params=_cparams("arbitrary", "arbitrary"), name=name)(df, w_up, w_up, w_down, cw3, up3, cv3,
                                                                      *(scatter[0] if ns else []))
    return res[:3], list(res[3:])


def _wspec(rows, cols, index_map):
    return pl.BlockSpec((None, None, rows, cols), index_map)


def _layer_forward(l, x0, h1, p, wg, tabs, gather=None, late=None):
    s = x0.shape[0]
    nm = s // TMM
    tag = f"_l{l}"
    riders = dict.fromkeys(DILATIONS)
    proj_rider = None
    if late is not None:
        halves = lambda t: (t[:, :t.shape[1] // 2], t[:, t.shape[1] // 2:])
        (down_a, down_b), (up_a, up_b) = halves(late["w_down"]), halves(late["w_up"])
        proj_rider = [late["w_out"], down_a]
        riders = dict(zip(DILATIONS, ([down_b], [up_a], [up_b])))
    proj = _matmul(
        h1, wg["w_in"], grid=(nm, N_CHIPS), a_spec=pl.BlockSpec((TMM, D_MODEL), lambda i, j: (i, 0)),
        b_spec=_wspec(D_MODEL, IN_COLS // N_CHIPS, lambda i, j: (j, 0, 0, 0)),
        o_spec=pl.BlockSpec((TMM, IN_COLS // N_CHIPS), lambda i, j: (i, j)), o_shape=(s, IN_COLS), o_dtype=BF16,
        dims=NN, nk=1, kaxis=None, acc_shape=None, name="proj" + tag, gather=proj_rider)
    if late is not None:
        proj, (w_out_all4, down_a) = proj
    ma = _mixer_a_fwd(proj, p["v_norm_g"], p["v_norm_b"], p["w_spatial"], p["bs_full"], p["out_norm_a"],
                      "mixer_a_fwd" + tag)
    q, k, v = _rope_fwd(proj, tabs, "rope_fwd" + tag)
    outs, lses, landed = zip(*[
        _attn_fwd(_as_classes(q[d]), _as_classes(k[d]), _as_classes(v[d]), f"attn_fwd_d{d}" + tag, riders[d])
        for d in DILATIONS])
    if late is not None:
        wg = dict(wg, w_out=w_out_all4, w_down=jnp.concatenate([down_a, landed[0][0]], axis=-1),
                  w_up=jnp.concatenate([landed[1][0], landed[2][0]], axis=-1))
    outs = [o.reshape(s, B_WIDTH) if d == 1 else o for o, d in zip(outs, DILATIONS)]
    lses = [t.reshape(s, B_WIDTH) if d == 1 else t for t, d in zip(lses, DILATIONS)]
    ob, lse, mixed = _attn_combine(outs, lses, p["out_norm_b"], ma, "attn_combine" + tag)
    w_out_all = pl.BlockSpec((N_CHIPS, None, D_MODEL // N_CHIPS, D_MODEL), lambda i: (0, 0, 0, 0))
    y1 = _matmul(
        mixed, wg["w_out"], grid=(nm,), a_spec=pl.BlockSpec((TMM, D_MODEL), lambda i: (i, 0)), b_spec=w_out_all,
        o_spec=pl.BlockSpec((TMM, D_MODEL), lambda i: (i, 0)), o_shape=(s, D_MODEL), o_dtype=F32,
        dims=NN, nk=1, kaxis=None, acc_shape=None, name="mix_out" + tag, b_2d=(D_MODEL, D_MODEL))
    x1, h2 = _residual_norm(x0, y1, p["post_mix_norm"], p["pre_ffn_norm"], "post_mix" + tag)
    (y, up3, cv3, f), gathered = _ffn_forward(h2, wg["w_up"], wg["w_down"], p["cw3"], p["cb3"], "ffn_fwd" + tag,
                                              gather)
    saved = dict(x0=x0, h1=h1, proj=proj, q=q, k=k, v=v, ob=ob, lse=lse, mixed=mixed, y1=y1, x1=x1, h2=h2,
                 up3=up3, cv3=cv3, y=y, f=f)
    return saved, gathered, wg


def _layer_backward(l, dx2, df, sv, p, wg, tabs, pos, scatter=None, hide=False):
    s = dx2.shape[0]
    nm = s // TMM
    tag = f"_l{l}"
    g = {}
    (dup3, dh2, conv_sums), scattered = _ffn_backward(df, wg["w_up"], wg["w_down"], sv["up3"], sv["cv3"], p["cw3"],
                                                      "ffn_bwd" + tag, scatter)
    sums = conv_sums.transpose(1, 2, 0, 3).reshape(2, 8, D_FF)
    g["conv_w"] = jnp.concatenate([sums[0, :3], sums[1, :3]], axis=1)
    g["conv_b"] = jnp.concatenate([sums[0, 3:4], sums[1, 3:4]], axis=1)
    tn = 1024
    done = {}
    gw_down = _matmul(
        sv["y"], df, grid=(D_FF // tn, 2, nm), a_spec=pl.BlockSpec((TMM, tn), lambda k, h, m: (m, k)),
        b_spec=pl.BlockSpec((TMM, D_MODEL // 2), lambda k, h, m: (m, h)),
        o_spec=pl.BlockSpec((None, tn, D_MODEL // 2), lambda k, h, m: (h, k, 0)),
        o_shape=(2, D_FF, D_MODEL // 2), o_dtype=BF16,
        dims=TN, nk=nm, kaxis=2, acc_shape=(tn, D_MODEL // 2), name="w_down_grad" + tag)
    down_sums = _chip_sums(l, dict(w_down=gw_down), pos, ("w_down",)) if hide else None
    gw_up = _matmul(
        sv["h2"], dup3, grid=(2 * D_FF // tn, nm), a_spec=pl.BlockSpec((TMM, D_MODEL), lambda n, m: (m, 0)),
        b_spec=pl.BlockSpec((None, TMM, tn), lambda n, m: (n // (D_FF // tn), m, n % (D_FF // tn))),
        o_spec=pl.BlockSpec((None, D_MODEL, tn), lambda n, m: (n // 2, 0, n % 2)),
        o_shape=(N_CHIPS, D_MODEL, 2 * D_FF // N_CHIPS), o_dtype=BF16,
        dims=TN, nk=nm, kaxis=1, acc_shape=(D_MODEL, tn), name="w_up_grad" + tag,
        scatter=(down_sums, ("w_down",)) if hide else None)
    up_sums = None
    if hide:
        gw_up, received = gw_up
        done[("w_down",)] = (down_sums, received)
        up_sums = _chip_sums(l, dict(w_up=gw_up), pos, ("w_up",))
    dx1, dy1, g["pre_ffn_norm"], g["post_mix_norm"] = _norm_bwd_mid(
        dx2, dh2, sv["x1"], p["pre_ffn_norm"], sv["y1"], p["post_mix_norm"], "norm_bwd_mid" + tag)
    w_out_all = pl.BlockSpec((N_CHIPS, None, D_MODEL // N_CHIPS, D_MODEL), lambda i: (0, 0, 0, 0))
    dmixed = _matmul(
        dy1, wg["w_out"], grid=(nm,), a_spec=pl.BlockSpec((TMM, D_MODEL), lambda i: (i, 0)), b_spec=w_out_all,
        o_spec=pl.BlockSpec((TMM, D_MODEL), lambda i: (i, 0)), o_shape=(s, D_MODEL), o_dtype=F32,
        dims=NT, nk=1, kaxis=None, acc_shape=None, name="mix_out_bwd" + tag, b_2d=(D_MODEL, D_MODEL))
    gw_out = _matmul(
        sv["mixed"], dy1, grid=(2, nm), a_spec=pl.BlockSpec((TMM, D_MODEL), lambda h, m: (m, 0)),
        b_spec=pl.BlockSpec((TMM, D_MODEL // 2), lambda h, m: (m, h)),
        o_spec=pl.BlockSpec((None, D_MODEL, D_MODEL // 2), lambda h, m: (h, 0, 0)),
        o_shape=(2, D_MODEL, D_MODEL // 2), o_dtype=BF16,
        dims=TN, nk=nm, kaxis=1, acc_shape=(D_MODEL, D_MODEL // 2), name="w_out_grad" + tag)
    dpa, g["out_norm_a"], g["v_norm_g"], g["v_norm_b"], dbs, g["w_spatial"] = _mixer_a_bwd(
        sv["proj"], dmixed, p["v_norm_g"], p["v_norm_b"], p["w_spatial"], p["bs_full"], p["out_norm_a"],
        "mixer_a_bwd" + tag)
    g["b_spatial"] = dbs[:, ::GROUP_DIM].T
    dob, delta, g["out_norm_b"] = _attn_bwd_prep(dmixed, sv["ob"], p["out_norm_b"], "attn_bwd_prep" + tag)
    last_dil = DILATIONS[-1]
    dqs, dks, dvs, received = zip(*[
        _attn_bwd(*(_as_classes(t[d]) for t in (sv["q"], sv["k"], sv["v"], dob, sv["lse"], delta)),
                  f"attn_bwd_d{d}" + tag, (up_sums, ("w_up",)) if hide and d == last_dil else None)
        for d in DILATIONS])
    if hide:
        done[("w_up",)] = (up_sums, received[-1])
    nat = lambda ts: [t.reshape(s, B_WIDTH) if d == 1 else t for t, d in zip(ts, DILATIONS)]
    dproj = _rope_bwd(nat(dqs), nat(dks), nat(dvs), tabs, dpa, "rope_bwd" + tag)
    wcol = IN_COLS // N_CHIPS
    dh1 = _matmul(
        dproj, wg["w_in"], grid=(nm, N_CHIPS), a_spec=pl.BlockSpec((TMM, wcol), lambda i, n: (i, n)),
        b_spec=_wspec(D_MODEL, wcol, lambda i, n: (n, 0, 0, 0)),
        o_spec=pl.BlockSpec((TMM, D_MODEL), lambda i, n: (i, 0)), o_shape=(s, D_MODEL), o_dtype=F32,
        dims=NT, nk=N_CHIPS, kaxis=1, acc_shape=(TMM, D_MODEL), name="proj_bwd" + tag)
    gw_in = _matmul(
        sv["h1"], dproj, grid=(N_CHIPS, nm), a_spec=pl.BlockSpec((TMM, D_MODEL), lambda n, m: (m, 0)),
        b_spec=pl.BlockSpec((TMM, wcol), lambda n, m: (m, n)),
        o_spec=pl.BlockSpec((None, D_MODEL, wcol), lambda n, m: (n, 0, 0)),
        o_shape=(N_CHIPS, D_MODEL, wcol), o_dtype=BF16,
        dims=TN, nk=nm, kaxis=1, acc_shape=(D_MODEL, wcol), name="w_in_grad" + tag)
    big = dict(w_in=gw_in, w_out=gw_out) if hide else dict(w_in=gw_in, w_up=gw_up, w_out=gw_out, w_down=gw_down)
    return dx1, dh1, big, g, scattered, done


SMALL = ("pre_mix_norm", "v_norm_g", "v_norm_b", "w_spatial", "b_spatial", "out_norm_a", "out_norm_b",
         "post_mix_norm", "pre_ffn_norm", "conv_b", "post_ffn_norm")
BIG = ("w_in", "w_out", "w_up", "w_down")
DEPTH = 2


def _layer_params(l, small, conv_w_full):
    p = {n: small[n][l].reshape(1, -1) for n in SMALL if n not in ("w_spatial", "b_spatial")}
    p["w_spatial"] = small["w_spatial"][l]
    p["bs_full"] = jnp.repeat(small["b_spatial"][l].T, GROUP_DIM, axis=1)
    p["cw3"] = conv_w_full[l].reshape(3, 2, D_FF).transpose(1, 0, 2)
    p["cb3"] = small["conv_b"][l].reshape(2, 1, D_FF)
    return p


def _mesh_pos():
    return lax.axis_index("x"), lax.axis_index("y"), lax.axis_index("c")


def _other_chips(x, y):
    return [(1 - x, y), (x, 1 - y), (1 - x, 1 - y)]


def _gathered_shapes(blocks):
    return [jax.ShapeDtypeStruct((N_CHIPS, 1) + a.shape, a.dtype) for a in blocks]


def _gather_sems(nw):
    n = 2 * nw * (N_CHIPS - 1) + nw
    return [pltpu.SemaphoreType.DMA((n,)), pltpu.SemaphoreType.DMA((n,))]


def _gather_steps(ins, outs, send, recv):
    nw, nrel = len(ins), N_CHIPS - 1
    x, y, c = _mesh_pos()
    mine, sibling, chips = 2 * x + y, (x, y, 1 - c), _other_chips(x, y)

    def copy(src, dst, slot, to):
        return pltpu.make_async_remote_copy(src_ref=src, dst_ref=dst, send_sem=send.at[slot],
                                            recv_sem=recv.at[slot], device_id=to, device_id_type=MESH)

    def half_rows(t, core):
        rows = ins[t].shape[0] // 2
        return pl.ds(pl.multiple_of(core * rows, rows), rows)

    def landing(t, chip, core):
        return outs[t].at[chip, 0, half_rows(t, core), :]

    slots = [(t, r, chip) for t in range(nw) for r, chip in enumerate(chips)]
    own = [copy(ins[t], outs[t].at[mine, 0], 2 * nw * nrel + t, sibling) for t in range(nw)]
    first = [copy(ins[t].at[half_rows(t, c), :], landing(t, mine, c), t * nrel + r, (px, py, c))
             for t, r, (px, py) in slots]
    relays = [copy(landing(t, 2 * px + py, c), landing(t, 2 * px + py, c), nw * nrel + t * nrel + r, sibling)
              for t, r, (px, py) in slots]

    def start():
        for cp in own + first:
            cp.start()

    def relay():
        for (t, r, (px, py)), cp in zip(slots, relays):
            copy(landing(t, 2 * px + py, c), landing(t, 2 * px + py, c), t * nrel + r, (px, py, c)).wait_recv()
            cp.start()

    def finish():
        for t, r, (px, py) in slots:
            passed = landing(t, 2 * px + py, 1 - c)
            copy(passed, passed, nw * nrel + t * nrel + r, sibling).wait_recv()
        for cp in first + relays:
            cp.wait_send()
        for cp in own:
            cp.wait()

    return start, relay, finish


def _gather_weights(blocks, name):
    nw = len(blocks)

    def body(*refs):
        start, relay, finish = _gather_steps(refs[:nw], refs[nw:2 * nw], *refs[2 * nw:])
        start()
        relay()
        finish()

    return pl.pallas_call(
        body, in_specs=[ANY] * nw, out_specs=[ANY] * nw, out_shape=_gathered_shapes(blocks),
        scratch_shapes=_gather_sems(nw), name=name)(*blocks)


HALF = 512

GRAD_GEOM = {"w_in": ("rows", D_MODEL, IN_COLS // N_CHIPS), "w_up": ("rows", D_MODEL, 2 * D_FF // N_CHIPS),
             "w_out": ("cols", D_MODEL, D_MODEL // N_CHIPS), "w_down": ("cols", D_FF, D_FF // N_CHIPS)}


def _exchange_shape(n):
    kind, a, b = GRAD_GEOM[n]
    return (N_CHIPS, HALF, b) if kind == "rows" else (a, HALF)


def _piece_shape(n):
    kind, _, b = GRAD_GEOM[n]
    return (HALF, b) if kind == "rows" else (b, HALF)


def _half_of(ref, n, core):
    if GRAD_GEOM[n][0] == "rows":
        return ref.at[:, pl.ds(pl.multiple_of(core * HALF, HALF), HALF), :]
    return ref.at[core]


def _piece_of(ref, n, chip):
    kind, _, b = GRAD_GEOM[n]
    return ref.at[chip] if kind == "rows" else ref.at[pl.ds(pl.multiple_of(chip * b, b), b), :]


def _pair_exchange(g, names, name):
    n = len(names)

    def body(*refs):
        send, recv = refs[2 * n:]
        x, y, c = _mesh_pos()
        o = 1 - c
        cps = [pltpu.make_async_remote_copy(src_ref=_half_of(refs[t], nm, o), dst_ref=refs[n + t], send_sem=send.at[t],
                                            recv_sem=recv.at[t], device_id=(x, y, o), device_id_type=MESH)
               for t, nm in enumerate(names)]
        for cp in cps:
            cp.start()
        for cp in cps:
            cp.wait()

    return pl.pallas_call(
        body, in_specs=[ANY] * n, out_specs=[ANY] * n,
        out_shape=[jax.ShapeDtypeStruct(_exchange_shape(nm), BF16) for nm in names],
        scratch_shapes=[pltpu.SemaphoreType.DMA((n,)), pltpu.SemaphoreType.DMA((n,))],
        name=name)(*[g[nm] for nm in names])


def _pair_sum(g, recv, pos, names, name_prefix):
    def add(a, b, grid, a_spec, b_spec, name):
        def body(pos_ref, a_ref, b_ref, o_ref):
            o_ref[...] = (a_ref[...].astype(F32) + b_ref[...].astype(F32)).astype(BF16)

        return pl.pallas_call(
            body, grid_spec=pltpu.PrefetchScalarGridSpec(
                num_scalar_prefetch=1, grid=grid, in_specs=[a_spec, b_spec], out_specs=b_spec),
            out_shape=jax.ShapeDtypeStruct(b.shape, BF16), compiler_params=_cparams("parallel"), name=name)(pos, a, b)

    out = []
    for nm, r in zip(names, recv):
        kind, rows, width = GRAD_GEOM[nm]
        if kind == "rows":
            out.append(add(g[nm], r, (N_CHIPS,), pl.BlockSpec((None, HALF, width), lambda j, pos: (j, pos[2], 0)),
                           pl.BlockSpec((None, HALF, width), lambda j, pos: (j, 0, 0)), f"{name_prefix}_{nm}"))
        else:
            out.append(add(g[nm], r, (rows // D_MODEL,), pl.BlockSpec((None, D_MODEL, HALF), lambda j, pos: (pos[2], j, 0)),
                           pl.BlockSpec((D_MODEL, HALF), lambda j, pos: (j, 0)), f"{name_prefix}_{nm}"))
    return out


def _scattered_shapes(names):
    return [jax.ShapeDtypeStruct((N_CHIPS - 1,) + _piece_shape(nm), BF16) for nm in names]


def _scatter_sems(n):
    return [pltpu.SemaphoreType.DMA((n * (N_CHIPS - 1),)), pltpu.SemaphoreType.DMA((n * (N_CHIPS - 1),))]


def _scatter_steps(sums, outs, send, recv, names):
    nrel = N_CHIPS - 1
    x, y, c = _mesh_pos()
    cps = []
    for r, (px, py) in enumerate(_other_chips(x, y)):
        for t, nm in enumerate(names):
            cps.append(pltpu.make_async_remote_copy(
                src_ref=_piece_of(sums[t], nm, 2 * px + py), dst_ref=outs[t].at[r], send_sem=send.at[t * nrel + r],
                recv_sem=recv.at[t * nrel + r], device_id=(px, py, c), device_id_type=MESH))

    def start():
        for cp in cps:
            cp.start()

    def finish():
        for cp in cps:
            cp.wait()

    return start, finish


def _chip_scatter(sums, names, name):
    n = len(names)

    def body(*refs):
        start, finish = _scatter_steps(refs[:n], refs[n:2 * n], *refs[2 * n:], names)
        start()
        finish()

    return pl.pallas_call(
        body, in_specs=[ANY] * n, out_specs=[ANY] * n, out_shape=_scattered_shapes(names),
        scratch_shapes=_scatter_sems(n), name=name)(*sums)


def _chip_sum(sums, recv, pos, names, name_prefix):
    def add(a, b, a_spec, shape, name):
        def body(pos_ref, a_ref, b_ref, o_ref):
            tot = a_ref[...].astype(F32)
            for r in range(N_CHIPS - 1):
                tot = tot + b_ref[r].astype(F32)
            o_ref[...] = tot

        return pl.pallas_call(
            body, grid_spec=pltpu.PrefetchScalarGridSpec(
                num_scalar_prefetch=1, grid=(1,), in_specs=[a_spec, pl.BlockSpec(b.shape, lambda i, pos: (0, 0, 0))],
                out_specs=pl.BlockSpec((None,) + shape, lambda i, pos: (pos[2], 0, 0))),
            out_shape=jax.ShapeDtypeStruct((2,) + shape, F32), compiler_params=_cparams("arbitrary"),
            name=name)(pos, a, b)

    chip = lambda pos: 2 * pos[0] + pos[1]
    out = []
    for nm, a, b in zip(names, sums, recv):
        shape = _piece_shape(nm)
        if GRAD_GEOM[nm][0] == "rows":
            spec = pl.BlockSpec((None,) + shape, lambda i, pos: (chip(pos), 0, 0))
        else:
            spec = pl.BlockSpec(shape, lambda i, pos: (chip(pos), 0))
        out.append(add(a, b, spec, shape, f"{name_prefix}_{nm}"))
    return out


def _pair_share(totals, name):
    n = len(totals)

    def body(*refs):
        ins, outs = refs[:n], refs[n:2 * n]
        send, recv = refs[2 * n:]
        x, y, c = _mesh_pos()
        o = 1 - c
        cps = [pltpu.make_async_remote_copy(src_ref=ins[t].at[c], dst_ref=outs[t].at[c], send_sem=send.at[t],
                                            recv_sem=recv.at[t], device_id=(x, y, o), device_id_type=MESH)
               for t in range(n)]
        for cp in cps:
            cp.start()
        for t in range(n):
            pltpu.make_async_remote_copy(src_ref=ins[t].at[o], dst_ref=outs[t].at[o], send_sem=send.at[t],
                                         recv_sem=recv.at[t], device_id=(x, y, o), device_id_type=MESH).wait_recv()
        for cp in cps:
            cp.wait_send()

    return pl.pallas_call(
        body, in_specs=[ANY] * n, out_specs=[ANY] * n,
        out_shape=[jax.ShapeDtypeStruct(t.shape, t.dtype) for t in totals],
        scratch_shapes=[pltpu.SemaphoreType.DMA((n,)), pltpu.SemaphoreType.DMA((n,))],
        input_output_aliases={t: t for t in range(n)}, name=name)(*totals)


def _chip_sums(l, g, pos, names):
    tag = f"l{l}_" + "_".join(names)
    recv = _pair_exchange(g, names, "pair_exchange_" + tag)
    return _pair_sum(g, recv, pos, names, "pair_sum_" + tag)


def _gradient_shards(l, sums, scattered, pos, names):
    tag = f"l{l}_" + "_".join(names)
    halves = _pair_share(_chip_sum(sums, scattered, pos, names, "chip_sum_" + tag), "pair_share_" + tag)
    out = {}
    for nm, t in zip(names, halves):
        rows, cols = _piece_shape(nm)
        out[nm] = t.reshape(2 * rows, cols) if GRAD_GEOM[nm][0] == "rows" else t.transpose(1, 0, 2).reshape(rows, 2 * cols)
    return out


N_DEV = 8


def _allreduce_small(packed, name):
    rows = packed.shape[0]

    def body(x_ref, out_ref, gath, send_sems, recv_sems, local_sem):
        x, y, c = _mesh_pos()
        me, sibling = (x, y, c), (x, y, 1 - c)
        chips = _other_chips(x, y)

        def blk(px, py, pc):
            return gath.at[pl.ds(pl.multiple_of((4 * px + 2 * py + pc) * rows, 8), rows), :]

        def copy(k, block, to, src=None):
            return pltpu.make_async_remote_copy(
                src_ref=blk(*block) if src is None else src, dst_ref=blk(*block), send_sem=send_sems.at[k],
                recv_sem=recv_sems.at[k], device_id=to, device_id_type=MESH)

        mine = pltpu.make_async_copy(x_ref, blk(*me), local_sem)
        mine.start()
        first = [copy(0, me, sibling, src=x_ref)]
        first += [copy(1 + j, me, (*chip, c), src=x_ref) for j, chip in enumerate(chips)]
        for cp in first:
            cp.start()
        passed = [copy(4 + j, (*chip, c), sibling) for j, chip in enumerate(chips)]
        for j, chip in enumerate(chips):
            copy(1 + j, (*chip, c), me).wait_recv()
            passed[j].start()
        copy(0, sibling, me).wait_recv()
        for j, chip in enumerate(chips):
            copy(4 + j, (*chip, 1 - c), me).wait_recv()
        for cp in first + passed:
            cp.wait_send()
        mine.wait()
        tot = gath[0:rows, :]
        for d in range(1, N_DEV):
            tot = tot + gath[d * rows:(d + 1) * rows, :]
        out_ref[...] = tot

    vmem = pl.BlockSpec(memory_space=pltpu.VMEM)
    return pl.pallas_call(
        body, in_specs=[vmem], out_specs=vmem, out_shape=jax.ShapeDtypeStruct((rows, LANES), F32),
        scratch_shapes=[pltpu.VMEM((N_DEV * rows, LANES), F32), pltpu.SemaphoreType.DMA((7,)),
                        pltpu.SemaphoreType.DMA((7,)), pltpu.SemaphoreType.DMA],
        compiler_params=pltpu.CompilerParams(vmem_limit_bytes=VMEM_LIMIT_BYTES),
        name=name)(packed)


def _adamw(w, g, m, v, name):
    rows, cols = w.shape
    tr = 256 if rows % 256 == 0 else rows

    def body(w_ref, g_ref, m_ref, v_ref, d_ref, mo_ref, vo_ref):
        gv = g_ref[...]
        mn = ADAM_B1 * m_ref[...] + (1.0 - ADAM_B1) * gv
        vn = ADAM_B2 * v_ref[...] + (1.0 - ADAM_B2) * (gv * gv)
        m_hat = mn / (1.0 - ADAM_B1 ** ADAM_STEP)
        v_hat = vn / (1.0 - ADAM_B2 ** ADAM_STEP)
        d_ref[...] = -ADAM_LR * (m_hat / (jnp.sqrt(v_hat) + ADAM_EPS) + ADAM_WD * w_ref[...])
        mo_ref[...] = mn
        vo_ref[...] = vn

    spec = pl.BlockSpec((tr, cols), lambda i: (i, 0))
    return pl.pallas_call(
        body, grid=(rows // tr,), in_specs=[spec] * 4, out_specs=[spec] * 3,
        out_shape=[jax.ShapeDtypeStruct((rows, cols), F32)] * 3, compiler_params=_cparams("parallel"),
        name=name)(w, g, m, v)


def _adamw_nd(w, g, m, v, name):
    cols = w.shape[-1] if w.shape[-1] % LANES == 0 else LANES
    outs = _adamw(*(t.reshape(-1, cols) for t in (w, g, m, v)), name)
    return tuple(t.reshape(w.shape) for t in outs)


def _pack(arrays):
    return jnp.concatenate([a.reshape(-1, LANES) for a in arrays], axis=0)


def _unpack(packed, shapes):
    out, row = [], 0
    for sh in shapes:
        n = math.prod(sh) // LANES
        out.append(packed[row:row + n].reshape(sh))
        row += n
    return out


WEIGHTS = ("pre_mix_norm", "w_in", "v_norm_g", "v_norm_b", "w_spatial", "b_spatial", "out_norm_a", "out_norm_b",
           "w_out", "post_mix_norm", "pre_ffn_norm", "w_up", "conv_w", "conv_b", "w_down", "post_ffn_norm")


def kernel(x, pre_mix_norm, w_in, v_norm_g, v_norm_b, w_spatial, b_spatial, out_norm_a, out_norm_b, w_out, post_mix_norm, pre_ffn_norm, w_up, conv_w, conv_b, w_down, post_ffn_norm, loss_target, m_pre_mix_norm, m_w_in, m_v_norm_g, m_v_norm_b, m_w_spatial, m_b_spatial, m_out_norm_a, m_out_norm_b, m_w_out, m_post_mix_norm, m_pre_ffn_norm, m_w_up, m_conv_w, m_conv_b, m_w_down, m_post_ffn_norm, v_pre_mix_norm, v_w_in, v_v_norm_g, v_v_norm_b, v_w_spatial, v_b_spatial, v_out_norm_a, v_out_norm_b, v_w_out, v_post_mix_norm, v_pre_ffn_norm, v_w_up, v_conv_w, v_conv_b, v_w_down, v_post_ffn_norm):
    w = dict(pre_mix_norm=pre_mix_norm, w_in=w_in, v_norm_g=v_norm_g, v_norm_b=v_norm_b, w_spatial=w_spatial,
             b_spatial=b_spatial, out_norm_a=out_norm_a, out_norm_b=out_norm_b, w_out=w_out,
             post_mix_norm=post_mix_norm, pre_ffn_norm=pre_ffn_norm, w_up=w_up, conv_w=conv_w, conv_b=conv_b,
             w_down=w_down, post_ffn_norm=post_ffn_norm)
    m = dict(pre_mix_norm=m_pre_mix_norm, w_in=m_w_in, v_norm_g=m_v_norm_g, v_norm_b=m_v_norm_b,
             w_spatial=m_w_spatial, b_spatial=m_b_spatial, out_norm_a=m_out_norm_a, out_norm_b=m_out_norm_b,
             w_out=m_w_out, post_mix_norm=m_post_mix_norm, pre_ffn_norm=m_pre_ffn_norm, w_up=m_w_up,
             conv_w=m_conv_w, conv_b=m_conv_b, w_down=m_w_down, post_ffn_norm=m_post_ffn_norm)
    v = dict(pre_mix_norm=v_pre_mix_norm, w_in=v_w_in, v_norm_g=v_v_norm_g, v_norm_b=v_v_norm_b,
             w_spatial=v_w_spatial, b_spatial=v_b_spatial, out_norm_a=v_out_norm_a, out_norm_b=v_out_norm_b,
             w_out=v_w_out, post_mix_norm=v_post_mix_norm, pre_ffn_norm=v_pre_ffn_norm, w_up=v_w_up,
             conv_w=v_conv_w, conv_b=v_conv_b, w_down=v_w_down, post_ffn_norm=v_post_ffn_norm)
    pos = jnp.stack([lax.axis_index("x"), lax.axis_index("y"), lax.axis_index("c")]).astype(jnp.int32)
    chip = 2 * lax.axis_index("x") + lax.axis_index("y")

    cw_cols = conv_w.shape[-1]
    cw_slab = lax.dynamic_update_slice(jnp.zeros((DEPTH, 3, 2 * D_FF), F32), conv_w, (0, 0, chip * cw_cols))
    conv_w_full = _allreduce_small(cw_slab.reshape(-1, LANES), "gather_conv_w").reshape(DEPTH, 3, 2 * D_FF)
    conv_w_full = conv_w_full * 0.5
    blocks = [{n: w[n][l].astype(BF16) for n in BIG} for l in range(DEPTH)]
    wg = dict(w_in=_gather_weights([blocks[0]["w_in"]], "gather_w_in_l0")[0])

    small = {n: w[n] for n in SMALL}
    xs, target = x[0], loss_target[0]
    tabs = _rope_tables(xs.shape[0])
    params = [_layer_params(l, small, conv_w_full) for l in range(DEPTH)]
    saved, wgs = [], []
    xin = xs
    h = _rms_cast(xin, params[0]["pre_mix_norm"], "pre_mix_l0")
    for l in range(DEPTH):
        sv, gathered, wg = _layer_forward(l, xin, h, params[l], wg, tabs,
                                          [blocks[l + 1][n] for n in BIG] if l + 1 < DEPTH else None,
                                          blocks[0] if l == 0 else None)
        saved.append(sv)
        wgs.append(wg)
        if l + 1 < DEPTH:
            wg = dict(zip(BIG, gathered))
            xin, h = _residual_norm(sv["x1"], sv["f"], params[l]["post_ffn_norm"], params[l + 1]["pre_mix_norm"],
                                    f"post_ffn_l{l}")
    loss_part, dx, df, g_post = _loss_norm_bwd(saved[-1]["x1"], saved[-1]["f"], params[-1]["post_ffn_norm"], target,
                                               "loss")
    smalls, shards = [None] * DEPTH, [{} for _ in range(DEPTH)]
    pending = None
    for l in reversed(range(DEPTH)):
        dx1, dh1, big, smalls[l], scattered, done = _layer_backward(l, dx, df, saved[l], params[l], wgs[l], tabs, pos,
                                                                    pending[1:] if pending else None, hide=l == 0)
        smalls[l]["post_ffn_norm"] = g_post
        if l > 0:
            dx, smalls[l]["pre_mix_norm"], df, g_post = _norm_bwd_in_out(
                dx1, dh1, saved[l]["x0"], params[l]["pre_mix_norm"], saved[l - 1]["f"], params[l - 1]["post_ffn_norm"],
                f"norm_bwd_in_out_l{l}")
        else:
            dx, smalls[l]["pre_mix_norm"] = _norm_bwd_in(dx1, dh1, saved[l]["x0"], params[l]["pre_mix_norm"],
                                                         "norm_bwd_in_l0")
        if pending:
            shards[pending[0]].update(_gradient_shards(pending[0], pending[1], scattered, pos, pending[2]))
        for names, (sums, received) in done.items():
            shards[l].update(_gradient_shards(l, sums, received, pos, names))
        names = tuple(big)
        pending = (l, _chip_sums(l, big, pos, names), names)
    shards[pending[0]].update(_gradient_shards(
        pending[0], pending[1], _chip_scatter(pending[1], pending[2], f"chip_scatter_l{pending[0]}"), pos, pending[2]))

    small_shapes = [w[n].shape for n in SMALL]
    stacked = [jnp.stack([smalls[l][n].reshape(w[n].shape[1:]) for l in range(DEPTH)]) for n in SMALL]
    cw_grad = jnp.stack([smalls[l]["conv_w"] for l in range(DEPTH)])
    packed = _pack(stacked + [cw_grad, loss_part])
    total = _allreduce_small(packed, "allreduce_small")
    parts = _unpack(total, small_shapes + [cw_grad.shape, (8, LANES)])
    g_small = dict(zip(SMALL, parts[:len(SMALL)]))
    loss = parts[-1][0, 0]
    g_conv_w = lax.dynamic_slice(parts[-2], (0, 0, chip * cw_cols), conv_w.shape)

    grads = {n: jnp.stack([shards[l][n] for l in range(DEPTH)]) for n in BIG}
    grads.update(g_small)
    grads["conv_w"] = g_conv_w

    dp, mp, vp = _adamw(_pack([w[n] for n in SMALL]), _pack([g_small[n] for n in SMALL]),
                        _pack([m[n] for n in SMALL]), _pack([v[n] for n in SMALL]), "adamw_small")
    delta = dict(zip(SMALL, _unpack(dp, small_shapes)))
    new_m = dict(zip(SMALL, _unpack(mp, small_shapes)))
    new_v = dict(zip(SMALL, _unpack(vp, small_shapes)))
    for n in BIG + ("conv_w",):
        delta[n], new_m[n], new_v[n] = _adamw_nd(w[n], grads[n], m[n], v[n], "adamw_" + n)

    return (loss, dx[None], *[grads[n] for n in WEIGHTS], *[delta[n] for n in WEIGHTS],
            *[new_m[n] for n in WEIGHTS], *[new_v[n] for n in WEIGHTS])
```

```python
import functools
import math

import jax
import jax.numpy as jnp
import numpy as np
from jax import lax
from jax.experimental import pallas as pl
from jax.experimental.pallas import tpu as pltpu

F32 = jnp.float32
BF16 = jnp.bfloat16
MESH = pl.DeviceIdType.MESH

D_MODEL = 1024
A_WIDTH = 512
A_GROUPS = 4
GROUP_DIM = 128
CHUNK = 128
B_WIDTH = 512
HEAD_DIM = 64
ROT_DIM = 16
ROPE_THETA = 500000.0
DILATIONS = (1, 4, 16)
BAND = 128
IN_COLS = 2560
D_FF = 4096
EPS = 1e-6
NEG_INF = -1e30
N_CHIPS = 4
LANES = 128

ADAM_LR = 0.001
ADAM_B1 = 0.9
ADAM_B2 = 0.999
ADAM_EPS = 1e-08
ADAM_WD = 0.01
ADAM_STEP = 10

VMEM_LIMIT_BYTES = 56 * 1024 * 1024
RSQRT2 = 0.7071067811865476
INV_SQRT_2PI = 0.3989422804014327
GELU_C = 0.7978845608028654
GELU_A = 0.044715

ANY = pl.BlockSpec(memory_space=pl.ANY)
NN = ((1,), (0,))
NT = ((1,), (1,))
TN = ((0,), (0,))


def _cparams(*sem):
    return pltpu.CompilerParams(dimension_semantics=sem, vmem_limit_bytes=VMEM_LIMIT_BYTES)


def _dot(a, b, dims):
    return lax.dot_general(a, b, (dims, ((), ())), preferred_element_type=F32)


def _rsq_mean(a):
    return lax.rsqrt(jnp.mean(a * a, axis=-1, keepdims=True) + EPS)


def _rms_bwd(a, r, g, dz):
    t = dz * g
    da = r * t - a * (r * r * r) * jnp.mean(t * a, axis=-1, keepdims=True)
    return da, dz * a * r


def _colsum(a):
    return jnp.sum(a, axis=0, keepdims=True)


def _gelu_tanh(x):
    u = x * x
    t = jnp.tanh(x * (GELU_C + (GELU_C * GELU_A) * u))
    hx = 0.5 * x
    act = hx + hx * t
    grad = 0.5 + 0.5 * t + (hx - hx * t * t) * (GELU_C + (3.0 * GELU_C * GELU_A) * u)
    return act, grad


def _grid_edges(grid):
    ids = [pl.program_id(ax) for ax in range(len(grid))]
    first = functools.reduce(jnp.logical_and, [i == 0 for i in ids])
    last = functools.reduce(jnp.logical_and, [i == n - 1 for i, n in zip(ids, grid)])
    return first, last


def _matmul(a, b, *, grid, a_spec, b_spec, o_spec, o_shape, o_dtype, dims, nk, kaxis, acc_shape, name, b_2d=None,
            halves=False, scatter=None, gather=None):
    assert scatter is None or gather is None
    ns = len(scatter[0]) if scatter else len(gather) if gather else 0

    def body(*refs):
        a_ref, b_ref = refs[:2]
        o_ref = refs[2 + ns]
        scratch = refs[3 + 2 * ns:]
        if ns:
            first, last = _grid_edges(grid)
            if scatter:
                start, finish = _scatter_steps(refs[2:2 + ns], refs[3 + ns:3 + 2 * ns], scratch[-2], scratch[-1],
                                               scatter[1])
            else:
                start, relay, last_wait = _gather_steps(refs[2:2 + ns], refs[3 + ns:3 + 2 * ns], scratch[-2],
                                                        scratch[-1])

                def finish():
                    relay()
                    last_wait()
            pl.when(first)(start)
        def store(val):
            if halves:
                half = val.shape[1] // 2
                o_ref[0] = val[:, :half].astype(o_dtype)
                o_ref[1] = val[:, half:].astype(o_dtype)
            else:
                o_ref[...] = val.astype(o_dtype)

        bv = b_ref[...] if b_2d is None else b_ref[...].reshape(b_2d)
        part = _dot(a_ref[...], bv, dims)
        if nk == 1:
            store(part)
        else:
            acc = scratch[0]
            k = pl.program_id(kaxis)

            @pl.when(k == 0)
            def _():
                acc[...] = part

            @pl.when(k > 0)
            def _():
                acc[...] += part

            @pl.when(k == nk - 1)
            def _():
                store(acc[...])

        if ns:
            pl.when(last)(finish)

    sem = tuple("arbitrary" if (ns or (nk > 1 and ax == kaxis)) else "parallel" for ax in range(len(grid)))
    riding = list(scatter[0]) if scatter else list(gather or [])
    rider_shapes = _scattered_shapes(scatter[1]) if scatter else _gathered_shapes(riding)
    rider_sems = _scatter_sems(ns) if scatter else _gather_sems(ns) if gather else []
    res = pl.pallas_call(
        body, grid=grid, in_specs=[a_spec, b_spec] + [ANY] * ns, out_specs=[o_spec] + [ANY] * ns,
        out_shape=[jax.ShapeDtypeStruct(o_shape, o_dtype)] + rider_shapes,
        scratch_shapes=([pltpu.VMEM(acc_shape, F32)] if nk > 1 else []) + rider_sems,
        compiler_params=_cparams(*sem), name=name)(a, b, *riding)
    return (res[0], list(res[1:])) if ns else res[0]


TM = 512
TMM = 1024


TR = 256


def _row_spec(width, col=0):
    return pl.BlockSpec((TR, width), lambda i, col=col: (i, col))


def _vec_spec(width):
    return pl.BlockSpec((1, width), lambda i: (0, 0))


def _rms_cast(x, g, name):
    s, d = x.shape

    def body(x_ref, g_ref, h_ref):
        a = x_ref[...]
        h_ref[...] = (a * _rsq_mean(a) * g_ref[...]).astype(BF16)

    return pl.pallas_call(
        body, grid=(s // TR,), in_specs=[_row_spec(d), _vec_spec(d)], out_specs=_row_spec(d),
        out_shape=jax.ShapeDtypeStruct((s, d), BF16), compiler_params=_cparams("parallel"), name=name)(x, g)


def _residual_norm(x0, y, g_post, g_next, name):
    s, d = x0.shape

    def body(x_ref, y_ref, gp_ref, gn_ref, x1_ref, h_ref):
        yv = y_ref[...]
        x1 = x_ref[...] + yv * _rsq_mean(yv) * gp_ref[...]
        x1_ref[...] = x1
        h_ref[...] = (x1 * _rsq_mean(x1) * gn_ref[...]).astype(BF16)

    return pl.pallas_call(
        body, grid=(s // TR,), in_specs=[_row_spec(d), _row_spec(d), _vec_spec(d), _vec_spec(d)],
        out_specs=[_row_spec(d), _row_spec(d)],
        out_shape=[jax.ShapeDtypeStruct((s, d), F32), jax.ShapeDtypeStruct((s, d), BF16)],
        compiler_params=_cparams("parallel"), name=name)(x0, y, g_post, g_next)


def _acc_init(refs):
    @pl.when(pl.program_id(0) == 0)
    def _():
        for r in refs:
            r[...] = jnp.zeros_like(r)


def _loss_norm_bwd(x1, f, g_post, target, name):
    s, d = x1.shape

    def body(x_ref, f_ref, gp_ref, t_ref, loss_ref, dx_ref, df_ref, dg_ref):
        _acc_init([loss_ref, dg_ref])
        fv = f_ref[...]
        r = _rsq_mean(fv)
        err = x_ref[...] + fv * r * gp_ref[...] - t_ref[...]
        dx = err * (1.0 / d)
        dx_ref[...] = dx
        part = 0.5 * jnp.sum(jnp.mean(err * err, axis=-1, keepdims=True), axis=0, keepdims=True)
        loss_ref[...] += jnp.broadcast_to(part, loss_ref.shape)
        da, dgt = _rms_bwd(fv, r, gp_ref[...], dx)
        df_ref[...] = da.astype(BF16)
        dg_ref[...] += _colsum(dgt)

    return pl.pallas_call(
        body, grid=(s // TR,), in_specs=[_row_spec(d), _row_spec(d), _vec_spec(d), _row_spec(d)],
        out_specs=[pl.BlockSpec((8, LANES), lambda i: (0, 0)), _row_spec(d), _row_spec(d), _vec_spec(d)],
        out_shape=[jax.ShapeDtypeStruct((8, LANES), F32), jax.ShapeDtypeStruct((s, d), F32),
                   jax.ShapeDtypeStruct((s, d), BF16), jax.ShapeDtypeStruct((1, d), F32)],
        compiler_params=_cparams("arbitrary"), name=name)(x1, f, g_post, target)


def _norm_bwd_mid(dx2, dh2, x1, g_pf, y1, g_pm, name):
    s, d = dx2.shape

    def body(dx2_ref, dh_ref, x1_ref, gpf_ref, y1_ref, gpm_ref, dx1_ref, dy1_ref, dgpf_ref, dgpm_ref):
        _acc_init([dgpf_ref, dgpm_ref])
        x1 = x1_ref[...]
        da, dgt = _rms_bwd(x1, _rsq_mean(x1), gpf_ref[...], dh_ref[...])
        dx1 = dx2_ref[...] + da
        dx1_ref[...] = dx1
        dgpf_ref[...] += _colsum(dgt)
        y1 = y1_ref[...]
        dy, dgt2 = _rms_bwd(y1, _rsq_mean(y1), gpm_ref[...], dx1)
        dy1_ref[...] = dy.astype(BF16)
        dgpm_ref[...] += _colsum(dgt2)

    return pl.pallas_call(
        body, grid=(s // TR,),
        in_specs=[_row_spec(d), _row_spec(d), _row_spec(d), _vec_spec(d), _row_spec(d), _vec_spec(d)],
        out_specs=[_row_spec(d), _row_spec(d), _vec_spec(d), _vec_spec(d)],
        out_shape=[jax.ShapeDtypeStruct((s, d), F32), jax.ShapeDtypeStruct((s, d), BF16),
                   jax.ShapeDtypeStruct((1, d), F32), jax.ShapeDtypeStruct((1, d), F32)],
        compiler_params=_cparams("arbitrary"), name=name)(dx2, dh2, x1, g_pf, y1, g_pm)


def _norm_bwd_in_out(dx1, dh1, x0, g1, f_below, g_post_below, name):
    s, d = dx1.shape

    def body(dx1_ref, dh_ref, x0_ref, g_ref, f_ref, gp_ref, dx0_ref, dg_ref, df_ref, dgp_ref):
        _acc_init([dg_ref, dgp_ref])
        x0 = x0_ref[...]
        da, dgt = _rms_bwd(x0, _rsq_mean(x0), g_ref[...], dh_ref[...])
        dx0 = dx1_ref[...] + da
        dx0_ref[...] = dx0
        dg_ref[...] += _colsum(dgt)
        fv = f_ref[...]
        db, dgt2 = _rms_bwd(fv, _rsq_mean(fv), gp_ref[...], dx0)
        df_ref[...] = db.astype(BF16)
        dgp_ref[...] += _colsum(dgt2)

    return pl.pallas_call(
        body, grid=(s // TR,),
        in_specs=[_row_spec(d), _row_spec(d), _row_spec(d), _vec_spec(d), _row_spec(d), _vec_spec(d)],
        out_specs=[_row_spec(d), _vec_spec(d), _row_spec(d), _vec_spec(d)],
        out_shape=[jax.ShapeDtypeStruct((s, d), F32), jax.ShapeDtypeStruct((1, d), F32),
                   jax.ShapeDtypeStruct((s, d), BF16), jax.ShapeDtypeStruct((1, d), F32)],
        compiler_params=_cparams("arbitrary"), name=name)(dx1, dh1, x0, g1, f_below, g_post_below)


def _norm_bwd_in(dx1, dh1, x0, g1, name):
    s, d = dx1.shape

    def body(dx1_ref, dh_ref, x0_ref, g_ref, dx0_ref, dg_ref):
        _acc_init([dg_ref])
        x0 = x0_ref[...]
        da, dgt = _rms_bwd(x0, _rsq_mean(x0), g_ref[...], dh_ref[...])
        dx0_ref[...] = dx1_ref[...] + da
        dg_ref[...] += _colsum(dgt)

    return pl.pallas_call(
        body, grid=(s // TR,), in_specs=[_row_spec(d), _row_spec(d), _row_spec(d), _vec_spec(d)],
        out_specs=[_row_spec(d), _vec_spec(d)],
        out_shape=[jax.ShapeDtypeStruct((s, d), F32), jax.ShapeDtypeStruct((1, d), F32)],
        compiler_params=_cparams("arbitrary"), name=name)(dx1, dh1, x0, g1)


def _tril_mask():
    row = lax.broadcasted_iota(jnp.int32, (CHUNK, CHUNK), 0)
    col = lax.broadcasted_iota(jnp.int32, (CHUNK, CHUNK), 1)
    return row >= col


def _gating_forward(pa, gv, bv, wt, bsf):
    er = lax.erf(pa * RSQRT2)
    za = 0.5 * pa * (1.0 + er)
    u = za[:, :A_WIDTH]
    va = za[:, A_WIDTH:]
    xc = va - jnp.mean(va, axis=-1, keepdims=True)
    rs = lax.rsqrt(jnp.mean(xc * xc, axis=-1, keepdims=True) + EPS)
    vn = xc * rs
    vlb = (vn * gv + bv).astype(BF16)
    sg = jnp.concatenate(
        [_dot(wt[g], vlb[:, g * GROUP_DIM:(g + 1) * GROUP_DIM], NN) for g in range(A_GROUPS)], axis=1) + bsf
    return er, u, rs, vn, vlb, sg


def _masked_ws(ws_ref):
    mask = _tril_mask()
    return [jnp.where(mask, ws_ref[g], 0.0).astype(BF16) for g in range(A_GROUPS)]


def _mixer_a_fwd(proj, gv, bv, ws, bsf, ga, name):
    s = proj.shape[0]

    def body(p_ref, gv_ref, bv_ref, ws_ref, bs_ref, ga_ref, o_ref):
        wt = _masked_ws(ws_ref)
        for ch in range(TR // CHUNK):
            rows = slice(ch * CHUNK, (ch + 1) * CHUNK)
            _, u, _, _, _, sg = _gating_forward(p_ref[rows, :].astype(F32), gv_ref[...], bv_ref[...], wt, bs_ref[...])
            oa = u * sg
            o_ref[rows, :] = (oa * _rsq_mean(oa) * ga_ref[...]).astype(BF16)

    return pl.pallas_call(
        body, grid=(s // TR,),
        in_specs=[_row_spec(2 * A_WIDTH), _vec_spec(A_WIDTH), _vec_spec(A_WIDTH),
                  pl.BlockSpec((A_GROUPS, CHUNK, CHUNK), lambda i: (0, 0, 0)),
                  pl.BlockSpec((CHUNK, A_WIDTH), lambda i: (0, 0)), _vec_spec(A_WIDTH)],
        out_specs=_row_spec(A_WIDTH), out_shape=jax.ShapeDtypeStruct((s, A_WIDTH + B_WIDTH), BF16),
        compiler_params=_cparams("parallel"), name=name)(proj, gv, bv, ws, bsf, ga)


def _mixer_a_bwd(proj, dmixed, gv, bv, ws, bsf, ga, name):
    s = proj.shape[0]
    nsteps = s // TR

    def body(p_ref, dm_ref, gv_ref, bv_ref, ws_ref, bs_ref, ga_ref,
             dp_ref, dga_ref, dgv_ref, dbv_ref, dbs_ref, dws_ref):
        _acc_init([dga_ref, dgv_ref, dbv_ref, dbs_ref, dws_ref])
        mask = _tril_mask()
        wt = _masked_ws(ws_ref)
        gvv = gv_ref[...]
        gav = ga_ref[...]
        for ch in range(TR // CHUNK):
            rows = slice(ch * CHUNK, (ch + 1) * CHUNK)
            pa = p_ref[rows, :].astype(F32)
            er, u, rs, vn, vlb, sg = _gating_forward(pa, gvv, bv_ref[...], wt, bs_ref[...])
            oa = u * sg
            doa, dgt = _rms_bwd(oa, _rsq_mean(oa), gav, dm_ref[rows, :])
            dga_ref[...] += _colsum(dgt)
            du = doa * sg
            dsg = doa * u
            dbs_ref[...] += dsg
            dsgb = dsg.astype(BF16)
            dvl = []
            for g in range(A_GROUPS):
                cols = slice(g * GROUP_DIM, (g + 1) * GROUP_DIM)
                dws_ref[g] += jnp.where(mask, _dot(dsgb[:, cols], vlb[:, cols], NT), 0.0)
                dvl.append(_dot(wt[g], dsgb[:, cols], TN))
            dvl = jnp.concatenate(dvl, axis=1)
            dgv_ref[...] += _colsum(dvl * vn)
            dbv_ref[...] += _colsum(dvl)
            dvn = dvl * gvv
            dva = rs * (dvn - jnp.mean(dvn, axis=-1, keepdims=True)
                        - vn * jnp.mean(dvn * vn, axis=-1, keepdims=True))
            gp = 0.5 * (1.0 + er) + pa * jnp.exp(-0.5 * pa * pa) * INV_SQRT_2PI
            dp_ref[rows, :] = (jnp.concatenate([du, dva], axis=1) * gp).astype(BF16)

        @pl.when(pl.program_id(0) == nsteps - 1)
        def _():
            for g in range(A_GROUPS):
                cols = slice(g * GROUP_DIM, (g + 1) * GROUP_DIM)
                tot = jnp.sum(dbs_ref[:, cols], axis=1, keepdims=True)
                dbs_ref[:, cols] = jnp.broadcast_to(tot, (CHUNK, GROUP_DIM))

    full = lambda *shape: pl.BlockSpec(shape, lambda i: (0,) * len(shape))
    return pl.pallas_call(
        body, grid=(nsteps,),
        in_specs=[_row_spec(2 * A_WIDTH), _row_spec(A_WIDTH), _vec_spec(A_WIDTH), _vec_spec(A_WIDTH),
                  full(A_GROUPS, CHUNK, CHUNK), full(CHUNK, A_WIDTH), _vec_spec(A_WIDTH)],
        out_specs=[_row_spec(2 * A_WIDTH), _vec_spec(A_WIDTH), _vec_spec(A_WIDTH), _vec_spec(A_WIDTH),
                   full(CHUNK, A_WIDTH), full(A_GROUPS, CHUNK, CHUNK)],
        out_shape=[jax.ShapeDtypeStruct((s, IN_COLS), BF16), jax.ShapeDtypeStruct((1, A_WIDTH), F32),
                   jax.ShapeDtypeStruct((1, A_WIDTH), F32), jax.ShapeDtypeStruct((1, A_WIDTH), F32),
                   jax.ShapeDtypeStruct((CHUNK, A_WIDTH), F32),
                   jax.ShapeDtypeStruct((A_GROUPS, CHUNK, CHUNK), F32)],
        compiler_params=_cparams("arbitrary"), name=name)(proj, dmixed, gv, bv, ws, bsf, ga)


def _rope_tables(s):
    half = ROT_DIM // 2
    inv = ROPE_THETA ** (-jnp.arange(0, ROT_DIM, 2, dtype=F32) / ROT_DIM)
    ang = jnp.arange(s, dtype=F32)[:, None] * inv[None, :]
    cos, sin = jnp.cos(ang), jnp.sin(ang)
    zeros = jnp.zeros((s, HEAD_DIM - ROT_DIM), F32)
    zh = jnp.zeros((s, half), F32)
    c = jnp.concatenate([cos, cos, zeros + 1.0], axis=1)
    s1 = jnp.concatenate([-sin, zh, zeros], axis=1)
    s2 = jnp.concatenate([zh, sin, zeros], axis=1)
    return tuple(jnp.concatenate([t, t], axis=1) for t in (c, s1, s2))


def _lane_blocks(width):
    return [slice(b * LANES, (b + 1) * LANES) for b in range(width // LANES)]


CLASS_DILS = tuple(d for d in DILATIONS if d > 1)


def _class_shape(s, dil, dtype):
    return jax.ShapeDtypeStruct((dil, s // dil, B_WIDTH), dtype)


def _class_spec(dil):
    return pl.BlockSpec((dil, TR // dil, B_WIDTH), lambda i, *_: (0, i, 0))


NBLK = B_WIDTH // LANES
STAGE = pltpu.VMEM((NBLK, TR, LANES), F32)


def _stage_put(stage, value):
    for b, sl in enumerate(_lane_blocks(B_WIDTH)):
        stage[b] = value[:, sl]


def _stage_get(stage):
    return jnp.concatenate([stage[b] for b in range(NBLK)], axis=1)


def _store_classes(stage, dst_ref, dil):
    for b, sl in enumerate(_lane_blocks(B_WIDTH)):
        for r in range(dil):
            dst_ref[r, :, sl] = stage[b, pl.ds(r, TR // dil, stride=dil), :].astype(dst_ref.dtype)


def _load_classes(src_ref, stage, dil):
    for b, sl in enumerate(_lane_blocks(B_WIDTH)):
        for r in range(dil):
            stage[b, pl.ds(r, TR // dil, stride=dil), :] = src_ref[r, :, sl].astype(F32)
    return _stage_get(stage)


def _rope_fwd(proj, tabs, name):
    s = proj.shape[0]
    half = ROT_DIM // 2
    scale = HEAD_DIM ** -0.5
    nlay = 1 + len(CLASS_DILS)

    def body(q_ref, k_ref, v_ref, c_ref, s1_ref, s2_ref, *rest):
        outs, stage = rest[:3 * nlay], rest[3 * nlay]
        c, s1, s2 = c_ref[...], s1_ref[...], s2_ref[...]
        for which, (src, mul) in enumerate(((q_ref, scale), (k_ref, 1.0), (v_ref, None))):
            if mul is None:
                _stage_put(stage, src[...].astype(F32))
            else:
                for b, sl in enumerate(_lane_blocks(B_WIDTH)):
                    a = src[:, sl].astype(F32)
                    r = a * c + pltpu.roll(a, LANES - half, 1) * s1 + pltpu.roll(a, half, 1) * s2
                    stage[b] = r * mul
            dst = outs[which * nlay:(which + 1) * nlay]
            dst[0][...] = _stage_get(stage).astype(BF16)
            for ref, d in zip(dst[1:], CLASS_DILS):
                _store_classes(stage, ref, d)

    tab = pl.BlockSpec((TR, LANES), lambda i: (i, 0))
    lay_specs = [_row_spec(B_WIDTH)] + [_class_spec(d) for d in CLASS_DILS]
    lay_shapes = [jax.ShapeDtypeStruct((s, B_WIDTH), BF16)] + [_class_shape(s, d, BF16) for d in CLASS_DILS]
    outs = pl.pallas_call(
        body, grid=(s // TR,),
        in_specs=[_row_spec(B_WIDTH, 2), _row_spec(B_WIDTH, 3), _row_spec(B_WIDTH, 4), tab, tab, tab],
        out_specs=lay_specs * 3, out_shape=lay_shapes * 3, scratch_shapes=[STAGE],
        compiler_params=_cparams("parallel"), name=name)(proj, proj, proj, *tabs)
    q, k, v = (dict(zip(DILATIONS, outs[w * nlay:(w + 1) * nlay])) for w in range(3))
    return q, k, v


def _as_classes(t):
    return t if t.ndim == 3 else t[None]


def _band_mask(i):
    qi = lax.broadcasted_iota(jnp.int32, (BAND, 2 * BAND), 0)
    kj = lax.broadcasted_iota(jnp.int32, (BAND, 2 * BAND), 1)
    return (kj >= qi) & (kj <= qi + BAND) & ((kj >= BAND) | (i > 0))


def _head_masks():
    lane = lax.broadcasted_iota(jnp.int32, (1, LANES), 1)
    return lane < HEAD_DIM, lane >= HEAD_DIM


def _stack_heads(t):
    lo, hi = _head_masks()
    zero = jnp.zeros_like(t)
    return jnp.concatenate([jnp.where(lo, t, zero), jnp.where(hi, t, zero)], axis=0)


def _attn_specs(last):
    cur = pl.BlockSpec((None, BAND, B_WIDTH), lambda r, i: (r, jnp.minimum(i, last), 0))
    prev = pl.BlockSpec((None, BAND, B_WIDTH), lambda r, i: (r, jnp.maximum(jnp.minimum(i, last) - 1, 0), 0))
    return cur, prev


def _attn_fwd(q, k, v, name, gather=None):
    dil, n, _ = q.shape
    nb = n // BAND
    ng = 0 if gather is None else len(gather)

    def body(*refs):
        q_ref, kc_ref, kp_ref, vc_ref, vp_ref = refs[:5]
        o_ref, l_ref = refs[5 + ng:7 + ng]
        if ng:
            start, relay, finish = _gather_steps(refs[5:5 + ng], refs[7 + ng:7 + 2 * ng], *refs[7 + 2 * ng:])
            first, last = _grid_edges((dil, nb))
            pl.when(first)(start)
        valid = _band_mask(pl.program_id(1))
        valid = jnp.concatenate([valid, valid], axis=0)
        lo, _ = _head_masks()
        for sl in _lane_blocks(B_WIDTH):
            kk = jnp.concatenate([kp_ref[:, sl], kc_ref[:, sl]], axis=0)
            vv = jnp.concatenate([vp_ref[:, sl], vc_ref[:, sl]], axis=0)
            sc = jnp.where(valid, _dot(_stack_heads(q_ref[:, sl]), kk, NT), NEG_INF)
            mx = jnp.max(sc, axis=1, keepdims=True)
            p = jnp.exp(sc - mx)
            den = jnp.sum(p, axis=1, keepdims=True)
            out = _dot(p.astype(BF16), vv, NN) / den
            lse = mx + jnp.log(den)
            o_ref[:, sl] = jnp.where(lo, out[:BAND], out[BAND:]).astype(BF16)
            l_ref[:, sl] = jnp.where(lo, lse[:BAND], lse[BAND:])

        if ng:
            @pl.when(last)
            def _():
                relay()
                finish()

    cur, prev = _attn_specs(nb - 1)
    sem = ("arbitrary", "arbitrary") if ng else ("parallel", "parallel")
    res = pl.pallas_call(
        body, grid=(dil, nb), in_specs=[cur, cur, prev, cur, prev] + [ANY] * ng, out_specs=[cur, cur] + [ANY] * ng,
        out_shape=[jax.ShapeDtypeStruct((dil, n, B_WIDTH), BF16), jax.ShapeDtypeStruct((dil, n, B_WIDTH), F32)]
        + _gathered_shapes(gather or []),
        scratch_shapes=_gather_sems(ng) if ng else [],
        compiler_params=_cparams(*sem), name=name)(q, k, k, v, v, *(gather or []))
    return res[0], res[1], list(res[2:])


MAX_CLASS_BLOCKS = 8


def _class_masks():
    qi = lax.broadcasted_iota(jnp.int32, (BAND, 2 * BAND), 0)
    kj = lax.broadcasted_iota(jnp.int32, (BAND, 2 * BAND), 1)
    both = (kj >= qi) & (kj <= qi + BAND)
    own = kj[:, :BAND] <= qi[:, :BAND]
    return jnp.concatenate([own, own], axis=0), jnp.concatenate([both, both], axis=0)


def _block_rows(g):
    return pl.ds(pl.multiple_of(g * BAND, BAND), BAND)


def _key_rows(g):
    return pl.ds(pl.multiple_of((g - 1) * BAND, BAND), 2 * BAND)


def _attn_fwd_class(q, k, v, name, gather=None):
    dil, n, _ = q.shape
    nb = n // BAND
    ng = 0 if gather is None else len(gather)

    def body(*refs):
        q_ref, k_ref, v_ref = refs[:3]
        o_ref, l_ref = refs[3 + ng:5 + ng]
        if ng:
            start, relay, finish = _gather_steps(refs[3:3 + ng], refs[5 + ng:5 + 2 * ng], *refs[5 + 2 * ng:])
            first, last = _grid_edges((dil,))
            pl.when(first)(start)
        own, both = _class_masks()
        lo, _ = _head_masks()

        def block(rows, keys, valid):
            for sl in _lane_blocks(B_WIDTH):
                sc = jnp.where(valid, _dot(_stack_heads(q_ref[rows, sl]), k_ref[keys, sl], NT), NEG_INF)
                mx = jnp.max(sc, axis=1, keepdims=True)
                p = jnp.exp(sc - mx)
                den = jnp.sum(p, axis=1, keepdims=True)
                out = _dot(p.astype(BF16), v_ref[keys, sl], NN) / den
                lse = mx + jnp.log(den)
                o_ref[rows, sl] = jnp.where(lo, out[:BAND], out[BAND:]).astype(BF16)
                l_ref[rows, sl] = jnp.where(lo, lse[:BAND], lse[BAND:])

        block(_block_rows(0), _block_rows(0), own)

        @pl.loop(1, nb)
        def _(g):
            block(_block_rows(g), _key_rows(g), both)

        if ng:
            @pl.when(last)
            def _():
                relay()
                finish()

    spec = pl.BlockSpec((None, n, B_WIDTH), lambda r: (r, 0, 0))
    res = pl.pallas_call(
        body, grid=(dil,), in_specs=[spec] * 3 + [ANY] * ng, out_specs=[spec, spec] + [ANY] * ng,
        out_shape=[jax.ShapeDtypeStruct((dil, n, B_WIDTH), BF16), jax.ShapeDtypeStruct((dil, n, B_WIDTH), F32)]
        + _gathered_shapes(gather or []),
        scratch_shapes=_gather_sems(ng) if ng else [],
        compiler_params=_cparams("arbitrary" if ng else "parallel"), name=name)(q, k, v, *(gather or []))
    return res[0], res[1], list(res[2:])


def _attn_bwd_class(q, k, v, do, lse, delta, name, scatter=None):
    dil, n, _ = q.shape
    nb = n // BAND
    ns = 0 if scatter is None else len(scatter[0])

    def body(*refs):
        q_ref, k_ref, v_ref, do_ref, lse_ref, dl_ref = refs[:6]
        dq_ref, dk_ref, dv_ref = refs[6 + ns:9 + ns]
        ck_ref, cv_ref = refs[9 + 2 * ns:11 + 2 * ns]
        if ns:
            start, finish = _scatter_steps(refs[6:6 + ns], refs[9 + ns:9 + 2 * ns], *refs[11 + 2 * ns:], scatter[1])
            first, last = _grid_edges((dil,))
            pl.when(first)(start)
        own, both = _class_masks()
        lo, _ = _head_masks()
        lane = lax.broadcasted_iota(jnp.int32, (1, LANES), 1)

        def per_head(t):
            return jnp.concatenate(
                [jnp.sum(jnp.where(lane == first, t, 0.0), axis=1, keepdims=True) for first in (0, HEAD_DIM)], axis=0)

        def grads(rows, keys, valid, sl):
            q2 = _stack_heads(q_ref[rows, sl])
            do2 = _stack_heads(do_ref[rows, sl])
            kk = k_ref[keys, sl]
            p = jnp.where(valid, jnp.exp(_dot(q2, kk, NT) - per_head(lse_ref[rows, sl])), 0.0)
            ds = (p * (_dot(do2, v_ref[keys, sl], NT) - per_head(dl_ref[rows, sl]))).astype(BF16)
            dq = _dot(ds, kk, NN)
            dq_ref[rows, sl] = jnp.where(lo, dq[:BAND], dq[BAND:]).astype(BF16)
            return _dot(ds, q2, TN), _dot(p.astype(BF16), do2, TN)

        for sl in _lane_blocks(B_WIDTH):
            ck_ref[:, sl], cv_ref[:, sl] = grads(_block_rows(0), _block_rows(0), own, sl)

        @pl.loop(1, nb)
        def _(g):
            before = _block_rows(g - 1)
            for sl in _lane_blocks(B_WIDTH):
                dkk, dvv = grads(_block_rows(g), _key_rows(g), both, sl)
                dk_ref[before, sl] = (ck_ref[:, sl] + dkk[:BAND]).astype(BF16)
                dv_ref[before, sl] = (cv_ref[:, sl] + dvv[:BAND]).astype(BF16)
                ck_ref[:, sl] = dkk[BAND:]
                cv_ref[:, sl] = dvv[BAND:]

        final = pl.ds((nb - 1) * BAND, BAND)
        dk_ref[final, :] = ck_ref[...].astype(BF16)
        dv_ref[final, :] = cv_ref[...].astype(BF16)

        if ns:
            pl.when(last)(finish)

    spec = pl.BlockSpec((None, n, B_WIDTH), lambda r: (r, 0, 0))
    shape = jax.ShapeDtypeStruct((dil, n, B_WIDTH), BF16)
    res = pl.pallas_call(
        body, grid=(dil,), in_specs=[spec] * 6 + [ANY] * ns, out_specs=[spec] * 3 + [ANY] * ns,
        out_shape=[shape] * 3 + (_scattered_shapes(scatter[1]) if ns else []),
        scratch_shapes=[pltpu.VMEM((BAND, B_WIDTH), F32)] * 2 + (_scatter_sems(ns) if ns else []),
        compiler_params=_cparams("arbitrary" if ns else "parallel"), name=name)(q, k, v, do, lse, delta,
                                                                             *(scatter[0] if ns else []))
    return res[0], res[1], res[2], list(res[3:])


def _attn_combine(outs, lses, gb, mixed, name):
    s = mixed.shape[0]
    npat = len(DILATIONS)
    w = B_WIDTH

    def body(*refs):
        o_refs, l_refs = refs[:npat], refs[npat:2 * npat]
        g_ref, _, ob_ref = refs[2 * npat:2 * npat + 3]
        lse_refs = refs[2 * npat + 3:3 * npat + 3]
        mb_ref, stage = refs[3 * npat + 3:]
        os_ = [o_refs[0][...].astype(F32)] + [_load_classes(r, stage, d) for r, d in zip(o_refs[1:], CLASS_DILS)]
        ls = [l_refs[0][...]] + [_load_classes(r, stage, d) for r, d in zip(l_refs[1:], CLASS_DILS)]
        mx = functools.reduce(jnp.maximum, ls)
        ws = [jnp.exp(l - mx) for l in ls]
        tot = functools.reduce(lambda a, b: a + b, ws)
        ob = functools.reduce(lambda a, b: a + b, [wt / tot * o for wt, o in zip(ws, os_)])
        ob_ref[...] = ob
        lse = mx + jnp.log(tot)
        _stage_put(stage, lse)
        lse_refs[0][...] = lse
        for ref, d in zip(lse_refs[1:], CLASS_DILS):
            _store_classes(stage, ref, d)
        mb_ref[...] = (ob * _rsq_mean(ob) * g_ref[...]).astype(BF16)

    lay_specs = [_row_spec(w)] + [_class_spec(d) for d in CLASS_DILS]
    res = pl.pallas_call(
        body, grid=(s // TR,), in_specs=lay_specs * 2 + [_vec_spec(w), ANY],
        out_specs=[_row_spec(w)] + lay_specs + [_row_spec(w, 1)],
        out_shape=[jax.ShapeDtypeStruct((s, w), F32), jax.ShapeDtypeStruct((s, w), F32)]
        + [_class_shape(s, d, F32) for d in CLASS_DILS] + [jax.ShapeDtypeStruct(mixed.shape, mixed.dtype)],
        scratch_shapes=[STAGE], input_output_aliases={2 * npat + 1: npat + 1},
        compiler_params=_cparams("parallel"), name=name)(*outs, *lses, gb, mixed)
    return res[0], dict(zip(DILATIONS, res[1:npat + 1])), res[npat + 1]


def _attn_bwd_prep(dmixed, ob, gb, name):
    s = ob.shape[0]
    w = B_WIDTH
    nlay = len(DILATIONS)

    def body(dm_ref, ob_ref, g_ref, *rest):
        do_refs, dl_refs = rest[:nlay], rest[nlay:2 * nlay]
        dg_ref, stage = rest[2 * nlay:]
        _acc_init([dg_ref])
        ob = ob_ref[...]
        dob, dgt = _rms_bwd(ob, _rsq_mean(ob), g_ref[...], dm_ref[...])
        dg_ref[...] += _colsum(dgt)
        _stage_put(stage, dob)
        do_refs[0][...] = dob.astype(BF16)
        for ref, d in zip(do_refs[1:], CLASS_DILS):
            _store_classes(stage, ref, d)
        lo, hi = _head_masks()
        t = dob * ob
        for b, sl in enumerate(_lane_blocks(w)):
            tb = t[:, sl]
            s0 = jnp.sum(jnp.where(lo, tb, 0.0), axis=1, keepdims=True)
            s1 = jnp.sum(jnp.where(hi, tb, 0.0), axis=1, keepdims=True)
            stage[b] = jnp.where(lo, s0, s1)
        dl_refs[0][...] = _stage_get(stage)
        for ref, d in zip(dl_refs[1:], CLASS_DILS):
            _store_classes(stage, ref, d)

    lay_specs = [_row_spec(w)] + [_class_spec(d) for d in CLASS_DILS]
    shapes = lambda dt: [jax.ShapeDtypeStruct((s, w), dt)] + [_class_shape(s, d, dt) for d in CLASS_DILS]
    res = pl.pallas_call(
        body, grid=(s // TR,), in_specs=[_row_spec(w, 1), _row_spec(w), _vec_spec(w)],
        out_specs=lay_specs * 2 + [_vec_spec(w)],
        out_shape=shapes(BF16) + shapes(F32) + [jax.ShapeDtypeStruct((1, w), F32)],
        scratch_shapes=[STAGE],
        compiler_params=_cparams("arbitrary"), name=name)(dmixed, ob, gb)
    return dict(zip(DILATIONS, res[:nlay])), dict(zip(DILATIONS, res[nlay:2 * nlay])), res[2 * nlay]


def _attn_bwd(q, k, v, do, lse, delta, name, scatter=None):
    dil, n, _ = q.shape
    nb = n // BAND
    ns = 0 if scatter is None else len(scatter[0])

    def body(*refs):
        q_ref, kc_ref, kp_ref, vc_ref, vp_ref, do_ref, lse_ref, dl_ref = refs[:8]
        dq_ref, dk_ref, dv_ref = refs[8 + ns:11 + ns]
        ck_ref, cv_ref = refs[11 + 2 * ns:13 + 2 * ns]
        i = pl.program_id(1)
        if ns:
            start, finish = _scatter_steps(refs[8:8 + ns], refs[11 + ns:11 + 2 * ns], *refs[13 + 2 * ns:],
                                           scatter[1])
            first, last = _grid_edges((dil, nb + 1))
            pl.when(first)(start)

        @pl.when(i == 0)
        def _():
            ck_ref[...] = jnp.zeros_like(ck_ref)
            cv_ref[...] = jnp.zeros_like(cv_ref)

        @pl.when(i < nb)
        def _():
            valid = _band_mask(i)
            valid = jnp.concatenate([valid, valid], axis=0)
            lo, _ = _head_masks()
            lane = lax.broadcasted_iota(jnp.int32, (1, LANES), 1)

            def per_head(t):
                return jnp.concatenate(
                    [jnp.sum(jnp.where(lane == first, t, 0.0), axis=1, keepdims=True) for first in (0, HEAD_DIM)], axis=0)

            for sl in _lane_blocks(B_WIDTH):
                q2 = _stack_heads(q_ref[:, sl])
                do2 = _stack_heads(do_ref[:, sl])
                kk = jnp.concatenate([kp_ref[:, sl], kc_ref[:, sl]], axis=0)
                vv = jnp.concatenate([vp_ref[:, sl], vc_ref[:, sl]], axis=0)
                p = jnp.where(valid, jnp.exp(_dot(q2, kk, NT) - per_head(lse_ref[:, sl])), 0.0)
                ds = (p * (_dot(do2, vv, NT) - per_head(dl_ref[:, sl]))).astype(BF16)
                dq = _dot(ds, kk, NN)
                dkk = _dot(ds, q2, TN)
                dvv = _dot(p.astype(BF16), do2, TN)
                dq_ref[:, sl] = jnp.where(lo, dq[:BAND], dq[BAND:]).astype(BF16)
                dk_ref[:, sl] = (ck_ref[:, sl] + dkk[:BAND]).astype(BF16)
                dv_ref[:, sl] = (cv_ref[:, sl] + dvv[:BAND]).astype(BF16)
                ck_ref[:, sl] = dkk[BAND:]
                cv_ref[:, sl] = dvv[BAND:]

        @pl.when(i == nb)
        def _():
            dk_ref[...] = ck_ref[...].astype(BF16)
            dv_ref[...] = cv_ref[...].astype(BF16)

        if ns:
            pl.when(last)(finish)

    cur, prev = _attn_specs(nb - 1)
    lag = pl.BlockSpec((None, BAND, B_WIDTH), lambda r, i: (r, jnp.maximum(i - 1, 0), 0))
    shape = jax.ShapeDtypeStruct((dil, n, B_WIDTH), BF16)
    res = pl.pallas_call(
        body, grid=(dil, nb + 1), in_specs=[cur, cur, prev, cur, prev, cur, cur, cur] + [ANY] * ns,
        out_specs=[cur, lag, lag] + [ANY] * ns,
        out_shape=[shape] * 3 + (_scattered_shapes(scatter[1]) if ns else []),
        scratch_shapes=[pltpu.VMEM((BAND, B_WIDTH), F32)] * 2 + (_scatter_sems(ns) if ns else []),
        compiler_params=_cparams("arbitrary", "arbitrary"), name=name)(q, k, k, v, v, do, lse, delta,
                                                                      *(scatter[0] if ns else []))
    return res[0], res[1], res[2], list(res[3:])


def _rope_bwd(dqs, dks, dvs, tabs, dproj, name):
    s = dproj.shape[0]
    half = ROT_DIM // 2
    scale = HEAD_DIM ** -0.5
    npat = len(DILATIONS)
    w = B_WIDTH

    def body(*refs):
        groups = [refs[g * npat:(g + 1) * npat] for g in range(3)]
        c_ref, s1_ref, s2_ref, _, o_ref, stage = refs[3 * npat:]

        def total(rs):
            acc = rs[0][...].astype(F32)
            for ref, d in zip(rs[1:], CLASS_DILS):
                acc = acc + _load_classes(ref, stage, d)
            return acc

        def unrope(g):
            c, s1, s2 = c_ref[...], s1_ref[...], s2_ref[...]
            for sl in _lane_blocks(w):
                gb = g[:, sl]
                o = gb * c + pltpu.roll(gb * s1, half, 1) + pltpu.roll(gb * s2, LANES - half, 1)
                o_ref[:, sl] = o.astype(BF16)

        which = pl.program_id(1)

        @pl.when(which == 0)
        def _():
            unrope(total(groups[0]) * scale)

        @pl.when(which == 1)
        def _():
            unrope(total(groups[1]))

        @pl.when(which == 2)
        def _():
            o_ref[...] = total(groups[2]).astype(BF16)

    tab = pl.BlockSpec((TR, LANES), lambda i, j: (i, 0))
    nat = pl.BlockSpec((TR, w), lambda i, j: (i, 0))
    lay_specs = [nat] + [_class_spec(d) for d in CLASS_DILS]
    first_col = 2 * A_WIDTH // w
    return pl.pallas_call(
        body, grid=(s // TR, 3), in_specs=lay_specs * 3 + [tab] * 3 + [ANY],
        out_specs=pl.BlockSpec((TR, w), lambda i, j: (i, first_col + j)),
        out_shape=jax.ShapeDtypeStruct(dproj.shape, dproj.dtype), scratch_shapes=[STAGE],
        input_output_aliases={3 * npat + 3: 0},
        compiler_params=_cparams("parallel", "arbitrary"), name=name)(*dqs, *dks, *dvs, *tabs, dproj)


TK = 512
HALO = 16


def _row_of(v, r):
    rows = lax.broadcasted_iota(jnp.int32, (v.shape[0], 1), 0)
    return jnp.sum(jnp.where(rows == r, v, 0.0), axis=0, keepdims=True)


def _taps_before(x, halo):
    row = lax.broadcasted_iota(jnp.int32, (x.shape[0], 1), 0)
    m1 = jnp.where(row == 0, _row_of(halo, HALO - 1), pltpu.roll(x, 1, 0))
    m2 = jnp.where(row == 0, _row_of(halo, HALO - 2), jnp.where(row == 1, _row_of(halo, HALO - 1), pltpu.roll(x, 2, 0)))
    return m2, m1, x


def _taps_after(x, halo):
    rows = x.shape[0]
    row = lax.broadcasted_iota(jnp.int32, (rows, 1), 0)
    p1 = jnp.where(row == rows - 1, _row_of(halo, 0), pltpu.roll(x, rows - 1, 0))
    p2 = jnp.where(row == rows - 2, _row_of(halo, 0), jnp.where(row == rows - 1, _row_of(halo, 1), pltpu.roll(x, rows - 2, 0)))
    return p1, p2


def _conv_value(taps, cw_ref, cb_ref, h):
    return cb_ref[h] + cw_ref[h, 0:1, :] * taps[0] + cw_ref[h, 1:2, :] * taps[1] + cw_ref[h, 2:3, :] * taps[2]


def _ffn_weight_specs(ncol):
    per_up = (2 * D_FF // N_CHIPS) // TK
    per_dn = (D_FF // N_CHIPS) // TK
    wg = pl.BlockSpec((None, None, D_MODEL, TK), lambda i, j: (j // per_up, 0, 0, j % per_up))
    wv = pl.BlockSpec((None, None, D_MODEL, TK), lambda i, j: ((j + ncol) // per_up, 0, 0, (j + ncol) % per_up))
    wd = pl.BlockSpec((None, None, TK, D_MODEL), lambda i, j: (j // per_dn, 0, j % per_dn, 0))
    cw = pl.BlockSpec((2, 3, TK), lambda i, j: (0, 0, j))
    cb = pl.BlockSpec((2, 1, TK), lambda i, j: (0, 0, j))
    return wg, wv, wd, cw, cb


def _ffn_forward(h2, w_up, w_down, cw3, cb3, name, gather=None):
    s = h2.shape[0]
    nm, ncol = s // TM, D_FF // TK
    ng = 0 if gather is None else len(gather)

    def body(*refs):
        h_ref, wg_ref, wv_ref, wd_ref, cw_ref, cb_ref = refs[:6]
        g_in = refs[6:6 + ng]
        y_ref, up_ref, cv_ref, f_ref = refs[6 + ng:10 + ng]
        g_out = refs[10 + ng:10 + 2 * ng]
        carry, acc = refs[10 + 2 * ng:12 + 2 * ng]
        i, j = pl.program_id(0), pl.program_id(1)
        if ng:
            start, relay, finish = _gather_steps(g_in, g_out, *refs[12 + 2 * ng:])
            pl.when((i == 0) & (j == 0))(start)
            pl.when((i == nm - 1) & (j == 0))(relay)

        @pl.when((i == 0) & (j == 0))
        def _():
            carry[...] = jnp.zeros_like(carry)

        h = h_ref[...]
        conv = []
        for hh, w_ref in ((0, wg_ref), (1, wv_ref)):
            up = _dot(h, w_ref[...], NN).astype(BF16)
            up_ref[hh] = up
            x = up.astype(F32)
            conv.append(_conv_value(_taps_before(x, carry[j, hh]), cw_ref, cb_ref, hh))
            cv_ref[hh] = conv[hh].astype(BF16)
            carry[j, hh] = x[TM - HALO:, :]
        y = (_gelu_tanh(conv[0])[0] * conv[1]).astype(BF16)
        y_ref[...] = y
        part = _dot(y, wd_ref[...], NN)

        @pl.when(j == 0)
        def _():
            acc[...] = part

        @pl.when(j > 0)
        def _():
            acc[...] += part

        @pl.when(j == ncol - 1)
        def _():
            f_ref[...] = acc[...]

        if ng:
            pl.when((i == nm - 1) & (j == ncol - 1))(finish)

    wg, wv, wd, cw, cb = _ffn_weight_specs(ncol)
    res = pl.pallas_call(
        body, grid=(nm, ncol),
        in_specs=[pl.BlockSpec((TM, D_MODEL), lambda i, j: (i, 0)), wg, wv, wd, cw, cb] + [ANY] * ng,
        out_specs=[pl.BlockSpec((TM, TK), lambda i, j: (i, j)), pl.BlockSpec((2, TM, TK), lambda i, j: (0, i, j)),
                   pl.BlockSpec((2, TM, TK), lambda i, j: (0, i, j)),
                   pl.BlockSpec((TM, D_MODEL), lambda i, j: (i, 0))] + [ANY] * ng,
        out_shape=[jax.ShapeDtypeStruct((s, D_FF), BF16), jax.ShapeDtypeStruct((2, s, D_FF), BF16),
                   jax.ShapeDtypeStruct((2, s, D_FF), BF16),
                   jax.ShapeDtypeStruct((s, D_MODEL), F32)] + _gathered_shapes(gather or []),
        scratch_shapes=[pltpu.VMEM((ncol, 2, HALO, TK), F32), pltpu.VMEM((TM, D_MODEL), F32)]
        + (_gather_sems(ng) if ng else []),
        compiler_params=_cparams("arbitrary", "arbitrary"), name=name)(h2, w_up, w_up, w_down, cw3, cb3,
                                                                      *(gather or []))
    return res[:4], list(res[4:])


def _ffn_backward(df, w_up, w_down, up3, cv3, cw3, name, scatter=None):
    s = df.shape[0]
    nm, ncol = s // TM, D_FF // TK
    ns = 0 if scatter is None else len(scatter[0])

    def body(*refs):
        df_ref, wg_ref, wv_ref, wd_ref, cw_ref, up_ref, cv_ref = refs[:7]
        s_in = refs[7:7 + ns]
        dup_ref, dh_ref, sums_ref = refs[7 + ns:10 + ns]
        s_out = refs[10 + ns:10 + 2 * ns]
        carry, acc = refs[10 + 2 * ns:12 + 2 * ns]
        i, j = pl.program_id(0), pl.program_id(1)
        if ns:
            start, finish = _scatter_steps(s_in, s_out, *refs[12 + 2 * ns:], scatter[1])
            pl.when((i == 0) & (j == 0))(start)

        @pl.when((i == 0) & (j == 0))
        def _():
            carry[...] = jnp.zeros_like(carry)
            sums_ref[...] = jnp.zeros_like(sums_ref)

        dy = _dot(df_ref[...], wd_ref[...], NT)
        act, grad = _gelu_tanh(cv_ref[0].astype(F32))
        dcs = (dy * cv_ref[1].astype(F32) * grad, dy * act)
        row = lax.broadcasted_iota(jnp.int32, (8, 1), 0)
        part = None
        for hh, w_ref in ((0, wg_ref), (1, wv_ref)):
            dc = dcs[hh]
            x = up_ref[hh].astype(F32)
            after1, after2 = _taps_after(dc, carry[j, hh])
            upd = jnp.zeros((8, TK), F32)
            for ridx, sm in enumerate((_colsum(after2 * x), _colsum(after1 * x), _colsum(dc * x), _colsum(dc))):
                upd = jnp.where(row == ridx, sm, upd)
            sums_ref[j, hh] += upd
            dup = (cw_ref[hh, 2:3, :] * dc + cw_ref[hh, 1:2, :] * after1 + cw_ref[hh, 0:1, :] * after2).astype(BF16)
            carry[j, hh] = dc[:HALO, :]
            dup_ref[hh] = dup
            d = _dot(dup, w_ref[...], NT)
            part = d if part is None else part + d

        @pl.when(j == 0)
        def _():
            acc[...] = part

        @pl.when(j > 0)
        def _():
            acc[...] += part

        @pl.when(j == ncol - 1)
        def _():
            dh_ref[...] = acc[...]

        if ns:
            pl.when((i == nm - 1) & (j == ncol - 1))(finish)

    wg, wv, wd, cw, _ = _ffn_weight_specs(ncol)
    rev = lambda i: nm - 1 - i
    res = pl.pallas_call(
        body, grid=(nm, ncol),
        in_specs=[pl.BlockSpec((TM, D_MODEL), lambda i, j: (rev(i), 0)), wg, wv, wd, cw,
                  pl.BlockSpec((2, TM, TK), lambda i, j: (0, rev(i), j)),
                  pl.BlockSpec((2, TM, TK), lambda i, j: (0, rev(i), j))] + [ANY] * ns,
        out_specs=[pl.BlockSpec((2, TM, TK), lambda i, j: (0, rev(i), j)),
                   pl.BlockSpec((TM, D_MODEL), lambda i, j: (rev(i), 0)),
                   pl.BlockSpec((ncol, 2, 8, TK), lambda i, j: (0, 0, 0, 0))] + [ANY] * ns,
        out_shape=[jax.ShapeDtypeStruct((2, s, D_FF), BF16), jax.ShapeDtypeStruct((s, D_MODEL), F32),
                   jax.ShapeDtypeStruct((ncol, 2, 8, TK), F32)] + (_scattered_shapes(scatter[1]) if ns else []),
        scratch_shapes=[pltpu.VMEM((ncol, 2, HALO, TK), F32), pltpu.VMEM((TM, D_MODEL), F32)]
        + (_scatter_sems(ns) if ns else []),
        compiler_params=_cparams("arbitrary", "arbitrary"), name=name)(df, w_up, w_up, w_down, cw3, up3, cv3,
                                                                      *(scatter[0] if ns else []))
    return res[:3], list(res[3:])


def _wspec(rows, cols, index_map):
    return pl.BlockSpec((None, None, rows, cols), index_map)


def _layer_forward(l, x0, h1, p, wg, tabs, gather=None, late=None):
    s = x0.shape[0]
    nm = s // TMM
    tag = f"_l{l}"
    riders = dict.fromkeys(DILATIONS)
    proj_rider = None
    if late is not None:
        halves = lambda t: (t[:, :t.shape[1] // 2], t[:, t.shape[1] // 2:])
        (down_a, down_b), (up_a, up_b) = halves(late["w_down"]), halves(late["w_up"])
        proj_rider = [late["w_out"], down_a]
        riders = dict(zip(DILATIONS, ([down_b], [up_a], [up_b])))
    proj = _matmul(
        h1, wg["w_in"], grid=(nm, N_CHIPS), a_spec=pl.BlockSpec((TMM, D_MODEL), lambda i, j: (i, 0)),
        b_spec=_wspec(D_MODEL, IN_COLS // N_CHIPS, lambda i, j: (j, 0, 0, 0)),
        o_spec=pl.BlockSpec((TMM, IN_COLS // N_CHIPS), lambda i, j: (i, j)), o_shape=(s, IN_COLS), o_dtype=BF16,
        dims=NN, nk=1, kaxis=None, acc_shape=None, name="proj" + tag, gather=proj_rider)
    if late is not None:
        proj, (w_out_all4, down_a) = proj
    ma = _mixer_a_fwd(proj, p["v_norm_g"], p["v_norm_b"], p["w_spatial"], p["bs_full"], p["out_norm_a"],
                      "mixer_a_fwd" + tag)
    q, k, v = _rope_fwd(proj, tabs, "rope_fwd" + tag)
    whole = lambda d: s // d // BAND <= MAX_CLASS_BLOCKS
    outs, lses, landed = zip(*[
        (_attn_fwd_class if whole(d) else _attn_fwd)(
            _as_classes(q[d]), _as_classes(k[d]), _as_classes(v[d]), f"attn_fwd_d{d}" + tag, riders[d])
        for d in DILATIONS])
    if late is not None:
        wg = dict(wg, w_out=w_out_all4, w_down=jnp.concatenate([down_a, landed[0][0]], axis=-1),
                  w_up=jnp.concatenate([landed[1][0], landed[2][0]], axis=-1))
    outs = [o.reshape(s, B_WIDTH) if d == 1 else o for o, d in zip(outs, DILATIONS)]
    lses = [t.reshape(s, B_WIDTH) if d == 1 else t for t, d in zip(lses, DILATIONS)]
    ob, lse, mixed = _attn_combine(outs, lses, p["out_norm_b"], ma, "attn_combine" + tag)
    w_out_all = pl.BlockSpec((N_CHIPS, None, D_MODEL // N_CHIPS, D_MODEL), lambda i: (0, 0, 0, 0))
    y1 = _matmul(
        mixed, wg["w_out"], grid=(nm,), a_spec=pl.BlockSpec((TMM, D_MODEL), lambda i: (i, 0)), b_spec=w_out_all,
        o_spec=pl.BlockSpec((TMM, D_MODEL), lambda i: (i, 0)), o_shape=(s, D_MODEL), o_dtype=F32,
        dims=NN, nk=1, kaxis=None, acc_shape=None, name="mix_out" + tag, b_2d=(D_MODEL, D_MODEL))
    x1, h2 = _residual_norm(x0, y1, p["post_mix_norm"], p["pre_ffn_norm"], "post_mix" + tag)
    (y, up3, cv3, f), gathered = _ffn_forward(h2, wg["w_up"], wg["w_down"], p["cw3"], p["cb3"], "ffn_fwd" + tag,
                                              gather)
    saved = dict(x0=x0, h1=h1, proj=proj, q=q, k=k, v=v, ob=ob, lse=lse, mixed=mixed, y1=y1, x1=x1, h2=h2,
                 up3=up3, cv3=cv3, y=y, f=f)
    return saved, gathered, wg


def _layer_backward(l, dx2, df, sv, p, wg, tabs, pos, scatter=None, hide=False):
    s = dx2.shape[0]
    nm = s // TMM
    tag = f"_l{l}"
    g = {}
    (dup3, dh2, conv_sums), scattered = _ffn_backward(df, wg["w_up"], wg["w_down"], sv["up3"], sv["cv3"], p["cw3"],
                                                      "ffn_bwd" + tag, scatter)
    sums = conv_sums.transpose(1, 2, 0, 3).reshape(2, 8, D_FF)
    g["conv_w"] = jnp.concatenate([sums[0, :3], sums[1, :3]], axis=1)
    g["conv_b"] = jnp.concatenate([sums[0, 3:4], sums[1, 3:4]], axis=1)
    tn = 1024
    done = {}
    gw_down = _matmul(
        sv["y"], df, grid=(D_FF // tn, nm), a_spec=pl.BlockSpec((TMM, tn), lambda k, m: (m, k)),
        b_spec=pl.BlockSpec((TMM, D_MODEL), lambda k, m: (m, 0)),
        o_spec=pl.BlockSpec((2, tn, D_MODEL // 2), lambda k, m: (0, k, 0)),
        o_shape=(2, D_FF, D_MODEL // 2), o_dtype=BF16,
        dims=TN, nk=nm, kaxis=1, acc_shape=(tn, D_MODEL), name="w_down_grad" + tag, halves=True)
    down_sums = _chip_sums(l, dict(w_down=gw_down), pos, ("w_down",)) if hide else None
    gw_up = _matmul(
        sv["h2"], dup3, grid=(2 * D_FF // tn, nm), a_spec=pl.BlockSpec((TMM, D_MODEL), lambda n, m: (m, 0)),
        b_spec=pl.BlockSpec((None, TMM, tn), lambda n, m: (n // (D_FF // tn), m, n % (D_FF // tn))),
        o_spec=pl.BlockSpec((None, D_MODEL, tn), lambda n, m: (n // 2, 0, n % 2)),
        o_shape=(N_CHIPS, D_MODEL, 2 * D_FF // N_CHIPS), o_dtype=BF16,
        dims=TN, nk=nm, kaxis=1, acc_shape=(D_MODEL, tn), name="w_up_grad" + tag,
        scatter=(down_sums, ("w_down",)) if hide else None)
    up_sums = None
    if hide:
        gw_up, received = gw_up
        done[("w_down",)] = (down_sums, received)
        up_sums = _chip_sums(l, dict(w_up=gw_up), pos, ("w_up",))
    dx1, dy1, g["pre_ffn_norm"], g["post_mix_norm"] = _norm_bwd_mid(
        dx2, dh2, sv["x1"], p["pre_ffn_norm"], sv["y1"], p["post_mix_norm"], "norm_bwd_mid" + tag)
    w_out_all = pl.BlockSpec((N_CHIPS, None, D_MODEL // N_CHIPS, D_MODEL), lambda i: (0, 0, 0, 0))
    dmixed = _matmul(
        dy1, wg["w_out"], grid=(nm,), a_spec=pl.BlockSpec((TMM, D_MODEL), lambda i: (i, 0)), b_spec=w_out_all,
        o_spec=pl.BlockSpec((TMM, D_MODEL), lambda i: (i, 0)), o_shape=(s, D_MODEL), o_dtype=F32,
        dims=NT, nk=1, kaxis=None, acc_shape=None, name="mix_out_bwd" + tag, b_2d=(D_MODEL, D_MODEL))
    gw_out = _matmul(
        sv["mixed"], dy1, grid=(nm,), a_spec=pl.BlockSpec((TMM, D_MODEL), lambda m: (m, 0)),
        b_spec=pl.BlockSpec((TMM, D_MODEL), lambda m: (m, 0)),
        o_spec=pl.BlockSpec((2, D_MODEL, D_MODEL // 2), lambda m: (0, 0, 0)),
        o_shape=(2, D_MODEL, D_MODEL // 2), o_dtype=BF16,
        dims=TN, nk=nm, kaxis=0, acc_shape=(D_MODEL, D_MODEL), name="w_out_grad" + tag, halves=True)
    dpa, g["out_norm_a"], g["v_norm_g"], g["v_norm_b"], dbs, g["w_spatial"] = _mixer_a_bwd(
        sv["proj"], dmixed, p["v_norm_g"], p["v_norm_b"], p["w_spatial"], p["bs_full"], p["out_norm_a"],
        "mixer_a_bwd" + tag)
    g["b_spatial"] = dbs[:, ::GROUP_DIM].T
    dob, delta, g["out_norm_b"] = _attn_bwd_prep(dmixed, sv["ob"], p["out_norm_b"], "attn_bwd_prep" + tag)
    last_dil = DILATIONS[-1]
    whole = lambda d: s // d // BAND <= MAX_CLASS_BLOCKS
    dqs, dks, dvs, received = zip(*[
        (_attn_bwd_class if whole(d) else _attn_bwd)(
            *(_as_classes(t[d]) for t in (sv["q"], sv["k"], sv["v"], dob, sv["lse"], delta)),
            f"attn_bwd_d{d}" + tag, (up_sums, ("w_up",)) if hide and d == last_dil else None)
        for d in DILATIONS])
    if hide:
        done[("w_up",)] = (up_sums, received[-1])
    nat = lambda ts: [t.reshape(s, B_WIDTH) if d == 1 else t for t, d in zip(ts, DILATIONS)]
    dproj = _rope_bwd(nat(dqs), nat(dks), nat(dvs), tabs, dpa, "rope_bwd" + tag)
    wcol = IN_COLS // N_CHIPS
    dh1 = _matmul(
        dproj, wg["w_in"], grid=(nm, N_CHIPS), a_spec=pl.BlockSpec((TMM, wcol), lambda i, n: (i, n)),
        b_spec=_wspec(D_MODEL, wcol, lambda i, n: (n, 0, 0, 0)),
        o_spec=pl.BlockSpec((TMM, D_MODEL), lambda i, n: (i, 0)), o_shape=(s, D_MODEL), o_dtype=F32,
        dims=NT, nk=N_CHIPS, kaxis=1, acc_shape=(TMM, D_MODEL), name="proj_bwd" + tag)
    gw_in = _matmul(
        sv["h1"], dproj, grid=(N_CHIPS, nm), a_spec=pl.BlockSpec((TMM, D_MODEL), lambda n, m: (m, 0)),
        b_spec=pl.BlockSpec((TMM, wcol), lambda n, m: (m, n)),
        o_spec=pl.BlockSpec((None, D_MODEL, wcol), lambda n, m: (n, 0, 0)),
        o_shape=(N_CHIPS, D_MODEL, wcol), o_dtype=BF16,
        dims=TN, nk=nm, kaxis=1, acc_shape=(D_MODEL, wcol), name="w_in_grad" + tag)
    big = dict(w_in=gw_in, w_out=gw_out) if hide else dict(w_in=gw_in, w_up=gw_up, w_out=gw_out, w_down=gw_down)
    return dx1, dh1, big, g, scattered, done


SMALL = ("pre_mix_norm", "v_norm_g", "v_norm_b", "w_spatial", "b_spatial", "out_norm_a", "out_norm_b",
         "post_mix_norm", "pre_ffn_norm", "conv_b", "post_ffn_norm")
BIG = ("w_in", "w_out", "w_up", "w_down")
DEPTH = 2


def _layer_params(l, small, conv_w_full):
    p = {n: small[n][l].reshape(1, -1) for n in SMALL if n not in ("w_spatial", "b_spatial")}
    p["w_spatial"] = small["w_spatial"][l]
    p["bs_full"] = jnp.repeat(small["b_spatial"][l].T, GROUP_DIM, axis=1)
    p["cw3"] = conv_w_full[l].reshape(3, 2, D_FF).transpose(1, 0, 2)
    p["cb3"] = small["conv_b"][l].reshape(2, 1, D_FF)
    return p


def _mesh_pos():
    return lax.axis_index("x"), lax.axis_index("y"), lax.axis_index("c")


def _other_chips(x, y):
    return [(1 - x, y), (x, 1 - y), (1 - x, 1 - y)]


def _gathered_shapes(blocks):
    return [jax.ShapeDtypeStruct((N_CHIPS, 1) + a.shape, a.dtype) for a in blocks]


def _gather_sems(nw):
    n = 2 * nw * (N_CHIPS - 1) + nw
    return [pltpu.SemaphoreType.DMA((n,)), pltpu.SemaphoreType.DMA((n,))]


def _gather_steps(ins, outs, send, recv):
    nw, nrel = len(ins), N_CHIPS - 1
    x, y, c = _mesh_pos()
    mine, sibling, chips = 2 * x + y, (x, y, 1 - c), _other_chips(x, y)

    def copy(src, dst, slot, to):
        return pltpu.make_async_remote_copy(src_ref=src, dst_ref=dst, send_sem=send.at[slot],
                                            recv_sem=recv.at[slot], device_id=to, device_id_type=MESH)

    def half_rows(t, core):
        rows = ins[t].shape[0] // 2
        return pl.ds(pl.multiple_of(core * rows, rows), rows)

    def landing(t, chip, core):
        return outs[t].at[chip, 0, half_rows(t, core), :]

    slots = [(t, r, chip) for t in range(nw) for r, chip in enumerate(chips)]
    own = [copy(ins[t], outs[t].at[mine, 0], 2 * nw * nrel + t, sibling) for t in range(nw)]
    first = [copy(ins[t].at[half_rows(t, c), :], landing(t, mine, c), t * nrel + r, (px, py, c))
             for t, r, (px, py) in slots]
    relays = [copy(landing(t, 2 * px + py, c), landing(t, 2 * px + py, c), nw * nrel + t * nrel + r, sibling)
              for t, r, (px, py) in slots]

    def start():
        for cp in own + first:
            cp.start()

    def relay():
        for (t, r, (px, py)), cp in zip(slots, relays):
            copy(landing(t, 2 * px + py, c), landing(t, 2 * px + py, c), t * nrel + r, (px, py, c)).wait_recv()
            cp.start()

    def finish():
        for t, r, (px, py) in slots:
            passed = landing(t, 2 * px + py, 1 - c)
            copy(passed, passed, nw * nrel + t * nrel + r, sibling).wait_recv()
        for cp in first + relays:
            cp.wait_send()
        for cp in own:
            cp.wait()

    return start, relay, finish


def _gather_weights(blocks, name):
    nw = len(blocks)

    def body(*refs):
        start, relay, finish = _gather_steps(refs[:nw], refs[nw:2 * nw], *refs[2 * nw:])
        start()
        relay()
        finish()

    return pl.pallas_call(
        body, in_specs=[ANY] * nw, out_specs=[ANY] * nw, out_shape=_gathered_shapes(blocks),
        scratch_shapes=_gather_sems(nw), name=name)(*blocks)


HALF = 512

GRAD_GEOM = {"w_in": ("rows", D_MODEL, IN_COLS // N_CHIPS), "w_up": ("rows", D_MODEL, 2 * D_FF // N_CHIPS),
             "w_out": ("cols", D_MODEL, D_MODEL // N_CHIPS), "w_down": ("cols", D_FF, D_FF // N_CHIPS)}


def _exchange_shape(n):
    kind, a, b = GRAD_GEOM[n]
    return (N_CHIPS, HALF, b) if kind == "rows" else (a, HALF)


def _piece_shape(n):
    kind, _, b = GRAD_GEOM[n]
    return (HALF, b) if kind == "rows" else (b, HALF)


def _half_of(ref, n, core):
    if GRAD_GEOM[n][0] == "rows":
        return ref.at[:, pl.ds(pl.multiple_of(core * HALF, HALF), HALF), :]
    return ref.at[core]


def _piece_of(ref, n, chip):
    kind, _, b = GRAD_GEOM[n]
    return ref.at[chip] if kind == "rows" else ref.at[pl.ds(pl.multiple_of(chip * b, b), b), :]


def _pair_exchange(g, names, name):
    n = len(names)

    def body(*refs):
        send, recv = refs[2 * n:]
        x, y, c = _mesh_pos()
        o = 1 - c
        cps = [pltpu.make_async_remote_copy(src_ref=_half_of(refs[t], nm, o), dst_ref=refs[n + t], send_sem=send.at[t],
                                            recv_sem=recv.at[t], device_id=(x, y, o), device_id_type=MESH)
               for t, nm in enumerate(names)]
        for cp in cps:
            cp.start()
        for cp in cps:
            cp.wait()

    return pl.pallas_call(
        body, in_specs=[ANY] * n, out_specs=[ANY] * n,
        out_shape=[jax.ShapeDtypeStruct(_exchange_shape(nm), BF16) for nm in names],
        scratch_shapes=[pltpu.SemaphoreType.DMA((n,)), pltpu.SemaphoreType.DMA((n,))],
        name=name)(*[g[nm] for nm in names])


def _pair_sum(g, recv, pos, names, name_prefix):
    def add(a, b, grid, a_spec, b_spec, name):
        def body(pos_ref, a_ref, b_ref, o_ref):
            o_ref[...] = (a_ref[...].astype(F32) + b_ref[...].astype(F32)).astype(BF16)

        return pl.pallas_call(
            body, grid_spec=pltpu.PrefetchScalarGridSpec(
                num_scalar_prefetch=1, grid=grid, in_specs=[a_spec, b_spec], out_specs=b_spec),
            out_shape=jax.ShapeDtypeStruct(b.shape, BF16), compiler_params=_cparams("parallel"), name=name)(pos, a, b)

    out = []
    for nm, r in zip(names, recv):
        kind, rows, width = GRAD_GEOM[nm]
        if kind == "rows":
            out.append(add(g[nm], r, (N_CHIPS,), pl.BlockSpec((None, HALF, width), lambda j, pos: (j, pos[2], 0)),
                           pl.BlockSpec((None, HALF, width), lambda j, pos: (j, 0, 0)), f"{name_prefix}_{nm}"))
        else:
            out.append(add(g[nm], r, (rows // D_MODEL,), pl.BlockSpec((None, D_MODEL, HALF), lambda j, pos: (pos[2], j, 0)),
                           pl.BlockSpec((D_MODEL, HALF), lambda j, pos: (j, 0)), f"{name_prefix}_{nm}"))
    return out


def _scattered_shapes(names):
    return [jax.ShapeDtypeStruct((N_CHIPS - 1,) + _piece_shape(nm), BF16) for nm in names]


def _scatter_sems(n):
    return [pltpu.SemaphoreType.DMA((n * (N_CHIPS - 1),)), pltpu.SemaphoreType.DMA((n * (N_CHIPS - 1),))]


def _scatter_steps(sums, outs, send, recv, names):
    nrel = N_CHIPS - 1
    x, y, c = _mesh_pos()
    cps = []
    for r, (px, py) in enumerate(_other_chips(x, y)):
        for t, nm in enumerate(names):
            cps.append(pltpu.make_async_remote_copy(
                src_ref=_piece_of(sums[t], nm, 2 * px + py), dst_ref=outs[t].at[r], send_sem=send.at[t * nrel + r],
                recv_sem=recv.at[t * nrel + r], device_id=(px, py, c), device_id_type=MESH))

    def start():
        for cp in cps:
            cp.start()

    def finish():
        for cp in cps:
            cp.wait()

    return start, finish


def _chip_scatter(sums, names, name):
    n = len(names)

    def body(*refs):
        start, finish = _scatter_steps(refs[:n], refs[n:2 * n], *refs[2 * n:], names)
        start()
        finish()

    return pl.pallas_call(
        body, in_specs=[ANY] * n, out_specs=[ANY] * n, out_shape=_scattered_shapes(names),
        scratch_shapes=_scatter_sems(n), name=name)(*sums)


def _chip_sum(sums, recv, pos, names, name_prefix):
    def add(a, b, a_spec, shape, name):
        def body(pos_ref, a_ref, b_ref, o_ref):
            tot = a_ref[...].astype(F32)
            for r in range(N_CHIPS - 1):
                tot = tot + b_ref[r].astype(F32)
            o_ref[...] = tot

        return pl.pallas_call(
            body, grid_spec=pltpu.PrefetchScalarGridSpec(
                num_scalar_prefetch=1, grid=(1,), in_specs=[a_spec, pl.BlockSpec(b.shape, lambda i, pos: (0, 0, 0))],
                out_specs=pl.BlockSpec((None,) + shape, lambda i, pos: (pos[2], 0, 0))),
            out_shape=jax.ShapeDtypeStruct((2,) + shape, F32), compiler_params=_cparams("arbitrary"),
            name=name)(pos, a, b)

    chip = lambda pos: 2 * pos[0] + pos[1]
    out = []
    for nm, a, b in zip(names, sums, recv):
        shape = _piece_shape(nm)
        if GRAD_GEOM[nm][0] == "rows":
            spec = pl.BlockSpec((None,) + shape, lambda i, pos: (chip(pos), 0, 0))
        else:
            spec = pl.BlockSpec(shape, lambda i, pos: (chip(pos), 0))
        out.append(add(a, b, spec, shape, f"{name_prefix}_{nm}"))
    return out


def _pair_share(totals, name):
    n = len(totals)

    def body(*refs):
        ins, outs = refs[:n], refs[n:2 * n]
        send, recv = refs[2 * n:]
        x, y, c = _mesh_pos()
        o = 1 - c
        cps = [pltpu.make_async_remote_copy(src_ref=ins[t].at[c], dst_ref=outs[t].at[c], send_sem=send.at[t],
                                            recv_sem=recv.at[t], device_id=(x, y, o), device_id_type=MESH)
               for t in range(n)]
        for cp in cps:
            cp.start()
        for t in range(n):
            pltpu.make_async_remote_copy(src_ref=ins[t].at[o], dst_ref=outs[t].at[o], send_sem=send.at[t],
                                         recv_sem=recv.at[t], device_id=(x, y, o), device_id_type=MESH).wait_recv()
        for cp in cps:
            cp.wait_send()

    return pl.pallas_call(
        body, in_specs=[ANY] * n, out_specs=[ANY] * n,
        out_shape=[jax.ShapeDtypeStruct(t.shape, t.dtype) for t in totals],
        scratch_shapes=[pltpu.SemaphoreType.DMA((n,)), pltpu.SemaphoreType.DMA((n,))],
        input_output_aliases={t: t for t in range(n)}, name=name)(*totals)


def _chip_sums(l, g, pos, names):
    tag = f"l{l}_" + "_".join(names)
    recv = _pair_exchange(g, names, "pair_exchange_" + tag)
    return _pair_sum(g, recv, pos, names, "pair_sum_" + tag)


def _gradient_shards(l, sums, scattered, pos, names):
    tag = f"l{l}_" + "_".join(names)
    halves = _pair_share(_chip_sum(sums, scattered, pos, names, "chip_sum_" + tag), "pair_share_" + tag)
    out = {}
    for nm, t in zip(names, halves):
        rows, cols = _piece_shape(nm)
        out[nm] = t.reshape(2 * rows, cols) if GRAD_GEOM[nm][0] == "rows" else t.transpose(1, 0, 2).reshape(rows, 2 * cols)
    return out


N_DEV = 8


def _allreduce_small(packed, name):
    rows = packed.shape[0]

    def body(x_ref, out_ref, gath, send_sems, recv_sems, local_sem):
        x, y, c = _mesh_pos()
        me, sibling = (x, y, c), (x, y, 1 - c)
        chips = _other_chips(x, y)

        def blk(px, py, pc):
            return gath.at[pl.ds(pl.multiple_of((4 * px + 2 * py + pc) * rows, 8), rows), :]

        def copy(k, block, to, src=None):
            return pltpu.make_async_remote_copy(
                src_ref=blk(*block) if src is None else src, dst_ref=blk(*block), send_sem=send_sems.at[k],
                recv_sem=recv_sems.at[k], device_id=to, device_id_type=MESH)

        mine = pltpu.make_async_copy(x_ref, blk(*me), local_sem)
        mine.start()
        first = [copy(0, me, sibling, src=x_ref)]
        first += [copy(1 + j, me, (*chip, c), src=x_ref) for j, chip in enumerate(chips)]
        for cp in first:
            cp.start()
        passed = [copy(4 + j, (*chip, c), sibling) for j, chip in enumerate(chips)]
        for j, chip in enumerate(chips):
            copy(1 + j, (*chip, c), me).wait_recv()
            passed[j].start()
        copy(0, sibling, me).wait_recv()
        for j, chip in enumerate(chips):
            copy(4 + j, (*chip, 1 - c), me).wait_recv()
        for cp in first + passed:
            cp.wait_send()
        mine.wait()
        tot = gath[0:rows, :]
        for d in range(1, N_DEV):
            tot = tot + gath[d * rows:(d + 1) * rows, :]
        out_ref[...] = tot

    vmem = pl.BlockSpec(memory_space=pltpu.VMEM)
    return pl.pallas_call(
        body, in_specs=[vmem], out_specs=vmem, out_shape=jax.ShapeDtypeStruct((rows, LANES), F32),
        scratch_shapes=[pltpu.VMEM((N_DEV * rows, LANES), F32), pltpu.SemaphoreType.DMA((7,)),
                        pltpu.SemaphoreType.DMA((7,)), pltpu.SemaphoreType.DMA],
        compiler_params=pltpu.CompilerParams(vmem_limit_bytes=VMEM_LIMIT_BYTES),
        name=name)(packed)


def _adamw(w, g, m, v, name):
    rows, cols = w.shape
    tr = 256 if rows % 256 == 0 else rows

    def body(w_ref, g_ref, m_ref, v_ref, d_ref, mo_ref, vo_ref):
        gv = g_ref[...]
        mn = ADAM_B1 * m_ref[...] + (1.0 - ADAM_B1) * gv
        vn = ADAM_B2 * v_ref[...] + (1.0 - ADAM_B2) * (gv * gv)
        m_hat = mn / (1.0 - ADAM_B1 ** ADAM_STEP)
        v_hat = vn / (1.0 - ADAM_B2 ** ADAM_STEP)
        d_ref[...] = -ADAM_LR * (m_hat / (jnp.sqrt(v_hat) + ADAM_EPS) + ADAM_WD * w_ref[...])
        mo_ref[...] = mn
        vo_ref[...] = vn

    spec = pl.BlockSpec((tr, cols), lambda i: (i, 0))
    return pl.pallas_call(
        body, grid=(rows // tr,), in_specs=[spec] * 4, out_specs=[spec] * 3,
        out_shape=[jax.ShapeDtypeStruct((rows, cols), F32)] * 3, compiler_params=_cparams("parallel"),
        name=name)(w, g, m, v)


def _adamw_nd(w, g, m, v, name):
    cols = w.shape[-1] if w.shape[-1] % LANES == 0 else LANES
    outs = _adamw(*(t.reshape(-1, cols) for t in (w, g, m, v)), name)
    return tuple(t.reshape(w.shape) for t in outs)


def _pack(arrays):
    return jnp.concatenate([a.reshape(-1, LANES) for a in arrays], axis=0)


def _unpack(packed, shapes):
    out, row = [], 0
    for sh in shapes:
        n = math.prod(sh) // LANES
        out.append(packed[row:row + n].reshape(sh))
        row += n
    return out


WEIGHTS = ("pre_mix_norm", "w_in", "v_norm_g", "v_norm_b", "w_spatial", "b_spatial", "out_norm_a", "out_norm_b",
           "w_out", "post_mix_norm", "pre_ffn_norm", "w_up", "conv_w", "conv_b", "w_down", "post_ffn_norm")


def kernel(x, pre_mix_norm, w_in, v_norm_g, v_norm_b, w_spatial, b_spatial, out_norm_a, out_norm_b, w_out, post_mix_norm, pre_ffn_norm, w_up, conv_w, conv_b, w_down, post_ffn_norm, loss_target, m_pre_mix_norm, m_w_in, m_v_norm_g, m_v_norm_b, m_w_spatial, m_b_spatial, m_out_norm_a, m_out_norm_b, m_w_out, m_post_mix_norm, m_pre_ffn_norm, m_w_up, m_conv_w, m_conv_b, m_w_down, m_post_ffn_norm, v_pre_mix_norm, v_w_in, v_v_norm_g, v_v_norm_b, v_w_spatial, v_b_spatial, v_out_norm_a, v_out_norm_b, v_w_out, v_post_mix_norm, v_pre_ffn_norm, v_w_up, v_conv_w, v_conv_b, v_w_down, v_post_ffn_norm):
    w = dict(pre_mix_norm=pre_mix_norm, w_in=w_in, v_norm_g=v_norm_g, v_norm_b=v_norm_b, w_spatial=w_spatial,
             b_spatial=b_spatial, out_norm_a=out_norm_a, out_norm_b=out_norm_b, w_out=w_out,
             post_mix_norm=post_mix_norm, pre_ffn_norm=pre_ffn_norm, w_up=w_up, conv_w=conv_w, conv_b=conv_b,
             w_down=w_down, post_ffn_norm=post_ffn_norm)
    m = dict(pre_mix_norm=m_pre_mix_norm, w_in=m_w_in, v_norm_g=m_v_norm_g, v_norm_b=m_v_norm_b,
             w_spatial=m_w_spatial, b_spatial=m_b_spatial, out_norm_a=m_out_norm_a, out_norm_b=m_out_norm_b,
             w_out=m_w_out, post_mix_norm=m_post_mix_norm, pre_ffn_norm=m_pre_ffn_norm, w_up=m_w_up,
             conv_w=m_conv_w, conv_b=m_conv_b, w_down=m_w_down, post_ffn_norm=m_post_ffn_norm)
    v = dict(pre_mix_norm=v_pre_mix_norm, w_in=v_w_in, v_norm_g=v_v_norm_g, v_norm_b=v_v_norm_b,
             w_spatial=v_w_spatial, b_spatial=v_b_spatial, out_norm_a=v_out_norm_a, out_norm_b=v_out_norm_b,
             w_out=v_w_out, post_mix_norm=v_post_mix_norm, pre_ffn_norm=v_pre_ffn_norm, w_up=v_w_up,
             conv_w=v_conv_w, conv_b=v_conv_b, w_down=v_w_down, post_ffn_norm=v_post_ffn_norm)
    pos = jnp.stack([lax.axis_index("x"), lax.axis_index("y"), lax.axis_index("c")]).astype(jnp.int32)
    chip = 2 * lax.axis_index("x") + lax.axis_index("y")

    cw_cols = conv_w.shape[-1]
    cw_slab = lax.dynamic_update_slice(jnp.zeros((DEPTH, 3, 2 * D_FF), F32), conv_w, (0, 0, chip * cw_cols))
    conv_w_full = _allreduce_small(cw_slab.reshape(-1, LANES), "gather_conv_w").reshape(DEPTH, 3, 2 * D_FF)
    conv_w_full = conv_w_full * 0.5
    blocks = [{n: w[n][l].astype(BF16) for n in BIG} for l in range(DEPTH)]
    wg = dict(w_in=_gather_weights([blocks[0]["w_in"]], "gather_w_in_l0")[0])

    small = {n: w[n] for n in SMALL}
    xs, target = x[0], loss_target[0]
    tabs = _rope_tables(xs.shape[0])
    params = [_layer_params(l, small, conv_w_full) for l in range(DEPTH)]
    saved, wgs = [], []
    xin = xs
    h = _rms_cast(xin, params[0]["pre_mix_norm"], "pre_mix_l0")
    for l in range(DEPTH):
        sv, gathered, wg = _layer_forward(l, xin, h, params[l], wg, tabs,
                                          [blocks[l + 1][n] for n in BIG] if l + 1 < DEPTH else None,
                                          blocks[0] if l == 0 else None)
        saved.append(sv)
        wgs.append(wg)
        if l + 1 < DEPTH:
            wg = dict(zip(BIG, gathered))
            xin, h = _residual_norm(sv["x1"], sv["f"], params[l]["post_ffn_norm"], params[l + 1]["pre_mix_norm"],
                                    f"post_ffn_l{l}")
    loss_part, dx, df, g_post = _loss_norm_bwd(saved[-1]["x1"], saved[-1]["f"], params[-1]["post_ffn_norm"], target,
                                               "loss")
    smalls, shards = [None] * DEPTH, [{} for _ in range(DEPTH)]
    pending = None
    for l in reversed(range(DEPTH)):
        dx1, dh1, big, smalls[l], scattered, done = _layer_backward(l, dx, df, saved[l], params[l], wgs[l], tabs, pos,
                                                                    pending[1:] if pending else None, hide=l == 0)
        smalls[l]["post_ffn_norm"] = g_post
        if l > 0:
            dx, smalls[l]["pre_mix_norm"], df, g_post = _norm_bwd_in_out(
                dx1, dh1, saved[l]["x0"], params[l]["pre_mix_norm"], saved[l - 1]["f"], params[l - 1]["post_ffn_norm"],
                f"norm_bwd_in_out_l{l}")
        else:
            dx, smalls[l]["pre_mix_norm"] = _norm_bwd_in(dx1, dh1, saved[l]["x0"], params[l]["pre_mix_norm"],
                                                         "norm_bwd_in_l0")
        if pending:
            shards[pending[0]].update(_gradient_shards(pending[0], pending[1], scattered, pos, pending[2]))
        for names, (sums, received) in done.items():
            shards[l].update(_gradient_shards(l, sums, received, pos, names))
        names = tuple(big)
        pending = (l, _chip_sums(l, big, pos, names), names)
    shards[pending[0]].update(_gradient_shards(
        pending[0], pending[1], _chip_scatter(pending[1], pending[2], f"chip_scatter_l{pending[0]}"), pos, pending[2]))

    small_shapes = [w[n].shape for n in SMALL]
    stacked = [jnp.stack([smalls[l][n].reshape(w[n].shape[1:]) for l in range(DEPTH)]) for n in SMALL]
    cw_grad = jnp.stack([smalls[l]["conv_w"] for l in range(DEPTH)])
    packed = _pack(stacked + [cw_grad, loss_part])
    total = _allreduce_small(packed, "allreduce_small")
    parts = _unpack(total, small_shapes + [cw_grad.shape, (8, LANES)])
    g_small = dict(zip(SMALL, parts[:len(SMALL)]))
    loss = parts[-1][0, 0]
    g_conv_w = lax.dynamic_slice(parts[-2], (0, 0, chip * cw_cols), conv_w.shape)

    grads = {n: jnp.stack([shards[l][n] for l in range(DEPTH)]) for n in BIG}
    grads.update(g_small)
    grads["conv_w"] = g_conv_w

    dp, mp, vp = _adamw(_pack([w[n] for n in SMALL]), _pack([g_small[n] for n in SMALL]),
                        _pack([m[n] for n in SMALL]), _pack([v[n] for n in SMALL]), "adamw_small")
    delta = dict(zip(SMALL, _unpack(dp, small_shapes)))
    new_m = dict(zip(SMALL, _unpack(mp, small_shapes)))
    new_v = dict(zip(SMALL, _unpack(vp, small_shapes)))
    for n in BIG + ("conv_w",):
        delta[n], new_m[n], new_v[n] = _adamw_nd(w[n], grads[n], m[n], v[n], "adamw_" + n)

    return (loss, dx[None], *[grads[n] for n in WEIGHTS], *[delta[n] for n in WEIGHTS],
            *[new_m[n] for n in WEIGHTS], *[new_v[n] for n in WEIGHTS])
```

```python
import functools
import math

import jax
import jax.numpy as jnp
import numpy as np
from jax import lax
from jax.experimental import pallas as pl
from jax.experimental.pallas import tpu as pltpu

F32 = jnp.float32
BF16 = jnp.bfloat16
MESH = pl.DeviceIdType.MESH

D_MODEL = 1024
A_WIDTH = 512
A_GROUPS = 4
GROUP_DIM = 128
CHUNK = 128
B_WIDTH = 512
HEAD_DIM = 64
ROT_DIM = 16
ROPE_THETA = 500000.0
DILATIONS = (1, 4, 16)
BAND = 128
IN_COLS = 2560
D_FF = 4096
EPS = 1e-6
NEG_INF = -1e30
N_CHIPS = 4
LANES = 128

ADAM_LR = 0.001
ADAM_B1 = 0.9
ADAM_B2 = 0.999
ADAM_EPS = 1e-08
ADAM_WD = 0.01
ADAM_STEP = 10

VMEM_LIMIT_BYTES = 56 * 1024 * 1024
RSQRT2 = 0.7071067811865476
INV_SQRT_2PI = 0.3989422804014327
GELU_C = 0.7978845608028654
GELU_A = 0.044715

ANY = pl.BlockSpec(memory_space=pl.ANY)
NN = ((1,), (0,))
NT = ((1,), (1,))
TN = ((0,), (0,))


def _cparams(*sem):
    return pltpu.CompilerParams(dimension_semantics=sem, vmem_limit_bytes=VMEM_LIMIT_BYTES)


def _dot(a, b, dims):
    return lax.dot_general(a, b, (dims, ((), ())), preferred_element_type=F32)


def _rsq_mean(a):
    return lax.rsqrt(jnp.mean(a * a, axis=-1, keepdims=True) + EPS)


def _rms_bwd(a, r, g, dz):
    t = dz * g
    da = r * t - a * (r * r * r) * jnp.mean(t * a, axis=-1, keepdims=True)
    return da, dz * a * r


def _colsum(a):
    return jnp.sum(a, axis=0, keepdims=True)


def _gelu_tanh(x):
    u = x * x
    t = jnp.tanh(x * (GELU_C + (GELU_C * GELU_A) * u))
    hx = 0.5 * x
    act = hx + hx * t
    grad = 0.5 + 0.5 * t + (hx - hx * t * t) * (GELU_C + (3.0 * GELU_C * GELU_A) * u)
    return act, grad


def _grid_edges(grid):
    ids = [pl.program_id(ax) for ax in range(len(grid))]
    first = functools.reduce(jnp.logical_and, [i == 0 for i in ids])
    last = functools.reduce(jnp.logical_and, [i == n - 1 for i, n in zip(ids, grid)])
    return first, last


def _matmul(a, b, *, grid, a_spec, b_spec, o_spec, o_shape, o_dtype, dims, nk, kaxis, acc_shape, name, b_2d=None,
            halves=False, scatter=None, gather=None):
    assert scatter is None or gather is None
    ns = len(scatter[0]) if scatter else len(gather) if gather else 0

    def body(*refs):
        a_ref, b_ref = refs[:2]
        o_ref = refs[2 + ns]
        scratch = refs[3 + 2 * ns:]
        if ns:
            first, last = _grid_edges(grid)
            if scatter:
                start, finish = _scatter_steps(refs[2:2 + ns], refs[3 + ns:3 + 2 * ns], scratch[-2], scratch[-1],
                                               scatter[1])
            else:
                start, relay, last_wait = _gather_steps(refs[2:2 + ns], refs[3 + ns:3 + 2 * ns], scratch[-2],
                                                        scratch[-1])

                def finish():
                    relay()
                    last_wait()
            pl.when(first)(start)
        def store(val):
            if halves:
                half = val.shape[1] // 2
                o_ref[0] = val[:, :half].astype(o_dtype)
                o_ref[1] = val[:, half:].astype(o_dtype)
            else:
                o_ref[...] = val.astype(o_dtype)

        bv = b_ref[...] if b_2d is None else b_ref[...].reshape(b_2d)
        part = _dot(a_ref[...], bv, dims)
        if nk == 1:
            store(part)
        else:
            acc = scratch[0]
            k = pl.program_id(kaxis)

            @pl.when(k == 0)
            def _():
                acc[...] = part

            @pl.when(k > 0)
            def _():
                acc[...] += part

            @pl.when(k == nk - 1)
            def _():
                store(acc[...])

        if ns:
            pl.when(last)(finish)

    sem = tuple("arbitrary" if (ns or (nk > 1 and ax == kaxis)) else "parallel" for ax in range(len(grid)))
    riding = list(scatter[0]) if scatter else list(gather or [])
    rider_shapes = _scattered_shapes(scatter[1]) if scatter else _gathered_shapes(riding)
    rider_sems = _scatter_sems(ns) if scatter else _gather_sems(ns) if gather else []
    res = pl.pallas_call(
        body, grid=grid, in_specs=[a_spec, b_spec] + [ANY] * ns, out_specs=[o_spec] + [ANY] * ns,
        out_shape=[jax.ShapeDtypeStruct(o_shape, o_dtype)] + rider_shapes,
        scratch_shapes=([pltpu.VMEM(acc_shape, F32)] if nk > 1 else []) + rider_sems,
        compiler_params=_cparams(*sem), name=name)(a, b, *riding)
    return (res[0], list(res[1:])) if ns else res[0]


def _mix_out_norm(mixed, w_out, x0, g_post, g_next, name):
    s, d = x0.shape
    tm = 512

    def body(a_ref, w_ref, x_ref, gp_ref, gn_ref, y_ref, x1_ref, h_ref):
        y = _dot(a_ref[...], w_ref[...].reshape(d, d), NN)
        y_ref[...] = y
        x1 = x_ref[...] + y * _rsq_mean(y) * gp_ref[...]
        x1_ref[...] = x1
        h_ref[...] = (x1 * _rsq_mean(x1) * gn_ref[...]).astype(BF16)

    row = pl.BlockSpec((tm, d), lambda i: (i, 0))
    vec = pl.BlockSpec((1, d), lambda i: (0, 0))
    return pl.pallas_call(
        body, grid=(s // tm,),
        in_specs=[row, pl.BlockSpec((N_CHIPS, None, d // N_CHIPS, d), lambda i: (0, 0, 0, 0)), row, vec, vec],
        out_specs=[row, row, row],
        out_shape=[jax.ShapeDtypeStruct((s, d), F32), jax.ShapeDtypeStruct((s, d), F32),
                   jax.ShapeDtypeStruct((s, d), BF16)],
        compiler_params=_cparams("parallel"), name=name)(mixed, w_out, x0, g_post, g_next)


TM = 512
TMM = 1024


TR = 256


def _row_spec(width, col=0):
    return pl.BlockSpec((TR, width), lambda i, col=col: (i, col))


def _vec_spec(width):
    return pl.BlockSpec((1, width), lambda i: (0, 0))


def _rms_cast(x, g, name):
    s, d = x.shape

    def body(x_ref, g_ref, h_ref):
        a = x_ref[...]
        h_ref[...] = (a * _rsq_mean(a) * g_ref[...]).astype(BF16)

    return pl.pallas_call(
        body, grid=(s // TR,), in_specs=[_row_spec(d), _vec_spec(d)], out_specs=_row_spec(d),
        out_shape=jax.ShapeDtypeStruct((s, d), BF16), compiler_params=_cparams("parallel"), name=name)(x, g)


def _acc_init(refs):
    @pl.when(pl.program_id(0) == 0)
    def _():
        for r in refs:
            r[...] = jnp.zeros_like(r)


def _loss_norm_bwd(x1, f, g_post, target, name):
    s, d = x1.shape

    def body(x_ref, f_ref, gp_ref, t_ref, loss_ref, dx_ref, df_ref, dg_ref):
        _acc_init([loss_ref, dg_ref])
        fv = f_ref[...]
        r = _rsq_mean(fv)
        err = x_ref[...] + fv * r * gp_ref[...] - t_ref[...]
        dx = err * (1.0 / d)
        dx_ref[...] = dx
        part = 0.5 * jnp.sum(jnp.mean(err * err, axis=-1, keepdims=True), axis=0, keepdims=True)
        loss_ref[...] += jnp.broadcast_to(part, loss_ref.shape)
        da, dgt = _rms_bwd(fv, r, gp_ref[...], dx)
        df_ref[...] = da.astype(BF16)
        dg_ref[...] += _colsum(dgt)

    return pl.pallas_call(
        body, grid=(s // TR,), in_specs=[_row_spec(d), _row_spec(d), _vec_spec(d), _row_spec(d)],
        out_specs=[pl.BlockSpec((8, LANES), lambda i: (0, 0)), _row_spec(d), _row_spec(d), _vec_spec(d)],
        out_shape=[jax.ShapeDtypeStruct((8, LANES), F32), jax.ShapeDtypeStruct((s, d), F32),
                   jax.ShapeDtypeStruct((s, d), BF16), jax.ShapeDtypeStruct((1, d), F32)],
        compiler_params=_cparams("arbitrary"), name=name)(x1, f, g_post, target)


def _norm_bwd_mid(dx2, dh2, x1, g_pf, y1, g_pm, name):
    s, d = dx2.shape

    def body(dx2_ref, dh_ref, x1_ref, gpf_ref, y1_ref, gpm_ref, dx1_ref, dy1_ref, dgpf_ref, dgpm_ref):
        _acc_init([dgpf_ref, dgpm_ref])
        x1 = x1_ref[...]
        da, dgt = _rms_bwd(x1, _rsq_mean(x1), gpf_ref[...], dh_ref[...])
        dx1 = dx2_ref[...] + da
        dx1_ref[...] = dx1
        dgpf_ref[...] += _colsum(dgt)
        y1 = y1_ref[...]
        dy, dgt2 = _rms_bwd(y1, _rsq_mean(y1), gpm_ref[...], dx1)
        dy1_ref[...] = dy.astype(BF16)
        dgpm_ref[...] += _colsum(dgt2)

    return pl.pallas_call(
        body, grid=(s // TR,),
        in_specs=[_row_spec(d), _row_spec(d), _row_spec(d), _vec_spec(d), _row_spec(d), _vec_spec(d)],
        out_specs=[_row_spec(d), _row_spec(d), _vec_spec(d), _vec_spec(d)],
        out_shape=[jax.ShapeDtypeStruct((s, d), F32), jax.ShapeDtypeStruct((s, d), BF16),
                   jax.ShapeDtypeStruct((1, d), F32), jax.ShapeDtypeStruct((1, d), F32)],
        compiler_params=_cparams("arbitrary"), name=name)(dx2, dh2, x1, g_pf, y1, g_pm)


def _norm_bwd_in_out(dx1, dh1, x0, g1, f_below, g_post_below, name):
    s, d = dx1.shape

    def body(dx1_ref, dh_ref, x0_ref, g_ref, f_ref, gp_ref, dx0_ref, dg_ref, df_ref, dgp_ref):
        _acc_init([dg_ref, dgp_ref])
        x0 = x0_ref[...]
        da, dgt = _rms_bwd(x0, _rsq_mean(x0), g_ref[...], dh_ref[...])
        dx0 = dx1_ref[...] + da
        dx0_ref[...] = dx0
        dg_ref[...] += _colsum(dgt)
        fv = f_ref[...]
        db, dgt2 = _rms_bwd(fv, _rsq_mean(fv), gp_ref[...], dx0)
        df_ref[...] = db.astype(BF16)
        dgp_ref[...] += _colsum(dgt2)

    return pl.pallas_call(
        body, grid=(s // TR,),
        in_specs=[_row_spec(d), _row_spec(d), _row_spec(d), _vec_spec(d), _row_spec(d), _vec_spec(d)],
        out_specs=[_row_spec(d), _vec_spec(d), _row_spec(d), _vec_spec(d)],
        out_shape=[jax.ShapeDtypeStruct((s, d), F32), jax.ShapeDtypeStruct((1, d), F32),
                   jax.ShapeDtypeStruct((s, d), BF16), jax.ShapeDtypeStruct((1, d), F32)],
        compiler_params=_cparams("arbitrary"), name=name)(dx1, dh1, x0, g1, f_below, g_post_below)


def _norm_bwd_in(dx1, dh1, x0, g1, name):
    s, d = dx1.shape

    def body(dx1_ref, dh_ref, x0_ref, g_ref, dx0_ref, dg_ref):
        _acc_init([dg_ref])
        x0 = x0_ref[...]
        da, dgt = _rms_bwd(x0, _rsq_mean(x0), g_ref[...], dh_ref[...])
        dx0_ref[...] = dx1_ref[...] + da
        dg_ref[...] += _colsum(dgt)

    return pl.pallas_call(
        body, grid=(s // TR,), in_specs=[_row_spec(d), _row_spec(d), _row_spec(d), _vec_spec(d)],
        out_specs=[_row_spec(d), _vec_spec(d)],
        out_shape=[jax.ShapeDtypeStruct((s, d), F32), jax.ShapeDtypeStruct((1, d), F32)],
        compiler_params=_cparams("arbitrary"), name=name)(dx1, dh1, x0, g1)


def _tril_mask():
    row = lax.broadcasted_iota(jnp.int32, (CHUNK, CHUNK), 0)
    col = lax.broadcasted_iota(jnp.int32, (CHUNK, CHUNK), 1)
    return row >= col


def _gating_forward(pa, gv, bv, wt, bsf):
    er = lax.erf(pa * RSQRT2)
    za = 0.5 * pa * (1.0 + er)
    u = za[:, :A_WIDTH]
    va = za[:, A_WIDTH:]
    xc = va - jnp.mean(va, axis=-1, keepdims=True)
    rs = lax.rsqrt(jnp.mean(xc * xc, axis=-1, keepdims=True) + EPS)
    vn = xc * rs
    vlb = (vn * gv + bv).astype(BF16)
    sg = jnp.concatenate(
        [_dot(wt[g], vlb[:, g * GROUP_DIM:(g + 1) * GROUP_DIM], NN) for g in range(A_GROUPS)], axis=1) + bsf
    return er, u, rs, vn, vlb, sg


def _masked_ws(ws_ref):
    mask = _tril_mask()
    return [jnp.where(mask, ws_ref[g], 0.0).astype(BF16) for g in range(A_GROUPS)]


def _mixer_a_fwd(proj, gv, bv, ws, bsf, ga, name):
    s = proj.shape[0]

    def body(p_ref, gv_ref, bv_ref, ws_ref, bs_ref, ga_ref, o_ref):
        wt = _masked_ws(ws_ref)
        for ch in range(TR // CHUNK):
            rows = slice(ch * CHUNK, (ch + 1) * CHUNK)
            _, u, _, _, _, sg = _gating_forward(p_ref[rows, :].astype(F32), gv_ref[...], bv_ref[...], wt, bs_ref[...])
            oa = u * sg
            o_ref[rows, :] = (oa * _rsq_mean(oa) * ga_ref[...]).astype(BF16)

    return pl.pallas_call(
        body, grid=(s // TR,),
        in_specs=[_row_spec(2 * A_WIDTH), _vec_spec(A_WIDTH), _vec_spec(A_WIDTH),
                  pl.BlockSpec((A_GROUPS, CHUNK, CHUNK), lambda i: (0, 0, 0)),
                  pl.BlockSpec((CHUNK, A_WIDTH), lambda i: (0, 0)), _vec_spec(A_WIDTH)],
        out_specs=_row_spec(A_WIDTH), out_shape=jax.ShapeDtypeStruct((s, A_WIDTH + B_WIDTH), BF16),
        compiler_params=_cparams("parallel"), name=name)(proj, gv, bv, ws, bsf, ga)


def _mixer_a_bwd(proj, dmixed, gv, bv, ws, bsf, ga, name):
    s = proj.shape[0]
    nsteps = s // TR

    def body(p_ref, dm_ref, gv_ref, bv_ref, ws_ref, bs_ref, ga_ref,
             dp_ref, dga_ref, dgv_ref, dbv_ref, dbs_ref, dws_ref):
        _acc_init([dga_ref, dgv_ref, dbv_ref, dbs_ref, dws_ref])
        mask = _tril_mask()
        wt = _masked_ws(ws_ref)
        gvv = gv_ref[...]
        gav = ga_ref[...]
        for ch in range(TR // CHUNK):
            rows = slice(ch * CHUNK, (ch + 1) * CHUNK)
            pa = p_ref[rows, :].astype(F32)
            er, u, rs, vn, vlb, sg = _gating_forward(pa, gvv, bv_ref[...], wt, bs_ref[...])
            oa = u * sg
            doa, dgt = _rms_bwd(oa, _rsq_mean(oa), gav, dm_ref[rows, :])
            dga_ref[...] += _colsum(dgt)
            du = doa * sg
            dsg = doa * u
            dbs_ref[...] += dsg
            dsgb = dsg.astype(BF16)
            dvl = []
            for g in range(A_GROUPS):
                cols = slice(g * GROUP_DIM, (g + 1) * GROUP_DIM)
                dws_ref[g] += jnp.where(mask, _dot(dsgb[:, cols], vlb[:, cols], NT), 0.0)
                dvl.append(_dot(wt[g], dsgb[:, cols], TN))
            dvl = jnp.concatenate(dvl, axis=1)
            dgv_ref[...] += _colsum(dvl * vn)
            dbv_ref[...] += _colsum(dvl)
            dvn = dvl * gvv
            dva = rs * (dvn - jnp.mean(dvn, axis=-1, keepdims=True)
                        - vn * jnp.mean(dvn * vn, axis=-1, keepdims=True))
            gp = 0.5 * (1.0 + er) + pa * jnp.exp(-0.5 * pa * pa) * INV_SQRT_2PI
            dp_ref[rows, :] = (jnp.concatenate([du, dva], axis=1) * gp).astype(BF16)

        @pl.when(pl.program_id(0) == nsteps - 1)
        def _():
            for g in range(A_GROUPS):
                cols = slice(g * GROUP_DIM, (g + 1) * GROUP_DIM)
                tot = jnp.sum(dbs_ref[:, cols], axis=1, keepdims=True)
                dbs_ref[:, cols] = jnp.broadcast_to(tot, (CHUNK, GROUP_DIM))

    full = lambda *shape: pl.BlockSpec(shape, lambda i: (0,) * len(shape))
    return pl.pallas_call(
        body, grid=(nsteps,),
        in_specs=[_row_spec(2 * A_WIDTH), _row_spec(A_WIDTH), _vec_spec(A_WIDTH), _vec_spec(A_WIDTH),
                  full(A_GROUPS, CHUNK, CHUNK), full(CHUNK, A_WIDTH), _vec_spec(A_WIDTH)],
        out_specs=[_row_spec(2 * A_WIDTH), _vec_spec(A_WIDTH), _vec_spec(A_WIDTH), _vec_spec(A_WIDTH),
                   full(CHUNK, A_WIDTH), full(A_GROUPS, CHUNK, CHUNK)],
        out_shape=[jax.ShapeDtypeStruct((s, IN_COLS), BF16), jax.ShapeDtypeStruct((1, A_WIDTH), F32),
                   jax.ShapeDtypeStruct((1, A_WIDTH), F32), jax.ShapeDtypeStruct((1, A_WIDTH), F32),
                   jax.ShapeDtypeStruct((CHUNK, A_WIDTH), F32),
                   jax.ShapeDtypeStruct((A_GROUPS, CHUNK, CHUNK), F32)],
        compiler_params=_cparams("arbitrary"), name=name)(proj, dmixed, gv, bv, ws, bsf, ga)


def _rope_tables(s):
    half = ROT_DIM // 2
    lane = jnp.arange(LANES) % HEAD_DIM
    inv = ROPE_THETA ** (-(2 * (lane % half)).astype(F32) / ROT_DIM)
    ang = jnp.arange(s, dtype=F32)[:, None] * inv[None, :]
    cos, sin = jnp.cos(ang), jnp.sin(ang)
    c = jnp.where(lane < ROT_DIM, cos, 1.0)
    s1 = jnp.where(lane < half, -sin, 0.0)
    s2 = jnp.where((lane >= half) & (lane < ROT_DIM), sin, 0.0)
    return c, s1, s2


def _lane_blocks(width):
    return [slice(b * LANES, (b + 1) * LANES) for b in range(width // LANES)]


CLASS_DILS = tuple(d for d in DILATIONS if d > 1)


def _class_shape(s, dil, dtype):
    return jax.ShapeDtypeStruct((dil, s // dil, B_WIDTH), dtype)


def _class_spec(dil):
    return pl.BlockSpec((dil, TR // dil, B_WIDTH), lambda i, *_: (0, i, 0))


NBLK = B_WIDTH // LANES
STAGE = pltpu.VMEM((NBLK, TR, LANES), F32)


def _stage_put(stage, value):
    for b, sl in enumerate(_lane_blocks(B_WIDTH)):
        stage[b] = value[:, sl]


def _stage_get(stage):
    return jnp.concatenate([stage[b] for b in range(NBLK)], axis=1)


def _store_classes(stage, dst_ref, dil):
    for b, sl in enumerate(_lane_blocks(B_WIDTH)):
        for r in range(dil):
            dst_ref[r, :, sl] = stage[b, pl.ds(r, TR // dil, stride=dil), :].astype(dst_ref.dtype)


def _load_classes(src_ref, stage, dil):
    for b, sl in enumerate(_lane_blocks(B_WIDTH)):
        for r in range(dil):
            stage[b, pl.ds(r, TR // dil, stride=dil), :] = src_ref[r, :, sl].astype(F32)
    return _stage_get(stage)


def _rope_fwd(proj, tabs, name):
    s = proj.shape[0]
    half = ROT_DIM // 2
    scale = HEAD_DIM ** -0.5
    nlay = 1 + len(CLASS_DILS)

    def body(q_ref, k_ref, v_ref, c_ref, s1_ref, s2_ref, *rest):
        outs, stage = rest[:3 * nlay], rest[3 * nlay]
        c, s1, s2 = c_ref[...], s1_ref[...], s2_ref[...]
        for which, (src, mul) in enumerate(((q_ref, scale), (k_ref, 1.0), (v_ref, None))):
            if mul is None:
                _stage_put(stage, src[...].astype(F32))
            else:
                for b, sl in enumerate(_lane_blocks(B_WIDTH)):
                    a = src[:, sl].astype(F32)
                    r = a * c + pltpu.roll(a, LANES - half, 1) * s1 + pltpu.roll(a, half, 1) * s2
                    stage[b] = r * mul
            dst = outs[which * nlay:(which + 1) * nlay]
            dst[0][...] = _stage_get(stage).astype(BF16)
            for ref, d in zip(dst[1:], CLASS_DILS):
                _store_classes(stage, ref, d)

    tab = pl.BlockSpec((TR, LANES), lambda i: (i, 0))
    lay_specs = [_row_spec(B_WIDTH)] + [_class_spec(d) for d in CLASS_DILS]
    lay_shapes = [jax.ShapeDtypeStruct((s, B_WIDTH), BF16)] + [_class_shape(s, d, BF16) for d in CLASS_DILS]
    outs = pl.pallas_call(
        body, grid=(s // TR,),
        in_specs=[_row_spec(B_WIDTH, 2), _row_spec(B_WIDTH, 3), _row_spec(B_WIDTH, 4), tab, tab, tab],
        out_specs=lay_specs * 3, out_shape=lay_shapes * 3, scratch_shapes=[STAGE],
        compiler_params=_cparams("parallel"), name=name)(proj, proj, proj, *tabs)
    q, k, v = (dict(zip(DILATIONS, outs[w * nlay:(w + 1) * nlay])) for w in range(3))
    return q, k, v


def _as_classes(t):
    return t if t.ndim == 3 else t[None]


def _band_mask(i):
    qi = lax.broadcasted_iota(jnp.int32, (BAND, 2 * BAND), 0)
    kj = lax.broadcasted_iota(jnp.int32, (BAND, 2 * BAND), 1)
    return (kj >= qi) & (kj <= qi + BAND) & ((kj >= BAND) | (i > 0))


def _head_masks():
    lane = lax.broadcasted_iota(jnp.int32, (1, LANES), 1)
    return lane < HEAD_DIM, lane >= HEAD_DIM


def _stack_heads(t):
    lo, hi = _head_masks()
    zero = jnp.zeros_like(t)
    return jnp.concatenate([jnp.where(lo, t, zero), jnp.where(hi, t, zero)], axis=0)


def _attn_specs(last):
    cur = pl.BlockSpec((None, BAND, B_WIDTH), lambda r, i: (r, jnp.minimum(i, last), 0))
    prev = pl.BlockSpec((None, BAND, B_WIDTH), lambda r, i: (r, jnp.maximum(jnp.minimum(i, last) - 1, 0), 0))
    return cur, prev


def _attn_fwd(q, k, v, name, gather=None):
    dil, n, _ = q.shape
    nb = n // BAND
    ng = 0 if gather is None else len(gather)

    def body(*refs):
        q_ref, kc_ref, kp_ref, vc_ref, vp_ref = refs[:5]
        o_ref, l_ref = refs[5 + ng:7 + ng]
        if ng:
            start, relay, finish = _gather_steps(refs[5:5 + ng], refs[7 + ng:7 + 2 * ng], *refs[7 + 2 * ng:])
            first, last = _grid_edges((dil, nb))
            pl.when(first)(start)
        valid = _band_mask(pl.program_id(1))
        valid = jnp.concatenate([valid, valid], axis=0)
        lo, _ = _head_masks()
        for sl in _lane_blocks(B_WIDTH):
            kk = jnp.concatenate([kp_ref[:, sl], kc_ref[:, sl]], axis=0)
            vv = jnp.concatenate([vp_ref[:, sl], vc_ref[:, sl]], axis=0)
            sc = jnp.where(valid, _dot(_stack_heads(q_ref[:, sl]), kk, NT), NEG_INF)
            mx = jnp.max(sc, axis=1, keepdims=True)
            p = jnp.exp(sc - mx)
            den = jnp.sum(p, axis=1, keepdims=True)
            out = _dot(p.astype(BF16), vv, NN) / den
            lse = mx + jnp.log(den)
            o_ref[:, sl] = jnp.where(lo, out[:BAND], out[BAND:]).astype(BF16)
            l_ref[:, sl] = jnp.where(lo, lse[:BAND], lse[BAND:])

        if ng:
            @pl.when(last)
            def _():
                relay()
                finish()

    cur, prev = _attn_specs(nb - 1)
    sem = ("arbitrary", "arbitrary") if ng else ("parallel", "parallel")
    res = pl.pallas_call(
        body, grid=(dil, nb), in_specs=[cur, cur, prev, cur, prev] + [ANY] * ng, out_specs=[cur, cur] + [ANY] * ng,
        out_shape=[jax.ShapeDtypeStruct((dil, n, B_WIDTH), BF16), jax.ShapeDtypeStruct((dil, n, B_WIDTH), F32)]
        + _gathered_shapes(gather or []),
        scratch_shapes=_gather_sems(ng) if ng else [],
        compiler_params=_cparams(*sem), name=name)(q, k, k, v, v, *(gather or []))
    return res[0], res[1], list(res[2:])


MAX_CLASS_BLOCKS = 8


def _class_masks():
    qi = lax.broadcasted_iota(jnp.int32, (BAND, 2 * BAND), 0)
    kj = lax.broadcasted_iota(jnp.int32, (BAND, 2 * BAND), 1)
    both = (kj >= qi) & (kj <= qi + BAND)
    own = kj[:, :BAND] <= qi[:, :BAND]
    return jnp.concatenate([own, own], axis=0), jnp.concatenate([both, both], axis=0)


def _block_rows(g):
    return pl.ds(pl.multiple_of(g * BAND, BAND), BAND)


def _key_rows(g):
    return pl.ds(pl.multiple_of((g - 1) * BAND, BAND), 2 * BAND)


def _attn_fwd_class(q, k, v, name, gather=None):
    dil, n, _ = q.shape
    nb = n // BAND
    ng = 0 if gather is None else len(gather)

    def body(*refs):
        q_ref, k_ref, v_ref = refs[:3]
        o_ref, l_ref = refs[3 + ng:5 + ng]
        if ng:
            start, relay, finish = _gather_steps(refs[3:3 + ng], refs[5 + ng:5 + 2 * ng], *refs[5 + 2 * ng:])
            first, last = _grid_edges((dil,))
            pl.when(first)(start)
        own, both = _class_masks()
        lo, _ = _head_masks()

        def block(rows, keys, valid):
            for sl in _lane_blocks(B_WIDTH):
                sc = jnp.where(valid, _dot(_stack_heads(q_ref[rows, sl]), k_ref[keys, sl], NT), NEG_INF)
                mx = jnp.max(sc, axis=1, keepdims=True)
                p = jnp.exp(sc - mx)
                den = jnp.sum(p, axis=1, keepdims=True)
                out = _dot(p.astype(BF16), v_ref[keys, sl], NN) / den
                lse = mx + jnp.log(den)
                o_ref[rows, sl] = jnp.where(lo, out[:BAND], out[BAND:]).astype(BF16)
                l_ref[rows, sl] = jnp.where(lo, lse[:BAND], lse[BAND:])

        block(_block_rows(0), _block_rows(0), own)

        @pl.loop(1, nb)
        def _(g):
            block(_block_rows(g), _key_rows(g), both)

        if ng:
            @pl.when(last)
            def _():
                relay()
                finish()

    spec = pl.BlockSpec((None, n, B_WIDTH), lambda r: (r, 0, 0))
    res = pl.pallas_call(
        body, grid=(dil,), in_specs=[spec] * 3 + [ANY] * ng, out_specs=[spec, spec] + [ANY] * ng,
        out_shape=[jax.ShapeDtypeStruct((dil, n, B_WIDTH), BF16), jax.ShapeDtypeStruct((dil, n, B_WIDTH), F32)]
        + _gathered_shapes(gather or []),
        scratch_shapes=_gather_sems(ng) if ng else [],
        compiler_params=_cparams("arbitrary" if ng else "parallel"), name=name)(q, k, v, *(gather or []))
    return res[0], res[1], list(res[2:])


def _attn_bwd_class(q, k, v, do, lse, delta, name, scatter=None):
    dil, n, _ = q.shape
    nb = n // BAND
    ns = 0 if scatter is None else len(scatter[0])

    def body(*refs):
        q_ref, k_ref, v_ref, do_ref, lse_ref, dl_ref = refs[:6]
        dq_ref, dk_ref, dv_ref = refs[6 + ns:9 + ns]
        ck_ref, cv_ref = refs[9 + 2 * ns:11 + 2 * ns]
        if ns:
            start, finish = _scatter_steps(refs[6:6 + ns], refs[9 + ns:9 + 2 * ns], *refs[11 + 2 * ns:], scatter[1])
            first, last = _grid_edges((dil,))
            pl.when(first)(start)
        own, both = _class_masks()
        lo, _ = _head_masks()
        lane = lax.broadcasted_iota(jnp.int32, (1, LANES), 1)

        def per_head(t):
            return jnp.concatenate(
                [jnp.sum(jnp.where(lane == first, t, 0.0), axis=1, keepdims=True) for first in (0, HEAD_DIM)], axis=0)

        def grads(rows, keys, valid, sl):
            q2 = _stack_heads(q_ref[rows, sl])
            do2 = _stack_heads(do_ref[rows, sl])
            kk = k_ref[keys, sl]
            p = jnp.where(valid, jnp.exp(_dot(q2, kk, NT) - per_head(lse_ref[rows, sl])), 0.0)
            ds = (p * (_dot(do2, v_ref[keys, sl], NT) - per_head(dl_ref[rows, sl]))).astype(BF16)
            dq = _dot(ds, kk, NN)
            dq_ref[rows, sl] = jnp.where(lo, dq[:BAND], dq[BAND:]).astype(BF16)
            return _dot(ds, q2, TN), _dot(p.astype(BF16), do2, TN)

        for sl in _lane_blocks(B_WIDTH):
            ck_ref[:, sl], cv_ref[:, sl] = grads(_block_rows(0), _block_rows(0), own, sl)

        @pl.loop(1, nb)
        def _(g):
            before = _block_rows(g - 1)
            for sl in _lane_blocks(B_WIDTH):
                dkk, dvv = grads(_block_rows(g), _key_rows(g), both, sl)
                dk_ref[before, sl] = (ck_ref[:, sl] + dkk[:BAND]).astype(BF16)
                dv_ref[before, sl] = (cv_ref[:, sl] + dvv[:BAND]).astype(BF16)
                ck_ref[:, sl] = dkk[BAND:]
                cv_ref[:, sl] = dvv[BAND:]

        final = pl.ds((nb - 1) * BAND, BAND)
        dk_ref[final, :] = ck_ref[...].astype(BF16)
        dv_ref[final, :] = cv_ref[...].astype(BF16)

        if ns:
            pl.when(last)(finish)

    spec = pl.BlockSpec((None, n, B_WIDTH), lambda r: (r, 0, 0))
    shape = jax.ShapeDtypeStruct((dil, n, B_WIDTH), BF16)
    res = pl.pallas_call(
        body, grid=(dil,), in_specs=[spec] * 6 + [ANY] * ns, out_specs=[spec] * 3 + [ANY] * ns,
        out_shape=[shape] * 3 + (_scattered_shapes(scatter[1]) if ns else []),
        scratch_shapes=[pltpu.VMEM((BAND, B_WIDTH), F32)] * 2 + (_scatter_sems(ns) if ns else []),
        compiler_params=_cparams("arbitrary" if ns else "parallel"), name=name)(q, k, v, do, lse, delta,
                                                                             *(scatter[0] if ns else []))
    return res[0], res[1], res[2], list(res[3:])


def _attn_combine(outs, lses, gb, mixed, name):
    s = mixed.shape[0]
    npat = len(DILATIONS)
    w = B_WIDTH

    def body(*refs):
        o_refs, l_refs = refs[:npat], refs[npat:2 * npat]
        g_ref, _, ob_ref = refs[2 * npat:2 * npat + 3]
        lse_refs = refs[2 * npat + 3:3 * npat + 3]
        mb_ref, stage = refs[3 * npat + 3:]
        os_ = [o_refs[0][...].astype(F32)] + [_load_classes(r, stage, d) for r, d in zip(o_refs[1:], CLASS_DILS)]
        ls = [l_refs[0][...]] + [_load_classes(r, stage, d) for r, d in zip(l_refs[1:], CLASS_DILS)]
        mx = functools.reduce(jnp.maximum, ls)
        ws = [jnp.exp(l - mx) for l in ls]
        tot = functools.reduce(lambda a, b: a + b, ws)
        ob = functools.reduce(lambda a, b: a + b, [wt / tot * o for wt, o in zip(ws, os_)])
        ob_ref[...] = ob
        lse = mx + jnp.log(tot)
        _stage_put(stage, lse)
        lse_refs[0][...] = lse
        for ref, d in zip(lse_refs[1:], CLASS_DILS):
            _store_classes(stage, ref, d)
        mb_ref[...] = (ob * _rsq_mean(ob) * g_ref[...]).astype(BF16)

    lay_specs = [_row_spec(w)] + [_class_spec(d) for d in CLASS_DILS]
    res = pl.pallas_call(
        body, grid=(s // TR,), in_specs=lay_specs * 2 + [_vec_spec(w), ANY],
        out_specs=[_row_spec(w)] + lay_specs + [_row_spec(w, 1)],
        out_shape=[jax.ShapeDtypeStruct((s, w), F32), jax.ShapeDtypeStruct((s, w), F32)]
        + [_class_shape(s, d, F32) for d in CLASS_DILS] + [jax.ShapeDtypeStruct(mixed.shape, mixed.dtype)],
        scratch_shapes=[STAGE], input_output_aliases={2 * npat + 1: npat + 1},
        compiler_params=_cparams("parallel"), name=name)(*outs, *lses, gb, mixed)
    return res[0], dict(zip(DILATIONS, res[1:npat + 1])), res[npat + 1]


def _attn_bwd_prep(dmixed, ob, gb, name):
    s = ob.shape[0]
    w = B_WIDTH
    nlay = len(DILATIONS)

    def body(dm_ref, ob_ref, g_ref, *rest):
        do_refs, dl_refs = rest[:nlay], rest[nlay:2 * nlay]
        dg_ref, stage = rest[2 * nlay:]
        _acc_init([dg_ref])
        ob = ob_ref[...]
        dob, dgt = _rms_bwd(ob, _rsq_mean(ob), g_ref[...], dm_ref[...])
        dg_ref[...] += _colsum(dgt)
        _stage_put(stage, dob)
        do_refs[0][...] = dob.astype(BF16)
        for ref, d in zip(do_refs[1:], CLASS_DILS):
            _store_classes(stage, ref, d)
        lo, hi = _head_masks()
        t = dob * ob
        for b, sl in enumerate(_lane_blocks(w)):
            tb = t[:, sl]
            s0 = jnp.sum(jnp.where(lo, tb, 0.0), axis=1, keepdims=True)
            s1 = jnp.sum(jnp.where(hi, tb, 0.0), axis=1, keepdims=True)
            stage[b] = jnp.where(lo, s0, s1)
        dl_refs[0][...] = _stage_get(stage)
        for ref, d in zip(dl_refs[1:], CLASS_DILS):
            _store_classes(stage, ref, d)

    lay_specs = [_row_spec(w)] + [_class_spec(d) for d in CLASS_DILS]
    shapes = lambda dt: [jax.ShapeDtypeStruct((s, w), dt)] + [_class_shape(s, d, dt) for d in CLASS_DILS]
    res = pl.pallas_call(
        body, grid=(s // TR,), in_specs=[_row_spec(w, 1), _row_spec(w), _vec_spec(w)],
        out_specs=lay_specs * 2 + [_vec_spec(w)],
        out_shape=shapes(BF16) + shapes(F32) + [jax.ShapeDtypeStruct((1, w), F32)],
        scratch_shapes=[STAGE],
        compiler_params=_cparams("arbitrary"), name=name)(dmixed, ob, gb)
    return dict(zip(DILATIONS, res[:nlay])), dict(zip(DILATIONS, res[nlay:2 * nlay])), res[2 * nlay]


def _attn_bwd(q, k, v, do, lse, delta, name, scatter=None):
    dil, n, _ = q.shape
    nb = n // BAND
    ns = 0 if scatter is None else len(scatter[0])

    def body(*refs):
        q_ref, kc_ref, kp_ref, vc_ref, vp_ref, do_ref, lse_ref, dl_ref = refs[:8]
        dq_ref, dk_ref, dv_ref = refs[8 + ns:11 + ns]
        ck_ref, cv_ref = refs[11 + 2 * ns:13 + 2 * ns]
        i = pl.program_id(1)
        if ns:
            start, finish = _scatter_steps(refs[8:8 + ns], refs[11 + ns:11 + 2 * ns], *refs[13 + 2 * ns:],
                                           scatter[1])
            first, last = _grid_edges((dil, nb + 1))
            pl.when(first)(start)

        @pl.when(i == 0)
        def _():
            ck_ref[...] = jnp.zeros_like(ck_ref)
            cv_ref[...] = jnp.zeros_like(cv_ref)

        @pl.when(i < nb)
        def _():
            valid = _band_mask(i)
            valid = jnp.concatenate([valid, valid], axis=0)
            lo, _ = _head_masks()
            lane = lax.broadcasted_iota(jnp.int32, (1, LANES), 1)

            def per_head(t):
                return jnp.concatenate(
                    [jnp.sum(jnp.where(lane == first, t, 0.0), axis=1, keepdims=True) for first in (0, HEAD_DIM)], axis=0)

            for sl in _lane_blocks(B_WIDTH):
                q2 = _stack_heads(q_ref[:, sl])
                do2 = _stack_heads(do_ref[:, sl])
                kk = jnp.concatenate([kp_ref[:, sl], kc_ref[:, sl]], axis=0)
                vv = jnp.concatenate([vp_ref[:, sl], vc_ref[:, sl]], axis=0)
                p = jnp.where(valid, jnp.exp(_dot(q2, kk, NT) - per_head(lse_ref[:, sl])), 0.0)
                ds = (p * (_dot(do2, vv, NT) - per_head(dl_ref[:, sl]))).astype(BF16)
                dq = _dot(ds, kk, NN)
                dkk = _dot(ds, q2, TN)
                dvv = _dot(p.astype(BF16), do2, TN)
                dq_ref[:, sl] = jnp.where(lo, dq[:BAND], dq[BAND:]).astype(BF16)
                dk_ref[:, sl] = (ck_ref[:, sl] + dkk[:BAND]).astype(BF16)
                dv_ref[:, sl] = (cv_ref[:, sl] + dvv[:BAND]).astype(BF16)
                ck_ref[:, sl] = dkk[BAND:]
                cv_ref[:, sl] = dvv[BAND:]

        @pl.when(i == nb)
        def _():
            dk_ref[...] = ck_ref[...].astype(BF16)
            dv_ref[...] = cv_ref[...].astype(BF16)

        if ns:
            pl.when(last)(finish)

    cur, prev = _attn_specs(nb - 1)
    lag = pl.BlockSpec((None, BAND, B_WIDTH), lambda r, i: (r, jnp.maximum(i - 1, 0), 0))
    shape = jax.ShapeDtypeStruct((dil, n, B_WIDTH), BF16)
    res = pl.pallas_call(
        body, grid=(dil, nb + 1), in_specs=[cur, cur, prev, cur, prev, cur, cur, cur] + [ANY] * ns,
        out_specs=[cur, lag, lag] + [ANY] * ns,
        out_shape=[shape] * 3 + (_scattered_shapes(scatter[1]) if ns else []),
        scratch_shapes=[pltpu.VMEM((BAND, B_WIDTH), F32)] * 2 + (_scatter_sems(ns) if ns else []),
        compiler_params=_cparams("arbitrary", "arbitrary"), name=name)(q, k, k, v, v, do, lse, delta,
                                                                      *(scatter[0] if ns else []))
    return res[0], res[1], res[2], list(res[3:])


def _rope_bwd(dqs, dks, dvs, tabs, dproj, name):
    s = dproj.shape[0]
    half = ROT_DIM // 2
    scale = HEAD_DIM ** -0.5
    npat = len(DILATIONS)
    w = B_WIDTH

    def body(*refs):
        groups = [refs[g * npat:(g + 1) * npat] for g in range(3)]
        c_ref, s1_ref, s2_ref, _, o_ref, stage = refs[3 * npat:]

        def total(rs):
            acc = rs[0][...].astype(F32)
            for ref, d in zip(rs[1:], CLASS_DILS):
                acc = acc + _load_classes(ref, stage, d)
            return acc

        def unrope(g):
            c, s1, s2 = c_ref[...], s1_ref[...], s2_ref[...]
            for sl in _lane_blocks(w):
                gb = g[:, sl]
                o = gb * c + pltpu.roll(gb * s1, half, 1) + pltpu.roll(gb * s2, LANES - half, 1)
                o_ref[:, sl] = o.astype(BF16)

        which = pl.program_id(1)

        @pl.when(which == 0)
        def _():
            unrope(total(groups[0]) * scale)

        @pl.when(which == 1)
        def _():
            unrope(total(groups[1]))

        @pl.when(which == 2)
        def _():
            o_ref[...] = total(groups[2]).astype(BF16)

    tab = pl.BlockSpec((TR, LANES), lambda i, j: (i, 0))
    nat = pl.BlockSpec((TR, w), lambda i, j: (i, 0))
    lay_specs = [nat] + [_class_spec(d) for d in CLASS_DILS]
    first_col = 2 * A_WIDTH // w
    return pl.pallas_call(
        body, grid=(s // TR, 3), in_specs=lay_specs * 3 + [tab] * 3 + [ANY],
        out_specs=pl.BlockSpec((TR, w), lambda i, j: (i, first_col + j)),
        out_shape=jax.ShapeDtypeStruct(dproj.shape, dproj.dtype), scratch_shapes=[STAGE],
        input_output_aliases={3 * npat + 3: 0},
        compiler_params=_cparams("parallel", "arbitrary"), name=name)(*dqs, *dks, *dvs, *tabs, dproj)


TK = 512
HALO = 16


def _row_of(v, r):
    rows = lax.broadcasted_iota(jnp.int32, (v.shape[0], 1), 0)
    return jnp.sum(jnp.where(rows == r, v, 0.0), axis=0, keepdims=True)


def _taps_before(x, halo):
    row = lax.broadcasted_iota(jnp.int32, (x.shape[0], 1), 0)
    m1 = jnp.where(row == 0, _row_of(halo, HALO - 1), pltpu.roll(x, 1, 0))
    m2 = jnp.where(row == 0, _row_of(halo, HALO - 2), jnp.where(row == 1, _row_of(halo, HALO - 1), pltpu.roll(x, 2, 0)))
    return m2, m1, x


def _taps_after(x, halo):
    rows = x.shape[0]
    row = lax.broadcasted_iota(jnp.int32, (rows, 1), 0)
    p1 = jnp.where(row == rows - 1, _row_of(halo, 0), pltpu.roll(x, rows - 1, 0))
    p2 = jnp.where(row == rows - 2, _row_of(halo, 0), jnp.where(row == rows - 1, _row_of(halo, 1), pltpu.roll(x, rows - 2, 0)))
    return p1, p2


def _conv_value(taps, cw_ref, cb_ref, h):
    return cb_ref[h] + cw_ref[h, 0:1, :] * taps[0] + cw_ref[h, 1:2, :] * taps[1] + cw_ref[h, 2:3, :] * taps[2]


def _ffn_weight_specs(ncol):
    per_up = (2 * D_FF // N_CHIPS) // TK
    per_dn = (D_FF // N_CHIPS) // TK
    wg = pl.BlockSpec((None, None, D_MODEL, TK), lambda i, j: (j // per_up, 0, 0, j % per_up))
    wv = pl.BlockSpec((None, None, D_MODEL, TK), lambda i, j: ((j + ncol) // per_up, 0, 0, (j + ncol) % per_up))
    wd = pl.BlockSpec((None, None, TK, D_MODEL), lambda i, j: (j // per_dn, 0, j % per_dn, 0))
    cw = pl.BlockSpec((2, 3, TK), lambda i, j: (0, 0, j))
    cb = pl.BlockSpec((2, 1, TK), lambda i, j: (0, 0, j))
    return wg, wv, wd, cw, cb


def _ffn_forward(h2, w_up, w_down, cw3, cb3, name, gather=None, post=None):
    s = h2.shape[0]
    nm, ncol = s // TM, D_FF // TK
    ng = 0 if gather is None else len(gather)
    npost = 0 if post is None else 3
    nout = 4 + (2 if post else 0)

    def body(*refs):
        h_ref, wg_ref, wv_ref, wd_ref, cw_ref, cb_ref = refs[:6]
        post_in = refs[6:6 + npost]
        g_in = refs[6 + npost:6 + npost + ng]
        outs = refs[6 + npost + ng:6 + npost + ng + nout]
        y_ref, up_ref, cv_ref, f_ref = outs[:4]
        g_out = refs[6 + npost + ng + nout:6 + npost + 2 * ng + nout]
        carry, acc = refs[6 + npost + 2 * ng + nout:8 + npost + 2 * ng + nout]
        i, j = pl.program_id(0), pl.program_id(1)
        if ng:
            start, relay, finish = _gather_steps(g_in, g_out, *refs[8 + npost + 2 * ng + nout:])
            pl.when((i == 0) & (j == 0))(start)
            pl.when((i == nm - 1) & (j == 0))(relay)

        @pl.when((i == 0) & (j == 0))
        def _():
            carry[...] = jnp.zeros_like(carry)

        h = h_ref[...]
        conv = []
        for hh, w_ref in ((0, wg_ref), (1, wv_ref)):
            up = _dot(h, w_ref[...], NN).astype(BF16)
            up_ref[hh] = up
            x = up.astype(F32)
            conv.append(_conv_value(_taps_before(x, carry[j, hh]), cw_ref, cb_ref, hh))
            cv_ref[hh] = conv[hh].astype(BF16)
            carry[j, hh] = x[TM - HALO:, :]
        y = (_gelu_tanh(conv[0])[0] * conv[1]).astype(BF16)
        y_ref[...] = y
        part = _dot(y, wd_ref[...], NN)

        @pl.when(j == 0)
        def _():
            acc[...] = part

        @pl.when(j > 0)
        def _():
            acc[...] += part

        @pl.when(j == ncol - 1)
        def _():
            f = acc[...]
            f_ref[...] = f
            if post:
                x1_ref, gp_ref, gn_ref = post_in
                x2 = x1_ref[...] + f * _rsq_mean(f) * gp_ref[...]
                outs[4][...] = x2
                outs[5][...] = (x2 * _rsq_mean(x2) * gn_ref[...]).astype(BF16)

        if ng:
            pl.when((i == nm - 1) & (j == ncol - 1))(finish)

    wg, wv, wd, cw, cb = _ffn_weight_specs(ncol)
    row = pl.BlockSpec((TM, D_MODEL), lambda i, j: (i, 0))
    vec = pl.BlockSpec((1, D_MODEL), lambda i, j: (0, 0))
    res = pl.pallas_call(
        body, grid=(nm, ncol),
        in_specs=[row, wg, wv, wd, cw, cb] + ([row, vec, vec] if post else []) + [ANY] * ng,
        out_specs=[pl.BlockSpec((TM, TK), lambda i, j: (i, j)), pl.BlockSpec((2, TM, TK), lambda i, j: (0, i, j)),
                   pl.BlockSpec((2, TM, TK), lambda i, j: (0, i, j)), row] + ([row, row] if post else [])
        + [ANY] * ng,
        out_shape=[jax.ShapeDtypeStruct((s, D_FF), BF16), jax.ShapeDtypeStruct((2, s, D_FF), BF16),
                   jax.ShapeDtypeStruct((2, s, D_FF), BF16), jax.ShapeDtypeStruct((s, D_MODEL), F32)]
        + ([jax.ShapeDtypeStruct((s, D_MODEL), F32), jax.ShapeDtypeStruct((s, D_MODEL), BF16)] if post else [])
        + _gathered_shapes(gather or []),
        scratch_shapes=[pltpu.VMEM((ncol, 2, HALO, TK), F32), pltpu.VMEM((TM, D_MODEL), F32)]
        + (_gather_sems(ng) if ng else []),
        compiler_params=_cparams("arbitrary", "arbitrary"), name=name)(h2, w_up, w_up, w_down, cw3, cb3,
                                                                      *(post or []), *(gather or []))
    return res[:nout], list(res[nout:])


def _ffn_backward(df, w_up, w_down, up3, cv3, cw3, name, scatter=None):
    s = df.shape[0]
    nm, ncol = s // TM, D_FF // TK
    ns = 0 if scatter is None else len(scatter[0])

    def body(*refs):
        df_ref, wg_ref, wv_ref, wd_ref, cw_ref, up_ref, cv_ref = refs[:7]
        s_in = refs[7:7 + ns]
        dup_ref, dh_ref, sums_ref = refs[7 + ns:10 + ns]
        s_out = refs[10 + ns:10 + 2 * ns]
        carry, acc = refs[10 + 2 * ns:12 + 2 * ns]
        i, j = pl.program_id(0), pl.program_id(1)
        if ns:
            start, finish = _scatter_steps(s_in, s_out, *refs[12 + 2 * ns:], scatter[1])
            pl.when((i == 0) & (j == 0))(start)

        @pl.when((i == 0) & (j == 0))
        def _():
            carry[...] = jnp.zeros_like(carry)
            sums_ref[...] = jnp.zeros_like(sums_ref)

        dy = _dot(df_ref[...], wd_ref[...], NT)
        act, grad = _gelu_tanh(cv_ref[0].astype(F32))
        dcs = (dy * cv_ref[1].astype(F32) * grad, dy * act)
        row = lax.broadcasted_iota(jnp.int32, (8, 1), 0)
        part = None
        for hh, w_ref in ((0, wg_ref), (1, wv_ref)):
            dc = dcs[hh]
            x = up_ref[hh].astype(F32)
            after1, after2 = _taps_after(dc, carry[j, hh])
            upd = jnp.zeros((8, TK), F32)
            for ridx, sm in enumerate((_colsum(after2 * x), _colsum(after1 * x), _colsum(dc * x), _colsum(dc))):
                upd = jnp.where(row == ridx, sm, upd)
            sums_ref[j, hh] += upd
            dup = (cw_ref[hh, 2:3, :] * dc + cw_ref[hh, 1:2, :] * after1 + cw_ref[hh, 0:1, :] * after2).astype(BF16)
            carry[j, hh] = dc[:HALO, :]
            dup_ref[hh] = dup
            d = _dot(dup, w_ref[...], NT)
            part = d if part is None else part + d

        @pl.when(j == 0)
        def _():
            acc[...] = part

        @pl.when(j > 0)
        def _():
            acc[...] += part

        @pl.when(j == ncol - 1)
        def _():
            dh_ref[...] = acc[...]

        if ns:
            pl.when((i == nm - 1) & (j == ncol - 1))(finish)

    wg, wv, wd, cw, _ = _ffn_weight_specs(ncol)
    rev = lambda i: nm - 1 - i
    res = pl.pallas_call(
        body, grid=(nm, ncol),
        in_specs=[pl.BlockSpec((TM, D_MODEL), lambda i, j: (rev(i), 0)), wg, wv, wd, cw,
                  pl.BlockSpec((2, TM, TK), lambda i, j: (0, rev(i), j)),
                  pl.BlockSpec((2, TM, TK), lambda i, j: (0, rev(i), j))] + [ANY] * ns,
        out_specs=[pl.BlockSpec((2, TM, TK), lambda i, j: (0, rev(i), j)),
                   pl.BlockSpec((TM, D_MODEL), lambda i, j: (rev(i), 0)),
                   pl.BlockSpec((ncol, 2, 8, TK), lambda i, j: (0, 0, 0, 0))] + [ANY] * ns,
        out_shape=[jax.ShapeDtypeStruct((2, s, D_FF), BF16), jax.ShapeDtypeStruct((s, D_MODEL), F32),
                   jax.ShapeDtypeStruct((ncol, 2, 8, TK), F32)] + (_scattered_shapes(scatter[1]) if ns else []),
        scratch_shapes=[pltpu.VMEM((ncol, 2, HALO, TK), F32), pltpu.VMEM((TM, D_MODEL), F32)]
        + (_scatter_sems(ns) if ns else []),
        compiler_params=_cparams("arbitrary", "arbitrary"), name=name)(df, w_up, w_up, w_down, cw3, up3, cv3,
                                                                      *(scatter[0] if ns else []))
    return res[:3], list(res[3:])


def _wspec(rows, cols, index_map):
    return pl.BlockSpec((None, None, rows, cols), index_map)


def _layer_forward(l, x0, h1, p, wg, tabs, gather=None, late=None, g_next=None):
    s = x0.shape[0]
    nm = s // TMM
    tag = f"_l{l}"
    riders = dict.fromkeys(DILATIONS)
    proj_rider = None
    if late is not None:
        halves = lambda t: (t[:, :t.shape[1] // 2], t[:, t.shape[1] // 2:])
        (down_a, down_b), (up_a, up_b) = halves(late["w_down"]), halves(late["w_up"])
        proj_rider = [late["w_out"], down_a]
        riders = dict(zip(DILATIONS, ([down_b], [up_a], [up_b])))
    proj = _matmul(
        h1, wg["w_in"], grid=(nm, N_CHIPS), a_spec=pl.BlockSpec((TMM, D_MODEL), lambda i, j: (i, 0)),
        b_spec=_wspec(D_MODEL, IN_COLS // N_CHIPS, lambda i, j: (j, 0, 0, 0)),
        o_spec=pl.BlockSpec((TMM, IN_COLS // N_CHIPS), lambda i, j: (i, j)), o_shape=(s, IN_COLS), o_dtype=BF16,
        dims=NN, nk=1, kaxis=None, acc_shape=None, name="proj" + tag, gather=proj_rider)
    if late is not None:
        proj, (w_out_all4, down_a) = proj
    ma = _mixer_a_fwd(proj, p["v_norm_g"], p["v_norm_b"], p["w_spatial"], p["bs_full"], p["out_norm_a"],
                      "mixer_a_fwd" + tag)
    q, k, v = _rope_fwd(proj, tabs, "rope_fwd" + tag)
    whole = lambda d: s // d // BAND <= MAX_CLASS_BLOCKS
    outs, lses, landed = zip(*[
        (_attn_fwd_class if whole(d) else _attn_fwd)(
            _as_classes(q[d]), _as_classes(k[d]), _as_classes(v[d]), f"attn_fwd_d{d}" + tag, riders[d])
        for d in DILATIONS])
    if late is not None:
        wg = dict(wg, w_out=w_out_all4, w_down=jnp.concatenate([down_a, landed[0][0]], axis=-1),
                  w_up=jnp.concatenate([landed[1][0], landed[2][0]], axis=-1))
    outs = [o.reshape(s, B_WIDTH) if d == 1 else o for o, d in zip(outs, DILATIONS)]
    lses = [t.reshape(s, B_WIDTH) if d == 1 else t for t, d in zip(lses, DILATIONS)]
    ob, lse, mixed = _attn_combine(outs, lses, p["out_norm_b"], ma, "attn_combine" + tag)
    y1, x1, h2 = _mix_out_norm(mixed, wg["w_out"], x0, p["post_mix_norm"], p["pre_ffn_norm"], "mix_out" + tag)
    post = None if g_next is None else (x1, p["post_ffn_norm"], g_next)
    (y, up3, cv3, f, *after), gathered = _ffn_forward(h2, wg["w_up"], wg["w_down"], p["cw3"], p["cb3"],
                                                      "ffn_fwd" + tag, gather, post)
    saved = dict(x0=x0, h1=h1, proj=proj, q=q, k=k, v=v, ob=ob, lse=lse, mixed=mixed, y1=y1, x1=x1, h2=h2,
                 up3=up3, cv3=cv3, y=y, f=f)
    if after:
        saved.update(x2=after[0], h_next=after[1])
    return saved, gathered, wg


def _layer_backward(l, dx2, df, sv, p, wg, tabs, pos, scatter=None, hide=False):
    s = dx2.shape[0]
    nm = s // TMM
    tag = f"_l{l}"
    g = {}
    (dup3, dh2, conv_sums), scattered = _ffn_backward(df, wg["w_up"], wg["w_down"], sv["up3"], sv["cv3"], p["cw3"],
                                                      "ffn_bwd" + tag, scatter)
    sums = conv_sums.transpose(1, 2, 0, 3).reshape(2, 8, D_FF)
    g["conv_w"] = jnp.concatenate([sums[0, :3], sums[1, :3]], axis=1)
    g["conv_b"] = jnp.concatenate([sums[0, 3:4], sums[1, 3:4]], axis=1)
    tn = 1024
    done = {}
    gw_down = _matmul(
        sv["y"], df, grid=(D_FF // tn, nm), a_spec=pl.BlockSpec((TMM, tn), lambda k, m: (m, k)),
        b_spec=pl.BlockSpec((TMM, D_MODEL), lambda k, m: (m, 0)),
        o_spec=pl.BlockSpec((2, tn, D_MODEL // 2), lambda k, m: (0, k, 0)),
        o_shape=(2, D_FF, D_MODEL // 2), o_dtype=BF16,
        dims=TN, nk=nm, kaxis=1, acc_shape=(tn, D_MODEL), name="w_down_grad" + tag, halves=True)
    down_sums = _chip_sums(l, dict(w_down=gw_down), pos, ("w_down",)) if hide else None
    gw_up = _matmul(
        sv["h2"], dup3, grid=(2 * D_FF // tn, nm), a_spec=pl.BlockSpec((TMM, D_MODEL), lambda n, m: (m, 0)),
        b_spec=pl.BlockSpec((None, TMM, tn), lambda n, m: (n // (D_FF // tn), m, n % (D_FF // tn))),
        o_spec=pl.BlockSpec((None, D_MODEL, tn), lambda n, m: (n // 2, 0, n % 2)),
        o_shape=(N_CHIPS, D_MODEL, 2 * D_FF // N_CHIPS), o_dtype=BF16,
        dims=TN, nk=nm, kaxis=1, acc_shape=(D_MODEL, tn), name="w_up_grad" + tag,
        scatter=(down_sums, ("w_down",)) if hide else None)
    up_sums = None
    if hide:
        gw_up, received = gw_up
        done[("w_down",)] = (down_sums, received)
        up_sums = _chip_sums(l, dict(w_up=gw_up), pos, ("w_up",))
    dx1, dy1, g["pre_ffn_norm"], g["post_mix_norm"] = _norm_bwd_mid(
        dx2, dh2, sv["x1"], p["pre_ffn_norm"], sv["y1"], p["post_mix_norm"], "norm_bwd_mid" + tag)
    w_out_all = pl.BlockSpec((N_CHIPS, None, D_MODEL // N_CHIPS, D_MODEL), lambda i: (0, 0, 0, 0))
    dmixed = _matmul(
        dy1, wg["w_out"], grid=(nm,), a_spec=pl.BlockSpec((TMM, D_MODEL), lambda i: (i, 0)), b_spec=w_out_all,
        o_spec=pl.BlockSpec((TMM, D_MODEL), lambda i: (i, 0)), o_shape=(s, D_MODEL), o_dtype=F32,
        dims=NT, nk=1, kaxis=None, acc_shape=None, name="mix_out_bwd" + tag, b_2d=(D_MODEL, D_MODEL))
    gw_out = _matmul(
        sv["mixed"], dy1, grid=(nm,), a_spec=pl.BlockSpec((TMM, D_MODEL), lambda m: (m, 0)),
        b_spec=pl.BlockSpec((TMM, D_MODEL), lambda m: (m, 0)),
        o_spec=pl.BlockSpec((2, D_MODEL, D_MODEL // 2), lambda m: (0, 0, 0)),
        o_shape=(2, D_MODEL, D_MODEL // 2), o_dtype=BF16,
        dims=TN, nk=nm, kaxis=0, acc_shape=(D_MODEL, D_MODEL), name="w_out_grad" + tag, halves=True)
    dpa, g["out_norm_a"], g["v_norm_g"], g["v_norm_b"], dbs, g["w_spatial"] = _mixer_a_bwd(
        sv["proj"], dmixed, p["v_norm_g"], p["v_norm_b"], p["w_spatial"], p["bs_full"], p["out_norm_a"],
        "mixer_a_bwd" + tag)
    g["b_spatial"] = dbs[:, ::GROUP_DIM].T
    dob, delta, g["out_norm_b"] = _attn_bwd_prep(dmixed, sv["ob"], p["out_norm_b"], "attn_bwd_prep" + tag)
    last_dil = DILATIONS[-1]
    whole = lambda d: s // d // BAND <= MAX_CLASS_BLOCKS
    dqs, dks, dvs, received = zip(*[
        (_attn_bwd_class if whole(d) else _attn_bwd)(
            *(_as_classes(t[d]) for t in (sv["q"], sv["k"], sv["v"], dob, sv["lse"], delta)),
            f"attn_bwd_d{d}" + tag, (up_sums, ("w_up",)) if hide and d == last_dil else None)
        for d in DILATIONS])
    if hide:
        done[("w_up",)] = (up_sums, received[-1])
    nat = lambda ts: [t.reshape(s, B_WIDTH) if d == 1 else t for t, d in zip(ts, DILATIONS)]
    dproj = _rope_bwd(nat(dqs), nat(dks), nat(dvs), tabs, dpa, "rope_bwd" + tag)
    wcol = IN_COLS // N_CHIPS
    dh1 = _matmul(
        dproj, wg["w_in"], grid=(nm, N_CHIPS), a_spec=pl.BlockSpec((TMM, wcol), lambda i, n: (i, n)),
        b_spec=_wspec(D_MODEL, wcol, lambda i, n: (n, 0, 0, 0)),
        o_spec=pl.BlockSpec((TMM, D_MODEL), lambda i, n: (i, 0)), o_shape=(s, D_MODEL), o_dtype=F32,
        dims=NT, nk=N_CHIPS, kaxis=1, acc_shape=(TMM, D_MODEL), name="proj_bwd" + tag)
    gw_in = _matmul(
        sv["h1"], dproj, grid=(N_CHIPS, nm), a_spec=pl.BlockSpec((TMM, D_MODEL), lambda n, m: (m, 0)),
        b_spec=pl.BlockSpec((TMM, wcol), lambda n, m: (m, n)),
        o_spec=pl.BlockSpec((None, D_MODEL, wcol), lambda n, m: (n, 0, 0)),
        o_shape=(N_CHIPS, D_MODEL, wcol), o_dtype=BF16,
        dims=TN, nk=nm, kaxis=1, acc_shape=(D_MODEL, wcol), name="w_in_grad" + tag)
    big = dict(w_in=gw_in, w_out=gw_out) if hide else dict(w_in=gw_in, w_up=gw_up, w_out=gw_out, w_down=gw_down)
    return dx1, dh1, big, g, scattered, done


SMALL = ("pre_mix_norm", "v_norm_g", "v_norm_b", "w_spatial", "b_spatial", "out_norm_a", "out_norm_b",
         "post_mix_norm", "pre_ffn_norm", "conv_b", "post_ffn_norm")
BIG = ("w_in", "w_out", "w_up", "w_down")
DEPTH = 2


def _layer_params(l, small, conv_w_full):
    p = {n: small[n][l].reshape(1, -1) for n in SMALL if n not in ("w_spatial", "b_spatial")}
    p["w_spatial"] = small["w_spatial"][l]
    p["bs_full"] = jnp.repeat(small["b_spatial"][l].T, GROUP_DIM, axis=1)
    p["cw3"] = conv_w_full[l].reshape(3, 2, D_FF).transpose(1, 0, 2)
    p["cb3"] = small["conv_b"][l].reshape(2, 1, D_FF)
    return p


def _mesh_pos():
    return lax.axis_index("x"), lax.axis_index("y"), lax.axis_index("c")


def _other_chips(x, y):
    return [(1 - x, y), (x, 1 - y), (1 - x, 1 - y)]


def _gathered_shapes(blocks):
    return [jax.ShapeDtypeStruct((N_CHIPS, 1) + a.shape, a.dtype) for a in blocks]


def _gather_sems(nw):
    n = 2 * nw * (N_CHIPS - 1) + nw
    return [pltpu.SemaphoreType.DMA((n,)), pltpu.SemaphoreType.DMA((n,))]


def _gather_steps(ins, outs, send, recv):
    nw, nrel = len(ins), N_CHIPS - 1
    x, y, c = _mesh_pos()
    mine, sibling, chips = 2 * x + y, (x, y, 1 - c), _other_chips(x, y)

    def copy(src, dst, slot, to):
        return pltpu.make_async_remote_copy(src_ref=src, dst_ref=dst, send_sem=send.at[slot],
                                            recv_sem=recv.at[slot], device_id=to, device_id_type=MESH)

    def half_rows(t, core):
        rows = ins[t].shape[0] // 2
        return pl.ds(pl.multiple_of(core * rows, rows), rows)

    def landing(t, chip, core):
        return outs[t].at[chip, 0, half_rows(t, core), :]

    slots = [(t, r, chip) for t in range(nw) for r, chip in enumerate(chips)]
    own = [copy(ins[t], outs[t].at[mine, 0], 2 * nw * nrel + t, sibling) for t in range(nw)]
    first = [copy(ins[t].at[half_rows(t, c), :], landing(t, mine, c), t * nrel + r, (px, py, c))
             for t, r, (px, py) in slots]
    relays = [copy(landing(t, 2 * px + py, c), landing(t, 2 * px + py, c), nw * nrel + t * nrel + r, sibling)
              for t, r, (px, py) in slots]

    def start():
        for cp in own + first:
            cp.start()

    def relay():
        for (t, r, (px, py)), cp in zip(slots, relays):
            copy(landing(t, 2 * px + py, c), landing(t, 2 * px + py, c), t * nrel + r, (px, py, c)).wait_recv()
            cp.start()

    def finish():
        for t, r, (px, py) in slots:
            passed = landing(t, 2 * px + py, 1 - c)
            copy(passed, passed, nw * nrel + t * nrel + r, sibling).wait_recv()
        for cp in first + relays:
            cp.wait_send()
        for cp in own:
            cp.wait()

    return start, relay, finish


def _gather_weights(blocks, name):
    nw = len(blocks)

    def body(*refs):
        start, relay, finish = _gather_steps(refs[:nw], refs[nw:2 * nw], *refs[2 * nw:])
        start()
        relay()
        finish()

    return pl.pallas_call(
        body, in_specs=[ANY] * nw, out_specs=[ANY] * nw, out_shape=_gathered_shapes(blocks),
        scratch_shapes=_gather_sems(nw), name=name)(*blocks)


HALF = 512

GRAD_GEOM = {"w_in": ("rows", D_MODEL, IN_COLS // N_CHIPS), "w_up": ("rows", D_MODEL, 2 * D_FF // N_CHIPS),
             "w_out": ("cols", D_MODEL, D_MODEL // N_CHIPS), "w_down": ("cols", D_FF, D_FF // N_CHIPS)}


def _exchange_shape(n):
    kind, a, b = GRAD_GEOM[n]
    return (N_CHIPS, HALF, b) if kind == "rows" else (a, HALF)


def _piece_shape(n):
    kind, _, b = GRAD_GEOM[n]
    return (HALF, b) if kind == "rows" else (b, HALF)


def _half_of(ref, n, core):
    if GRAD_GEOM[n][0] == "rows":
        return ref.at[:, pl.ds(pl.multiple_of(core * HALF, HALF), HALF), :]
    return ref.at[core]


def _piece_of(ref, n, chip):
    kind, _, b = GRAD_GEOM[n]
    return ref.at[chip] if kind == "rows" else ref.at[pl.ds(pl.multiple_of(chip * b, b), b), :]


def _pair_exchange(g, names, name):
    n = len(names)

    def body(*refs):
        send, recv = refs[2 * n:]
        x, y, c = _mesh_pos()
        o = 1 - c
        cps = [pltpu.make_async_remote_copy(src_ref=_half_of(refs[t], nm, o), dst_ref=refs[n + t], send_sem=send.at[t],
                                            recv_sem=recv.at[t], device_id=(x, y, o), device_id_type=MESH)
               for t, nm in enumerate(names)]
        for cp in cps:
            cp.start()
        for cp in cps:
            cp.wait()

    return pl.pallas_call(
        body, in_specs=[ANY] * n, out_specs=[ANY] * n,
        out_shape=[jax.ShapeDtypeStruct(_exchange_shape(nm), BF16) for nm in names],
        scratch_shapes=[pltpu.SemaphoreType.DMA((n,)), pltpu.SemaphoreType.DMA((n,))],
        name=name)(*[g[nm] for nm in names])


def _pair_sum(g, recv, pos, names, name_prefix):
    def add(a, b, grid, a_spec, b_spec, name):
        def body(pos_ref, a_ref, b_ref, o_ref):
            o_ref[...] = (a_ref[...].astype(F32) + b_ref[...].astype(F32)).astype(BF16)

        return pl.pallas_call(
            body, grid_spec=pltpu.PrefetchScalarGridSpec(
                num_scalar_prefetch=1, grid=grid, in_specs=[a_spec, b_spec], out_specs=b_spec),
            out_shape=jax.ShapeDtypeStruct(b.shape, BF16), compiler_params=_cparams("parallel"), name=name)(pos, a, b)

    out = []
    for nm, r in zip(names, recv):
        kind, rows, width = GRAD_GEOM[nm]
        if kind == "rows":
            out.append(add(g[nm], r, (N_CHIPS,), pl.BlockSpec((None, HALF, width), lambda j, pos: (j, pos[2], 0)),
                           pl.BlockSpec((None, HALF, width), lambda j, pos: (j, 0, 0)), f"{name_prefix}_{nm}"))
        else:
            out.append(add(g[nm], r, (rows // D_MODEL,), pl.BlockSpec((None, D_MODEL, HALF), lambda j, pos: (pos[2], j, 0)),
                           pl.BlockSpec((D_MODEL, HALF), lambda j, pos: (j, 0)), f"{name_prefix}_{nm}"))
    return out


def _scattered_shapes(names):
    return [jax.ShapeDtypeStruct((N_CHIPS - 1,) + _piece_shape(nm), BF16) for nm in names]


def _scatter_sems(n):
    return [pltpu.SemaphoreType.DMA((n * (N_CHIPS - 1),)), pltpu.SemaphoreType.DMA((n * (N_CHIPS - 1),))]


def _scatter_steps(sums, outs, send, recv, names):
    nrel = N_CHIPS - 1
    x, y, c = _mesh_pos()
    cps = []
    for r, (px, py) in enumerate(_other_chips(x, y)):
        for t, nm in enumerate(names):
            cps.append(pltpu.make_async_remote_copy(
                src_ref=_piece_of(sums[t], nm, 2 * px + py), dst_ref=outs[t].at[r], send_sem=send.at[t * nrel + r],
                recv_sem=recv.at[t * nrel + r], device_id=(px, py, c), device_id_type=MESH))

    def start():
        for cp in cps:
            cp.start()

    def finish():
        for cp in cps:
            cp.wait()

    return start, finish


def _chip_scatter(sums, names, name):
    n = len(names)

    def body(*refs):
        start, finish = _scatter_steps(refs[:n], refs[n:2 * n], *refs[2 * n:], names)
        start()
        finish()

    return pl.pallas_call(
        body, in_specs=[ANY] * n, out_specs=[ANY] * n, out_shape=_scattered_shapes(names),
        scratch_shapes=_scatter_sems(n), name=name)(*sums)


def _chip_sum(sums, recv, pos, names, name_prefix):
    def add(a, b, a_spec, shape, name):
        def body(pos_ref, a_ref, b_ref, o_ref):
            tot = a_ref[...].astype(F32)
            for r in range(N_CHIPS - 1):
                tot = tot + b_ref[r].astype(F32)
            o_ref[...] = tot

        return pl.pallas_call(
            body, grid_spec=pltpu.PrefetchScalarGridSpec(
                num_scalar_prefetch=1, grid=(1,), in_specs=[a_spec, pl.BlockSpec(b.shape, lambda i, pos: (0, 0, 0))],
                out_specs=pl.BlockSpec((None,) + shape, lambda i, pos: (pos[2], 0, 0))),
            out_shape=jax.ShapeDtypeStruct((2,) + shape, F32), compiler_params=_cparams("arbitrary"),
            name=name)(pos, a, b)

    chip = lambda pos: 2 * pos[0] + pos[1]
    out = []
    for nm, a, b in zip(names, sums, recv):
        shape = _piece_shape(nm)
        if GRAD_GEOM[nm][0] == "rows":
            spec = pl.BlockSpec((None,) + shape, lambda i, pos: (chip(pos), 0, 0))
        else:
            spec = pl.BlockSpec(shape, lambda i, pos: (chip(pos), 0))
        out.append(add(a, b, spec, shape, f"{name_prefix}_{nm}"))
    return out


def _pair_share(totals, name):
    n = len(totals)

    def body(*refs):
        ins, outs = refs[:n], refs[n:2 * n]
        send, recv = refs[2 * n:]
        x, y, c = _mesh_pos()
        o = 1 - c
        cps = [pltpu.make_async_remote_copy(src_ref=ins[t].at[c], dst_ref=outs[t].at[c], send_sem=send.at[t],
                                            recv_sem=recv.at[t], device_id=(x, y, o), device_id_type=MESH)
               for t in range(n)]
        for cp in cps:
            cp.start()
        for t in range(n):
            pltpu.make_async_remote_copy(src_ref=ins[t].at[o], dst_ref=outs[t].at[o], send_sem=send.at[t],
                                         recv_sem=recv.at[t], device_id=(x, y, o), device_id_type=MESH).wait_recv()
        for cp in cps:
            cp.wait_send()

    return pl.pallas_call(
        body, in_specs=[ANY] * n, out_specs=[ANY] * n,
        out_shape=[jax.ShapeDtypeStruct(t.shape, t.dtype) for t in totals],
        scratch_shapes=[pltpu.SemaphoreType.DMA((n,)), pltpu.SemaphoreType.DMA((n,))],
        input_output_aliases={t: t for t in range(n)}, name=name)(*totals)


def _chip_sums(l, g, pos, names):
    tag = f"l{l}_" + "_".join(names)
    recv = _pair_exchange(g, names, "pair_exchange_" + tag)
    return _pair_sum(g, recv, pos, names, "pair_sum_" + tag)


def _gradient_shards(l, sums, scattered, pos, names):
    tag = f"l{l}_" + "_".join(names)
    halves = _pair_share(_chip_sum(sums, scattered, pos, names, "chip_sum_" + tag), "pair_share_" + tag)
    out = {}
    for nm, t in zip(names, halves):
        rows, cols = _piece_shape(nm)
        out[nm] = t.reshape(2 * rows, cols) if GRAD_GEOM[nm][0] == "rows" else t.transpose(1, 0, 2).reshape(rows, 2 * cols)
    return out


N_DEV = 8


def _allreduce_small(packed, name):
    rows = packed.shape[0]

    def body(x_ref, out_ref, gath, send_sems, recv_sems, local_sem):
        x, y, c = _mesh_pos()
        me, sibling = (x, y, c), (x, y, 1 - c)
        chips = _other_chips(x, y)

        def blk(px, py, pc):
            return gath.at[pl.ds(pl.multiple_of((4 * px + 2 * py + pc) * rows, 8), rows), :]

        def copy(k, block, to, src=None):
            return pltpu.make_async_remote_copy(
                src_ref=blk(*block) if src is None else src, dst_ref=blk(*block), send_sem=send_sems.at[k],
                recv_sem=recv_sems.at[k], device_id=to, device_id_type=MESH)

        mine = pltpu.make_async_copy(x_ref, blk(*me), local_sem)
        mine.start()
        first = [copy(0, me, sibling, src=x_ref)]
        first += [copy(1 + j, me, (*chip, c), src=x_ref) for j, chip in enumerate(chips)]
        for cp in first:
            cp.start()
        passed = [copy(4 + j, (*chip, c), sibling) for j, chip in enumerate(chips)]
        for j, chip in enumerate(chips):
            copy(1 + j, (*chip, c), me).wait_recv()
            passed[j].start()
        copy(0, sibling, me).wait_recv()
        for j, chip in enumerate(chips):
            copy(4 + j, (*chip, 1 - c), me).wait_recv()
        for cp in first + passed:
            cp.wait_send()
        mine.wait()
        tot = gath[0:rows, :]
        for d in range(1, N_DEV):
            tot = tot + gath[d * rows:(d + 1) * rows, :]
        out_ref[...] = tot

    vmem = pl.BlockSpec(memory_space=pltpu.VMEM)
    return pl.pallas_call(
        body, in_specs=[vmem], out_specs=vmem, out_shape=jax.ShapeDtypeStruct((rows, LANES), F32),
        scratch_shapes=[pltpu.VMEM((N_DEV * rows, LANES), F32), pltpu.SemaphoreType.DMA((7,)),
                        pltpu.SemaphoreType.DMA((7,)), pltpu.SemaphoreType.DMA],
        compiler_params=pltpu.CompilerParams(vmem_limit_bytes=VMEM_LIMIT_BYTES),
        name=name)(packed)


def _adamw(w, g, m, v, name):
    rows, cols = w.shape
    tr = 256 if rows % 256 == 0 else rows

    def body(w_ref, g_ref, m_ref, v_ref, d_ref, mo_ref, vo_ref):
        gv = g_ref[...]
        mn = ADAM_B1 * m_ref[...] + (1.0 - ADAM_B1) * gv
        vn = ADAM_B2 * v_ref[...] + (1.0 - ADAM_B2) * (gv * gv)
        m_hat = mn / (1.0 - ADAM_B1 ** ADAM_STEP)
        v_hat = vn / (1.0 - ADAM_B2 ** ADAM_STEP)
        d_ref[...] = -ADAM_LR * (m_hat / (jnp.sqrt(v_hat) + ADAM_EPS) + ADAM_WD * w_ref[...])
        mo_ref[...] = mn
        vo_ref[...] = vn

    spec = pl.BlockSpec((tr, cols), lambda i: (i, 0))
    return pl.pallas_call(
        body, grid=(rows // tr,), in_specs=[spec] * 4, out_specs=[spec] * 3,
        out_shape=[jax.ShapeDtypeStruct((rows, cols), F32)] * 3, compiler_params=_cparams("parallel"),
        name=name)(w, g, m, v)


def _adamw_nd(w, g, m, v, name):
    cols = w.shape[-1] if w.shape[-1] % LANES == 0 else LANES
    outs = _adamw(*(t.reshape(-1, cols) for t in (w, g, m, v)), name)
    return tuple(t.reshape(w.shape) for t in outs)


def _pack(arrays):
    return jnp.concatenate([a.reshape(-1, LANES) for a in arrays], axis=0)


def _unpack(packed, shapes):
    out, row = [], 0
    for sh in shapes:
        n = math.prod(sh) // LANES
        out.append(packed[row:row + n].reshape(sh))
        row += n
    return out


WEIGHTS = ("pre_mix_norm", "w_in", "v_norm_g", "v_norm_b", "w_spatial", "b_spatial", "out_norm_a", "out_norm_b",
           "w_out", "post_mix_norm", "pre_ffn_norm", "w_up", "conv_w", "conv_b", "w_down", "post_ffn_norm")


def kernel(x, pre_mix_norm, w_in, v_norm_g, v_norm_b, w_spatial, b_spatial, out_norm_a, out_norm_b, w_out, post_mix_norm, pre_ffn_norm, w_up, conv_w, conv_b, w_down, post_ffn_norm, loss_target, m_pre_mix_norm, m_w_in, m_v_norm_g, m_v_norm_b, m_w_spatial, m_b_spatial, m_out_norm_a, m_out_norm_b, m_w_out, m_post_mix_norm, m_pre_ffn_norm, m_w_up, m_conv_w, m_conv_b, m_w_down, m_post_ffn_norm, v_pre_mix_norm, v_w_in, v_v_norm_g, v_v_norm_b, v_w_spatial, v_b_spatial, v_out_norm_a, v_out_norm_b, v_w_out, v_post_mix_norm, v_pre_ffn_norm, v_w_up, v_conv_w, v_conv_b, v_w_down, v_post_ffn_norm):
    w = dict(pre_mix_norm=pre_mix_norm, w_in=w_in, v_norm_g=v_norm_g, v_norm_b=v_norm_b, w_spatial=w_spatial,
             b_spatial=b_spatial, out_norm_a=out_norm_a, out_norm_b=out_norm_b, w_out=w_out,
             post_mix_norm=post_mix_norm, pre_ffn_norm=pre_ffn_norm, w_up=w_up, conv_w=conv_w, conv_b=conv_b,
             w_down=w_down, post_ffn_norm=post_ffn_norm)
    m = dict(pre_mix_norm=m_pre_mix_norm, w_in=m_w_in, v_norm_g=m_v_norm_g, v_norm_b=m_v_norm_b,
             w_spatial=m_w_spatial, b_spatial=m_b_spatial, out_norm_a=m_out_norm_a, out_norm_b=m_out_norm_b,
             w_out=m_w_out, post_mix_norm=m_post_mix_norm, pre_ffn_norm=m_pre_ffn_norm, w_up=m_w_up,
             conv_w=m_conv_w, conv_b=m_conv_b, w_down=m_w_down, post_ffn_norm=m_post_ffn_norm)
    v = dict(pre_mix_norm=v_pre_mix_norm, w_in=v_w_in, v_norm_g=v_v_norm_g, v_norm_b=v_v_norm_b,
             w_spatial=v_w_spatial, b_spatial=v_b_spatial, out_norm_a=v_out_norm_a, out_norm_b=v_out_norm_b,
             w_out=v_w_out, post_mix_norm=v_post_mix_norm, pre_ffn_norm=v_pre_ffn_norm, w_up=v_w_up,
             conv_w=v_conv_w, conv_b=v_conv_b, w_down=v_w_down, post_ffn_norm=v_post_ffn_norm)
    pos = jnp.stack([lax.axis_index("x"), lax.axis_index("y"), lax.axis_index("c")]).astype(jnp.int32)
    chip = 2 * lax.axis_index("x") + lax.axis_index("y")

    cw_cols = conv_w.shape[-1]
    cw_slab = lax.dynamic_update_slice(jnp.zeros((DEPTH, 3, 2 * D_FF), F32), conv_w, (0, 0, chip * cw_cols))
    conv_w_full = _allreduce_small(cw_slab.reshape(-1, LANES), "gather_conv_w").reshape(DEPTH, 3, 2 * D_FF)
    conv_w_full = conv_w_full * 0.5
    blocks = [{n: w[n][l].astype(BF16) for n in BIG} for l in range(DEPTH)]
    wg = dict(w_in=_gather_weights([blocks[0]["w_in"]], "gather_w_in_l0")[0])

    small = {n: w[n] for n in SMALL}
    xs, target = x[0], loss_target[0]
    tabs = _rope_tables(xs.shape[0])
    params = [_layer_params(l, small, conv_w_full) for l in range(DEPTH)]
    saved, wgs = [], []
    xin = xs
    h = _rms_cast(xin, params[0]["pre_mix_norm"], "pre_mix_l0")
    for l in range(DEPTH):
        sv, gathered, wg = _layer_forward(l, xin, h, params[l], wg, tabs,
                                          [blocks[l + 1][n] for n in BIG] if l + 1 < DEPTH else None,
                                          blocks[0] if l == 0 else None,
                                          params[l + 1]["pre_mix_norm"] if l + 1 < DEPTH else None)
        saved.append(sv)
        wgs.append(wg)
        if l + 1 < DEPTH:
            wg = dict(zip(BIG, gathered))
            xin, h = sv["x2"], sv["h_next"]
    loss_part, dx, df, g_post = _loss_norm_bwd(saved[-1]["x1"], saved[-1]["f"], params[-1]["post_ffn_norm"], target,
                                               "loss")
    smalls, shards = [None] * DEPTH, [{} for _ in range(DEPTH)]
    pending = None
    for l in reversed(range(DEPTH)):
        dx1, dh1, big, smalls[l], scattered, done = _layer_backward(l, dx, df, saved[l], params[l], wgs[l], tabs, pos,
                                                                    pending[1:] if pending else None, hide=l == 0)
        smalls[l]["post_ffn_norm"] = g_post
        if l > 0:
            dx, smalls[l]["pre_mix_norm"], df, g_post = _norm_bwd_in_out(
                dx1, dh1, saved[l]["x0"], params[l]["pre_mix_norm"], saved[l - 1]["f"], params[l - 1]["post_ffn_norm"],
                f"norm_bwd_in_out_l{l}")
        else:
            dx, smalls[l]["pre_mix_norm"] = _norm_bwd_in(dx1, dh1, saved[l]["x0"], params[l]["pre_mix_norm"],
                                                         "norm_bwd_in_l0")
        if pending:
            shards[pending[0]].update(_gradient_shards(pending[0], pending[1], scattered, pos, pending[2]))
        for names, (sums, received) in done.items():
            shards[l].update(_gradient_shards(l, sums, received, pos, names))
        names = tuple(big)
        pending = (l, _chip_sums(l, big, pos, names), names)
    shards[pending[0]].update(_gradient_shards(
        pending[0], pending[1], _chip_scatter(pending[1], pending[2], f"chip_scatter_l{pending[0]}"), pos, pending[2]))

    small_shapes = [w[n].shape for n in SMALL]
    stacked = [jnp.stack([smalls[l][n].reshape(w[n].shape[1:]) for l in range(DEPTH)]) for n in SMALL]
    cw_grad = jnp.stack([smalls[l]["conv_w"] for l in range(DEPTH)])
    packed = _pack(stacked + [cw_grad, loss_part])
    total = _allreduce_small(packed, "allreduce_small")
    parts = _unpack(total, small_shapes + [cw_grad.shape, (8, LANES)])
    g_small = dict(zip(SMALL, parts[:len(SMALL)]))
    loss = parts[-1][0, 0]
    g_conv_w = lax.dynamic_slice(parts[-2], (0, 0, chip * cw_cols), conv_w.shape)

    grads = {n: jnp.stack([shards[l][n] for l in range(DEPTH)]) for n in BIG}
    grads.update(g_small)
    grads["conv_w"] = g_conv_w

    dp, mp, vp = _adamw(_pack([w[n] for n in SMALL]), _pack([g_small[n] for n in SMALL]),
                        _pack([m[n] for n in SMALL]), _pack([v[n] for n in SMALL]), "adamw_small")
    delta = dict(zip(SMALL, _unpack(dp, small_shapes)))
    new_m = dict(zip(SMALL, _unpack(mp, small_shapes)))
    new_v = dict(zip(SMALL, _unpack(vp, small_shapes)))
    for n in BIG + ("conv_w",):
        delta[n], new_m[n], new_v[n] = _adamw_nd(w[n], grads[n], m[n], v[n], "adamw_" + n)

    return (loss, dx[None], *[grads[n] for n in WEIGHTS], *[delta[n] for n in WEIGHTS],
            *[new_m[n] for n in WEIGHTS], *[new_v[n] for n in WEIGHTS])
```

```python
import functools
import math

import jax
import jax.numpy as jnp
import numpy as np
from jax import lax
from jax.experimental import pallas as pl
from jax.experimental.pallas import tpu as pltpu

F32 = jnp.float32
BF16 = jnp.bfloat16
MESH = pl.DeviceIdType.MESH

D_MODEL = 1024
A_WIDTH = 512
A_GROUPS = 4
GROUP_DIM = 128
CHUNK = 128
B_WIDTH = 512
HEAD_DIM = 64
ROT_DIM = 16
ROPE_THETA = 500000.0
DILATIONS = (1, 4, 16)
BAND = 128
IN_COLS = 2560
D_FF = 4096
EPS = 1e-6
NEG_INF = -1e30
N_CHIPS = 4
LANES = 128

ADAM_LR = 0.001
ADAM_B1 = 0.9
ADAM_B2 = 0.999
ADAM_EPS = 1e-08
ADAM_WD = 0.01
ADAM_STEP = 10

VMEM_LIMIT_BYTES = 56 * 1024 * 1024
RSQRT2 = 0.7071067811865476
INV_SQRT_2PI = 0.3989422804014327
GELU_C = 0.7978845608028654
GELU_A = 0.044715

ANY = pl.BlockSpec(memory_space=pl.ANY)
NN = ((1,), (0,))
NT = ((1,), (1,))
TN = ((0,), (0,))


def _cparams(*sem):
    return pltpu.CompilerParams(dimension_semantics=sem, vmem_limit_bytes=VMEM_LIMIT_BYTES)


def _dot(a, b, dims):
    return lax.dot_general(a, b, (dims, ((), ())), preferred_element_type=F32)


def _rsq_mean(a):
    return lax.rsqrt(jnp.mean(a * a, axis=-1, keepdims=True) + EPS)


def _rms_bwd(a, r, g, dz):
    t = dz * g
    da = r * t - a * (r * r * r) * jnp.mean(t * a, axis=-1, keepdims=True)
    return da, dz * a * r


def _colsum(a):
    return jnp.sum(a, axis=0, keepdims=True)


def _gelu_tanh(x):
    u = x * x
    t = jnp.tanh(x * (GELU_C + (GELU_C * GELU_A) * u))
    hx = 0.5 * x
    act = hx + hx * t
    grad = 0.5 + 0.5 * t + (hx - hx * t * t) * (GELU_C + (3.0 * GELU_C * GELU_A) * u)
    return act, grad


def _grid_edges(grid):
    ids = [pl.program_id(ax) for ax in range(len(grid))]
    first = functools.reduce(jnp.logical_and, [i == 0 for i in ids])
    last = functools.reduce(jnp.logical_and, [i == n - 1 for i, n in zip(ids, grid)])
    return first, last


def _matmul(a, b, *, grid, a_spec, b_spec, o_spec, o_shape, o_dtype, dims, nk, kaxis, acc_shape, name, b_2d=None,
            halves=False, scatter=None, gather=None):
    assert scatter is None or gather is None
    ns = len(scatter[0]) if scatter else len(gather) if gather else 0

    def body(*refs):
        a_ref, b_ref = refs[:2]
        o_ref = refs[2 + ns]
        scratch = refs[3 + 2 * ns:]
        if ns:
            first, last = _grid_edges(grid)
            if scatter:
                start, finish = _scatter_steps(refs[2:2 + ns], refs[3 + ns:3 + 2 * ns], scratch[-2], scratch[-1],
                                               scatter[1])
            else:
                start, relay, last_wait = _gather_steps(refs[2:2 + ns], refs[3 + ns:3 + 2 * ns], scratch[-2],
                                                        scratch[-1])

                def finish():
                    relay()
                    last_wait()
            pl.when(first)(start)
        def store(val):
            if halves:
                half = val.shape[1] // 2
                o_ref[0] = val[:, :half].astype(o_dtype)
                o_ref[1] = val[:, half:].astype(o_dtype)
            else:
                o_ref[...] = val.astype(o_dtype)

        bv = b_ref[...] if b_2d is None else b_ref[...].reshape(b_2d)
        part = _dot(a_ref[...], bv, dims)
        if nk == 1:
            store(part)
        else:
            acc = scratch[0]
            k = pl.program_id(kaxis)

            @pl.when(k == 0)
            def _():
                acc[...] = part

            @pl.when(k > 0)
            def _():
                acc[...] += part

            @pl.when(k == nk - 1)
            def _():
                store(acc[...])

        if ns:
            pl.when(last)(finish)

    sem = tuple("arbitrary" if (ns or (nk > 1 and ax == kaxis)) else "parallel" for ax in range(len(grid)))
    riding = list(scatter[0]) if scatter else list(gather or [])
    rider_shapes = _scattered_shapes(scatter[1]) if scatter else _gathered_shapes(riding)
    rider_sems = _scatter_sems(ns) if scatter else _gather_sems(ns) if gather else []
    res = pl.pallas_call(
        body, grid=grid, in_specs=[a_spec, b_spec] + [ANY] * ns, out_specs=[o_spec] + [ANY] * ns,
        out_shape=[jax.ShapeDtypeStruct(o_shape, o_dtype)] + rider_shapes,
        scratch_shapes=([pltpu.VMEM(acc_shape, F32)] if nk > 1 else []) + rider_sems,
        compiler_params=_cparams(*sem), name=name)(a, b, *riding)
    return (res[0], list(res[1:])) if ns else res[0]


def _mix_out_norm(mixed, w_out, x0, g_post, g_next, name):
    s, d = x0.shape
    tm = 512

    def body(a_ref, w_ref, x_ref, gp_ref, gn_ref, y_ref, x1_ref, h_ref):
        y = _dot(a_ref[...], w_ref[...].reshape(d, d), NN)
        y_ref[...] = y
        x1 = x_ref[...] + y * _rsq_mean(y) * gp_ref[...]
        x1_ref[...] = x1
        h_ref[...] = (x1 * _rsq_mean(x1) * gn_ref[...]).astype(BF16)

    row = pl.BlockSpec((tm, d), lambda i: (i, 0))
    vec = pl.BlockSpec((1, d), lambda i: (0, 0))
    return pl.pallas_call(
        body, grid=(s // tm,),
        in_specs=[row, pl.BlockSpec((N_CHIPS, None, d // N_CHIPS, d), lambda i: (0, 0, 0, 0)), row, vec, vec],
        out_specs=[row, row, row],
        out_shape=[jax.ShapeDtypeStruct((s, d), F32), jax.ShapeDtypeStruct((s, d), F32),
                   jax.ShapeDtypeStruct((s, d), BF16)],
        compiler_params=_cparams("parallel"), name=name)(mixed, w_out, x0, g_post, g_next)


TM = 512
TMM = 1024


TR = 256


def _row_spec(width, col=0):
    return pl.BlockSpec((TR, width), lambda i, col=col: (i, col))


def _vec_spec(width):
    return pl.BlockSpec((1, width), lambda i: (0, 0))


def _rms_cast(x, g, name):
    s, d = x.shape

    def body(x_ref, g_ref, h_ref):
        a = x_ref[...]
        h_ref[...] = (a * _rsq_mean(a) * g_ref[...]).astype(BF16)

    return pl.pallas_call(
        body, grid=(s // TR,), in_specs=[_row_spec(d), _vec_spec(d)], out_specs=_row_spec(d),
        out_shape=jax.ShapeDtypeStruct((s, d), BF16), compiler_params=_cparams("parallel"), name=name)(x, g)


def _acc_init(refs):
    @pl.when(pl.program_id(0) == 0)
    def _():
        for r in refs:
            r[...] = jnp.zeros_like(r)


def _loss_norm_bwd(x1, f, g_post, target, name):
    s, d = x1.shape

    def body(x_ref, f_ref, gp_ref, t_ref, loss_ref, dx_ref, df_ref, dg_ref):
        _acc_init([loss_ref, dg_ref])
        fv = f_ref[...]
        r = _rsq_mean(fv)
        err = x_ref[...] + fv * r * gp_ref[...] - t_ref[...]
        dx = err * (1.0 / d)
        dx_ref[...] = dx
        part = 0.5 * jnp.sum(jnp.mean(err * err, axis=-1, keepdims=True), axis=0, keepdims=True)
        loss_ref[...] += jnp.broadcast_to(part, loss_ref.shape)
        da, dgt = _rms_bwd(fv, r, gp_ref[...], dx)
        df_ref[...] = da.astype(BF16)
        dg_ref[...] += _colsum(dgt)

    return pl.pallas_call(
        body, grid=(s // TR,), in_specs=[_row_spec(d), _row_spec(d), _vec_spec(d), _row_spec(d)],
        out_specs=[pl.BlockSpec((8, LANES), lambda i: (0, 0)), _row_spec(d), _row_spec(d), _vec_spec(d)],
        out_shape=[jax.ShapeDtypeStruct((8, LANES), F32), jax.ShapeDtypeStruct((s, d), F32),
                   jax.ShapeDtypeStruct((s, d), BF16), jax.ShapeDtypeStruct((1, d), F32)],
        compiler_params=_cparams("arbitrary"), name=name)(x1, f, g_post, target)


def _norm_bwd_mid(dx2, dh2, x1, g_pf, y1, g_pm, name):
    s, d = dx2.shape

    def body(dx2_ref, dh_ref, x1_ref, gpf_ref, y1_ref, gpm_ref, dx1_ref, dy1_ref, dgpf_ref, dgpm_ref):
        _acc_init([dgpf_ref, dgpm_ref])
        x1 = x1_ref[...]
        da, dgt = _rms_bwd(x1, _rsq_mean(x1), gpf_ref[...], dh_ref[...])
        dx1 = dx2_ref[...] + da
        dx1_ref[...] = dx1
        dgpf_ref[...] += _colsum(dgt)
        y1 = y1_ref[...]
        dy, dgt2 = _rms_bwd(y1, _rsq_mean(y1), gpm_ref[...], dx1)
        dy1_ref[...] = dy.astype(BF16)
        dgpm_ref[...] += _colsum(dgt2)

    return pl.pallas_call(
        body, grid=(s // TR,),
        in_specs=[_row_spec(d), _row_spec(d), _row_spec(d), _vec_spec(d), _row_spec(d), _vec_spec(d)],
        out_specs=[_row_spec(d), _row_spec(d), _vec_spec(d), _vec_spec(d)],
        out_shape=[jax.ShapeDtypeStruct((s, d), F32), jax.ShapeDtypeStruct((s, d), BF16),
                   jax.ShapeDtypeStruct((1, d), F32), jax.ShapeDtypeStruct((1, d), F32)],
        compiler_params=_cparams("arbitrary"), name=name)(dx2, dh2, x1, g_pf, y1, g_pm)


def _norm_bwd_in_out(dx1, dh1, x0, g1, f_below, g_post_below, name):
    s, d = dx1.shape

    def body(dx1_ref, dh_ref, x0_ref, g_ref, f_ref, gp_ref, dx0_ref, dg_ref, df_ref, dgp_ref):
        _acc_init([dg_ref, dgp_ref])
        x0 = x0_ref[...]
        da, dgt = _rms_bwd(x0, _rsq_mean(x0), g_ref[...], dh_ref[...])
        dx0 = dx1_ref[...] + da
        dx0_ref[...] = dx0
        dg_ref[...] += _colsum(dgt)
        fv = f_ref[...]
        db, dgt2 = _rms_bwd(fv, _rsq_mean(fv), gp_ref[...], dx0)
        df_ref[...] = db.astype(BF16)
        dgp_ref[...] += _colsum(dgt2)

    return pl.pallas_call(
        body, grid=(s // TR,),
        in_specs=[_row_spec(d), _row_spec(d), _row_spec(d), _vec_spec(d), _row_spec(d), _vec_spec(d)],
        out_specs=[_row_spec(d), _vec_spec(d), _row_spec(d), _vec_spec(d)],
        out_shape=[jax.ShapeDtypeStruct((s, d), F32), jax.ShapeDtypeStruct((1, d), F32),
                   jax.ShapeDtypeStruct((s, d), BF16), jax.ShapeDtypeStruct((1, d), F32)],
        compiler_params=_cparams("arbitrary"), name=name)(dx1, dh1, x0, g1, f_below, g_post_below)


def _norm_bwd_in(dx1, dh1, x0, g1, name):
    s, d = dx1.shape

    def body(dx1_ref, dh_ref, x0_ref, g_ref, dx0_ref, dg_ref):
        _acc_init([dg_ref])
        x0 = x0_ref[...]
        da, dgt = _rms_bwd(x0, _rsq_mean(x0), g_ref[...], dh_ref[...])
        dx0_ref[...] = dx1_ref[...] + da
        dg_ref[...] += _colsum(dgt)

    return pl.pallas_call(
        body, grid=(s // TR,), in_specs=[_row_spec(d), _row_spec(d), _row_spec(d), _vec_spec(d)],
        out_specs=[_row_spec(d), _vec_spec(d)],
        out_shape=[jax.ShapeDtypeStruct((s, d), F32), jax.ShapeDtypeStruct((1, d), F32)],
        compiler_params=_cparams("arbitrary"), name=name)(dx1, dh1, x0, g1)


def _tril_mask():
    row = lax.broadcasted_iota(jnp.int32, (CHUNK, CHUNK), 0)
    col = lax.broadcasted_iota(jnp.int32, (CHUNK, CHUNK), 1)
    return row >= col


def _gating_forward(pa, gv, bv, wt, bsf):
    er = lax.erf(pa * RSQRT2)
    za = 0.5 * pa * (1.0 + er)
    u = za[:, :A_WIDTH]
    va = za[:, A_WIDTH:]
    xc = va - jnp.mean(va, axis=-1, keepdims=True)
    rs = lax.rsqrt(jnp.mean(xc * xc, axis=-1, keepdims=True) + EPS)
    vn = xc * rs
    vlb = (vn * gv + bv).astype(BF16)
    sg = jnp.concatenate(
        [_dot(wt[g], vlb[:, g * GROUP_DIM:(g + 1) * GROUP_DIM], NN) for g in range(A_GROUPS)], axis=1) + bsf
    return er, u, rs, vn, vlb, sg


def _masked_ws(ws_ref):
    mask = _tril_mask()
    return [jnp.where(mask, ws_ref[g], 0.0).astype(BF16) for g in range(A_GROUPS)]


def _mixer_a_fwd(proj, gv, bv, ws, bsf, ga, name):
    s = proj.shape[0]

    def body(p_ref, gv_ref, bv_ref, ws_ref, bs_ref, ga_ref, o_ref):
        wt = _masked_ws(ws_ref)
        for ch in range(TR // CHUNK):
            rows = slice(ch * CHUNK, (ch + 1) * CHUNK)
            _, u, _, _, _, sg = _gating_forward(p_ref[rows, :].astype(F32), gv_ref[...], bv_ref[...], wt, bs_ref[...])
            oa = u * sg
            o_ref[rows, :] = (oa * _rsq_mean(oa) * ga_ref[...]).astype(BF16)

    return pl.pallas_call(
        body, grid=(s // TR,),
        in_specs=[_row_spec(2 * A_WIDTH), _vec_spec(A_WIDTH), _vec_spec(A_WIDTH),
                  pl.BlockSpec((A_GROUPS, CHUNK, CHUNK), lambda i: (0, 0, 0)),
                  pl.BlockSpec((CHUNK, A_WIDTH), lambda i: (0, 0)), _vec_spec(A_WIDTH)],
        out_specs=_row_spec(A_WIDTH), out_shape=jax.ShapeDtypeStruct((s, A_WIDTH + B_WIDTH), BF16),
        compiler_params=_cparams("parallel"), name=name)(proj, gv, bv, ws, bsf, ga)


def _mixer_a_bwd(proj, dmixed, gv, bv, ws, bsf, ga, name):
    s = proj.shape[0]
    nsteps = s // TR

    def body(p_ref, dm_ref, gv_ref, bv_ref, ws_ref, bs_ref, ga_ref,
             dp_ref, dga_ref, dgv_ref, dbv_ref, dbs_ref, dws_ref):
        _acc_init([dga_ref, dgv_ref, dbv_ref, dbs_ref, dws_ref])
        mask = _tril_mask()
        wt = _masked_ws(ws_ref)
        gvv = gv_ref[...]
        gav = ga_ref[...]
        for ch in range(TR // CHUNK):
            rows = slice(ch * CHUNK, (ch + 1) * CHUNK)
            pa = p_ref[rows, :].astype(F32)
            er, u, rs, vn, vlb, sg = _gating_forward(pa, gvv, bv_ref[...], wt, bs_ref[...])
            oa = u * sg
            doa, dgt = _rms_bwd(oa, _rsq_mean(oa), gav, dm_ref[rows, :])
            dga_ref[...] += _colsum(dgt)
            du = doa * sg
            dsg = doa * u
            dbs_ref[...] += dsg
            dsgb = dsg.astype(BF16)
            dvl = []
            for g in range(A_GROUPS):
                cols = slice(g * GROUP_DIM, (g + 1) * GROUP_DIM)
                dws_ref[g] += jnp.where(mask, _dot(dsgb[:, cols], vlb[:, cols], NT), 0.0)
                dvl.append(_dot(wt[g], dsgb[:, cols], TN))
            dvl = jnp.concatenate(dvl, axis=1)
            dgv_ref[...] += _colsum(dvl * vn)
            dbv_ref[...] += _colsum(dvl)
            dvn = dvl * gvv
            dva = rs * (dvn - jnp.mean(dvn, axis=-1, keepdims=True)
                        - vn * jnp.mean(dvn * vn, axis=-1, keepdims=True))
            gp = 0.5 * (1.0 + er) + pa * jnp.exp(-0.5 * pa * pa) * INV_SQRT_2PI
            dp_ref[rows, :] = (jnp.concatenate([du, dva], axis=1) * gp).astype(BF16)

        @pl.when(pl.program_id(0) == nsteps - 1)
        def _():
            for g in range(A_GROUPS):
                cols = slice(g * GROUP_DIM, (g + 1) * GROUP_DIM)
                tot = jnp.sum(dbs_ref[:, cols], axis=1, keepdims=True)
                dbs_ref[:, cols] = jnp.broadcast_to(tot, (CHUNK, GROUP_DIM))

    full = lambda *shape: pl.BlockSpec(shape, lambda i: (0,) * len(shape))
    return pl.pallas_call(
        body, grid=(nsteps,),
        in_specs=[_row_spec(2 * A_WIDTH), _row_spec(A_WIDTH), _vec_spec(A_WIDTH), _vec_spec(A_WIDTH),
                  full(A_GROUPS, CHUNK, CHUNK), full(CHUNK, A_WIDTH), _vec_spec(A_WIDTH)],
        out_specs=[_row_spec(2 * A_WIDTH), _vec_spec(A_WIDTH), _vec_spec(A_WIDTH), _vec_spec(A_WIDTH),
                   full(CHUNK, A_WIDTH), full(A_GROUPS, CHUNK, CHUNK)],
        out_shape=[jax.ShapeDtypeStruct((s, IN_COLS), BF16), jax.ShapeDtypeStruct((1, A_WIDTH), F32),
                   jax.ShapeDtypeStruct((1, A_WIDTH), F32), jax.ShapeDtypeStruct((1, A_WIDTH), F32),
                   jax.ShapeDtypeStruct((CHUNK, A_WIDTH), F32),
                   jax.ShapeDtypeStruct((A_GROUPS, CHUNK, CHUNK), F32)],
        compiler_params=_cparams("arbitrary"), name=name)(proj, dmixed, gv, bv, ws, bsf, ga)


def _rope_tables(s):
    half = ROT_DIM // 2
    lane = jnp.arange(LANES) % HEAD_DIM
    inv = ROPE_THETA ** (-(2 * (lane % half)).astype(F32) / ROT_DIM)
    ang = jnp.arange(s, dtype=F32)[:, None] * inv[None, :]
    cos, sin = jnp.cos(ang), jnp.sin(ang)
    c = jnp.where(lane < ROT_DIM, cos, 1.0)
    s1 = jnp.where(lane < half, -sin, 0.0)
    s2 = jnp.where((lane >= half) & (lane < ROT_DIM), sin, 0.0)
    return c, s1, s2


def _lane_blocks(width):
    return [slice(b * LANES, (b + 1) * LANES) for b in range(width // LANES)]


CLASS_DILS = tuple(d for d in DILATIONS if d > 1)


def _class_shape(s, dil, dtype):
    return jax.ShapeDtypeStruct((dil, s // dil, B_WIDTH), dtype)


def _class_spec(dil):
    return pl.BlockSpec((dil, TR // dil, B_WIDTH), lambda i, *_: (0, i, 0))


NBLK = B_WIDTH // LANES
STAGE = pltpu.VMEM((NBLK, TR, LANES), F32)


def _stage_put(stage, value):
    for b, sl in enumerate(_lane_blocks(B_WIDTH)):
        stage[b] = value[:, sl]


def _stage_get(stage):
    return jnp.concatenate([stage[b] for b in range(NBLK)], axis=1)


def _store_classes(stage, dst_ref, dil):
    for b, sl in enumerate(_lane_blocks(B_WIDTH)):
        for r in range(dil):
            dst_ref[r, :, sl] = stage[b, pl.ds(r, TR // dil, stride=dil), :].astype(dst_ref.dtype)


def _load_classes(src_ref, stage, dil):
    for b, sl in enumerate(_lane_blocks(B_WIDTH)):
        for r in range(dil):
            stage[b, pl.ds(r, TR // dil, stride=dil), :] = src_ref[r, :, sl].astype(F32)
    return _stage_get(stage)


def _rope_fwd(proj, tabs, name):
    s = proj.shape[0]
    half = ROT_DIM // 2
    scale = HEAD_DIM ** -0.5
    nlay = 1 + len(CLASS_DILS)

    def body(q_ref, k_ref, v_ref, c_ref, s1_ref, s2_ref, *rest):
        outs, stage = rest[:3 * nlay], rest[3 * nlay]
        c, s1, s2 = c_ref[...], s1_ref[...], s2_ref[...]
        for which, (src, mul) in enumerate(((q_ref, scale), (k_ref, 1.0), (v_ref, None))):
            if mul is None:
                _stage_put(stage, src[...].astype(F32))
            else:
                for b, sl in enumerate(_lane_blocks(B_WIDTH)):
                    a = src[:, sl].astype(F32)
                    r = a * c + pltpu.roll(a, LANES - half, 1) * s1 + pltpu.roll(a, half, 1) * s2
                    stage[b] = r * mul
            dst = outs[which * nlay:(which + 1) * nlay]
            dst[0][...] = _stage_get(stage).astype(BF16)
            for ref, d in zip(dst[1:], CLASS_DILS):
                _store_classes(stage, ref, d)

    tab = pl.BlockSpec((TR, LANES), lambda i: (i, 0))
    lay_specs = [_row_spec(B_WIDTH)] + [_class_spec(d) for d in CLASS_DILS]
    lay_shapes = [jax.ShapeDtypeStruct((s, B_WIDTH), BF16)] + [_class_shape(s, d, BF16) for d in CLASS_DILS]
    outs = pl.pallas_call(
        body, grid=(s // TR,),
        in_specs=[_row_spec(B_WIDTH, 2), _row_spec(B_WIDTH, 3), _row_spec(B_WIDTH, 4), tab, tab, tab],
        out_specs=lay_specs * 3, out_shape=lay_shapes * 3, scratch_shapes=[STAGE],
        compiler_params=_cparams("parallel"), name=name)(proj, proj, proj, *tabs)
    q, k, v = (dict(zip(DILATIONS, outs[w * nlay:(w + 1) * nlay])) for w in range(3))
    return q, k, v


def _as_classes(t):
    return t if t.ndim == 3 else t[None]


def _band_mask(i):
    qi = lax.broadcasted_iota(jnp.int32, (BAND, 2 * BAND), 0)
    kj = lax.broadcasted_iota(jnp.int32, (BAND, 2 * BAND), 1)
    return (kj >= qi) & (kj <= qi + BAND) & ((kj >= BAND) | (i > 0))


def _head_masks():
    lane = lax.broadcasted_iota(jnp.int32, (1, LANES), 1)
    return lane < HEAD_DIM, lane >= HEAD_DIM


def _stack_heads(t):
    lo, hi = _head_masks()
    zero = jnp.zeros_like(t)
    return jnp.concatenate([jnp.where(lo, t, zero), jnp.where(hi, t, zero)], axis=0)


def _attn_specs(last):
    cur = pl.BlockSpec((None, BAND, B_WIDTH), lambda r, i: (r, jnp.minimum(i, last), 0))
    prev = pl.BlockSpec((None, BAND, B_WIDTH), lambda r, i: (r, jnp.maximum(jnp.minimum(i, last) - 1, 0), 0))
    return cur, prev


def _attn_fwd(q, k, v, name, gather=None):
    dil, n, _ = q.shape
    nb = n // BAND
    ng = 0 if gather is None else len(gather)

    def body(*refs):
        q_ref, kc_ref, kp_ref, vc_ref, vp_ref = refs[:5]
        o_ref, l_ref = refs[5 + ng:7 + ng]
        if ng:
            start, relay, finish = _gather_steps(refs[5:5 + ng], refs[7 + ng:7 + 2 * ng], *refs[7 + 2 * ng:])
            first, last = _grid_edges((dil, nb))
            pl.when(first)(start)
        valid = _band_mask(pl.program_id(1))
        valid = jnp.concatenate([valid, valid], axis=0)
        lo, _ = _head_masks()
        for sl in _lane_blocks(B_WIDTH):
            kk = jnp.concatenate([kp_ref[:, sl], kc_ref[:, sl]], axis=0)
            vv = jnp.concatenate([vp_ref[:, sl], vc_ref[:, sl]], axis=0)
            sc = jnp.where(valid, _dot(_stack_heads(q_ref[:, sl]), kk, NT), NEG_INF)
            mx = jnp.max(sc, axis=1, keepdims=True)
            p = jnp.exp(sc - mx)
            den = jnp.sum(p, axis=1, keepdims=True)
            out = _dot(p.astype(BF16), vv, NN) / den
            lse = mx + jnp.log(den)
            o_ref[:, sl] = jnp.where(lo, out[:BAND], out[BAND:]).astype(BF16)
            l_ref[:, sl] = jnp.where(lo, lse[:BAND], lse[BAND:])

        if ng:
            @pl.when(last)
            def _():
                relay()
                finish()

    cur, prev = _attn_specs(nb - 1)
    sem = ("arbitrary", "arbitrary") if ng else ("parallel", "parallel")
    res = pl.pallas_call(
        body, grid=(dil, nb), in_specs=[cur, cur, prev, cur, prev] + [ANY] * ng, out_specs=[cur, cur] + [ANY] * ng,
        out_shape=[jax.ShapeDtypeStruct((dil, n, B_WIDTH), BF16), jax.ShapeDtypeStruct((dil, n, B_WIDTH), F32)]
        + _gathered_shapes(gather or []),
        scratch_shapes=_gather_sems(ng) if ng else [],
        compiler_params=_cparams(*sem), name=name)(q, k, k, v, v, *(gather or []))
    return res[0], res[1], list(res[2:])


MAX_CLASS_BLOCKS = 8


def _class_masks():
    qi = lax.broadcasted_iota(jnp.int32, (BAND, 2 * BAND), 0)
    kj = lax.broadcasted_iota(jnp.int32, (BAND, 2 * BAND), 1)
    both = (kj >= qi) & (kj <= qi + BAND)
    own = kj[:, :BAND] <= qi[:, :BAND]
    return jnp.concatenate([own, own], axis=0), jnp.concatenate([both, both], axis=0)


def _block_rows(g):
    return pl.ds(pl.multiple_of(g * BAND, BAND), BAND)


def _key_rows(g):
    return pl.ds(pl.multiple_of((g - 1) * BAND, BAND), 2 * BAND)


def _attn_fwd_class(q, k, v, name, gather=None):
    dil, n, _ = q.shape
    nb = n // BAND
    ng = 0 if gather is None else len(gather)

    def body(*refs):
        q_ref, k_ref, v_ref = refs[:3]
        o_ref, l_ref = refs[3 + ng:5 + ng]
        if ng:
            start, relay, finish = _gather_steps(refs[3:3 + ng], refs[5 + ng:5 + 2 * ng], *refs[5 + 2 * ng:])
            first, last = _grid_edges((dil,))
            pl.when(first)(start)
        own, both = _class_masks()
        lo, _ = _head_masks()

        def block(rows, keys, valid):
            for sl in _lane_blocks(B_WIDTH):
                sc = jnp.where(valid, _dot(_stack_heads(q_ref[rows, sl]), k_ref[keys, sl], NT), NEG_INF)
                mx = jnp.max(sc, axis=1, keepdims=True)
                p = jnp.exp(sc - mx)
                den = jnp.sum(p, axis=1, keepdims=True)
                out = _dot(p.astype(BF16), v_ref[keys, sl], NN) / den
                lse = mx + jnp.log(den)
                o_ref[rows, sl] = jnp.where(lo, out[:BAND], out[BAND:]).astype(BF16)
                l_ref[rows, sl] = jnp.where(lo, lse[:BAND], lse[BAND:])

        block(_block_rows(0), _block_rows(0), own)

        @pl.loop(1, nb)
        def _(g):
            block(_block_rows(g), _key_rows(g), both)

        if ng:
            @pl.when(last)
            def _():
                relay()
                finish()

    spec = pl.BlockSpec((None, n, B_WIDTH), lambda r: (r, 0, 0))
    res = pl.pallas_call(
        body, grid=(dil,), in_specs=[spec] * 3 + [ANY] * ng, out_specs=[spec, spec] + [ANY] * ng,
        out_shape=[jax.ShapeDtypeStruct((dil, n, B_WIDTH), BF16), jax.ShapeDtypeStruct((dil, n, B_WIDTH), F32)]
        + _gathered_shapes(gather or []),
        scratch_shapes=_gather_sems(ng) if ng else [],
        compiler_params=_cparams("arbitrary" if ng else "parallel"), name=name)(q, k, v, *(gather or []))
    return res[0], res[1], list(res[2:])


def _attn_bwd_class(q, k, v, do, lse, delta, name, scatter=None):
    dil, n, _ = q.shape
    nb = n // BAND
    ns = 0 if scatter is None else len(scatter[0])

    def body(*refs):
        q_ref, k_ref, v_ref, do_ref, lse_ref, dl_ref = refs[:6]
        dq_ref, dk_ref, dv_ref = refs[6 + ns:9 + ns]
        ck_ref, cv_ref = refs[9 + 2 * ns:11 + 2 * ns]
        if ns:
            start, finish = _scatter_steps(refs[6:6 + ns], refs[9 + ns:9 + 2 * ns], *refs[11 + 2 * ns:], scatter[1])
            first, last = _grid_edges((dil,))
            pl.when(first)(start)
        own, both = _class_masks()
        lo, _ = _head_masks()
        lane = lax.broadcasted_iota(jnp.int32, (1, LANES), 1)

        def per_head(t):
            return jnp.concatenate(
                [jnp.sum(jnp.where(lane == first, t, 0.0), axis=1, keepdims=True) for first in (0, HEAD_DIM)], axis=0)

        def grads(rows, keys, valid, sl):
            q2 = _stack_heads(q_ref[rows, sl])
            do2 = _stack_heads(do_ref[rows, sl])
            kk = k_ref[keys, sl]
            p = jnp.where(valid, jnp.exp(_dot(q2, kk, NT) - per_head(lse_ref[rows, sl])), 0.0)
            ds = (p * (_dot(do2, v_ref[keys, sl], NT) - per_head(dl_ref[rows, sl]))).astype(BF16)
            dq = _dot(ds, kk, NN)
            dq_ref[rows, sl] = jnp.where(lo, dq[:BAND], dq[BAND:]).astype(BF16)
            return _dot(ds, q2, TN), _dot(p.astype(BF16), do2, TN)

        for sl in _lane_blocks(B_WIDTH):
            ck_ref[:, sl], cv_ref[:, sl] = grads(_block_rows(0), _block_rows(0), own, sl)

        @pl.loop(1, nb)
        def _(g):
            before = _block_rows(g - 1)
            for sl in _lane_blocks(B_WIDTH):
                dkk, dvv = grads(_block_rows(g), _key_rows(g), both, sl)
                dk_ref[before, sl] = (ck_ref[:, sl] + dkk[:BAND]).astype(BF16)
                dv_ref[before, sl] = (cv_ref[:, sl] + dvv[:BAND]).astype(BF16)
                ck_ref[:, sl] = dkk[BAND:]
                cv_ref[:, sl] = dvv[BAND:]

        final = pl.ds((nb - 1) * BAND, BAND)
        dk_ref[final, :] = ck_ref[...].astype(BF16)
        dv_ref[final, :] = cv_ref[...].astype(BF16)

        if ns:
            pl.when(last)(finish)

    spec = pl.BlockSpec((None, n, B_WIDTH), lambda r: (r, 0, 0))
    shape = jax.ShapeDtypeStruct((dil, n, B_WIDTH), BF16)
    res = pl.pallas_call(
        body, grid=(dil,), in_specs=[spec] * 6 + [ANY] * ns, out_specs=[spec] * 3 + [ANY] * ns,
        out_shape=[shape] * 3 + (_scattered_shapes(scatter[1]) if ns else []),
        scratch_shapes=[pltpu.VMEM((BAND, B_WIDTH), F32)] * 2 + (_scatter_sems(ns) if ns else []),
        compiler_params=_cparams("arbitrary" if ns else "parallel"), name=name)(q, k, v, do, lse, delta,
                                                                             *(scatter[0] if ns else []))
    return res[0], res[1], res[2], list(res[3:])


def _attn_combine(outs, lses, gb, mixed, name):
    s = mixed.shape[0]
    npat = len(DILATIONS)
    w = B_WIDTH

    def body(*refs):
        o_refs, l_refs = refs[:npat], refs[npat:2 * npat]
        g_ref, _, ob_ref = refs[2 * npat:2 * npat + 3]
        lse_refs = refs[2 * npat + 3:3 * npat + 3]
        mb_ref, stage = refs[3 * npat + 3:]
        os_ = [o_refs[0][...].astype(F32)] + [_load_classes(r, stage, d) for r, d in zip(o_refs[1:], CLASS_DILS)]
        ls = [l_refs[0][...]] + [_load_classes(r, stage, d) for r, d in zip(l_refs[1:], CLASS_DILS)]
        mx = functools.reduce(jnp.maximum, ls)
        ws = [jnp.exp(l - mx) for l in ls]
        tot = functools.reduce(lambda a, b: a + b, ws)
        ob = functools.reduce(lambda a, b: a + b, [wt / tot * o for wt, o in zip(ws, os_)])
        ob_ref[...] = ob
        lse = mx + jnp.log(tot)
        _stage_put(stage, lse)
        lse_refs[0][...] = lse
        for ref, d in zip(lse_refs[1:], CLASS_DILS):
            _store_classes(stage, ref, d)
        mb_ref[...] = (ob * _rsq_mean(ob) * g_ref[...]).astype(BF16)

    lay_specs = [_row_spec(w)] + [_class_spec(d) for d in CLASS_DILS]
    res = pl.pallas_call(
        body, grid=(s // TR,), in_specs=lay_specs * 2 + [_vec_spec(w), ANY],
        out_specs=[_row_spec(w)] + lay_specs + [_row_spec(w, 1)],
        out_shape=[jax.ShapeDtypeStruct((s, w), F32), jax.ShapeDtypeStruct((s, w), F32)]
        + [_class_shape(s, d, F32) for d in CLASS_DILS] + [jax.ShapeDtypeStruct(mixed.shape, mixed.dtype)],
        scratch_shapes=[STAGE], input_output_aliases={2 * npat + 1: npat + 1},
        compiler_params=_cparams("parallel"), name=name)(*outs, *lses, gb, mixed)
    return res[0], dict(zip(DILATIONS, res[1:npat + 1])), res[npat + 1]


def _attn_bwd_prep(dmixed, ob, gb, name):
    s = ob.shape[0]
    w = B_WIDTH
    nlay = len(DILATIONS)

    def body(dm_ref, ob_ref, g_ref, *rest):
        do_refs, dl_refs = rest[:nlay], rest[nlay:2 * nlay]
        dg_ref, stage = rest[2 * nlay:]
        _acc_init([dg_ref])
        ob = ob_ref[...]
        dob, dgt = _rms_bwd(ob, _rsq_mean(ob), g_ref[...], dm_ref[...])
        dg_ref[...] += _colsum(dgt)
        _stage_put(stage, dob)
        do_refs[0][...] = dob.astype(BF16)
        for ref, d in zip(do_refs[1:], CLASS_DILS):
            _store_classes(stage, ref, d)
        lo, hi = _head_masks()
        t = dob * ob
        for b, sl in enumerate(_lane_blocks(w)):
            tb = t[:, sl]
            s0 = jnp.sum(jnp.where(lo, tb, 0.0), axis=1, keepdims=True)
            s1 = jnp.sum(jnp.where(hi, tb, 0.0), axis=1, keepdims=True)
            stage[b] = jnp.where(lo, s0, s1)
        dl_refs[0][...] = _stage_get(stage)
        for ref, d in zip(dl_refs[1:], CLASS_DILS):
            _store_classes(stage, ref, d)

    lay_specs = [_row_spec(w)] + [_class_spec(d) for d in CLASS_DILS]
    shapes = lambda dt: [jax.ShapeDtypeStruct((s, w), dt)] + [_class_shape(s, d, dt) for d in CLASS_DILS]
    res = pl.pallas_call(
        body, grid=(s // TR,), in_specs=[_row_spec(w, 1), _row_spec(w), _vec_spec(w)],
        out_specs=lay_specs * 2 + [_vec_spec(w)],
        out_shape=shapes(BF16) + shapes(F32) + [jax.ShapeDtypeStruct((1, w), F32)],
        scratch_shapes=[STAGE],
        compiler_params=_cparams("arbitrary"), name=name)(dmixed, ob, gb)
    return dict(zip(DILATIONS, res[:nlay])), dict(zip(DILATIONS, res[nlay:2 * nlay])), res[2 * nlay]


def _attn_bwd(q, k, v, do, lse, delta, name, scatter=None):
    dil, n, _ = q.shape
    nb = n // BAND
    ns = 0 if scatter is None else len(scatter[0])

    def body(*refs):
        q_ref, kc_ref, kp_ref, vc_ref, vp_ref, do_ref, lse_ref, dl_ref = refs[:8]
        dq_ref, dk_ref, dv_ref = refs[8 + ns:11 + ns]
        ck_ref, cv_ref = refs[11 + 2 * ns:13 + 2 * ns]
        i = pl.program_id(1)
        if ns:
            start, finish = _scatter_steps(refs[8:8 + ns], refs[11 + ns:11 + 2 * ns], *refs[13 + 2 * ns:],
                                           scatter[1])
            first, last = _grid_edges((dil, nb + 1))
            pl.when(first)(start)

        @pl.when(i == 0)
        def _():
            ck_ref[...] = jnp.zeros_like(ck_ref)
            cv_ref[...] = jnp.zeros_like(cv_ref)

        @pl.when(i < nb)
        def _():
            valid = _band_mask(i)
            valid = jnp.concatenate([valid, valid], axis=0)
            lo, _ = _head_masks()
            lane = lax.broadcasted_iota(jnp.int32, (1, LANES), 1)

            def per_head(t):
                return jnp.concatenate(
                    [jnp.sum(jnp.where(lane == first, t, 0.0), axis=1, keepdims=True) for first in (0, HEAD_DIM)], axis=0)

            for sl in _lane_blocks(B_WIDTH):
                q2 = _stack_heads(q_ref[:, sl])
                do2 = _stack_heads(do_ref[:, sl])
                kk = jnp.concatenate([kp_ref[:, sl], kc_ref[:, sl]], axis=0)
                vv = jnp.concatenate([vp_ref[:, sl], vc_ref[:, sl]], axis=0)
                p = jnp.where(valid, jnp.exp(_dot(q2, kk, NT) - per_head(lse_ref[:, sl])), 0.0)
                ds = (p * (_dot(do2, vv, NT) - per_head(dl_ref[:, sl]))).astype(BF16)
                dq = _dot(ds, kk, NN)
                dkk = _dot(ds, q2, TN)
                dvv = _dot(p.astype(BF16), do2, TN)
                dq_ref[:, sl] = jnp.where(lo, dq[:BAND], dq[BAND:]).astype(BF16)
                dk_ref[:, sl] = (ck_ref[:, sl] + dkk[:BAND]).astype(BF16)
                dv_ref[:, sl] = (cv_ref[:, sl] + dvv[:BAND]).astype(BF16)
                ck_ref[:, sl] = dkk[BAND:]
                cv_ref[:, sl] = dvv[BAND:]

        @pl.when(i == nb)
        def _():
            dk_ref[...] = ck_ref[...].astype(BF16)
            dv_ref[...] = cv_ref[...].astype(BF16)

        if ns:
            pl.when(last)(finish)

    cur, prev = _attn_specs(nb - 1)
    lag = pl.BlockSpec((None, BAND, B_WIDTH), lambda r, i: (r, jnp.maximum(i - 1, 0), 0))
    shape = jax.ShapeDtypeStruct((dil, n, B_WIDTH), BF16)
    res = pl.pallas_call(
        body, grid=(dil, nb + 1), in_specs=[cur, cur, prev, cur, prev, cur, cur, cur] + [ANY] * ns,
        out_specs=[cur, lag, lag] + [ANY] * ns,
        out_shape=[shape] * 3 + (_scattered_shapes(scatter[1]) if ns else []),
        scratch_shapes=[pltpu.VMEM((BAND, B_WIDTH), F32)] * 2 + (_scatter_sems(ns) if ns else []),
        compiler_params=_cparams("arbitrary", "arbitrary"), name=name)(q, k, k, v, v, do, lse, delta,
                                                                      *(scatter[0] if ns else []))
    return res[0], res[1], res[2], list(res[3:])


def _rope_bwd(dqs, dks, dvs, tabs, dproj, name):
    s = dproj.shape[0]
    half = ROT_DIM // 2
    scale = HEAD_DIM ** -0.5
    npat = len(DILATIONS)
    w = B_WIDTH

    def body(*refs):
        groups = [refs[g * npat:(g + 1) * npat] for g in range(3)]
        c_ref, s1_ref, s2_ref, _, o_ref, stage = refs[3 * npat:]

        def total(rs):
            acc = rs[0][...].astype(F32)
            for ref, d in zip(rs[1:], CLASS_DILS):
                acc = acc + _load_classes(ref, stage, d)
            return acc

        def unrope(g):
            c, s1, s2 = c_ref[...], s1_ref[...], s2_ref[...]
            for sl in _lane_blocks(w):
                gb = g[:, sl]
                o = gb * c + pltpu.roll(gb * s1, half, 1) + pltpu.roll(gb * s2, LANES - half, 1)
                o_ref[:, sl] = o.astype(BF16)

        which = pl.program_id(1)

        @pl.when(which == 0)
        def _():
            unrope(total(groups[0]) * scale)

        @pl.when(which == 1)
        def _():
            unrope(total(groups[1]))

        @pl.when(which == 2)
        def _():
            o_ref[...] = total(groups[2]).astype(BF16)

    tab = pl.BlockSpec((TR, LANES), lambda i, j: (i, 0))
    nat = pl.BlockSpec((TR, w), lambda i, j: (i, 0))
    lay_specs = [nat] + [_class_spec(d) for d in CLASS_DILS]
    first_col = 2 * A_WIDTH // w
    return pl.pallas_call(
        body, grid=(s // TR, 3), in_specs=lay_specs * 3 + [tab] * 3 + [ANY],
        out_specs=pl.BlockSpec((TR, w), lambda i, j: (i, first_col + j)),
        out_shape=jax.ShapeDtypeStruct(dproj.shape, dproj.dtype), scratch_shapes=[STAGE],
        input_output_aliases={3 * npat + 3: 0},
        compiler_params=_cparams("parallel", "arbitrary"), name=name)(*dqs, *dks, *dvs, *tabs, dproj)


TK = 512
HALO = 16


def _row_of(v, r):
    rows = lax.broadcasted_iota(jnp.int32, (v.shape[0], 1), 0)
    return jnp.sum(jnp.where(rows == r, v, 0.0), axis=0, keepdims=True)


def _taps_before(x, halo):
    row = lax.broadcasted_iota(jnp.int32, (x.shape[0], 1), 0)
    m1 = jnp.where(row == 0, _row_of(halo, HALO - 1), pltpu.roll(x, 1, 0))
    m2 = jnp.where(row == 0, _row_of(halo, HALO - 2), jnp.where(row == 1, _row_of(halo, HALO - 1), pltpu.roll(x, 2, 0)))
    return m2, m1, x


def _taps_after(x, halo):
    rows = x.shape[0]
    row = lax.broadcasted_iota(jnp.int32, (rows, 1), 0)
    p1 = jnp.where(row == rows - 1, _row_of(halo, 0), pltpu.roll(x, rows - 1, 0))
    p2 = jnp.where(row == rows - 2, _row_of(halo, 0), jnp.where(row == rows - 1, _row_of(halo, 1), pltpu.roll(x, rows - 2, 0)))
    return p1, p2


def _conv_value(taps, cw_ref, cb_ref, h):
    return cb_ref[h] + cw_ref[h, 0:1, :] * taps[0] + cw_ref[h, 1:2, :] * taps[1] + cw_ref[h, 2:3, :] * taps[2]


def _ffn_weight_specs(ncol):
    per_up = (2 * D_FF // N_CHIPS) // TK
    per_dn = (D_FF // N_CHIPS) // TK
    wg = pl.BlockSpec((None, None, D_MODEL, TK), lambda i, j: (j // per_up, 0, 0, j % per_up))
    wv = pl.BlockSpec((None, None, D_MODEL, TK), lambda i, j: ((j + ncol) // per_up, 0, 0, (j + ncol) % per_up))
    wd = pl.BlockSpec((None, None, TK, D_MODEL), lambda i, j: (j // per_dn, 0, j % per_dn, 0))
    cw = pl.BlockSpec((2, 3, TK), lambda i, j: (0, 0, j))
    cb = pl.BlockSpec((2, 1, TK), lambda i, j: (0, 0, j))
    return wg, wv, wd, cw, cb


def _ffn_forward(h2, w_up, w_down, cw3, cb3, name, gather=None, post=None):
    s = h2.shape[0]
    nm, ncol = s // TM, D_FF // TK
    ng = 0 if gather is None else len(gather)
    npost = 0 if post is None else 3
    nout = 4 + (2 if post else 0)

    def body(*refs):
        h_ref, wg_ref, wv_ref, wd_ref, cw_ref, cb_ref = refs[:6]
        post_in = refs[6:6 + npost]
        g_in = refs[6 + npost:6 + npost + ng]
        outs = refs[6 + npost + ng:6 + npost + ng + nout]
        y_ref, up_ref, cv_ref, f_ref = outs[:4]
        g_out = refs[6 + npost + ng + nout:6 + npost + 2 * ng + nout]
        carry, acc = refs[6 + npost + 2 * ng + nout:8 + npost + 2 * ng + nout]
        i, j = pl.program_id(0), pl.program_id(1)
        if ng:
            start, relay, finish = _gather_steps(g_in, g_out, *refs[8 + npost + 2 * ng + nout:])
            pl.when((i == 0) & (j == 0))(start)
            pl.when((i == nm - 1) & (j == 0))(relay)

        @pl.when((i == 0) & (j == 0))
        def _():
            carry[...] = jnp.zeros_like(carry)

        h = h_ref[...]
        conv = []
        for hh, w_ref in ((0, wg_ref), (1, wv_ref)):
            up = _dot(h, w_ref[...], NN).astype(BF16)
            up_ref[hh] = up
            x = up.astype(F32)
            conv.append(_conv_value(_taps_before(x, carry[j, hh]), cw_ref, cb_ref, hh))
            cv_ref[hh] = conv[hh].astype(BF16)
            carry[j, hh] = x[TM - HALO:, :]
        y = (_gelu_tanh(conv[0])[0] * conv[1]).astype(BF16)
        y_ref[...] = y
        part = _dot(y, wd_ref[...], NN)

        @pl.when(j == 0)
        def _():
            acc[...] = part

        @pl.when(j > 0)
        def _():
            acc[...] += part

        @pl.when(j == ncol - 1)
        def _():
            f = acc[...]
            f_ref[...] = f
            if post:
                x1_ref, gp_ref, gn_ref = post_in
                x2 = x1_ref[...] + f * _rsq_mean(f) * gp_ref[...]
                outs[4][...] = x2
                outs[5][...] = (x2 * _rsq_mean(x2) * gn_ref[...]).astype(BF16)

        if ng:
            pl.when((i == nm - 1) & (j == ncol - 1))(finish)

    wg, wv, wd, cw, cb = _ffn_weight_specs(ncol)
    row = pl.BlockSpec((TM, D_MODEL), lambda i, j: (i, 0))
    vec = pl.BlockSpec((1, D_MODEL), lambda i, j: (0, 0))
    res = pl.pallas_call(
        body, grid=(nm, ncol),
        in_specs=[row, wg, wv, wd, cw, cb] + ([row, vec, vec] if post else []) + [ANY] * ng,
        out_specs=[pl.BlockSpec((TM, TK), lambda i, j: (i, j)), pl.BlockSpec((2, TM, TK), lambda i, j: (0, i, j)),
                   pl.BlockSpec((2, TM, TK), lambda i, j: (0, i, j)), row] + ([row, row] if post else [])
        + [ANY] * ng,
        out_shape=[jax.ShapeDtypeStruct((s, D_FF), BF16), jax.ShapeDtypeStruct((2, s, D_FF), BF16),
                   jax.ShapeDtypeStruct((2, s, D_FF), BF16), jax.ShapeDtypeStruct((s, D_MODEL), F32)]
        + ([jax.ShapeDtypeStruct((s, D_MODEL), F32), jax.ShapeDtypeStruct((s, D_MODEL), BF16)] if post else [])
        + _gathered_shapes(gather or []),
        scratch_shapes=[pltpu.VMEM((ncol, 2, HALO, TK), F32), pltpu.VMEM((TM, D_MODEL), F32)]
        + (_gather_sems(ng) if ng else []),
        compiler_params=_cparams("arbitrary", "arbitrary"), name=name)(h2, w_up, w_up, w_down, cw3, cb3,
                                                                      *(post or []), *(gather or []))
    return res[:nout], list(res[nout:])


def _ffn_backward(df, w_up, w_down, up3, cv3, cw3, name, scatter=None):
    s = df.shape[0]
    nm, ncol = s // TM, D_FF // TK
    ns = 0 if scatter is None else len(scatter[0])

    def body(*refs):
        df_ref, wg_ref, wv_ref, wd_ref, cw_ref, up_ref, cv_ref = refs[:7]
        s_in = refs[7:7 + ns]
        dup_ref, dh_ref, sums_ref = refs[7 + ns:10 + ns]
        s_out = refs[10 + ns:10 + 2 * ns]
        carry, acc = refs[10 + 2 * ns:12 + 2 * ns]
        i, j = pl.program_id(0), pl.program_id(1)
        if ns:
            start, finish = _scatter_steps(s_in, s_out, *refs[12 + 2 * ns:], scatter[1])
            pl.when((i == 0) & (j == 0))(start)

        @pl.when((i == 0) & (j == 0))
        def _():
            carry[...] = jnp.zeros_like(carry)
            sums_ref[...] = jnp.zeros_like(sums_ref)

        dy = _dot(df_ref[...], wd_ref[...], NT)
        act, grad = _gelu_tanh(cv_ref[0].astype(F32))
        dcs = (dy * cv_ref[1].astype(F32) * grad, dy * act)
        row = lax.broadcasted_iota(jnp.int32, (8, 1), 0)
        part = None
        for hh, w_ref in ((0, wg_ref), (1, wv_ref)):
            dc = dcs[hh]
            x = up_ref[hh].astype(F32)
            after1, after2 = _taps_after(dc, carry[j, hh])
            upd = jnp.zeros((8, TK), F32)
            for ridx, sm in enumerate((_colsum(after2 * x), _colsum(after1 * x), _colsum(dc * x), _colsum(dc))):
                upd = jnp.where(row == ridx, sm, upd)
            sums_ref[j, hh] += upd
            dup = (cw_ref[hh, 2:3, :] * dc + cw_ref[hh, 1:2, :] * after1 + cw_ref[hh, 0:1, :] * after2).astype(BF16)
            carry[j, hh] = dc[:HALO, :]
            dup_ref[hh] = dup
            d = _dot(dup, w_ref[...], NT)
            part = d if part is None else part + d

        @pl.when(j == 0)
        def _():
            acc[...] = part

        @pl.when(j > 0)
        def _():
            acc[...] += part

        @pl.when(j == ncol - 1)
        def _():
            dh_ref[...] = acc[...]

        if ns:
            pl.when((i == nm - 1) & (j == ncol - 1))(finish)

    wg, wv, wd, cw, _ = _ffn_weight_specs(ncol)
    rev = lambda i: nm - 1 - i
    res = pl.pallas_call(
        body, grid=(nm, ncol),
        in_specs=[pl.BlockSpec((TM, D_MODEL), lambda i, j: (rev(i), 0)), wg, wv, wd, cw,
                  pl.BlockSpec((2, TM, TK), lambda i, j: (0, rev(i), j)),
                  pl.BlockSpec((2, TM, TK), lambda i, j: (0, rev(i), j))] + [ANY] * ns,
        out_specs=[pl.BlockSpec((2, TM, TK), lambda i, j: (0, rev(i), j)),
                   pl.BlockSpec((TM, D_MODEL), lambda i, j: (rev(i), 0)),
                   pl.BlockSpec((ncol, 2, 8, TK), lambda i, j: (0, 0, 0, 0))] + [ANY] * ns,
        out_shape=[jax.ShapeDtypeStruct((2, s, D_FF), BF16), jax.ShapeDtypeStruct((s, D_MODEL), F32),
                   jax.ShapeDtypeStruct((ncol, 2, 8, TK), F32)] + (_scattered_shapes(scatter[1]) if ns else []),
        scratch_shapes=[pltpu.VMEM((ncol, 2, HALO, TK), F32), pltpu.VMEM((TM, D_MODEL), F32)]
        + (_scatter_sems(ns) if ns else []),
        compiler_params=_cparams("arbitrary", "arbitrary"), name=name)(df, w_up, w_up, w_down, cw3, up3, cv3,
                                                                      *(scatter[0] if ns else []))
    return res[:3], list(res[3:])


def _wspec(rows, cols, index_map):
    return pl.BlockSpec((None, None, rows, cols), index_map)


def _layer_forward(l, x0, h1, p, wg, tabs, gather=None, late=None, g_next=None):
    s = x0.shape[0]
    nm = s // TMM
    tag = f"_l{l}"
    riders = dict.fromkeys(DILATIONS)
    proj_rider = None
    if late is not None:
        halves = lambda t: (t[:, :t.shape[1] // 2], t[:, t.shape[1] // 2:])
        (down_a, down_b), (up_a, up_b) = halves(late["w_down"]), halves(late["w_up"])
        proj_rider = [late["w_out"], down_a]
        riders = dict(zip(DILATIONS, ([down_b], [up_a], [up_b])))
    proj = _matmul(
        h1, wg["w_in"], grid=(nm, N_CHIPS), a_spec=pl.BlockSpec((TMM, D_MODEL), lambda i, j: (i, 0)),
        b_spec=_wspec(D_MODEL, IN_COLS // N_CHIPS, lambda i, j: (j, 0, 0, 0)),
        o_spec=pl.BlockSpec((TMM, IN_COLS // N_CHIPS), lambda i, j: (i, j)), o_shape=(s, IN_COLS), o_dtype=BF16,
        dims=NN, nk=1, kaxis=None, acc_shape=None, name="proj" + tag, gather=proj_rider)
    if late is not None:
        proj, (w_out_all4, down_a) = proj
    ma = _mixer_a_fwd(proj, p["v_norm_g"], p["v_norm_b"], p["w_spatial"], p["bs_full"], p["out_norm_a"],
                      "mixer_a_fwd" + tag)
    q, k, v = _rope_fwd(proj, tabs, "rope_fwd" + tag)
    whole = lambda d: s // d // BAND <= MAX_CLASS_BLOCKS
    outs, lses, landed = zip(*[
        (_attn_fwd_class if whole(d) else _attn_fwd)(
            _as_classes(q[d]), _as_classes(k[d]), _as_classes(v[d]), f"attn_fwd_d{d}" + tag, riders[d])
        for d in DILATIONS])
    if late is not None:
        wg = dict(wg, w_out=w_out_all4, w_down=jnp.concatenate([down_a, landed[0][0]], axis=-1),
                  w_up=jnp.concatenate([landed[1][0], landed[2][0]], axis=-1))
    outs = [o.reshape(s, B_WIDTH) if d == 1 else o for o, d in zip(outs, DILATIONS)]
    lses = [t.reshape(s, B_WIDTH) if d == 1 else t for t, d in zip(lses, DILATIONS)]
    ob, lse, mixed = _attn_combine(outs, lses, p["out_norm_b"], ma, "attn_combine" + tag)
    y1, x1, h2 = _mix_out_norm(mixed, wg["w_out"], x0, p["post_mix_norm"], p["pre_ffn_norm"], "mix_out" + tag)
    post = None if g_next is None else (x1, p["post_ffn_norm"], g_next)
    (y, up3, cv3, f, *after), gathered = _ffn_forward(h2, wg["w_up"], wg["w_down"], p["cw3"], p["cb3"],
                                                      "ffn_fwd" + tag, gather, post)
    saved = dict(x0=x0, h1=h1, proj=proj, q=q, k=k, v=v, ob=ob, lse=lse, mixed=mixed, y1=y1, x1=x1, h2=h2,
                 up3=up3, cv3=cv3, y=y, f=f)
    if after:
        saved.update(x2=after[0], h_next=after[1])
    return saved, gathered, wg


def _layer_backward(l, dx2, df, sv, p, wg, tabs, pos, scatter=None, hide=False):
    s = dx2.shape[0]
    nm = s // TMM
    tag = f"_l{l}"
    g = {}
    (dup3, dh2, conv_sums), scattered = _ffn_backward(df, wg["w_up"], wg["w_down"], sv["up3"], sv["cv3"], p["cw3"],
                                                      "ffn_bwd" + tag, scatter)
    sums = conv_sums.transpose(1, 2, 0, 3).reshape(2, 8, D_FF)
    g["conv_w"] = jnp.concatenate([sums[0, :3], sums[1, :3]], axis=1)
    g["conv_b"] = jnp.concatenate([sums[0, 3:4], sums[1, 3:4]], axis=1)
    tn = 1024
    done = {}
    gw_down = _matmul(
        sv["y"], df, grid=(D_FF // tn,), a_spec=pl.BlockSpec((s, tn), lambda k: (0, k)),
        b_spec=pl.BlockSpec((s, D_MODEL), lambda k: (0, 0)),
        o_spec=pl.BlockSpec((2, tn, D_MODEL // 2), lambda k: (0, k, 0)),
        o_shape=(2, D_FF, D_MODEL // 2), o_dtype=BF16,
        dims=TN, nk=1, kaxis=None, acc_shape=None, name="w_down_grad" + tag, halves=True)
    down_sums = _chip_sums(l, dict(w_down=gw_down), pos, ("w_down",)) if hide else None
    gw_up = _matmul(
        sv["h2"], dup3, grid=(2 * D_FF // tn,), a_spec=pl.BlockSpec((s, D_MODEL), lambda n: (0, 0)),
        b_spec=pl.BlockSpec((None, s, tn), lambda n: (n // (D_FF // tn), 0, n % (D_FF // tn))),
        o_spec=pl.BlockSpec((None, D_MODEL, tn), lambda n: (n // 2, 0, n % 2)),
        o_shape=(N_CHIPS, D_MODEL, 2 * D_FF // N_CHIPS), o_dtype=BF16,
        dims=TN, nk=1, kaxis=None, acc_shape=None, name="w_up_grad" + tag,
        scatter=(down_sums, ("w_down",)) if hide else None)
    up_sums = None
    if hide:
        gw_up, received = gw_up
        done[("w_down",)] = (down_sums, received)
        up_sums = _chip_sums(l, dict(w_up=gw_up), pos, ("w_up",))
    dx1, dy1, g["pre_ffn_norm"], g["post_mix_norm"] = _norm_bwd_mid(
        dx2, dh2, sv["x1"], p["pre_ffn_norm"], sv["y1"], p["post_mix_norm"], "norm_bwd_mid" + tag)
    w_out_all = pl.BlockSpec((N_CHIPS, None, D_MODEL // N_CHIPS, D_MODEL), lambda i: (0, 0, 0, 0))
    dmixed = _matmul(
        dy1, wg["w_out"], grid=(nm,), a_spec=pl.BlockSpec((TMM, D_MODEL), lambda i: (i, 0)), b_spec=w_out_all,
        o_spec=pl.BlockSpec((TMM, D_MODEL), lambda i: (i, 0)), o_shape=(s, D_MODEL), o_dtype=F32,
        dims=NT, nk=1, kaxis=None, acc_shape=None, name="mix_out_bwd" + tag, b_2d=(D_MODEL, D_MODEL))
    gw_out = _matmul(
        sv["mixed"], dy1, grid=(1,), a_spec=pl.BlockSpec((s, D_MODEL), lambda m: (0, 0)),
        b_spec=pl.BlockSpec((s, D_MODEL), lambda m: (0, 0)),
        o_spec=pl.BlockSpec((2, D_MODEL, D_MODEL // 2), lambda m: (0, 0, 0)),
        o_shape=(2, D_MODEL, D_MODEL // 2), o_dtype=BF16,
        dims=TN, nk=1, kaxis=None, acc_shape=None, name="w_out_grad" + tag, halves=True)
    dpa, g["out_norm_a"], g["v_norm_g"], g["v_norm_b"], dbs, g["w_spatial"] = _mixer_a_bwd(
        sv["proj"], dmixed, p["v_norm_g"], p["v_norm_b"], p["w_spatial"], p["bs_full"], p["out_norm_a"],
        "mixer_a_bwd" + tag)
    g["b_spatial"] = dbs[:, ::GROUP_DIM].T
    dob, delta, g["out_norm_b"] = _attn_bwd_prep(dmixed, sv["ob"], p["out_norm_b"], "attn_bwd_prep" + tag)
    last_dil = DILATIONS[-1]
    whole = lambda d: s // d // BAND <= MAX_CLASS_BLOCKS
    dqs, dks, dvs, received = zip(*[
        (_attn_bwd_class if whole(d) else _attn_bwd)(
            *(_as_classes(t[d]) for t in (sv["q"], sv["k"], sv["v"], dob, sv["lse"], delta)),
            f"attn_bwd_d{d}" + tag, (up_sums, ("w_up",)) if hide and d == last_dil else None)
        for d in DILATIONS])
    if hide:
        done[("w_up",)] = (up_sums, received[-1])
    nat = lambda ts: [t.reshape(s, B_WIDTH) if d == 1 else t for t, d in zip(ts, DILATIONS)]
    dproj = _rope_bwd(nat(dqs), nat(dks), nat(dvs), tabs, dpa, "rope_bwd" + tag)
    wcol = IN_COLS // N_CHIPS
    dh1 = _matmul(
        dproj, wg["w_in"], grid=(nm, N_CHIPS), a_spec=pl.BlockSpec((TMM, wcol), lambda i, n: (i, n)),
        b_spec=_wspec(D_MODEL, wcol, lambda i, n: (n, 0, 0, 0)),
        o_spec=pl.BlockSpec((TMM, D_MODEL), lambda i, n: (i, 0)), o_shape=(s, D_MODEL), o_dtype=F32,
        dims=NT, nk=N_CHIPS, kaxis=1, acc_shape=(TMM, D_MODEL), name="proj_bwd" + tag)
    gw_in = _matmul(
        sv["h1"], dproj, grid=(N_CHIPS,), a_spec=pl.BlockSpec((s, D_MODEL), lambda n: (0, 0)),
        b_spec=pl.BlockSpec((s, wcol), lambda n: (0, n)),
        o_spec=pl.BlockSpec((None, D_MODEL, wcol), lambda n: (n, 0, 0)),
        o_shape=(N_CHIPS, D_MODEL, wcol), o_dtype=BF16,
        dims=TN, nk=1, kaxis=None, acc_shape=None, name="w_in_grad" + tag)
    big = dict(w_in=gw_in, w_out=gw_out) if hide else dict(w_in=gw_in, w_up=gw_up, w_out=gw_out, w_down=gw_down)
    return dx1, dh1, big, g, scattered, done


SMALL = ("pre_mix_norm", "v_norm_g", "v_norm_b", "w_spatial", "b_spatial", "out_norm_a", "out_norm_b",
         "post_mix_norm", "pre_ffn_norm", "conv_b", "post_ffn_norm")
BIG = ("w_in", "w_out", "w_up", "w_down")
DEPTH = 2


def _layer_params(l, small, conv_w_full):
    p = {n: small[n][l].reshape(1, -1) for n in SMALL if n not in ("w_spatial", "b_spatial")}
    p["w_spatial"] = small["w_spatial"][l]
    p["bs_full"] = jnp.repeat(small["b_spatial"][l].T, GROUP_DIM, axis=1)
    p["cw3"] = conv_w_full[l].reshape(3, 2, D_FF).transpose(1, 0, 2)
    p["cb3"] = small["conv_b"][l].reshape(2, 1, D_FF)
    return p


def _mesh_pos():
    return lax.axis_index("x"), lax.axis_index("y"), lax.axis_index("c")


def _other_chips(x, y):
    return [(1 - x, y), (x, 1 - y), (1 - x, 1 - y)]


def _gathered_shapes(blocks):
    return [jax.ShapeDtypeStruct((N_CHIPS, 1) + a.shape, a.dtype) for a in blocks]


def _gather_sems(nw):
    n = 2 * nw * (N_CHIPS - 1) + nw
    return [pltpu.SemaphoreType.DMA((n,)), pltpu.SemaphoreType.DMA((n,))]


def _gather_steps(ins, outs, send, recv):
    nw, nrel = len(ins), N_CHIPS - 1
    x, y, c = _mesh_pos()
    mine, sibling, chips = 2 * x + y, (x, y, 1 - c), _other_chips(x, y)

    def copy(src, dst, slot, to):
        return pltpu.make_async_remote_copy(src_ref=src, dst_ref=dst, send_sem=send.at[slot],
                                            recv_sem=recv.at[slot], device_id=to, device_id_type=MESH)

    def half_rows(t, core):
        rows = ins[t].shape[0] // 2
        return pl.ds(pl.multiple_of(core * rows, rows), rows)

    def landing(t, chip, core):
        return outs[t].at[chip, 0, half_rows(t, core), :]

    slots = [(t, r, chip) for t in range(nw) for r, chip in enumerate(chips)]
    own = [copy(ins[t], outs[t].at[mine, 0], 2 * nw * nrel + t, sibling) for t in range(nw)]
    first = [copy(ins[t].at[half_rows(t, c), :], landing(t, mine, c), t * nrel + r, (px, py, c))
             for t, r, (px, py) in slots]
    relays = [copy(landing(t, 2 * px + py, c), landing(t, 2 * px + py, c), nw * nrel + t * nrel + r, sibling)
              for t, r, (px, py) in slots]

    def start():
        for cp in own + first:
            cp.start()

    def relay():
        for (t, r, (px, py)), cp in zip(slots, relays):
            copy(landing(t, 2 * px + py, c), landing(t, 2 * px + py, c), t * nrel + r, (px, py, c)).wait_recv()
            cp.start()

    def finish():
        for t, r, (px, py) in slots:
            passed = landing(t, 2 * px + py, 1 - c)
            copy(passed, passed, nw * nrel + t * nrel + r, sibling).wait_recv()
        for cp in first + relays:
            cp.wait_send()
        for cp in own:
            cp.wait()

    return start, relay, finish


def _gather_weights(blocks, name):
    nw = len(blocks)

    def body(*refs):
        start, relay, finish = _gather_steps(refs[:nw], refs[nw:2 * nw], *refs[2 * nw:])
        start()
        relay()
        finish()

    return pl.pallas_call(
        body, in_specs=[ANY] * nw, out_specs=[ANY] * nw, out_shape=_gathered_shapes(blocks),
        scratch_shapes=_gather_sems(nw), name=name)(*blocks)


HALF = 512

GRAD_GEOM = {"w_in": ("rows", D_MODEL, IN_COLS // N_CHIPS), "w_up": ("rows", D_MODEL, 2 * D_FF // N_CHIPS),
             "w_out": ("cols", D_MODEL, D_MODEL // N_CHIPS), "w_down": ("cols", D_FF, D_FF // N_CHIPS)}


def _exchange_shape(n):
    kind, a, b = GRAD_GEOM[n]
    return (N_CHIPS, HALF, b) if kind == "rows" else (a, HALF)


def _piece_shape(n):
    kind, _, b = GRAD_GEOM[n]
    return (HALF, b) if kind == "rows" else (b, HALF)


def _half_of(ref, n, core):
    if GRAD_GEOM[n][0] == "rows":
        return ref.at[:, pl.ds(pl.multiple_of(core * HALF, HALF), HALF), :]
    return ref.at[core]


def _piece_of(ref, n, chip):
    kind, _, b = GRAD_GEOM[n]
    return ref.at[chip] if kind == "rows" else ref.at[pl.ds(pl.multiple_of(chip * b, b), b), :]


def _pair_exchange(g, names, name):
    n = len(names)

    def body(*refs):
        send, recv = refs[2 * n:]
        x, y, c = _mesh_pos()
        o = 1 - c
        cps = [pltpu.make_async_remote_copy(src_ref=_half_of(refs[t], nm, o), dst_ref=refs[n + t], send_sem=send.at[t],
                                            recv_sem=recv.at[t], device_id=(x, y, o), device_id_type=MESH)
               for t, nm in enumerate(names)]
        for cp in cps:
            cp.start()
        for cp in cps:
            cp.wait()

    return pl.pallas_call(
        body, in_specs=[ANY] * n, out_specs=[ANY] * n,
        out_shape=[jax.ShapeDtypeStruct(_exchange_shape(nm), BF16) for nm in names],
        scratch_shapes=[pltpu.SemaphoreType.DMA((n,)), pltpu.SemaphoreType.DMA((n,))],
        name=name)(*[g[nm] for nm in names])


def _pair_sum(g, recv, pos, names, name_prefix):
    def add(a, b, grid, a_spec, b_spec, name):
        def body(pos_ref, a_ref, b_ref, o_ref):
            o_ref[...] = (a_ref[...].astype(F32) + b_ref[...].astype(F32)).astype(BF16)

        return pl.pallas_call(
            body, grid_spec=pltpu.PrefetchScalarGridSpec(
                num_scalar_prefetch=1, grid=grid, in_specs=[a_spec, b_spec], out_specs=b_spec),
            out_shape=jax.ShapeDtypeStruct(b.shape, BF16), compiler_params=_cparams("parallel"), name=name)(pos, a, b)

    out = []
    for nm, r in zip(names, recv):
        kind, rows, width = GRAD_GEOM[nm]
        if kind == "rows":
            out.append(add(g[nm], r, (N_CHIPS,), pl.BlockSpec((None, HALF, width), lambda j, pos: (j, pos[2], 0)),
                           pl.BlockSpec((None, HALF, width), lambda j, pos: (j, 0, 0)), f"{name_prefix}_{nm}"))
        else:
            out.append(add(g[nm], r, (rows // D_MODEL,), pl.BlockSpec((None, D_MODEL, HALF), lambda j, pos: (pos[2], j, 0)),
                           pl.BlockSpec((D_MODEL, HALF), lambda j, pos: (j, 0)), f"{name_prefix}_{nm}"))
    return out


def _scattered_shapes(names):
    return [jax.ShapeDtypeStruct((N_CHIPS - 1,) + _piece_shape(nm), BF16) for nm in names]


def _scatter_sems(n):
    return [pltpu.SemaphoreType.DMA((n * (N_CHIPS - 1),)), pltpu.SemaphoreType.DMA((n * (N_CHIPS - 1),))]


def _scatter_steps(sums, outs, send, recv, names):
    nrel = N_CHIPS - 1
    x, y, c = _mesh_pos()
    cps = []
    for r, (px, py) in enumerate(_other_chips(x, y)):
        for t, nm in enumerate(names):
            cps.append(pltpu.make_async_remote_copy(
                src_ref=_piece_of(sums[t], nm, 2 * px + py), dst_ref=outs[t].at[r], send_sem=send.at[t * nrel + r],
                recv_sem=recv.at[t * nrel + r], device_id=(px, py, c), device_id_type=MESH))

    def start():
        for cp in cps:
            cp.start()

    def finish():
        for cp in cps:
            cp.wait()

    return start, finish


def _chip_scatter(sums, names, name):
    n = len(names)

    def body(*refs):
        start, finish = _scatter_steps(refs[:n], refs[n:2 * n], *refs[2 * n:], names)
        start()
        finish()

    return pl.pallas_call(
        body, in_specs=[ANY] * n, out_specs=[ANY] * n, out_shape=_scattered_shapes(names),
        scratch_shapes=_scatter_sems(n), name=name)(*sums)


def _chip_sum(sums, recv, pos, names, name_prefix):
    def add(a, b, a_spec, shape, name):
        def body(pos_ref, a_ref, b_ref, o_ref):
            tot = a_ref[...].astype(F32)
            for r in range(N_CHIPS - 1):
                tot = tot + b_ref[r].astype(F32)
            o_ref[...] = tot

        return pl.pallas_call(
            body, grid_spec=pltpu.PrefetchScalarGridSpec(
                num_scalar_prefetch=1, grid=(1,), in_specs=[a_spec, pl.BlockSpec(b.shape, lambda i, pos: (0, 0, 0))],
                out_specs=pl.BlockSpec((None,) + shape, lambda i, pos: (pos[2], 0, 0))),
            out_shape=jax.ShapeDtypeStruct((2,) + shape, F32), compiler_params=_cparams("arbitrary"),
            name=name)(pos, a, b)

    chip = lambda pos: 2 * pos[0] + pos[1]
    out = []
    for nm, a, b in zip(names, sums, recv):
        shape = _piece_shape(nm)
        if GRAD_GEOM[nm][0] == "rows":
            spec = pl.BlockSpec((None,) + shape, lambda i, pos: (chip(pos), 0, 0))
        else:
            spec = pl.BlockSpec(shape, lambda i, pos: (chip(pos), 0))
        out.append(add(a, b, spec, shape, f"{name_prefix}_{nm}"))
    return out


def _pair_share(totals, name):
    n = len(totals)

    def body(*refs):
        ins, outs = refs[:n], refs[n:2 * n]
        send, recv = refs[2 * n:]
        x, y, c = _mesh_pos()
        o = 1 - c
        cps = [pltpu.make_async_remote_copy(src_ref=ins[t].at[c], dst_ref=outs[t].at[c], send_sem=send.at[t],
                                            recv_sem=recv.at[t], device_id=(x, y, o), device_id_type=MESH)
               for t in range(n)]
        for cp in cps:
            cp.start()
        for t in range(n):
            pltpu.make_async_remote_copy(src_ref=ins[t].at[o], dst_ref=outs[t].at[o], send_sem=send.at[t],
                                         recv_sem=recv.at[t], device_id=(x, y, o), device_id_type=MESH).wait_recv()
        for cp in cps:
            cp.wait_send()

    return pl.pallas_call(
        body, in_specs=[ANY] * n, out_specs=[ANY] * n,
        out_shape=[jax.ShapeDtypeStruct(t.shape, t.dtype) for t in totals],
        scratch_shapes=[pltpu.SemaphoreType.DMA((n,)), pltpu.SemaphoreType.DMA((n,))],
        input_output_aliases={t: t for t in range(n)}, name=name)(*totals)


def _chip_sums(l, g, pos, names):
    tag = f"l{l}_" + "_".join(names)
    recv = _pair_exchange(g, names, "pair_exchange_" + tag)
    return _pair_sum(g, recv, pos, names, "pair_sum_" + tag)


def _gradient_shards(l, sums, scattered, pos, names):
    tag = f"l{l}_" + "_".join(names)
    halves = _pair_share(_chip_sum(sums, scattered, pos, names, "chip_sum_" + tag), "pair_share_" + tag)
    out = {}
    for nm, t in zip(names, halves):
        rows, cols = _piece_shape(nm)
        out[nm] = t.reshape(2 * rows, cols) if GRAD_GEOM[nm][0] == "rows" else t.transpose(1, 0, 2).reshape(rows, 2 * cols)
    return out


N_DEV = 8


def _allreduce_small(packed, name):
    rows = packed.shape[0]

    def body(x_ref, out_ref, gath, send_sems, recv_sems, local_sem):
        x, y, c = _mesh_pos()
        me, sibling = (x, y, c), (x, y, 1 - c)
        chips = _other_chips(x, y)

        def blk(px, py, pc):
            return gath.at[pl.ds(pl.multiple_of((4 * px + 2 * py + pc) * rows, 8), rows), :]

        def copy(k, block, to, src=None):
            return pltpu.make_async_remote_copy(
                src_ref=blk(*block) if src is None else src, dst_ref=blk(*block), send_sem=send_sems.at[k],
                recv_sem=recv_sems.at[k], device_id=to, device_id_type=MESH)

        mine = pltpu.make_async_copy(x_ref, blk(*me), local_sem)
        mine.start()
        first = [copy(0, me, sibling, src=x_ref)]
        first += [copy(1 + j, me, (*chip, c), src=x_ref) for j, chip in enumerate(chips)]
        for cp in first:
            cp.start()
        passed = [copy(4 + j, (*chip, c), sibling) for j, chip in enumerate(chips)]
        for j, chip in enumerate(chips):
            copy(1 + j, (*chip, c), me).wait_recv()
            passed[j].start()
        copy(0, sibling, me).wait_recv()
        for j, chip in enumerate(chips):
            copy(4 + j, (*chip, 1 - c), me).wait_recv()
        for cp in first + passed:
            cp.wait_send()
        mine.wait()
        tot = gath[0:rows, :]
        for d in range(1, N_DEV):
            tot = tot + gath[d * rows:(d + 1) * rows, :]
        out_ref[...] = tot

    vmem = pl.BlockSpec(memory_space=pltpu.VMEM)
    return pl.pallas_call(
        body, in_specs=[vmem], out_specs=vmem, out_shape=jax.ShapeDtypeStruct((rows, LANES), F32),
        scratch_shapes=[pltpu.VMEM((N_DEV * rows, LANES), F32), pltpu.SemaphoreType.DMA((7,)),
                        pltpu.SemaphoreType.DMA((7,)), pltpu.SemaphoreType.DMA],
        compiler_params=pltpu.CompilerParams(vmem_limit_bytes=VMEM_LIMIT_BYTES),
        name=name)(packed)


def _adamw(w, g, m, v, name):
    rows, cols = w.shape
    tr = 256 if rows % 256 == 0 else rows

    def body(w_ref, g_ref, m_ref, v_ref, d_ref, mo_ref, vo_ref):
        gv = g_ref[...]
        mn = ADAM_B1 * m_ref[...] + (1.0 - ADAM_B1) * gv
        vn = ADAM_B2 * v_ref[...] + (1.0 - ADAM_B2) * (gv * gv)
        m_hat = mn / (1.0 - ADAM_B1 ** ADAM_STEP)
        v_hat = vn / (1.0 - ADAM_B2 ** ADAM_STEP)
        d_ref[...] = -ADAM_LR * (m_hat / (jnp.sqrt(v_hat) + ADAM_EPS) + ADAM_WD * w_ref[...])
        mo_ref[...] = mn
        vo_ref[...] = vn

    spec = pl.BlockSpec((tr, cols), lambda i: (i, 0))
    return pl.pallas_call(
        body, grid=(rows // tr,), in_specs=[spec] * 4, out_specs=[spec] * 3,
        out_shape=[jax.ShapeDtypeStruct((rows, cols), F32)] * 3, compiler_params=_cparams("parallel"),
        name=name)(w, g, m, v)


def _adamw_nd(w, g, m, v, name):
    cols = w.shape[-1] if w.shape[-1] % LANES == 0 else LANES
    outs = _adamw(*(t.reshape(-1, cols) for t in (w, g, m, v)), name)
    return tuple(t.reshape(w.shape) for t in outs)


def _pack(arrays):
    return jnp.concatenate([a.reshape(-1, LANES) for a in arrays], axis=0)


def _unpack(packed, shapes):
    out, row = [], 0
    for sh in shapes:
        n = math.prod(sh) // LANES
        out.append(packed[row:row + n].reshape(sh))
        row += n
    return out


WEIGHTS = ("pre_mix_norm", "w_in", "v_norm_g", "v_norm_b", "w_spatial", "b_spatial", "out_norm_a", "out_norm_b",
           "w_out", "post_mix_norm", "pre_ffn_norm", "w_up", "conv_w", "conv_b", "w_down", "post_ffn_norm")


def kernel(x, pre_mix_norm, w_in, v_norm_g, v_norm_b, w_spatial, b_spatial, out_norm_a, out_norm_b, w_out, post_mix_norm, pre_ffn_norm, w_up, conv_w, conv_b, w_down, post_ffn_norm, loss_target, m_pre_mix_norm, m_w_in, m_v_norm_g, m_v_norm_b, m_w_spatial, m_b_spatial, m_out_norm_a, m_out_norm_b, m_w_out, m_post_mix_norm, m_pre_ffn_norm, m_w_up, m_conv_w, m_conv_b, m_w_down, m_post_ffn_norm, v_pre_mix_norm, v_w_in, v_v_norm_g, v_v_norm_b, v_w_spatial, v_b_spatial, v_out_norm_a, v_out_norm_b, v_w_out, v_post_mix_norm, v_pre_ffn_norm, v_w_up, v_conv_w, v_conv_b, v_w_down, v_post_ffn_norm):
    w = dict(pre_mix_norm=pre_mix_norm, w_in=w_in, v_norm_g=v_norm_g, v_norm_b=v_norm_b, w_spatial=w_spatial,
             b_spatial=b_spatial, out_norm_a=out_norm_a, out_norm_b=out_norm_b, w_out=w_out,
             post_mix_norm=post_mix_norm, pre_ffn_norm=pre_ffn_norm, w_up=w_up, conv_w=conv_w, conv_b=conv_b,
             w_down=w_down, post_ffn_norm=post_ffn_norm)
    m = dict(pre_mix_norm=m_pre_mix_norm, w_in=m_w_in, v_norm_g=m_v_norm_g, v_norm_b=m_v_norm_b,
             w_spatial=m_w_spatial, b_spatial=m_b_spatial, out_norm_a=m_out_norm_a, out_norm_b=m_out_norm_b,
             w_out=m_w_out, post_mix_norm=m_post_mix_norm, pre_ffn_norm=m_pre_ffn_norm, w_up=m_w_up,
             conv_w=m_conv_w, conv_b=m_conv_b, w_down=m_w_down, post_ffn_norm=m_post_ffn_norm)
    v = dict(pre_mix_norm=v_pre_mix_norm, w_in=v_w_in, v_norm_g=v_v_norm_g, v_norm_b=v_v_norm_b,
             w_spatial=v_w_spatial, b_spatial=v_b_spatial, out_norm_a=v_out_norm_a, out_norm_b=v_out_norm_b,
             w_out=v_w_out, post_mix_norm=v_post_mix_norm, pre_ffn_norm=v_pre_ffn_norm, w_up=v_w_up,
             conv_w=v_conv_w, conv_b=v_conv_b, w_down=v_w_down, post_ffn_norm=v_post_ffn_norm)
    pos = jnp.stack([lax.axis_index("x"), lax.axis_index("y"), lax.axis_index("c")]).astype(jnp.int32)
    chip = 2 * lax.axis_index("x") + lax.axis_index("y")

    cw_cols = conv_w.shape[-1]
    cw_slab = lax.dynamic_update_slice(jnp.zeros((DEPTH, 3, 2 * D_FF), F32), conv_w, (0, 0, chip * cw_cols))
    conv_w_full = _allreduce_small(cw_slab.reshape(-1, LANES), "gather_conv_w").reshape(DEPTH, 3, 2 * D_FF)
    conv_w_full = conv_w_full * 0.5
    blocks = [{n: w[n][l].astype(BF16) for n in BIG} for l in range(DEPTH)]
    wg = dict(w_in=_gather_weights([blocks[0]["w_in"]], "gather_w_in_l0")[0])

    small = {n: w[n] for n in SMALL}
    xs, target = x[0], loss_target[0]
    tabs = _rope_tables(xs.shape[0])
    params = [_layer_params(l, small, conv_w_full) for l in range(DEPTH)]
    saved, wgs = [], []
    xin = xs
    h = _rms_cast(xin, params[0]["pre_mix_norm"], "pre_mix_l0")
    for l in range(DEPTH):
        sv, gathered, wg = _layer_forward(l, xin, h, params[l], wg, tabs,
                                          [blocks[l + 1][n] for n in BIG] if l + 1 < DEPTH else None,
                                          blocks[0] if l == 0 else None,
                                          params[l + 1]["pre_mix_norm"] if l + 1 < DEPTH else None)
        saved.append(sv)
        wgs.append(wg)
        if l + 1 < DEPTH:
            wg = dict(zip(BIG, gathered))
            xin, h = sv["x2"], sv["h_next"]
    loss_part, dx, df, g_post = _loss_norm_bwd(saved[-1]["x1"], saved[-1]["f"], params[-1]["post_ffn_norm"], target,
                                               "loss")
    smalls, shards = [None] * DEPTH, [{} for _ in range(DEPTH)]
    pending = None
    for l in reversed(range(DEPTH)):
        dx1, dh1, big, smalls[l], scattered, done = _layer_backward(l, dx, df, saved[l], params[l], wgs[l], tabs, pos,
                                                                    pending[1:] if pending else None, hide=l == 0)
        smalls[l]["post_ffn_norm"] = g_post
        if l > 0:
            dx, smalls[l]["pre_mix_norm"], df, g_post = _norm_bwd_in_out(
                dx1, dh1, saved[l]["x0"], params[l]["pre_mix_norm"], saved[l - 1]["f"], params[l - 1]["post_ffn_norm"],
                f"norm_bwd_in_out_l{l}")
        else:
            dx, smalls[l]["pre_mix_norm"] = _norm_bwd_in(dx1, dh1, saved[l]["x0"], params[l]["pre_mix_norm"],
                                                         "norm_bwd_in_l0")
        if pending:
            shards[pending[0]].update(_gradient_shards(pending[0], pending[1], scattered, pos, pending[2]))
        for names, (sums, received) in done.items():
            shards[l].update(_gradient_shards(l, sums, received, pos, names))
        names = tuple(big)
        pending = (l, _chip_sums(l, big, pos, names), names)
    shards[pending[0]].update(_gradient_shards(
        pending[0], pending[1], _chip_scatter(pending[1], pending[2], f"chip_scatter_l{pending[0]}"), pos, pending[2]))

    small_shapes = [w[n].shape for n in SMALL]
    stacked = [jnp.stack([smalls[l][n].reshape(w[n].shape[1:]) for l in range(DEPTH)]) for n in SMALL]
    cw_grad = jnp.stack([smalls[l]["conv_w"] for l in range(DEPTH)])
    packed = _pack(stacked + [cw_grad, loss_part])
    total = _allreduce_small(packed, "allreduce_small")
    parts = _unpack(total, small_shapes + [cw_grad.shape, (8, LANES)])
    g_small = dict(zip(SMALL, parts[:len(SMALL)]))
    loss = parts[-1][0, 0]
    g_conv_w = lax.dynamic_slice(parts[-2], (0, 0, chip * cw_cols), conv_w.shape)

    grads = {n: jnp.stack([shards[l][n] for l in range(DEPTH)]) for n in BIG}
    grads.update(g_small)
    grads["conv_w"] = g_conv_w

    dp, mp, vp = _adamw(_pack([w[n] for n in SMALL]), _pack([g_small[n] for n in SMALL]),
                        _pack([m[n] for n in SMALL]), _pack([v[n] for n in SMALL]), "adamw_small")
    delta = dict(zip(SMALL, _unpack(dp, small_shapes)))
    new_m = dict(zip(SMALL, _unpack(mp, small_shapes)))
    new_v = dict(zip(SMALL, _unpack(vp, small_shapes)))
    for n in BIG + ("conv_w",):
        delta[n], new_m[n], new_v[n] = _adamw_nd(w[n], grads[n], m[n], v[n], "adamw_" + n)

    return (loss, dx[None], *[grads[n] for n in WEIGHTS], *[delta[n] for n in WEIGHTS],
            *[new_m[n] for n in WEIGHTS], *[new_v[n] for n in WEIGHTS])
```

```python
import functools
import math

import jax
import jax.numpy as jnp
import numpy as np
from jax import lax
from jax.experimental import pallas as pl
from jax.experimental.pallas import tpu as pltpu

F32 = jnp.float32
BF16 = jnp.bfloat16
MESH = pl.DeviceIdType.MESH

D_MODEL = 1024
A_WIDTH = 512
A_GROUPS = 4
GROUP_DIM = 128
CHUNK = 128
B_WIDTH = 512
HEAD_DIM = 64
ROT_DIM = 16
ROPE_THETA = 500000.0
DILATIONS = (1, 4, 16)
BAND = 128
IN_COLS = 2560
D_FF = 4096
EPS = 1e-6
NEG_INF = -1e30
N_CHIPS = 4
LANES = 128

ADAM_LR = 0.001
ADAM_B1 = 0.9
ADAM_B2 = 0.999
ADAM_EPS = 1e-08
ADAM_WD = 0.01
ADAM_STEP = 10

VMEM_LIMIT_BYTES = 56 * 1024 * 1024
RSQRT2 = 0.7071067811865476
INV_SQRT_2PI = 0.3989422804014327
GELU_C = 0.7978845608028654
GELU_A = 0.044715

ANY = pl.BlockSpec(memory_space=pl.ANY)
NN = ((1,), (0,))
NT = ((1,), (1,))
TN = ((0,), (0,))


def _cparams(*sem):
    return pltpu.CompilerParams(dimension_semantics=sem, vmem_limit_bytes=VMEM_LIMIT_BYTES)


def _dot(a, b, dims):
    return lax.dot_general(a, b, (dims, ((), ())), preferred_element_type=F32)


def _rsq_mean(a):
    return lax.rsqrt(jnp.mean(a * a, axis=-1, keepdims=True) + EPS)


def _rms_bwd(a, r, g, dz):
    t = dz * g
    da = r * t - a * (r * r * r) * jnp.mean(t * a, axis=-1, keepdims=True)
    return da, dz * a * r


def _colsum(a):
    return jnp.sum(a, axis=0, keepdims=True)


def _gelu_tanh(x):
    u = x * x
    t = jnp.tanh(x * (GELU_C + (GELU_C * GELU_A) * u))
    hx = 0.5 * x
    act = hx + hx * t
    grad = 0.5 + 0.5 * t + (hx - hx * t * t) * (GELU_C + (3.0 * GELU_C * GELU_A) * u)
    return act, grad


def _grid_edges(grid):
    ids = [pl.program_id(ax) for ax in range(len(grid))]
    first = functools.reduce(jnp.logical_and, [i == 0 for i in ids])
    last = functools.reduce(jnp.logical_and, [i == n - 1 for i, n in zip(ids, grid)])
    return first, last


def _matmul(a, b, *, grid, a_spec, b_spec, o_spec, o_shape, o_dtype, dims, nk, kaxis, acc_shape, name, b_2d=None,
            halves=False, scatter=None, gather=None):
    assert scatter is None or gather is None
    ns = len(scatter[0]) if scatter else len(gather) if gather else 0

    def body(*refs):
        a_ref, b_ref = refs[:2]
        o_ref = refs[2 + ns]
        scratch = refs[3 + 2 * ns:]
        if ns:
            first, last = _grid_edges(grid)
            if scatter:
                start, finish = _scatter_steps(refs[2:2 + ns], refs[3 + ns:3 + 2 * ns], scratch[-2], scratch[-1],
                                               scatter[1])
            else:
                start, relay, last_wait = _gather_steps(refs[2:2 + ns], refs[3 + ns:3 + 2 * ns], scratch[-2],
                                                        scratch[-1])

                def finish():
                    relay()
                    last_wait()
            pl.when(first)(start)
        def store(val):
            if halves:
                half = val.shape[1] // 2
                o_ref[0] = val[:, :half].astype(o_dtype)
                o_ref[1] = val[:, half:].astype(o_dtype)
            else:
                o_ref[...] = val.astype(o_dtype)

        bv = b_ref[...] if b_2d is None else b_ref[...].reshape(b_2d)
        part = _dot(a_ref[...], bv, dims)
        if nk == 1:
            store(part)
        else:
            acc = scratch[0]
            k = pl.program_id(kaxis)

            @pl.when(k == 0)
            def _():
                acc[...] = part

            @pl.when(k > 0)
            def _():
                acc[...] += part

            @pl.when(k == nk - 1)
            def _():
                store(acc[...])

        if ns:
            pl.when(last)(finish)

    sem = tuple("arbitrary" if (ns or (nk > 1 and ax == kaxis)) else "parallel" for ax in range(len(grid)))
    riding = list(scatter[0]) if scatter else list(gather or [])
    rider_shapes = _scattered_shapes(scatter[1]) if scatter else _gathered_shapes(riding)
    rider_sems = _scatter_sems(ns) if scatter else _gather_sems(ns) if gather else []
    res = pl.pallas_call(
        body, grid=grid, in_specs=[a_spec, b_spec] + [ANY] * ns, out_specs=[o_spec] + [ANY] * ns,
        out_shape=[jax.ShapeDtypeStruct(o_shape, o_dtype)] + rider_shapes,
        scratch_shapes=([pltpu.VMEM(acc_shape, F32)] if nk > 1 else []) + rider_sems,
        compiler_params=_cparams(*sem), name=name)(a, b, *riding)
    return (res[0], list(res[1:])) if ns else res[0]


def _mix_out_norm(mixed, w_out, x0, g_post, g_next, name):
    s, d = x0.shape
    tm = 512

    def body(a_ref, w_ref, x_ref, gp_ref, gn_ref, y_ref, x1_ref, h_ref):
        y = _dot(a_ref[...], w_ref[...].reshape(d, d), NN)
        y_ref[...] = y
        x1 = x_ref[...] + y * _rsq_mean(y) * gp_ref[...]
        x1_ref[...] = x1
        h_ref[...] = (x1 * _rsq_mean(x1) * gn_ref[...]).astype(BF16)

    row = pl.BlockSpec((tm, d), lambda i: (i, 0))
    vec = pl.BlockSpec((1, d), lambda i: (0, 0))
    return pl.pallas_call(
        body, grid=(s // tm,),
        in_specs=[row, pl.BlockSpec((N_CHIPS, None, d // N_CHIPS, d), lambda i: (0, 0, 0, 0)), row, vec, vec],
        out_specs=[row, row, row],
        out_shape=[jax.ShapeDtypeStruct((s, d), F32), jax.ShapeDtypeStruct((s, d), F32),
                   jax.ShapeDtypeStruct((s, d), BF16)],
        compiler_params=_cparams("parallel"), name=name)(mixed, w_out, x0, g_post, g_next)


TM = 512
TMM = 1024


TR = 256


def _row_spec(width, col=0):
    return pl.BlockSpec((TR, width), lambda i, col=col: (i, col))


def _vec_spec(width):
    return pl.BlockSpec((1, width), lambda i: (0, 0))


def _rms_cast(x, g, name):
    s, d = x.shape

    def body(x_ref, g_ref, h_ref):
        a = x_ref[...]
        h_ref[...] = (a * _rsq_mean(a) * g_ref[...]).astype(BF16)

    return pl.pallas_call(
        body, grid=(s // TR,), in_specs=[_row_spec(d), _vec_spec(d)], out_specs=_row_spec(d),
        out_shape=jax.ShapeDtypeStruct((s, d), BF16), compiler_params=_cparams("parallel"), name=name)(x, g)


def _acc_init(refs):
    @pl.when(pl.program_id(0) == 0)
    def _():
        for r in refs:
            r[...] = jnp.zeros_like(r)


def _loss_norm_bwd(x1, f, g_post, target, name):
    s, d = x1.shape

    def body(x_ref, f_ref, gp_ref, t_ref, loss_ref, dx_ref, df_ref, dg_ref):
        _acc_init([loss_ref, dg_ref])
        fv = f_ref[...]
        r = _rsq_mean(fv)
        err = x_ref[...] + fv * r * gp_ref[...] - t_ref[...]
        dx = err * (1.0 / d)
        dx_ref[...] = dx
        part = 0.5 * jnp.sum(jnp.mean(err * err, axis=-1, keepdims=True), axis=0, keepdims=True)
        loss_ref[...] += jnp.broadcast_to(part, loss_ref.shape)
        da, dgt = _rms_bwd(fv, r, gp_ref[...], dx)
        df_ref[...] = da.astype(BF16)
        dg_ref[...] += _colsum(dgt)

    return pl.pallas_call(
        body, grid=(s // TR,), in_specs=[_row_spec(d), _row_spec(d), _vec_spec(d), _row_spec(d)],
        out_specs=[pl.BlockSpec((8, LANES), lambda i: (0, 0)), _row_spec(d), _row_spec(d), _vec_spec(d)],
        out_shape=[jax.ShapeDtypeStruct((8, LANES), F32), jax.ShapeDtypeStruct((s, d), F32),
                   jax.ShapeDtypeStruct((s, d), BF16), jax.ShapeDtypeStruct((1, d), F32)],
        compiler_params=_cparams("arbitrary"), name=name)(x1, f, g_post, target)


def _norm_bwd_mid(dx2, dh2, x1, g_pf, y1, g_pm, name):
    s, d = dx2.shape

    def body(dx2_ref, dh_ref, x1_ref, gpf_ref, y1_ref, gpm_ref, dx1_ref, dy1_ref, dgpf_ref, dgpm_ref):
        _acc_init([dgpf_ref, dgpm_ref])
        x1 = x1_ref[...]
        da, dgt = _rms_bwd(x1, _rsq_mean(x1), gpf_ref[...], dh_ref[...])
        dx1 = dx2_ref[...] + da
        dx1_ref[...] = dx1
        dgpf_ref[...] += _colsum(dgt)
        y1 = y1_ref[...]
        dy, dgt2 = _rms_bwd(y1, _rsq_mean(y1), gpm_ref[...], dx1)
        dy1_ref[...] = dy.astype(BF16)
        dgpm_ref[...] += _colsum(dgt2)

    return pl.pallas_call(
        body, grid=(s // TR,),
        in_specs=[_row_spec(d), _row_spec(d), _row_spec(d), _vec_spec(d), _row_spec(d), _vec_spec(d)],
        out_specs=[_row_spec(d), _row_spec(d), _vec_spec(d), _vec_spec(d)],
        out_shape=[jax.ShapeDtypeStruct((s, d), F32), jax.ShapeDtypeStruct((s, d), BF16),
                   jax.ShapeDtypeStruct((1, d), F32), jax.ShapeDtypeStruct((1, d), F32)],
        compiler_params=_cparams("arbitrary"), name=name)(dx2, dh2, x1, g_pf, y1, g_pm)


def _norm_bwd_in_out(dx1, dh1, x0, g1, f_below, g_post_below, name):
    s, d = dx1.shape

    def body(dx1_ref, dh_ref, x0_ref, g_ref, f_ref, gp_ref, dx0_ref, dg_ref, df_ref, dgp_ref):
        _acc_init([dg_ref, dgp_ref])
        x0 = x0_ref[...]
        da, dgt = _rms_bwd(x0, _rsq_mean(x0), g_ref[...], dh_ref[...])
        dx0 = dx1_ref[...] + da
        dx0_ref[...] = dx0
        dg_ref[...] += _colsum(dgt)
        fv = f_ref[...]
        db, dgt2 = _rms_bwd(fv, _rsq_mean(fv), gp_ref[...], dx0)
        df_ref[...] = db.astype(BF16)
        dgp_ref[...] += _colsum(dgt2)

    return pl.pallas_call(
        body, grid=(s // TR,),
        in_specs=[_row_spec(d), _row_spec(d), _row_spec(d), _vec_spec(d), _row_spec(d), _vec_spec(d)],
        out_specs=[_row_spec(d), _vec_spec(d), _row_spec(d), _vec_spec(d)],
        out_shape=[jax.ShapeDtypeStruct((s, d), F32), jax.ShapeDtypeStruct((1, d), F32),
                   jax.ShapeDtypeStruct((s, d), BF16), jax.ShapeDtypeStruct((1, d), F32)],
        compiler_params=_cparams("arbitrary"), name=name)(dx1, dh1, x0, g1, f_below, g_post_below)


def _norm_bwd_in(dx1, dh1, x0, g1, name):
    s, d = dx1.shape

    def body(dx1_ref, dh_ref, x0_ref, g_ref, dx0_ref, dg_ref):
        _acc_init([dg_ref])
        x0 = x0_ref[...]
        da, dgt = _rms_bwd(x0, _rsq_mean(x0), g_ref[...], dh_ref[...])
        dx0_ref[...] = dx1_ref[...] + da
        dg_ref[...] += _colsum(dgt)

    return pl.pallas_call(
        body, grid=(s // TR,), in_specs=[_row_spec(d), _row_spec(d), _row_spec(d), _vec_spec(d)],
        out_specs=[_row_spec(d), _vec_spec(d)],
        out_shape=[jax.ShapeDtypeStruct((s, d), F32), jax.ShapeDtypeStruct((1, d), F32)],
        compiler_params=_cparams("arbitrary"), name=name)(dx1, dh1, x0, g1)


def _tril_mask():
    row = lax.broadcasted_iota(jnp.int32, (CHUNK, CHUNK), 0)
    col = lax.broadcasted_iota(jnp.int32, (CHUNK, CHUNK), 1)
    return row >= col


def _gating_forward(pa, gv, bv, wt, bsf):
    er = lax.erf(pa * RSQRT2)
    za = 0.5 * pa * (1.0 + er)
    u = za[:, :A_WIDTH]
    va = za[:, A_WIDTH:]
    xc = va - jnp.mean(va, axis=-1, keepdims=True)
    rs = lax.rsqrt(jnp.mean(xc * xc, axis=-1, keepdims=True) + EPS)
    vn = xc * rs
    vlb = (vn * gv + bv).astype(BF16)
    sg = jnp.concatenate(
        [_dot(wt[g], vlb[:, g * GROUP_DIM:(g + 1) * GROUP_DIM], NN) for g in range(A_GROUPS)], axis=1) + bsf
    return er, u, rs, vn, vlb, sg


def _masked_ws(ws_ref):
    mask = _tril_mask()
    return [jnp.where(mask, ws_ref[g], 0.0).astype(BF16) for g in range(A_GROUPS)]


def _mixer_a_fwd(proj, gv, bv, ws, bsf, ga, name):
    s = proj.shape[0]

    def body(p_ref, gv_ref, bv_ref, ws_ref, bs_ref, ga_ref, o_ref):
        wt = _masked_ws(ws_ref)
        for ch in range(TR // CHUNK):
            rows = slice(ch * CHUNK, (ch + 1) * CHUNK)
            _, u, _, _, _, sg = _gating_forward(p_ref[rows, :].astype(F32), gv_ref[...], bv_ref[...], wt, bs_ref[...])
            oa = u * sg
            o_ref[rows, :] = (oa * _rsq_mean(oa) * ga_ref[...]).astype(BF16)

    return pl.pallas_call(
        body, grid=(s // TR,),
        in_specs=[_row_spec(2 * A_WIDTH), _vec_spec(A_WIDTH), _vec_spec(A_WIDTH),
                  pl.BlockSpec((A_GROUPS, CHUNK, CHUNK), lambda i: (0, 0, 0)),
                  pl.BlockSpec((CHUNK, A_WIDTH), lambda i: (0, 0)), _vec_spec(A_WIDTH)],
        out_specs=_row_spec(A_WIDTH), out_shape=jax.ShapeDtypeStruct((s, A_WIDTH + B_WIDTH), BF16),
        compiler_params=_cparams("parallel"), name=name)(proj, gv, bv, ws, bsf, ga)


def _mixer_a_bwd(proj, dmixed, gv, bv, ws, bsf, ga, name):
    s = proj.shape[0]
    nsteps = s // TR

    def body(p_ref, dm_ref, gv_ref, bv_ref, ws_ref, bs_ref, ga_ref,
             dp_ref, dga_ref, dgv_ref, dbv_ref, dbs_ref, dws_ref):
        _acc_init([dga_ref, dgv_ref, dbv_ref, dbs_ref, dws_ref])
        mask = _tril_mask()
        wt = _masked_ws(ws_ref)
        gvv = gv_ref[...]
        gav = ga_ref[...]
        for ch in range(TR // CHUNK):
            rows = slice(ch * CHUNK, (ch + 1) * CHUNK)
            pa = p_ref[rows, :].astype(F32)
            er, u, rs, vn, vlb, sg = _gating_forward(pa, gvv, bv_ref[...], wt, bs_ref[...])
            oa = u * sg
            doa, dgt = _rms_bwd(oa, _rsq_mean(oa), gav, dm_ref[rows, :])
            dga_ref[...] += _colsum(dgt)
            du = doa * sg
            dsg = doa * u
            dbs_ref[...] += dsg
            dsgb = dsg.astype(BF16)
            dvl = []
            for g in range(A_GROUPS):
                cols = slice(g * GROUP_DIM, (g + 1) * GROUP_DIM)
                dws_ref[g] += jnp.where(mask, _dot(dsgb[:, cols], vlb[:, cols], NT), 0.0)
                dvl.append(_dot(wt[g], dsgb[:, cols], TN))
            dvl = jnp.concatenate(dvl, axis=1)
            dgv_ref[...] += _colsum(dvl * vn)
            dbv_ref[...] += _colsum(dvl)
            dvn = dvl * gvv
            dva = rs * (dvn - jnp.mean(dvn, axis=-1, keepdims=True)
                        - vn * jnp.mean(dvn * vn, axis=-1, keepdims=True))
            gp = 0.5 * (1.0 + er) + pa * jnp.exp(-0.5 * pa * pa) * INV_SQRT_2PI
            dp_ref[rows, :] = (jnp.concatenate([du, dva], axis=1) * gp).astype(BF16)

        @pl.when(pl.program_id(0) == nsteps - 1)
        def _():
            for g in range(A_GROUPS):
                cols = slice(g * GROUP_DIM, (g + 1) * GROUP_DIM)
                tot = jnp.sum(dbs_ref[:, cols], axis=1, keepdims=True)
                dbs_ref[:, cols] = jnp.broadcast_to(tot, (CHUNK, GROUP_DIM))

    full = lambda *shape: pl.BlockSpec(shape, lambda i: (0,) * len(shape))
    return pl.pallas_call(
        body, grid=(nsteps,),
        in_specs=[_row_spec(2 * A_WIDTH), _row_spec(A_WIDTH), _vec_spec(A_WIDTH), _vec_spec(A_WIDTH),
                  full(A_GROUPS, CHUNK, CHUNK), full(CHUNK, A_WIDTH), _vec_spec(A_WIDTH)],
        out_specs=[_row_spec(2 * A_WIDTH), _vec_spec(A_WIDTH), _vec_spec(A_WIDTH), _vec_spec(A_WIDTH),
                   full(CHUNK, A_WIDTH), full(A_GROUPS, CHUNK, CHUNK)],
        out_shape=[jax.ShapeDtypeStruct((s, IN_COLS), BF16), jax.ShapeDtypeStruct((1, A_WIDTH), F32),
                   jax.ShapeDtypeStruct((1, A_WIDTH), F32), jax.ShapeDtypeStruct((1, A_WIDTH), F32),
                   jax.ShapeDtypeStruct((CHUNK, A_WIDTH), F32),
                   jax.ShapeDtypeStruct((A_GROUPS, CHUNK, CHUNK), F32)],
        compiler_params=_cparams("arbitrary"), name=name)(proj, dmixed, gv, bv, ws, bsf, ga)


def _rope_tables(s):
    half = ROT_DIM // 2
    lane = jnp.arange(LANES) % HEAD_DIM
    inv = ROPE_THETA ** (-(2 * (lane % half)).astype(F32) / ROT_DIM)
    ang = jnp.arange(s, dtype=F32)[:, None] * inv[None, :]
    cos, sin = jnp.cos(ang), jnp.sin(ang)
    c = jnp.where(lane < ROT_DIM, cos, 1.0)
    s1 = jnp.where(lane < half, -sin, 0.0)
    s2 = jnp.where((lane >= half) & (lane < ROT_DIM), sin, 0.0)
    return c, s1, s2


def _lane_blocks(width):
    return [slice(b * LANES, (b + 1) * LANES) for b in range(width // LANES)]


CLASS_DILS = tuple(d for d in DILATIONS if d > 1)


def _class_shape(s, dil, dtype):
    return jax.ShapeDtypeStruct((dil, s // dil, B_WIDTH), dtype)


def _class_spec(dil):
    return pl.BlockSpec((dil, TR // dil, B_WIDTH), lambda i, *_: (0, i, 0))


NBLK = B_WIDTH // LANES
STAGE = pltpu.VMEM((NBLK, TR, LANES), F32)


def _stage_put(stage, value):
    for b, sl in enumerate(_lane_blocks(B_WIDTH)):
        stage[b] = value[:, sl]


def _stage_get(stage):
    return jnp.concatenate([stage[b] for b in range(NBLK)], axis=1)


def _store_classes(stage, dst_ref, dil):
    for b, sl in enumerate(_lane_blocks(B_WIDTH)):
        for r in range(dil):
            dst_ref[r, :, sl] = stage[b, pl.ds(r, TR // dil, stride=dil), :].astype(dst_ref.dtype)


def _load_classes(src_ref, stage, dil):
    for b, sl in enumerate(_lane_blocks(B_WIDTH)):
        for r in range(dil):
            stage[b, pl.ds(r, TR // dil, stride=dil), :] = src_ref[r, :, sl].astype(F32)
    return _stage_get(stage)


def _rope_fwd(proj, tabs, name, gather=None):
    s = proj.shape[0]
    half = ROT_DIM // 2
    scale = HEAD_DIM ** -0.5
    nlay = 1 + len(CLASS_DILS)
    ng = 0 if gather is None else len(gather)

    def body(q_ref, k_ref, v_ref, c_ref, s1_ref, s2_ref, *rest):
        outs, stage = rest[ng:ng + 3 * nlay], rest[2 * ng + 3 * nlay]
        if ng:
            start, relay, finish = _gather_steps(rest[:ng], rest[ng + 3 * nlay:2 * ng + 3 * nlay],
                                                 *rest[2 * ng + 3 * nlay + 1:])
            first, last = _grid_edges((s // TR,))
            pl.when(first)(start)
        c, s1, s2 = c_ref[...], s1_ref[...], s2_ref[...]
        for which, (src, mul) in enumerate(((q_ref, scale), (k_ref, 1.0), (v_ref, None))):
            if mul is None:
                _stage_put(stage, src[...].astype(F32))
            else:
                for b, sl in enumerate(_lane_blocks(B_WIDTH)):
                    a = src[:, sl].astype(F32)
                    r = a * c + pltpu.roll(a, LANES - half, 1) * s1 + pltpu.roll(a, half, 1) * s2
                    stage[b] = r * mul
            dst = outs[which * nlay:(which + 1) * nlay]
            dst[0][...] = _stage_get(stage).astype(BF16)
            for ref, d in zip(dst[1:], CLASS_DILS):
                _store_classes(stage, ref, d)

        if ng:
            @pl.when(last)
            def _():
                relay()
                finish()

    tab = pl.BlockSpec((TR, LANES), lambda i: (i, 0))
    lay_specs = [_row_spec(B_WIDTH)] + [_class_spec(d) for d in CLASS_DILS]
    lay_shapes = [jax.ShapeDtypeStruct((s, B_WIDTH), BF16)] + [_class_shape(s, d, BF16) for d in CLASS_DILS]
    outs = pl.pallas_call(
        body, grid=(s // TR,),
        in_specs=[_row_spec(B_WIDTH, 2), _row_spec(B_WIDTH, 3), _row_spec(B_WIDTH, 4), tab, tab, tab] + [ANY] * ng,
        out_specs=lay_specs * 3 + [ANY] * ng, out_shape=lay_shapes * 3 + _gathered_shapes(gather or []),
        scratch_shapes=[STAGE] + (_gather_sems(ng) if ng else []),
        compiler_params=_cparams("arbitrary" if ng else "parallel"), name=name)(proj, proj, proj, *tabs,
                                                                              *(gather or []))
    q, k, v = (dict(zip(DILATIONS, outs[w * nlay:(w + 1) * nlay])) for w in range(3))
    return q, k, v, list(outs[3 * nlay:])


def _as_classes(t):
    return t if t.ndim == 3 else t[None]


def _band_mask(i):
    qi = lax.broadcasted_iota(jnp.int32, (BAND, 2 * BAND), 0)
    kj = lax.broadcasted_iota(jnp.int32, (BAND, 2 * BAND), 1)
    return (kj >= qi) & (kj <= qi + BAND) & ((kj >= BAND) | (i > 0))


def _head_masks():
    lane = lax.broadcasted_iota(jnp.int32, (1, LANES), 1)
    return lane < HEAD_DIM, lane >= HEAD_DIM


def _stack_heads(t):
    lo, hi = _head_masks()
    zero = jnp.zeros_like(t)
    return jnp.concatenate([jnp.where(lo, t, zero), jnp.where(hi, t, zero)], axis=0)


def _attn_specs(last):
    cur = pl.BlockSpec((None, BAND, B_WIDTH), lambda r, i: (r, jnp.minimum(i, last), 0))
    prev = pl.BlockSpec((None, BAND, B_WIDTH), lambda r, i: (r, jnp.maximum(jnp.minimum(i, last) - 1, 0), 0))
    return cur, prev


def _attn_fwd(q, k, v, name, gather=None):
    dil, n, _ = q.shape
    nb = n // BAND
    ng = 0 if gather is None else len(gather)

    def body(*refs):
        q_ref, kc_ref, kp_ref, vc_ref, vp_ref = refs[:5]
        o_ref, l_ref = refs[5 + ng:7 + ng]
        if ng:
            start, relay, finish = _gather_steps(refs[5:5 + ng], refs[7 + ng:7 + 2 * ng], *refs[7 + 2 * ng:])
            first, last = _grid_edges((dil, nb))
            pl.when(first)(start)
        valid = _band_mask(pl.program_id(1))
        valid = jnp.concatenate([valid, valid], axis=0)
        lo, _ = _head_masks()
        for sl in _lane_blocks(B_WIDTH):
            kk = jnp.concatenate([kp_ref[:, sl], kc_ref[:, sl]], axis=0)
            vv = jnp.concatenate([vp_ref[:, sl], vc_ref[:, sl]], axis=0)
            sc = jnp.where(valid, _dot(_stack_heads(q_ref[:, sl]), kk, NT), NEG_INF)
            mx = jnp.max(sc, axis=1, keepdims=True)
            p = jnp.exp(sc - mx)
            den = jnp.sum(p, axis=1, keepdims=True)
            out = _dot(p.astype(BF16), vv, NN) / den
            lse = mx + jnp.log(den)
            o_ref[:, sl] = jnp.where(lo, out[:BAND], out[BAND:]).astype(BF16)
            l_ref[:, sl] = jnp.where(lo, lse[:BAND], lse[BAND:])

        if ng:
            @pl.when(last)
            def _():
                relay()
                finish()

    cur, prev = _attn_specs(nb - 1)
    sem = ("arbitrary", "arbitrary") if ng else ("parallel", "parallel")
    res = pl.pallas_call(
        body, grid=(dil, nb), in_specs=[cur, cur, prev, cur, prev] + [ANY] * ng, out_specs=[cur, cur] + [ANY] * ng,
        out_shape=[jax.ShapeDtypeStruct((dil, n, B_WIDTH), BF16), jax.ShapeDtypeStruct((dil, n, B_WIDTH), F32)]
        + _gathered_shapes(gather or []),
        scratch_shapes=_gather_sems(ng) if ng else [],
        compiler_params=_cparams(*sem), name=name)(q, k, k, v, v, *(gather or []))
    return res[0], res[1], list(res[2:])


MAX_CLASS_BLOCKS = 8


def _class_masks():
    qi = lax.broadcasted_iota(jnp.int32, (BAND, 2 * BAND), 0)
    kj = lax.broadcasted_iota(jnp.int32, (BAND, 2 * BAND), 1)
    both = (kj >= qi) & (kj <= qi + BAND)
    own = kj[:, :BAND] <= qi[:, :BAND]
    return jnp.concatenate([own, own], axis=0), jnp.concatenate([both, both], axis=0)


def _block_rows(g):
    return pl.ds(pl.multiple_of(g * BAND, BAND), BAND)


def _key_rows(g):
    return pl.ds(pl.multiple_of((g - 1) * BAND, BAND), 2 * BAND)


def _attn_fwd_class(q, k, v, name, gather=None):
    dil, n, _ = q.shape
    nb = n // BAND
    ng = 0 if gather is None else len(gather)

    def body(*refs):
        q_ref, k_ref, v_ref = refs[:3]
        o_ref, l_ref = refs[3 + ng:5 + ng]
        if ng:
            start, relay, finish = _gather_steps(refs[3:3 + ng], refs[5 + ng:5 + 2 * ng], *refs[5 + 2 * ng:])
            first, last = _grid_edges((dil,))
            pl.when(first)(start)
        own, both = _class_masks()
        lo, _ = _head_masks()

        def block(rows, keys, valid):
            for sl in _lane_blocks(B_WIDTH):
                sc = jnp.where(valid, _dot(_stack_heads(q_ref[rows, sl]), k_ref[keys, sl], NT), NEG_INF)
                mx = jnp.max(sc, axis=1, keepdims=True)
                p = jnp.exp(sc - mx)
                den = jnp.sum(p, axis=1, keepdims=True)
                out = _dot(p.astype(BF16), v_ref[keys, sl], NN) / den
                lse = mx + jnp.log(den)
                o_ref[rows, sl] = jnp.where(lo, out[:BAND], out[BAND:]).astype(BF16)
                l_ref[rows, sl] = jnp.where(lo, lse[:BAND], lse[BAND:])

        block(_block_rows(0), _block_rows(0), own)

        @pl.loop(1, nb)
        def _(g):
            block(_block_rows(g), _key_rows(g), both)

        if ng:
            @pl.when(last)
            def _():
                relay()
                finish()

    spec = pl.BlockSpec((None, n, B_WIDTH), lambda r: (r, 0, 0))
    res = pl.pallas_call(
        body, grid=(dil,), in_specs=[spec] * 3 + [ANY] * ng, out_specs=[spec, spec] + [ANY] * ng,
        out_shape=[jax.ShapeDtypeStruct((dil, n, B_WIDTH), BF16), jax.ShapeDtypeStruct((dil, n, B_WIDTH), F32)]
        + _gathered_shapes(gather or []),
        scratch_shapes=_gather_sems(ng) if ng else [],
        compiler_params=_cparams("arbitrary" if ng else "parallel"), name=name)(q, k, v, *(gather or []))
    return res[0], res[1], list(res[2:])


def _attn_bwd_class(q, k, v, do, lse, delta, name, scatter=None):
    dil, n, _ = q.shape
    nb = n // BAND
    ns = 0 if scatter is None else len(scatter[0])

    def body(*refs):
        q_ref, k_ref, v_ref, do_ref, lse_ref, dl_ref = refs[:6]
        dq_ref, dk_ref, dv_ref = refs[6 + ns:9 + ns]
        ck_ref, cv_ref = refs[9 + 2 * ns:11 + 2 * ns]
        if ns:
            start, finish = _scatter_steps(refs[6:6 + ns], refs[9 + ns:9 + 2 * ns], *refs[11 + 2 * ns:], scatter[1])
            first, last = _grid_edges((dil,))
            pl.when(first)(start)
        own, both = _class_masks()
        lo, _ = _head_masks()
        lane = lax.broadcasted_iota(jnp.int32, (1, LANES), 1)

        def per_head(t):
            return jnp.concatenate(
                [jnp.sum(jnp.where(lane == first, t, 0.0), axis=1, keepdims=True) for first in (0, HEAD_DIM)], axis=0)

        def grads(rows, keys, valid, sl):
            q2 = _stack_heads(q_ref[rows, sl])
            do2 = _stack_heads(do_ref[rows, sl])
            kk = k_ref[keys, sl]
            p = jnp.where(valid, jnp.exp(_dot(q2, kk, NT) - per_head(lse_ref[rows, sl])), 0.0)
            ds = (p * (_dot(do2, v_ref[keys, sl], NT) - per_head(dl_ref[rows, sl]))).astype(BF16)
            dq = _dot(ds, kk, NN)
            dq_ref[rows, sl] = jnp.where(lo, dq[:BAND], dq[BAND:]).astype(BF16)
            return _dot(ds, q2, TN), _dot(p.astype(BF16), do2, TN)

        for sl in _lane_blocks(B_WIDTH):
            ck_ref[:, sl], cv_ref[:, sl] = grads(_block_rows(0), _block_rows(0), own, sl)

        @pl.loop(1, nb)
        def _(g):
            before = _block_rows(g - 1)
            for sl in _lane_blocks(B_WIDTH):
                dkk, dvv = grads(_block_rows(g), _key_rows(g), both, sl)
                dk_ref[before, sl] = (ck_ref[:, sl] + dkk[:BAND]).astype(BF16)
                dv_ref[before, sl] = (cv_ref[:, sl] + dvv[:BAND]).astype(BF16)
                ck_ref[:, sl] = dkk[BAND:]
                cv_ref[:, sl] = dvv[BAND:]

        final = pl.ds((nb - 1) * BAND, BAND)
        dk_ref[final, :] = ck_ref[...].astype(BF16)
        dv_ref[final, :] = cv_ref[...].astype(BF16)

        if ns:
            pl.when(last)(finish)

    spec = pl.BlockSpec((None, n, B_WIDTH), lambda r: (r, 0, 0))
    shape = jax.ShapeDtypeStruct((dil, n, B_WIDTH), BF16)
    res = pl.pallas_call(
        body, grid=(dil,), in_specs=[spec] * 6 + [ANY] * ns, out_specs=[spec] * 3 + [ANY] * ns,
        out_shape=[shape] * 3 + (_scattered_shapes(scatter[1]) if ns else []),
        scratch_shapes=[pltpu.VMEM((BAND, B_WIDTH), F32)] * 2 + (_scatter_sems(ns) if ns else []),
        compiler_params=_cparams("arbitrary" if ns else "parallel"), name=name)(q, k, v, do, lse, delta,
                                                                             *(scatter[0] if ns else []))
    return res[0], res[1], res[2], list(res[3:])


def _attn_combine(outs, lses, gb, mixed, name, gather=None):
    s = mixed.shape[0]
    npat = len(DILATIONS)
    w = B_WIDTH
    ng = 0 if gather is None else len(gather)

    def body(*refs):
        o_refs, l_refs = refs[:npat], refs[npat:2 * npat]
        g_ref = refs[2 * npat]
        ob_ref = refs[2 * npat + 2 + ng]
        lse_refs = refs[2 * npat + 3 + ng:3 * npat + 3 + ng]
        mb_ref = refs[3 * npat + 3 + ng]
        stage = refs[3 * npat + 4 + 2 * ng]
        if ng:
            start, relay, finish = _gather_steps(refs[2 * npat + 2:2 * npat + 2 + ng],
                                                 refs[3 * npat + 4 + ng:3 * npat + 4 + 2 * ng],
                                                 *refs[3 * npat + 5 + 2 * ng:])
            first, last = _grid_edges((s // TR,))
            pl.when(first)(start)
        os_ = [o_refs[0][...].astype(F32)] + [_load_classes(r, stage, d) for r, d in zip(o_refs[1:], CLASS_DILS)]
        ls = [l_refs[0][...]] + [_load_classes(r, stage, d) for r, d in zip(l_refs[1:], CLASS_DILS)]
        mx = functools.reduce(jnp.maximum, ls)
        ws = [jnp.exp(l - mx) for l in ls]
        tot = functools.reduce(lambda a, b: a + b, ws)
        ob = functools.reduce(lambda a, b: a + b, [wt / tot * o for wt, o in zip(ws, os_)])
        ob_ref[...] = ob
        lse = mx + jnp.log(tot)
        _stage_put(stage, lse)
        lse_refs[0][...] = lse
        for ref, d in zip(lse_refs[1:], CLASS_DILS):
            _store_classes(stage, ref, d)
        mb_ref[...] = (ob * _rsq_mean(ob) * g_ref[...]).astype(BF16)

        if ng:
            @pl.when(last)
            def _():
                relay()
                finish()

    lay_specs = [_row_spec(w)] + [_class_spec(d) for d in CLASS_DILS]
    res = pl.pallas_call(
        body, grid=(s // TR,), in_specs=lay_specs * 2 + [_vec_spec(w), ANY] + [ANY] * ng,
        out_specs=[_row_spec(w)] + lay_specs + [_row_spec(w, 1)] + [ANY] * ng,
        out_shape=[jax.ShapeDtypeStruct((s, w), F32), jax.ShapeDtypeStruct((s, w), F32)]
        + [_class_shape(s, d, F32) for d in CLASS_DILS] + [jax.ShapeDtypeStruct(mixed.shape, mixed.dtype)]
        + _gathered_shapes(gather or []),
        scratch_shapes=[STAGE] + (_gather_sems(ng) if ng else []), input_output_aliases={2 * npat + 1: npat + 1},
        compiler_params=_cparams("arbitrary" if ng else "parallel"), name=name)(*outs, *lses, gb, mixed,
                                                                              *(gather or []))
    return res[0], dict(zip(DILATIONS, res[1:npat + 1])), res[npat + 1], list(res[npat + 2:])


def _attn_bwd_prep(dmixed, ob, gb, name):
    s = ob.shape[0]
    w = B_WIDTH
    nlay = len(DILATIONS)

    def body(dm_ref, ob_ref, g_ref, *rest):
        do_refs, dl_refs = rest[:nlay], rest[nlay:2 * nlay]
        dg_ref, stage = rest[2 * nlay:]
        _acc_init([dg_ref])
        ob = ob_ref[...]
        dob, dgt = _rms_bwd(ob, _rsq_mean(ob), g_ref[...], dm_ref[...])
        dg_ref[...] += _colsum(dgt)
        _stage_put(stage, dob)
        do_refs[0][...] = dob.astype(BF16)
        for ref, d in zip(do_refs[1:], CLASS_DILS):
            _store_classes(stage, ref, d)
        lo, hi = _head_masks()
        t = dob * ob
        for b, sl in enumerate(_lane_blocks(w)):
            tb = t[:, sl]
            s0 = jnp.sum(jnp.where(lo, tb, 0.0), axis=1, keepdims=True)
            s1 = jnp.sum(jnp.where(hi, tb, 0.0), axis=1, keepdims=True)
            stage[b] = jnp.where(lo, s0, s1)
        dl_refs[0][...] = _stage_get(stage)
        for ref, d in zip(dl_refs[1:], CLASS_DILS):
            _store_classes(stage, ref, d)

    lay_specs = [_row_spec(w)] + [_class_spec(d) for d in CLASS_DILS]
    shapes = lambda dt: [jax.ShapeDtypeStruct((s, w), dt)] + [_class_shape(s, d, dt) for d in CLASS_DILS]
    res = pl.pallas_call(
        body, grid=(s // TR,), in_specs=[_row_spec(w, 1), _row_spec(w), _vec_spec(w)],
        out_specs=lay_specs * 2 + [_vec_spec(w)],
        out_shape=shapes(BF16) + shapes(F32) + [jax.ShapeDtypeStruct((1, w), F32)],
        scratch_shapes=[STAGE],
        compiler_params=_cparams("arbitrary"), name=name)(dmixed, ob, gb)
    return dict(zip(DILATIONS, res[:nlay])), dict(zip(DILATIONS, res[nlay:2 * nlay])), res[2 * nlay]


def _attn_bwd(q, k, v, do, lse, delta, name, scatter=None):
    dil, n, _ = q.shape
    nb = n // BAND
    ns = 0 if scatter is None else len(scatter[0])

    def body(*refs):
        q_ref, kc_ref, kp_ref, vc_ref, vp_ref, do_ref, lse_ref, dl_ref = refs[:8]
        dq_ref, dk_ref, dv_ref = refs[8 + ns:11 + ns]
        ck_ref, cv_ref = refs[11 + 2 * ns:13 + 2 * ns]
        i = pl.program_id(1)
        if ns:
            start, finish = _scatter_steps(refs[8:8 + ns], refs[11 + ns:11 + 2 * ns], *refs[13 + 2 * ns:],
                                           scatter[1])
            first, last = _grid_edges((dil, nb + 1))
            pl.when(first)(start)

        @pl.when(i == 0)
        def _():
            ck_ref[...] = jnp.zeros_like(ck_ref)
            cv_ref[...] = jnp.zeros_like(cv_ref)

        @pl.when(i < nb)
        def _():
            valid = _band_mask(i)
            valid = jnp.concatenate([valid, valid], axis=0)
            lo, _ = _head_masks()
            lane = lax.broadcasted_iota(jnp.int32, (1, LANES), 1)

            def per_head(t):
                return jnp.concatenate(
                    [jnp.sum(jnp.where(lane == first, t, 0.0), axis=1, keepdims=True) for first in (0, HEAD_DIM)], axis=0)

            for sl in _lane_blocks(B_WIDTH):
                q2 = _stack_heads(q_ref[:, sl])
                do2 = _stack_heads(do_ref[:, sl])
                kk = jnp.concatenate([kp_ref[:, sl], kc_ref[:, sl]], axis=0)
                vv = jnp.concatenate([vp_ref[:, sl], vc_ref[:, sl]], axis=0)
                p = jnp.where(valid, jnp.exp(_dot(q2, kk, NT) - per_head(lse_ref[:, sl])), 0.0)
                ds = (p * (_dot(do2, vv, NT) - per_head(dl_ref[:, sl]))).astype(BF16)
                dq = _dot(ds, kk, NN)
                dkk = _dot(ds, q2, TN)
                dvv = _dot(p.astype(BF16), do2, TN)
                dq_ref[:, sl] = jnp.where(lo, dq[:BAND], dq[BAND:]).astype(BF16)
                dk_ref[:, sl] = (ck_ref[:, sl] + dkk[:BAND]).astype(BF16)
                dv_ref[:, sl] = (cv_ref[:, sl] + dvv[:BAND]).astype(BF16)
                ck_ref[:, sl] = dkk[BAND:]
                cv_ref[:, sl] = dvv[BAND:]

        @pl.when(i == nb)
        def _():
            dk_ref[...] = ck_ref[...].astype(BF16)
            dv_ref[...] = cv_ref[...].astype(BF16)

        if ns:
            pl.when(last)(finish)

    cur, prev = _attn_specs(nb - 1)
    lag = pl.BlockSpec((None, BAND, B_WIDTH), lambda r, i: (r, jnp.maximum(i - 1, 0), 0))
    shape = jax.ShapeDtypeStruct((dil, n, B_WIDTH), BF16)
    res = pl.pallas_call(
        body, grid=(dil, nb + 1), in_specs=[cur, cur, prev, cur, prev, cur, cur, cur] + [ANY] * ns,
        out_specs=[cur, lag, lag] + [ANY] * ns,
        out_shape=[shape] * 3 + (_scattered_shapes(scatter[1]) if ns else []),
        scratch_shapes=[pltpu.VMEM((BAND, B_WIDTH), F32)] * 2 + (_scatter_sems(ns) if ns else []),
        compiler_params=_cparams("arbitrary", "arbitrary"), name=name)(q, k, k, v, v, do, lse, delta,
                                                                      *(scatter[0] if ns else []))
    return res[0], res[1], res[2], list(res[3:])


def _rope_bwd(dqs, dks, dvs, tabs, dproj, name):
    s = dproj.shape[0]
    half = ROT_DIM // 2
    scale = HEAD_DIM ** -0.5
    npat = len(DILATIONS)
    w = B_WIDTH

    def body(*refs):
        groups = [refs[g * npat:(g + 1) * npat] for g in range(3)]
        c_ref, s1_ref, s2_ref, _, o_ref, stage = refs[3 * npat:]

        def total(rs):
            acc = rs[0][...].astype(F32)
            for ref, d in zip(rs[1:], CLASS_DILS):
                acc = acc + _load_classes(ref, stage, d)
            return acc

        def unrope(g):
            c, s1, s2 = c_ref[...], s1_ref[...], s2_ref[...]
            for sl in _lane_blocks(w):
                gb = g[:, sl]
                o = gb * c + pltpu.roll(gb * s1, half, 1) + pltpu.roll(gb * s2, LANES - half, 1)
                o_ref[:, sl] = o.astype(BF16)

        which = pl.program_id(1)

        @pl.when(which == 0)
        def _():
            unrope(total(groups[0]) * scale)

        @pl.when(which == 1)
        def _():
            unrope(total(groups[1]))

        @pl.when(which == 2)
        def _():
            o_ref[...] = total(groups[2]).astype(BF16)

    tab = pl.BlockSpec((TR, LANES), lambda i, j: (i, 0))
    nat = pl.BlockSpec((TR, w), lambda i, j: (i, 0))
    lay_specs = [nat] + [_class_spec(d) for d in CLASS_DILS]
    first_col = 2 * A_WIDTH // w
    return pl.pallas_call(
        body, grid=(s // TR, 3), in_specs=lay_specs * 3 + [tab] * 3 + [ANY],
        out_specs=pl.BlockSpec((TR, w), lambda i, j: (i, first_col + j)),
        out_shape=jax.ShapeDtypeStruct(dproj.shape, dproj.dtype), scratch_shapes=[STAGE],
        input_output_aliases={3 * npat + 3: 0},
        compiler_params=_cparams("parallel", "arbitrary"), name=name)(*dqs, *dks, *dvs, *tabs, dproj)


TK = 512
HALO = 16


def _row_of(v, r):
    rows = lax.broadcasted_iota(jnp.int32, (v.shape[0], 1), 0)
    return jnp.sum(jnp.where(rows == r, v, 0.0), axis=0, keepdims=True)


def _taps_before(x, halo):
    row = lax.broadcasted_iota(jnp.int32, (x.shape[0], 1), 0)
    m1 = jnp.where(row == 0, _row_of(halo, HALO - 1), pltpu.roll(x, 1, 0))
    m2 = jnp.where(row == 0, _row_of(halo, HALO - 2), jnp.where(row == 1, _row_of(halo, HALO - 1), pltpu.roll(x, 2, 0)))
    return m2, m1, x


def _taps_after(x, halo):
    rows = x.shape[0]
    row = lax.broadcasted_iota(jnp.int32, (rows, 1), 0)
    p1 = jnp.where(row == rows - 1, _row_of(halo, 0), pltpu.roll(x, rows - 1, 0))
    p2 = jnp.where(row == rows - 2, _row_of(halo, 0), jnp.where(row == rows - 1, _row_of(halo, 1), pltpu.roll(x, rows - 2, 0)))
    return p1, p2


def _conv_value(taps, cw_ref, cb_ref, h):
    return cb_ref[h] + cw_ref[h, 0:1, :] * taps[0] + cw_ref[h, 1:2, :] * taps[1] + cw_ref[h, 2:3, :] * taps[2]


def _ffn_weight_specs(ncol):
    per_up = (2 * D_FF // N_CHIPS) // TK
    per_dn = (D_FF // N_CHIPS) // TK
    wg = pl.BlockSpec((None, None, D_MODEL, TK), lambda i, j: (j // per_up, 0, 0, j % per_up))
    wv = pl.BlockSpec((None, None, D_MODEL, TK), lambda i, j: ((j + ncol) // per_up, 0, 0, (j + ncol) % per_up))
    wd = pl.BlockSpec((None, None, TK, D_MODEL), lambda i, j: (j // per_dn, 0, j % per_dn, 0))
    cw = pl.BlockSpec((2, 3, TK), lambda i, j: (0, 0, j))
    cb = pl.BlockSpec((2, 1, TK), lambda i, j: (0, 0, j))
    return wg, wv, wd, cw, cb


def _ffn_forward(h2, w_up, w_down, cw3, cb3, name, gather=None, post=None):
    s = h2.shape[0]
    nm, ncol = s // TM, D_FF // TK
    ng = 0 if gather is None else len(gather)
    npost = 0 if post is None else 3
    nout = 4 + (2 if post else 0)

    def body(*refs):
        h_ref, wg_ref, wv_ref, wd_ref, cw_ref, cb_ref = refs[:6]
        post_in = refs[6:6 + npost]
        g_in = refs[6 + npost:6 + npost + ng]
        outs = refs[6 + npost + ng:6 + npost + ng + nout]
        y_ref, up_ref, cv_ref, f_ref = outs[:4]
        g_out = refs[6 + npost + ng + nout:6 + npost + 2 * ng + nout]
        carry, acc = refs[6 + npost + 2 * ng + nout:8 + npost + 2 * ng + nout]
        i, j = pl.program_id(0), pl.program_id(1)
        if ng:
            start, relay, finish = _gather_steps(g_in, g_out, *refs[8 + npost + 2 * ng + nout:])
            pl.when((i == 0) & (j == 0))(start)
            pl.when((i == nm - 1) & (j == 0))(relay)

        @pl.when((i == 0) & (j == 0))
        def _():
            carry[...] = jnp.zeros_like(carry)

        h = h_ref[...]
        conv = []
        for hh, w_ref in ((0, wg_ref), (1, wv_ref)):
            up = _dot(h, w_ref[...], NN).astype(BF16)
            up_ref[hh] = up
            x = up.astype(F32)
            conv.append(_conv_value(_taps_before(x, carry[j, hh]), cw_ref, cb_ref, hh))
            cv_ref[hh] = conv[hh].astype(BF16)
            carry[j, hh] = x[TM - HALO:, :]
        y = (_gelu_tanh(conv[0])[0] * conv[1]).astype(BF16)
        y_ref[...] = y
        part = _dot(y, wd_ref[...], NN)

        @pl.when(j == 0)
        def _():
            acc[...] = part

        @pl.when(j > 0)
        def _():
            acc[...] += part

        @pl.when(j == ncol - 1)
        def _():
            f = acc[...]
            f_ref[...] = f
            if post:
                x1_ref, gp_ref, gn_ref = post_in
                x2 = x1_ref[...] + f * _rsq_mean(f) * gp_ref[...]
                outs[4][...] = x2
                outs[5][...] = (x2 * _rsq_mean(x2) * gn_ref[...]).astype(BF16)

        if ng:
            pl.when((i == nm - 1) & (j == ncol - 1))(finish)

    wg, wv, wd, cw, cb = _ffn_weight_specs(ncol)
    row = pl.BlockSpec((TM, D_MODEL), lambda i, j: (i, 0))
    vec = pl.BlockSpec((1, D_MODEL), lambda i, j: (0, 0))
    res = pl.pallas_call(
        body, grid=(nm, ncol),
        in_specs=[row, wg, wv, wd, cw, cb] + ([row, vec, vec] if post else []) + [ANY] * ng,
        out_specs=[pl.BlockSpec((TM, TK), lambda i, j: (i, j)), pl.BlockSpec((2, TM, TK), lambda i, j: (0, i, j)),
                   pl.BlockSpec((2, TM, TK), lambda i, j: (0, i, j)), row] + ([row, row] if post else [])
        + [ANY] * ng,
        out_shape=[jax.ShapeDtypeStruct((s, D_FF), BF16), jax.ShapeDtypeStruct((2, s, D_FF), BF16),
                   jax.ShapeDtypeStruct((2, s, D_FF), BF16), jax.ShapeDtypeStruct((s, D_MODEL), F32)]
        + ([jax.ShapeDtypeStruct((s, D_MODEL), F32), jax.ShapeDtypeStruct((s, D_MODEL), BF16)] if post else [])
        + _gathered_shapes(gather or []),
        scratch_shapes=[pltpu.VMEM((ncol, 2, HALO, TK), F32), pltpu.VMEM((TM, D_MODEL), F32)]
        + (_gather_sems(ng) if ng else []),
        compiler_params=_cparams("arbitrary", "arbitrary"), name=name)(h2, w_up, w_up, w_down, cw3, cb3,
                                                                      *(post or []), *(gather or []))
    return res[:nout], list(res[nout:])


def _ffn_backward(df, w_up, w_down, up3, cv3, cw3, name, scatter=None):
    s = df.shape[0]
    nm, ncol = s // TM, D_FF // TK
    ns = 0 if scatter is None else len(scatter[0])

    def body(*refs):
        df_ref, wg_ref, wv_ref, wd_ref, cw_ref, up_ref, cv_ref = refs[:7]
        s_in = refs[7:7 + ns]
        dup_ref, dh_ref, sums_ref = refs[7 + ns:10 + ns]
        s_out = refs[10 + ns:10 + 2 * ns]
        carry, acc = refs[10 + 2 * ns:12 + 2 * ns]
        i, j = pl.program_id(0), pl.program_id(1)
        if ns:
            start, finish = _scatter_steps(s_in, s_out, *refs[12 + 2 * ns:], scatter[1])
            pl.when((i == 0) & (j == 0))(start)

        @pl.when((i == 0) & (j == 0))
        def _():
            carry[...] = jnp.zeros_like(carry)
            sums_ref[...] = jnp.zeros_like(sums_ref)

        dy = _dot(df_ref[...], wd_ref[...], NT)
        act, grad = _gelu_tanh(cv_ref[0].astype(F32))
        dcs = (dy * cv_ref[1].astype(F32) * grad, dy * act)
        row = lax.broadcasted_iota(jnp.int32, (8, 1), 0)
        part = None
        for hh, w_ref in ((0, wg_ref), (1, wv_ref)):
            dc = dcs[hh]
            x = up_ref[hh].astype(F32)
            after1, after2 = _taps_after(dc, carry[j, hh])
            upd = jnp.zeros((8, TK), F32)
            for ridx, sm in enumerate((_colsum(after2 * x), _colsum(after1 * x), _colsum(dc * x), _colsum(dc))):
                upd = jnp.where(row == ridx, sm, upd)
            sums_ref[j, hh] += upd
            dup = (cw_ref[hh, 2:3, :] * dc + cw_ref[hh, 1:2, :] * after1 + cw_ref[hh, 0:1, :] * after2).astype(BF16)
            carry[j, hh] = dc[:HALO, :]
            dup_ref[hh] = dup
            d = _dot(dup, w_ref[...], NT)
            part = d if part is None else part + d

        @pl.when(j == 0)
        def _():
            acc[...] = part

        @pl.when(j > 0)
        def _():
            acc[...] += part

        @pl.when(j == ncol - 1)
        def _():
            dh_ref[...] = acc[...]

        if ns:
            pl.when((i == nm - 1) & (j == ncol - 1))(finish)

    wg, wv, wd, cw, _ = _ffn_weight_specs(ncol)
    rev = lambda i: nm - 1 - i
    res = pl.pallas_call(
        body, grid=(nm, ncol),
        in_specs=[pl.BlockSpec((TM, D_MODEL), lambda i, j: (rev(i), 0)), wg, wv, wd, cw,
                  pl.BlockSpec((2, TM, TK), lambda i, j: (0, rev(i), j)),
                  pl.BlockSpec((2, TM, TK), lambda i, j: (0, rev(i), j))] + [ANY] * ns,
        out_specs=[pl.BlockSpec((2, TM, TK), lambda i, j: (0, rev(i), j)),
                   pl.BlockSpec((TM, D_MODEL), lambda i, j: (rev(i), 0)),
                   pl.BlockSpec((ncol, 2, 8, TK), lambda i, j: (0, 0, 0, 0))] + [ANY] * ns,
        out_shape=[jax.ShapeDtypeStruct((2, s, D_FF), BF16), jax.ShapeDtypeStruct((s, D_MODEL), F32),
                   jax.ShapeDtypeStruct((ncol, 2, 8, TK), F32)] + (_scattered_shapes(scatter[1]) if ns else []),
        scratch_shapes=[pltpu.VMEM((ncol, 2, HALO, TK), F32), pltpu.VMEM((TM, D_MODEL), F32)]
        + (_scatter_sems(ns) if ns else []),
        compiler_params=_cparams("arbitrary", "arbitrary"), name=name)(df, w_up, w_up, w_down, cw3, up3, cv3,
                                                                      *(scatter[0] if ns else []))
    return res[:3], list(res[3:])


def _wspec(rows, cols, index_map):
    return pl.BlockSpec((None, None, rows, cols), index_map)


def _layer_forward(l, x0, h1, p, wg, tabs, gather=None, late=None, g_next=None):
    s = x0.shape[0]
    nm = s // TMM
    tag = f"_l{l}"
    riders = dict.fromkeys(DILATIONS)
    proj_rider = rope_rider = combine_rider = None
    if late is not None:
        cols = lambda t, parts: [t[:, i * t.shape[1] // parts:(i + 1) * t.shape[1] // parts] for i in range(parts)]
        (down_a, down_b), up_q = cols(late["w_down"], 2), cols(late["w_up"], 4)
        proj_rider, rope_rider, combine_rider = [late["w_out"], down_a], [up_q[2]], [up_q[3]]
        riders = dict(zip(DILATIONS, ([down_b], [up_q[0]], [up_q[1]])))
    proj = _matmul(
        h1, wg["w_in"], grid=(nm, N_CHIPS), a_spec=pl.BlockSpec((TMM, D_MODEL), lambda i, j: (i, 0)),
        b_spec=_wspec(D_MODEL, IN_COLS // N_CHIPS, lambda i, j: (j, 0, 0, 0)),
        o_spec=pl.BlockSpec((TMM, IN_COLS // N_CHIPS), lambda i, j: (i, j)), o_shape=(s, IN_COLS), o_dtype=BF16,
        dims=NN, nk=1, kaxis=None, acc_shape=None, name="proj" + tag, gather=proj_rider)
    if late is not None:
        proj, (w_out_all4, down_a) = proj
    ma = _mixer_a_fwd(proj, p["v_norm_g"], p["v_norm_b"], p["w_spatial"], p["bs_full"], p["out_norm_a"],
                      "mixer_a_fwd" + tag)
    q, k, v, rope_landed = _rope_fwd(proj, tabs, "rope_fwd" + tag, rope_rider)
    whole = lambda d: s // d // BAND <= MAX_CLASS_BLOCKS
    outs, lses, landed = zip(*[
        (_attn_fwd_class if whole(d) else _attn_fwd)(
            _as_classes(q[d]), _as_classes(k[d]), _as_classes(v[d]), f"attn_fwd_d{d}" + tag, riders[d])
        for d in DILATIONS])
    outs = [o.reshape(s, B_WIDTH) if d == 1 else o for o, d in zip(outs, DILATIONS)]
    lses = [t.reshape(s, B_WIDTH) if d == 1 else t for t, d in zip(lses, DILATIONS)]
    ob, lse, mixed, combine_landed = _attn_combine(outs, lses, p["out_norm_b"], ma, "attn_combine" + tag,
                                                   combine_rider)
    if late is not None:
        wg = dict(wg, w_out=w_out_all4, w_down=jnp.concatenate([down_a, landed[0][0]], axis=-1),
                  w_up=jnp.concatenate([landed[1][0], landed[2][0], rope_landed[0], combine_landed[0]], axis=-1))
    y1, x1, h2 = _mix_out_norm(mixed, wg["w_out"], x0, p["post_mix_norm"], p["pre_ffn_norm"], "mix_out" + tag)
    post = None if g_next is None else (x1, p["post_ffn_norm"], g_next)
    (y, up3, cv3, f, *after), gathered = _ffn_forward(h2, wg["w_up"], wg["w_down"], p["cw3"], p["cb3"],
                                                      "ffn_fwd" + tag, gather, post)
    saved = dict(x0=x0, h1=h1, proj=proj, q=q, k=k, v=v, ob=ob, lse=lse, mixed=mixed, y1=y1, x1=x1, h2=h2,
                 up3=up3, cv3=cv3, y=y, f=f)
    if after:
        saved.update(x2=after[0], h_next=after[1])
    return saved, gathered, wg


def _layer_backward(l, dx2, df, sv, p, wg, tabs, pos, scatter=None, hide=False):
    s = dx2.shape[0]
    nm = s // TMM
    tag = f"_l{l}"
    g = {}
    (dup3, dh2, conv_sums), scattered = _ffn_backward(df, wg["w_up"], wg["w_down"], sv["up3"], sv["cv3"], p["cw3"],
                                                      "ffn_bwd" + tag, scatter)
    sums = conv_sums.transpose(1, 2, 0, 3).reshape(2, 8, D_FF)
    g["conv_w"] = jnp.concatenate([sums[0, :3], sums[1, :3]], axis=1)
    g["conv_b"] = jnp.concatenate([sums[0, 3:4], sums[1, 3:4]], axis=1)
    tn = 1024
    done = {}
    gw_down = _matmul(
        sv["y"], df, grid=(D_FF // tn,), a_spec=pl.BlockSpec((s, tn), lambda k: (0, k)),
        b_spec=pl.BlockSpec((s, D_MODEL), lambda k: (0, 0)),
        o_spec=pl.BlockSpec((2, tn, D_MODEL // 2), lambda k: (0, k, 0)),
        o_shape=(2, D_FF, D_MODEL // 2), o_dtype=BF16,
        dims=TN, nk=1, kaxis=None, acc_shape=None, name="w_down_grad" + tag, halves=True)
    down_sums = _chip_sums(l, dict(w_down=gw_down), pos, ("w_down",)) if hide else None
    gw_up = _matmul(
        sv["h2"], dup3, grid=(2 * D_FF // tn,), a_spec=pl.BlockSpec((s, D_MODEL), lambda n: (0, 0)),
        b_spec=pl.BlockSpec((None, s, tn), lambda n: (n // (D_FF // tn), 0, n % (D_FF // tn))),
        o_spec=pl.BlockSpec((None, D_MODEL, tn), lambda n: (n // 2, 0, n % 2)),
        o_shape=(N_CHIPS, D_MODEL, 2 * D_FF // N_CHIPS), o_dtype=BF16,
        dims=TN, nk=1, kaxis=None, acc_shape=None, name="w_up_grad" + tag,
        scatter=(down_sums, ("w_down",)) if hide else None)
    up_sums = None
    if hide:
        gw_up, received = gw_up
        done[("w_down",)] = (down_sums, received)
        up_sums = _chip_sums(l, dict(w_up=gw_up), pos, ("w_up",))
    dx1, dy1, g["pre_ffn_norm"], g["post_mix_norm"] = _norm_bwd_mid(
        dx2, dh2, sv["x1"], p["pre_ffn_norm"], sv["y1"], p["post_mix_norm"], "norm_bwd_mid" + tag)
    w_out_all = pl.BlockSpec((N_CHIPS, None, D_MODEL // N_CHIPS, D_MODEL), lambda i: (0, 0, 0, 0))
    dmixed = _matmul(
        dy1, wg["w_out"], grid=(nm,), a_spec=pl.BlockSpec((TMM, D_MODEL), lambda i: (i, 0)), b_spec=w_out_all,
        o_spec=pl.BlockSpec((TMM, D_MODEL), lambda i: (i, 0)), o_shape=(s, D_MODEL), o_dtype=F32,
        dims=NT, nk=1, kaxis=None, acc_shape=None, name="mix_out_bwd" + tag, b_2d=(D_MODEL, D_MODEL))
    gw_out = _matmul(
        sv["mixed"], dy1, grid=(1,), a_spec=pl.BlockSpec((s, D_MODEL), lambda m: (0, 0)),
        b_spec=pl.BlockSpec((s, D_MODEL), lambda m: (0, 0)),
        o_spec=pl.BlockSpec((2, D_MODEL, D_MODEL // 2), lambda m: (0, 0, 0)),
        o_shape=(2, D_MODEL, D_MODEL // 2), o_dtype=BF16,
        dims=TN, nk=1, kaxis=None, acc_shape=None, name="w_out_grad" + tag, halves=True)
    dpa, g["out_norm_a"], g["v_norm_g"], g["v_norm_b"], dbs, g["w_spatial"] = _mixer_a_bwd(
        sv["proj"], dmixed, p["v_norm_g"], p["v_norm_b"], p["w_spatial"], p["bs_full"], p["out_norm_a"],
        "mixer_a_bwd" + tag)
    g["b_spatial"] = dbs[:, ::GROUP_DIM].T
    dob, delta, g["out_norm_b"] = _attn_bwd_prep(dmixed, sv["ob"], p["out_norm_b"], "attn_bwd_prep" + tag)
    whole = lambda d: s // d // BAND <= MAX_CLASS_BLOCKS
    halves = dict(zip(DILATIONS, ("w_up:0", "w_up:1"))) if hide else {}
    dqs, dks, dvs, received = zip(*[
        (_attn_bwd_class if whole(d) else _attn_bwd)(
            *(_as_classes(t[d]) for t in (sv["q"], sv["k"], sv["v"], dob, sv["lse"], delta)),
            f"attn_bwd_d{d}" + tag, (up_sums, (halves[d],)) if d in halves else None)
        for d in DILATIONS])
    if hide:
        done[("w_up",)] = (up_sums, [jnp.concatenate([received[0][0], received[1][0]], axis=-1)])
    nat = lambda ts: [t.reshape(s, B_WIDTH) if d == 1 else t for t, d in zip(ts, DILATIONS)]
    dproj = _rope_bwd(nat(dqs), nat(dks), nat(dvs), tabs, dpa, "rope_bwd" + tag)
    wcol = IN_COLS // N_CHIPS
    dh1 = _matmul(
        dproj, wg["w_in"], grid=(nm, N_CHIPS), a_spec=pl.BlockSpec((TMM, wcol), lambda i, n: (i, n)),
        b_spec=_wspec(D_MODEL, wcol, lambda i, n: (n, 0, 0, 0)),
        o_spec=pl.BlockSpec((TMM, D_MODEL), lambda i, n: (i, 0)), o_shape=(s, D_MODEL), o_dtype=F32,
        dims=NT, nk=N_CHIPS, kaxis=1, acc_shape=(TMM, D_MODEL), name="proj_bwd" + tag)
    gw_in = _matmul(
        sv["h1"], dproj, grid=(N_CHIPS,), a_spec=pl.BlockSpec((s, D_MODEL), lambda n: (0, 0)),
        b_spec=pl.BlockSpec((s, wcol), lambda n: (0, n)),
        o_spec=pl.BlockSpec((None, D_MODEL, wcol), lambda n: (n, 0, 0)),
        o_shape=(N_CHIPS, D_MODEL, wcol), o_dtype=BF16,
        dims=TN, nk=1, kaxis=None, acc_shape=None, name="w_in_grad" + tag)
    big = dict(w_in=gw_in, w_out=gw_out) if hide else dict(w_in=gw_in, w_up=gw_up, w_out=gw_out, w_down=gw_down)
    return dx1, dh1, big, g, scattered, done


SMALL = ("pre_mix_norm", "v_norm_g", "v_norm_b", "w_spatial", "b_spatial", "out_norm_a", "out_norm_b",
         "post_mix_norm", "pre_ffn_norm", "conv_b", "post_ffn_norm")
BIG = ("w_in", "w_out", "w_up", "w_down")
DEPTH = 2


def _layer_params(l, small, conv_w_full):
    p = {n: small[n][l].reshape(1, -1) for n in SMALL if n not in ("w_spatial", "b_spatial")}
    p["w_spatial"] = small["w_spatial"][l]
    p["bs_full"] = jnp.repeat(small["b_spatial"][l].T, GROUP_DIM, axis=1)
    p["cw3"] = conv_w_full[l].reshape(3, 2, D_FF).transpose(1, 0, 2)
    p["cb3"] = small["conv_b"][l].reshape(2, 1, D_FF)
    return p


def _mesh_pos():
    return lax.axis_index("x"), lax.axis_index("y"), lax.axis_index("c")


def _other_chips(x, y):
    return [(1 - x, y), (x, 1 - y), (1 - x, 1 - y)]


def _gathered_shapes(blocks):
    return [jax.ShapeDtypeStruct((N_CHIPS, 1) + a.shape, a.dtype) for a in blocks]


def _gather_sems(nw):
    n = 2 * nw * (N_CHIPS - 1) + nw
    return [pltpu.SemaphoreType.DMA((n,)), pltpu.SemaphoreType.DMA((n,))]


def _gather_steps(ins, outs, send, recv):
    nw, nrel = len(ins), N_CHIPS - 1
    x, y, c = _mesh_pos()
    mine, sibling, chips = 2 * x + y, (x, y, 1 - c), _other_chips(x, y)

    def copy(src, dst, slot, to):
        return pltpu.make_async_remote_copy(src_ref=src, dst_ref=dst, send_sem=send.at[slot],
                                            recv_sem=recv.at[slot], device_id=to, device_id_type=MESH)

    def half_rows(t, core):
        rows = ins[t].shape[0] // 2
        return pl.ds(pl.multiple_of(core * rows, rows), rows)

    def landing(t, chip, core):
        return outs[t].at[chip, 0, half_rows(t, core), :]

    slots = [(t, r, chip) for t in range(nw) for r, chip in enumerate(chips)]
    own = [copy(ins[t], outs[t].at[mine, 0], 2 * nw * nrel + t, sibling) for t in range(nw)]
    first = [copy(ins[t].at[half_rows(t, c), :], landing(t, mine, c), t * nrel + r, (px, py, c))
             for t, r, (px, py) in slots]
    relays = [copy(landing(t, 2 * px + py, c), landing(t, 2 * px + py, c), nw * nrel + t * nrel + r, sibling)
              for t, r, (px, py) in slots]

    def start():
        for cp in own + first:
            cp.start()

    def relay():
        for (t, r, (px, py)), cp in zip(slots, relays):
            copy(landing(t, 2 * px + py, c), landing(t, 2 * px + py, c), t * nrel + r, (px, py, c)).wait_recv()
            cp.start()

    def finish():
        for t, r, (px, py) in slots:
            passed = landing(t, 2 * px + py, 1 - c)
            copy(passed, passed, nw * nrel + t * nrel + r, sibling).wait_recv()
        for cp in first + relays:
            cp.wait_send()
        for cp in own:
            cp.wait()

    return start, relay, finish


def _gather_weights(blocks, name):
    nw = len(blocks)

    def body(*refs):
        start, relay, finish = _gather_steps(refs[:nw], refs[nw:2 * nw], *refs[2 * nw:])
        start()
        relay()
        finish()

    return pl.pallas_call(
        body, in_specs=[ANY] * nw, out_specs=[ANY] * nw, out_shape=_gathered_shapes(blocks),
        scratch_shapes=_gather_sems(nw), name=name)(*blocks)


HALF = 512

GRAD_GEOM = {"w_in": ("rows", D_MODEL, IN_COLS // N_CHIPS), "w_up": ("rows", D_MODEL, 2 * D_FF // N_CHIPS),
             "w_out": ("cols", D_MODEL, D_MODEL // N_CHIPS), "w_down": ("cols", D_FF, D_FF // N_CHIPS)}


def _exchange_shape(n):
    kind, a, b = GRAD_GEOM[n]
    return (N_CHIPS, HALF, b) if kind == "rows" else (a, HALF)


def _piece_shape(n):
    name, _, part = n.partition(":")
    kind, _, b = GRAD_GEOM[name]
    if part:
        assert kind == "rows"
        return (HALF, b // 2)
    return (HALF, b) if kind == "rows" else (b, HALF)


def _half_of(ref, n, core):
    if GRAD_GEOM[n][0] == "rows":
        return ref.at[:, pl.ds(pl.multiple_of(core * HALF, HALF), HALF), :]
    return ref.at[core]


def _piece_of(ref, n, chip):
    name, _, part = n.partition(":")
    kind, _, b = GRAD_GEOM[name]
    if part:
        return ref.at[chip, :, pl.ds(int(part) * (b // 2), b // 2)]
    return ref.at[chip] if kind == "rows" else ref.at[pl.ds(pl.multiple_of(chip * b, b), b), :]


def _pair_exchange(g, names, name):
    n = len(names)

    def body(*refs):
        send, recv = refs[2 * n:]
        x, y, c = _mesh_pos()
        o = 1 - c
        cps = [pltpu.make_async_remote_copy(src_ref=_half_of(refs[t], nm, o), dst_ref=refs[n + t], send_sem=send.at[t],
                                            recv_sem=recv.at[t], device_id=(x, y, o), device_id_type=MESH)
               for t, nm in enumerate(names)]
        for cp in cps:
            cp.start()
        for cp in cps:
            cp.wait()

    return pl.pallas_call(
        body, in_specs=[ANY] * n, out_specs=[ANY] * n,
        out_shape=[jax.ShapeDtypeStruct(_exchange_shape(nm), BF16) for nm in names],
        scratch_shapes=[pltpu.SemaphoreType.DMA((n,)), pltpu.SemaphoreType.DMA((n,))],
        name=name)(*[g[nm] for nm in names])


def _pair_sum(g, recv, pos, names, name_prefix):
    def add(a, b, grid, a_spec, b_spec, name):
        def body(pos_ref, a_ref, b_ref, o_ref):
            o_ref[...] = (a_ref[...].astype(F32) + b_ref[...].astype(F32)).astype(BF16)

        return pl.pallas_call(
            body, grid_spec=pltpu.PrefetchScalarGridSpec(
                num_scalar_prefetch=1, grid=grid, in_specs=[a_spec, b_spec], out_specs=b_spec),
            out_shape=jax.ShapeDtypeStruct(b.shape, BF16), compiler_params=_cparams("parallel"), name=name)(pos, a, b)

    out = []
    for nm, r in zip(names, recv):
        kind, rows, width = GRAD_GEOM[nm]
        if kind == "rows":
            out.append(add(g[nm], r, (N_CHIPS,), pl.BlockSpec((None, HALF, width), lambda j, pos: (j, pos[2], 0)),
                           pl.BlockSpec((None, HALF, width), lambda j, pos: (j, 0, 0)), f"{name_prefix}_{nm}"))
        else:
            out.append(add(g[nm], r, (rows // D_MODEL,), pl.BlockSpec((None, D_MODEL, HALF), lambda j, pos: (pos[2], j, 0)),
                           pl.BlockSpec((D_MODEL, HALF), lambda j, pos: (j, 0)), f"{name_prefix}_{nm}"))
    return out


def _scattered_shapes(names):
    return [jax.ShapeDtypeStruct((N_CHIPS - 1,) + _piece_shape(nm), BF16) for nm in names]


def _scatter_sems(n):
    return [pltpu.SemaphoreType.DMA((n * (N_CHIPS - 1),)), pltpu.SemaphoreType.DMA((n * (N_CHIPS - 1),))]


def _scatter_steps(sums, outs, send, recv, names):
    nrel = N_CHIPS - 1
    x, y, c = _mesh_pos()
    cps = []
    for r, (px, py) in enumerate(_other_chips(x, y)):
        for t, nm in enumerate(names):
            cps.append(pltpu.make_async_remote_copy(
                src_ref=_piece_of(sums[t], nm, 2 * px + py), dst_ref=outs[t].at[r], send_sem=send.at[t * nrel + r],
                recv_sem=recv.at[t * nrel + r], device_id=(px, py, c), device_id_type=MESH))

    def start():
        for cp in cps:
            cp.start()

    def finish():
        for cp in cps:
            cp.wait()

    return start, finish


def _chip_scatter(sums, names, name):
    n = len(names)

    def body(*refs):
        start, finish = _scatter_steps(refs[:n], refs[n:2 * n], *refs[2 * n:], names)
        start()
        finish()

    return pl.pallas_call(
        body, in_specs=[ANY] * n, out_specs=[ANY] * n, out_shape=_scattered_shapes(names),
        scratch_shapes=_scatter_sems(n), name=name)(*sums)


def _chip_sum(sums, recv, pos, names, name_prefix):
    def add(a, b, a_spec, shape, name):
        def body(pos_ref, a_ref, b_ref, o_ref):
            tot = a_ref[...].astype(F32)
            for r in range(N_CHIPS - 1):
                tot = tot + b_ref[r].astype(F32)
            o_ref[...] = tot

        return pl.pallas_call(
            body, grid_spec=pltpu.PrefetchScalarGridSpec(
                num_scalar_prefetch=1, grid=(1,), in_specs=[a_spec, pl.BlockSpec(b.shape, lambda i, pos: (0, 0, 0))],
                out_specs=pl.BlockSpec((None,) + shape, lambda i, pos: (pos[2], 0, 0))),
            out_shape=jax.ShapeDtypeStruct((2,) + shape, F32), compiler_params=_cparams("arbitrary"),
            name=name)(pos, a, b)

    chip = lambda pos: 2 * pos[0] + pos[1]
    out = []
    for nm, a, b in zip(names, sums, recv):
        shape = _piece_shape(nm)
        if GRAD_GEOM[nm][0] == "rows":
            spec = pl.BlockSpec((None,) + shape, lambda i, pos: (chip(pos), 0, 0))
        else:
            spec = pl.BlockSpec(shape, lambda i, pos: (chip(pos), 0))
        out.append(add(a, b, spec, shape, f"{name_prefix}_{nm}"))
    return out


def _pair_share(totals, name):
    n = len(totals)

    def body(*refs):
        ins, outs = refs[:n], refs[n:2 * n]
        send, recv = refs[2 * n:]
        x, y, c = _mesh_pos()
        o = 1 - c
        cps = [pltpu.make_async_remote_copy(src_ref=ins[t].at[c], dst_ref=outs[t].at[c], send_sem=send.at[t],
                                            recv_sem=recv.at[t], device_id=(x, y, o), device_id_type=MESH)
               for t in range(n)]
        for cp in cps:
            cp.start()
        for t in range(n):
            pltpu.make_async_remote_copy(src_ref=ins[t].at[o], dst_ref=outs[t].at[o], send_sem=send.at[t],
                                         recv_sem=recv.at[t], device_id=(x, y, o), device_id_type=MESH).wait_recv()
        for cp in cps:
            cp.wait_send()

    return pl.pallas_call(
        body, in_specs=[ANY] * n, out_specs=[ANY] * n,
        out_shape=[jax.ShapeDtypeStruct(t.shape, t.dtype) for t in totals],
        scratch_shapes=[pltpu.SemaphoreType.DMA((n,)), pltpu.SemaphoreType.DMA((n,))],
        input_output_aliases={t: t for t in range(n)}, name=name)(*totals)


def _chip_sums(l, g, pos, names):
    tag = f"l{l}_" + "_".join(names)
    recv = _pair_exchange(g, names, "pair_exchange_" + tag)
    return _pair_sum(g, recv, pos, names, "pair_sum_" + tag)


def _gradient_shards(l, sums, scattered, pos, names):
    tag = f"l{l}_" + "_".join(names)
    halves = _pair_share(_chip_sum(sums, scattered, pos, names, "chip_sum_" + tag), "pair_share_" + tag)
    out = {}
    for nm, t in zip(names, halves):
        rows, cols = _piece_shape(nm)
        out[nm] = t.reshape(2 * rows, cols) if GRAD_GEOM[nm][0] == "rows" else t.transpose(1, 0, 2).reshape(rows, 2 * cols)
    return out


N_DEV = 8


def _allreduce_small(packed, name):
    rows = packed.shape[0]

    def body(x_ref, out_ref, gath, send_sems, recv_sems, local_sem):
        x, y, c = _mesh_pos()
        me, sibling = (x, y, c), (x, y, 1 - c)
        chips = _other_chips(x, y)

        def blk(px, py, pc):
            return gath.at[pl.ds(pl.multiple_of((4 * px + 2 * py + pc) * rows, 8), rows), :]

        def copy(k, block, to, src=None):
            return pltpu.make_async_remote_copy(
                src_ref=blk(*block) if src is None else src, dst_ref=blk(*block), send_sem=send_sems.at[k],
                recv_sem=recv_sems.at[k], device_id=to, device_id_type=MESH)

        mine = pltpu.make_async_copy(x_ref, blk(*me), local_sem)
        mine.start()
        first = [copy(0, me, sibling, src=x_ref)]
        first += [copy(1 + j, me, (*chip, c), src=x_ref) for j, chip in enumerate(chips)]
        for cp in first:
            cp.start()
        passed = [copy(4 + j, (*chip, c), sibling) for j, chip in enumerate(chips)]
        for j, chip in enumerate(chips):
            copy(1 + j, (*chip, c), me).wait_recv()
            passed[j].start()
        copy(0, sibling, me).wait_recv()
        for j, chip in enumerate(chips):
            copy(4 + j, (*chip, 1 - c), me).wait_recv()
        for cp in first + passed:
            cp.wait_send()
        mine.wait()
        tot = gath[0:rows, :]
        for d in range(1, N_DEV):
            tot = tot + gath[d * rows:(d + 1) * rows, :]
        out_ref[...] = tot

    vmem = pl.BlockSpec(memory_space=pltpu.VMEM)
    return pl.pallas_call(
        body, in_specs=[vmem], out_specs=vmem, out_shape=jax.ShapeDtypeStruct((rows, LANES), F32),
        scratch_shapes=[pltpu.VMEM((N_DEV * rows, LANES), F32), pltpu.SemaphoreType.DMA((7,)),
                        pltpu.SemaphoreType.DMA((7,)), pltpu.SemaphoreType.DMA],
        compiler_params=pltpu.CompilerParams(vmem_limit_bytes=VMEM_LIMIT_BYTES),
        name=name)(packed)


def _adamw(w, g, m, v, name):
    rows, cols = w.shape
    tr = 256 if rows % 256 == 0 else rows

    def body(w_ref, g_ref, m_ref, v_ref, d_ref, mo_ref, vo_ref):
        gv = g_ref[...]
        mn = ADAM_B1 * m_ref[...] + (1.0 - ADAM_B1) * gv
        vn = ADAM_B2 * v_ref[...] + (1.0 - ADAM_B2) * (gv * gv)
        m_hat = mn / (1.0 - ADAM_B1 ** ADAM_STEP)
        v_hat = vn / (1.0 - ADAM_B2 ** ADAM_STEP)
        d_ref[...] = -ADAM_LR * (m_hat / (jnp.sqrt(v_hat) + ADAM_EPS) + ADAM_WD * w_ref[...])
        mo_ref[...] = mn
        vo_ref[...] = vn

    spec = pl.BlockSpec((tr, cols), lambda i: (i, 0))
    return pl.pallas_call(
        body, grid=(rows // tr,), in_specs=[spec] * 4, out_specs=[spec] * 3,
        out_shape=[jax.ShapeDtypeStruct((rows, cols), F32)] * 3, compiler_params=_cparams("parallel"),
        name=name)(w, g, m, v)


def _adamw_nd(w, g, m, v, name):
    cols = w.shape[-1] if w.shape[-1] % LANES == 0 else LANES
    outs = _adamw(*(t.reshape(-1, cols) for t in (w, g, m, v)), name)
    return tuple(t.reshape(w.shape) for t in outs)


def _pack(arrays):
    return jnp.concatenate([a.reshape(-1, LANES) for a in arrays], axis=0)


def _unpack(packed, shapes):
    out, row = [], 0
    for sh in shapes:
        n = math.prod(sh) // LANES
        out.append(packed[row:row + n].reshape(sh))
        row += n
    return out


WEIGHTS = ("pre_mix_norm", "w_in", "v_norm_g", "v_norm_b", "w_spatial", "b_spatial", "out_norm_a", "out_norm_b",
           "w_out", "post_mix_norm", "pre_ffn_norm", "w_up", "conv_w", "conv_b", "w_down", "post_ffn_norm")


def kernel(x, pre_mix_norm, w_in, v_norm_g, v_norm_b, w_spatial, b_spatial, out_norm_a, out_norm_b, w_out, post_mix_norm, pre_ffn_norm, w_up, conv_w, conv_b, w_down, post_ffn_norm, loss_target, m_pre_mix_norm, m_w_in, m_v_norm_g, m_v_norm_b, m_w_spatial, m_b_spatial, m_out_norm_a, m_out_norm_b, m_w_out, m_post_mix_norm, m_pre_ffn_norm, m_w_up, m_conv_w, m_conv_b, m_w_down, m_post_ffn_norm, v_pre_mix_norm, v_w_in, v_v_norm_g, v_v_norm_b, v_w_spatial, v_b_spatial, v_out_norm_a, v_out_norm_b, v_w_out, v_post_mix_norm, v_pre_ffn_norm, v_w_up, v_conv_w, v_conv_b, v_w_down, v_post_ffn_norm):
    w = dict(pre_mix_norm=pre_mix_norm, w_in=w_in, v_norm_g=v_norm_g, v_norm_b=v_norm_b, w_spatial=w_spatial,
             b_spatial=b_spatial, out_norm_a=out_norm_a, out_norm_b=out_norm_b, w_out=w_out,
             post_mix_norm=post_mix_norm, pre_ffn_norm=pre_ffn_norm, w_up=w_up, conv_w=conv_w, conv_b=conv_b,
             w_down=w_down, post_ffn_norm=post_ffn_norm)
    m = dict(pre_mix_norm=m_pre_mix_norm, w_in=m_w_in, v_norm_g=m_v_norm_g, v_norm_b=m_v_norm_b,
             w_spatial=m_w_spatial, b_spatial=m_b_spatial, out_norm_a=m_out_norm_a, out_norm_b=m_out_norm_b,
             w_out=m_w_out, post_mix_norm=m_post_mix_norm, pre_ffn_norm=m_pre_ffn_norm, w_up=m_w_up,
             conv_w=m_conv_w, conv_b=m_conv_b, w_down=m_w_down, post_ffn_norm=m_post_ffn_norm)
    v = dict(pre_mix_norm=v_pre_mix_norm, w_in=v_w_in, v_norm_g=v_v_norm_g, v_norm_b=v_v_norm_b,
             w_spatial=v_w_spatial, b_spatial=v_b_spatial, out_norm_a=v_out_norm_a, out_norm_b=v_out_norm_b,
             w_out=v_w_out, post_mix_norm=v_post_mix_norm, pre_ffn_norm=v_pre_ffn_norm, w_up=v_w_up,
             conv_w=v_conv_w, conv_b=v_conv_b, w_down=v_w_down, post_ffn_norm=v_post_ffn_norm)
    pos = jnp.stack([lax.axis_index("x"), lax.axis_index("y"), lax.axis_index("c")]).astype(jnp.int32)
    chip = 2 * lax.axis_index("x") + lax.axis_index("y")

    cw_cols = conv_w.shape[-1]
    cw_slab = lax.dynamic_update_slice(jnp.zeros((DEPTH, 3, 2 * D_FF), F32), conv_w, (0, 0, chip * cw_cols))
    conv_w_full = _allreduce_small(cw_slab.reshape(-1, LANES), "gather_conv_w").reshape(DEPTH, 3, 2 * D_FF)
    conv_w_full = conv_w_full * 0.5
    blocks = [{n: w[n][l].astype(BF16) for n in BIG} for l in range(DEPTH)]
    wg = dict(w_in=_gather_weights([blocks[0]["w_in"]], "gather_w_in_l0")[0])

    small = {n: w[n] for n in SMALL}
    xs, target = x[0], loss_target[0]
    tabs = _rope_tables(xs.shape[0])
    params = [_layer_params(l, small, conv_w_full) for l in range(DEPTH)]
    saved, wgs = [], []
    xin = xs
    h = _rms_cast(xin, params[0]["pre_mix_norm"], "pre_mix_l0")
    for l in range(DEPTH):
        sv, gathered, wg = _layer_forward(l, xin, h, params[l], wg, tabs,
                                          [blocks[l + 1][n] for n in BIG] if l + 1 < DEPTH else None,
                                          blocks[0] if l == 0 else None,
                                          params[l + 1]["pre_mix_norm"] if l + 1 < DEPTH else None)
        saved.append(sv)
        wgs.append(wg)
        if l + 1 < DEPTH:
            wg = dict(zip(BIG, gathered))
            xin, h = sv["x2"], sv["h_next"]
    loss_part, dx, df, g_post = _loss_norm_bwd(saved[-1]["x1"], saved[-1]["f"], params[-1]["post_ffn_norm"], target,
                                               "loss")
    smalls, shards = [None] * DEPTH, [{} for _ in range(DEPTH)]
    pending = None
    for l in reversed(range(DEPTH)):
        dx1, dh1, big, smalls[l], scattered, done = _layer_backward(l, dx, df, saved[l], params[l], wgs[l], tabs, pos,
                                                                    pending[1:] if pending else None, hide=l == 0)
        smalls[l]["post_ffn_norm"] = g_post
        if l > 0:
            dx, smalls[l]["pre_mix_norm"], df, g_post = _norm_bwd_in_out(
                dx1, dh1, saved[l]["x0"], params[l]["pre_mix_norm"], saved[l - 1]["f"], params[l - 1]["post_ffn_norm"],
                f"norm_bwd_in_out_l{l}")
        else:
            dx, smalls[l]["pre_mix_norm"] = _norm_bwd_in(dx1, dh1, saved[l]["x0"], params[l]["pre_mix_norm"],
                                                         "norm_bwd_in_l0")
        if pending:
            shards[pending[0]].update(_gradient_shards(pending[0], pending[1], scattered, pos, pending[2]))
        for names, (sums, received) in done.items():
            shards[l].update(_gradient_shards(l, sums, received, pos, names))
        names = tuple(big)
        pending = (l, _chip_sums(l, big, pos, names), names)
    shards[pending[0]].update(_gradient_shards(
        pending[0], pending[1], _chip_scatter(pending[1], pending[2], f"chip_scatter_l{pending[0]}"), pos, pending[2]))

    small_shapes = [w[n].shape for n in SMALL]
    stacked = [jnp.stack([smalls[l][n].reshape(w[n].shape[1:]) for l in range(DEPTH)]) for n in SMALL]
    cw_grad = jnp.stack([smalls[l]["conv_w"] for l in range(DEPTH)])
    packed = _pack(stacked + [cw_grad, loss_part])
    total = _allreduce_small(packed, "allreduce_small")
    parts = _unpack(total, small_shapes + [cw_grad.shape, (8, LANES)])
    g_small = dict(zip(SMALL, parts[:len(SMALL)]))
    loss = parts[-1][0, 0]
    g_conv_w = lax.dynamic_slice(parts[-2], (0, 0, chip * cw_cols), conv_w.shape)

    grads = {n: jnp.stack([shards[l][n] for l in range(DEPTH)]) for n in BIG}
    grads.update(g_small)
    grads["conv_w"] = g_conv_w

    dp, mp, vp = _adamw(_pack([w[n] for n in SMALL]), _pack([g_small[n] for n in SMALL]),
                        _pack([m[n] for n in SMALL]), _pack([v[n] for n in SMALL]), "adamw_small")
    delta = dict(zip(SMALL, _unpack(dp, small_shapes)))
    new_m = dict(zip(SMALL, _unpack(mp, small_shapes)))
    new_v = dict(zip(SMALL, _unpack(vp, small_shapes)))
    for n in BIG + ("conv_w",):
        delta[n], new_m[n], new_v[n] = _adamw_nd(w[n], grads[n], m[n], v[n], "adamw_" + n)

    return (loss, dx[None], *[grads[n] for n in WEIGHTS], *[delta[n] for n in WEIGHTS],
            *[new_m[n] for n in WEIGHTS], *[new_v[n] for n in WEIGHTS])
```

```python
import functools
import math

import jax
import jax.numpy as jnp
import numpy as np
from jax import lax
from jax.experimental import pallas as pl
from jax.experimental.pallas import tpu as pltpu

F32 = jnp.float32
BF16 = jnp.bfloat16
MESH = pl.DeviceIdType.MESH

D_MODEL = 1024
A_WIDTH = 512
A_GROUPS = 4
GROUP_DIM = 128
CHUNK = 128
B_WIDTH = 512
HEAD_DIM = 64
ROT_DIM = 16
ROPE_THETA = 500000.0
DILATIONS = (1, 4, 16)
BAND = 128
IN_COLS = 2560
D_FF = 4096
EPS = 1e-6
NEG_INF = -1e30
N_CHIPS = 4
LANES = 128

ADAM_LR = 0.001
ADAM_B1 = 0.9
ADAM_B2 = 0.999
ADAM_EPS = 1e-08
ADAM_WD = 0.01
ADAM_STEP = 10

VMEM_LIMIT_BYTES = 56 * 1024 * 1024
RSQRT2 = 0.7071067811865476
INV_SQRT_2PI = 0.3989422804014327
GELU_C = 0.7978845608028654
GELU_A = 0.044715

ANY = pl.BlockSpec(memory_space=pl.ANY)
NN = ((1,), (0,))
NT = ((1,), (1,))
TN = ((0,), (0,))


def _cparams(*sem):
    return pltpu.CompilerParams(dimension_semantics=sem, vmem_limit_bytes=VMEM_LIMIT_BYTES)


def _dot(a, b, dims):
    return lax.dot_general(a, b, (dims, ((), ())), preferred_element_type=F32)


def _rsq_mean(a):
    return lax.rsqrt(jnp.mean(a * a, axis=-1, keepdims=True) + EPS)


def _rms_bwd(a, r, g, dz):
    t = dz * g
    da = r * t - a * (r * r * r) * jnp.mean(t * a, axis=-1, keepdims=True)
    return da, dz * a * r


def _colsum(a):
    return jnp.sum(a, axis=0, keepdims=True)


def _gelu_tanh(x):
    u = x * x
    t = jnp.tanh(x * (GELU_C + (GELU_C * GELU_A) * u))
    hx = 0.5 * x
    act = hx + hx * t
    grad = 0.5 + 0.5 * t + (hx - hx * t * t) * (GELU_C + (3.0 * GELU_C * GELU_A) * u)
    return act, grad


def _grid_edges(grid):
    ids = [pl.program_id(ax) for ax in range(len(grid))]
    first = functools.reduce(jnp.logical_and, [i == 0 for i in ids])
    last = functools.reduce(jnp.logical_and, [i == n - 1 for i, n in zip(ids, grid)])
    return first, last


def _matmul(a, b, *, grid, a_spec, b_spec, o_spec, o_shape, o_dtype, dims, nk, kaxis, acc_shape, name, b_2d=None,
            halves=False, scatter=None, gather=None):
    assert scatter is None or gather is None
    ns = len(scatter[0]) if scatter else len(gather) if gather else 0

    def body(*refs):
        a_ref, b_ref = refs[:2]
        o_ref = refs[2 + ns]
        scratch = refs[3 + 2 * ns:]
        if ns:
            first, last = _grid_edges(grid)
            if scatter:
                start, finish = _scatter_steps(refs[2:2 + ns], refs[3 + ns:3 + 2 * ns], scratch[-2], scratch[-1],
                                               scatter[1])
            else:
                start, relay, last_wait = _gather_steps(refs[2:2 + ns], refs[3 + ns:3 + 2 * ns], scratch[-2],
                                                        scratch[-1])

                def finish():
                    relay()
                    last_wait()
            pl.when(first)(start)
        def store(val):
            if halves:
                half = val.shape[1] // 2
                o_ref[0] = val[:, :half].astype(o_dtype)
                o_ref[1] = val[:, half:].astype(o_dtype)
            else:
                o_ref[...] = val.astype(o_dtype)

        bv = b_ref[...] if b_2d is None else b_ref[...].reshape(b_2d)
        part = _dot(a_ref[...], bv, dims)
        if nk == 1:
            store(part)
        else:
            acc = scratch[0]
            k = pl.program_id(kaxis)

            @pl.when(k == 0)
            def _():
                acc[...] = part

            @pl.when(k > 0)
            def _():
                acc[...] += part

            @pl.when(k == nk - 1)
            def _():
                store(acc[...])

        if ns:
            pl.when(last)(finish)

    sem = tuple("arbitrary" if (ns or (nk > 1 and ax == kaxis)) else "parallel" for ax in range(len(grid)))
    riding = list(scatter[0]) if scatter else list(gather or [])
    rider_shapes = _scattered_shapes(scatter[1]) if scatter else _gathered_shapes(riding)
    rider_sems = _scatter_sems(ns) if scatter else _gather_sems(ns) if gather else []
    res = pl.pallas_call(
        body, grid=grid, in_specs=[a_spec, b_spec] + [ANY] * ns, out_specs=[o_spec] + [ANY] * ns,
        out_shape=[jax.ShapeDtypeStruct(o_shape, o_dtype)] + rider_shapes,
        scratch_shapes=([pltpu.VMEM(acc_shape, F32)] if nk > 1 else []) + rider_sems,
        compiler_params=_cparams(*sem), name=name)(a, b, *riding)
    return (res[0], list(res[1:])) if ns else res[0]


def _mix_out_norm(mixed, w_out, x0, g_post, g_next, name):
    s, d = x0.shape
    tm = 512

    def body(a_ref, w_ref, x_ref, gp_ref, gn_ref, y_ref, x1_ref, h_ref):
        y = _dot(a_ref[...], w_ref[...].reshape(d, d), NN)
        y_ref[...] = y
        x1 = x_ref[...] + y * _rsq_mean(y) * gp_ref[...]
        x1_ref[...] = x1
        h_ref[...] = (x1 * _rsq_mean(x1) * gn_ref[...]).astype(BF16)

    row = pl.BlockSpec((tm, d), lambda i: (i, 0))
    vec = pl.BlockSpec((1, d), lambda i: (0, 0))
    return pl.pallas_call(
        body, grid=(s // tm,),
        in_specs=[row, pl.BlockSpec((N_CHIPS, None, d // N_CHIPS, d), lambda i: (0, 0, 0, 0)), row, vec, vec],
        out_specs=[row, row, row],
        out_shape=[jax.ShapeDtypeStruct((s, d), F32), jax.ShapeDtypeStruct((s, d), F32),
                   jax.ShapeDtypeStruct((s, d), BF16)],
        compiler_params=_cparams("parallel"), name=name)(mixed, w_out, x0, g_post, g_next)


def _proj_bwd(dproj, w_in, name):
    s = dproj.shape[0]
    wcol = IN_COLS // N_CHIPS

    def body(a_ref, w_ref, o_ref):
        acc = _dot(a_ref[:, :wcol], w_ref[0], NT)
        for j in range(1, N_CHIPS):
            acc = acc + _dot(a_ref[:, j * wcol:(j + 1) * wcol], w_ref[j], NT)
        o_ref[...] = acc

    return pl.pallas_call(
        body, grid=(s // TMM,),
        in_specs=[pl.BlockSpec((TMM, IN_COLS), lambda i: (i, 0)),
                  pl.BlockSpec((N_CHIPS, None, D_MODEL, wcol), lambda i: (0, 0, 0, 0))],
        out_specs=pl.BlockSpec((TMM, D_MODEL), lambda i: (i, 0)),
        out_shape=jax.ShapeDtypeStruct((s, D_MODEL), F32), compiler_params=_cparams("parallel"), name=name)(dproj, w_in)


TM = 512
TMM = 1024


TR = 256


def _row_spec(width, col=0):
    return pl.BlockSpec((TR, width), lambda i, col=col: (i, col))


def _vec_spec(width):
    return pl.BlockSpec((1, width), lambda i: (0, 0))


def _rms_cast(x, g, name):
    s, d = x.shape

    def body(x_ref, g_ref, h_ref):
        a = x_ref[...]
        h_ref[...] = (a * _rsq_mean(a) * g_ref[...]).astype(BF16)

    return pl.pallas_call(
        body, grid=(s // TR,), in_specs=[_row_spec(d), _vec_spec(d)], out_specs=_row_spec(d),
        out_shape=jax.ShapeDtypeStruct((s, d), BF16), compiler_params=_cparams("parallel"), name=name)(x, g)


def _acc_init(refs):
    @pl.when(pl.program_id(0) == 0)
    def _():
        for r in refs:
            r[...] = jnp.zeros_like(r)


def _loss_norm_bwd(x1, f, g_post, target, name):
    s, d = x1.shape

    def body(x_ref, f_ref, gp_ref, t_ref, loss_ref, dx_ref, df_ref, dg_ref):
        _acc_init([loss_ref, dg_ref])
        fv = f_ref[...]
        r = _rsq_mean(fv)
        err = x_ref[...] + fv * r * gp_ref[...] - t_ref[...]
        dx = err * (1.0 / d)
        dx_ref[...] = dx
        part = 0.5 * jnp.sum(jnp.mean(err * err, axis=-1, keepdims=True), axis=0, keepdims=True)
        loss_ref[...] += jnp.broadcast_to(part, loss_ref.shape)
        da, dgt = _rms_bwd(fv, r, gp_ref[...], dx)
        df_ref[...] = da.astype(BF16)
        dg_ref[...] += _colsum(dgt)

    return pl.pallas_call(
        body, grid=(s // TR,), in_specs=[_row_spec(d), _row_spec(d), _vec_spec(d), _row_spec(d)],
        out_specs=[pl.BlockSpec((8, LANES), lambda i: (0, 0)), _row_spec(d), _row_spec(d), _vec_spec(d)],
        out_shape=[jax.ShapeDtypeStruct((8, LANES), F32), jax.ShapeDtypeStruct((s, d), F32),
                   jax.ShapeDtypeStruct((s, d), BF16), jax.ShapeDtypeStruct((1, d), F32)],
        compiler_params=_cparams("arbitrary"), name=name)(x1, f, g_post, target)


def _norm_bwd_mid(dx2, dh2, x1, g_pf, y1, g_pm, name):
    s, d = dx2.shape

    def body(dx2_ref, dh_ref, x1_ref, gpf_ref, y1_ref, gpm_ref, dx1_ref, dy1_ref, dgpf_ref, dgpm_ref):
        _acc_init([dgpf_ref, dgpm_ref])
        x1 = x1_ref[...]
        da, dgt = _rms_bwd(x1, _rsq_mean(x1), gpf_ref[...], dh_ref[...])
        dx1 = dx2_ref[...] + da
        dx1_ref[...] = dx1
        dgpf_ref[...] += _colsum(dgt)
        y1 = y1_ref[...]
        dy, dgt2 = _rms_bwd(y1, _rsq_mean(y1), gpm_ref[...], dx1)
        dy1_ref[...] = dy.astype(BF16)
        dgpm_ref[...] += _colsum(dgt2)

    return pl.pallas_call(
        body, grid=(s // TR,),
        in_specs=[_row_spec(d), _row_spec(d), _row_spec(d), _vec_spec(d), _row_spec(d), _vec_spec(d)],
        out_specs=[_row_spec(d), _row_spec(d), _vec_spec(d), _vec_spec(d)],
        out_shape=[jax.ShapeDtypeStruct((s, d), F32), jax.ShapeDtypeStruct((s, d), BF16),
                   jax.ShapeDtypeStruct((1, d), F32), jax.ShapeDtypeStruct((1, d), F32)],
        compiler_params=_cparams("arbitrary"), name=name)(dx2, dh2, x1, g_pf, y1, g_pm)


def _norm_bwd_in_out(dx1, dh1, x0, g1, f_below, g_post_below, name):
    s, d = dx1.shape

    def body(dx1_ref, dh_ref, x0_ref, g_ref, f_ref, gp_ref, dx0_ref, dg_ref, df_ref, dgp_ref):
        _acc_init([dg_ref, dgp_ref])
        x0 = x0_ref[...]
        da, dgt = _rms_bwd(x0, _rsq_mean(x0), g_ref[...], dh_ref[...])
        dx0 = dx1_ref[...] + da
        dx0_ref[...] = dx0
        dg_ref[...] += _colsum(dgt)
        fv = f_ref[...]
        db, dgt2 = _rms_bwd(fv, _rsq_mean(fv), gp_ref[...], dx0)
        df_ref[...] = db.astype(BF16)
        dgp_ref[...] += _colsum(dgt2)

    return pl.pallas_call(
        body, grid=(s // TR,),
        in_specs=[_row_spec(d), _row_spec(d), _row_spec(d), _vec_spec(d), _row_spec(d), _vec_spec(d)],
        out_specs=[_row_spec(d), _vec_spec(d), _row_spec(d), _vec_spec(d)],
        out_shape=[jax.ShapeDtypeStruct((s, d), F32), jax.ShapeDtypeStruct((1, d), F32),
                   jax.ShapeDtypeStruct((s, d), BF16), jax.ShapeDtypeStruct((1, d), F32)],
        compiler_params=_cparams("arbitrary"), name=name)(dx1, dh1, x0, g1, f_below, g_post_below)


def _norm_bwd_in(dx1, dh1, x0, g1, name):
    s, d = dx1.shape

    def body(dx1_ref, dh_ref, x0_ref, g_ref, dx0_ref, dg_ref):
        _acc_init([dg_ref])
        x0 = x0_ref[...]
        da, dgt = _rms_bwd(x0, _rsq_mean(x0), g_ref[...], dh_ref[...])
        dx0_ref[...] = dx1_ref[...] + da
        dg_ref[...] += _colsum(dgt)

    return pl.pallas_call(
        body, grid=(s // TR,), in_specs=[_row_spec(d), _row_spec(d), _row_spec(d), _vec_spec(d)],
        out_specs=[_row_spec(d), _vec_spec(d)],
        out_shape=[jax.ShapeDtypeStruct((s, d), F32), jax.ShapeDtypeStruct((1, d), F32)],
        compiler_params=_cparams("arbitrary"), name=name)(dx1, dh1, x0, g1)


def _tril_mask():
    row = lax.broadcasted_iota(jnp.int32, (CHUNK, CHUNK), 0)
    col = lax.broadcasted_iota(jnp.int32, (CHUNK, CHUNK), 1)
    return row >= col


def _gating_forward(pa, gv, bv, wt, bsf):
    er = lax.erf(pa * RSQRT2)
    za = 0.5 * pa * (1.0 + er)
    u = za[:, :A_WIDTH]
    va = za[:, A_WIDTH:]
    xc = va - jnp.mean(va, axis=-1, keepdims=True)
    rs = lax.rsqrt(jnp.mean(xc * xc, axis=-1, keepdims=True) + EPS)
    vn = xc * rs
    vlb = (vn * gv + bv).astype(BF16)
    sg = jnp.concatenate(
        [_dot(wt[g], vlb[:, g * GROUP_DIM:(g + 1) * GROUP_DIM], NN) for g in range(A_GROUPS)], axis=1) + bsf
    return er, u, rs, vn, vlb, sg


def _masked_ws(ws_ref):
    mask = _tril_mask()
    return [jnp.where(mask, ws_ref[g], 0.0).astype(BF16) for g in range(A_GROUPS)]


def _mixer_a_fwd(proj, gv, bv, ws, bsf, ga, name):
    s = proj.shape[0]

    def body(p_ref, gv_ref, bv_ref, ws_ref, bs_ref, ga_ref, o_ref):
        wt = _masked_ws(ws_ref)
        for ch in range(TR // CHUNK):
            rows = slice(ch * CHUNK, (ch + 1) * CHUNK)
            _, u, _, _, _, sg = _gating_forward(p_ref[rows, :].astype(F32), gv_ref[...], bv_ref[...], wt, bs_ref[...])
            oa = u * sg
            o_ref[rows, :] = (oa * _rsq_mean(oa) * ga_ref[...]).astype(BF16)

    return pl.pallas_call(
        body, grid=(s // TR,),
        in_specs=[_row_spec(2 * A_WIDTH), _vec_spec(A_WIDTH), _vec_spec(A_WIDTH),
                  pl.BlockSpec((A_GROUPS, CHUNK, CHUNK), lambda i: (0, 0, 0)),
                  pl.BlockSpec((CHUNK, A_WIDTH), lambda i: (0, 0)), _vec_spec(A_WIDTH)],
        out_specs=_row_spec(A_WIDTH), out_shape=jax.ShapeDtypeStruct((s, A_WIDTH + B_WIDTH), BF16),
        compiler_params=_cparams("parallel"), name=name)(proj, gv, bv, ws, bsf, ga)


def _mixer_a_bwd(proj, dmixed, gv, bv, ws, bsf, ga, name):
    s = proj.shape[0]
    nsteps = s // TR

    def body(p_ref, dm_ref, gv_ref, bv_ref, ws_ref, bs_ref, ga_ref,
             dp_ref, dga_ref, dgv_ref, dbv_ref, dbs_ref, dws_ref):
        _acc_init([dga_ref, dgv_ref, dbv_ref, dbs_ref, dws_ref])
        mask = _tril_mask()
        wt = _masked_ws(ws_ref)
        gvv = gv_ref[...]
        gav = ga_ref[...]
        for ch in range(TR // CHUNK):
            rows = slice(ch * CHUNK, (ch + 1) * CHUNK)
            pa = p_ref[rows, :].astype(F32)
            er, u, rs, vn, vlb, sg = _gating_forward(pa, gvv, bv_ref[...], wt, bs_ref[...])
            oa = u * sg
            doa, dgt = _rms_bwd(oa, _rsq_mean(oa), gav, dm_ref[rows, :])
            dga_ref[...] += _colsum(dgt)
            du = doa * sg
            dsg = doa * u
            dbs_ref[...] += dsg
            dsgb = dsg.astype(BF16)
            dvl = []
            for g in range(A_GROUPS):
                cols = slice(g * GROUP_DIM, (g + 1) * GROUP_DIM)
                dws_ref[g] += jnp.where(mask, _dot(dsgb[:, cols], vlb[:, cols], NT), 0.0)
                dvl.append(_dot(wt[g], dsgb[:, cols], TN))
            dvl = jnp.concatenate(dvl, axis=1)
            dgv_ref[...] += _colsum(dvl * vn)
            dbv_ref[...] += _colsum(dvl)
            dvn = dvl * gvv
            dva = rs * (dvn - jnp.mean(dvn, axis=-1, keepdims=True)
                        - vn * jnp.mean(dvn * vn, axis=-1, keepdims=True))
            gp = 0.5 * (1.0 + er) + pa * jnp.exp(-0.5 * pa * pa) * INV_SQRT_2PI
            dp_ref[rows, :] = (jnp.concatenate([du, dva], axis=1) * gp).astype(BF16)

        @pl.when(pl.program_id(0) == nsteps - 1)
        def _():
            for g in range(A_GROUPS):
                cols = slice(g * GROUP_DIM, (g + 1) * GROUP_DIM)
                tot = jnp.sum(dbs_ref[:, cols], axis=1, keepdims=True)
                dbs_ref[:, cols] = jnp.broadcast_to(tot, (CHUNK, GROUP_DIM))

    full = lambda *shape: pl.BlockSpec(shape, lambda i: (0,) * len(shape))
    return pl.pallas_call(
        body, grid=(nsteps,),
        in_specs=[_row_spec(2 * A_WIDTH), _row_spec(A_WIDTH), _vec_spec(A_WIDTH), _vec_spec(A_WIDTH),
                  full(A_GROUPS, CHUNK, CHUNK), full(CHUNK, A_WIDTH), _vec_spec(A_WIDTH)],
        out_specs=[_row_spec(2 * A_WIDTH), _vec_spec(A_WIDTH), _vec_spec(A_WIDTH), _vec_spec(A_WIDTH),
                   full(CHUNK, A_WIDTH), full(A_GROUPS, CHUNK, CHUNK)],
        out_shape=[jax.ShapeDtypeStruct((s, IN_COLS), BF16), jax.ShapeDtypeStruct((1, A_WIDTH), F32),
                   jax.ShapeDtypeStruct((1, A_WIDTH), F32), jax.ShapeDtypeStruct((1, A_WIDTH), F32),
                   jax.ShapeDtypeStruct((CHUNK, A_WIDTH), F32),
                   jax.ShapeDtypeStruct((A_GROUPS, CHUNK, CHUNK), F32)],
        compiler_params=_cparams("arbitrary"), name=name)(proj, dmixed, gv, bv, ws, bsf, ga)


def _rope_tables(s):
    half = ROT_DIM // 2
    lane = jnp.arange(LANES) % HEAD_DIM
    inv = ROPE_THETA ** (-(2 * (lane % half)).astype(F32) / ROT_DIM)
    ang = jnp.arange(s, dtype=F32)[:, None] * inv[None, :]
    cos, sin = jnp.cos(ang), jnp.sin(ang)
    c = jnp.where(lane < ROT_DIM, cos, 1.0)
    s1 = jnp.where(lane < half, -sin, 0.0)
    s2 = jnp.where((lane >= half) & (lane < ROT_DIM), sin, 0.0)
    return c, s1, s2


def _lane_blocks(width):
    return [slice(b * LANES, (b + 1) * LANES) for b in range(width // LANES)]


CLASS_DILS = tuple(d for d in DILATIONS if d > 1)


def _class_shape(s, dil, dtype):
    return jax.ShapeDtypeStruct((dil, s // dil, B_WIDTH), dtype)


def _class_spec(dil):
    return pl.BlockSpec((dil, TR // dil, B_WIDTH), lambda i, *_: (0, i, 0))


NBLK = B_WIDTH // LANES
STAGE = pltpu.VMEM((NBLK, TR, LANES), F32)


def _stage_put(stage, value):
    for b, sl in enumerate(_lane_blocks(B_WIDTH)):
        stage[b] = value[:, sl]


def _stage_get(stage):
    return jnp.concatenate([stage[b] for b in range(NBLK)], axis=1)


def _store_classes(stage, dst_ref, dil):
    for b, sl in enumerate(_lane_blocks(B_WIDTH)):
        for r in range(dil):
            dst_ref[r, :, sl] = stage[b, pl.ds(r, TR // dil, stride=dil), :].astype(dst_ref.dtype)


def _load_classes(src_ref, stage, dil):
    for b, sl in enumerate(_lane_blocks(B_WIDTH)):
        for r in range(dil):
            stage[b, pl.ds(r, TR // dil, stride=dil), :] = src_ref[r, :, sl].astype(F32)
    return _stage_get(stage)


def _rope_fwd(proj, tabs, name, gather=None):
    s = proj.shape[0]
    half = ROT_DIM // 2
    scale = HEAD_DIM ** -0.5
    nlay = 1 + len(CLASS_DILS)
    ng = 0 if gather is None else len(gather)

    def body(q_ref, k_ref, v_ref, c_ref, s1_ref, s2_ref, *rest):
        outs, stage = rest[ng:ng + 3 * nlay], rest[2 * ng + 3 * nlay]
        if ng:
            start, relay, finish = _gather_steps(rest[:ng], rest[ng + 3 * nlay:2 * ng + 3 * nlay],
                                                 *rest[2 * ng + 3 * nlay + 1:])
            first, last = _grid_edges((s // TR,))
            pl.when(first)(start)
        c, s1, s2 = c_ref[...], s1_ref[...], s2_ref[...]
        for which, (src, mul) in enumerate(((q_ref, scale), (k_ref, 1.0), (v_ref, None))):
            if mul is None:
                _stage_put(stage, src[...].astype(F32))
            else:
                for b, sl in enumerate(_lane_blocks(B_WIDTH)):
                    a = src[:, sl].astype(F32)
                    r = a * c + pltpu.roll(a, LANES - half, 1) * s1 + pltpu.roll(a, half, 1) * s2
                    stage[b] = r * mul
            dst = outs[which * nlay:(which + 1) * nlay]
            dst[0][...] = _stage_get(stage).astype(BF16)
            for ref, d in zip(dst[1:], CLASS_DILS):
                _store_classes(stage, ref, d)

        if ng:
            @pl.when(last)
            def _():
                relay()
                finish()

    tab = pl.BlockSpec((TR, LANES), lambda i: (i, 0))
    lay_specs = [_row_spec(B_WIDTH)] + [_class_spec(d) for d in CLASS_DILS]
    lay_shapes = [jax.ShapeDtypeStruct((s, B_WIDTH), BF16)] + [_class_shape(s, d, BF16) for d in CLASS_DILS]
    outs = pl.pallas_call(
        body, grid=(s // TR,),
        in_specs=[_row_spec(B_WIDTH, 2), _row_spec(B_WIDTH, 3), _row_spec(B_WIDTH, 4), tab, tab, tab] + [ANY] * ng,
        out_specs=lay_specs * 3 + [ANY] * ng, out_shape=lay_shapes * 3 + _gathered_shapes(gather or []),
        scratch_shapes=[STAGE] + (_gather_sems(ng) if ng else []),
        compiler_params=_cparams("arbitrary" if ng else "parallel"), name=name)(proj, proj, proj, *tabs,
                                                                              *(gather or []))
    q, k, v = (dict(zip(DILATIONS, outs[w * nlay:(w + 1) * nlay])) for w in range(3))
    return q, k, v, list(outs[3 * nlay:])


def _as_classes(t):
    return t if t.ndim == 3 else t[None]


def _band_mask(i):
    qi = lax.broadcasted_iota(jnp.int32, (BAND, 2 * BAND), 0)
    kj = lax.broadcasted_iota(jnp.int32, (BAND, 2 * BAND), 1)
    return (kj >= qi) & (kj <= qi + BAND) & ((kj >= BAND) | (i > 0))


def _head_masks():
    lane = lax.broadcasted_iota(jnp.int32, (1, LANES), 1)
    return lane < HEAD_DIM, lane >= HEAD_DIM


def _stack_heads(t):
    lo, hi = _head_masks()
    zero = jnp.zeros_like(t)
    return jnp.concatenate([jnp.where(lo, t, zero), jnp.where(hi, t, zero)], axis=0)


def _attn_specs(last):
    cur = pl.BlockSpec((None, BAND, B_WIDTH), lambda r, i: (r, jnp.minimum(i, last), 0))
    prev = pl.BlockSpec((None, BAND, B_WIDTH), lambda r, i: (r, jnp.maximum(jnp.minimum(i, last) - 1, 0), 0))
    return cur, prev


def _attn_fwd(q, k, v, name, gather=None):
    dil, n, _ = q.shape
    nb = n // BAND
    ng = 0 if gather is None else len(gather)

    def body(*refs):
        q_ref, kc_ref, kp_ref, vc_ref, vp_ref = refs[:5]
        o_ref, l_ref = refs[5 + ng:7 + ng]
        if ng:
            start, relay, finish = _gather_steps(refs[5:5 + ng], refs[7 + ng:7 + 2 * ng], *refs[7 + 2 * ng:])
            first, last = _grid_edges((dil, nb))
            pl.when(first)(start)
        valid = _band_mask(pl.program_id(1))
        valid = jnp.concatenate([valid, valid], axis=0)
        lo, _ = _head_masks()
        for sl in _lane_blocks(B_WIDTH):
            kk = jnp.concatenate([kp_ref[:, sl], kc_ref[:, sl]], axis=0)
            vv = jnp.concatenate([vp_ref[:, sl], vc_ref[:, sl]], axis=0)
            sc = jnp.where(valid, _dot(_stack_heads(q_ref[:, sl]), kk, NT), NEG_INF)
            mx = jnp.max(sc, axis=1, keepdims=True)
            p = jnp.exp(sc - mx)
            den = jnp.sum(p, axis=1, keepdims=True)
            out = _dot(p.astype(BF16), vv, NN) / den
            lse = mx + jnp.log(den)
            o_ref[:, sl] = jnp.where(lo, out[:BAND], out[BAND:]).astype(BF16)
            l_ref[:, sl] = jnp.where(lo, lse[:BAND], lse[BAND:])

        if ng:
            @pl.when(last)
            def _():
                relay()
                finish()

    cur, prev = _attn_specs(nb - 1)
    sem = ("arbitrary", "arbitrary") if ng else ("parallel", "parallel")
    res = pl.pallas_call(
        body, grid=(dil, nb), in_specs=[cur, cur, prev, cur, prev] + [ANY] * ng, out_specs=[cur, cur] + [ANY] * ng,
        out_shape=[jax.ShapeDtypeStruct((dil, n, B_WIDTH), BF16), jax.ShapeDtypeStruct((dil, n, B_WIDTH), F32)]
        + _gathered_shapes(gather or []),
        scratch_shapes=_gather_sems(ng) if ng else [],
        compiler_params=_cparams(*sem), name=name)(q, k, k, v, v, *(gather or []))
    return res[0], res[1], list(res[2:])


MAX_CLASS_BLOCKS = 8


def _class_masks():
    qi = lax.broadcasted_iota(jnp.int32, (BAND, 2 * BAND), 0)
    kj = lax.broadcasted_iota(jnp.int32, (BAND, 2 * BAND), 1)
    both = (kj >= qi) & (kj <= qi + BAND)
    own = kj[:, :BAND] <= qi[:, :BAND]
    return jnp.concatenate([own, own], axis=0), jnp.concatenate([both, both], axis=0)


def _block_rows(g):
    return pl.ds(pl.multiple_of(g * BAND, BAND), BAND)


def _key_rows(g):
    return pl.ds(pl.multiple_of((g - 1) * BAND, BAND), 2 * BAND)


def _attn_fwd_class(q, k, v, name, gather=None):
    dil, n, _ = q.shape
    nb = n // BAND
    ng = 0 if gather is None else len(gather)

    def body(*refs):
        q_ref, k_ref, v_ref = refs[:3]
        o_ref, l_ref = refs[3 + ng:5 + ng]
        if ng:
            start, relay, finish = _gather_steps(refs[3:3 + ng], refs[5 + ng:5 + 2 * ng], *refs[5 + 2 * ng:])
            first, last = _grid_edges((dil,))
            pl.when(first)(start)
        own, both = _class_masks()
        lo, _ = _head_masks()

        def block(rows, keys, valid):
            for sl in _lane_blocks(B_WIDTH):
                sc = jnp.where(valid, _dot(_stack_heads(q_ref[rows, sl]), k_ref[keys, sl], NT), NEG_INF)
                mx = jnp.max(sc, axis=1, keepdims=True)
                p = jnp.exp(sc - mx)
                den = jnp.sum(p, axis=1, keepdims=True)
                out = _dot(p.astype(BF16), v_ref[keys, sl], NN) / den
                lse = mx + jnp.log(den)
                o_ref[rows, sl] = jnp.where(lo, out[:BAND], out[BAND:]).astype(BF16)
                l_ref[rows, sl] = jnp.where(lo, lse[:BAND], lse[BAND:])

        block(_block_rows(0), _block_rows(0), own)

        @pl.loop(1, nb)
        def _(g):
            block(_block_rows(g), _key_rows(g), both)

        if ng:
            @pl.when(last)
            def _():
                relay()
                finish()

    spec = pl.BlockSpec((None, n, B_WIDTH), lambda r: (r, 0, 0))
    res = pl.pallas_call(
        body, grid=(dil,), in_specs=[spec] * 3 + [ANY] * ng, out_specs=[spec, spec] + [ANY] * ng,
        out_shape=[jax.ShapeDtypeStruct((dil, n, B_WIDTH), BF16), jax.ShapeDtypeStruct((dil, n, B_WIDTH), F32)]
        + _gathered_shapes(gather or []),
        scratch_shapes=_gather_sems(ng) if ng else [],
        compiler_params=_cparams("arbitrary" if ng else "parallel"), name=name)(q, k, v, *(gather or []))
    return res[0], res[1], list(res[2:])


def _attn_bwd_class(q, k, v, do, lse, delta, name, scatter=None):
    dil, n, _ = q.shape
    nb = n // BAND
    ns = 0 if scatter is None else len(scatter[0])

    def body(*refs):
        q_ref, k_ref, v_ref, do_ref, lse_ref, dl_ref = refs[:6]
        dq_ref, dk_ref, dv_ref = refs[6 + ns:9 + ns]
        ck_ref, cv_ref = refs[9 + 2 * ns:11 + 2 * ns]
        if ns:
            start, finish = _scatter_steps(refs[6:6 + ns], refs[9 + ns:9 + 2 * ns], *refs[11 + 2 * ns:], scatter[1])
            first, last = _grid_edges((dil,))
            pl.when(first)(start)
        own, both = _class_masks()
        lo, _ = _head_masks()
        lane = lax.broadcasted_iota(jnp.int32, (1, LANES), 1)

        def per_head(t):
            return jnp.concatenate(
                [jnp.sum(jnp.where(lane == first, t, 0.0), axis=1, keepdims=True) for first in (0, HEAD_DIM)], axis=0)

        def grads(rows, keys, valid, sl):
            q2 = _stack_heads(q_ref[rows, sl])
            do2 = _stack_heads(do_ref[rows, sl])
            kk = k_ref[keys, sl]
            p = jnp.where(valid, jnp.exp(_dot(q2, kk, NT) - per_head(lse_ref[rows, sl])), 0.0)
            ds = (p * (_dot(do2, v_ref[keys, sl], NT) - per_head(dl_ref[rows, sl]))).astype(BF16)
            dq = _dot(ds, kk, NN)
            dq_ref[rows, sl] = jnp.where(lo, dq[:BAND], dq[BAND:]).astype(BF16)
            return _dot(ds, q2, TN), _dot(p.astype(BF16), do2, TN)

        for sl in _lane_blocks(B_WIDTH):
            ck_ref[:, sl], cv_ref[:, sl] = grads(_block_rows(0), _block_rows(0), own, sl)

        @pl.loop(1, nb)
        def _(g):
            before = _block_rows(g - 1)
            for sl in _lane_blocks(B_WIDTH):
                dkk, dvv = grads(_block_rows(g), _key_rows(g), both, sl)
                dk_ref[before, sl] = (ck_ref[:, sl] + dkk[:BAND]).astype(BF16)
                dv_ref[before, sl] = (cv_ref[:, sl] + dvv[:BAND]).astype(BF16)
                ck_ref[:, sl] = dkk[BAND:]
                cv_ref[:, sl] = dvv[BAND:]

        final = pl.ds((nb - 1) * BAND, BAND)
        dk_ref[final, :] = ck_ref[...].astype(BF16)
        dv_ref[final, :] = cv_ref[...].astype(BF16)

        if ns:
            pl.when(last)(finish)

    spec = pl.BlockSpec((None, n, B_WIDTH), lambda r: (r, 0, 0))
    shape = jax.ShapeDtypeStruct((dil, n, B_WIDTH), BF16)
    res = pl.pallas_call(
        body, grid=(dil,), in_specs=[spec] * 6 + [ANY] * ns, out_specs=[spec] * 3 + [ANY] * ns,
        out_shape=[shape] * 3 + (_scattered_shapes(scatter[1]) if ns else []),
        scratch_shapes=[pltpu.VMEM((BAND, B_WIDTH), F32)] * 2 + (_scatter_sems(ns) if ns else []),
        compiler_params=_cparams("arbitrary" if ns else "parallel"), name=name)(q, k, v, do, lse, delta,
                                                                             *(scatter[0] if ns else []))
    return res[0], res[1], res[2], list(res[3:])


def _attn_combine(outs, lses, gb, mixed, name, gather=None):
    s = mixed.shape[0]
    npat = len(DILATIONS)
    w = B_WIDTH
    ng = 0 if gather is None else len(gather)

    def body(*refs):
        o_refs, l_refs = refs[:npat], refs[npat:2 * npat]
        g_ref = refs[2 * npat]
        ob_ref = refs[2 * npat + 2 + ng]
        lse_refs = refs[2 * npat + 3 + ng:3 * npat + 3 + ng]
        mb_ref = refs[3 * npat + 3 + ng]
        stage = refs[3 * npat + 4 + 2 * ng]
        if ng:
            start, relay, finish = _gather_steps(refs[2 * npat + 2:2 * npat + 2 + ng],
                                                 refs[3 * npat + 4 + ng:3 * npat + 4 + 2 * ng],
                                                 *refs[3 * npat + 5 + 2 * ng:])
            first, last = _grid_edges((s // TR,))
            pl.when(first)(start)
        os_ = [o_refs[0][...].astype(F32)] + [_load_classes(r, stage, d) for r, d in zip(o_refs[1:], CLASS_DILS)]
        ls = [l_refs[0][...]] + [_load_classes(r, stage, d) for r, d in zip(l_refs[1:], CLASS_DILS)]
        mx = functools.reduce(jnp.maximum, ls)
        ws = [jnp.exp(l - mx) for l in ls]
        tot = functools.reduce(lambda a, b: a + b, ws)
        ob = functools.reduce(lambda a, b: a + b, [wt / tot * o for wt, o in zip(ws, os_)])
        ob_ref[...] = ob
        lse = mx + jnp.log(tot)
        _stage_put(stage, lse)
        lse_refs[0][...] = lse
        for ref, d in zip(lse_refs[1:], CLASS_DILS):
            _store_classes(stage, ref, d)
        mb_ref[...] = (ob * _rsq_mean(ob) * g_ref[...]).astype(BF16)

        if ng:
            @pl.when(last)
            def _():
                relay()
                finish()

    lay_specs = [_row_spec(w)] + [_class_spec(d) for d in CLASS_DILS]
    res = pl.pallas_call(
        body, grid=(s // TR,), in_specs=lay_specs * 2 + [_vec_spec(w), ANY] + [ANY] * ng,
        out_specs=[_row_spec(w)] + lay_specs + [_row_spec(w, 1)] + [ANY] * ng,
        out_shape=[jax.ShapeDtypeStruct((s, w), F32), jax.ShapeDtypeStruct((s, w), F32)]
        + [_class_shape(s, d, F32) for d in CLASS_DILS] + [jax.ShapeDtypeStruct(mixed.shape, mixed.dtype)]
        + _gathered_shapes(gather or []),
        scratch_shapes=[STAGE] + (_gather_sems(ng) if ng else []), input_output_aliases={2 * npat + 1: npat + 1},
        compiler_params=_cparams("arbitrary" if ng else "parallel"), name=name)(*outs, *lses, gb, mixed,
                                                                              *(gather or []))
    return res[0], dict(zip(DILATIONS, res[1:npat + 1])), res[npat + 1], list(res[npat + 2:])


def _attn_bwd_prep(dmixed, ob, gb, name):
    s = ob.shape[0]
    w = B_WIDTH
    nlay = len(DILATIONS)

    def body(dm_ref, ob_ref, g_ref, *rest):
        do_refs, dl_refs = rest[:nlay], rest[nlay:2 * nlay]
        dg_ref, stage = rest[2 * nlay:]
        _acc_init([dg_ref])
        ob = ob_ref[...]
        dob, dgt = _rms_bwd(ob, _rsq_mean(ob), g_ref[...], dm_ref[...])
        dg_ref[...] += _colsum(dgt)
        _stage_put(stage, dob)
        do_refs[0][...] = dob.astype(BF16)
        for ref, d in zip(do_refs[1:], CLASS_DILS):
            _store_classes(stage, ref, d)
        lo, hi = _head_masks()
        t = dob * ob
        for b, sl in enumerate(_lane_blocks(w)):
            tb = t[:, sl]
            s0 = jnp.sum(jnp.where(lo, tb, 0.0), axis=1, keepdims=True)
            s1 = jnp.sum(jnp.where(hi, tb, 0.0), axis=1, keepdims=True)
            stage[b] = jnp.where(lo, s0, s1)
        dl_refs[0][...] = _stage_get(stage)
        for ref, d in zip(dl_refs[1:], CLASS_DILS):
            _store_classes(stage, ref, d)

    lay_specs = [_row_spec(w)] + [_class_spec(d) for d in CLASS_DILS]
    shapes = lambda dt: [jax.ShapeDtypeStruct((s, w), dt)] + [_class_shape(s, d, dt) for d in CLASS_DILS]
    res = pl.pallas_call(
        body, grid=(s // TR,), in_specs=[_row_spec(w, 1), _row_spec(w), _vec_spec(w)],
        out_specs=lay_specs * 2 + [_vec_spec(w)],
        out_shape=shapes(BF16) + shapes(F32) + [jax.ShapeDtypeStruct((1, w), F32)],
        scratch_shapes=[STAGE],
        compiler_params=_cparams("arbitrary"), name=name)(dmixed, ob, gb)
    return dict(zip(DILATIONS, res[:nlay])), dict(zip(DILATIONS, res[nlay:2 * nlay])), res[2 * nlay]


def _attn_bwd(q, k, v, do, lse, delta, name, scatter=None):
    dil, n, _ = q.shape
    nb = n // BAND
    ns = 0 if scatter is None else len(scatter[0])

    def body(*refs):
        q_ref, kc_ref, kp_ref, vc_ref, vp_ref, do_ref, lse_ref, dl_ref = refs[:8]
        dq_ref, dk_ref, dv_ref = refs[8 + ns:11 + ns]
        ck_ref, cv_ref = refs[11 + 2 * ns:13 + 2 * ns]
        i = pl.program_id(1)
        if ns:
            start, finish = _scatter_steps(refs[8:8 + ns], refs[11 + ns:11 + 2 * ns], *refs[13 + 2 * ns:],
                                           scatter[1])
            first, last = _grid_edges((dil, nb + 1))
            pl.when(first)(start)

        @pl.when(i == 0)
        def _():
            ck_ref[...] = jnp.zeros_like(ck_ref)
            cv_ref[...] = jnp.zeros_like(cv_ref)

        @pl.when(i < nb)
        def _():
            valid = _band_mask(i)
            valid = jnp.concatenate([valid, valid], axis=0)
            lo, _ = _head_masks()
            lane = lax.broadcasted_iota(jnp.int32, (1, LANES), 1)

            def per_head(t):
                return jnp.concatenate(
                    [jnp.sum(jnp.where(lane == first, t, 0.0), axis=1, keepdims=True) for first in (0, HEAD_DIM)], axis=0)

            for sl in _lane_blocks(B_WIDTH):
                q2 = _stack_heads(q_ref[:, sl])
                do2 = _stack_heads(do_ref[:, sl])
                kk = jnp.concatenate([kp_ref[:, sl], kc_ref[:, sl]], axis=0)
                vv = jnp.concatenate([vp_ref[:, sl], vc_ref[:, sl]], axis=0)
                p = jnp.where(valid, jnp.exp(_dot(q2, kk, NT) - per_head(lse_ref[:, sl])), 0.0)
                ds = (p * (_dot(do2, vv, NT) - per_head(dl_ref[:, sl]))).astype(BF16)
                dq = _dot(ds, kk, NN)
                dkk = _dot(ds, q2, TN)
                dvv = _dot(p.astype(BF16), do2, TN)
                dq_ref[:, sl] = jnp.where(lo, dq[:BAND], dq[BAND:]).astype(BF16)
                dk_ref[:, sl] = (ck_ref[:, sl] + dkk[:BAND]).astype(BF16)
                dv_ref[:, sl] = (cv_ref[:, sl] + dvv[:BAND]).astype(BF16)
                ck_ref[:, sl] = dkk[BAND:]
                cv_ref[:, sl] = dvv[BAND:]

        @pl.when(i == nb)
        def _():
            dk_ref[...] = ck_ref[...].astype(BF16)
            dv_ref[...] = cv_ref[...].astype(BF16)

        if ns:
            pl.when(last)(finish)

    cur, prev = _attn_specs(nb - 1)
    lag = pl.BlockSpec((None, BAND, B_WIDTH), lambda r, i: (r, jnp.maximum(i - 1, 0), 0))
    shape = jax.ShapeDtypeStruct((dil, n, B_WIDTH), BF16)
    res = pl.pallas_call(
        body, grid=(dil, nb + 1), in_specs=[cur, cur, prev, cur, prev, cur, cur, cur] + [ANY] * ns,
        out_specs=[cur, lag, lag] + [ANY] * ns,
        out_shape=[shape] * 3 + (_scattered_shapes(scatter[1]) if ns else []),
        scratch_shapes=[pltpu.VMEM((BAND, B_WIDTH), F32)] * 2 + (_scatter_sems(ns) if ns else []),
        compiler_params=_cparams("arbitrary", "arbitrary"), name=name)(q, k, k, v, v, do, lse, delta,
                                                                      *(scatter[0] if ns else []))
    return res[0], res[1], res[2], list(res[3:])


def _rope_bwd(dqs, dks, dvs, tabs, dproj, name):
    s = dproj.shape[0]
    half = ROT_DIM // 2
    scale = HEAD_DIM ** -0.5
    npat = len(DILATIONS)
    w = B_WIDTH

    def body(*refs):
        groups = [refs[g * npat:(g + 1) * npat] for g in range(3)]
        c_ref, s1_ref, s2_ref, _, o_ref, stage = refs[3 * npat:]

        def total(rs):
            acc = rs[0][...].astype(F32)
            for ref, d in zip(rs[1:], CLASS_DILS):
                acc = acc + _load_classes(ref, stage, d)
            return acc

        def unrope(g):
            c, s1, s2 = c_ref[...], s1_ref[...], s2_ref[...]
            for sl in _lane_blocks(w):
                gb = g[:, sl]
                o = gb * c + pltpu.roll(gb * s1, half, 1) + pltpu.roll(gb * s2, LANES - half, 1)
                o_ref[:, sl] = o.astype(BF16)

        which = pl.program_id(1)

        @pl.when(which == 0)
        def _():
            unrope(total(groups[0]) * scale)

        @pl.when(which == 1)
        def _():
            unrope(total(groups[1]))

        @pl.when(which == 2)
        def _():
            o_ref[...] = total(groups[2]).astype(BF16)

    tab = pl.BlockSpec((TR, LANES), lambda i, j: (i, 0))
    nat = pl.BlockSpec((TR, w), lambda i, j: (i, 0))
    lay_specs = [nat] + [_class_spec(d) for d in CLASS_DILS]
    first_col = 2 * A_WIDTH // w
    return pl.pallas_call(
        body, grid=(s // TR, 3), in_specs=lay_specs * 3 + [tab] * 3 + [ANY],
        out_specs=pl.BlockSpec((TR, w), lambda i, j: (i, first_col + j)),
        out_shape=jax.ShapeDtypeStruct(dproj.shape, dproj.dtype), scratch_shapes=[STAGE],
        input_output_aliases={3 * npat + 3: 0},
        compiler_params=_cparams("parallel", "arbitrary"), name=name)(*dqs, *dks, *dvs, *tabs, dproj)


TK = 512
HALO = 16


def _row_of(v, r):
    rows = lax.broadcasted_iota(jnp.int32, (v.shape[0], 1), 0)
    return jnp.sum(jnp.where(rows == r, v, 0.0), axis=0, keepdims=True)


def _taps_before(x, halo):
    row = lax.broadcasted_iota(jnp.int32, (x.shape[0], 1), 0)
    m1 = jnp.where(row == 0, _row_of(halo, HALO - 1), pltpu.roll(x, 1, 0))
    m2 = jnp.where(row == 0, _row_of(halo, HALO - 2), jnp.where(row == 1, _row_of(halo, HALO - 1), pltpu.roll(x, 2, 0)))
    return m2, m1, x


def _taps_after(x, halo):
    rows = x.shape[0]
    row = lax.broadcasted_iota(jnp.int32, (rows, 1), 0)
    p1 = jnp.where(row == rows - 1, _row_of(halo, 0), pltpu.roll(x, rows - 1, 0))
    p2 = jnp.where(row == rows - 2, _row_of(halo, 0), jnp.where(row == rows - 1, _row_of(halo, 1), pltpu.roll(x, rows - 2, 0)))
    return p1, p2


def _conv_value(taps, cw_ref, cb_ref, h):
    return cb_ref[h] + cw_ref[h, 0:1, :] * taps[0] + cw_ref[h, 1:2, :] * taps[1] + cw_ref[h, 2:3, :] * taps[2]


def _ffn_weight_specs(ncol):
    per_up = (2 * D_FF // N_CHIPS) // TK
    per_dn = (D_FF // N_CHIPS) // TK
    wg = pl.BlockSpec((None, None, D_MODEL, TK), lambda i, j: (j // per_up, 0, 0, j % per_up))
    wv = pl.BlockSpec((None, None, D_MODEL, TK), lambda i, j: ((j + ncol) // per_up, 0, 0, (j + ncol) % per_up))
    wd = pl.BlockSpec((None, None, TK, D_MODEL), lambda i, j: (j // per_dn, 0, j % per_dn, 0))
    cw = pl.BlockSpec((2, 3, TK), lambda i, j: (0, 0, j))
    cb = pl.BlockSpec((2, 1, TK), lambda i, j: (0, 0, j))
    return wg, wv, wd, cw, cb


def _ffn_forward(h2, w_up, w_down, cw3, cb3, name, gather=None, post=None):
    s = h2.shape[0]
    nm, ncol = s // TM, D_FF // TK
    ng = 0 if gather is None else len(gather)
    npost = 0 if post is None else 3
    nout = 4 + (2 if post else 0)

    def body(*refs):
        h_ref, wg_ref, wv_ref, wd_ref, cw_ref, cb_ref = refs[:6]
        post_in = refs[6:6 + npost]
        g_in = refs[6 + npost:6 + npost + ng]
        outs = refs[6 + npost + ng:6 + npost + ng + nout]
        y_ref, up_ref, cv_ref, f_ref = outs[:4]
        g_out = refs[6 + npost + ng + nout:6 + npost + 2 * ng + nout]
        carry, acc = refs[6 + npost + 2 * ng + nout:8 + npost + 2 * ng + nout]
        i, j = pl.program_id(0), pl.program_id(1)
        if ng:
            start, relay, finish = _gather_steps(g_in, g_out, *refs[8 + npost + 2 * ng + nout:])
            pl.when((i == 0) & (j == 0))(start)
            pl.when((i == nm - 1) & (j == 0))(relay)

        @pl.when((i == 0) & (j == 0))
        def _():
            carry[...] = jnp.zeros_like(carry)

        h = h_ref[...]
        conv = []
        for hh, w_ref in ((0, wg_ref), (1, wv_ref)):
            up = _dot(h, w_ref[...], NN).astype(BF16)
            up_ref[hh] = up
            x = up.astype(F32)
            conv.append(_conv_value(_taps_before(x, carry[j, hh]), cw_ref, cb_ref, hh))
            cv_ref[hh] = conv[hh].astype(BF16)
            carry[j, hh] = x[TM - HALO:, :]
        y = (_gelu_tanh(conv[0])[0] * conv[1]).astype(BF16)
        y_ref[...] = y
        part = _dot(y, wd_ref[...], NN)

        @pl.when(j == 0)
        def _():
            acc[...] = part

        @pl.when(j > 0)
        def _():
            acc[...] += part

        @pl.when(j == ncol - 1)
        def _():
            f = acc[...]
            f_ref[...] = f
            if post:
                x1_ref, gp_ref, gn_ref = post_in
                x2 = x1_ref[...] + f * _rsq_mean(f) * gp_ref[...]
                outs[4][...] = x2
                outs[5][...] = (x2 * _rsq_mean(x2) * gn_ref[...]).astype(BF16)

        if ng:
            pl.when((i == nm - 1) & (j == ncol - 1))(finish)

    wg, wv, wd, cw, cb = _ffn_weight_specs(ncol)
    row = pl.BlockSpec((TM, D_MODEL), lambda i, j: (i, 0))
    vec = pl.BlockSpec((1, D_MODEL), lambda i, j: (0, 0))
    res = pl.pallas_call(
        body, grid=(nm, ncol),
        in_specs=[row, wg, wv, wd, cw, cb] + ([row, vec, vec] if post else []) + [ANY] * ng,
        out_specs=[pl.BlockSpec((TM, TK), lambda i, j: (i, j)), pl.BlockSpec((2, TM, TK), lambda i, j: (0, i, j)),
                   pl.BlockSpec((2, TM, TK), lambda i, j: (0, i, j)), row] + ([row, row] if post else [])
        + [ANY] * ng,
        out_shape=[jax.ShapeDtypeStruct((s, D_FF), BF16), jax.ShapeDtypeStruct((2, s, D_FF), BF16),
                   jax.ShapeDtypeStruct((2, s, D_FF), BF16), jax.ShapeDtypeStruct((s, D_MODEL), F32)]
        + ([jax.ShapeDtypeStruct((s, D_MODEL), F32), jax.ShapeDtypeStruct((s, D_MODEL), BF16)] if post else [])
        + _gathered_shapes(gather or []),
        scratch_shapes=[pltpu.VMEM((ncol, 2, HALO, TK), F32), pltpu.VMEM((TM, D_MODEL), F32)]
        + (_gather_sems(ng) if ng else []),
        compiler_params=_cparams("arbitrary", "arbitrary"), name=name)(h2, w_up, w_up, w_down, cw3, cb3,
                                                                      *(post or []), *(gather or []))
    return res[:nout], list(res[nout:])


def _ffn_backward(df, w_up, w_down, up3, cv3, cw3, name, scatter=None):
    s = df.shape[0]
    nm, ncol = s // TM, D_FF // TK
    ns = 0 if scatter is None else len(scatter[0])

    def body(*refs):
        df_ref, wg_ref, wv_ref, wd_ref, cw_ref, up_ref, cv_ref = refs[:7]
        s_in = refs[7:7 + ns]
        dup_ref, dh_ref, sums_ref = refs[7 + ns:10 + ns]
        s_out = refs[10 + ns:10 + 2 * ns]
        carry, acc = refs[10 + 2 * ns:12 + 2 * ns]
        i, j = pl.program_id(0), pl.program_id(1)
        if ns:
            start, finish = _scatter_steps(s_in, s_out, *refs[12 + 2 * ns:], scatter[1])
            pl.when((i == 0) & (j == 0))(start)

        @pl.when((i == 0) & (j == 0))
        def _():
            carry[...] = jnp.zeros_like(carry)
            sums_ref[...] = jnp.zeros_like(sums_ref)

        dy = _dot(df_ref[...], wd_ref[...], NT)
        act, grad = _gelu_tanh(cv_ref[0].astype(F32))
        dcs = (dy * cv_ref[1].astype(F32) * grad, dy * act)
        row = lax.broadcasted_iota(jnp.int32, (8, 1), 0)
        part = None
        for hh, w_ref in ((0, wg_ref), (1, wv_ref)):
            dc = dcs[hh]
            x = up_ref[hh].astype(F32)
            after1, after2 = _taps_after(dc, carry[j, hh])
            upd = jnp.zeros((8, TK), F32)
            for ridx, sm in enumerate((_colsum(after2 * x), _colsum(after1 * x), _colsum(dc * x), _colsum(dc))):
                upd = jnp.where(row == ridx, sm, upd)
            sums_ref[j, hh] += upd
            dup = (cw_ref[hh, 2:3, :] * dc + cw_ref[hh, 1:2, :] * after1 + cw_ref[hh, 0:1, :] * after2).astype(BF16)
            carry[j, hh] = dc[:HALO, :]
            dup_ref[hh] = dup
            d = _dot(dup, w_ref[...], NT)
            part = d if part is None else part + d

        @pl.when(j == 0)
        def _():
            acc[...] = part

        @pl.when(j > 0)
        def _():
            acc[...] += part

        @pl.when(j == ncol - 1)
        def _():
            dh_ref[...] = acc[...]

        if ns:
            pl.when((i == nm - 1) & (j == ncol - 1))(finish)

    wg, wv, wd, cw, _ = _ffn_weight_specs(ncol)
    rev = lambda i: nm - 1 - i
    res = pl.pallas_call(
        body, grid=(nm, ncol),
        in_specs=[pl.BlockSpec((TM, D_MODEL), lambda i, j: (rev(i), 0)), wg, wv, wd, cw,
                  pl.BlockSpec((2, TM, TK), lambda i, j: (0, rev(i), j)),
                  pl.BlockSpec((2, TM, TK), lambda i, j: (0, rev(i), j))] + [ANY] * ns,
        out_specs=[pl.BlockSpec((2, TM, TK), lambda i, j: (0, rev(i), j)),
                   pl.BlockSpec((TM, D_MODEL), lambda i, j: (rev(i), 0)),
                   pl.BlockSpec((ncol, 2, 8, TK), lambda i, j: (0, 0, 0, 0))] + [ANY] * ns,
        out_shape=[jax.ShapeDtypeStruct((2, s, D_FF), BF16), jax.ShapeDtypeStruct((s, D_MODEL), F32),
                   jax.ShapeDtypeStruct((ncol, 2, 8, TK), F32)] + (_scattered_shapes(scatter[1]) if ns else []),
        scratch_shapes=[pltpu.VMEM((ncol, 2, HALO, TK), F32), pltpu.VMEM((TM, D_MODEL), F32)]
        + (_scatter_sems(ns) if ns else []),
        compiler_params=_cparams("arbitrary", "arbitrary"), name=name)(df, w_up, w_up, w_down, cw3, up3, cv3,
                                                                      *(scatter[0] if ns else []))
    return res[:3], list(res[3:])


def _wspec(rows, cols, index_map):
    return pl.BlockSpec((None, None, rows, cols), index_map)


def _layer_forward(l, x0, h1, p, wg, tabs, gather=None, late=None, g_next=None):
    s = x0.shape[0]
    nm = s // TMM
    tag = f"_l{l}"
    riders = dict.fromkeys(DILATIONS)
    proj_rider = rope_rider = combine_rider = None
    if late is not None:
        cols = lambda t, parts: [t[:, i * t.shape[1] // parts:(i + 1) * t.shape[1] // parts] for i in range(parts)]
        (down_a, down_b), up_q = cols(late["w_down"], 2), cols(late["w_up"], 4)
        proj_rider, rope_rider, combine_rider = [late["w_out"], down_a], [up_q[2]], [up_q[3]]
        riders = dict(zip(DILATIONS, ([down_b], [up_q[0]], [up_q[1]])))
    proj = _matmul(
        h1, wg["w_in"], grid=(nm, N_CHIPS), a_spec=pl.BlockSpec((TMM, D_MODEL), lambda i, j: (i, 0)),
        b_spec=_wspec(D_MODEL, IN_COLS // N_CHIPS, lambda i, j: (j, 0, 0, 0)),
        o_spec=pl.BlockSpec((TMM, IN_COLS // N_CHIPS), lambda i, j: (i, j)), o_shape=(s, IN_COLS), o_dtype=BF16,
        dims=NN, nk=1, kaxis=None, acc_shape=None, name="proj" + tag, gather=proj_rider)
    if late is not None:
        proj, (w_out_all4, down_a) = proj
    ma = _mixer_a_fwd(proj, p["v_norm_g"], p["v_norm_b"], p["w_spatial"], p["bs_full"], p["out_norm_a"],
                      "mixer_a_fwd" + tag)
    q, k, v, rope_landed = _rope_fwd(proj, tabs, "rope_fwd" + tag, rope_rider)
    whole = lambda d: s // d // BAND <= MAX_CLASS_BLOCKS
    outs, lses, landed = zip(*[
        (_attn_fwd_class if whole(d) else _attn_fwd)(
            _as_classes(q[d]), _as_classes(k[d]), _as_classes(v[d]), f"attn_fwd_d{d}" + tag, riders[d])
        for d in DILATIONS])
    outs = [o.reshape(s, B_WIDTH) if d == 1 else o for o, d in zip(outs, DILATIONS)]
    lses = [t.reshape(s, B_WIDTH) if d == 1 else t for t, d in zip(lses, DILATIONS)]
    ob, lse, mixed, combine_landed = _attn_combine(outs, lses, p["out_norm_b"], ma, "attn_combine" + tag,
                                                   combine_rider)
    if late is not None:
        wg = dict(wg, w_out=w_out_all4, w_down=jnp.concatenate([down_a, landed[0][0]], axis=-1),
                  w_up=jnp.concatenate([landed[1][0], landed[2][0], rope_landed[0], combine_landed[0]], axis=-1))
    y1, x1, h2 = _mix_out_norm(mixed, wg["w_out"], x0, p["post_mix_norm"], p["pre_ffn_norm"], "mix_out" + tag)
    post = None if g_next is None else (x1, p["post_ffn_norm"], g_next)
    (y, up3, cv3, f, *after), gathered = _ffn_forward(h2, wg["w_up"], wg["w_down"], p["cw3"], p["cb3"],
                                                      "ffn_fwd" + tag, gather, post)
    saved = dict(x0=x0, h1=h1, proj=proj, q=q, k=k, v=v, ob=ob, lse=lse, mixed=mixed, y1=y1, x1=x1, h2=h2,
                 up3=up3, cv3=cv3, y=y, f=f)
    if after:
        saved.update(x2=after[0], h_next=after[1])
    return saved, gathered, wg


def _layer_backward(l, dx2, df, sv, p, wg, tabs, pos, scatter=None, hide=False):
    s = dx2.shape[0]
    nm = s // TMM
    tag = f"_l{l}"
    g = {}
    (dup3, dh2, conv_sums), scattered = _ffn_backward(df, wg["w_up"], wg["w_down"], sv["up3"], sv["cv3"], p["cw3"],
                                                      "ffn_bwd" + tag, scatter)
    sums = conv_sums.transpose(1, 2, 0, 3).reshape(2, 8, D_FF)
    g["conv_w"] = jnp.concatenate([sums[0, :3], sums[1, :3]], axis=1)
    g["conv_b"] = jnp.concatenate([sums[0, 3:4], sums[1, 3:4]], axis=1)
    tn = 1024
    done = {}
    gw_down = _matmul(
        sv["y"], df, grid=(D_FF // tn,), a_spec=pl.BlockSpec((s, tn), lambda k: (0, k)),
        b_spec=pl.BlockSpec((s, D_MODEL), lambda k: (0, 0)),
        o_spec=pl.BlockSpec((2, tn, D_MODEL // 2), lambda k: (0, k, 0)),
        o_shape=(2, D_FF, D_MODEL // 2), o_dtype=BF16,
        dims=TN, nk=1, kaxis=None, acc_shape=None, name="w_down_grad" + tag, halves=True)
    down_sums = _chip_sums(l, dict(w_down=gw_down), pos, ("w_down",)) if hide else None
    gw_up = _matmul(
        sv["h2"], dup3, grid=(2 * D_FF // tn,), a_spec=pl.BlockSpec((s, D_MODEL), lambda n: (0, 0)),
        b_spec=pl.BlockSpec((None, s, tn), lambda n: (n // (D_FF // tn), 0, n % (D_FF // tn))),
        o_spec=pl.BlockSpec((None, D_MODEL, tn), lambda n: (n // 2, 0, n % 2)),
        o_shape=(N_CHIPS, D_MODEL, 2 * D_FF // N_CHIPS), o_dtype=BF16,
        dims=TN, nk=1, kaxis=None, acc_shape=None, name="w_up_grad" + tag,
        scatter=(down_sums, ("w_down",)) if hide else None)
    up_sums = None
    if hide:
        gw_up, received = gw_up
        done[("w_down",)] = (down_sums, received)
        up_sums = _chip_sums(l, dict(w_up=gw_up), pos, ("w_up",))
    dx1, dy1, g["pre_ffn_norm"], g["post_mix_norm"] = _norm_bwd_mid(
        dx2, dh2, sv["x1"], p["pre_ffn_norm"], sv["y1"], p["post_mix_norm"], "norm_bwd_mid" + tag)
    w_out_all = pl.BlockSpec((N_CHIPS, None, D_MODEL // N_CHIPS, D_MODEL), lambda i: (0, 0, 0, 0))
    dmixed = _matmul(
        dy1, wg["w_out"], grid=(nm,), a_spec=pl.BlockSpec((TMM, D_MODEL), lambda i: (i, 0)), b_spec=w_out_all,
        o_spec=pl.BlockSpec((TMM, D_MODEL), lambda i: (i, 0)), o_shape=(s, D_MODEL), o_dtype=F32,
        dims=NT, nk=1, kaxis=None, acc_shape=None, name="mix_out_bwd" + tag, b_2d=(D_MODEL, D_MODEL))
    gw_out = _matmul(
        sv["mixed"], dy1, grid=(1,), a_spec=pl.BlockSpec((s, D_MODEL), lambda m: (0, 0)),
        b_spec=pl.BlockSpec((s, D_MODEL), lambda m: (0, 0)),
        o_spec=pl.BlockSpec((2, D_MODEL, D_MODEL // 2), lambda m: (0, 0, 0)),
        o_shape=(2, D_MODEL, D_MODEL // 2), o_dtype=BF16,
        dims=TN, nk=1, kaxis=None, acc_shape=None, name="w_out_grad" + tag, halves=True)
    dpa, g["out_norm_a"], g["v_norm_g"], g["v_norm_b"], dbs, g["w_spatial"] = _mixer_a_bwd(
        sv["proj"], dmixed, p["v_norm_g"], p["v_norm_b"], p["w_spatial"], p["bs_full"], p["out_norm_a"],
        "mixer_a_bwd" + tag)
    g["b_spatial"] = dbs[:, ::GROUP_DIM].T
    dob, delta, g["out_norm_b"] = _attn_bwd_prep(dmixed, sv["ob"], p["out_norm_b"], "attn_bwd_prep" + tag)
    whole = lambda d: s // d // BAND <= MAX_CLASS_BLOCKS
    halves = dict(zip(DILATIONS, ("w_up:0", "w_up:1"))) if hide else {}
    dqs, dks, dvs, received = zip(*[
        (_attn_bwd_class if whole(d) else _attn_bwd)(
            *(_as_classes(t[d]) for t in (sv["q"], sv["k"], sv["v"], dob, sv["lse"], delta)),
            f"attn_bwd_d{d}" + tag, (up_sums, (halves[d],)) if d in halves else None)
        for d in DILATIONS])
    if hide:
        done[("w_up",)] = (up_sums, [jnp.concatenate([received[0][0], received[1][0]], axis=-1)])
    nat = lambda ts: [t.reshape(s, B_WIDTH) if d == 1 else t for t, d in zip(ts, DILATIONS)]
    dproj = _rope_bwd(nat(dqs), nat(dks), nat(dvs), tabs, dpa, "rope_bwd" + tag)
    wcol = IN_COLS // N_CHIPS
    dh1 = _proj_bwd(dproj, wg["w_in"], "proj_bwd" + tag)
    gw_in = _matmul(
        sv["h1"], dproj, grid=(N_CHIPS,), a_spec=pl.BlockSpec((s, D_MODEL), lambda n: (0, 0)),
        b_spec=pl.BlockSpec((s, wcol), lambda n: (0, n)),
        o_spec=pl.BlockSpec((None, D_MODEL, wcol), lambda n: (n, 0, 0)),
        o_shape=(N_CHIPS, D_MODEL, wcol), o_dtype=BF16,
        dims=TN, nk=1, kaxis=None, acc_shape=None, name="w_in_grad" + tag)
    big = dict(w_in=gw_in, w_out=gw_out) if hide else dict(w_in=gw_in, w_up=gw_up, w_out=gw_out, w_down=gw_down)
    return dx1, dh1, big, g, scattered, done


SMALL = ("pre_mix_norm", "v_norm_g", "v_norm_b", "w_spatial", "b_spatial", "out_norm_a", "out_norm_b",
         "post_mix_norm", "pre_ffn_norm", "conv_b", "post_ffn_norm")
BIG = ("w_in", "w_out", "w_up", "w_down")
DEPTH = 2


def _layer_params(l, small, conv_w_full):
    p = {n: small[n][l].reshape(1, -1) for n in SMALL if n not in ("w_spatial", "b_spatial")}
    p["w_spatial"] = small["w_spatial"][l]
    p["bs_full"] = jnp.repeat(small["b_spatial"][l].T, GROUP_DIM, axis=1)
    p["cw3"] = conv_w_full[l].reshape(3, 2, D_FF).transpose(1, 0, 2)
    p["cb3"] = small["conv_b"][l].reshape(2, 1, D_FF)
    return p


def _mesh_pos():
    return lax.axis_index("x"), lax.axis_index("y"), lax.axis_index("c")


def _other_chips(x, y):
    return [(1 - x, y), (x, 1 - y), (1 - x, 1 - y)]


def _gathered_shapes(blocks):
    return [jax.ShapeDtypeStruct((N_CHIPS, 1) + a.shape, a.dtype) for a in blocks]


def _gather_sems(nw):
    n = 2 * nw * (N_CHIPS - 1) + nw
    return [pltpu.SemaphoreType.DMA((n,)), pltpu.SemaphoreType.DMA((n,))]


def _gather_steps(ins, outs, send, recv):
    nw, nrel = len(ins), N_CHIPS - 1
    x, y, c = _mesh_pos()
    mine, sibling, chips = 2 * x + y, (x, y, 1 - c), _other_chips(x, y)

    def copy(src, dst, slot, to):
        return pltpu.make_async_remote_copy(src_ref=src, dst_ref=dst, send_sem=send.at[slot],
                                            recv_sem=recv.at[slot], device_id=to, device_id_type=MESH)

    def half_rows(t, core):
        rows = ins[t].shape[0] // 2
        return pl.ds(pl.multiple_of(core * rows, rows), rows)

    def landing(t, chip, core):
        return outs[t].at[chip, 0, half_rows(t, core), :]

    slots = [(t, r, chip) for t in range(nw) for r, chip in enumerate(chips)]
    own = [copy(ins[t], outs[t].at[mine, 0], 2 * nw * nrel + t, sibling) for t in range(nw)]
    first = [copy(ins[t].at[half_rows(t, c), :], landing(t, mine, c), t * nrel + r, (px, py, c))
             for t, r, (px, py) in slots]
    relays = [copy(landing(t, 2 * px + py, c), landing(t, 2 * px + py, c), nw * nrel + t * nrel + r, sibling)
              for t, r, (px, py) in slots]

    def start():
        for cp in own + first:
            cp.start()

    def relay():
        for (t, r, (px, py)), cp in zip(slots, relays):
            copy(landing(t, 2 * px + py, c), landing(t, 2 * px + py, c), t * nrel + r, (px, py, c)).wait_recv()
            cp.start()

    def finish():
        for t, r, (px, py) in slots:
            passed = landing(t, 2 * px + py, 1 - c)
            copy(passed, passed, nw * nrel + t * nrel + r, sibling).wait_recv()
        for cp in first + relays:
            cp.wait_send()
        for cp in own:
            cp.wait()

    return start, relay, finish


def _gather_weights(blocks, name):
    nw = len(blocks)

    def body(*refs):
        start, relay, finish = _gather_steps(refs[:nw], refs[nw:2 * nw], *refs[2 * nw:])
        start()
        relay()
        finish()

    return pl.pallas_call(
        body, in_specs=[ANY] * nw, out_specs=[ANY] * nw, out_shape=_gathered_shapes(blocks),
        scratch_shapes=_gather_sems(nw), name=name)(*blocks)


HALF = 512

GRAD_GEOM = {"w_in": ("rows", D_MODEL, IN_COLS // N_CHIPS), "w_up": ("rows", D_MODEL, 2 * D_FF // N_CHIPS),
             "w_out": ("cols", D_MODEL, D_MODEL // N_CHIPS), "w_down": ("cols", D_FF, D_FF // N_CHIPS)}


def _exchange_shape(n):
    kind, a, b = GRAD_GEOM[n]
    return (N_CHIPS, HALF, b) if kind == "rows" else (a, HALF)


def _piece_shape(n):
    name, _, part = n.partition(":")
    kind, _, b = GRAD_GEOM[name]
    if part:
        assert kind == "rows"
        return (HALF, b // 2)
    return (HALF, b) if kind == "rows" else (b, HALF)


def _half_of(ref, n, core):
    if GRAD_GEOM[n][0] == "rows":
        return ref.at[:, pl.ds(pl.multiple_of(core * HALF, HALF), HALF), :]
    return ref.at[core]


def _piece_of(ref, n, chip):
    name, _, part = n.partition(":")
    kind, _, b = GRAD_GEOM[name]
    if part:
        return ref.at[chip, :, pl.ds(int(part) * (b // 2), b // 2)]
    return ref.at[chip] if kind == "rows" else ref.at[pl.ds(pl.multiple_of(chip * b, b), b), :]


def _pair_exchange(g, names, name):
    n = len(names)

    def body(*refs):
        send, recv = refs[2 * n:]
        x, y, c = _mesh_pos()
        o = 1 - c
        cps = [pltpu.make_async_remote_copy(src_ref=_half_of(refs[t], nm, o), dst_ref=refs[n + t], send_sem=send.at[t],
                                            recv_sem=recv.at[t], device_id=(x, y, o), device_id_type=MESH)
               for t, nm in enumerate(names)]
        for cp in cps:
            cp.start()
        for cp in cps:
            cp.wait()

    return pl.pallas_call(
        body, in_specs=[ANY] * n, out_specs=[ANY] * n,
        out_shape=[jax.ShapeDtypeStruct(_exchange_shape(nm), BF16) for nm in names],
        scratch_shapes=[pltpu.SemaphoreType.DMA((n,)), pltpu.SemaphoreType.DMA((n,))],
        name=name)(*[g[nm] for nm in names])


def _pair_sum(g, recv, pos, names, name_prefix):
    def add(a, b, grid, a_spec, b_spec, name):
        def body(pos_ref, a_ref, b_ref, o_ref):
            o_ref[...] = (a_ref[...].astype(F32) + b_ref[...].astype(F32)).astype(BF16)

        return pl.pallas_call(
            body, grid_spec=pltpu.PrefetchScalarGridSpec(
                num_scalar_prefetch=1, grid=grid, in_specs=[a_spec, b_spec], out_specs=b_spec),
            out_shape=jax.ShapeDtypeStruct(b.shape, BF16), compiler_params=_cparams("parallel"), name=name)(pos, a, b)

    out = []
    for nm, r in zip(names, recv):
        kind, rows, width = GRAD_GEOM[nm]
        if kind == "rows":
            out.append(add(g[nm], r, (N_CHIPS,), pl.BlockSpec((None, HALF, width), lambda j, pos: (j, pos[2], 0)),
                           pl.BlockSpec((None, HALF, width), lambda j, pos: (j, 0, 0)), f"{name_prefix}_{nm}"))
        else:
            out.append(add(g[nm], r, (rows // D_MODEL,), pl.BlockSpec((None, D_MODEL, HALF), lambda j, pos: (pos[2], j, 0)),
                           pl.BlockSpec((D_MODEL, HALF), lambda j, pos: (j, 0)), f"{name_prefix}_{nm}"))
    return out


def _scattered_shapes(names):
    return [jax.ShapeDtypeStruct((N_CHIPS - 1,) + _piece_shape(nm), BF16) for nm in names]


def _scatter_sems(n):
    return [pltpu.SemaphoreType.DMA((n * (N_CHIPS - 1),)), pltpu.SemaphoreType.DMA((n * (N_CHIPS - 1),))]


def _scatter_steps(sums, outs, send, recv, names):
    nrel = N_CHIPS - 1
    x, y, c = _mesh_pos()
    cps = []
    for r, (px, py) in enumerate(_other_chips(x, y)):
        for t, nm in enumerate(names):
            cps.append(pltpu.make_async_remote_copy(
                src_ref=_piece_of(sums[t], nm, 2 * px + py), dst_ref=outs[t].at[r], send_sem=send.at[t * nrel + r],
                recv_sem=recv.at[t * nrel + r], device_id=(px, py, c), device_id_type=MESH))

    def start():
        for cp in cps:
            cp.start()

    def finish():
        for cp in cps:
            cp.wait()

    return start, finish


def _chip_scatter(sums, names, name):
    n = len(names)

    def body(*refs):
        start, finish = _scatter_steps(refs[:n], refs[n:2 * n], *refs[2 * n:], names)
        start()
        finish()

    return pl.pallas_call(
        body, in_specs=[ANY] * n, out_specs=[ANY] * n, out_shape=_scattered_shapes(names),
        scratch_shapes=_scatter_sems(n), name=name)(*sums)


def _chip_sum(sums, recv, pos, names, name_prefix):
    def add(a, b, a_spec, shape, name):
        def body(pos_ref, a_ref, b_ref, o_ref):
            tot = a_ref[...].astype(F32)
            for r in range(N_CHIPS - 1):
                tot = tot + b_ref[r].astype(F32)
            o_ref[...] = tot

        return pl.pallas_call(
            body, grid_spec=pltpu.PrefetchScalarGridSpec(
                num_scalar_prefetch=1, grid=(1,), in_specs=[a_spec, pl.BlockSpec(b.shape, lambda i, pos: (0, 0, 0))],
                out_specs=pl.BlockSpec((None,) + shape, lambda i, pos: (pos[2], 0, 0))),
            out_shape=jax.ShapeDtypeStruct((2,) + shape, F32), compiler_params=_cparams("arbitrary"),
            name=name)(pos, a, b)

    chip = lambda pos: 2 * pos[0] + pos[1]
    out = []
    for nm, a, b in zip(names, sums, recv):
        shape = _piece_shape(nm)
        if GRAD_GEOM[nm][0] == "rows":
            spec = pl.BlockSpec((None,) + shape, lambda i, pos: (chip(pos), 0, 0))
        else:
            spec = pl.BlockSpec(shape, lambda i, pos: (chip(pos), 0))
        out.append(add(a, b, spec, shape, f"{name_prefix}_{nm}"))
    return out


def _pair_share(totals, name):
    n = len(totals)

    def body(*refs):
        ins, outs = refs[:n], refs[n:2 * n]
        send, recv = refs[2 * n:]
        x, y, c = _mesh_pos()
        o = 1 - c
        cps = [pltpu.make_async_remote_copy(src_ref=ins[t].at[c], dst_ref=outs[t].at[c], send_sem=send.at[t],
                                            recv_sem=recv.at[t], device_id=(x, y, o), device_id_type=MESH)
               for t in range(n)]
        for cp in cps:
            cp.start()
        for t in range(n):
            pltpu.make_async_remote_copy(src_ref=ins[t].at[o], dst_ref=outs[t].at[o], send_sem=send.at[t],
                                         recv_sem=recv.at[t], device_id=(x, y, o), device_id_type=MESH).wait_recv()
        for cp in cps:
            cp.wait_send()

    return pl.pallas_call(
        body, in_specs=[ANY] * n, out_specs=[ANY] * n,
        out_shape=[jax.ShapeDtypeStruct(t.shape, t.dtype) for t in totals],
        scratch_shapes=[pltpu.SemaphoreType.DMA((n,)), pltpu.SemaphoreType.DMA((n,))],
        input_output_aliases={t: t for t in range(n)}, name=name)(*totals)


def _chip_sums(l, g, pos, names):
    tag = f"l{l}_" + "_".join(names)
    recv = _pair_exchange(g, names, "pair_exchange_" + tag)
    return _pair_sum(g, recv, pos, names, "pair_sum_" + tag)


def _gradient_shards(l, sums, scattered, pos, names):
    tag = f"l{l}_" + "_".join(names)
    halves = _pair_share(_chip_sum(sums, scattered, pos, names, "chip_sum_" + tag), "pair_share_" + tag)
    out = {}
    for nm, t in zip(names, halves):
        rows, cols = _piece_shape(nm)
        out[nm] = t.reshape(2 * rows, cols) if GRAD_GEOM[nm][0] == "rows" else t.transpose(1, 0, 2).reshape(rows, 2 * cols)
    return out


N_DEV = 8


def _allreduce_small(packed, name):
    rows = packed.shape[0]

    def body(x_ref, out_ref, gath, send_sems, recv_sems, local_sem):
        x, y, c = _mesh_pos()
        me, sibling = (x, y, c), (x, y, 1 - c)
        chips = _other_chips(x, y)

        def blk(px, py, pc):
            return gath.at[pl.ds(pl.multiple_of((4 * px + 2 * py + pc) * rows, 8), rows), :]

        def copy(k, block, to, src=None):
            return pltpu.make_async_remote_copy(
                src_ref=blk(*block) if src is None else src, dst_ref=blk(*block), send_sem=send_sems.at[k],
                recv_sem=recv_sems.at[k], device_id=to, device_id_type=MESH)

        mine = pltpu.make_async_copy(x_ref, blk(*me), local_sem)
        mine.start()
        first = [copy(0, me, sibling, src=x_ref)]
        first += [copy(1 + j, me, (*chip, c), src=x_ref) for j, chip in enumerate(chips)]
        for cp in first:
            cp.start()
        passed = [copy(4 + j, (*chip, c), sibling) for j, chip in enumerate(chips)]
        for j, chip in enumerate(chips):
            copy(1 + j, (*chip, c), me).wait_recv()
            passed[j].start()
        copy(0, sibling, me).wait_recv()
        for j, chip in enumerate(chips):
            copy(4 + j, (*chip, 1 - c), me).wait_recv()
        for cp in first + passed:
            cp.wait_send()
        mine.wait()
        tot = gath[0:rows, :]
        for d in range(1, N_DEV):
            tot = tot + gath[d * rows:(d + 1) * rows, :]
        out_ref[...] = tot

    vmem = pl.BlockSpec(memory_space=pltpu.VMEM)
    return pl.pallas_call(
        body, in_specs=[vmem], out_specs=vmem, out_shape=jax.ShapeDtypeStruct((rows, LANES), F32),
        scratch_shapes=[pltpu.VMEM((N_DEV * rows, LANES), F32), pltpu.SemaphoreType.DMA((7,)),
                        pltpu.SemaphoreType.DMA((7,)), pltpu.SemaphoreType.DMA],
        compiler_params=pltpu.CompilerParams(vmem_limit_bytes=VMEM_LIMIT_BYTES),
        name=name)(packed)


def _adamw(w, g, m, v, name):
    rows, cols = w.shape
    tr = 256 if rows % 256 == 0 else rows

    def body(w_ref, g_ref, m_ref, v_ref, d_ref, mo_ref, vo_ref):
        gv = g_ref[...]
        mn = ADAM_B1 * m_ref[...] + (1.0 - ADAM_B1) * gv
        vn = ADAM_B2 * v_ref[...] + (1.0 - ADAM_B2) * (gv * gv)
        m_hat = mn / (1.0 - ADAM_B1 ** ADAM_STEP)
        v_hat = vn / (1.0 - ADAM_B2 ** ADAM_STEP)
        d_ref[...] = -ADAM_LR * (m_hat / (jnp.sqrt(v_hat) + ADAM_EPS) + ADAM_WD * w_ref[...])
        mo_ref[...] = mn
        vo_ref[...] = vn

    spec = pl.BlockSpec((tr, cols), lambda i: (i, 0))
    return pl.pallas_call(
        body, grid=(rows // tr,), in_specs=[spec] * 4, out_specs=[spec] * 3,
        out_shape=[jax.ShapeDtypeStruct((rows, cols), F32)] * 3, compiler_params=_cparams("parallel"),
        name=name)(w, g, m, v)


def _adamw_nd(w, g, m, v, name):
    cols = w.shape[-1] if w.shape[-1] % LANES == 0 else LANES
    outs = _adamw(*(t.reshape(-1, cols) for t in (w, g, m, v)), name)
    return tuple(t.reshape(w.shape) for t in outs)


def _pack(arrays):
    return jnp.concatenate([a.reshape(-1, LANES) for a in arrays], axis=0)


def _unpack(packed, shapes):
    out, row = [], 0
    for sh in shapes:
        n = math.prod(sh) // LANES
        out.append(packed[row:row + n].reshape(sh))
        row += n
    return out


WEIGHTS = ("pre_mix_norm", "w_in", "v_norm_g", "v_norm_b", "w_spatial", "b_spatial", "out_norm_a", "out_norm_b",
           "w_out", "post_mix_norm", "pre_ffn_norm", "w_up", "conv_w", "conv_b", "w_down", "post_ffn_norm")


def kernel(x, pre_mix_norm, w_in, v_norm_g, v_norm_b, w_spatial, b_spatial, out_norm_a, out_norm_b, w_out, post_mix_norm, pre_ffn_norm, w_up, conv_w, conv_b, w_down, post_ffn_norm, loss_target, m_pre_mix_norm, m_w_in, m_v_norm_g, m_v_norm_b, m_w_spatial, m_b_spatial, m_out_norm_a, m_out_norm_b, m_w_out, m_post_mix_norm, m_pre_ffn_norm, m_w_up, m_conv_w, m_conv_b, m_w_down, m_post_ffn_norm, v_pre_mix_norm, v_w_in, v_v_norm_g, v_v_norm_b, v_w_spatial, v_b_spatial, v_out_norm_a, v_out_norm_b, v_w_out, v_post_mix_norm, v_pre_ffn_norm, v_w_up, v_conv_w, v_conv_b, v_w_down, v_post_ffn_norm):
    w = dict(pre_mix_norm=pre_mix_norm, w_in=w_in, v_norm_g=v_norm_g, v_norm_b=v_norm_b, w_spatial=w_spatial,
             b_spatial=b_spatial, out_norm_a=out_norm_a, out_norm_b=out_norm_b, w_out=w_out,
             post_mix_norm=post_mix_norm, pre_ffn_norm=pre_ffn_norm, w_up=w_up, conv_w=conv_w, conv_b=conv_b,
             w_down=w_down, post_ffn_norm=post_ffn_norm)
    m = dict(pre_mix_norm=m_pre_mix_norm, w_in=m_w_in, v_norm_g=m_v_norm_g, v_norm_b=m_v_norm_b,
             w_spatial=m_w_spatial, b_spatial=m_b_spatial, out_norm_a=m_out_norm_a, out_norm_b=m_out_norm_b,
             w_out=m_w_out, post_mix_norm=m_post_mix_norm, pre_ffn_norm=m_pre_ffn_norm, w_up=m_w_up,
             conv_w=m_conv_w, conv_b=m_conv_b, w_down=m_w_down, post_ffn_norm=m_post_ffn_norm)
    v = dict(pre_mix_norm=v_pre_mix_norm, w_in=v_w_in, v_norm_g=v_v_norm_g, v_norm_b=v_v_norm_b,
             w_spatial=v_w_spatial, b_spatial=v_b_spatial, out_norm_a=v_out_norm_a, out_norm_b=v_out_norm_b,
             w_out=v_w_out, post_mix_norm=v_post_mix_norm, pre_ffn_norm=v_pre_ffn_norm, w_up=v_w_up,
             conv_w=v_conv_w, conv_b=v_conv_b, w_down=v_w_down, post_ffn_norm=v_post_ffn_norm)
    pos = jnp.stack([lax.axis_index("x"), lax.axis_index("y"), lax.axis_index("c")]).astype(jnp.int32)
    chip = 2 * lax.axis_index("x") + lax.axis_index("y")

    cw_cols = conv_w.shape[-1]
    blocks = [{n: w[n][l].astype(BF16) for n in BIG} for l in range(DEPTH)]
    w_in0, cw_all = _gather_weights([blocks[0]["w_in"], conv_w.reshape(-1, LANES)], "gather_w_in_l0")
    wg = dict(w_in=w_in0)
    conv_w_full = cw_all.reshape(N_CHIPS, DEPTH, 3, cw_cols).transpose(1, 2, 0, 3).reshape(DEPTH, 3, 2 * D_FF)

    small = {n: w[n] for n in SMALL}
    xs, target = x[0], loss_target[0]
    tabs = _rope_tables(xs.shape[0])
    params = [_layer_params(l, small, conv_w_full) for l in range(DEPTH)]
    saved, wgs = [], []
    xin = xs
    h = _rms_cast(xin, params[0]["pre_mix_norm"], "pre_mix_l0")
    for l in range(DEPTH):
        sv, gathered, wg = _layer_forward(l, xin, h, params[l], wg, tabs,
                                          [blocks[l + 1][n] for n in BIG] if l + 1 < DEPTH else None,
                                          blocks[0] if l == 0 else None,
                                          params[l + 1]["pre_mix_norm"] if l + 1 < DEPTH else None)
        saved.append(sv)
        wgs.append(wg)
        if l + 1 < DEPTH:
            wg = dict(zip(BIG, gathered))
            xin, h = sv["x2"], sv["h_next"]
    loss_part, dx, df, g_post = _loss_norm_bwd(saved[-1]["x1"], saved[-1]["f"], params[-1]["post_ffn_norm"], target,
                                               "loss")
    smalls, shards = [None] * DEPTH, [{} for _ in range(DEPTH)]
    pending = None
    for l in reversed(range(DEPTH)):
        dx1, dh1, big, smalls[l], scattered, done = _layer_backward(l, dx, df, saved[l], params[l], wgs[l], tabs, pos,
                                                                    pending[1:] if pending else None, hide=l == 0)
        smalls[l]["post_ffn_norm"] = g_post
        if l > 0:
            dx, smalls[l]["pre_mix_norm"], df, g_post = _norm_bwd_in_out(
                dx1, dh1, saved[l]["x0"], params[l]["pre_mix_norm"], saved[l - 1]["f"], params[l - 1]["post_ffn_norm"],
                f"norm_bwd_in_out_l{l}")
        else:
            dx, smalls[l]["pre_mix_norm"] = _norm_bwd_in(dx1, dh1, saved[l]["x0"], params[l]["pre_mix_norm"],
                                                         "norm_bwd_in_l0")
        if pending:
            shards[pending[0]].update(_gradient_shards(pending[0], pending[1], scattered, pos, pending[2]))
        for names, (sums, received) in done.items():
            shards[l].update(_gradient_shards(l, sums, received, pos, names))
        names = tuple(big)
        pending = (l, _chip_sums(l, big, pos, names), names)
    shards[pending[0]].update(_gradient_shards(
        pending[0], pending[1], _chip_scatter(pending[1], pending[2], f"chip_scatter_l{pending[0]}"), pos, pending[2]))

    small_shapes = [w[n].shape for n in SMALL]
    stacked = [jnp.stack([smalls[l][n].reshape(w[n].shape[1:]) for l in range(DEPTH)]) for n in SMALL]
    cw_grad = jnp.stack([smalls[l]["conv_w"] for l in range(DEPTH)])
    packed = _pack(stacked + [cw_grad, loss_part])
    total = _allreduce_small(packed, "allreduce_small")
    parts = _unpack(total, small_shapes + [cw_grad.shape, (8, LANES)])
    g_small = dict(zip(SMALL, parts[:len(SMALL)]))
    loss = parts[-1][0, 0]
    g_conv_w = lax.dynamic_slice(parts[-2], (0, 0, chip * cw_cols), conv_w.shape)

    grads = {n: jnp.stack([shards[l][n] for l in range(DEPTH)]) for n in BIG}
    grads.update(g_small)
    grads["conv_w"] = g_conv_w

    dp, mp, vp = _adamw(_pack([w[n] for n in SMALL]), _pack([g_small[n] for n in SMALL]),
                        _pack([m[n] for n in SMALL]), _pack([v[n] for n in SMALL]), "adamw_small")
    delta = dict(zip(SMALL, _unpack(dp, small_shapes)))
    new_m = dict(zip(SMALL, _unpack(mp, small_shapes)))
    new_v = dict(zip(SMALL, _unpack(vp, small_shapes)))
    for n in BIG + ("conv_w",):
        delta[n], new_m[n], new_v[n] = _adamw_nd(w[n], grads[n], m[n], v[n], "adamw_" + n)

    return (loss, dx[None], *[grads[n] for n in WEIGHTS], *[delta[n] for n in WEIGHTS],
            *[new_m[n] for n in WEIGHTS], *[new_v[n] for n in WEIGHTS])
```

```python
import functools
import math

import jax
import jax.numpy as jnp
import numpy as np
from jax import lax
from jax.experimental import pallas as pl
from jax.experimental.pallas import tpu as pltpu

F32 = jnp.float32
BF16 = jnp.bfloat16
MESH = pl.DeviceIdType.MESH

D_MODEL = 1024
A_WIDTH = 512
A_GROUPS = 4
GROUP_DIM = 128
CHUNK = 128
B_WIDTH = 512
HEAD_DIM = 64
ROT_DIM = 16
ROPE_THETA = 500000.0
DILATIONS = (1, 4, 16)
BAND = 128
IN_COLS = 2560
D_FF = 4096
EPS = 1e-6
NEG_INF = -1e30
N_CHIPS = 4
LANES = 128

ADAM_LR = 0.001
ADAM_B1 = 0.9
ADAM_B2 = 0.999
ADAM_EPS = 1e-08
ADAM_WD = 0.01
ADAM_STEP = 10

VMEM_LIMIT_BYTES = 56 * 1024 * 1024
RSQRT2 = 0.7071067811865476
INV_SQRT_2PI = 0.3989422804014327
GELU_C = 0.7978845608028654
GELU_A = 0.044715

ANY = pl.BlockSpec(memory_space=pl.ANY)
NN = ((1,), (0,))
NT = ((1,), (1,))
TN = ((0,), (0,))


def _cparams(*sem):
    return pltpu.CompilerParams(dimension_semantics=sem, vmem_limit_bytes=VMEM_LIMIT_BYTES)


def _dot(a, b, dims):
    return lax.dot_general(a, b, (dims, ((), ())), preferred_element_type=F32)


def _rsq_mean(a):
    return lax.rsqrt(jnp.mean(a * a, axis=-1, keepdims=True) + EPS)


def _rms_bwd(a, r, g, dz):
    t = dz * g
    da = r * t - a * (r * r * r) * jnp.mean(t * a, axis=-1, keepdims=True)
    return da, dz * a * r


def _colsum(a):
    return jnp.sum(a, axis=0, keepdims=True)


def _gelu_tanh(x):
    u = x * x
    t = jnp.tanh(x * (GELU_C + (GELU_C * GELU_A) * u))
    hx = 0.5 * x
    act = hx + hx * t
    grad = 0.5 + 0.5 * t + (hx - hx * t * t) * (GELU_C + (3.0 * GELU_C * GELU_A) * u)
    return act, grad


def _grid_edges(grid):
    ids = [pl.program_id(ax) for ax in range(len(grid))]
    first = functools.reduce(jnp.logical_and, [i == 0 for i in ids])
    last = functools.reduce(jnp.logical_and, [i == n - 1 for i, n in zip(ids, grid)])
    return first, last


def _matmul(a, b, *, grid, a_spec, b_spec, o_spec, o_shape, o_dtype, dims, nk, kaxis, acc_shape, name, b_2d=None,
            halves=False, scatter=None, gather=None):
    assert scatter is None or gather is None
    ns = len(scatter[0]) if scatter else len(gather) if gather else 0

    def body(*refs):
        a_ref, b_ref = refs[:2]
        o_ref = refs[2 + ns]
        scratch = refs[3 + 2 * ns:]
        if ns:
            first, last = _grid_edges(grid)
            if scatter:
                start, finish = _scatter_steps(refs[2:2 + ns], refs[3 + ns:3 + 2 * ns], scratch[-2], scratch[-1],
                                               scatter[1])
            else:
                start, relay, last_wait = _gather_steps(refs[2:2 + ns], refs[3 + ns:3 + 2 * ns], scratch[-2],
                                                        scratch[-1])

                def finish():
                    relay()
                    last_wait()
            pl.when(first)(start)
        def store(val):
            if halves:
                half = val.shape[1] // 2
                o_ref[0] = val[:, :half].astype(o_dtype)
                o_ref[1] = val[:, half:].astype(o_dtype)
            else:
                o_ref[...] = val.astype(o_dtype)

        bv = b_ref[...] if b_2d is None else b_ref[...].reshape(b_2d)
        part = _dot(a_ref[...], bv, dims)
        if nk == 1:
            store(part)
        else:
            acc = scratch[0]
            k = pl.program_id(kaxis)

            @pl.when(k == 0)
            def _():
                acc[...] = part

            @pl.when(k > 0)
            def _():
                acc[...] += part

            @pl.when(k == nk - 1)
            def _():
                store(acc[...])

        if ns:
            pl.when(last)(finish)

    sem = tuple("arbitrary" if (ns or (nk > 1 and ax == kaxis)) else "parallel" for ax in range(len(grid)))
    riding = list(scatter[0]) if scatter else list(gather or [])
    rider_shapes = _scattered_shapes(scatter[1]) if scatter else _gathered_shapes(riding)
    rider_sems = _scatter_sems(ns) if scatter else _gather_sems(ns) if gather else []
    res = pl.pallas_call(
        body, grid=grid, in_specs=[a_spec, b_spec] + [ANY] * ns, out_specs=[o_spec] + [ANY] * ns,
        out_shape=[jax.ShapeDtypeStruct(o_shape, o_dtype)] + rider_shapes,
        scratch_shapes=([pltpu.VMEM(acc_shape, F32)] if nk > 1 else []) + rider_sems,
        compiler_params=_cparams(*sem), name=name)(a, b, *riding)
    return (res[0], list(res[1:])) if ns else res[0]


def _mix_out_norm(mixed, w_out, x0, g_post, g_next, name):
    s, d = x0.shape
    tm = 512

    def body(a_ref, w_ref, x_ref, gp_ref, gn_ref, y_ref, x1_ref, h_ref):
        y = _dot(a_ref[...], w_ref[...].reshape(d, d), NN)
        y_ref[...] = y
        x1 = x_ref[...] + y * _rsq_mean(y) * gp_ref[...]
        x1_ref[...] = x1
        h_ref[...] = (x1 * _rsq_mean(x1) * gn_ref[...]).astype(BF16)

    row = pl.BlockSpec((tm, d), lambda i: (i, 0))
    vec = pl.BlockSpec((1, d), lambda i: (0, 0))
    return pl.pallas_call(
        body, grid=(s // tm,),
        in_specs=[row, pl.BlockSpec((N_CHIPS, None, d // N_CHIPS, d), lambda i: (0, 0, 0, 0)), row, vec, vec],
        out_specs=[row, row, row],
        out_shape=[jax.ShapeDtypeStruct((s, d), F32), jax.ShapeDtypeStruct((s, d), F32),
                   jax.ShapeDtypeStruct((s, d), BF16)],
        compiler_params=_cparams("parallel"), name=name)(mixed, w_out, x0, g_post, g_next)


def _proj_bwd(dproj, w_in, name):
    s = dproj.shape[0]
    wcol = IN_COLS // N_CHIPS

    def body(a_ref, w_ref, o_ref):
        acc = _dot(a_ref[:, :wcol], w_ref[0], NT)
        for j in range(1, N_CHIPS):
            acc = acc + _dot(a_ref[:, j * wcol:(j + 1) * wcol], w_ref[j], NT)
        o_ref[...] = acc

    return pl.pallas_call(
        body, grid=(s // TMM,),
        in_specs=[pl.BlockSpec((TMM, IN_COLS), lambda i: (i, 0)),
                  pl.BlockSpec((N_CHIPS, None, D_MODEL, wcol), lambda i: (0, 0, 0, 0))],
        out_specs=pl.BlockSpec((TMM, D_MODEL), lambda i: (i, 0)),
        out_shape=jax.ShapeDtypeStruct((s, D_MODEL), F32), compiler_params=_cparams("parallel"), name=name)(dproj, w_in)


TM = 512
TMM = 1024


TR = 256


def _row_spec(width, col=0):
    return pl.BlockSpec((TR, width), lambda i, col=col: (i, col))


def _vec_spec(width):
    return pl.BlockSpec((1, width), lambda i: (0, 0))


def _rms_cast(x, g, name):
    s, d = x.shape

    def body(x_ref, g_ref, h_ref):
        a = x_ref[...]
        h_ref[...] = (a * _rsq_mean(a) * g_ref[...]).astype(BF16)

    return pl.pallas_call(
        body, grid=(s // TR,), in_specs=[_row_spec(d), _vec_spec(d)], out_specs=_row_spec(d),
        out_shape=jax.ShapeDtypeStruct((s, d), BF16), compiler_params=_cparams("parallel"), name=name)(x, g)


def _acc_init(refs):
    @pl.when(pl.program_id(0) == 0)
    def _():
        for r in refs:
            r[...] = jnp.zeros_like(r)


def _loss_norm_bwd(x1, f, g_post, target, name):
    s, d = x1.shape

    def body(x_ref, f_ref, gp_ref, t_ref, loss_ref, dx_ref, df_ref, dg_ref):
        _acc_init([loss_ref, dg_ref])
        fv = f_ref[...]
        r = _rsq_mean(fv)
        err = x_ref[...] + fv * r * gp_ref[...] - t_ref[...]
        dx = err * (1.0 / d)
        dx_ref[...] = dx
        part = 0.5 * jnp.sum(jnp.mean(err * err, axis=-1, keepdims=True), axis=0, keepdims=True)
        loss_ref[...] += jnp.broadcast_to(part, loss_ref.shape)
        da, dgt = _rms_bwd(fv, r, gp_ref[...], dx)
        df_ref[...] = da.astype(BF16)
        dg_ref[...] += _colsum(dgt)

    return pl.pallas_call(
        body, grid=(s // TR,), in_specs=[_row_spec(d), _row_spec(d), _vec_spec(d), _row_spec(d)],
        out_specs=[pl.BlockSpec((8, LANES), lambda i: (0, 0)), _row_spec(d), _row_spec(d), _vec_spec(d)],
        out_shape=[jax.ShapeDtypeStruct((8, LANES), F32), jax.ShapeDtypeStruct((s, d), F32),
                   jax.ShapeDtypeStruct((s, d), BF16), jax.ShapeDtypeStruct((1, d), F32)],
        compiler_params=_cparams("arbitrary"), name=name)(x1, f, g_post, target)


def _norm_bwd_mid(dx2, dh2, x1, g_pf, y1, g_pm, name):
    s, d = dx2.shape

    def body(dx2_ref, dh_ref, x1_ref, gpf_ref, y1_ref, gpm_ref, dx1_ref, dy1_ref, dgpf_ref, dgpm_ref):
        _acc_init([dgpf_ref, dgpm_ref])
        x1 = x1_ref[...]
        da, dgt = _rms_bwd(x1, _rsq_mean(x1), gpf_ref[...], dh_ref[...])
        dx1 = dx2_ref[...] + da
        dx1_ref[...] = dx1
        dgpf_ref[...] += _colsum(dgt)
        y1 = y1_ref[...]
        dy, dgt2 = _rms_bwd(y1, _rsq_mean(y1), gpm_ref[...], dx1)
        dy1_ref[...] = dy.astype(BF16)
        dgpm_ref[...] += _colsum(dgt2)

    return pl.pallas_call(
        body, grid=(s // TR,),
        in_specs=[_row_spec(d), _row_spec(d), _row_spec(d), _vec_spec(d), _row_spec(d), _vec_spec(d)],
        out_specs=[_row_spec(d), _row_spec(d), _vec_spec(d), _vec_spec(d)],
        out_shape=[jax.ShapeDtypeStruct((s, d), F32), jax.ShapeDtypeStruct((s, d), BF16),
                   jax.ShapeDtypeStruct((1, d), F32), jax.ShapeDtypeStruct((1, d), F32)],
        compiler_params=_cparams("arbitrary"), name=name)(dx2, dh2, x1, g_pf, y1, g_pm)


def _norm_bwd_in_out(dx1, dh1, x0, g1, f_below, g_post_below, name):
    s, d = dx1.shape

    def body(dx1_ref, dh_ref, x0_ref, g_ref, f_ref, gp_ref, dx0_ref, dg_ref, df_ref, dgp_ref):
        _acc_init([dg_ref, dgp_ref])
        x0 = x0_ref[...]
        da, dgt = _rms_bwd(x0, _rsq_mean(x0), g_ref[...], dh_ref[...])
        dx0 = dx1_ref[...] + da
        dx0_ref[...] = dx0
        dg_ref[...] += _colsum(dgt)
        fv = f_ref[...]
        db, dgt2 = _rms_bwd(fv, _rsq_mean(fv), gp_ref[...], dx0)
        df_ref[...] = db.astype(BF16)
        dgp_ref[...] += _colsum(dgt2)

    return pl.pallas_call(
        body, grid=(s // TR,),
        in_specs=[_row_spec(d), _row_spec(d), _row_spec(d), _vec_spec(d), _row_spec(d), _vec_spec(d)],
        out_specs=[_row_spec(d), _vec_spec(d), _row_spec(d), _vec_spec(d)],
        out_shape=[jax.ShapeDtypeStruct((s, d), F32), jax.ShapeDtypeStruct((1, d), F32),
                   jax.ShapeDtypeStruct((s, d), BF16), jax.ShapeDtypeStruct((1, d), F32)],
        compiler_params=_cparams("arbitrary"), name=name)(dx1, dh1, x0, g1, f_below, g_post_below)


def _norm_bwd_in(dx1, dh1, x0, g1, name):
    s, d = dx1.shape

    def body(dx1_ref, dh_ref, x0_ref, g_ref, dx0_ref, dg_ref):
        _acc_init([dg_ref])
        x0 = x0_ref[...]
        da, dgt = _rms_bwd(x0, _rsq_mean(x0), g_ref[...], dh_ref[...])
        dx0_ref[...] = dx1_ref[...] + da
        dg_ref[...] += _colsum(dgt)

    return pl.pallas_call(
        body, grid=(s // TR,), in_specs=[_row_spec(d), _row_spec(d), _row_spec(d), _vec_spec(d)],
        out_specs=[_row_spec(d), _vec_spec(d)],
        out_shape=[jax.ShapeDtypeStruct((s, d), F32), jax.ShapeDtypeStruct((1, d), F32)],
        compiler_params=_cparams("arbitrary"), name=name)(dx1, dh1, x0, g1)


def _tril_mask():
    row = lax.broadcasted_iota(jnp.int32, (CHUNK, CHUNK), 0)
    col = lax.broadcasted_iota(jnp.int32, (CHUNK, CHUNK), 1)
    return row >= col


def _gating_forward(pa, gv, bv, wt, bsf):
    er = lax.erf(pa * RSQRT2)
    za = 0.5 * pa * (1.0 + er)
    u = za[:, :A_WIDTH]
    va = za[:, A_WIDTH:]
    xc = va - jnp.mean(va, axis=-1, keepdims=True)
    rs = lax.rsqrt(jnp.mean(xc * xc, axis=-1, keepdims=True) + EPS)
    vn = xc * rs
    vlb = (vn * gv + bv).astype(BF16)
    sg = jnp.concatenate(
        [_dot(wt[g], vlb[:, g * GROUP_DIM:(g + 1) * GROUP_DIM], NN) for g in range(A_GROUPS)], axis=1) + bsf
    return er, u, rs, vn, vlb, sg


def _masked_ws(ws_ref):
    mask = _tril_mask()
    return [jnp.where(mask, ws_ref[g], 0.0).astype(BF16) for g in range(A_GROUPS)]


def _mixer_a_fwd(proj, gv, bv, ws, bsf, ga, name):
    s = proj.shape[0]

    def body(p_ref, gv_ref, bv_ref, ws_ref, bs_ref, ga_ref, o_ref):
        wt = _masked_ws(ws_ref)
        for ch in range(TR // CHUNK):
            rows = slice(ch * CHUNK, (ch + 1) * CHUNK)
            _, u, _, _, _, sg = _gating_forward(p_ref[rows, :].astype(F32), gv_ref[...], bv_ref[...], wt, bs_ref[...])
            oa = u * sg
            o_ref[rows, :] = (oa * _rsq_mean(oa) * ga_ref[...]).astype(BF16)

    return pl.pallas_call(
        body, grid=(s // TR,),
        in_specs=[_row_spec(2 * A_WIDTH), _vec_spec(A_WIDTH), _vec_spec(A_WIDTH),
                  pl.BlockSpec((A_GROUPS, CHUNK, CHUNK), lambda i: (0, 0, 0)),
                  pl.BlockSpec((CHUNK, A_WIDTH), lambda i: (0, 0)), _vec_spec(A_WIDTH)],
        out_specs=_row_spec(A_WIDTH), out_shape=jax.ShapeDtypeStruct((s, A_WIDTH + B_WIDTH), BF16),
        compiler_params=_cparams("parallel"), name=name)(proj, gv, bv, ws, bsf, ga)


def _mixer_a_bwd(proj, dmixed, gv, bv, ws, bsf, ga, name):
    s = proj.shape[0]
    nsteps = s // TR

    def body(p_ref, dm_ref, gv_ref, bv_ref, ws_ref, bs_ref, ga_ref,
             dp_ref, dga_ref, dgv_ref, dbv_ref, dbs_ref, dws_ref):
        _acc_init([dga_ref, dgv_ref, dbv_ref, dbs_ref, dws_ref])
        mask = _tril_mask()
        wt = _masked_ws(ws_ref)
        gvv = gv_ref[...]
        gav = ga_ref[...]
        for ch in range(TR // CHUNK):
            rows = slice(ch * CHUNK, (ch + 1) * CHUNK)
            pa = p_ref[rows, :].astype(F32)
            er, u, rs, vn, vlb, sg = _gating_forward(pa, gvv, bv_ref[...], wt, bs_ref[...])
            oa = u * sg
            doa, dgt = _rms_bwd(oa, _rsq_mean(oa), gav, dm_ref[rows, :])
            dga_ref[...] += _colsum(dgt)
            du = doa * sg
            dsg = doa * u
            dbs_ref[...] += dsg
            dsgb = dsg.astype(BF16)
            dvl = []
            for g in range(A_GROUPS):
                cols = slice(g * GROUP_DIM, (g + 1) * GROUP_DIM)
                dws_ref[g] += jnp.where(mask, _dot(dsgb[:, cols], vlb[:, cols], NT), 0.0)
                dvl.append(_dot(wt[g], dsgb[:, cols], TN))
            dvl = jnp.concatenate(dvl, axis=1)
            dgv_ref[...] += _colsum(dvl * vn)
            dbv_ref[...] += _colsum(dvl)
            dvn = dvl * gvv
            dva = rs * (dvn - jnp.mean(dvn, axis=-1, keepdims=True)
                        - vn * jnp.mean(dvn * vn, axis=-1, keepdims=True))
            gp = 0.5 * (1.0 + er) + pa * jnp.exp(-0.5 * pa * pa) * INV_SQRT_2PI
            dp_ref[rows, :] = (jnp.concatenate([du, dva], axis=1) * gp).astype(BF16)

        @pl.when(pl.program_id(0) == nsteps - 1)
        def _():
            for g in range(A_GROUPS):
                cols = slice(g * GROUP_DIM, (g + 1) * GROUP_DIM)
                tot = jnp.sum(dbs_ref[:, cols], axis=1, keepdims=True)
                dbs_ref[:, cols] = jnp.broadcast_to(tot, (CHUNK, GROUP_DIM))

    full = lambda *shape: pl.BlockSpec(shape, lambda i: (0,) * len(shape))
    return pl.pallas_call(
        body, grid=(nsteps,),
        in_specs=[_row_spec(2 * A_WIDTH), _row_spec(A_WIDTH), _vec_spec(A_WIDTH), _vec_spec(A_WIDTH),
                  full(A_GROUPS, CHUNK, CHUNK), full(CHUNK, A_WIDTH), _vec_spec(A_WIDTH)],
        out_specs=[_row_spec(2 * A_WIDTH), _vec_spec(A_WIDTH), _vec_spec(A_WIDTH), _vec_spec(A_WIDTH),
                   full(CHUNK, A_WIDTH), full(A_GROUPS, CHUNK, CHUNK)],
        out_shape=[jax.ShapeDtypeStruct((s, IN_COLS), BF16), jax.ShapeDtypeStruct((1, A_WIDTH), F32),
                   jax.ShapeDtypeStruct((1, A_WIDTH), F32), jax.ShapeDtypeStruct((1, A_WIDTH), F32),
                   jax.ShapeDtypeStruct((CHUNK, A_WIDTH), F32),
                   jax.ShapeDtypeStruct((A_GROUPS, CHUNK, CHUNK), F32)],
        compiler_params=_cparams("arbitrary"), name=name)(proj, dmixed, gv, bv, ws, bsf, ga)


def _rope_tables(s):
    half = ROT_DIM // 2
    lane = jnp.arange(LANES) % HEAD_DIM
    inv = ROPE_THETA ** (-(2 * (lane % half)).astype(F32) / ROT_DIM)
    ang = jnp.arange(s, dtype=F32)[:, None] * inv[None, :]
    cos, sin = jnp.cos(ang), jnp.sin(ang)
    c = jnp.where(lane < ROT_DIM, cos, 1.0)
    s1 = jnp.where(lane < half, -sin, 0.0)
    s2 = jnp.where((lane >= half) & (lane < ROT_DIM), sin, 0.0)
    return c, s1, s2


def _lane_blocks(width):
    return [slice(b * LANES, (b + 1) * LANES) for b in range(width // LANES)]


CLASS_DILS = tuple(d for d in DILATIONS if d > 1)


def _class_shape(s, dil, dtype):
    return jax.ShapeDtypeStruct((dil, s // dil, B_WIDTH), dtype)


def _class_spec(dil):
    return pl.BlockSpec((dil, TR // dil, B_WIDTH), lambda i, *_: (0, i, 0))


NBLK = B_WIDTH // LANES
STAGE = pltpu.VMEM((NBLK, TR, LANES), F32)


def _stage_put(stage, value):
    for b, sl in enumerate(_lane_blocks(B_WIDTH)):
        stage[b] = value[:, sl]


def _stage_get(stage):
    return jnp.concatenate([stage[b] for b in range(NBLK)], axis=1)


def _store_classes(stage, dst_ref, dil):
    for b, sl in enumerate(_lane_blocks(B_WIDTH)):
        for r in range(dil):
            dst_ref[r, :, sl] = stage[b, pl.ds(r, TR // dil, stride=dil), :].astype(dst_ref.dtype)


def _load_classes(src_ref, stage, dil):
    for b, sl in enumerate(_lane_blocks(B_WIDTH)):
        for r in range(dil):
            stage[b, pl.ds(r, TR // dil, stride=dil), :] = src_ref[r, :, sl].astype(F32)
    return _stage_get(stage)


def _rope_fwd(proj, tabs, name, gather=None):
    s = proj.shape[0]
    half = ROT_DIM // 2
    scale = HEAD_DIM ** -0.5
    nlay = 1 + len(CLASS_DILS)
    ng = 0 if gather is None else len(gather)

    def body(q_ref, k_ref, v_ref, c_ref, s1_ref, s2_ref, *rest):
        outs, stage = rest[ng:ng + 3 * nlay], rest[2 * ng + 3 * nlay]
        if ng:
            start, relay, finish = _gather_steps(rest[:ng], rest[ng + 3 * nlay:2 * ng + 3 * nlay],
                                                 *rest[2 * ng + 3 * nlay + 1:])
            first, last = _grid_edges((s // TR,))
            pl.when(first)(start)
        c, s1, s2 = c_ref[...], s1_ref[...], s2_ref[...]
        for which, (src, mul) in enumerate(((q_ref, scale), (k_ref, 1.0), (v_ref, None))):
            if mul is None:
                _stage_put(stage, src[...].astype(F32))
            else:
                for b, sl in enumerate(_lane_blocks(B_WIDTH)):
                    a = src[:, sl].astype(F32)
                    r = a * c + pltpu.roll(a, LANES - half, 1) * s1 + pltpu.roll(a, half, 1) * s2
                    stage[b] = r * mul
            dst = outs[which * nlay:(which + 1) * nlay]
            dst[0][...] = _stage_get(stage).astype(BF16)
            for ref, d in zip(dst[1:], CLASS_DILS):
                _store_classes(stage, ref, d)

        if ng:
            @pl.when(last)
            def _():
                relay()
                finish()

    tab = pl.BlockSpec((TR, LANES), lambda i: (i, 0))
    lay_specs = [_row_spec(B_WIDTH)] + [_class_spec(d) for d in CLASS_DILS]
    lay_shapes = [jax.ShapeDtypeStruct((s, B_WIDTH), BF16)] + [_class_shape(s, d, BF16) for d in CLASS_DILS]
    outs = pl.pallas_call(
        body, grid=(s // TR,),
        in_specs=[_row_spec(B_WIDTH, 2), _row_spec(B_WIDTH, 3), _row_spec(B_WIDTH, 4), tab, tab, tab] + [ANY] * ng,
        out_specs=lay_specs * 3 + [ANY] * ng, out_shape=lay_shapes * 3 + _gathered_shapes(gather or []),
        scratch_shapes=[STAGE] + (_gather_sems(ng) if ng else []),
        compiler_params=_cparams("arbitrary" if ng else "parallel"), name=name)(proj, proj, proj, *tabs,
                                                                              *(gather or []))
    q, k, v = (dict(zip(DILATIONS, outs[w * nlay:(w + 1) * nlay])) for w in range(3))
    return q, k, v, list(outs[3 * nlay:])


def _as_classes(t):
    return t if t.ndim == 3 else t[None]


def _head_masks():
    lane = lax.broadcasted_iota(jnp.int32, (1, LANES), 1)
    return lane < HEAD_DIM, lane >= HEAD_DIM


def _stack_heads(t):
    lo, hi = _head_masks()
    zero = jnp.zeros_like(t)
    return jnp.concatenate([jnp.where(lo, t, zero), jnp.where(hi, t, zero)], axis=0)


MAX_SEGMENT_BLOCKS = 8


def _segment_masks(j):
    qi = lax.broadcasted_iota(jnp.int32, (BAND, 2 * BAND), 0)
    kj = lax.broadcasted_iota(jnp.int32, (BAND, 2 * BAND), 1)
    both = (kj >= qi) & (kj <= qi + BAND)
    own = kj[:, :BAND] <= qi[:, :BAND]
    head = both & ((kj >= BAND) | (j > 0))
    return tuple(jnp.concatenate([m, m], axis=0) for m in (own, both, head))


def _block_rows(g):
    return pl.ds(pl.multiple_of(g * BAND, BAND), BAND)


def _key_rows(g):
    return pl.ds(pl.multiple_of((g - 1) * BAND, BAND), 2 * BAND)


def _segments(n):
    nb = n // BAND
    seg = min(nb, MAX_SEGMENT_BLOCKS)
    return seg, nb // seg


def _segment_specs(seg):
    main = pl.BlockSpec((None, seg * BAND, B_WIDTH), lambda r, j: (r, j, 0))
    halo = pl.BlockSpec((None, BAND, B_WIDTH), lambda r, j: (r, jnp.maximum(j * seg - 1, 0), 0))
    return main, halo


def _attn_fwd(q, k, v, name, gather=None):
    dil, n, _ = q.shape
    seg, nseg = _segments(n)
    nh = 2 if nseg > 1 else 0
    ng = 0 if gather is None else len(gather)

    def body(*refs):
        q_ref, k_ref, v_ref = refs[:3]
        halos = refs[3:3 + nh]
        o_ref, l_ref = refs[3 + nh + ng:5 + nh + ng]
        if ng:
            start, relay, finish = _gather_steps(refs[3 + nh:3 + nh + ng], refs[5 + nh + ng:5 + nh + 2 * ng],
                                                 *refs[5 + nh + 2 * ng:])
            first, last = _grid_edges((dil, nseg))
            pl.when(first)(start)
        own, both, head = _segment_masks(pl.program_id(1))
        lo, _ = _head_masks()

        def block(rows, keys_of, valid):
            for sl in _lane_blocks(B_WIDTH):
                kk, vv = keys_of(sl)
                sc = jnp.where(valid, _dot(_stack_heads(q_ref[rows, sl]), kk, NT), NEG_INF)
                mx = jnp.max(sc, axis=1, keepdims=True)
                p = jnp.exp(sc - mx)
                den = jnp.sum(p, axis=1, keepdims=True)
                out = _dot(p.astype(BF16), vv, NN) / den
                lse = mx + jnp.log(den)
                o_ref[rows, sl] = jnp.where(lo, out[:BAND], out[BAND:]).astype(BF16)
                l_ref[rows, sl] = jnp.where(lo, lse[:BAND], lse[BAND:])

        if nh:
            block(_block_rows(0), lambda sl: (jnp.concatenate([halos[0][:, sl], k_ref[0:BAND, sl]], axis=0),
                                              jnp.concatenate([halos[1][:, sl], v_ref[0:BAND, sl]], axis=0)), head)
        else:
            block(_block_rows(0), lambda sl: (k_ref[0:BAND, sl], v_ref[0:BAND, sl]), own)

        @pl.loop(1, seg)
        def _(g):
            block(_block_rows(g), lambda sl: (k_ref[_key_rows(g), sl], v_ref[_key_rows(g), sl]), both)

        if ng:
            @pl.when(last)
            def _():
                relay()
                finish()

    main, halo = _segment_specs(seg)
    res = pl.pallas_call(
        body, grid=(dil, nseg), in_specs=[main] * 3 + [halo] * nh + [ANY] * ng, out_specs=[main, main] + [ANY] * ng,
        out_shape=[jax.ShapeDtypeStruct((dil, n, B_WIDTH), BF16), jax.ShapeDtypeStruct((dil, n, B_WIDTH), F32)]
        + _gathered_shapes(gather or []),
        scratch_shapes=_gather_sems(ng) if ng else [],
        compiler_params=_cparams(*(["arbitrary"] * 2 if ng else ["parallel"] * 2)), name=name)(
            q, k, v, *([k, v] if nh else []), *(gather or []))
    return res[0], res[1], list(res[2:])


def _attn_bwd(q, k, v, do, lse, delta, name, scatter=None):
    dil, n, _ = q.shape
    seg, nseg = _segments(n)
    nh = 2 if nseg > 1 else 0
    ns = 0 if scatter is None else len(scatter[0])

    def body(*refs):
        q_ref, k_ref, v_ref, do_ref, lse_ref, dl_ref = refs[:6]
        halos = refs[6:6 + nh]
        dq_ref, dk_ref, dv_ref = refs[6 + nh + ns:9 + nh + ns]
        halo_out = refs[9 + nh + ns:9 + 2 * nh + ns]
        ck_ref, cv_ref = refs[9 + 2 * nh + 2 * ns:11 + 2 * nh + 2 * ns]
        if ns:
            start, finish = _scatter_steps(refs[6 + nh:6 + nh + ns], refs[9 + 2 * nh + ns:9 + 2 * nh + 2 * ns],
                                           *refs[11 + 2 * nh + 2 * ns:], scatter[1])
            first, last = _grid_edges((dil, nseg))
            pl.when(first)(start)
        own, both, head = _segment_masks(pl.program_id(1))
        lo, _ = _head_masks()
        lane = lax.broadcasted_iota(jnp.int32, (1, LANES), 1)

        def per_head(t):
            return jnp.concatenate(
                [jnp.sum(jnp.where(lane == first, t, 0.0), axis=1, keepdims=True) for first in (0, HEAD_DIM)], axis=0)

        def grads(rows, kk, vv, valid, sl):
            q2 = _stack_heads(q_ref[rows, sl])
            do2 = _stack_heads(do_ref[rows, sl])
            p = jnp.where(valid, jnp.exp(_dot(q2, kk, NT) - per_head(lse_ref[rows, sl])), 0.0)
            ds = (p * (_dot(do2, vv, NT) - per_head(dl_ref[rows, sl]))).astype(BF16)
            dq = _dot(ds, kk, NN)
            dq_ref[rows, sl] = jnp.where(lo, dq[:BAND], dq[BAND:]).astype(BF16)
            return _dot(ds, q2, TN), _dot(p.astype(BF16), do2, TN)

        for sl in _lane_blocks(B_WIDTH):
            if nh:
                dkk, dvv = grads(_block_rows(0), jnp.concatenate([halos[0][:, sl], k_ref[0:BAND, sl]], axis=0),
                                 jnp.concatenate([halos[1][:, sl], v_ref[0:BAND, sl]], axis=0), head, sl)
                halo_out[0][:, sl], halo_out[1][:, sl] = dkk[:BAND], dvv[:BAND]
                ck_ref[:, sl], cv_ref[:, sl] = dkk[BAND:], dvv[BAND:]
            else:
                ck_ref[:, sl], cv_ref[:, sl] = grads(_block_rows(0), k_ref[0:BAND, sl], v_ref[0:BAND, sl], own, sl)

        @pl.loop(1, seg)
        def _(g):
            before = _block_rows(g - 1)
            for sl in _lane_blocks(B_WIDTH):
                dkk, dvv = grads(_block_rows(g), k_ref[_key_rows(g), sl], v_ref[_key_rows(g), sl], both, sl)
                dk_ref[before, sl] = (ck_ref[:, sl] + dkk[:BAND]).astype(BF16)
                dv_ref[before, sl] = (cv_ref[:, sl] + dvv[:BAND]).astype(BF16)
                ck_ref[:, sl] = dkk[BAND:]
                cv_ref[:, sl] = dvv[BAND:]

        final = pl.ds((seg - 1) * BAND, BAND)
        dk_ref[final, :] = ck_ref[...].astype(BF16)
        dv_ref[final, :] = cv_ref[...].astype(BF16)

        if ns:
            pl.when(last)(finish)

    main, halo = _segment_specs(seg)
    shape = jax.ShapeDtypeStruct((dil, n, B_WIDTH), BF16)
    halo_shape = jax.ShapeDtypeStruct((dil, nseg, BAND, B_WIDTH), F32)
    halo_spec = pl.BlockSpec((None, None, BAND, B_WIDTH), lambda r, j: (r, j, 0, 0))
    res = pl.pallas_call(
        body, grid=(dil, nseg), in_specs=[main] * 6 + [halo] * nh + [ANY] * ns,
        out_specs=[main] * 3 + [halo_spec] * nh + [ANY] * ns,
        out_shape=[shape] * 3 + [halo_shape] * nh + (_scattered_shapes(scatter[1]) if ns else []),
        scratch_shapes=[pltpu.VMEM((BAND, B_WIDTH), F32)] * 2 + (_scatter_sems(ns) if ns else []),
        compiler_params=_cparams(*(["arbitrary"] * 2 if ns else ["parallel"] * 2)), name=name)(
            q, k, v, do, lse, delta, *([k, v] if nh else []), *(scatter[0] if ns else []))
    return res[0], res[1], res[2], (tuple(res[3:3 + nh]) if nh else None), list(res[3 + nh:])


def _attn_combine(outs, lses, gb, mixed, name, gather=None):
    s = mixed.shape[0]
    npat = len(DILATIONS)
    w = B_WIDTH
    ng = 0 if gather is None else len(gather)

    def body(*refs):
        o_refs, l_refs = refs[:npat], refs[npat:2 * npat]
        g_ref = refs[2 * npat]
        ob_ref = refs[2 * npat + 2 + ng]
        lse_refs = refs[2 * npat + 3 + ng:3 * npat + 3 + ng]
        mb_ref = refs[3 * npat + 3 + ng]
        stage = refs[3 * npat + 4 + 2 * ng]
        if ng:
            start, relay, finish = _gather_steps(refs[2 * npat + 2:2 * npat + 2 + ng],
                                                 refs[3 * npat + 4 + ng:3 * npat + 4 + 2 * ng],
                                                 *refs[3 * npat + 5 + 2 * ng:])
            first, last = _grid_edges((s // TR,))
            pl.when(first)(start)
        os_ = [o_refs[0][...].astype(F32)] + [_load_classes(r, stage, d) for r, d in zip(o_refs[1:], CLASS_DILS)]
        ls = [l_refs[0][...]] + [_load_classes(r, stage, d) for r, d in zip(l_refs[1:], CLASS_DILS)]
        mx = functools.reduce(jnp.maximum, ls)
        ws = [jnp.exp(l - mx) for l in ls]
        tot = functools.reduce(lambda a, b: a + b, ws)
        ob = functools.reduce(lambda a, b: a + b, [wt / tot * o for wt, o in zip(ws, os_)])
        ob_ref[...] = ob
        lse = mx + jnp.log(tot)
        _stage_put(stage, lse)
        lse_refs[0][...] = lse
        for ref, d in zip(lse_refs[1:], CLASS_DILS):
            _store_classes(stage, ref, d)
        mb_ref[...] = (ob * _rsq_mean(ob) * g_ref[...]).astype(BF16)

        if ng:
            @pl.when(last)
            def _():
                relay()
                finish()

    lay_specs = [_row_spec(w)] + [_class_spec(d) for d in CLASS_DILS]
    res = pl.pallas_call(
        body, grid=(s // TR,), in_specs=lay_specs * 2 + [_vec_spec(w), ANY] + [ANY] * ng,
        out_specs=[_row_spec(w)] + lay_specs + [_row_spec(w, 1)] + [ANY] * ng,
        out_shape=[jax.ShapeDtypeStruct((s, w), F32), jax.ShapeDtypeStruct((s, w), F32)]
        + [_class_shape(s, d, F32) for d in CLASS_DILS] + [jax.ShapeDtypeStruct(mixed.shape, mixed.dtype)]
        + _gathered_shapes(gather or []),
        scratch_shapes=[STAGE] + (_gather_sems(ng) if ng else []), input_output_aliases={2 * npat + 1: npat + 1},
        compiler_params=_cparams("arbitrary" if ng else "parallel"), name=name)(*outs, *lses, gb, mixed,
                                                                              *(gather or []))
    return res[0], dict(zip(DILATIONS, res[1:npat + 1])), res[npat + 1], list(res[npat + 2:])


def _attn_bwd_prep(dmixed, ob, gb, name):
    s = ob.shape[0]
    w = B_WIDTH
    nlay = len(DILATIONS)

    def body(dm_ref, ob_ref, g_ref, *rest):
        do_refs, dl_refs = rest[:nlay], rest[nlay:2 * nlay]
        dg_ref, stage = rest[2 * nlay:]
        _acc_init([dg_ref])
        ob = ob_ref[...]
        dob, dgt = _rms_bwd(ob, _rsq_mean(ob), g_ref[...], dm_ref[...])
        dg_ref[...] += _colsum(dgt)
        _stage_put(stage, dob)
        do_refs[0][...] = dob.astype(BF16)
        for ref, d in zip(do_refs[1:], CLASS_DILS):
            _store_classes(stage, ref, d)
        lo, hi = _head_masks()
        t = dob * ob
        for b, sl in enumerate(_lane_blocks(w)):
            tb = t[:, sl]
            s0 = jnp.sum(jnp.where(lo, tb, 0.0), axis=1, keepdims=True)
            s1 = jnp.sum(jnp.where(hi, tb, 0.0), axis=1, keepdims=True)
            stage[b] = jnp.where(lo, s0, s1)
        dl_refs[0][...] = _stage_get(stage)
        for ref, d in zip(dl_refs[1:], CLASS_DILS):
            _store_classes(stage, ref, d)

    lay_specs = [_row_spec(w)] + [_class_spec(d) for d in CLASS_DILS]
    shapes = lambda dt: [jax.ShapeDtypeStruct((s, w), dt)] + [_class_shape(s, d, dt) for d in CLASS_DILS]
    res = pl.pallas_call(
        body, grid=(s // TR,), in_specs=[_row_spec(w, 1), _row_spec(w), _vec_spec(w)],
        out_specs=lay_specs * 2 + [_vec_spec(w)],
        out_shape=shapes(BF16) + shapes(F32) + [jax.ShapeDtypeStruct((1, w), F32)],
        scratch_shapes=[STAGE],
        compiler_params=_cparams("arbitrary"), name=name)(dmixed, ob, gb)
    return dict(zip(DILATIONS, res[:nlay])), dict(zip(DILATIONS, res[nlay:2 * nlay])), res[2 * nlay]


def _rope_bwd(dqs, dks, dvs, halos, tabs, dproj, name):
    s = dproj.shape[0]
    half = ROT_DIM // 2
    scale = HEAD_DIM ** -0.5
    npat = len(DILATIONS)
    w = B_WIDTH
    nseg = halos[0].shape[0]
    per = s // nseg // TR

    def body(*refs):
        groups = [refs[g * npat:(g + 1) * npat] for g in range(3)]
        halo_refs = (None,) + tuple(refs[3 * npat:3 * npat + 2])
        c_ref, s1_ref, s2_ref, _, o_ref, stage = refs[3 * npat + 2:]
        i = pl.program_id(0)
        at_edge = ((i + 1) % per == 0) & ((i + 1) // per < nseg)

        def total(rs, halo_ref=None):
            acc = rs[0][...].astype(F32)
            if halo_ref is not None:
                edge = jnp.concatenate([jnp.zeros((TR - BAND, w), F32), halo_ref[...]], axis=0)
                acc = acc + jnp.where(at_edge, edge, 0.0)
            for ref, d in zip(rs[1:], CLASS_DILS):
                acc = acc + _load_classes(ref, stage, d)
            return acc

        def unrope(g):
            c, s1, s2 = c_ref[...], s1_ref[...], s2_ref[...]
            for sl in _lane_blocks(w):
                gb = g[:, sl]
                o = gb * c + pltpu.roll(gb * s1, half, 1) + pltpu.roll(gb * s2, LANES - half, 1)
                o_ref[:, sl] = o.astype(BF16)

        which = pl.program_id(1)

        @pl.when(which == 0)
        def _():
            unrope(total(groups[0]) * scale)

        @pl.when(which == 1)
        def _():
            unrope(total(groups[1], halo_refs[1]))

        @pl.when(which == 2)
        def _():
            o_ref[...] = total(groups[2], halo_refs[2]).astype(BF16)

    tab = pl.BlockSpec((TR, LANES), lambda i, j: (i, 0))
    nat = pl.BlockSpec((TR, w), lambda i, j: (i, 0))
    lay_specs = [nat] + [_class_spec(d) for d in CLASS_DILS]
    edge_spec = pl.BlockSpec((None, BAND, w), lambda i, j: (jnp.minimum((i + 1) // per, nseg - 1), 0, 0))
    first_col = 2 * A_WIDTH // w
    return pl.pallas_call(
        body, grid=(s // TR, 3), in_specs=lay_specs * 3 + [edge_spec] * 2 + [tab] * 3 + [ANY],
        out_specs=pl.BlockSpec((TR, w), lambda i, j: (i, first_col + j)),
        out_shape=jax.ShapeDtypeStruct(dproj.shape, dproj.dtype), scratch_shapes=[STAGE],
        input_output_aliases={3 * npat + 5: 0},
        compiler_params=_cparams("parallel", "arbitrary"), name=name)(*dqs, *dks, *dvs, *halos, *tabs, dproj)


TK = 512
HALO = 16


def _row_of(v, r):
    rows = lax.broadcasted_iota(jnp.int32, (v.shape[0], 1), 0)
    return jnp.sum(jnp.where(rows == r, v, 0.0), axis=0, keepdims=True)


def _taps_before(x, halo):
    row = lax.broadcasted_iota(jnp.int32, (x.shape[0], 1), 0)
    m1 = jnp.where(row == 0, _row_of(halo, HALO - 1), pltpu.roll(x, 1, 0))
    m2 = jnp.where(row == 0, _row_of(halo, HALO - 2), jnp.where(row == 1, _row_of(halo, HALO - 1), pltpu.roll(x, 2, 0)))
    return m2, m1, x


def _taps_after(x, halo):
    rows = x.shape[0]
    row = lax.broadcasted_iota(jnp.int32, (rows, 1), 0)
    p1 = jnp.where(row == rows - 1, _row_of(halo, 0), pltpu.roll(x, rows - 1, 0))
    p2 = jnp.where(row == rows - 2, _row_of(halo, 0), jnp.where(row == rows - 1, _row_of(halo, 1), pltpu.roll(x, rows - 2, 0)))
    return p1, p2


def _conv_value(taps, cw_ref, cb_ref, h):
    return cb_ref[h] + cw_ref[h, 0:1, :] * taps[0] + cw_ref[h, 1:2, :] * taps[1] + cw_ref[h, 2:3, :] * taps[2]


def _ffn_weight_specs(ncol):
    per_up = (2 * D_FF // N_CHIPS) // TK
    per_dn = (D_FF // N_CHIPS) // TK
    wg = pl.BlockSpec((None, None, D_MODEL, TK), lambda i, j: (j // per_up, 0, 0, j % per_up))
    wv = pl.BlockSpec((None, None, D_MODEL, TK), lambda i, j: ((j + ncol) // per_up, 0, 0, (j + ncol) % per_up))
    wd = pl.BlockSpec((None, None, TK, D_MODEL), lambda i, j: (j // per_dn, 0, j % per_dn, 0))
    cw = pl.BlockSpec((2, 3, TK), lambda i, j: (0, 0, j))
    cb = pl.BlockSpec((2, 1, TK), lambda i, j: (0, 0, j))
    return wg, wv, wd, cw, cb


def _ffn_forward(h2, w_up, w_down, cw3, cb3, name, gather=None, post=None):
    s = h2.shape[0]
    nm, ncol = s // TM, D_FF // TK
    ng = 0 if gather is None else len(gather)
    npost = 0 if post is None else 3
    nout = 4 + (2 if post else 0)

    def body(*refs):
        h_ref, wg_ref, wv_ref, wd_ref, cw_ref, cb_ref = refs[:6]
        post_in = refs[6:6 + npost]
        g_in = refs[6 + npost:6 + npost + ng]
        outs = refs[6 + npost + ng:6 + npost + ng + nout]
        y_ref, up_ref, cv_ref, f_ref = outs[:4]
        g_out = refs[6 + npost + ng + nout:6 + npost + 2 * ng + nout]
        carry, acc = refs[6 + npost + 2 * ng + nout:8 + npost + 2 * ng + nout]
        i, j = pl.program_id(0), pl.program_id(1)
        if ng:
            start, relay, finish = _gather_steps(g_in, g_out, *refs[8 + npost + 2 * ng + nout:])
            pl.when((i == 0) & (j == 0))(start)
            pl.when((i == nm - 1) & (j == 0))(relay)

        @pl.when((i == 0) & (j == 0))
        def _():
            carry[...] = jnp.zeros_like(carry)

        h = h_ref[...]
        conv = []
        for hh, w_ref in ((0, wg_ref), (1, wv_ref)):
            up = _dot(h, w_ref[...], NN).astype(BF16)
            up_ref[hh] = up
            x = up.astype(F32)
            conv.append(_conv_value(_taps_before(x, carry[j, hh]), cw_ref, cb_ref, hh))
            cv_ref[hh] = conv[hh].astype(BF16)
            carry[j, hh] = x[TM - HALO:, :]
        y = (_gelu_tanh(conv[0])[0] * conv[1]).astype(BF16)
        y_ref[...] = y
        part = _dot(y, wd_ref[...], NN)

        @pl.when(j == 0)
        def _():
            acc[...] = part

        @pl.when(j > 0)
        def _():
            acc[...] += part

        @pl.when(j == ncol - 1)
        def _():
            f = acc[...]
            f_ref[...] = f
            if post:
                x1_ref, gp_ref, gn_ref = post_in
                x2 = x1_ref[...] + f * _rsq_mean(f) * gp_ref[...]
                outs[4][...] = x2
                outs[5][...] = (x2 * _rsq_mean(x2) * gn_ref[...]).astype(BF16)

        if ng:
            pl.when((i == nm - 1) & (j == ncol - 1))(finish)

    wg, wv, wd, cw, cb = _ffn_weight_specs(ncol)
    row = pl.BlockSpec((TM, D_MODEL), lambda i, j: (i, 0))
    vec = pl.BlockSpec((1, D_MODEL), lambda i, j: (0, 0))
    res = pl.pallas_call(
        body, grid=(nm, ncol),
        in_specs=[row, wg, wv, wd, cw, cb] + ([row, vec, vec] if post else []) + [ANY] * ng,
        out_specs=[pl.BlockSpec((TM, TK), lambda i, j: (i, j)), pl.BlockSpec((2, TM, TK), lambda i, j: (0, i, j)),
                   pl.BlockSpec((2, TM, TK), lambda i, j: (0, i, j)), row] + ([row, row] if post else [])
        + [ANY] * ng,
        out_shape=[jax.ShapeDtypeStruct((s, D_FF), BF16), jax.ShapeDtypeStruct((2, s, D_FF), BF16),
                   jax.ShapeDtypeStruct((2, s, D_FF), BF16), jax.ShapeDtypeStruct((s, D_MODEL), F32)]
        + ([jax.ShapeDtypeStruct((s, D_MODEL), F32), jax.ShapeDtypeStruct((s, D_MODEL), BF16)] if post else [])
        + _gathered_shapes(gather or []),
        scratch_shapes=[pltpu.VMEM((ncol, 2, HALO, TK), F32), pltpu.VMEM((TM, D_MODEL), F32)]
        + (_gather_sems(ng) if ng else []),
        compiler_params=_cparams("arbitrary", "arbitrary"), name=name)(h2, w_up, w_up, w_down, cw3, cb3,
                                                                      *(post or []), *(gather or []))
    return res[:nout], list(res[nout:])


def _ffn_backward(df, w_up, w_down, up3, cv3, cw3, name, scatter=None):
    s = df.shape[0]
    nm, ncol = s // TM, D_FF // TK
    ns = 0 if scatter is None else len(scatter[0])

    def body(*refs):
        df_ref, wg_ref, wv_ref, wd_ref, cw_ref, up_ref, cv_ref = refs[:7]
        s_in = refs[7:7 + ns]
        dup_ref, dh_ref, sums_ref = refs[7 + ns:10 + ns]
        s_out = refs[10 + ns:10 + 2 * ns]
        carry, acc = refs[10 + 2 * ns:12 + 2 * ns]
        i, j = pl.program_id(0), pl.program_id(1)
        if ns:
            start, finish = _scatter_steps(s_in, s_out, *refs[12 + 2 * ns:], scatter[1])
            pl.when((i == 0) & (j == 0))(start)

        @pl.when((i == 0) & (j == 0))
        def _():
            carry[...] = jnp.zeros_like(carry)
            sums_ref[...] = jnp.zeros_like(sums_ref)

        dy = _dot(df_ref[...], wd_ref[...], NT)
        act, grad = _gelu_tanh(cv_ref[0].astype(F32))
        dcs = (dy * cv_ref[1].astype(F32) * grad, dy * act)
        row = lax.broadcasted_iota(jnp.int32, (8, 1), 0)
        part = None
        for hh, w_ref in ((0, wg_ref), (1, wv_ref)):
            dc = dcs[hh]
            x = up_ref[hh].astype(F32)
            after1, after2 = _taps_after(dc, carry[j, hh])
            upd = jnp.zeros((8, TK), F32)
            for ridx, sm in enumerate((_colsum(after2 * x), _colsum(after1 * x), _colsum(dc * x), _colsum(dc))):
                upd = jnp.where(row == ridx, sm, upd)
            sums_ref[j, hh] += upd
            dup = (cw_ref[hh, 2:3, :] * dc + cw_ref[hh, 1:2, :] * after1 + cw_ref[hh, 0:1, :] * after2).astype(BF16)
            carry[j, hh] = dc[:HALO, :]
            dup_ref[hh] = dup
            d = _dot(dup, w_ref[...], NT)
            part = d if part is None else part + d

        @pl.when(j == 0)
        def _():
            acc[...] = part

        @pl.when(j > 0)
        def _():
            acc[...] += part

        @pl.when(j == ncol - 1)
        def _():
            dh_ref[...] = acc[...]

        if ns:
            pl.when((i == nm - 1) & (j == ncol - 1))(finish)

    wg, wv, wd, cw, _ = _ffn_weight_specs(ncol)
    rev = lambda i: nm - 1 - i
    res = pl.pallas_call(
        body, grid=(nm, ncol),
        in_specs=[pl.BlockSpec((TM, D_MODEL), lambda i, j: (rev(i), 0)), wg, wv, wd, cw,
                  pl.BlockSpec((2, TM, TK), lambda i, j: (0, rev(i), j)),
                  pl.BlockSpec((2, TM, TK), lambda i, j: (0, rev(i), j))] + [ANY] * ns,
        out_specs=[pl.BlockSpec((2, TM, TK), lambda i, j: (0, rev(i), j)),
                   pl.BlockSpec((TM, D_MODEL), lambda i, j: (rev(i), 0)),
                   pl.BlockSpec((ncol, 2, 8, TK), lambda i, j: (0, 0, 0, 0))] + [ANY] * ns,
        out_shape=[jax.ShapeDtypeStruct((2, s, D_FF), BF16), jax.ShapeDtypeStruct((s, D_MODEL), F32),
                   jax.ShapeDtypeStruct((ncol, 2, 8, TK), F32)] + (_scattered_shapes(scatter[1]) if ns else []),
        scratch_shapes=[pltpu.VMEM((ncol, 2, HALO, TK), F32), pltpu.VMEM((TM, D_MODEL), F32)]
        + (_scatter_sems(ns) if ns else []),
        compiler_params=_cparams("arbitrary", "arbitrary"), name=name)(df, w_up, w_up, w_down, cw3, up3, cv3,
                                                                      *(scatter[0] if ns else []))
    return res[:3], list(res[3:])


def _wspec(rows, cols, index_map):
    return pl.BlockSpec((None, None, rows, cols), index_map)


def _layer_forward(l, x0, h1, p, wg, tabs, gather=None, late=None, g_next=None):
    s = x0.shape[0]
    nm = s // TMM
    tag = f"_l{l}"
    riders = dict.fromkeys(DILATIONS)
    proj_rider = rope_rider = combine_rider = None
    if late is not None:
        cols = lambda t, parts: [t[:, i * t.shape[1] // parts:(i + 1) * t.shape[1] // parts] for i in range(parts)]
        (down_a, down_b), up_q = cols(late["w_down"], 2), cols(late["w_up"], 4)
        proj_rider, rope_rider, combine_rider = [late["w_out"], down_a], [up_q[2]], [up_q[3]]
        riders = dict(zip(DILATIONS, ([down_b], [up_q[0]], [up_q[1]])))
    proj = _matmul(
        h1, wg["w_in"], grid=(nm, N_CHIPS), a_spec=pl.BlockSpec((TMM, D_MODEL), lambda i, j: (i, 0)),
        b_spec=_wspec(D_MODEL, IN_COLS // N_CHIPS, lambda i, j: (j, 0, 0, 0)),
        o_spec=pl.BlockSpec((TMM, IN_COLS // N_CHIPS), lambda i, j: (i, j)), o_shape=(s, IN_COLS), o_dtype=BF16,
        dims=NN, nk=1, kaxis=None, acc_shape=None, name="proj" + tag, gather=proj_rider)
    if late is not None:
        proj, (w_out_all4, down_a) = proj
    ma = _mixer_a_fwd(proj, p["v_norm_g"], p["v_norm_b"], p["w_spatial"], p["bs_full"], p["out_norm_a"],
                      "mixer_a_fwd" + tag)
    q, k, v, rope_landed = _rope_fwd(proj, tabs, "rope_fwd" + tag, rope_rider)
    outs, lses, landed = zip(*[
        _attn_fwd(_as_classes(q[d]), _as_classes(k[d]), _as_classes(v[d]), f"attn_fwd_d{d}" + tag, riders[d])
        for d in DILATIONS])
    outs = [o.reshape(s, B_WIDTH) if d == 1 else o for o, d in zip(outs, DILATIONS)]
    lses = [t.reshape(s, B_WIDTH) if d == 1 else t for t, d in zip(lses, DILATIONS)]
    ob, lse, mixed, combine_landed = _attn_combine(outs, lses, p["out_norm_b"], ma, "attn_combine" + tag,
                                                   combine_rider)
    if late is not None:
        wg = dict(wg, w_out=w_out_all4, w_down=jnp.concatenate([down_a, landed[0][0]], axis=-1),
                  w_up=jnp.concatenate([landed[1][0], landed[2][0], rope_landed[0], combine_landed[0]], axis=-1))
    y1, x1, h2 = _mix_out_norm(mixed, wg["w_out"], x0, p["post_mix_norm"], p["pre_ffn_norm"], "mix_out" + tag)
    post = None if g_next is None else (x1, p["post_ffn_norm"], g_next)
    (y, up3, cv3, f, *after), gathered = _ffn_forward(h2, wg["w_up"], wg["w_down"], p["cw3"], p["cb3"],
                                                      "ffn_fwd" + tag, gather, post)
    saved = dict(x0=x0, h1=h1, proj=proj, q=q, k=k, v=v, ob=ob, lse=lse, mixed=mixed, y1=y1, x1=x1, h2=h2,
                 up3=up3, cv3=cv3, y=y, f=f)
    if after:
        saved.update(x2=after[0], h_next=after[1])
    return saved, gathered, wg


def _layer_backward(l, dx2, df, sv, p, wg, tabs, pos, scatter=None, hide=False):
    s = dx2.shape[0]
    nm = s // TMM
    tag = f"_l{l}"
    g = {}
    (dup3, dh2, conv_sums), scattered = _ffn_backward(df, wg["w_up"], wg["w_down"], sv["up3"], sv["cv3"], p["cw3"],
                                                      "ffn_bwd" + tag, scatter)
    sums = conv_sums.transpose(1, 2, 0, 3).reshape(2, 8, D_FF)
    g["conv_w"] = jnp.concatenate([sums[0, :3], sums[1, :3]], axis=1)
    g["conv_b"] = jnp.concatenate([sums[0, 3:4], sums[1, 3:4]], axis=1)
    tn = 1024
    done = {}
    gw_down = _matmul(
        sv["y"], df, grid=(D_FF // tn,), a_spec=pl.BlockSpec((s, tn), lambda k: (0, k)),
        b_spec=pl.BlockSpec((s, D_MODEL), lambda k: (0, 0)),
        o_spec=pl.BlockSpec((2, tn, D_MODEL // 2), lambda k: (0, k, 0)),
        o_shape=(2, D_FF, D_MODEL // 2), o_dtype=BF16,
        dims=TN, nk=1, kaxis=None, acc_shape=None, name="w_down_grad" + tag, halves=True)
    down_sums = _chip_sums(l, dict(w_down=gw_down), pos, ("w_down",)) if hide else None
    gw_up = _matmul(
        sv["h2"], dup3, grid=(2 * D_FF // tn,), a_spec=pl.BlockSpec((s, D_MODEL), lambda n: (0, 0)),
        b_spec=pl.BlockSpec((None, s, tn), lambda n: (n // (D_FF // tn), 0, n % (D_FF // tn))),
        o_spec=pl.BlockSpec((None, D_MODEL, tn), lambda n: (n // 2, 0, n % 2)),
        o_shape=(N_CHIPS, D_MODEL, 2 * D_FF // N_CHIPS), o_dtype=BF16,
        dims=TN, nk=1, kaxis=None, acc_shape=None, name="w_up_grad" + tag,
        scatter=(down_sums, ("w_down",)) if hide else None)
    up_sums = None
    if hide:
        gw_up, received = gw_up
        done[("w_down",)] = (down_sums, received)
        up_sums = _chip_sums(l, dict(w_up=gw_up), pos, ("w_up",))
    dx1, dy1, g["pre_ffn_norm"], g["post_mix_norm"] = _norm_bwd_mid(
        dx2, dh2, sv["x1"], p["pre_ffn_norm"], sv["y1"], p["post_mix_norm"], "norm_bwd_mid" + tag)
    w_out_all = pl.BlockSpec((N_CHIPS, None, D_MODEL // N_CHIPS, D_MODEL), lambda i: (0, 0, 0, 0))
    dmixed = _matmul(
        dy1, wg["w_out"], grid=(nm,), a_spec=pl.BlockSpec((TMM, D_MODEL), lambda i: (i, 0)), b_spec=w_out_all,
        o_spec=pl.BlockSpec((TMM, D_MODEL), lambda i: (i, 0)), o_shape=(s, D_MODEL), o_dtype=F32,
        dims=NT, nk=1, kaxis=None, acc_shape=None, name="mix_out_bwd" + tag, b_2d=(D_MODEL, D_MODEL))
    gw_out = _matmul(
        sv["mixed"], dy1, grid=(1,), a_spec=pl.BlockSpec((s, D_MODEL), lambda m: (0, 0)),
        b_spec=pl.BlockSpec((s, D_MODEL), lambda m: (0, 0)),
        o_spec=pl.BlockSpec((2, D_MODEL, D_MODEL // 2), lambda m: (0, 0, 0)),
        o_shape=(2, D_MODEL, D_MODEL // 2), o_dtype=BF16,
        dims=TN, nk=1, kaxis=None, acc_shape=None, name="w_out_grad" + tag, halves=True)
    dpa, g["out_norm_a"], g["v_norm_g"], g["v_norm_b"], dbs, g["w_spatial"] = _mixer_a_bwd(
        sv["proj"], dmixed, p["v_norm_g"], p["v_norm_b"], p["w_spatial"], p["bs_full"], p["out_norm_a"],
        "mixer_a_bwd" + tag)
    g["b_spatial"] = dbs[:, ::GROUP_DIM].T
    dob, delta, g["out_norm_b"] = _attn_bwd_prep(dmixed, sv["ob"], p["out_norm_b"], "attn_bwd_prep" + tag)
    halves = dict(zip(DILATIONS, ("w_up:0", "w_up:1"))) if hide else {}
    dqs, dks, dvs, edges, received = zip(*[
        _attn_bwd(*(_as_classes(t[d]) for t in (sv["q"], sv["k"], sv["v"], dob, sv["lse"], delta)),
                  f"attn_bwd_d{d}" + tag, (up_sums, (halves[d],)) if d in halves else None)
        for d in DILATIONS])
    if hide:
        done[("w_up",)] = (up_sums, [jnp.concatenate([received[0][0], received[1][0]], axis=-1)])
    nat = lambda ts: [t.reshape(s, B_WIDTH) if d == 1 else t for t, d in zip(ts, DILATIONS)]
    halos = [t[0] for t in edges[0]]
    dproj = _rope_bwd(nat(dqs), nat(dks), nat(dvs), halos, tabs, dpa, "rope_bwd" + tag)
    wcol = IN_COLS // N_CHIPS
    dh1 = _proj_bwd(dproj, wg["w_in"], "proj_bwd" + tag)
    gw_in = _matmul(
        sv["h1"], dproj, grid=(N_CHIPS,), a_spec=pl.BlockSpec((s, D_MODEL), lambda n: (0, 0)),
        b_spec=pl.BlockSpec((s, wcol), lambda n: (0, n)),
        o_spec=pl.BlockSpec((None, D_MODEL, wcol), lambda n: (n, 0, 0)),
        o_shape=(N_CHIPS, D_MODEL, wcol), o_dtype=BF16,
        dims=TN, nk=1, kaxis=None, acc_shape=None, name="w_in_grad" + tag)
    big = dict(w_in=gw_in, w_out=gw_out) if hide else dict(w_in=gw_in, w_up=gw_up, w_out=gw_out, w_down=gw_down)
    return dx1, dh1, big, g, scattered, done


SMALL = ("pre_mix_norm", "v_norm_g", "v_norm_b", "w_spatial", "b_spatial", "out_norm_a", "out_norm_b",
         "post_mix_norm", "pre_ffn_norm", "conv_b", "post_ffn_norm")
BIG = ("w_in", "w_out", "w_up", "w_down")
DEPTH = 2


def _layer_params(l, small, conv_w_full):
    p = {n: small[n][l].reshape(1, -1) for n in SMALL if n not in ("w_spatial", "b_spatial")}
    p["w_spatial"] = small["w_spatial"][l]
    p["bs_full"] = jnp.repeat(small["b_spatial"][l].T, GROUP_DIM, axis=1)
    p["cw3"] = conv_w_full[l].reshape(3, 2, D_FF).transpose(1, 0, 2)
    p["cb3"] = small["conv_b"][l].reshape(2, 1, D_FF)
    return p


def _mesh_pos():
    return lax.axis_index("x"), lax.axis_index("y"), lax.axis_index("c")


def _other_chips(x, y):
    return [(1 - x, y), (x, 1 - y), (1 - x, 1 - y)]


def _gathered_shapes(blocks):
    return [jax.ShapeDtypeStruct((N_CHIPS, 1) + a.shape, a.dtype) for a in blocks]


def _gather_sems(nw):
    n = 2 * nw * (N_CHIPS - 1) + nw
    return [pltpu.SemaphoreType.DMA((n,)), pltpu.SemaphoreType.DMA((n,))]


def _gather_steps(ins, outs, send, recv):
    nw, nrel = len(ins), N_CHIPS - 1
    x, y, c = _mesh_pos()
    mine, sibling, chips = 2 * x + y, (x, y, 1 - c), _other_chips(x, y)

    def copy(src, dst, slot, to):
        return pltpu.make_async_remote_copy(src_ref=src, dst_ref=dst, send_sem=send.at[slot],
                                            recv_sem=recv.at[slot], device_id=to, device_id_type=MESH)

    def half_rows(t, core):
        rows = ins[t].shape[0] // 2
        return pl.ds(pl.multiple_of(core * rows, rows), rows)

    def landing(t, chip, core):
        return outs[t].at[chip, 0, half_rows(t, core), :]

    slots = [(t, r, chip) for t in range(nw) for r, chip in enumerate(chips)]
    own = [copy(ins[t], outs[t].at[mine, 0], 2 * nw * nrel + t, sibling) for t in range(nw)]
    first = [copy(ins[t].at[half_rows(t, c), :], landing(t, mine, c), t * nrel + r, (px, py, c))
             for t, r, (px, py) in slots]
    relays = [copy(landing(t, 2 * px + py, c), landing(t, 2 * px + py, c), nw * nrel + t * nrel + r, sibling)
              for t, r, (px, py) in slots]

    def start():
        for cp in own + first:
            cp.start()

    def relay():
        for (t, r, (px, py)), cp in zip(slots, relays):
            copy(landing(t, 2 * px + py, c), landing(t, 2 * px + py, c), t * nrel + r, (px, py, c)).wait_recv()
            cp.start()

    def finish():
        for t, r, (px, py) in slots:
            passed = landing(t, 2 * px + py, 1 - c)
            copy(passed, passed, nw * nrel + t * nrel + r, sibling).wait_recv()
        for cp in first + relays:
            cp.wait_send()
        for cp in own:
            cp.wait()

    return start, relay, finish


def _gather_weights(blocks, name):
    nw = len(blocks)

    def body(*refs):
        start, relay, finish = _gather_steps(refs[:nw], refs[nw:2 * nw], *refs[2 * nw:])
        start()
        relay()
        finish()

    return pl.pallas_call(
        body, in_specs=[ANY] * nw, out_specs=[ANY] * nw, out_shape=_gathered_shapes(blocks),
        scratch_shapes=_gather_sems(nw), name=name)(*blocks)


HALF = 512

GRAD_GEOM = {"w_in": ("rows", D_MODEL, IN_COLS // N_CHIPS), "w_up": ("rows", D_MODEL, 2 * D_FF // N_CHIPS),
             "w_out": ("cols", D_MODEL, D_MODEL // N_CHIPS), "w_down": ("cols", D_FF, D_FF // N_CHIPS)}


def _exchange_shape(n):
    kind, a, b = GRAD_GEOM[n]
    return (N_CHIPS, HALF, b) if kind == "rows" else (a, HALF)


def _piece_shape(n):
    name, _, part = n.partition(":")
    kind, _, b = GRAD_GEOM[name]
    if part:
        assert kind == "rows"
        return (HALF, b // 2)
    return (HALF, b) if kind == "rows" else (b, HALF)


def _half_of(ref, n, core):
    if GRAD_GEOM[n][0] == "rows":
        return ref.at[:, pl.ds(pl.multiple_of(core * HALF, HALF), HALF), :]
    return ref.at[core]


def _piece_of(ref, n, chip):
    name, _, part = n.partition(":")
    kind, _, b = GRAD_GEOM[name]
    if part:
        return ref.at[chip, :, pl.ds(int(part) * (b // 2), b // 2)]
    return ref.at[chip] if kind == "rows" else ref.at[pl.ds(pl.multiple_of(chip * b, b), b), :]


def _pair_exchange(g, names, name):
    n = len(names)

    def body(*refs):
        send, recv = refs[2 * n:]
        x, y, c = _mesh_pos()
        o = 1 - c
        cps = [pltpu.make_async_remote_copy(src_ref=_half_of(refs[t], nm, o), dst_ref=refs[n + t], send_sem=send.at[t],
                                            recv_sem=recv.at[t], device_id=(x, y, o), device_id_type=MESH)
               for t, nm in enumerate(names)]
        for cp in cps:
            cp.start()
        for cp in cps:
            cp.wait()

    return pl.pallas_call(
        body, in_specs=[ANY] * n, out_specs=[ANY] * n,
        out_shape=[jax.ShapeDtypeStruct(_exchange_shape(nm), BF16) for nm in names],
        scratch_shapes=[pltpu.SemaphoreType.DMA((n,)), pltpu.SemaphoreType.DMA((n,))],
        name=name)(*[g[nm] for nm in names])


def _pair_sum(g, recv, pos, names, name_prefix):
    def add(a, b, grid, a_spec, b_spec, name):
        def body(pos_ref, a_ref, b_ref, o_ref):
            o_ref[...] = (a_ref[...].astype(F32) + b_ref[...].astype(F32)).astype(BF16)

        return pl.pallas_call(
            body, grid_spec=pltpu.PrefetchScalarGridSpec(
                num_scalar_prefetch=1, grid=grid, in_specs=[a_spec, b_spec], out_specs=b_spec),
            out_shape=jax.ShapeDtypeStruct(b.shape, BF16), compiler_params=_cparams("parallel"), name=name)(pos, a, b)

    out = []
    for nm, r in zip(names, recv):
        kind, rows, width = GRAD_GEOM[nm]
        if kind == "rows":
            out.append(add(g[nm], r, (N_CHIPS,), pl.BlockSpec((None, HALF, width), lambda j, pos: (j, pos[2], 0)),
                           pl.BlockSpec((None, HALF, width), lambda j, pos: (j, 0, 0)), f"{name_prefix}_{nm}"))
        else:
            out.append(add(g[nm], r, (rows // D_MODEL,), pl.BlockSpec((None, D_MODEL, HALF), lambda j, pos: (pos[2], j, 0)),
                           pl.BlockSpec((D_MODEL, HALF), lambda j, pos: (j, 0)), f"{name_prefix}_{nm}"))
    return out


def _scattered_shapes(names):
    return [jax.ShapeDtypeStruct((N_CHIPS - 1,) + _piece_shape(nm), BF16) for nm in names]


def _scatter_sems(n):
    return [pltpu.SemaphoreType.DMA((n * (N_CHIPS - 1),)), pltpu.SemaphoreType.DMA((n * (N_CHIPS - 1),))]


def _scatter_steps(sums, outs, send, recv, names):
    nrel = N_CHIPS - 1
    x, y, c = _mesh_pos()
    cps = []
    for r, (px, py) in enumerate(_other_chips(x, y)):
        for t, nm in enumerate(names):
            cps.append(pltpu.make_async_remote_copy(
                src_ref=_piece_of(sums[t], nm, 2 * px + py), dst_ref=outs[t].at[r], send_sem=send.at[t * nrel + r],
                recv_sem=recv.at[t * nrel + r], device_id=(px, py, c), device_id_type=MESH))

    def start():
        for cp in cps:
            cp.start()

    def finish():
        for cp in cps:
            cp.wait()

    return start, finish


def _chip_scatter(sums, names, name):
    n = len(names)

    def body(*refs):
        start, finish = _scatter_steps(refs[:n], refs[n:2 * n], *refs[2 * n:], names)
        start()
        finish()

    return pl.pallas_call(
        body, in_specs=[ANY] * n, out_specs=[ANY] * n, out_shape=_scattered_shapes(names),
        scratch_shapes=_scatter_sems(n), name=name)(*sums)


def _chip_sum(sums, recv, pos, names, name_prefix):
    def add(a, b, a_spec, shape, name):
        def body(pos_ref, a_ref, b_ref, o_ref):
            tot = a_ref[...].astype(F32)
            for r in range(N_CHIPS - 1):
                tot = tot + b_ref[r].astype(F32)
            o_ref[...] = tot

        return pl.pallas_call(
            body, grid_spec=pltpu.PrefetchScalarGridSpec(
                num_scalar_prefetch=1, grid=(1,), in_specs=[a_spec, pl.BlockSpec(b.shape, lambda i, pos: (0, 0, 0))],
                out_specs=pl.BlockSpec((None,) + shape, lambda i, pos: (pos[2], 0, 0))),
            out_shape=jax.ShapeDtypeStruct((2,) + shape, F32), compiler_params=_cparams("arbitrary"),
            name=name)(pos, a, b)

    chip = lambda pos: 2 * pos[0] + pos[1]
    out = []
    for nm, a, b in zip(names, sums, recv):
        shape = _piece_shape(nm)
        if GRAD_GEOM[nm][0] == "rows":
            spec = pl.BlockSpec((None,) + shape, lambda i, pos: (chip(pos), 0, 0))
        else:
            spec = pl.BlockSpec(shape, lambda i, pos: (chip(pos), 0))
        out.append(add(a, b, spec, shape, f"{name_prefix}_{nm}"))
    return out


def _pair_share(totals, name):
    n = len(totals)

    def body(*refs):
        ins, outs = refs[:n], refs[n:2 * n]
        send, recv = refs[2 * n:]
        x, y, c = _mesh_pos()
        o = 1 - c
        cps = [pltpu.make_async_remote_copy(src_ref=ins[t].at[c], dst_ref=outs[t].at[c], send_sem=send.at[t],
                                            recv_sem=recv.at[t], device_id=(x, y, o), device_id_type=MESH)
               for t in range(n)]
        for cp in cps:
            cp.start()
        for t in range(n):
            pltpu.make_async_remote_copy(src_ref=ins[t].at[o], dst_ref=outs[t].at[o], send_sem=send.at[t],
                                         recv_sem=recv.at[t], device_id=(x, y, o), device_id_type=MESH).wait_recv()
        for cp in cps:
            cp.wait_send()

    return pl.pallas_call(
        body, in_specs=[ANY] * n, out_specs=[ANY] * n,
        out_shape=[jax.ShapeDtypeStruct(t.shape, t.dtype) for t in totals],
        scratch_shapes=[pltpu.SemaphoreType.DMA((n,)), pltpu.SemaphoreType.DMA((n,))],
        input_output_aliases={t: t for t in range(n)}, name=name)(*totals)


def _chip_sums(l, g, pos, names):
    tag = f"l{l}_" + "_".join(names)
    recv = _pair_exchange(g, names, "pair_exchange_" + tag)
    return _pair_sum(g, recv, pos, names, "pair_sum_" + tag)


def _gradient_shards(l, sums, scattered, pos, names):
    tag = f"l{l}_" + "_".join(names)
    halves = _pair_share(_chip_sum(sums, scattered, pos, names, "chip_sum_" + tag), "pair_share_" + tag)
    out = {}
    for nm, t in zip(names, halves):
        rows, cols = _piece_shape(nm)
        out[nm] = t.reshape(2 * rows, cols) if GRAD_GEOM[nm][0] == "rows" else t.transpose(1, 0, 2).reshape(rows, 2 * cols)
    return out


N_DEV = 8


def _allreduce_small(packed, name):
    rows = packed.shape[0]

    def body(x_ref, out_ref, gath, send_sems, recv_sems, local_sem):
        x, y, c = _mesh_pos()
        me, sibling = (x, y, c), (x, y, 1 - c)
        chips = _other_chips(x, y)

        def blk(px, py, pc):
            return gath.at[pl.ds(pl.multiple_of((4 * px + 2 * py + pc) * rows, 8), rows), :]

        def copy(k, block, to, src=None):
            return pltpu.make_async_remote_copy(
                src_ref=blk(*block) if src is None else src, dst_ref=blk(*block), send_sem=send_sems.at[k],
                recv_sem=recv_sems.at[k], device_id=to, device_id_type=MESH)

        mine = pltpu.make_async_copy(x_ref, blk(*me), local_sem)
        mine.start()
        first = [copy(0, me, sibling, src=x_ref)]
        first += [copy(1 + j, me, (*chip, c), src=x_ref) for j, chip in enumerate(chips)]
        for cp in first:
            cp.start()
        passed = [copy(4 + j, (*chip, c), sibling) for j, chip in enumerate(chips)]
        for j, chip in enumerate(chips):
            copy(1 + j, (*chip, c), me).wait_recv()
            passed[j].start()
        copy(0, sibling, me).wait_recv()
        for j, chip in enumerate(chips):
            copy(4 + j, (*chip, 1 - c), me).wait_recv()
        for cp in first + passed:
            cp.wait_send()
        mine.wait()
        tot = gath[0:rows, :]
        for d in range(1, N_DEV):
            tot = tot + gath[d * rows:(d + 1) * rows, :]
        out_ref[...] = tot

    vmem = pl.BlockSpec(memory_space=pltpu.VMEM)
    return pl.pallas_call(
        body, in_specs=[vmem], out_specs=vmem, out_shape=jax.ShapeDtypeStruct((rows, LANES), F32),
        scratch_shapes=[pltpu.VMEM((N_DEV * rows, LANES), F32), pltpu.SemaphoreType.DMA((7,)),
                        pltpu.SemaphoreType.DMA((7,)), pltpu.SemaphoreType.DMA],
        compiler_params=pltpu.CompilerParams(vmem_limit_bytes=VMEM_LIMIT_BYTES),
        name=name)(packed)


def _adamw(w, g, m, v, name):
    rows, cols = w.shape
    tr = 256 if rows % 256 == 0 else rows

    def body(w_ref, g_ref, m_ref, v_ref, d_ref, mo_ref, vo_ref):
        gv = g_ref[...]
        mn = ADAM_B1 * m_ref[...] + (1.0 - ADAM_B1) * gv
        vn = ADAM_B2 * v_ref[...] + (1.0 - ADAM_B2) * (gv * gv)
        m_hat = mn / (1.0 - ADAM_B1 ** ADAM_STEP)
        v_hat = vn / (1.0 - ADAM_B2 ** ADAM_STEP)
        d_ref[...] = -ADAM_LR * (m_hat / (jnp.sqrt(v_hat) + ADAM_EPS) + ADAM_WD * w_ref[...])
        mo_ref[...] = mn
        vo_ref[...] = vn

    spec = pl.BlockSpec((tr, cols), lambda i: (i, 0))
    return pl.pallas_call(
        body, grid=(rows // tr,), in_specs=[spec] * 4, out_specs=[spec] * 3,
        out_shape=[jax.ShapeDtypeStruct((rows, cols), F32)] * 3, compiler_params=_cparams("parallel"),
        name=name)(w, g, m, v)


def _adamw_nd(w, g, m, v, name):
    cols = w.shape[-1] if w.shape[-1] % LANES == 0 else LANES
    outs = _adamw(*(t.reshape(-1, cols) for t in (w, g, m, v)), name)
    return tuple(t.reshape(w.shape) for t in outs)


def _pack(arrays):
    return jnp.concatenate([a.reshape(-1, LANES) for a in arrays], axis=0)


def _unpack(packed, shapes):
    out, row = [], 0
    for sh in shapes:
        n = math.prod(sh) // LANES
        out.append(packed[row:row + n].reshape(sh))
        row += n
    return out


WEIGHTS = ("pre_mix_norm", "w_in", "v_norm_g", "v_norm_b", "w_spatial", "b_spatial", "out_norm_a", "out_norm_b",
           "w_out", "post_mix_norm", "pre_ffn_norm", "w_up", "conv_w", "conv_b", "w_down", "post_ffn_norm")


def kernel(x, pre_mix_norm, w_in, v_norm_g, v_norm_b, w_spatial, b_spatial, out_norm_a, out_norm_b, w_out, post_mix_norm, pre_ffn_norm, w_up, conv_w, conv_b, w_down, post_ffn_norm, loss_target, m_pre_mix_norm, m_w_in, m_v_norm_g, m_v_norm_b, m_w_spatial, m_b_spatial, m_out_norm_a, m_out_norm_b, m_w_out, m_post_mix_norm, m_pre_ffn_norm, m_w_up, m_conv_w, m_conv_b, m_w_down, m_post_ffn_norm, v_pre_mix_norm, v_w_in, v_v_norm_g, v_v_norm_b, v_w_spatial, v_b_spatial, v_out_norm_a, v_out_norm_b, v_w_out, v_post_mix_norm, v_pre_ffn_norm, v_w_up, v_conv_w, v_conv_b, v_w_down, v_post_ffn_norm):
    w = dict(pre_mix_norm=pre_mix_norm, w_in=w_in, v_norm_g=v_norm_g, v_norm_b=v_norm_b, w_spatial=w_spatial,
             b_spatial=b_spatial, out_norm_a=out_norm_a, out_norm_b=out_norm_b, w_out=w_out,
             post_mix_norm=post_mix_norm, pre_ffn_norm=pre_ffn_norm, w_up=w_up, conv_w=conv_w, conv_b=conv_b,
             w_down=w_down, post_ffn_norm=post_ffn_norm)
    m = dict(pre_mix_norm=m_pre_mix_norm, w_in=m_w_in, v_norm_g=m_v_norm_g, v_norm_b=m_v_norm_b,
             w_spatial=m_w_spatial, b_spatial=m_b_spatial, out_norm_a=m_out_norm_a, out_norm_b=m_out_norm_b,
             w_out=m_w_out, post_mix_norm=m_post_mix_norm, pre_ffn_norm=m_pre_ffn_norm, w_up=m_w_up,
             conv_w=m_conv_w, conv_b=m_conv_b, w_down=m_w_down, post_ffn_norm=m_post_ffn_norm)
    v = dict(pre_mix_norm=v_pre_mix_norm, w_in=v_w_in, v_norm_g=v_v_norm_g, v_norm_b=v_v_norm_b,
             w_spatial=v_w_spatial, b_spatial=v_b_spatial, out_norm_a=v_out_norm_a, out_norm_b=v_out_norm_b,
             w_out=v_w_out, post_mix_norm=v_post_mix_norm, pre_ffn_norm=v_pre_ffn_norm, w_up=v_w_up,
             conv_w=v_conv_w, conv_b=v_conv_b, w_down=v_w_down, post_ffn_norm=v_post_ffn_norm)
    pos = jnp.stack([lax.axis_index("x"), lax.axis_index("y"), lax.axis_index("c")]).astype(jnp.int32)
    chip = 2 * lax.axis_index("x") + lax.axis_index("y")

    cw_cols = conv_w.shape[-1]
    blocks = [{n: w[n][l].astype(BF16) for n in BIG} for l in range(DEPTH)]
    w_in0, cw_all = _gather_weights([blocks[0]["w_in"], conv_w.reshape(-1, LANES)], "gather_w_in_l0")
    wg = dict(w_in=w_in0)
    conv_w_full = cw_all.reshape(N_CHIPS, DEPTH, 3, cw_cols).transpose(1, 2, 0, 3).reshape(DEPTH, 3, 2 * D_FF)

    small = {n: w[n] for n in SMALL}
    xs, target = x[0], loss_target[0]
    tabs = _rope_tables(xs.shape[0])
    params = [_layer_params(l, small, conv_w_full) for l in range(DEPTH)]
    saved, wgs = [], []
    xin = xs
    h = _rms_cast(xin, params[0]["pre_mix_norm"], "pre_mix_l0")
    for l in range(DEPTH):
        sv, gathered, wg = _layer_forward(l, xin, h, params[l], wg, tabs,
                                          [blocks[l + 1][n] for n in BIG] if l + 1 < DEPTH else None,
                                          blocks[0] if l == 0 else None,
                                          params[l + 1]["pre_mix_norm"] if l + 1 < DEPTH else None)
        saved.append(sv)
        wgs.append(wg)
        if l + 1 < DEPTH:
            wg = dict(zip(BIG, gathered))
            xin, h = sv["x2"], sv["h_next"]
    loss_part, dx, df, g_post = _loss_norm_bwd(saved[-1]["x1"], saved[-1]["f"], params[-1]["post_ffn_norm"], target,
                                               "loss")
    smalls, shards = [None] * DEPTH, [{} for _ in range(DEPTH)]
    pending = None
    for l in reversed(range(DEPTH)):
        dx1, dh1, big, smalls[l], scattered, done = _layer_backward(l, dx, df, saved[l], params[l], wgs[l], tabs, pos,
                                                                    pending[1:] if pending else None, hide=l == 0)
        smalls[l]["post_ffn_norm"] = g_post
        if l > 0:
            dx, smalls[l]["pre_mix_norm"], df, g_post = _norm_bwd_in_out(
                dx1, dh1, saved[l]["x0"], params[l]["pre_mix_norm"], saved[l - 1]["f"], params[l - 1]["post_ffn_norm"],
                f"norm_bwd_in_out_l{l}")
        else:
            dx, smalls[l]["pre_mix_norm"] = _norm_bwd_in(dx1, dh1, saved[l]["x0"], params[l]["pre_mix_norm"],
                                                         "norm_bwd_in_l0")
        if pending:
            shards[pending[0]].update(_gradient_shards(pending[0], pending[1], scattered, pos, pending[2]))
        for names, (sums, received) in done.items():
            shards[l].update(_gradient_shards(l, sums, received, pos, names))
        names = tuple(big)
        pending = (l, _chip_sums(l, big, pos, names), names)
    shards[pending[0]].update(_gradient_shards(
        pending[0], pending[1], _chip_scatter(pending[1], pending[2], f"chip_scatter_l{pending[0]}"), pos, pending[2]))

    small_shapes = [w[n].shape for n in SMALL]
    stacked = [jnp.stack([smalls[l][n].reshape(w[n].shape[1:]) for l in range(DEPTH)]) for n in SMALL]
    cw_grad = jnp.stack([smalls[l]["conv_w"] for l in range(DEPTH)])
    packed = _pack(stacked + [cw_grad, loss_part])
    total = _allreduce_small(packed, "allreduce_small")
    parts = _unpack(total, small_shapes + [cw_grad.shape, (8, LANES)])
    g_small = dict(zip(SMALL, parts[:len(SMALL)]))
    loss = parts[-1][0, 0]
    g_conv_w = lax.dynamic_slice(parts[-2], (0, 0, chip * cw_cols), conv_w.shape)

    grads = {n: jnp.stack([shards[l][n] for l in range(DEPTH)]) for n in BIG}
    grads.update(g_small)
    grads["conv_w"] = g_conv_w

    dp, mp, vp = _adamw(_pack([w[n] for n in SMALL]), _pack([g_small[n] for n in SMALL]),
                        _pack([m[n] for n in SMALL]), _pack([v[n] for n in SMALL]), "adamw_small")
    delta = dict(zip(SMALL, _unpack(dp, small_shapes)))
    new_m = dict(zip(SMALL, _unpack(mp, small_shapes)))
    new_v = dict(zip(SMALL, _unpack(vp, small_shapes)))
    for n in BIG + ("conv_w",):
        delta[n], new_m[n], new_v[n] = _adamw_nd(w[n], grads[n], m[n], v[n], "adamw_" + n)

    return (loss, dx[None], *[grads[n] for n in WEIGHTS], *[delta[n] for n in WEIGHTS],
            *[new_m[n] for n in WEIGHTS], *[new_v[n] for n in WEIGHTS])
```

```python
import functools
import math

import jax
import jax.numpy as jnp
import numpy as np
from jax import lax
from jax.experimental import pallas as pl
from jax.experimental.pallas import tpu as pltpu

F32 = jnp.float32
BF16 = jnp.bfloat16
MESH = pl.DeviceIdType.MESH

D_MODEL = 1024
A_WIDTH = 512
A_GROUPS = 4
GROUP_DIM = 128
CHUNK = 128
B_WIDTH = 512
HEAD_DIM = 64
ROT_DIM = 16
ROPE_THETA = 500000.0
DILATIONS = (1, 4, 16)
BAND = 128
IN_COLS = 2560
D_FF = 4096
EPS = 1e-6
NEG_INF = -1e30
N_CHIPS = 4
LANES = 128

ADAM_LR = 0.001
ADAM_B1 = 0.9
ADAM_B2 = 0.999
ADAM_EPS = 1e-08
ADAM_WD = 0.01
ADAM_STEP = 10

VMEM_LIMIT_BYTES = 56 * 1024 * 1024
RSQRT2 = 0.7071067811865476
INV_SQRT_2PI = 0.3989422804014327
GELU_C = 0.7978845608028654
GELU_A = 0.044715

ANY = pl.BlockSpec(memory_space=pl.ANY)
NN = ((1,), (0,))
NT = ((1,), (1,))
TN = ((0,), (0,))


def _cparams(*sem):
    return pltpu.CompilerParams(dimension_semantics=sem, vmem_limit_bytes=VMEM_LIMIT_BYTES)


def _dot(a, b, dims):
    return lax.dot_general(a, b, (dims, ((), ())), preferred_element_type=F32)


def _rsq_mean(a):
    return lax.rsqrt(jnp.mean(a * a, axis=-1, keepdims=True) + EPS)


def _rms_bwd(a, r, g, dz):
    t = dz * g
    da = r * t - a * (r * r * r) * jnp.mean(t * a, axis=-1, keepdims=True)
    return da, dz * a * r


def _colsum(a):
    return jnp.sum(a, axis=0, keepdims=True)


def _gelu_tanh(x):
    u = x * x
    t = jnp.tanh(x * (GELU_C + (GELU_C * GELU_A) * u))
    hx = 0.5 * x
    act = hx + hx * t
    grad = 0.5 + 0.5 * t + (hx - hx * t * t) * (GELU_C + (3.0 * GELU_C * GELU_A) * u)
    return act, grad


def _grid_edges(grid):
    ids = [pl.program_id(ax) for ax in range(len(grid))]
    first = functools.reduce(jnp.logical_and, [i == 0 for i in ids])
    last = functools.reduce(jnp.logical_and, [i == n - 1 for i, n in zip(ids, grid)])
    return first, last


def _matmul(a, b, *, grid, a_spec, b_spec, o_spec, o_shape, o_dtype, dims, nk, kaxis, acc_shape, name, b_2d=None,
            halves=False, scatter=None, gather=None):
    assert scatter is None or gather is None
    ns = len(scatter[0]) if scatter else len(gather) if gather else 0

    def body(*refs):
        a_ref, b_ref = refs[:2]
        o_ref = refs[2 + ns]
        scratch = refs[3 + 2 * ns:]
        if ns:
            first, last = _grid_edges(grid)
            if scatter:
                start, finish = _scatter_steps(refs[2:2 + ns], refs[3 + ns:3 + 2 * ns], scratch[-2], scratch[-1],
                                               scatter[1])
            else:
                start, relay, last_wait = _gather_steps(refs[2:2 + ns], refs[3 + ns:3 + 2 * ns], scratch[-2],
                                                        scratch[-1])

                def finish():
                    relay()
                    last_wait()
            pl.when(first)(start)
        def store(val):
            if halves:
                half = val.shape[1] // 2
                o_ref[0] = val[:, :half].astype(o_dtype)
                o_ref[1] = val[:, half:].astype(o_dtype)
            else:
                o_ref[...] = val.astype(o_dtype)

        bv = b_ref[...] if b_2d is None else b_ref[...].reshape(b_2d)
        part = _dot(a_ref[...], bv, dims)
        if nk == 1:
            store(part)
        else:
            acc = scratch[0]
            k = pl.program_id(kaxis)

            @pl.when(k == 0)
            def _():
                acc[...] = part

            @pl.when(k > 0)
            def _():
                acc[...] += part

            @pl.when(k == nk - 1)
            def _():
                store(acc[...])

        if ns:
            pl.when(last)(finish)

    sem = tuple("arbitrary" if (ns or (nk > 1 and ax == kaxis)) else "parallel" for ax in range(len(grid)))
    riding = list(scatter[0]) if scatter else list(gather or [])
    rider_shapes = _scattered_shapes(scatter[1]) if scatter else _gathered_shapes(riding)
    rider_sems = _scatter_sems(ns) if scatter else _gather_sems(ns) if gather else []
    res = pl.pallas_call(
        body, grid=grid, in_specs=[a_spec, b_spec] + [ANY] * ns, out_specs=[o_spec] + [ANY] * ns,
        out_shape=[jax.ShapeDtypeStruct(o_shape, o_dtype)] + rider_shapes,
        scratch_shapes=([pltpu.VMEM(acc_shape, F32)] if nk > 1 else []) + rider_sems,
        compiler_params=_cparams(*sem), name=name)(a, b, *riding)
    return (res[0], list(res[1:])) if ns else res[0]


def _mix_out_norm(mixed, w_out, x0, g_post, g_next, name):
    s, d = x0.shape
    tm = 512

    def body(a_ref, w_ref, x_ref, gp_ref, gn_ref, y_ref, x1_ref, h_ref):
        y = _dot(a_ref[...], w_ref[...].reshape(d, d), NN)
        y_ref[...] = y
        x1 = x_ref[...] + y * _rsq_mean(y) * gp_ref[...]
        x1_ref[...] = x1
        h_ref[...] = (x1 * _rsq_mean(x1) * gn_ref[...]).astype(BF16)

    row = pl.BlockSpec((tm, d), lambda i: (i, 0))
    vec = pl.BlockSpec((1, d), lambda i: (0, 0))
    return pl.pallas_call(
        body, grid=(s // tm,),
        in_specs=[row, pl.BlockSpec((N_CHIPS, None, d // N_CHIPS, d), lambda i: (0, 0, 0, 0)), row, vec, vec],
        out_specs=[row, row, row],
        out_shape=[jax.ShapeDtypeStruct((s, d), F32), jax.ShapeDtypeStruct((s, d), F32),
                   jax.ShapeDtypeStruct((s, d), BF16)],
        compiler_params=_cparams("parallel"), name=name)(mixed, w_out, x0, g_post, g_next)


def _proj_bwd(dproj, w_in, name, scatter=None):
    s = dproj.shape[0]
    wcol = IN_COLS // N_CHIPS
    ns = 0 if scatter is None else len(scatter[0])

    def body(*refs):
        a_ref, w_ref = refs[:2]
        o_ref = refs[2 + ns]
        if ns:
            start, finish = _scatter_steps(refs[2:2 + ns], refs[3 + ns:3 + 2 * ns], *refs[3 + 2 * ns:], scatter[1])
            first, last = _grid_edges((s // TMM,))
            pl.when(first)(start)
        acc = _dot(a_ref[:, :wcol], w_ref[0], NT)
        for j in range(1, N_CHIPS):
            acc = acc + _dot(a_ref[:, j * wcol:(j + 1) * wcol], w_ref[j], NT)
        o_ref[...] = acc
        if ns:
            pl.when(last)(finish)

    res = pl.pallas_call(
        body, grid=(s // TMM,),
        in_specs=[pl.BlockSpec((TMM, IN_COLS), lambda i: (i, 0)),
                  pl.BlockSpec((N_CHIPS, None, D_MODEL, wcol), lambda i: (0, 0, 0, 0))] + [ANY] * ns,
        out_specs=[pl.BlockSpec((TMM, D_MODEL), lambda i: (i, 0))] + [ANY] * ns,
        out_shape=[jax.ShapeDtypeStruct((s, D_MODEL), F32)] + (_scattered_shapes(scatter[1]) if ns else []),
        scratch_shapes=_scatter_sems(ns) if ns else [],
        compiler_params=_cparams("arbitrary" if ns else "parallel"), name=name)(dproj, w_in,
                                                                              *(scatter[0] if ns else []))
    return res[0], list(res[1:])


TM = 512
TMM = 1024


TR = 256


def _row_spec(width, col=0):
    return pl.BlockSpec((TR, width), lambda i, col=col: (i, col))


def _vec_spec(width):
    return pl.BlockSpec((1, width), lambda i: (0, 0))


def _rms_cast(x, g, name):
    s, d = x.shape

    def body(x_ref, g_ref, h_ref):
        a = x_ref[...]
        h_ref[...] = (a * _rsq_mean(a) * g_ref[...]).astype(BF16)

    return pl.pallas_call(
        body, grid=(s // TR,), in_specs=[_row_spec(d), _vec_spec(d)], out_specs=_row_spec(d),
        out_shape=jax.ShapeDtypeStruct((s, d), BF16), compiler_params=_cparams("parallel"), name=name)(x, g)


def _acc_init(refs):
    @pl.when(pl.program_id(0) == 0)
    def _():
        for r in refs:
            r[...] = jnp.zeros_like(r)


def _loss_norm_bwd(x1, f, g_post, target, name):
    s, d = x1.shape

    def body(x_ref, f_ref, gp_ref, t_ref, loss_ref, dx_ref, df_ref, dg_ref):
        _acc_init([loss_ref, dg_ref])
        fv = f_ref[...]
        r = _rsq_mean(fv)
        err = x_ref[...] + fv * r * gp_ref[...] - t_ref[...]
        dx = err * (1.0 / d)
        dx_ref[...] = dx
        part = 0.5 * jnp.sum(jnp.mean(err * err, axis=-1, keepdims=True), axis=0, keepdims=True)
        loss_ref[...] += jnp.broadcast_to(part, loss_ref.shape)
        da, dgt = _rms_bwd(fv, r, gp_ref[...], dx)
        df_ref[...] = da.astype(BF16)
        dg_ref[...] += _colsum(dgt)

    return pl.pallas_call(
        body, grid=(s // TR,), in_specs=[_row_spec(d), _row_spec(d), _vec_spec(d), _row_spec(d)],
        out_specs=[pl.BlockSpec((8, LANES), lambda i: (0, 0)), _row_spec(d), _row_spec(d), _vec_spec(d)],
        out_shape=[jax.ShapeDtypeStruct((8, LANES), F32), jax.ShapeDtypeStruct((s, d), F32),
                   jax.ShapeDtypeStruct((s, d), BF16), jax.ShapeDtypeStruct((1, d), F32)],
        compiler_params=_cparams("arbitrary"), name=name)(x1, f, g_post, target)


def _norm_bwd_mid(dx2, dh2, x1, g_pf, y1, g_pm, name):
    s, d = dx2.shape

    def body(dx2_ref, dh_ref, x1_ref, gpf_ref, y1_ref, gpm_ref, dx1_ref, dy1_ref, dgpf_ref, dgpm_ref):
        _acc_init([dgpf_ref, dgpm_ref])
        x1 = x1_ref[...]
        da, dgt = _rms_bwd(x1, _rsq_mean(x1), gpf_ref[...], dh_ref[...])
        dx1 = dx2_ref[...] + da
        dx1_ref[...] = dx1
        dgpf_ref[...] += _colsum(dgt)
        y1 = y1_ref[...]
        dy, dgt2 = _rms_bwd(y1, _rsq_mean(y1), gpm_ref[...], dx1)
        dy1_ref[...] = dy.astype(BF16)
        dgpm_ref[...] += _colsum(dgt2)

    return pl.pallas_call(
        body, grid=(s // TR,),
        in_specs=[_row_spec(d), _row_spec(d), _row_spec(d), _vec_spec(d), _row_spec(d), _vec_spec(d)],
        out_specs=[_row_spec(d), _row_spec(d), _vec_spec(d), _vec_spec(d)],
        out_shape=[jax.ShapeDtypeStruct((s, d), F32), jax.ShapeDtypeStruct((s, d), BF16),
                   jax.ShapeDtypeStruct((1, d), F32), jax.ShapeDtypeStruct((1, d), F32)],
        compiler_params=_cparams("arbitrary"), name=name)(dx2, dh2, x1, g_pf, y1, g_pm)


def _norm_bwd_in_out(dx1, dh1, x0, g1, f_below, g_post_below, name):
    s, d = dx1.shape

    def body(dx1_ref, dh_ref, x0_ref, g_ref, f_ref, gp_ref, dx0_ref, dg_ref, df_ref, dgp_ref):
        _acc_init([dg_ref, dgp_ref])
        x0 = x0_ref[...]
        da, dgt = _rms_bwd(x0, _rsq_mean(x0), g_ref[...], dh_ref[...])
        dx0 = dx1_ref[...] + da
        dx0_ref[...] = dx0
        dg_ref[...] += _colsum(dgt)
        fv = f_ref[...]
        db, dgt2 = _rms_bwd(fv, _rsq_mean(fv), gp_ref[...], dx0)
        df_ref[...] = db.astype(BF16)
        dgp_ref[...] += _colsum(dgt2)

    return pl.pallas_call(
        body, grid=(s // TR,),
        in_specs=[_row_spec(d), _row_spec(d), _row_spec(d), _vec_spec(d), _row_spec(d), _vec_spec(d)],
        out_specs=[_row_spec(d), _vec_spec(d), _row_spec(d), _vec_spec(d)],
        out_shape=[jax.ShapeDtypeStruct((s, d), F32), jax.ShapeDtypeStruct((1, d), F32),
                   jax.ShapeDtypeStruct((s, d), BF16), jax.ShapeDtypeStruct((1, d), F32)],
        compiler_params=_cparams("arbitrary"), name=name)(dx1, dh1, x0, g1, f_below, g_post_below)


def _norm_bwd_in(dx1, dh1, x0, g1, name):
    s, d = dx1.shape

    def body(dx1_ref, dh_ref, x0_ref, g_ref, dx0_ref, dg_ref):
        _acc_init([dg_ref])
        x0 = x0_ref[...]
        da, dgt = _rms_bwd(x0, _rsq_mean(x0), g_ref[...], dh_ref[...])
        dx0_ref[...] = dx1_ref[...] + da
        dg_ref[...] += _colsum(dgt)

    return pl.pallas_call(
        body, grid=(s // TR,), in_specs=[_row_spec(d), _row_spec(d), _row_spec(d), _vec_spec(d)],
        out_specs=[_row_spec(d), _vec_spec(d)],
        out_shape=[jax.ShapeDtypeStruct((s, d), F32), jax.ShapeDtypeStruct((1, d), F32)],
        compiler_params=_cparams("arbitrary"), name=name)(dx1, dh1, x0, g1)


def _tril_mask():
    row = lax.broadcasted_iota(jnp.int32, (CHUNK, CHUNK), 0)
    col = lax.broadcasted_iota(jnp.int32, (CHUNK, CHUNK), 1)
    return row >= col


def _gating_forward(pa, gv, bv, wt, bsf):
    er = lax.erf(pa * RSQRT2)
    za = 0.5 * pa * (1.0 + er)
    u = za[:, :A_WIDTH]
    va = za[:, A_WIDTH:]
    xc = va - jnp.mean(va, axis=-1, keepdims=True)
    rs = lax.rsqrt(jnp.mean(xc * xc, axis=-1, keepdims=True) + EPS)
    vn = xc * rs
    vlb = (vn * gv + bv).astype(BF16)
    sg = jnp.concatenate(
        [_dot(wt[g], vlb[:, g * GROUP_DIM:(g + 1) * GROUP_DIM], NN) for g in range(A_GROUPS)], axis=1) + bsf
    return er, u, rs, vn, vlb, sg


def _masked_ws(ws_ref):
    mask = _tril_mask()
    return [jnp.where(mask, ws_ref[g], 0.0).astype(BF16) for g in range(A_GROUPS)]


def _mixer_a_fwd(proj, gv, bv, ws, bsf, ga, name):
    s = proj.shape[0]

    def body(p_ref, gv_ref, bv_ref, ws_ref, bs_ref, ga_ref, o_ref):
        wt = _masked_ws(ws_ref)
        for ch in range(TR // CHUNK):
            rows = slice(ch * CHUNK, (ch + 1) * CHUNK)
            _, u, _, _, _, sg = _gating_forward(p_ref[rows, :].astype(F32), gv_ref[...], bv_ref[...], wt, bs_ref[...])
            oa = u * sg
            o_ref[rows, :] = (oa * _rsq_mean(oa) * ga_ref[...]).astype(BF16)

    return pl.pallas_call(
        body, grid=(s // TR,),
        in_specs=[_row_spec(2 * A_WIDTH), _vec_spec(A_WIDTH), _vec_spec(A_WIDTH),
                  pl.BlockSpec((A_GROUPS, CHUNK, CHUNK), lambda i: (0, 0, 0)),
                  pl.BlockSpec((CHUNK, A_WIDTH), lambda i: (0, 0)), _vec_spec(A_WIDTH)],
        out_specs=_row_spec(A_WIDTH), out_shape=jax.ShapeDtypeStruct((s, A_WIDTH + B_WIDTH), BF16),
        compiler_params=_cparams("parallel"), name=name)(proj, gv, bv, ws, bsf, ga)


def _mixer_a_bwd(proj, dmixed, gv, bv, ws, bsf, ga, name, scatter=None):
    s = proj.shape[0]
    nsteps = s // TR
    ns = 0 if scatter is None else len(scatter[0])

    def body(*refs):
        p_ref, dm_ref, gv_ref, bv_ref, ws_ref, bs_ref, ga_ref = refs[:7]
        dp_ref, dga_ref, dgv_ref, dbv_ref, dbs_ref, dws_ref = refs[7 + ns:13 + ns]
        if ns:
            start, finish = _scatter_steps(refs[7:7 + ns], refs[13 + ns:13 + 2 * ns], *refs[13 + 2 * ns:], scatter[1])
            first, last = _grid_edges((nsteps,))
            pl.when(first)(start)
        _acc_init([dga_ref, dgv_ref, dbv_ref, dbs_ref, dws_ref])
        mask = _tril_mask()
        wt = _masked_ws(ws_ref)
        gvv = gv_ref[...]
        gav = ga_ref[...]
        for ch in range(TR // CHUNK):
            rows = slice(ch * CHUNK, (ch + 1) * CHUNK)
            pa = p_ref[rows, :].astype(F32)
            er, u, rs, vn, vlb, sg = _gating_forward(pa, gvv, bv_ref[...], wt, bs_ref[...])
            oa = u * sg
            doa, dgt = _rms_bwd(oa, _rsq_mean(oa), gav, dm_ref[rows, :])
            dga_ref[...] += _colsum(dgt)
            du = doa * sg
            dsg = doa * u
            dbs_ref[...] += dsg
            dsgb = dsg.astype(BF16)
            dvl = []
            for g in range(A_GROUPS):
                cols = slice(g * GROUP_DIM, (g + 1) * GROUP_DIM)
                dws_ref[g] += jnp.where(mask, _dot(dsgb[:, cols], vlb[:, cols], NT), 0.0)
                dvl.append(_dot(wt[g], dsgb[:, cols], TN))
            dvl = jnp.concatenate(dvl, axis=1)
            dgv_ref[...] += _colsum(dvl * vn)
            dbv_ref[...] += _colsum(dvl)
            dvn = dvl * gvv
            dva = rs * (dvn - jnp.mean(dvn, axis=-1, keepdims=True)
                        - vn * jnp.mean(dvn * vn, axis=-1, keepdims=True))
            gp = 0.5 * (1.0 + er) + pa * jnp.exp(-0.5 * pa * pa) * INV_SQRT_2PI
            dp_ref[rows, :] = (jnp.concatenate([du, dva], axis=1) * gp).astype(BF16)

        @pl.when(pl.program_id(0) == nsteps - 1)
        def _():
            for g in range(A_GROUPS):
                cols = slice(g * GROUP_DIM, (g + 1) * GROUP_DIM)
                tot = jnp.sum(dbs_ref[:, cols], axis=1, keepdims=True)
                dbs_ref[:, cols] = jnp.broadcast_to(tot, (CHUNK, GROUP_DIM))

        if ns:
            pl.when(last)(finish)

    full = lambda *shape: pl.BlockSpec(shape, lambda i: (0,) * len(shape))
    res = pl.pallas_call(
        body, grid=(nsteps,),
        in_specs=[_row_spec(2 * A_WIDTH), _row_spec(A_WIDTH), _vec_spec(A_WIDTH), _vec_spec(A_WIDTH),
                  full(A_GROUPS, CHUNK, CHUNK), full(CHUNK, A_WIDTH), _vec_spec(A_WIDTH)] + [ANY] * ns,
        out_specs=[_row_spec(2 * A_WIDTH), _vec_spec(A_WIDTH), _vec_spec(A_WIDTH), _vec_spec(A_WIDTH),
                   full(CHUNK, A_WIDTH), full(A_GROUPS, CHUNK, CHUNK)] + [ANY] * ns,
        out_shape=[jax.ShapeDtypeStruct((s, IN_COLS), BF16), jax.ShapeDtypeStruct((1, A_WIDTH), F32),
                   jax.ShapeDtypeStruct((1, A_WIDTH), F32), jax.ShapeDtypeStruct((1, A_WIDTH), F32),
                   jax.ShapeDtypeStruct((CHUNK, A_WIDTH), F32),
                   jax.ShapeDtypeStruct((A_GROUPS, CHUNK, CHUNK), F32)]
        + (_scattered_shapes(scatter[1]) if ns else []),
        scratch_shapes=_scatter_sems(ns) if ns else [],
        compiler_params=_cparams("arbitrary"), name=name)(proj, dmixed, gv, bv, ws, bsf, ga,
                                                          *(scatter[0] if ns else []))
    return res[:6] + (list(res[6:]),)


def _rope_tables(s):
    half = ROT_DIM // 2
    lane = jnp.arange(LANES) % HEAD_DIM
    inv = ROPE_THETA ** (-(2 * (lane % half)).astype(F32) / ROT_DIM)
    ang = jnp.arange(s, dtype=F32)[:, None] * inv[None, :]
    cos, sin = jnp.cos(ang), jnp.sin(ang)
    c = jnp.where(lane < ROT_DIM, cos, 1.0)
    s1 = jnp.where(lane < half, -sin, 0.0)
    s2 = jnp.where((lane >= half) & (lane < ROT_DIM), sin, 0.0)
    return c, s1, s2


def _lane_blocks(width):
    return [slice(b * LANES, (b + 1) * LANES) for b in range(width // LANES)]


CLASS_DILS = tuple(d for d in DILATIONS if d > 1)


def _class_shape(s, dil, dtype):
    return jax.ShapeDtypeStruct((dil, s // dil, B_WIDTH), dtype)


def _class_spec(dil):
    return pl.BlockSpec((dil, TR // dil, B_WIDTH), lambda i, *_: (0, i, 0))


NBLK = B_WIDTH // LANES
STAGE = pltpu.VMEM((NBLK, TR, LANES), F32)


def _stage_put(stage, value):
    for b, sl in enumerate(_lane_blocks(B_WIDTH)):
        stage[b] = value[:, sl]


def _stage_get(stage):
    return jnp.concatenate([stage[b] for b in range(NBLK)], axis=1)


def _store_classes(stage, dst_ref, dil):
    for b, sl in enumerate(_lane_blocks(B_WIDTH)):
        for r in range(dil):
            dst_ref[r, :, sl] = stage[b, pl.ds(r, TR // dil, stride=dil), :].astype(dst_ref.dtype)


def _load_classes(src_ref, stage, dil):
    for b, sl in enumerate(_lane_blocks(B_WIDTH)):
        for r in range(dil):
            stage[b, pl.ds(r, TR // dil, stride=dil), :] = src_ref[r, :, sl].astype(F32)
    return _stage_get(stage)


def _rope_fwd(proj, tabs, name, gather=None):
    s = proj.shape[0]
    half = ROT_DIM // 2
    scale = HEAD_DIM ** -0.5
    nlay = 1 + len(CLASS_DILS)
    ng = 0 if gather is None else len(gather)

    def body(q_ref, k_ref, v_ref, c_ref, s1_ref, s2_ref, *rest):
        outs, stage = rest[ng:ng + 3 * nlay], rest[2 * ng + 3 * nlay]
        if ng:
            start, relay, finish = _gather_steps(rest[:ng], rest[ng + 3 * nlay:2 * ng + 3 * nlay],
                                                 *rest[2 * ng + 3 * nlay + 1:])
            first, last = _grid_edges((s // TR,))
            pl.when(first)(start)
        c, s1, s2 = c_ref[...], s1_ref[...], s2_ref[...]
        for which, (src, mul) in enumerate(((q_ref, scale), (k_ref, 1.0), (v_ref, None))):
            if mul is None:
                _stage_put(stage, src[...].astype(F32))
            else:
                for b, sl in enumerate(_lane_blocks(B_WIDTH)):
                    a = src[:, sl].astype(F32)
                    r = a * c + pltpu.roll(a, LANES - half, 1) * s1 + pltpu.roll(a, half, 1) * s2
                    stage[b] = r * mul
            dst = outs[which * nlay:(which + 1) * nlay]
            dst[0][...] = _stage_get(stage).astype(BF16)
            for ref, d in zip(dst[1:], CLASS_DILS):
                _store_classes(stage, ref, d)

        if ng:
            @pl.when(last)
            def _():
                relay()
                finish()

    tab = pl.BlockSpec((TR, LANES), lambda i: (i, 0))
    lay_specs = [_row_spec(B_WIDTH)] + [_class_spec(d) for d in CLASS_DILS]
    lay_shapes = [jax.ShapeDtypeStruct((s, B_WIDTH), BF16)] + [_class_shape(s, d, BF16) for d in CLASS_DILS]
    outs = pl.pallas_call(
        body, grid=(s // TR,),
        in_specs=[_row_spec(B_WIDTH, 2), _row_spec(B_WIDTH, 3), _row_spec(B_WIDTH, 4), tab, tab, tab] + [ANY] * ng,
        out_specs=lay_specs * 3 + [ANY] * ng, out_shape=lay_shapes * 3 + _gathered_shapes(gather or []),
        scratch_shapes=[STAGE] + (_gather_sems(ng) if ng else []),
        compiler_params=_cparams("arbitrary" if ng else "parallel"), name=name)(proj, proj, proj, *tabs,
                                                                              *(gather or []))
    q, k, v = (dict(zip(DILATIONS, outs[w * nlay:(w + 1) * nlay])) for w in range(3))
    return q, k, v, list(outs[3 * nlay:])


def _as_classes(t):
    return t if t.ndim == 3 else t[None]


def _head_masks():
    lane = lax.broadcasted_iota(jnp.int32, (1, LANES), 1)
    return lane < HEAD_DIM, lane >= HEAD_DIM


def _stack_heads(t):
    lo, hi = _head_masks()
    zero = jnp.zeros_like(t)
    return jnp.concatenate([jnp.where(lo, t, zero), jnp.where(hi, t, zero)], axis=0)


MAX_SEGMENT_BLOCKS = 8


def _segment_masks(j):
    qi = lax.broadcasted_iota(jnp.int32, (BAND, 2 * BAND), 0)
    kj = lax.broadcasted_iota(jnp.int32, (BAND, 2 * BAND), 1)
    both = (kj >= qi) & (kj <= qi + BAND)
    own = kj[:, :BAND] <= qi[:, :BAND]
    head = both & ((kj >= BAND) | (j > 0))
    return tuple(jnp.concatenate([m, m], axis=0) for m in (own, both, head))


def _block_rows(g):
    return pl.ds(pl.multiple_of(g * BAND, BAND), BAND)


def _key_rows(g):
    return pl.ds(pl.multiple_of((g - 1) * BAND, BAND), 2 * BAND)


def _segments(n):
    nb = n // BAND
    seg = min(nb, MAX_SEGMENT_BLOCKS)
    return seg, nb // seg


def _segment_specs(seg):
    main = pl.BlockSpec((None, seg * BAND, B_WIDTH), lambda r, j: (r, j, 0))
    halo = pl.BlockSpec((None, BAND, B_WIDTH), lambda r, j: (r, jnp.maximum(j * seg - 1, 0), 0))
    return main, halo


def _attn_fwd(q, k, v, name, gather=None):
    dil, n, _ = q.shape
    seg, nseg = _segments(n)
    nh = 2 if nseg > 1 else 0
    ng = 0 if gather is None else len(gather)

    def body(*refs):
        q_ref, k_ref, v_ref = refs[:3]
        halos = refs[3:3 + nh]
        o_ref, l_ref = refs[3 + nh + ng:5 + nh + ng]
        if ng:
            start, relay, finish = _gather_steps(refs[3 + nh:3 + nh + ng], refs[5 + nh + ng:5 + nh + 2 * ng],
                                                 *refs[5 + nh + 2 * ng:])
            first, last = _grid_edges((dil, nseg))
            pl.when(first)(start)
        own, both, head = _segment_masks(pl.program_id(1))
        lo, _ = _head_masks()

        def block(rows, keys_of, valid):
            for sl in _lane_blocks(B_WIDTH):
                kk, vv = keys_of(sl)
                sc = jnp.where(valid, _dot(_stack_heads(q_ref[rows, sl]), kk, NT), NEG_INF)
                mx = jnp.max(sc, axis=1, keepdims=True)
                p = jnp.exp(sc - mx)
                den = jnp.sum(p, axis=1, keepdims=True)
                out = _dot(p.astype(BF16), vv, NN) / den
                lse = mx + jnp.log(den)
                o_ref[rows, sl] = jnp.where(lo, out[:BAND], out[BAND:]).astype(BF16)
                l_ref[rows, sl] = jnp.where(lo, lse[:BAND], lse[BAND:])

        if nh:
            block(_block_rows(0), lambda sl: (jnp.concatenate([halos[0][:, sl], k_ref[0:BAND, sl]], axis=0),
                                              jnp.concatenate([halos[1][:, sl], v_ref[0:BAND, sl]], axis=0)), head)
        else:
            block(_block_rows(0), lambda sl: (k_ref[0:BAND, sl], v_ref[0:BAND, sl]), own)

        @pl.loop(1, seg)
        def _(g):
            block(_block_rows(g), lambda sl: (k_ref[_key_rows(g), sl], v_ref[_key_rows(g), sl]), both)

        if ng:
            @pl.when(last)
            def _():
                relay()
                finish()

    main, halo = _segment_specs(seg)
    res = pl.pallas_call(
        body, grid=(dil, nseg), in_specs=[main] * 3 + [halo] * nh + [ANY] * ng, out_specs=[main, main] + [ANY] * ng,
        out_shape=[jax.ShapeDtypeStruct((dil, n, B_WIDTH), BF16), jax.ShapeDtypeStruct((dil, n, B_WIDTH), F32)]
        + _gathered_shapes(gather or []),
        scratch_shapes=_gather_sems(ng) if ng else [],
        compiler_params=_cparams(*(["arbitrary"] * 2 if ng else ["parallel"] * 2)), name=name)(
            q, k, v, *([k, v] if nh else []), *(gather or []))
    return res[0], res[1], list(res[2:])


def _attn_bwd(q, k, v, do, lse, delta, name, scatter=None):
    dil, n, _ = q.shape
    seg, nseg = _segments(n)
    nh = 2 if nseg > 1 else 0
    ns = 0 if scatter is None else len(scatter[0])

    def body(*refs):
        q_ref, k_ref, v_ref, do_ref, lse_ref, dl_ref = refs[:6]
        halos = refs[6:6 + nh]
        dq_ref, dk_ref, dv_ref = refs[6 + nh + ns:9 + nh + ns]
        halo_out = refs[9 + nh + ns:9 + 2 * nh + ns]
        ck_ref, cv_ref = refs[9 + 2 * nh + 2 * ns:11 + 2 * nh + 2 * ns]
        if ns:
            start, finish = _scatter_steps(refs[6 + nh:6 + nh + ns], refs[9 + 2 * nh + ns:9 + 2 * nh + 2 * ns],
                                           *refs[11 + 2 * nh + 2 * ns:], scatter[1])
            first, last = _grid_edges((dil, nseg))
            pl.when(first)(start)
        own, both, head = _segment_masks(pl.program_id(1))
        lo, _ = _head_masks()
        lane = lax.broadcasted_iota(jnp.int32, (1, LANES), 1)

        def per_head(t):
            return jnp.concatenate(
                [jnp.sum(jnp.where(lane == first, t, 0.0), axis=1, keepdims=True) for first in (0, HEAD_DIM)], axis=0)

        def grads(rows, kk, vv, valid, sl):
            q2 = _stack_heads(q_ref[rows, sl])
            do2 = _stack_heads(do_ref[rows, sl])
            p = jnp.where(valid, jnp.exp(_dot(q2, kk, NT) - per_head(lse_ref[rows, sl])), 0.0)
            ds = (p * (_dot(do2, vv, NT) - per_head(dl_ref[rows, sl]))).astype(BF16)
            dq = _dot(ds, kk, NN)
            dq_ref[rows, sl] = jnp.where(lo, dq[:BAND], dq[BAND:]).astype(BF16)
            return _dot(ds, q2, TN), _dot(p.astype(BF16), do2, TN)

        for sl in _lane_blocks(B_WIDTH):
            if nh:
                dkk, dvv = grads(_block_rows(0), jnp.concatenate([halos[0][:, sl], k_ref[0:BAND, sl]], axis=0),
                                 jnp.concatenate([halos[1][:, sl], v_ref[0:BAND, sl]], axis=0), head, sl)
                halo_out[0][:, sl], halo_out[1][:, sl] = dkk[:BAND], dvv[:BAND]
                ck_ref[:, sl], cv_ref[:, sl] = dkk[BAND:], dvv[BAND:]
            else:
                ck_ref[:, sl], cv_ref[:, sl] = grads(_block_rows(0), k_ref[0:BAND, sl], v_ref[0:BAND, sl], own, sl)

        @pl.loop(1, seg)
        def _(g):
            before = _block_rows(g - 1)
            for sl in _lane_blocks(B_WIDTH):
                dkk, dvv = grads(_block_rows(g), k_ref[_key_rows(g), sl], v_ref[_key_rows(g), sl], both, sl)
                dk_ref[before, sl] = (ck_ref[:, sl] + dkk[:BAND]).astype(BF16)
                dv_ref[before, sl] = (cv_ref[:, sl] + dvv[:BAND]).astype(BF16)
                ck_ref[:, sl] = dkk[BAND:]
                cv_ref[:, sl] = dvv[BAND:]

        final = pl.ds((seg - 1) * BAND, BAND)
        dk_ref[final, :] = ck_ref[...].astype(BF16)
        dv_ref[final, :] = cv_ref[...].astype(BF16)

        if ns:
            pl.when(last)(finish)

    main, halo = _segment_specs(seg)
    shape = jax.ShapeDtypeStruct((dil, n, B_WIDTH), BF16)
    halo_shape = jax.ShapeDtypeStruct((dil, nseg, BAND, B_WIDTH), F32)
    halo_spec = pl.BlockSpec((None, None, BAND, B_WIDTH), lambda r, j: (r, j, 0, 0))
    res = pl.pallas_call(
        body, grid=(dil, nseg), in_specs=[main] * 6 + [halo] * nh + [ANY] * ns,
        out_specs=[main] * 3 + [halo_spec] * nh + [ANY] * ns,
        out_shape=[shape] * 3 + [halo_shape] * nh + (_scattered_shapes(scatter[1]) if ns else []),
        scratch_shapes=[pltpu.VMEM((BAND, B_WIDTH), F32)] * 2 + (_scatter_sems(ns) if ns else []),
        compiler_params=_cparams(*(["arbitrary"] * 2 if ns else ["parallel"] * 2)), name=name)(
            q, k, v, do, lse, delta, *([k, v] if nh else []), *(scatter[0] if ns else []))
    return res[0], res[1], res[2], (tuple(res[3:3 + nh]) if nh else None), list(res[3 + nh:])


def _attn_combine(outs, lses, gb, mixed, name, gather=None):
    s = mixed.shape[0]
    npat = len(DILATIONS)
    w = B_WIDTH
    ng = 0 if gather is None else len(gather)

    def body(*refs):
        o_refs, l_refs = refs[:npat], refs[npat:2 * npat]
        g_ref = refs[2 * npat]
        ob_ref = refs[2 * npat + 2 + ng]
        lse_refs = refs[2 * npat + 3 + ng:3 * npat + 3 + ng]
        mb_ref = refs[3 * npat + 3 + ng]
        stage = refs[3 * npat + 4 + 2 * ng]
        if ng:
            start, relay, finish = _gather_steps(refs[2 * npat + 2:2 * npat + 2 + ng],
                                                 refs[3 * npat + 4 + ng:3 * npat + 4 + 2 * ng],
                                                 *refs[3 * npat + 5 + 2 * ng:])
            first, last = _grid_edges((s // TR,))
            pl.when(first)(start)
        os_ = [o_refs[0][...].astype(F32)] + [_load_classes(r, stage, d) for r, d in zip(o_refs[1:], CLASS_DILS)]
        ls = [l_refs[0][...]] + [_load_classes(r, stage, d) for r, d in zip(l_refs[1:], CLASS_DILS)]
        mx = functools.reduce(jnp.maximum, ls)
        ws = [jnp.exp(l - mx) for l in ls]
        tot = functools.reduce(lambda a, b: a + b, ws)
        ob = functools.reduce(lambda a, b: a + b, [wt / tot * o for wt, o in zip(ws, os_)])
        ob_ref[...] = ob
        lse = mx + jnp.log(tot)
        _stage_put(stage, lse)
        lse_refs[0][...] = lse
        for ref, d in zip(lse_refs[1:], CLASS_DILS):
            _store_classes(stage, ref, d)
        mb_ref[...] = (ob * _rsq_mean(ob) * g_ref[...]).astype(BF16)

        if ng:
            @pl.when(last)
            def _():
                relay()
                finish()

    lay_specs = [_row_spec(w)] + [_class_spec(d) for d in CLASS_DILS]
    res = pl.pallas_call(
        body, grid=(s // TR,), in_specs=lay_specs * 2 + [_vec_spec(w), ANY] + [ANY] * ng,
        out_specs=[_row_spec(w)] + lay_specs + [_row_spec(w, 1)] + [ANY] * ng,
        out_shape=[jax.ShapeDtypeStruct((s, w), F32), jax.ShapeDtypeStruct((s, w), F32)]
        + [_class_shape(s, d, F32) for d in CLASS_DILS] + [jax.ShapeDtypeStruct(mixed.shape, mixed.dtype)]
        + _gathered_shapes(gather or []),
        scratch_shapes=[STAGE] + (_gather_sems(ng) if ng else []), input_output_aliases={2 * npat + 1: npat + 1},
        compiler_params=_cparams("arbitrary" if ng else "parallel"), name=name)(*outs, *lses, gb, mixed,
                                                                              *(gather or []))
    return res[0], dict(zip(DILATIONS, res[1:npat + 1])), res[npat + 1], list(res[npat + 2:])


def _attn_bwd_prep(dmixed, ob, gb, name):
    s = ob.shape[0]
    w = B_WIDTH
    nlay = len(DILATIONS)

    def body(dm_ref, ob_ref, g_ref, *rest):
        do_refs, dl_refs = rest[:nlay], rest[nlay:2 * nlay]
        dg_ref, stage = rest[2 * nlay:]
        _acc_init([dg_ref])
        ob = ob_ref[...]
        dob, dgt = _rms_bwd(ob, _rsq_mean(ob), g_ref[...], dm_ref[...])
        dg_ref[...] += _colsum(dgt)
        _stage_put(stage, dob)
        do_refs[0][...] = dob.astype(BF16)
        for ref, d in zip(do_refs[1:], CLASS_DILS):
            _store_classes(stage, ref, d)
        lo, hi = _head_masks()
        t = dob * ob
        for b, sl in enumerate(_lane_blocks(w)):
            tb = t[:, sl]
            s0 = jnp.sum(jnp.where(lo, tb, 0.0), axis=1, keepdims=True)
            s1 = jnp.sum(jnp.where(hi, tb, 0.0), axis=1, keepdims=True)
            stage[b] = jnp.where(lo, s0, s1)
        dl_refs[0][...] = _stage_get(stage)
        for ref, d in zip(dl_refs[1:], CLASS_DILS):
            _store_classes(stage, ref, d)

    lay_specs = [_row_spec(w)] + [_class_spec(d) for d in CLASS_DILS]
    shapes = lambda dt: [jax.ShapeDtypeStruct((s, w), dt)] + [_class_shape(s, d, dt) for d in CLASS_DILS]
    res = pl.pallas_call(
        body, grid=(s // TR,), in_specs=[_row_spec(w, 1), _row_spec(w), _vec_spec(w)],
        out_specs=lay_specs * 2 + [_vec_spec(w)],
        out_shape=shapes(BF16) + shapes(F32) + [jax.ShapeDtypeStruct((1, w), F32)],
        scratch_shapes=[STAGE],
        compiler_params=_cparams("arbitrary"), name=name)(dmixed, ob, gb)
    return dict(zip(DILATIONS, res[:nlay])), dict(zip(DILATIONS, res[nlay:2 * nlay])), res[2 * nlay]


def _rope_bwd(dqs, dks, dvs, halos, tabs, dproj, name):
    s = dproj.shape[0]
    half = ROT_DIM // 2
    scale = HEAD_DIM ** -0.5
    npat = len(DILATIONS)
    w = B_WIDTH
    nseg = halos[0].shape[0]
    per = s // nseg // TR

    def body(*refs):
        groups = [refs[g * npat:(g + 1) * npat] for g in range(3)]
        halo_refs = (None,) + tuple(refs[3 * npat:3 * npat + 2])
        c_ref, s1_ref, s2_ref, _, o_ref, stage = refs[3 * npat + 2:]
        i = pl.program_id(0)
        at_edge = ((i + 1) % per == 0) & ((i + 1) // per < nseg)

        def total(rs, halo_ref=None):
            acc = rs[0][...].astype(F32)
            if halo_ref is not None:
                edge = jnp.concatenate([jnp.zeros((TR - BAND, w), F32), halo_ref[...]], axis=0)
                acc = acc + jnp.where(at_edge, edge, 0.0)
            for ref, d in zip(rs[1:], CLASS_DILS):
                acc = acc + _load_classes(ref, stage, d)
            return acc

        def unrope(g):
            c, s1, s2 = c_ref[...], s1_ref[...], s2_ref[...]
            for sl in _lane_blocks(w):
                gb = g[:, sl]
                o = gb * c + pltpu.roll(gb * s1, half, 1) + pltpu.roll(gb * s2, LANES - half, 1)
                o_ref[:, sl] = o.astype(BF16)

        which = pl.program_id(1)

        @pl.when(which == 0)
        def _():
            unrope(total(groups[0]) * scale)

        @pl.when(which == 1)
        def _():
            unrope(total(groups[1], halo_refs[1]))

        @pl.when(which == 2)
        def _():
            o_ref[...] = total(groups[2], halo_refs[2]).astype(BF16)

    tab = pl.BlockSpec((TR, LANES), lambda i, j: (i, 0))
    nat = pl.BlockSpec((TR, w), lambda i, j: (i, 0))
    lay_specs = [nat] + [_class_spec(d) for d in CLASS_DILS]
    edge_spec = pl.BlockSpec((None, BAND, w), lambda i, j: (jnp.minimum((i + 1) // per, nseg - 1), 0, 0))
    first_col = 2 * A_WIDTH // w
    return pl.pallas_call(
        body, grid=(s // TR, 3), in_specs=lay_specs * 3 + [edge_spec] * 2 + [tab] * 3 + [ANY],
        out_specs=pl.BlockSpec((TR, w), lambda i, j: (i, first_col + j)),
        out_shape=jax.ShapeDtypeStruct(dproj.shape, dproj.dtype), scratch_shapes=[STAGE],
        input_output_aliases={3 * npat + 5: 0},
        compiler_params=_cparams("parallel", "arbitrary"), name=name)(*dqs, *dks, *dvs, *halos, *tabs, dproj)


TK = 512
HALO = 16


def _row_of(v, r):
    rows = lax.broadcasted_iota(jnp.int32, (v.shape[0], 1), 0)
    return jnp.sum(jnp.where(rows == r, v, 0.0), axis=0, keepdims=True)


def _taps_before(x, halo):
    row = lax.broadcasted_iota(jnp.int32, (x.shape[0], 1), 0)
    m1 = jnp.where(row == 0, _row_of(halo, HALO - 1), pltpu.roll(x, 1, 0))
    m2 = jnp.where(row == 0, _row_of(halo, HALO - 2), jnp.where(row == 1, _row_of(halo, HALO - 1), pltpu.roll(x, 2, 0)))
    return m2, m1, x


def _taps_after(x, halo):
    rows = x.shape[0]
    row = lax.broadcasted_iota(jnp.int32, (rows, 1), 0)
    p1 = jnp.where(row == rows - 1, _row_of(halo, 0), pltpu.roll(x, rows - 1, 0))
    p2 = jnp.where(row == rows - 2, _row_of(halo, 0), jnp.where(row == rows - 1, _row_of(halo, 1), pltpu.roll(x, rows - 2, 0)))
    return p1, p2


def _conv_value(taps, cw_ref, cb_ref, h):
    return cb_ref[h] + cw_ref[h, 0:1, :] * taps[0] + cw_ref[h, 1:2, :] * taps[1] + cw_ref[h, 2:3, :] * taps[2]


def _ffn_weight_specs(ncol):
    per_up = (2 * D_FF // N_CHIPS) // TK
    per_dn = (D_FF // N_CHIPS) // TK
    wg = pl.BlockSpec((None, None, D_MODEL, TK), lambda i, j: (j // per_up, 0, 0, j % per_up))
    wv = pl.BlockSpec((None, None, D_MODEL, TK), lambda i, j: ((j + ncol) // per_up, 0, 0, (j + ncol) % per_up))
    wd = pl.BlockSpec((None, None, TK, D_MODEL), lambda i, j: (j // per_dn, 0, j % per_dn, 0))
    cw = pl.BlockSpec((2, 3, TK), lambda i, j: (0, 0, j))
    cb = pl.BlockSpec((2, 1, TK), lambda i, j: (0, 0, j))
    return wg, wv, wd, cw, cb


def _ffn_forward(h2, w_up, w_down, cw3, cb3, name, gather=None, post=None):
    s = h2.shape[0]
    nm, ncol = s // TM, D_FF // TK
    ng = 0 if gather is None else len(gather)
    npost = 0 if post is None else 3
    nout = 4 + (2 if post else 0)

    def body(*refs):
        h_ref, wg_ref, wv_ref, wd_ref, cw_ref, cb_ref = refs[:6]
        post_in = refs[6:6 + npost]
        g_in = refs[6 + npost:6 + npost + ng]
        outs = refs[6 + npost + ng:6 + npost + ng + nout]
        y_ref, up_ref, cv_ref, f_ref = outs[:4]
        g_out = refs[6 + npost + ng + nout:6 + npost + 2 * ng + nout]
        carry, acc = refs[6 + npost + 2 * ng + nout:8 + npost + 2 * ng + nout]
        i, j = pl.program_id(0), pl.program_id(1)
        if ng:
            start, relay, finish = _gather_steps(g_in, g_out, *refs[8 + npost + 2 * ng + nout:])
            pl.when((i == 0) & (j == 0))(start)
            pl.when((i == nm - 1) & (j == 0))(relay)

        @pl.when((i == 0) & (j == 0))
        def _():
            carry[...] = jnp.zeros_like(carry)

        h = h_ref[...]
        conv = []
        for hh, w_ref in ((0, wg_ref), (1, wv_ref)):
            up = _dot(h, w_ref[...], NN).astype(BF16)
            up_ref[hh] = up
            x = up.astype(F32)
            conv.append(_conv_value(_taps_before(x, carry[j, hh]), cw_ref, cb_ref, hh))
            cv_ref[hh] = conv[hh].astype(BF16)
            carry[j, hh] = x[TM - HALO:, :]
        y = (_gelu_tanh(conv[0])[0] * conv[1]).astype(BF16)
        y_ref[...] = y
        part = _dot(y, wd_ref[...], NN)

        @pl.when(j == 0)
        def _():
            acc[...] = part

        @pl.when(j > 0)
        def _():
            acc[...] += part

        @pl.when(j == ncol - 1)
        def _():
            f = acc[...]
            f_ref[...] = f
            if post:
                x1_ref, gp_ref, gn_ref = post_in
                x2 = x1_ref[...] + f * _rsq_mean(f) * gp_ref[...]
                outs[4][...] = x2
                outs[5][...] = (x2 * _rsq_mean(x2) * gn_ref[...]).astype(BF16)

        if ng:
            pl.when((i == nm - 1) & (j == ncol - 1))(finish)

    wg, wv, wd, cw, cb = _ffn_weight_specs(ncol)
    row = pl.BlockSpec((TM, D_MODEL), lambda i, j: (i, 0))
    vec = pl.BlockSpec((1, D_MODEL), lambda i, j: (0, 0))
    res = pl.pallas_call(
        body, grid=(nm, ncol),
        in_specs=[row, wg, wv, wd, cw, cb] + ([row, vec, vec] if post else []) + [ANY] * ng,
        out_specs=[pl.BlockSpec((TM, TK), lambda i, j: (i, j)), pl.BlockSpec((2, TM, TK), lambda i, j: (0, i, j)),
                   pl.BlockSpec((2, TM, TK), lambda i, j: (0, i, j)), row] + ([row, row] if post else [])
        + [ANY] * ng,
        out_shape=[jax.ShapeDtypeStruct((s, D_FF), BF16), jax.ShapeDtypeStruct((2, s, D_FF), BF16),
                   jax.ShapeDtypeStruct((2, s, D_FF), BF16), jax.ShapeDtypeStruct((s, D_MODEL), F32)]
        + ([jax.ShapeDtypeStruct((s, D_MODEL), F32), jax.ShapeDtypeStruct((s, D_MODEL), BF16)] if post else [])
        + _gathered_shapes(gather or []),
        scratch_shapes=[pltpu.VMEM((ncol, 2, HALO, TK), F32), pltpu.VMEM((TM, D_MODEL), F32)]
        + (_gather_sems(ng) if ng else []),
        compiler_params=_cparams("arbitrary", "arbitrary"), name=name)(h2, w_up, w_up, w_down, cw3, cb3,
                                                                      *(post or []), *(gather or []))
    return res[:nout], list(res[nout:])


def _ffn_backward(df, w_up, w_down, up3, cv3, cw3, name, scatter=None):
    s = df.shape[0]
    nm, ncol = s // TM, D_FF // TK
    ns = 0 if scatter is None else len(scatter[0])

    def body(*refs):
        df_ref, wg_ref, wv_ref, wd_ref, cw_ref, up_ref, cv_ref = refs[:7]
        s_in = refs[7:7 + ns]
        dup_ref, dh_ref, sums_ref = refs[7 + ns:10 + ns]
        s_out = refs[10 + ns:10 + 2 * ns]
        carry, acc = refs[10 + 2 * ns:12 + 2 * ns]
        i, j = pl.program_id(0), pl.program_id(1)
        if ns:
            start, finish = _scatter_steps(s_in, s_out, *refs[12 + 2 * ns:], scatter[1])
            pl.when((i == 0) & (j == 0))(start)

        @pl.when((i == 0) & (j == 0))
        def _():
            carry[...] = jnp.zeros_like(carry)
            sums_ref[...] = jnp.zeros_like(sums_ref)

        dy = _dot(df_ref[...], wd_ref[...], NT)
        act, grad = _gelu_tanh(cv_ref[0].astype(F32))
        dcs = (dy * cv_ref[1].astype(F32) * grad, dy * act)
        row = lax.broadcasted_iota(jnp.int32, (8, 1), 0)
        part = None
        for hh, w_ref in ((0, wg_ref), (1, wv_ref)):
            dc = dcs[hh]
            x = up_ref[hh].astype(F32)
            after1, after2 = _taps_after(dc, carry[j, hh])
            upd = jnp.zeros((8, TK), F32)
            for ridx, sm in enumerate((_colsum(after2 * x), _colsum(after1 * x), _colsum(dc * x), _colsum(dc))):
                upd = jnp.where(row == ridx, sm, upd)
            sums_ref[j, hh] += upd
            dup = (cw_ref[hh, 2:3, :] * dc + cw_ref[hh, 1:2, :] * after1 + cw_ref[hh, 0:1, :] * after2).astype(BF16)
            carry[j, hh] = dc[:HALO, :]
            dup_ref[hh] = dup
            d = _dot(dup, w_ref[...], NT)
            part = d if part is None else part + d

        @pl.when(j == 0)
        def _():
            acc[...] = part

        @pl.when(j > 0)
        def _():
            acc[...] += part

        @pl.when(j == ncol - 1)
        def _():
            dh_ref[...] = acc[...]

        if ns:
            pl.when((i == nm - 1) & (j == ncol - 1))(finish)

    wg, wv, wd, cw, _ = _ffn_weight_specs(ncol)
    rev = lambda i: nm - 1 - i
    res = pl.pallas_call(
        body, grid=(nm, ncol),
        in_specs=[pl.BlockSpec((TM, D_MODEL), lambda i, j: (rev(i), 0)), wg, wv, wd, cw,
                  pl.BlockSpec((2, TM, TK), lambda i, j: (0, rev(i), j)),
                  pl.BlockSpec((2, TM, TK), lambda i, j: (0, rev(i), j))] + [ANY] * ns,
        out_specs=[pl.BlockSpec((2, TM, TK), lambda i, j: (0, rev(i), j)),
                   pl.BlockSpec((TM, D_MODEL), lambda i, j: (rev(i), 0)),
                   pl.BlockSpec((ncol, 2, 8, TK), lambda i, j: (0, 0, 0, 0))] + [ANY] * ns,
        out_shape=[jax.ShapeDtypeStruct((2, s, D_FF), BF16), jax.ShapeDtypeStruct((s, D_MODEL), F32),
                   jax.ShapeDtypeStruct((ncol, 2, 8, TK), F32)] + (_scattered_shapes(scatter[1]) if ns else []),
        scratch_shapes=[pltpu.VMEM((ncol, 2, HALO, TK), F32), pltpu.VMEM((TM, D_MODEL), F32)]
        + (_scatter_sems(ns) if ns else []),
        compiler_params=_cparams("arbitrary", "arbitrary"), name=name)(df, w_up, w_up, w_down, cw3, up3, cv3,
                                                                      *(scatter[0] if ns else []))
    return res[:3], list(res[3:])


def _wspec(rows, cols, index_map):
    return pl.BlockSpec((None, None, rows, cols), index_map)


def _layer_forward(l, x0, h1, p, wg, tabs, gather=None, late=None, g_next=None):
    s = x0.shape[0]
    nm = s // TMM
    tag = f"_l{l}"
    riders = dict.fromkeys(DILATIONS)
    proj_rider = rope_rider = combine_rider = None
    if late is not None:
        cols = lambda t, parts: [t[:, i * t.shape[1] // parts:(i + 1) * t.shape[1] // parts] for i in range(parts)]
        (down_a, down_b), up_q = cols(late["w_down"], 2), cols(late["w_up"], 4)
        proj_rider, rope_rider, combine_rider = [late["w_out"], down_a], [up_q[2]], [up_q[3]]
        riders = dict(zip(DILATIONS, ([down_b], [up_q[0]], [up_q[1]])))
    proj = _matmul(
        h1, wg["w_in"], grid=(nm, N_CHIPS), a_spec=pl.BlockSpec((TMM, D_MODEL), lambda i, j: (i, 0)),
        b_spec=_wspec(D_MODEL, IN_COLS // N_CHIPS, lambda i, j: (j, 0, 0, 0)),
        o_spec=pl.BlockSpec((TMM, IN_COLS // N_CHIPS), lambda i, j: (i, j)), o_shape=(s, IN_COLS), o_dtype=BF16,
        dims=NN, nk=1, kaxis=None, acc_shape=None, name="proj" + tag, gather=proj_rider)
    if late is not None:
        proj, (w_out_all4, down_a) = proj
    ma = _mixer_a_fwd(proj, p["v_norm_g"], p["v_norm_b"], p["w_spatial"], p["bs_full"], p["out_norm_a"],
                      "mixer_a_fwd" + tag)
    q, k, v, rope_landed = _rope_fwd(proj, tabs, "rope_fwd" + tag, rope_rider)
    outs, lses, landed = zip(*[
        _attn_fwd(_as_classes(q[d]), _as_classes(k[d]), _as_classes(v[d]), f"attn_fwd_d{d}" + tag, riders[d])
        for d in DILATIONS])
    outs = [o.reshape(s, B_WIDTH) if d == 1 else o for o, d in zip(outs, DILATIONS)]
    lses = [t.reshape(s, B_WIDTH) if d == 1 else t for t, d in zip(lses, DILATIONS)]
    ob, lse, mixed, combine_landed = _attn_combine(outs, lses, p["out_norm_b"], ma, "attn_combine" + tag,
                                                   combine_rider)
    if late is not None:
        wg = dict(wg, w_out=w_out_all4, w_down=jnp.concatenate([down_a, landed[0][0]], axis=-1),
                  w_up=jnp.concatenate([landed[1][0], landed[2][0], rope_landed[0], combine_landed[0]], axis=-1))
    y1, x1, h2 = _mix_out_norm(mixed, wg["w_out"], x0, p["post_mix_norm"], p["pre_ffn_norm"], "mix_out" + tag)
    post = None if g_next is None else (x1, p["post_ffn_norm"], g_next)
    (y, up3, cv3, f, *after), gathered = _ffn_forward(h2, wg["w_up"], wg["w_down"], p["cw3"], p["cb3"],
                                                      "ffn_fwd" + tag, gather, post)
    saved = dict(x0=x0, h1=h1, proj=proj, q=q, k=k, v=v, ob=ob, lse=lse, mixed=mixed, y1=y1, x1=x1, h2=h2,
                 up3=up3, cv3=cv3, y=y, f=f)
    if after:
        saved.update(x2=after[0], h_next=after[1])
    return saved, gathered, wg


def _layer_backward(l, dx2, df, sv, p, wg, tabs, pos, scatter=None, hide=False):
    s = dx2.shape[0]
    nm = s // TMM
    tag = f"_l{l}"
    g = {}
    (dup3, dh2, conv_sums), scattered = _ffn_backward(df, wg["w_up"], wg["w_down"], sv["up3"], sv["cv3"], p["cw3"],
                                                      "ffn_bwd" + tag, scatter)
    sums = conv_sums.transpose(1, 2, 0, 3).reshape(2, 8, D_FF)
    g["conv_w"] = jnp.concatenate([sums[0, :3], sums[1, :3]], axis=1)
    g["conv_b"] = jnp.concatenate([sums[0, 3:4], sums[1, 3:4]], axis=1)
    tn = 1024
    done = {}
    gw_down = _matmul(
        sv["y"], df, grid=(D_FF // tn,), a_spec=pl.BlockSpec((s, tn), lambda k: (0, k)),
        b_spec=pl.BlockSpec((s, D_MODEL), lambda k: (0, 0)),
        o_spec=pl.BlockSpec((2, tn, D_MODEL // 2), lambda k: (0, k, 0)),
        o_shape=(2, D_FF, D_MODEL // 2), o_dtype=BF16,
        dims=TN, nk=1, kaxis=None, acc_shape=None, name="w_down_grad" + tag, halves=True)
    down_sums = _chip_sums(l, dict(w_down=gw_down), pos, ("w_down",)) if hide else None
    gw_up = _matmul(
        sv["h2"], dup3, grid=(2 * D_FF // tn,), a_spec=pl.BlockSpec((s, D_MODEL), lambda n: (0, 0)),
        b_spec=pl.BlockSpec((None, s, tn), lambda n: (n // (D_FF // tn), 0, n % (D_FF // tn))),
        o_spec=pl.BlockSpec((None, D_MODEL, tn), lambda n: (n // 2, 0, n % 2)),
        o_shape=(N_CHIPS, D_MODEL, 2 * D_FF // N_CHIPS), o_dtype=BF16,
        dims=TN, nk=1, kaxis=None, acc_shape=None, name="w_up_grad" + tag,
        scatter=(down_sums, ("w_down",)) if hide else None)
    up_sums = None
    if hide:
        gw_up, received = gw_up
        done[("w_down",)] = (down_sums, received)
        up_sums = _chip_sums(l, dict(w_up=gw_up), pos, ("w_up",))
    dx1, dy1, g["pre_ffn_norm"], g["post_mix_norm"] = _norm_bwd_mid(
        dx2, dh2, sv["x1"], p["pre_ffn_norm"], sv["y1"], p["post_mix_norm"], "norm_bwd_mid" + tag)
    w_out_all = pl.BlockSpec((N_CHIPS, None, D_MODEL // N_CHIPS, D_MODEL), lambda i: (0, 0, 0, 0))
    dmixed = _matmul(
        dy1, wg["w_out"], grid=(nm,), a_spec=pl.BlockSpec((TMM, D_MODEL), lambda i: (i, 0)), b_spec=w_out_all,
        o_spec=pl.BlockSpec((TMM, D_MODEL), lambda i: (i, 0)), o_shape=(s, D_MODEL), o_dtype=F32,
        dims=NT, nk=1, kaxis=None, acc_shape=None, name="mix_out_bwd" + tag, b_2d=(D_MODEL, D_MODEL))
    gw_out = _matmul(
        sv["mixed"], dy1, grid=(1,), a_spec=pl.BlockSpec((s, D_MODEL), lambda m: (0, 0)),
        b_spec=pl.BlockSpec((s, D_MODEL), lambda m: (0, 0)),
        o_spec=pl.BlockSpec((2, D_MODEL, D_MODEL // 2), lambda m: (0, 0, 0)),
        o_shape=(2, D_MODEL, D_MODEL // 2), o_dtype=BF16,
        dims=TN, nk=1, kaxis=None, acc_shape=None, name="w_out_grad" + tag, halves=True)
    out_sums = _chip_sums(l, dict(w_out=gw_out), pos, ("w_out",)) if hide else None
    dpa, g["out_norm_a"], g["v_norm_g"], g["v_norm_b"], dbs, g["w_spatial"], received = _mixer_a_bwd(
        sv["proj"], dmixed, p["v_norm_g"], p["v_norm_b"], p["w_spatial"], p["bs_full"], p["out_norm_a"],
        "mixer_a_bwd" + tag, (out_sums, ("w_out",)) if hide else None)
    if hide:
        done[("w_out",)] = (out_sums, received)
    g["b_spatial"] = dbs[:, ::GROUP_DIM].T
    dob, delta, g["out_norm_b"] = _attn_bwd_prep(dmixed, sv["ob"], p["out_norm_b"], "attn_bwd_prep" + tag)
    halves = dict(zip(DILATIONS, ("w_up:0", "w_up:1"))) if hide else {}
    dqs, dks, dvs, edges, received = zip(*[
        _attn_bwd(*(_as_classes(t[d]) for t in (sv["q"], sv["k"], sv["v"], dob, sv["lse"], delta)),
                  f"attn_bwd_d{d}" + tag, (up_sums, (halves[d],)) if d in halves else None)
        for d in DILATIONS])
    if hide:
        done[("w_up",)] = (up_sums, [jnp.concatenate([received[0][0], received[1][0]], axis=-1)])
    nat = lambda ts: [t.reshape(s, B_WIDTH) if d == 1 else t for t, d in zip(ts, DILATIONS)]
    halos = [t[0] for t in edges[0]]
    dproj = _rope_bwd(nat(dqs), nat(dks), nat(dvs), halos, tabs, dpa, "rope_bwd" + tag)
    wcol = IN_COLS // N_CHIPS
    gw_in = _matmul(
        sv["h1"], dproj, grid=(N_CHIPS,), a_spec=pl.BlockSpec((s, D_MODEL), lambda n: (0, 0)),
        b_spec=pl.BlockSpec((s, wcol), lambda n: (0, n)),
        o_spec=pl.BlockSpec((None, D_MODEL, wcol), lambda n: (n, 0, 0)),
        o_shape=(N_CHIPS, D_MODEL, wcol), o_dtype=BF16,
        dims=TN, nk=1, kaxis=None, acc_shape=None, name="w_in_grad" + tag)
    in_sums = _chip_sums(l, dict(w_in=gw_in), pos, ("w_in",)) if hide else None
    dh1, received = _proj_bwd(dproj, wg["w_in"], "proj_bwd" + tag, (in_sums, ("w_in",)) if hide else None)
    if hide:
        done[("w_in",)] = (in_sums, received)
    big = {} if hide else dict(w_in=gw_in, w_up=gw_up, w_out=gw_out, w_down=gw_down)
    return dx1, dh1, big, g, scattered, done


SMALL = ("pre_mix_norm", "v_norm_g", "v_norm_b", "w_spatial", "b_spatial", "out_norm_a", "out_norm_b",
         "post_mix_norm", "pre_ffn_norm", "conv_b", "post_ffn_norm")
BIG = ("w_in", "w_out", "w_up", "w_down")
DEPTH = 2


def _layer_params(l, small, conv_w_full):
    p = {n: small[n][l].reshape(1, -1) for n in SMALL if n not in ("w_spatial", "b_spatial")}
    p["w_spatial"] = small["w_spatial"][l]
    p["bs_full"] = jnp.repeat(small["b_spatial"][l].T, GROUP_DIM, axis=1)
    p["cw3"] = conv_w_full[l].reshape(3, 2, D_FF).transpose(1, 0, 2)
    p["cb3"] = small["conv_b"][l].reshape(2, 1, D_FF)
    return p


def _mesh_pos():
    return lax.axis_index("x"), lax.axis_index("y"), lax.axis_index("c")


def _other_chips(x, y):
    return [(1 - x, y), (x, 1 - y), (1 - x, 1 - y)]


def _gathered_shapes(blocks):
    return [jax.ShapeDtypeStruct((N_CHIPS, 1) + a.shape, a.dtype) for a in blocks]


def _gather_sems(nw):
    n = 2 * nw * (N_CHIPS - 1) + nw
    return [pltpu.SemaphoreType.DMA((n,)), pltpu.SemaphoreType.DMA((n,))]


def _gather_steps(ins, outs, send, recv):
    nw, nrel = len(ins), N_CHIPS - 1
    x, y, c = _mesh_pos()
    mine, sibling, chips = 2 * x + y, (x, y, 1 - c), _other_chips(x, y)

    def copy(src, dst, slot, to):
        return pltpu.make_async_remote_copy(src_ref=src, dst_ref=dst, send_sem=send.at[slot],
                                            recv_sem=recv.at[slot], device_id=to, device_id_type=MESH)

    def half_rows(t, core):
        rows = ins[t].shape[0] // 2
        return pl.ds(pl.multiple_of(core * rows, rows), rows)

    def landing(t, chip, core):
        return outs[t].at[chip, 0, half_rows(t, core), :]

    slots = [(t, r, chip) for t in range(nw) for r, chip in enumerate(chips)]
    own = [copy(ins[t], outs[t].at[mine, 0], 2 * nw * nrel + t, sibling) for t in range(nw)]
    first = [copy(ins[t].at[half_rows(t, c), :], landing(t, mine, c), t * nrel + r, (px, py, c))
             for t, r, (px, py) in slots]
    relays = [copy(landing(t, 2 * px + py, c), landing(t, 2 * px + py, c), nw * nrel + t * nrel + r, sibling)
              for t, r, (px, py) in slots]

    def start():
        for cp in own + first:
            cp.start()

    def relay():
        for (t, r, (px, py)), cp in zip(slots, relays):
            copy(landing(t, 2 * px + py, c), landing(t, 2 * px + py, c), t * nrel + r, (px, py, c)).wait_recv()
            cp.start()

    def finish():
        for t, r, (px, py) in slots:
            passed = landing(t, 2 * px + py, 1 - c)
            copy(passed, passed, nw * nrel + t * nrel + r, sibling).wait_recv()
        for cp in first + relays:
            cp.wait_send()
        for cp in own:
            cp.wait()

    return start, relay, finish


def _gather_weights(blocks, name):
    nw = len(blocks)

    def body(*refs):
        start, relay, finish = _gather_steps(refs[:nw], refs[nw:2 * nw], *refs[2 * nw:])
        start()
        relay()
        finish()

    return pl.pallas_call(
        body, in_specs=[ANY] * nw, out_specs=[ANY] * nw, out_shape=_gathered_shapes(blocks),
        scratch_shapes=_gather_sems(nw), name=name)(*blocks)


HALF = 512

GRAD_GEOM = {"w_in": ("rows", D_MODEL, IN_COLS // N_CHIPS), "w_up": ("rows", D_MODEL, 2 * D_FF // N_CHIPS),
             "w_out": ("cols", D_MODEL, D_MODEL // N_CHIPS), "w_down": ("cols", D_FF, D_FF // N_CHIPS)}


def _exchange_shape(n):
    kind, a, b = GRAD_GEOM[n]
    return (N_CHIPS, HALF, b) if kind == "rows" else (a, HALF)


def _piece_shape(n):
    name, _, part = n.partition(":")
    kind, _, b = GRAD_GEOM[name]
    if part:
        assert kind == "rows"
        return (HALF, b // 2)
    return (HALF, b) if kind == "rows" else (b, HALF)


def _half_of(ref, n, core):
    if GRAD_GEOM[n][0] == "rows":
        return ref.at[:, pl.ds(pl.multiple_of(core * HALF, HALF), HALF), :]
    return ref.at[core]


def _piece_of(ref, n, chip):
    name, _, part = n.partition(":")
    kind, _, b = GRAD_GEOM[name]
    if part:
        return ref.at[chip, :, pl.ds(int(part) * (b // 2), b // 2)]
    return ref.at[chip] if kind == "rows" else ref.at[pl.ds(pl.multiple_of(chip * b, b), b), :]


def _pair_exchange(g, names, name):
    n = len(names)

    def body(*refs):
        send, recv = refs[2 * n:]
        x, y, c = _mesh_pos()
        o = 1 - c
        cps = [pltpu.make_async_remote_copy(src_ref=_half_of(refs[t], nm, o), dst_ref=refs[n + t], send_sem=send.at[t],
                                            recv_sem=recv.at[t], device_id=(x, y, o), device_id_type=MESH)
               for t, nm in enumerate(names)]
        for cp in cps:
            cp.start()
        for cp in cps:
            cp.wait()

    return pl.pallas_call(
        body, in_specs=[ANY] * n, out_specs=[ANY] * n,
        out_shape=[jax.ShapeDtypeStruct(_exchange_shape(nm), BF16) for nm in names],
        scratch_shapes=[pltpu.SemaphoreType.DMA((n,)), pltpu.SemaphoreType.DMA((n,))],
        name=name)(*[g[nm] for nm in names])


def _pair_sum(g, recv, pos, names, name_prefix):
    def add(a, b, grid, a_spec, b_spec, name):
        def body(pos_ref, a_ref, b_ref, o_ref):
            o_ref[...] = (a_ref[...].astype(F32) + b_ref[...].astype(F32)).astype(BF16)

        return pl.pallas_call(
            body, grid_spec=pltpu.PrefetchScalarGridSpec(
                num_scalar_prefetch=1, grid=grid, in_specs=[a_spec, b_spec], out_specs=b_spec),
            out_shape=jax.ShapeDtypeStruct(b.shape, BF16), compiler_params=_cparams("parallel"), name=name)(pos, a, b)

    out = []
    for nm, r in zip(names, recv):
        kind, rows, width = GRAD_GEOM[nm]
        if kind == "rows":
            out.append(add(g[nm], r, (N_CHIPS,), pl.BlockSpec((None, HALF, width), lambda j, pos: (j, pos[2], 0)),
                           pl.BlockSpec((None, HALF, width), lambda j, pos: (j, 0, 0)), f"{name_prefix}_{nm}"))
        else:
            out.append(add(g[nm], r, (rows // D_MODEL,), pl.BlockSpec((None, D_MODEL, HALF), lambda j, pos: (pos[2], j, 0)),
                           pl.BlockSpec((D_MODEL, HALF), lambda j, pos: (j, 0)), f"{name_prefix}_{nm}"))
    return out


def _scattered_shapes(names):
    return [jax.ShapeDtypeStruct((N_CHIPS - 1,) + _piece_shape(nm), BF16) for nm in names]


def _scatter_sems(n):
    return [pltpu.SemaphoreType.DMA((n * (N_CHIPS - 1),)), pltpu.SemaphoreType.DMA((n * (N_CHIPS - 1),))]


def _scatter_steps(sums, outs, send, recv, names):
    nrel = N_CHIPS - 1
    x, y, c = _mesh_pos()
    cps = []
    for r, (px, py) in enumerate(_other_chips(x, y)):
        for t, nm in enumerate(names):
            cps.append(pltpu.make_async_remote_copy(
                src_ref=_piece_of(sums[t], nm, 2 * px + py), dst_ref=outs[t].at[r], send_sem=send.at[t * nrel + r],
                recv_sem=recv.at[t * nrel + r], device_id=(px, py, c), device_id_type=MESH))

    def start():
        for cp in cps:
            cp.start()

    def finish():
        for cp in cps:
            cp.wait()

    return start, finish


def _chip_scatter(sums, names, name):
    n = len(names)

    def body(*refs):
        start, finish = _scatter_steps(refs[:n], refs[n:2 * n], *refs[2 * n:], names)
        start()
        finish()

    return pl.pallas_call(
        body, in_specs=[ANY] * n, out_specs=[ANY] * n, out_shape=_scattered_shapes(names),
        scratch_shapes=_scatter_sems(n), name=name)(*sums)


def _chip_sum(sums, recv, pos, names, name_prefix):
    def add(a, b, a_spec, shape, name):
        def body(pos_ref, a_ref, b_ref, o_ref):
            tot = a_ref[...].astype(F32)
            for r in range(N_CHIPS - 1):
                tot = tot + b_ref[r].astype(F32)
            o_ref[...] = tot

        return pl.pallas_call(
            body, grid_spec=pltpu.PrefetchScalarGridSpec(
                num_scalar_prefetch=1, grid=(1,), in_specs=[a_spec, pl.BlockSpec(b.shape, lambda i, pos: (0, 0, 0))],
                out_specs=pl.BlockSpec((None,) + shape, lambda i, pos: (pos[2], 0, 0))),
            out_shape=jax.ShapeDtypeStruct((2,) + shape, F32), compiler_params=_cparams("arbitrary"),
            name=name)(pos, a, b)

    chip = lambda pos: 2 * pos[0] + pos[1]
    out = []
    for nm, a, b in zip(names, sums, recv):
        shape = _piece_shape(nm)
        if GRAD_GEOM[nm][0] == "rows":
            spec = pl.BlockSpec((None,) + shape, lambda i, pos: (chip(pos), 0, 0))
        else:
            spec = pl.BlockSpec(shape, lambda i, pos: (chip(pos), 0))
        out.append(add(a, b, spec, shape, f"{name_prefix}_{nm}"))
    return out


def _pair_share(totals, name):
    n = len(totals)

    def body(*refs):
        ins, outs = refs[:n], refs[n:2 * n]
        send, recv = refs[2 * n:]
        x, y, c = _mesh_pos()
        o = 1 - c
        cps = [pltpu.make_async_remote_copy(src_ref=ins[t].at[c], dst_ref=outs[t].at[c], send_sem=send.at[t],
                                            recv_sem=recv.at[t], device_id=(x, y, o), device_id_type=MESH)
               for t in range(n)]
        for cp in cps:
            cp.start()
        for t in range(n):
            pltpu.make_async_remote_copy(src_ref=ins[t].at[o], dst_ref=outs[t].at[o], send_sem=send.at[t],
                                         recv_sem=recv.at[t], device_id=(x, y, o), device_id_type=MESH).wait_recv()
        for cp in cps:
            cp.wait_send()

    return pl.pallas_call(
        body, in_specs=[ANY] * n, out_specs=[ANY] * n,
        out_shape=[jax.ShapeDtypeStruct(t.shape, t.dtype) for t in totals],
        scratch_shapes=[pltpu.SemaphoreType.DMA((n,)), pltpu.SemaphoreType.DMA((n,))],
        input_output_aliases={t: t for t in range(n)}, name=name)(*totals)


def _chip_sums(l, g, pos, names):
    tag = f"l{l}_" + "_".join(names)
    recv = _pair_exchange(g, names, "pair_exchange_" + tag)
    return _pair_sum(g, recv, pos, names, "pair_sum_" + tag)


def _gradient_shards(l, sums, scattered, pos, names):
    tag = f"l{l}_" + "_".join(names)
    halves = _pair_share(_chip_sum(sums, scattered, pos, names, "chip_sum_" + tag), "pair_share_" + tag)
    out = {}
    for nm, t in zip(names, halves):
        rows, cols = _piece_shape(nm)
        out[nm] = t.reshape(2 * rows, cols) if GRAD_GEOM[nm][0] == "rows" else t.transpose(1, 0, 2).reshape(rows, 2 * cols)
    return out


N_DEV = 8


def _allreduce_small(packed, name):
    rows = packed.shape[0]

    def body(x_ref, out_ref, gath, send_sems, recv_sems, local_sem):
        x, y, c = _mesh_pos()
        me, sibling = (x, y, c), (x, y, 1 - c)
        chips = _other_chips(x, y)

        def blk(px, py, pc):
            return gath.at[pl.ds(pl.multiple_of((4 * px + 2 * py + pc) * rows, 8), rows), :]

        def copy(k, block, to, src=None):
            return pltpu.make_async_remote_copy(
                src_ref=blk(*block) if src is None else src, dst_ref=blk(*block), send_sem=send_sems.at[k],
                recv_sem=recv_sems.at[k], device_id=to, device_id_type=MESH)

        mine = pltpu.make_async_copy(x_ref, blk(*me), local_sem)
        mine.start()
        first = [copy(0, me, sibling, src=x_ref)]
        first += [copy(1 + j, me, (*chip, c), src=x_ref) for j, chip in enumerate(chips)]
        for cp in first:
            cp.start()
        passed = [copy(4 + j, (*chip, c), sibling) for j, chip in enumerate(chips)]
        for j, chip in enumerate(chips):
            copy(1 + j, (*chip, c), me).wait_recv()
            passed[j].start()
        copy(0, sibling, me).wait_recv()
        for j, chip in enumerate(chips):
            copy(4 + j, (*chip, 1 - c), me).wait_recv()
        for cp in first + passed:
            cp.wait_send()
        mine.wait()
        tot = gath[0:rows, :]
        for d in range(1, N_DEV):
            tot = tot + gath[d * rows:(d + 1) * rows, :]
        out_ref[...] = tot

    vmem = pl.BlockSpec(memory_space=pltpu.VMEM)
    return pl.pallas_call(
        body, in_specs=[vmem], out_specs=vmem, out_shape=jax.ShapeDtypeStruct((rows, LANES), F32),
        scratch_shapes=[pltpu.VMEM((N_DEV * rows, LANES), F32), pltpu.SemaphoreType.DMA((7,)),
                        pltpu.SemaphoreType.DMA((7,)), pltpu.SemaphoreType.DMA],
        compiler_params=pltpu.CompilerParams(vmem_limit_bytes=VMEM_LIMIT_BYTES),
        name=name)(packed)


def _adamw(w, g, m, v, name):
    rows, cols = w.shape
    tr = 256 if rows % 256 == 0 else rows

    def body(w_ref, g_ref, m_ref, v_ref, d_ref, mo_ref, vo_ref):
        gv = g_ref[...]
        mn = ADAM_B1 * m_ref[...] + (1.0 - ADAM_B1) * gv
        vn = ADAM_B2 * v_ref[...] + (1.0 - ADAM_B2) * (gv * gv)
        m_hat = mn / (1.0 - ADAM_B1 ** ADAM_STEP)
        v_hat = vn / (1.0 - ADAM_B2 ** ADAM_STEP)
        d_ref[...] = -ADAM_LR * (m_hat / (jnp.sqrt(v_hat) + ADAM_EPS) + ADAM_WD * w_ref[...])
        mo_ref[...] = mn
        vo_ref[...] = vn

    spec = pl.BlockSpec((tr, cols), lambda i: (i, 0))
    return pl.pallas_call(
        body, grid=(rows // tr,), in_specs=[spec] * 4, out_specs=[spec] * 3,
        out_shape=[jax.ShapeDtypeStruct((rows, cols), F32)] * 3, compiler_params=_cparams("parallel"),
        name=name)(w, g, m, v)


def _adamw_nd(w, g, m, v, name):
    cols = w.shape[-1] if w.shape[-1] % LANES == 0 else LANES
    outs = _adamw(*(t.reshape(-1, cols) for t in (w, g, m, v)), name)
    return tuple(t.reshape(w.shape) for t in outs)


def _pack(arrays):
    return jnp.concatenate([a.reshape(-1, LANES) for a in arrays], axis=0)


def _unpack(packed, shapes):
    out, row = [], 0
    for sh in shapes:
        n = math.prod(sh) // LANES
        out.append(packed[row:row + n].reshape(sh))
        row += n
    return out


WEIGHTS = ("pre_mix_norm", "w_in", "v_norm_g", "v_norm_b", "w_spatial", "b_spatial", "out_norm_a", "out_norm_b",
           "w_out", "post_mix_norm", "pre_ffn_norm", "w_up", "conv_w", "conv_b", "w_down", "post_ffn_norm")


def kernel(x, pre_mix_norm, w_in, v_norm_g, v_norm_b, w_spatial, b_spatial, out_norm_a, out_norm_b, w_out, post_mix_norm, pre_ffn_norm, w_up, conv_w, conv_b, w_down, post_ffn_norm, loss_target, m_pre_mix_norm, m_w_in, m_v_norm_g, m_v_norm_b, m_w_spatial, m_b_spatial, m_out_norm_a, m_out_norm_b, m_w_out, m_post_mix_norm, m_pre_ffn_norm, m_w_up, m_conv_w, m_conv_b, m_w_down, m_post_ffn_norm, v_pre_mix_norm, v_w_in, v_v_norm_g, v_v_norm_b, v_w_spatial, v_b_spatial, v_out_norm_a, v_out_norm_b, v_w_out, v_post_mix_norm, v_pre_ffn_norm, v_w_up, v_conv_w, v_conv_b, v_w_down, v_post_ffn_norm):
    w = dict(pre_mix_norm=pre_mix_norm, w_in=w_in, v_norm_g=v_norm_g, v_norm_b=v_norm_b, w_spatial=w_spatial,
             b_spatial=b_spatial, out_norm_a=out_norm_a, out_norm_b=out_norm_b, w_out=w_out,
             post_mix_norm=post_mix_norm, pre_ffn_norm=pre_ffn_norm, w_up=w_up, conv_w=conv_w, conv_b=conv_b,
             w_down=w_down, post_ffn_norm=post_ffn_norm)
    m = dict(pre_mix_norm=m_pre_mix_norm, w_in=m_w_in, v_norm_g=m_v_norm_g, v_norm_b=m_v_norm_b,
             w_spatial=m_w_spatial, b_spatial=m_b_spatial, out_norm_a=m_out_norm_a, out_norm_b=m_out_norm_b,
             w_out=m_w_out, post_mix_norm=m_post_mix_norm, pre_ffn_norm=m_pre_ffn_norm, w_up=m_w_up,
             conv_w=m_conv_w, conv_b=m_conv_b, w_down=m_w_down, post_ffn_norm=m_post_ffn_norm)
    v = dict(pre_mix_norm=v_pre_mix_norm, w_in=v_w_in, v_norm_g=v_v_norm_g, v_norm_b=v_v_norm_b,
             w_spatial=v_w_spatial, b_spatial=v_b_spatial, out_norm_a=v_out_norm_a, out_norm_b=v_out_norm_b,
             w_out=v_w_out, post_mix_norm=v_post_mix_norm, pre_ffn_norm=v_pre_ffn_norm, w_up=v_w_up,
             conv_w=v_conv_w, conv_b=v_conv_b, w_down=v_w_down, post_ffn_norm=v_post_ffn_norm)
    pos = jnp.stack([lax.axis_index("x"), lax.axis_index("y"), lax.axis_index("c")]).astype(jnp.int32)
    chip = 2 * lax.axis_index("x") + lax.axis_index("y")

    cw_cols = conv_w.shape[-1]
    blocks = [{n: w[n][l].astype(BF16) for n in BIG} for l in range(DEPTH)]
    w_in0, cw_all = _gather_weights([blocks[0]["w_in"], conv_w.reshape(-1, LANES)], "gather_w_in_l0")
    wg = dict(w_in=w_in0)
    conv_w_full = cw_all.reshape(N_CHIPS, DEPTH, 3, cw_cols).transpose(1, 2, 0, 3).reshape(DEPTH, 3, 2 * D_FF)

    small = {n: w[n] for n in SMALL}
    xs, target = x[0], loss_target[0]
    tabs = _rope_tables(xs.shape[0])
    params = [_layer_params(l, small, conv_w_full) for l in range(DEPTH)]
    saved, wgs = [], []
    xin = xs
    h = _rms_cast(xin, params[0]["pre_mix_norm"], "pre_mix_l0")
    for l in range(DEPTH):
        sv, gathered, wg = _layer_forward(l, xin, h, params[l], wg, tabs,
                                          [blocks[l + 1][n] for n in BIG] if l + 1 < DEPTH else None,
                                          blocks[0] if l == 0 else None,
                                          params[l + 1]["pre_mix_norm"] if l + 1 < DEPTH else None)
        saved.append(sv)
        wgs.append(wg)
        if l + 1 < DEPTH:
            wg = dict(zip(BIG, gathered))
            xin, h = sv["x2"], sv["h_next"]
    loss_part, dx, df, g_post = _loss_norm_bwd(saved[-1]["x1"], saved[-1]["f"], params[-1]["post_ffn_norm"], target,
                                               "loss")
    smalls, shards = [None] * DEPTH, [{} for _ in range(DEPTH)]
    pending = None
    for l in reversed(range(DEPTH)):
        dx1, dh1, big, smalls[l], scattered, done = _layer_backward(l, dx, df, saved[l], params[l], wgs[l], tabs, pos,
                                                                    pending[1:] if pending else None, hide=l == 0)
        smalls[l]["post_ffn_norm"] = g_post
        if l > 0:
            dx, smalls[l]["pre_mix_norm"], df, g_post = _norm_bwd_in_out(
                dx1, dh1, saved[l]["x0"], params[l]["pre_mix_norm"], saved[l - 1]["f"], params[l - 1]["post_ffn_norm"],
                f"norm_bwd_in_out_l{l}")
        else:
            dx, smalls[l]["pre_mix_norm"] = _norm_bwd_in(dx1, dh1, saved[l]["x0"], params[l]["pre_mix_norm"],
                                                         "norm_bwd_in_l0")
        if pending:
            shards[pending[0]].update(_gradient_shards(pending[0], pending[1], scattered, pos, pending[2]))
        for names, (sums, received) in done.items():
            shards[l].update(_gradient_shards(l, sums, received, pos, names))
        names = tuple(big)
        pending = (l, _chip_sums(l, big, pos, names), names) if names else None
    if pending:
        shards[pending[0]].update(_gradient_shards(
            pending[0], pending[1], _chip_scatter(pending[1], pending[2], f"chip_scatter_l{pending[0]}"), pos,
            pending[2]))

    small_shapes = [w[n].shape for n in SMALL]
    stacked = [jnp.stack([smalls[l][n].reshape(w[n].shape[1:]) for l in range(DEPTH)]) for n in SMALL]
    cw_grad = jnp.stack([smalls[l]["conv_w"] for l in range(DEPTH)])
    packed = _pack(stacked + [cw_grad, loss_part])
    total = _allreduce_small(packed, "allreduce_small")
    parts = _unpack(total, small_shapes + [cw_grad.shape, (8, LANES)])
    g_small = dict(zip(SMALL, parts[:len(SMALL)]))
    loss = parts[-1][0, 0]
    g_conv_w = lax.dynamic_slice(parts[-2], (0, 0, chip * cw_cols), conv_w.shape)

    grads = {n: jnp.stack([shards[l][n] for l in range(DEPTH)]) for n in BIG}
    grads.update(g_small)
    grads["conv_w"] = g_conv_w

    dp, mp, vp = _adamw(_pack([w[n] for n in SMALL]), _pack([g_small[n] for n in SMALL]),
                        _pack([m[n] for n in SMALL]), _pack([v[n] for n in SMALL]), "adamw_small")
    delta = dict(zip(SMALL, _unpack(dp, small_shapes)))
    new_m = dict(zip(SMALL, _unpack(mp, small_shapes)))
    new_v = dict(zip(SMALL, _unpack(vp, small_shapes)))
    for n in BIG + ("conv_w",):
        delta[n], new_m[n], new_v[n] = _adamw_nd(w[n], grads[n], m[n], v[n], "adamw_" + n)

    return (loss, dx[None], *[grads[n] for n in WEIGHTS], *[delta[n] for n in WEIGHTS],
            *[new_m[n] for n in WEIGHTS], *[new_v[n] for n in WEIGHTS])
```

```python
import functools
import math

import jax
import jax.numpy as jnp
import numpy as np
from jax import lax
from jax.experimental import pallas as pl
from jax.experimental.pallas import tpu as pltpu

F32 = jnp.float32
BF16 = jnp.bfloat16
MESH = pl.DeviceIdType.MESH

D_MODEL = 1024
A_WIDTH = 512
A_GROUPS = 4
GROUP_DIM = 128
CHUNK = 128
B_WIDTH = 512
HEAD_DIM = 64
ROT_DIM = 16
ROPE_THETA = 500000.0
DILATIONS = (1, 4, 16)
BAND = 128
IN_COLS = 2560
D_FF = 4096
EPS = 1e-6
NEG_INF = -1e30
N_CHIPS = 4
LANES = 128

ADAM_LR = 0.001
ADAM_B1 = 0.9
ADAM_B2 = 0.999
ADAM_EPS = 1e-08
ADAM_WD = 0.01
ADAM_STEP = 10

VMEM_LIMIT_BYTES = 56 * 1024 * 1024
RSQRT2 = 0.7071067811865476
INV_SQRT_2PI = 0.3989422804014327
GELU_C = 0.7978845608028654
GELU_A = 0.044715

ANY = pl.BlockSpec(memory_space=pl.ANY)
NN = ((1,), (0,))
NT = ((1,), (1,))
TN = ((0,), (0,))


def _cparams(*sem):
    return pltpu.CompilerParams(dimension_semantics=sem, vmem_limit_bytes=VMEM_LIMIT_BYTES)


def _dot(a, b, dims):
    return lax.dot_general(a, b, (dims, ((), ())), preferred_element_type=F32)


def _rsq_mean(a):
    return lax.rsqrt(jnp.mean(a * a, axis=-1, keepdims=True) + EPS)


def _rms_bwd(a, r, g, dz):
    t = dz * g
    da = r * t - a * (r * r * r) * jnp.mean(t * a, axis=-1, keepdims=True)
    return da, dz * a * r


def _colsum(a):
    return jnp.sum(a, axis=0, keepdims=True)


def _gelu_tanh(x):
    u = x * x
    t = jnp.tanh(x * (GELU_C + (GELU_C * GELU_A) * u))
    hx = 0.5 * x
    act = hx + hx * t
    grad = 0.5 + 0.5 * t + (hx - hx * t * t) * (GELU_C + (3.0 * GELU_C * GELU_A) * u)
    return act, grad


def _grid_edges(grid):
    ids = [pl.program_id(ax) for ax in range(len(grid))]
    first = functools.reduce(jnp.logical_and, [i == 0 for i in ids])
    last = functools.reduce(jnp.logical_and, [i == n - 1 for i, n in zip(ids, grid)])
    return first, last


def _matmul(a, b, *, grid, a_spec, b_spec, o_spec, o_shape, o_dtype, dims, nk, kaxis, acc_shape, name, b_2d=None,
            halves=False, scatter=None, gather=None):
    assert scatter is None or gather is None
    ns = len(scatter[0]) if scatter else len(gather) if gather else 0

    def body(*refs):
        a_ref, b_ref = refs[:2]
        o_ref = refs[2 + ns]
        scratch = refs[3 + 2 * ns:]
        if ns:
            first, last = _grid_edges(grid)
            if scatter:
                start, finish = _scatter_steps(refs[2:2 + ns], refs[3 + ns:3 + 2 * ns], scratch[-2], scratch[-1],
                                               scatter[1])
            else:
                start, relay, last_wait = _gather_steps(refs[2:2 + ns], refs[3 + ns:3 + 2 * ns], scratch[-2],
                                                        scratch[-1])

                def finish():
                    relay()
                    last_wait()
            pl.when(first)(start)
        def store(val):
            if halves:
                half = val.shape[1] // 2
                o_ref[0] = val[:, :half].astype(o_dtype)
                o_ref[1] = val[:, half:].astype(o_dtype)
            else:
                o_ref[...] = val.astype(o_dtype)

        bv = b_ref[...] if b_2d is None else b_ref[...].reshape(b_2d)
        part = _dot(a_ref[...], bv, dims)
        if nk == 1:
            store(part)
        else:
            acc = scratch[0]
            k = pl.program_id(kaxis)

            @pl.when(k == 0)
            def _():
                acc[...] = part

            @pl.when(k > 0)
            def _():
                acc[...] += part

            @pl.when(k == nk - 1)
            def _():
                store(acc[...])

        if ns:
            pl.when(last)(finish)

    sem = tuple("arbitrary" if (ns or (nk > 1 and ax == kaxis)) else "parallel" for ax in range(len(grid)))
    riding = list(scatter[0]) if scatter else list(gather or [])
    rider_shapes = _scattered_shapes(scatter[1]) if scatter else _gathered_shapes(riding)
    rider_sems = _scatter_sems(ns) if scatter else _gather_sems(ns) if gather else []
    res = pl.pallas_call(
        body, grid=grid, in_specs=[a_spec, b_spec] + [ANY] * ns, out_specs=[o_spec] + [ANY] * ns,
        out_shape=[jax.ShapeDtypeStruct(o_shape, o_dtype)] + rider_shapes,
        scratch_shapes=([pltpu.VMEM(acc_shape, F32)] if nk > 1 else []) + rider_sems,
        compiler_params=_cparams(*sem), name=name)(a, b, *riding)
    return (res[0], list(res[1:])) if ns else res[0]


def _mix_out_norm(mixed, w_out, x0, g_post, g_next, name):
    s, d = x0.shape
    tm = 512

    def body(a_ref, w_ref, x_ref, gp_ref, gn_ref, y_ref, x1_ref, h_ref):
        y = _dot(a_ref[...], w_ref[...].reshape(d, d), NN)
        y_ref[...] = y
        x1 = x_ref[...] + y * _rsq_mean(y) * gp_ref[...]
        x1_ref[...] = x1
        h_ref[...] = (x1 * _rsq_mean(x1) * gn_ref[...]).astype(BF16)

    row = pl.BlockSpec((tm, d), lambda i: (i, 0))
    vec = pl.BlockSpec((1, d), lambda i: (0, 0))
    return pl.pallas_call(
        body, grid=(s // tm,),
        in_specs=[row, pl.BlockSpec((N_CHIPS, None, d // N_CHIPS, d), lambda i: (0, 0, 0, 0)), row, vec, vec],
        out_specs=[row, row, row],
        out_shape=[jax.ShapeDtypeStruct((s, d), F32), jax.ShapeDtypeStruct((s, d), F32),
                   jax.ShapeDtypeStruct((s, d), BF16)],
        compiler_params=_cparams("parallel"), name=name)(mixed, w_out, x0, g_post, g_next)


def _proj_bwd(dproj, w_in, name, scatter=None):
    s = dproj.shape[0]
    wcol = IN_COLS // N_CHIPS
    ns = 0 if scatter is None else len(scatter[0])

    def body(*refs):
        a_ref, w_ref = refs[:2]
        o_ref = refs[2 + ns]
        if ns:
            start, finish = _scatter_steps(refs[2:2 + ns], refs[3 + ns:3 + 2 * ns], *refs[3 + 2 * ns:], scatter[1])
            first, last = _grid_edges((s // TMM,))
            pl.when(first)(start)
        acc = _dot(a_ref[:, :wcol], w_ref[0], NT)
        for j in range(1, N_CHIPS):
            acc = acc + _dot(a_ref[:, j * wcol:(j + 1) * wcol], w_ref[j], NT)
        o_ref[...] = acc
        if ns:
            pl.when(last)(finish)

    res = pl.pallas_call(
        body, grid=(s // TMM,),
        in_specs=[pl.BlockSpec((TMM, IN_COLS), lambda i: (i, 0)),
                  pl.BlockSpec((N_CHIPS, None, D_MODEL, wcol), lambda i: (0, 0, 0, 0))] + [ANY] * ns,
        out_specs=[pl.BlockSpec((TMM, D_MODEL), lambda i: (i, 0))] + [ANY] * ns,
        out_shape=[jax.ShapeDtypeStruct((s, D_MODEL), F32)] + (_scattered_shapes(scatter[1]) if ns else []),
        scratch_shapes=_scatter_sems(ns) if ns else [],
        compiler_params=_cparams("arbitrary" if ns else "parallel"), name=name)(dproj, w_in,
                                                                              *(scatter[0] if ns else []))
    return res[0], list(res[1:])


TM = 512
TMM = 1024


TR = 256


def _row_spec(width, col=0):
    return pl.BlockSpec((TR, width), lambda i, col=col: (i, col))


def _vec_spec(width):
    return pl.BlockSpec((1, width), lambda i: (0, 0))


def _rms_cast(x, g, name):
    s, d = x.shape

    def body(x_ref, g_ref, h_ref):
        a = x_ref[...]
        h_ref[...] = (a * _rsq_mean(a) * g_ref[...]).astype(BF16)

    return pl.pallas_call(
        body, grid=(s // TR,), in_specs=[_row_spec(d), _vec_spec(d)], out_specs=_row_spec(d),
        out_shape=jax.ShapeDtypeStruct((s, d), BF16), compiler_params=_cparams("parallel"), name=name)(x, g)


def _acc_init(refs):
    @pl.when(pl.program_id(0) == 0)
    def _():
        for r in refs:
            r[...] = jnp.zeros_like(r)


def _loss_norm_bwd(x1, f, g_post, target, name):
    s, d = x1.shape

    def body(x_ref, f_ref, gp_ref, t_ref, loss_ref, dx_ref, df_ref, dg_ref):
        _acc_init([loss_ref, dg_ref])
        fv = f_ref[...]
        r = _rsq_mean(fv)
        err = x_ref[...] + fv * r * gp_ref[...] - t_ref[...]
        dx = err * (1.0 / d)
        dx_ref[...] = dx
        part = 0.5 * jnp.sum(jnp.mean(err * err, axis=-1, keepdims=True), axis=0, keepdims=True)
        loss_ref[...] += jnp.broadcast_to(part, loss_ref.shape)
        da, dgt = _rms_bwd(fv, r, gp_ref[...], dx)
        df_ref[...] = da.astype(BF16)
        dg_ref[...] += _colsum(dgt)

    return pl.pallas_call(
        body, grid=(s // TR,), in_specs=[_row_spec(d), _row_spec(d), _vec_spec(d), _row_spec(d)],
        out_specs=[pl.BlockSpec((8, LANES), lambda i: (0, 0)), _row_spec(d), _row_spec(d), _vec_spec(d)],
        out_shape=[jax.ShapeDtypeStruct((8, LANES), F32), jax.ShapeDtypeStruct((s, d), F32),
                   jax.ShapeDtypeStruct((s, d), BF16), jax.ShapeDtypeStruct((1, d), F32)],
        compiler_params=_cparams("arbitrary"), name=name)(x1, f, g_post, target)


def _norm_bwd_mid(dx2, dh2, x1, g_pf, y1, g_pm, name):
    s, d = dx2.shape

    def body(dx2_ref, dh_ref, x1_ref, gpf_ref, y1_ref, gpm_ref, dx1_ref, dy1_ref, dgpf_ref, dgpm_ref):
        _acc_init([dgpf_ref, dgpm_ref])
        x1 = x1_ref[...]
        da, dgt = _rms_bwd(x1, _rsq_mean(x1), gpf_ref[...], dh_ref[...])
        dx1 = dx2_ref[...] + da
        dx1_ref[...] = dx1
        dgpf_ref[...] += _colsum(dgt)
        y1 = y1_ref[...]
        dy, dgt2 = _rms_bwd(y1, _rsq_mean(y1), gpm_ref[...], dx1)
        dy1_ref[...] = dy.astype(BF16)
        dgpm_ref[...] += _colsum(dgt2)

    return pl.pallas_call(
        body, grid=(s // TR,),
        in_specs=[_row_spec(d), _row_spec(d), _row_spec(d), _vec_spec(d), _row_spec(d), _vec_spec(d)],
        out_specs=[_row_spec(d), _row_spec(d), _vec_spec(d), _vec_spec(d)],
        out_shape=[jax.ShapeDtypeStruct((s, d), F32), jax.ShapeDtypeStruct((s, d), BF16),
                   jax.ShapeDtypeStruct((1, d), F32), jax.ShapeDtypeStruct((1, d), F32)],
        compiler_params=_cparams("arbitrary"), name=name)(dx2, dh2, x1, g_pf, y1, g_pm)


def _norm_bwd_in_out(dx1, dh1, x0, g1, f_below, g_post_below, name):
    s, d = dx1.shape

    def body(dx1_ref, dh_ref, x0_ref, g_ref, f_ref, gp_ref, dx0_ref, dg_ref, df_ref, dgp_ref):
        _acc_init([dg_ref, dgp_ref])
        x0 = x0_ref[...]
        da, dgt = _rms_bwd(x0, _rsq_mean(x0), g_ref[...], dh_ref[...])
        dx0 = dx1_ref[...] + da
        dx0_ref[...] = dx0
        dg_ref[...] += _colsum(dgt)
        fv = f_ref[...]
        db, dgt2 = _rms_bwd(fv, _rsq_mean(fv), gp_ref[...], dx0)
        df_ref[...] = db.astype(BF16)
        dgp_ref[...] += _colsum(dgt2)

    return pl.pallas_call(
        body, grid=(s // TR,),
        in_specs=[_row_spec(d), _row_spec(d), _row_spec(d), _vec_spec(d), _row_spec(d), _vec_spec(d)],
        out_specs=[_row_spec(d), _vec_spec(d), _row_spec(d), _vec_spec(d)],
        out_shape=[jax.ShapeDtypeStruct((s, d), F32), jax.ShapeDtypeStruct((1, d), F32),
                   jax.ShapeDtypeStruct((s, d), BF16), jax.ShapeDtypeStruct((1, d), F32)],
        compiler_params=_cparams("arbitrary"), name=name)(dx1, dh1, x0, g1, f_below, g_post_below)


def _norm_bwd_in(dx1, dh1, x0, g1, name):
    s, d = dx1.shape

    def body(dx1_ref, dh_ref, x0_ref, g_ref, dx0_ref, dg_ref):
        _acc_init([dg_ref])
        x0 = x0_ref[...]
        da, dgt = _rms_bwd(x0, _rsq_mean(x0), g_ref[...], dh_ref[...])
        dx0_ref[...] = dx1_ref[...] + da
        dg_ref[...] += _colsum(dgt)

    return pl.pallas_call(
        body, grid=(s // TR,), in_specs=[_row_spec(d), _row_spec(d), _row_spec(d), _vec_spec(d)],
        out_specs=[_row_spec(d), _vec_spec(d)],
        out_shape=[jax.ShapeDtypeStruct((s, d), F32), jax.ShapeDtypeStruct((1, d), F32)],
        compiler_params=_cparams("arbitrary"), name=name)(dx1, dh1, x0, g1)


def _tril_mask():
    row = lax.broadcasted_iota(jnp.int32, (CHUNK, CHUNK), 0)
    col = lax.broadcasted_iota(jnp.int32, (CHUNK, CHUNK), 1)
    return row >= col


def _gating_forward(pa, gv, bv, wt, bsf):
    er = lax.erf(pa * RSQRT2)
    za = 0.5 * pa * (1.0 + er)
    u = za[:, :A_WIDTH]
    va = za[:, A_WIDTH:]
    xc = va - jnp.mean(va, axis=-1, keepdims=True)
    rs = lax.rsqrt(jnp.mean(xc * xc, axis=-1, keepdims=True) + EPS)
    vn = xc * rs
    vlb = (vn * gv + bv).astype(BF16)
    sg = jnp.concatenate(
        [_dot(wt[g], vlb[:, g * GROUP_DIM:(g + 1) * GROUP_DIM], NN) for g in range(A_GROUPS)], axis=1) + bsf
    return er, u, rs, vn, vlb, sg


def _masked_ws(ws_ref):
    mask = _tril_mask()
    return [jnp.where(mask, ws_ref[g], 0.0).astype(BF16) for g in range(A_GROUPS)]


def _mixer_a_fwd(proj, gv, bv, ws, bsf, ga, name):
    s = proj.shape[0]

    def body(p_ref, gv_ref, bv_ref, ws_ref, bs_ref, ga_ref, o_ref):
        wt = _masked_ws(ws_ref)
        for ch in range(TR // CHUNK):
            rows = slice(ch * CHUNK, (ch + 1) * CHUNK)
            _, u, _, _, _, sg = _gating_forward(p_ref[rows, :].astype(F32), gv_ref[...], bv_ref[...], wt, bs_ref[...])
            oa = u * sg
            o_ref[rows, :] = (oa * _rsq_mean(oa) * ga_ref[...]).astype(BF16)

    return pl.pallas_call(
        body, grid=(s // TR,),
        in_specs=[_row_spec(2 * A_WIDTH), _vec_spec(A_WIDTH), _vec_spec(A_WIDTH),
                  pl.BlockSpec((A_GROUPS, CHUNK, CHUNK), lambda i: (0, 0, 0)),
                  pl.BlockSpec((CHUNK, A_WIDTH), lambda i: (0, 0)), _vec_spec(A_WIDTH)],
        out_specs=_row_spec(A_WIDTH), out_shape=jax.ShapeDtypeStruct((s, A_WIDTH + B_WIDTH), BF16),
        compiler_params=_cparams("parallel"), name=name)(proj, gv, bv, ws, bsf, ga)


def _mixer_a_bwd(proj, dmixed, gv, bv, ws, bsf, ga, name, scatter=None):
    s = proj.shape[0]
    nsteps = s // TR
    ns = 0 if scatter is None else len(scatter[0])

    def body(*refs):
        p_ref, dm_ref, gv_ref, bv_ref, ws_ref, bs_ref, ga_ref = refs[:7]
        dp_ref, dga_ref, dgv_ref, dbv_ref, dbs_ref, dws_ref = refs[7 + ns:13 + ns]
        if ns:
            start, finish = _scatter_steps(refs[7:7 + ns], refs[13 + ns:13 + 2 * ns], *refs[13 + 2 * ns:], scatter[1])
            first, last = _grid_edges((nsteps,))
            pl.when(first)(start)
        _acc_init([dga_ref, dgv_ref, dbv_ref, dbs_ref, dws_ref])
        mask = _tril_mask()
        wt = _masked_ws(ws_ref)
        gvv = gv_ref[...]
        gav = ga_ref[...]
        for ch in range(TR // CHUNK):
            rows = slice(ch * CHUNK, (ch + 1) * CHUNK)
            pa = p_ref[rows, :].astype(F32)
            er, u, rs, vn, vlb, sg = _gating_forward(pa, gvv, bv_ref[...], wt, bs_ref[...])
            oa = u * sg
            doa, dgt = _rms_bwd(oa, _rsq_mean(oa), gav, dm_ref[rows, :])
            dga_ref[...] += _colsum(dgt)
            du = doa * sg
            dsg = doa * u
            dbs_ref[...] += dsg
            dsgb = dsg.astype(BF16)
            dvl = []
            for g in range(A_GROUPS):
                cols = slice(g * GROUP_DIM, (g + 1) * GROUP_DIM)
                dws_ref[g] += jnp.where(mask, _dot(dsgb[:, cols], vlb[:, cols], NT), 0.0)
                dvl.append(_dot(wt[g], dsgb[:, cols], TN))
            dvl = jnp.concatenate(dvl, axis=1)
            dgv_ref[...] += _colsum(dvl * vn)
            dbv_ref[...] += _colsum(dvl)
            dvn = dvl * gvv
            dva = rs * (dvn - jnp.mean(dvn, axis=-1, keepdims=True)
                        - vn * jnp.mean(dvn * vn, axis=-1, keepdims=True))
            gp = 0.5 * (1.0 + er) + pa * jnp.exp(-0.5 * pa * pa) * INV_SQRT_2PI
            dp_ref[rows, :] = (jnp.concatenate([du, dva], axis=1) * gp).astype(BF16)

        @pl.when(pl.program_id(0) == nsteps - 1)
        def _():
            for g in range(A_GROUPS):
                cols = slice(g * GROUP_DIM, (g + 1) * GROUP_DIM)
                tot = jnp.sum(dbs_ref[:, cols], axis=1, keepdims=True)
                dbs_ref[:, cols] = jnp.broadcast_to(tot, (CHUNK, GROUP_DIM))

        if ns:
            pl.when(last)(finish)

    full = lambda *shape: pl.BlockSpec(shape, lambda i: (0,) * len(shape))
    res = pl.pallas_call(
        body, grid=(nsteps,),
        in_specs=[_row_spec(2 * A_WIDTH), _row_spec(A_WIDTH), _vec_spec(A_WIDTH), _vec_spec(A_WIDTH),
                  full(A_GROUPS, CHUNK, CHUNK), full(CHUNK, A_WIDTH), _vec_spec(A_WIDTH)] + [ANY] * ns,
        out_specs=[_row_spec(2 * A_WIDTH), _vec_spec(A_WIDTH), _vec_spec(A_WIDTH), _vec_spec(A_WIDTH),
                   full(CHUNK, A_WIDTH), full(A_GROUPS, CHUNK, CHUNK)] + [ANY] * ns,
        out_shape=[jax.ShapeDtypeStruct((s, IN_COLS), BF16), jax.ShapeDtypeStruct((1, A_WIDTH), F32),
                   jax.ShapeDtypeStruct((1, A_WIDTH), F32), jax.ShapeDtypeStruct((1, A_WIDTH), F32),
                   jax.ShapeDtypeStruct((CHUNK, A_WIDTH), F32),
                   jax.ShapeDtypeStruct((A_GROUPS, CHUNK, CHUNK), F32)]
        + (_scattered_shapes(scatter[1]) if ns else []),
        scratch_shapes=_scatter_sems(ns) if ns else [],
        compiler_params=_cparams("arbitrary"), name=name)(proj, dmixed, gv, bv, ws, bsf, ga,
                                                          *(scatter[0] if ns else []))
    return res[:6] + (list(res[6:]),)


def _rope_tables(s):
    half = ROT_DIM // 2
    lane = jnp.arange(LANES) % HEAD_DIM
    inv = ROPE_THETA ** (-(2 * (lane % half)).astype(F32) / ROT_DIM)
    ang = jnp.arange(s, dtype=F32)[:, None] * inv[None, :]
    cos, sin = jnp.cos(ang), jnp.sin(ang)
    c = jnp.where(lane < ROT_DIM, cos, 1.0)
    s1 = jnp.where(lane < half, -sin, 0.0)
    s2 = jnp.where((lane >= half) & (lane < ROT_DIM), sin, 0.0)
    return c, s1, s2


def _lane_blocks(width):
    return [slice(b * LANES, (b + 1) * LANES) for b in range(width // LANES)]


CLASS_DILS = tuple(d for d in DILATIONS if d > 1)


def _class_shape(s, dil, dtype):
    return jax.ShapeDtypeStruct((dil, s // dil, B_WIDTH), dtype)


def _class_spec(dil):
    return pl.BlockSpec((dil, TR // dil, B_WIDTH), lambda i, *_: (0, i, 0))


NBLK = B_WIDTH // LANES
STAGE = pltpu.VMEM((NBLK, TR, LANES), F32)


def _stage_put(stage, value):
    for b, sl in enumerate(_lane_blocks(B_WIDTH)):
        stage[b] = value[:, sl]


def _stage_get(stage):
    return jnp.concatenate([stage[b] for b in range(NBLK)], axis=1)


def _store_classes(stage, dst_ref, dil):
    for b, sl in enumerate(_lane_blocks(B_WIDTH)):
        for r in range(dil):
            dst_ref[r, :, sl] = stage[b, pl.ds(r, TR // dil, stride=dil), :].astype(dst_ref.dtype)


def _load_classes(src_ref, stage, dil):
    for b, sl in enumerate(_lane_blocks(B_WIDTH)):
        for r in range(dil):
            stage[b, pl.ds(r, TR // dil, stride=dil), :] = src_ref[r, :, sl].astype(F32)
    return _stage_get(stage)


def _rope_fwd(proj, tabs, name, gather=None):
    s = proj.shape[0]
    half = ROT_DIM // 2
    scale = HEAD_DIM ** -0.5
    nlay = 1 + len(CLASS_DILS)
    ng = 0 if gather is None else len(gather)

    def body(q_ref, k_ref, v_ref, c_ref, s1_ref, s2_ref, *rest):
        outs, stage = rest[ng:ng + 3 * nlay], rest[2 * ng + 3 * nlay]
        if ng:
            start, relay, finish = _gather_steps(rest[:ng], rest[ng + 3 * nlay:2 * ng + 3 * nlay],
                                                 *rest[2 * ng + 3 * nlay + 1:])
            first, last = _grid_edges((s // TR,))
            pl.when(first)(start)
        c, s1, s2 = c_ref[...], s1_ref[...], s2_ref[...]
        for which, (src, mul) in enumerate(((q_ref, scale), (k_ref, 1.0), (v_ref, None))):
            if mul is None:
                _stage_put(stage, src[...].astype(F32))
            else:
                for b, sl in enumerate(_lane_blocks(B_WIDTH)):
                    a = src[:, sl].astype(F32)
                    r = a * c + pltpu.roll(a, LANES - half, 1) * s1 + pltpu.roll(a, half, 1) * s2
                    stage[b] = r * mul
            dst = outs[which * nlay:(which + 1) * nlay]
            dst[0][...] = _stage_get(stage).astype(BF16)
            for ref, d in zip(dst[1:], CLASS_DILS):
                _store_classes(stage, ref, d)

        if ng:
            @pl.when(last)
            def _():
                relay()
                finish()

    tab = pl.BlockSpec((TR, LANES), lambda i: (i, 0))
    lay_specs = [_row_spec(B_WIDTH)] + [_class_spec(d) for d in CLASS_DILS]
    lay_shapes = [jax.ShapeDtypeStruct((s, B_WIDTH), BF16)] + [_class_shape(s, d, BF16) for d in CLASS_DILS]
    outs = pl.pallas_call(
        body, grid=(s // TR,),
        in_specs=[_row_spec(B_WIDTH, 2), _row_spec(B_WIDTH, 3), _row_spec(B_WIDTH, 4), tab, tab, tab] + [ANY] * ng,
        out_specs=lay_specs * 3 + [ANY] * ng, out_shape=lay_shapes * 3 + _gathered_shapes(gather or []),
        scratch_shapes=[STAGE] + (_gather_sems(ng) if ng else []),
        compiler_params=_cparams("arbitrary" if ng else "parallel"), name=name)(proj, proj, proj, *tabs,
                                                                              *(gather or []))
    q, k, v = (dict(zip(DILATIONS, outs[w * nlay:(w + 1) * nlay])) for w in range(3))
    return q, k, v, list(outs[3 * nlay:])


def _as_classes(t):
    return t if t.ndim == 3 else t[None]


def _head_masks():
    lane = lax.broadcasted_iota(jnp.int32, (1, LANES), 1)
    return lane < HEAD_DIM, lane >= HEAD_DIM


def _stack_heads(t):
    lo, hi = _head_masks()
    zero = jnp.zeros_like(t)
    return jnp.concatenate([jnp.where(lo, t, zero), jnp.where(hi, t, zero)], axis=0)


MAX_SEGMENT_BLOCKS = 8


def _segment_masks(j):
    qi = lax.broadcasted_iota(jnp.int32, (BAND, 2 * BAND), 0)
    kj = lax.broadcasted_iota(jnp.int32, (BAND, 2 * BAND), 1)
    both = (kj >= qi) & (kj <= qi + BAND)
    own = kj[:, :BAND] <= qi[:, :BAND]
    head = both & ((kj >= BAND) | (j > 0))
    return tuple(jnp.concatenate([m, m], axis=0) for m in (own, both, head))


def _block_rows(g):
    return pl.ds(pl.multiple_of(g * BAND, BAND), BAND)


def _key_rows(g):
    return pl.ds(pl.multiple_of((g - 1) * BAND, BAND), 2 * BAND)


def _segments(n):
    nb = n // BAND
    seg = min(nb, MAX_SEGMENT_BLOCKS)
    return seg, nb // seg


def _segment_specs(seg):
    main = pl.BlockSpec((None, seg * BAND, B_WIDTH), lambda r, j: (r, j, 0))
    halo = pl.BlockSpec((None, BAND, B_WIDTH), lambda r, j: (r, jnp.maximum(j * seg - 1, 0), 0))
    return main, halo


def _attn_fwd(q, k, v, name, gather=None):
    dil, n, _ = q.shape
    seg, nseg = _segments(n)
    nh = 2 if nseg > 1 else 0
    ng = 0 if gather is None else len(gather)

    def body(*refs):
        q_ref, k_ref, v_ref = refs[:3]
        halos = refs[3:3 + nh]
        o_ref, l_ref = refs[3 + nh + ng:5 + nh + ng]
        if ng:
            start, relay, finish = _gather_steps(refs[3 + nh:3 + nh + ng], refs[5 + nh + ng:5 + nh + 2 * ng],
                                                 *refs[5 + nh + 2 * ng:])
            first, last = _grid_edges((dil, nseg))
            pl.when(first)(start)
        own, both, head = _segment_masks(pl.program_id(1))
        lo, _ = _head_masks()

        def block(rows, keys_of, valid):
            for sl in _lane_blocks(B_WIDTH):
                kk, vv = keys_of(sl)
                sc = jnp.where(valid, _dot(_stack_heads(q_ref[rows, sl]), kk, NT), NEG_INF)
                mx = jnp.max(sc, axis=1, keepdims=True)
                p = jnp.exp(sc - mx)
                den = jnp.sum(p, axis=1, keepdims=True)
                out = _dot(p.astype(BF16), vv, NN) / den
                lse = mx + jnp.log(den)
                o_ref[rows, sl] = jnp.where(lo, out[:BAND], out[BAND:]).astype(BF16)
                l_ref[rows, sl] = jnp.where(lo, lse[:BAND], lse[BAND:])

        if nh:
            block(_block_rows(0), lambda sl: (jnp.concatenate([halos[0][:, sl], k_ref[0:BAND, sl]], axis=0),
                                              jnp.concatenate([halos[1][:, sl], v_ref[0:BAND, sl]], axis=0)), head)
        else:
            block(_block_rows(0), lambda sl: (k_ref[0:BAND, sl], v_ref[0:BAND, sl]), own)

        @pl.loop(1, seg)
        def _(g):
            block(_block_rows(g), lambda sl: (k_ref[_key_rows(g), sl], v_ref[_key_rows(g), sl]), both)

        if ng:
            @pl.when(last)
            def _():
                relay()
                finish()

    main, halo = _segment_specs(seg)
    res = pl.pallas_call(
        body, grid=(dil, nseg), in_specs=[main] * 3 + [halo] * nh + [ANY] * ng, out_specs=[main, main] + [ANY] * ng,
        out_shape=[jax.ShapeDtypeStruct((dil, n, B_WIDTH), BF16), jax.ShapeDtypeStruct((dil, n, B_WIDTH), F32)]
        + _gathered_shapes(gather or []),
        scratch_shapes=_gather_sems(ng) if ng else [],
        compiler_params=_cparams(*(["arbitrary"] * 2 if ng else ["parallel"] * 2)), name=name)(
            q, k, v, *([k, v] if nh else []), *(gather or []))
    return res[0], res[1], list(res[2:])


def _attn_bwd(q, k, v, do, lse, delta, name, scatter=None):
    dil, n, _ = q.shape
    seg, nseg = _segments(n)
    nh = 2 if nseg > 1 else 0
    ns = 0 if scatter is None else len(scatter[0])

    def body(*refs):
        q_ref, k_ref, v_ref, do_ref, lse_ref, dl_ref = refs[:6]
        halos = refs[6:6 + nh]
        dq_ref, dk_ref, dv_ref = refs[6 + nh + ns:9 + nh + ns]
        halo_out = refs[9 + nh + ns:9 + 2 * nh + ns]
        ck_ref, cv_ref = refs[9 + 2 * nh + 2 * ns:11 + 2 * nh + 2 * ns]
        if ns:
            start, finish = _scatter_steps(refs[6 + nh:6 + nh + ns], refs[9 + 2 * nh + ns:9 + 2 * nh + 2 * ns],
                                           *refs[11 + 2 * nh + 2 * ns:], scatter[1])
            first, last = _grid_edges((dil, nseg))
            pl.when(first)(start)
        own, both, head = _segment_masks(pl.program_id(1))
        lo, _ = _head_masks()
        lane = lax.broadcasted_iota(jnp.int32, (1, LANES), 1)

        def per_head(t):
            return jnp.concatenate(
                [jnp.sum(jnp.where(lane == first, t, 0.0), axis=1, keepdims=True) for first in (0, HEAD_DIM)], axis=0)

        def grads(rows, kk, vv, valid, sl):
            q2 = _stack_heads(q_ref[rows, sl])
            do2 = _stack_heads(do_ref[rows, sl])
            p = jnp.where(valid, jnp.exp(_dot(q2, kk, NT) - per_head(lse_ref[rows, sl])), 0.0)
            ds = (p * (_dot(do2, vv, NT) - per_head(dl_ref[rows, sl]))).astype(BF16)
            dq = _dot(ds, kk, NN)
            dq_ref[rows, sl] = jnp.where(lo, dq[:BAND], dq[BAND:]).astype(BF16)
            return _dot(ds, q2, TN), _dot(p.astype(BF16), do2, TN)

        for sl in _lane_blocks(B_WIDTH):
            if nh:
                dkk, dvv = grads(_block_rows(0), jnp.concatenate([halos[0][:, sl], k_ref[0:BAND, sl]], axis=0),
                                 jnp.concatenate([halos[1][:, sl], v_ref[0:BAND, sl]], axis=0), head, sl)
                halo_out[0][:, sl], halo_out[1][:, sl] = dkk[:BAND], dvv[:BAND]
                ck_ref[:, sl], cv_ref[:, sl] = dkk[BAND:], dvv[BAND:]
            else:
                ck_ref[:, sl], cv_ref[:, sl] = grads(_block_rows(0), k_ref[0:BAND, sl], v_ref[0:BAND, sl], own, sl)

        @pl.loop(1, seg)
        def _(g):
            before = _block_rows(g - 1)
            for sl in _lane_blocks(B_WIDTH):
                dkk, dvv = grads(_block_rows(g), k_ref[_key_rows(g), sl], v_ref[_key_rows(g), sl], both, sl)
                dk_ref[before, sl] = (ck_ref[:, sl] + dkk[:BAND]).astype(BF16)
                dv_ref[before, sl] = (cv_ref[:, sl] + dvv[:BAND]).astype(BF16)
                ck_ref[:, sl] = dkk[BAND:]
                cv_ref[:, sl] = dvv[BAND:]

        final = pl.ds((seg - 1) * BAND, BAND)
        dk_ref[final, :] = ck_ref[...].astype(BF16)
        dv_ref[final, :] = cv_ref[...].astype(BF16)

        if ns:
            pl.when(last)(finish)

    main, halo = _segment_specs(seg)
    shape = jax.ShapeDtypeStruct((dil, n, B_WIDTH), BF16)
    halo_shape = jax.ShapeDtypeStruct((dil, nseg, BAND, B_WIDTH), F32)
    halo_spec = pl.BlockSpec((None, None, BAND, B_WIDTH), lambda r, j: (r, j, 0, 0))
    res = pl.pallas_call(
        body, grid=(dil, nseg), in_specs=[main] * 6 + [halo] * nh + [ANY] * ns,
        out_specs=[main] * 3 + [halo_spec] * nh + [ANY] * ns,
        out_shape=[shape] * 3 + [halo_shape] * nh + (_scattered_shapes(scatter[1]) if ns else []),
        scratch_shapes=[pltpu.VMEM((BAND, B_WIDTH), F32)] * 2 + (_scatter_sems(ns) if ns else []),
        compiler_params=_cparams(*(["arbitrary"] * 2 if ns else ["parallel"] * 2)), name=name)(
            q, k, v, do, lse, delta, *([k, v] if nh else []), *(scatter[0] if ns else []))
    return res[0], res[1], res[2], (tuple(res[3:3 + nh]) if nh else None), list(res[3 + nh:])


def _attn_combine(outs, lses, gb, mixed, name, gather=None):
    s = mixed.shape[0]
    npat = len(DILATIONS)
    w = B_WIDTH
    ng = 0 if gather is None else len(gather)

    def body(*refs):
        o_refs, l_refs = refs[:npat], refs[npat:2 * npat]
        g_ref = refs[2 * npat]
        ob_ref = refs[2 * npat + 2 + ng]
        lse_refs = refs[2 * npat + 3 + ng:3 * npat + 3 + ng]
        mb_ref = refs[3 * npat + 3 + ng]
        stage = refs[3 * npat + 4 + 2 * ng]
        if ng:
            start, relay, finish = _gather_steps(refs[2 * npat + 2:2 * npat + 2 + ng],
                                                 refs[3 * npat + 4 + ng:3 * npat + 4 + 2 * ng],
                                                 *refs[3 * npat + 5 + 2 * ng:])
            first, last = _grid_edges((s // TR,))
            pl.when(first)(start)
        os_ = [o_refs[0][...].astype(F32)] + [_load_classes(r, stage, d) for r, d in zip(o_refs[1:], CLASS_DILS)]
        ls = [l_refs[0][...]] + [_load_classes(r, stage, d) for r, d in zip(l_refs[1:], CLASS_DILS)]
        mx = functools.reduce(jnp.maximum, ls)
        ws = [jnp.exp(l - mx) for l in ls]
        tot = functools.reduce(lambda a, b: a + b, ws)
        ob = functools.reduce(lambda a, b: a + b, [wt / tot * o for wt, o in zip(ws, os_)])
        ob_ref[...] = ob
        lse = mx + jnp.log(tot)
        _stage_put(stage, lse)
        lse_refs[0][...] = lse
        for ref, d in zip(lse_refs[1:], CLASS_DILS):
            _store_classes(stage, ref, d)
        mb_ref[...] = (ob * _rsq_mean(ob) * g_ref[...]).astype(BF16)

        if ng:
            @pl.when(last)
            def _():
                relay()
                finish()

    lay_specs = [_row_spec(w)] + [_class_spec(d) for d in CLASS_DILS]
    res = pl.pallas_call(
        body, grid=(s // TR,), in_specs=lay_specs * 2 + [_vec_spec(w), ANY] + [ANY] * ng,
        out_specs=[_row_spec(w)] + lay_specs + [_row_spec(w, 1)] + [ANY] * ng,
        out_shape=[jax.ShapeDtypeStruct((s, w), F32), jax.ShapeDtypeStruct((s, w), F32)]
        + [_class_shape(s, d, F32) for d in CLASS_DILS] + [jax.ShapeDtypeStruct(mixed.shape, mixed.dtype)]
        + _gathered_shapes(gather or []),
        scratch_shapes=[STAGE] + (_gather_sems(ng) if ng else []), input_output_aliases={2 * npat + 1: npat + 1},
        compiler_params=_cparams("arbitrary" if ng else "parallel"), name=name)(*outs, *lses, gb, mixed,
                                                                              *(gather or []))
    return res[0], dict(zip(DILATIONS, res[1:npat + 1])), res[npat + 1], list(res[npat + 2:])


def _attn_bwd_prep(dmixed, ob, gb, name):
    s = ob.shape[0]
    w = B_WIDTH
    nlay = len(DILATIONS)

    def body(dm_ref, ob_ref, g_ref, *rest):
        do_refs, dl_refs = rest[:nlay], rest[nlay:2 * nlay]
        dg_ref, stage = rest[2 * nlay:]
        _acc_init([dg_ref])
        ob = ob_ref[...]
        dob, dgt = _rms_bwd(ob, _rsq_mean(ob), g_ref[...], dm_ref[...])
        dg_ref[...] += _colsum(dgt)
        _stage_put(stage, dob)
        do_refs[0][...] = dob.astype(BF16)
        for ref, d in zip(do_refs[1:], CLASS_DILS):
            _store_classes(stage, ref, d)
        lo, hi = _head_masks()
        t = dob * ob
        for b, sl in enumerate(_lane_blocks(w)):
            tb = t[:, sl]
            s0 = jnp.sum(jnp.where(lo, tb, 0.0), axis=1, keepdims=True)
            s1 = jnp.sum(jnp.where(hi, tb, 0.0), axis=1, keepdims=True)
            stage[b] = jnp.where(lo, s0, s1)
        dl_refs[0][...] = _stage_get(stage)
        for ref, d in zip(dl_refs[1:], CLASS_DILS):
            _store_classes(stage, ref, d)

    lay_specs = [_row_spec(w)] + [_class_spec(d) for d in CLASS_DILS]
    shapes = lambda dt: [jax.ShapeDtypeStruct((s, w), dt)] + [_class_shape(s, d, dt) for d in CLASS_DILS]
    res = pl.pallas_call(
        body, grid=(s // TR,), in_specs=[_row_spec(w, 1), _row_spec(w), _vec_spec(w)],
        out_specs=lay_specs * 2 + [_vec_spec(w)],
        out_shape=shapes(BF16) + shapes(F32) + [jax.ShapeDtypeStruct((1, w), F32)],
        scratch_shapes=[STAGE],
        compiler_params=_cparams("arbitrary"), name=name)(dmixed, ob, gb)
    return dict(zip(DILATIONS, res[:nlay])), dict(zip(DILATIONS, res[nlay:2 * nlay])), res[2 * nlay]


def _rope_bwd(dqs, dks, dvs, halos, tabs, dproj, name):
    s = dproj.shape[0]
    half = ROT_DIM // 2
    scale = HEAD_DIM ** -0.5
    npat = len(DILATIONS)
    w = B_WIDTH
    nseg = halos[0].shape[0]
    per = s // nseg // TR

    def body(*refs):
        groups = [refs[g * npat:(g + 1) * npat] for g in range(3)]
        halo_refs = (None,) + tuple(refs[3 * npat:3 * npat + 2])
        c_ref, s1_ref, s2_ref, _, o_ref, stage = refs[3 * npat + 2:]
        i = pl.program_id(0)
        at_edge = ((i + 1) % per == 0) & ((i + 1) // per < nseg)

        def total(rs, halo_ref=None):
            acc = rs[0][...].astype(F32)
            if halo_ref is not None:
                edge = jnp.concatenate([jnp.zeros((TR - BAND, w), F32), halo_ref[...]], axis=0)
                acc = acc + jnp.where(at_edge, edge, 0.0)
            for ref, d in zip(rs[1:], CLASS_DILS):
                acc = acc + _load_classes(ref, stage, d)
            return acc

        def unrope(g):
            c, s1, s2 = c_ref[...], s1_ref[...], s2_ref[...]
            for sl in _lane_blocks(w):
                gb = g[:, sl]
                o = gb * c + pltpu.roll(gb * s1, half, 1) + pltpu.roll(gb * s2, LANES - half, 1)
                o_ref[:, sl] = o.astype(BF16)

        which = pl.program_id(1)

        @pl.when(which == 0)
        def _():
            unrope(total(groups[0]) * scale)

        @pl.when(which == 1)
        def _():
            unrope(total(groups[1], halo_refs[1]))

        @pl.when(which == 2)
        def _():
            o_ref[...] = total(groups[2], halo_refs[2]).astype(BF16)

    tab = pl.BlockSpec((TR, LANES), lambda i, j: (i, 0))
    nat = pl.BlockSpec((TR, w), lambda i, j: (i, 0))
    lay_specs = [nat] + [_class_spec(d) for d in CLASS_DILS]
    edge_spec = pl.BlockSpec((None, BAND, w), lambda i, j: (jnp.minimum((i + 1) // per, nseg - 1), 0, 0))
    first_col = 2 * A_WIDTH // w
    return pl.pallas_call(
        body, grid=(s // TR, 3), in_specs=lay_specs * 3 + [edge_spec] * 2 + [tab] * 3 + [ANY],
        out_specs=pl.BlockSpec((TR, w), lambda i, j: (i, first_col + j)),
        out_shape=jax.ShapeDtypeStruct(dproj.shape, dproj.dtype), scratch_shapes=[STAGE],
        input_output_aliases={3 * npat + 5: 0},
        compiler_params=_cparams("parallel", "arbitrary"), name=name)(*dqs, *dks, *dvs, *halos, *tabs, dproj)


TK = 512
HALO = 16
FFN_CHUNKS = tuple(slice(r, r + TM // 2) for r in range(0, TM, TM // 2))


def _row_of(v, r):
    rows = lax.broadcasted_iota(jnp.int32, (v.shape[0], 1), 0)
    return jnp.sum(jnp.where(rows == r, v, 0.0), axis=0, keepdims=True)


def _taps_before(x, halo):
    row = lax.broadcasted_iota(jnp.int32, (x.shape[0], 1), 0)
    m1 = jnp.where(row == 0, _row_of(halo, HALO - 1), pltpu.roll(x, 1, 0))
    m2 = jnp.where(row == 0, _row_of(halo, HALO - 2), jnp.where(row == 1, _row_of(halo, HALO - 1), pltpu.roll(x, 2, 0)))
    return m2, m1, x


def _taps_after(x, halo):
    rows = x.shape[0]
    row = lax.broadcasted_iota(jnp.int32, (rows, 1), 0)
    p1 = jnp.where(row == rows - 1, _row_of(halo, 0), pltpu.roll(x, rows - 1, 0))
    p2 = jnp.where(row == rows - 2, _row_of(halo, 0), jnp.where(row == rows - 1, _row_of(halo, 1), pltpu.roll(x, rows - 2, 0)))
    return p1, p2


def _conv_value(taps, cw_ref, cb_ref, h):
    return cb_ref[h] + cw_ref[h, 0:1, :] * taps[0] + cw_ref[h, 1:2, :] * taps[1] + cw_ref[h, 2:3, :] * taps[2]


def _ffn_weight_specs(ncol):
    per_up = (2 * D_FF // N_CHIPS) // TK
    per_dn = (D_FF // N_CHIPS) // TK
    wg = pl.BlockSpec((None, None, D_MODEL, TK), lambda i, j: (j // per_up, 0, 0, j % per_up))
    wv = pl.BlockSpec((None, None, D_MODEL, TK), lambda i, j: ((j + ncol) // per_up, 0, 0, (j + ncol) % per_up))
    wd = pl.BlockSpec((None, None, TK, D_MODEL), lambda i, j: (j // per_dn, 0, j % per_dn, 0))
    cw = pl.BlockSpec((2, 3, TK), lambda i, j: (0, 0, j))
    cb = pl.BlockSpec((2, 1, TK), lambda i, j: (0, 0, j))
    return wg, wv, wd, cw, cb


def _ffn_forward(h2, w_up, w_down, cw3, cb3, name, gather=None, post=None):
    s = h2.shape[0]
    nm, ncol = s // TM, D_FF // TK
    ng = 0 if gather is None else len(gather)
    npost = 0 if post is None else 3
    nout = 4 + (2 if post else 0)

    def body(*refs):
        h_ref, wg_ref, wv_ref, wd_ref, cw_ref, cb_ref = refs[:6]
        post_in = refs[6:6 + npost]
        g_in = refs[6 + npost:6 + npost + ng]
        outs = refs[6 + npost + ng:6 + npost + ng + nout]
        y_ref, up_ref, cv_ref, f_ref = outs[:4]
        g_out = refs[6 + npost + ng + nout:6 + npost + 2 * ng + nout]
        carry, acc = refs[6 + npost + 2 * ng + nout:8 + npost + 2 * ng + nout]
        i, j = pl.program_id(0), pl.program_id(1)
        if ng:
            start, relay, finish = _gather_steps(g_in, g_out, *refs[8 + npost + 2 * ng + nout:])
            pl.when((i == 0) & (j == 0))(start)
            pl.when((i == nm - 1) & (j == 0))(relay)

        @pl.when((i == 0) & (j == 0))
        def _():
            carry[...] = jnp.zeros_like(carry)

        @pl.when(j == 0)
        def _():
            acc[...] = jnp.zeros_like(acc)

        ups = []
        for rs in FFN_CHUNKS:
            hc = h_ref[rs, :]
            ups.append([_dot(hc, w_ref[...], NN).astype(BF16) for w_ref in (wg_ref, wv_ref)])
            for hh in range(2):
                up_ref[hh, rs, :] = ups[-1][hh]
        before = [carry[j, hh] for hh in range(2)]
        for rs, up in zip(FFN_CHUNKS, ups):
            conv = []
            for hh in range(2):
                x = up[hh].astype(F32)
                conv.append(_conv_value(_taps_before(x, before[hh]), cw_ref, cb_ref, hh))
                cv_ref[hh, rs, :] = conv[hh].astype(BF16)
                before[hh] = x[x.shape[0] - HALO:, :]
            y = (_gelu_tanh(conv[0])[0] * conv[1]).astype(BF16)
            y_ref[rs, :] = y
            acc[rs, :] += _dot(y, wd_ref[...], NN)
        for hh in range(2):
            carry[j, hh] = before[hh]

        @pl.when(j == ncol - 1)
        def _():
            f = acc[...]
            f_ref[...] = f
            if post:
                x1_ref, gp_ref, gn_ref = post_in
                x2 = x1_ref[...] + f * _rsq_mean(f) * gp_ref[...]
                outs[4][...] = x2
                outs[5][...] = (x2 * _rsq_mean(x2) * gn_ref[...]).astype(BF16)

        if ng:
            pl.when((i == nm - 1) & (j == ncol - 1))(finish)

    wg, wv, wd, cw, cb = _ffn_weight_specs(ncol)
    row = pl.BlockSpec((TM, D_MODEL), lambda i, j: (i, 0))
    vec = pl.BlockSpec((1, D_MODEL), lambda i, j: (0, 0))
    res = pl.pallas_call(
        body, grid=(nm, ncol),
        in_specs=[row, wg, wv, wd, cw, cb] + ([row, vec, vec] if post else []) + [ANY] * ng,
        out_specs=[pl.BlockSpec((TM, TK), lambda i, j: (i, j)), pl.BlockSpec((2, TM, TK), lambda i, j: (0, i, j)),
                   pl.BlockSpec((2, TM, TK), lambda i, j: (0, i, j)), row] + ([row, row] if post else [])
        + [ANY] * ng,
        out_shape=[jax.ShapeDtypeStruct((s, D_FF), BF16), jax.ShapeDtypeStruct((2, s, D_FF), BF16),
                   jax.ShapeDtypeStruct((2, s, D_FF), BF16), jax.ShapeDtypeStruct((s, D_MODEL), F32)]
        + ([jax.ShapeDtypeStruct((s, D_MODEL), F32), jax.ShapeDtypeStruct((s, D_MODEL), BF16)] if post else [])
        + _gathered_shapes(gather or []),
        scratch_shapes=[pltpu.VMEM((ncol, 2, HALO, TK), F32), pltpu.VMEM((TM, D_MODEL), F32)]
        + (_gather_sems(ng) if ng else []),
        compiler_params=_cparams("arbitrary", "arbitrary"), name=name)(h2, w_up, w_up, w_down, cw3, cb3,
                                                                      *(post or []), *(gather or []))
    return res[:nout], list(res[nout:])


def _ffn_backward(df, w_up, w_down, up3, cv3, cw3, name, scatter=None):
    s = df.shape[0]
    nm, ncol = s // TM, D_FF // TK
    ns = 0 if scatter is None else len(scatter[0])

    def body(*refs):
        df_ref, wg_ref, wv_ref, wd_ref, cw_ref, up_ref, cv_ref = refs[:7]
        s_in = refs[7:7 + ns]
        dup_ref, dh_ref, sums_ref = refs[7 + ns:10 + ns]
        s_out = refs[10 + ns:10 + 2 * ns]
        carry, acc = refs[10 + 2 * ns:12 + 2 * ns]
        i, j = pl.program_id(0), pl.program_id(1)
        if ns:
            start, finish = _scatter_steps(s_in, s_out, *refs[12 + 2 * ns:], scatter[1])
            pl.when((i == 0) & (j == 0))(start)

        @pl.when((i == 0) & (j == 0))
        def _():
            carry[...] = jnp.zeros_like(carry)
            sums_ref[...] = jnp.zeros_like(sums_ref)

        @pl.when(j == 0)
        def _():
            acc[...] = jnp.zeros_like(acc)

        chunks = FFN_CHUNKS[::-1]
        dys = [_dot(df_ref[rs, :], wd_ref[...], NT) for rs in chunks]
        row = lax.broadcasted_iota(jnp.int32, (8, 1), 0)
        after = [carry[j, hh] for hh in range(2)]
        upd = [jnp.zeros((8, TK), F32) for _ in range(2)]
        for rs, dy in zip(chunks, dys):
            act, grad = _gelu_tanh(cv_ref[0, rs, :].astype(F32))
            dcs = (dy * cv_ref[1, rs, :].astype(F32) * grad, dy * act)
            part = acc[rs, :]
            for hh, w_ref in ((0, wg_ref), (1, wv_ref)):
                dc = dcs[hh]
                x = up_ref[hh, rs, :].astype(F32)
                after1, after2 = _taps_after(dc, after[hh])
                for ridx, sm in enumerate((_colsum(after2 * x), _colsum(after1 * x), _colsum(dc * x), _colsum(dc))):
                    upd[hh] = upd[hh] + jnp.where(row == ridx, sm, 0.0)
                dup = (cw_ref[hh, 2:3, :] * dc + cw_ref[hh, 1:2, :] * after1 + cw_ref[hh, 0:1, :] * after2).astype(BF16)
                after[hh] = dc[:HALO, :]
                dup_ref[hh, rs, :] = dup
                part = part + _dot(dup, w_ref[...], NT)
            acc[rs, :] = part
        for hh in range(2):
            sums_ref[j, hh] += upd[hh]
            carry[j, hh] = after[hh]

        @pl.when(j == ncol - 1)
        def _():
            dh_ref[...] = acc[...]

        if ns:
            pl.when((i == nm - 1) & (j == ncol - 1))(finish)

    wg, wv, wd, cw, _ = _ffn_weight_specs(ncol)
    rev = lambda i: nm - 1 - i
    res = pl.pallas_call(
        body, grid=(nm, ncol),
        in_specs=[pl.BlockSpec((TM, D_MODEL), lambda i, j: (rev(i), 0)), wg, wv, wd, cw,
                  pl.BlockSpec((2, TM, TK), lambda i, j: (0, rev(i), j)),
                  pl.BlockSpec((2, TM, TK), lambda i, j: (0, rev(i), j))] + [ANY] * ns,
        out_specs=[pl.BlockSpec((2, TM, TK), lambda i, j: (0, rev(i), j)),
                   pl.BlockSpec((TM, D_MODEL), lambda i, j: (rev(i), 0)),
                   pl.BlockSpec((ncol, 2, 8, TK), lambda i, j: (0, 0, 0, 0))] + [ANY] * ns,
        out_shape=[jax.ShapeDtypeStruct((2, s, D_FF), BF16), jax.ShapeDtypeStruct((s, D_MODEL), F32),
                   jax.ShapeDtypeStruct((ncol, 2, 8, TK), F32)] + (_scattered_shapes(scatter[1]) if ns else []),
        scratch_shapes=[pltpu.VMEM((ncol, 2, HALO, TK), F32), pltpu.VMEM((TM, D_MODEL), F32)]
        + (_scatter_sems(ns) if ns else []),
        compiler_params=_cparams("arbitrary", "arbitrary"), name=name)(df, w_up, w_up, w_down, cw3, up3, cv3,
                                                                      *(scatter[0] if ns else []))
    return res[:3], list(res[3:])


def _wspec(rows, cols, index_map):
    return pl.BlockSpec((None, None, rows, cols), index_map)


def _layer_forward(l, x0, h1, p, wg, tabs, gather=None, late=None, g_next=None):
    s = x0.shape[0]
    nm = s // TMM
    tag = f"_l{l}"
    riders = dict.fromkeys(DILATIONS)
    proj_rider = rope_rider = combine_rider = None
    if late is not None:
        cols = lambda t, parts: [t[:, i * t.shape[1] // parts:(i + 1) * t.shape[1] // parts] for i in range(parts)]
        (down_a, down_b), up_q = cols(late["w_down"], 2), cols(late["w_up"], 4)
        proj_rider, rope_rider, combine_rider = [late["w_out"], down_a], [up_q[2]], [up_q[3]]
        riders = dict(zip(DILATIONS, ([down_b], [up_q[0]], [up_q[1]])))
    proj = _matmul(
        h1, wg["w_in"], grid=(nm, N_CHIPS), a_spec=pl.BlockSpec((TMM, D_MODEL), lambda i, j: (i, 0)),
        b_spec=_wspec(D_MODEL, IN_COLS // N_CHIPS, lambda i, j: (j, 0, 0, 0)),
        o_spec=pl.BlockSpec((TMM, IN_COLS // N_CHIPS), lambda i, j: (i, j)), o_shape=(s, IN_COLS), o_dtype=BF16,
        dims=NN, nk=1, kaxis=None, acc_shape=None, name="proj" + tag, gather=proj_rider)
    if late is not None:
        proj, (w_out_all4, down_a) = proj
    ma = _mixer_a_fwd(proj, p["v_norm_g"], p["v_norm_b"], p["w_spatial"], p["bs_full"], p["out_norm_a"],
                      "mixer_a_fwd" + tag)
    q, k, v, rope_landed = _rope_fwd(proj, tabs, "rope_fwd" + tag, rope_rider)
    outs, lses, landed = zip(*[
        _attn_fwd(_as_classes(q[d]), _as_classes(k[d]), _as_classes(v[d]), f"attn_fwd_d{d}" + tag, riders[d])
        for d in DILATIONS])
    outs = [o.reshape(s, B_WIDTH) if d == 1 else o for o, d in zip(outs, DILATIONS)]
    lses = [t.reshape(s, B_WIDTH) if d == 1 else t for t, d in zip(lses, DILATIONS)]
    ob, lse, mixed, combine_landed = _attn_combine(outs, lses, p["out_norm_b"], ma, "attn_combine" + tag,
                                                   combine_rider)
    if late is not None:
        wg = dict(wg, w_out=w_out_all4, w_down=jnp.concatenate([down_a, landed[0][0]], axis=-1),
                  w_up=jnp.concatenate([landed[1][0], landed[2][0], rope_landed[0], combine_landed[0]], axis=-1))
    y1, x1, h2 = _mix_out_norm(mixed, wg["w_out"], x0, p["post_mix_norm"], p["pre_ffn_norm"], "mix_out" + tag)
    post = None if g_next is None else (x1, p["post_ffn_norm"], g_next)
    (y, up3, cv3, f, *after), gathered = _ffn_forward(h2, wg["w_up"], wg["w_down"], p["cw3"], p["cb3"],
                                                      "ffn_fwd" + tag, gather, post)
    saved = dict(x0=x0, h1=h1, proj=proj, q=q, k=k, v=v, ob=ob, lse=lse, mixed=mixed, y1=y1, x1=x1, h2=h2,
                 up3=up3, cv3=cv3, y=y, f=f)
    if after:
        saved.update(x2=after[0], h_next=after[1])
    return saved, gathered, wg


def _layer_backward(l, dx2, df, sv, p, wg, tabs, pos, scatter=None, hide=False):
    s = dx2.shape[0]
    nm = s // TMM
    tag = f"_l{l}"
    g = {}
    (dup3, dh2, conv_sums), scattered = _ffn_backward(df, wg["w_up"], wg["w_down"], sv["up3"], sv["cv3"], p["cw3"],
                                                      "ffn_bwd" + tag, scatter)
    sums = conv_sums.transpose(1, 2, 0, 3).reshape(2, 8, D_FF)
    g["conv_w"] = jnp.concatenate([sums[0, :3], sums[1, :3]], axis=1)
    g["conv_b"] = jnp.concatenate([sums[0, 3:4], sums[1, 3:4]], axis=1)
    tn = 1024
    done = {}
    gw_down = _matmul(
        sv["y"], df, grid=(D_FF // tn,), a_spec=pl.BlockSpec((s, tn), lambda k: (0, k)),
        b_spec=pl.BlockSpec((s, D_MODEL), lambda k: (0, 0)),
        o_spec=pl.BlockSpec((2, tn, D_MODEL // 2), lambda k: (0, k, 0)),
        o_shape=(2, D_FF, D_MODEL // 2), o_dtype=BF16,
        dims=TN, nk=1, kaxis=None, acc_shape=None, name="w_down_grad" + tag, halves=True)
    down_sums = _chip_sums(l, dict(w_down=gw_down), pos, ("w_down",)) if hide else None
    gw_up = _matmul(
        sv["h2"], dup3, grid=(2 * D_FF // tn,), a_spec=pl.BlockSpec((s, D_MODEL), lambda n: (0, 0)),
        b_spec=pl.BlockSpec((None, s, tn), lambda n: (n // (D_FF // tn), 0, n % (D_FF // tn))),
        o_spec=pl.BlockSpec((None, D_MODEL, tn), lambda n: (n // 2, 0, n % 2)),
        o_shape=(N_CHIPS, D_MODEL, 2 * D_FF // N_CHIPS), o_dtype=BF16,
        dims=TN, nk=1, kaxis=None, acc_shape=None, name="w_up_grad" + tag,
        scatter=(down_sums, ("w_down",)) if hide else None)
    up_sums = None
    if hide:
        gw_up, received = gw_up
        done[("w_down",)] = (down_sums, received)
        up_sums = _chip_sums(l, dict(w_up=gw_up), pos, ("w_up",))
    dx1, dy1, g["pre_ffn_norm"], g["post_mix_norm"] = _norm_bwd_mid(
        dx2, dh2, sv["x1"], p["pre_ffn_norm"], sv["y1"], p["post_mix_norm"], "norm_bwd_mid" + tag)
    w_out_all = pl.BlockSpec((N_CHIPS, None, D_MODEL // N_CHIPS, D_MODEL), lambda i: (0, 0, 0, 0))
    dmixed = _matmul(
        dy1, wg["w_out"], grid=(nm,), a_spec=pl.BlockSpec((TMM, D_MODEL), lambda i: (i, 0)), b_spec=w_out_all,
        o_spec=pl.BlockSpec((TMM, D_MODEL), lambda i: (i, 0)), o_shape=(s, D_MODEL), o_dtype=F32,
        dims=NT, nk=1, kaxis=None, acc_shape=None, name="mix_out_bwd" + tag, b_2d=(D_MODEL, D_MODEL))
    gw_out = _matmul(
        sv["mixed"], dy1, grid=(1,), a_spec=pl.BlockSpec((s, D_MODEL), lambda m: (0, 0)),
        b_spec=pl.BlockSpec((s, D_MODEL), lambda m: (0, 0)),
        o_spec=pl.BlockSpec((2, D_MODEL, D_MODEL // 2), lambda m: (0, 0, 0)),
        o_shape=(2, D_MODEL, D_MODEL // 2), o_dtype=BF16,
        dims=TN, nk=1, kaxis=None, acc_shape=None, name="w_out_grad" + tag, halves=True)
    out_sums = _chip_sums(l, dict(w_out=gw_out), pos, ("w_out",)) if hide else None
    dpa, g["out_norm_a"], g["v_norm_g"], g["v_norm_b"], dbs, g["w_spatial"], received = _mixer_a_bwd(
        sv["proj"], dmixed, p["v_norm_g"], p["v_norm_b"], p["w_spatial"], p["bs_full"], p["out_norm_a"],
        "mixer_a_bwd" + tag, (out_sums, ("w_out",)) if hide else None)
    if hide:
        done[("w_out",)] = (out_sums, received)
    g["b_spatial"] = dbs[:, ::GROUP_DIM].T
    dob, delta, g["out_norm_b"] = _attn_bwd_prep(dmixed, sv["ob"], p["out_norm_b"], "attn_bwd_prep" + tag)
    halves = dict(zip(DILATIONS, ("w_up:0", "w_up:1"))) if hide else {}
    dqs, dks, dvs, edges, received = zip(*[
        _attn_bwd(*(_as_classes(t[d]) for t in (sv["q"], sv["k"], sv["v"], dob, sv["lse"], delta)),
                  f"attn_bwd_d{d}" + tag, (up_sums, (halves[d],)) if d in halves else None)
        for d in DILATIONS])
    if hide:
        done[("w_up",)] = (up_sums, [jnp.concatenate([received[0][0], received[1][0]], axis=-1)])
    nat = lambda ts: [t.reshape(s, B_WIDTH) if d == 1 else t for t, d in zip(ts, DILATIONS)]
    halos = [t[0] for t in edges[0]]
    dproj = _rope_bwd(nat(dqs), nat(dks), nat(dvs), halos, tabs, dpa, "rope_bwd" + tag)
    wcol = IN_COLS // N_CHIPS
    gw_in = _matmul(
        sv["h1"], dproj, grid=(N_CHIPS,), a_spec=pl.BlockSpec((s, D_MODEL), lambda n: (0, 0)),
        b_spec=pl.BlockSpec((s, wcol), lambda n: (0, n)),
        o_spec=pl.BlockSpec((None, D_MODEL, wcol), lambda n: (n, 0, 0)),
        o_shape=(N_CHIPS, D_MODEL, wcol), o_dtype=BF16,
        dims=TN, nk=1, kaxis=None, acc_shape=None, name="w_in_grad" + tag)
    in_sums = _chip_sums(l, dict(w_in=gw_in), pos, ("w_in",)) if hide else None
    dh1, received = _proj_bwd(dproj, wg["w_in"], "proj_bwd" + tag, (in_sums, ("w_in",)) if hide else None)
    if hide:
        done[("w_in",)] = (in_sums, received)
    big = {} if hide else dict(w_in=gw_in, w_up=gw_up, w_out=gw_out, w_down=gw_down)
    return dx1, dh1, big, g, scattered, done


SMALL = ("pre_mix_norm", "v_norm_g", "v_norm_b", "w_spatial", "b_spatial", "out_norm_a", "out_norm_b",
         "post_mix_norm", "pre_ffn_norm", "conv_b", "post_ffn_norm")
BIG = ("w_in", "w_out", "w_up", "w_down")
DEPTH = 2


def _layer_params(l, small, conv_w_full):
    p = {n: small[n][l].reshape(1, -1) for n in SMALL if n not in ("w_spatial", "b_spatial")}
    p["w_spatial"] = small["w_spatial"][l]
    p["bs_full"] = jnp.repeat(small["b_spatial"][l].T, GROUP_DIM, axis=1)
    p["cw3"] = conv_w_full[l].reshape(3, 2, D_FF).transpose(1, 0, 2)
    p["cb3"] = small["conv_b"][l].reshape(2, 1, D_FF)
    return p


def _mesh_pos():
    return lax.axis_index("x"), lax.axis_index("y"), lax.axis_index("c")


def _other_chips(x, y):
    return [(1 - x, y), (x, 1 - y), (1 - x, 1 - y)]


def _gathered_shapes(blocks):
    return [jax.ShapeDtypeStruct((N_CHIPS, 1) + a.shape, a.dtype) for a in blocks]


def _gather_sems(nw):
    n = 2 * nw * (N_CHIPS - 1) + nw
    return [pltpu.SemaphoreType.DMA((n,)), pltpu.SemaphoreType.DMA((n,))]


def _gather_steps(ins, outs, send, recv):
    nw, nrel = len(ins), N_CHIPS - 1
    x, y, c = _mesh_pos()
    mine, sibling, chips = 2 * x + y, (x, y, 1 - c), _other_chips(x, y)

    def copy(src, dst, slot, to):
        return pltpu.make_async_remote_copy(src_ref=src, dst_ref=dst, send_sem=send.at[slot],
                                            recv_sem=recv.at[slot], device_id=to, device_id_type=MESH)

    def half_rows(t, core):
        rows = ins[t].shape[0] // 2
        return pl.ds(pl.multiple_of(core * rows, rows), rows)

    def landing(t, chip, core):
        return outs[t].at[chip, 0, half_rows(t, core), :]

    slots = [(t, r, chip) for t in range(nw) for r, chip in enumerate(chips)]
    own = [copy(ins[t], outs[t].at[mine, 0], 2 * nw * nrel + t, sibling) for t in range(nw)]
    first = [copy(ins[t].at[half_rows(t, c), :], landing(t, mine, c), t * nrel + r, (px, py, c))
             for t, r, (px, py) in slots]
    relays = [copy(landing(t, 2 * px + py, c), landing(t, 2 * px + py, c), nw * nrel + t * nrel + r, sibling)
              for t, r, (px, py) in slots]

    def start():
        for cp in own + first:
            cp.start()

    def relay():
        for (t, r, (px, py)), cp in zip(slots, relays):
            copy(landing(t, 2 * px + py, c), landing(t, 2 * px + py, c), t * nrel + r, (px, py, c)).wait_recv()
            cp.start()

    def finish():
        for t, r, (px, py) in slots:
            passed = landing(t, 2 * px + py, 1 - c)
            copy(passed, passed, nw * nrel + t * nrel + r, sibling).wait_recv()
        for cp in first + relays:
            cp.wait_send()
        for cp in own:
            cp.wait()

    return start, relay, finish


def _gather_weights(blocks, name):
    nw = len(blocks)

    def body(*refs):
        start, relay, finish = _gather_steps(refs[:nw], refs[nw:2 * nw], *refs[2 * nw:])
        start()
        relay()
        finish()

    return pl.pallas_call(
        body, in_specs=[ANY] * nw, out_specs=[ANY] * nw, out_shape=_gathered_shapes(blocks),
        scratch_shapes=_gather_sems(nw), name=name)(*blocks)


HALF = 512

GRAD_GEOM = {"w_in": ("rows", D_MODEL, IN_COLS // N_CHIPS), "w_up": ("rows", D_MODEL, 2 * D_FF // N_CHIPS),
             "w_out": ("cols", D_MODEL, D_MODEL // N_CHIPS), "w_down": ("cols", D_FF, D_FF // N_CHIPS)}


def _exchange_shape(n):
    kind, a, b = GRAD_GEOM[n]
    return (N_CHIPS, HALF, b) if kind == "rows" else (a, HALF)


def _piece_shape(n):
    name, _, part = n.partition(":")
    kind, _, b = GRAD_GEOM[name]
    if part:
        assert kind == "rows"
        return (HALF, b // 2)
    return (HALF, b) if kind == "rows" else (b, HALF)


def _half_of(ref, n, core):
    if GRAD_GEOM[n][0] == "rows":
        return ref.at[:, pl.ds(pl.multiple_of(core * HALF, HALF), HALF), :]
    return ref.at[core]


def _piece_of(ref, n, chip):
    name, _, part = n.partition(":")
    kind, _, b = GRAD_GEOM[name]
    if part:
        return ref.at[chip, :, pl.ds(int(part) * (b // 2), b // 2)]
    return ref.at[chip] if kind == "rows" else ref.at[pl.ds(pl.multiple_of(chip * b, b), b), :]


def _pair_exchange(g, names, name):
    n = len(names)

    def body(*refs):
        send, recv = refs[2 * n:]
        x, y, c = _mesh_pos()
        o = 1 - c
        cps = [pltpu.make_async_remote_copy(src_ref=_half_of(refs[t], nm, o), dst_ref=refs[n + t], send_sem=send.at[t],
                                            recv_sem=recv.at[t], device_id=(x, y, o), device_id_type=MESH)
               for t, nm in enumerate(names)]
        for cp in cps:
            cp.start()
        for cp in cps:
            cp.wait()

    return pl.pallas_call(
        body, in_specs=[ANY] * n, out_specs=[ANY] * n,
        out_shape=[jax.ShapeDtypeStruct(_exchange_shape(nm), BF16) for nm in names],
        scratch_shapes=[pltpu.SemaphoreType.DMA((n,)), pltpu.SemaphoreType.DMA((n,))],
        name=name)(*[g[nm] for nm in names])


def _pair_sum(g, recv, pos, names, name_prefix):
    def add(a, b, grid, a_spec, b_spec, name):
        def body(pos_ref, a_ref, b_ref, o_ref):
            o_ref[...] = (a_ref[...].astype(F32) + b_ref[...].astype(F32)).astype(BF16)

        return pl.pallas_call(
            body, grid_spec=pltpu.PrefetchScalarGridSpec(
                num_scalar_prefetch=1, grid=grid, in_specs=[a_spec, b_spec], out_specs=b_spec),
            out_shape=jax.ShapeDtypeStruct(b.shape, BF16), compiler_params=_cparams("parallel"), name=name)(pos, a, b)

    out = []
    for nm, r in zip(names, recv):
        kind, rows, width = GRAD_GEOM[nm]
        if kind == "rows":
            out.append(add(g[nm], r, (N_CHIPS,), pl.BlockSpec((None, HALF, width), lambda j, pos: (j, pos[2], 0)),
                           pl.BlockSpec((None, HALF, width), lambda j, pos: (j, 0, 0)), f"{name_prefix}_{nm}"))
        else:
            out.append(add(g[nm], r, (rows // D_MODEL,), pl.BlockSpec((None, D_MODEL, HALF), lambda j, pos: (pos[2], j, 0)),
                           pl.BlockSpec((D_MODEL, HALF), lambda j, pos: (j, 0)), f"{name_prefix}_{nm}"))
    return out


def _scattered_shapes(names):
    return [jax.ShapeDtypeStruct((N_CHIPS - 1,) + _piece_shape(nm), BF16) for nm in names]


def _scatter_sems(n):
    return [pltpu.SemaphoreType.DMA((n * (N_CHIPS - 1),)), pltpu.SemaphoreType.DMA((n * (N_CHIPS - 1),))]


def _scatter_steps(sums, outs, send, recv, names):
    nrel = N_CHIPS - 1
    x, y, c = _mesh_pos()
    cps = []
    for r, (px, py) in enumerate(_other_chips(x, y)):
        for t, nm in enumerate(names):
            cps.append(pltpu.make_async_remote_copy(
                src_ref=_piece_of(sums[t], nm, 2 * px + py), dst_ref=outs[t].at[r], send_sem=send.at[t * nrel + r],
                recv_sem=recv.at[t * nrel + r], device_id=(px, py, c), device_id_type=MESH))

    def start():
        for cp in cps:
            cp.start()

    def finish():
        for cp in cps:
            cp.wait()

    return start, finish


def _chip_scatter(sums, names, name):
    n = len(names)

    def body(*refs):
        start, finish = _scatter_steps(refs[:n], refs[n:2 * n], *refs[2 * n:], names)
        start()
        finish()

    return pl.pallas_call(
        body, in_specs=[ANY] * n, out_specs=[ANY] * n, out_shape=_scattered_shapes(names),
        scratch_shapes=_scatter_sems(n), name=name)(*sums)


def _chip_sum(sums, recv, pos, names, name_prefix):
    def add(a, b, a_spec, shape, name):
        def body(pos_ref, a_ref, b_ref, o_ref):
            tot = a_ref[...].astype(F32)
            for r in range(N_CHIPS - 1):
                tot = tot + b_ref[r].astype(F32)
            o_ref[...] = tot

        return pl.pallas_call(
            body, grid_spec=pltpu.PrefetchScalarGridSpec(
                num_scalar_prefetch=1, grid=(1,), in_specs=[a_spec, pl.BlockSpec(b.shape, lambda i, pos: (0, 0, 0))],
                out_specs=pl.BlockSpec((None,) + shape, lambda i, pos: (pos[2], 0, 0))),
            out_shape=jax.ShapeDtypeStruct((2,) + shape, F32), compiler_params=_cparams("arbitrary"),
            name=name)(pos, a, b)

    chip = lambda pos: 2 * pos[0] + pos[1]
    out = []
    for nm, a, b in zip(names, sums, recv):
        shape = _piece_shape(nm)
        if GRAD_GEOM[nm][0] == "rows":
            spec = pl.BlockSpec((None,) + shape, lambda i, pos: (chip(pos), 0, 0))
        else:
            spec = pl.BlockSpec(shape, lambda i, pos: (chip(pos), 0))
        out.append(add(a, b, spec, shape, f"{name_prefix}_{nm}"))
    return out


def _pair_share(totals, name):
    n = len(totals)

    def body(*refs):
        ins, outs = refs[:n], refs[n:2 * n]
        send, recv = refs[2 * n:]
        x, y, c = _mesh_pos()
        o = 1 - c
        cps = [pltpu.make_async_remote_copy(src_ref=ins[t].at[c], dst_ref=outs[t].at[c], send_sem=send.at[t],
                                            recv_sem=recv.at[t], device_id=(x, y, o), device_id_type=MESH)
               for t in range(n)]
        for cp in cps:
            cp.start()
        for t in range(n):
            pltpu.make_async_remote_copy(src_ref=ins[t].at[o], dst_ref=outs[t].at[o], send_sem=send.at[t],
                                         recv_sem=recv.at[t], device_id=(x, y, o), device_id_type=MESH).wait_recv()
        for cp in cps:
            cp.wait_send()

    return pl.pallas_call(
        body, in_specs=[ANY] * n, out_specs=[ANY] * n,
        out_shape=[jax.ShapeDtypeStruct(t.shape, t.dtype) for t in totals],
        scratch_shapes=[pltpu.SemaphoreType.DMA((n,)), pltpu.SemaphoreType.DMA((n,))],
        input_output_aliases={t: t for t in range(n)}, name=name)(*totals)


def _chip_sums(l, g, pos, names):
    tag = f"l{l}_" + "_".join(names)
    recv = _pair_exchange(g, names, "pair_exchange_" + tag)
    return _pair_sum(g, recv, pos, names, "pair_sum_" + tag)


def _gradient_shards(l, sums, scattered, pos, names):
    tag = f"l{l}_" + "_".join(names)
    halves = _pair_share(_chip_sum(sums, scattered, pos, names, "chip_sum_" + tag), "pair_share_" + tag)
    out = {}
    for nm, t in zip(names, halves):
        rows, cols = _piece_shape(nm)
        out[nm] = t.reshape(2 * rows, cols) if GRAD_GEOM[nm][0] == "rows" else t.transpose(1, 0, 2).reshape(rows, 2 * cols)
    return out


N_DEV = 8


def _allreduce_small(packed, name):
    rows = packed.shape[0]

    def body(x_ref, out_ref, gath, send_sems, recv_sems, local_sem):
        x, y, c = _mesh_pos()
        me, sibling = (x, y, c), (x, y, 1 - c)
        chips = _other_chips(x, y)

        def blk(px, py, pc):
            return gath.at[pl.ds(pl.multiple_of((4 * px + 2 * py + pc) * rows, 8), rows), :]

        def copy(k, block, to, src=None):
            return pltpu.make_async_remote_copy(
                src_ref=blk(*block) if src is None else src, dst_ref=blk(*block), send_sem=send_sems.at[k],
                recv_sem=recv_sems.at[k], device_id=to, device_id_type=MESH)

        mine = pltpu.make_async_copy(x_ref, blk(*me), local_sem)
        mine.start()
        first = [copy(0, me, sibling, src=x_ref)]
        first += [copy(1 + j, me, (*chip, c), src=x_ref) for j, chip in enumerate(chips)]
        for cp in first:
            cp.start()
        passed = [copy(4 + j, (*chip, c), sibling) for j, chip in enumerate(chips)]
        for j, chip in enumerate(chips):
            copy(1 + j, (*chip, c), me).wait_recv()
            passed[j].start()
        copy(0, sibling, me).wait_recv()
        for j, chip in enumerate(chips):
            copy(4 + j, (*chip, 1 - c), me).wait_recv()
        for cp in first + passed:
            cp.wait_send()
        mine.wait()
        tot = gath[0:rows, :]
        for d in range(1, N_DEV):
            tot = tot + gath[d * rows:(d + 1) * rows, :]
        out_ref[...] = tot

    vmem = pl.BlockSpec(memory_space=pltpu.VMEM)
    return pl.pallas_call(
        body, in_specs=[vmem], out_specs=vmem, out_shape=jax.ShapeDtypeStruct((rows, LANES), F32),
        scratch_shapes=[pltpu.VMEM((N_DEV * rows, LANES), F32), pltpu.SemaphoreType.DMA((7,)),
                        pltpu.SemaphoreType.DMA((7,)), pltpu.SemaphoreType.DMA],
        compiler_params=pltpu.CompilerParams(vmem_limit_bytes=VMEM_LIMIT_BYTES),
        name=name)(packed)


def _adamw(w, g, m, v, name):
    rows, cols = w.shape
    tr = 256 if rows % 256 == 0 else rows

    def body(w_ref, g_ref, m_ref, v_ref, d_ref, mo_ref, vo_ref):
        gv = g_ref[...]
        mn = ADAM_B1 * m_ref[...] + (1.0 - ADAM_B1) * gv
        vn = ADAM_B2 * v_ref[...] + (1.0 - ADAM_B2) * (gv * gv)
        m_hat = mn / (1.0 - ADAM_B1 ** ADAM_STEP)
        v_hat = vn / (1.0 - ADAM_B2 ** ADAM_STEP)
        d_ref[...] = -ADAM_LR * (m_hat / (jnp.sqrt(v_hat) + ADAM_EPS) + ADAM_WD * w_ref[...])
        mo_ref[...] = mn
        vo_ref[...] = vn

    spec = pl.BlockSpec((tr, cols), lambda i: (i, 0))
    return pl.pallas_call(
        body, grid=(rows // tr,), in_specs=[spec] * 4, out_specs=[spec] * 3,
        out_shape=[jax.ShapeDtypeStruct((rows, cols), F32)] * 3, compiler_params=_cparams("parallel"),
        name=name)(w, g, m, v)


def _adamw_nd(w, g, m, v, name):
    cols = w.shape[-1] if w.shape[-1] % LANES == 0 else LANES
    outs = _adamw(*(t.reshape(-1, cols) for t in (w, g, m, v)), name)
    return tuple(t.reshape(w.shape) for t in outs)


def _pack(arrays):
    return jnp.concatenate([a.reshape(-1, LANES) for a in arrays], axis=0)


def _unpack(packed, shapes):
    out, row = [], 0
    for sh in shapes:
        n = math.prod(sh) // LANES
        out.append(packed[row:row + n].reshape(sh))
        row += n
    return out


WEIGHTS = ("pre_mix_norm", "w_in", "v_norm_g", "v_norm_b", "w_spatial", "b_spatial", "out_norm_a", "out_norm_b",
           "w_out", "post_mix_norm", "pre_ffn_norm", "w_up", "conv_w", "conv_b", "w_down", "post_ffn_norm")


def kernel(x, pre_mix_norm, w_in, v_norm_g, v_norm_b, w_spatial, b_spatial, out_norm_a, out_norm_b, w_out, post_mix_norm, pre_ffn_norm, w_up, conv_w, conv_b, w_down, post_ffn_norm, loss_target, m_pre_mix_norm, m_w_in, m_v_norm_g, m_v_norm_b, m_w_spatial, m_b_spatial, m_out_norm_a, m_out_norm_b, m_w_out, m_post_mix_norm, m_pre_ffn_norm, m_w_up, m_conv_w, m_conv_b, m_w_down, m_post_ffn_norm, v_pre_mix_norm, v_w_in, v_v_norm_g, v_v_norm_b, v_w_spatial, v_b_spatial, v_out_norm_a, v_out_norm_b, v_w_out, v_post_mix_norm, v_pre_ffn_norm, v_w_up, v_conv_w, v_conv_b, v_w_down, v_post_ffn_norm):
    w = dict(pre_mix_norm=pre_mix_norm, w_in=w_in, v_norm_g=v_norm_g, v_norm_b=v_norm_b, w_spatial=w_spatial,
             b_spatial=b_spatial, out_norm_a=out_norm_a, out_norm_b=out_norm_b, w_out=w_out,
             post_mix_norm=post_mix_norm, pre_ffn_norm=pre_ffn_norm, w_up=w_up, conv_w=conv_w, conv_b=conv_b,
             w_down=w_down, post_ffn_norm=post_ffn_norm)
    m = dict(pre_mix_norm=m_pre_mix_norm, w_in=m_w_in, v_norm_g=m_v_norm_g, v_norm_b=m_v_norm_b,
             w_spatial=m_w_spatial, b_spatial=m_b_spatial, out_norm_a=m_out_norm_a, out_norm_b=m_out_norm_b,
             w_out=m_w_out, post_mix_norm=m_post_mix_norm, pre_ffn_norm=m_pre_ffn_norm, w_up=m_w_up,
             conv_w=m_conv_w, conv_b=m_conv_b, w_down=m_w_down, post_ffn_norm=m_post_ffn_norm)
    v = dict(pre_mix_norm=v_pre_mix_norm, w_in=v_w_in, v_norm_g=v_v_norm_g, v_norm_b=v_v_norm_b,
             w_spatial=v_w_spatial, b_spatial=v_b_spatial, out_norm_a=v_out_norm_a, out_norm_b=v_out_norm_b,
             w_out=v_w_out, post_mix_norm=v_post_mix_norm, pre_ffn_norm=v_pre_ffn_norm, w_up=v_w_up,
             conv_w=v_conv_w, conv_b=v_conv_b, w_down=v_w_down, post_ffn_norm=v_post_ffn_norm)
    pos = jnp.stack([lax.axis_index("x"), lax.axis_index("y"), lax.axis_index("c")]).astype(jnp.int32)
    chip = 2 * lax.axis_index("x") + lax.axis_index("y")

    cw_cols = conv_w.shape[-1]
    blocks = [{n: w[n][l].astype(BF16) for n in BIG} for l in range(DEPTH)]
    w_in0, cw_all = _gather_weights([blocks[0]["w_in"], conv_w.reshape(-1, LANES)], "gather_w_in_l0")
    wg = dict(w_in=w_in0)
    conv_w_full = cw_all.reshape(N_CHIPS, DEPTH, 3, cw_cols).transpose(1, 2, 0, 3).reshape(DEPTH, 3, 2 * D_FF)

    small = {n: w[n] for n in SMALL}
    xs, target = x[0], loss_target[0]
    tabs = _rope_tables(xs.shape[0])
    params = [_layer_params(l, small, conv_w_full) for l in range(DEPTH)]
    saved, wgs = [], []
    xin = xs
    h = _rms_cast(xin, params[0]["pre_mix_norm"], "pre_mix_l0")
    for l in range(DEPTH):
        sv, gathered, wg = _layer_forward(l, xin, h, params[l], wg, tabs,
                                          [blocks[l + 1][n] for n in BIG] if l + 1 < DEPTH else None,
                                          blocks[0] if l == 0 else None,
                                          params[l + 1]["pre_mix_norm"] if l + 1 < DEPTH else None)
        saved.append(sv)
        wgs.append(wg)
        if l + 1 < DEPTH:
            wg = dict(zip(BIG, gathered))
            xin, h = sv["x2"], sv["h_next"]
    loss_part, dx, df, g_post = _loss_norm_bwd(saved[-1]["x1"], saved[-1]["f"], params[-1]["post_ffn_norm"], target,
                                               "loss")
    smalls, shards = [None] * DEPTH, [{} for _ in range(DEPTH)]
    pending = None
    for l in reversed(range(DEPTH)):
        dx1, dh1, big, smalls[l], scattered, done = _layer_backward(l, dx, df, saved[l], params[l], wgs[l], tabs, pos,
                                                                    pending[1:] if pending else None, hide=l == 0)
        smalls[l]["post_ffn_norm"] = g_post
        if l > 0:
            dx, smalls[l]["pre_mix_norm"], df, g_post = _norm_bwd_in_out(
                dx1, dh1, saved[l]["x0"], params[l]["pre_mix_norm"], saved[l - 1]["f"], params[l - 1]["post_ffn_norm"],
                f"norm_bwd_in_out_l{l}")
        else:
            dx, smalls[l]["pre_mix_norm"] = _norm_bwd_in(dx1, dh1, saved[l]["x0"], params[l]["pre_mix_norm"],
                                                         "norm_bwd_in_l0")
        if pending:
            shards[pending[0]].update(_gradient_shards(pending[0], pending[1], scattered, pos, pending[2]))
        for names, (sums, received) in done.items():
            shards[l].update(_gradient_shards(l, sums, received, pos, names))
        names = tuple(big)
        pending = (l, _chip_sums(l, big, pos, names), names) if names else None
    if pending:
        shards[pending[0]].update(_gradient_shards(
            pending[0], pending[1], _chip_scatter(pending[1], pending[2], f"chip_scatter_l{pending[0]}"), pos,
            pending[2]))

    small_shapes = [w[n].shape for n in SMALL]
    stacked = [jnp.stack([smalls[l][n].reshape(w[n].shape[1:]) for l in range(DEPTH)]) for n in SMALL]
    cw_grad = jnp.stack([smalls[l]["conv_w"] for l in range(DEPTH)])
    packed = _pack(stacked + [cw_grad, loss_part])
    total = _allreduce_small(packed, "allreduce_small")
    parts = _unpack(total, small_shapes + [cw_grad.shape, (8, LANES)])
    g_small = dict(zip(SMALL, parts[:len(SMALL)]))
    loss = parts[-1][0, 0]
    g_conv_w = lax.dynamic_slice(parts[-2], (0, 0, chip * cw_cols), conv_w.shape)

    grads = {n: jnp.stack([shards[l][n] for l in range(DEPTH)]) for n in BIG}
    grads.update(g_small)
    grads["conv_w"] = g_conv_w

    dp, mp, vp = _adamw(_pack([w[n] for n in SMALL]), _pack([g_small[n] for n in SMALL]),
                        _pack([m[n] for n in SMALL]), _pack([v[n] for n in SMALL]), "adamw_small")
    delta = dict(zip(SMALL, _unpack(dp, small_shapes)))
    new_m = dict(zip(SMALL, _unpack(mp, small_shapes)))
    new_v = dict(zip(SMALL, _unpack(vp, small_shapes)))
    for n in BIG + ("conv_w",):
        delta[n], new_m[n], new_v[n] = _adamw_nd(w[n], grads[n], m[n], v[n], "adamw_" + n)

    return (loss, dx[None], *[grads[n] for n in WEIGHTS], *[delta[n] for n in WEIGHTS],
            *[new_m[n] for n in WEIGHTS], *[new_v[n] for n in WEIGHTS])
```

```python
import functools
import math

import jax
import jax.numpy as jnp
import numpy as np
from jax import lax
from jax.experimental import pallas as pl
from jax.experimental.pallas import tpu as pltpu

F32 = jnp.float32
BF16 = jnp.bfloat16
MESH = pl.DeviceIdType.MESH

D_MODEL = 1024
A_WIDTH = 512
A_GROUPS = 4
GROUP_DIM = 128
CHUNK = 128
B_WIDTH = 512
HEAD_DIM = 64
ROT_DIM = 16
ROPE_THETA = 500000.0
DILATIONS = (1, 4, 16)
BAND = 128
IN_COLS = 2560
D_FF = 4096
EPS = 1e-6
NEG_INF = -1e30
N_CHIPS = 4
LANES = 128

ADAM_LR = 0.001
ADAM_B1 = 0.9
ADAM_B2 = 0.999
ADAM_EPS = 1e-08
ADAM_WD = 0.01
ADAM_STEP = 10

VMEM_LIMIT_BYTES = 56 * 1024 * 1024
RSQRT2 = 0.7071067811865476
INV_SQRT_2PI = 0.3989422804014327
GELU_C = 0.7978845608028654
GELU_A = 0.044715

ANY = pl.BlockSpec(memory_space=pl.ANY)
NN = ((1,), (0,))
NT = ((1,), (1,))
TN = ((0,), (0,))


def _cparams(*sem):
    return pltpu.CompilerParams(dimension_semantics=sem, vmem_limit_bytes=VMEM_LIMIT_BYTES)


def _dot(a, b, dims):
    return lax.dot_general(a, b, (dims, ((), ())), preferred_element_type=F32)


def _rsq_mean(a):
    return lax.rsqrt(jnp.mean(a * a, axis=-1, keepdims=True) + EPS)


def _rms_bwd(a, r, g, dz):
    t = dz * g
    da = r * t - a * (r * r * r) * jnp.mean(t * a, axis=-1, keepdims=True)
    return da, dz * a * r


def _colsum(a):
    return jnp.sum(a, axis=0, keepdims=True)


def _gelu_tanh(x):
    u = x * x
    t = jnp.tanh(x * (GELU_C + (GELU_C * GELU_A) * u))
    hx = 0.5 * x
    act = hx + hx * t
    grad = 0.5 + 0.5 * t + (hx - hx * t * t) * (GELU_C + (3.0 * GELU_C * GELU_A) * u)
    return act, grad


def _grid_edges(grid):
    ids = [pl.program_id(ax) for ax in range(len(grid))]
    first = functools.reduce(jnp.logical_and, [i == 0 for i in ids])
    last = functools.reduce(jnp.logical_and, [i == n - 1 for i, n in zip(ids, grid)])
    return first, last


def _matmul(a, b, *, grid, a_spec, b_spec, o_spec, o_shape, o_dtype, dims, nk, kaxis, acc_shape, name, b_2d=None,
            halves=False, scatter=None, gather=None):
    assert scatter is None or gather is None
    ns = len(scatter[0]) if scatter else len(gather) if gather else 0

    def body(*refs):
        a_ref, b_ref = refs[:2]
        o_ref = refs[2 + ns]
        scratch = refs[3 + 2 * ns:]
        if ns:
            first, last = _grid_edges(grid)
            if scatter:
                start, finish = _scatter_steps(refs[2:2 + ns], refs[3 + ns:3 + 2 * ns], scratch[-2], scratch[-1],
                                               scatter[1])
            else:
                start, relay, last_wait = _gather_steps(refs[2:2 + ns], refs[3 + ns:3 + 2 * ns], scratch[-2],
                                                        scratch[-1])

                def finish():
                    relay()
                    last_wait()
            pl.when(first)(start)
        def store(val):
            if halves:
                half = val.shape[1] // 2
                o_ref[0] = val[:, :half].astype(o_dtype)
                o_ref[1] = val[:, half:].astype(o_dtype)
            else:
                o_ref[...] = val.astype(o_dtype)

        bv = b_ref[...] if b_2d is None else b_ref[...].reshape(b_2d)
        part = _dot(a_ref[...], bv, dims)
        if nk == 1:
            store(part)
        else:
            acc = scratch[0]
            k = pl.program_id(kaxis)

            @pl.when(k == 0)
            def _():
                acc[...] = part

            @pl.when(k > 0)
            def _():
                acc[...] += part

            @pl.when(k == nk - 1)
            def _():
                store(acc[...])

        if ns:
            pl.when(last)(finish)

    sem = tuple("arbitrary" if (ns or (nk > 1 and ax == kaxis)) else "parallel" for ax in range(len(grid)))
    riding = list(scatter[0]) if scatter else list(gather or [])
    rider_shapes = _scattered_shapes(scatter[1]) if scatter else _gathered_shapes(riding)
    rider_sems = _scatter_sems(ns) if scatter else _gather_sems(ns) if gather else []
    res = pl.pallas_call(
        body, grid=grid, in_specs=[a_spec, b_spec] + [ANY] * ns, out_specs=[o_spec] + [ANY] * ns,
        out_shape=[jax.ShapeDtypeStruct(o_shape, o_dtype)] + rider_shapes,
        scratch_shapes=([pltpu.VMEM(acc_shape, F32)] if nk > 1 else []) + rider_sems,
        compiler_params=_cparams(*sem), name=name)(a, b, *riding)
    return (res[0], list(res[1:])) if ns else res[0]


def _mix_out_norm(mixed, w_out, x0, g_post, g_next, name):
    s, d = x0.shape
    tm = 512

    def body(a_ref, w_ref, x_ref, gp_ref, gn_ref, y_ref, x1_ref, h_ref):
        y = _dot(a_ref[...], w_ref[...].reshape(d, d), NN)
        y_ref[...] = y
        x1 = x_ref[...] + y * _rsq_mean(y) * gp_ref[...]
        x1_ref[...] = x1
        h_ref[...] = (x1 * _rsq_mean(x1) * gn_ref[...]).astype(BF16)

    row = pl.BlockSpec((tm, d), lambda i: (i, 0))
    vec = pl.BlockSpec((1, d), lambda i: (0, 0))
    return pl.pallas_call(
        body, grid=(s // tm,),
        in_specs=[row, pl.BlockSpec((N_CHIPS, None, d // N_CHIPS, d), lambda i: (0, 0, 0, 0)), row, vec, vec],
        out_specs=[row, row, row],
        out_shape=[jax.ShapeDtypeStruct((s, d), F32), jax.ShapeDtypeStruct((s, d), F32),
                   jax.ShapeDtypeStruct((s, d), BF16)],
        compiler_params=_cparams("parallel"), name=name)(mixed, w_out, x0, g_post, g_next)


def _proj_bwd(dproj, w_in, name, scatter=None):
    s = dproj.shape[0]
    wcol = IN_COLS // N_CHIPS
    ns = 0 if scatter is None else len(scatter[0])

    def body(*refs):
        a_ref, w_ref = refs[:2]
        o_ref = refs[2 + ns]
        if ns:
            start, finish = _scatter_steps(refs[2:2 + ns], refs[3 + ns:3 + 2 * ns], *refs[3 + 2 * ns:], scatter[1])
            first, last = _grid_edges((s // TMM,))
            pl.when(first)(start)
        acc = _dot(a_ref[:, :wcol], w_ref[0], NT)
        for j in range(1, N_CHIPS):
            acc = acc + _dot(a_ref[:, j * wcol:(j + 1) * wcol], w_ref[j], NT)
        o_ref[...] = acc
        if ns:
            pl.when(last)(finish)

    res = pl.pallas_call(
        body, grid=(s // TMM,),
        in_specs=[pl.BlockSpec((TMM, IN_COLS), lambda i: (i, 0)),
                  pl.BlockSpec((N_CHIPS, None, D_MODEL, wcol), lambda i: (0, 0, 0, 0))] + [ANY] * ns,
        out_specs=[pl.BlockSpec((TMM, D_MODEL), lambda i: (i, 0))] + [ANY] * ns,
        out_shape=[jax.ShapeDtypeStruct((s, D_MODEL), F32)] + (_scattered_shapes(scatter[1]) if ns else []),
        scratch_shapes=_scatter_sems(ns) if ns else [],
        compiler_params=_cparams("arbitrary" if ns else "parallel"), name=name)(dproj, w_in,
                                                                              *(scatter[0] if ns else []))
    return res[0], list(res[1:])


TM = 1024
TMM = 1024


TR = 256


def _row_spec(width, col=0):
    return pl.BlockSpec((TR, width), lambda i, col=col: (i, col))


def _vec_spec(width):
    return pl.BlockSpec((1, width), lambda i: (0, 0))


def _rms_cast(x, g, name):
    s, d = x.shape

    def body(x_ref, g_ref, h_ref):
        a = x_ref[...]
        h_ref[...] = (a * _rsq_mean(a) * g_ref[...]).astype(BF16)

    return pl.pallas_call(
        body, grid=(s // TR,), in_specs=[_row_spec(d), _vec_spec(d)], out_specs=_row_spec(d),
        out_shape=jax.ShapeDtypeStruct((s, d), BF16), compiler_params=_cparams("parallel"), name=name)(x, g)


def _acc_init(refs):
    @pl.when(pl.program_id(0) == 0)
    def _():
        for r in refs:
            r[...] = jnp.zeros_like(r)


def _loss_norm_bwd(x1, f, g_post, target, name):
    s, d = x1.shape

    def body(x_ref, f_ref, gp_ref, t_ref, loss_ref, dx_ref, df_ref, dg_ref):
        _acc_init([loss_ref, dg_ref])
        fv = f_ref[...]
        r = _rsq_mean(fv)
        err = x_ref[...] + fv * r * gp_ref[...] - t_ref[...]
        dx = err * (1.0 / d)
        dx_ref[...] = dx
        part = 0.5 * jnp.sum(jnp.mean(err * err, axis=-1, keepdims=True), axis=0, keepdims=True)
        loss_ref[...] += jnp.broadcast_to(part, loss_ref.shape)
        da, dgt = _rms_bwd(fv, r, gp_ref[...], dx)
        df_ref[...] = da.astype(BF16)
        dg_ref[...] += _colsum(dgt)

    return pl.pallas_call(
        body, grid=(s // TR,), in_specs=[_row_spec(d), _row_spec(d), _vec_spec(d), _row_spec(d)],
        out_specs=[pl.BlockSpec((8, LANES), lambda i: (0, 0)), _row_spec(d), _row_spec(d), _vec_spec(d)],
        out_shape=[jax.ShapeDtypeStruct((8, LANES), F32), jax.ShapeDtypeStruct((s, d), F32),
                   jax.ShapeDtypeStruct((s, d), BF16), jax.ShapeDtypeStruct((1, d), F32)],
        compiler_params=_cparams("arbitrary"), name=name)(x1, f, g_post, target)


def _norm_bwd_mid(dx2, dh2, x1, g_pf, y1, g_pm, name):
    s, d = dx2.shape

    def body(dx2_ref, dh_ref, x1_ref, gpf_ref, y1_ref, gpm_ref, dx1_ref, dy1_ref, dgpf_ref, dgpm_ref):
        _acc_init([dgpf_ref, dgpm_ref])
        x1 = x1_ref[...]
        da, dgt = _rms_bwd(x1, _rsq_mean(x1), gpf_ref[...], dh_ref[...])
        dx1 = dx2_ref[...] + da
        dx1_ref[...] = dx1
        dgpf_ref[...] += _colsum(dgt)
        y1 = y1_ref[...]
        dy, dgt2 = _rms_bwd(y1, _rsq_mean(y1), gpm_ref[...], dx1)
        dy1_ref[...] = dy.astype(BF16)
        dgpm_ref[...] += _colsum(dgt2)

    return pl.pallas_call(
        body, grid=(s // TR,),
        in_specs=[_row_spec(d), _row_spec(d), _row_spec(d), _vec_spec(d), _row_spec(d), _vec_spec(d)],
        out_specs=[_row_spec(d), _row_spec(d), _vec_spec(d), _vec_spec(d)],
        out_shape=[jax.ShapeDtypeStruct((s, d), F32), jax.ShapeDtypeStruct((s, d), BF16),
                   jax.ShapeDtypeStruct((1, d), F32), jax.ShapeDtypeStruct((1, d), F32)],
        compiler_params=_cparams("arbitrary"), name=name)(dx2, dh2, x1, g_pf, y1, g_pm)


def _norm_bwd_in_out(dx1, dh1, x0, g1, f_below, g_post_below, name):
    s, d = dx1.shape

    def body(dx1_ref, dh_ref, x0_ref, g_ref, f_ref, gp_ref, dx0_ref, dg_ref, df_ref, dgp_ref):
        _acc_init([dg_ref, dgp_ref])
        x0 = x0_ref[...]
        da, dgt = _rms_bwd(x0, _rsq_mean(x0), g_ref[...], dh_ref[...])
        dx0 = dx1_ref[...] + da
        dx0_ref[...] = dx0
        dg_ref[...] += _colsum(dgt)
        fv = f_ref[...]
        db, dgt2 = _rms_bwd(fv, _rsq_mean(fv), gp_ref[...], dx0)
        df_ref[...] = db.astype(BF16)
        dgp_ref[...] += _colsum(dgt2)

    return pl.pallas_call(
        body, grid=(s // TR,),
        in_specs=[_row_spec(d), _row_spec(d), _row_spec(d), _vec_spec(d), _row_spec(d), _vec_spec(d)],
        out_specs=[_row_spec(d), _vec_spec(d), _row_spec(d), _vec_spec(d)],
        out_shape=[jax.ShapeDtypeStruct((s, d), F32), jax.ShapeDtypeStruct((1, d), F32),
                   jax.ShapeDtypeStruct((s, d), BF16), jax.ShapeDtypeStruct((1, d), F32)],
        compiler_params=_cparams("arbitrary"), name=name)(dx1, dh1, x0, g1, f_below, g_post_below)


def _norm_bwd_in(dx1, dh1, x0, g1, name):
    s, d = dx1.shape

    def body(dx1_ref, dh_ref, x0_ref, g_ref, dx0_ref, dg_ref):
        _acc_init([dg_ref])
        x0 = x0_ref[...]
        da, dgt = _rms_bwd(x0, _rsq_mean(x0), g_ref[...], dh_ref[...])
        dx0_ref[...] = dx1_ref[...] + da
        dg_ref[...] += _colsum(dgt)

    return pl.pallas_call(
        body, grid=(s // TR,), in_specs=[_row_spec(d), _row_spec(d), _row_spec(d), _vec_spec(d)],
        out_specs=[_row_spec(d), _vec_spec(d)],
        out_shape=[jax.ShapeDtypeStruct((s, d), F32), jax.ShapeDtypeStruct((1, d), F32)],
        compiler_params=_cparams("arbitrary"), name=name)(dx1, dh1, x0, g1)


def _tril_mask():
    row = lax.broadcasted_iota(jnp.int32, (CHUNK, CHUNK), 0)
    col = lax.broadcasted_iota(jnp.int32, (CHUNK, CHUNK), 1)
    return row >= col


def _gating_forward(pa, gv, bv, wt, bsf):
    er = lax.erf(pa * RSQRT2)
    za = 0.5 * pa * (1.0 + er)
    u = za[:, :A_WIDTH]
    va = za[:, A_WIDTH:]
    xc = va - jnp.mean(va, axis=-1, keepdims=True)
    rs = lax.rsqrt(jnp.mean(xc * xc, axis=-1, keepdims=True) + EPS)
    vn = xc * rs
    vlb = (vn * gv + bv).astype(BF16)
    sg = jnp.concatenate(
        [_dot(wt[g], vlb[:, g * GROUP_DIM:(g + 1) * GROUP_DIM], NN) for g in range(A_GROUPS)], axis=1) + bsf
    return er, u, rs, vn, vlb, sg


def _masked_ws(ws_ref):
    mask = _tril_mask()
    return [jnp.where(mask, ws_ref[g], 0.0).astype(BF16) for g in range(A_GROUPS)]


def _mixer_a_fwd(proj, gv, bv, ws, bsf, ga, name):
    s = proj.shape[0]

    def body(p_ref, gv_ref, bv_ref, ws_ref, bs_ref, ga_ref, o_ref):
        wt = _masked_ws(ws_ref)
        for ch in range(TR // CHUNK):
            rows = slice(ch * CHUNK, (ch + 1) * CHUNK)
            _, u, _, _, _, sg = _gating_forward(p_ref[rows, :].astype(F32), gv_ref[...], bv_ref[...], wt, bs_ref[...])
            oa = u * sg
            o_ref[rows, :] = (oa * _rsq_mean(oa) * ga_ref[...]).astype(BF16)

    return pl.pallas_call(
        body, grid=(s // TR,),
        in_specs=[_row_spec(2 * A_WIDTH), _vec_spec(A_WIDTH), _vec_spec(A_WIDTH),
                  pl.BlockSpec((A_GROUPS, CHUNK, CHUNK), lambda i: (0, 0, 0)),
                  pl.BlockSpec((CHUNK, A_WIDTH), lambda i: (0, 0)), _vec_spec(A_WIDTH)],
        out_specs=_row_spec(A_WIDTH), out_shape=jax.ShapeDtypeStruct((s, A_WIDTH + B_WIDTH), BF16),
        compiler_params=_cparams("parallel"), name=name)(proj, gv, bv, ws, bsf, ga)


def _mixer_a_bwd(proj, dmixed, gv, bv, ws, bsf, ga, name, scatter=None):
    s = proj.shape[0]
    nsteps = s // TR
    ns = 0 if scatter is None else len(scatter[0])

    def body(*refs):
        p_ref, dm_ref, gv_ref, bv_ref, ws_ref, bs_ref, ga_ref = refs[:7]
        dp_ref, dga_ref, dgv_ref, dbv_ref, dbs_ref, dws_ref = refs[7 + ns:13 + ns]
        if ns:
            start, finish = _scatter_steps(refs[7:7 + ns], refs[13 + ns:13 + 2 * ns], *refs[13 + 2 * ns:], scatter[1])
            first, last = _grid_edges((nsteps,))
            pl.when(first)(start)
        _acc_init([dga_ref, dgv_ref, dbv_ref, dbs_ref, dws_ref])
        mask = _tril_mask()
        wt = _masked_ws(ws_ref)
        gvv = gv_ref[...]
        gav = ga_ref[...]
        for ch in range(TR // CHUNK):
            rows = slice(ch * CHUNK, (ch + 1) * CHUNK)
            pa = p_ref[rows, :].astype(F32)
            er, u, rs, vn, vlb, sg = _gating_forward(pa, gvv, bv_ref[...], wt, bs_ref[...])
            oa = u * sg
            doa, dgt = _rms_bwd(oa, _rsq_mean(oa), gav, dm_ref[rows, :])
            dga_ref[...] += _colsum(dgt)
            du = doa * sg
            dsg = doa * u
            dbs_ref[...] += dsg
            dsgb = dsg.astype(BF16)
            dvl = []
            for g in range(A_GROUPS):
                cols = slice(g * GROUP_DIM, (g + 1) * GROUP_DIM)
                dws_ref[g] += jnp.where(mask, _dot(dsgb[:, cols], vlb[:, cols], NT), 0.0)
                dvl.append(_dot(wt[g], dsgb[:, cols], TN))
            dvl = jnp.concatenate(dvl, axis=1)
            dgv_ref[...] += _colsum(dvl * vn)
            dbv_ref[...] += _colsum(dvl)
            dvn = dvl * gvv
            dva = rs * (dvn - jnp.mean(dvn, axis=-1, keepdims=True)
                        - vn * jnp.mean(dvn * vn, axis=-1, keepdims=True))
            gp = 0.5 * (1.0 + er) + pa * jnp.exp(-0.5 * pa * pa) * INV_SQRT_2PI
            dp_ref[rows, :] = (jnp.concatenate([du, dva], axis=1) * gp).astype(BF16)

        @pl.when(pl.program_id(0) == nsteps - 1)
        def _():
            for g in range(A_GROUPS):
                cols = slice(g * GROUP_DIM, (g + 1) * GROUP_DIM)
                tot = jnp.sum(dbs_ref[:, cols], axis=1, keepdims=True)
                dbs_ref[:, cols] = jnp.broadcast_to(tot, (CHUNK, GROUP_DIM))

        if ns:
            pl.when(last)(finish)

    full = lambda *shape: pl.BlockSpec(shape, lambda i: (0,) * len(shape))
    res = pl.pallas_call(
        body, grid=(nsteps,),
        in_specs=[_row_spec(2 * A_WIDTH), _row_spec(A_WIDTH), _vec_spec(A_WIDTH), _vec_spec(A_WIDTH),
                  full(A_GROUPS, CHUNK, CHUNK), full(CHUNK, A_WIDTH), _vec_spec(A_WIDTH)] + [ANY] * ns,
        out_specs=[_row_spec(2 * A_WIDTH), _vec_spec(A_WIDTH), _vec_spec(A_WIDTH), _vec_spec(A_WIDTH),
                   full(CHUNK, A_WIDTH), full(A_GROUPS, CHUNK, CHUNK)] + [ANY] * ns,
        out_shape=[jax.ShapeDtypeStruct((s, IN_COLS), BF16), jax.ShapeDtypeStruct((1, A_WIDTH), F32),
                   jax.ShapeDtypeStruct((1, A_WIDTH), F32), jax.ShapeDtypeStruct((1, A_WIDTH), F32),
                   jax.ShapeDtypeStruct((CHUNK, A_WIDTH), F32),
                   jax.ShapeDtypeStruct((A_GROUPS, CHUNK, CHUNK), F32)]
        + (_scattered_shapes(scatter[1]) if ns else []),
        scratch_shapes=_scatter_sems(ns) if ns else [],
        compiler_params=_cparams("arbitrary"), name=name)(proj, dmixed, gv, bv, ws, bsf, ga,
                                                          *(scatter[0] if ns else []))
    return res[:6] + (list(res[6:]),)


def _rope_tables(s):
    half = ROT_DIM // 2
    lane = jnp.arange(LANES) % HEAD_DIM
    inv = ROPE_THETA ** (-(2 * (lane % half)).astype(F32) / ROT_DIM)
    ang = jnp.arange(s, dtype=F32)[:, None] * inv[None, :]
    cos, sin = jnp.cos(ang), jnp.sin(ang)
    c = jnp.where(lane < ROT_DIM, cos, 1.0)
    s1 = jnp.where(lane < half, -sin, 0.0)
    s2 = jnp.where((lane >= half) & (lane < ROT_DIM), sin, 0.0)
    return c, s1, s2


def _lane_blocks(width):
    return [slice(b * LANES, (b + 1) * LANES) for b in range(width // LANES)]


CLASS_DILS = tuple(d for d in DILATIONS if d > 1)


def _class_shape(s, dil, dtype):
    return jax.ShapeDtypeStruct((dil, s // dil, B_WIDTH), dtype)


def _class_spec(dil):
    return pl.BlockSpec((dil, TR // dil, B_WIDTH), lambda i, *_: (0, i, 0))


NBLK = B_WIDTH // LANES
STAGE = pltpu.VMEM((NBLK, TR, LANES), F32)


def _stage_put(stage, value):
    for b, sl in enumerate(_lane_blocks(B_WIDTH)):
        stage[b] = value[:, sl]


def _stage_get(stage):
    return jnp.concatenate([stage[b] for b in range(NBLK)], axis=1)


def _store_classes(stage, dst_ref, dil):
    for b, sl in enumerate(_lane_blocks(B_WIDTH)):
        for r in range(dil):
            dst_ref[r, :, sl] = stage[b, pl.ds(r, TR // dil, stride=dil), :].astype(dst_ref.dtype)


def _load_classes(src_ref, stage, dil):
    for b, sl in enumerate(_lane_blocks(B_WIDTH)):
        for r in range(dil):
            stage[b, pl.ds(r, TR // dil, stride=dil), :] = src_ref[r, :, sl].astype(F32)
    return _stage_get(stage)


def _rope_fwd(proj, tabs, name, gather=None):
    s = proj.shape[0]
    half = ROT_DIM // 2
    scale = HEAD_DIM ** -0.5
    nlay = 1 + len(CLASS_DILS)
    ng = 0 if gather is None else len(gather)

    def body(q_ref, k_ref, v_ref, c_ref, s1_ref, s2_ref, *rest):
        outs, stage = rest[ng:ng + 3 * nlay], rest[2 * ng + 3 * nlay]
        if ng:
            start, relay, finish = _gather_steps(rest[:ng], rest[ng + 3 * nlay:2 * ng + 3 * nlay],
                                                 *rest[2 * ng + 3 * nlay + 1:])
            first, last = _grid_edges((s // TR,))
            pl.when(first)(start)
        c, s1, s2 = c_ref[...], s1_ref[...], s2_ref[...]
        for which, (src, mul) in enumerate(((q_ref, scale), (k_ref, 1.0), (v_ref, None))):
            if mul is None:
                _stage_put(stage, src[...].astype(F32))
            else:
                for b, sl in enumerate(_lane_blocks(B_WIDTH)):
                    a = src[:, sl].astype(F32)
                    r = a * c + pltpu.roll(a, LANES - half, 1) * s1 + pltpu.roll(a, half, 1) * s2
                    stage[b] = r * mul
            dst = outs[which * nlay:(which + 1) * nlay]
            dst[0][...] = _stage_get(stage).astype(BF16)
            for ref, d in zip(dst[1:], CLASS_DILS):
                _store_classes(stage, ref, d)

        if ng:
            @pl.when(last)
            def _():
                relay()
                finish()

    tab = pl.BlockSpec((TR, LANES), lambda i: (i, 0))
    lay_specs = [_row_spec(B_WIDTH)] + [_class_spec(d) for d in CLASS_DILS]
    lay_shapes = [jax.ShapeDtypeStruct((s, B_WIDTH), BF16)] + [_class_shape(s, d, BF16) for d in CLASS_DILS]
    outs = pl.pallas_call(
        body, grid=(s // TR,),
        in_specs=[_row_spec(B_WIDTH, 2), _row_spec(B_WIDTH, 3), _row_spec(B_WIDTH, 4), tab, tab, tab] + [ANY] * ng,
        out_specs=lay_specs * 3 + [ANY] * ng, out_shape=lay_shapes * 3 + _gathered_shapes(gather or []),
        scratch_shapes=[STAGE] + (_gather_sems(ng) if ng else []),
        compiler_params=_cparams("arbitrary" if ng else "parallel"), name=name)(proj, proj, proj, *tabs,
                                                                              *(gather or []))
    q, k, v = (dict(zip(DILATIONS, outs[w * nlay:(w + 1) * nlay])) for w in range(3))
    return q, k, v, list(outs[3 * nlay:])


def _as_classes(t):
    return t if t.ndim == 3 else t[None]


def _head_masks():
    lane = lax.broadcasted_iota(jnp.int32, (1, LANES), 1)
    return lane < HEAD_DIM, lane >= HEAD_DIM


def _stack_heads(t):
    lo, hi = _head_masks()
    zero = jnp.zeros_like(t)
    return jnp.concatenate([jnp.where(lo, t, zero), jnp.where(hi, t, zero)], axis=0)


MAX_SEGMENT_BLOCKS = 8


def _segment_masks(j):
    qi = lax.broadcasted_iota(jnp.int32, (BAND, 2 * BAND), 0)
    kj = lax.broadcasted_iota(jnp.int32, (BAND, 2 * BAND), 1)
    both = (kj >= qi) & (kj <= qi + BAND)
    own = kj[:, :BAND] <= qi[:, :BAND]
    head = both & ((kj >= BAND) | (j > 0))
    return tuple(jnp.concatenate([m, m], axis=0) for m in (own, both, head))


def _block_rows(g):
    return pl.ds(pl.multiple_of(g * BAND, BAND), BAND)


def _key_rows(g):
    return pl.ds(pl.multiple_of((g - 1) * BAND, BAND), 2 * BAND)


def _segments(n):
    nb = n // BAND
    seg = min(nb, MAX_SEGMENT_BLOCKS)
    return seg, nb // seg


def _segment_specs(seg):
    main = pl.BlockSpec((None, seg * BAND, B_WIDTH), lambda r, j: (r, j, 0))
    halo = pl.BlockSpec((None, BAND, B_WIDTH), lambda r, j: (r, jnp.maximum(j * seg - 1, 0), 0))
    return main, halo


def _attn_fwd(q, k, v, name, gather=None):
    dil, n, _ = q.shape
    seg, nseg = _segments(n)
    nh = 2 if nseg > 1 else 0
    ng = 0 if gather is None else len(gather)

    def body(*refs):
        q_ref, k_ref, v_ref = refs[:3]
        halos = refs[3:3 + nh]
        o_ref, l_ref = refs[3 + nh + ng:5 + nh + ng]
        if ng:
            start, relay, finish = _gather_steps(refs[3 + nh:3 + nh + ng], refs[5 + nh + ng:5 + nh + 2 * ng],
                                                 *refs[5 + nh + 2 * ng:])
            first, last = _grid_edges((dil, nseg))
            pl.when(first)(start)
        own, both, head = _segment_masks(pl.program_id(1))
        lo, _ = _head_masks()

        def block(rows, keys_of, valid):
            for sl in _lane_blocks(B_WIDTH):
                kk, vv = keys_of(sl)
                sc = jnp.where(valid, _dot(_stack_heads(q_ref[rows, sl]), kk, NT), NEG_INF)
                mx = jnp.max(sc, axis=1, keepdims=True)
                p = jnp.exp(sc - mx)
                den = jnp.sum(p, axis=1, keepdims=True)
                out = _dot(p.astype(BF16), vv, NN) / den
                lse = mx + jnp.log(den)
                o_ref[rows, sl] = jnp.where(lo, out[:BAND], out[BAND:]).astype(BF16)
                l_ref[rows, sl] = jnp.where(lo, lse[:BAND], lse[BAND:])

        if nh:
            block(_block_rows(0), lambda sl: (jnp.concatenate([halos[0][:, sl], k_ref[0:BAND, sl]], axis=0),
                                              jnp.concatenate([halos[1][:, sl], v_ref[0:BAND, sl]], axis=0)), head)
        else:
            block(_block_rows(0), lambda sl: (k_ref[0:BAND, sl], v_ref[0:BAND, sl]), own)

        @pl.loop(1, seg)
        def _(g):
            block(_block_rows(g), lambda sl: (k_ref[_key_rows(g), sl], v_ref[_key_rows(g), sl]), both)

        if ng:
            @pl.when(last)
            def _():
                relay()
                finish()

    main, halo = _segment_specs(seg)
    res = pl.pallas_call(
        body, grid=(dil, nseg), in_specs=[main] * 3 + [halo] * nh + [ANY] * ng, out_specs=[main, main] + [ANY] * ng,
        out_shape=[jax.ShapeDtypeStruct((dil, n, B_WIDTH), BF16), jax.ShapeDtypeStruct((dil, n, B_WIDTH), F32)]
        + _gathered_shapes(gather or []),
        scratch_shapes=_gather_sems(ng) if ng else [],
        compiler_params=_cparams(*(["arbitrary"] * 2 if ng else ["parallel"] * 2)), name=name)(
            q, k, v, *([k, v] if nh else []), *(gather or []))
    return res[0], res[1], list(res[2:])


def _attn_bwd(q, k, v, do, lse, delta, name, scatter=None):
    dil, n, _ = q.shape
    seg, nseg = _segments(n)
    nh = 2 if nseg > 1 else 0
    ns = 0 if scatter is None else len(scatter[0])

    def body(*refs):
        q_ref, k_ref, v_ref, do_ref, lse_ref, dl_ref = refs[:6]
        halos = refs[6:6 + nh]
        dq_ref, dk_ref, dv_ref = refs[6 + nh + ns:9 + nh + ns]
        halo_out = refs[9 + nh + ns:9 + 2 * nh + ns]
        ck_ref, cv_ref = refs[9 + 2 * nh + 2 * ns:11 + 2 * nh + 2 * ns]
        if ns:
            start, finish = _scatter_steps(refs[6 + nh:6 + nh + ns], refs[9 + 2 * nh + ns:9 + 2 * nh + 2 * ns],
                                           *refs[11 + 2 * nh + 2 * ns:], scatter[1])
            first, last = _grid_edges((dil, nseg))
            pl.when(first)(start)
        own, both, head = _segment_masks(pl.program_id(1))
        lo, _ = _head_masks()
        lane = lax.broadcasted_iota(jnp.int32, (1, LANES), 1)

        def per_head(t):
            return jnp.concatenate(
                [jnp.sum(jnp.where(lane == first, t, 0.0), axis=1, keepdims=True) for first in (0, HEAD_DIM)], axis=0)

        def grads(rows, kk, vv, valid, sl):
            q2 = _stack_heads(q_ref[rows, sl])
            do2 = _stack_heads(do_ref[rows, sl])
            p = jnp.where(valid, jnp.exp(_dot(q2, kk, NT) - per_head(lse_ref[rows, sl])), 0.0)
            ds = (p * (_dot(do2, vv, NT) - per_head(dl_ref[rows, sl]))).astype(BF16)
            dq = _dot(ds, kk, NN)
            dq_ref[rows, sl] = jnp.where(lo, dq[:BAND], dq[BAND:]).astype(BF16)
            return _dot(ds, q2, TN), _dot(p.astype(BF16), do2, TN)

        for sl in _lane_blocks(B_WIDTH):
            if nh:
                dkk, dvv = grads(_block_rows(0), jnp.concatenate([halos[0][:, sl], k_ref[0:BAND, sl]], axis=0),
                                 jnp.concatenate([halos[1][:, sl], v_ref[0:BAND, sl]], axis=0), head, sl)
                halo_out[0][:, sl], halo_out[1][:, sl] = dkk[:BAND], dvv[:BAND]
                ck_ref[:, sl], cv_ref[:, sl] = dkk[BAND:], dvv[BAND:]
            else:
                ck_ref[:, sl], cv_ref[:, sl] = grads(_block_rows(0), k_ref[0:BAND, sl], v_ref[0:BAND, sl], own, sl)

        @pl.loop(1, seg)
        def _(g):
            before = _block_rows(g - 1)
            for sl in _lane_blocks(B_WIDTH):
                dkk, dvv = grads(_block_rows(g), k_ref[_key_rows(g), sl], v_ref[_key_rows(g), sl], both, sl)
                dk_ref[before, sl] = (ck_ref[:, sl] + dkk[:BAND]).astype(BF16)
                dv_ref[before, sl] = (cv_ref[:, sl] + dvv[:BAND]).astype(BF16)
                ck_ref[:, sl] = dkk[BAND:]
                cv_ref[:, sl] = dvv[BAND:]

        final = pl.ds((seg - 1) * BAND, BAND)
        dk_ref[final, :] = ck_ref[...].astype(BF16)
        dv_ref[final, :] = cv_ref[...].astype(BF16)

        if ns:
            pl.when(last)(finish)

    main, halo = _segment_specs(seg)
    shape = jax.ShapeDtypeStruct((dil, n, B_WIDTH), BF16)
    halo_shape = jax.ShapeDtypeStruct((dil, nseg, BAND, B_WIDTH), F32)
    halo_spec = pl.BlockSpec((None, None, BAND, B_WIDTH), lambda r, j: (r, j, 0, 0))
    res = pl.pallas_call(
        body, grid=(dil, nseg), in_specs=[main] * 6 + [halo] * nh + [ANY] * ns,
        out_specs=[main] * 3 + [halo_spec] * nh + [ANY] * ns,
        out_shape=[shape] * 3 + [halo_shape] * nh + (_scattered_shapes(scatter[1]) if ns else []),
        scratch_shapes=[pltpu.VMEM((BAND, B_WIDTH), F32)] * 2 + (_scatter_sems(ns) if ns else []),
        compiler_params=_cparams(*(["arbitrary"] * 2 if ns else ["parallel"] * 2)), name=name)(
            q, k, v, do, lse, delta, *([k, v] if nh else []), *(scatter[0] if ns else []))
    return res[0], res[1], res[2], (tuple(res[3:3 + nh]) if nh else None), list(res[3 + nh:])


def _attn_combine(outs, lses, gb, mixed, name, gather=None):
    s = mixed.shape[0]
    npat = len(DILATIONS)
    w = B_WIDTH
    ng = 0 if gather is None else len(gather)

    def body(*refs):
        o_refs, l_refs = refs[:npat], refs[npat:2 * npat]
        g_ref = refs[2 * npat]
        ob_ref = refs[2 * npat + 2 + ng]
        lse_refs = refs[2 * npat + 3 + ng:3 * npat + 3 + ng]
        mb_ref = refs[3 * npat + 3 + ng]
        stage = refs[3 * npat + 4 + 2 * ng]
        if ng:
            start, relay, finish = _gather_steps(refs[2 * npat + 2:2 * npat + 2 + ng],
                                                 refs[3 * npat + 4 + ng:3 * npat + 4 + 2 * ng],
                                                 *refs[3 * npat + 5 + 2 * ng:])
            first, last = _grid_edges((s // TR,))
            pl.when(first)(start)
        os_ = [o_refs[0][...].astype(F32)] + [_load_classes(r, stage, d) for r, d in zip(o_refs[1:], CLASS_DILS)]
        ls = [l_refs[0][...]] + [_load_classes(r, stage, d) for r, d in zip(l_refs[1:], CLASS_DILS)]
        mx = functools.reduce(jnp.maximum, ls)
        ws = [jnp.exp(l - mx) for l in ls]
        tot = functools.reduce(lambda a, b: a + b, ws)
        ob = functools.reduce(lambda a, b: a + b, [wt / tot * o for wt, o in zip(ws, os_)])
        ob_ref[...] = ob
        lse = mx + jnp.log(tot)
        _stage_put(stage, lse)
        lse_refs[0][...] = lse
        for ref, d in zip(lse_refs[1:], CLASS_DILS):
            _store_classes(stage, ref, d)
        mb_ref[...] = (ob * _rsq_mean(ob) * g_ref[...]).astype(BF16)

        if ng:
            @pl.when(last)
            def _():
                relay()
                finish()

    lay_specs = [_row_spec(w)] + [_class_spec(d) for d in CLASS_DILS]
    res = pl.pallas_call(
        body, grid=(s // TR,), in_specs=lay_specs * 2 + [_vec_spec(w), ANY] + [ANY] * ng,
        out_specs=[_row_spec(w)] + lay_specs + [_row_spec(w, 1)] + [ANY] * ng,
        out_shape=[jax.ShapeDtypeStruct((s, w), F32), jax.ShapeDtypeStruct((s, w), F32)]
        + [_class_shape(s, d, F32) for d in CLASS_DILS] + [jax.ShapeDtypeStruct(mixed.shape, mixed.dtype)]
        + _gathered_shapes(gather or []),
        scratch_shapes=[STAGE] + (_gather_sems(ng) if ng else []), input_output_aliases={2 * npat + 1: npat + 1},
        compiler_params=_cparams("arbitrary" if ng else "parallel"), name=name)(*outs, *lses, gb, mixed,
                                                                              *(gather or []))
    return res[0], dict(zip(DILATIONS, res[1:npat + 1])), res[npat + 1], list(res[npat + 2:])


def _attn_bwd_prep(dmixed, ob, gb, name):
    s = ob.shape[0]
    w = B_WIDTH
    nlay = len(DILATIONS)

    def body(dm_ref, ob_ref, g_ref, *rest):
        do_refs, dl_refs = rest[:nlay], rest[nlay:2 * nlay]
        dg_ref, stage = rest[2 * nlay:]
        _acc_init([dg_ref])
        ob = ob_ref[...]
        dob, dgt = _rms_bwd(ob, _rsq_mean(ob), g_ref[...], dm_ref[...])
        dg_ref[...] += _colsum(dgt)
        _stage_put(stage, dob)
        do_refs[0][...] = dob.astype(BF16)
        for ref, d in zip(do_refs[1:], CLASS_DILS):
            _store_classes(stage, ref, d)
        lo, hi = _head_masks()
        t = dob * ob
        for b, sl in enumerate(_lane_blocks(w)):
            tb = t[:, sl]
            s0 = jnp.sum(jnp.where(lo, tb, 0.0), axis=1, keepdims=True)
            s1 = jnp.sum(jnp.where(hi, tb, 0.0), axis=1, keepdims=True)
            stage[b] = jnp.where(lo, s0, s1)
        dl_refs[0][...] = _stage_get(stage)
        for ref, d in zip(dl_refs[1:], CLASS_DILS):
            _store_classes(stage, ref, d)

    lay_specs = [_row_spec(w)] + [_class_spec(d) for d in CLASS_DILS]
    shapes = lambda dt: [jax.ShapeDtypeStruct((s, w), dt)] + [_class_shape(s, d, dt) for d in CLASS_DILS]
    res = pl.pallas_call(
        body, grid=(s // TR,), in_specs=[_row_spec(w, 1), _row_spec(w), _vec_spec(w)],
        out_specs=lay_specs * 2 + [_vec_spec(w)],
        out_shape=shapes(BF16) + shapes(F32) + [jax.ShapeDtypeStruct((1, w), F32)],
        scratch_shapes=[STAGE],
        compiler_params=_cparams("arbitrary"), name=name)(dmixed, ob, gb)
    return dict(zip(DILATIONS, res[:nlay])), dict(zip(DILATIONS, res[nlay:2 * nlay])), res[2 * nlay]


def _rope_bwd(dqs, dks, dvs, halos, tabs, dproj, name):
    s = dproj.shape[0]
    half = ROT_DIM // 2
    scale = HEAD_DIM ** -0.5
    npat = len(DILATIONS)
    w = B_WIDTH
    nseg = halos[0].shape[0]
    per = s // nseg // TR

    def body(*refs):
        groups = [refs[g * npat:(g + 1) * npat] for g in range(3)]
        halo_refs = (None,) + tuple(refs[3 * npat:3 * npat + 2])
        c_ref, s1_ref, s2_ref, _, o_ref, stage = refs[3 * npat + 2:]
        i = pl.program_id(0)
        at_edge = ((i + 1) % per == 0) & ((i + 1) // per < nseg)

        def total(rs, halo_ref=None):
            acc = rs[0][...].astype(F32)
            if halo_ref is not None:
                edge = jnp.concatenate([jnp.zeros((TR - BAND, w), F32), halo_ref[...]], axis=0)
                acc = acc + jnp.where(at_edge, edge, 0.0)
            for ref, d in zip(rs[1:], CLASS_DILS):
                acc = acc + _load_classes(ref, stage, d)
            return acc

        def unrope(g):
            c, s1, s2 = c_ref[...], s1_ref[...], s2_ref[...]
            for sl in _lane_blocks(w):
                gb = g[:, sl]
                o = gb * c + pltpu.roll(gb * s1, half, 1) + pltpu.roll(gb * s2, LANES - half, 1)
                o_ref[:, sl] = o.astype(BF16)

        which = pl.program_id(1)

        @pl.when(which == 0)
        def _():
            unrope(total(groups[0]) * scale)

        @pl.when(which == 1)
        def _():
            unrope(total(groups[1], halo_refs[1]))

        @pl.when(which == 2)
        def _():
            o_ref[...] = total(groups[2], halo_refs[2]).astype(BF16)

    tab = pl.BlockSpec((TR, LANES), lambda i, j: (i, 0))
    nat = pl.BlockSpec((TR, w), lambda i, j: (i, 0))
    lay_specs = [nat] + [_class_spec(d) for d in CLASS_DILS]
    edge_spec = pl.BlockSpec((None, BAND, w), lambda i, j: (jnp.minimum((i + 1) // per, nseg - 1), 0, 0))
    first_col = 2 * A_WIDTH // w
    return pl.pallas_call(
        body, grid=(s // TR, 3), in_specs=lay_specs * 3 + [edge_spec] * 2 + [tab] * 3 + [ANY],
        out_specs=pl.BlockSpec((TR, w), lambda i, j: (i, first_col + j)),
        out_shape=jax.ShapeDtypeStruct(dproj.shape, dproj.dtype), scratch_shapes=[STAGE],
        input_output_aliases={3 * npat + 5: 0},
        compiler_params=_cparams("parallel", "arbitrary"), name=name)(*dqs, *dks, *dvs, *halos, *tabs, dproj)


TK = 512
HALO = 16
FFN_ROWS = 256
FFN_CHUNKS = tuple(slice(r, r + FFN_ROWS) for r in range(0, TM, FFN_ROWS))


def _row_of(v, r):
    rows = lax.broadcasted_iota(jnp.int32, (v.shape[0], 1), 0)
    return jnp.sum(jnp.where(rows == r, v, 0.0), axis=0, keepdims=True)


def _taps_before(x, halo):
    row = lax.broadcasted_iota(jnp.int32, (x.shape[0], 1), 0)
    m1 = jnp.where(row == 0, _row_of(halo, HALO - 1), pltpu.roll(x, 1, 0))
    m2 = jnp.where(row == 0, _row_of(halo, HALO - 2), jnp.where(row == 1, _row_of(halo, HALO - 1), pltpu.roll(x, 2, 0)))
    return m2, m1, x


def _taps_after(x, halo):
    rows = x.shape[0]
    row = lax.broadcasted_iota(jnp.int32, (rows, 1), 0)
    p1 = jnp.where(row == rows - 1, _row_of(halo, 0), pltpu.roll(x, rows - 1, 0))
    p2 = jnp.where(row == rows - 2, _row_of(halo, 0), jnp.where(row == rows - 1, _row_of(halo, 1), pltpu.roll(x, rows - 2, 0)))
    return p1, p2


def _conv_value(taps, cw_ref, cb_ref, h):
    return cb_ref[h] + cw_ref[h, 0:1, :] * taps[0] + cw_ref[h, 1:2, :] * taps[1] + cw_ref[h, 2:3, :] * taps[2]


def _ffn_weight_specs(ncol):
    per_up = (2 * D_FF // N_CHIPS) // TK
    per_dn = (D_FF // N_CHIPS) // TK
    wg = pl.BlockSpec((None, None, D_MODEL, TK), lambda i, j: (j // per_up, 0, 0, j % per_up))
    wv = pl.BlockSpec((None, None, D_MODEL, TK), lambda i, j: ((j + ncol) // per_up, 0, 0, (j + ncol) % per_up))
    wd = pl.BlockSpec((None, None, TK, D_MODEL), lambda i, j: (j // per_dn, 0, j % per_dn, 0))
    cw = pl.BlockSpec((2, 3, TK), lambda i, j: (0, 0, j))
    cb = pl.BlockSpec((2, 1, TK), lambda i, j: (0, 0, j))
    return wg, wv, wd, cw, cb


def _ffn_forward(h2, w_up, w_down, cw3, cb3, name, gather=None, post=None):
    s = h2.shape[0]
    nm, ncol = s // TM, D_FF // TK
    ng = 0 if gather is None else len(gather)
    npost = 0 if post is None else 3
    nout = 4 + (2 if post else 0)

    def body(*refs):
        h_ref, wg_ref, wv_ref, wd_ref, cw_ref, cb_ref = refs[:6]
        post_in = refs[6:6 + npost]
        g_in = refs[6 + npost:6 + npost + ng]
        outs = refs[6 + npost + ng:6 + npost + ng + nout]
        y_ref, up_ref, cv_ref, f_ref = outs[:4]
        g_out = refs[6 + npost + ng + nout:6 + npost + 2 * ng + nout]
        carry = refs[6 + npost + 2 * ng + nout]
        i, j = pl.program_id(0), pl.program_id(1)
        if ng:
            start, relay, finish = _gather_steps(g_in, g_out, *refs[7 + npost + 2 * ng + nout:])
            pl.when((i == 0) & (j == 0))(start)
            pl.when((i == nm - 1) & (j == 0))(relay)

        @pl.when((i == 0) & (j == 0))
        def _():
            carry[...] = jnp.zeros_like(carry)

        @pl.when(j == 0)
        def _():
            f_ref[...] = jnp.zeros_like(f_ref)

        ups = []
        for rs in FFN_CHUNKS:
            hc = h_ref[rs, :]
            ups.append([_dot(hc, w_ref[...], NN).astype(BF16) for w_ref in (wg_ref, wv_ref)])
            for hh in range(2):
                up_ref[hh, rs, :] = ups[-1][hh]
        before = [carry[j, hh] for hh in range(2)]
        for rs, up in zip(FFN_CHUNKS, ups):
            conv = []
            for hh in range(2):
                x = up[hh].astype(F32)
                conv.append(_conv_value(_taps_before(x, before[hh]), cw_ref, cb_ref, hh))
                cv_ref[hh, rs, :] = conv[hh].astype(BF16)
                before[hh] = x[x.shape[0] - HALO:, :]
            y = (_gelu_tanh(conv[0])[0] * conv[1]).astype(BF16)
            y_ref[rs, :] = y
            f_ref[rs, :] += _dot(y, wd_ref[...], NN)
        for hh in range(2):
            carry[j, hh] = before[hh]

        @pl.when(j == ncol - 1)
        def _():
            if post:
                f = f_ref[...]
                x1_ref, gp_ref, gn_ref = post_in
                x2 = x1_ref[...] + f * _rsq_mean(f) * gp_ref[...]
                outs[4][...] = x2
                outs[5][...] = (x2 * _rsq_mean(x2) * gn_ref[...]).astype(BF16)

        if ng:
            pl.when((i == nm - 1) & (j == ncol - 1))(finish)

    wg, wv, wd, cw, cb = _ffn_weight_specs(ncol)
    row = pl.BlockSpec((TM, D_MODEL), lambda i, j: (i, 0))
    vec = pl.BlockSpec((1, D_MODEL), lambda i, j: (0, 0))
    res = pl.pallas_call(
        body, grid=(nm, ncol),
        in_specs=[row, wg, wv, wd, cw, cb] + ([row, vec, vec] if post else []) + [ANY] * ng,
        out_specs=[pl.BlockSpec((TM, TK), lambda i, j: (i, j)), pl.BlockSpec((2, TM, TK), lambda i, j: (0, i, j)),
                   pl.BlockSpec((2, TM, TK), lambda i, j: (0, i, j)), row] + ([row, row] if post else [])
        + [ANY] * ng,
        out_shape=[jax.ShapeDtypeStruct((s, D_FF), BF16), jax.ShapeDtypeStruct((2, s, D_FF), BF16),
                   jax.ShapeDtypeStruct((2, s, D_FF), BF16), jax.ShapeDtypeStruct((s, D_MODEL), F32)]
        + ([jax.ShapeDtypeStruct((s, D_MODEL), F32), jax.ShapeDtypeStruct((s, D_MODEL), BF16)] if post else [])
        + _gathered_shapes(gather or []),
        scratch_shapes=[pltpu.VMEM((ncol, 2, HALO, TK), F32)] + (_gather_sems(ng) if ng else []),
        compiler_params=_cparams("arbitrary", "arbitrary"), name=name)(h2, w_up, w_up, w_down, cw3, cb3,
                                                                      *(post or []), *(gather or []))
    return res[:nout], list(res[nout:])


def _ffn_backward(df, w_up, w_down, up3, cv3, cw3, name, scatter=None):
    s = df.shape[0]
    nm, ncol = s // TM, D_FF // TK
    ns = 0 if scatter is None else len(scatter[0])

    def body(*refs):
        df_ref, wg_ref, wv_ref, wd_ref, cw_ref, up_ref, cv_ref = refs[:7]
        s_in = refs[7:7 + ns]
        dup_ref, dh_ref, sums_ref = refs[7 + ns:10 + ns]
        s_out = refs[10 + ns:10 + 2 * ns]
        carry = refs[10 + 2 * ns]
        i, j = pl.program_id(0), pl.program_id(1)
        if ns:
            start, finish = _scatter_steps(s_in, s_out, *refs[11 + 2 * ns:], scatter[1])
            pl.when((i == 0) & (j == 0))(start)

        @pl.when((i == 0) & (j == 0))
        def _():
            carry[...] = jnp.zeros_like(carry)
            sums_ref[...] = jnp.zeros_like(sums_ref)

        @pl.when(j == 0)
        def _():
            dh_ref[...] = jnp.zeros_like(dh_ref)

        chunks = FFN_CHUNKS[::-1]
        dys = [_dot(df_ref[rs, :], wd_ref[...], NT) for rs in chunks]
        row = lax.broadcasted_iota(jnp.int32, (8, 1), 0)
        after = [carry[j, hh] for hh in range(2)]
        upd = [jnp.zeros((8, TK), F32) for _ in range(2)]
        for rs, dy in zip(chunks, dys):
            act, grad = _gelu_tanh(cv_ref[0, rs, :].astype(F32))
            dcs = (dy * cv_ref[1, rs, :].astype(F32) * grad, dy * act)
            part = dh_ref[rs, :]
            for hh, w_ref in ((0, wg_ref), (1, wv_ref)):
                dc = dcs[hh]
                x = up_ref[hh, rs, :].astype(F32)
                after1, after2 = _taps_after(dc, after[hh])
                for ridx, sm in enumerate((_colsum(after2 * x), _colsum(after1 * x), _colsum(dc * x), _colsum(dc))):
                    upd[hh] = upd[hh] + jnp.where(row == ridx, sm, 0.0)
                dup = (cw_ref[hh, 2:3, :] * dc + cw_ref[hh, 1:2, :] * after1 + cw_ref[hh, 0:1, :] * after2).astype(BF16)
                after[hh] = dc[:HALO, :]
                dup_ref[hh, rs, :] = dup
                part = part + _dot(dup, w_ref[...], NT)
            dh_ref[rs, :] = part
        for hh in range(2):
            sums_ref[j, hh] += upd[hh]
            carry[j, hh] = after[hh]

        if ns:
            pl.when((i == nm - 1) & (j == ncol - 1))(finish)

    wg, wv, wd, cw, _ = _ffn_weight_specs(ncol)
    rev = lambda i: nm - 1 - i
    res = pl.pallas_call(
        body, grid=(nm, ncol),
        in_specs=[pl.BlockSpec((TM, D_MODEL), lambda i, j: (rev(i), 0)), wg, wv, wd, cw,
                  pl.BlockSpec((2, TM, TK), lambda i, j: (0, rev(i), j)),
                  pl.BlockSpec((2, TM, TK), lambda i, j: (0, rev(i), j))] + [ANY] * ns,
        out_specs=[pl.BlockSpec((2, TM, TK), lambda i, j: (0, rev(i), j)),
                   pl.BlockSpec((TM, D_MODEL), lambda i, j: (rev(i), 0)),
                   pl.BlockSpec((ncol, 2, 8, TK), lambda i, j: (0, 0, 0, 0))] + [ANY] * ns,
        out_shape=[jax.ShapeDtypeStruct((2, s, D_FF), BF16), jax.ShapeDtypeStruct((s, D_MODEL), F32),
                   jax.ShapeDtypeStruct((ncol, 2, 8, TK), F32)] + (_scattered_shapes(scatter[1]) if ns else []),
        scratch_shapes=[pltpu.VMEM((ncol, 2, HALO, TK), F32)] + (_scatter_sems(ns) if ns else []),
        compiler_params=_cparams("arbitrary", "arbitrary"), name=name)(df, w_up, w_up, w_down, cw3, up3, cv3,
                                                                      *(scatter[0] if ns else []))
    return res[:3], list(res[3:])


def _wspec(rows, cols, index_map):
    return pl.BlockSpec((None, None, rows, cols), index_map)


def _layer_forward(l, x0, h1, p, wg, tabs, gather=None, late=None, g_next=None):
    s = x0.shape[0]
    nm = s // TMM
    tag = f"_l{l}"
    riders = dict.fromkeys(DILATIONS)
    proj_rider = rope_rider = combine_rider = None
    if late is not None:
        cols = lambda t, parts: [t[:, i * t.shape[1] // parts:(i + 1) * t.shape[1] // parts] for i in range(parts)]
        (down_a, down_b), up_q = cols(late["w_down"], 2), cols(late["w_up"], 4)
        proj_rider, rope_rider, combine_rider = [late["w_out"], down_a], [up_q[2]], [up_q[3]]
        riders = dict(zip(DILATIONS, ([down_b], [up_q[0]], [up_q[1]])))
    proj = _matmul(
        h1, wg["w_in"], grid=(nm, N_CHIPS), a_spec=pl.BlockSpec((TMM, D_MODEL), lambda i, j: (i, 0)),
        b_spec=_wspec(D_MODEL, IN_COLS // N_CHIPS, lambda i, j: (j, 0, 0, 0)),
        o_spec=pl.BlockSpec((TMM, IN_COLS // N_CHIPS), lambda i, j: (i, j)), o_shape=(s, IN_COLS), o_dtype=BF16,
        dims=NN, nk=1, kaxis=None, acc_shape=None, name="proj" + tag, gather=proj_rider)
    if late is not None:
        proj, (w_out_all4, down_a) = proj
    ma = _mixer_a_fwd(proj, p["v_norm_g"], p["v_norm_b"], p["w_spatial"], p["bs_full"], p["out_norm_a"],
                      "mixer_a_fwd" + tag)
    q, k, v, rope_landed = _rope_fwd(proj, tabs, "rope_fwd" + tag, rope_rider)
    outs, lses, landed = zip(*[
        _attn_fwd(_as_classes(q[d]), _as_classes(k[d]), _as_classes(v[d]), f"attn_fwd_d{d}" + tag, riders[d])
        for d in DILATIONS])
    outs = [o.reshape(s, B_WIDTH) if d == 1 else o for o, d in zip(outs, DILATIONS)]
    lses = [t.reshape(s, B_WIDTH) if d == 1 else t for t, d in zip(lses, DILATIONS)]
    ob, lse, mixed, combine_landed = _attn_combine(outs, lses, p["out_norm_b"], ma, "attn_combine" + tag,
                                                   combine_rider)
    if late is not None:
        wg = dict(wg, w_out=w_out_all4, w_down=jnp.concatenate([down_a, landed[0][0]], axis=-1),
                  w_up=jnp.concatenate([landed[1][0], landed[2][0], rope_landed[0], combine_landed[0]], axis=-1))
    y1, x1, h2 = _mix_out_norm(mixed, wg["w_out"], x0, p["post_mix_norm"], p["pre_ffn_norm"], "mix_out" + tag)
    post = None if g_next is None else (x1, p["post_ffn_norm"], g_next)
    (y, up3, cv3, f, *after), gathered = _ffn_forward(h2, wg["w_up"], wg["w_down"], p["cw3"], p["cb3"],
                                                      "ffn_fwd" + tag, gather, post)
    saved = dict(x0=x0, h1=h1, proj=proj, q=q, k=k, v=v, ob=ob, lse=lse, mixed=mixed, y1=y1, x1=x1, h2=h2,
                 up3=up3, cv3=cv3, y=y, f=f)
    if after:
        saved.update(x2=after[0], h_next=after[1])
    return saved, gathered, wg


def _layer_backward(l, dx2, df, sv, p, wg, tabs, pos, scatter=None, hide=False):
    s = dx2.shape[0]
    nm = s // TMM
    tag = f"_l{l}"
    g = {}
    (dup3, dh2, conv_sums), scattered = _ffn_backward(df, wg["w_up"], wg["w_down"], sv["up3"], sv["cv3"], p["cw3"],
                                                      "ffn_bwd" + tag, scatter)
    sums = conv_sums.transpose(1, 2, 0, 3).reshape(2, 8, D_FF)
    g["conv_w"] = jnp.concatenate([sums[0, :3], sums[1, :3]], axis=1)
    g["conv_b"] = jnp.concatenate([sums[0, 3:4], sums[1, 3:4]], axis=1)
    tn = 1024
    done = {}
    gw_down = _matmul(
        sv["y"], df, grid=(D_FF // tn,), a_spec=pl.BlockSpec((s, tn), lambda k: (0, k)),
        b_spec=pl.BlockSpec((s, D_MODEL), lambda k: (0, 0)),
        o_spec=pl.BlockSpec((2, tn, D_MODEL // 2), lambda k: (0, k, 0)),
        o_shape=(2, D_FF, D_MODEL // 2), o_dtype=BF16,
        dims=TN, nk=1, kaxis=None, acc_shape=None, name="w_down_grad" + tag, halves=True)
    down_sums = _chip_sums(l, dict(w_down=gw_down), pos, ("w_down",)) if hide else None
    gw_up = _matmul(
        sv["h2"], dup3, grid=(2 * D_FF // tn,), a_spec=pl.BlockSpec((s, D_MODEL), lambda n: (0, 0)),
        b_spec=pl.BlockSpec((None, s, tn), lambda n: (n // (D_FF // tn), 0, n % (D_FF // tn))),
        o_spec=pl.BlockSpec((None, D_MODEL, tn), lambda n: (n // 2, 0, n % 2)),
        o_shape=(N_CHIPS, D_MODEL, 2 * D_FF // N_CHIPS), o_dtype=BF16,
        dims=TN, nk=1, kaxis=None, acc_shape=None, name="w_up_grad" + tag,
        scatter=(down_sums, ("w_down",)) if hide else None)
    up_sums = None
    if hide:
        gw_up, received = gw_up
        done[("w_down",)] = (down_sums, received)
        up_sums = _chip_sums(l, dict(w_up=gw_up), pos, ("w_up",))
    dx1, dy1, g["pre_ffn_norm"], g["post_mix_norm"] = _norm_bwd_mid(
        dx2, dh2, sv["x1"], p["pre_ffn_norm"], sv["y1"], p["post_mix_norm"], "norm_bwd_mid" + tag)
    w_out_all = pl.BlockSpec((N_CHIPS, None, D_MODEL // N_CHIPS, D_MODEL), lambda i: (0, 0, 0, 0))
    dmixed = _matmul(
        dy1, wg["w_out"], grid=(nm,), a_spec=pl.BlockSpec((TMM, D_MODEL), lambda i: (i, 0)), b_spec=w_out_all,
        o_spec=pl.BlockSpec((TMM, D_MODEL), lambda i: (i, 0)), o_shape=(s, D_MODEL), o_dtype=F32,
        dims=NT, nk=1, kaxis=None, acc_shape=None, name="mix_out_bwd" + tag, b_2d=(D_MODEL, D_MODEL))
    gw_out = _matmul(
        sv["mixed"], dy1, grid=(1,), a_spec=pl.BlockSpec((s, D_MODEL), lambda m: (0, 0)),
        b_spec=pl.BlockSpec((s, D_MODEL), lambda m: (0, 0)),
        o_spec=pl.BlockSpec((2, D_MODEL, D_MODEL // 2), lambda m: (0, 0, 0)),
        o_shape=(2, D_MODEL, D_MODEL // 2), o_dtype=BF16,
        dims=TN, nk=1, kaxis=None, acc_shape=None, name="w_out_grad" + tag, halves=True)
    out_sums = _chip_sums(l, dict(w_out=gw_out), pos, ("w_out",)) if hide else None
    dpa, g["out_norm_a"], g["v_norm_g"], g["v_norm_b"], dbs, g["w_spatial"], received = _mixer_a_bwd(
        sv["proj"], dmixed, p["v_norm_g"], p["v_norm_b"], p["w_spatial"], p["bs_full"], p["out_norm_a"],
        "mixer_a_bwd" + tag, (out_sums, ("w_out",)) if hide else None)
    if hide:
        done[("w_out",)] = (out_sums, received)
    g["b_spatial"] = dbs[:, ::GROUP_DIM].T
    dob, delta, g["out_norm_b"] = _attn_bwd_prep(dmixed, sv["ob"], p["out_norm_b"], "attn_bwd_prep" + tag)
    halves = dict(zip(DILATIONS, ("w_up:0", "w_up:1"))) if hide else {}
    dqs, dks, dvs, edges, received = zip(*[
        _attn_bwd(*(_as_classes(t[d]) for t in (sv["q"], sv["k"], sv["v"], dob, sv["lse"], delta)),
                  f"attn_bwd_d{d}" + tag, (up_sums, (halves[d],)) if d in halves else None)
        for d in DILATIONS])
    if hide:
        done[("w_up",)] = (up_sums, [jnp.concatenate([received[0][0], received[1][0]], axis=-1)])
    nat = lambda ts: [t.reshape(s, B_WIDTH) if d == 1 else t for t, d in zip(ts, DILATIONS)]
    halos = [t[0] for t in edges[0]]
    dproj = _rope_bwd(nat(dqs), nat(dks), nat(dvs), halos, tabs, dpa, "rope_bwd" + tag)
    wcol = IN_COLS // N_CHIPS
    gw_in = _matmul(
        sv["h1"], dproj, grid=(N_CHIPS,), a_spec=pl.BlockSpec((s, D_MODEL), lambda n: (0, 0)),
        b_spec=pl.BlockSpec((s, wcol), lambda n: (0, n)),
        o_spec=pl.BlockSpec((None, D_MODEL, wcol), lambda n: (n, 0, 0)),
        o_shape=(N_CHIPS, D_MODEL, wcol), o_dtype=BF16,
        dims=TN, nk=1, kaxis=None, acc_shape=None, name="w_in_grad" + tag)
    in_sums = _chip_sums(l, dict(w_in=gw_in), pos, ("w_in",)) if hide else None
    dh1, received = _proj_bwd(dproj, wg["w_in"], "proj_bwd" + tag, (in_sums, ("w_in",)) if hide else None)
    if hide:
        done[("w_in",)] = (in_sums, received)
    big = {} if hide else dict(w_in=gw_in, w_up=gw_up, w_out=gw_out, w_down=gw_down)
    return dx1, dh1, big, g, scattered, done


SMALL = ("pre_mix_norm", "v_norm_g", "v_norm_b", "w_spatial", "b_spatial", "out_norm_a", "out_norm_b",
         "post_mix_norm", "pre_ffn_norm", "conv_b", "post_ffn_norm")
BIG = ("w_in", "w_out", "w_up", "w_down")
DEPTH = 2


def _layer_params(l, small, conv_w_full):
    p = {n: small[n][l].reshape(1, -1) for n in SMALL if n not in ("w_spatial", "b_spatial")}
    p["w_spatial"] = small["w_spatial"][l]
    p["bs_full"] = jnp.repeat(small["b_spatial"][l].T, GROUP_DIM, axis=1)
    p["cw3"] = conv_w_full[l].reshape(3, 2, D_FF).transpose(1, 0, 2)
    p["cb3"] = small["conv_b"][l].reshape(2, 1, D_FF)
    return p


def _mesh_pos():
    return lax.axis_index("x"), lax.axis_index("y"), lax.axis_index("c")


def _other_chips(x, y):
    return [(1 - x, y), (x, 1 - y), (1 - x, 1 - y)]


def _gathered_shapes(blocks):
    return [jax.ShapeDtypeStruct((N_CHIPS, 1) + a.shape, a.dtype) for a in blocks]


def _gather_sems(nw):
    n = 2 * nw * (N_CHIPS - 1) + nw
    return [pltpu.SemaphoreType.DMA((n,)), pltpu.SemaphoreType.DMA((n,))]


def _gather_steps(ins, outs, send, recv):
    nw, nrel = len(ins), N_CHIPS - 1
    x, y, c = _mesh_pos()
    mine, sibling, chips = 2 * x + y, (x, y, 1 - c), _other_chips(x, y)

    def copy(src, dst, slot, to):
        return pltpu.make_async_remote_copy(src_ref=src, dst_ref=dst, send_sem=send.at[slot],
                                            recv_sem=recv.at[slot], device_id=to, device_id_type=MESH)

    def half_rows(t, core):
        rows = ins[t].shape[0] // 2
        return pl.ds(pl.multiple_of(core * rows, rows), rows)

    def landing(t, chip, core):
        return outs[t].at[chip, 0, half_rows(t, core), :]

    slots = [(t, r, chip) for t in range(nw) for r, chip in enumerate(chips)]
    own = [copy(ins[t], outs[t].at[mine, 0], 2 * nw * nrel + t, sibling) for t in range(nw)]
    first = [copy(ins[t].at[half_rows(t, c), :], landing(t, mine, c), t * nrel + r, (px, py, c))
             for t, r, (px, py) in slots]
    relays = [copy(landing(t, 2 * px + py, c), landing(t, 2 * px + py, c), nw * nrel + t * nrel + r, sibling)
              for t, r, (px, py) in slots]

    def start():
        for cp in own + first:
            cp.start()

    def relay():
        for (t, r, (px, py)), cp in zip(slots, relays):
            copy(landing(t, 2 * px + py, c), landing(t, 2 * px + py, c), t * nrel + r, (px, py, c)).wait_recv()
            cp.start()

    def finish():
        for t, r, (px, py) in slots:
            passed = landing(t, 2 * px + py, 1 - c)
            copy(passed, passed, nw * nrel + t * nrel + r, sibling).wait_recv()
        for cp in first + relays:
            cp.wait_send()
        for cp in own:
            cp.wait()

    return start, relay, finish


def _gather_weights(blocks, name):
    nw = len(blocks)

    def body(*refs):
        start, relay, finish = _gather_steps(refs[:nw], refs[nw:2 * nw], *refs[2 * nw:])
        start()
        relay()
        finish()

    return pl.pallas_call(
        body, in_specs=[ANY] * nw, out_specs=[ANY] * nw, out_shape=_gathered_shapes(blocks),
        scratch_shapes=_gather_sems(nw), name=name)(*blocks)


HALF = 512

GRAD_GEOM = {"w_in": ("rows", D_MODEL, IN_COLS // N_CHIPS), "w_up": ("rows", D_MODEL, 2 * D_FF // N_CHIPS),
             "w_out": ("cols", D_MODEL, D_MODEL // N_CHIPS), "w_down": ("cols", D_FF, D_FF // N_CHIPS)}


def _exchange_shape(n):
    kind, a, b = GRAD_GEOM[n]
    return (N_CHIPS, HALF, b) if kind == "rows" else (a, HALF)


def _piece_shape(n):
    name, _, part = n.partition(":")
    kind, _, b = GRAD_GEOM[name]
    if part:
        assert kind == "rows"
        return (HALF, b // 2)
    return (HALF, b) if kind == "rows" else (b, HALF)


def _half_of(ref, n, core):
    if GRAD_GEOM[n][0] == "rows":
        return ref.at[:, pl.ds(pl.multiple_of(core * HALF, HALF), HALF), :]
    return ref.at[core]


def _piece_of(ref, n, chip):
    name, _, part = n.partition(":")
    kind, _, b = GRAD_GEOM[name]
    if part:
        return ref.at[chip, :, pl.ds(int(part) * (b // 2), b // 2)]
    return ref.at[chip] if kind == "rows" else ref.at[pl.ds(pl.multiple_of(chip * b, b), b), :]


def _pair_exchange(g, names, name):
    n = len(names)

    def body(*refs):
        send, recv = refs[2 * n:]
        x, y, c = _mesh_pos()
        o = 1 - c
        cps = [pltpu.make_async_remote_copy(src_ref=_half_of(refs[t], nm, o), dst_ref=refs[n + t], send_sem=send.at[t],
                                            recv_sem=recv.at[t], device_id=(x, y, o), device_id_type=MESH)
               for t, nm in enumerate(names)]
        for cp in cps:
            cp.start()
        for cp in cps:
            cp.wait()

    return pl.pallas_call(
        body, in_specs=[ANY] * n, out_specs=[ANY] * n,
        out_shape=[jax.ShapeDtypeStruct(_exchange_shape(nm), BF16) for nm in names],
        scratch_shapes=[pltpu.SemaphoreType.DMA((n,)), pltpu.SemaphoreType.DMA((n,))],
        name=name)(*[g[nm] for nm in names])


def _pair_sum(g, recv, pos, names, name_prefix):
    def add(a, b, grid, a_spec, b_spec, name):
        def body(pos_ref, a_ref, b_ref, o_ref):
            o_ref[...] = (a_ref[...].astype(F32) + b_ref[...].astype(F32)).astype(BF16)

        return pl.pallas_call(
            body, grid_spec=pltpu.PrefetchScalarGridSpec(
                num_scalar_prefetch=1, grid=grid, in_specs=[a_spec, b_spec], out_specs=b_spec),
            out_shape=jax.ShapeDtypeStruct(b.shape, BF16), compiler_params=_cparams("parallel"), name=name)(pos, a, b)

    out = []
    for nm, r in zip(names, recv):
        kind, rows, width = GRAD_GEOM[nm]
        if kind == "rows":
            out.append(add(g[nm], r, (N_CHIPS,), pl.BlockSpec((None, HALF, width), lambda j, pos: (j, pos[2], 0)),
                           pl.BlockSpec((None, HALF, width), lambda j, pos: (j, 0, 0)), f"{name_prefix}_{nm}"))
        else:
            out.append(add(g[nm], r, (rows // D_MODEL,), pl.BlockSpec((None, D_MODEL, HALF), lambda j, pos: (pos[2], j, 0)),
                           pl.BlockSpec((D_MODEL, HALF), lambda j, pos: (j, 0)), f"{name_prefix}_{nm}"))
    return out


def _scattered_shapes(names):
    return [jax.ShapeDtypeStruct((N_CHIPS - 1,) + _piece_shape(nm), BF16) for nm in names]


def _scatter_sems(n):
    return [pltpu.SemaphoreType.DMA((n * (N_CHIPS - 1),)), pltpu.SemaphoreType.DMA((n * (N_CHIPS - 1),))]


def _scatter_steps(sums, outs, send, recv, names):
    nrel = N_CHIPS - 1
    x, y, c = _mesh_pos()
    cps = []
    for r, (px, py) in enumerate(_other_chips(x, y)):
        for t, nm in enumerate(names):
            cps.append(pltpu.make_async_remote_copy(
                src_ref=_piece_of(sums[t], nm, 2 * px + py), dst_ref=outs[t].at[r], send_sem=send.at[t * nrel + r],
                recv_sem=recv.at[t * nrel + r], device_id=(px, py, c), device_id_type=MESH))

    def start():
        for cp in cps:
            cp.start()

    def finish():
        for cp in cps:
            cp.wait()

    return start, finish


def _chip_scatter(sums, names, name):
    n = len(names)

    def body(*refs):
        start, finish = _scatter_steps(refs[:n], refs[n:2 * n], *refs[2 * n:], names)
        start()
        finish()

    return pl.pallas_call(
        body, in_specs=[ANY] * n, out_specs=[ANY] * n, out_shape=_scattered_shapes(names),
        scratch_shapes=_scatter_sems(n), name=name)(*sums)


def _chip_sum(sums, recv, pos, names, name_prefix):
    def add(a, b, a_spec, shape, name):
        def body(pos_ref, a_ref, b_ref, o_ref):
            tot = a_ref[...].astype(F32)
            for r in range(N_CHIPS - 1):
                tot = tot + b_ref[r].astype(F32)
            o_ref[...] = tot

        return pl.pallas_call(
            body, grid_spec=pltpu.PrefetchScalarGridSpec(
                num_scalar_prefetch=1, grid=(1,), in_specs=[a_spec, pl.BlockSpec(b.shape, lambda i, pos: (0, 0, 0))],
                out_specs=pl.BlockSpec((None,) + shape, lambda i, pos: (pos[2], 0, 0))),
            out_shape=jax.ShapeDtypeStruct((2,) + shape, F32), compiler_params=_cparams("arbitrary"),
            name=name)(pos, a, b)

    chip = lambda pos: 2 * pos[0] + pos[1]
    out = []
    for nm, a, b in zip(names, sums, recv):
        shape = _piece_shape(nm)
        if GRAD_GEOM[nm][0] == "rows":
            spec = pl.BlockSpec((None,) + shape, lambda i, pos: (chip(pos), 0, 0))
        else:
            spec = pl.BlockSpec(shape, lambda i, pos: (chip(pos), 0))
        out.append(add(a, b, spec, shape, f"{name_prefix}_{nm}"))
    return out


def _pair_share(totals, name):
    n = len(totals)

    def body(*refs):
        ins, outs = refs[:n], refs[n:2 * n]
        send, recv = refs[2 * n:]
        x, y, c = _mesh_pos()
        o = 1 - c
        cps = [pltpu.make_async_remote_copy(src_ref=ins[t].at[c], dst_ref=outs[t].at[c], send_sem=send.at[t],
                                            recv_sem=recv.at[t], device_id=(x, y, o), device_id_type=MESH)
               for t in range(n)]
        for cp in cps:
            cp.start()
        for t in range(n):
            pltpu.make_async_remote_copy(src_ref=ins[t].at[o], dst_ref=outs[t].at[o], send_sem=send.at[t],
                                         recv_sem=recv.at[t], device_id=(x, y, o), device_id_type=MESH).wait_recv()
        for cp in cps:
            cp.wait_send()

    return pl.pallas_call(
        body, in_specs=[ANY] * n, out_specs=[ANY] * n,
        out_shape=[jax.ShapeDtypeStruct(t.shape, t.dtype) for t in totals],
        scratch_shapes=[pltpu.SemaphoreType.DMA((n,)), pltpu.SemaphoreType.DMA((n,))],
        input_output_aliases={t: t for t in range(n)}, name=name)(*totals)


def _chip_sums(l, g, pos, names):
    tag = f"l{l}_" + "_".join(names)
    recv = _pair_exchange(g, names, "pair_exchange_" + tag)
    return _pair_sum(g, recv, pos, names, "pair_sum_" + tag)


def _gradient_shards(l, sums, scattered, pos, names):
    tag = f"l{l}_" + "_".join(names)
    halves = _pair_share(_chip_sum(sums, scattered, pos, names, "chip_sum_" + tag), "pair_share_" + tag)
    out = {}
    for nm, t in zip(names, halves):
        rows, cols = _piece_shape(nm)
        out[nm] = t.reshape(2 * rows, cols) if GRAD_GEOM[nm][0] == "rows" else t.transpose(1, 0, 2).reshape(rows, 2 * cols)
    return out


N_DEV = 8


def _allreduce_small(packed, name):
    rows = packed.shape[0]

    def body(x_ref, out_ref, gath, send_sems, recv_sems, local_sem):
        x, y, c = _mesh_pos()
        me, sibling = (x, y, c), (x, y, 1 - c)
        chips = _other_chips(x, y)

        def blk(px, py, pc):
            return gath.at[pl.ds(pl.multiple_of((4 * px + 2 * py + pc) * rows, 8), rows), :]

        def copy(k, block, to, src=None):
            return pltpu.make_async_remote_copy(
                src_ref=blk(*block) if src is None else src, dst_ref=blk(*block), send_sem=send_sems.at[k],
                recv_sem=recv_sems.at[k], device_id=to, device_id_type=MESH)

        mine = pltpu.make_async_copy(x_ref, blk(*me), local_sem)
        mine.start()
        first = [copy(0, me, sibling, src=x_ref)]
        first += [copy(1 + j, me, (*chip, c), src=x_ref) for j, chip in enumerate(chips)]
        for cp in first:
            cp.start()
        passed = [copy(4 + j, (*chip, c), sibling) for j, chip in enumerate(chips)]
        for j, chip in enumerate(chips):
            copy(1 + j, (*chip, c), me).wait_recv()
            passed[j].start()
        copy(0, sibling, me).wait_recv()
        for j, chip in enumerate(chips):
            copy(4 + j, (*chip, 1 - c), me).wait_recv()
        for cp in first + passed:
            cp.wait_send()
        mine.wait()
        tot = gath[0:rows, :]
        for d in range(1, N_DEV):
            tot = tot + gath[d * rows:(d + 1) * rows, :]
        out_ref[...] = tot

    vmem = pl.BlockSpec(memory_space=pltpu.VMEM)
    return pl.pallas_call(
        body, in_specs=[vmem], out_specs=vmem, out_shape=jax.ShapeDtypeStruct((rows, LANES), F32),
        scratch_shapes=[pltpu.VMEM((N_DEV * rows, LANES), F32), pltpu.SemaphoreType.DMA((7,)),
                        pltpu.SemaphoreType.DMA((7,)), pltpu.SemaphoreType.DMA],
        compiler_params=pltpu.CompilerParams(vmem_limit_bytes=VMEM_LIMIT_BYTES),
        name=name)(packed)


def _adamw(w, g, m, v, name):
    rows, cols = w.shape
    tr = 256 if rows % 256 == 0 else rows

    def body(w_ref, g_ref, m_ref, v_ref, d_ref, mo_ref, vo_ref):
        gv = g_ref[...]
        mn = ADAM_B1 * m_ref[...] + (1.0 - ADAM_B1) * gv
        vn = ADAM_B2 * v_ref[...] + (1.0 - ADAM_B2) * (gv * gv)
        m_hat = mn / (1.0 - ADAM_B1 ** ADAM_STEP)
        v_hat = vn / (1.0 - ADAM_B2 ** ADAM_STEP)
        d_ref[...] = -ADAM_LR * (m_hat / (jnp.sqrt(v_hat) + ADAM_EPS) + ADAM_WD * w_ref[...])
        mo_ref[...] = mn
        vo_ref[...] = vn

    spec = pl.BlockSpec((tr, cols), lambda i: (i, 0))
    return pl.pallas_call(
        body, grid=(rows // tr,), in_specs=[spec] * 4, out_specs=[spec] * 3,
        out_shape=[jax.ShapeDtypeStruct((rows, cols), F32)] * 3, compiler_params=_cparams("parallel"),
        name=name)(w, g, m, v)


def _adamw_nd(w, g, m, v, name):
    cols = w.shape[-1] if w.shape[-1] % LANES == 0 else LANES
    outs = _adamw(*(t.reshape(-1, cols) for t in (w, g, m, v)), name)
    return tuple(t.reshape(w.shape) for t in outs)


def _pack(arrays):
    return jnp.concatenate([a.reshape(-1, LANES) for a in arrays], axis=0)


def _unpack(packed, shapes):
    out, row = [], 0
    for sh in shapes:
        n = math.prod(sh) // LANES
        out.append(packed[row:row + n].reshape(sh))
        row += n
    return out


WEIGHTS = ("pre_mix_norm", "w_in", "v_norm_g", "v_norm_b", "w_spatial", "b_spatial", "out_norm_a", "out_norm_b",
           "w_out", "post_mix_norm", "pre_ffn_norm", "w_up", "conv_w", "conv_b", "w_down", "post_ffn_norm")


def kernel(x, pre_mix_norm, w_in, v_norm_g, v_norm_b, w_spatial, b_spatial, out_norm_a, out_norm_b, w_out, post_mix_norm, pre_ffn_norm, w_up, conv_w, conv_b, w_down, post_ffn_norm, loss_target, m_pre_mix_norm, m_w_in, m_v_norm_g, m_v_norm_b, m_w_spatial, m_b_spatial, m_out_norm_a, m_out_norm_b, m_w_out, m_post_mix_norm, m_pre_ffn_norm, m_w_up, m_conv_w, m_conv_b, m_w_down, m_post_ffn_norm, v_pre_mix_norm, v_w_in, v_v_norm_g, v_v_norm_b, v_w_spatial, v_b_spatial, v_out_norm_a, v_out_norm_b, v_w_out, v_post_mix_norm, v_pre_ffn_norm, v_w_up, v_conv_w, v_conv_b, v_w_down, v_post_ffn_norm):
    w = dict(pre_mix_norm=pre_mix_norm, w_in=w_in, v_norm_g=v_norm_g, v_norm_b=v_norm_b, w_spatial=w_spatial,
             b_spatial=b_spatial, out_norm_a=out_norm_a, out_norm_b=out_norm_b, w_out=w_out,
             post_mix_norm=post_mix_norm, pre_ffn_norm=pre_ffn_norm, w_up=w_up, conv_w=conv_w, conv_b=conv_b,
             w_down=w_down, post_ffn_norm=post_ffn_norm)
    m = dict(pre_mix_norm=m_pre_mix_norm, w_in=m_w_in, v_norm_g=m_v_norm_g, v_norm_b=m_v_norm_b,
             w_spatial=m_w_spatial, b_spatial=m_b_spatial, out_norm_a=m_out_norm_a, out_norm_b=m_out_norm_b,
             w_out=m_w_out, post_mix_norm=m_post_mix_norm, pre_ffn_norm=m_pre_ffn_norm, w_up=m_w_up,
             conv_w=m_conv_w, conv_b=m_conv_b, w_down=m_w_down, post_ffn_norm=m_post_ffn_norm)
    v = dict(pre_mix_norm=v_pre_mix_norm, w_in=v_w_in, v_norm_g=v_v_norm_g, v_norm_b=v_v_norm_b,
             w_spatial=v_w_spatial, b_spatial=v_b_spatial, out_norm_a=v_out_norm_a, out_norm_b=v_out_norm_b,
             w_out=v_w_out, post_mix_norm=v_post_mix_norm, pre_ffn_norm=v_pre_ffn_norm, w_up=v_w_up,
             conv_w=v_conv_w, conv_b=v_conv_b, w_down=v_w_down, post_ffn_norm=v_post_ffn_norm)
    pos = jnp.stack([lax.axis_index("x"), lax.axis_index("y"), lax.axis_index("c")]).astype(jnp.int32)
    chip = 2 * lax.axis_index("x") + lax.axis_index("y")

    cw_cols = conv_w.shape[-1]
    blocks = [{n: w[n][l].astype(BF16) for n in BIG} for l in range(DEPTH)]
    w_in0, cw_all = _gather_weights([blocks[0]["w_in"], conv_w.reshape(-1, LANES)], "gather_w_in_l0")
    wg = dict(w_in=w_in0)
    conv_w_full = cw_all.reshape(N_CHIPS, DEPTH, 3, cw_cols).transpose(1, 2, 0, 3).reshape(DEPTH, 3, 2 * D_FF)

    small = {n: w[n] for n in SMALL}
    xs, target = x[0], loss_target[0]
    tabs = _rope_tables(xs.shape[0])
    params = [_layer_params(l, small, conv_w_full) for l in range(DEPTH)]
    saved, wgs = [], []
    xin = xs
    h = _rms_cast(xin, params[0]["pre_mix_norm"], "pre_mix_l0")
    for l in range(DEPTH):
        sv, gathered, wg = _layer_forward(l, xin, h, params[l], wg, tabs,
                                          [blocks[l + 1][n] for n in BIG] if l + 1 < DEPTH else None,
                                          blocks[0] if l == 0 else None,
                                          params[l + 1]["pre_mix_norm"] if l + 1 < DEPTH else None)
        saved.append(sv)
        wgs.append(wg)
        if l + 1 < DEPTH:
            wg = dict(zip(BIG, gathered))
            xin, h = sv["x2"], sv["h_next"]
    loss_part, dx, df, g_post = _loss_norm_bwd(saved[-1]["x1"], saved[-1]["f"], params[-1]["post_ffn_norm"], target,
                                               "loss")
    smalls, shards = [None] * DEPTH, [{} for _ in range(DEPTH)]
    pending = None
    for l in reversed(range(DEPTH)):
        dx1, dh1, big, smalls[l], scattered, done = _layer_backward(l, dx, df, saved[l], params[l], wgs[l], tabs, pos,
                                                                    pending[1:] if pending else None, hide=l == 0)
        smalls[l]["post_ffn_norm"] = g_post
        if l > 0:
            dx, smalls[l]["pre_mix_norm"], df, g_post = _norm_bwd_in_out(
                dx1, dh1, saved[l]["x0"], params[l]["pre_mix_norm"], saved[l - 1]["f"], params[l - 1]["post_ffn_norm"],
                f"norm_bwd_in_out_l{l}")
        else:
            dx, smalls[l]["pre_mix_norm"] = _norm_bwd_in(dx1, dh1, saved[l]["x0"], params[l]["pre_mix_norm"],
                                                         "norm_bwd_in_l0")
        if pending:
            shards[pending[0]].update(_gradient_shards(pending[0], pending[1], scattered, pos, pending[2]))
        for names, (sums, received) in done.items():
            shards[l].update(_gradient_shards(l, sums, received, pos, names))
        names = tuple(big)
        pending = (l, _chip_sums(l, big, pos, names), names) if names else None
    if pending:
        shards[pending[0]].update(_gradient_shards(
            pending[0], pending[1], _chip_scatter(pending[1], pending[2], f"chip_scatter_l{pending[0]}"), pos,
            pending[2]))

    small_shapes = [w[n].shape for n in SMALL]
    stacked = [jnp.stack([smalls[l][n].reshape(w[n].shape[1:]) for l in range(DEPTH)]) for n in SMALL]
    cw_grad = jnp.stack([smalls[l]["conv_w"] for l in range(DEPTH)])
    packed = _pack(stacked + [cw_grad, loss_part])
    total = _allreduce_small(packed, "allreduce_small")
    parts = _unpack(total, small_shapes + [cw_grad.shape, (8, LANES)])
    g_small = dict(zip(SMALL, parts[:len(SMALL)]))
    loss = parts[-1][0, 0]
    g_conv_w = lax.dynamic_slice(parts[-2], (0, 0, chip * cw_cols), conv_w.shape)

    grads = {n: jnp.stack([shards[l][n] for l in range(DEPTH)]) for n in BIG}
    grads.update(g_small)
    grads["conv_w"] = g_conv_w

    dp, mp, vp = _adamw(_pack([w[n] for n in SMALL]), _pack([g_small[n] for n in SMALL]),
                        _pack([m[n] for n in SMALL]), _pack([v[n] for n in SMALL]), "adamw_small")
    delta = dict(zip(SMALL, _unpack(dp, small_shapes)))
    new_m = dict(zip(SMALL, _unpack(mp, small_shapes)))
    new_v = dict(zip(SMALL, _unpack(vp, small_shapes)))
    for n in BIG + ("conv_w",):
        delta[n], new_m[n], new_v[n] = _adamw_nd(w[n], grads[n], m[n], v[n], "adamw_" + n)

    return (loss, dx[None], *[grads[n] for n in WEIGHTS], *[delta[n] for n in WEIGHTS],
            *[new_m[n] for n in WEIGHTS], *[new_v[n] for n in WEIGHTS])
```

```python
import functools
import math

import jax
import jax.numpy as jnp
import numpy as np
from jax import lax
from jax.experimental import pallas as pl
from jax.experimental.pallas import tpu as pltpu

F32 = jnp.float32
BF16 = jnp.bfloat16
MESH = pl.DeviceIdType.MESH

D_MODEL = 1024
A_WIDTH = 512
A_GROUPS = 4
GROUP_DIM = 128
CHUNK = 128
B_WIDTH = 512
HEAD_DIM = 64
ROT_DIM = 16
ROPE_THETA = 500000.0
DILATIONS = (1, 4, 16)
BAND = 128
IN_COLS = 2560
D_FF = 4096
EPS = 1e-6
NEG_INF = -1e30
N_CHIPS = 4
LANES = 128

ADAM_LR = 0.001
ADAM_B1 = 0.9
ADAM_B2 = 0.999
ADAM_EPS = 1e-08
ADAM_WD = 0.01
ADAM_STEP = 10

VMEM_LIMIT_BYTES = 56 * 1024 * 1024
RSQRT2 = 0.7071067811865476
INV_SQRT_2PI = 0.3989422804014327
GELU_C = 0.7978845608028654
GELU_A = 0.044715

ANY = pl.BlockSpec(memory_space=pl.ANY)
NN = ((1,), (0,))
NT = ((1,), (1,))
TN = ((0,), (0,))


def _cparams(*sem):
    return pltpu.CompilerParams(dimension_semantics=sem, vmem_limit_bytes=VMEM_LIMIT_BYTES)


def _dot(a, b, dims):
    return lax.dot_general(a, b, (dims, ((), ())), preferred_element_type=F32)


def _rsq_mean(a):
    return lax.rsqrt(jnp.mean(a * a, axis=-1, keepdims=True) + EPS)


def _rms_bwd(a, r, g, dz):
    t = dz * g
    da = r * t - a * (r * r * r) * jnp.mean(t * a, axis=-1, keepdims=True)
    return da, dz * a * r


def _colsum(a):
    return jnp.sum(a, axis=0, keepdims=True)


def _gelu_tanh(x):
    u = x * x
    t = jnp.tanh(x * (GELU_C + (GELU_C * GELU_A) * u))
    hx = 0.5 * x
    act = hx + hx * t
    grad = 0.5 + 0.5 * t + (hx - hx * t * t) * (GELU_C + (3.0 * GELU_C * GELU_A) * u)
    return act, grad


def _grid_edges(grid):
    ids = [pl.program_id(ax) for ax in range(len(grid))]
    first = functools.reduce(jnp.logical_and, [i == 0 for i in ids])
    last = functools.reduce(jnp.logical_and, [i == n - 1 for i, n in zip(ids, grid)])
    return first, last


def _matmul(a, b, *, grid, a_spec, b_spec, o_spec, o_shape, o_dtype, dims, nk, kaxis, acc_shape, name, b_2d=None,
            halves=False, scatter=None, gather=None):
    assert scatter is None or gather is None
    ns = len(scatter[0]) if scatter else len(gather) if gather else 0

    def body(*refs):
        a_ref, b_ref = refs[:2]
        o_ref = refs[2 + ns]
        scratch = refs[3 + 2 * ns:]
        if ns:
            first, last = _grid_edges(grid)
            if scatter:
                start, finish = _scatter_steps(refs[2:2 + ns], refs[3 + ns:3 + 2 * ns], scratch[-2], scratch[-1],
                                               scatter[1])
            else:
                start, relay, last_wait = _gather_steps(refs[2:2 + ns], refs[3 + ns:3 + 2 * ns], scratch[-2],
                                                        scratch[-1])

                def finish():
                    relay()
                    last_wait()
            pl.when(first)(start)
        def store(val):
            if halves:
                half = val.shape[1] // 2
                o_ref[0] = val[:, :half].astype(o_dtype)
                o_ref[1] = val[:, half:].astype(o_dtype)
            else:
                o_ref[...] = val.astype(o_dtype)

        bv = b_ref[...] if b_2d is None else b_ref[...].reshape(b_2d)
        part = _dot(a_ref[...], bv, dims)
        if nk == 1:
            store(part)
        else:
            acc = scratch[0]
            k = pl.program_id(kaxis)

            @pl.when(k == 0)
            def _():
                acc[...] = part

            @pl.when(k > 0)
            def _():
                acc[...] += part

            @pl.when(k == nk - 1)
            def _():
                store(acc[...])

        if ns:
            pl.when(last)(finish)

    sem = tuple("arbitrary" if (ns or (nk > 1 and ax == kaxis)) else "parallel" for ax in range(len(grid)))
    riding = list(scatter[0]) if scatter else list(gather or [])
    rider_shapes = _scattered_shapes(scatter[1]) if scatter else _gathered_shapes(riding)
    rider_sems = _scatter_sems(ns) if scatter else _gather_sems(ns) if gather else []
    res = pl.pallas_call(
        body, grid=grid, in_specs=[a_spec, b_spec] + [ANY] * ns, out_specs=[o_spec] + [ANY] * ns,
        out_shape=[jax.ShapeDtypeStruct(o_shape, o_dtype)] + rider_shapes,
        scratch_shapes=([pltpu.VMEM(acc_shape, F32)] if nk > 1 else []) + rider_sems,
        compiler_params=_cparams(*sem), name=name)(a, b, *riding)
    return (res[0], list(res[1:])) if ns else res[0]


def _mix_out_norm(mixed, w_out, x0, g_post, g_next, name, gather=None):
    s, d = x0.shape
    tm = 512
    ng = 0 if gather is None else len(gather)

    def body(a_ref, w_ref, x_ref, gp_ref, gn_ref, *rest):
        y_ref, x1_ref, h_ref = rest[ng:ng + 3]
        if ng:
            start, relay, finish = _gather_steps(rest[:ng], rest[ng + 3:2 * ng + 3], *rest[2 * ng + 3:])
            first, last = _grid_edges((s // tm,))
            pl.when(first)(start)
        y = _dot(a_ref[...], w_ref[...].reshape(d, d), NN)
        y_ref[...] = y
        x1 = x_ref[...] + y * _rsq_mean(y) * gp_ref[...]
        x1_ref[...] = x1
        h_ref[...] = (x1 * _rsq_mean(x1) * gn_ref[...]).astype(BF16)

        if ng:
            @pl.when(last)
            def _():
                relay()
                finish()

    row = pl.BlockSpec((tm, d), lambda i: (i, 0))
    vec = pl.BlockSpec((1, d), lambda i: (0, 0))
    res = pl.pallas_call(
        body, grid=(s // tm,),
        in_specs=[row, pl.BlockSpec((N_CHIPS, None, d // N_CHIPS, d), lambda i: (0, 0, 0, 0)), row, vec, vec]
        + [ANY] * ng,
        out_specs=[row, row, row] + [ANY] * ng,
        out_shape=[jax.ShapeDtypeStruct((s, d), F32), jax.ShapeDtypeStruct((s, d), F32),
                   jax.ShapeDtypeStruct((s, d), BF16)] + _gathered_shapes(gather or []),
        scratch_shapes=_gather_sems(ng) if ng else [],
        compiler_params=_cparams("arbitrary" if ng else "parallel"), name=name)(mixed, w_out, x0, g_post, g_next,
                                                                              *(gather or []))
    return res[:3], list(res[3:])


def _proj_bwd(dproj, w_in, name, scatter=None):
    s = dproj.shape[0]
    wcol = IN_COLS // N_CHIPS
    ns = 0 if scatter is None else len(scatter[0])

    def body(*refs):
        a_ref, w_ref = refs[:2]
        o_ref = refs[2 + ns]
        if ns:
            start, finish = _scatter_steps(refs[2:2 + ns], refs[3 + ns:3 + 2 * ns], *refs[3 + 2 * ns:], scatter[1])
            first, last = _grid_edges((s // TMM,))
            pl.when(first)(start)
        acc = _dot(a_ref[:, :wcol], w_ref[0], NT)
        for j in range(1, N_CHIPS):
            acc = acc + _dot(a_ref[:, j * wcol:(j + 1) * wcol], w_ref[j], NT)
        o_ref[...] = acc
        if ns:
            pl.when(last)(finish)

    res = pl.pallas_call(
        body, grid=(s // TMM,),
        in_specs=[pl.BlockSpec((TMM, IN_COLS), lambda i: (i, 0)),
                  pl.BlockSpec((N_CHIPS, None, D_MODEL, wcol), lambda i: (0, 0, 0, 0))] + [ANY] * ns,
        out_specs=[pl.BlockSpec((TMM, D_MODEL), lambda i: (i, 0))] + [ANY] * ns,
        out_shape=[jax.ShapeDtypeStruct((s, D_MODEL), F32)] + (_scattered_shapes(scatter[1]) if ns else []),
        scratch_shapes=_scatter_sems(ns) if ns else [],
        compiler_params=_cparams("arbitrary" if ns else "parallel"), name=name)(dproj, w_in,
                                                                              *(scatter[0] if ns else []))
    return res[0], list(res[1:])


TM = 1024
TMM = 1024


TR = 256


def _row_spec(width, col=0):
    return pl.BlockSpec((TR, width), lambda i, col=col: (i, col))


def _vec_spec(width):
    return pl.BlockSpec((1, width), lambda i: (0, 0))


def _rms_cast(x, g, name):
    s, d = x.shape

    def body(x_ref, g_ref, h_ref):
        a = x_ref[...]
        h_ref[...] = (a * _rsq_mean(a) * g_ref[...]).astype(BF16)

    return pl.pallas_call(
        body, grid=(s // TR,), in_specs=[_row_spec(d), _vec_spec(d)], out_specs=_row_spec(d),
        out_shape=jax.ShapeDtypeStruct((s, d), BF16), compiler_params=_cparams("parallel"), name=name)(x, g)


def _acc_init(refs):
    @pl.when(pl.program_id(0) == 0)
    def _():
        for r in refs:
            r[...] = jnp.zeros_like(r)


def _loss_norm_bwd(x1, f, g_post, target, name):
    s, d = x1.shape

    def body(x_ref, f_ref, gp_ref, t_ref, loss_ref, dx_ref, df_ref, dg_ref):
        _acc_init([loss_ref, dg_ref])
        fv = f_ref[...]
        r = _rsq_mean(fv)
        err = x_ref[...] + fv * r * gp_ref[...] - t_ref[...]
        dx = err * (1.0 / d)
        dx_ref[...] = dx
        part = 0.5 * jnp.sum(jnp.mean(err * err, axis=-1, keepdims=True), axis=0, keepdims=True)
        loss_ref[...] += jnp.broadcast_to(part, loss_ref.shape)
        da, dgt = _rms_bwd(fv, r, gp_ref[...], dx)
        df_ref[...] = da.astype(BF16)
        dg_ref[...] += _colsum(dgt)

    return pl.pallas_call(
        body, grid=(s // TR,), in_specs=[_row_spec(d), _row_spec(d), _vec_spec(d), _row_spec(d)],
        out_specs=[pl.BlockSpec((8, LANES), lambda i: (0, 0)), _row_spec(d), _row_spec(d), _vec_spec(d)],
        out_shape=[jax.ShapeDtypeStruct((8, LANES), F32), jax.ShapeDtypeStruct((s, d), F32),
                   jax.ShapeDtypeStruct((s, d), BF16), jax.ShapeDtypeStruct((1, d), F32)],
        compiler_params=_cparams("arbitrary"), name=name)(x1, f, g_post, target)


def _norm_bwd_mid(dx2, dh2, x1, g_pf, y1, g_pm, name):
    s, d = dx2.shape

    def body(dx2_ref, dh_ref, x1_ref, gpf_ref, y1_ref, gpm_ref, dx1_ref, dy1_ref, dgpf_ref, dgpm_ref):
        _acc_init([dgpf_ref, dgpm_ref])
        x1 = x1_ref[...]
        da, dgt = _rms_bwd(x1, _rsq_mean(x1), gpf_ref[...], dh_ref[...])
        dx1 = dx2_ref[...] + da
        dx1_ref[...] = dx1
        dgpf_ref[...] += _colsum(dgt)
        y1 = y1_ref[...]
        dy, dgt2 = _rms_bwd(y1, _rsq_mean(y1), gpm_ref[...], dx1)
        dy1_ref[...] = dy.astype(BF16)
        dgpm_ref[...] += _colsum(dgt2)

    return pl.pallas_call(
        body, grid=(s // TR,),
        in_specs=[_row_spec(d), _row_spec(d), _row_spec(d), _vec_spec(d), _row_spec(d), _vec_spec(d)],
        out_specs=[_row_spec(d), _row_spec(d), _vec_spec(d), _vec_spec(d)],
        out_shape=[jax.ShapeDtypeStruct((s, d), F32), jax.ShapeDtypeStruct((s, d), BF16),
                   jax.ShapeDtypeStruct((1, d), F32), jax.ShapeDtypeStruct((1, d), F32)],
        compiler_params=_cparams("arbitrary"), name=name)(dx2, dh2, x1, g_pf, y1, g_pm)


def _norm_bwd_in_out(dx1, dh1, x0, g1, f_below, g_post_below, name):
    s, d = dx1.shape

    def body(dx1_ref, dh_ref, x0_ref, g_ref, f_ref, gp_ref, dx0_ref, dg_ref, df_ref, dgp_ref):
        _acc_init([dg_ref, dgp_ref])
        x0 = x0_ref[...]
        da, dgt = _rms_bwd(x0, _rsq_mean(x0), g_ref[...], dh_ref[...])
        dx0 = dx1_ref[...] + da
        dx0_ref[...] = dx0
        dg_ref[...] += _colsum(dgt)
        fv = f_ref[...]
        db, dgt2 = _rms_bwd(fv, _rsq_mean(fv), gp_ref[...], dx0)
        df_ref[...] = db.astype(BF16)
        dgp_ref[...] += _colsum(dgt2)

    return pl.pallas_call(
        body, grid=(s // TR,),
        in_specs=[_row_spec(d), _row_spec(d), _row_spec(d), _vec_spec(d), _row_spec(d), _vec_spec(d)],
        out_specs=[_row_spec(d), _vec_spec(d), _row_spec(d), _vec_spec(d)],
        out_shape=[jax.ShapeDtypeStruct((s, d), F32), jax.ShapeDtypeStruct((1, d), F32),
                   jax.ShapeDtypeStruct((s, d), BF16), jax.ShapeDtypeStruct((1, d), F32)],
        compiler_params=_cparams("arbitrary"), name=name)(dx1, dh1, x0, g1, f_below, g_post_below)


def _norm_bwd_in(dx1, dh1, x0, g1, name):
    s, d = dx1.shape

    def body(dx1_ref, dh_ref, x0_ref, g_ref, dx0_ref, dg_ref):
        _acc_init([dg_ref])
        x0 = x0_ref[...]
        da, dgt = _rms_bwd(x0, _rsq_mean(x0), g_ref[...], dh_ref[...])
        dx0_ref[...] = dx1_ref[...] + da
        dg_ref[...] += _colsum(dgt)

    return pl.pallas_call(
        body, grid=(s // TR,), in_specs=[_row_spec(d), _row_spec(d), _row_spec(d), _vec_spec(d)],
        out_specs=[_row_spec(d), _vec_spec(d)],
        out_shape=[jax.ShapeDtypeStruct((s, d), F32), jax.ShapeDtypeStruct((1, d), F32)],
        compiler_params=_cparams("arbitrary"), name=name)(dx1, dh1, x0, g1)


def _tril_mask():
    row = lax.broadcasted_iota(jnp.int32, (CHUNK, CHUNK), 0)
    col = lax.broadcasted_iota(jnp.int32, (CHUNK, CHUNK), 1)
    return row >= col


def _gating_forward(pa, gv, bv, wt, bsf):
    er = lax.erf(pa * RSQRT2)
    za = 0.5 * pa * (1.0 + er)
    u = za[:, :A_WIDTH]
    va = za[:, A_WIDTH:]
    xc = va - jnp.mean(va, axis=-1, keepdims=True)
    rs = lax.rsqrt(jnp.mean(xc * xc, axis=-1, keepdims=True) + EPS)
    vn = xc * rs
    vlb = (vn * gv + bv).astype(BF16)
    sg = jnp.concatenate(
        [_dot(wt[g], vlb[:, g * GROUP_DIM:(g + 1) * GROUP_DIM], NN) for g in range(A_GROUPS)], axis=1) + bsf
    return er, u, rs, vn, vlb, sg


def _masked_ws(ws_ref):
    mask = _tril_mask()
    return [jnp.where(mask, ws_ref[g], 0.0).astype(BF16) for g in range(A_GROUPS)]


def _mixer_a_fwd(proj, gv, bv, ws, bsf, ga, name, gather=None):
    s = proj.shape[0]
    ng = 0 if gather is None else len(gather)

    def body(p_ref, gv_ref, bv_ref, ws_ref, bs_ref, ga_ref, *rest):
        o_ref = rest[ng]
        if ng:
            start, relay, finish = _gather_steps(rest[:ng], rest[ng + 1:2 * ng + 1], *rest[2 * ng + 1:])
            first, last = _grid_edges((s // TR,))
            pl.when(first)(start)
        wt = _masked_ws(ws_ref)
        for ch in range(TR // CHUNK):
            rows = slice(ch * CHUNK, (ch + 1) * CHUNK)
            _, u, _, _, _, sg = _gating_forward(p_ref[rows, :].astype(F32), gv_ref[...], bv_ref[...], wt, bs_ref[...])
            oa = u * sg
            o_ref[rows, :] = (oa * _rsq_mean(oa) * ga_ref[...]).astype(BF16)

        if ng:
            @pl.when(last)
            def _():
                relay()
                finish()

    res = pl.pallas_call(
        body, grid=(s // TR,),
        in_specs=[_row_spec(2 * A_WIDTH), _vec_spec(A_WIDTH), _vec_spec(A_WIDTH),
                  pl.BlockSpec((A_GROUPS, CHUNK, CHUNK), lambda i: (0, 0, 0)),
                  pl.BlockSpec((CHUNK, A_WIDTH), lambda i: (0, 0)), _vec_spec(A_WIDTH)] + [ANY] * ng,
        out_specs=[_row_spec(A_WIDTH)] + [ANY] * ng,
        out_shape=[jax.ShapeDtypeStruct((s, A_WIDTH + B_WIDTH), BF16)] + _gathered_shapes(gather or []),
        scratch_shapes=_gather_sems(ng) if ng else [],
        compiler_params=_cparams("arbitrary" if ng else "parallel"), name=name)(proj, gv, bv, ws, bsf, ga,
                                                                              *(gather or []))
    return res[0], list(res[1:])


def _mixer_a_bwd(proj, dmixed, gv, bv, ws, bsf, ga, name, scatter=None):
    s = proj.shape[0]
    nsteps = s // TR
    ns = 0 if scatter is None else len(scatter[0])

    def body(*refs):
        p_ref, dm_ref, gv_ref, bv_ref, ws_ref, bs_ref, ga_ref = refs[:7]
        dp_ref, dga_ref, dgv_ref, dbv_ref, dbs_ref, dws_ref = refs[7 + ns:13 + ns]
        if ns:
            start, finish = _scatter_steps(refs[7:7 + ns], refs[13 + ns:13 + 2 * ns], *refs[13 + 2 * ns:], scatter[1])
            first, last = _grid_edges((nsteps,))
            pl.when(first)(start)
        _acc_init([dga_ref, dgv_ref, dbv_ref, dbs_ref, dws_ref])
        mask = _tril_mask()
        wt = _masked_ws(ws_ref)
        gvv = gv_ref[...]
        gav = ga_ref[...]
        for ch in range(TR // CHUNK):
            rows = slice(ch * CHUNK, (ch + 1) * CHUNK)
            pa = p_ref[rows, :].astype(F32)
            er, u, rs, vn, vlb, sg = _gating_forward(pa, gvv, bv_ref[...], wt, bs_ref[...])
            oa = u * sg
            doa, dgt = _rms_bwd(oa, _rsq_mean(oa), gav, dm_ref[rows, :])
            dga_ref[...] += _colsum(dgt)
            du = doa * sg
            dsg = doa * u
            dbs_ref[...] += dsg
            dsgb = dsg.astype(BF16)
            dvl = []
            for g in range(A_GROUPS):
                cols = slice(g * GROUP_DIM, (g + 1) * GROUP_DIM)
                dws_ref[g] += jnp.where(mask, _dot(dsgb[:, cols], vlb[:, cols], NT), 0.0)
                dvl.append(_dot(wt[g], dsgb[:, cols], TN))
            dvl = jnp.concatenate(dvl, axis=1)
            dgv_ref[...] += _colsum(dvl * vn)
            dbv_ref[...] += _colsum(dvl)
            dvn = dvl * gvv
            dva = rs * (dvn - jnp.mean(dvn, axis=-1, keepdims=True)
                        - vn * jnp.mean(dvn * vn, axis=-1, keepdims=True))
            gp = 0.5 * (1.0 + er) + pa * jnp.exp(-0.5 * pa * pa) * INV_SQRT_2PI
            dp_ref[rows, :] = (jnp.concatenate([du, dva], axis=1) * gp).astype(BF16)

        @pl.when(pl.program_id(0) == nsteps - 1)
        def _():
            for g in range(A_GROUPS):
                cols = slice(g * GROUP_DIM, (g + 1) * GROUP_DIM)
                tot = jnp.sum(dbs_ref[:, cols], axis=1, keepdims=True)
                dbs_ref[:, cols] = jnp.broadcast_to(tot, (CHUNK, GROUP_DIM))

        if ns:
            pl.when(last)(finish)

    full = lambda *shape: pl.BlockSpec(shape, lambda i: (0,) * len(shape))
    res = pl.pallas_call(
        body, grid=(nsteps,),
        in_specs=[_row_spec(2 * A_WIDTH), _row_spec(A_WIDTH), _vec_spec(A_WIDTH), _vec_spec(A_WIDTH),
                  full(A_GROUPS, CHUNK, CHUNK), full(CHUNK, A_WIDTH), _vec_spec(A_WIDTH)] + [ANY] * ns,
        out_specs=[_row_spec(2 * A_WIDTH), _vec_spec(A_WIDTH), _vec_spec(A_WIDTH), _vec_spec(A_WIDTH),
                   full(CHUNK, A_WIDTH), full(A_GROUPS, CHUNK, CHUNK)] + [ANY] * ns,
        out_shape=[jax.ShapeDtypeStruct((s, IN_COLS), BF16), jax.ShapeDtypeStruct((1, A_WIDTH), F32),
                   jax.ShapeDtypeStruct((1, A_WIDTH), F32), jax.ShapeDtypeStruct((1, A_WIDTH), F32),
                   jax.ShapeDtypeStruct((CHUNK, A_WIDTH), F32),
                   jax.ShapeDtypeStruct((A_GROUPS, CHUNK, CHUNK), F32)]
        + (_scattered_shapes(scatter[1]) if ns else []),
        scratch_shapes=_scatter_sems(ns) if ns else [],
        compiler_params=_cparams("arbitrary"), name=name)(proj, dmixed, gv, bv, ws, bsf, ga,
                                                          *(scatter[0] if ns else []))
    return res[:6] + (list(res[6:]),)


def _rope_tables(s):
    half = ROT_DIM // 2
    lane = jnp.arange(LANES) % HEAD_DIM
    inv = ROPE_THETA ** (-(2 * (lane % half)).astype(F32) / ROT_DIM)
    ang = jnp.arange(s, dtype=F32)[:, None] * inv[None, :]
    cos, sin = jnp.cos(ang), jnp.sin(ang)
    c = jnp.where(lane < ROT_DIM, cos, 1.0)
    s1 = jnp.where(lane < half, -sin, 0.0)
    s2 = jnp.where((lane >= half) & (lane < ROT_DIM), sin, 0.0)
    return c, s1, s2


def _lane_blocks(width):
    return [slice(b * LANES, (b + 1) * LANES) for b in range(width // LANES)]


CLASS_DILS = tuple(d for d in DILATIONS if d > 1)


def _class_shape(s, dil, dtype):
    return jax.ShapeDtypeStruct((dil, s // dil, B_WIDTH), dtype)


def _class_spec(dil):
    return pl.BlockSpec((dil, TR // dil, B_WIDTH), lambda i, *_: (0, i, 0))


NBLK = B_WIDTH // LANES
STAGE = pltpu.VMEM((NBLK, TR, LANES), F32)


def _stage_put(stage, value):
    for b, sl in enumerate(_lane_blocks(B_WIDTH)):
        stage[b] = value[:, sl]


def _stage_get(stage):
    return jnp.concatenate([stage[b] for b in range(NBLK)], axis=1)


def _store_classes(stage, dst_ref, dil):
    for b, sl in enumerate(_lane_blocks(B_WIDTH)):
        for r in range(dil):
            dst_ref[r, :, sl] = stage[b, pl.ds(r, TR // dil, stride=dil), :].astype(dst_ref.dtype)


def _load_classes(src_ref, stage, dil):
    for b, sl in enumerate(_lane_blocks(B_WIDTH)):
        for r in range(dil):
            stage[b, pl.ds(r, TR // dil, stride=dil), :] = src_ref[r, :, sl].astype(F32)
    return _stage_get(stage)


def _rope_fwd(proj, tabs, name, gather=None):
    s = proj.shape[0]
    half = ROT_DIM // 2
    scale = HEAD_DIM ** -0.5
    nlay = 1 + len(CLASS_DILS)
    ng = 0 if gather is None else len(gather)

    def body(q_ref, k_ref, v_ref, c_ref, s1_ref, s2_ref, *rest):
        outs, stage = rest[ng:ng + 3 * nlay], rest[2 * ng + 3 * nlay]
        if ng:
            start, relay, finish = _gather_steps(rest[:ng], rest[ng + 3 * nlay:2 * ng + 3 * nlay],
                                                 *rest[2 * ng + 3 * nlay + 1:])
            first, last = _grid_edges((s // TR,))
            pl.when(first)(start)
        c, s1, s2 = c_ref[...], s1_ref[...], s2_ref[...]
        for which, (src, mul) in enumerate(((q_ref, scale), (k_ref, 1.0), (v_ref, None))):
            if mul is None:
                _stage_put(stage, src[...].astype(F32))
            else:
                for b, sl in enumerate(_lane_blocks(B_WIDTH)):
                    a = src[:, sl].astype(F32)
                    r = a * c + pltpu.roll(a, LANES - half, 1) * s1 + pltpu.roll(a, half, 1) * s2
                    stage[b] = r * mul
            dst = outs[which * nlay:(which + 1) * nlay]
            dst[0][...] = _stage_get(stage).astype(BF16)
            for ref, d in zip(dst[1:], CLASS_DILS):
                _store_classes(stage, ref, d)

        if ng:
            @pl.when(last)
            def _():
                relay()
                finish()

    tab = pl.BlockSpec((TR, LANES), lambda i: (i, 0))
    lay_specs = [_row_spec(B_WIDTH)] + [_class_spec(d) for d in CLASS_DILS]
    lay_shapes = [jax.ShapeDtypeStruct((s, B_WIDTH), BF16)] + [_class_shape(s, d, BF16) for d in CLASS_DILS]
    outs = pl.pallas_call(
        body, grid=(s // TR,),
        in_specs=[_row_spec(B_WIDTH, 2), _row_spec(B_WIDTH, 3), _row_spec(B_WIDTH, 4), tab, tab, tab] + [ANY] * ng,
        out_specs=lay_specs * 3 + [ANY] * ng, out_shape=lay_shapes * 3 + _gathered_shapes(gather or []),
        scratch_shapes=[STAGE] + (_gather_sems(ng) if ng else []),
        compiler_params=_cparams("arbitrary" if ng else "parallel"), name=name)(proj, proj, proj, *tabs,
                                                                              *(gather or []))
    q, k, v = (dict(zip(DILATIONS, outs[w * nlay:(w + 1) * nlay])) for w in range(3))
    return q, k, v, list(outs[3 * nlay:])


def _as_classes(t):
    return t if t.ndim == 3 else t[None]


def _head_masks():
    lane = lax.broadcasted_iota(jnp.int32, (1, LANES), 1)
    return lane < HEAD_DIM, lane >= HEAD_DIM


def _stack_heads(t):
    lo, hi = _head_masks()
    zero = jnp.zeros_like(t)
    return jnp.concatenate([jnp.where(lo, t, zero), jnp.where(hi, t, zero)], axis=0)


MAX_SEGMENT_BLOCKS = 8


def _segment_masks(j):
    qi = lax.broadcasted_iota(jnp.int32, (BAND, 2 * BAND), 0)
    kj = lax.broadcasted_iota(jnp.int32, (BAND, 2 * BAND), 1)
    both = (kj >= qi) & (kj <= qi + BAND)
    own = kj[:, :BAND] <= qi[:, :BAND]
    head = both & ((kj >= BAND) | (j > 0))
    return tuple(jnp.concatenate([m, m], axis=0) for m in (own, both, head))


def _block_rows(g):
    return pl.ds(pl.multiple_of(g * BAND, BAND), BAND)


def _key_rows(g):
    return pl.ds(pl.multiple_of((g - 1) * BAND, BAND), 2 * BAND)


def _segments(n):
    nb = n // BAND
    seg = min(nb, MAX_SEGMENT_BLOCKS)
    return seg, nb // seg


def _segment_specs(seg):
    main = pl.BlockSpec((None, seg * BAND, B_WIDTH), lambda r, j: (r, j, 0))
    halo = pl.BlockSpec((None, BAND, B_WIDTH), lambda r, j: (r, jnp.maximum(j * seg - 1, 0), 0))
    return main, halo


def _attn_fwd(q, k, v, name, gather=None):
    dil, n, _ = q.shape
    seg, nseg = _segments(n)
    nh = 2 if nseg > 1 else 0
    ng = 0 if gather is None else len(gather)

    def body(*refs):
        q_ref, k_ref, v_ref = refs[:3]
        halos = refs[3:3 + nh]
        o_ref, l_ref = refs[3 + nh + ng:5 + nh + ng]
        if ng:
            start, relay, finish = _gather_steps(refs[3 + nh:3 + nh + ng], refs[5 + nh + ng:5 + nh + 2 * ng],
                                                 *refs[5 + nh + 2 * ng:])
            first, last = _grid_edges((dil, nseg))
            pl.when(first)(start)
        own, both, head = _segment_masks(pl.program_id(1))
        lo, _ = _head_masks()

        def block(rows, keys_of, valid):
            for sl in _lane_blocks(B_WIDTH):
                kk, vv = keys_of(sl)
                sc = jnp.where(valid, _dot(_stack_heads(q_ref[rows, sl]), kk, NT), NEG_INF)
                mx = jnp.max(sc, axis=1, keepdims=True)
                p = jnp.exp(sc - mx)
                den = jnp.sum(p, axis=1, keepdims=True)
                out = _dot(p.astype(BF16), vv, NN) / den
                lse = mx + jnp.log(den)
                o_ref[rows, sl] = jnp.where(lo, out[:BAND], out[BAND:]).astype(BF16)
                l_ref[rows, sl] = jnp.where(lo, lse[:BAND], lse[BAND:])

        if nh:
            block(_block_rows(0), lambda sl: (jnp.concatenate([halos[0][:, sl], k_ref[0:BAND, sl]], axis=0),
                                              jnp.concatenate([halos[1][:, sl], v_ref[0:BAND, sl]], axis=0)), head)
        else:
            block(_block_rows(0), lambda sl: (k_ref[0:BAND, sl], v_ref[0:BAND, sl]), own)

        @pl.loop(1, seg)
        def _(g):
            block(_block_rows(g), lambda sl: (k_ref[_key_rows(g), sl], v_ref[_key_rows(g), sl]), both)

        if ng:
            @pl.when(last)
            def _():
                relay()
                finish()

    main, halo = _segment_specs(seg)
    res = pl.pallas_call(
        body, grid=(dil, nseg), in_specs=[main] * 3 + [halo] * nh + [ANY] * ng, out_specs=[main, main] + [ANY] * ng,
        out_shape=[jax.ShapeDtypeStruct((dil, n, B_WIDTH), BF16), jax.ShapeDtypeStruct((dil, n, B_WIDTH), F32)]
        + _gathered_shapes(gather or []),
        scratch_shapes=_gather_sems(ng) if ng else [],
        compiler_params=_cparams(*(["arbitrary"] * 2 if ng else ["parallel"] * 2)), name=name)(
            q, k, v, *([k, v] if nh else []), *(gather or []))
    return res[0], res[1], list(res[2:])


def _attn_bwd(q, k, v, do, lse, delta, name, scatter=None):
    dil, n, _ = q.shape
    seg, nseg = _segments(n)
    nh = 2 if nseg > 1 else 0
    ns = 0 if scatter is None else len(scatter[0])

    def body(*refs):
        q_ref, k_ref, v_ref, do_ref, lse_ref, dl_ref = refs[:6]
        halos = refs[6:6 + nh]
        dq_ref, dk_ref, dv_ref = refs[6 + nh + ns:9 + nh + ns]
        halo_out = refs[9 + nh + ns:9 + 2 * nh + ns]
        ck_ref, cv_ref = refs[9 + 2 * nh + 2 * ns:11 + 2 * nh + 2 * ns]
        if ns:
            start, finish = _scatter_steps(refs[6 + nh:6 + nh + ns], refs[9 + 2 * nh + ns:9 + 2 * nh + 2 * ns],
                                           *refs[11 + 2 * nh + 2 * ns:], scatter[1])
            first, last = _grid_edges((dil, nseg))
            pl.when(first)(start)
        own, both, head = _segment_masks(pl.program_id(1))
        lo, _ = _head_masks()
        lane = lax.broadcasted_iota(jnp.int32, (1, LANES), 1)

        def per_head(t):
            return jnp.concatenate(
                [jnp.sum(jnp.where(lane == first, t, 0.0), axis=1, keepdims=True) for first in (0, HEAD_DIM)], axis=0)

        def grads(rows, kk, vv, valid, sl):
            q2 = _stack_heads(q_ref[rows, sl])
            do2 = _stack_heads(do_ref[rows, sl])
            p = jnp.where(valid, jnp.exp(_dot(q2, kk, NT) - per_head(lse_ref[rows, sl])), 0.0)
            ds = (p * (_dot(do2, vv, NT) - per_head(dl_ref[rows, sl]))).astype(BF16)
            dq = _dot(ds, kk, NN)
            dq_ref[rows, sl] = jnp.where(lo, dq[:BAND], dq[BAND:]).astype(BF16)
            return _dot(ds, q2, TN), _dot(p.astype(BF16), do2, TN)

        for sl in _lane_blocks(B_WIDTH):
            if nh:
                dkk, dvv = grads(_block_rows(0), jnp.concatenate([halos[0][:, sl], k_ref[0:BAND, sl]], axis=0),
                                 jnp.concatenate([halos[1][:, sl], v_ref[0:BAND, sl]], axis=0), head, sl)
                halo_out[0][:, sl], halo_out[1][:, sl] = dkk[:BAND], dvv[:BAND]
                ck_ref[:, sl], cv_ref[:, sl] = dkk[BAND:], dvv[BAND:]
            else:
                ck_ref[:, sl], cv_ref[:, sl] = grads(_block_rows(0), k_ref[0:BAND, sl], v_ref[0:BAND, sl], own, sl)

        @pl.loop(1, seg)
        def _(g):
            before = _block_rows(g - 1)
            for sl in _lane_blocks(B_WIDTH):
                dkk, dvv = grads(_block_rows(g), k_ref[_key_rows(g), sl], v_ref[_key_rows(g), sl], both, sl)
                dk_ref[before, sl] = (ck_ref[:, sl] + dkk[:BAND]).astype(BF16)
                dv_ref[before, sl] = (cv_ref[:, sl] + dvv[:BAND]).astype(BF16)
                ck_ref[:, sl] = dkk[BAND:]
                cv_ref[:, sl] = dvv[BAND:]

        final = pl.ds((seg - 1) * BAND, BAND)
        dk_ref[final, :] = ck_ref[...].astype(BF16)
        dv_ref[final, :] = cv_ref[...].astype(BF16)

        if ns:
            pl.when(last)(finish)

    main, halo = _segment_specs(seg)
    shape = jax.ShapeDtypeStruct((dil, n, B_WIDTH), BF16)
    halo_shape = jax.ShapeDtypeStruct((dil, nseg, BAND, B_WIDTH), F32)
    halo_spec = pl.BlockSpec((None, None, BAND, B_WIDTH), lambda r, j: (r, j, 0, 0))
    res = pl.pallas_call(
        body, grid=(dil, nseg), in_specs=[main] * 6 + [halo] * nh + [ANY] * ns,
        out_specs=[main] * 3 + [halo_spec] * nh + [ANY] * ns,
        out_shape=[shape] * 3 + [halo_shape] * nh + (_scattered_shapes(scatter[1]) if ns else []),
        scratch_shapes=[pltpu.VMEM((BAND, B_WIDTH), F32)] * 2 + (_scatter_sems(ns) if ns else []),
        compiler_params=_cparams(*(["arbitrary"] * 2 if ns else ["parallel"] * 2)), name=name)(
            q, k, v, do, lse, delta, *([k, v] if nh else []), *(scatter[0] if ns else []))
    return res[0], res[1], res[2], (tuple(res[3:3 + nh]) if nh else None), list(res[3 + nh:])


def _attn_combine(outs, lses, gb, mixed, name, gather=None):
    s = mixed.shape[0]
    npat = len(DILATIONS)
    w = B_WIDTH
    ng = 0 if gather is None else len(gather)

    def body(*refs):
        o_refs, l_refs = refs[:npat], refs[npat:2 * npat]
        g_ref = refs[2 * npat]
        ob_ref = refs[2 * npat + 2 + ng]
        lse_refs = refs[2 * npat + 3 + ng:3 * npat + 3 + ng]
        mb_ref = refs[3 * npat + 3 + ng]
        stage = refs[3 * npat + 4 + 2 * ng]
        if ng:
            start, relay, finish = _gather_steps(refs[2 * npat + 2:2 * npat + 2 + ng],
                                                 refs[3 * npat + 4 + ng:3 * npat + 4 + 2 * ng],
                                                 *refs[3 * npat + 5 + 2 * ng:])
            first, last = _grid_edges((s // TR,))
            pl.when(first)(start)
        os_ = [o_refs[0][...].astype(F32)] + [_load_classes(r, stage, d) for r, d in zip(o_refs[1:], CLASS_DILS)]
        ls = [l_refs[0][...]] + [_load_classes(r, stage, d) for r, d in zip(l_refs[1:], CLASS_DILS)]
        mx = functools.reduce(jnp.maximum, ls)
        ws = [jnp.exp(l - mx) for l in ls]
        tot = functools.reduce(lambda a, b: a + b, ws)
        ob = functools.reduce(lambda a, b: a + b, [wt / tot * o for wt, o in zip(ws, os_)])
        ob_ref[...] = ob
        lse = mx + jnp.log(tot)
        _stage_put(stage, lse)
        lse_refs[0][...] = lse
        for ref, d in zip(lse_refs[1:], CLASS_DILS):
            _store_classes(stage, ref, d)
        mb_ref[...] = (ob * _rsq_mean(ob) * g_ref[...]).astype(BF16)

        if ng:
            @pl.when(last)
            def _():
                relay()
                finish()

    lay_specs = [_row_spec(w)] + [_class_spec(d) for d in CLASS_DILS]
    res = pl.pallas_call(
        body, grid=(s // TR,), in_specs=lay_specs * 2 + [_vec_spec(w), ANY] + [ANY] * ng,
        out_specs=[_row_spec(w)] + lay_specs + [_row_spec(w, 1)] + [ANY] * ng,
        out_shape=[jax.ShapeDtypeStruct((s, w), F32), jax.ShapeDtypeStruct((s, w), F32)]
        + [_class_shape(s, d, F32) for d in CLASS_DILS] + [jax.ShapeDtypeStruct(mixed.shape, mixed.dtype)]
        + _gathered_shapes(gather or []),
        scratch_shapes=[STAGE] + (_gather_sems(ng) if ng else []), input_output_aliases={2 * npat + 1: npat + 1},
        compiler_params=_cparams("arbitrary" if ng else "parallel"), name=name)(*outs, *lses, gb, mixed,
                                                                              *(gather or []))
    return res[0], dict(zip(DILATIONS, res[1:npat + 1])), res[npat + 1], list(res[npat + 2:])


def _attn_bwd_prep(dmixed, ob, gb, name):
    s = ob.shape[0]
    w = B_WIDTH
    nlay = len(DILATIONS)

    def body(dm_ref, ob_ref, g_ref, *rest):
        do_refs, dl_refs = rest[:nlay], rest[nlay:2 * nlay]
        dg_ref, stage = rest[2 * nlay:]
        _acc_init([dg_ref])
        ob = ob_ref[...]
        dob, dgt = _rms_bwd(ob, _rsq_mean(ob), g_ref[...], dm_ref[...])
        dg_ref[...] += _colsum(dgt)
        _stage_put(stage, dob)
        do_refs[0][...] = dob.astype(BF16)
        for ref, d in zip(do_refs[1:], CLASS_DILS):
            _store_classes(stage, ref, d)
        lo, hi = _head_masks()
        t = dob * ob
        for b, sl in enumerate(_lane_blocks(w)):
            tb = t[:, sl]
            s0 = jnp.sum(jnp.where(lo, tb, 0.0), axis=1, keepdims=True)
            s1 = jnp.sum(jnp.where(hi, tb, 0.0), axis=1, keepdims=True)
            stage[b] = jnp.where(lo, s0, s1)
        dl_refs[0][...] = _stage_get(stage)
        for ref, d in zip(dl_refs[1:], CLASS_DILS):
            _store_classes(stage, ref, d)

    lay_specs = [_row_spec(w)] + [_class_spec(d) for d in CLASS_DILS]
    shapes = lambda dt: [jax.ShapeDtypeStruct((s, w), dt)] + [_class_shape(s, d, dt) for d in CLASS_DILS]
    res = pl.pallas_call(
        body, grid=(s // TR,), in_specs=[_row_spec(w, 1), _row_spec(w), _vec_spec(w)],
        out_specs=lay_specs * 2 + [_vec_spec(w)],
        out_shape=shapes(BF16) + shapes(F32) + [jax.ShapeDtypeStruct((1, w), F32)],
        scratch_shapes=[STAGE],
        compiler_params=_cparams("arbitrary"), name=name)(dmixed, ob, gb)
    return dict(zip(DILATIONS, res[:nlay])), dict(zip(DILATIONS, res[nlay:2 * nlay])), res[2 * nlay]


def _rope_bwd(dqs, dks, dvs, halos, tabs, dproj, name):
    s = dproj.shape[0]
    half = ROT_DIM // 2
    scale = HEAD_DIM ** -0.5
    npat = len(DILATIONS)
    w = B_WIDTH
    nseg = halos[0].shape[0]
    per = s // nseg // TR

    def body(*refs):
        groups = [refs[g * npat:(g + 1) * npat] for g in range(3)]
        halo_refs = (None,) + tuple(refs[3 * npat:3 * npat + 2])
        c_ref, s1_ref, s2_ref, _, o_ref, stage = refs[3 * npat + 2:]
        i = pl.program_id(0)
        at_edge = ((i + 1) % per == 0) & ((i + 1) // per < nseg)

        def total(rs, halo_ref=None):
            acc = rs[0][...].astype(F32)
            if halo_ref is not None:
                edge = jnp.concatenate([jnp.zeros((TR - BAND, w), F32), halo_ref[...]], axis=0)
                acc = acc + jnp.where(at_edge, edge, 0.0)
            for ref, d in zip(rs[1:], CLASS_DILS):
                acc = acc + _load_classes(ref, stage, d)
            return acc

        def unrope(g):
            c, s1, s2 = c_ref[...], s1_ref[...], s2_ref[...]
            for sl in _lane_blocks(w):
                gb = g[:, sl]
                o = gb * c + pltpu.roll(gb * s1, half, 1) + pltpu.roll(gb * s2, LANES - half, 1)
                o_ref[:, sl] = o.astype(BF16)

        which = pl.program_id(1)

        @pl.when(which == 0)
        def _():
            unrope(total(groups[0]) * scale)

        @pl.when(which == 1)
        def _():
            unrope(total(groups[1], halo_refs[1]))

        @pl.when(which == 2)
        def _():
            o_ref[...] = total(groups[2], halo_refs[2]).astype(BF16)

    tab = pl.BlockSpec((TR, LANES), lambda i, j: (i, 0))
    nat = pl.BlockSpec((TR, w), lambda i, j: (i, 0))
    lay_specs = [nat] + [_class_spec(d) for d in CLASS_DILS]
    edge_spec = pl.BlockSpec((None, BAND, w), lambda i, j: (jnp.minimum((i + 1) // per, nseg - 1), 0, 0))
    first_col = 2 * A_WIDTH // w
    return pl.pallas_call(
        body, grid=(s // TR, 3), in_specs=lay_specs * 3 + [edge_spec] * 2 + [tab] * 3 + [ANY],
        out_specs=pl.BlockSpec((TR, w), lambda i, j: (i, first_col + j)),
        out_shape=jax.ShapeDtypeStruct(dproj.shape, dproj.dtype), scratch_shapes=[STAGE],
        input_output_aliases={3 * npat + 5: 0},
        compiler_params=_cparams("parallel", "arbitrary"), name=name)(*dqs, *dks, *dvs, *halos, *tabs, dproj)


TK = 512
HALO = 16
FFN_ROWS = 256
FFN_CHUNKS = tuple(slice(r, r + FFN_ROWS) for r in range(0, TM, FFN_ROWS))


def _row_of(v, r):
    rows = lax.broadcasted_iota(jnp.int32, (v.shape[0], 1), 0)
    return jnp.sum(jnp.where(rows == r, v, 0.0), axis=0, keepdims=True)


def _taps_before(x, halo):
    row = lax.broadcasted_iota(jnp.int32, (x.shape[0], 1), 0)
    m1 = jnp.where(row == 0, _row_of(halo, HALO - 1), pltpu.roll(x, 1, 0))
    m2 = jnp.where(row == 0, _row_of(halo, HALO - 2), jnp.where(row == 1, _row_of(halo, HALO - 1), pltpu.roll(x, 2, 0)))
    return m2, m1, x


def _taps_after(x, halo):
    rows = x.shape[0]
    row = lax.broadcasted_iota(jnp.int32, (rows, 1), 0)
    p1 = jnp.where(row == rows - 1, _row_of(halo, 0), pltpu.roll(x, rows - 1, 0))
    p2 = jnp.where(row == rows - 2, _row_of(halo, 0), jnp.where(row == rows - 1, _row_of(halo, 1), pltpu.roll(x, rows - 2, 0)))
    return p1, p2


def _conv_value(taps, cw_ref, cb_ref, h):
    return cb_ref[h] + cw_ref[h, 0:1, :] * taps[0] + cw_ref[h, 1:2, :] * taps[1] + cw_ref[h, 2:3, :] * taps[2]


def _ffn_weight_specs(ncol):
    per_up = (2 * D_FF // N_CHIPS) // TK
    per_dn = (D_FF // N_CHIPS) // TK
    wg = pl.BlockSpec((None, None, D_MODEL, TK), lambda i, j: (j // per_up, 0, 0, j % per_up))
    wv = pl.BlockSpec((None, None, D_MODEL, TK), lambda i, j: ((j + ncol) // per_up, 0, 0, (j + ncol) % per_up))
    wd = pl.BlockSpec((None, None, TK, D_MODEL), lambda i, j: (j // per_dn, 0, j % per_dn, 0))
    cw = pl.BlockSpec((2, 3, TK), lambda i, j: (0, 0, j))
    cb = pl.BlockSpec((2, 1, TK), lambda i, j: (0, 0, j))
    return wg, wv, wd, cw, cb


def _ffn_forward(h2, w_up, w_down, cw3, cb3, name, gather=None, post=None):
    s = h2.shape[0]
    nm, ncol = s // TM, D_FF // TK
    ng = 0 if gather is None else len(gather)
    npost = 0 if post is None else 3
    nout = 4 + (2 if post else 0)

    def body(*refs):
        h_ref, wg_ref, wv_ref, wd_ref, cw_ref, cb_ref = refs[:6]
        post_in = refs[6:6 + npost]
        g_in = refs[6 + npost:6 + npost + ng]
        outs = refs[6 + npost + ng:6 + npost + ng + nout]
        y_ref, up_ref, cv_ref, f_ref = outs[:4]
        g_out = refs[6 + npost + ng + nout:6 + npost + 2 * ng + nout]
        carry = refs[6 + npost + 2 * ng + nout]
        i, j = pl.program_id(0), pl.program_id(1)
        if ng:
            start, relay, finish = _gather_steps(g_in, g_out, *refs[7 + npost + 2 * ng + nout:])
            pl.when((i == 0) & (j == 0))(start)
            pl.when((i == nm - 1) & (j == 0))(relay)

        @pl.when((i == 0) & (j == 0))
        def _():
            carry[...] = jnp.zeros_like(carry)

        @pl.when(j == 0)
        def _():
            f_ref[...] = jnp.zeros_like(f_ref)

        ups = []
        for rs in FFN_CHUNKS:
            hc = h_ref[rs, :]
            ups.append([_dot(hc, w_ref[...], NN).astype(BF16) for w_ref in (wg_ref, wv_ref)])
            for hh in range(2):
                up_ref[hh, rs, :] = ups[-1][hh]
        before = [carry[j, hh] for hh in range(2)]
        for rs, up in zip(FFN_CHUNKS, ups):
            conv = []
            for hh in range(2):
                x = up[hh].astype(F32)
                conv.append(_conv_value(_taps_before(x, before[hh]), cw_ref, cb_ref, hh))
                cv_ref[hh, rs, :] = conv[hh].astype(BF16)
                before[hh] = x[x.shape[0] - HALO:, :]
            y = (_gelu_tanh(conv[0])[0] * conv[1]).astype(BF16)
            y_ref[rs, :] = y
            f_ref[rs, :] += _dot(y, wd_ref[...], NN)
        for hh in range(2):
            carry[j, hh] = before[hh]

        @pl.when(j == ncol - 1)
        def _():
            if post:
                f = f_ref[...]
                x1_ref, gp_ref, gn_ref = post_in
                x2 = x1_ref[...] + f * _rsq_mean(f) * gp_ref[...]
                outs[4][...] = x2
                outs[5][...] = (x2 * _rsq_mean(x2) * gn_ref[...]).astype(BF16)

        if ng:
            pl.when((i == nm - 1) & (j == ncol - 1))(finish)

    wg, wv, wd, cw, cb = _ffn_weight_specs(ncol)
    row = pl.BlockSpec((TM, D_MODEL), lambda i, j: (i, 0))
    vec = pl.BlockSpec((1, D_MODEL), lambda i, j: (0, 0))
    res = pl.pallas_call(
        body, grid=(nm, ncol),
        in_specs=[row, wg, wv, wd, cw, cb] + ([row, vec, vec] if post else []) + [ANY] * ng,
        out_specs=[pl.BlockSpec((TM, TK), lambda i, j: (i, j)), pl.BlockSpec((2, TM, TK), lambda i, j: (0, i, j)),
                   pl.BlockSpec((2, TM, TK), lambda i, j: (0, i, j)), row] + ([row, row] if post else [])
        + [ANY] * ng,
        out_shape=[jax.ShapeDtypeStruct((s, D_FF), BF16), jax.ShapeDtypeStruct((2, s, D_FF), BF16),
                   jax.ShapeDtypeStruct((2, s, D_FF), BF16), jax.ShapeDtypeStruct((s, D_MODEL), F32)]
        + ([jax.ShapeDtypeStruct((s, D_MODEL), F32), jax.ShapeDtypeStruct((s, D_MODEL), BF16)] if post else [])
        + _gathered_shapes(gather or []),
        scratch_shapes=[pltpu.VMEM((ncol, 2, HALO, TK), F32)] + (_gather_sems(ng) if ng else []),
        compiler_params=_cparams("arbitrary", "arbitrary"), name=name)(h2, w_up, w_up, w_down, cw3, cb3,
                                                                      *(post or []), *(gather or []))
    return res[:nout], list(res[nout:])


def _ffn_backward(df, w_up, w_down, up3, cv3, cw3, name, scatter=None):
    s = df.shape[0]
    nm, ncol = s // TM, D_FF // TK
    ns = 0 if scatter is None else len(scatter[0])

    def body(*refs):
        df_ref, wg_ref, wv_ref, wd_ref, cw_ref, up_ref, cv_ref = refs[:7]
        s_in = refs[7:7 + ns]
        dup_ref, dh_ref, sums_ref = refs[7 + ns:10 + ns]
        s_out = refs[10 + ns:10 + 2 * ns]
        carry = refs[10 + 2 * ns]
        i, j = pl.program_id(0), pl.program_id(1)
        if ns:
            start, finish = _scatter_steps(s_in, s_out, *refs[11 + 2 * ns:], scatter[1])
            pl.when((i == 0) & (j == 0))(start)

        @pl.when((i == 0) & (j == 0))
        def _():
            carry[...] = jnp.zeros_like(carry)
            sums_ref[...] = jnp.zeros_like(sums_ref)

        @pl.when(j == 0)
        def _():
            dh_ref[...] = jnp.zeros_like(dh_ref)

        chunks = FFN_CHUNKS[::-1]
        dys = [_dot(df_ref[rs, :], wd_ref[...], NT) for rs in chunks]
        row = lax.broadcasted_iota(jnp.int32, (8, 1), 0)
        after = [carry[j, hh] for hh in range(2)]
        upd = [jnp.zeros((8, TK), F32) for _ in range(2)]
        for rs, dy in zip(chunks, dys):
            act, grad = _gelu_tanh(cv_ref[0, rs, :].astype(F32))
            dcs = (dy * cv_ref[1, rs, :].astype(F32) * grad, dy * act)
            part = dh_ref[rs, :]
            for hh, w_ref in ((0, wg_ref), (1, wv_ref)):
                dc = dcs[hh]
                x = up_ref[hh, rs, :].astype(F32)
                after1, after2 = _taps_after(dc, after[hh])
                for ridx, sm in enumerate((_colsum(after2 * x), _colsum(after1 * x), _colsum(dc * x), _colsum(dc))):
                    upd[hh] = upd[hh] + jnp.where(row == ridx, sm, 0.0)
                dup = (cw_ref[hh, 2:3, :] * dc + cw_ref[hh, 1:2, :] * after1 + cw_ref[hh, 0:1, :] * after2).astype(BF16)
                after[hh] = dc[:HALO, :]
                dup_ref[hh, rs, :] = dup
                part = part + _dot(dup, w_ref[...], NT)
            dh_ref[rs, :] = part
        for hh in range(2):
            sums_ref[j, hh] += upd[hh]
            carry[j, hh] = after[hh]

        if ns:
            pl.when((i == nm - 1) & (j == ncol - 1))(finish)

    wg, wv, wd, cw, _ = _ffn_weight_specs(ncol)
    rev = lambda i: nm - 1 - i
    res = pl.pallas_call(
        body, grid=(nm, ncol),
        in_specs=[pl.BlockSpec((TM, D_MODEL), lambda i, j: (rev(i), 0)), wg, wv, wd, cw,
                  pl.BlockSpec((2, TM, TK), lambda i, j: (0, rev(i), j)),
                  pl.BlockSpec((2, TM, TK), lambda i, j: (0, rev(i), j))] + [ANY] * ns,
        out_specs=[pl.BlockSpec((2, TM, TK), lambda i, j: (0, rev(i), j)),
                   pl.BlockSpec((TM, D_MODEL), lambda i, j: (rev(i), 0)),
                   pl.BlockSpec((ncol, 2, 8, TK), lambda i, j: (0, 0, 0, 0))] + [ANY] * ns,
        out_shape=[jax.ShapeDtypeStruct((2, s, D_FF), BF16), jax.ShapeDtypeStruct((s, D_MODEL), F32),
                   jax.ShapeDtypeStruct((ncol, 2, 8, TK), F32)] + (_scattered_shapes(scatter[1]) if ns else []),
        scratch_shapes=[pltpu.VMEM((ncol, 2, HALO, TK), F32)] + (_scatter_sems(ns) if ns else []),
        compiler_params=_cparams("arbitrary", "arbitrary"), name=name)(df, w_up, w_up, w_down, cw3, up3, cv3,
                                                                      *(scatter[0] if ns else []))
    return res[:3], list(res[3:])


def _wspec(rows, cols, index_map):
    return pl.BlockSpec((None, None, rows, cols), index_map)


def _layer_forward(l, x0, h1, p, wg, tabs, gather=None, late=None, g_next=None):
    s = x0.shape[0]
    nm = s // TMM
    tag = f"_l{l}"
    riders = dict.fromkeys(DILATIONS)
    proj_rider = rope_rider = combine_rider = None
    if late is not None:
        cols = lambda t, parts: [t[:, i * t.shape[1] // parts:(i + 1) * t.shape[1] // parts] for i in range(parts)]
        (down_a, down_b), up_q = cols(late["w_down"], 2), cols(late["w_up"], 4)
        proj_rider, rope_rider, combine_rider = [late["w_out"], down_a], [up_q[2]], [up_q[3]]
        riders = dict(zip(DILATIONS, ([down_b], [up_q[0]], [up_q[1]])))
    proj = _matmul(
        h1, wg["w_in"], grid=(nm, N_CHIPS), a_spec=pl.BlockSpec((TMM, D_MODEL), lambda i, j: (i, 0)),
        b_spec=_wspec(D_MODEL, IN_COLS // N_CHIPS, lambda i, j: (j, 0, 0, 0)),
        o_spec=pl.BlockSpec((TMM, IN_COLS // N_CHIPS), lambda i, j: (i, j)), o_shape=(s, IN_COLS), o_dtype=BF16,
        dims=NN, nk=1, kaxis=None, acc_shape=None, name="proj" + tag, gather=proj_rider)
    if late is not None:
        proj, (w_out_all4, down_a) = proj
    ma, next_out = _mixer_a_fwd(proj, p["v_norm_g"], p["v_norm_b"], p["w_spatial"], p["bs_full"], p["out_norm_a"],
                                "mixer_a_fwd" + tag, [gather["w_out"]] if gather else None)
    q, k, v, rope_landed = _rope_fwd(proj, tabs, "rope_fwd" + tag, rope_rider)
    outs, lses, landed = zip(*[
        _attn_fwd(_as_classes(q[d]), _as_classes(k[d]), _as_classes(v[d]), f"attn_fwd_d{d}" + tag, riders[d])
        for d in DILATIONS])
    outs = [o.reshape(s, B_WIDTH) if d == 1 else o for o, d in zip(outs, DILATIONS)]
    lses = [t.reshape(s, B_WIDTH) if d == 1 else t for t, d in zip(lses, DILATIONS)]
    ob, lse, mixed, combine_landed = _attn_combine(outs, lses, p["out_norm_b"], ma, "attn_combine" + tag,
                                                   combine_rider)
    if late is not None:
        wg = dict(wg, w_out=w_out_all4, w_down=jnp.concatenate([down_a, landed[0][0]], axis=-1),
                  w_up=jnp.concatenate([landed[1][0], landed[2][0], rope_landed[0], combine_landed[0]], axis=-1))
    (y1, x1, h2), next_in = _mix_out_norm(mixed, wg["w_out"], x0, p["post_mix_norm"], p["pre_ffn_norm"],
                                          "mix_out" + tag, [gather["w_in"]] if gather else None)
    post = None if g_next is None else (x1, p["post_ffn_norm"], g_next)
    (y, up3, cv3, f, *after), next_ffn = _ffn_forward(h2, wg["w_up"], wg["w_down"], p["cw3"], p["cb3"], "ffn_fwd" + tag,
                                                      [gather["w_up"], gather["w_down"]] if gather else None, post)
    gathered = dict(w_in=next_in[0], w_out=next_out[0], w_up=next_ffn[0], w_down=next_ffn[1]) if gather else None
    saved = dict(x0=x0, h1=h1, proj=proj, q=q, k=k, v=v, ob=ob, lse=lse, mixed=mixed, y1=y1, x1=x1, h2=h2,
                 up3=up3, cv3=cv3, y=y, f=f)
    if after:
        saved.update(x2=after[0], h_next=after[1])
    return saved, gathered, wg


def _layer_backward(l, dx2, df, sv, p, wg, tabs, pos, scatter=None, hide=False):
    s = dx2.shape[0]
    nm = s // TMM
    tag = f"_l{l}"
    g = {}
    (dup3, dh2, conv_sums), scattered = _ffn_backward(df, wg["w_up"], wg["w_down"], sv["up3"], sv["cv3"], p["cw3"],
                                                      "ffn_bwd" + tag, scatter)
    sums = conv_sums.transpose(1, 2, 0, 3).reshape(2, 8, D_FF)
    g["conv_w"] = jnp.concatenate([sums[0, :3], sums[1, :3]], axis=1)
    g["conv_b"] = jnp.concatenate([sums[0, 3:4], sums[1, 3:4]], axis=1)
    tn = 1024
    done = {}
    gw_down = _matmul(
        sv["y"], df, grid=(D_FF // tn,), a_spec=pl.BlockSpec((s, tn), lambda k: (0, k)),
        b_spec=pl.BlockSpec((s, D_MODEL), lambda k: (0, 0)),
        o_spec=pl.BlockSpec((2, tn, D_MODEL // 2), lambda k: (0, k, 0)),
        o_shape=(2, D_FF, D_MODEL // 2), o_dtype=BF16,
        dims=TN, nk=1, kaxis=None, acc_shape=None, name="w_down_grad" + tag, halves=True)
    down_sums = _chip_sums(l, dict(w_down=gw_down), pos, ("w_down",)) if hide else None
    gw_up = _matmul(
        sv["h2"], dup3, grid=(2 * D_FF // tn,), a_spec=pl.BlockSpec((s, D_MODEL), lambda n: (0, 0)),
        b_spec=pl.BlockSpec((None, s, tn), lambda n: (n // (D_FF // tn), 0, n % (D_FF // tn))),
        o_spec=pl.BlockSpec((None, D_MODEL, tn), lambda n: (n // 2, 0, n % 2)),
        o_shape=(N_CHIPS, D_MODEL, 2 * D_FF // N_CHIPS), o_dtype=BF16,
        dims=TN, nk=1, kaxis=None, acc_shape=None, name="w_up_grad" + tag,
        scatter=(down_sums, ("w_down",)) if hide else None)
    up_sums = None
    if hide:
        gw_up, received = gw_up
        done[("w_down",)] = (down_sums, received)
        up_sums = _chip_sums(l, dict(w_up=gw_up), pos, ("w_up",))
    dx1, dy1, g["pre_ffn_norm"], g["post_mix_norm"] = _norm_bwd_mid(
        dx2, dh2, sv["x1"], p["pre_ffn_norm"], sv["y1"], p["post_mix_norm"], "norm_bwd_mid" + tag)
    w_out_all = pl.BlockSpec((N_CHIPS, None, D_MODEL // N_CHIPS, D_MODEL), lambda i: (0, 0, 0, 0))
    dmixed = _matmul(
        dy1, wg["w_out"], grid=(nm,), a_spec=pl.BlockSpec((TMM, D_MODEL), lambda i: (i, 0)), b_spec=w_out_all,
        o_spec=pl.BlockSpec((TMM, D_MODEL), lambda i: (i, 0)), o_shape=(s, D_MODEL), o_dtype=F32,
        dims=NT, nk=1, kaxis=None, acc_shape=None, name="mix_out_bwd" + tag, b_2d=(D_MODEL, D_MODEL))
    gw_out = _matmul(
        sv["mixed"], dy1, grid=(1,), a_spec=pl.BlockSpec((s, D_MODEL), lambda m: (0, 0)),
        b_spec=pl.BlockSpec((s, D_MODEL), lambda m: (0, 0)),
        o_spec=pl.BlockSpec((2, D_MODEL, D_MODEL // 2), lambda m: (0, 0, 0)),
        o_shape=(2, D_MODEL, D_MODEL // 2), o_dtype=BF16,
        dims=TN, nk=1, kaxis=None, acc_shape=None, name="w_out_grad" + tag, halves=True)
    out_sums = _chip_sums(l, dict(w_out=gw_out), pos, ("w_out",)) if hide else None
    dpa, g["out_norm_a"], g["v_norm_g"], g["v_norm_b"], dbs, g["w_spatial"], received = _mixer_a_bwd(
        sv["proj"], dmixed, p["v_norm_g"], p["v_norm_b"], p["w_spatial"], p["bs_full"], p["out_norm_a"],
        "mixer_a_bwd" + tag, (out_sums, ("w_out",)) if hide else None)
    if hide:
        done[("w_out",)] = (out_sums, received)
    g["b_spatial"] = dbs[:, ::GROUP_DIM].T
    dob, delta, g["out_norm_b"] = _attn_bwd_prep(dmixed, sv["ob"], p["out_norm_b"], "attn_bwd_prep" + tag)
    halves = dict(zip(DILATIONS, ("w_up:0", "w_up:1"))) if hide else {}
    dqs, dks, dvs, edges, received = zip(*[
        _attn_bwd(*(_as_classes(t[d]) for t in (sv["q"], sv["k"], sv["v"], dob, sv["lse"], delta)),
                  f"attn_bwd_d{d}" + tag, (up_sums, (halves[d],)) if d in halves else None)
        for d in DILATIONS])
    if hide:
        done[("w_up",)] = (up_sums, [jnp.concatenate([received[0][0], received[1][0]], axis=-1)])
    nat = lambda ts: [t.reshape(s, B_WIDTH) if d == 1 else t for t, d in zip(ts, DILATIONS)]
    halos = [t[0] for t in edges[0]]
    dproj = _rope_bwd(nat(dqs), nat(dks), nat(dvs), halos, tabs, dpa, "rope_bwd" + tag)
    wcol = IN_COLS // N_CHIPS
    gw_in = _matmul(
        sv["h1"], dproj, grid=(N_CHIPS,), a_spec=pl.BlockSpec((s, D_MODEL), lambda n: (0, 0)),
        b_spec=pl.BlockSpec((s, wcol), lambda n: (0, n)),
        o_spec=pl.BlockSpec((None, D_MODEL, wcol), lambda n: (n, 0, 0)),
        o_shape=(N_CHIPS, D_MODEL, wcol), o_dtype=BF16,
        dims=TN, nk=1, kaxis=None, acc_shape=None, name="w_in_grad" + tag)
    in_sums = _chip_sums(l, dict(w_in=gw_in), pos, ("w_in",)) if hide else None
    dh1, received = _proj_bwd(dproj, wg["w_in"], "proj_bwd" + tag, (in_sums, ("w_in",)) if hide else None)
    if hide:
        done[("w_in",)] = (in_sums, received)
    big = {} if hide else dict(w_in=gw_in, w_up=gw_up, w_out=gw_out, w_down=gw_down)
    return dx1, dh1, big, g, scattered, done


SMALL = ("pre_mix_norm", "v_norm_g", "v_norm_b", "w_spatial", "b_spatial", "out_norm_a", "out_norm_b",
         "post_mix_norm", "pre_ffn_norm", "conv_b", "post_ffn_norm")
BIG = ("w_in", "w_out", "w_up", "w_down")
DEPTH = 2


def _layer_params(l, small, conv_w_full):
    p = {n: small[n][l].reshape(1, -1) for n in SMALL if n not in ("w_spatial", "b_spatial")}
    p["w_spatial"] = small["w_spatial"][l]
    p["bs_full"] = jnp.repeat(small["b_spatial"][l].T, GROUP_DIM, axis=1)
    p["cw3"] = conv_w_full[l].reshape(3, 2, D_FF).transpose(1, 0, 2)
    p["cb3"] = small["conv_b"][l].reshape(2, 1, D_FF)
    return p


def _mesh_pos():
    return lax.axis_index("x"), lax.axis_index("y"), lax.axis_index("c")


def _other_chips(x, y):
    return [(1 - x, y), (x, 1 - y), (1 - x, 1 - y)]


def _gathered_shapes(blocks):
    return [jax.ShapeDtypeStruct((N_CHIPS, 1) + a.shape, a.dtype) for a in blocks]


def _gather_sems(nw):
    n = 2 * nw * (N_CHIPS - 1) + nw
    return [pltpu.SemaphoreType.DMA((n,)), pltpu.SemaphoreType.DMA((n,))]


def _gather_steps(ins, outs, send, recv):
    nw, nrel = len(ins), N_CHIPS - 1
    x, y, c = _mesh_pos()
    mine, sibling, chips = 2 * x + y, (x, y, 1 - c), _other_chips(x, y)

    def copy(src, dst, slot, to):
        return pltpu.make_async_remote_copy(src_ref=src, dst_ref=dst, send_sem=send.at[slot],
                                            recv_sem=recv.at[slot], device_id=to, device_id_type=MESH)

    def half_rows(t, core):
        rows = ins[t].shape[0] // 2
        return pl.ds(pl.multiple_of(core * rows, rows), rows)

    def landing(t, chip, core):
        return outs[t].at[chip, 0, half_rows(t, core), :]

    slots = [(t, r, chip) for t in range(nw) for r, chip in enumerate(chips)]
    own = [copy(ins[t], outs[t].at[mine, 0], 2 * nw * nrel + t, sibling) for t in range(nw)]
    first = [copy(ins[t].at[half_rows(t, c), :], landing(t, mine, c), t * nrel + r, (px, py, c))
             for t, r, (px, py) in slots]
    relays = [copy(landing(t, 2 * px + py, c), landing(t, 2 * px + py, c), nw * nrel + t * nrel + r, sibling)
              for t, r, (px, py) in slots]

    def start():
        for cp in own + first:
            cp.start()

    def relay():
        for (t, r, (px, py)), cp in zip(slots, relays):
            copy(landing(t, 2 * px + py, c), landing(t, 2 * px + py, c), t * nrel + r, (px, py, c)).wait_recv()
            cp.start()

    def finish():
        for t, r, (px, py) in slots:
            passed = landing(t, 2 * px + py, 1 - c)
            copy(passed, passed, nw * nrel + t * nrel + r, sibling).wait_recv()
        for cp in first + relays:
            cp.wait_send()
        for cp in own:
            cp.wait()

    return start, relay, finish


def _gather_weights(blocks, name):
    nw = len(blocks)

    def body(*refs):
        start, relay, finish = _gather_steps(refs[:nw], refs[nw:2 * nw], *refs[2 * nw:])
        start()
        relay()
        finish()

    return pl.pallas_call(
        body, in_specs=[ANY] * nw, out_specs=[ANY] * nw, out_shape=_gathered_shapes(blocks),
        scratch_shapes=_gather_sems(nw), name=name)(*blocks)


HALF = 512

GRAD_GEOM = {"w_in": ("rows", D_MODEL, IN_COLS // N_CHIPS), "w_up": ("rows", D_MODEL, 2 * D_FF // N_CHIPS),
             "w_out": ("cols", D_MODEL, D_MODEL // N_CHIPS), "w_down": ("cols", D_FF, D_FF // N_CHIPS)}


def _exchange_shape(n):
    kind, a, b = GRAD_GEOM[n]
    return (N_CHIPS, HALF, b) if kind == "rows" else (a, HALF)


def _piece_shape(n):
    name, _, part = n.partition(":")
    kind, _, b = GRAD_GEOM[name]
    if part:
        assert kind == "rows"
        return (HALF, b // 2)
    return (HALF, b) if kind == "rows" else (b, HALF)


def _half_of(ref, n, core):
    if GRAD_GEOM[n][0] == "rows":
        return ref.at[:, pl.ds(pl.multiple_of(core * HALF, HALF), HALF), :]
    return ref.at[core]


def _piece_of(ref, n, chip):
    name, _, part = n.partition(":")
    kind, _, b = GRAD_GEOM[name]
    if part:
        return ref.at[chip, :, pl.ds(int(part) * (b // 2), b // 2)]
    return ref.at[chip] if kind == "rows" else ref.at[pl.ds(pl.multiple_of(chip * b, b), b), :]


def _pair_exchange(g, names, name):
    n = len(names)

    def body(*refs):
        send, recv = refs[2 * n:]
        x, y, c = _mesh_pos()
        o = 1 - c
        cps = [pltpu.make_async_remote_copy(src_ref=_half_of(refs[t], nm, o), dst_ref=refs[n + t], send_sem=send.at[t],
                                            recv_sem=recv.at[t], device_id=(x, y, o), device_id_type=MESH)
               for t, nm in enumerate(names)]
        for cp in cps:
            cp.start()
        for cp in cps:
            cp.wait()

    return pl.pallas_call(
        body, in_specs=[ANY] * n, out_specs=[ANY] * n,
        out_shape=[jax.ShapeDtypeStruct(_exchange_shape(nm), BF16) for nm in names],
        scratch_shapes=[pltpu.SemaphoreType.DMA((n,)), pltpu.SemaphoreType.DMA((n,))],
        name=name)(*[g[nm] for nm in names])


def _pair_sum(g, recv, pos, names, name_prefix):
    def add(a, b, grid, a_spec, b_spec, name):
        def body(pos_ref, a_ref, b_ref, o_ref):
            o_ref[...] = (a_ref[...].astype(F32) + b_ref[...].astype(F32)).astype(BF16)

        return pl.pallas_call(
            body, grid_spec=pltpu.PrefetchScalarGridSpec(
                num_scalar_prefetch=1, grid=grid, in_specs=[a_spec, b_spec], out_specs=b_spec),
            out_shape=jax.ShapeDtypeStruct(b.shape, BF16), compiler_params=_cparams("parallel"), name=name)(pos, a, b)

    out = []
    for nm, r in zip(names, recv):
        kind, rows, width = GRAD_GEOM[nm]
        if kind == "rows":
            out.append(add(g[nm], r, (N_CHIPS,), pl.BlockSpec((None, HALF, width), lambda j, pos: (j, pos[2], 0)),
                           pl.BlockSpec((None, HALF, width), lambda j, pos: (j, 0, 0)), f"{name_prefix}_{nm}"))
        else:
            out.append(add(g[nm], r, (rows // D_MODEL,), pl.BlockSpec((None, D_MODEL, HALF), lambda j, pos: (pos[2], j, 0)),
                           pl.BlockSpec((D_MODEL, HALF), lambda j, pos: (j, 0)), f"{name_prefix}_{nm}"))
    return out


def _scattered_shapes(names):
    return [jax.ShapeDtypeStruct((N_CHIPS - 1,) + _piece_shape(nm), BF16) for nm in names]


def _scatter_sems(n):
    return [pltpu.SemaphoreType.DMA((n * (N_CHIPS - 1),)), pltpu.SemaphoreType.DMA((n * (N_CHIPS - 1),))]


def _scatter_steps(sums, outs, send, recv, names):
    nrel = N_CHIPS - 1
    x, y, c = _mesh_pos()
    cps = []
    for r, (px, py) in enumerate(_other_chips(x, y)):
        for t, nm in enumerate(names):
            cps.append(pltpu.make_async_remote_copy(
                src_ref=_piece_of(sums[t], nm, 2 * px + py), dst_ref=outs[t].at[r], send_sem=send.at[t * nrel + r],
                recv_sem=recv.at[t * nrel + r], device_id=(px, py, c), device_id_type=MESH))

    def start():
        for cp in cps:
            cp.start()

    def finish():
        for cp in cps:
            cp.wait()

    return start, finish


def _chip_scatter(sums, names, name):
    n = len(names)

    def body(*refs):
        start, finish = _scatter_steps(refs[:n], refs[n:2 * n], *refs[2 * n:], names)
        start()
        finish()

    return pl.pallas_call(
        body, in_specs=[ANY] * n, out_specs=[ANY] * n, out_shape=_scattered_shapes(names),
        scratch_shapes=_scatter_sems(n), name=name)(*sums)


def _chip_sum(sums, recv, pos, names, name_prefix):
    def add(a, b, a_spec, shape, name):
        def body(pos_ref, a_ref, b_ref, o_ref):
            tot = a_ref[...].astype(F32)
            for r in range(N_CHIPS - 1):
                tot = tot + b_ref[r].astype(F32)
            o_ref[...] = tot

        return pl.pallas_call(
            body, grid_spec=pltpu.PrefetchScalarGridSpec(
                num_scalar_prefetch=1, grid=(1,), in_specs=[a_spec, pl.BlockSpec(b.shape, lambda i, pos: (0, 0, 0))],
                out_specs=pl.BlockSpec((None,) + shape, lambda i, pos: (pos[2], 0, 0))),
            out_shape=jax.ShapeDtypeStruct((2,) + shape, F32), compiler_params=_cparams("arbitrary"),
            name=name)(pos, a, b)

    chip = lambda pos: 2 * pos[0] + pos[1]
    out = []
    for nm, a, b in zip(names, sums, recv):
        shape = _piece_shape(nm)
        if GRAD_GEOM[nm][0] == "rows":
            spec = pl.BlockSpec((None,) + shape, lambda i, pos: (chip(pos), 0, 0))
        else:
            spec = pl.BlockSpec(shape, lambda i, pos: (chip(pos), 0))
        out.append(add(a, b, spec, shape, f"{name_prefix}_{nm}"))
    return out


def _pair_share(totals, name):
    n = len(totals)

    def body(*refs):
        ins, outs = refs[:n], refs[n:2 * n]
        send, recv = refs[2 * n:]
        x, y, c = _mesh_pos()
        o = 1 - c
        cps = [pltpu.make_async_remote_copy(src_ref=ins[t].at[c], dst_ref=outs[t].at[c], send_sem=send.at[t],
                                            recv_sem=recv.at[t], device_id=(x, y, o), device_id_type=MESH)
               for t in range(n)]
        for cp in cps:
            cp.start()
        for t in range(n):
            pltpu.make_async_remote_copy(src_ref=ins[t].at[o], dst_ref=outs[t].at[o], send_sem=send.at[t],
                                         recv_sem=recv.at[t], device_id=(x, y, o), device_id_type=MESH).wait_recv()
        for cp in cps:
            cp.wait_send()

    return pl.pallas_call(
        body, in_specs=[ANY] * n, out_specs=[ANY] * n,
        out_shape=[jax.ShapeDtypeStruct(t.shape, t.dtype) for t in totals],
        scratch_shapes=[pltpu.SemaphoreType.DMA((n,)), pltpu.SemaphoreType.DMA((n,))],
        input_output_aliases={t: t for t in range(n)}, name=name)(*totals)


def _chip_sums(l, g, pos, names):
    tag = f"l{l}_" + "_".join(names)
    recv = _pair_exchange(g, names, "pair_exchange_" + tag)
    return _pair_sum(g, recv, pos, names, "pair_sum_" + tag)


def _gradient_shards(l, sums, scattered, pos, names):
    tag = f"l{l}_" + "_".join(names)
    halves = _pair_share(_chip_sum(sums, scattered, pos, names, "chip_sum_" + tag), "pair_share_" + tag)
    out = {}
    for nm, t in zip(names, halves):
        rows, cols = _piece_shape(nm)
        out[nm] = t.reshape(2 * rows, cols) if GRAD_GEOM[nm][0] == "rows" else t.transpose(1, 0, 2).reshape(rows, 2 * cols)
    return out


N_DEV = 8


def _allreduce_small(packed, name):
    rows = packed.shape[0]

    def body(x_ref, out_ref, gath, send_sems, recv_sems, local_sem):
        x, y, c = _mesh_pos()
        me, sibling = (x, y, c), (x, y, 1 - c)
        chips = _other_chips(x, y)

        def blk(px, py, pc):
            return gath.at[pl.ds(pl.multiple_of((4 * px + 2 * py + pc) * rows, 8), rows), :]

        def copy(k, block, to, src=None):
            return pltpu.make_async_remote_copy(
                src_ref=blk(*block) if src is None else src, dst_ref=blk(*block), send_sem=send_sems.at[k],
                recv_sem=recv_sems.at[k], device_id=to, device_id_type=MESH)

        mine = pltpu.make_async_copy(x_ref, blk(*me), local_sem)
        mine.start()
        first = [copy(0, me, sibling, src=x_ref)]
        first += [copy(1 + j, me, (*chip, c), src=x_ref) for j, chip in enumerate(chips)]
        for cp in first:
            cp.start()
        passed = [copy(4 + j, (*chip, c), sibling) for j, chip in enumerate(chips)]
        for j, chip in enumerate(chips):
            copy(1 + j, (*chip, c), me).wait_recv()
            passed[j].start()
        copy(0, sibling, me).wait_recv()
        for j, chip in enumerate(chips):
            copy(4 + j, (*chip, 1 - c), me).wait_recv()
        for cp in first + passed:
            cp.wait_send()
        mine.wait()
        tot = gath[0:rows, :]
        for d in range(1, N_DEV):
            tot = tot + gath[d * rows:(d + 1) * rows, :]
        out_ref[...] = tot

    vmem = pl.BlockSpec(memory_space=pltpu.VMEM)
    return pl.pallas_call(
        body, in_specs=[vmem], out_specs=vmem, out_shape=jax.ShapeDtypeStruct((rows, LANES), F32),
        scratch_shapes=[pltpu.VMEM((N_DEV * rows, LANES), F32), pltpu.SemaphoreType.DMA((7,)),
                        pltpu.SemaphoreType.DMA((7,)), pltpu.SemaphoreType.DMA],
        compiler_params=pltpu.CompilerParams(vmem_limit_bytes=VMEM_LIMIT_BYTES),
        name=name)(packed)


def _adamw(w, g, m, v, name):
    rows, cols = w.shape
    tr = 256 if rows % 256 == 0 else rows

    def body(w_ref, g_ref, m_ref, v_ref, d_ref, mo_ref, vo_ref):
        gv = g_ref[...]
        mn = ADAM_B1 * m_ref[...] + (1.0 - ADAM_B1) * gv
        vn = ADAM_B2 * v_ref[...] + (1.0 - ADAM_B2) * (gv * gv)
        m_hat = mn / (1.0 - ADAM_B1 ** ADAM_STEP)
        v_hat = vn / (1.0 - ADAM_B2 ** ADAM_STEP)
        d_ref[...] = -ADAM_LR * (m_hat / (jnp.sqrt(v_hat) + ADAM_EPS) + ADAM_WD * w_ref[...])
        mo_ref[...] = mn
        vo_ref[...] = vn

    spec = pl.BlockSpec((tr, cols), lambda i: (i, 0))
    return pl.pallas_call(
        body, grid=(rows // tr,), in_specs=[spec] * 4, out_specs=[spec] * 3,
        out_shape=[jax.ShapeDtypeStruct((rows, cols), F32)] * 3, compiler_params=_cparams("parallel"),
        name=name)(w, g, m, v)


def _adamw_nd(w, g, m, v, name):
    cols = w.shape[-1] if w.shape[-1] % LANES == 0 else LANES
    outs = _adamw(*(t.reshape(-1, cols) for t in (w, g, m, v)), name)
    return tuple(t.reshape(w.shape) for t in outs)


def _pack(arrays):
    return jnp.concatenate([a.reshape(-1, LANES) for a in arrays], axis=0)


def _unpack(packed, shapes):
    out, row = [], 0
    for sh in shapes:
        n = math.prod(sh) // LANES
        out.append(packed[row:row + n].reshape(sh))
        row += n
    return out


WEIGHTS = ("pre_mix_norm", "w_in", "v_norm_g", "v_norm_b", "w_spatial", "b_spatial", "out_norm_a", "out_norm_b",
           "w_out", "post_mix_norm", "pre_ffn_norm", "w_up", "conv_w", "conv_b", "w_down", "post_ffn_norm")


def kernel(x, pre_mix_norm, w_in, v_norm_g, v_norm_b, w_spatial, b_spatial, out_norm_a, out_norm_b, w_out, post_mix_norm, pre_ffn_norm, w_up, conv_w, conv_b, w_down, post_ffn_norm, loss_target, m_pre_mix_norm, m_w_in, m_v_norm_g, m_v_norm_b, m_w_spatial, m_b_spatial, m_out_norm_a, m_out_norm_b, m_w_out, m_post_mix_norm, m_pre_ffn_norm, m_w_up, m_conv_w, m_conv_b, m_w_down, m_post_ffn_norm, v_pre_mix_norm, v_w_in, v_v_norm_g, v_v_norm_b, v_w_spatial, v_b_spatial, v_out_norm_a, v_out_norm_b, v_w_out, v_post_mix_norm, v_pre_ffn_norm, v_w_up, v_conv_w, v_conv_b, v_w_down, v_post_ffn_norm):
    w = dict(pre_mix_norm=pre_mix_norm, w_in=w_in, v_norm_g=v_norm_g, v_norm_b=v_norm_b, w_spatial=w_spatial,
             b_spatial=b_spatial, out_norm_a=out_norm_a, out_norm_b=out_norm_b, w_out=w_out,
             post_mix_norm=post_mix_norm, pre_ffn_norm=pre_ffn_norm, w_up=w_up, conv_w=conv_w, conv_b=conv_b,
             w_down=w_down, post_ffn_norm=post_ffn_norm)
    m = dict(pre_mix_norm=m_pre_mix_norm, w_in=m_w_in, v_norm_g=m_v_norm_g, v_norm_b=m_v_norm_b,
             w_spatial=m_w_spatial, b_spatial=m_b_spatial, out_norm_a=m_out_norm_a, out_norm_b=m_out_norm_b,
             w_out=m_w_out, post_mix_norm=m_post_mix_norm, pre_ffn_norm=m_pre_ffn_norm, w_up=m_w_up,
             conv_w=m_conv_w, conv_b=m_conv_b, w_down=m_w_down, post_ffn_norm=m_post_ffn_norm)
    v = dict(pre_mix_norm=v_pre_mix_norm, w_in=v_w_in, v_norm_g=v_v_norm_g, v_norm_b=v_v_norm_b,
             w_spatial=v_w_spatial, b_spatial=v_b_spatial, out_norm_a=v_out_norm_a, out_norm_b=v_out_norm_b,
             w_out=v_w_out, post_mix_norm=v_post_mix_norm, pre_ffn_norm=v_pre_ffn_norm, w_up=v_w_up,
             conv_w=v_conv_w, conv_b=v_conv_b, w_down=v_w_down, post_ffn_norm=v_post_ffn_norm)
    pos = jnp.stack([lax.axis_index("x"), lax.axis_index("y"), lax.axis_index("c")]).astype(jnp.int32)
    chip = 2 * lax.axis_index("x") + lax.axis_index("y")

    cw_cols = conv_w.shape[-1]
    blocks = [{n: w[n][l].astype(BF16) for n in BIG} for l in range(DEPTH)]
    w_in0, cw_all = _gather_weights([blocks[0]["w_in"], conv_w.reshape(-1, LANES)], "gather_w_in_l0")
    wg = dict(w_in=w_in0)
    conv_w_full = cw_all.reshape(N_CHIPS, DEPTH, 3, cw_cols).transpose(1, 2, 0, 3).reshape(DEPTH, 3, 2 * D_FF)

    small = {n: w[n] for n in SMALL}
    xs, target = x[0], loss_target[0]
    tabs = _rope_tables(xs.shape[0])
    params = [_layer_params(l, small, conv_w_full) for l in range(DEPTH)]
    saved, wgs = [], []
    xin = xs
    h = _rms_cast(xin, params[0]["pre_mix_norm"], "pre_mix_l0")
    for l in range(DEPTH):
        sv, gathered, wg = _layer_forward(l, xin, h, params[l], wg, tabs,
                                          blocks[l + 1] if l + 1 < DEPTH else None,
                                          blocks[0] if l == 0 else None,
                                          params[l + 1]["pre_mix_norm"] if l + 1 < DEPTH else None)
        saved.append(sv)
        wgs.append(wg)
        if l + 1 < DEPTH:
            wg = gathered
            xin, h = sv["x2"], sv["h_next"]
    loss_part, dx, df, g_post = _loss_norm_bwd(saved[-1]["x1"], saved[-1]["f"], params[-1]["post_ffn_norm"], target,
                                               "loss")
    smalls, shards = [None] * DEPTH, [{} for _ in range(DEPTH)]
    pending = None
    for l in reversed(range(DEPTH)):
        dx1, dh1, big, smalls[l], scattered, done = _layer_backward(l, dx, df, saved[l], params[l], wgs[l], tabs, pos,
                                                                    pending[1:] if pending else None, hide=l == 0)
        smalls[l]["post_ffn_norm"] = g_post
        if l > 0:
            dx, smalls[l]["pre_mix_norm"], df, g_post = _norm_bwd_in_out(
                dx1, dh1, saved[l]["x0"], params[l]["pre_mix_norm"], saved[l - 1]["f"], params[l - 1]["post_ffn_norm"],
                f"norm_bwd_in_out_l{l}")
        else:
            dx, smalls[l]["pre_mix_norm"] = _norm_bwd_in(dx1, dh1, saved[l]["x0"], params[l]["pre_mix_norm"],
                                                         "norm_bwd_in_l0")
        if pending:
            shards[pending[0]].update(_gradient_shards(pending[0], pending[1], scattered, pos, pending[2]))
        for names, (sums, received) in done.items():
            shards[l].update(_gradient_shards(l, sums, received, pos, names))
        names = tuple(big)
        pending = (l, _chip_sums(l, big, pos, names), names) if names else None
    if pending:
        shards[pending[0]].update(_gradient_shards(
            pending[0], pending[1], _chip_scatter(pending[1], pending[2], f"chip_scatter_l{pending[0]}"), pos,
            pending[2]))

    small_shapes = [w[n].shape for n in SMALL]
    stacked = [jnp.stack([smalls[l][n].reshape(w[n].shape[1:]) for l in range(DEPTH)]) for n in SMALL]
    cw_grad = jnp.stack([smalls[l]["conv_w"] for l in range(DEPTH)])
    packed = _pack(stacked + [cw_grad, loss_part])
    total = _allreduce_small(packed, "allreduce_small")
    parts = _unpack(total, small_shapes + [cw_grad.shape, (8, LANES)])
    g_small = dict(zip(SMALL, parts[:len(SMALL)]))
    loss = parts[-1][0, 0]
    g_conv_w = lax.dynamic_slice(parts[-2], (0, 0, chip * cw_cols), conv_w.shape)

    grads = {n: jnp.stack([shards[l][n] for l in range(DEPTH)]) for n in BIG}
    grads.update(g_small)
    grads["conv_w"] = g_conv_w

    dp, mp, vp = _adamw(_pack([w[n] for n in SMALL]), _pack([g_small[n] for n in SMALL]),
                        _pack([m[n] for n in SMALL]), _pack([v[n] for n in SMALL]), "adamw_small")
    delta = dict(zip(SMALL, _unpack(dp, small_shapes)))
    new_m = dict(zip(SMALL, _unpack(mp, small_shapes)))
    new_v = dict(zip(SMALL, _unpack(vp, small_shapes)))
    for n in BIG + ("conv_w",):
        delta[n], new_m[n], new_v[n] = _adamw_nd(w[n], grads[n], m[n], v[n], "adamw_" + n)

    return (loss, dx[None], *[grads[n] for n in WEIGHTS], *[delta[n] for n in WEIGHTS],
            *[new_m[n] for n in WEIGHTS], *[new_v[n] for n in WEIGHTS])
```

```python
import functools
import math

import jax
import jax.numpy as jnp
import numpy as np
from jax import lax
from jax.experimental import pallas as pl
from jax.experimental.pallas import tpu as pltpu

F32 = jnp.float32
BF16 = jnp.bfloat16
MESH = pl.DeviceIdType.MESH

D_MODEL = 1024
A_WIDTH = 512
A_GROUPS = 4
GROUP_DIM = 128
CHUNK = 128
B_WIDTH = 512
HEAD_DIM = 64
ROT_DIM = 16
ROPE_THETA = 500000.0
DILATIONS = (1, 4, 16)
BAND = 128
IN_COLS = 2560
D_FF = 4096
EPS = 1e-6
NEG_INF = -1e30
N_CHIPS = 4
LANES = 128

ADAM_LR = 0.001
ADAM_B1 = 0.9
ADAM_B2 = 0.999
ADAM_EPS = 1e-08
ADAM_WD = 0.01
ADAM_STEP = 10

VMEM_LIMIT_BYTES = 56 * 1024 * 1024
RSQRT2 = 0.7071067811865476
INV_SQRT_2PI = 0.3989422804014327
GELU_C = 0.7978845608028654
GELU_A = 0.044715

ANY = pl.BlockSpec(memory_space=pl.ANY)
NN = ((1,), (0,))
NT = ((1,), (1,))
TN = ((0,), (0,))


def _cparams(*sem):
    return pltpu.CompilerParams(dimension_semantics=sem, vmem_limit_bytes=VMEM_LIMIT_BYTES)


def _dot(a, b, dims):
    return lax.dot_general(a, b, (dims, ((), ())), preferred_element_type=F32)


def _rsq_mean(a):
    return lax.rsqrt(jnp.mean(a * a, axis=-1, keepdims=True) + EPS)


def _rms_bwd(a, r, g, dz):
    t = dz * g
    da = r * t - a * (r * r * r) * jnp.mean(t * a, axis=-1, keepdims=True)
    return da, dz * a * r


def _colsum(a):
    return jnp.sum(a, axis=0, keepdims=True)


def _gelu_tanh(x):
    u = x * x
    t = jnp.tanh(x * (GELU_C + (GELU_C * GELU_A) * u))
    hx = 0.5 * x
    act = hx + hx * t
    grad = 0.5 + 0.5 * t + (hx - hx * t * t) * (GELU_C + (3.0 * GELU_C * GELU_A) * u)
    return act, grad


def _grid_edges(grid):
    ids = [pl.program_id(ax) for ax in range(len(grid))]
    first = functools.reduce(jnp.logical_and, [i == 0 for i in ids])
    last = functools.reduce(jnp.logical_and, [i == n - 1 for i, n in zip(ids, grid)])
    return first, last


def _matmul(a, b, *, grid, a_spec, b_spec, o_spec, o_shape, o_dtype, dims, nk, kaxis, acc_shape, name, b_2d=None,
            halves=False, scatter=None, gather=None):
    assert scatter is None or gather is None
    ns = len(scatter[0]) if scatter else len(gather) if gather else 0

    def body(*refs):
        a_ref, b_ref = refs[:2]
        o_ref = refs[2 + ns]
        scratch = refs[3 + 2 * ns:]
        if ns:
            first, last = _grid_edges(grid)
            if scatter:
                start, finish = _scatter_steps(refs[2:2 + ns], refs[3 + ns:3 + 2 * ns], scratch[-2], scratch[-1],
                                               scatter[1])
            else:
                start, relay, last_wait = _gather_steps(refs[2:2 + ns], refs[3 + ns:3 + 2 * ns], scratch[-2],
                                                        scratch[-1])

                def finish():
                    relay()
                    last_wait()
            pl.when(first)(start)
        def store(val):
            if halves:
                half = val.shape[1] // 2
                o_ref[0] = val[:, :half].astype(o_dtype)
                o_ref[1] = val[:, half:].astype(o_dtype)
            else:
                o_ref[...] = val.astype(o_dtype)

        bv = b_ref[...] if b_2d is None else b_ref[...].reshape(b_2d)
        part = _dot(a_ref[...], bv, dims)
        if nk == 1:
            store(part)
        else:
            acc = scratch[0]
            k = pl.program_id(kaxis)

            @pl.when(k == 0)
            def _():
                acc[...] = part

            @pl.when(k > 0)
            def _():
                acc[...] += part

            @pl.when(k == nk - 1)
            def _():
                store(acc[...])

        if ns:
            pl.when(last)(finish)

    sem = tuple("arbitrary" if (ns or (nk > 1 and ax == kaxis)) else "parallel" for ax in range(len(grid)))
    riding = list(scatter[0]) if scatter else list(gather or [])
    rider_shapes = _scattered_shapes(scatter[1]) if scatter else _gathered_shapes(riding)
    rider_sems = _scatter_sems(ns) if scatter else _gather_sems(ns) if gather else []
    res = pl.pallas_call(
        body, grid=grid, in_specs=[a_spec, b_spec] + [ANY] * ns, out_specs=[o_spec] + [ANY] * ns,
        out_shape=[jax.ShapeDtypeStruct(o_shape, o_dtype)] + rider_shapes,
        scratch_shapes=([pltpu.VMEM(acc_shape, F32)] if nk > 1 else []) + rider_sems,
        compiler_params=_cparams(*sem), name=name)(a, b, *riding)
    return (res[0], list(res[1:])) if ns else res[0]


def _mix_out_norm(mixed, w_out, x0, g_post, g_next, name, gather=None):
    s, d = x0.shape
    tm = 512
    ng = 0 if gather is None else len(gather)

    def body(a_ref, w_ref, x_ref, gp_ref, gn_ref, *rest):
        y_ref, x1_ref, h_ref = rest[ng:ng + 3]
        if ng:
            start, relay, finish = _gather_steps(rest[:ng], rest[ng + 3:2 * ng + 3], *rest[2 * ng + 3:])
            first, last = _grid_edges((s // tm,))
            pl.when(first)(start)
        y = _dot(a_ref[...], w_ref[...].reshape(d, d), NN)
        y_ref[...] = y
        x1 = x_ref[...] + y * _rsq_mean(y) * gp_ref[...]
        x1_ref[...] = x1
        h_ref[...] = (x1 * _rsq_mean(x1) * gn_ref[...]).astype(BF16)

        if ng:
            @pl.when(last)
            def _():
                relay()
                finish()

    row = pl.BlockSpec((tm, d), lambda i: (i, 0))
    vec = pl.BlockSpec((1, d), lambda i: (0, 0))
    res = pl.pallas_call(
        body, grid=(s // tm,),
        in_specs=[row, pl.BlockSpec((N_CHIPS, None, d // N_CHIPS, d), lambda i: (0, 0, 0, 0)), row, vec, vec]
        + [ANY] * ng,
        out_specs=[row, row, row] + [ANY] * ng,
        out_shape=[jax.ShapeDtypeStruct((s, d), F32), jax.ShapeDtypeStruct((s, d), F32),
                   jax.ShapeDtypeStruct((s, d), BF16)] + _gathered_shapes(gather or []),
        scratch_shapes=_gather_sems(ng) if ng else [],
        compiler_params=_cparams("arbitrary" if ng else "parallel"), name=name)(mixed, w_out, x0, g_post, g_next,
                                                                              *(gather or []))
    return res[:3], list(res[3:])


def _proj_bwd(dproj, w_in, name, scatter=None):
    s = dproj.shape[0]
    wcol = IN_COLS // N_CHIPS
    ns = 0 if scatter is None else len(scatter[0])

    def body(*refs):
        a_ref, w_ref = refs[:2]
        o_ref = refs[2 + ns]
        if ns:
            start, finish = _scatter_steps(refs[2:2 + ns], refs[3 + ns:3 + 2 * ns], *refs[3 + 2 * ns:], scatter[1])
            first, last = _grid_edges((s // TMM,))
            pl.when(first)(start)
        acc = _dot(a_ref[:, :wcol], w_ref[0], NT)
        for j in range(1, N_CHIPS):
            acc = acc + _dot(a_ref[:, j * wcol:(j + 1) * wcol], w_ref[j], NT)
        o_ref[...] = acc
        if ns:
            pl.when(last)(finish)

    res = pl.pallas_call(
        body, grid=(s // TMM,),
        in_specs=[pl.BlockSpec((TMM, IN_COLS), lambda i: (i, 0)),
                  pl.BlockSpec((N_CHIPS, None, D_MODEL, wcol), lambda i: (0, 0, 0, 0))] + [ANY] * ns,
        out_specs=[pl.BlockSpec((TMM, D_MODEL), lambda i: (i, 0))] + [ANY] * ns,
        out_shape=[jax.ShapeDtypeStruct((s, D_MODEL), F32)] + (_scattered_shapes(scatter[1]) if ns else []),
        scratch_shapes=_scatter_sems(ns) if ns else [],
        compiler_params=_cparams("arbitrary" if ns else "parallel"), name=name)(dproj, w_in,
                                                                              *(scatter[0] if ns else []))
    return res[0], list(res[1:])


TM = 1024
TMM = 1024


TR = 256


def _row_spec(width, col=0):
    return pl.BlockSpec((TR, width), lambda i, col=col: (i, col))


def _vec_spec(width):
    return pl.BlockSpec((1, width), lambda i: (0, 0))


def _rms_cast(x, g, name, gather=None):
    s, d = x.shape
    ng = 0 if gather is None else len(gather)

    def body(x_ref, g_ref, *rest):
        if ng:
            start, relay, finish = _gather_steps(rest[:ng], rest[ng + 1:2 * ng + 1], *rest[2 * ng + 1:])
            first, last = _grid_edges((s // TR,))
            pl.when(first)(start)
        a = x_ref[...]
        rest[ng][...] = (a * _rsq_mean(a) * g_ref[...]).astype(BF16)

        if ng:
            @pl.when(last)
            def _():
                relay()
                finish()

    res = pl.pallas_call(
        body, grid=(s // TR,), in_specs=[_row_spec(d), _vec_spec(d)] + [ANY] * ng,
        out_specs=[_row_spec(d)] + [ANY] * ng,
        out_shape=[jax.ShapeDtypeStruct((s, d), BF16)] + _gathered_shapes(gather or []),
        scratch_shapes=_gather_sems(ng) if ng else [],
        compiler_params=_cparams("arbitrary" if ng else "parallel"), name=name)(x, g, *(gather or []))
    return res[0], list(res[1:])


def _acc_init(refs):
    @pl.when(pl.program_id(0) == 0)
    def _():
        for r in refs:
            r[...] = jnp.zeros_like(r)


def _loss_norm_bwd(x1, f, g_post, target, name):
    s, d = x1.shape

    def body(x_ref, f_ref, gp_ref, t_ref, loss_ref, dx_ref, df_ref, dg_ref):
        _acc_init([loss_ref, dg_ref])
        fv = f_ref[...]
        r = _rsq_mean(fv)
        err = x_ref[...] + fv * r * gp_ref[...] - t_ref[...]
        dx = err * (1.0 / d)
        dx_ref[...] = dx
        part = 0.5 * jnp.sum(jnp.mean(err * err, axis=-1, keepdims=True), axis=0, keepdims=True)
        loss_ref[...] += jnp.broadcast_to(part, loss_ref.shape)
        da, dgt = _rms_bwd(fv, r, gp_ref[...], dx)
        df_ref[...] = da.astype(BF16)
        dg_ref[...] += _colsum(dgt)

    return pl.pallas_call(
        body, grid=(s // TR,), in_specs=[_row_spec(d), _row_spec(d), _vec_spec(d), _row_spec(d)],
        out_specs=[pl.BlockSpec((8, LANES), lambda i: (0, 0)), _row_spec(d), _row_spec(d), _vec_spec(d)],
        out_shape=[jax.ShapeDtypeStruct((8, LANES), F32), jax.ShapeDtypeStruct((s, d), F32),
                   jax.ShapeDtypeStruct((s, d), BF16), jax.ShapeDtypeStruct((1, d), F32)],
        compiler_params=_cparams("arbitrary"), name=name)(x1, f, g_post, target)


def _norm_bwd_mid(dx2, dh2, x1, g_pf, y1, g_pm, name):
    s, d = dx2.shape

    def body(dx2_ref, dh_ref, x1_ref, gpf_ref, y1_ref, gpm_ref, dx1_ref, dy1_ref, dgpf_ref, dgpm_ref):
        _acc_init([dgpf_ref, dgpm_ref])
        x1 = x1_ref[...]
        da, dgt = _rms_bwd(x1, _rsq_mean(x1), gpf_ref[...], dh_ref[...])
        dx1 = dx2_ref[...] + da
        dx1_ref[...] = dx1
        dgpf_ref[...] += _colsum(dgt)
        y1 = y1_ref[...]
        dy, dgt2 = _rms_bwd(y1, _rsq_mean(y1), gpm_ref[...], dx1)
        dy1_ref[...] = dy.astype(BF16)
        dgpm_ref[...] += _colsum(dgt2)

    return pl.pallas_call(
        body, grid=(s // TR,),
        in_specs=[_row_spec(d), _row_spec(d), _row_spec(d), _vec_spec(d), _row_spec(d), _vec_spec(d)],
        out_specs=[_row_spec(d), _row_spec(d), _vec_spec(d), _vec_spec(d)],
        out_shape=[jax.ShapeDtypeStruct((s, d), F32), jax.ShapeDtypeStruct((s, d), BF16),
                   jax.ShapeDtypeStruct((1, d), F32), jax.ShapeDtypeStruct((1, d), F32)],
        compiler_params=_cparams("arbitrary"), name=name)(dx2, dh2, x1, g_pf, y1, g_pm)


def _norm_bwd_in_out(dx1, dh1, x0, g1, f_below, g_post_below, name):
    s, d = dx1.shape

    def body(dx1_ref, dh_ref, x0_ref, g_ref, f_ref, gp_ref, dx0_ref, dg_ref, df_ref, dgp_ref):
        _acc_init([dg_ref, dgp_ref])
        x0 = x0_ref[...]
        da, dgt = _rms_bwd(x0, _rsq_mean(x0), g_ref[...], dh_ref[...])
        dx0 = dx1_ref[...] + da
        dx0_ref[...] = dx0
        dg_ref[...] += _colsum(dgt)
        fv = f_ref[...]
        db, dgt2 = _rms_bwd(fv, _rsq_mean(fv), gp_ref[...], dx0)
        df_ref[...] = db.astype(BF16)
        dgp_ref[...] += _colsum(dgt2)

    return pl.pallas_call(
        body, grid=(s // TR,),
        in_specs=[_row_spec(d), _row_spec(d), _row_spec(d), _vec_spec(d), _row_spec(d), _vec_spec(d)],
        out_specs=[_row_spec(d), _vec_spec(d), _row_spec(d), _vec_spec(d)],
        out_shape=[jax.ShapeDtypeStruct((s, d), F32), jax.ShapeDtypeStruct((1, d), F32),
                   jax.ShapeDtypeStruct((s, d), BF16), jax.ShapeDtypeStruct((1, d), F32)],
        compiler_params=_cparams("arbitrary"), name=name)(dx1, dh1, x0, g1, f_below, g_post_below)


def _norm_bwd_in(dx1, dh1, x0, g1, name):
    s, d = dx1.shape

    def body(dx1_ref, dh_ref, x0_ref, g_ref, dx0_ref, dg_ref):
        _acc_init([dg_ref])
        x0 = x0_ref[...]
        da, dgt = _rms_bwd(x0, _rsq_mean(x0), g_ref[...], dh_ref[...])
        dx0_ref[...] = dx1_ref[...] + da
        dg_ref[...] += _colsum(dgt)

    return pl.pallas_call(
        body, grid=(s // TR,), in_specs=[_row_spec(d), _row_spec(d), _row_spec(d), _vec_spec(d)],
        out_specs=[_row_spec(d), _vec_spec(d)],
        out_shape=[jax.ShapeDtypeStruct((s, d), F32), jax.ShapeDtypeStruct((1, d), F32)],
        compiler_params=_cparams("arbitrary"), name=name)(dx1, dh1, x0, g1)


def _tril_mask():
    row = lax.broadcasted_iota(jnp.int32, (CHUNK, CHUNK), 0)
    col = lax.broadcasted_iota(jnp.int32, (CHUNK, CHUNK), 1)
    return row >= col


def _gating_forward(pa, gv, bv, wt, bsf):
    er = lax.erf(pa * RSQRT2)
    za = 0.5 * pa * (1.0 + er)
    u = za[:, :A_WIDTH]
    va = za[:, A_WIDTH:]
    xc = va - jnp.mean(va, axis=-1, keepdims=True)
    rs = lax.rsqrt(jnp.mean(xc * xc, axis=-1, keepdims=True) + EPS)
    vn = xc * rs
    vlb = (vn * gv + bv).astype(BF16)
    sg = jnp.concatenate(
        [_dot(wt[g], vlb[:, g * GROUP_DIM:(g + 1) * GROUP_DIM], NN) for g in range(A_GROUPS)], axis=1) + bsf
    return er, u, rs, vn, vlb, sg


def _masked_ws(ws_ref):
    mask = _tril_mask()
    return [jnp.where(mask, ws_ref[g], 0.0).astype(BF16) for g in range(A_GROUPS)]


def _mixer_a_fwd(proj, gv, bv, ws, bsf, ga, name, gather=None):
    s = proj.shape[0]
    ng = 0 if gather is None else len(gather)

    def body(p_ref, gv_ref, bv_ref, ws_ref, bs_ref, ga_ref, *rest):
        o_ref = rest[ng]
        if ng:
            start, relay, finish = _gather_steps(rest[:ng], rest[ng + 1:2 * ng + 1], *rest[2 * ng + 1:])
            first, last = _grid_edges((s // TR,))
            pl.when(first)(start)
        wt = _masked_ws(ws_ref)
        for ch in range(TR // CHUNK):
            rows = slice(ch * CHUNK, (ch + 1) * CHUNK)
            _, u, _, _, _, sg = _gating_forward(p_ref[rows, :].astype(F32), gv_ref[...], bv_ref[...], wt, bs_ref[...])
            oa = u * sg
            o_ref[rows, :] = (oa * _rsq_mean(oa) * ga_ref[...]).astype(BF16)

        if ng:
            @pl.when(last)
            def _():
                relay()
                finish()

    res = pl.pallas_call(
        body, grid=(s // TR,),
        in_specs=[_row_spec(2 * A_WIDTH), _vec_spec(A_WIDTH), _vec_spec(A_WIDTH),
                  pl.BlockSpec((A_GROUPS, CHUNK, CHUNK), lambda i: (0, 0, 0)),
                  pl.BlockSpec((CHUNK, A_WIDTH), lambda i: (0, 0)), _vec_spec(A_WIDTH)] + [ANY] * ng,
        out_specs=[_row_spec(A_WIDTH)] + [ANY] * ng,
        out_shape=[jax.ShapeDtypeStruct((s, A_WIDTH + B_WIDTH), BF16)] + _gathered_shapes(gather or []),
        scratch_shapes=_gather_sems(ng) if ng else [],
        compiler_params=_cparams("arbitrary" if ng else "parallel"), name=name)(proj, gv, bv, ws, bsf, ga,
                                                                              *(gather or []))
    return res[0], list(res[1:])


def _mixer_a_bwd(proj, dmixed, gv, bv, ws, bsf, ga, name, scatter=None):
    s = proj.shape[0]
    nsteps = s // TR
    ns = 0 if scatter is None else len(scatter[0])

    def body(*refs):
        p_ref, dm_ref, gv_ref, bv_ref, ws_ref, bs_ref, ga_ref = refs[:7]
        dp_ref, dga_ref, dgv_ref, dbv_ref, dbs_ref, dws_ref = refs[7 + ns:13 + ns]
        if ns:
            start, finish = _scatter_steps(refs[7:7 + ns], refs[13 + ns:13 + 2 * ns], *refs[13 + 2 * ns:], scatter[1])
            first, last = _grid_edges((nsteps,))
            pl.when(first)(start)
        _acc_init([dga_ref, dgv_ref, dbv_ref, dbs_ref, dws_ref])
        mask = _tril_mask()
        wt = _masked_ws(ws_ref)
        gvv = gv_ref[...]
        gav = ga_ref[...]
        for ch in range(TR // CHUNK):
            rows = slice(ch * CHUNK, (ch + 1) * CHUNK)
            pa = p_ref[rows, :].astype(F32)
            er, u, rs, vn, vlb, sg = _gating_forward(pa, gvv, bv_ref[...], wt, bs_ref[...])
            oa = u * sg
            doa, dgt = _rms_bwd(oa, _rsq_mean(oa), gav, dm_ref[rows, :])
            dga_ref[...] += _colsum(dgt)
            du = doa * sg
            dsg = doa * u
            dbs_ref[...] += dsg
            dsgb = dsg.astype(BF16)
            dvl = []
            for g in range(A_GROUPS):
                cols = slice(g * GROUP_DIM, (g + 1) * GROUP_DIM)
                dws_ref[g] += jnp.where(mask, _dot(dsgb[:, cols], vlb[:, cols], NT), 0.0)
                dvl.append(_dot(wt[g], dsgb[:, cols], TN))
            dvl = jnp.concatenate(dvl, axis=1)
            dgv_ref[...] += _colsum(dvl * vn)
            dbv_ref[...] += _colsum(dvl)
            dvn = dvl * gvv
            dva = rs * (dvn - jnp.mean(dvn, axis=-1, keepdims=True)
                        - vn * jnp.mean(dvn * vn, axis=-1, keepdims=True))
            gp = 0.5 * (1.0 + er) + pa * jnp.exp(-0.5 * pa * pa) * INV_SQRT_2PI
            dp_ref[rows, :] = (jnp.concatenate([du, dva], axis=1) * gp).astype(BF16)

        @pl.when(pl.program_id(0) == nsteps - 1)
        def _():
            for g in range(A_GROUPS):
                cols = slice(g * GROUP_DIM, (g + 1) * GROUP_DIM)
                tot = jnp.sum(dbs_ref[:, cols], axis=1, keepdims=True)
                dbs_ref[:, cols] = jnp.broadcast_to(tot, (CHUNK, GROUP_DIM))

        if ns:
            pl.when(last)(finish)

    full = lambda *shape: pl.BlockSpec(shape, lambda i: (0,) * len(shape))
    res = pl.pallas_call(
        body, grid=(nsteps,),
        in_specs=[_row_spec(2 * A_WIDTH), _row_spec(A_WIDTH), _vec_spec(A_WIDTH), _vec_spec(A_WIDTH),
                  full(A_GROUPS, CHUNK, CHUNK), full(CHUNK, A_WIDTH), _vec_spec(A_WIDTH)] + [ANY] * ns,
        out_specs=[_row_spec(2 * A_WIDTH), _vec_spec(A_WIDTH), _vec_spec(A_WIDTH), _vec_spec(A_WIDTH),
                   full(CHUNK, A_WIDTH), full(A_GROUPS, CHUNK, CHUNK)] + [ANY] * ns,
        out_shape=[jax.ShapeDtypeStruct((s, IN_COLS), BF16), jax.ShapeDtypeStruct((1, A_WIDTH), F32),
                   jax.ShapeDtypeStruct((1, A_WIDTH), F32), jax.ShapeDtypeStruct((1, A_WIDTH), F32),
                   jax.ShapeDtypeStruct((CHUNK, A_WIDTH), F32),
                   jax.ShapeDtypeStruct((A_GROUPS, CHUNK, CHUNK), F32)]
        + (_scattered_shapes(scatter[1]) if ns else []),
        scratch_shapes=_scatter_sems(ns) if ns else [],
        compiler_params=_cparams("arbitrary"), name=name)(proj, dmixed, gv, bv, ws, bsf, ga,
                                                          *(scatter[0] if ns else []))
    return res[:6] + (list(res[6:]),)


def _rope_tables(s):
    half = ROT_DIM // 2
    lane = jnp.arange(LANES) % HEAD_DIM
    inv = ROPE_THETA ** (-(2 * (lane % half)).astype(F32) / ROT_DIM)
    ang = jnp.arange(s, dtype=F32)[:, None] * inv[None, :]
    cos, sin = jnp.cos(ang), jnp.sin(ang)
    c = jnp.where(lane < ROT_DIM, cos, 1.0)
    s1 = jnp.where(lane < half, -sin, 0.0)
    s2 = jnp.where((lane >= half) & (lane < ROT_DIM), sin, 0.0)
    return c, s1, s2


def _lane_blocks(width):
    return [slice(b * LANES, (b + 1) * LANES) for b in range(width // LANES)]


CLASS_DILS = tuple(d for d in DILATIONS if d > 1)


def _class_shape(s, dil, dtype):
    return jax.ShapeDtypeStruct((dil, s // dil, B_WIDTH), dtype)


def _class_spec(dil):
    return pl.BlockSpec((dil, TR // dil, B_WIDTH), lambda i, *_: (0, i, 0))


NBLK = B_WIDTH // LANES
STAGE = pltpu.VMEM((NBLK, TR, LANES), F32)


def _stage_put(stage, value):
    for b, sl in enumerate(_lane_blocks(B_WIDTH)):
        stage[b] = value[:, sl]


def _stage_get(stage):
    return jnp.concatenate([stage[b] for b in range(NBLK)], axis=1)


def _store_classes(stage, dst_ref, dil):
    for b, sl in enumerate(_lane_blocks(B_WIDTH)):
        for r in range(dil):
            dst_ref[r, :, sl] = stage[b, pl.ds(r, TR // dil, stride=dil), :].astype(dst_ref.dtype)


def _load_classes(src_ref, stage, dil):
    for b, sl in enumerate(_lane_blocks(B_WIDTH)):
        for r in range(dil):
            stage[b, pl.ds(r, TR // dil, stride=dil), :] = src_ref[r, :, sl].astype(F32)
    return _stage_get(stage)


def _rope_fwd(proj, tabs, name, gather=None):
    s = proj.shape[0]
    half = ROT_DIM // 2
    scale = HEAD_DIM ** -0.5
    nlay = 1 + len(CLASS_DILS)
    ng = 0 if gather is None else len(gather)

    def body(q_ref, k_ref, v_ref, c_ref, s1_ref, s2_ref, *rest):
        outs, stage = rest[ng:ng + 3 * nlay], rest[2 * ng + 3 * nlay]
        if ng:
            start, relay, finish = _gather_steps(rest[:ng], rest[ng + 3 * nlay:2 * ng + 3 * nlay],
                                                 *rest[2 * ng + 3 * nlay + 1:])
            first, last = _grid_edges((s // TR,))
            pl.when(first)(start)
        c, s1, s2 = c_ref[...], s1_ref[...], s2_ref[...]
        for which, (src, mul) in enumerate(((q_ref, scale), (k_ref, 1.0), (v_ref, None))):
            if mul is None:
                _stage_put(stage, src[...].astype(F32))
            else:
                for b, sl in enumerate(_lane_blocks(B_WIDTH)):
                    a = src[:, sl].astype(F32)
                    r = a * c + pltpu.roll(a, LANES - half, 1) * s1 + pltpu.roll(a, half, 1) * s2
                    stage[b] = r * mul
            dst = outs[which * nlay:(which + 1) * nlay]
            dst[0][...] = _stage_get(stage).astype(BF16)
            for ref, d in zip(dst[1:], CLASS_DILS):
                _store_classes(stage, ref, d)

        if ng:
            @pl.when(last)
            def _():
                relay()
                finish()

    tab = pl.BlockSpec((TR, LANES), lambda i: (i, 0))
    lay_specs = [_row_spec(B_WIDTH)] + [_class_spec(d) for d in CLASS_DILS]
    lay_shapes = [jax.ShapeDtypeStruct((s, B_WIDTH), BF16)] + [_class_shape(s, d, BF16) for d in CLASS_DILS]
    outs = pl.pallas_call(
        body, grid=(s // TR,),
        in_specs=[_row_spec(B_WIDTH, 2), _row_spec(B_WIDTH, 3), _row_spec(B_WIDTH, 4), tab, tab, tab] + [ANY] * ng,
        out_specs=lay_specs * 3 + [ANY] * ng, out_shape=lay_shapes * 3 + _gathered_shapes(gather or []),
        scratch_shapes=[STAGE] + (_gather_sems(ng) if ng else []),
        compiler_params=_cparams("arbitrary" if ng else "parallel"), name=name)(proj, proj, proj, *tabs,
                                                                              *(gather or []))
    q, k, v = (dict(zip(DILATIONS, outs[w * nlay:(w + 1) * nlay])) for w in range(3))
    return q, k, v, list(outs[3 * nlay:])


def _as_classes(t):
    return t if t.ndim == 3 else t[None]


def _head_masks():
    lane = lax.broadcasted_iota(jnp.int32, (1, LANES), 1)
    return lane < HEAD_DIM, lane >= HEAD_DIM


def _stack_heads(t):
    lo, hi = _head_masks()
    zero = jnp.zeros_like(t)
    return jnp.concatenate([jnp.where(lo, t, zero), jnp.where(hi, t, zero)], axis=0)


MAX_SEGMENT_BLOCKS = 8


def _segment_masks(j):
    qi = lax.broadcasted_iota(jnp.int32, (BAND, 2 * BAND), 0)
    kj = lax.broadcasted_iota(jnp.int32, (BAND, 2 * BAND), 1)
    both = (kj >= qi) & (kj <= qi + BAND)
    own = kj[:, :BAND] <= qi[:, :BAND]
    head = both & ((kj >= BAND) | (j > 0))
    return tuple(jnp.concatenate([m, m], axis=0) for m in (own, both, head))


def _block_rows(g):
    return pl.ds(pl.multiple_of(g * BAND, BAND), BAND)


def _key_rows(g):
    return pl.ds(pl.multiple_of((g - 1) * BAND, BAND), 2 * BAND)


def _segments(n):
    nb = n // BAND
    seg = min(nb, MAX_SEGMENT_BLOCKS)
    return seg, nb // seg


def _segment_specs(seg):
    main = pl.BlockSpec((None, seg * BAND, B_WIDTH), lambda r, j: (r, j, 0))
    halo = pl.BlockSpec((None, BAND, B_WIDTH), lambda r, j: (r, jnp.maximum(j * seg - 1, 0), 0))
    return main, halo


def _attn_fwd(q, k, v, name, gather=None):
    dil, n, _ = q.shape
    seg, nseg = _segments(n)
    nh = 2 if nseg > 1 else 0
    ng = 0 if gather is None else len(gather)

    def body(*refs):
        q_ref, k_ref, v_ref = refs[:3]
        halos = refs[3:3 + nh]
        o_ref, l_ref = refs[3 + nh + ng:5 + nh + ng]
        if ng:
            start, relay, finish = _gather_steps(refs[3 + nh:3 + nh + ng], refs[5 + nh + ng:5 + nh + 2 * ng],
                                                 *refs[5 + nh + 2 * ng:])
            first, last = _grid_edges((dil, nseg))
            pl.when(first)(start)
        own, both, head = _segment_masks(pl.program_id(1))
        lo, _ = _head_masks()

        def block(rows, keys_of, valid):
            for sl in _lane_blocks(B_WIDTH):
                kk, vv = keys_of(sl)
                sc = jnp.where(valid, _dot(_stack_heads(q_ref[rows, sl]), kk, NT), NEG_INF)
                mx = jnp.max(sc, axis=1, keepdims=True)
                p = jnp.exp(sc - mx)
                den = jnp.sum(p, axis=1, keepdims=True)
                out = _dot(p.astype(BF16), vv, NN) / den
                lse = mx + jnp.log(den)
                o_ref[rows, sl] = jnp.where(lo, out[:BAND], out[BAND:]).astype(BF16)
                l_ref[rows, sl] = jnp.where(lo, lse[:BAND], lse[BAND:])

        if nh:
            block(_block_rows(0), lambda sl: (jnp.concatenate([halos[0][:, sl], k_ref[0:BAND, sl]], axis=0),
                                              jnp.concatenate([halos[1][:, sl], v_ref[0:BAND, sl]], axis=0)), head)
        else:
            block(_block_rows(0), lambda sl: (k_ref[0:BAND, sl], v_ref[0:BAND, sl]), own)

        @pl.loop(1, seg)
        def _(g):
            block(_block_rows(g), lambda sl: (k_ref[_key_rows(g), sl], v_ref[_key_rows(g), sl]), both)

        if ng:
            @pl.when(last)
            def _():
                relay()
                finish()

    main, halo = _segment_specs(seg)
    res = pl.pallas_call(
        body, grid=(dil, nseg), in_specs=[main] * 3 + [halo] * nh + [ANY] * ng, out_specs=[main, main] + [ANY] * ng,
        out_shape=[jax.ShapeDtypeStruct((dil, n, B_WIDTH), BF16), jax.ShapeDtypeStruct((dil, n, B_WIDTH), F32)]
        + _gathered_shapes(gather or []),
        scratch_shapes=_gather_sems(ng) if ng else [],
        compiler_params=_cparams(*(["arbitrary"] * 2 if ng else ["parallel"] * 2)), name=name)(
            q, k, v, *([k, v] if nh else []), *(gather or []))
    return res[0], res[1], list(res[2:])


def _attn_bwd(q, k, v, do, lse, delta, name, scatter=None):
    dil, n, _ = q.shape
    seg, nseg = _segments(n)
    nh = 2 if nseg > 1 else 0
    ns = 0 if scatter is None else len(scatter[0])

    def body(*refs):
        q_ref, k_ref, v_ref, do_ref, lse_ref, dl_ref = refs[:6]
        halos = refs[6:6 + nh]
        dq_ref, dk_ref, dv_ref = refs[6 + nh + ns:9 + nh + ns]
        halo_out = refs[9 + nh + ns:9 + 2 * nh + ns]
        ck_ref, cv_ref = refs[9 + 2 * nh + 2 * ns:11 + 2 * nh + 2 * ns]
        if ns:
            start, finish = _scatter_steps(refs[6 + nh:6 + nh + ns], refs[9 + 2 * nh + ns:9 + 2 * nh + 2 * ns],
                                           *refs[11 + 2 * nh + 2 * ns:], scatter[1])
            first, last = _grid_edges((dil, nseg))
            pl.when(first)(start)
        own, both, head = _segment_masks(pl.program_id(1))
        lo, _ = _head_masks()
        lane = lax.broadcasted_iota(jnp.int32, (1, LANES), 1)

        def per_head(t):
            return jnp.concatenate(
                [jnp.sum(jnp.where(lane == first, t, 0.0), axis=1, keepdims=True) for first in (0, HEAD_DIM)], axis=0)

        def grads(rows, kk, vv, valid, sl):
            q2 = _stack_heads(q_ref[rows, sl])
            do2 = _stack_heads(do_ref[rows, sl])
            p = jnp.where(valid, jnp.exp(_dot(q2, kk, NT) - per_head(lse_ref[rows, sl])), 0.0)
            ds = (p * (_dot(do2, vv, NT) - per_head(dl_ref[rows, sl]))).astype(BF16)
            dq = _dot(ds, kk, NN)
            dq_ref[rows, sl] = jnp.where(lo, dq[:BAND], dq[BAND:]).astype(BF16)
            return _dot(ds, q2, TN), _dot(p.astype(BF16), do2, TN)

        for sl in _lane_blocks(B_WIDTH):
            if nh:
                dkk, dvv = grads(_block_rows(0), jnp.concatenate([halos[0][:, sl], k_ref[0:BAND, sl]], axis=0),
                                 jnp.concatenate([halos[1][:, sl], v_ref[0:BAND, sl]], axis=0), head, sl)
                halo_out[0][:, sl], halo_out[1][:, sl] = dkk[:BAND], dvv[:BAND]
                ck_ref[:, sl], cv_ref[:, sl] = dkk[BAND:], dvv[BAND:]
            else:
                ck_ref[:, sl], cv_ref[:, sl] = grads(_block_rows(0), k_ref[0:BAND, sl], v_ref[0:BAND, sl], own, sl)

        @pl.loop(1, seg)
        def _(g):
            before = _block_rows(g - 1)
            for sl in _lane_blocks(B_WIDTH):
                dkk, dvv = grads(_block_rows(g), k_ref[_key_rows(g), sl], v_ref[_key_rows(g), sl], both, sl)
                dk_ref[before, sl] = (ck_ref[:, sl] + dkk[:BAND]).astype(BF16)
                dv_ref[before, sl] = (cv_ref[:, sl] + dvv[:BAND]).astype(BF16)
                ck_ref[:, sl] = dkk[BAND:]
                cv_ref[:, sl] = dvv[BAND:]

        final = pl.ds((seg - 1) * BAND, BAND)
        dk_ref[final, :] = ck_ref[...].astype(BF16)
        dv_ref[final, :] = cv_ref[...].astype(BF16)

        if ns:
            pl.when(last)(finish)

    main, halo = _segment_specs(seg)
    shape = jax.ShapeDtypeStruct((dil, n, B_WIDTH), BF16)
    halo_shape = jax.ShapeDtypeStruct((dil, nseg, BAND, B_WIDTH), F32)
    halo_spec = pl.BlockSpec((None, None, BAND, B_WIDTH), lambda r, j: (r, j, 0, 0))
    res = pl.pallas_call(
        body, grid=(dil, nseg), in_specs=[main] * 6 + [halo] * nh + [ANY] * ns,
        out_specs=[main] * 3 + [halo_spec] * nh + [ANY] * ns,
        out_shape=[shape] * 3 + [halo_shape] * nh + (_scattered_shapes(scatter[1]) if ns else []),
        scratch_shapes=[pltpu.VMEM((BAND, B_WIDTH), F32)] * 2 + (_scatter_sems(ns) if ns else []),
        compiler_params=_cparams(*(["arbitrary"] * 2 if ns else ["parallel"] * 2)), name=name)(
            q, k, v, do, lse, delta, *([k, v] if nh else []), *(scatter[0] if ns else []))
    return res[0], res[1], res[2], (tuple(res[3:3 + nh]) if nh else None), list(res[3 + nh:])


def _attn_combine(outs, lses, gb, mixed, name, gather=None):
    s = mixed.shape[0]
    npat = len(DILATIONS)
    w = B_WIDTH
    ng = 0 if gather is None else len(gather)

    def body(*refs):
        o_refs, l_refs = refs[:npat], refs[npat:2 * npat]
        g_ref = refs[2 * npat]
        ob_ref = refs[2 * npat + 2 + ng]
        lse_refs = refs[2 * npat + 3 + ng:3 * npat + 3 + ng]
        mb_ref = refs[3 * npat + 3 + ng]
        stage = refs[3 * npat + 4 + 2 * ng]
        if ng:
            start, relay, finish = _gather_steps(refs[2 * npat + 2:2 * npat + 2 + ng],
                                                 refs[3 * npat + 4 + ng:3 * npat + 4 + 2 * ng],
                                                 *refs[3 * npat + 5 + 2 * ng:])
            first, last = _grid_edges((s // TR,))
            pl.when(first)(start)
        os_ = [o_refs[0][...].astype(F32)] + [_load_classes(r, stage, d) for r, d in zip(o_refs[1:], CLASS_DILS)]
        ls = [l_refs[0][...]] + [_load_classes(r, stage, d) for r, d in zip(l_refs[1:], CLASS_DILS)]
        mx = functools.reduce(jnp.maximum, ls)
        ws = [jnp.exp(l - mx) for l in ls]
        tot = functools.reduce(lambda a, b: a + b, ws)
        ob = functools.reduce(lambda a, b: a + b, [wt / tot * o for wt, o in zip(ws, os_)])
        ob_ref[...] = ob
        lse = mx + jnp.log(tot)
        _stage_put(stage, lse)
        lse_refs[0][...] = lse
        for ref, d in zip(lse_refs[1:], CLASS_DILS):
            _store_classes(stage, ref, d)
        mb_ref[...] = (ob * _rsq_mean(ob) * g_ref[...]).astype(BF16)

        if ng:
            @pl.when(last)
            def _():
                relay()
                finish()

    lay_specs = [_row_spec(w)] + [_class_spec(d) for d in CLASS_DILS]
    res = pl.pallas_call(
        body, grid=(s // TR,), in_specs=lay_specs * 2 + [_vec_spec(w), ANY] + [ANY] * ng,
        out_specs=[_row_spec(w)] + lay_specs + [_row_spec(w, 1)] + [ANY] * ng,
        out_shape=[jax.ShapeDtypeStruct((s, w), F32), jax.ShapeDtypeStruct((s, w), F32)]
        + [_class_shape(s, d, F32) for d in CLASS_DILS] + [jax.ShapeDtypeStruct(mixed.shape, mixed.dtype)]
        + _gathered_shapes(gather or []),
        scratch_shapes=[STAGE] + (_gather_sems(ng) if ng else []), input_output_aliases={2 * npat + 1: npat + 1},
        compiler_params=_cparams("arbitrary" if ng else "parallel"), name=name)(*outs, *lses, gb, mixed,
                                                                              *(gather or []))
    return res[0], dict(zip(DILATIONS, res[1:npat + 1])), res[npat + 1], list(res[npat + 2:])


def _attn_bwd_prep(dmixed, ob, gb, name):
    s = ob.shape[0]
    w = B_WIDTH
    nlay = len(DILATIONS)

    def body(dm_ref, ob_ref, g_ref, *rest):
        do_refs, dl_refs = rest[:nlay], rest[nlay:2 * nlay]
        dg_ref, stage = rest[2 * nlay:]
        _acc_init([dg_ref])
        ob = ob_ref[...]
        dob, dgt = _rms_bwd(ob, _rsq_mean(ob), g_ref[...], dm_ref[...])
        dg_ref[...] += _colsum(dgt)
        _stage_put(stage, dob)
        do_refs[0][...] = dob.astype(BF16)
        for ref, d in zip(do_refs[1:], CLASS_DILS):
            _store_classes(stage, ref, d)
        lo, hi = _head_masks()
        t = dob * ob
        for b, sl in enumerate(_lane_blocks(w)):
            tb = t[:, sl]
            s0 = jnp.sum(jnp.where(lo, tb, 0.0), axis=1, keepdims=True)
            s1 = jnp.sum(jnp.where(hi, tb, 0.0), axis=1, keepdims=True)
            stage[b] = jnp.where(lo, s0, s1)
        dl_refs[0][...] = _stage_get(stage)
        for ref, d in zip(dl_refs[1:], CLASS_DILS):
            _store_classes(stage, ref, d)

    lay_specs = [_row_spec(w)] + [_class_spec(d) for d in CLASS_DILS]
    shapes = lambda dt: [jax.ShapeDtypeStruct((s, w), dt)] + [_class_shape(s, d, dt) for d in CLASS_DILS]
    res = pl.pallas_call(
        body, grid=(s // TR,), in_specs=[_row_spec(w, 1), _row_spec(w), _vec_spec(w)],
        out_specs=lay_specs * 2 + [_vec_spec(w)],
        out_shape=shapes(BF16) + shapes(F32) + [jax.ShapeDtypeStruct((1, w), F32)],
        scratch_shapes=[STAGE],
        compiler_params=_cparams("arbitrary"), name=name)(dmixed, ob, gb)
    return dict(zip(DILATIONS, res[:nlay])), dict(zip(DILATIONS, res[nlay:2 * nlay])), res[2 * nlay]


def _rope_bwd(dqs, dks, dvs, halos, tabs, dproj, name):
    s = dproj.shape[0]
    half = ROT_DIM // 2
    scale = HEAD_DIM ** -0.5
    npat = len(DILATIONS)
    w = B_WIDTH
    nseg = halos[0].shape[0]
    per = s // nseg // TR

    def body(*refs):
        groups = [refs[g * npat:(g + 1) * npat] for g in range(3)]
        halo_refs = (None,) + tuple(refs[3 * npat:3 * npat + 2])
        c_ref, s1_ref, s2_ref, _, o_ref, stage = refs[3 * npat + 2:]
        i = pl.program_id(0)
        at_edge = ((i + 1) % per == 0) & ((i + 1) // per < nseg)

        def total(rs, halo_ref=None):
            acc = rs[0][...].astype(F32)
            if halo_ref is not None:
                edge = jnp.concatenate([jnp.zeros((TR - BAND, w), F32), halo_ref[...]], axis=0)
                acc = acc + jnp.where(at_edge, edge, 0.0)
            for ref, d in zip(rs[1:], CLASS_DILS):
                acc = acc + _load_classes(ref, stage, d)
            return acc

        def unrope(g):
            c, s1, s2 = c_ref[...], s1_ref[...], s2_ref[...]
            for sl in _lane_blocks(w):
                gb = g[:, sl]
                o = gb * c + pltpu.roll(gb * s1, half, 1) + pltpu.roll(gb * s2, LANES - half, 1)
                o_ref[:, sl] = o.astype(BF16)

        which = pl.program_id(1)

        @pl.when(which == 0)
        def _():
            unrope(total(groups[0]) * scale)

        @pl.when(which == 1)
        def _():
            unrope(total(groups[1], halo_refs[1]))

        @pl.when(which == 2)
        def _():
            o_ref[...] = total(groups[2], halo_refs[2]).astype(BF16)

    tab = pl.BlockSpec((TR, LANES), lambda i, j: (i, 0))
    nat = pl.BlockSpec((TR, w), lambda i, j: (i, 0))
    lay_specs = [nat] + [_class_spec(d) for d in CLASS_DILS]
    edge_spec = pl.BlockSpec((None, BAND, w), lambda i, j: (jnp.minimum((i + 1) // per, nseg - 1), 0, 0))
    first_col = 2 * A_WIDTH // w
    return pl.pallas_call(
        body, grid=(s // TR, 3), in_specs=lay_specs * 3 + [edge_spec] * 2 + [tab] * 3 + [ANY],
        out_specs=pl.BlockSpec((TR, w), lambda i, j: (i, first_col + j)),
        out_shape=jax.ShapeDtypeStruct(dproj.shape, dproj.dtype), scratch_shapes=[STAGE],
        input_output_aliases={3 * npat + 5: 0},
        compiler_params=_cparams("parallel", "arbitrary"), name=name)(*dqs, *dks, *dvs, *halos, *tabs, dproj)


TK = 512
HALO = 16
FFN_ROWS = 256
FFN_CHUNKS = tuple(slice(r, r + FFN_ROWS) for r in range(0, TM, FFN_ROWS))


def _row_of(v, r):
    rows = lax.broadcasted_iota(jnp.int32, (v.shape[0], 1), 0)
    return jnp.sum(jnp.where(rows == r, v, 0.0), axis=0, keepdims=True)


def _taps_before(x, halo):
    row = lax.broadcasted_iota(jnp.int32, (x.shape[0], 1), 0)
    m1 = jnp.where(row == 0, _row_of(halo, HALO - 1), pltpu.roll(x, 1, 0))
    m2 = jnp.where(row == 0, _row_of(halo, HALO - 2), jnp.where(row == 1, _row_of(halo, HALO - 1), pltpu.roll(x, 2, 0)))
    return m2, m1, x


def _taps_after(x, halo):
    rows = x.shape[0]
    row = lax.broadcasted_iota(jnp.int32, (rows, 1), 0)
    p1 = jnp.where(row == rows - 1, _row_of(halo, 0), pltpu.roll(x, rows - 1, 0))
    p2 = jnp.where(row == rows - 2, _row_of(halo, 0), jnp.where(row == rows - 1, _row_of(halo, 1), pltpu.roll(x, rows - 2, 0)))
    return p1, p2


def _conv_value(taps, cw_ref, cb_ref, h):
    return cb_ref[h] + cw_ref[h, 0:1, :] * taps[0] + cw_ref[h, 1:2, :] * taps[1] + cw_ref[h, 2:3, :] * taps[2]


def _ffn_weight_specs(ncol):
    per_up = (2 * D_FF // N_CHIPS) // TK
    per_dn = (D_FF // N_CHIPS) // TK
    wg = pl.BlockSpec((None, None, D_MODEL, TK), lambda i, j: (j // per_up, 0, 0, j % per_up))
    wv = pl.BlockSpec((None, None, D_MODEL, TK), lambda i, j: ((j + ncol) // per_up, 0, 0, (j + ncol) % per_up))
    wd = pl.BlockSpec((None, None, TK, D_MODEL), lambda i, j: (j // per_dn, 0, j % per_dn, 0))
    cw = pl.BlockSpec((2, 3, TK), lambda i, j: (0, 0, j))
    cb = pl.BlockSpec((2, 1, TK), lambda i, j: (0, 0, j))
    return wg, wv, wd, cw, cb


def _ffn_forward(h2, w_up, w_down, cw3, cb3, name, gather=None, post=None):
    s = h2.shape[0]
    nm, ncol = s // TM, D_FF // TK
    ng = 0 if gather is None else len(gather)
    npost = 0 if post is None else 3
    nout = 4 + (2 if post else 0)

    def body(*refs):
        h_ref, wg_ref, wv_ref, wd_ref, cw_ref, cb_ref = refs[:6]
        post_in = refs[6:6 + npost]
        g_in = refs[6 + npost:6 + npost + ng]
        outs = refs[6 + npost + ng:6 + npost + ng + nout]
        y_ref, up_ref, cv_ref, f_ref = outs[:4]
        g_out = refs[6 + npost + ng + nout:6 + npost + 2 * ng + nout]
        carry = refs[6 + npost + 2 * ng + nout]
        i, j = pl.program_id(0), pl.program_id(1)
        if ng:
            start, relay, finish = _gather_steps(g_in, g_out, *refs[7 + npost + 2 * ng + nout:])
            pl.when((i == 0) & (j == 0))(start)
            pl.when((i == nm - 1) & (j == 0))(relay)

        @pl.when((i == 0) & (j == 0))
        def _():
            carry[...] = jnp.zeros_like(carry)

        @pl.when(j == 0)
        def _():
            f_ref[...] = jnp.zeros_like(f_ref)

        ups = []
        for rs in FFN_CHUNKS:
            hc = h_ref[rs, :]
            ups.append([_dot(hc, w_ref[...], NN).astype(BF16) for w_ref in (wg_ref, wv_ref)])
            for hh in range(2):
                up_ref[hh, rs, :] = ups[-1][hh]
        before = [carry[j, hh] for hh in range(2)]
        for rs, up in zip(FFN_CHUNKS, ups):
            conv = []
            for hh in range(2):
                x = up[hh].astype(F32)
                conv.append(_conv_value(_taps_before(x, before[hh]), cw_ref, cb_ref, hh))
                cv_ref[hh, rs, :] = conv[hh].astype(BF16)
                before[hh] = x[x.shape[0] - HALO:, :]
            y = (_gelu_tanh(conv[0])[0] * conv[1]).astype(BF16)
            y_ref[rs, :] = y
            f_ref[rs, :] += _dot(y, wd_ref[...], NN)
        for hh in range(2):
            carry[j, hh] = before[hh]

        @pl.when(j == ncol - 1)
        def _():
            if post:
                f = f_ref[...]
                x1_ref, gp_ref, gn_ref = post_in
                x2 = x1_ref[...] + f * _rsq_mean(f) * gp_ref[...]
                outs[4][...] = x2
                outs[5][...] = (x2 * _rsq_mean(x2) * gn_ref[...]).astype(BF16)

        if ng:
            pl.when((i == nm - 1) & (j == ncol - 1))(finish)

    wg, wv, wd, cw, cb = _ffn_weight_specs(ncol)
    row = pl.BlockSpec((TM, D_MODEL), lambda i, j: (i, 0))
    vec = pl.BlockSpec((1, D_MODEL), lambda i, j: (0, 0))
    res = pl.pallas_call(
        body, grid=(nm, ncol),
        in_specs=[row, wg, wv, wd, cw, cb] + ([row, vec, vec] if post else []) + [ANY] * ng,
        out_specs=[pl.BlockSpec((TM, TK), lambda i, j: (i, j)), pl.BlockSpec((2, TM, TK), lambda i, j: (0, i, j)),
                   pl.BlockSpec((2, TM, TK), lambda i, j: (0, i, j)), row] + ([row, row] if post else [])
        + [ANY] * ng,
        out_shape=[jax.ShapeDtypeStruct((s, D_FF), BF16), jax.ShapeDtypeStruct((2, s, D_FF), BF16),
                   jax.ShapeDtypeStruct((2, s, D_FF), BF16), jax.ShapeDtypeStruct((s, D_MODEL), F32)]
        + ([jax.ShapeDtypeStruct((s, D_MODEL), F32), jax.ShapeDtypeStruct((s, D_MODEL), BF16)] if post else [])
        + _gathered_shapes(gather or []),
        scratch_shapes=[pltpu.VMEM((ncol, 2, HALO, TK), F32)] + (_gather_sems(ng) if ng else []),
        compiler_params=_cparams("arbitrary", "arbitrary"), name=name)(h2, w_up, w_up, w_down, cw3, cb3,
                                                                      *(post or []), *(gather or []))
    return res[:nout], list(res[nout:])


def _ffn_backward(df, w_up, w_down, up3, cv3, cw3, name, scatter=None):
    s = df.shape[0]
    nm, ncol = s // TM, D_FF // TK
    ns = 0 if scatter is None else len(scatter[0])

    def body(*refs):
        df_ref, wg_ref, wv_ref, wd_ref, cw_ref, up_ref, cv_ref = refs[:7]
        s_in = refs[7:7 + ns]
        dup_ref, dh_ref, sums_ref = refs[7 + ns:10 + ns]
        s_out = refs[10 + ns:10 + 2 * ns]
        carry = refs[10 + 2 * ns]
        i, j = pl.program_id(0), pl.program_id(1)
        if ns:
            start, finish = _scatter_steps(s_in, s_out, *refs[11 + 2 * ns:], scatter[1])
            pl.when((i == 0) & (j == 0))(start)

        @pl.when((i == 0) & (j == 0))
        def _():
            carry[...] = jnp.zeros_like(carry)
            sums_ref[...] = jnp.zeros_like(sums_ref)

        @pl.when(j == 0)
        def _():
            dh_ref[...] = jnp.zeros_like(dh_ref)

        chunks = FFN_CHUNKS[::-1]
        dys = [_dot(df_ref[rs, :], wd_ref[...], NT) for rs in chunks]
        row = lax.broadcasted_iota(jnp.int32, (8, 1), 0)
        after = [carry[j, hh] for hh in range(2)]
        upd = [jnp.zeros((8, TK), F32) for _ in range(2)]
        for rs, dy in zip(chunks, dys):
            act, grad = _gelu_tanh(cv_ref[0, rs, :].astype(F32))
            dcs = (dy * cv_ref[1, rs, :].astype(F32) * grad, dy * act)
            part = dh_ref[rs, :]
            for hh, w_ref in ((0, wg_ref), (1, wv_ref)):
                dc = dcs[hh]
                x = up_ref[hh, rs, :].astype(F32)
                after1, after2 = _taps_after(dc, after[hh])
                for ridx, sm in enumerate((_colsum(after2 * x), _colsum(after1 * x), _colsum(dc * x), _colsum(dc))):
                    upd[hh] = upd[hh] + jnp.where(row == ridx, sm, 0.0)
                dup = (cw_ref[hh, 2:3, :] * dc + cw_ref[hh, 1:2, :] * after1 + cw_ref[hh, 0:1, :] * after2).astype(BF16)
                after[hh] = dc[:HALO, :]
                dup_ref[hh, rs, :] = dup
                part = part + _dot(dup, w_ref[...], NT)
            dh_ref[rs, :] = part
        for hh in range(2):
            sums_ref[j, hh] += upd[hh]
            carry[j, hh] = after[hh]

        if ns:
            pl.when((i == nm - 1) & (j == ncol - 1))(finish)

    wg, wv, wd, cw, _ = _ffn_weight_specs(ncol)
    rev = lambda i: nm - 1 - i
    res = pl.pallas_call(
        body, grid=(nm, ncol),
        in_specs=[pl.BlockSpec((TM, D_MODEL), lambda i, j: (rev(i), 0)), wg, wv, wd, cw,
                  pl.BlockSpec((2, TM, TK), lambda i, j: (0, rev(i), j)),
                  pl.BlockSpec((2, TM, TK), lambda i, j: (0, rev(i), j))] + [ANY] * ns,
        out_specs=[pl.BlockSpec((2, TM, TK), lambda i, j: (0, rev(i), j)),
                   pl.BlockSpec((TM, D_MODEL), lambda i, j: (rev(i), 0)),
                   pl.BlockSpec((ncol, 2, 8, TK), lambda i, j: (0, 0, 0, 0))] + [ANY] * ns,
        out_shape=[jax.ShapeDtypeStruct((2, s, D_FF), BF16), jax.ShapeDtypeStruct((s, D_MODEL), F32),
                   jax.ShapeDtypeStruct((ncol, 2, 8, TK), F32)] + (_scattered_shapes(scatter[1]) if ns else []),
        scratch_shapes=[pltpu.VMEM((ncol, 2, HALO, TK), F32)] + (_scatter_sems(ns) if ns else []),
        compiler_params=_cparams("arbitrary", "arbitrary"), name=name)(df, w_up, w_up, w_down, cw3, up3, cv3,
                                                                      *(scatter[0] if ns else []))
    return res[:3], list(res[3:])


def _wspec(rows, cols, index_map):
    return pl.BlockSpec((None, None, rows, cols), index_map)


def _layer_forward(l, x0, h1, p, wg, tabs, gather=None, late=None, g_next=None):
    s = x0.shape[0]
    nm = s // TMM
    tag = f"_l{l}"
    riders = dict.fromkeys(DILATIONS)
    proj_rider = rope_rider = combine_rider = None
    if late is not None:
        cols = lambda t, parts: [t[:, i * t.shape[1] // parts:(i + 1) * t.shape[1] // parts] for i in range(parts)]
        (down_a, down_b), up_q = cols(late["w_down"], 2), cols(late["w_up"], 4)
        proj_rider, rope_rider, combine_rider = [late["w_out"], down_a], [up_q[2]], [up_q[3]]
        riders = dict(zip(DILATIONS, ([down_b], [up_q[0]], [up_q[1]])))
    proj = _matmul(
        h1, wg["w_in"], grid=(nm, N_CHIPS), a_spec=pl.BlockSpec((TMM, D_MODEL), lambda i, j: (i, 0)),
        b_spec=_wspec(D_MODEL, IN_COLS // N_CHIPS, lambda i, j: (j, 0, 0, 0)),
        o_spec=pl.BlockSpec((TMM, IN_COLS // N_CHIPS), lambda i, j: (i, j)), o_shape=(s, IN_COLS), o_dtype=BF16,
        dims=NN, nk=1, kaxis=None, acc_shape=None, name="proj" + tag, gather=proj_rider)
    if late is not None:
        proj, (w_out_all4, down_a) = proj
    ma, next_out = _mixer_a_fwd(proj, p["v_norm_g"], p["v_norm_b"], p["w_spatial"], p["bs_full"], p["out_norm_a"],
                                "mixer_a_fwd" + tag, [gather["w_out"]] if gather else None)
    q, k, v, rope_landed = _rope_fwd(proj, tabs, "rope_fwd" + tag, rope_rider)
    outs, lses, landed = zip(*[
        _attn_fwd(_as_classes(q[d]), _as_classes(k[d]), _as_classes(v[d]), f"attn_fwd_d{d}" + tag, riders[d])
        for d in DILATIONS])
    outs = [o.reshape(s, B_WIDTH) if d == 1 else o for o, d in zip(outs, DILATIONS)]
    lses = [t.reshape(s, B_WIDTH) if d == 1 else t for t, d in zip(lses, DILATIONS)]
    ob, lse, mixed, combine_landed = _attn_combine(outs, lses, p["out_norm_b"], ma, "attn_combine" + tag,
                                                   combine_rider)
    if late is not None:
        wg = dict(wg, w_out=w_out_all4, w_down=jnp.concatenate([down_a, landed[0][0]], axis=-1),
                  w_up=jnp.concatenate([landed[1][0], landed[2][0], rope_landed[0], combine_landed[0]], axis=-1))
    (y1, x1, h2), next_in = _mix_out_norm(mixed, wg["w_out"], x0, p["post_mix_norm"], p["pre_ffn_norm"],
                                          "mix_out" + tag, [gather["w_in"]] if gather else None)
    post = None if g_next is None else (x1, p["post_ffn_norm"], g_next)
    (y, up3, cv3, f, *after), next_ffn = _ffn_forward(h2, wg["w_up"], wg["w_down"], p["cw3"], p["cb3"], "ffn_fwd" + tag,
                                                      [gather["w_up"], gather["w_down"]] if gather else None, post)
    gathered = dict(w_in=next_in[0], w_out=next_out[0], w_up=next_ffn[0], w_down=next_ffn[1]) if gather else None
    saved = dict(x0=x0, h1=h1, proj=proj, q=q, k=k, v=v, ob=ob, lse=lse, mixed=mixed, y1=y1, x1=x1, h2=h2,
                 up3=up3, cv3=cv3, y=y, f=f)
    if after:
        saved.update(x2=after[0], h_next=after[1])
    return saved, gathered, wg


def _layer_backward(l, dx2, df, sv, p, wg, tabs, pos, scatter=None, hide=False):
    s = dx2.shape[0]
    nm = s // TMM
    tag = f"_l{l}"
    g = {}
    (dup3, dh2, conv_sums), scattered = _ffn_backward(df, wg["w_up"], wg["w_down"], sv["up3"], sv["cv3"], p["cw3"],
                                                      "ffn_bwd" + tag, scatter)
    sums = conv_sums.transpose(1, 2, 0, 3).reshape(2, 8, D_FF)
    g["conv_w"] = jnp.concatenate([sums[0, :3], sums[1, :3]], axis=1)
    g["conv_b"] = jnp.concatenate([sums[0, 3:4], sums[1, 3:4]], axis=1)
    tn = 1024
    done = {}
    gw_down = _matmul(
        sv["y"], df, grid=(D_FF // tn,), a_spec=pl.BlockSpec((s, tn), lambda k: (0, k)),
        b_spec=pl.BlockSpec((s, D_MODEL), lambda k: (0, 0)),
        o_spec=pl.BlockSpec((2, tn, D_MODEL // 2), lambda k: (0, k, 0)),
        o_shape=(2, D_FF, D_MODEL // 2), o_dtype=BF16,
        dims=TN, nk=1, kaxis=None, acc_shape=None, name="w_down_grad" + tag, halves=True)
    down_sums = _chip_sums(l, dict(w_down=gw_down), pos, ("w_down",)) if hide else None
    gw_up = _matmul(
        sv["h2"], dup3, grid=(2 * D_FF // tn,), a_spec=pl.BlockSpec((s, D_MODEL), lambda n: (0, 0)),
        b_spec=pl.BlockSpec((None, s, tn), lambda n: (n // (D_FF // tn), 0, n % (D_FF // tn))),
        o_spec=pl.BlockSpec((None, D_MODEL, tn), lambda n: (n // 2, 0, n % 2)),
        o_shape=(N_CHIPS, D_MODEL, 2 * D_FF // N_CHIPS), o_dtype=BF16,
        dims=TN, nk=1, kaxis=None, acc_shape=None, name="w_up_grad" + tag,
        scatter=(down_sums, ("w_down",)) if hide else None)
    up_sums = None
    if hide:
        gw_up, received = gw_up
        done[("w_down",)] = (down_sums, received)
        up_sums = _chip_sums(l, dict(w_up=gw_up), pos, ("w_up",))
    dx1, dy1, g["pre_ffn_norm"], g["post_mix_norm"] = _norm_bwd_mid(
        dx2, dh2, sv["x1"], p["pre_ffn_norm"], sv["y1"], p["post_mix_norm"], "norm_bwd_mid" + tag)
    w_out_all = pl.BlockSpec((N_CHIPS, None, D_MODEL // N_CHIPS, D_MODEL), lambda i: (0, 0, 0, 0))
    dmixed = _matmul(
        dy1, wg["w_out"], grid=(nm,), a_spec=pl.BlockSpec((TMM, D_MODEL), lambda i: (i, 0)), b_spec=w_out_all,
        o_spec=pl.BlockSpec((TMM, D_MODEL), lambda i: (i, 0)), o_shape=(s, D_MODEL), o_dtype=F32,
        dims=NT, nk=1, kaxis=None, acc_shape=None, name="mix_out_bwd" + tag, b_2d=(D_MODEL, D_MODEL))
    gw_out = _matmul(
        sv["mixed"], dy1, grid=(1,), a_spec=pl.BlockSpec((s, D_MODEL), lambda m: (0, 0)),
        b_spec=pl.BlockSpec((s, D_MODEL), lambda m: (0, 0)),
        o_spec=pl.BlockSpec((2, D_MODEL, D_MODEL // 2), lambda m: (0, 0, 0)),
        o_shape=(2, D_MODEL, D_MODEL // 2), o_dtype=BF16,
        dims=TN, nk=1, kaxis=None, acc_shape=None, name="w_out_grad" + tag, halves=True)
    out_sums = _chip_sums(l, dict(w_out=gw_out), pos, ("w_out",)) if hide else None
    dpa, g["out_norm_a"], g["v_norm_g"], g["v_norm_b"], dbs, g["w_spatial"], received = _mixer_a_bwd(
        sv["proj"], dmixed, p["v_norm_g"], p["v_norm_b"], p["w_spatial"], p["bs_full"], p["out_norm_a"],
        "mixer_a_bwd" + tag, (out_sums, ("w_out",)) if hide else None)
    if hide:
        done[("w_out",)] = (out_sums, received)
    g["b_spatial"] = dbs[:, ::GROUP_DIM].T
    dob, delta, g["out_norm_b"] = _attn_bwd_prep(dmixed, sv["ob"], p["out_norm_b"], "attn_bwd_prep" + tag)
    halves = dict(zip(DILATIONS, ("w_up:0", "w_up:1"))) if hide else {}
    dqs, dks, dvs, edges, received = zip(*[
        _attn_bwd(*(_as_classes(t[d]) for t in (sv["q"], sv["k"], sv["v"], dob, sv["lse"], delta)),
                  f"attn_bwd_d{d}" + tag, (up_sums, (halves[d],)) if d in halves else None)
        for d in DILATIONS])
    if hide:
        done[("w_up",)] = (up_sums, [jnp.concatenate([received[0][0], received[1][0]], axis=-1)])
    nat = lambda ts: [t.reshape(s, B_WIDTH) if d == 1 else t for t, d in zip(ts, DILATIONS)]
    halos = [t[0] for t in edges[0]]
    dproj = _rope_bwd(nat(dqs), nat(dks), nat(dvs), halos, tabs, dpa, "rope_bwd" + tag)
    wcol = IN_COLS // N_CHIPS
    gw_in = _matmul(
        sv["h1"], dproj, grid=(N_CHIPS,), a_spec=pl.BlockSpec((s, D_MODEL), lambda n: (0, 0)),
        b_spec=pl.BlockSpec((s, wcol), lambda n: (0, n)),
        o_spec=pl.BlockSpec((None, D_MODEL, wcol), lambda n: (n, 0, 0)),
        o_shape=(N_CHIPS, D_MODEL, wcol), o_dtype=BF16,
        dims=TN, nk=1, kaxis=None, acc_shape=None, name="w_in_grad" + tag)
    in_sums = _chip_sums(l, dict(w_in=gw_in), pos, ("w_in",)) if hide else None
    dh1, received = _proj_bwd(dproj, wg["w_in"], "proj_bwd" + tag, (in_sums, ("w_in",)) if hide else None)
    if hide:
        done[("w_in",)] = (in_sums, received)
    big = {} if hide else dict(w_in=gw_in, w_up=gw_up, w_out=gw_out, w_down=gw_down)
    return dx1, dh1, big, g, scattered, done


SMALL = ("pre_mix_norm", "v_norm_g", "v_norm_b", "w_spatial", "b_spatial", "out_norm_a", "out_norm_b",
         "post_mix_norm", "pre_ffn_norm", "conv_b", "post_ffn_norm")
BIG = ("w_in", "w_out", "w_up", "w_down")
DEPTH = 2


def _layer_params(l, small, conv_w_full):
    p = {n: small[n][l].reshape(1, -1) for n in SMALL if n not in ("w_spatial", "b_spatial")}
    p["w_spatial"] = small["w_spatial"][l]
    p["bs_full"] = jnp.repeat(small["b_spatial"][l].T, GROUP_DIM, axis=1)
    p["cw3"] = conv_w_full[l].reshape(3, 2, D_FF).transpose(1, 0, 2)
    p["cb3"] = small["conv_b"][l].reshape(2, 1, D_FF)
    return p


def _mesh_pos():
    return lax.axis_index("x"), lax.axis_index("y"), lax.axis_index("c")


def _other_chips(x, y):
    return [(1 - x, y), (x, 1 - y), (1 - x, 1 - y)]


def _gathered_shapes(blocks):
    return [jax.ShapeDtypeStruct((N_CHIPS, 1) + a.shape, a.dtype) for a in blocks]


def _gather_sems(nw):
    n = 2 * nw * (N_CHIPS - 1) + nw
    return [pltpu.SemaphoreType.DMA((n,)), pltpu.SemaphoreType.DMA((n,))]


def _gather_steps(ins, outs, send, recv):
    nw, nrel = len(ins), N_CHIPS - 1
    x, y, c = _mesh_pos()
    mine, sibling, chips = 2 * x + y, (x, y, 1 - c), _other_chips(x, y)

    def copy(src, dst, slot, to):
        return pltpu.make_async_remote_copy(src_ref=src, dst_ref=dst, send_sem=send.at[slot],
                                            recv_sem=recv.at[slot], device_id=to, device_id_type=MESH)

    def half_rows(t, core):
        rows = ins[t].shape[0] // 2
        return pl.ds(pl.multiple_of(core * rows, rows), rows)

    def landing(t, chip, core):
        return outs[t].at[chip, 0, half_rows(t, core), :]

    slots = [(t, r, chip) for t in range(nw) for r, chip in enumerate(chips)]
    own = [copy(ins[t], outs[t].at[mine, 0], 2 * nw * nrel + t, sibling) for t in range(nw)]
    first = [copy(ins[t].at[half_rows(t, c), :], landing(t, mine, c), t * nrel + r, (px, py, c))
             for t, r, (px, py) in slots]
    relays = [copy(landing(t, 2 * px + py, c), landing(t, 2 * px + py, c), nw * nrel + t * nrel + r, sibling)
              for t, r, (px, py) in slots]

    def start():
        for cp in own + first:
            cp.start()

    def relay():
        for (t, r, (px, py)), cp in zip(slots, relays):
            copy(landing(t, 2 * px + py, c), landing(t, 2 * px + py, c), t * nrel + r, (px, py, c)).wait_recv()
            cp.start()

    def finish():
        for t, r, (px, py) in slots:
            passed = landing(t, 2 * px + py, 1 - c)
            copy(passed, passed, nw * nrel + t * nrel + r, sibling).wait_recv()
        for cp in first + relays:
            cp.wait_send()
        for cp in own:
            cp.wait()

    return start, relay, finish


HALF = 512

GRAD_GEOM = {"w_in": ("rows", D_MODEL, IN_COLS // N_CHIPS), "w_up": ("rows", D_MODEL, 2 * D_FF // N_CHIPS),
             "w_out": ("cols", D_MODEL, D_MODEL // N_CHIPS), "w_down": ("cols", D_FF, D_FF // N_CHIPS)}


def _exchange_shape(n):
    kind, a, b = GRAD_GEOM[n]
    return (N_CHIPS, HALF, b) if kind == "rows" else (a, HALF)


def _piece_shape(n):
    name, _, part = n.partition(":")
    kind, _, b = GRAD_GEOM[name]
    if part:
        assert kind == "rows"
        return (HALF, b // 2)
    return (HALF, b) if kind == "rows" else (b, HALF)


def _half_of(ref, n, core):
    if GRAD_GEOM[n][0] == "rows":
        return ref.at[:, pl.ds(pl.multiple_of(core * HALF, HALF), HALF), :]
    return ref.at[core]


def _piece_of(ref, n, chip):
    name, _, part = n.partition(":")
    kind, _, b = GRAD_GEOM[name]
    if part:
        return ref.at[chip, :, pl.ds(int(part) * (b // 2), b // 2)]
    return ref.at[chip] if kind == "rows" else ref.at[pl.ds(pl.multiple_of(chip * b, b), b), :]


def _pair_exchange(g, names, name):
    n = len(names)

    def body(*refs):
        send, recv = refs[2 * n:]
        x, y, c = _mesh_pos()
        o = 1 - c
        cps = [pltpu.make_async_remote_copy(src_ref=_half_of(refs[t], nm, o), dst_ref=refs[n + t], send_sem=send.at[t],
                                            recv_sem=recv.at[t], device_id=(x, y, o), device_id_type=MESH)
               for t, nm in enumerate(names)]
        for cp in cps:
            cp.start()
        for cp in cps:
            cp.wait()

    return pl.pallas_call(
        body, in_specs=[ANY] * n, out_specs=[ANY] * n,
        out_shape=[jax.ShapeDtypeStruct(_exchange_shape(nm), BF16) for nm in names],
        scratch_shapes=[pltpu.SemaphoreType.DMA((n,)), pltpu.SemaphoreType.DMA((n,))],
        name=name)(*[g[nm] for nm in names])


def _pair_sum(g, recv, pos, names, name_prefix):
    def add(a, b, grid, a_spec, b_spec, name):
        def body(pos_ref, a_ref, b_ref, o_ref):
            o_ref[...] = (a_ref[...].astype(F32) + b_ref[...].astype(F32)).astype(BF16)

        return pl.pallas_call(
            body, grid_spec=pltpu.PrefetchScalarGridSpec(
                num_scalar_prefetch=1, grid=grid, in_specs=[a_spec, b_spec], out_specs=b_spec),
            out_shape=jax.ShapeDtypeStruct(b.shape, BF16), compiler_params=_cparams("parallel"), name=name)(pos, a, b)

    out = []
    for nm, r in zip(names, recv):
        kind, rows, width = GRAD_GEOM[nm]
        if kind == "rows":
            out.append(add(g[nm], r, (N_CHIPS,), pl.BlockSpec((None, HALF, width), lambda j, pos: (j, pos[2], 0)),
                           pl.BlockSpec((None, HALF, width), lambda j, pos: (j, 0, 0)), f"{name_prefix}_{nm}"))
        else:
            out.append(add(g[nm], r, (rows // D_MODEL,), pl.BlockSpec((None, D_MODEL, HALF), lambda j, pos: (pos[2], j, 0)),
                           pl.BlockSpec((D_MODEL, HALF), lambda j, pos: (j, 0)), f"{name_prefix}_{nm}"))
    return out


def _scattered_shapes(names):
    return [jax.ShapeDtypeStruct((N_CHIPS - 1,) + _piece_shape(nm), BF16) for nm in names]


def _scatter_sems(n):
    return [pltpu.SemaphoreType.DMA((n * (N_CHIPS - 1),)), pltpu.SemaphoreType.DMA((n * (N_CHIPS - 1),))]


def _scatter_steps(sums, outs, send, recv, names):
    nrel = N_CHIPS - 1
    x, y, c = _mesh_pos()
    cps = []
    for r, (px, py) in enumerate(_other_chips(x, y)):
        for t, nm in enumerate(names):
            cps.append(pltpu.make_async_remote_copy(
                src_ref=_piece_of(sums[t], nm, 2 * px + py), dst_ref=outs[t].at[r], send_sem=send.at[t * nrel + r],
                recv_sem=recv.at[t * nrel + r], device_id=(px, py, c), device_id_type=MESH))

    def start():
        for cp in cps:
            cp.start()

    def finish():
        for cp in cps:
            cp.wait()

    return start, finish


def _chip_scatter(sums, names, name):
    n = len(names)

    def body(*refs):
        start, finish = _scatter_steps(refs[:n], refs[n:2 * n], *refs[2 * n:], names)
        start()
        finish()

    return pl.pallas_call(
        body, in_specs=[ANY] * n, out_specs=[ANY] * n, out_shape=_scattered_shapes(names),
        scratch_shapes=_scatter_sems(n), name=name)(*sums)


def _chip_sum(sums, recv, pos, names, name_prefix):
    def add(a, b, a_spec, shape, name):
        def body(pos_ref, a_ref, b_ref, o_ref):
            tot = a_ref[...].astype(F32)
            for r in range(N_CHIPS - 1):
                tot = tot + b_ref[r].astype(F32)
            o_ref[...] = tot

        return pl.pallas_call(
            body, grid_spec=pltpu.PrefetchScalarGridSpec(
                num_scalar_prefetch=1, grid=(1,), in_specs=[a_spec, pl.BlockSpec(b.shape, lambda i, pos: (0, 0, 0))],
                out_specs=pl.BlockSpec((None,) + shape, lambda i, pos: (pos[2], 0, 0))),
            out_shape=jax.ShapeDtypeStruct((2,) + shape, F32), compiler_params=_cparams("arbitrary"),
            name=name)(pos, a, b)

    chip = lambda pos: 2 * pos[0] + pos[1]
    out = []
    for nm, a, b in zip(names, sums, recv):
        shape = _piece_shape(nm)
        if GRAD_GEOM[nm][0] == "rows":
            spec = pl.BlockSpec((None,) + shape, lambda i, pos: (chip(pos), 0, 0))
        else:
            spec = pl.BlockSpec(shape, lambda i, pos: (chip(pos), 0))
        out.append(add(a, b, spec, shape, f"{name_prefix}_{nm}"))
    return out


def _pair_share(totals, name):
    n = len(totals)

    def body(*refs):
        ins, outs = refs[:n], refs[n:2 * n]
        send, recv = refs[2 * n:]
        x, y, c = _mesh_pos()
        o = 1 - c
        cps = [pltpu.make_async_remote_copy(src_ref=ins[t].at[c], dst_ref=outs[t].at[c], send_sem=send.at[t],
                                            recv_sem=recv.at[t], device_id=(x, y, o), device_id_type=MESH)
               for t in range(n)]
        for cp in cps:
            cp.start()
        for t in range(n):
            pltpu.make_async_remote_copy(src_ref=ins[t].at[o], dst_ref=outs[t].at[o], send_sem=send.at[t],
                                         recv_sem=recv.at[t], device_id=(x, y, o), device_id_type=MESH).wait_recv()
        for cp in cps:
            cp.wait_send()

    return pl.pallas_call(
        body, in_specs=[ANY] * n, out_specs=[ANY] * n,
        out_shape=[jax.ShapeDtypeStruct(t.shape, t.dtype) for t in totals],
        scratch_shapes=[pltpu.SemaphoreType.DMA((n,)), pltpu.SemaphoreType.DMA((n,))],
        input_output_aliases={t: t for t in range(n)}, name=name)(*totals)


def _chip_sums(l, g, pos, names):
    tag = f"l{l}_" + "_".join(names)
    recv = _pair_exchange(g, names, "pair_exchange_" + tag)
    return _pair_sum(g, recv, pos, names, "pair_sum_" + tag)


def _gradient_shards(l, sums, scattered, pos, names):
    tag = f"l{l}_" + "_".join(names)
    halves = _pair_share(_chip_sum(sums, scattered, pos, names, "chip_sum_" + tag), "pair_share_" + tag)
    out = {}
    for nm, t in zip(names, halves):
        rows, cols = _piece_shape(nm)
        out[nm] = t.reshape(2 * rows, cols) if GRAD_GEOM[nm][0] == "rows" else t.transpose(1, 0, 2).reshape(rows, 2 * cols)
    return out


N_DEV = 8


def _allreduce_small(packed, name):
    rows = packed.shape[0]

    def body(x_ref, out_ref, gath, send_sems, recv_sems, local_sem):
        x, y, c = _mesh_pos()
        me, sibling = (x, y, c), (x, y, 1 - c)
        chips = _other_chips(x, y)

        def blk(px, py, pc):
            return gath.at[pl.ds(pl.multiple_of((4 * px + 2 * py + pc) * rows, 8), rows), :]

        def copy(k, block, to, src=None):
            return pltpu.make_async_remote_copy(
                src_ref=blk(*block) if src is None else src, dst_ref=blk(*block), send_sem=send_sems.at[k],
                recv_sem=recv_sems.at[k], device_id=to, device_id_type=MESH)

        mine = pltpu.make_async_copy(x_ref, blk(*me), local_sem)
        mine.start()
        first = [copy(0, me, sibling, src=x_ref)]
        first += [copy(1 + j, me, (*chip, c), src=x_ref) for j, chip in enumerate(chips)]
        for cp in first:
            cp.start()
        passed = [copy(4 + j, (*chip, c), sibling) for j, chip in enumerate(chips)]
        for j, chip in enumerate(chips):
            copy(1 + j, (*chip, c), me).wait_recv()
            passed[j].start()
        copy(0, sibling, me).wait_recv()
        for j, chip in enumerate(chips):
            copy(4 + j, (*chip, 1 - c), me).wait_recv()
        for cp in first + passed:
            cp.wait_send()
        mine.wait()
        tot = gath[0:rows, :]
        for d in range(1, N_DEV):
            tot = tot + gath[d * rows:(d + 1) * rows, :]
        out_ref[...] = tot

    vmem = pl.BlockSpec(memory_space=pltpu.VMEM)
    return pl.pallas_call(
        body, in_specs=[vmem], out_specs=vmem, out_shape=jax.ShapeDtypeStruct((rows, LANES), F32),
        scratch_shapes=[pltpu.VMEM((N_DEV * rows, LANES), F32), pltpu.SemaphoreType.DMA((7,)),
                        pltpu.SemaphoreType.DMA((7,)), pltpu.SemaphoreType.DMA],
        compiler_params=pltpu.CompilerParams(vmem_limit_bytes=VMEM_LIMIT_BYTES),
        name=name)(packed)


def _adamw(w, g, m, v, name):
    rows, cols = w.shape
    tr = 256 if rows % 256 == 0 else rows

    def body(w_ref, g_ref, m_ref, v_ref, d_ref, mo_ref, vo_ref):
        gv = g_ref[...]
        mn = ADAM_B1 * m_ref[...] + (1.0 - ADAM_B1) * gv
        vn = ADAM_B2 * v_ref[...] + (1.0 - ADAM_B2) * (gv * gv)
        m_hat = mn / (1.0 - ADAM_B1 ** ADAM_STEP)
        v_hat = vn / (1.0 - ADAM_B2 ** ADAM_STEP)
        d_ref[...] = -ADAM_LR * (m_hat / (jnp.sqrt(v_hat) + ADAM_EPS) + ADAM_WD * w_ref[...])
        mo_ref[...] = mn
        vo_ref[...] = vn

    spec = pl.BlockSpec((tr, cols), lambda i: (i, 0))
    return pl.pallas_call(
        body, grid=(rows // tr,), in_specs=[spec] * 4, out_specs=[spec] * 3,
        out_shape=[jax.ShapeDtypeStruct((rows, cols), F32)] * 3, compiler_params=_cparams("parallel"),
        name=name)(w, g, m, v)


def _adamw_nd(w, g, m, v, name):
    cols = w.shape[-1] if w.shape[-1] % LANES == 0 else LANES
    outs = _adamw(*(t.reshape(-1, cols) for t in (w, g, m, v)), name)
    return tuple(t.reshape(w.shape) for t in outs)


def _pack(arrays):
    return jnp.concatenate([a.reshape(-1, LANES) for a in arrays], axis=0)


def _unpack(packed, shapes):
    out, row = [], 0
    for sh in shapes:
        n = math.prod(sh) // LANES
        out.append(packed[row:row + n].reshape(sh))
        row += n
    return out


WEIGHTS = ("pre_mix_norm", "w_in", "v_norm_g", "v_norm_b", "w_spatial", "b_spatial", "out_norm_a", "out_norm_b",
           "w_out", "post_mix_norm", "pre_ffn_norm", "w_up", "conv_w", "conv_b", "w_down", "post_ffn_norm")


def kernel(x, pre_mix_norm, w_in, v_norm_g, v_norm_b, w_spatial, b_spatial, out_norm_a, out_norm_b, w_out, post_mix_norm, pre_ffn_norm, w_up, conv_w, conv_b, w_down, post_ffn_norm, loss_target, m_pre_mix_norm, m_w_in, m_v_norm_g, m_v_norm_b, m_w_spatial, m_b_spatial, m_out_norm_a, m_out_norm_b, m_w_out, m_post_mix_norm, m_pre_ffn_norm, m_w_up, m_conv_w, m_conv_b, m_w_down, m_post_ffn_norm, v_pre_mix_norm, v_w_in, v_v_norm_g, v_v_norm_b, v_w_spatial, v_b_spatial, v_out_norm_a, v_out_norm_b, v_w_out, v_post_mix_norm, v_pre_ffn_norm, v_w_up, v_conv_w, v_conv_b, v_w_down, v_post_ffn_norm):
    w = dict(pre_mix_norm=pre_mix_norm, w_in=w_in, v_norm_g=v_norm_g, v_norm_b=v_norm_b, w_spatial=w_spatial,
             b_spatial=b_spatial, out_norm_a=out_norm_a, out_norm_b=out_norm_b, w_out=w_out,
             post_mix_norm=post_mix_norm, pre_ffn_norm=pre_ffn_norm, w_up=w_up, conv_w=conv_w, conv_b=conv_b,
             w_down=w_down, post_ffn_norm=post_ffn_norm)
    m = dict(pre_mix_norm=m_pre_mix_norm, w_in=m_w_in, v_norm_g=m_v_norm_g, v_norm_b=m_v_norm_b,
             w_spatial=m_w_spatial, b_spatial=m_b_spatial, out_norm_a=m_out_norm_a, out_norm_b=m_out_norm_b,
             w_out=m_w_out, post_mix_norm=m_post_mix_norm, pre_ffn_norm=m_pre_ffn_norm, w_up=m_w_up,
             conv_w=m_conv_w, conv_b=m_conv_b, w_down=m_w_down, post_ffn_norm=m_post_ffn_norm)
    v = dict(pre_mix_norm=v_pre_mix_norm, w_in=v_w_in, v_norm_g=v_v_norm_g, v_norm_b=v_v_norm_b,
             w_spatial=v_w_spatial, b_spatial=v_b_spatial, out_norm_a=v_out_norm_a, out_norm_b=v_out_norm_b,
             w_out=v_w_out, post_mix_norm=v_post_mix_norm, pre_ffn_norm=v_pre_ffn_norm, w_up=v_w_up,
             conv_w=v_conv_w, conv_b=v_conv_b, w_down=v_w_down, post_ffn_norm=v_post_ffn_norm)
    pos = jnp.stack([lax.axis_index("x"), lax.axis_index("y"), lax.axis_index("c")]).astype(jnp.int32)
    chip = 2 * lax.axis_index("x") + lax.axis_index("y")

    cw_cols = conv_w.shape[-1]
    blocks = [{n: w[n][l].astype(BF16) for n in BIG} for l in range(DEPTH)]
    small = {n: w[n] for n in SMALL}
    xs, target = x[0], loss_target[0]
    xin = xs
    h, (w_in0, cw_all) = _rms_cast(xin, small["pre_mix_norm"][0].reshape(1, -1), "pre_mix_l0",
                                   [blocks[0]["w_in"], conv_w.reshape(-1, LANES)])
    wg = dict(w_in=w_in0)
    conv_w_full = cw_all.reshape(N_CHIPS, DEPTH, 3, cw_cols).transpose(1, 2, 0, 3).reshape(DEPTH, 3, 2 * D_FF)

    tabs = _rope_tables(xs.shape[0])
    params = [_layer_params(l, small, conv_w_full) for l in range(DEPTH)]
    saved, wgs = [], []
    for l in range(DEPTH):
        sv, gathered, wg = _layer_forward(l, xin, h, params[l], wg, tabs,
                                          blocks[l + 1] if l + 1 < DEPTH else None,
                                          blocks[0] if l == 0 else None,
                                          params[l + 1]["pre_mix_norm"] if l + 1 < DEPTH else None)
        saved.append(sv)
        wgs.append(wg)
        if l + 1 < DEPTH:
            wg = gathered
            xin, h = sv["x2"], sv["h_next"]
    loss_part, dx, df, g_post = _loss_norm_bwd(saved[-1]["x1"], saved[-1]["f"], params[-1]["post_ffn_norm"], target,
                                               "loss")
    smalls, shards = [None] * DEPTH, [{} for _ in range(DEPTH)]
    pending = None
    for l in reversed(range(DEPTH)):
        dx1, dh1, big, smalls[l], scattered, done = _layer_backward(l, dx, df, saved[l], params[l], wgs[l], tabs, pos,
                                                                    pending[1:] if pending else None, hide=l == 0)
        smalls[l]["post_ffn_norm"] = g_post
        if l > 0:
            dx, smalls[l]["pre_mix_norm"], df, g_post = _norm_bwd_in_out(
                dx1, dh1, saved[l]["x0"], params[l]["pre_mix_norm"], saved[l - 1]["f"], params[l - 1]["post_ffn_norm"],
                f"norm_bwd_in_out_l{l}")
        else:
            dx, smalls[l]["pre_mix_norm"] = _norm_bwd_in(dx1, dh1, saved[l]["x0"], params[l]["pre_mix_norm"],
                                                         "norm_bwd_in_l0")
        if pending:
            shards[pending[0]].update(_gradient_shards(pending[0], pending[1], scattered, pos, pending[2]))
        if done:
            shards[l].update(_gradient_shards(
                l, [t for sums, _ in done.values() for t in sums], [t for _, received in done.values() for t in received],
                pos, tuple(n for names in done for n in names)))
        names = tuple(big)
        pending = (l, _chip_sums(l, big, pos, names), names) if names else None
    if pending:
        shards[pending[0]].update(_gradient_shards(
            pending[0], pending[1], _chip_scatter(pending[1], pending[2], f"chip_scatter_l{pending[0]}"), pos,
            pending[2]))

    small_shapes = [w[n].shape for n in SMALL]
    stacked = [jnp.stack([smalls[l][n].reshape(w[n].shape[1:]) for l in range(DEPTH)]) for n in SMALL]
    cw_grad = jnp.stack([smalls[l]["conv_w"] for l in range(DEPTH)])
    packed = _pack(stacked + [cw_grad, loss_part])
    total = _allreduce_small(packed, "allreduce_small")
    parts = _unpack(total, small_shapes + [cw_grad.shape, (8, LANES)])
    g_small = dict(zip(SMALL, parts[:len(SMALL)]))
    loss = parts[-1][0, 0]
    g_conv_w = lax.dynamic_slice(parts[-2], (0, 0, chip * cw_cols), conv_w.shape)

    grads = {n: jnp.stack([shards[l][n] for l in range(DEPTH)]) for n in BIG}
    grads.update(g_small)
    grads["conv_w"] = g_conv_w

    dp, mp, vp = _adamw(_pack([w[n] for n in SMALL]), _pack([g_small[n] for n in SMALL]),
                        _pack([m[n] for n in SMALL]), _pack([v[n] for n in SMALL]), "adamw_small")
    delta = dict(zip(SMALL, _unpack(dp, small_shapes)))
    new_m = dict(zip(SMALL, _unpack(mp, small_shapes)))
    new_v = dict(zip(SMALL, _unpack(vp, small_shapes)))
    for n in BIG + ("conv_w",):
        delta[n], new_m[n], new_v[n] = _adamw_nd(w[n], grads[n], m[n], v[n], "adamw_" + n)

    return (loss, dx[None], *[grads[n] for n in WEIGHTS], *[delta[n] for n in WEIGHTS],
            *[new_m[n] for n in WEIGHTS], *[new_v[n] for n in WEIGHTS])
```

```python
import functools
import math

import jax
import jax.numpy as jnp
import numpy as np
from jax import lax
from jax.experimental import pallas as pl
from jax.experimental.pallas import tpu as pltpu

F32 = jnp.float32
BF16 = jnp.bfloat16
MESH = pl.DeviceIdType.MESH

D_MODEL = 1024
A_WIDTH = 512
A_GROUPS = 4
GROUP_DIM = 128
CHUNK = 128
B_WIDTH = 512
HEAD_DIM = 64
ROT_DIM = 16
ROPE_THETA = 500000.0
DILATIONS = (1, 4, 16)
BAND = 128
IN_COLS = 2560
D_FF = 4096
EPS = 1e-6
NEG_INF = -1e30
N_CHIPS = 4
LANES = 128

ADAM_LR = 0.001
ADAM_B1 = 0.9
ADAM_B2 = 0.999
ADAM_EPS = 1e-08
ADAM_WD = 0.01
ADAM_STEP = 10

VMEM_LIMIT_BYTES = 56 * 1024 * 1024
RSQRT2 = 0.7071067811865476
INV_SQRT_2PI = 0.3989422804014327
GELU_C = 0.7978845608028654
GELU_A = 0.044715

ANY = pl.BlockSpec(memory_space=pl.ANY)
NN = ((1,), (0,))
NT = ((1,), (1,))
TN = ((0,), (0,))


def _cparams(*sem):
    return pltpu.CompilerParams(dimension_semantics=sem, vmem_limit_bytes=VMEM_LIMIT_BYTES)


def _dot(a, b, dims):
    return lax.dot_general(a, b, (dims, ((), ())), preferred_element_type=F32)


def _rsq_mean(a):
    return lax.rsqrt(jnp.mean(a * a, axis=-1, keepdims=True) + EPS)


def _rms_bwd(a, r, g, dz):
    t = dz * g
    da = r * t - a * (r * r * r) * jnp.mean(t * a, axis=-1, keepdims=True)
    return da, dz * a * r


def _colsum(a):
    return jnp.sum(a, axis=0, keepdims=True)


def _gelu_tanh(x):
    u = x * x
    t = jnp.tanh(x * (GELU_C + (GELU_C * GELU_A) * u))
    hx = 0.5 * x
    act = hx + hx * t
    grad = 0.5 + 0.5 * t + (hx - hx * t * t) * (GELU_C + (3.0 * GELU_C * GELU_A) * u)
    return act, grad


def _grid_edges(grid):
    ids = [pl.program_id(ax) for ax in range(len(grid))]
    first = functools.reduce(jnp.logical_and, [i == 0 for i in ids])
    last = functools.reduce(jnp.logical_and, [i == n - 1 for i, n in zip(ids, grid)])
    return first, last


def _matmul(a, b, *, grid, a_spec, b_spec, o_spec, o_shape, o_dtype, dims, nk, kaxis, acc_shape, name, b_2d=None,
            halves=False, scatter=None, gather=None):
    assert scatter is None or gather is None
    ns = len(scatter[0]) if scatter else len(gather) if gather else 0

    def body(*refs):
        a_ref, b_ref = refs[:2]
        o_ref = refs[2 + ns]
        scratch = refs[3 + 2 * ns:]
        if ns:
            first, last = _grid_edges(grid)
            if scatter:
                start, finish = _scatter_steps(refs[2:2 + ns], refs[3 + ns:3 + 2 * ns], scratch[-2], scratch[-1],
                                               scatter[1])
            else:
                start, relay, last_wait = _gather_steps(refs[2:2 + ns], refs[3 + ns:3 + 2 * ns], scratch[-2],
                                                        scratch[-1])

                def finish():
                    relay()
                    last_wait()
            pl.when(first)(start)
        def store(val):
            if halves:
                half = val.shape[1] // 2
                o_ref[0] = val[:, :half].astype(o_dtype)
                o_ref[1] = val[:, half:].astype(o_dtype)
            else:
                o_ref[...] = val.astype(o_dtype)

        bv = b_ref[...] if b_2d is None else b_ref[...].reshape(b_2d)
        part = _dot(a_ref[...], bv, dims)
        if nk == 1:
            store(part)
        else:
            acc = scratch[0]
            k = pl.program_id(kaxis)

            @pl.when(k == 0)
            def _():
                acc[...] = part

            @pl.when(k > 0)
            def _():
                acc[...] += part

            @pl.when(k == nk - 1)
            def _():
                store(acc[...])

        if ns:
            pl.when(last)(finish)

    sem = tuple("arbitrary" if (ns or (nk > 1 and ax == kaxis)) else "parallel" for ax in range(len(grid)))
    riding = list(scatter[0]) if scatter else list(gather or [])
    rider_shapes = _scattered_shapes(scatter[1]) if scatter else _gathered_shapes(riding)
    rider_sems = _scatter_sems(ns) if scatter else _gather_sems(ns) if gather else []
    res = pl.pallas_call(
        body, grid=grid, in_specs=[a_spec, b_spec] + [ANY] * ns, out_specs=[o_spec] + [ANY] * ns,
        out_shape=[jax.ShapeDtypeStruct(o_shape, o_dtype)] + rider_shapes,
        scratch_shapes=([pltpu.VMEM(acc_shape, F32)] if nk > 1 else []) + rider_sems,
        compiler_params=_cparams(*sem), name=name)(a, b, *riding)
    return (res[0], list(res[1:])) if ns else res[0]


def _mix_out_norm(mixed, w_out, x0, g_post, g_next, name, gather=None):
    s, d = x0.shape
    tm = 512
    ng = 0 if gather is None else len(gather)

    def body(a_ref, w_ref, x_ref, gp_ref, gn_ref, *rest):
        y_ref, x1_ref, h_ref = rest[ng:ng + 3]
        if ng:
            start, relay, finish = _gather_steps(rest[:ng], rest[ng + 3:2 * ng + 3], *rest[2 * ng + 3:])
            first, last = _grid_edges((s // tm,))
            pl.when(first)(start)
        y = _dot(a_ref[...], w_ref[...].reshape(d, d), NN)
        y_ref[...] = y
        x1 = x_ref[...] + y * _rsq_mean(y) * gp_ref[...]
        x1_ref[...] = x1
        h_ref[...] = (x1 * _rsq_mean(x1) * gn_ref[...]).astype(BF16)

        if ng:
            @pl.when(last)
            def _():
                relay()
                finish()

    row = pl.BlockSpec((tm, d), lambda i: (i, 0))
    vec = pl.BlockSpec((1, d), lambda i: (0, 0))
    res = pl.pallas_call(
        body, grid=(s // tm,),
        in_specs=[row, pl.BlockSpec((N_CHIPS, None, d // N_CHIPS, d), lambda i: (0, 0, 0, 0)), row, vec, vec]
        + [ANY] * ng,
        out_specs=[row, row, row] + [ANY] * ng,
        out_shape=[jax.ShapeDtypeStruct((s, d), F32), jax.ShapeDtypeStruct((s, d), F32),
                   jax.ShapeDtypeStruct((s, d), BF16)] + _gathered_shapes(gather or []),
        scratch_shapes=_gather_sems(ng) if ng else [],
        compiler_params=_cparams("arbitrary" if ng else "parallel"), name=name)(mixed, w_out, x0, g_post, g_next,
                                                                              *(gather or []))
    return res[:3], list(res[3:])


def _proj_bwd(dproj, w_in, name, scatter=None):
    s = dproj.shape[0]
    wcol = IN_COLS // N_CHIPS
    ns = 0 if scatter is None else len(scatter[0])

    def body(*refs):
        a_ref, w_ref = refs[:2]
        o_ref = refs[2 + ns]
        if ns:
            start, finish = _scatter_steps(refs[2:2 + ns], refs[3 + ns:3 + 2 * ns], *refs[3 + 2 * ns:], scatter[1])
            first, last = _grid_edges((s // TMM,))
            pl.when(first)(start)
        acc = _dot(a_ref[:, :wcol], w_ref[0], NT)
        for j in range(1, N_CHIPS):
            acc = acc + _dot(a_ref[:, j * wcol:(j + 1) * wcol], w_ref[j], NT)
        o_ref[...] = acc
        if ns:
            pl.when(last)(finish)

    res = pl.pallas_call(
        body, grid=(s // TMM,),
        in_specs=[pl.BlockSpec((TMM, IN_COLS), lambda i: (i, 0)),
                  pl.BlockSpec((N_CHIPS, None, D_MODEL, wcol), lambda i: (0, 0, 0, 0))] + [ANY] * ns,
        out_specs=[pl.BlockSpec((TMM, D_MODEL), lambda i: (i, 0))] + [ANY] * ns,
        out_shape=[jax.ShapeDtypeStruct((s, D_MODEL), F32)] + (_scattered_shapes(scatter[1]) if ns else []),
        scratch_shapes=_scatter_sems(ns) if ns else [],
        compiler_params=_cparams("arbitrary" if ns else "parallel"), name=name)(dproj, w_in,
                                                                              *(scatter[0] if ns else []))
    return res[0], list(res[1:])


TM = 1024
TMM = 1024


TR = 256


def _row_spec(width, col=0):
    return pl.BlockSpec((TR, width), lambda i, col=col: (i, col))


def _vec_spec(width):
    return pl.BlockSpec((1, width), lambda i: (0, 0))


def _rms_cast(x, g, name, gather=None):
    s, d = x.shape
    ng = 0 if gather is None else len(gather)

    def body(x_ref, g_ref, *rest):
        if ng:
            start, relay, finish = _gather_steps(rest[:ng], rest[ng + 1:2 * ng + 1], *rest[2 * ng + 1:])
            first, last = _grid_edges((s // TR,))
            pl.when(first)(start)
        a = x_ref[...]
        rest[ng][...] = (a * _rsq_mean(a) * g_ref[...]).astype(BF16)

        if ng:
            @pl.when(last)
            def _():
                relay()
                finish()

    res = pl.pallas_call(
        body, grid=(s // TR,), in_specs=[_row_spec(d), _vec_spec(d)] + [ANY] * ng,
        out_specs=[_row_spec(d)] + [ANY] * ng,
        out_shape=[jax.ShapeDtypeStruct((s, d), BF16)] + _gathered_shapes(gather or []),
        scratch_shapes=_gather_sems(ng) if ng else [],
        compiler_params=_cparams("arbitrary" if ng else "parallel"), name=name)(x, g, *(gather or []))
    return res[0], list(res[1:])


def _acc_init(refs):
    @pl.when(pl.program_id(0) == 0)
    def _():
        for r in refs:
            r[...] = jnp.zeros_like(r)


def _loss_norm_bwd(x1, f, g_post, target, name):
    s, d = x1.shape

    def body(x_ref, f_ref, gp_ref, t_ref, loss_ref, dx_ref, df_ref, dg_ref):
        _acc_init([loss_ref, dg_ref])
        fv = f_ref[...]
        r = _rsq_mean(fv)
        err = x_ref[...] + fv * r * gp_ref[...] - t_ref[...]
        dx = err * (1.0 / d)
        dx_ref[...] = dx
        part = 0.5 * jnp.sum(jnp.mean(err * err, axis=-1, keepdims=True), axis=0, keepdims=True)
        loss_ref[...] += jnp.broadcast_to(part, loss_ref.shape)
        da, dgt = _rms_bwd(fv, r, gp_ref[...], dx)
        df_ref[...] = da.astype(BF16)
        dg_ref[...] += _colsum(dgt)

    return pl.pallas_call(
        body, grid=(s // TR,), in_specs=[_row_spec(d), _row_spec(d), _vec_spec(d), _row_spec(d)],
        out_specs=[pl.BlockSpec((8, LANES), lambda i: (0, 0)), _row_spec(d), _row_spec(d), _vec_spec(d)],
        out_shape=[jax.ShapeDtypeStruct((8, LANES), F32), jax.ShapeDtypeStruct((s, d), F32),
                   jax.ShapeDtypeStruct((s, d), BF16), jax.ShapeDtypeStruct((1, d), F32)],
        compiler_params=_cparams("arbitrary"), name=name)(x1, f, g_post, target)


def _norm_bwd_mid(dx2, dh2, x1, g_pf, y1, g_pm, name):
    s, d = dx2.shape

    def body(dx2_ref, dh_ref, x1_ref, gpf_ref, y1_ref, gpm_ref, dx1_ref, dy1_ref, dgpf_ref, dgpm_ref):
        _acc_init([dgpf_ref, dgpm_ref])
        x1 = x1_ref[...]
        da, dgt = _rms_bwd(x1, _rsq_mean(x1), gpf_ref[...], dh_ref[...])
        dx1 = dx2_ref[...] + da
        dx1_ref[...] = dx1
        dgpf_ref[...] += _colsum(dgt)
        y1 = y1_ref[...]
        dy, dgt2 = _rms_bwd(y1, _rsq_mean(y1), gpm_ref[...], dx1)
        dy1_ref[...] = dy.astype(BF16)
        dgpm_ref[...] += _colsum(dgt2)

    return pl.pallas_call(
        body, grid=(s // TR,),
        in_specs=[_row_spec(d), _row_spec(d), _row_spec(d), _vec_spec(d), _row_spec(d), _vec_spec(d)],
        out_specs=[_row_spec(d), _row_spec(d), _vec_spec(d), _vec_spec(d)],
        out_shape=[jax.ShapeDtypeStruct((s, d), F32), jax.ShapeDtypeStruct((s, d), BF16),
                   jax.ShapeDtypeStruct((1, d), F32), jax.ShapeDtypeStruct((1, d), F32)],
        compiler_params=_cparams("arbitrary"), name=name)(dx2, dh2, x1, g_pf, y1, g_pm)


def _norm_bwd_in_out(dx1, dh1, x0, g1, f_below, g_post_below, name):
    s, d = dx1.shape

    def body(dx1_ref, dh_ref, x0_ref, g_ref, f_ref, gp_ref, dx0_ref, dg_ref, df_ref, dgp_ref):
        _acc_init([dg_ref, dgp_ref])
        x0 = x0_ref[...]
        da, dgt = _rms_bwd(x0, _rsq_mean(x0), g_ref[...], dh_ref[...])
        dx0 = dx1_ref[...] + da
        dx0_ref[...] = dx0
        dg_ref[...] += _colsum(dgt)
        fv = f_ref[...]
        db, dgt2 = _rms_bwd(fv, _rsq_mean(fv), gp_ref[...], dx0)
        df_ref[...] = db.astype(BF16)
        dgp_ref[...] += _colsum(dgt2)

    return pl.pallas_call(
        body, grid=(s // TR,),
        in_specs=[_row_spec(d), _row_spec(d), _row_spec(d), _vec_spec(d), _row_spec(d), _vec_spec(d)],
        out_specs=[_row_spec(d), _vec_spec(d), _row_spec(d), _vec_spec(d)],
        out_shape=[jax.ShapeDtypeStruct((s, d), F32), jax.ShapeDtypeStruct((1, d), F32),
                   jax.ShapeDtypeStruct((s, d), BF16), jax.ShapeDtypeStruct((1, d), F32)],
        compiler_params=_cparams("arbitrary"), name=name)(dx1, dh1, x0, g1, f_below, g_post_below)


def _norm_bwd_in(dx1, dh1, x0, g1, name):
    s, d = dx1.shape

    def body(dx1_ref, dh_ref, x0_ref, g_ref, dx0_ref, dg_ref):
        _acc_init([dg_ref])
        x0 = x0_ref[...]
        da, dgt = _rms_bwd(x0, _rsq_mean(x0), g_ref[...], dh_ref[...])
        dx0_ref[...] = dx1_ref[...] + da
        dg_ref[...] += _colsum(dgt)

    return pl.pallas_call(
        body, grid=(s // TR,), in_specs=[_row_spec(d), _row_spec(d), _row_spec(d), _vec_spec(d)],
        out_specs=[_row_spec(d), _vec_spec(d)],
        out_shape=[jax.ShapeDtypeStruct((s, d), F32), jax.ShapeDtypeStruct((1, d), F32)],
        compiler_params=_cparams("arbitrary"), name=name)(dx1, dh1, x0, g1)


def _tril_mask():
    row = lax.broadcasted_iota(jnp.int32, (CHUNK, CHUNK), 0)
    col = lax.broadcasted_iota(jnp.int32, (CHUNK, CHUNK), 1)
    return row >= col


def _gating_forward(pa, gv, bv, wt, bsf):
    er = lax.erf(pa * RSQRT2)
    za = 0.5 * pa * (1.0 + er)
    u = za[:, :A_WIDTH]
    va = za[:, A_WIDTH:]
    xc = va - jnp.mean(va, axis=-1, keepdims=True)
    rs = lax.rsqrt(jnp.mean(xc * xc, axis=-1, keepdims=True) + EPS)
    vn = xc * rs
    vlb = (vn * gv + bv).astype(BF16)
    sg = jnp.concatenate(
        [_dot(wt[g], vlb[:, g * GROUP_DIM:(g + 1) * GROUP_DIM], NN) for g in range(A_GROUPS)], axis=1) + bsf
    return er, u, rs, vn, vlb, sg


def _masked_ws(ws_ref):
    mask = _tril_mask()
    return [jnp.where(mask, ws_ref[g], 0.0).astype(BF16) for g in range(A_GROUPS)]


def _mixer_a_fwd(proj, gv, bv, ws, bsf, ga, name, gather=None):
    s = proj.shape[0]
    ng = 0 if gather is None else len(gather)

    def body(p_ref, gv_ref, bv_ref, ws_ref, bs_ref, ga_ref, *rest):
        o_ref = rest[ng]
        if ng:
            start, relay, finish = _gather_steps(rest[:ng], rest[ng + 1:2 * ng + 1], *rest[2 * ng + 1:])
            first, last = _grid_edges((s // TR,))
            pl.when(first)(start)
        wt = _masked_ws(ws_ref)
        for ch in range(TR // CHUNK):
            rows = slice(ch * CHUNK, (ch + 1) * CHUNK)
            _, u, _, _, _, sg = _gating_forward(p_ref[rows, :].astype(F32), gv_ref[...], bv_ref[...], wt, bs_ref[...])
            oa = u * sg
            o_ref[rows, :] = (oa * _rsq_mean(oa) * ga_ref[...]).astype(BF16)

        if ng:
            @pl.when(last)
            def _():
                relay()
                finish()

    res = pl.pallas_call(
        body, grid=(s // TR,),
        in_specs=[_row_spec(2 * A_WIDTH), _vec_spec(A_WIDTH), _vec_spec(A_WIDTH),
                  pl.BlockSpec((A_GROUPS, CHUNK, CHUNK), lambda i: (0, 0, 0)),
                  pl.BlockSpec((CHUNK, A_WIDTH), lambda i: (0, 0)), _vec_spec(A_WIDTH)] + [ANY] * ng,
        out_specs=[_row_spec(A_WIDTH)] + [ANY] * ng,
        out_shape=[jax.ShapeDtypeStruct((s, A_WIDTH + B_WIDTH), BF16)] + _gathered_shapes(gather or []),
        scratch_shapes=_gather_sems(ng) if ng else [],
        compiler_params=_cparams("arbitrary" if ng else "parallel"), name=name)(proj, gv, bv, ws, bsf, ga,
                                                                              *(gather or []))
    return res[0], list(res[1:])


def _mixer_a_bwd(proj, dmixed, gv, bv, ws, bsf, ga, name, scatter=None):
    s = proj.shape[0]
    nsteps = s // TR
    ns = 0 if scatter is None else len(scatter[0])

    def body(*refs):
        p_ref, dm_ref, gv_ref, bv_ref, ws_ref, bs_ref, ga_ref = refs[:7]
        dp_ref, dga_ref, dgv_ref, dbv_ref, dbs_ref, dws_ref = refs[7 + ns:13 + ns]
        if ns:
            start, finish = _scatter_steps(refs[7:7 + ns], refs[13 + ns:13 + 2 * ns], *refs[13 + 2 * ns:], scatter[1])
            first, last = _grid_edges((nsteps,))
            pl.when(first)(start)
        _acc_init([dga_ref, dgv_ref, dbv_ref, dbs_ref, dws_ref])
        mask = _tril_mask()
        wt = _masked_ws(ws_ref)
        gvv = gv_ref[...]
        gav = ga_ref[...]
        for ch in range(TR // CHUNK):
            rows = slice(ch * CHUNK, (ch + 1) * CHUNK)
            pa = p_ref[rows, :].astype(F32)
            er, u, rs, vn, vlb, sg = _gating_forward(pa, gvv, bv_ref[...], wt, bs_ref[...])
            oa = u * sg
            doa, dgt = _rms_bwd(oa, _rsq_mean(oa), gav, dm_ref[rows, :])
            dga_ref[...] += _colsum(dgt)
            du = doa * sg
            dsg = doa * u
            dbs_ref[...] += dsg
            dsgb = dsg.astype(BF16)
            dvl = []
            for g in range(A_GROUPS):
                cols = slice(g * GROUP_DIM, (g + 1) * GROUP_DIM)
                dws_ref[g] += jnp.where(mask, _dot(dsgb[:, cols], vlb[:, cols], NT), 0.0)
                dvl.append(_dot(wt[g], dsgb[:, cols], TN))
            dvl = jnp.concatenate(dvl, axis=1)
            dgv_ref[...] += _colsum(dvl * vn)
            dbv_ref[...] += _colsum(dvl)
            dvn = dvl * gvv
            dva = rs * (dvn - jnp.mean(dvn, axis=-1, keepdims=True)
                        - vn * jnp.mean(dvn * vn, axis=-1, keepdims=True))
            gp = 0.5 * (1.0 + er) + pa * jnp.exp(-0.5 * pa * pa) * INV_SQRT_2PI
            dp_ref[rows, :] = (jnp.concatenate([du, dva], axis=1) * gp).astype(BF16)

        @pl.when(pl.program_id(0) == nsteps - 1)
        def _():
            for g in range(A_GROUPS):
                cols = slice(g * GROUP_DIM, (g + 1) * GROUP_DIM)
                tot = jnp.sum(dbs_ref[:, cols], axis=1, keepdims=True)
                dbs_ref[:, cols] = jnp.broadcast_to(tot, (CHUNK, GROUP_DIM))

        if ns:
            pl.when(last)(finish)

    full = lambda *shape: pl.BlockSpec(shape, lambda i: (0,) * len(shape))
    res = pl.pallas_call(
        body, grid=(nsteps,),
        in_specs=[_row_spec(2 * A_WIDTH), _row_spec(A_WIDTH), _vec_spec(A_WIDTH), _vec_spec(A_WIDTH),
                  full(A_GROUPS, CHUNK, CHUNK), full(CHUNK, A_WIDTH), _vec_spec(A_WIDTH)] + [ANY] * ns,
        out_specs=[_row_spec(2 * A_WIDTH), _vec_spec(A_WIDTH), _vec_spec(A_WIDTH), _vec_spec(A_WIDTH),
                   full(CHUNK, A_WIDTH), full(A_GROUPS, CHUNK, CHUNK)] + [ANY] * ns,
        out_shape=[jax.ShapeDtypeStruct((s, IN_COLS), BF16), jax.ShapeDtypeStruct((1, A_WIDTH), F32),
                   jax.ShapeDtypeStruct((1, A_WIDTH), F32), jax.ShapeDtypeStruct((1, A_WIDTH), F32),
                   jax.ShapeDtypeStruct((CHUNK, A_WIDTH), F32),
                   jax.ShapeDtypeStruct((A_GROUPS, CHUNK, CHUNK), F32)]
        + (_scattered_shapes(scatter[1]) if ns else []),
        scratch_shapes=_scatter_sems(ns) if ns else [],
        compiler_params=_cparams("arbitrary"), name=name)(proj, dmixed, gv, bv, ws, bsf, ga,
                                                          *(scatter[0] if ns else []))
    return res[:6] + (list(res[6:]),)


def _rope_tables(s):
    half = ROT_DIM // 2
    lane = jnp.arange(LANES) % HEAD_DIM
    inv = ROPE_THETA ** (-(2 * (lane % half)).astype(F32) / ROT_DIM)
    ang = jnp.arange(s, dtype=F32)[:, None] * inv[None, :]
    cos, sin = jnp.cos(ang), jnp.sin(ang)
    c = jnp.where(lane < ROT_DIM, cos, 1.0)
    s1 = jnp.where(lane < half, -sin, 0.0)
    s2 = jnp.where((lane >= half) & (lane < ROT_DIM), sin, 0.0)
    return c, s1, s2


def _lane_blocks(width):
    return [slice(b * LANES, (b + 1) * LANES) for b in range(width // LANES)]


CLASS_DILS = tuple(d for d in DILATIONS if d > 1)


def _class_shape(s, dil, dtype):
    return jax.ShapeDtypeStruct((dil, s // dil, B_WIDTH), dtype)


def _class_spec(dil):
    return pl.BlockSpec((dil, TR // dil, B_WIDTH), lambda i, *_: (0, i, 0))


NBLK = B_WIDTH // LANES
STAGE = pltpu.VMEM((NBLK, TR, LANES), F32)


def _stage_put(stage, value):
    for b, sl in enumerate(_lane_blocks(B_WIDTH)):
        stage[b] = value[:, sl]


def _stage_get(stage):
    return jnp.concatenate([stage[b] for b in range(NBLK)], axis=1)


def _store_classes(stage, dst_ref, dil):
    for b, sl in enumerate(_lane_blocks(B_WIDTH)):
        for r in range(dil):
            dst_ref[r, :, sl] = stage[b, pl.ds(r, TR // dil, stride=dil), :].astype(dst_ref.dtype)


def _load_classes(src_ref, stage, dil):
    for b, sl in enumerate(_lane_blocks(B_WIDTH)):
        for r in range(dil):
            stage[b, pl.ds(r, TR // dil, stride=dil), :] = src_ref[r, :, sl].astype(F32)
    return _stage_get(stage)


def _rope_fwd(proj, tabs, name, gather=None):
    s = proj.shape[0]
    half = ROT_DIM // 2
    scale = HEAD_DIM ** -0.5
    nlay = 1 + len(CLASS_DILS)
    ng = 0 if gather is None else len(gather)

    def body(q_ref, k_ref, v_ref, c_ref, s1_ref, s2_ref, *rest):
        outs, stage = rest[ng:ng + 3 * nlay], rest[2 * ng + 3 * nlay]
        if ng:
            start, relay, finish = _gather_steps(rest[:ng], rest[ng + 3 * nlay:2 * ng + 3 * nlay],
                                                 *rest[2 * ng + 3 * nlay + 1:])
            first, last = _grid_edges((s // TR,))
            pl.when(first)(start)
        c, s1, s2 = c_ref[...], s1_ref[...], s2_ref[...]
        for which, (src, mul) in enumerate(((q_ref, scale), (k_ref, 1.0), (v_ref, None))):
            if mul is None:
                _stage_put(stage, src[...].astype(F32))
            else:
                for b, sl in enumerate(_lane_blocks(B_WIDTH)):
                    a = src[:, sl].astype(F32)
                    r = a * c + pltpu.roll(a, LANES - half, 1) * s1 + pltpu.roll(a, half, 1) * s2
                    stage[b] = r * mul
            dst = outs[which * nlay:(which + 1) * nlay]
            dst[0][...] = _stage_get(stage).astype(BF16)
            for ref, d in zip(dst[1:], CLASS_DILS):
                _store_classes(stage, ref, d)

        if ng:
            @pl.when(last)
            def _():
                relay()
                finish()

    tab = pl.BlockSpec((TR, LANES), lambda i: (i, 0))
    lay_specs = [_row_spec(B_WIDTH)] + [_class_spec(d) for d in CLASS_DILS]
    lay_shapes = [jax.ShapeDtypeStruct((s, B_WIDTH), BF16)] + [_class_shape(s, d, BF16) for d in CLASS_DILS]
    outs = pl.pallas_call(
        body, grid=(s // TR,),
        in_specs=[_row_spec(B_WIDTH, 2), _row_spec(B_WIDTH, 3), _row_spec(B_WIDTH, 4), tab, tab, tab] + [ANY] * ng,
        out_specs=lay_specs * 3 + [ANY] * ng, out_shape=lay_shapes * 3 + _gathered_shapes(gather or []),
        scratch_shapes=[STAGE] + (_gather_sems(ng) if ng else []),
        compiler_params=_cparams("arbitrary" if ng else "parallel"), name=name)(proj, proj, proj, *tabs,
                                                                              *(gather or []))
    q, k, v = (dict(zip(DILATIONS, outs[w * nlay:(w + 1) * nlay])) for w in range(3))
    return q, k, v, list(outs[3 * nlay:])


def _as_classes(t):
    return t if t.ndim == 3 else t[None]


def _head_masks():
    lane = lax.broadcasted_iota(jnp.int32, (1, LANES), 1)
    return lane < HEAD_DIM, lane >= HEAD_DIM


def _stack_heads(t):
    lo, hi = _head_masks()
    zero = jnp.zeros_like(t)
    return jnp.concatenate([jnp.where(lo, t, zero), jnp.where(hi, t, zero)], axis=0)


MAX_SEGMENT_BLOCKS = 8


def _segment_masks(j):
    qi = lax.broadcasted_iota(jnp.int32, (BAND, 2 * BAND), 0)
    kj = lax.broadcasted_iota(jnp.int32, (BAND, 2 * BAND), 1)
    both = (kj >= qi) & (kj <= qi + BAND)
    own = kj[:, :BAND] <= qi[:, :BAND]
    head = both & ((kj >= BAND) | (j > 0))
    return tuple(jnp.concatenate([m, m], axis=0) for m in (own, both, head))


def _block_rows(g):
    return pl.ds(pl.multiple_of(g * BAND, BAND), BAND)


def _key_rows(g):
    return pl.ds(pl.multiple_of((g - 1) * BAND, BAND), 2 * BAND)


def _segments(n):
    nb = n // BAND
    seg = min(nb, MAX_SEGMENT_BLOCKS)
    return seg, nb // seg


def _segment_specs(seg):
    main = pl.BlockSpec((None, seg * BAND, B_WIDTH), lambda r, j: (r, j, 0))
    halo = pl.BlockSpec((None, BAND, B_WIDTH), lambda r, j: (r, jnp.maximum(j * seg - 1, 0), 0))
    return main, halo


def _attn_fwd(q, k, v, name, gather=None):
    dil, n, _ = q.shape
    seg, nseg = _segments(n)
    nh = 2 if nseg > 1 else 0
    ng = 0 if gather is None else len(gather)

    def body(*refs):
        q_ref, k_ref, v_ref = refs[:3]
        halos = refs[3:3 + nh]
        o_ref, l_ref = refs[3 + nh + ng:5 + nh + ng]
        if ng:
            start, relay, finish = _gather_steps(refs[3 + nh:3 + nh + ng], refs[5 + nh + ng:5 + nh + 2 * ng],
                                                 *refs[5 + nh + 2 * ng:])
            first, last = _grid_edges((dil, nseg))
            pl.when(first)(start)
        own, both, head = _segment_masks(pl.program_id(1))
        lo, _ = _head_masks()

        def block(rows, keys_of, valid):
            for sl in _lane_blocks(B_WIDTH):
                kk, vv = keys_of(sl)
                sc = jnp.where(valid, _dot(_stack_heads(q_ref[rows, sl]), kk, NT), NEG_INF)
                mx = jnp.max(sc, axis=1, keepdims=True)
                p = jnp.exp(sc - mx)
                den = jnp.sum(p, axis=1, keepdims=True)
                out = _dot(p.astype(BF16), vv, NN) / den
                lse = mx + jnp.log(den)
                o_ref[rows, sl] = jnp.where(lo, out[:BAND], out[BAND:]).astype(BF16)
                l_ref[rows, sl] = jnp.where(lo, lse[:BAND], lse[BAND:])

        if nh:
            block(_block_rows(0), lambda sl: (jnp.concatenate([halos[0][:, sl], k_ref[0:BAND, sl]], axis=0),
                                              jnp.concatenate([halos[1][:, sl], v_ref[0:BAND, sl]], axis=0)), head)
        else:
            block(_block_rows(0), lambda sl: (k_ref[0:BAND, sl], v_ref[0:BAND, sl]), own)

        @pl.loop(1, seg)
        def _(g):
            block(_block_rows(g), lambda sl: (k_ref[_key_rows(g), sl], v_ref[_key_rows(g), sl]), both)

        if ng:
            @pl.when(last)
            def _():
                relay()
                finish()

    main, halo = _segment_specs(seg)
    res = pl.pallas_call(
        body, grid=(dil, nseg), in_specs=[main] * 3 + [halo] * nh + [ANY] * ng, out_specs=[main, main] + [ANY] * ng,
        out_shape=[jax.ShapeDtypeStruct((dil, n, B_WIDTH), BF16), jax.ShapeDtypeStruct((dil, n, B_WIDTH), F32)]
        + _gathered_shapes(gather or []),
        scratch_shapes=_gather_sems(ng) if ng else [],
        compiler_params=_cparams(*(["arbitrary"] * 2 if ng else ["parallel"] * 2)), name=name)(
            q, k, v, *([k, v] if nh else []), *(gather or []))
    return res[0], res[1], list(res[2:])


def _attn_bwd(q, k, v, do, lse, delta, name, scatter=None):
    dil, n, _ = q.shape
    seg, nseg = _segments(n)
    nh = 2 if nseg > 1 else 0
    ns = 0 if scatter is None else len(scatter[0])

    def body(*refs):
        q_ref, k_ref, v_ref, do_ref, lse_ref, dl_ref = refs[:6]
        halos = refs[6:6 + nh]
        dq_ref, dk_ref, dv_ref = refs[6 + nh + ns:9 + nh + ns]
        halo_out = refs[9 + nh + ns:9 + 2 * nh + ns]
        ck_ref, cv_ref = refs[9 + 2 * nh + 2 * ns:11 + 2 * nh + 2 * ns]
        if ns:
            start, finish = _scatter_steps(refs[6 + nh:6 + nh + ns], refs[9 + 2 * nh + ns:9 + 2 * nh + 2 * ns],
                                           *refs[11 + 2 * nh + 2 * ns:], scatter[1])
            first, last = _grid_edges((dil, nseg))
            pl.when(first)(start)
        own, both, head = _segment_masks(pl.program_id(1))
        lo, _ = _head_masks()
        lane = lax.broadcasted_iota(jnp.int32, (1, LANES), 1)

        def per_head(t):
            return jnp.concatenate(
                [jnp.sum(jnp.where(lane == first, t, 0.0), axis=1, keepdims=True) for first in (0, HEAD_DIM)], axis=0)

        def grads(rows, kk, vv, valid, sl):
            q2 = _stack_heads(q_ref[rows, sl])
            do2 = _stack_heads(do_ref[rows, sl])
            p = jnp.where(valid, jnp.exp(_dot(q2, kk, NT) - per_head(lse_ref[rows, sl])), 0.0)
            ds = (p * (_dot(do2, vv, NT) - per_head(dl_ref[rows, sl]))).astype(BF16)
            dq = _dot(ds, kk, NN)
            dq_ref[rows, sl] = jnp.where(lo, dq[:BAND], dq[BAND:]).astype(BF16)
            return _dot(ds, q2, TN), _dot(p.astype(BF16), do2, TN)

        for sl in _lane_blocks(B_WIDTH):
            if nh:
                dkk, dvv = grads(_block_rows(0), jnp.concatenate([halos[0][:, sl], k_ref[0:BAND, sl]], axis=0),
                                 jnp.concatenate([halos[1][:, sl], v_ref[0:BAND, sl]], axis=0), head, sl)
                halo_out[0][:, sl], halo_out[1][:, sl] = dkk[:BAND], dvv[:BAND]
                ck_ref[:, sl], cv_ref[:, sl] = dkk[BAND:], dvv[BAND:]
            else:
                ck_ref[:, sl], cv_ref[:, sl] = grads(_block_rows(0), k_ref[0:BAND, sl], v_ref[0:BAND, sl], own, sl)

        @pl.loop(1, seg)
        def _(g):
            before = _block_rows(g - 1)
            for sl in _lane_blocks(B_WIDTH):
                dkk, dvv = grads(_block_rows(g), k_ref[_key_rows(g), sl], v_ref[_key_rows(g), sl], both, sl)
                dk_ref[before, sl] = (ck_ref[:, sl] + dkk[:BAND]).astype(BF16)
                dv_ref[before, sl] = (cv_ref[:, sl] + dvv[:BAND]).astype(BF16)
                ck_ref[:, sl] = dkk[BAND:]
                cv_ref[:, sl] = dvv[BAND:]

        final = pl.ds((seg - 1) * BAND, BAND)
        dk_ref[final, :] = ck_ref[...].astype(BF16)
        dv_ref[final, :] = cv_ref[...].astype(BF16)

        if ns:
            pl.when(last)(finish)

    main, halo = _segment_specs(seg)
    shape = jax.ShapeDtypeStruct((dil, n, B_WIDTH), BF16)
    halo_shape = jax.ShapeDtypeStruct((dil, nseg, BAND, B_WIDTH), F32)
    halo_spec = pl.BlockSpec((None, None, BAND, B_WIDTH), lambda r, j: (r, j, 0, 0))
    res = pl.pallas_call(
        body, grid=(dil, nseg), in_specs=[main] * 6 + [halo] * nh + [ANY] * ns,
        out_specs=[main] * 3 + [halo_spec] * nh + [ANY] * ns,
        out_shape=[shape] * 3 + [halo_shape] * nh + (_scattered_shapes(scatter[1]) if ns else []),
        scratch_shapes=[pltpu.VMEM((BAND, B_WIDTH), F32)] * 2 + (_scatter_sems(ns) if ns else []),
        compiler_params=_cparams(*(["arbitrary"] * 2 if ns else ["parallel"] * 2)), name=name)(
            q, k, v, do, lse, delta, *([k, v] if nh else []), *(scatter[0] if ns else []))
    return res[0], res[1], res[2], (tuple(res[3:3 + nh]) if nh else None), list(res[3 + nh:])


def _attn_combine(outs, lses, gb, mixed, name, gather=None):
    s = mixed.shape[0]
    npat = len(DILATIONS)
    w = B_WIDTH
    ng = 0 if gather is None else len(gather)

    def body(*refs):
        o_refs, l_refs = refs[:npat], refs[npat:2 * npat]
        g_ref = refs[2 * npat]
        ob_ref = refs[2 * npat + 2 + ng]
        lse_refs = refs[2 * npat + 3 + ng:3 * npat + 3 + ng]
        mb_ref = refs[3 * npat + 3 + ng]
        stage = refs[3 * npat + 4 + 2 * ng]
        if ng:
            start, relay, finish = _gather_steps(refs[2 * npat + 2:2 * npat + 2 + ng],
                                                 refs[3 * npat + 4 + ng:3 * npat + 4 + 2 * ng],
                                                 *refs[3 * npat + 5 + 2 * ng:])
            first, last = _grid_edges((s // TR,))
            pl.when(first)(start)
        os_ = [o_refs[0][...].astype(F32)] + [_load_classes(r, stage, d) for r, d in zip(o_refs[1:], CLASS_DILS)]
        ls = [l_refs[0][...]] + [_load_classes(r, stage, d) for r, d in zip(l_refs[1:], CLASS_DILS)]
        mx = functools.reduce(jnp.maximum, ls)
        ws = [jnp.exp(l - mx) for l in ls]
        tot = functools.reduce(lambda a, b: a + b, ws)
        ob = functools.reduce(lambda a, b: a + b, [wt / tot * o for wt, o in zip(ws, os_)])
        ob_ref[...] = ob
        lse = mx + jnp.log(tot)
        _stage_put(stage, lse)
        lse_refs[0][...] = lse
        for ref, d in zip(lse_refs[1:], CLASS_DILS):
            _store_classes(stage, ref, d)
        mb_ref[...] = (ob * _rsq_mean(ob) * g_ref[...]).astype(BF16)

        if ng:
            @pl.when(last)
            def _():
                relay()
                finish()

    lay_specs = [_row_spec(w)] + [_class_spec(d) for d in CLASS_DILS]
    res = pl.pallas_call(
        body, grid=(s // TR,), in_specs=lay_specs * 2 + [_vec_spec(w), ANY] + [ANY] * ng,
        out_specs=[_row_spec(w)] + lay_specs + [_row_spec(w, 1)] + [ANY] * ng,
        out_shape=[jax.ShapeDtypeStruct((s, w), F32), jax.ShapeDtypeStruct((s, w), F32)]
        + [_class_shape(s, d, F32) for d in CLASS_DILS] + [jax.ShapeDtypeStruct(mixed.shape, mixed.dtype)]
        + _gathered_shapes(gather or []),
        scratch_shapes=[STAGE] + (_gather_sems(ng) if ng else []), input_output_aliases={2 * npat + 1: npat + 1},
        compiler_params=_cparams("arbitrary" if ng else "parallel"), name=name)(*outs, *lses, gb, mixed,
                                                                              *(gather or []))
    return res[0], dict(zip(DILATIONS, res[1:npat + 1])), res[npat + 1], list(res[npat + 2:])


def _attn_bwd_prep(dmixed, ob, gb, name):
    s = ob.shape[0]
    w = B_WIDTH
    nlay = len(DILATIONS)

    def body(dm_ref, ob_ref, g_ref, *rest):
        do_refs, dl_refs = rest[:nlay], rest[nlay:2 * nlay]
        dg_ref, stage = rest[2 * nlay:]
        _acc_init([dg_ref])
        ob = ob_ref[...]
        dob, dgt = _rms_bwd(ob, _rsq_mean(ob), g_ref[...], dm_ref[...])
        dg_ref[...] += _colsum(dgt)
        _stage_put(stage, dob)
        do_refs[0][...] = dob.astype(BF16)
        for ref, d in zip(do_refs[1:], CLASS_DILS):
            _store_classes(stage, ref, d)
        lo, hi = _head_masks()
        t = dob * ob
        for b, sl in enumerate(_lane_blocks(w)):
            tb = t[:, sl]
            s0 = jnp.sum(jnp.where(lo, tb, 0.0), axis=1, keepdims=True)
            s1 = jnp.sum(jnp.where(hi, tb, 0.0), axis=1, keepdims=True)
            stage[b] = jnp.where(lo, s0, s1)
        dl_refs[0][...] = _stage_get(stage)
        for ref, d in zip(dl_refs[1:], CLASS_DILS):
            _store_classes(stage, ref, d)

    lay_specs = [_row_spec(w)] + [_class_spec(d) for d in CLASS_DILS]
    shapes = lambda dt: [jax.ShapeDtypeStruct((s, w), dt)] + [_class_shape(s, d, dt) for d in CLASS_DILS]
    res = pl.pallas_call(
        body, grid=(s // TR,), in_specs=[_row_spec(w, 1), _row_spec(w), _vec_spec(w)],
        out_specs=lay_specs * 2 + [_vec_spec(w)],
        out_shape=shapes(BF16) + shapes(F32) + [jax.ShapeDtypeStruct((1, w), F32)],
        scratch_shapes=[STAGE],
        compiler_params=_cparams("arbitrary"), name=name)(dmixed, ob, gb)
    return dict(zip(DILATIONS, res[:nlay])), dict(zip(DILATIONS, res[nlay:2 * nlay])), res[2 * nlay]


def _rope_bwd(dqs, dks, dvs, halos, tabs, dproj, name):
    s = dproj.shape[0]
    half = ROT_DIM // 2
    scale = HEAD_DIM ** -0.5
    npat = len(DILATIONS)
    w = B_WIDTH
    nseg = halos[0].shape[0]
    per = s // nseg // TR

    def body(*refs):
        groups = [refs[g * npat:(g + 1) * npat] for g in range(3)]
        halo_refs = (None,) + tuple(refs[3 * npat:3 * npat + 2])
        c_ref, s1_ref, s2_ref, _, o_ref, stage = refs[3 * npat + 2:]
        i = pl.program_id(0)
        at_edge = ((i + 1) % per == 0) & ((i + 1) // per < nseg)

        def total(rs, halo_ref=None):
            acc = rs[0][...].astype(F32)
            if halo_ref is not None:
                edge = jnp.concatenate([jnp.zeros((TR - BAND, w), F32), halo_ref[...]], axis=0)
                acc = acc + jnp.where(at_edge, edge, 0.0)
            for ref, d in zip(rs[1:], CLASS_DILS):
                acc = acc + _load_classes(ref, stage, d)
            return acc

        def unrope(g):
            c, s1, s2 = c_ref[...], s1_ref[...], s2_ref[...]
            for sl in _lane_blocks(w):
                gb = g[:, sl]
                o = gb * c + pltpu.roll(gb * s1, half, 1) + pltpu.roll(gb * s2, LANES - half, 1)
                o_ref[:, sl] = o.astype(BF16)

        which = pl.program_id(1)

        @pl.when(which == 0)
        def _():
            unrope(total(groups[0]) * scale)

        @pl.when(which == 1)
        def _():
            unrope(total(groups[1], halo_refs[1]))

        @pl.when(which == 2)
        def _():
            o_ref[...] = total(groups[2], halo_refs[2]).astype(BF16)

    tab = pl.BlockSpec((TR, LANES), lambda i, j: (i, 0))
    nat = pl.BlockSpec((TR, w), lambda i, j: (i, 0))
    lay_specs = [nat] + [_class_spec(d) for d in CLASS_DILS]
    edge_spec = pl.BlockSpec((None, BAND, w), lambda i, j: (jnp.minimum((i + 1) // per, nseg - 1), 0, 0))
    first_col = 2 * A_WIDTH // w
    return pl.pallas_call(
        body, grid=(s // TR, 3), in_specs=lay_specs * 3 + [edge_spec] * 2 + [tab] * 3 + [ANY],
        out_specs=pl.BlockSpec((TR, w), lambda i, j: (i, first_col + j)),
        out_shape=jax.ShapeDtypeStruct(dproj.shape, dproj.dtype), scratch_shapes=[STAGE],
        input_output_aliases={3 * npat + 5: 0},
        compiler_params=_cparams("parallel", "arbitrary"), name=name)(*dqs, *dks, *dvs, *halos, *tabs, dproj)


TK = 512
HALO = 16
FFN_ROWS = 256
FFN_CHUNKS = tuple(slice(r, r + FFN_ROWS) for r in range(0, TM, FFN_ROWS))


def _row_of(v, r):
    rows = lax.broadcasted_iota(jnp.int32, (v.shape[0], 1), 0)
    return jnp.sum(jnp.where(rows == r, v, 0.0), axis=0, keepdims=True)


def _taps_before(x, halo):
    row = lax.broadcasted_iota(jnp.int32, (x.shape[0], 1), 0)
    m1 = jnp.where(row == 0, _row_of(halo, HALO - 1), pltpu.roll(x, 1, 0))
    m2 = jnp.where(row == 0, _row_of(halo, HALO - 2), jnp.where(row == 1, _row_of(halo, HALO - 1), pltpu.roll(x, 2, 0)))
    return m2, m1, x


def _taps_after(x, halo):
    rows = x.shape[0]
    row = lax.broadcasted_iota(jnp.int32, (rows, 1), 0)
    p1 = jnp.where(row == rows - 1, _row_of(halo, 0), pltpu.roll(x, rows - 1, 0))
    p2 = jnp.where(row == rows - 2, _row_of(halo, 0), jnp.where(row == rows - 1, _row_of(halo, 1), pltpu.roll(x, rows - 2, 0)))
    return p1, p2


def _conv_value(taps, cw_ref, cb_ref, h):
    return cb_ref[h] + cw_ref[h, 0:1, :] * taps[0] + cw_ref[h, 1:2, :] * taps[1] + cw_ref[h, 2:3, :] * taps[2]


def _ffn_weight_specs(ncol):
    per_up = (2 * D_FF // N_CHIPS) // TK
    per_dn = (D_FF // N_CHIPS) // TK
    wg = pl.BlockSpec((None, None, D_MODEL, TK), lambda i, j: (j // per_up, 0, 0, j % per_up))
    wv = pl.BlockSpec((None, None, D_MODEL, TK), lambda i, j: ((j + ncol) // per_up, 0, 0, (j + ncol) % per_up))
    wd = pl.BlockSpec((None, None, TK, D_MODEL), lambda i, j: (j // per_dn, 0, j % per_dn, 0))
    cw = pl.BlockSpec((2, 3, TK), lambda i, j: (0, 0, j))
    cb = pl.BlockSpec((2, 1, TK), lambda i, j: (0, 0, j))
    return wg, wv, wd, cw, cb


def _ffn_forward(h2, w_up, w_down, cw3, cb3, name, gather=None, post=None):
    s = h2.shape[0]
    nm, ncol = s // TM, D_FF // TK
    ng = 0 if gather is None else len(gather)
    npost = 0 if post is None else 3
    nout = 4 + (2 if post else 0)

    def body(*refs):
        h_ref, wg_ref, wv_ref, wd_ref, cw_ref, cb_ref = refs[:6]
        post_in = refs[6:6 + npost]
        g_in = refs[6 + npost:6 + npost + ng]
        outs = refs[6 + npost + ng:6 + npost + ng + nout]
        y_ref, up_ref, cv_ref, f_ref = outs[:4]
        g_out = refs[6 + npost + ng + nout:6 + npost + 2 * ng + nout]
        carry = refs[6 + npost + 2 * ng + nout]
        i, j = pl.program_id(0), pl.program_id(1)
        if ng:
            start, relay, finish = _gather_steps(g_in, g_out, *refs[7 + npost + 2 * ng + nout:])
            pl.when((i == 0) & (j == 0))(start)
            pl.when((i == nm - 1) & (j == 0))(relay)

        @pl.when((i == 0) & (j == 0))
        def _():
            carry[...] = jnp.zeros_like(carry)

        @pl.when(j == 0)
        def _():
            f_ref[...] = jnp.zeros_like(f_ref)

        ups = []
        for rs in FFN_CHUNKS:
            hc = h_ref[rs, :]
            ups.append([_dot(hc, w_ref[...], NN).astype(BF16) for w_ref in (wg_ref, wv_ref)])
            for hh in range(2):
                up_ref[hh, rs, :] = ups[-1][hh]
        before = [carry[j, hh] for hh in range(2)]
        for rs, up in zip(FFN_CHUNKS, ups):
            conv = []
            for hh in range(2):
                x = up[hh].astype(F32)
                conv.append(_conv_value(_taps_before(x, before[hh]), cw_ref, cb_ref, hh))
                cv_ref[hh, rs, :] = conv[hh].astype(BF16)
                before[hh] = x[x.shape[0] - HALO:, :]
            y = (_gelu_tanh(conv[0])[0] * conv[1]).astype(BF16)
            y_ref[rs, :] = y
            f_ref[rs, :] += _dot(y, wd_ref[...], NN)
        for hh in range(2):
            carry[j, hh] = before[hh]

        @pl.when(j == ncol - 1)
        def _():
            if post:
                f = f_ref[...]
                x1_ref, gp_ref, gn_ref = post_in
                x2 = x1_ref[...] + f * _rsq_mean(f) * gp_ref[...]
                outs[4][...] = x2
                outs[5][...] = (x2 * _rsq_mean(x2) * gn_ref[...]).astype(BF16)

        if ng:
            pl.when((i == nm - 1) & (j == ncol - 1))(finish)

    wg, wv, wd, cw, cb = _ffn_weight_specs(ncol)
    row = pl.BlockSpec((TM, D_MODEL), lambda i, j: (i, 0))
    vec = pl.BlockSpec((1, D_MODEL), lambda i, j: (0, 0))
    res = pl.pallas_call(
        body, grid=(nm, ncol),
        in_specs=[row, wg, wv, wd, cw, cb] + ([row, vec, vec] if post else []) + [ANY] * ng,
        out_specs=[pl.BlockSpec((TM, TK), lambda i, j: (i, j)), pl.BlockSpec((2, TM, TK), lambda i, j: (0, i, j)),
                   pl.BlockSpec((2, TM, TK), lambda i, j: (0, i, j)), row] + ([row, row] if post else [])
        + [ANY] * ng,
        out_shape=[jax.ShapeDtypeStruct((s, D_FF), BF16), jax.ShapeDtypeStruct((2, s, D_FF), BF16),
                   jax.ShapeDtypeStruct((2, s, D_FF), BF16), jax.ShapeDtypeStruct((s, D_MODEL), F32)]
        + ([jax.ShapeDtypeStruct((s, D_MODEL), F32), jax.ShapeDtypeStruct((s, D_MODEL), BF16)] if post else [])
        + _gathered_shapes(gather or []),
        scratch_shapes=[pltpu.VMEM((ncol, 2, HALO, TK), F32)] + (_gather_sems(ng) if ng else []),
        compiler_params=_cparams("arbitrary", "arbitrary"), name=name)(h2, w_up, w_up, w_down, cw3, cb3,
                                                                      *(post or []), *(gather or []))
    return res[:nout], list(res[nout:])


def _ffn_backward(df, w_up, w_down, up3, cv3, cw3, name, scatter=None):
    s = df.shape[0]
    nm, ncol = s // TM, D_FF // TK
    ns = 0 if scatter is None else len(scatter[0])

    def body(*refs):
        df_ref, wg_ref, wv_ref, wd_ref, cw_ref, up_ref, cv_ref = refs[:7]
        s_in = refs[7:7 + ns]
        dup_ref, dh_ref, sums_ref = refs[7 + ns:10 + ns]
        s_out = refs[10 + ns:10 + 2 * ns]
        carry = refs[10 + 2 * ns]
        i, j = pl.program_id(0), pl.program_id(1)
        if ns:
            start, finish = _scatter_steps(s_in, s_out, *refs[11 + 2 * ns:], scatter[1])
            pl.when((i == 0) & (j == 0))(start)

        @pl.when((i == 0) & (j == 0))
        def _():
            carry[...] = jnp.zeros_like(carry)
            sums_ref[...] = jnp.zeros_like(sums_ref)

        @pl.when(j == 0)
        def _():
            dh_ref[...] = jnp.zeros_like(dh_ref)

        chunks = FFN_CHUNKS[::-1]
        dys = [_dot(df_ref[rs, :], wd_ref[...], NT) for rs in chunks]
        row = lax.broadcasted_iota(jnp.int32, (8, 1), 0)
        after = [carry[j, hh] for hh in range(2)]
        upd = [jnp.zeros((8, TK), F32) for _ in range(2)]
        for rs, dy in zip(chunks, dys):
            act, grad = _gelu_tanh(cv_ref[0, rs, :].astype(F32))
            dcs = (dy * cv_ref[1, rs, :].astype(F32) * grad, dy * act)
            part = dh_ref[rs, :]
            for hh, w_ref in ((0, wg_ref), (1, wv_ref)):
                dc = dcs[hh]
                x = up_ref[hh, rs, :].astype(F32)
                after1, after2 = _taps_after(dc, after[hh])
                for ridx, sm in enumerate((_colsum(after2 * x), _colsum(after1 * x), _colsum(dc * x), _colsum(dc))):
                    upd[hh] = upd[hh] + jnp.where(row == ridx, sm, 0.0)
                dup = (cw_ref[hh, 2:3, :] * dc + cw_ref[hh, 1:2, :] * after1 + cw_ref[hh, 0:1, :] * after2).astype(BF16)
                after[hh] = dc[:HALO, :]
                dup_ref[hh, rs, :] = dup
                part = part + _dot(dup, w_ref[...], NT)
            dh_ref[rs, :] = part
        for hh in range(2):
            sums_ref[j, hh] += upd[hh]
            carry[j, hh] = after[hh]

        if ns:
            pl.when((i == nm - 1) & (j == ncol - 1))(finish)

    wg, wv, wd, cw, _ = _ffn_weight_specs(ncol)
    rev = lambda i: nm - 1 - i
    res = pl.pallas_call(
        body, grid=(nm, ncol),
        in_specs=[pl.BlockSpec((TM, D_MODEL), lambda i, j: (rev(i), 0)), wg, wv, wd, cw,
                  pl.BlockSpec((2, TM, TK), lambda i, j: (0, rev(i), j)),
                  pl.BlockSpec((2, TM, TK), lambda i, j: (0, rev(i), j))] + [ANY] * ns,
        out_specs=[pl.BlockSpec((2, TM, TK), lambda i, j: (0, rev(i), j)),
                   pl.BlockSpec((TM, D_MODEL), lambda i, j: (rev(i), 0)),
                   pl.BlockSpec((ncol, 2, 8, TK), lambda i, j: (0, 0, 0, 0))] + [ANY] * ns,
        out_shape=[jax.ShapeDtypeStruct((2, s, D_FF), BF16), jax.ShapeDtypeStruct((s, D_MODEL), F32),
                   jax.ShapeDtypeStruct((ncol, 2, 8, TK), F32)] + (_scattered_shapes(scatter[1]) if ns else []),
        scratch_shapes=[pltpu.VMEM((ncol, 2, HALO, TK), F32)] + (_scatter_sems(ns) if ns else []),
        compiler_params=_cparams("arbitrary", "arbitrary"), name=name)(df, w_up, w_up, w_down, cw3, up3, cv3,
                                                                      *(scatter[0] if ns else []))
    return res[:3], list(res[3:])


def _wspec(rows, cols, index_map):
    return pl.BlockSpec((None, None, rows, cols), index_map)


def _layer_forward(l, x0, h1, p, wg, tabs, gather=None, late=None, g_next=None):
    s = x0.shape[0]
    nm = s // TMM
    tag = f"_l{l}"
    riders = dict.fromkeys(DILATIONS)
    proj_rider = rope_rider = combine_rider = None
    if late is not None:
        cols = lambda t, parts: [t[:, i * t.shape[1] // parts:(i + 1) * t.shape[1] // parts] for i in range(parts)]
        (down_a, down_b), up_q = cols(late["w_down"], 2), cols(late["w_up"], 4)
        proj_rider, rope_rider, combine_rider = [late["w_out"], down_a], [up_q[2]], [up_q[3]]
        riders = dict(zip(DILATIONS, ([down_b], [up_q[0]], [up_q[1]])))
    proj = _matmul(
        h1, wg["w_in"], grid=(nm, N_CHIPS), a_spec=pl.BlockSpec((TMM, D_MODEL), lambda i, j: (i, 0)),
        b_spec=_wspec(D_MODEL, IN_COLS // N_CHIPS, lambda i, j: (j, 0, 0, 0)),
        o_spec=pl.BlockSpec((TMM, IN_COLS // N_CHIPS), lambda i, j: (i, j)), o_shape=(s, IN_COLS), o_dtype=BF16,
        dims=NN, nk=1, kaxis=None, acc_shape=None, name="proj" + tag, gather=proj_rider)
    if late is not None:
        proj, (w_out_all4, down_a) = proj
    ma, next_out = _mixer_a_fwd(proj, p["v_norm_g"], p["v_norm_b"], p["w_spatial"], p["bs_full"], p["out_norm_a"],
                                "mixer_a_fwd" + tag, [gather["w_out"]] if gather else None)
    q, k, v, rope_landed = _rope_fwd(proj, tabs, "rope_fwd" + tag, rope_rider)
    outs, lses, landed = zip(*[
        _attn_fwd(_as_classes(q[d]), _as_classes(k[d]), _as_classes(v[d]), f"attn_fwd_d{d}" + tag, riders[d])
        for d in DILATIONS])
    outs = [o.reshape(s, B_WIDTH) if d == 1 else o for o, d in zip(outs, DILATIONS)]
    lses = [t.reshape(s, B_WIDTH) if d == 1 else t for t, d in zip(lses, DILATIONS)]
    ob, lse, mixed, combine_landed = _attn_combine(outs, lses, p["out_norm_b"], ma, "attn_combine" + tag,
                                                   combine_rider)
    if late is not None:
        wg = dict(wg, w_out=w_out_all4, w_down=jnp.concatenate([down_a, landed[0][0]], axis=-1),
                  w_up=jnp.concatenate([landed[1][0], landed[2][0], rope_landed[0], combine_landed[0]], axis=-1))
    (y1, x1, h2), next_in = _mix_out_norm(mixed, wg["w_out"], x0, p["post_mix_norm"], p["pre_ffn_norm"],
                                          "mix_out" + tag, [gather["w_in"]] if gather else None)
    post = None if g_next is None else (x1, p["post_ffn_norm"], g_next)
    (y, up3, cv3, f, *after), next_ffn = _ffn_forward(h2, wg["w_up"], wg["w_down"], p["cw3"], p["cb3"], "ffn_fwd" + tag,
                                                      [gather["w_up"], gather["w_down"]] if gather else None, post)
    gathered = dict(w_in=next_in[0], w_out=next_out[0], w_up=next_ffn[0], w_down=next_ffn[1]) if gather else None
    saved = dict(x0=x0, h1=h1, proj=proj, q=q, k=k, v=v, ob=ob, lse=lse, mixed=mixed, y1=y1, x1=x1, h2=h2,
                 up3=up3, cv3=cv3, y=y, f=f)
    if after:
        saved.update(x2=after[0], h_next=after[1])
    return saved, gathered, wg


def _layer_backward(l, dx2, df, sv, p, wg, tabs, pos, scatter=None, hide=False):
    s = dx2.shape[0]
    nm = s // TMM
    tag = f"_l{l}"
    g = {}
    (dup3, dh2, conv_sums), scattered = _ffn_backward(df, wg["w_up"], wg["w_down"], sv["up3"], sv["cv3"], p["cw3"],
                                                      "ffn_bwd" + tag, scatter)
    sums = conv_sums.transpose(1, 2, 0, 3).reshape(2, 8, D_FF)
    g["conv_w"] = jnp.concatenate([sums[0, :3], sums[1, :3]], axis=1)
    g["conv_b"] = jnp.concatenate([sums[0, 3:4], sums[1, 3:4]], axis=1)
    tn = 1024
    done = {}
    gw_down = _matmul(
        sv["y"], df, grid=(D_FF // tn,), a_spec=pl.BlockSpec((s, tn), lambda k: (0, k)),
        b_spec=pl.BlockSpec((s, D_MODEL), lambda k: (0, 0)),
        o_spec=pl.BlockSpec((2, tn, D_MODEL // 2), lambda k: (0, k, 0)),
        o_shape=(2, D_FF, D_MODEL // 2), o_dtype=BF16,
        dims=TN, nk=1, kaxis=None, acc_shape=None, name="w_down_grad" + tag, halves=True)
    pair_sum = lambda n, grad, recv: _pair_sum({n: grad}, recv, pos, (n,), f"pair_sum_l{l}")
    gw_up, received = _matmul(
        sv["h2"], dup3, grid=(2 * D_FF // tn,), a_spec=pl.BlockSpec((s, D_MODEL), lambda n: (0, 0)),
        b_spec=pl.BlockSpec((None, s, tn), lambda n: (n // (D_FF // tn), 0, n % (D_FF // tn))),
        o_spec=pl.BlockSpec((None, D_MODEL, tn), lambda n: (n // 2, 0, n % 2)),
        o_shape=(N_CHIPS, D_MODEL, 2 * D_FF // N_CHIPS), o_dtype=BF16,
        dims=TN, nk=1, kaxis=None, acc_shape=None, name="w_up_grad" + tag, scatter=([gw_down], ("x:w_down",)))
    down_sums = pair_sum("w_down", gw_down, received)
    dx1, dy1, g["pre_ffn_norm"], g["post_mix_norm"] = _norm_bwd_mid(
        dx2, dh2, sv["x1"], p["pre_ffn_norm"], sv["y1"], p["post_mix_norm"], "norm_bwd_mid" + tag)
    w_out_all = pl.BlockSpec((N_CHIPS, None, D_MODEL // N_CHIPS, D_MODEL), lambda i: (0, 0, 0, 0))
    dmixed, received = _matmul(
        dy1, wg["w_out"], grid=(nm,), a_spec=pl.BlockSpec((TMM, D_MODEL), lambda i: (i, 0)), b_spec=w_out_all,
        o_spec=pl.BlockSpec((TMM, D_MODEL), lambda i: (i, 0)), o_shape=(s, D_MODEL), o_dtype=F32,
        dims=NT, nk=1, kaxis=None, acc_shape=None, name="mix_out_bwd" + tag, b_2d=(D_MODEL, D_MODEL),
        scatter=([gw_up], ("x:w_up",)))
    up_sums = pair_sum("w_up", gw_up, received)
    gw_out = _matmul(
        sv["mixed"], dy1, grid=(1,), a_spec=pl.BlockSpec((s, D_MODEL), lambda m: (0, 0)),
        b_spec=pl.BlockSpec((s, D_MODEL), lambda m: (0, 0)),
        o_spec=pl.BlockSpec((2, D_MODEL, D_MODEL // 2), lambda m: (0, 0, 0)),
        o_shape=(2, D_MODEL, D_MODEL // 2), o_dtype=BF16,
        dims=TN, nk=1, kaxis=None, acc_shape=None, name="w_out_grad" + tag, halves=True)
    dpa, g["out_norm_a"], g["v_norm_g"], g["v_norm_b"], dbs, g["w_spatial"], received = _mixer_a_bwd(
        sv["proj"], dmixed, p["v_norm_g"], p["v_norm_b"], p["w_spatial"], p["bs_full"], p["out_norm_a"],
        "mixer_a_bwd" + tag, ([gw_out], ("x:w_out",)))
    out_sums = pair_sum("w_out", gw_out, received)
    g["b_spatial"] = dbs[:, ::GROUP_DIM].T
    dob, delta, g["out_norm_b"] = _attn_bwd_prep(dmixed, sv["ob"], p["out_norm_b"], "attn_bwd_prep" + tag)
    riders = dict(zip(DILATIONS, ((down_sums, ("w_down",)), (up_sums, ("w_up:0",)), (up_sums, ("w_up:1",))))) if hide else {}
    dqs, dks, dvs, edges, received = zip(*[
        _attn_bwd(*(_as_classes(t[d]) for t in (sv["q"], sv["k"], sv["v"], dob, sv["lse"], delta)),
                  f"attn_bwd_d{d}" + tag, riders.get(d))
        for d in DILATIONS])
    if hide:
        done[("w_down",)] = (down_sums, received[0])
        done[("w_up",)] = (up_sums, [jnp.concatenate([received[1][0], received[2][0]], axis=-1)])
    nat = lambda ts: [t.reshape(s, B_WIDTH) if d == 1 else t for t, d in zip(ts, DILATIONS)]
    halos = [t[0] for t in edges[0]]
    dproj = _rope_bwd(nat(dqs), nat(dks), nat(dvs), halos, tabs, dpa, "rope_bwd" + tag)
    wcol = IN_COLS // N_CHIPS
    gw_in = _matmul(
        sv["h1"], dproj, grid=(N_CHIPS,), a_spec=pl.BlockSpec((s, D_MODEL), lambda n: (0, 0)),
        b_spec=pl.BlockSpec((s, wcol), lambda n: (0, n)),
        o_spec=pl.BlockSpec((None, D_MODEL, wcol), lambda n: (n, 0, 0)),
        o_shape=(N_CHIPS, D_MODEL, wcol), o_dtype=BF16,
        dims=TN, nk=1, kaxis=None, acc_shape=None, name="w_in_grad" + tag,
        scatter=(out_sums, ("w_out",)) if hide else None)
    if hide:
        gw_in, received = gw_in
        done[("w_out",)] = (out_sums, received)
        in_sums = _chip_sums(l, dict(w_in=gw_in), pos, ("w_in",))
        dh1, received = _proj_bwd(dproj, wg["w_in"], "proj_bwd" + tag, (in_sums, ("w_in",)))
        done[("w_in",)] = (in_sums, received)
        return dx1, dh1, {}, g, scattered, done
    dh1, received = _proj_bwd(dproj, wg["w_in"], "proj_bwd" + tag, ([gw_in], ("x:w_in",)))
    sums = dict(w_in=pair_sum("w_in", gw_in, received)[0], w_up=up_sums[0], w_out=out_sums[0], w_down=down_sums[0])
    return dx1, dh1, sums, g, scattered, done


SMALL = ("pre_mix_norm", "v_norm_g", "v_norm_b", "w_spatial", "b_spatial", "out_norm_a", "out_norm_b",
         "post_mix_norm", "pre_ffn_norm", "conv_b", "post_ffn_norm")
BIG = ("w_in", "w_out", "w_up", "w_down")
DEPTH = 2


def _layer_params(l, small, conv_w_full):
    p = {n: small[n][l].reshape(1, -1) for n in SMALL if n not in ("w_spatial", "b_spatial")}
    p["w_spatial"] = small["w_spatial"][l]
    p["bs_full"] = jnp.repeat(small["b_spatial"][l].T, GROUP_DIM, axis=1)
    p["cw3"] = conv_w_full[l].reshape(3, 2, D_FF).transpose(1, 0, 2)
    p["cb3"] = small["conv_b"][l].reshape(2, 1, D_FF)
    return p


def _mesh_pos():
    return lax.axis_index("x"), lax.axis_index("y"), lax.axis_index("c")


def _other_chips(x, y):
    return [(1 - x, y), (x, 1 - y), (1 - x, 1 - y)]


def _gathered_shapes(blocks):
    return [jax.ShapeDtypeStruct((N_CHIPS, 1) + a.shape, a.dtype) for a in blocks]


def _gather_sems(nw):
    n = 2 * nw * (N_CHIPS - 1) + nw
    return [pltpu.SemaphoreType.DMA((n,)), pltpu.SemaphoreType.DMA((n,))]


def _gather_steps(ins, outs, send, recv):
    nw, nrel = len(ins), N_CHIPS - 1
    x, y, c = _mesh_pos()
    mine, sibling, chips = 2 * x + y, (x, y, 1 - c), _other_chips(x, y)

    def copy(src, dst, slot, to):
        return pltpu.make_async_remote_copy(src_ref=src, dst_ref=dst, send_sem=send.at[slot],
                                            recv_sem=recv.at[slot], device_id=to, device_id_type=MESH)

    def half_rows(t, core):
        rows = ins[t].shape[0] // 2
        return pl.ds(pl.multiple_of(core * rows, rows), rows)

    def landing(t, chip, core):
        return outs[t].at[chip, 0, half_rows(t, core), :]

    slots = [(t, r, chip) for t in range(nw) for r, chip in enumerate(chips)]
    own = [copy(ins[t], outs[t].at[mine, 0], 2 * nw * nrel + t, sibling) for t in range(nw)]
    first = [copy(ins[t].at[half_rows(t, c), :], landing(t, mine, c), t * nrel + r, (px, py, c))
             for t, r, (px, py) in slots]
    relays = [copy(landing(t, 2 * px + py, c), landing(t, 2 * px + py, c), nw * nrel + t * nrel + r, sibling)
              for t, r, (px, py) in slots]

    def start():
        for cp in own + first:
            cp.start()

    def relay():
        for (t, r, (px, py)), cp in zip(slots, relays):
            copy(landing(t, 2 * px + py, c), landing(t, 2 * px + py, c), t * nrel + r, (px, py, c)).wait_recv()
            cp.start()

    def finish():
        for t, r, (px, py) in slots:
            passed = landing(t, 2 * px + py, 1 - c)
            copy(passed, passed, nw * nrel + t * nrel + r, sibling).wait_recv()
        for cp in first + relays:
            cp.wait_send()
        for cp in own:
            cp.wait()

    return start, relay, finish


HALF = 512

GRAD_GEOM = {"w_in": ("rows", D_MODEL, IN_COLS // N_CHIPS), "w_up": ("rows", D_MODEL, 2 * D_FF // N_CHIPS),
             "w_out": ("cols", D_MODEL, D_MODEL // N_CHIPS), "w_down": ("cols", D_FF, D_FF // N_CHIPS)}


def _exchange_shape(n):
    kind, a, b = GRAD_GEOM[n]
    return (N_CHIPS, HALF, b) if kind == "rows" else (a, HALF)


def _piece_shape(n):
    name, _, part = n.partition(":")
    kind, _, b = GRAD_GEOM[name]
    if part:
        assert kind == "rows"
        return (HALF, b // 2)
    return (HALF, b) if kind == "rows" else (b, HALF)


def _half_of(ref, n, core):
    if GRAD_GEOM[n][0] == "rows":
        return ref.at[:, pl.ds(pl.multiple_of(core * HALF, HALF), HALF), :]
    return ref.at[core]


def _piece_of(ref, n, chip):
    name, _, part = n.partition(":")
    kind, _, b = GRAD_GEOM[name]
    if part:
        return ref.at[chip, :, pl.ds(int(part) * (b // 2), b // 2)]
    return ref.at[chip] if kind == "rows" else ref.at[pl.ds(pl.multiple_of(chip * b, b), b), :]


def _pair_exchange(g, names, name):
    n = len(names)

    def body(*refs):
        send, recv = refs[2 * n:]
        x, y, c = _mesh_pos()
        o = 1 - c
        cps = [pltpu.make_async_remote_copy(src_ref=_half_of(refs[t], nm, o), dst_ref=refs[n + t], send_sem=send.at[t],
                                            recv_sem=recv.at[t], device_id=(x, y, o), device_id_type=MESH)
               for t, nm in enumerate(names)]
        for cp in cps:
            cp.start()
        for cp in cps:
            cp.wait()

    return pl.pallas_call(
        body, in_specs=[ANY] * n, out_specs=[ANY] * n,
        out_shape=[jax.ShapeDtypeStruct(_exchange_shape(nm), BF16) for nm in names],
        scratch_shapes=[pltpu.SemaphoreType.DMA((n,)), pltpu.SemaphoreType.DMA((n,))],
        name=name)(*[g[nm] for nm in names])


def _pair_sum(g, recv, pos, names, name_prefix):
    def add(a, b, grid, a_spec, b_spec, name):
        def body(pos_ref, a_ref, b_ref, o_ref):
            o_ref[...] = (a_ref[...].astype(F32) + b_ref[...].astype(F32)).astype(BF16)

        return pl.pallas_call(
            body, grid_spec=pltpu.PrefetchScalarGridSpec(
                num_scalar_prefetch=1, grid=grid, in_specs=[a_spec, b_spec], out_specs=b_spec),
            out_shape=jax.ShapeDtypeStruct(b.shape, BF16), compiler_params=_cparams("parallel"), name=name)(pos, a, b)

    out = []
    for nm, r in zip(names, recv):
        kind, rows, width = GRAD_GEOM[nm]
        if kind == "rows":
            out.append(add(g[nm], r, (N_CHIPS,), pl.BlockSpec((None, HALF, width), lambda j, pos: (j, pos[2], 0)),
                           pl.BlockSpec((None, HALF, width), lambda j, pos: (j, 0, 0)), f"{name_prefix}_{nm}"))
        else:
            out.append(add(g[nm], r, (rows // D_MODEL,), pl.BlockSpec((None, D_MODEL, HALF), lambda j, pos: (pos[2], j, 0)),
                           pl.BlockSpec((D_MODEL, HALF), lambda j, pos: (j, 0)), f"{name_prefix}_{nm}"))
    return out


def _scattered_shapes(names):
    return [jax.ShapeDtypeStruct(_exchange_shape(nm[2:]) if nm.startswith("x:") else (N_CHIPS - 1,) + _piece_shape(nm),
                                 BF16) for nm in names]


def _scatter_sems(n):
    return [pltpu.SemaphoreType.DMA((n * (N_CHIPS - 1),)), pltpu.SemaphoreType.DMA((n * (N_CHIPS - 1),))]


def _scatter_steps(sums, outs, send, recv, names):
    nrel = N_CHIPS - 1
    x, y, c = _mesh_pos()
    cps = [pltpu.make_async_remote_copy(
        src_ref=_half_of(sums[t], nm[2:], 1 - c), dst_ref=outs[t], send_sem=send.at[t * nrel],
        recv_sem=recv.at[t * nrel], device_id=(x, y, 1 - c), device_id_type=MESH)
        for t, nm in enumerate(names) if nm.startswith("x:")]
    for r, (px, py) in enumerate(_other_chips(x, y)):
        for t, nm in enumerate(names):
            if nm.startswith("x:"):
                continue
            cps.append(pltpu.make_async_remote_copy(
                src_ref=_piece_of(sums[t], nm, 2 * px + py), dst_ref=outs[t].at[r], send_sem=send.at[t * nrel + r],
                recv_sem=recv.at[t * nrel + r], device_id=(px, py, c), device_id_type=MESH))

    def start():
        for cp in cps:
            cp.start()

    def finish():
        for cp in cps:
            cp.wait()

    return start, finish


def _chip_scatter(sums, names, name):
    n = len(names)

    def body(*refs):
        start, finish = _scatter_steps(refs[:n], refs[n:2 * n], *refs[2 * n:], names)
        start()
        finish()

    return pl.pallas_call(
        body, in_specs=[ANY] * n, out_specs=[ANY] * n, out_shape=_scattered_shapes(names),
        scratch_shapes=_scatter_sems(n), name=name)(*sums)


def _chip_sum(sums, recv, pos, names, name_prefix):
    def add(a, b, a_spec, shape, name):
        def body(pos_ref, a_ref, b_ref, o_ref):
            tot = a_ref[...].astype(F32)
            for r in range(N_CHIPS - 1):
                tot = tot + b_ref[r].astype(F32)
            o_ref[...] = tot

        return pl.pallas_call(
            body, grid_spec=pltpu.PrefetchScalarGridSpec(
                num_scalar_prefetch=1, grid=(1,), in_specs=[a_spec, pl.BlockSpec(b.shape, lambda i, pos: (0, 0, 0))],
                out_specs=pl.BlockSpec((None,) + shape, lambda i, pos: (pos[2], 0, 0))),
            out_shape=jax.ShapeDtypeStruct((2,) + shape, F32), compiler_params=_cparams("arbitrary"),
            name=name)(pos, a, b)

    chip = lambda pos: 2 * pos[0] + pos[1]
    out = []
    for nm, a, b in zip(names, sums, recv):
        shape = _piece_shape(nm)
        if GRAD_GEOM[nm][0] == "rows":
            spec = pl.BlockSpec((None,) + shape, lambda i, pos: (chip(pos), 0, 0))
        else:
            spec = pl.BlockSpec(shape, lambda i, pos: (chip(pos), 0))
        out.append(add(a, b, spec, shape, f"{name_prefix}_{nm}"))
    return out


def _pair_share(totals, name):
    n = len(totals)

    def body(*refs):
        ins, outs = refs[:n], refs[n:2 * n]
        send, recv = refs[2 * n:]
        x, y, c = _mesh_pos()
        o = 1 - c
        cps = [pltpu.make_async_remote_copy(src_ref=ins[t].at[c], dst_ref=outs[t].at[c], send_sem=send.at[t],
                                            recv_sem=recv.at[t], device_id=(x, y, o), device_id_type=MESH)
               for t in range(n)]
        for cp in cps:
            cp.start()
        for t in range(n):
            pltpu.make_async_remote_copy(src_ref=ins[t].at[o], dst_ref=outs[t].at[o], send_sem=send.at[t],
                                         recv_sem=recv.at[t], device_id=(x, y, o), device_id_type=MESH).wait_recv()
        for cp in cps:
            cp.wait_send()

    return pl.pallas_call(
        body, in_specs=[ANY] * n, out_specs=[ANY] * n,
        out_shape=[jax.ShapeDtypeStruct(t.shape, t.dtype) for t in totals],
        scratch_shapes=[pltpu.SemaphoreType.DMA((n,)), pltpu.SemaphoreType.DMA((n,))],
        input_output_aliases={t: t for t in range(n)}, name=name)(*totals)


def _chip_sums(l, g, pos, names):
    tag = f"l{l}_" + "_".join(names)
    recv = _pair_exchange(g, names, "pair_exchange_" + tag)
    return _pair_sum(g, recv, pos, names, "pair_sum_" + tag)


def _gradient_shards(l, sums, scattered, pos, names):
    tag = f"l{l}_" + "_".join(names)
    halves = _pair_share(_chip_sum(sums, scattered, pos, names, "chip_sum_" + tag), "pair_share_" + tag)
    out = {}
    for nm, t in zip(names, halves):
        rows, cols = _piece_shape(nm)
        out[nm] = t.reshape(2 * rows, cols) if GRAD_GEOM[nm][0] == "rows" else t.transpose(1, 0, 2).reshape(rows, 2 * cols)
    return out


N_DEV = 8


def _allreduce_small(packed, name):
    rows = packed.shape[0]

    def body(x_ref, out_ref, gath, send_sems, recv_sems, local_sem):
        x, y, c = _mesh_pos()
        me, sibling = (x, y, c), (x, y, 1 - c)
        chips = _other_chips(x, y)

        def blk(px, py, pc):
            return gath.at[pl.ds(pl.multiple_of((4 * px + 2 * py + pc) * rows, 8), rows), :]

        def copy(k, block, to, src=None):
            return pltpu.make_async_remote_copy(
                src_ref=blk(*block) if src is None else src, dst_ref=blk(*block), send_sem=send_sems.at[k],
                recv_sem=recv_sems.at[k], device_id=to, device_id_type=MESH)

        mine = pltpu.make_async_copy(x_ref, blk(*me), local_sem)
        mine.start()
        first = [copy(0, me, sibling, src=x_ref)]
        first += [copy(1 + j, me, (*chip, c), src=x_ref) for j, chip in enumerate(chips)]
        for cp in first:
            cp.start()
        passed = [copy(4 + j, (*chip, c), sibling) for j, chip in enumerate(chips)]
        for j, chip in enumerate(chips):
            copy(1 + j, (*chip, c), me).wait_recv()
            passed[j].start()
        copy(0, sibling, me).wait_recv()
        for j, chip in enumerate(chips):
            copy(4 + j, (*chip, 1 - c), me).wait_recv()
        for cp in first + passed:
            cp.wait_send()
        mine.wait()
        tot = gath[0:rows, :]
        for d in range(1, N_DEV):
            tot = tot + gath[d * rows:(d + 1) * rows, :]
        out_ref[...] = tot

    vmem = pl.BlockSpec(memory_space=pltpu.VMEM)
    return pl.pallas_call(
        body, in_specs=[vmem], out_specs=vmem, out_shape=jax.ShapeDtypeStruct((rows, LANES), F32),
        scratch_shapes=[pltpu.VMEM((N_DEV * rows, LANES), F32), pltpu.SemaphoreType.DMA((7,)),
                        pltpu.SemaphoreType.DMA((7,)), pltpu.SemaphoreType.DMA],
        compiler_params=pltpu.CompilerParams(vmem_limit_bytes=VMEM_LIMIT_BYTES),
        name=name)(packed)


def _adamw(w, g, m, v, name):
    rows, cols = w.shape
    tr = 256 if rows % 256 == 0 else rows

    def body(w_ref, g_ref, m_ref, v_ref, d_ref, mo_ref, vo_ref):
        gv = g_ref[...]
        mn = ADAM_B1 * m_ref[...] + (1.0 - ADAM_B1) * gv
        vn = ADAM_B2 * v_ref[...] + (1.0 - ADAM_B2) * (gv * gv)
        m_hat = mn / (1.0 - ADAM_B1 ** ADAM_STEP)
        v_hat = vn / (1.0 - ADAM_B2 ** ADAM_STEP)
        d_ref[...] = -ADAM_LR * (m_hat / (jnp.sqrt(v_hat) + ADAM_EPS) + ADAM_WD * w_ref[...])
        mo_ref[...] = mn
        vo_ref[...] = vn

    spec = pl.BlockSpec((tr, cols), lambda i: (i, 0))
    return pl.pallas_call(
        body, grid=(rows // tr,), in_specs=[spec] * 4, out_specs=[spec] * 3,
        out_shape=[jax.ShapeDtypeStruct((rows, cols), F32)] * 3, compiler_params=_cparams("parallel"),
        name=name)(w, g, m, v)


def _adamw_nd(w, g, m, v, name):
    cols = w.shape[-1] if w.shape[-1] % LANES == 0 else LANES
    outs = _adamw(*(t.reshape(-1, cols) for t in (w, g, m, v)), name)
    return tuple(t.reshape(w.shape) for t in outs)


def _pack(arrays):
    return jnp.concatenate([a.reshape(-1, LANES) for a in arrays], axis=0)


def _unpack(packed, shapes):
    out, row = [], 0
    for sh in shapes:
        n = math.prod(sh) // LANES
        out.append(packed[row:row + n].reshape(sh))
        row += n
    return out


WEIGHTS = ("pre_mix_norm", "w_in", "v_norm_g", "v_norm_b", "w_spatial", "b_spatial", "out_norm_a", "out_norm_b",
           "w_out", "post_mix_norm", "pre_ffn_norm", "w_up", "conv_w", "conv_b", "w_down", "post_ffn_norm")


def kernel(x, pre_mix_norm, w_in, v_norm_g, v_norm_b, w_spatial, b_spatial, out_norm_a, out_norm_b, w_out, post_mix_norm, pre_ffn_norm, w_up, conv_w, conv_b, w_down, post_ffn_norm, loss_target, m_pre_mix_norm, m_w_in, m_v_norm_g, m_v_norm_b, m_w_spatial, m_b_spatial, m_out_norm_a, m_out_norm_b, m_w_out, m_post_mix_norm, m_pre_ffn_norm, m_w_up, m_conv_w, m_conv_b, m_w_down, m_post_ffn_norm, v_pre_mix_norm, v_w_in, v_v_norm_g, v_v_norm_b, v_w_spatial, v_b_spatial, v_out_norm_a, v_out_norm_b, v_w_out, v_post_mix_norm, v_pre_ffn_norm, v_w_up, v_conv_w, v_conv_b, v_w_down, v_post_ffn_norm):
    w = dict(pre_mix_norm=pre_mix_norm, w_in=w_in, v_norm_g=v_norm_g, v_norm_b=v_norm_b, w_spatial=w_spatial,
             b_spatial=b_spatial, out_norm_a=out_norm_a, out_norm_b=out_norm_b, w_out=w_out,
             post_mix_norm=post_mix_norm, pre_ffn_norm=pre_ffn_norm, w_up=w_up, conv_w=conv_w, conv_b=conv_b,
             w_down=w_down, post_ffn_norm=post_ffn_norm)
    m = dict(pre_mix_norm=m_pre_mix_norm, w_in=m_w_in, v_norm_g=m_v_norm_g, v_norm_b=m_v_norm_b,
             w_spatial=m_w_spatial, b_spatial=m_b_spatial, out_norm_a=m_out_norm_a, out_norm_b=m_out_norm_b,
             w_out=m_w_out, post_mix_norm=m_post_mix_norm, pre_ffn_norm=m_pre_ffn_norm, w_up=m_w_up,
             conv_w=m_conv_w, conv_b=m_conv_b, w_down=m_w_down, post_ffn_norm=m_post_ffn_norm)
    v = dict(pre_mix_norm=v_pre_mix_norm, w_in=v_w_in, v_norm_g=v_v_norm_g, v_norm_b=v_v_norm_b,
             w_spatial=v_w_spatial, b_spatial=v_b_spatial, out_norm_a=v_out_norm_a, out_norm_b=v_out_norm_b,
             w_out=v_w_out, post_mix_norm=v_post_mix_norm, pre_ffn_norm=v_pre_ffn_norm, w_up=v_w_up,
             conv_w=v_conv_w, conv_b=v_conv_b, w_down=v_w_down, post_ffn_norm=v_post_ffn_norm)
    pos = jnp.stack([lax.axis_index("x"), lax.axis_index("y"), lax.axis_index("c")]).astype(jnp.int32)
    chip = 2 * lax.axis_index("x") + lax.axis_index("y")

    cw_cols = conv_w.shape[-1]
    blocks = [{n: w[n][l].astype(BF16) for n in BIG} for l in range(DEPTH)]
    small = {n: w[n] for n in SMALL}
    xs, target = x[0], loss_target[0]
    xin = xs
    h, (w_in0, cw_all) = _rms_cast(xin, small["pre_mix_norm"][0].reshape(1, -1), "pre_mix_l0",
                                   [blocks[0]["w_in"], conv_w.reshape(-1, LANES)])
    wg = dict(w_in=w_in0)
    conv_w_full = cw_all.reshape(N_CHIPS, DEPTH, 3, cw_cols).transpose(1, 2, 0, 3).reshape(DEPTH, 3, 2 * D_FF)

    tabs = _rope_tables(xs.shape[0])
    params = [_layer_params(l, small, conv_w_full) for l in range(DEPTH)]
    saved, wgs = [], []
    for l in range(DEPTH):
        sv, gathered, wg = _layer_forward(l, xin, h, params[l], wg, tabs,
                                          blocks[l + 1] if l + 1 < DEPTH else None,
                                          blocks[0] if l == 0 else None,
                                          params[l + 1]["pre_mix_norm"] if l + 1 < DEPTH else None)
        saved.append(sv)
        wgs.append(wg)
        if l + 1 < DEPTH:
            wg = gathered
            xin, h = sv["x2"], sv["h_next"]
    loss_part, dx, df, g_post = _loss_norm_bwd(saved[-1]["x1"], saved[-1]["f"], params[-1]["post_ffn_norm"], target,
                                               "loss")
    smalls, shards = [None] * DEPTH, [{} for _ in range(DEPTH)]
    pending = None
    for l in reversed(range(DEPTH)):
        dx1, dh1, big, smalls[l], scattered, done = _layer_backward(l, dx, df, saved[l], params[l], wgs[l], tabs, pos,
                                                                    pending[1:] if pending else None, hide=l == 0)
        smalls[l]["post_ffn_norm"] = g_post
        if l > 0:
            dx, smalls[l]["pre_mix_norm"], df, g_post = _norm_bwd_in_out(
                dx1, dh1, saved[l]["x0"], params[l]["pre_mix_norm"], saved[l - 1]["f"], params[l - 1]["post_ffn_norm"],
                f"norm_bwd_in_out_l{l}")
        else:
            dx, smalls[l]["pre_mix_norm"] = _norm_bwd_in(dx1, dh1, saved[l]["x0"], params[l]["pre_mix_norm"],
                                                         "norm_bwd_in_l0")
        if pending:
            shards[pending[0]].update(_gradient_shards(pending[0], pending[1], scattered, pos, pending[2]))
        if done:
            shards[l].update(_gradient_shards(
                l, [t for sums, _ in done.values() for t in sums], [t for _, received in done.values() for t in received],
                pos, tuple(n for names in done for n in names)))
        names = tuple(big)
        pending = (l, [big[n] for n in names], names) if names else None
    if pending:
        shards[pending[0]].update(_gradient_shards(
            pending[0], pending[1], _chip_scatter(pending[1], pending[2], f"chip_scatter_l{pending[0]}"), pos,
            pending[2]))

    small_shapes = [w[n].shape for n in SMALL]
    stacked = [jnp.stack([smalls[l][n].reshape(w[n].shape[1:]) for l in range(DEPTH)]) for n in SMALL]
    cw_grad = jnp.stack([smalls[l]["conv_w"] for l in range(DEPTH)])
    packed = _pack(stacked + [cw_grad, loss_part])
    total = _allreduce_small(packed, "allreduce_small")
    parts = _unpack(total, small_shapes + [cw_grad.shape, (8, LANES)])
    g_small = dict(zip(SMALL, parts[:len(SMALL)]))
    loss = parts[-1][0, 0]
    g_conv_w = lax.dynamic_slice(parts[-2], (0, 0, chip * cw_cols), conv_w.shape)

    grads = {n: jnp.stack([shards[l][n] for l in range(DEPTH)]) for n in BIG}
    grads.update(g_small)
    grads["conv_w"] = g_conv_w

    dp, mp, vp = _adamw(_pack([w[n] for n in SMALL]), _pack([g_small[n] for n in SMALL]),
                        _pack([m[n] for n in SMALL]), _pack([v[n] for n in SMALL]), "adamw_small")
    delta = dict(zip(SMALL, _unpack(dp, small_shapes)))
    new_m = dict(zip(SMALL, _unpack(mp, small_shapes)))
    new_v = dict(zip(SMALL, _unpack(vp, small_shapes)))
    for n in BIG + ("conv_w",):
        delta[n], new_m[n], new_v[n] = _adamw_nd(w[n], grads[n], m[n], v[n], "adamw_" + n)

    return (loss, dx[None], *[grads[n] for n in WEIGHTS], *[delta[n] for n in WEIGHTS],
            *[new_m[n] for n in WEIGHTS], *[new_v[n] for n in WEIGHTS])
```

```python
import functools
import math

import jax
import jax.numpy as jnp
import numpy as np
from jax import lax
from jax.experimental import pallas as pl
from jax.experimental.pallas import tpu as pltpu

F32 = jnp.float32
BF16 = jnp.bfloat16
MESH = pl.DeviceIdType.MESH

D_MODEL = 1024
A_WIDTH = 512
A_GROUPS = 4
GROUP_DIM = 128
CHUNK = 128
B_WIDTH = 512
HEAD_DIM = 64
ROT_DIM = 16
ROPE_THETA = 500000.0
DILATIONS = (1, 4, 16)
BAND = 128
IN_COLS = 2560
D_FF = 4096
EPS = 1e-6
NEG_INF = -1e30
N_CHIPS = 4
LANES = 128

ADAM_LR = 0.001
ADAM_B1 = 0.9
ADAM_B2 = 0.999
ADAM_EPS = 1e-08
ADAM_WD = 0.01
ADAM_STEP = 10

VMEM_LIMIT_BYTES = 56 * 1024 * 1024
RSQRT2 = 0.7071067811865476
INV_SQRT_2PI = 0.3989422804014327
GELU_C = 0.7978845608028654
GELU_A = 0.044715

ANY = pl.BlockSpec(memory_space=pl.ANY)
NN = ((1,), (0,))
NT = ((1,), (1,))
TN = ((0,), (0,))


def _cparams(*sem):
    return pltpu.CompilerParams(dimension_semantics=sem, vmem_limit_bytes=VMEM_LIMIT_BYTES)


def _dot(a, b, dims):
    return lax.dot_general(a, b, (dims, ((), ())), preferred_element_type=F32)


def _rsq_mean(a):
    return lax.rsqrt(jnp.mean(a * a, axis=-1, keepdims=True) + EPS)


def _rms_bwd(a, r, g, dz):
    t = dz * g
    da = r * t - a * (r * r * r) * jnp.mean(t * a, axis=-1, keepdims=True)
    return da, dz * a * r


def _colsum(a):
    return jnp.sum(a, axis=0, keepdims=True)


def _gelu_tanh(x):
    u = x * x
    t = jnp.tanh(x * (GELU_C + (GELU_C * GELU_A) * u))
    hx = 0.5 * x
    act = hx + hx * t
    grad = 0.5 + 0.5 * t + (hx - hx * t * t) * (GELU_C + (3.0 * GELU_C * GELU_A) * u)
    return act, grad


def _grid_edges(grid):
    ids = [pl.program_id(ax) for ax in range(len(grid))]
    first = functools.reduce(jnp.logical_and, [i == 0 for i in ids])
    last = functools.reduce(jnp.logical_and, [i == n - 1 for i, n in zip(ids, grid)])
    return first, last


def _matmul(a, b, *, grid, a_spec, b_spec, o_spec, o_shape, o_dtype, dims, nk, kaxis, acc_shape, name, b_2d=None,
            halves=False, scatter=None, gather=None):
    assert scatter is None or gather is None
    ns = len(scatter[0]) if scatter else len(gather) if gather else 0

    def body(*refs):
        a_ref, b_ref = refs[:2]
        o_ref = refs[2 + ns]
        scratch = refs[3 + 2 * ns:]
        if ns:
            first, last = _grid_edges(grid)
            if scatter:
                start, finish = _scatter_steps(refs[2:2 + ns], refs[3 + ns:3 + 2 * ns], scratch[-2], scratch[-1],
                                               scatter[1])
            else:
                start, relay, last_wait = _gather_steps(refs[2:2 + ns], refs[3 + ns:3 + 2 * ns], scratch[-2],
                                                        scratch[-1])

                def finish():
                    relay()
                    last_wait()
            pl.when(first)(start)
        def store(val):
            if halves:
                half = val.shape[1] // 2
                o_ref[0] = val[:, :half].astype(o_dtype)
                o_ref[1] = val[:, half:].astype(o_dtype)
            else:
                o_ref[...] = val.astype(o_dtype)

        bv = b_ref[...] if b_2d is None else b_ref[...].reshape(b_2d)
        part = _dot(a_ref[...], bv, dims)
        if nk == 1:
            store(part)
        else:
            acc = scratch[0]
            k = pl.program_id(kaxis)

            @pl.when(k == 0)
            def _():
                acc[...] = part

            @pl.when(k > 0)
            def _():
                acc[...] += part

            @pl.when(k == nk - 1)
            def _():
                store(acc[...])

        if ns:
            pl.when(last)(finish)

    sem = tuple("arbitrary" if (ns or (nk > 1 and ax == kaxis)) else "parallel" for ax in range(len(grid)))
    riding = list(scatter[0]) if scatter else list(gather or [])
    rider_shapes = _scattered_shapes(scatter[1]) if scatter else _gathered_shapes(riding)
    rider_sems = _scatter_sems(ns) if scatter else _gather_sems(ns) if gather else []
    res = pl.pallas_call(
        body, grid=grid, in_specs=[a_spec, b_spec] + [ANY] * ns, out_specs=[o_spec] + [ANY] * ns,
        out_shape=[jax.ShapeDtypeStruct(o_shape, o_dtype)] + rider_shapes,
        scratch_shapes=([pltpu.VMEM(acc_shape, F32)] if nk > 1 else []) + rider_sems,
        compiler_params=_cparams(*sem), name=name)(a, b, *riding)
    return (res[0], list(res[1:])) if ns else res[0]


def _mix_out_norm(mixed, w_out, x0, g_post, g_next, name, gather=None):
    s, d = x0.shape
    tm = 512
    ng = 0 if gather is None else len(gather)

    def body(a_ref, w_ref, x_ref, gp_ref, gn_ref, *rest):
        y_ref, x1_ref, h_ref = rest[ng:ng + 3]
        if ng:
            start, relay, finish = _gather_steps(rest[:ng], rest[ng + 3:2 * ng + 3], *rest[2 * ng + 3:])
            first, last = _grid_edges((s // tm,))
            pl.when(first)(start)
        y = _dot(a_ref[...], w_ref[...].reshape(d, d), NN)
        y_ref[...] = y
        x1 = x_ref[...] + y * _rsq_mean(y) * gp_ref[...]
        x1_ref[...] = x1
        h_ref[...] = (x1 * _rsq_mean(x1) * gn_ref[...]).astype(BF16)

        if ng:
            @pl.when(last)
            def _():
                relay()
                finish()

    row = pl.BlockSpec((tm, d), lambda i: (i, 0))
    vec = pl.BlockSpec((1, d), lambda i: (0, 0))
    res = pl.pallas_call(
        body, grid=(s // tm,),
        in_specs=[row, pl.BlockSpec((N_CHIPS, None, d // N_CHIPS, d), lambda i: (0, 0, 0, 0)), row, vec, vec]
        + [ANY] * ng,
        out_specs=[row, row, row] + [ANY] * ng,
        out_shape=[jax.ShapeDtypeStruct((s, d), F32), jax.ShapeDtypeStruct((s, d), F32),
                   jax.ShapeDtypeStruct((s, d), BF16)] + _gathered_shapes(gather or []),
        scratch_shapes=_gather_sems(ng) if ng else [],
        compiler_params=_cparams("arbitrary" if ng else "parallel"), name=name)(mixed, w_out, x0, g_post, g_next,
                                                                              *(gather or []))
    return res[:3], list(res[3:])


def _proj_bwd(dproj, w_in, name, scatter=None):
    s = dproj.shape[0]
    wcol = IN_COLS // N_CHIPS
    ns = 0 if scatter is None else len(scatter[0])

    def body(*refs):
        a_ref, w_ref = refs[:2]
        o_ref = refs[2 + ns]
        if ns:
            start, finish = _scatter_steps(refs[2:2 + ns], refs[3 + ns:3 + 2 * ns], *refs[3 + 2 * ns:], scatter[1])
            first, last = _grid_edges((s // TMM,))
            pl.when(first)(start)
        acc = _dot(a_ref[:, :wcol], w_ref[0], NT)
        for j in range(1, N_CHIPS):
            acc = acc + _dot(a_ref[:, j * wcol:(j + 1) * wcol], w_ref[j], NT)
        o_ref[...] = acc
        if ns:
            pl.when(last)(finish)

    res = pl.pallas_call(
        body, grid=(s // TMM,),
        in_specs=[pl.BlockSpec((TMM, IN_COLS), lambda i: (i, 0)),
                  pl.BlockSpec((N_CHIPS, None, D_MODEL, wcol), lambda i: (0, 0, 0, 0))] + [ANY] * ns,
        out_specs=[pl.BlockSpec((TMM, D_MODEL), lambda i: (i, 0))] + [ANY] * ns,
        out_shape=[jax.ShapeDtypeStruct((s, D_MODEL), F32)] + (_scattered_shapes(scatter[1]) if ns else []),
        scratch_shapes=_scatter_sems(ns) if ns else [],
        compiler_params=_cparams("arbitrary" if ns else "parallel"), name=name)(dproj, w_in,
                                                                              *(scatter[0] if ns else []))
    return res[0], list(res[1:])


TM = 1024
TMM = 1024


TR = 256


def _row_spec(width, col=0):
    return pl.BlockSpec((TR, width), lambda i, col=col: (i, col))


def _vec_spec(width):
    return pl.BlockSpec((1, width), lambda i: (0, 0))


def _rms_cast(x, g, name, gather=None):
    s, d = x.shape
    ng = 0 if gather is None else len(gather)

    def body(x_ref, g_ref, *rest):
        if ng:
            start, relay, finish = _gather_steps(rest[:ng], rest[ng + 1:2 * ng + 1], *rest[2 * ng + 1:])
            first, last = _grid_edges((s // TR,))
            pl.when(first)(start)
        a = x_ref[...]
        rest[ng][...] = (a * _rsq_mean(a) * g_ref[...]).astype(BF16)

        if ng:
            @pl.when(last)
            def _():
                relay()
                finish()

    res = pl.pallas_call(
        body, grid=(s // TR,), in_specs=[_row_spec(d), _vec_spec(d)] + [ANY] * ng,
        out_specs=[_row_spec(d)] + [ANY] * ng,
        out_shape=[jax.ShapeDtypeStruct((s, d), BF16)] + _gathered_shapes(gather or []),
        scratch_shapes=_gather_sems(ng) if ng else [],
        compiler_params=_cparams("arbitrary" if ng else "parallel"), name=name)(x, g, *(gather or []))
    return res[0], list(res[1:])


def _acc_init(refs):
    @pl.when(pl.program_id(0) == 0)
    def _():
        for r in refs:
            r[...] = jnp.zeros_like(r)


def _loss_norm_bwd(x1, f, g_post, target, name):
    s, d = x1.shape

    def body(x_ref, f_ref, gp_ref, t_ref, loss_ref, dx_ref, df_ref, dg_ref):
        _acc_init([loss_ref, dg_ref])
        fv = f_ref[...]
        r = _rsq_mean(fv)
        err = x_ref[...] + fv * r * gp_ref[...] - t_ref[...]
        dx = err * (1.0 / d)
        dx_ref[...] = dx
        part = 0.5 * jnp.sum(jnp.mean(err * err, axis=-1, keepdims=True), axis=0, keepdims=True)
        loss_ref[...] += jnp.broadcast_to(part, loss_ref.shape)
        da, dgt = _rms_bwd(fv, r, gp_ref[...], dx)
        df_ref[...] = da.astype(BF16)
        dg_ref[...] += _colsum(dgt)

    return pl.pallas_call(
        body, grid=(s // TR,), in_specs=[_row_spec(d), _row_spec(d), _vec_spec(d), _row_spec(d)],
        out_specs=[pl.BlockSpec((8, LANES), lambda i: (0, 0)), _row_spec(d), _row_spec(d), _vec_spec(d)],
        out_shape=[jax.ShapeDtypeStruct((8, LANES), F32), jax.ShapeDtypeStruct((s, d), F32),
                   jax.ShapeDtypeStruct((s, d), BF16), jax.ShapeDtypeStruct((1, d), F32)],
        compiler_params=_cparams("arbitrary"), name=name)(x1, f, g_post, target)


def _norm_bwd_mid(dx2, dh2, x1, g_pf, y1, g_pm, name):
    s, d = dx2.shape

    def body(dx2_ref, dh_ref, x1_ref, gpf_ref, y1_ref, gpm_ref, dx1_ref, dy1_ref, dgpf_ref, dgpm_ref):
        _acc_init([dgpf_ref, dgpm_ref])
        x1 = x1_ref[...]
        da, dgt = _rms_bwd(x1, _rsq_mean(x1), gpf_ref[...], dh_ref[...])
        dx1 = dx2_ref[...] + da
        dx1_ref[...] = dx1
        dgpf_ref[...] += _colsum(dgt)
        y1 = y1_ref[...]
        dy, dgt2 = _rms_bwd(y1, _rsq_mean(y1), gpm_ref[...], dx1)
        dy1_ref[...] = dy.astype(BF16)
        dgpm_ref[...] += _colsum(dgt2)

    return pl.pallas_call(
        body, grid=(s // TR,),
        in_specs=[_row_spec(d), _row_spec(d), _row_spec(d), _vec_spec(d), _row_spec(d), _vec_spec(d)],
        out_specs=[_row_spec(d), _row_spec(d), _vec_spec(d), _vec_spec(d)],
        out_shape=[jax.ShapeDtypeStruct((s, d), F32), jax.ShapeDtypeStruct((s, d), BF16),
                   jax.ShapeDtypeStruct((1, d), F32), jax.ShapeDtypeStruct((1, d), F32)],
        compiler_params=_cparams("arbitrary"), name=name)(dx2, dh2, x1, g_pf, y1, g_pm)


def _norm_bwd_in_out(dx1, dh1, x0, g1, f_below, g_post_below, name):
    s, d = dx1.shape

    def body(dx1_ref, dh_ref, x0_ref, g_ref, f_ref, gp_ref, dx0_ref, dg_ref, df_ref, dgp_ref):
        _acc_init([dg_ref, dgp_ref])
        x0 = x0_ref[...]
        da, dgt = _rms_bwd(x0, _rsq_mean(x0), g_ref[...], dh_ref[...])
        dx0 = dx1_ref[...] + da
        dx0_ref[...] = dx0
        dg_ref[...] += _colsum(dgt)
        fv = f_ref[...]
        db, dgt2 = _rms_bwd(fv, _rsq_mean(fv), gp_ref[...], dx0)
        df_ref[...] = db.astype(BF16)
        dgp_ref[...] += _colsum(dgt2)

    return pl.pallas_call(
        body, grid=(s // TR,),
        in_specs=[_row_spec(d), _row_spec(d), _row_spec(d), _vec_spec(d), _row_spec(d), _vec_spec(d)],
        out_specs=[_row_spec(d), _vec_spec(d), _row_spec(d), _vec_spec(d)],
        out_shape=[jax.ShapeDtypeStruct((s, d), F32), jax.ShapeDtypeStruct((1, d), F32),
                   jax.ShapeDtypeStruct((s, d), BF16), jax.ShapeDtypeStruct((1, d), F32)],
        compiler_params=_cparams("arbitrary"), name=name)(dx1, dh1, x0, g1, f_below, g_post_below)


def _norm_bwd_in(dx1, dh1, x0, g1, name):
    s, d = dx1.shape

    def body(dx1_ref, dh_ref, x0_ref, g_ref, dx0_ref, dg_ref):
        _acc_init([dg_ref])
        x0 = x0_ref[...]
        da, dgt = _rms_bwd(x0, _rsq_mean(x0), g_ref[...], dh_ref[...])
        dx0_ref[...] = dx1_ref[...] + da
        dg_ref[...] += _colsum(dgt)

    return pl.pallas_call(
        body, grid=(s // TR,), in_specs=[_row_spec(d), _row_spec(d), _row_spec(d), _vec_spec(d)],
        out_specs=[_row_spec(d), _vec_spec(d)],
        out_shape=[jax.ShapeDtypeStruct((s, d), F32), jax.ShapeDtypeStruct((1, d), F32)],
        compiler_params=_cparams("arbitrary"), name=name)(dx1, dh1, x0, g1)


def _tril_mask():
    row = lax.broadcasted_iota(jnp.int32, (CHUNK, CHUNK), 0)
    col = lax.broadcasted_iota(jnp.int32, (CHUNK, CHUNK), 1)
    return row >= col


def _gating_forward(pa, gv, bv, wt, bsf):
    er = lax.erf(pa * RSQRT2)
    za = 0.5 * pa * (1.0 + er)
    u = za[:, :A_WIDTH]
    va = za[:, A_WIDTH:]
    xc = va - jnp.mean(va, axis=-1, keepdims=True)
    rs = lax.rsqrt(jnp.mean(xc * xc, axis=-1, keepdims=True) + EPS)
    vn = xc * rs
    vlb = (vn * gv + bv).astype(BF16)
    sg = jnp.concatenate(
        [_dot(wt[g], vlb[:, g * GROUP_DIM:(g + 1) * GROUP_DIM], NN) for g in range(A_GROUPS)], axis=1) + bsf
    return er, u, rs, vn, vlb, sg


def _masked_ws(ws_ref):
    mask = _tril_mask()
    return [jnp.where(mask, ws_ref[g], 0.0).astype(BF16) for g in range(A_GROUPS)]


def _mixer_a_fwd(proj, gv, bv, ws, bsf, ga, name, gather=None):
    s = proj.shape[0]
    ng = 0 if gather is None else len(gather)

    def body(p_ref, gv_ref, bv_ref, ws_ref, bs_ref, ga_ref, *rest):
        o_ref = rest[ng]
        if ng:
            start, relay, finish = _gather_steps(rest[:ng], rest[ng + 1:2 * ng + 1], *rest[2 * ng + 1:])
            first, last = _grid_edges((s // TR,))
            pl.when(first)(start)
        wt = _masked_ws(ws_ref)
        for ch in range(TR // CHUNK):
            rows = slice(ch * CHUNK, (ch + 1) * CHUNK)
            _, u, _, _, _, sg = _gating_forward(p_ref[rows, :].astype(F32), gv_ref[...], bv_ref[...], wt, bs_ref[...])
            oa = u * sg
            o_ref[rows, :] = (oa * _rsq_mean(oa) * ga_ref[...]).astype(BF16)

        if ng:
            @pl.when(last)
            def _():
                relay()
                finish()

    res = pl.pallas_call(
        body, grid=(s // TR,),
        in_specs=[_row_spec(2 * A_WIDTH), _vec_spec(A_WIDTH), _vec_spec(A_WIDTH),
                  pl.BlockSpec((A_GROUPS, CHUNK, CHUNK), lambda i: (0, 0, 0)),
                  pl.BlockSpec((CHUNK, A_WIDTH), lambda i: (0, 0)), _vec_spec(A_WIDTH)] + [ANY] * ng,
        out_specs=[_row_spec(A_WIDTH)] + [ANY] * ng,
        out_shape=[jax.ShapeDtypeStruct((s, A_WIDTH + B_WIDTH), BF16)] + _gathered_shapes(gather or []),
        scratch_shapes=_gather_sems(ng) if ng else [],
        compiler_params=_cparams("arbitrary" if ng else "parallel"), name=name)(proj, gv, bv, ws, bsf, ga,
                                                                              *(gather or []))
    return res[0], list(res[1:])


def _mixer_a_bwd(proj, dmixed, gv, bv, ws, bsf, ga, name, scatter=None):
    s = proj.shape[0]
    nsteps = s // TR
    ns = 0 if scatter is None else len(scatter[0])

    def body(*refs):
        p_ref, dm_ref, gv_ref, bv_ref, ws_ref, bs_ref, ga_ref = refs[:7]
        dp_ref, dga_ref, dgv_ref, dbv_ref, dbs_ref, dws_ref = refs[7 + ns:13 + ns]
        if ns:
            start, finish = _scatter_steps(refs[7:7 + ns], refs[13 + ns:13 + 2 * ns], *refs[13 + 2 * ns:], scatter[1])
            first, last = _grid_edges((nsteps,))
            pl.when(first)(start)
        _acc_init([dga_ref, dgv_ref, dbv_ref, dbs_ref, dws_ref])
        mask = _tril_mask()
        wt = _masked_ws(ws_ref)
        gvv = gv_ref[...]
        gav = ga_ref[...]
        for ch in range(TR // CHUNK):
            rows = slice(ch * CHUNK, (ch + 1) * CHUNK)
            pa = p_ref[rows, :].astype(F32)
            er, u, rs, vn, vlb, sg = _gating_forward(pa, gvv, bv_ref[...], wt, bs_ref[...])
            oa = u * sg
            doa, dgt = _rms_bwd(oa, _rsq_mean(oa), gav, dm_ref[rows, :])
            dga_ref[...] += _colsum(dgt)
            du = doa * sg
            dsg = doa * u
            dbs_ref[...] += dsg
            dsgb = dsg.astype(BF16)
            dvl = []
            for g in range(A_GROUPS):
                cols = slice(g * GROUP_DIM, (g + 1) * GROUP_DIM)
                dws_ref[g] += jnp.where(mask, _dot(dsgb[:, cols], vlb[:, cols], NT), 0.0)
                dvl.append(_dot(wt[g], dsgb[:, cols], TN))
            dvl = jnp.concatenate(dvl, axis=1)
            dgv_ref[...] += _colsum(dvl * vn)
            dbv_ref[...] += _colsum(dvl)
            dvn = dvl * gvv
            dva = rs * (dvn - jnp.mean(dvn, axis=-1, keepdims=True)
                        - vn * jnp.mean(dvn * vn, axis=-1, keepdims=True))
            gp = 0.5 * (1.0 + er) + pa * jnp.exp(-0.5 * pa * pa) * INV_SQRT_2PI
            dp_ref[rows, :] = (jnp.concatenate([du, dva], axis=1) * gp).astype(BF16)

        @pl.when(pl.program_id(0) == nsteps - 1)
        def _():
            for g in range(A_GROUPS):
                cols = slice(g * GROUP_DIM, (g + 1) * GROUP_DIM)
                tot = jnp.sum(dbs_ref[:, cols], axis=1, keepdims=True)
                dbs_ref[:, cols] = jnp.broadcast_to(tot, (CHUNK, GROUP_DIM))

        if ns:
            pl.when(last)(finish)

    full = lambda *shape: pl.BlockSpec(shape, lambda i: (0,) * len(shape))
    res = pl.pallas_call(
        body, grid=(nsteps,),
        in_specs=[_row_spec(2 * A_WIDTH), _row_spec(A_WIDTH), _vec_spec(A_WIDTH), _vec_spec(A_WIDTH),
                  full(A_GROUPS, CHUNK, CHUNK), full(CHUNK, A_WIDTH), _vec_spec(A_WIDTH)] + [ANY] * ns,
        out_specs=[_row_spec(2 * A_WIDTH), _vec_spec(A_WIDTH), _vec_spec(A_WIDTH), _vec_spec(A_WIDTH),
                   full(CHUNK, A_WIDTH), full(A_GROUPS, CHUNK, CHUNK)] + [ANY] * ns,
        out_shape=[jax.ShapeDtypeStruct((s, IN_COLS), BF16), jax.ShapeDtypeStruct((1, A_WIDTH), F32),
                   jax.ShapeDtypeStruct((1, A_WIDTH), F32), jax.ShapeDtypeStruct((1, A_WIDTH), F32),
                   jax.ShapeDtypeStruct((CHUNK, A_WIDTH), F32),
                   jax.ShapeDtypeStruct((A_GROUPS, CHUNK, CHUNK), F32)]
        + (_scattered_shapes(scatter[1]) if ns else []),
        scratch_shapes=_scatter_sems(ns) if ns else [],
        compiler_params=_cparams("arbitrary"), name=name)(proj, dmixed, gv, bv, ws, bsf, ga,
                                                          *(scatter[0] if ns else []))
    return res[:6] + (list(res[6:]),)


def _rope_tables(s):
    half = ROT_DIM // 2
    lane = jnp.arange(LANES) % HEAD_DIM
    inv = ROPE_THETA ** (-(2 * (lane % half)).astype(F32) / ROT_DIM)
    ang = jnp.arange(s, dtype=F32)[:, None] * inv[None, :]
    cos, sin = jnp.cos(ang), jnp.sin(ang)
    c = jnp.where(lane < ROT_DIM, cos, 1.0)
    s1 = jnp.where(lane < half, -sin, 0.0)
    s2 = jnp.where((lane >= half) & (lane < ROT_DIM), sin, 0.0)
    return c, s1, s2


def _lane_blocks(width):
    return [slice(b * LANES, (b + 1) * LANES) for b in range(width // LANES)]


CLASS_DILS = tuple(d for d in DILATIONS if d > 1)


def _class_shape(s, dil, dtype):
    return jax.ShapeDtypeStruct((dil, s // dil, B_WIDTH), dtype)


def _class_spec(dil):
    return pl.BlockSpec((dil, TR // dil, B_WIDTH), lambda i, *_: (0, i, 0))


NBLK = B_WIDTH // LANES
STAGE = pltpu.VMEM((NBLK, TR, LANES), F32)


def _stage_put(stage, value):
    for b, sl in enumerate(_lane_blocks(B_WIDTH)):
        stage[b] = value[:, sl]


def _stage_get(stage):
    return jnp.concatenate([stage[b] for b in range(NBLK)], axis=1)


def _store_classes(stage, dst_ref, dil):
    for b, sl in enumerate(_lane_blocks(B_WIDTH)):
        for r in range(dil):
            dst_ref[r, :, sl] = stage[b, pl.ds(r, TR // dil, stride=dil), :].astype(dst_ref.dtype)


def _load_classes(src_ref, stage, dil):
    for b, sl in enumerate(_lane_blocks(B_WIDTH)):
        for r in range(dil):
            stage[b, pl.ds(r, TR // dil, stride=dil), :] = src_ref[r, :, sl].astype(F32)
    return _stage_get(stage)


def _rope_fwd(proj, tabs, name, gather=None):
    s = proj.shape[0]
    half = ROT_DIM // 2
    scale = HEAD_DIM ** -0.5
    nlay = 1 + len(CLASS_DILS)
    ng = 0 if gather is None else len(gather)

    def body(q_ref, k_ref, v_ref, c_ref, s1_ref, s2_ref, *rest):
        outs, stage = rest[ng:ng + 3 * nlay], rest[2 * ng + 3 * nlay]
        if ng:
            start, relay, finish = _gather_steps(rest[:ng], rest[ng + 3 * nlay:2 * ng + 3 * nlay],
                                                 *rest[2 * ng + 3 * nlay + 1:])
            first, last = _grid_edges((s // TR,))
            pl.when(first)(start)
        c, s1, s2 = c_ref[...], s1_ref[...], s2_ref[...]
        for which, (src, mul) in enumerate(((q_ref, scale), (k_ref, 1.0), (v_ref, None))):
            if mul is None:
                _stage_put(stage, src[...].astype(F32))
            else:
                for b, sl in enumerate(_lane_blocks(B_WIDTH)):
                    a = src[:, sl].astype(F32)
                    r = a * c + pltpu.roll(a, LANES - half, 1) * s1 + pltpu.roll(a, half, 1) * s2
                    stage[b] = r * mul
            dst = outs[which * nlay:(which + 1) * nlay]
            dst[0][...] = _stage_get(stage).astype(BF16)
            for ref, d in zip(dst[1:], CLASS_DILS):
                _store_classes(stage, ref, d)

        if ng:
            @pl.when(last)
            def _():
                relay()
                finish()

    tab = pl.BlockSpec((TR, LANES), lambda i: (i, 0))
    lay_specs = [_row_spec(B_WIDTH)] + [_class_spec(d) for d in CLASS_DILS]
    lay_shapes = [jax.ShapeDtypeStruct((s, B_WIDTH), BF16)] + [_class_shape(s, d, BF16) for d in CLASS_DILS]
    outs = pl.pallas_call(
        body, grid=(s // TR,),
        in_specs=[_row_spec(B_WIDTH, 2), _row_spec(B_WIDTH, 3), _row_spec(B_WIDTH, 4), tab, tab, tab] + [ANY] * ng,
        out_specs=lay_specs * 3 + [ANY] * ng, out_shape=lay_shapes * 3 + _gathered_shapes(gather or []),
        scratch_shapes=[STAGE] + (_gather_sems(ng) if ng else []),
        compiler_params=_cparams("arbitrary" if ng else "parallel"), name=name)(proj, proj, proj, *tabs,
                                                                              *(gather or []))
    q, k, v = (dict(zip(DILATIONS, outs[w * nlay:(w + 1) * nlay])) for w in range(3))
    return q, k, v, list(outs[3 * nlay:])


def _as_classes(t):
    return t if t.ndim == 3 else t[None]


def _head_masks():
    lane = lax.broadcasted_iota(jnp.int32, (1, LANES), 1)
    return lane < HEAD_DIM, lane >= HEAD_DIM


def _stack_heads(t):
    lo, hi = _head_masks()
    zero = jnp.zeros_like(t)
    return jnp.concatenate([jnp.where(lo, t, zero), jnp.where(hi, t, zero)], axis=0)


MAX_SEGMENT_BLOCKS = 8


def _segment_masks(j):
    qi = lax.broadcasted_iota(jnp.int32, (BAND, 2 * BAND), 0)
    kj = lax.broadcasted_iota(jnp.int32, (BAND, 2 * BAND), 1)
    both = (kj >= qi) & (kj <= qi + BAND)
    own = kj[:, :BAND] <= qi[:, :BAND]
    head = both & ((kj >= BAND) | (j > 0))
    return tuple(jnp.concatenate([m, m], axis=0) for m in (own, both, head))


def _block_rows(g):
    return pl.ds(pl.multiple_of(g * BAND, BAND), BAND)


def _key_rows(g):
    return pl.ds(pl.multiple_of((g - 1) * BAND, BAND), 2 * BAND)


def _segments(n):
    nb = n // BAND
    seg = min(nb, MAX_SEGMENT_BLOCKS)
    return seg, nb // seg


def _segment_specs(seg):
    main = pl.BlockSpec((None, seg * BAND, B_WIDTH), lambda r, j: (r, j, 0))
    halo = pl.BlockSpec((None, BAND, B_WIDTH), lambda r, j: (r, jnp.maximum(j * seg - 1, 0), 0))
    return main, halo


def _attn_fwd(q, k, v, name, gather=None):
    dil, n, _ = q.shape
    seg, nseg = _segments(n)
    nh = 2 if nseg > 1 else 0
    ng = 0 if gather is None else len(gather)

    def body(*refs):
        q_ref, k_ref, v_ref = refs[:3]
        halos = refs[3:3 + nh]
        o_ref, l_ref = refs[3 + nh + ng:5 + nh + ng]
        if ng:
            start, relay, finish = _gather_steps(refs[3 + nh:3 + nh + ng], refs[5 + nh + ng:5 + nh + 2 * ng],
                                                 *refs[5 + nh + 2 * ng:])
            first, last = _grid_edges((dil, nseg))
            pl.when(first)(start)
        own, both, head = _segment_masks(pl.program_id(1))
        lo, _ = _head_masks()

        def block(rows, keys_of, valid):
            for sl in _lane_blocks(B_WIDTH):
                kk, vv = keys_of(sl)
                sc = jnp.where(valid, _dot(_stack_heads(q_ref[rows, sl]), kk, NT), NEG_INF)
                mx = jnp.max(sc, axis=1, keepdims=True)
                p = jnp.exp(sc - mx)
                den = jnp.sum(p, axis=1, keepdims=True)
                out = _dot(p.astype(BF16), vv, NN) / den
                lse = mx + jnp.log(den)
                o_ref[rows, sl] = jnp.where(lo, out[:BAND], out[BAND:]).astype(BF16)
                l_ref[rows, sl] = jnp.where(lo, lse[:BAND], lse[BAND:])

        if nh:
            block(_block_rows(0), lambda sl: (jnp.concatenate([halos[0][:, sl], k_ref[0:BAND, sl]], axis=0),
                                              jnp.concatenate([halos[1][:, sl], v_ref[0:BAND, sl]], axis=0)), head)
        else:
            block(_block_rows(0), lambda sl: (k_ref[0:BAND, sl], v_ref[0:BAND, sl]), own)

        @pl.loop(1, seg)
        def _(g):
            block(_block_rows(g), lambda sl: (k_ref[_key_rows(g), sl], v_ref[_key_rows(g), sl]), both)

        if ng:
            @pl.when(last)
            def _():
                relay()
                finish()

    main, halo = _segment_specs(seg)
    res = pl.pallas_call(
        body, grid=(dil, nseg), in_specs=[main] * 3 + [halo] * nh + [ANY] * ng, out_specs=[main, main] + [ANY] * ng,
        out_shape=[jax.ShapeDtypeStruct((dil, n, B_WIDTH), BF16), jax.ShapeDtypeStruct((dil, n, B_WIDTH), F32)]
        + _gathered_shapes(gather or []),
        scratch_shapes=_gather_sems(ng) if ng else [],
        compiler_params=_cparams(*(["arbitrary"] * 2 if ng else ["parallel"] * 2)), name=name)(
            q, k, v, *([k, v] if nh else []), *(gather or []))
    return res[0], res[1], list(res[2:])


def _attn_bwd(q, k, v, do, lse, delta, name, scatter=None):
    dil, n, _ = q.shape
    seg, nseg = _segments(n)
    nh = 2 if nseg > 1 else 0
    ns = 0 if scatter is None else len(scatter[0])

    def body(*refs):
        q_ref, k_ref, v_ref, do_ref, lse_ref, dl_ref = refs[:6]
        halos = refs[6:6 + nh]
        dq_ref, dk_ref, dv_ref = refs[6 + nh + ns:9 + nh + ns]
        halo_out = refs[9 + nh + ns:9 + 2 * nh + ns]
        ck_ref, cv_ref = refs[9 + 2 * nh + 2 * ns:11 + 2 * nh + 2 * ns]
        if ns:
            start, finish = _scatter_steps(refs[6 + nh:6 + nh + ns], refs[9 + 2 * nh + ns:9 + 2 * nh + 2 * ns],
                                           *refs[11 + 2 * nh + 2 * ns:], scatter[1])
            first, last = _grid_edges((dil, nseg))
            pl.when(first)(start)
        own, both, head = _segment_masks(pl.program_id(1))
        lo, _ = _head_masks()
        lane = lax.broadcasted_iota(jnp.int32, (1, LANES), 1)

        def per_head(t):
            return jnp.concatenate(
                [jnp.sum(jnp.where(lane == first, t, 0.0), axis=1, keepdims=True) for first in (0, HEAD_DIM)], axis=0)

        def grads(rows, kk, vv, valid, sl):
            q2 = _stack_heads(q_ref[rows, sl])
            do2 = _stack_heads(do_ref[rows, sl])
            p = jnp.where(valid, jnp.exp(_dot(q2, kk, NT) - per_head(lse_ref[rows, sl])), 0.0)
            ds = (p * (_dot(do2, vv, NT) - per_head(dl_ref[rows, sl]))).astype(BF16)
            dq = _dot(ds, kk, NN)
            dq_ref[rows, sl] = jnp.where(lo, dq[:BAND], dq[BAND:]).astype(BF16)
            return _dot(ds, q2, TN), _dot(p.astype(BF16), do2, TN)

        for sl in _lane_blocks(B_WIDTH):
            if nh:
                dkk, dvv = grads(_block_rows(0), jnp.concatenate([halos[0][:, sl], k_ref[0:BAND, sl]], axis=0),
                                 jnp.concatenate([halos[1][:, sl], v_ref[0:BAND, sl]], axis=0), head, sl)
                halo_out[0][:, sl], halo_out[1][:, sl] = dkk[:BAND], dvv[:BAND]
                ck_ref[:, sl], cv_ref[:, sl] = dkk[BAND:], dvv[BAND:]
            else:
                ck_ref[:, sl], cv_ref[:, sl] = grads(_block_rows(0), k_ref[0:BAND, sl], v_ref[0:BAND, sl], own, sl)

        @pl.loop(1, seg)
        def _(g):
            before = _block_rows(g - 1)
            for sl in _lane_blocks(B_WIDTH):
                dkk, dvv = grads(_block_rows(g), k_ref[_key_rows(g), sl], v_ref[_key_rows(g), sl], both, sl)
                dk_ref[before, sl] = (ck_ref[:, sl] + dkk[:BAND]).astype(BF16)
                dv_ref[before, sl] = (cv_ref[:, sl] + dvv[:BAND]).astype(BF16)
                ck_ref[:, sl] = dkk[BAND:]
                cv_ref[:, sl] = dvv[BAND:]

        final = pl.ds((seg - 1) * BAND, BAND)
        dk_ref[final, :] = ck_ref[...].astype(BF16)
        dv_ref[final, :] = cv_ref[...].astype(BF16)

        if ns:
            pl.when(last)(finish)

    main, halo = _segment_specs(seg)
    shape = jax.ShapeDtypeStruct((dil, n, B_WIDTH), BF16)
    halo_shape = jax.ShapeDtypeStruct((dil, nseg, BAND, B_WIDTH), F32)
    halo_spec = pl.BlockSpec((None, None, BAND, B_WIDTH), lambda r, j: (r, j, 0, 0))
    res = pl.pallas_call(
        body, grid=(dil, nseg), in_specs=[main] * 6 + [halo] * nh + [ANY] * ns,
        out_specs=[main] * 3 + [halo_spec] * nh + [ANY] * ns,
        out_shape=[shape] * 3 + [halo_shape] * nh + (_scattered_shapes(scatter[1]) if ns else []),
        scratch_shapes=[pltpu.VMEM((BAND, B_WIDTH), F32)] * 2 + (_scatter_sems(ns) if ns else []),
        compiler_params=_cparams(*(["arbitrary"] * 2 if ns else ["parallel"] * 2)), name=name)(
            q, k, v, do, lse, delta, *([k, v] if nh else []), *(scatter[0] if ns else []))
    return res[0], res[1], res[2], (tuple(res[3:3 + nh]) if nh else None), list(res[3 + nh:])


def _attn_combine(outs, lses, gb, mixed, name, gather=None):
    s = mixed.shape[0]
    npat = len(DILATIONS)
    w = B_WIDTH
    ng = 0 if gather is None else len(gather)

    def body(*refs):
        o_refs, l_refs = refs[:npat], refs[npat:2 * npat]
        g_ref = refs[2 * npat]
        ob_ref = refs[2 * npat + 2 + ng]
        lse_refs = refs[2 * npat + 3 + ng:3 * npat + 3 + ng]
        mb_ref = refs[3 * npat + 3 + ng]
        stage = refs[3 * npat + 4 + 2 * ng]
        if ng:
            start, relay, finish = _gather_steps(refs[2 * npat + 2:2 * npat + 2 + ng],
                                                 refs[3 * npat + 4 + ng:3 * npat + 4 + 2 * ng],
                                                 *refs[3 * npat + 5 + 2 * ng:])
            first, last = _grid_edges((s // TR,))
            pl.when(first)(start)
        os_ = [o_refs[0][...].astype(F32)] + [_load_classes(r, stage, d) for r, d in zip(o_refs[1:], CLASS_DILS)]
        ls = [l_refs[0][...]] + [_load_classes(r, stage, d) for r, d in zip(l_refs[1:], CLASS_DILS)]
        mx = functools.reduce(jnp.maximum, ls)
        ws = [jnp.exp(l - mx) for l in ls]
        tot = functools.reduce(lambda a, b: a + b, ws)
        ob = functools.reduce(lambda a, b: a + b, [wt / tot * o for wt, o in zip(ws, os_)])
        ob_ref[...] = ob
        lse = mx + jnp.log(tot)
        _stage_put(stage, lse)
        lse_refs[0][...] = lse
        for ref, d in zip(lse_refs[1:], CLASS_DILS):
            _store_classes(stage, ref, d)
        mb_ref[...] = (ob * _rsq_mean(ob) * g_ref[...]).astype(BF16)

        if ng:
            @pl.when(last)
            def _():
                relay()
                finish()

    lay_specs = [_row_spec(w)] + [_class_spec(d) for d in CLASS_DILS]
    res = pl.pallas_call(
        body, grid=(s // TR,), in_specs=lay_specs * 2 + [_vec_spec(w), ANY] + [ANY] * ng,
        out_specs=[_row_spec(w)] + lay_specs + [_row_spec(w, 1)] + [ANY] * ng,
        out_shape=[jax.ShapeDtypeStruct((s, w), F32), jax.ShapeDtypeStruct((s, w), F32)]
        + [_class_shape(s, d, F32) for d in CLASS_DILS] + [jax.ShapeDtypeStruct(mixed.shape, mixed.dtype)]
        + _gathered_shapes(gather or []),
        scratch_shapes=[STAGE] + (_gather_sems(ng) if ng else []), input_output_aliases={2 * npat + 1: npat + 1},
        compiler_params=_cparams("arbitrary" if ng else "parallel"), name=name)(*outs, *lses, gb, mixed,
                                                                              *(gather or []))
    return res[0], dict(zip(DILATIONS, res[1:npat + 1])), res[npat + 1], list(res[npat + 2:])


def _attn_bwd_prep(dmixed, ob, gb, name):
    s = ob.shape[0]
    w = B_WIDTH
    nlay = len(DILATIONS)

    def body(dm_ref, ob_ref, g_ref, *rest):
        do_refs, dl_refs = rest[:nlay], rest[nlay:2 * nlay]
        dg_ref, stage = rest[2 * nlay:]
        _acc_init([dg_ref])
        ob = ob_ref[...]
        dob, dgt = _rms_bwd(ob, _rsq_mean(ob), g_ref[...], dm_ref[...])
        dg_ref[...] += _colsum(dgt)
        _stage_put(stage, dob)
        do_refs[0][...] = dob.astype(BF16)
        for ref, d in zip(do_refs[1:], CLASS_DILS):
            _store_classes(stage, ref, d)
        lo, hi = _head_masks()
        t = dob * ob
        for b, sl in enumerate(_lane_blocks(w)):
            tb = t[:, sl]
            s0 = jnp.sum(jnp.where(lo, tb, 0.0), axis=1, keepdims=True)
            s1 = jnp.sum(jnp.where(hi, tb, 0.0), axis=1, keepdims=True)
            stage[b] = jnp.where(lo, s0, s1)
        dl_refs[0][...] = _stage_get(stage)
        for ref, d in zip(dl_refs[1:], CLASS_DILS):
            _store_classes(stage, ref, d)

    lay_specs = [_row_spec(w)] + [_class_spec(d) for d in CLASS_DILS]
    shapes = lambda dt: [jax.ShapeDtypeStruct((s, w), dt)] + [_class_shape(s, d, dt) for d in CLASS_DILS]
    res = pl.pallas_call(
        body, grid=(s // TR,), in_specs=[_row_spec(w, 1), _row_spec(w), _vec_spec(w)],
        out_specs=lay_specs * 2 + [_vec_spec(w)],
        out_shape=shapes(BF16) + shapes(F32) + [jax.ShapeDtypeStruct((1, w), F32)],
        scratch_shapes=[STAGE],
        compiler_params=_cparams("arbitrary"), name=name)(dmixed, ob, gb)
    return dict(zip(DILATIONS, res[:nlay])), dict(zip(DILATIONS, res[nlay:2 * nlay])), res[2 * nlay]


def _rope_bwd(dqs, dks, dvs, halos, tabs, dproj, name):
    s = dproj.shape[0]
    half = ROT_DIM // 2
    scale = HEAD_DIM ** -0.5
    npat = len(DILATIONS)
    w = B_WIDTH
    nseg = halos[0].shape[0]
    per = s // nseg // TR

    def body(*refs):
        groups = [refs[g * npat:(g + 1) * npat] for g in range(3)]
        halo_refs = (None,) + tuple(refs[3 * npat:3 * npat + 2])
        c_ref, s1_ref, s2_ref, _, o_ref, stage = refs[3 * npat + 2:]
        i = pl.program_id(0)
        at_edge = ((i + 1) % per == 0) & ((i + 1) // per < nseg)

        def total(rs, halo_ref=None):
            acc = rs[0][...].astype(F32)
            if halo_ref is not None:
                edge = jnp.concatenate([jnp.zeros((TR - BAND, w), F32), halo_ref[...]], axis=0)
                acc = acc + jnp.where(at_edge, edge, 0.0)
            for ref, d in zip(rs[1:], CLASS_DILS):
                acc = acc + _load_classes(ref, stage, d)
            return acc

        def unrope(g):
            c, s1, s2 = c_ref[...], s1_ref[...], s2_ref[...]
            for sl in _lane_blocks(w):
                gb = g[:, sl]
                o = gb * c + pltpu.roll(gb * s1, half, 1) + pltpu.roll(gb * s2, LANES - half, 1)
                o_ref[:, sl] = o.astype(BF16)

        which = pl.program_id(1)

        @pl.when(which == 0)
        def _():
            unrope(total(groups[0]) * scale)

        @pl.when(which == 1)
        def _():
            unrope(total(groups[1], halo_refs[1]))

        @pl.when(which == 2)
        def _():
            o_ref[...] = total(groups[2], halo_refs[2]).astype(BF16)

    tab = pl.BlockSpec((TR, LANES), lambda i, j: (i, 0))
    nat = pl.BlockSpec((TR, w), lambda i, j: (i, 0))
    lay_specs = [nat] + [_class_spec(d) for d in CLASS_DILS]
    edge_spec = pl.BlockSpec((None, BAND, w), lambda i, j: (jnp.minimum((i + 1) // per, nseg - 1), 0, 0))
    first_col = 2 * A_WIDTH // w
    return pl.pallas_call(
        body, grid=(s // TR, 3), in_specs=lay_specs * 3 + [edge_spec] * 2 + [tab] * 3 + [ANY],
        out_specs=pl.BlockSpec((TR, w), lambda i, j: (i, first_col + j)),
        out_shape=jax.ShapeDtypeStruct(dproj.shape, dproj.dtype), scratch_shapes=[STAGE],
        input_output_aliases={3 * npat + 5: 0},
        compiler_params=_cparams("parallel", "arbitrary"), name=name)(*dqs, *dks, *dvs, *halos, *tabs, dproj)


TK = 512
HALO = 16
FFN_ROWS = 256
FFN_CHUNKS = tuple(slice(r, r + FFN_ROWS) for r in range(0, TM, FFN_ROWS))


def _row_of(v, r):
    rows = lax.broadcasted_iota(jnp.int32, (v.shape[0], 1), 0)
    return jnp.sum(jnp.where(rows == r, v, 0.0), axis=0, keepdims=True)


def _taps_before(x, halo):
    row = lax.broadcasted_iota(jnp.int32, (x.shape[0], 1), 0)
    m1 = jnp.where(row == 0, _row_of(halo, HALO - 1), pltpu.roll(x, 1, 0))
    m2 = jnp.where(row == 0, _row_of(halo, HALO - 2), jnp.where(row == 1, _row_of(halo, HALO - 1), pltpu.roll(x, 2, 0)))
    return m2, m1, x


def _taps_after(x, halo):
    rows = x.shape[0]
    row = lax.broadcasted_iota(jnp.int32, (rows, 1), 0)
    p1 = jnp.where(row == rows - 1, _row_of(halo, 0), pltpu.roll(x, rows - 1, 0))
    p2 = jnp.where(row == rows - 2, _row_of(halo, 0), jnp.where(row == rows - 1, _row_of(halo, 1), pltpu.roll(x, rows - 2, 0)))
    return p1, p2


def _conv_value(taps, cw_ref, cb_ref, h):
    return cb_ref[h] + cw_ref[h, 0:1, :] * taps[0] + cw_ref[h, 1:2, :] * taps[1] + cw_ref[h, 2:3, :] * taps[2]


def _ffn_weight_specs(ncol):
    per_up = (2 * D_FF // N_CHIPS) // TK
    per_dn = (D_FF // N_CHIPS) // TK
    wg = pl.BlockSpec((None, None, D_MODEL, TK), lambda i, j: (j // per_up, 0, 0, j % per_up))
    wv = pl.BlockSpec((None, None, D_MODEL, TK), lambda i, j: ((j + ncol) // per_up, 0, 0, (j + ncol) % per_up))
    wd = pl.BlockSpec((None, None, TK, D_MODEL), lambda i, j: (j // per_dn, 0, j % per_dn, 0))
    cw = pl.BlockSpec((2, 3, TK), lambda i, j: (0, 0, j))
    cb = pl.BlockSpec((2, 1, TK), lambda i, j: (0, 0, j))
    return wg, wv, wd, cw, cb


def _ffn_forward(h2, w_up, w_down, cw3, cb3, name, gather=None, post=None):
    s = h2.shape[0]
    nm, ncol = s // TM, D_FF // TK
    ng = 0 if gather is None else len(gather)
    npost = 0 if post is None else 3
    nout = 4 + (2 if post else 0)

    def body(*refs):
        h_ref, wg_ref, wv_ref, wd_ref, cw_ref, cb_ref = refs[:6]
        post_in = refs[6:6 + npost]
        g_in = refs[6 + npost:6 + npost + ng]
        outs = refs[6 + npost + ng:6 + npost + ng + nout]
        y_ref, up_ref, cv_ref, f_ref = outs[:4]
        g_out = refs[6 + npost + ng + nout:6 + npost + 2 * ng + nout]
        carry = refs[6 + npost + 2 * ng + nout]
        i, j = pl.program_id(0), pl.program_id(1)
        if ng:
            start, relay, finish = _gather_steps(g_in, g_out, *refs[7 + npost + 2 * ng + nout:])
            pl.when((i == 0) & (j == 0))(start)
            pl.when((i == nm - 1) & (j == 0))(relay)

        @pl.when((i == 0) & (j == 0))
        def _():
            carry[...] = jnp.zeros_like(carry)

        @pl.when(j == 0)
        def _():
            f_ref[...] = jnp.zeros_like(f_ref)

        ups = []
        for rs in FFN_CHUNKS:
            hc = h_ref[rs, :]
            ups.append([_dot(hc, w_ref[...], NN).astype(BF16) for w_ref in (wg_ref, wv_ref)])
            for hh in range(2):
                up_ref[hh, rs, :] = ups[-1][hh]
        before = [carry[j, hh] for hh in range(2)]
        for rs, up in zip(FFN_CHUNKS, ups):
            conv = []
            for hh in range(2):
                x = up[hh].astype(F32)
                conv.append(_conv_value(_taps_before(x, before[hh]), cw_ref, cb_ref, hh))
                cv_ref[hh, rs, :] = conv[hh].astype(BF16)
                before[hh] = x[x.shape[0] - HALO:, :]
            y = (_gelu_tanh(conv[0])[0] * conv[1]).astype(BF16)
            y_ref[rs, :] = y
            f_ref[rs, :] += _dot(y, wd_ref[...], NN)
        for hh in range(2):
            carry[j, hh] = before[hh]

        @pl.when(j == ncol - 1)
        def _():
            if post:
                f = f_ref[...]
                x1_ref, gp_ref, gn_ref = post_in
                x2 = x1_ref[...] + f * _rsq_mean(f) * gp_ref[...]
                outs[4][...] = x2
                outs[5][...] = (x2 * _rsq_mean(x2) * gn_ref[...]).astype(BF16)

        if ng:
            pl.when((i == nm - 1) & (j == ncol - 1))(finish)

    wg, wv, wd, cw, cb = _ffn_weight_specs(ncol)
    row = pl.BlockSpec((TM, D_MODEL), lambda i, j: (i, 0))
    vec = pl.BlockSpec((1, D_MODEL), lambda i, j: (0, 0))
    res = pl.pallas_call(
        body, grid=(nm, ncol),
        in_specs=[row, wg, wv, wd, cw, cb] + ([row, vec, vec] if post else []) + [ANY] * ng,
        out_specs=[pl.BlockSpec((TM, TK), lambda i, j: (i, j)), pl.BlockSpec((2, TM, TK), lambda i, j: (0, i, j)),
                   pl.BlockSpec((2, TM, TK), lambda i, j: (0, i, j)), row] + ([row, row] if post else [])
        + [ANY] * ng,
        out_shape=[jax.ShapeDtypeStruct((s, D_FF), BF16), jax.ShapeDtypeStruct((2, s, D_FF), BF16),
                   jax.ShapeDtypeStruct((2, s, D_FF), BF16), jax.ShapeDtypeStruct((s, D_MODEL), F32)]
        + ([jax.ShapeDtypeStruct((s, D_MODEL), F32), jax.ShapeDtypeStruct((s, D_MODEL), BF16)] if post else [])
        + _gathered_shapes(gather or []),
        scratch_shapes=[pltpu.VMEM((ncol, 2, HALO, TK), F32)] + (_gather_sems(ng) if ng else []),
        compiler_params=_cparams("arbitrary", "arbitrary"), name=name)(h2, w_up, w_up, w_down, cw3, cb3,
                                                                      *(post or []), *(gather or []))
    return res[:nout], list(res[nout:])


def _ffn_backward(df, w_up, w_down, up3, cv3, cw3, name, scatter=None):
    s = df.shape[0]
    nm, ncol = s // TM, D_FF // TK
    ns = 0 if scatter is None else len(scatter[0])

    def body(*refs):
        df_ref, wg_ref, wv_ref, wd_ref, cw_ref, up_ref, cv_ref = refs[:7]
        s_in = refs[7:7 + ns]
        dup_ref, dh_ref, sums_ref = refs[7 + ns:10 + ns]
        s_out = refs[10 + ns:10 + 2 * ns]
        carry = refs[10 + 2 * ns]
        i, j = pl.program_id(0), pl.program_id(1)
        if ns:
            start, finish = _scatter_steps(s_in, s_out, *refs[11 + 2 * ns:], scatter[1])
            pl.when((i == 0) & (j == 0))(start)

        @pl.when((i == 0) & (j == 0))
        def _():
            carry[...] = jnp.zeros_like(carry)
            sums_ref[...] = jnp.zeros_like(sums_ref)

        @pl.when(j == 0)
        def _():
            dh_ref[...] = jnp.zeros_like(dh_ref)

        chunks = FFN_CHUNKS[::-1]
        dys = [_dot(df_ref[rs, :], wd_ref[...], NT) for rs in chunks]
        row = lax.broadcasted_iota(jnp.int32, (8, 1), 0)
        after = [carry[j, hh] for hh in range(2)]
        upd = [jnp.zeros((8, TK), F32) for _ in range(2)]
        for rs, dy in zip(chunks, dys):
            act, grad = _gelu_tanh(cv_ref[0, rs, :].astype(F32))
            dcs = (dy * cv_ref[1, rs, :].astype(F32) * grad, dy * act)
            part = dh_ref[rs, :]
            for hh, w_ref in ((0, wg_ref), (1, wv_ref)):
                dc = dcs[hh]
                x = up_ref[hh, rs, :].astype(F32)
                after1, after2 = _taps_after(dc, after[hh])
                for ridx, sm in enumerate((_colsum(after2 * x), _colsum(after1 * x), _colsum(dc * x), _colsum(dc))):
                    upd[hh] = upd[hh] + jnp.where(row == ridx, sm, 0.0)
                dup = (cw_ref[hh, 2:3, :] * dc + cw_ref[hh, 1:2, :] * after1 + cw_ref[hh, 0:1, :] * after2).astype(BF16)
                after[hh] = dc[:HALO, :]
                dup_ref[hh, rs, :] = dup
                part = part + _dot(dup, w_ref[...], NT)
            dh_ref[rs, :] = part
        for hh in range(2):
            sums_ref[j, hh] += upd[hh]
            carry[j, hh] = after[hh]

        if ns:
            pl.when((i == nm - 1) & (j == ncol - 1))(finish)

    wg, wv, wd, cw, _ = _ffn_weight_specs(ncol)
    rev = lambda i: nm - 1 - i
    res = pl.pallas_call(
        body, grid=(nm, ncol),
        in_specs=[pl.BlockSpec((TM, D_MODEL), lambda i, j: (rev(i), 0)), wg, wv, wd, cw,
                  pl.BlockSpec((2, TM, TK), lambda i, j: (0, rev(i), j)),
                  pl.BlockSpec((2, TM, TK), lambda i, j: (0, rev(i), j))] + [ANY] * ns,
        out_specs=[pl.BlockSpec((2, TM, TK), lambda i, j: (0, rev(i), j)),
                   pl.BlockSpec((TM, D_MODEL), lambda i, j: (rev(i), 0)),
                   pl.BlockSpec((ncol, 2, 8, TK), lambda i, j: (0, 0, 0, 0))] + [ANY] * ns,
        out_shape=[jax.ShapeDtypeStruct((2, s, D_FF), BF16), jax.ShapeDtypeStruct((s, D_MODEL), F32),
                   jax.ShapeDtypeStruct((ncol, 2, 8, TK), F32)] + (_scattered_shapes(scatter[1]) if ns else []),
        scratch_shapes=[pltpu.VMEM((ncol, 2, HALO, TK), F32)] + (_scatter_sems(ns) if ns else []),
        compiler_params=_cparams("arbitrary", "arbitrary"), name=name)(df, w_up, w_up, w_down, cw3, up3, cv3,
                                                                      *(scatter[0] if ns else []))
    return res[:3], list(res[3:])


def _wspec(rows, cols, index_map):
    return pl.BlockSpec((None, None, rows, cols), index_map)


def _layer_forward(l, x0, h1, p, wg, tabs, gather=None, late=None, g_next=None):
    s = x0.shape[0]
    nm = s // TMM
    tag = f"_l{l}"
    riders = dict.fromkeys(DILATIONS)
    proj_rider = rope_rider = combine_rider = None
    if late is not None:
        cols = lambda t, parts: [t[:, i * t.shape[1] // parts:(i + 1) * t.shape[1] // parts] for i in range(parts)]
        (down_a, down_b), up_q = cols(late["w_down"], 2), cols(late["w_up"], 4)
        proj_rider, rope_rider, combine_rider = [late["w_out"], down_a], [up_q[2]], [up_q[3]]
        riders = dict(zip(DILATIONS, ([down_b], [up_q[0]], [up_q[1]])))
    proj = _matmul(
        h1, wg["w_in"], grid=(nm, N_CHIPS), a_spec=pl.BlockSpec((TMM, D_MODEL), lambda i, j: (i, 0)),
        b_spec=_wspec(D_MODEL, IN_COLS // N_CHIPS, lambda i, j: (j, 0, 0, 0)),
        o_spec=pl.BlockSpec((TMM, IN_COLS // N_CHIPS), lambda i, j: (i, j)), o_shape=(s, IN_COLS), o_dtype=BF16,
        dims=NN, nk=1, kaxis=None, acc_shape=None, name="proj" + tag, gather=proj_rider)
    if late is not None:
        proj, (w_out_all4, down_a) = proj
    ma, next_out = _mixer_a_fwd(proj, p["v_norm_g"], p["v_norm_b"], p["w_spatial"], p["bs_full"], p["out_norm_a"],
                                "mixer_a_fwd" + tag, [gather["w_out"]] if gather else None)
    q, k, v, rope_landed = _rope_fwd(proj, tabs, "rope_fwd" + tag, rope_rider)
    outs, lses, landed = zip(*[
        _attn_fwd(_as_classes(q[d]), _as_classes(k[d]), _as_classes(v[d]), f"attn_fwd_d{d}" + tag, riders[d])
        for d in DILATIONS])
    outs = [o.reshape(s, B_WIDTH) if d == 1 else o for o, d in zip(outs, DILATIONS)]
    lses = [t.reshape(s, B_WIDTH) if d == 1 else t for t, d in zip(lses, DILATIONS)]
    ob, lse, mixed, combine_landed = _attn_combine(outs, lses, p["out_norm_b"], ma, "attn_combine" + tag,
                                                   combine_rider)
    if late is not None:
        wg = dict(wg, w_out=w_out_all4, w_down=jnp.concatenate([down_a, landed[0][0]], axis=-1),
                  w_up=jnp.concatenate([landed[1][0], landed[2][0], rope_landed[0], combine_landed[0]], axis=-1))
    (y1, x1, h2), next_in = _mix_out_norm(mixed, wg["w_out"], x0, p["post_mix_norm"], p["pre_ffn_norm"],
                                          "mix_out" + tag, [gather["w_in"]] if gather else None)
    post = None if g_next is None else (x1, p["post_ffn_norm"], g_next)
    (y, up3, cv3, f, *after), next_ffn = _ffn_forward(h2, wg["w_up"], wg["w_down"], p["cw3"], p["cb3"], "ffn_fwd" + tag,
                                                      [gather["w_up"], gather["w_down"]] if gather else None, post)
    gathered = dict(w_in=next_in[0], w_out=next_out[0], w_up=next_ffn[0], w_down=next_ffn[1]) if gather else None
    saved = dict(x0=x0, h1=h1, proj=proj, q=q, k=k, v=v, ob=ob, lse=lse, mixed=mixed, y1=y1, x1=x1, h2=h2,
                 up3=up3, cv3=cv3, y=y, f=f)
    if after:
        saved.update(x2=after[0], h_next=after[1])
    return saved, gathered, wg


def _layer_backward(l, dx2, df, sv, p, wg, tabs, pos, scatter=None, hide=False):
    s = dx2.shape[0]
    nm = s // TMM
    tag = f"_l{l}"
    g = {}
    (dup3, dh2, conv_sums), scattered = _ffn_backward(df, wg["w_up"], wg["w_down"], sv["up3"], sv["cv3"], p["cw3"],
                                                      "ffn_bwd" + tag, scatter)
    sums = conv_sums.transpose(1, 2, 0, 3).reshape(2, 8, D_FF)
    g["conv_w"] = jnp.concatenate([sums[0, :3], sums[1, :3]], axis=1)
    g["conv_b"] = jnp.concatenate([sums[0, 3:4], sums[1, 3:4]], axis=1)
    tn = 1024
    done = {}
    gw_down = _matmul(
        sv["y"], df, grid=(D_FF // tn,), a_spec=pl.BlockSpec((s, tn), lambda k: (0, k)),
        b_spec=pl.BlockSpec((s, D_MODEL), lambda k: (0, 0)),
        o_spec=pl.BlockSpec((2, tn, D_MODEL // 2), lambda k: (0, k, 0)),
        o_shape=(2, D_FF, D_MODEL // 2), o_dtype=BF16,
        dims=TN, nk=1, kaxis=None, acc_shape=None, name="w_down_grad" + tag, halves=True)
    pair_sum = lambda n, grad, recv: _pair_sum({n: grad}, recv, pos, (n,), f"pair_sum_l{l}")
    gw_up, received = _matmul(
        sv["h2"], dup3, grid=(2 * D_FF // tn,), a_spec=pl.BlockSpec((s, D_MODEL), lambda n: (0, 0)),
        b_spec=pl.BlockSpec((None, s, tn), lambda n: (n // (D_FF // tn), 0, n % (D_FF // tn))),
        o_spec=pl.BlockSpec((None, D_MODEL, tn), lambda n: (n // 2, 0, n % 2)),
        o_shape=(N_CHIPS, D_MODEL, 2 * D_FF // N_CHIPS), o_dtype=BF16,
        dims=TN, nk=1, kaxis=None, acc_shape=None, name="w_up_grad" + tag, scatter=([gw_down], ("x:w_down",)))
    down_sums = pair_sum("w_down", gw_down, received)
    dx1, dy1, g["pre_ffn_norm"], g["post_mix_norm"] = _norm_bwd_mid(
        dx2, dh2, sv["x1"], p["pre_ffn_norm"], sv["y1"], p["post_mix_norm"], "norm_bwd_mid" + tag)
    w_out_all = pl.BlockSpec((N_CHIPS, None, D_MODEL // N_CHIPS, D_MODEL), lambda i: (0, 0, 0, 0))
    dmixed, received = _matmul(
        dy1, wg["w_out"], grid=(nm,), a_spec=pl.BlockSpec((TMM, D_MODEL), lambda i: (i, 0)), b_spec=w_out_all,
        o_spec=pl.BlockSpec((TMM, D_MODEL), lambda i: (i, 0)), o_shape=(s, D_MODEL), o_dtype=F32,
        dims=NT, nk=1, kaxis=None, acc_shape=None, name="mix_out_bwd" + tag, b_2d=(D_MODEL, D_MODEL),
        scatter=([gw_up], ("x:w_up",)))
    up_sums = pair_sum("w_up", gw_up, received)
    gw_out = _matmul(
        sv["mixed"], dy1, grid=(1,), a_spec=pl.BlockSpec((s, D_MODEL), lambda m: (0, 0)),
        b_spec=pl.BlockSpec((s, D_MODEL), lambda m: (0, 0)),
        o_spec=pl.BlockSpec((2, D_MODEL, D_MODEL // 2), lambda m: (0, 0, 0)),
        o_shape=(2, D_MODEL, D_MODEL // 2), o_dtype=BF16,
        dims=TN, nk=1, kaxis=None, acc_shape=None, name="w_out_grad" + tag, halves=True)
    dpa, g["out_norm_a"], g["v_norm_g"], g["v_norm_b"], dbs, g["w_spatial"], received = _mixer_a_bwd(
        sv["proj"], dmixed, p["v_norm_g"], p["v_norm_b"], p["w_spatial"], p["bs_full"], p["out_norm_a"],
        "mixer_a_bwd" + tag, ([gw_out], ("x:w_out",)))
    out_sums = pair_sum("w_out", gw_out, received)
    g["b_spatial"] = dbs[:, ::GROUP_DIM].T
    dob, delta, g["out_norm_b"] = _attn_bwd_prep(dmixed, sv["ob"], p["out_norm_b"], "attn_bwd_prep" + tag)
    riders = dict(zip(DILATIONS, ((down_sums, ("w_down",)), (up_sums, ("w_up:0",)), (up_sums, ("w_up:1",))))) if hide else {}
    dqs, dks, dvs, edges, received = zip(*[
        _attn_bwd(*(_as_classes(t[d]) for t in (sv["q"], sv["k"], sv["v"], dob, sv["lse"], delta)),
                  f"attn_bwd_d{d}" + tag, riders.get(d))
        for d in DILATIONS])
    if hide:
        done[("w_down",)] = (down_sums, received[0])
        done[("w_up",)] = (up_sums, [jnp.concatenate([received[1][0], received[2][0]], axis=-1)])
    nat = lambda ts: [t.reshape(s, B_WIDTH) if d == 1 else t for t, d in zip(ts, DILATIONS)]
    halos = [t[0] for t in edges[0]]
    dproj = _rope_bwd(nat(dqs), nat(dks), nat(dvs), halos, tabs, dpa, "rope_bwd" + tag)
    wcol = IN_COLS // N_CHIPS
    gw_in = _matmul(
        sv["h1"], dproj, grid=(N_CHIPS,), a_spec=pl.BlockSpec((s, D_MODEL), lambda n: (0, 0)),
        b_spec=pl.BlockSpec((s, wcol), lambda n: (0, n)),
        o_spec=pl.BlockSpec((None, D_MODEL, wcol), lambda n: (n, 0, 0)),
        o_shape=(N_CHIPS, D_MODEL, wcol), o_dtype=BF16,
        dims=TN, nk=1, kaxis=None, acc_shape=None, name="w_in_grad" + tag,
        scatter=(out_sums, ("w_out",)) if hide else None)
    if hide:
        gw_in, received = gw_in
        done[("w_out",)] = (out_sums, received)
        in_sums = _chip_sums(l, dict(w_in=gw_in), pos, ("w_in",))
        dh1, received = _proj_bwd(dproj, wg["w_in"], "proj_bwd" + tag, (in_sums, ("w_in",)))
        done[("w_in",)] = (in_sums, received)
        return dx1, dh1, {}, g, scattered, done
    dh1, received = _proj_bwd(dproj, wg["w_in"], "proj_bwd" + tag, ([gw_in], ("x:w_in",)))
    sums = dict(w_in=pair_sum("w_in", gw_in, received)[0], w_up=up_sums[0], w_out=out_sums[0], w_down=down_sums[0])
    return dx1, dh1, sums, g, scattered, done


SMALL = ("pre_mix_norm", "v_norm_g", "v_norm_b", "w_spatial", "b_spatial", "out_norm_a", "out_norm_b",
         "post_mix_norm", "pre_ffn_norm", "conv_b", "post_ffn_norm")
BIG = ("w_in", "w_out", "w_up", "w_down")
DEPTH = 2


def _layer_params(l, small, conv_w_full):
    p = {n: small[n][l].reshape(1, -1) for n in SMALL if n not in ("w_spatial", "b_spatial")}
    p["w_spatial"] = small["w_spatial"][l]
    p["bs_full"] = jnp.repeat(small["b_spatial"][l].T, GROUP_DIM, axis=1)
    p["cw3"] = conv_w_full[l].reshape(3, 2, D_FF).transpose(1, 0, 2)
    p["cb3"] = small["conv_b"][l].reshape(2, 1, D_FF)
    return p


def _mesh_pos():
    return lax.axis_index("x"), lax.axis_index("y"), lax.axis_index("c")


def _other_chips(x, y):
    return [(1 - x, y), (x, 1 - y), (1 - x, 1 - y)]


def _gathered_shapes(blocks):
    return [jax.ShapeDtypeStruct((N_CHIPS, 1) + a.shape, a.dtype) for a in blocks]


def _gather_sems(nw):
    n = 2 * nw * (N_CHIPS - 1) + nw
    return [pltpu.SemaphoreType.DMA((n,)), pltpu.SemaphoreType.DMA((n,))]


def _gather_steps(ins, outs, send, recv):
    nw, nrel = len(ins), N_CHIPS - 1
    x, y, c = _mesh_pos()
    mine, sibling, chips = 2 * x + y, (x, y, 1 - c), _other_chips(x, y)

    def copy(src, dst, slot, to):
        return pltpu.make_async_remote_copy(src_ref=src, dst_ref=dst, send_sem=send.at[slot],
                                            recv_sem=recv.at[slot], device_id=to, device_id_type=MESH)

    def half_rows(t, core):
        rows = ins[t].shape[0] // 2
        return pl.ds(pl.multiple_of(core * rows, rows), rows)

    def landing(t, chip, core):
        return outs[t].at[chip, 0, half_rows(t, core), :]

    slots = [(t, r, chip) for t in range(nw) for r, chip in enumerate(chips)]
    own = [copy(ins[t], outs[t].at[mine, 0], 2 * nw * nrel + t, sibling) for t in range(nw)]
    first = [copy(ins[t].at[half_rows(t, c), :], landing(t, mine, c), t * nrel + r, (px, py, c))
             for t, r, (px, py) in slots]
    relays = [copy(landing(t, 2 * px + py, c), landing(t, 2 * px + py, c), nw * nrel + t * nrel + r, sibling)
              for t, r, (px, py) in slots]

    def start():
        for cp in own + first:
            cp.start()

    def relay():
        for (t, r, (px, py)), cp in zip(slots, relays):
            copy(landing(t, 2 * px + py, c), landing(t, 2 * px + py, c), t * nrel + r, (px, py, c)).wait_recv()
            cp.start()

    def finish():
        for t, r, (px, py) in slots:
            passed = landing(t, 2 * px + py, 1 - c)
            copy(passed, passed, nw * nrel + t * nrel + r, sibling).wait_recv()
        for cp in first + relays:
            cp.wait_send()
        for cp in own:
            cp.wait()

    return start, relay, finish


HALF = 512

GRAD_GEOM = {"w_in": ("rows", D_MODEL, IN_COLS // N_CHIPS), "w_up": ("rows", D_MODEL, 2 * D_FF // N_CHIPS),
             "w_out": ("cols", D_MODEL, D_MODEL // N_CHIPS), "w_down": ("cols", D_FF, D_FF // N_CHIPS)}


def _exchange_shape(n):
    kind, a, b = GRAD_GEOM[n]
    return (N_CHIPS, HALF, b) if kind == "rows" else (a, HALF)


def _piece_shape(n):
    name, _, part = n.partition(":")
    kind, _, b = GRAD_GEOM[name]
    if part:
        assert kind == "rows"
        return (HALF, b // 2)
    return (HALF, b) if kind == "rows" else (b, HALF)


def _half_of(ref, n, core):
    if GRAD_GEOM[n][0] == "rows":
        return ref.at[:, pl.ds(pl.multiple_of(core * HALF, HALF), HALF), :]
    return ref.at[core]


def _piece_of(ref, n, chip):
    name, _, part = n.partition(":")
    kind, _, b = GRAD_GEOM[name]
    if part:
        return ref.at[chip, :, pl.ds(int(part) * (b // 2), b // 2)]
    return ref.at[chip] if kind == "rows" else ref.at[pl.ds(pl.multiple_of(chip * b, b), b), :]


def _pair_exchange(g, names, name):
    n = len(names)

    def body(*refs):
        send, recv = refs[2 * n:]
        x, y, c = _mesh_pos()
        o = 1 - c
        cps = [pltpu.make_async_remote_copy(src_ref=_half_of(refs[t], nm, o), dst_ref=refs[n + t], send_sem=send.at[t],
                                            recv_sem=recv.at[t], device_id=(x, y, o), device_id_type=MESH)
               for t, nm in enumerate(names)]
        for cp in cps:
            cp.start()
        for cp in cps:
            cp.wait()

    return pl.pallas_call(
        body, in_specs=[ANY] * n, out_specs=[ANY] * n,
        out_shape=[jax.ShapeDtypeStruct(_exchange_shape(nm), BF16) for nm in names],
        scratch_shapes=[pltpu.SemaphoreType.DMA((n,)), pltpu.SemaphoreType.DMA((n,))],
        name=name)(*[g[nm] for nm in names])


def _pair_sum(g, recv, pos, names, name_prefix):
    def add(a, b, grid, a_spec, b_spec, name):
        def body(pos_ref, a_ref, b_ref, o_ref):
            o_ref[...] = (a_ref[...].astype(F32) + b_ref[...].astype(F32)).astype(BF16)

        return pl.pallas_call(
            body, grid_spec=pltpu.PrefetchScalarGridSpec(
                num_scalar_prefetch=1, grid=grid, in_specs=[a_spec, b_spec], out_specs=b_spec),
            out_shape=jax.ShapeDtypeStruct(b.shape, BF16), compiler_params=_cparams("parallel"), name=name)(pos, a, b)

    out = []
    for nm, r in zip(names, recv):
        kind, rows, width = GRAD_GEOM[nm]
        if kind == "rows":
            out.append(add(g[nm], r, (N_CHIPS,), pl.BlockSpec((None, HALF, width), lambda j, pos: (j, pos[2], 0)),
                           pl.BlockSpec((None, HALF, width), lambda j, pos: (j, 0, 0)), f"{name_prefix}_{nm}"))
        else:
            out.append(add(g[nm], r, (rows // D_MODEL,), pl.BlockSpec((None, D_MODEL, HALF), lambda j, pos: (pos[2], j, 0)),
                           pl.BlockSpec((D_MODEL, HALF), lambda j, pos: (j, 0)), f"{name_prefix}_{nm}"))
    return out


def _scattered_shapes(names):
    return [jax.ShapeDtypeStruct(_exchange_shape(nm[2:]) if nm.startswith("x:") else (N_CHIPS - 1,) + _piece_shape(nm),
                                 BF16) for nm in names]


def _scatter_sems(n):
    return [pltpu.SemaphoreType.DMA((n * (N_CHIPS - 1),)), pltpu.SemaphoreType.DMA((n * (N_CHIPS - 1),))]


def _scatter_steps(sums, outs, send, recv, names):
    nrel = N_CHIPS - 1
    x, y, c = _mesh_pos()
    cps = [pltpu.make_async_remote_copy(
        src_ref=_half_of(sums[t], nm[2:], 1 - c), dst_ref=outs[t], send_sem=send.at[t * nrel],
        recv_sem=recv.at[t * nrel], device_id=(x, y, 1 - c), device_id_type=MESH)
        for t, nm in enumerate(names) if nm.startswith("x:")]
    for r, (px, py) in enumerate(_other_chips(x, y)):
        for t, nm in enumerate(names):
            if nm.startswith("x:"):
                continue
            cps.append(pltpu.make_async_remote_copy(
                src_ref=_piece_of(sums[t], nm, 2 * px + py), dst_ref=outs[t].at[r], send_sem=send.at[t * nrel + r],
                recv_sem=recv.at[t * nrel + r], device_id=(px, py, c), device_id_type=MESH))

    def start():
        for cp in cps:
            cp.start()

    def finish():
        for cp in cps:
            cp.wait()

    return start, finish


def _chip_scatter(sums, names, name):
    n = len(names)

    def body(*refs):
        start, finish = _scatter_steps(refs[:n], refs[n:2 * n], *refs[2 * n:], names)
        start()
        finish()

    return pl.pallas_call(
        body, in_specs=[ANY] * n, out_specs=[ANY] * n, out_shape=_scattered_shapes(names),
        scratch_shapes=_scatter_sems(n), name=name)(*sums)


def _chip_sum(sums, recv, pos, names, name_prefix):
    def add(a, b, a_spec, shape, name):
        def body(pos_ref, a_ref, b_ref, o_ref):
            tot = a_ref[...].astype(F32)
            for r in range(N_CHIPS - 1):
                tot = tot + b_ref[r].astype(F32)
            o_ref[...] = tot

        return pl.pallas_call(
            body, grid_spec=pltpu.PrefetchScalarGridSpec(
                num_scalar_prefetch=1, grid=(1,), in_specs=[a_spec, pl.BlockSpec(b.shape, lambda i, pos: (0, 0, 0))],
                out_specs=pl.BlockSpec((None,) + shape, lambda i, pos: (pos[2], 0, 0))),
            out_shape=jax.ShapeDtypeStruct((2,) + shape, F32), compiler_params=_cparams("arbitrary"),
            name=name)(pos, a, b)

    chip = lambda pos: 2 * pos[0] + pos[1]
    out = []
    for nm, a, b in zip(names, sums, recv):
        shape = _piece_shape(nm)
        if GRAD_GEOM[nm][0] == "rows":
            spec = pl.BlockSpec((None,) + shape, lambda i, pos: (chip(pos), 0, 0))
        else:
            spec = pl.BlockSpec(shape, lambda i, pos: (chip(pos), 0))
        out.append(add(a, b, spec, shape, f"{name_prefix}_{nm}"))
    return out


def _pair_share(totals, name):
    n = len(totals)

    def body(*refs):
        ins, outs = refs[:n], refs[n:2 * n]
        send, recv = refs[2 * n:]
        x, y, c = _mesh_pos()
        o = 1 - c
        cps = [pltpu.make_async_remote_copy(src_ref=ins[t].at[c], dst_ref=outs[t].at[c], send_sem=send.at[t],
                                            recv_sem=recv.at[t], device_id=(x, y, o), device_id_type=MESH)
               for t in range(n)]
        for cp in cps:
            cp.start()
        for t in range(n):
            pltpu.make_async_remote_copy(src_ref=ins[t].at[o], dst_ref=outs[t].at[o], send_sem=send.at[t],
                                         recv_sem=recv.at[t], device_id=(x, y, o), device_id_type=MESH).wait_recv()
        for cp in cps:
            cp.wait_send()

    return pl.pallas_call(
        body, in_specs=[ANY] * n, out_specs=[ANY] * n,
        out_shape=[jax.ShapeDtypeStruct(t.shape, t.dtype) for t in totals],
        scratch_shapes=[pltpu.SemaphoreType.DMA((n,)), pltpu.SemaphoreType.DMA((n,))],
        input_output_aliases={t: t for t in range(n)}, name=name)(*totals)


def _chip_sums(l, g, pos, names):
    tag = f"l{l}_" + "_".join(names)
    recv = _pair_exchange(g, names, "pair_exchange_" + tag)
    return _pair_sum(g, recv, pos, names, "pair_sum_" + tag)


def _gradient_shards(l, sums, scattered, pos, names):
    tag = f"l{l}_" + "_".join(names)
    halves = _pair_share(_chip_sum(sums, scattered, pos, names, "chip_sum_" + tag), "pair_share_" + tag)
    out = {}
    for nm, t in zip(names, halves):
        rows, cols = _piece_shape(nm)
        out[nm] = t.reshape(2 * rows, cols) if GRAD_GEOM[nm][0] == "rows" else t
    return out


N_DEV = 8


def _allreduce_small(packed, name):
    rows = packed.shape[0]

    def body(x_ref, out_ref, gath, send_sems, recv_sems, local_sem):
        x, y, c = _mesh_pos()
        me, sibling = (x, y, c), (x, y, 1 - c)
        chips = _other_chips(x, y)

        def blk(px, py, pc):
            return gath.at[pl.ds(pl.multiple_of((4 * px + 2 * py + pc) * rows, 8), rows), :]

        def copy(k, block, to, src=None):
            return pltpu.make_async_remote_copy(
                src_ref=blk(*block) if src is None else src, dst_ref=blk(*block), send_sem=send_sems.at[k],
                recv_sem=recv_sems.at[k], device_id=to, device_id_type=MESH)

        mine = pltpu.make_async_copy(x_ref, blk(*me), local_sem)
        mine.start()
        first = [copy(0, me, sibling, src=x_ref)]
        first += [copy(1 + j, me, (*chip, c), src=x_ref) for j, chip in enumerate(chips)]
        for cp in first:
            cp.start()
        passed = [copy(4 + j, (*chip, c), sibling) for j, chip in enumerate(chips)]
        for j, chip in enumerate(chips):
            copy(1 + j, (*chip, c), me).wait_recv()
            passed[j].start()
        copy(0, sibling, me).wait_recv()
        for j, chip in enumerate(chips):
            copy(4 + j, (*chip, 1 - c), me).wait_recv()
        for cp in first + passed:
            cp.wait_send()
        mine.wait()
        tot = gath[0:rows, :]
        for d in range(1, N_DEV):
            tot = tot + gath[d * rows:(d + 1) * rows, :]
        out_ref[...] = tot

    vmem = pl.BlockSpec(memory_space=pltpu.VMEM)
    return pl.pallas_call(
        body, in_specs=[vmem], out_specs=vmem, out_shape=jax.ShapeDtypeStruct((rows, LANES), F32),
        scratch_shapes=[pltpu.VMEM((N_DEV * rows, LANES), F32), pltpu.SemaphoreType.DMA((7,)),
                        pltpu.SemaphoreType.DMA((7,)), pltpu.SemaphoreType.DMA],
        compiler_params=pltpu.CompilerParams(vmem_limit_bytes=VMEM_LIMIT_BYTES),
        name=name)(packed)


def _adamw_step(w, g, m, v):
    mn = ADAM_B1 * m + (1.0 - ADAM_B1) * g
    vn = ADAM_B2 * v + (1.0 - ADAM_B2) * (g * g)
    m_hat = mn / (1.0 - ADAM_B1 ** ADAM_STEP)
    v_hat = vn / (1.0 - ADAM_B2 ** ADAM_STEP)
    return -ADAM_LR * (m_hat / (jnp.sqrt(v_hat) + ADAM_EPS) + ADAM_WD * w), mn, vn


def _adamw(w, g, m, v, name):
    rows, cols = w.shape
    tr = 256 if rows % 256 == 0 else rows

    def body(w_ref, g_ref, m_ref, v_ref, d_ref, mo_ref, vo_ref):
        d_ref[...], mo_ref[...], vo_ref[...] = _adamw_step(w_ref[...], g_ref[...], m_ref[...], v_ref[...])

    spec = pl.BlockSpec((tr, cols), lambda i: (i, 0))
    return pl.pallas_call(
        body, grid=(rows // tr,), in_specs=[spec] * 4, out_specs=[spec] * 3,
        out_shape=[jax.ShapeDtypeStruct((rows, cols), F32)] * 3, compiler_params=_cparams("parallel"),
        name=name)(w, g, m, v)


def _adamw_layers(w, gs, m, v, name):
    depth, rows, cols = w.shape
    tr = 256
    nblk = rows // tr
    split = gs[0].ndim == 3

    def body(w_ref, m_ref, v_ref, *rest):
        g_refs, (g_out, d_ref, mo_ref, vo_ref) = rest[:depth], rest[depth:]
        layer = pl.program_id(0)
        load = (lambda r: jnp.concatenate([r[0], r[1]], axis=-1)) if split else (lambda r: r[...])
        gv = load(g_refs[0])
        for k in range(1, depth):
            gv = jnp.where(layer == k, load(g_refs[k]), gv)
        g_out[...] = gv
        d_ref[...], mo_ref[...], vo_ref[...] = _adamw_step(w_ref[...], gv, m_ref[...], v_ref[...])

    def g_spec(k):
        tile = lambda l, i: jnp.where(l == k, i, jnp.where(l < k, 0, nblk - 1))
        if split:
            return pl.BlockSpec((2, tr, cols // 2), lambda l, i: (0, tile(l, i), 0))
        return pl.BlockSpec((tr, cols), lambda l, i: (tile(l, i), 0))

    spec = pl.BlockSpec((None, tr, cols), lambda l, i: (l, i, 0))
    return pl.pallas_call(
        body, grid=(depth, nblk), in_specs=[spec] * 3 + [g_spec(k) for k in range(depth)], out_specs=[spec] * 4,
        out_shape=[jax.ShapeDtypeStruct(w.shape, F32)] * 4, compiler_params=_cparams("parallel", "parallel"),
        name=name)(w, m, v, *gs)


def _adamw_nd(w, g, m, v, name):
    cols = w.shape[-1] if w.shape[-1] % LANES == 0 else LANES
    outs = _adamw(*(t.reshape(-1, cols) for t in (w, g, m, v)), name)
    return tuple(t.reshape(w.shape) for t in outs)


def _pack(arrays):
    return jnp.concatenate([a.reshape(-1, LANES) for a in arrays], axis=0)


def _unpack(packed, shapes):
    out, row = [], 0
    for sh in shapes:
        n = math.prod(sh) // LANES
        out.append(packed[row:row + n].reshape(sh))
        row += n
    return out


WEIGHTS = ("pre_mix_norm", "w_in", "v_norm_g", "v_norm_b", "w_spatial", "b_spatial", "out_norm_a", "out_norm_b",
           "w_out", "post_mix_norm", "pre_ffn_norm", "w_up", "conv_w", "conv_b", "w_down", "post_ffn_norm")


def kernel(x, pre_mix_norm, w_in, v_norm_g, v_norm_b, w_spatial, b_spatial, out_norm_a, out_norm_b, w_out, post_mix_norm, pre_ffn_norm, w_up, conv_w, conv_b, w_down, post_ffn_norm, loss_target, m_pre_mix_norm, m_w_in, m_v_norm_g, m_v_norm_b, m_w_spatial, m_b_spatial, m_out_norm_a, m_out_norm_b, m_w_out, m_post_mix_norm, m_pre_ffn_norm, m_w_up, m_conv_w, m_conv_b, m_w_down, m_post_ffn_norm, v_pre_mix_norm, v_w_in, v_v_norm_g, v_v_norm_b, v_w_spatial, v_b_spatial, v_out_norm_a, v_out_norm_b, v_w_out, v_post_mix_norm, v_pre_ffn_norm, v_w_up, v_conv_w, v_conv_b, v_w_down, v_post_ffn_norm):
    w = dict(pre_mix_norm=pre_mix_norm, w_in=w_in, v_norm_g=v_norm_g, v_norm_b=v_norm_b, w_spatial=w_spatial,
             b_spatial=b_spatial, out_norm_a=out_norm_a, out_norm_b=out_norm_b, w_out=w_out,
             post_mix_norm=post_mix_norm, pre_ffn_norm=pre_ffn_norm, w_up=w_up, conv_w=conv_w, conv_b=conv_b,
             w_down=w_down, post_ffn_norm=post_ffn_norm)
    m = dict(pre_mix_norm=m_pre_mix_norm, w_in=m_w_in, v_norm_g=m_v_norm_g, v_norm_b=m_v_norm_b,
             w_spatial=m_w_spatial, b_spatial=m_b_spatial, out_norm_a=m_out_norm_a, out_norm_b=m_out_norm_b,
             w_out=m_w_out, post_mix_norm=m_post_mix_norm, pre_ffn_norm=m_pre_ffn_norm, w_up=m_w_up,
             conv_w=m_conv_w, conv_b=m_conv_b, w_down=m_w_down, post_ffn_norm=m_post_ffn_norm)
    v = dict(pre_mix_norm=v_pre_mix_norm, w_in=v_w_in, v_norm_g=v_v_norm_g, v_norm_b=v_v_norm_b,
             w_spatial=v_w_spatial, b_spatial=v_b_spatial, out_norm_a=v_out_norm_a, out_norm_b=v_out_norm_b,
             w_out=v_w_out, post_mix_norm=v_post_mix_norm, pre_ffn_norm=v_pre_ffn_norm, w_up=v_w_up,
             conv_w=v_conv_w, conv_b=v_conv_b, w_down=v_w_down, post_ffn_norm=v_post_ffn_norm)
    pos = jnp.stack([lax.axis_index("x"), lax.axis_index("y"), lax.axis_index("c")]).astype(jnp.int32)
    chip = 2 * lax.axis_index("x") + lax.axis_index("y")

    cw_cols = conv_w.shape[-1]
    blocks = [{n: w[n][l].astype(BF16) for n in BIG} for l in range(DEPTH)]
    small = {n: w[n] for n in SMALL}
    xs, target = x[0], loss_target[0]
    xin = xs
    h, (w_in0, cw_all) = _rms_cast(xin, small["pre_mix_norm"][0].reshape(1, -1), "pre_mix_l0",
                                   [blocks[0]["w_in"], conv_w.reshape(-1, LANES)])
    wg = dict(w_in=w_in0)
    conv_w_full = cw_all.reshape(N_CHIPS, DEPTH, 3, cw_cols).transpose(1, 2, 0, 3).reshape(DEPTH, 3, 2 * D_FF)

    tabs = _rope_tables(xs.shape[0])
    params = [_layer_params(l, small, conv_w_full) for l in range(DEPTH)]
    saved, wgs = [], []
    for l in range(DEPTH):
        sv, gathered, wg = _layer_forward(l, xin, h, params[l], wg, tabs,
                                          blocks[l + 1] if l + 1 < DEPTH else None,
                                          blocks[0] if l == 0 else None,
                                          params[l + 1]["pre_mix_norm"] if l + 1 < DEPTH else None)
        saved.append(sv)
        wgs.append(wg)
        if l + 1 < DEPTH:
            wg = gathered
            xin, h = sv["x2"], sv["h_next"]
    loss_part, dx, df, g_post = _loss_norm_bwd(saved[-1]["x1"], saved[-1]["f"], params[-1]["post_ffn_norm"], target,
                                               "loss")
    smalls, shards = [None] * DEPTH, [{} for _ in range(DEPTH)]
    pending = None
    for l in reversed(range(DEPTH)):
        dx1, dh1, big, smalls[l], scattered, done = _layer_backward(l, dx, df, saved[l], params[l], wgs[l], tabs, pos,
                                                                    pending[1:] if pending else None, hide=l == 0)
        smalls[l]["post_ffn_norm"] = g_post
        if l > 0:
            dx, smalls[l]["pre_mix_norm"], df, g_post = _norm_bwd_in_out(
                dx1, dh1, saved[l]["x0"], params[l]["pre_mix_norm"], saved[l - 1]["f"], params[l - 1]["post_ffn_norm"],
                f"norm_bwd_in_out_l{l}")
        else:
            dx, smalls[l]["pre_mix_norm"] = _norm_bwd_in(dx1, dh1, saved[l]["x0"], params[l]["pre_mix_norm"],
                                                         "norm_bwd_in_l0")
        if pending:
            shards[pending[0]].update(_gradient_shards(pending[0], pending[1], scattered, pos, pending[2]))
        if done:
            shards[l].update(_gradient_shards(
                l, [t for sums, _ in done.values() for t in sums], [t for _, received in done.values() for t in received],
                pos, tuple(n for names in done for n in names)))
        names = tuple(big)
        pending = (l, [big[n] for n in names], names) if names else None
    if pending:
        shards[pending[0]].update(_gradient_shards(
            pending[0], pending[1], _chip_scatter(pending[1], pending[2], f"chip_scatter_l{pending[0]}"), pos,
            pending[2]))

    small_shapes = [w[n].shape for n in SMALL]
    stacked = [jnp.stack([smalls[l][n].reshape(w[n].shape[1:]) for l in range(DEPTH)]) for n in SMALL]
    cw_grad = jnp.stack([smalls[l]["conv_w"] for l in range(DEPTH)])
    packed = _pack(stacked + [cw_grad, loss_part])
    total = _allreduce_small(packed, "allreduce_small")
    parts = _unpack(total, small_shapes + [cw_grad.shape, (8, LANES)])
    g_small = dict(zip(SMALL, parts[:len(SMALL)]))
    loss = parts[-1][0, 0]
    g_conv_w = lax.dynamic_slice(parts[-2], (0, 0, chip * cw_cols), conv_w.shape)

    grads = dict(g_small, conv_w=g_conv_w)

    dp, mp, vp = _adamw(_pack([w[n] for n in SMALL]), _pack([g_small[n] for n in SMALL]),
                        _pack([m[n] for n in SMALL]), _pack([v[n] for n in SMALL]), "adamw_small")
    delta = dict(zip(SMALL, _unpack(dp, small_shapes)))
    new_m = dict(zip(SMALL, _unpack(mp, small_shapes)))
    new_v = dict(zip(SMALL, _unpack(vp, small_shapes)))
    delta["conv_w"], new_m["conv_w"], new_v["conv_w"] = _adamw_nd(w["conv_w"], g_conv_w, m["conv_w"], v["conv_w"],
                                                                  "adamw_conv_w")
    for n in BIG:
        grads[n], delta[n], new_m[n], new_v[n] = _adamw_layers(w[n], [shards[l][n] for l in range(DEPTH)], m[n],
                                                               v[n], "adamw_" + n)

    return (loss, dx[None], *[grads[n] for n in WEIGHTS], *[delta[n] for n in WEIGHTS],
            *[new_m[n] for n in WEIGHTS], *[new_v[n] for n in WEIGHTS])
```

```python
import functools
import math

import jax
import jax.numpy as jnp
import numpy as np
from jax import lax
from jax.experimental import pallas as pl
from jax.experimental.pallas import tpu as pltpu

F32 = jnp.float32
BF16 = jnp.bfloat16
MESH = pl.DeviceIdType.MESH

D_MODEL = 1024
A_WIDTH = 512
A_GROUPS = 4
GROUP_DIM = 128
CHUNK = 128
B_WIDTH = 512
HEAD_DIM = 64
ROT_DIM = 16
ROPE_THETA = 500000.0
DILATIONS = (1, 4, 16)
BAND = 128
IN_COLS = 2560
D_FF = 4096
EPS = 1e-6
NEG_INF = -1e30
N_CHIPS = 4
LANES = 128

ADAM_LR = 0.001
ADAM_B1 = 0.9
ADAM_B2 = 0.999
ADAM_EPS = 1e-08
ADAM_WD = 0.01
ADAM_STEP = 10

VMEM_LIMIT_BYTES = 56 * 1024 * 1024
RSQRT2 = 0.7071067811865476
INV_SQRT_2PI = 0.3989422804014327
GELU_C = 0.7978845608028654
GELU_A = 0.044715

ANY = pl.BlockSpec(memory_space=pl.ANY)
NN = ((1,), (0,))
NT = ((1,), (1,))
TN = ((0,), (0,))


def _cparams(*sem):
    return pltpu.CompilerParams(dimension_semantics=sem, vmem_limit_bytes=VMEM_LIMIT_BYTES)


def _dot(a, b, dims):
    return lax.dot_general(a, b, (dims, ((), ())), preferred_element_type=F32)


def _rsq_mean(a):
    return lax.rsqrt(jnp.mean(a * a, axis=-1, keepdims=True) + EPS)


def _rms_bwd(a, r, g, dz):
    t = dz * g
    da = r * t - a * (r * r * r) * jnp.mean(t * a, axis=-1, keepdims=True)
    return da, dz * a * r


def _colsum(a):
    return jnp.sum(a, axis=0, keepdims=True)


def _gelu_tanh(x):
    u = x * x
    t = jnp.tanh(x * (GELU_C + (GELU_C * GELU_A) * u))
    hx = 0.5 * x
    act = hx + hx * t
    grad = 0.5 + 0.5 * t + (hx - hx * t * t) * (GELU_C + (3.0 * GELU_C * GELU_A) * u)
    return act, grad


def _grid_edges(grid):
    ids = [pl.program_id(ax) for ax in range(len(grid))]
    first = functools.reduce(jnp.logical_and, [i == 0 for i in ids])
    last = functools.reduce(jnp.logical_and, [i == n - 1 for i, n in zip(ids, grid)])
    return first, last


def _matmul(a, b, *, grid, a_spec, b_spec, o_spec, o_shape, o_dtype, dims, nk, kaxis, acc_shape, name, b_2d=None,
            halves=False, scatter=None, gather=None):
    assert scatter is None or gather is None
    ns = len(scatter[0]) if scatter else len(gather) if gather else 0

    def body(*refs):
        a_ref, b_ref = refs[:2]
        o_ref = refs[2 + ns]
        scratch = refs[3 + 2 * ns:]
        if ns:
            first, last = _grid_edges(grid)
            if scatter:
                start, finish = _scatter_steps(refs[2:2 + ns], refs[3 + ns:3 + 2 * ns], scratch[-2], scratch[-1],
                                               scatter[1])
            else:
                start, relay, last_wait = _gather_steps(refs[2:2 + ns], refs[3 + ns:3 + 2 * ns], scratch[-2],
                                                        scratch[-1])

                def finish():
                    relay()
                    last_wait()
            pl.when(first)(start)
        def store(val):
            if halves:
                half = val.shape[1] // 2
                o_ref[0] = val[:, :half].astype(o_dtype)
                o_ref[1] = val[:, half:].astype(o_dtype)
            else:
                o_ref[...] = val.astype(o_dtype)

        bv = b_ref[...] if b_2d is None else b_ref[...].reshape(b_2d)
        part = _dot(a_ref[...], bv, dims)
        if nk == 1:
            store(part)
        else:
            acc = scratch[0]
            k = pl.program_id(kaxis)

            @pl.when(k == 0)
            def _():
                acc[...] = part

            @pl.when(k > 0)
            def _():
                acc[...] += part

            @pl.when(k == nk - 1)
            def _():
                store(acc[...])

        if ns:
            pl.when(last)(finish)

    sem = tuple("arbitrary" if (ns or (nk > 1 and ax == kaxis)) else "parallel" for ax in range(len(grid)))
    riding = list(scatter[0]) if scatter else list(gather or [])
    rider_shapes = _scattered_shapes(scatter[1]) if scatter else _gathered_shapes(riding)
    rider_sems = _scatter_sems(ns) if scatter else _gather_sems(ns) if gather else []
    res = pl.pallas_call(
        body, grid=grid, in_specs=[a_spec, b_spec] + [ANY] * ns, out_specs=[o_spec] + [ANY] * ns,
        out_shape=[jax.ShapeDtypeStruct(o_shape, o_dtype)] + rider_shapes,
        scratch_shapes=([pltpu.VMEM(acc_shape, F32)] if nk > 1 else []) + rider_sems,
        compiler_params=_cparams(*sem), name=name)(a, b, *riding)
    return (res[0], list(res[1:])) if ns else res[0]


def _mix_out_norm(mixed, w_out, x0, g_post, g_next, name, gather=None):
    s, d = x0.shape
    tm = 512
    ng = 0 if gather is None else len(gather)

    def body(a_ref, w_ref, x_ref, gp_ref, gn_ref, *rest):
        y_ref, x1_ref, h_ref = rest[ng:ng + 3]
        if ng:
            start, relay, finish = _gather_steps(rest[:ng], rest[ng + 3:2 * ng + 3], *rest[2 * ng + 3:])
            first, last = _grid_edges((s // tm,))
            pl.when(first)(start)
        y = _dot(a_ref[...], w_ref[...].reshape(d, d), NN)
        y_ref[...] = y
        x1 = x_ref[...] + y * _rsq_mean(y) * gp_ref[...]
        x1_ref[...] = x1
        h_ref[...] = (x1 * _rsq_mean(x1) * gn_ref[...]).astype(BF16)

        if ng:
            @pl.when(last)
            def _():
                relay()
                finish()

    row = pl.BlockSpec((tm, d), lambda i: (i, 0))
    vec = pl.BlockSpec((1, d), lambda i: (0, 0))
    res = pl.pallas_call(
        body, grid=(s // tm,),
        in_specs=[row, pl.BlockSpec((N_CHIPS, None, d // N_CHIPS, d), lambda i: (0, 0, 0, 0)), row, vec, vec]
        + [ANY] * ng,
        out_specs=[row, row, row] + [ANY] * ng,
        out_shape=[jax.ShapeDtypeStruct((s, d), F32), jax.ShapeDtypeStruct((s, d), F32),
                   jax.ShapeDtypeStruct((s, d), BF16)] + _gathered_shapes(gather or []),
        scratch_shapes=_gather_sems(ng) if ng else [],
        compiler_params=_cparams("arbitrary" if ng else "parallel"), name=name)(mixed, w_out, x0, g_post, g_next,
                                                                              *(gather or []))
    return res[:3], list(res[3:])


def _proj_bwd(dproj, w_in, name, scatter=None):
    s = dproj.shape[0]
    wcol = IN_COLS // N_CHIPS
    ns = 0 if scatter is None else len(scatter[0])

    def body(*refs):
        a_ref, w_ref = refs[:2]
        o_ref = refs[2 + ns]
        if ns:
            start, finish = _scatter_steps(refs[2:2 + ns], refs[3 + ns:3 + 2 * ns], *refs[3 + 2 * ns:], scatter[1])
            first, last = _grid_edges((s // TMM,))
            pl.when(first)(start)
        acc = _dot(a_ref[:, :wcol], w_ref[0], NT)
        for j in range(1, N_CHIPS):
            acc = acc + _dot(a_ref[:, j * wcol:(j + 1) * wcol], w_ref[j], NT)
        o_ref[...] = acc
        if ns:
            pl.when(last)(finish)

    res = pl.pallas_call(
        body, grid=(s // TMM,),
        in_specs=[pl.BlockSpec((TMM, IN_COLS), lambda i: (i, 0)),
                  pl.BlockSpec((N_CHIPS, None, D_MODEL, wcol), lambda i: (0, 0, 0, 0))] + [ANY] * ns,
        out_specs=[pl.BlockSpec((TMM, D_MODEL), lambda i: (i, 0))] + [ANY] * ns,
        out_shape=[jax.ShapeDtypeStruct((s, D_MODEL), F32)] + (_scattered_shapes(scatter[1]) if ns else []),
        scratch_shapes=_scatter_sems(ns) if ns else [],
        compiler_params=_cparams("arbitrary" if ns else "parallel"), name=name)(dproj, w_in,
                                                                              *(scatter[0] if ns else []))
    return res[0], list(res[1:])


TM = 1024
TMM = 1024


TR = 256


def _row_spec(width, col=0):
    return pl.BlockSpec((TR, width), lambda i, col=col: (i, col))


def _vec_spec(width):
    return pl.BlockSpec((1, width), lambda i: (0, 0))


def _rms_cast(x, g, name, gather=None):
    s, d = x.shape
    ng = 0 if gather is None else len(gather)

    def body(x_ref, g_ref, *rest):
        if ng:
            start, relay, finish = _gather_steps(rest[:ng], rest[ng + 1:2 * ng + 1], *rest[2 * ng + 1:])
            first, last = _grid_edges((s // TR,))
            pl.when(first)(start)
        a = x_ref[...]
        rest[ng][...] = (a * _rsq_mean(a) * g_ref[...]).astype(BF16)

        if ng:
            @pl.when(last)
            def _():
                relay()
                finish()

    res = pl.pallas_call(
        body, grid=(s // TR,), in_specs=[_row_spec(d), _vec_spec(d)] + [ANY] * ng,
        out_specs=[_row_spec(d)] + [ANY] * ng,
        out_shape=[jax.ShapeDtypeStruct((s, d), BF16)] + _gathered_shapes(gather or []),
        scratch_shapes=_gather_sems(ng) if ng else [],
        compiler_params=_cparams("arbitrary" if ng else "parallel"), name=name)(x, g, *(gather or []))
    return res[0], list(res[1:])


def _acc_init(refs):
    @pl.when(pl.program_id(0) == 0)
    def _():
        for r in refs:
            r[...] = jnp.zeros_like(r)


def _loss_norm_bwd(x1, f, g_post, target, name):
    s, d = x1.shape

    def body(x_ref, f_ref, gp_ref, t_ref, loss_ref, dx_ref, df_ref, dg_ref):
        _acc_init([loss_ref, dg_ref])
        fv = f_ref[...]
        r = _rsq_mean(fv)
        err = x_ref[...] + fv * r * gp_ref[...] - t_ref[...]
        dx = err * (1.0 / d)
        dx_ref[...] = dx
        part = 0.5 * jnp.sum(jnp.mean(err * err, axis=-1, keepdims=True), axis=0, keepdims=True)
        loss_ref[...] += jnp.broadcast_to(part, loss_ref.shape)
        da, dgt = _rms_bwd(fv, r, gp_ref[...], dx)
        df_ref[...] = da.astype(BF16)
        dg_ref[...] += _colsum(dgt)

    return pl.pallas_call(
        body, grid=(s // TR,), in_specs=[_row_spec(d), _row_spec(d), _vec_spec(d), _row_spec(d)],
        out_specs=[pl.BlockSpec((8, LANES), lambda i: (0, 0)), _row_spec(d), _row_spec(d), _vec_spec(d)],
        out_shape=[jax.ShapeDtypeStruct((8, LANES), F32), jax.ShapeDtypeStruct((s, d), F32),
                   jax.ShapeDtypeStruct((s, d), BF16), jax.ShapeDtypeStruct((1, d), F32)],
        compiler_params=_cparams("arbitrary"), name=name)(x1, f, g_post, target)


def _norm_bwd_mid(dx2, dh2, x1, g_pf, y1, g_pm, name):
    s, d = dx2.shape

    def body(dx2_ref, dh_ref, x1_ref, gpf_ref, y1_ref, gpm_ref, dx1_ref, dy1_ref, dgpf_ref, dgpm_ref):
        _acc_init([dgpf_ref, dgpm_ref])
        x1 = x1_ref[...]
        da, dgt = _rms_bwd(x1, _rsq_mean(x1), gpf_ref[...], dh_ref[...])
        dx1 = dx2_ref[...] + da
        dx1_ref[...] = dx1
        dgpf_ref[...] += _colsum(dgt)
        y1 = y1_ref[...]
        dy, dgt2 = _rms_bwd(y1, _rsq_mean(y1), gpm_ref[...], dx1)
        dy1_ref[...] = dy.astype(BF16)
        dgpm_ref[...] += _colsum(dgt2)

    return pl.pallas_call(
        body, grid=(s // TR,),
        in_specs=[_row_spec(d), _row_spec(d), _row_spec(d), _vec_spec(d), _row_spec(d), _vec_spec(d)],
        out_specs=[_row_spec(d), _row_spec(d), _vec_spec(d), _vec_spec(d)],
        out_shape=[jax.ShapeDtypeStruct((s, d), F32), jax.ShapeDtypeStruct((s, d), BF16),
                   jax.ShapeDtypeStruct((1, d), F32), jax.ShapeDtypeStruct((1, d), F32)],
        compiler_params=_cparams("arbitrary"), name=name)(dx2, dh2, x1, g_pf, y1, g_pm)


def _norm_bwd_in_out(dx1, dh1, x0, g1, f_below, g_post_below, name):
    s, d = dx1.shape

    def body(dx1_ref, dh_ref, x0_ref, g_ref, f_ref, gp_ref, dx0_ref, dg_ref, df_ref, dgp_ref):
        _acc_init([dg_ref, dgp_ref])
        x0 = x0_ref[...]
        da, dgt = _rms_bwd(x0, _rsq_mean(x0), g_ref[...], dh_ref[...])
        dx0 = dx1_ref[...] + da
        dx0_ref[...] = dx0
        dg_ref[...] += _colsum(dgt)
        fv = f_ref[...]
        db, dgt2 = _rms_bwd(fv, _rsq_mean(fv), gp_ref[...], dx0)
        df_ref[...] = db.astype(BF16)
        dgp_ref[...] += _colsum(dgt2)

    return pl.pallas_call(
        body, grid=(s // TR,),
        in_specs=[_row_spec(d), _row_spec(d), _row_spec(d), _vec_spec(d), _row_spec(d), _vec_spec(d)],
        out_specs=[_row_spec(d), _vec_spec(d), _row_spec(d), _vec_spec(d)],
        out_shape=[jax.ShapeDtypeStruct((s, d), F32), jax.ShapeDtypeStruct((1, d), F32),
                   jax.ShapeDtypeStruct((s, d), BF16), jax.ShapeDtypeStruct((1, d), F32)],
        compiler_params=_cparams("arbitrary"), name=name)(dx1, dh1, x0, g1, f_below, g_post_below)


def _norm_bwd_in(dx1, dh1, x0, g1, name):
    s, d = dx1.shape

    def body(dx1_ref, dh_ref, x0_ref, g_ref, dx0_ref, dg_ref):
        _acc_init([dg_ref])
        x0 = x0_ref[...]
        da, dgt = _rms_bwd(x0, _rsq_mean(x0), g_ref[...], dh_ref[...])
        dx0_ref[...] = dx1_ref[...] + da
        dg_ref[...] += _colsum(dgt)

    return pl.pallas_call(
        body, grid=(s // TR,), in_specs=[_row_spec(d), _row_spec(d), _row_spec(d), _vec_spec(d)],
        out_specs=[_row_spec(d), _vec_spec(d)],
        out_shape=[jax.ShapeDtypeStruct((s, d), F32), jax.ShapeDtypeStruct((1, d), F32)],
        compiler_params=_cparams("arbitrary"), name=name)(dx1, dh1, x0, g1)


def _tril_mask():
    row = lax.broadcasted_iota(jnp.int32, (CHUNK, CHUNK), 0)
    col = lax.broadcasted_iota(jnp.int32, (CHUNK, CHUNK), 1)
    return row >= col


def _gating_forward(pa, gv, bv, wt, bsf):
    er = lax.erf(pa * RSQRT2)
    za = 0.5 * pa * (1.0 + er)
    u = za[:, :A_WIDTH]
    va = za[:, A_WIDTH:]
    xc = va - jnp.mean(va, axis=-1, keepdims=True)
    rs = lax.rsqrt(jnp.mean(xc * xc, axis=-1, keepdims=True) + EPS)
    vn = xc * rs
    vlb = (vn * gv + bv).astype(BF16)
    sg = jnp.concatenate(
        [_dot(wt[g], vlb[:, g * GROUP_DIM:(g + 1) * GROUP_DIM], NN) for g in range(A_GROUPS)], axis=1) + bsf
    return er, u, rs, vn, vlb, sg


def _masked_ws(ws_ref):
    mask = _tril_mask()
    return [jnp.where(mask, ws_ref[g], 0.0).astype(BF16) for g in range(A_GROUPS)]


def _mixer_a_fwd(proj, gv, bv, ws, bsf, ga, name, gather=None):
    s = proj.shape[0]
    ng = 0 if gather is None else len(gather)

    def body(p_ref, gv_ref, bv_ref, ws_ref, bs_ref, ga_ref, *rest):
        o_ref = rest[ng]
        if ng:
            start, relay, finish = _gather_steps(rest[:ng], rest[ng + 1:2 * ng + 1], *rest[2 * ng + 1:])
            first, last = _grid_edges((s // TR,))
            pl.when(first)(start)
        wt = _masked_ws(ws_ref)
        for ch in range(TR // CHUNK):
            rows = slice(ch * CHUNK, (ch + 1) * CHUNK)
            _, u, _, _, _, sg = _gating_forward(p_ref[rows, :].astype(F32), gv_ref[...], bv_ref[...], wt, bs_ref[...])
            oa = u * sg
            o_ref[rows, :] = (oa * _rsq_mean(oa) * ga_ref[...]).astype(BF16)

        if ng:
            @pl.when(last)
            def _():
                relay()
                finish()

    res = pl.pallas_call(
        body, grid=(s // TR,),
        in_specs=[_row_spec(2 * A_WIDTH), _vec_spec(A_WIDTH), _vec_spec(A_WIDTH),
                  pl.BlockSpec((A_GROUPS, CHUNK, CHUNK), lambda i: (0, 0, 0)),
                  pl.BlockSpec((CHUNK, A_WIDTH), lambda i: (0, 0)), _vec_spec(A_WIDTH)] + [ANY] * ng,
        out_specs=[_row_spec(A_WIDTH)] + [ANY] * ng,
        out_shape=[jax.ShapeDtypeStruct((s, A_WIDTH + B_WIDTH), BF16)] + _gathered_shapes(gather or []),
        scratch_shapes=_gather_sems(ng) if ng else [],
        compiler_params=_cparams("arbitrary" if ng else "parallel"), name=name)(proj, gv, bv, ws, bsf, ga,
                                                                              *(gather or []))
    return res[0], list(res[1:])


def _mixer_a_bwd(proj, dmixed, gv, bv, ws, bsf, ga, name, scatter=None):
    s = proj.shape[0]
    nsteps = s // TR
    ns = 0 if scatter is None else len(scatter[0])

    def body(*refs):
        p_ref, dm_ref, gv_ref, bv_ref, ws_ref, bs_ref, ga_ref = refs[:7]
        dp_ref, dga_ref, dgv_ref, dbv_ref, dbs_ref, dws_ref = refs[7 + ns:13 + ns]
        if ns:
            start, finish = _scatter_steps(refs[7:7 + ns], refs[13 + ns:13 + 2 * ns], *refs[13 + 2 * ns:], scatter[1])
            first, last = _grid_edges((nsteps,))
            pl.when(first)(start)
        _acc_init([dga_ref, dgv_ref, dbv_ref, dbs_ref, dws_ref])
        mask = _tril_mask()
        wt = _masked_ws(ws_ref)
        gvv = gv_ref[...]
        gav = ga_ref[...]
        for ch in range(TR // CHUNK):
            rows = slice(ch * CHUNK, (ch + 1) * CHUNK)
            pa = p_ref[rows, :].astype(F32)
            er, u, rs, vn, vlb, sg = _gating_forward(pa, gvv, bv_ref[...], wt, bs_ref[...])
            oa = u * sg
            doa, dgt = _rms_bwd(oa, _rsq_mean(oa), gav, dm_ref[rows, :])
            dga_ref[...] += _colsum(dgt)
            du = doa * sg
            dsg = doa * u
            dbs_ref[...] += dsg
            dsgb = dsg.astype(BF16)
            dvl = []
            for g in range(A_GROUPS):
                cols = slice(g * GROUP_DIM, (g + 1) * GROUP_DIM)
                dws_ref[g] += jnp.where(mask, _dot(dsgb[:, cols], vlb[:, cols], NT), 0.0)
                dvl.append(_dot(wt[g], dsgb[:, cols], TN))
            dvl = jnp.concatenate(dvl, axis=1)
            dgv_ref[...] += _colsum(dvl * vn)
            dbv_ref[...] += _colsum(dvl)
            dvn = dvl * gvv
            dva = rs * (dvn - jnp.mean(dvn, axis=-1, keepdims=True)
                        - vn * jnp.mean(dvn * vn, axis=-1, keepdims=True))
            gp = 0.5 * (1.0 + er) + pa * jnp.exp(-0.5 * pa * pa) * INV_SQRT_2PI
            dp_ref[rows, :] = (jnp.concatenate([du, dva], axis=1) * gp).astype(BF16)

        @pl.when(pl.program_id(0) == nsteps - 1)
        def _():
            for g in range(A_GROUPS):
                cols = slice(g * GROUP_DIM, (g + 1) * GROUP_DIM)
                tot = jnp.sum(dbs_ref[:, cols], axis=1, keepdims=True)
                dbs_ref[:, cols] = jnp.broadcast_to(tot, (CHUNK, GROUP_DIM))

        if ns:
            pl.when(last)(finish)

    full = lambda *shape: pl.BlockSpec(shape, lambda i: (0,) * len(shape))
    res = pl.pallas_call(
        body, grid=(nsteps,),
        in_specs=[_row_spec(2 * A_WIDTH), _row_spec(A_WIDTH), _vec_spec(A_WIDTH), _vec_spec(A_WIDTH),
                  full(A_GROUPS, CHUNK, CHUNK), full(CHUNK, A_WIDTH), _vec_spec(A_WIDTH)] + [ANY] * ns,
        out_specs=[_row_spec(2 * A_WIDTH), _vec_spec(A_WIDTH), _vec_spec(A_WIDTH), _vec_spec(A_WIDTH),
                   full(CHUNK, A_WIDTH), full(A_GROUPS, CHUNK, CHUNK)] + [ANY] * ns,
        out_shape=[jax.ShapeDtypeStruct((s, IN_COLS), BF16), jax.ShapeDtypeStruct((1, A_WIDTH), F32),
                   jax.ShapeDtypeStruct((1, A_WIDTH), F32), jax.ShapeDtypeStruct((1, A_WIDTH), F32),
                   jax.ShapeDtypeStruct((CHUNK, A_WIDTH), F32),
                   jax.ShapeDtypeStruct((A_GROUPS, CHUNK, CHUNK), F32)]
        + (_scattered_shapes(scatter[1]) if ns else []),
        scratch_shapes=_scatter_sems(ns) if ns else [],
        compiler_params=_cparams("arbitrary"), name=name)(proj, dmixed, gv, bv, ws, bsf, ga,
                                                          *(scatter[0] if ns else []))
    return res[:6] + (list(res[6:]),)


def _rope_tables(s):
    half = ROT_DIM // 2
    lane = jnp.arange(LANES) % HEAD_DIM
    inv = ROPE_THETA ** (-(2 * (lane % half)).astype(F32) / ROT_DIM)
    ang = jnp.arange(s, dtype=F32)[:, None] * inv[None, :]
    cos, sin = jnp.cos(ang), jnp.sin(ang)
    c = jnp.where(lane < ROT_DIM, cos, 1.0)
    s1 = jnp.where(lane < half, -sin, 0.0)
    s2 = jnp.where((lane >= half) & (lane < ROT_DIM), sin, 0.0)
    return c, s1, s2


def _lane_blocks(width):
    return [slice(b * LANES, (b + 1) * LANES) for b in range(width // LANES)]


CLASS_DILS = tuple(d for d in DILATIONS if d > 1)


def _class_shape(s, dil, dtype):
    return jax.ShapeDtypeStruct((dil, s // dil, B_WIDTH), dtype)


def _class_spec(dil):
    return pl.BlockSpec((dil, TR // dil, B_WIDTH), lambda i, *_: (0, i, 0))


NBLK = B_WIDTH // LANES
STAGE = pltpu.VMEM((NBLK, TR, LANES), F32)


def _stage_put(stage, value):
    for b, sl in enumerate(_lane_blocks(B_WIDTH)):
        stage[b] = value[:, sl]


def _stage_get(stage):
    return jnp.concatenate([stage[b] for b in range(NBLK)], axis=1)


def _store_classes(stage, dst_ref, dil):
    for b, sl in enumerate(_lane_blocks(B_WIDTH)):
        for r in range(dil):
            dst_ref[r, :, sl] = stage[b, pl.ds(r, TR // dil, stride=dil), :].astype(dst_ref.dtype)


def _load_classes(src_ref, stage, dil):
    for b, sl in enumerate(_lane_blocks(B_WIDTH)):
        for r in range(dil):
            stage[b, pl.ds(r, TR // dil, stride=dil), :] = src_ref[r, :, sl].astype(F32)
    return _stage_get(stage)


def _rope_fwd(proj, tabs, name, gather=None):
    s = proj.shape[0]
    half = ROT_DIM // 2
    scale = HEAD_DIM ** -0.5
    nlay = 1 + len(CLASS_DILS)
    ng = 0 if gather is None else len(gather)

    def body(q_ref, k_ref, v_ref, c_ref, s1_ref, s2_ref, *rest):
        outs, stage = rest[ng:ng + 3 * nlay], rest[2 * ng + 3 * nlay]
        if ng:
            start, relay, finish = _gather_steps(rest[:ng], rest[ng + 3 * nlay:2 * ng + 3 * nlay],
                                                 *rest[2 * ng + 3 * nlay + 1:])
            first, last = _grid_edges((s // TR,))
            pl.when(first)(start)
        c, s1, s2 = c_ref[...], s1_ref[...], s2_ref[...]
        for which, (src, mul) in enumerate(((q_ref, scale), (k_ref, 1.0), (v_ref, None))):
            if mul is None:
                _stage_put(stage, src[...].astype(F32))
            else:
                for b, sl in enumerate(_lane_blocks(B_WIDTH)):
                    a = src[:, sl].astype(F32)
                    r = a * c + pltpu.roll(a, LANES - half, 1) * s1 + pltpu.roll(a, half, 1) * s2
                    stage[b] = r * mul
            dst = outs[which * nlay:(which + 1) * nlay]
            dst[0][...] = _stage_get(stage).astype(BF16)
            for ref, d in zip(dst[1:], CLASS_DILS):
                _store_classes(stage, ref, d)

        if ng:
            @pl.when(last)
            def _():
                relay()
                finish()

    tab = pl.BlockSpec((TR, LANES), lambda i: (i, 0))
    lay_specs = [_row_spec(B_WIDTH)] + [_class_spec(d) for d in CLASS_DILS]
    lay_shapes = [jax.ShapeDtypeStruct((s, B_WIDTH), BF16)] + [_class_shape(s, d, BF16) for d in CLASS_DILS]
    outs = pl.pallas_call(
        body, grid=(s // TR,),
        in_specs=[_row_spec(B_WIDTH, 2), _row_spec(B_WIDTH, 3), _row_spec(B_WIDTH, 4), tab, tab, tab] + [ANY] * ng,
        out_specs=lay_specs * 3 + [ANY] * ng, out_shape=lay_shapes * 3 + _gathered_shapes(gather or []),
        scratch_shapes=[STAGE] + (_gather_sems(ng) if ng else []),
        compiler_params=_cparams("arbitrary" if ng else "parallel"), name=name)(proj, proj, proj, *tabs,
                                                                              *(gather or []))
    q, k, v = (dict(zip(DILATIONS, outs[w * nlay:(w + 1) * nlay])) for w in range(3))
    return q, k, v, list(outs[3 * nlay:])


def _as_classes(t):
    return t if t.ndim == 3 else t[None]


def _head_masks():
    lane = lax.broadcasted_iota(jnp.int32, (1, LANES), 1)
    return lane < HEAD_DIM, lane >= HEAD_DIM


def _stack_heads(t):
    lo, hi = _head_masks()
    zero = jnp.zeros_like(t)
    return jnp.concatenate([jnp.where(lo, t, zero), jnp.where(hi, t, zero)], axis=0)


MAX_SEGMENT_BLOCKS = 8


def _segment_masks(j):
    qi = lax.broadcasted_iota(jnp.int32, (BAND, 2 * BAND), 0)
    kj = lax.broadcasted_iota(jnp.int32, (BAND, 2 * BAND), 1)
    both = (kj >= qi) & (kj <= qi + BAND)
    own = kj[:, :BAND] <= qi[:, :BAND]
    head = both & ((kj >= BAND) | (j > 0))
    return tuple(jnp.concatenate([m, m], axis=0) for m in (own, both, head))


def _block_rows(g):
    return pl.ds(pl.multiple_of(g * BAND, BAND), BAND)


def _key_rows(g):
    return pl.ds(pl.multiple_of((g - 1) * BAND, BAND), 2 * BAND)


def _segments(n):
    nb = n // BAND
    seg = min(nb, MAX_SEGMENT_BLOCKS)
    return seg, nb // seg


def _segment_specs(seg):
    main = pl.BlockSpec((None, seg * BAND, B_WIDTH), lambda r, j: (r, j, 0))
    halo = pl.BlockSpec((None, BAND, B_WIDTH), lambda r, j: (r, jnp.maximum(j * seg - 1, 0), 0))
    return main, halo


def _attn_fwd(q, k, v, name, gather=None):
    dil, n, _ = q.shape
    seg, nseg = _segments(n)
    nh = 2 if nseg > 1 else 0
    ng = 0 if gather is None else len(gather)

    def body(*refs):
        q_ref, k_ref, v_ref = refs[:3]
        halos = refs[3:3 + nh]
        o_ref, l_ref = refs[3 + nh + ng:5 + nh + ng]
        if ng:
            start, relay, finish = _gather_steps(refs[3 + nh:3 + nh + ng], refs[5 + nh + ng:5 + nh + 2 * ng],
                                                 *refs[5 + nh + 2 * ng:])
            first, last = _grid_edges((dil, nseg))
            pl.when(first)(start)
        own, both, head = _segment_masks(pl.program_id(1))
        lo, _ = _head_masks()

        def block(rows, keys_of, valid):
            for sl in _lane_blocks(B_WIDTH):
                kk, vv = keys_of(sl)
                sc = jnp.where(valid, _dot(_stack_heads(q_ref[rows, sl]), kk, NT), NEG_INF)
                mx = jnp.max(sc, axis=1, keepdims=True)
                p = jnp.exp(sc - mx)
                den = jnp.sum(p, axis=1, keepdims=True)
                out = _dot(p.astype(BF16), vv, NN) / den
                lse = mx + jnp.log(den)
                o_ref[rows, sl] = jnp.where(lo, out[:BAND], out[BAND:]).astype(BF16)
                l_ref[rows, sl] = jnp.where(lo, lse[:BAND], lse[BAND:])

        if nh:
            block(_block_rows(0), lambda sl: (jnp.concatenate([halos[0][:, sl], k_ref[0:BAND, sl]], axis=0),
                                              jnp.concatenate([halos[1][:, sl], v_ref[0:BAND, sl]], axis=0)), head)
        else:
            block(_block_rows(0), lambda sl: (k_ref[0:BAND, sl], v_ref[0:BAND, sl]), own)

        @pl.loop(1, seg)
        def _(g):
            block(_block_rows(g), lambda sl: (k_ref[_key_rows(g), sl], v_ref[_key_rows(g), sl]), both)

        if ng:
            @pl.when(last)
            def _():
                relay()
                finish()

    main, halo = _segment_specs(seg)
    res = pl.pallas_call(
        body, grid=(dil, nseg), in_specs=[main] * 3 + [halo] * nh + [ANY] * ng, out_specs=[main, main] + [ANY] * ng,
        out_shape=[jax.ShapeDtypeStruct((dil, n, B_WIDTH), BF16), jax.ShapeDtypeStruct((dil, n, B_WIDTH), F32)]
        + _gathered_shapes(gather or []),
        scratch_shapes=_gather_sems(ng) if ng else [],
        compiler_params=_cparams(*(["arbitrary"] * 2 if ng else ["parallel"] * 2)), name=name)(
            q, k, v, *([k, v] if nh else []), *(gather or []))
    return res[0], res[1], list(res[2:])


def _attn_bwd(q, k, v, do, lse, delta, name, scatter=None):
    dil, n, _ = q.shape
    seg, nseg = _segments(n)
    nh = 2 if nseg > 1 else 0
    ns = 0 if scatter is None else len(scatter[0])

    def body(*refs):
        q_ref, k_ref, v_ref, do_ref, lse_ref, dl_ref = refs[:6]
        halos = refs[6:6 + nh]
        dq_ref, dk_ref, dv_ref = refs[6 + nh + ns:9 + nh + ns]
        halo_out = refs[9 + nh + ns:9 + 2 * nh + ns]
        ck_ref, cv_ref = refs[9 + 2 * nh + 2 * ns:11 + 2 * nh + 2 * ns]
        if ns:
            start, finish = _scatter_steps(refs[6 + nh:6 + nh + ns], refs[9 + 2 * nh + ns:9 + 2 * nh + 2 * ns],
                                           *refs[11 + 2 * nh + 2 * ns:], scatter[1])
            first, last = _grid_edges((dil, nseg))
            pl.when(first)(start)
        own, both, head = _segment_masks(pl.program_id(1))
        lo, _ = _head_masks()
        lane = lax.broadcasted_iota(jnp.int32, (1, LANES), 1)

        def per_head(t):
            return jnp.concatenate(
                [jnp.sum(jnp.where(lane == first, t, 0.0), axis=1, keepdims=True) for first in (0, HEAD_DIM)], axis=0)

        def grads(rows, kk, vv, valid, sl):
            q2 = _stack_heads(q_ref[rows, sl])
            do2 = _stack_heads(do_ref[rows, sl])
            p = jnp.where(valid, jnp.exp(_dot(q2, kk, NT) - per_head(lse_ref[rows, sl])), 0.0)
            ds = (p * (_dot(do2, vv, NT) - per_head(dl_ref[rows, sl]))).astype(BF16)
            dq = _dot(ds, kk, NN)
            dq_ref[rows, sl] = jnp.where(lo, dq[:BAND], dq[BAND:]).astype(BF16)
            return _dot(ds, q2, TN), _dot(p.astype(BF16), do2, TN)

        for sl in _lane_blocks(B_WIDTH):
            if nh:
                dkk, dvv = grads(_block_rows(0), jnp.concatenate([halos[0][:, sl], k_ref[0:BAND, sl]], axis=0),
                                 jnp.concatenate([halos[1][:, sl], v_ref[0:BAND, sl]], axis=0), head, sl)
                halo_out[0][:, sl], halo_out[1][:, sl] = dkk[:BAND], dvv[:BAND]
                ck_ref[:, sl], cv_ref[:, sl] = dkk[BAND:], dvv[BAND:]
            else:
                ck_ref[:, sl], cv_ref[:, sl] = grads(_block_rows(0), k_ref[0:BAND, sl], v_ref[0:BAND, sl], own, sl)

        @pl.loop(1, seg)
        def _(g):
            before = _block_rows(g - 1)
            for sl in _lane_blocks(B_WIDTH):
                dkk, dvv = grads(_block_rows(g), k_ref[_key_rows(g), sl], v_ref[_key_rows(g), sl], both, sl)
                dk_ref[before, sl] = (ck_ref[:, sl] + dkk[:BAND]).astype(BF16)
                dv_ref[before, sl] = (cv_ref[:, sl] + dvv[:BAND]).astype(BF16)
                ck_ref[:, sl] = dkk[BAND:]
                cv_ref[:, sl] = dvv[BAND:]

        final = pl.ds((seg - 1) * BAND, BAND)
        dk_ref[final, :] = ck_ref[...].astype(BF16)
        dv_ref[final, :] = cv_ref[...].astype(BF16)

        if ns:
            pl.when(last)(finish)

    main, halo = _segment_specs(seg)
    shape = jax.ShapeDtypeStruct((dil, n, B_WIDTH), BF16)
    halo_shape = jax.ShapeDtypeStruct((dil, nseg, BAND, B_WIDTH), F32)
    halo_spec = pl.BlockSpec((None, None, BAND, B_WIDTH), lambda r, j: (r, j, 0, 0))
    res = pl.pallas_call(
        body, grid=(dil, nseg), in_specs=[main] * 6 + [halo] * nh + [ANY] * ns,
        out_specs=[main] * 3 + [halo_spec] * nh + [ANY] * ns,
        out_shape=[shape] * 3 + [halo_shape] * nh + (_scattered_shapes(scatter[1]) if ns else []),
        scratch_shapes=[pltpu.VMEM((BAND, B_WIDTH), F32)] * 2 + (_scatter_sems(ns) if ns else []),
        compiler_params=_cparams(*(["arbitrary"] * 2 if ns else ["parallel"] * 2)), name=name)(
            q, k, v, do, lse, delta, *([k, v] if nh else []), *(scatter[0] if ns else []))
    return res[0], res[1], res[2], (tuple(res[3:3 + nh]) if nh else None), list(res[3 + nh:])


def _attn_combine(outs, lses, gb, mixed, name, gather=None):
    s = mixed.shape[0]
    npat = len(DILATIONS)
    w = B_WIDTH
    ng = 0 if gather is None else len(gather)

    def body(*refs):
        o_refs, l_refs = refs[:npat], refs[npat:2 * npat]
        g_ref = refs[2 * npat]
        ob_ref = refs[2 * npat + 2 + ng]
        lse_refs = refs[2 * npat + 3 + ng:3 * npat + 3 + ng]
        mb_ref = refs[3 * npat + 3 + ng]
        stage = refs[3 * npat + 4 + 2 * ng]
        if ng:
            start, relay, finish = _gather_steps(refs[2 * npat + 2:2 * npat + 2 + ng],
                                                 refs[3 * npat + 4 + ng:3 * npat + 4 + 2 * ng],
                                                 *refs[3 * npat + 5 + 2 * ng:])
            first, last = _grid_edges((s // TR,))
            pl.when(first)(start)
        os_ = [o_refs[0][...].astype(F32)] + [_load_classes(r, stage, d) for r, d in zip(o_refs[1:], CLASS_DILS)]
        ls = [l_refs[0][...]] + [_load_classes(r, stage, d) for r, d in zip(l_refs[1:], CLASS_DILS)]
        mx = functools.reduce(jnp.maximum, ls)
        ws = [jnp.exp(l - mx) for l in ls]
        tot = functools.reduce(lambda a, b: a + b, ws)
        ob = functools.reduce(lambda a, b: a + b, [wt / tot * o for wt, o in zip(ws, os_)])
        ob_ref[...] = ob
        lse = mx + jnp.log(tot)
        _stage_put(stage, lse)
        lse_refs[0][...] = lse
        for ref, d in zip(lse_refs[1:], CLASS_DILS):
            _store_classes(stage, ref, d)
        mb_ref[...] = (ob * _rsq_mean(ob) * g_ref[...]).astype(BF16)

        if ng:
            @pl.when(last)
            def _():
                relay()
                finish()

    lay_specs = [_row_spec(w)] + [_class_spec(d) for d in CLASS_DILS]
    res = pl.pallas_call(
        body, grid=(s // TR,), in_specs=lay_specs * 2 + [_vec_spec(w), ANY] + [ANY] * ng,
        out_specs=[_row_spec(w)] + lay_specs + [_row_spec(w, 1)] + [ANY] * ng,
        out_shape=[jax.ShapeDtypeStruct((s, w), F32), jax.ShapeDtypeStruct((s, w), F32)]
        + [_class_shape(s, d, F32) for d in CLASS_DILS] + [jax.ShapeDtypeStruct(mixed.shape, mixed.dtype)]
        + _gathered_shapes(gather or []),
        scratch_shapes=[STAGE] + (_gather_sems(ng) if ng else []), input_output_aliases={2 * npat + 1: npat + 1},
        compiler_params=_cparams("arbitrary" if ng else "parallel"), name=name)(*outs, *lses, gb, mixed,
                                                                              *(gather or []))
    return res[0], dict(zip(DILATIONS, res[1:npat + 1])), res[npat + 1], list(res[npat + 2:])


def _attn_bwd_prep(dmixed, ob, gb, name):
    s = ob.shape[0]
    w = B_WIDTH
    nlay = len(DILATIONS)

    def body(dm_ref, ob_ref, g_ref, *rest):
        do_refs, dl_refs = rest[:nlay], rest[nlay:2 * nlay]
        dg_ref, stage = rest[2 * nlay:]
        _acc_init([dg_ref])
        ob = ob_ref[...]
        dob, dgt = _rms_bwd(ob, _rsq_mean(ob), g_ref[...], dm_ref[...])
        dg_ref[...] += _colsum(dgt)
        _stage_put(stage, dob)
        do_refs[0][...] = dob.astype(BF16)
        for ref, d in zip(do_refs[1:], CLASS_DILS):
            _store_classes(stage, ref, d)
        lo, hi = _head_masks()
        t = dob * ob
        for b, sl in enumerate(_lane_blocks(w)):
            tb = t[:, sl]
            s0 = jnp.sum(jnp.where(lo, tb, 0.0), axis=1, keepdims=True)
            s1 = jnp.sum(jnp.where(hi, tb, 0.0), axis=1, keepdims=True)
            stage[b] = jnp.where(lo, s0, s1)
        dl_refs[0][...] = _stage_get(stage)
        for ref, d in zip(dl_refs[1:], CLASS_DILS):
            _store_classes(stage, ref, d)

    lay_specs = [_row_spec(w)] + [_class_spec(d) for d in CLASS_DILS]
    shapes = lambda dt: [jax.ShapeDtypeStruct((s, w), dt)] + [_class_shape(s, d, dt) for d in CLASS_DILS]
    res = pl.pallas_call(
        body, grid=(s // TR,), in_specs=[_row_spec(w, 1), _row_spec(w), _vec_spec(w)],
        out_specs=lay_specs * 2 + [_vec_spec(w)],
        out_shape=shapes(BF16) + shapes(F32) + [jax.ShapeDtypeStruct((1, w), F32)],
        scratch_shapes=[STAGE],
        compiler_params=_cparams("arbitrary"), name=name)(dmixed, ob, gb)
    return dict(zip(DILATIONS, res[:nlay])), dict(zip(DILATIONS, res[nlay:2 * nlay])), res[2 * nlay]


def _rope_bwd(dqs, dks, dvs, halos, tabs, dproj, name):
    s = dproj.shape[0]
    half = ROT_DIM // 2
    scale = HEAD_DIM ** -0.5
    npat = len(DILATIONS)
    w = B_WIDTH
    nseg = halos[0].shape[0]
    per = s // nseg // TR

    def body(*refs):
        groups = [refs[g * npat:(g + 1) * npat] for g in range(3)]
        halo_refs = (None,) + tuple(refs[3 * npat:3 * npat + 2])
        c_ref, s1_ref, s2_ref, _, o_ref, stage = refs[3 * npat + 2:]
        i = pl.program_id(0)
        at_edge = ((i + 1) % per == 0) & ((i + 1) // per < nseg)

        def total(rs, halo_ref=None):
            acc = rs[0][...].astype(F32)
            if halo_ref is not None:
                edge = jnp.concatenate([jnp.zeros((TR - BAND, w), F32), halo_ref[...]], axis=0)
                acc = acc + jnp.where(at_edge, edge, 0.0)
            for ref, d in zip(rs[1:], CLASS_DILS):
                acc = acc + _load_classes(ref, stage, d)
            return acc

        def unrope(g):
            c, s1, s2 = c_ref[...], s1_ref[...], s2_ref[...]
            for sl in _lane_blocks(w):
                gb = g[:, sl]
                o = gb * c + pltpu.roll(gb * s1, half, 1) + pltpu.roll(gb * s2, LANES - half, 1)
                o_ref[:, sl] = o.astype(BF16)

        which = pl.program_id(1)

        @pl.when(which == 0)
        def _():
            unrope(total(groups[0]) * scale)

        @pl.when(which == 1)
        def _():
            unrope(total(groups[1], halo_refs[1]))

        @pl.when(which == 2)
        def _():
            o_ref[...] = total(groups[2], halo_refs[2]).astype(BF16)

    tab = pl.BlockSpec((TR, LANES), lambda i, j: (i, 0))
    nat = pl.BlockSpec((TR, w), lambda i, j: (i, 0))
    lay_specs = [nat] + [_class_spec(d) for d in CLASS_DILS]
    edge_spec = pl.BlockSpec((None, BAND, w), lambda i, j: (jnp.minimum((i + 1) // per, nseg - 1), 0, 0))
    first_col = 2 * A_WIDTH // w
    return pl.pallas_call(
        body, grid=(s // TR, 3), in_specs=lay_specs * 3 + [edge_spec] * 2 + [tab] * 3 + [ANY],
        out_specs=pl.BlockSpec((TR, w), lambda i, j: (i, first_col + j)),
        out_shape=jax.ShapeDtypeStruct(dproj.shape, dproj.dtype), scratch_shapes=[STAGE],
        input_output_aliases={3 * npat + 5: 0},
        compiler_params=_cparams("parallel", "arbitrary"), name=name)(*dqs, *dks, *dvs, *halos, *tabs, dproj)


TK = 512
HALO = 16
FFN_ROWS = 256
FFN_CHUNKS = tuple(slice(r, r + FFN_ROWS) for r in range(0, TM, FFN_ROWS))


def _row_of(v, r):
    rows = lax.broadcasted_iota(jnp.int32, (v.shape[0], 1), 0)
    return jnp.sum(jnp.where(rows == r, v, 0.0), axis=0, keepdims=True)


def _taps_before(x, halo):
    row = lax.broadcasted_iota(jnp.int32, (x.shape[0], 1), 0)
    m1 = jnp.where(row == 0, _row_of(halo, HALO - 1), pltpu.roll(x, 1, 0))
    m2 = jnp.where(row == 0, _row_of(halo, HALO - 2), jnp.where(row == 1, _row_of(halo, HALO - 1), pltpu.roll(x, 2, 0)))
    return m2, m1, x


def _taps_after(x, halo):
    rows = x.shape[0]
    row = lax.broadcasted_iota(jnp.int32, (rows, 1), 0)
    p1 = jnp.where(row == rows - 1, _row_of(halo, 0), pltpu.roll(x, rows - 1, 0))
    p2 = jnp.where(row == rows - 2, _row_of(halo, 0), jnp.where(row == rows - 1, _row_of(halo, 1), pltpu.roll(x, rows - 2, 0)))
    return p1, p2


def _conv_value(taps, cw_ref, cb_ref, h):
    return cb_ref[h] + cw_ref[h, 0:1, :] * taps[0] + cw_ref[h, 1:2, :] * taps[1] + cw_ref[h, 2:3, :] * taps[2]


def _ffn_weight_specs(ncol):
    per_up = (2 * D_FF // N_CHIPS) // TK
    per_dn = (D_FF // N_CHIPS) // TK
    wg = pl.BlockSpec((None, None, D_MODEL, TK), lambda i, j: (j // per_up, 0, 0, j % per_up))
    wv = pl.BlockSpec((None, None, D_MODEL, TK), lambda i, j: ((j + ncol) // per_up, 0, 0, (j + ncol) % per_up))
    wd = pl.BlockSpec((None, None, TK, D_MODEL), lambda i, j: (j // per_dn, 0, j % per_dn, 0))
    cw = pl.BlockSpec((2, 3, TK), lambda i, j: (0, 0, j))
    cb = pl.BlockSpec((2, 1, TK), lambda i, j: (0, 0, j))
    return wg, wv, wd, cw, cb


def _ffn_forward(h2, w_up, w_down, cw3, cb3, name, gather=None, post=None):
    s = h2.shape[0]
    nm, ncol = s // TM, D_FF // TK
    ng = 0 if gather is None else len(gather)
    npost = 0 if post is None else 3
    nout = 4 + (2 if post else 0)

    def body(*refs):
        h_ref, wg_ref, wv_ref, wd_ref, cw_ref, cb_ref = refs[:6]
        post_in = refs[6:6 + npost]
        g_in = refs[6 + npost:6 + npost + ng]
        outs = refs[6 + npost + ng:6 + npost + ng + nout]
        y_ref, up_ref, cv_ref, f_ref = outs[:4]
        g_out = refs[6 + npost + ng + nout:6 + npost + 2 * ng + nout]
        carry = refs[6 + npost + 2 * ng + nout]
        i, j = pl.program_id(0), pl.program_id(1)
        if ng:
            start, relay, finish = _gather_steps(g_in, g_out, *refs[7 + npost + 2 * ng + nout:])
            pl.when((i == 0) & (j == 0))(start)
            pl.when((i == nm - 1) & (j == 0))(relay)

        @pl.when((i == 0) & (j == 0))
        def _():
            carry[...] = jnp.zeros_like(carry)

        @pl.when(j == 0)
        def _():
            f_ref[...] = jnp.zeros_like(f_ref)

        ups = []
        for rs in FFN_CHUNKS:
            hc = h_ref[rs, :]
            ups.append([_dot(hc, w_ref[...], NN).astype(BF16) for w_ref in (wg_ref, wv_ref)])
            for hh in range(2):
                up_ref[hh, rs, :] = ups[-1][hh]
        before = [carry[j, hh] for hh in range(2)]
        for rs, up in zip(FFN_CHUNKS, ups):
            conv = []
            for hh in range(2):
                x = up[hh].astype(F32)
                conv.append(_conv_value(_taps_before(x, before[hh]), cw_ref, cb_ref, hh))
                cv_ref[hh, rs, :] = conv[hh].astype(BF16)
                before[hh] = x[x.shape[0] - HALO:, :]
            y = (_gelu_tanh(conv[0])[0] * conv[1]).astype(BF16)
            y_ref[rs, :] = y
            f_ref[rs, :] += _dot(y, wd_ref[...], NN)
        for hh in range(2):
            carry[j, hh] = before[hh]

        @pl.when(j == ncol - 1)
        def _():
            if post:
                f = f_ref[...]
                x1_ref, gp_ref, gn_ref = post_in
                x2 = x1_ref[...] + f * _rsq_mean(f) * gp_ref[...]
                outs[4][...] = x2
                outs[5][...] = (x2 * _rsq_mean(x2) * gn_ref[...]).astype(BF16)

        if ng:
            pl.when((i == nm - 1) & (j == ncol - 1))(finish)

    wg, wv, wd, cw, cb = _ffn_weight_specs(ncol)
    row = pl.BlockSpec((TM, D_MODEL), lambda i, j: (i, 0))
    vec = pl.BlockSpec((1, D_MODEL), lambda i, j: (0, 0))
    res = pl.pallas_call(
        body, grid=(nm, ncol),
        in_specs=[row, wg, wv, wd, cw, cb] + ([row, vec, vec] if post else []) + [ANY] * ng,
        out_specs=[pl.BlockSpec((TM, TK), lambda i, j: (i, j)), pl.BlockSpec((2, TM, TK), lambda i, j: (0, i, j)),
                   pl.BlockSpec((2, TM, TK), lambda i, j: (0, i, j)), row] + ([row, row] if post else [])
        + [ANY] * ng,
        out_shape=[jax.ShapeDtypeStruct((s, D_FF), BF16), jax.ShapeDtypeStruct((2, s, D_FF), BF16),
                   jax.ShapeDtypeStruct((2, s, D_FF), BF16), jax.ShapeDtypeStruct((s, D_MODEL), F32)]
        + ([jax.ShapeDtypeStruct((s, D_MODEL), F32), jax.ShapeDtypeStruct((s, D_MODEL), BF16)] if post else [])
        + _gathered_shapes(gather or []),
        scratch_shapes=[pltpu.VMEM((ncol, 2, HALO, TK), F32)] + (_gather_sems(ng) if ng else []),
        compiler_params=_cparams("arbitrary", "arbitrary"), name=name)(h2, w_up, w_up, w_down, cw3, cb3,
                                                                      *(post or []), *(gather or []))
    return res[:nout], list(res[nout:])


def _ffn_backward(df, w_up, w_down, up3, cv3, cw3, name, scatter=None):
    s = df.shape[0]
    nm, ncol = s // TM, D_FF // TK
    ns = 0 if scatter is None else len(scatter[0])

    def body(*refs):
        df_ref, wg_ref, wv_ref, wd_ref, cw_ref, up_ref, cv_ref = refs[:7]
        s_in = refs[7:7 + ns]
        dup_ref, dh_ref, sums_ref = refs[7 + ns:10 + ns]
        s_out = refs[10 + ns:10 + 2 * ns]
        carry = refs[10 + 2 * ns]
        i, j = pl.program_id(0), pl.program_id(1)
        if ns:
            start, finish = _scatter_steps(s_in, s_out, *refs[11 + 2 * ns:], scatter[1])
            pl.when((i == 0) & (j == 0))(start)

        @pl.when((i == 0) & (j == 0))
        def _():
            carry[...] = jnp.zeros_like(carry)
            sums_ref[...] = jnp.zeros_like(sums_ref)

        @pl.when(j == 0)
        def _():
            dh_ref[...] = jnp.zeros_like(dh_ref)

        chunks = FFN_CHUNKS[::-1]
        dys = [_dot(df_ref[rs, :], wd_ref[...], NT) for rs in chunks]
        row = lax.broadcasted_iota(jnp.int32, (8, 1), 0)
        after = [carry[j, hh] for hh in range(2)]
        upd = [jnp.zeros((8, TK), F32) for _ in range(2)]
        for rs, dy in zip(chunks, dys):
            act, grad = _gelu_tanh(cv_ref[0, rs, :].astype(F32))
            dcs = (dy * cv_ref[1, rs, :].astype(F32) * grad, dy * act)
            part = dh_ref[rs, :]
            for hh, w_ref in ((0, wg_ref), (1, wv_ref)):
                dc = dcs[hh]
                x = up_ref[hh, rs, :].astype(F32)
                after1, after2 = _taps_after(dc, after[hh])
                for ridx, sm in enumerate((_colsum(after2 * x), _colsum(after1 * x), _colsum(dc * x), _colsum(dc))):
                    upd[hh] = upd[hh] + jnp.where(row == ridx, sm, 0.0)
                dup = (cw_ref[hh, 2:3, :] * dc + cw_ref[hh, 1:2, :] * after1 + cw_ref[hh, 0:1, :] * after2).astype(BF16)
                after[hh] = dc[:HALO, :]
                dup_ref[hh, rs, :] = dup
                part = part + _dot(dup, w_ref[...], NT)
            dh_ref[rs, :] = part
        for hh in range(2):
            sums_ref[j, hh] += upd[hh]
            carry[j, hh] = after[hh]

        if ns:
            pl.when((i == nm - 1) & (j == ncol - 1))(finish)

    wg, wv, wd, cw, _ = _ffn_weight_specs(ncol)
    rev = lambda i: nm - 1 - i
    res = pl.pallas_call(
        body, grid=(nm, ncol),
        in_specs=[pl.BlockSpec((TM, D_MODEL), lambda i, j: (rev(i), 0)), wg, wv, wd, cw,
                  pl.BlockSpec((2, TM, TK), lambda i, j: (0, rev(i), j)),
                  pl.BlockSpec((2, TM, TK), lambda i, j: (0, rev(i), j))] + [ANY] * ns,
        out_specs=[pl.BlockSpec((2, TM, TK), lambda i, j: (0, rev(i), j)),
                   pl.BlockSpec((TM, D_MODEL), lambda i, j: (rev(i), 0)),
                   pl.BlockSpec((ncol, 2, 8, TK), lambda i, j: (0, 0, 0, 0))] + [ANY] * ns,
        out_shape=[jax.ShapeDtypeStruct((2, s, D_FF), BF16), jax.ShapeDtypeStruct((s, D_MODEL), F32),
                   jax.ShapeDtypeStruct((ncol, 2, 8, TK), F32)] + (_scattered_shapes(scatter[1]) if ns else []),
        scratch_shapes=[pltpu.VMEM((ncol, 2, HALO, TK), F32)] + (_scatter_sems(ns) if ns else []),
        compiler_params=_cparams("arbitrary", "arbitrary"), name=name)(df, w_up, w_up, w_down, cw3, up3, cv3,
                                                                      *(scatter[0] if ns else []))
    return res[:3], list(res[3:])


def _wspec(rows, cols, index_map):
    return pl.BlockSpec((None, None, rows, cols), index_map)


def _layer_forward(l, x0, h1, p, wg, tabs, gather=None, late=None, g_next=None):
    s = x0.shape[0]
    nm = s // TMM
    tag = f"_l{l}"
    riders = dict.fromkeys(DILATIONS)
    proj_rider = rope_rider = combine_rider = None
    if late is not None:
        cols = lambda t, parts: [t[:, i * t.shape[1] // parts:(i + 1) * t.shape[1] // parts] for i in range(parts)]
        (down_a, down_b), up_q = cols(late["w_down"], 2), cols(late["w_up"], 4)
        proj_rider, rope_rider, combine_rider = [late["w_out"], down_a], [up_q[2]], [up_q[3]]
        riders = dict(zip(DILATIONS, ([down_b], [up_q[0]], [up_q[1]])))
    proj = _matmul(
        h1, wg["w_in"], grid=(nm, N_CHIPS), a_spec=pl.BlockSpec((TMM, D_MODEL), lambda i, j: (i, 0)),
        b_spec=_wspec(D_MODEL, IN_COLS // N_CHIPS, lambda i, j: (j, 0, 0, 0)),
        o_spec=pl.BlockSpec((TMM, IN_COLS // N_CHIPS), lambda i, j: (i, j)), o_shape=(s, IN_COLS), o_dtype=BF16,
        dims=NN, nk=1, kaxis=None, acc_shape=None, name="proj" + tag, gather=proj_rider)
    if late is not None:
        proj, (w_out_all4, down_a) = proj
    ma, next_out = _mixer_a_fwd(proj, p["v_norm_g"], p["v_norm_b"], p["w_spatial"], p["bs_full"], p["out_norm_a"],
                                "mixer_a_fwd" + tag, [gather["w_out"]] if gather else None)
    q, k, v, rope_landed = _rope_fwd(proj, tabs, "rope_fwd" + tag, rope_rider)
    outs, lses, landed = zip(*[
        _attn_fwd(_as_classes(q[d]), _as_classes(k[d]), _as_classes(v[d]), f"attn_fwd_d{d}" + tag, riders[d])
        for d in DILATIONS])
    outs = [o.reshape(s, B_WIDTH) if d == 1 else o for o, d in zip(outs, DILATIONS)]
    lses = [t.reshape(s, B_WIDTH) if d == 1 else t for t, d in zip(lses, DILATIONS)]
    ob, lse, mixed, combine_landed = _attn_combine(outs, lses, p["out_norm_b"], ma, "attn_combine" + tag,
                                                   combine_rider)
    if late is not None:
        wg = dict(wg, w_out=w_out_all4, w_down=jnp.concatenate([down_a, landed[0][0]], axis=-1),
                  w_up=jnp.concatenate([landed[1][0], landed[2][0], rope_landed[0], combine_landed[0]], axis=-1))
    (y1, x1, h2), next_in = _mix_out_norm(mixed, wg["w_out"], x0, p["post_mix_norm"], p["pre_ffn_norm"],
                                          "mix_out" + tag, [gather["w_in"]] if gather else None)
    post = None if g_next is None else (x1, p["post_ffn_norm"], g_next)
    (y, up3, cv3, f, *after), next_ffn = _ffn_forward(h2, wg["w_up"], wg["w_down"], p["cw3"], p["cb3"], "ffn_fwd" + tag,
                                                      [gather["w_up"], gather["w_down"]] if gather else None, post)
    gathered = dict(w_in=next_in[0], w_out=next_out[0], w_up=next_ffn[0], w_down=next_ffn[1]) if gather else None
    saved = dict(x0=x0, h1=h1, proj=proj, q=q, k=k, v=v, ob=ob, lse=lse, mixed=mixed, y1=y1, x1=x1, h2=h2,
                 up3=up3, cv3=cv3, y=y, f=f)
    if after:
        saved.update(x2=after[0], h_next=after[1])
    return saved, gathered, wg


def _layer_backward(l, dx2, df, sv, p, wg, tabs, pos, scatter=None, hide=False):
    s = dx2.shape[0]
    nm = s // TMM
    tag = f"_l{l}"
    g = {}
    (dup3, dh2, conv_sums), scattered = _ffn_backward(df, wg["w_up"], wg["w_down"], sv["up3"], sv["cv3"], p["cw3"],
                                                      "ffn_bwd" + tag, scatter)
    sums = conv_sums.transpose(1, 2, 0, 3).reshape(2, 8, D_FF)
    g["conv_w"] = jnp.concatenate([sums[0, :3], sums[1, :3]], axis=1)
    g["conv_b"] = jnp.concatenate([sums[0, 3:4], sums[1, 3:4]], axis=1)
    tn = 1024
    done = {}
    gw_down = _matmul(
        sv["y"], df, grid=(D_FF // tn,), a_spec=pl.BlockSpec((s, tn), lambda k: (0, k)),
        b_spec=pl.BlockSpec((s, D_MODEL), lambda k: (0, 0)),
        o_spec=pl.BlockSpec((2, tn, D_MODEL // 2), lambda k: (0, k, 0)),
        o_shape=(2, D_FF, D_MODEL // 2), o_dtype=BF16,
        dims=TN, nk=1, kaxis=None, acc_shape=None, name="w_down_grad" + tag, halves=True)
    pair_sum = lambda n, grad, recv: _pair_sum({n: grad}, recv, pos, (n,), f"pair_sum_l{l}")
    gw_up, received = _matmul(
        sv["h2"], dup3, grid=(2 * D_FF // tn,), a_spec=pl.BlockSpec((s, D_MODEL), lambda n: (0, 0)),
        b_spec=pl.BlockSpec((None, s, tn), lambda n: (n // (D_FF // tn), 0, n % (D_FF // tn))),
        o_spec=pl.BlockSpec((None, D_MODEL, tn), lambda n: (n // 2, 0, n % 2)),
        o_shape=(N_CHIPS, D_MODEL, 2 * D_FF // N_CHIPS), o_dtype=BF16,
        dims=TN, nk=1, kaxis=None, acc_shape=None, name="w_up_grad" + tag, scatter=([gw_down], ("x:w_down",)))
    down_sums = pair_sum("w_down", gw_down, received)
    dx1, dy1, g["pre_ffn_norm"], g["post_mix_norm"] = _norm_bwd_mid(
        dx2, dh2, sv["x1"], p["pre_ffn_norm"], sv["y1"], p["post_mix_norm"], "norm_bwd_mid" + tag)
    w_out_all = pl.BlockSpec((N_CHIPS, None, D_MODEL // N_CHIPS, D_MODEL), lambda i: (0, 0, 0, 0))
    dmixed, received = _matmul(
        dy1, wg["w_out"], grid=(nm,), a_spec=pl.BlockSpec((TMM, D_MODEL), lambda i: (i, 0)), b_spec=w_out_all,
        o_spec=pl.BlockSpec((TMM, D_MODEL), lambda i: (i, 0)), o_shape=(s, D_MODEL), o_dtype=F32,
        dims=NT, nk=1, kaxis=None, acc_shape=None, name="mix_out_bwd" + tag, b_2d=(D_MODEL, D_MODEL),
        scatter=([gw_up], ("x:w_up",)))
    up_sums = pair_sum("w_up", gw_up, received)
    gw_out = _matmul(
        sv["mixed"], dy1, grid=(1,), a_spec=pl.BlockSpec((s, D_MODEL), lambda m: (0, 0)),
        b_spec=pl.BlockSpec((s, D_MODEL), lambda m: (0, 0)),
        o_spec=pl.BlockSpec((2, D_MODEL, D_MODEL // 2), lambda m: (0, 0, 0)),
        o_shape=(2, D_MODEL, D_MODEL // 2), o_dtype=BF16,
        dims=TN, nk=1, kaxis=None, acc_shape=None, name="w_out_grad" + tag, halves=True)
    dpa, g["out_norm_a"], g["v_norm_g"], g["v_norm_b"], dbs, g["w_spatial"], received = _mixer_a_bwd(
        sv["proj"], dmixed, p["v_norm_g"], p["v_norm_b"], p["w_spatial"], p["bs_full"], p["out_norm_a"],
        "mixer_a_bwd" + tag, ([gw_out], ("x:w_out",)))
    out_sums = pair_sum("w_out", gw_out, received)
    g["b_spatial"] = dbs[:, ::GROUP_DIM].T
    dob, delta, g["out_norm_b"] = _attn_bwd_prep(dmixed, sv["ob"], p["out_norm_b"], "attn_bwd_prep" + tag)
    riders = dict(zip(DILATIONS, ((down_sums, ("w_down",)), (up_sums, ("w_up:0",)), (up_sums, ("w_up:1",))))) if hide else {}
    dqs, dks, dvs, edges, received = zip(*[
        _attn_bwd(*(_as_classes(t[d]) for t in (sv["q"], sv["k"], sv["v"], dob, sv["lse"], delta)),
                  f"attn_bwd_d{d}" + tag, riders.get(d))
        for d in DILATIONS])
    if hide:
        done[("w_down",)] = (down_sums, received[0])
        done[("w_up",)] = (up_sums, [jnp.concatenate([received[1][0], received[2][0]], axis=-1)])
    nat = lambda ts: [t.reshape(s, B_WIDTH) if d == 1 else t for t, d in zip(ts, DILATIONS)]
    halos = [t[0] for t in edges[0]]
    dproj = _rope_bwd(nat(dqs), nat(dks), nat(dvs), halos, tabs, dpa, "rope_bwd" + tag)
    wcol = IN_COLS // N_CHIPS
    gw_in = _matmul(
        sv["h1"], dproj, grid=(N_CHIPS,), a_spec=pl.BlockSpec((s, D_MODEL), lambda n: (0, 0)),
        b_spec=pl.BlockSpec((s, wcol), lambda n: (0, n)),
        o_spec=pl.BlockSpec((None, D_MODEL, wcol), lambda n: (n, 0, 0)),
        o_shape=(N_CHIPS, D_MODEL, wcol), o_dtype=BF16,
        dims=TN, nk=1, kaxis=None, acc_shape=None, name="w_in_grad" + tag,
        scatter=(out_sums, ("w_out",)) if hide else None)
    if hide:
        gw_in, received = gw_in
        done[("w_out",)] = (out_sums, received)
        in_sums = _chip_sums(l, dict(w_in=gw_in), pos, ("w_in",))
        dh1, received = _proj_bwd(dproj, wg["w_in"], "proj_bwd" + tag, (in_sums, ("w_in",)))
        done[("w_in",)] = (in_sums, received)
        return dx1, dh1, {}, g, scattered, done
    dh1, received = _proj_bwd(dproj, wg["w_in"], "proj_bwd" + tag, ([gw_in], ("x:w_in",)))
    sums = dict(w_in=pair_sum("w_in", gw_in, received)[0], w_up=up_sums[0], w_out=out_sums[0], w_down=down_sums[0])
    return dx1, dh1, sums, g, scattered, done


SMALL = ("pre_mix_norm", "v_norm_g", "v_norm_b", "w_spatial", "b_spatial", "out_norm_a", "out_norm_b",
         "post_mix_norm", "pre_ffn_norm", "conv_b", "post_ffn_norm")
BIG = ("w_in", "w_out", "w_up", "w_down")
DEPTH = 2


def _layer_params(l, small, conv_w_full):
    p = {n: small[n][l].reshape(1, -1) for n in SMALL if n not in ("w_spatial", "b_spatial")}
    p["w_spatial"] = small["w_spatial"][l]
    p["bs_full"] = jnp.repeat(small["b_spatial"][l].T, GROUP_DIM, axis=1)
    p["cw3"] = conv_w_full[l].reshape(3, 2, D_FF).transpose(1, 0, 2)
    p["cb3"] = small["conv_b"][l].reshape(2, 1, D_FF)
    return p


def _mesh_pos():
    return lax.axis_index("x"), lax.axis_index("y"), lax.axis_index("c")


def _other_chips(x, y):
    return [(1 - x, y), (x, 1 - y), (1 - x, 1 - y)]


def _gathered_shapes(blocks):
    return [jax.ShapeDtypeStruct((N_CHIPS, 1) + a.shape, a.dtype) for a in blocks]


def _gather_sems(nw):
    n = 2 * nw * (N_CHIPS - 1) + nw
    return [pltpu.SemaphoreType.DMA((n,)), pltpu.SemaphoreType.DMA((n,))]


def _gather_steps(ins, outs, send, recv):
    nw, nrel = len(ins), N_CHIPS - 1
    x, y, c = _mesh_pos()
    mine, sibling, chips = 2 * x + y, (x, y, 1 - c), _other_chips(x, y)

    def copy(src, dst, slot, to):
        return pltpu.make_async_remote_copy(src_ref=src, dst_ref=dst, send_sem=send.at[slot],
                                            recv_sem=recv.at[slot], device_id=to, device_id_type=MESH)

    def half_rows(t, core):
        rows = ins[t].shape[0] // 2
        return pl.ds(pl.multiple_of(core * rows, rows), rows)

    def landing(t, chip, core):
        return outs[t].at[chip, 0, half_rows(t, core), :]

    slots = [(t, r, chip) for t in range(nw) for r, chip in enumerate(chips)]
    own = [copy(ins[t], outs[t].at[mine, 0], 2 * nw * nrel + t, sibling) for t in range(nw)]
    first = [copy(ins[t].at[half_rows(t, c), :], landing(t, mine, c), t * nrel + r, (px, py, c))
             for t, r, (px, py) in slots]
    relays = [copy(landing(t, 2 * px + py, c), landing(t, 2 * px + py, c), nw * nrel + t * nrel + r, sibling)
              for t, r, (px, py) in slots]

    def start():
        for cp in own + first:
            cp.start()

    def relay():
        for (t, r, (px, py)), cp in zip(slots, relays):
            copy(landing(t, 2 * px + py, c), landing(t, 2 * px + py, c), t * nrel + r, (px, py, c)).wait_recv()
            cp.start()

    def finish():
        for t, r, (px, py) in slots:
            passed = landing(t, 2 * px + py, 1 - c)
            copy(passed, passed, nw * nrel + t * nrel + r, sibling).wait_recv()
        for cp in first + relays:
            cp.wait_send()
        for cp in own:
            cp.wait()

    return start, relay, finish


HALF = 512

GRAD_GEOM = {"w_in": ("rows", D_MODEL, IN_COLS // N_CHIPS), "w_up": ("rows", D_MODEL, 2 * D_FF // N_CHIPS),
             "w_out": ("cols", D_MODEL, D_MODEL // N_CHIPS), "w_down": ("cols", D_FF, D_FF // N_CHIPS)}


def _exchange_shape(n):
    kind, a, b = GRAD_GEOM[n]
    return (N_CHIPS, HALF, b) if kind == "rows" else (a, HALF)


def _piece_shape(n):
    name, _, part = n.partition(":")
    kind, _, b = GRAD_GEOM[name]
    if part:
        assert kind == "rows"
        return (HALF, b // 2)
    return (HALF, b) if kind == "rows" else (b, HALF)


def _half_of(ref, n, core):
    if GRAD_GEOM[n][0] == "rows":
        return ref.at[:, pl.ds(pl.multiple_of(core * HALF, HALF), HALF), :]
    return ref.at[core]


def _piece_of(ref, n, chip):
    name, _, part = n.partition(":")
    kind, _, b = GRAD_GEOM[name]
    if part:
        return ref.at[chip, :, pl.ds(int(part) * (b // 2), b // 2)]
    return ref.at[chip] if kind == "rows" else ref.at[pl.ds(pl.multiple_of(chip * b, b), b), :]


def _pair_exchange(g, names, name):
    n = len(names)

    def body(*refs):
        send, recv = refs[2 * n:]
        x, y, c = _mesh_pos()
        o = 1 - c
        cps = [pltpu.make_async_remote_copy(src_ref=_half_of(refs[t], nm, o), dst_ref=refs[n + t], send_sem=send.at[t],
                                            recv_sem=recv.at[t], device_id=(x, y, o), device_id_type=MESH)
               for t, nm in enumerate(names)]
        for cp in cps:
            cp.start()
        for cp in cps:
            cp.wait()

    return pl.pallas_call(
        body, in_specs=[ANY] * n, out_specs=[ANY] * n,
        out_shape=[jax.ShapeDtypeStruct(_exchange_shape(nm), BF16) for nm in names],
        scratch_shapes=[pltpu.SemaphoreType.DMA((n,)), pltpu.SemaphoreType.DMA((n,))],
        name=name)(*[g[nm] for nm in names])


def _pair_sum(g, recv, pos, names, name_prefix):
    def add(a, b, grid, a_spec, b_spec, name):
        def body(pos_ref, a_ref, b_ref, o_ref):
            o_ref[...] = (a_ref[...].astype(F32) + b_ref[...].astype(F32)).astype(BF16)

        return pl.pallas_call(
            body, grid_spec=pltpu.PrefetchScalarGridSpec(
                num_scalar_prefetch=1, grid=grid, in_specs=[a_spec, b_spec], out_specs=b_spec),
            out_shape=jax.ShapeDtypeStruct(b.shape, BF16), compiler_params=_cparams("parallel"), name=name)(pos, a, b)

    out = []
    for nm, r in zip(names, recv):
        kind, rows, width = GRAD_GEOM[nm]
        if kind == "rows":
            out.append(add(g[nm], r, (N_CHIPS,), pl.BlockSpec((None, HALF, width), lambda j, pos: (j, pos[2], 0)),
                           pl.BlockSpec((None, HALF, width), lambda j, pos: (j, 0, 0)), f"{name_prefix}_{nm}"))
        else:
            out.append(add(g[nm], r, (rows // D_MODEL,), pl.BlockSpec((None, D_MODEL, HALF), lambda j, pos: (pos[2], j, 0)),
                           pl.BlockSpec((D_MODEL, HALF), lambda j, pos: (j, 0)), f"{name_prefix}_{nm}"))
    return out


def _scattered_shapes(names):
    return [jax.ShapeDtypeStruct(_exchange_shape(nm[2:]) if nm.startswith("x:") else (N_CHIPS - 1,) + _piece_shape(nm),
                                 BF16) for nm in names]


def _scatter_sems(n):
    return [pltpu.SemaphoreType.DMA((n * (N_CHIPS - 1),)), pltpu.SemaphoreType.DMA((n * (N_CHIPS - 1),))]


def _scatter_steps(sums, outs, send, recv, names):
    nrel = N_CHIPS - 1
    x, y, c = _mesh_pos()
    cps = [pltpu.make_async_remote_copy(
        src_ref=_half_of(sums[t], nm[2:], 1 - c), dst_ref=outs[t], send_sem=send.at[t * nrel],
        recv_sem=recv.at[t * nrel], device_id=(x, y, 1 - c), device_id_type=MESH)
        for t, nm in enumerate(names) if nm.startswith("x:")]
    for r, (px, py) in enumerate(_other_chips(x, y)):
        for t, nm in enumerate(names):
            if nm.startswith("x:"):
                continue
            cps.append(pltpu.make_async_remote_copy(
                src_ref=_piece_of(sums[t], nm, 2 * px + py), dst_ref=outs[t].at[r], send_sem=send.at[t * nrel + r],
                recv_sem=recv.at[t * nrel + r], device_id=(px, py, c), device_id_type=MESH))

    def start():
        for cp in cps:
            cp.start()

    def finish():
        for cp in cps:
            cp.wait()

    return start, finish


def _chip_scatter(sums, names, name):
    n = len(names)

    def body(*refs):
        start, finish = _scatter_steps(refs[:n], refs[n:2 * n], *refs[2 * n:], names)
        start()
        finish()

    return pl.pallas_call(
        body, in_specs=[ANY] * n, out_specs=[ANY] * n, out_shape=_scattered_shapes(names),
        scratch_shapes=_scatter_sems(n), name=name)(*sums)


def _chip_sum(sums, recv, pos, names, name_prefix):
    def add(a, b, a_spec, shape, name):
        def body(pos_ref, a_ref, b_ref, o_ref):
            tot = a_ref[...].astype(F32)
            for r in range(N_CHIPS - 1):
                tot = tot + b_ref[r].astype(F32)
            o_ref[...] = tot

        return pl.pallas_call(
            body, grid_spec=pltpu.PrefetchScalarGridSpec(
                num_scalar_prefetch=1, grid=(1,), in_specs=[a_spec, pl.BlockSpec(b.shape, lambda i, pos: (0, 0, 0))],
                out_specs=pl.BlockSpec((None,) + shape, lambda i, pos: (pos[2], 0, 0))),
            out_shape=jax.ShapeDtypeStruct((2,) + shape, F32), compiler_params=_cparams("arbitrary"),
            name=name)(pos, a, b)

    chip = lambda pos: 2 * pos[0] + pos[1]
    out = []
    for nm, a, b in zip(names, sums, recv):
        shape = _piece_shape(nm)
        if GRAD_GEOM[nm][0] == "rows":
            spec = pl.BlockSpec((None,) + shape, lambda i, pos: (chip(pos), 0, 0))
        else:
            spec = pl.BlockSpec(shape, lambda i, pos: (chip(pos), 0))
        out.append(add(a, b, spec, shape, f"{name_prefix}_{nm}"))
    return out


def _pair_share(totals, name):
    n = len(totals)

    def body(*refs):
        ins, outs = refs[:n], refs[n:2 * n]
        send, recv = refs[2 * n:]
        x, y, c = _mesh_pos()
        o = 1 - c
        cps = [pltpu.make_async_remote_copy(src_ref=ins[t].at[c], dst_ref=outs[t].at[c], send_sem=send.at[t],
                                            recv_sem=recv.at[t], device_id=(x, y, o), device_id_type=MESH)
               for t in range(n)]
        for cp in cps:
            cp.start()
        for t in range(n):
            pltpu.make_async_remote_copy(src_ref=ins[t].at[o], dst_ref=outs[t].at[o], send_sem=send.at[t],
                                         recv_sem=recv.at[t], device_id=(x, y, o), device_id_type=MESH).wait_recv()
        for cp in cps:
            cp.wait_send()

    return pl.pallas_call(
        body, in_specs=[ANY] * n, out_specs=[ANY] * n,
        out_shape=[jax.ShapeDtypeStruct(t.shape, t.dtype) for t in totals],
        scratch_shapes=[pltpu.SemaphoreType.DMA((n,)), pltpu.SemaphoreType.DMA((n,))],
        input_output_aliases={t: t for t in range(n)}, name=name)(*totals)


def _chip_sums(l, g, pos, names):
    tag = f"l{l}_" + "_".join(names)
    recv = _pair_exchange(g, names, "pair_exchange_" + tag)
    return _pair_sum(g, recv, pos, names, "pair_sum_" + tag)


def _gradient_shards(l, sums, scattered, pos, names):
    tag = f"l{l}_" + "_".join(names)
    halves = _pair_share(_chip_sum(sums, scattered, pos, names, "chip_sum_" + tag), "pair_share_" + tag)
    out = {}
    for nm, t in zip(names, halves):
        rows, cols = _piece_shape(nm)
        out[nm] = t.reshape(2 * rows, cols) if GRAD_GEOM[nm][0] == "rows" else t
    return out


def _allreduce_small(packed, name):
    rows = packed.shape[0]
    half = rows // 2
    assert half % 8 == 0

    def body(x_ref, out_ref, sib, parts, done, landed, send_sems, recv_sems):
        x, y, c = _mesh_pos()
        sibling = (x, y, 1 - c)
        mine = pl.ds(pl.multiple_of(c * half, 8), half)
        other = pl.ds(pl.multiple_of((1 - c) * half, 8), half)

        def copy(k, src, dst, to):
            return pltpu.make_async_remote_copy(src_ref=src, dst_ref=dst, send_sem=send_sems.at[k],
                                                recv_sem=recv_sems.at[k], device_id=to, device_id_type=MESH)

        swap = copy(0, x_ref.at[other, :], sib, sibling)
        swap.start()
        swap.wait()
        parts[0] = x_ref[mine, :] + sib[...]
        sends = [copy(1 + j, parts.at[0], parts.at[1 + j], (*chip, c)) for j, chip in enumerate(_other_chips(x, y))]
        for cp in sends:
            cp.start()
        for cp in sends:
            cp.wait()
        tot = None
        for chip in range(N_CHIPS):
            rel = jnp.bitwise_xor(chip, 2 * x + y)
            slot = jnp.where(rel == 0, 0, jnp.where(rel == 2, 1, jnp.where(rel == 1, 2, 3)))
            term = parts[slot]
            tot = term if tot is None else tot + term
        out_ref[mine, :] = tot
        done[...] = tot
        back = copy(4, done, landed, sibling)
        back.start()
        back.wait()
        out_ref[other, :] = landed[...]

    vmem = pl.BlockSpec(memory_space=pltpu.VMEM)
    return pl.pallas_call(
        body, in_specs=[vmem], out_specs=vmem, out_shape=jax.ShapeDtypeStruct((rows, LANES), F32),
        scratch_shapes=[pltpu.VMEM((half, LANES), F32), pltpu.VMEM((N_CHIPS, half, LANES), F32),
                        pltpu.VMEM((half, LANES), F32), pltpu.VMEM((half, LANES), F32),
                        pltpu.SemaphoreType.DMA((5,)), pltpu.SemaphoreType.DMA((5,))],
        compiler_params=pltpu.CompilerParams(vmem_limit_bytes=VMEM_LIMIT_BYTES),
        name=name)(packed)


def _adamw_step(w, g, m, v):
    mn = ADAM_B1 * m + (1.0 - ADAM_B1) * g
    vn = ADAM_B2 * v + (1.0 - ADAM_B2) * (g * g)
    m_hat = mn / (1.0 - ADAM_B1 ** ADAM_STEP)
    v_hat = vn / (1.0 - ADAM_B2 ** ADAM_STEP)
    return -ADAM_LR * (m_hat / (jnp.sqrt(v_hat) + ADAM_EPS) + ADAM_WD * w), mn, vn


def _adamw(w, g, m, v, name):
    rows, cols = w.shape
    tr = 256 if rows % 256 == 0 else rows

    def body(w_ref, g_ref, m_ref, v_ref, d_ref, mo_ref, vo_ref):
        d_ref[...], mo_ref[...], vo_ref[...] = _adamw_step(w_ref[...], g_ref[...], m_ref[...], v_ref[...])

    spec = pl.BlockSpec((tr, cols), lambda i: (i, 0))
    return pl.pallas_call(
        body, grid=(rows // tr,), in_specs=[spec] * 4, out_specs=[spec] * 3,
        out_shape=[jax.ShapeDtypeStruct((rows, cols), F32)] * 3, compiler_params=_cparams("parallel"),
        name=name)(w, g, m, v)


def _adamw_layers(w, gs, m, v, name):
    depth, rows, cols = w.shape
    tr = 256
    nblk = rows // tr
    split = gs[0].ndim == 3

    def body(w_ref, m_ref, v_ref, *rest):
        g_refs, (g_out, d_ref, mo_ref, vo_ref) = rest[:depth], rest[depth:]
        layer = pl.program_id(0)
        load = (lambda r: jnp.concatenate([r[0], r[1]], axis=-1)) if split else (lambda r: r[...])
        gv = load(g_refs[0])
        for k in range(1, depth):
            gv = jnp.where(layer == k, load(g_refs[k]), gv)
        g_out[...] = gv
        d_ref[...], mo_ref[...], vo_ref[...] = _adamw_step(w_ref[...], gv, m_ref[...], v_ref[...])

    def g_spec(k):
        tile = lambda l, i: jnp.where(l == k, i, jnp.where(l < k, 0, nblk - 1))
        if split:
            return pl.BlockSpec((2, tr, cols // 2), lambda l, i: (0, tile(l, i), 0))
        return pl.BlockSpec((tr, cols), lambda l, i: (tile(l, i), 0))

    spec = pl.BlockSpec((None, tr, cols), lambda l, i: (l, i, 0))
    return pl.pallas_call(
        body, grid=(depth, nblk), in_specs=[spec] * 3 + [g_spec(k) for k in range(depth)], out_specs=[spec] * 4,
        out_shape=[jax.ShapeDtypeStruct(w.shape, F32)] * 4, compiler_params=_cparams("parallel", "parallel"),
        name=name)(w, m, v, *gs)


def _adamw_nd(w, g, m, v, name):
    cols = w.shape[-1] if w.shape[-1] % LANES == 0 else LANES
    outs = _adamw(*(t.reshape(-1, cols) for t in (w, g, m, v)), name)
    return tuple(t.reshape(w.shape) for t in outs)


def _pack(arrays):
    return jnp.concatenate([a.reshape(-1, LANES) for a in arrays], axis=0)


def _unpack(packed, shapes):
    out, row = [], 0
    for sh in shapes:
        n = math.prod(sh) // LANES
        out.append(packed[row:row + n].reshape(sh))
        row += n
    return out


WEIGHTS = ("pre_mix_norm", "w_in", "v_norm_g", "v_norm_b", "w_spatial", "b_spatial", "out_norm_a", "out_norm_b",
           "w_out", "post_mix_norm", "pre_ffn_norm", "w_up", "conv_w", "conv_b", "w_down", "post_ffn_norm")


def kernel(x, pre_mix_norm, w_in, v_norm_g, v_norm_b, w_spatial, b_spatial, out_norm_a, out_norm_b, w_out, post_mix_norm, pre_ffn_norm, w_up, conv_w, conv_b, w_down, post_ffn_norm, loss_target, m_pre_mix_norm, m_w_in, m_v_norm_g, m_v_norm_b, m_w_spatial, m_b_spatial, m_out_norm_a, m_out_norm_b, m_w_out, m_post_mix_norm, m_pre_ffn_norm, m_w_up, m_conv_w, m_conv_b, m_w_down, m_post_ffn_norm, v_pre_mix_norm, v_w_in, v_v_norm_g, v_v_norm_b, v_w_spatial, v_b_spatial, v_out_norm_a, v_out_norm_b, v_w_out, v_post_mix_norm, v_pre_ffn_norm, v_w_up, v_conv_w, v_conv_b, v_w_down, v_post_ffn_norm):
    w = dict(pre_mix_norm=pre_mix_norm, w_in=w_in, v_norm_g=v_norm_g, v_norm_b=v_norm_b, w_spatial=w_spatial,
             b_spatial=b_spatial, out_norm_a=out_norm_a, out_norm_b=out_norm_b, w_out=w_out,
             post_mix_norm=post_mix_norm, pre_ffn_norm=pre_ffn_norm, w_up=w_up, conv_w=conv_w, conv_b=conv_b,
             w_down=w_down, post_ffn_norm=post_ffn_norm)
    m = dict(pre_mix_norm=m_pre_mix_norm, w_in=m_w_in, v_norm_g=m_v_norm_g, v_norm_b=m_v_norm_b,
             w_spatial=m_w_spatial, b_spatial=m_b_spatial, out_norm_a=m_out_norm_a, out_norm_b=m_out_norm_b,
             w_out=m_w_out, post_mix_norm=m_post_mix_norm, pre_ffn_norm=m_pre_ffn_norm, w_up=m_w_up,
             conv_w=m_conv_w, conv_b=m_conv_b, w_down=m_w_down, post_ffn_norm=m_post_ffn_norm)
    v = dict(pre_mix_norm=v_pre_mix_norm, w_in=v_w_in, v_norm_g=v_v_norm_g, v_norm_b=v_v_norm_b,
             w_spatial=v_w_spatial, b_spatial=v_b_spatial, out_norm_a=v_out_norm_a, out_norm_b=v_out_norm_b,
             w_out=v_w_out, post_mix_norm=v_post_mix_norm, pre_ffn_norm=v_pre_ffn_norm, w_up=v_w_up,
             conv_w=v_conv_w, conv_b=v_conv_b, w_down=v_w_down, post_ffn_norm=v_post_ffn_norm)
    pos = jnp.stack([lax.axis_index("x"), lax.axis_index("y"), lax.axis_index("c")]).astype(jnp.int32)
    chip = 2 * lax.axis_index("x") + lax.axis_index("y")

    cw_cols = conv_w.shape[-1]
    blocks = [{n: w[n][l].astype(BF16) for n in BIG} for l in range(DEPTH)]
    small = {n: w[n] for n in SMALL}
    xs, target = x[0], loss_target[0]
    xin = xs
    h, (w_in0, cw_all) = _rms_cast(xin, small["pre_mix_norm"][0].reshape(1, -1), "pre_mix_l0",
                                   [blocks[0]["w_in"], conv_w.reshape(-1, LANES)])
    wg = dict(w_in=w_in0)
    conv_w_full = cw_all.reshape(N_CHIPS, DEPTH, 3, cw_cols).transpose(1, 2, 0, 3).reshape(DEPTH, 3, 2 * D_FF)

    tabs = _rope_tables(xs.shape[0])
    params = [_layer_params(l, small, conv_w_full) for l in range(DEPTH)]
    saved, wgs = [], []
    for l in range(DEPTH):
        sv, gathered, wg = _layer_forward(l, xin, h, params[l], wg, tabs,
                                          blocks[l + 1] if l + 1 < DEPTH else None,
                                          blocks[0] if l == 0 else None,
                                          params[l + 1]["pre_mix_norm"] if l + 1 < DEPTH else None)
        saved.append(sv)
        wgs.append(wg)
        if l + 1 < DEPTH:
            wg = gathered
            xin, h = sv["x2"], sv["h_next"]
    loss_part, dx, df, g_post = _loss_norm_bwd(saved[-1]["x1"], saved[-1]["f"], params[-1]["post_ffn_norm"], target,
                                               "loss")
    smalls, shards = [None] * DEPTH, [{} for _ in range(DEPTH)]
    pending = None
    for l in reversed(range(DEPTH)):
        dx1, dh1, big, smalls[l], scattered, done = _layer_backward(l, dx, df, saved[l], params[l], wgs[l], tabs, pos,
                                                                    pending[1:] if pending else None, hide=l == 0)
        smalls[l]["post_ffn_norm"] = g_post
        if l > 0:
            dx, smalls[l]["pre_mix_norm"], df, g_post = _norm_bwd_in_out(
                dx1, dh1, saved[l]["x0"], params[l]["pre_mix_norm"], saved[l - 1]["f"], params[l - 1]["post_ffn_norm"],
                f"norm_bwd_in_out_l{l}")
        else:
            dx, smalls[l]["pre_mix_norm"] = _norm_bwd_in(dx1, dh1, saved[l]["x0"], params[l]["pre_mix_norm"],
                                                         "norm_bwd_in_l0")
        if pending:
            shards[pending[0]].update(_gradient_shards(pending[0], pending[1], scattered, pos, pending[2]))
        if done:
            shards[l].update(_gradient_shards(
                l, [t for sums, _ in done.values() for t in sums], [t for _, received in done.values() for t in received],
                pos, tuple(n for names in done for n in names)))
        names = tuple(big)
        pending = (l, [big[n] for n in names], names) if names else None
    if pending:
        shards[pending[0]].update(_gradient_shards(
            pending[0], pending[1], _chip_scatter(pending[1], pending[2], f"chip_scatter_l{pending[0]}"), pos,
            pending[2]))

    small_shapes = [w[n].shape for n in SMALL]
    stacked = [jnp.stack([smalls[l][n].reshape(w[n].shape[1:]) for l in range(DEPTH)]) for n in SMALL]
    cw_grad = jnp.stack([smalls[l]["conv_w"] for l in range(DEPTH)])
    packed = _pack(stacked + [cw_grad, loss_part])
    total = _allreduce_small(packed, "allreduce_small")
    parts = _unpack(total, small_shapes + [cw_grad.shape, (8, LANES)])
    g_small = dict(zip(SMALL, parts[:len(SMALL)]))
    loss = parts[-1][0, 0]
    g_conv_w = lax.dynamic_slice(parts[-2], (0, 0, chip * cw_cols), conv_w.shape)

    grads = dict(g_small, conv_w=g_conv_w)

    dp, mp, vp = _adamw(_pack([w[n] for n in SMALL]), _pack([g_small[n] for n in SMALL]),
                        _pack([m[n] for n in SMALL]), _pack([v[n] for n in SMALL]), "adamw_small")
    delta = dict(zip(SMALL, _unpack(dp, small_shapes)))
    new_m = dict(zip(SMALL, _unpack(mp, small_shapes)))
    new_v = dict(zip(SMALL, _unpack(vp, small_shapes)))
    delta["conv_w"], new_m["conv_w"], new_v["conv_w"] = _adamw_nd(w["conv_w"], g_conv_w, m["conv_w"], v["conv_w"],
                                                                  "adamw_conv_w")
    for n in BIG:
        grads[n], delta[n], new_m[n], new_v[n] = _adamw_layers(w[n], [shards[l][n] for l in range(DEPTH)], m[n],
                                                               v[n], "adamw_" + n)

    return (loss, dx[None], *[grads[n] for n in WEIGHTS], *[delta[n] for n in WEIGHTS],
            *[new_m[n] for n in WEIGHTS], *[new_v[n] for n in WEIGHTS])
```

```python
import functools
import math

import jax
import jax.numpy as jnp
import numpy as np
from jax import lax
from jax.experimental import pallas as pl
from jax.experimental.pallas import tpu as pltpu

F32 = jnp.float32
BF16 = jnp.bfloat16
MESH = pl.DeviceIdType.MESH

D_MODEL = 1024
A_WIDTH = 512
A_GROUPS = 4
GROUP_DIM = 128
CHUNK = 128
B_WIDTH = 512
HEAD_DIM = 64
ROT_DIM = 16
ROPE_THETA = 500000.0
DILATIONS = (1, 4, 16)
BAND = 128
IN_COLS = 2560
D_FF = 4096
EPS = 1e-6
NEG_INF = -1e30
N_CHIPS = 4
LANES = 128

ADAM_LR = 0.001
ADAM_B1 = 0.9
ADAM_B2 = 0.999
ADAM_EPS = 1e-08
ADAM_WD = 0.01
ADAM_STEP = 10

VMEM_LIMIT_BYTES = 56 * 1024 * 1024
RSQRT2 = 0.7071067811865476
INV_SQRT_2PI = 0.3989422804014327
GELU_C = 0.7978845608028654
GELU_A = 0.044715

ANY = pl.BlockSpec(memory_space=pl.ANY)
NN = ((1,), (0,))
NT = ((1,), (1,))
TN = ((0,), (0,))


def _cparams(*sem):
    return pltpu.CompilerParams(dimension_semantics=sem, vmem_limit_bytes=VMEM_LIMIT_BYTES)


def _dot(a, b, dims):
    return lax.dot_general(a, b, (dims, ((), ())), preferred_element_type=F32)


def _rsq_mean(a):
    return lax.rsqrt(jnp.mean(a * a, axis=-1, keepdims=True) + EPS)


def _rms_bwd(a, r, g, dz):
    t = dz * g
    da = r * t - a * (r * r * r) * jnp.mean(t * a, axis=-1, keepdims=True)
    return da, dz * a * r


def _colsum(a):
    return jnp.sum(a, axis=0, keepdims=True)


def _gelu_tanh(x):
    u = x * x
    t = jnp.tanh(x * (GELU_C + (GELU_C * GELU_A) * u))
    hx = 0.5 * x
    act = hx + hx * t
    grad = 0.5 + 0.5 * t + (hx - hx * t * t) * (GELU_C + (3.0 * GELU_C * GELU_A) * u)
    return act, grad


def _grid_edges(grid):
    ids = [pl.program_id(ax) for ax in range(len(grid))]
    first = functools.reduce(jnp.logical_and, [i == 0 for i in ids])
    last = functools.reduce(jnp.logical_and, [i == n - 1 for i, n in zip(ids, grid)])
    return first, last


def _matmul(a, b, *, grid, a_spec, b_spec, o_spec, o_shape, o_dtype, dims, nk, kaxis, acc_shape, name, b_2d=None,
            halves=False, scatter=None, gather=None):
    assert scatter is None or gather is None
    ns = len(scatter[0]) if scatter else len(gather) if gather else 0

    def body(*refs):
        a_ref, b_ref = refs[:2]
        o_ref = refs[2 + ns]
        scratch = refs[3 + 2 * ns:]
        if ns:
            first, last = _grid_edges(grid)
            if scatter:
                start, finish = _scatter_steps(refs[2:2 + ns], refs[3 + ns:3 + 2 * ns], scratch[-2], scratch[-1],
                                               scatter[1])
            else:
                start, relay, last_wait = _gather_steps(refs[2:2 + ns], refs[3 + ns:3 + 2 * ns], scratch[-2],
                                                        scratch[-1])

                def finish():
                    relay()
                    last_wait()
            pl.when(first)(start)
        def store(val):
            if halves:
                half = val.shape[1] // 2
                o_ref[0] = val[:, :half].astype(o_dtype)
                o_ref[1] = val[:, half:].astype(o_dtype)
            else:
                o_ref[...] = val.astype(o_dtype)

        bv = b_ref[...] if b_2d is None else b_ref[...].reshape(b_2d)
        part = _dot(a_ref[...], bv, dims)
        if nk == 1:
            store(part)
        else:
            acc = scratch[0]
            k = pl.program_id(kaxis)

            @pl.when(k == 0)
            def _():
                acc[...] = part

            @pl.when(k > 0)
            def _():
                acc[...] += part

            @pl.when(k == nk - 1)
            def _():
                store(acc[...])

        if ns:
            pl.when(last)(finish)

    sem = tuple("arbitrary" if (ns or (nk > 1 and ax == kaxis)) else "parallel" for ax in range(len(grid)))
    riding = list(scatter[0]) if scatter else list(gather or [])
    rider_shapes = _scattered_shapes(scatter[1]) if scatter else _gathered_shapes(riding)
    rider_sems = _scatter_sems(ns) if scatter else _gather_sems(ns) if gather else []
    res = pl.pallas_call(
        body, grid=grid, in_specs=[a_spec, b_spec] + [ANY] * ns, out_specs=[o_spec] + [ANY] * ns,
        out_shape=[jax.ShapeDtypeStruct(o_shape, o_dtype)] + rider_shapes,
        scratch_shapes=([pltpu.VMEM(acc_shape, F32)] if nk > 1 else []) + rider_sems,
        compiler_params=_cparams(*sem), name=name)(a, b, *riding)
    return (res[0], list(res[1:])) if ns else res[0]


def _mix_out_norm(mixed, w_out, x0, g_post, g_next, name, gather=None):
    s, d = x0.shape
    tm = 512
    ng = 0 if gather is None else len(gather)

    def body(a_ref, w_ref, x_ref, gp_ref, gn_ref, *rest):
        y_ref, x1_ref, h_ref = rest[ng:ng + 3]
        if ng:
            start, relay, finish = _gather_steps(rest[:ng], rest[ng + 3:2 * ng + 3], *rest[2 * ng + 3:])
            first, last = _grid_edges((s // tm,))
            pl.when(first)(start)
        y = _dot(a_ref[...], w_ref[...].reshape(d, d), NN)
        y_ref[...] = y
        x1 = x_ref[...] + y * _rsq_mean(y) * gp_ref[...]
        x1_ref[...] = x1
        h_ref[...] = (x1 * _rsq_mean(x1) * gn_ref[...]).astype(BF16)

        if ng:
            @pl.when(last)
            def _():
                relay()
                finish()

    row = pl.BlockSpec((tm, d), lambda i: (i, 0))
    vec = pl.BlockSpec((1, d), lambda i: (0, 0))
    res = pl.pallas_call(
        body, grid=(s // tm,),
        in_specs=[row, pl.BlockSpec((N_CHIPS, None, d // N_CHIPS, d), lambda i: (0, 0, 0, 0)), row, vec, vec]
        + [ANY] * ng,
        out_specs=[row, row, row] + [ANY] * ng,
        out_shape=[jax.ShapeDtypeStruct((s, d), F32), jax.ShapeDtypeStruct((s, d), F32),
                   jax.ShapeDtypeStruct((s, d), BF16)] + _gathered_shapes(gather or []),
        scratch_shapes=_gather_sems(ng) if ng else [],
        compiler_params=_cparams("arbitrary" if ng else "parallel"), name=name)(mixed, w_out, x0, g_post, g_next,
                                                                              *(gather or []))
    return res[:3], list(res[3:])


def _proj_bwd(dproj, w_in, name, scatter=None):
    s = dproj.shape[0]
    wcol = IN_COLS // N_CHIPS
    ns = 0 if scatter is None else len(scatter[0])

    def body(*refs):
        a_ref, w_ref = refs[:2]
        o_ref = refs[2 + ns]
        if ns:
            start, finish = _scatter_steps(refs[2:2 + ns], refs[3 + ns:3 + 2 * ns], *refs[3 + 2 * ns:], scatter[1])
            first, last = _grid_edges((s // TMM,))
            pl.when(first)(start)
        acc = _dot(a_ref[:, :wcol], w_ref[0], NT)
        for j in range(1, N_CHIPS):
            acc = acc + _dot(a_ref[:, j * wcol:(j + 1) * wcol], w_ref[j], NT)
        o_ref[...] = acc
        if ns:
            pl.when(last)(finish)

    res = pl.pallas_call(
        body, grid=(s // TMM,),
        in_specs=[pl.BlockSpec((TMM, IN_COLS), lambda i: (i, 0)),
                  pl.BlockSpec((N_CHIPS, None, D_MODEL, wcol), lambda i: (0, 0, 0, 0))] + [ANY] * ns,
        out_specs=[pl.BlockSpec((TMM, D_MODEL), lambda i: (i, 0))] + [ANY] * ns,
        out_shape=[jax.ShapeDtypeStruct((s, D_MODEL), F32)] + (_scattered_shapes(scatter[1]) if ns else []),
        scratch_shapes=_scatter_sems(ns) if ns else [],
        compiler_params=_cparams("arbitrary" if ns else "parallel"), name=name)(dproj, w_in,
                                                                              *(scatter[0] if ns else []))
    return res[0], list(res[1:])


TM = 1024
TMM = 1024


TR = 512


def _row_spec(width, col=0):
    return pl.BlockSpec((TR, width), lambda i, col=col: (i, col))


def _vec_spec(width):
    return pl.BlockSpec((1, width), lambda i: (0, 0))


def _rms_cast(x, g, name, gather=None):
    s, d = x.shape
    ng = 0 if gather is None else len(gather)

    def body(x_ref, g_ref, *rest):
        if ng:
            start, relay, finish = _gather_steps(rest[:ng], rest[ng + 1:2 * ng + 1], *rest[2 * ng + 1:])
            first, last = _grid_edges((s // TR,))
            pl.when(first)(start)
        a = x_ref[...]
        rest[ng][...] = (a * _rsq_mean(a) * g_ref[...]).astype(BF16)

        if ng:
            @pl.when(last)
            def _():
                relay()
                finish()

    res = pl.pallas_call(
        body, grid=(s // TR,), in_specs=[_row_spec(d), _vec_spec(d)] + [ANY] * ng,
        out_specs=[_row_spec(d)] + [ANY] * ng,
        out_shape=[jax.ShapeDtypeStruct((s, d), BF16)] + _gathered_shapes(gather or []),
        scratch_shapes=_gather_sems(ng) if ng else [],
        compiler_params=_cparams("arbitrary" if ng else "parallel"), name=name)(x, g, *(gather or []))
    return res[0], list(res[1:])


def _acc_init(refs):
    @pl.when(pl.program_id(0) == 0)
    def _():
        for r in refs:
            r[...] = jnp.zeros_like(r)


def _loss_norm_bwd(x1, f, g_post, target, name):
    s, d = x1.shape

    def body(x_ref, f_ref, gp_ref, t_ref, loss_ref, dx_ref, df_ref, dg_ref):
        _acc_init([loss_ref, dg_ref])
        fv = f_ref[...]
        r = _rsq_mean(fv)
        err = x_ref[...] + fv * r * gp_ref[...] - t_ref[...]
        dx = err * (1.0 / d)
        dx_ref[...] = dx
        part = 0.5 * jnp.sum(jnp.mean(err * err, axis=-1, keepdims=True), axis=0, keepdims=True)
        loss_ref[...] += jnp.broadcast_to(part, loss_ref.shape)
        da, dgt = _rms_bwd(fv, r, gp_ref[...], dx)
        df_ref[...] = da.astype(BF16)
        dg_ref[...] += _colsum(dgt)

    return pl.pallas_call(
        body, grid=(s // TR,), in_specs=[_row_spec(d), _row_spec(d), _vec_spec(d), _row_spec(d)],
        out_specs=[pl.BlockSpec((8, LANES), lambda i: (0, 0)), _row_spec(d), _row_spec(d), _vec_spec(d)],
        out_shape=[jax.ShapeDtypeStruct((8, LANES), F32), jax.ShapeDtypeStruct((s, d), F32),
                   jax.ShapeDtypeStruct((s, d), BF16), jax.ShapeDtypeStruct((1, d), F32)],
        compiler_params=_cparams("arbitrary"), name=name)(x1, f, g_post, target)


def _norm_bwd_mid(dx2, dh2, x1, g_pf, y1, g_pm, name):
    s, d = dx2.shape

    def body(dx2_ref, dh_ref, x1_ref, gpf_ref, y1_ref, gpm_ref, dx1_ref, dy1_ref, dgpf_ref, dgpm_ref):
        _acc_init([dgpf_ref, dgpm_ref])
        x1 = x1_ref[...]
        da, dgt = _rms_bwd(x1, _rsq_mean(x1), gpf_ref[...], dh_ref[...])
        dx1 = dx2_ref[...] + da
        dx1_ref[...] = dx1
        dgpf_ref[...] += _colsum(dgt)
        y1 = y1_ref[...]
        dy, dgt2 = _rms_bwd(y1, _rsq_mean(y1), gpm_ref[...], dx1)
        dy1_ref[...] = dy.astype(BF16)
        dgpm_ref[...] += _colsum(dgt2)

    return pl.pallas_call(
        body, grid=(s // TR,),
        in_specs=[_row_spec(d), _row_spec(d), _row_spec(d), _vec_spec(d), _row_spec(d), _vec_spec(d)],
        out_specs=[_row_spec(d), _row_spec(d), _vec_spec(d), _vec_spec(d)],
        out_shape=[jax.ShapeDtypeStruct((s, d), F32), jax.ShapeDtypeStruct((s, d), BF16),
                   jax.ShapeDtypeStruct((1, d), F32), jax.ShapeDtypeStruct((1, d), F32)],
        compiler_params=_cparams("arbitrary"), name=name)(dx2, dh2, x1, g_pf, y1, g_pm)


def _norm_bwd_in_out(dx1, dh1, x0, g1, f_below, g_post_below, name):
    s, d = dx1.shape

    def body(dx1_ref, dh_ref, x0_ref, g_ref, f_ref, gp_ref, dx0_ref, dg_ref, df_ref, dgp_ref):
        _acc_init([dg_ref, dgp_ref])
        x0 = x0_ref[...]
        da, dgt = _rms_bwd(x0, _rsq_mean(x0), g_ref[...], dh_ref[...])
        dx0 = dx1_ref[...] + da
        dx0_ref[...] = dx0
        dg_ref[...] += _colsum(dgt)
        fv = f_ref[...]
        db, dgt2 = _rms_bwd(fv, _rsq_mean(fv), gp_ref[...], dx0)
        df_ref[...] = db.astype(BF16)
        dgp_ref[...] += _colsum(dgt2)

    return pl.pallas_call(
        body, grid=(s // TR,),
        in_specs=[_row_spec(d), _row_spec(d), _row_spec(d), _vec_spec(d), _row_spec(d), _vec_spec(d)],
        out_specs=[_row_spec(d), _vec_spec(d), _row_spec(d), _vec_spec(d)],
        out_shape=[jax.ShapeDtypeStruct((s, d), F32), jax.ShapeDtypeStruct((1, d), F32),
                   jax.ShapeDtypeStruct((s, d), BF16), jax.ShapeDtypeStruct((1, d), F32)],
        compiler_params=_cparams("arbitrary"), name=name)(dx1, dh1, x0, g1, f_below, g_post_below)


def _norm_bwd_in(dx1, dh1, x0, g1, name):
    s, d = dx1.shape

    def body(dx1_ref, dh_ref, x0_ref, g_ref, dx0_ref, dg_ref):
        _acc_init([dg_ref])
        x0 = x0_ref[...]
        da, dgt = _rms_bwd(x0, _rsq_mean(x0), g_ref[...], dh_ref[...])
        dx0_ref[...] = dx1_ref[...] + da
        dg_ref[...] += _colsum(dgt)

    return pl.pallas_call(
        body, grid=(s // TR,), in_specs=[_row_spec(d), _row_spec(d), _row_spec(d), _vec_spec(d)],
        out_specs=[_row_spec(d), _vec_spec(d)],
        out_shape=[jax.ShapeDtypeStruct((s, d), F32), jax.ShapeDtypeStruct((1, d), F32)],
        compiler_params=_cparams("arbitrary"), name=name)(dx1, dh1, x0, g1)


def _tril_mask():
    row = lax.broadcasted_iota(jnp.int32, (CHUNK, CHUNK), 0)
    col = lax.broadcasted_iota(jnp.int32, (CHUNK, CHUNK), 1)
    return row >= col


def _gating_forward(pa, gv, bv, wt, bsf):
    er = lax.erf(pa * RSQRT2)
    za = 0.5 * pa * (1.0 + er)
    u = za[:, :A_WIDTH]
    va = za[:, A_WIDTH:]
    xc = va - jnp.mean(va, axis=-1, keepdims=True)
    rs = lax.rsqrt(jnp.mean(xc * xc, axis=-1, keepdims=True) + EPS)
    vn = xc * rs
    vlb = (vn * gv + bv).astype(BF16)
    sg = jnp.concatenate(
        [_dot(wt[g], vlb[:, g * GROUP_DIM:(g + 1) * GROUP_DIM], NN) for g in range(A_GROUPS)], axis=1) + bsf
    return er, u, rs, vn, vlb, sg


def _masked_ws(ws_ref):
    mask = _tril_mask()
    return [jnp.where(mask, ws_ref[g], 0.0).astype(BF16) for g in range(A_GROUPS)]


def _mixer_a_fwd(proj, gv, bv, ws, bsf, ga, name, gather=None):
    s = proj.shape[0]
    ng = 0 if gather is None else len(gather)

    def body(p_ref, gv_ref, bv_ref, ws_ref, bs_ref, ga_ref, *rest):
        o_ref = rest[ng]
        if ng:
            start, relay, finish = _gather_steps(rest[:ng], rest[ng + 1:2 * ng + 1], *rest[2 * ng + 1:])
            first, last = _grid_edges((s // TR,))
            pl.when(first)(start)
        wt = _masked_ws(ws_ref)
        for ch in range(TR // CHUNK):
            rows = slice(ch * CHUNK, (ch + 1) * CHUNK)
            _, u, _, _, _, sg = _gating_forward(p_ref[rows, :].astype(F32), gv_ref[...], bv_ref[...], wt, bs_ref[...])
            oa = u * sg
            o_ref[rows, :] = (oa * _rsq_mean(oa) * ga_ref[...]).astype(BF16)

        if ng:
            @pl.when(last)
            def _():
                relay()
                finish()

    res = pl.pallas_call(
        body, grid=(s // TR,),
        in_specs=[_row_spec(2 * A_WIDTH), _vec_spec(A_WIDTH), _vec_spec(A_WIDTH),
                  pl.BlockSpec((A_GROUPS, CHUNK, CHUNK), lambda i: (0, 0, 0)),
                  pl.BlockSpec((CHUNK, A_WIDTH), lambda i: (0, 0)), _vec_spec(A_WIDTH)] + [ANY] * ng,
        out_specs=[_row_spec(A_WIDTH)] + [ANY] * ng,
        out_shape=[jax.ShapeDtypeStruct((s, A_WIDTH + B_WIDTH), BF16)] + _gathered_shapes(gather or []),
        scratch_shapes=_gather_sems(ng) if ng else [],
        compiler_params=_cparams("arbitrary" if ng else "parallel"), name=name)(proj, gv, bv, ws, bsf, ga,
                                                                              *(gather or []))
    return res[0], list(res[1:])


def _mixer_a_bwd(proj, dmixed, gv, bv, ws, bsf, ga, name, scatter=None):
    s = proj.shape[0]
    nsteps = s // TR
    ns = 0 if scatter is None else len(scatter[0])

    def body(*refs):
        p_ref, dm_ref, gv_ref, bv_ref, ws_ref, bs_ref, ga_ref = refs[:7]
        dp_ref, dga_ref, dgv_ref, dbv_ref, dbs_ref, dws_ref = refs[7 + ns:13 + ns]
        if ns:
            start, finish = _scatter_steps(refs[7:7 + ns], refs[13 + ns:13 + 2 * ns], *refs[13 + 2 * ns:], scatter[1])
            first, last = _grid_edges((nsteps,))
            pl.when(first)(start)
        _acc_init([dga_ref, dgv_ref, dbv_ref, dbs_ref, dws_ref])
        mask = _tril_mask()
        wt = _masked_ws(ws_ref)
        gvv = gv_ref[...]
        gav = ga_ref[...]
        for ch in range(TR // CHUNK):
            rows = slice(ch * CHUNK, (ch + 1) * CHUNK)
            pa = p_ref[rows, :].astype(F32)
            er, u, rs, vn, vlb, sg = _gating_forward(pa, gvv, bv_ref[...], wt, bs_ref[...])
            oa = u * sg
            doa, dgt = _rms_bwd(oa, _rsq_mean(oa), gav, dm_ref[rows, :])
            dga_ref[...] += _colsum(dgt)
            du = doa * sg
            dsg = doa * u
            dbs_ref[...] += dsg
            dsgb = dsg.astype(BF16)
            dvl = []
            for g in range(A_GROUPS):
                cols = slice(g * GROUP_DIM, (g + 1) * GROUP_DIM)
                dws_ref[g] += jnp.where(mask, _dot(dsgb[:, cols], vlb[:, cols], NT), 0.0)
                dvl.append(_dot(wt[g], dsgb[:, cols], TN))
            dvl = jnp.concatenate(dvl, axis=1)
            dgv_ref[...] += _colsum(dvl * vn)
            dbv_ref[...] += _colsum(dvl)
            dvn = dvl * gvv
            dva = rs * (dvn - jnp.mean(dvn, axis=-1, keepdims=True)
                        - vn * jnp.mean(dvn * vn, axis=-1, keepdims=True))
            gp = 0.5 * (1.0 + er) + pa * jnp.exp(-0.5 * pa * pa) * INV_SQRT_2PI
            dp_ref[rows, :] = (jnp.concatenate([du, dva], axis=1) * gp).astype(BF16)

        @pl.when(pl.program_id(0) == nsteps - 1)
        def _():
            for g in range(A_GROUPS):
                cols = slice(g * GROUP_DIM, (g + 1) * GROUP_DIM)
                tot = jnp.sum(dbs_ref[:, cols], axis=1, keepdims=True)
                dbs_ref[:, cols] = jnp.broadcast_to(tot, (CHUNK, GROUP_DIM))

        if ns:
            pl.when(last)(finish)

    full = lambda *shape: pl.BlockSpec(shape, lambda i: (0,) * len(shape))
    res = pl.pallas_call(
        body, grid=(nsteps,),
        in_specs=[_row_spec(2 * A_WIDTH), _row_spec(A_WIDTH), _vec_spec(A_WIDTH), _vec_spec(A_WIDTH),
                  full(A_GROUPS, CHUNK, CHUNK), full(CHUNK, A_WIDTH), _vec_spec(A_WIDTH)] + [ANY] * ns,
        out_specs=[_row_spec(2 * A_WIDTH), _vec_spec(A_WIDTH), _vec_spec(A_WIDTH), _vec_spec(A_WIDTH),
                   full(CHUNK, A_WIDTH), full(A_GROUPS, CHUNK, CHUNK)] + [ANY] * ns,
        out_shape=[jax.ShapeDtypeStruct((s, IN_COLS), BF16), jax.ShapeDtypeStruct((1, A_WIDTH), F32),
                   jax.ShapeDtypeStruct((1, A_WIDTH), F32), jax.ShapeDtypeStruct((1, A_WIDTH), F32),
                   jax.ShapeDtypeStruct((CHUNK, A_WIDTH), F32),
                   jax.ShapeDtypeStruct((A_GROUPS, CHUNK, CHUNK), F32)]
        + (_scattered_shapes(scatter[1]) if ns else []),
        scratch_shapes=_scatter_sems(ns) if ns else [],
        compiler_params=_cparams("arbitrary"), name=name)(proj, dmixed, gv, bv, ws, bsf, ga,
                                                          *(scatter[0] if ns else []))
    return res[:6] + (list(res[6:]),)


def _rope_tables(s):
    half = ROT_DIM // 2
    lane = jnp.arange(LANES) % HEAD_DIM
    inv = ROPE_THETA ** (-(2 * (lane % half)).astype(F32) / ROT_DIM)
    ang = jnp.arange(s, dtype=F32)[:, None] * inv[None, :]
    cos, sin = jnp.cos(ang), jnp.sin(ang)
    c = jnp.where(lane < ROT_DIM, cos, 1.0)
    s1 = jnp.where(lane < half, -sin, 0.0)
    s2 = jnp.where((lane >= half) & (lane < ROT_DIM), sin, 0.0)
    return c, s1, s2


def _lane_blocks(width):
    return [slice(b * LANES, (b + 1) * LANES) for b in range(width // LANES)]


CLASS_DILS = tuple(d for d in DILATIONS if d > 1)


def _class_shape(s, dil, dtype):
    return jax.ShapeDtypeStruct((dil, s // dil, B_WIDTH), dtype)


def _class_spec(dil):
    return pl.BlockSpec((dil, TR // dil, B_WIDTH), lambda i, *_: (0, i, 0))


NBLK = B_WIDTH // LANES
STAGE = pltpu.VMEM((NBLK, TR, LANES), F32)


def _stage_put(stage, value):
    for b, sl in enumerate(_lane_blocks(B_WIDTH)):
        stage[b] = value[:, sl]


def _stage_get(stage):
    return jnp.concatenate([stage[b] for b in range(NBLK)], axis=1)


def _store_classes(stage, dst_ref, dil):
    for b, sl in enumerate(_lane_blocks(B_WIDTH)):
        for r in range(dil):
            dst_ref[r, :, sl] = stage[b, pl.ds(r, TR // dil, stride=dil), :].astype(dst_ref.dtype)


def _load_classes(src_ref, stage, dil):
    for b, sl in enumerate(_lane_blocks(B_WIDTH)):
        for r in range(dil):
            stage[b, pl.ds(r, TR // dil, stride=dil), :] = src_ref[r, :, sl].astype(F32)
    return _stage_get(stage)


def _rope_fwd(proj, tabs, name, gather=None):
    s = proj.shape[0]
    half = ROT_DIM // 2
    scale = HEAD_DIM ** -0.5
    nlay = 1 + len(CLASS_DILS)
    ng = 0 if gather is None else len(gather)

    def body(q_ref, k_ref, v_ref, c_ref, s1_ref, s2_ref, *rest):
        outs, stage = rest[ng:ng + 3 * nlay], rest[2 * ng + 3 * nlay]
        if ng:
            start, relay, finish = _gather_steps(rest[:ng], rest[ng + 3 * nlay:2 * ng + 3 * nlay],
                                                 *rest[2 * ng + 3 * nlay + 1:])
            first, last = _grid_edges((s // TR,))
            pl.when(first)(start)
        c, s1, s2 = c_ref[...], s1_ref[...], s2_ref[...]
        for which, (src, mul) in enumerate(((q_ref, scale), (k_ref, 1.0), (v_ref, None))):
            if mul is None:
                _stage_put(stage, src[...].astype(F32))
            else:
                for b, sl in enumerate(_lane_blocks(B_WIDTH)):
                    a = src[:, sl].astype(F32)
                    r = a * c + pltpu.roll(a, LANES - half, 1) * s1 + pltpu.roll(a, half, 1) * s2
                    stage[b] = r * mul
            dst = outs[which * nlay:(which + 1) * nlay]
            dst[0][...] = _stage_get(stage).astype(BF16)
            for ref, d in zip(dst[1:], CLASS_DILS):
                _store_classes(stage, ref, d)

        if ng:
            @pl.when(last)
            def _():
                relay()
                finish()

    tab = pl.BlockSpec((TR, LANES), lambda i: (i, 0))
    lay_specs = [_row_spec(B_WIDTH)] + [_class_spec(d) for d in CLASS_DILS]
    lay_shapes = [jax.ShapeDtypeStruct((s, B_WIDTH), BF16)] + [_class_shape(s, d, BF16) for d in CLASS_DILS]
    outs = pl.pallas_call(
        body, grid=(s // TR,),
        in_specs=[_row_spec(B_WIDTH, 2), _row_spec(B_WIDTH, 3), _row_spec(B_WIDTH, 4), tab, tab, tab] + [ANY] * ng,
        out_specs=lay_specs * 3 + [ANY] * ng, out_shape=lay_shapes * 3 + _gathered_shapes(gather or []),
        scratch_shapes=[STAGE] + (_gather_sems(ng) if ng else []),
        compiler_params=_cparams("arbitrary" if ng else "parallel"), name=name)(proj, proj, proj, *tabs,
                                                                              *(gather or []))
    q, k, v = (dict(zip(DILATIONS, outs[w * nlay:(w + 1) * nlay])) for w in range(3))
    return q, k, v, list(outs[3 * nlay:])


def _as_classes(t):
    return t if t.ndim == 3 else t[None]


def _head_masks():
    lane = lax.broadcasted_iota(jnp.int32, (1, LANES), 1)
    return lane < HEAD_DIM, lane >= HEAD_DIM


def _stack_heads(t):
    lo, hi = _head_masks()
    zero = jnp.zeros_like(t)
    return jnp.concatenate([jnp.where(lo, t, zero), jnp.where(hi, t, zero)], axis=0)


MAX_SEGMENT_BLOCKS = 8


def _segment_masks(j):
    qi = lax.broadcasted_iota(jnp.int32, (BAND, 2 * BAND), 0)
    kj = lax.broadcasted_iota(jnp.int32, (BAND, 2 * BAND), 1)
    both = (kj >= qi) & (kj <= qi + BAND)
    own = kj[:, :BAND] <= qi[:, :BAND]
    head = both & ((kj >= BAND) | (j > 0))
    return tuple(jnp.concatenate([m, m], axis=0) for m in (own, both, head))


def _block_rows(g):
    return pl.ds(pl.multiple_of(g * BAND, BAND), BAND)


def _key_rows(g):
    return pl.ds(pl.multiple_of((g - 1) * BAND, BAND), 2 * BAND)


def _segments(n):
    nb = n // BAND
    seg = min(nb, MAX_SEGMENT_BLOCKS)
    return seg, nb // seg


def _segment_specs(seg):
    main = pl.BlockSpec((None, seg * BAND, B_WIDTH), lambda r, j: (r, j, 0))
    halo = pl.BlockSpec((None, BAND, B_WIDTH), lambda r, j: (r, jnp.maximum(j * seg - 1, 0), 0))
    return main, halo


def _attn_fwd(q, k, v, name, gather=None):
    dil, n, _ = q.shape
    seg, nseg = _segments(n)
    nh = 2 if nseg > 1 else 0
    ng = 0 if gather is None else len(gather)

    def body(*refs):
        q_ref, k_ref, v_ref = refs[:3]
        halos = refs[3:3 + nh]
        o_ref, l_ref = refs[3 + nh + ng:5 + nh + ng]
        if ng:
            start, relay, finish = _gather_steps(refs[3 + nh:3 + nh + ng], refs[5 + nh + ng:5 + nh + 2 * ng],
                                                 *refs[5 + nh + 2 * ng:])
            first, last = _grid_edges((dil, nseg))
            pl.when(first)(start)
        own, both, head = _segment_masks(pl.program_id(1))
        lo, _ = _head_masks()

        def block(rows, keys_of, valid):
            for sl in _lane_blocks(B_WIDTH):
                kk, vv = keys_of(sl)
                sc = jnp.where(valid, _dot(_stack_heads(q_ref[rows, sl]), kk, NT), NEG_INF)
                mx = jnp.max(sc, axis=1, keepdims=True)
                p = jnp.exp(sc - mx)
                den = jnp.sum(p, axis=1, keepdims=True)
                out = _dot(p.astype(BF16), vv, NN) / den
                lse = mx + jnp.log(den)
                o_ref[rows, sl] = jnp.where(lo, out[:BAND], out[BAND:]).astype(BF16)
                l_ref[rows, sl] = jnp.where(lo, lse[:BAND], lse[BAND:])

        if nh:
            block(_block_rows(0), lambda sl: (jnp.concatenate([halos[0][:, sl], k_ref[0:BAND, sl]], axis=0),
                                              jnp.concatenate([halos[1][:, sl], v_ref[0:BAND, sl]], axis=0)), head)
        else:
            block(_block_rows(0), lambda sl: (k_ref[0:BAND, sl], v_ref[0:BAND, sl]), own)

        @pl.loop(1, seg)
        def _(g):
            block(_block_rows(g), lambda sl: (k_ref[_key_rows(g), sl], v_ref[_key_rows(g), sl]), both)

        if ng:
            @pl.when(last)
            def _():
                relay()
                finish()

    main, halo = _segment_specs(seg)
    res = pl.pallas_call(
        body, grid=(dil, nseg), in_specs=[main] * 3 + [halo] * nh + [ANY] * ng, out_specs=[main, main] + [ANY] * ng,
        out_shape=[jax.ShapeDtypeStruct((dil, n, B_WIDTH), BF16), jax.ShapeDtypeStruct((dil, n, B_WIDTH), F32)]
        + _gathered_shapes(gather or []),
        scratch_shapes=_gather_sems(ng) if ng else [],
        compiler_params=_cparams(*(["arbitrary"] * 2 if ng else ["parallel"] * 2)), name=name)(
            q, k, v, *([k, v] if nh else []), *(gather or []))
    return res[0], res[1], list(res[2:])


def _attn_bwd(q, k, v, do, lse, delta, name, scatter=None):
    dil, n, _ = q.shape
    seg, nseg = _segments(n)
    nh = 2 if nseg > 1 else 0
    ns = 0 if scatter is None else len(scatter[0])

    def body(*refs):
        q_ref, k_ref, v_ref, do_ref, lse_ref, dl_ref = refs[:6]
        halos = refs[6:6 + nh]
        dq_ref, dk_ref, dv_ref = refs[6 + nh + ns:9 + nh + ns]
        halo_out = refs[9 + nh + ns:9 + 2 * nh + ns]
        ck_ref, cv_ref = refs[9 + 2 * nh + 2 * ns:11 + 2 * nh + 2 * ns]
        if ns:
            start, finish = _scatter_steps(refs[6 + nh:6 + nh + ns], refs[9 + 2 * nh + ns:9 + 2 * nh + 2 * ns],
                                           *refs[11 + 2 * nh + 2 * ns:], scatter[1])
            first, last = _grid_edges((dil, nseg))
            pl.when(first)(start)
        own, both, head = _segment_masks(pl.program_id(1))
        lo, _ = _head_masks()
        lane = lax.broadcasted_iota(jnp.int32, (1, LANES), 1)

        def per_head(t):
            return jnp.concatenate(
                [jnp.sum(jnp.where(lane == first, t, 0.0), axis=1, keepdims=True) for first in (0, HEAD_DIM)], axis=0)

        def grads(rows, kk, vv, valid, sl):
            q2 = _stack_heads(q_ref[rows, sl])
            do2 = _stack_heads(do_ref[rows, sl])
            p = jnp.where(valid, jnp.exp(_dot(q2, kk, NT) - per_head(lse_ref[rows, sl])), 0.0)
            ds = (p * (_dot(do2, vv, NT) - per_head(dl_ref[rows, sl]))).astype(BF16)
            dq = _dot(ds, kk, NN)
            dq_ref[rows, sl] = jnp.where(lo, dq[:BAND], dq[BAND:]).astype(BF16)
            return _dot(ds, q2, TN), _dot(p.astype(BF16), do2, TN)

        for sl in _lane_blocks(B_WIDTH):
            if nh:
                dkk, dvv = grads(_block_rows(0), jnp.concatenate([halos[0][:, sl], k_ref[0:BAND, sl]], axis=0),
                                 jnp.concatenate([halos[1][:, sl], v_ref[0:BAND, sl]], axis=0), head, sl)
                halo_out[0][:, sl], halo_out[1][:, sl] = dkk[:BAND], dvv[:BAND]
                ck_ref[:, sl], cv_ref[:, sl] = dkk[BAND:], dvv[BAND:]
            else:
                ck_ref[:, sl], cv_ref[:, sl] = grads(_block_rows(0), k_ref[0:BAND, sl], v_ref[0:BAND, sl], own, sl)

        @pl.loop(1, seg)
        def _(g):
            before = _block_rows(g - 1)
            for sl in _lane_blocks(B_WIDTH):
                dkk, dvv = grads(_block_rows(g), k_ref[_key_rows(g), sl], v_ref[_key_rows(g), sl], both, sl)
                dk_ref[before, sl] = (ck_ref[:, sl] + dkk[:BAND]).astype(BF16)
                dv_ref[before, sl] = (cv_ref[:, sl] + dvv[:BAND]).astype(BF16)
                ck_ref[:, sl] = dkk[BAND:]
                cv_ref[:, sl] = dvv[BAND:]

        final = pl.ds((seg - 1) * BAND, BAND)
        dk_ref[final, :] = ck_ref[...].astype(BF16)
        dv_ref[final, :] = cv_ref[...].astype(BF16)

        if ns:
            pl.when(last)(finish)

    main, halo = _segment_specs(seg)
    shape = jax.ShapeDtypeStruct((dil, n, B_WIDTH), BF16)
    halo_shape = jax.ShapeDtypeStruct((dil, nseg, BAND, B_WIDTH), F32)
    halo_spec = pl.BlockSpec((None, None, BAND, B_WIDTH), lambda r, j: (r, j, 0, 0))
    res = pl.pallas_call(
        body, grid=(dil, nseg), in_specs=[main] * 6 + [halo] * nh + [ANY] * ns,
        out_specs=[main] * 3 + [halo_spec] * nh + [ANY] * ns,
        out_shape=[shape] * 3 + [halo_shape] * nh + (_scattered_shapes(scatter[1]) if ns else []),
        scratch_shapes=[pltpu.VMEM((BAND, B_WIDTH), F32)] * 2 + (_scatter_sems(ns) if ns else []),
        compiler_params=_cparams(*(["arbitrary"] * 2 if ns else ["parallel"] * 2)), name=name)(
            q, k, v, do, lse, delta, *([k, v] if nh else []), *(scatter[0] if ns else []))
    return res[0], res[1], res[2], (tuple(res[3:3 + nh]) if nh else None), list(res[3 + nh:])


def _attn_combine(outs, lses, gb, mixed, name, gather=None):
    s = mixed.shape[0]
    npat = len(DILATIONS)
    w = B_WIDTH
    ng = 0 if gather is None else len(gather)

    def body(*refs):
        o_refs, l_refs = refs[:npat], refs[npat:2 * npat]
        g_ref = refs[2 * npat]
        ob_ref = refs[2 * npat + 2 + ng]
        lse_refs = refs[2 * npat + 3 + ng:3 * npat + 3 + ng]
        mb_ref = refs[3 * npat + 3 + ng]
        stage = refs[3 * npat + 4 + 2 * ng]
        if ng:
            start, relay, finish = _gather_steps(refs[2 * npat + 2:2 * npat + 2 + ng],
                                                 refs[3 * npat + 4 + ng:3 * npat + 4 + 2 * ng],
                                                 *refs[3 * npat + 5 + 2 * ng:])
            first, last = _grid_edges((s // TR,))
            pl.when(first)(start)
        os_ = [o_refs[0][...].astype(F32)] + [_load_classes(r, stage, d) for r, d in zip(o_refs[1:], CLASS_DILS)]
        ls = [l_refs[0][...]] + [_load_classes(r, stage, d) for r, d in zip(l_refs[1:], CLASS_DILS)]
        mx = functools.reduce(jnp.maximum, ls)
        ws = [jnp.exp(l - mx) for l in ls]
        tot = functools.reduce(lambda a, b: a + b, ws)
        ob = functools.reduce(lambda a, b: a + b, [wt / tot * o for wt, o in zip(ws, os_)])
        ob_ref[...] = ob
        lse = mx + jnp.log(tot)
        _stage_put(stage, lse)
        lse_refs[0][...] = lse
        for ref, d in zip(lse_refs[1:], CLASS_DILS):
            _store_classes(stage, ref, d)
        mb_ref[...] = (ob * _rsq_mean(ob) * g_ref[...]).astype(BF16)

        if ng:
            @pl.when(last)
            def _():
                relay()
                finish()

    lay_specs = [_row_spec(w)] + [_class_spec(d) for d in CLASS_DILS]
    res = pl.pallas_call(
        body, grid=(s // TR,), in_specs=lay_specs * 2 + [_vec_spec(w), ANY] + [ANY] * ng,
        out_specs=[_row_spec(w)] + lay_specs + [_row_spec(w, 1)] + [ANY] * ng,
        out_shape=[jax.ShapeDtypeStruct((s, w), F32), jax.ShapeDtypeStruct((s, w), F32)]
        + [_class_shape(s, d, F32) for d in CLASS_DILS] + [jax.ShapeDtypeStruct(mixed.shape, mixed.dtype)]
        + _gathered_shapes(gather or []),
        scratch_shapes=[STAGE] + (_gather_sems(ng) if ng else []), input_output_aliases={2 * npat + 1: npat + 1},
        compiler_params=_cparams("arbitrary" if ng else "parallel"), name=name)(*outs, *lses, gb, mixed,
                                                                              *(gather or []))
    return res[0], dict(zip(DILATIONS, res[1:npat + 1])), res[npat + 1], list(res[npat + 2:])


def _attn_bwd_prep(dmixed, ob, gb, name):
    s = ob.shape[0]
    w = B_WIDTH
    nlay = len(DILATIONS)

    def body(dm_ref, ob_ref, g_ref, *rest):
        do_refs, dl_refs = rest[:nlay], rest[nlay:2 * nlay]
        dg_ref, stage = rest[2 * nlay:]
        _acc_init([dg_ref])
        ob = ob_ref[...]
        dob, dgt = _rms_bwd(ob, _rsq_mean(ob), g_ref[...], dm_ref[...])
        dg_ref[...] += _colsum(dgt)
        _stage_put(stage, dob)
        do_refs[0][...] = dob.astype(BF16)
        for ref, d in zip(do_refs[1:], CLASS_DILS):
            _store_classes(stage, ref, d)
        lo, hi = _head_masks()
        t = dob * ob
        for b, sl in enumerate(_lane_blocks(w)):
            tb = t[:, sl]
            s0 = jnp.sum(jnp.where(lo, tb, 0.0), axis=1, keepdims=True)
            s1 = jnp.sum(jnp.where(hi, tb, 0.0), axis=1, keepdims=True)
            stage[b] = jnp.where(lo, s0, s1)
        dl_refs[0][...] = _stage_get(stage)
        for ref, d in zip(dl_refs[1:], CLASS_DILS):
            _store_classes(stage, ref, d)

    lay_specs = [_row_spec(w)] + [_class_spec(d) for d in CLASS_DILS]
    shapes = lambda dt: [jax.ShapeDtypeStruct((s, w), dt)] + [_class_shape(s, d, dt) for d in CLASS_DILS]
    res = pl.pallas_call(
        body, grid=(s // TR,), in_specs=[_row_spec(w, 1), _row_spec(w), _vec_spec(w)],
        out_specs=lay_specs * 2 + [_vec_spec(w)],
        out_shape=shapes(BF16) + shapes(F32) + [jax.ShapeDtypeStruct((1, w), F32)],
        scratch_shapes=[STAGE],
        compiler_params=_cparams("arbitrary"), name=name)(dmixed, ob, gb)
    return dict(zip(DILATIONS, res[:nlay])), dict(zip(DILATIONS, res[nlay:2 * nlay])), res[2 * nlay]


def _rope_bwd(dqs, dks, dvs, halos, tabs, dproj, name):
    s = dproj.shape[0]
    half = ROT_DIM // 2
    scale = HEAD_DIM ** -0.5
    npat = len(DILATIONS)
    w = B_WIDTH
    nseg = halos[0].shape[0]
    per = s // nseg // TR

    def body(*refs):
        groups = [refs[g * npat:(g + 1) * npat] for g in range(3)]
        halo_refs = (None,) + tuple(refs[3 * npat:3 * npat + 2])
        c_ref, s1_ref, s2_ref, _, o_ref, stage = refs[3 * npat + 2:]
        i = pl.program_id(0)
        at_edge = ((i + 1) % per == 0) & ((i + 1) // per < nseg)

        def total(rs, halo_ref=None):
            acc = rs[0][...].astype(F32)
            if halo_ref is not None:
                edge = jnp.concatenate([jnp.zeros((TR - BAND, w), F32), halo_ref[...]], axis=0)
                acc = acc + jnp.where(at_edge, edge, 0.0)
            for ref, d in zip(rs[1:], CLASS_DILS):
                acc = acc + _load_classes(ref, stage, d)
            return acc

        def unrope(g):
            c, s1, s2 = c_ref[...], s1_ref[...], s2_ref[...]
            for sl in _lane_blocks(w):
                gb = g[:, sl]
                o = gb * c + pltpu.roll(gb * s1, half, 1) + pltpu.roll(gb * s2, LANES - half, 1)
                o_ref[:, sl] = o.astype(BF16)

        which = pl.program_id(1)

        @pl.when(which == 0)
        def _():
            unrope(total(groups[0]) * scale)

        @pl.when(which == 1)
        def _():
            unrope(total(groups[1], halo_refs[1]))

        @pl.when(which == 2)
        def _():
            o_ref[...] = total(groups[2], halo_refs[2]).astype(BF16)

    tab = pl.BlockSpec((TR, LANES), lambda i, j: (i, 0))
    nat = pl.BlockSpec((TR, w), lambda i, j: (i, 0))
    lay_specs = [nat] + [_class_spec(d) for d in CLASS_DILS]
    edge_spec = pl.BlockSpec((None, BAND, w), lambda i, j: (jnp.minimum((i + 1) // per, nseg - 1), 0, 0))
    first_col = 2 * A_WIDTH // w
    return pl.pallas_call(
        body, grid=(s // TR, 3), in_specs=lay_specs * 3 + [edge_spec] * 2 + [tab] * 3 + [ANY],
        out_specs=pl.BlockSpec((TR, w), lambda i, j: (i, first_col + j)),
        out_shape=jax.ShapeDtypeStruct(dproj.shape, dproj.dtype), scratch_shapes=[STAGE],
        input_output_aliases={3 * npat + 5: 0},
        compiler_params=_cparams("parallel", "arbitrary"), name=name)(*dqs, *dks, *dvs, *halos, *tabs, dproj)


TK = 512
HALO = 16
FFN_ROWS = 256
FFN_CHUNKS = tuple(slice(r, r + FFN_ROWS) for r in range(0, TM, FFN_ROWS))


def _row_of(v, r):
    rows = lax.broadcasted_iota(jnp.int32, (v.shape[0], 1), 0)
    return jnp.sum(jnp.where(rows == r, v, 0.0), axis=0, keepdims=True)


def _taps_before(x, halo):
    row = lax.broadcasted_iota(jnp.int32, (x.shape[0], 1), 0)
    m1 = jnp.where(row == 0, _row_of(halo, HALO - 1), pltpu.roll(x, 1, 0))
    m2 = jnp.where(row == 0, _row_of(halo, HALO - 2), jnp.where(row == 1, _row_of(halo, HALO - 1), pltpu.roll(x, 2, 0)))
    return m2, m1, x


def _taps_after(x, halo):
    rows = x.shape[0]
    row = lax.broadcasted_iota(jnp.int32, (rows, 1), 0)
    p1 = jnp.where(row == rows - 1, _row_of(halo, 0), pltpu.roll(x, rows - 1, 0))
    p2 = jnp.where(row == rows - 2, _row_of(halo, 0), jnp.where(row == rows - 1, _row_of(halo, 1), pltpu.roll(x, rows - 2, 0)))
    return p1, p2


def _conv_value(taps, cw_ref, cb_ref, h):
    return cb_ref[h] + cw_ref[h, 0:1, :] * taps[0] + cw_ref[h, 1:2, :] * taps[1] + cw_ref[h, 2:3, :] * taps[2]


def _ffn_weight_specs(ncol):
    per_up = (2 * D_FF // N_CHIPS) // TK
    per_dn = (D_FF // N_CHIPS) // TK
    wg = pl.BlockSpec((None, None, D_MODEL, TK), lambda i, j: (j // per_up, 0, 0, j % per_up))
    wv = pl.BlockSpec((None, None, D_MODEL, TK), lambda i, j: ((j + ncol) // per_up, 0, 0, (j + ncol) % per_up))
    wd = pl.BlockSpec((None, None, TK, D_MODEL), lambda i, j: (j // per_dn, 0, j % per_dn, 0))
    cw = pl.BlockSpec((2, 3, TK), lambda i, j: (0, 0, j))
    cb = pl.BlockSpec((2, 1, TK), lambda i, j: (0, 0, j))
    return wg, wv, wd, cw, cb


def _ffn_forward(h2, w_up, w_down, cw3, cb3, name, gather=None, post=None):
    s = h2.shape[0]
    nm, ncol = s // TM, D_FF // TK
    ng = 0 if gather is None else len(gather)
    npost = 0 if post is None else 3
    nout = 4 + (2 if post else 0)

    def body(*refs):
        h_ref, wg_ref, wv_ref, wd_ref, cw_ref, cb_ref = refs[:6]
        post_in = refs[6:6 + npost]
        g_in = refs[6 + npost:6 + npost + ng]
        outs = refs[6 + npost + ng:6 + npost + ng + nout]
        y_ref, up_ref, cv_ref, f_ref = outs[:4]
        g_out = refs[6 + npost + ng + nout:6 + npost + 2 * ng + nout]
        carry = refs[6 + npost + 2 * ng + nout]
        i, j = pl.program_id(0), pl.program_id(1)
        if ng:
            start, relay, finish = _gather_steps(g_in, g_out, *refs[7 + npost + 2 * ng + nout:])
            pl.when((i == 0) & (j == 0))(start)
            pl.when((i == nm - 1) & (j == 0))(relay)

        @pl.when((i == 0) & (j == 0))
        def _():
            carry[...] = jnp.zeros_like(carry)

        @pl.when(j == 0)
        def _():
            f_ref[...] = jnp.zeros_like(f_ref)

        ups = []
        for rs in FFN_CHUNKS:
            hc = h_ref[rs, :]
            ups.append([_dot(hc, w_ref[...], NN).astype(BF16) for w_ref in (wg_ref, wv_ref)])
            for hh in range(2):
                up_ref[hh, rs, :] = ups[-1][hh]
        before = [carry[j, hh] for hh in range(2)]
        for rs, up in zip(FFN_CHUNKS, ups):
            conv = []
            for hh in range(2):
                x = up[hh].astype(F32)
                conv.append(_conv_value(_taps_before(x, before[hh]), cw_ref, cb_ref, hh))
                cv_ref[hh, rs, :] = conv[hh].astype(BF16)
                before[hh] = x[x.shape[0] - HALO:, :]
            y = (_gelu_tanh(conv[0])[0] * conv[1]).astype(BF16)
            y_ref[rs, :] = y
            f_ref[rs, :] += _dot(y, wd_ref[...], NN)
        for hh in range(2):
            carry[j, hh] = before[hh]

        @pl.when(j == ncol - 1)
        def _():
            if post:
                f = f_ref[...]
                x1_ref, gp_ref, gn_ref = post_in
                x2 = x1_ref[...] + f * _rsq_mean(f) * gp_ref[...]
                outs[4][...] = x2
                outs[5][...] = (x2 * _rsq_mean(x2) * gn_ref[...]).astype(BF16)

        if ng:
            pl.when((i == nm - 1) & (j == ncol - 1))(finish)

    wg, wv, wd, cw, cb = _ffn_weight_specs(ncol)
    row = pl.BlockSpec((TM, D_MODEL), lambda i, j: (i, 0))
    vec = pl.BlockSpec((1, D_MODEL), lambda i, j: (0, 0))
    res = pl.pallas_call(
        body, grid=(nm, ncol),
        in_specs=[row, wg, wv, wd, cw, cb] + ([row, vec, vec] if post else []) + [ANY] * ng,
        out_specs=[pl.BlockSpec((TM, TK), lambda i, j: (i, j)), pl.BlockSpec((2, TM, TK), lambda i, j: (0, i, j)),
                   pl.BlockSpec((2, TM, TK), lambda i, j: (0, i, j)), row] + ([row, row] if post else [])
        + [ANY] * ng,
        out_shape=[jax.ShapeDtypeStruct((s, D_FF), BF16), jax.ShapeDtypeStruct((2, s, D_FF), BF16),
                   jax.ShapeDtypeStruct((2, s, D_FF), BF16), jax.ShapeDtypeStruct((s, D_MODEL), F32)]
        + ([jax.ShapeDtypeStruct((s, D_MODEL), F32), jax.ShapeDtypeStruct((s, D_MODEL), BF16)] if post else [])
        + _gathered_shapes(gather or []),
        scratch_shapes=[pltpu.VMEM((ncol, 2, HALO, TK), F32)] + (_gather_sems(ng) if ng else []),
        compiler_params=_cparams("arbitrary", "arbitrary"), name=name)(h2, w_up, w_up, w_down, cw3, cb3,
                                                                      *(post or []), *(gather or []))
    return res[:nout], list(res[nout:])


def _ffn_backward(df, w_up, w_down, up3, cv3, cw3, name, scatter=None):
    s = df.shape[0]
    nm, ncol = s // TM, D_FF // TK
    ns = 0 if scatter is None else len(scatter[0])

    def body(*refs):
        df_ref, wg_ref, wv_ref, wd_ref, cw_ref, up_ref, cv_ref = refs[:7]
        s_in = refs[7:7 + ns]
        dup_ref, dh_ref, sums_ref = refs[7 + ns:10 + ns]
        s_out = refs[10 + ns:10 + 2 * ns]
        carry = refs[10 + 2 * ns]
        i, j = pl.program_id(0), pl.program_id(1)
        if ns:
            start, finish = _scatter_steps(s_in, s_out, *refs[11 + 2 * ns:], scatter[1])
            pl.when((i == 0) & (j == 0))(start)

        @pl.when((i == 0) & (j == 0))
        def _():
            carry[...] = jnp.zeros_like(carry)
            sums_ref[...] = jnp.zeros_like(sums_ref)

        @pl.when(j == 0)
        def _():
            dh_ref[...] = jnp.zeros_like(dh_ref)

        chunks = FFN_CHUNKS[::-1]
        dys = [_dot(df_ref[rs, :], wd_ref[...], NT) for rs in chunks]
        row = lax.broadcasted_iota(jnp.int32, (8, 1), 0)
        after = [carry[j, hh] for hh in range(2)]
        upd = [jnp.zeros((8, TK), F32) for _ in range(2)]
        for rs, dy in zip(chunks, dys):
            act, grad = _gelu_tanh(cv_ref[0, rs, :].astype(F32))
            dcs = (dy * cv_ref[1, rs, :].astype(F32) * grad, dy * act)
            part = dh_ref[rs, :]
            for hh, w_ref in ((0, wg_ref), (1, wv_ref)):
                dc = dcs[hh]
                x = up_ref[hh, rs, :].astype(F32)
                after1, after2 = _taps_after(dc, after[hh])
                for ridx, sm in enumerate((_colsum(after2 * x), _colsum(after1 * x), _colsum(dc * x), _colsum(dc))):
                    upd[hh] = upd[hh] + jnp.where(row == ridx, sm, 0.0)
                dup = (cw_ref[hh, 2:3, :] * dc + cw_ref[hh, 1:2, :] * after1 + cw_ref[hh, 0:1, :] * after2).astype(BF16)
                after[hh] = dc[:HALO, :]
                dup_ref[hh, rs, :] = dup
                part = part + _dot(dup, w_ref[...], NT)
            dh_ref[rs, :] = part
        for hh in range(2):
            sums_ref[j, hh] += upd[hh]
            carry[j, hh] = after[hh]

        if ns:
            pl.when((i == nm - 1) & (j == ncol - 1))(finish)

    wg, wv, wd, cw, _ = _ffn_weight_specs(ncol)
    rev = lambda i: nm - 1 - i
    res = pl.pallas_call(
        body, grid=(nm, ncol),
        in_specs=[pl.BlockSpec((TM, D_MODEL), lambda i, j: (rev(i), 0)), wg, wv, wd, cw,
                  pl.BlockSpec((2, TM, TK), lambda i, j: (0, rev(i), j)),
                  pl.BlockSpec((2, TM, TK), lambda i, j: (0, rev(i), j))] + [ANY] * ns,
        out_specs=[pl.BlockSpec((2, TM, TK), lambda i, j: (0, rev(i), j)),
                   pl.BlockSpec((TM, D_MODEL), lambda i, j: (rev(i), 0)),
                   pl.BlockSpec((ncol, 2, 8, TK), lambda i, j: (0, 0, 0, 0))] + [ANY] * ns,
        out_shape=[jax.ShapeDtypeStruct((2, s, D_FF), BF16), jax.ShapeDtypeStruct((s, D_MODEL), F32),
                   jax.ShapeDtypeStruct((ncol, 2, 8, TK), F32)] + (_scattered_shapes(scatter[1]) if ns else []),
        scratch_shapes=[pltpu.VMEM((ncol, 2, HALO, TK), F32)] + (_scatter_sems(ns) if ns else []),
        compiler_params=_cparams("arbitrary", "arbitrary"), name=name)(df, w_up, w_up, w_down, cw3, up3, cv3,
                                                                      *(scatter[0] if ns else []))
    return res[:3], list(res[3:])


def _wspec(rows, cols, index_map):
    return pl.BlockSpec((None, None, rows, cols), index_map)


def _layer_forward(l, x0, h1, p, wg, tabs, gather=None, late=None, g_next=None):
    s = x0.shape[0]
    nm = s // TMM
    tag = f"_l{l}"
    riders = dict.fromkeys(DILATIONS)
    proj_rider = rope_rider = combine_rider = None
    if late is not None:
        cols = lambda t, parts: [t[:, i * t.shape[1] // parts:(i + 1) * t.shape[1] // parts] for i in range(parts)]
        (down_a, down_b), up_q = cols(late["w_down"], 2), cols(late["w_up"], 4)
        proj_rider, rope_rider, combine_rider = [late["w_out"], down_a], [up_q[2]], [up_q[3]]
        riders = dict(zip(DILATIONS, ([down_b], [up_q[0]], [up_q[1]])))
    proj = _matmul(
        h1, wg["w_in"], grid=(nm, N_CHIPS), a_spec=pl.BlockSpec((TMM, D_MODEL), lambda i, j: (i, 0)),
        b_spec=_wspec(D_MODEL, IN_COLS // N_CHIPS, lambda i, j: (j, 0, 0, 0)),
        o_spec=pl.BlockSpec((TMM, IN_COLS // N_CHIPS), lambda i, j: (i, j)), o_shape=(s, IN_COLS), o_dtype=BF16,
        dims=NN, nk=1, kaxis=None, acc_shape=None, name="proj" + tag, gather=proj_rider)
    if late is not None:
        proj, (w_out_all4, down_a) = proj
    ma, next_out = _mixer_a_fwd(proj, p["v_norm_g"], p["v_norm_b"], p["w_spatial"], p["bs_full"], p["out_norm_a"],
                                "mixer_a_fwd" + tag, [gather["w_out"]] if gather else None)
    q, k, v, rope_landed = _rope_fwd(proj, tabs, "rope_fwd" + tag, rope_rider)
    outs, lses, landed = zip(*[
        _attn_fwd(_as_classes(q[d]), _as_classes(k[d]), _as_classes(v[d]), f"attn_fwd_d{d}" + tag, riders[d])
        for d in DILATIONS])
    outs = [o.reshape(s, B_WIDTH) if d == 1 else o for o, d in zip(outs, DILATIONS)]
    lses = [t.reshape(s, B_WIDTH) if d == 1 else t for t, d in zip(lses, DILATIONS)]
    ob, lse, mixed, combine_landed = _attn_combine(outs, lses, p["out_norm_b"], ma, "attn_combine" + tag,
                                                   combine_rider)
    if late is not None:
        wg = dict(wg, w_out=w_out_all4, w_down=jnp.concatenate([down_a, landed[0][0]], axis=-1),
                  w_up=jnp.concatenate([landed[1][0], landed[2][0], rope_landed[0], combine_landed[0]], axis=-1))
    (y1, x1, h2), next_in = _mix_out_norm(mixed, wg["w_out"], x0, p["post_mix_norm"], p["pre_ffn_norm"],
                                          "mix_out" + tag, [gather["w_in"]] if gather else None)
    post = None if g_next is None else (x1, p["post_ffn_norm"], g_next)
    (y, up3, cv3, f, *after), next_ffn = _ffn_forward(h2, wg["w_up"], wg["w_down"], p["cw3"], p["cb3"], "ffn_fwd" + tag,
                                                      [gather["w_up"], gather["w_down"]] if gather else None, post)
    gathered = dict(w_in=next_in[0], w_out=next_out[0], w_up=next_ffn[0], w_down=next_ffn[1]) if gather else None
    saved = dict(x0=x0, h1=h1, proj=proj, q=q, k=k, v=v, ob=ob, lse=lse, mixed=mixed, y1=y1, x1=x1, h2=h2,
                 up3=up3, cv3=cv3, y=y, f=f)
    if after:
        saved.update(x2=after[0], h_next=after[1])
    return saved, gathered, wg


def _layer_backward(l, dx2, df, sv, p, wg, tabs, pos, scatter=None, hide=False):
    s = dx2.shape[0]
    nm = s // TMM
    tag = f"_l{l}"
    g = {}
    (dup3, dh2, conv_sums), scattered = _ffn_backward(df, wg["w_up"], wg["w_down"], sv["up3"], sv["cv3"], p["cw3"],
                                                      "ffn_bwd" + tag, scatter)
    sums = conv_sums.transpose(1, 2, 0, 3).reshape(2, 8, D_FF)
    g["conv_w"] = jnp.concatenate([sums[0, :3], sums[1, :3]], axis=1)
    g["conv_b"] = jnp.concatenate([sums[0, 3:4], sums[1, 3:4]], axis=1)
    tn = 1024
    done = {}
    gw_down = _matmul(
        sv["y"], df, grid=(D_FF // tn,), a_spec=pl.BlockSpec((s, tn), lambda k: (0, k)),
        b_spec=pl.BlockSpec((s, D_MODEL), lambda k: (0, 0)),
        o_spec=pl.BlockSpec((2, tn, D_MODEL // 2), lambda k: (0, k, 0)),
        o_shape=(2, D_FF, D_MODEL // 2), o_dtype=BF16,
        dims=TN, nk=1, kaxis=None, acc_shape=None, name="w_down_grad" + tag, halves=True)
    pair_sum = lambda n, grad, recv: _pair_sum({n: grad}, recv, pos, (n,), f"pair_sum_l{l}")
    gw_up, received = _matmul(
        sv["h2"], dup3, grid=(2 * D_FF // tn,), a_spec=pl.BlockSpec((s, D_MODEL), lambda n: (0, 0)),
        b_spec=pl.BlockSpec((None, s, tn), lambda n: (n // (D_FF // tn), 0, n % (D_FF // tn))),
        o_spec=pl.BlockSpec((None, D_MODEL, tn), lambda n: (n // 2, 0, n % 2)),
        o_shape=(N_CHIPS, D_MODEL, 2 * D_FF // N_CHIPS), o_dtype=BF16,
        dims=TN, nk=1, kaxis=None, acc_shape=None, name="w_up_grad" + tag, scatter=([gw_down], ("x:w_down",)))
    down_sums = pair_sum("w_down", gw_down, received)
    dx1, dy1, g["pre_ffn_norm"], g["post_mix_norm"] = _norm_bwd_mid(
        dx2, dh2, sv["x1"], p["pre_ffn_norm"], sv["y1"], p["post_mix_norm"], "norm_bwd_mid" + tag)
    w_out_all = pl.BlockSpec((N_CHIPS, None, D_MODEL // N_CHIPS, D_MODEL), lambda i: (0, 0, 0, 0))
    dmixed, received = _matmul(
        dy1, wg["w_out"], grid=(nm,), a_spec=pl.BlockSpec((TMM, D_MODEL), lambda i: (i, 0)), b_spec=w_out_all,
        o_spec=pl.BlockSpec((TMM, D_MODEL), lambda i: (i, 0)), o_shape=(s, D_MODEL), o_dtype=F32,
        dims=NT, nk=1, kaxis=None, acc_shape=None, name="mix_out_bwd" + tag, b_2d=(D_MODEL, D_MODEL),
        scatter=([gw_up], ("x:w_up",)))
    up_sums = pair_sum("w_up", gw_up, received)
    gw_out = _matmul(
        sv["mixed"], dy1, grid=(1,), a_spec=pl.BlockSpec((s, D_MODEL), lambda m: (0, 0)),
        b_spec=pl.BlockSpec((s, D_MODEL), lambda m: (0, 0)),
        o_spec=pl.BlockSpec((2, D_MODEL, D_MODEL // 2), lambda m: (0, 0, 0)),
        o_shape=(2, D_MODEL, D_MODEL // 2), o_dtype=BF16,
        dims=TN, nk=1, kaxis=None, acc_shape=None, name="w_out_grad" + tag, halves=True)
    dpa, g["out_norm_a"], g["v_norm_g"], g["v_norm_b"], dbs, g["w_spatial"], received = _mixer_a_bwd(
        sv["proj"], dmixed, p["v_norm_g"], p["v_norm_b"], p["w_spatial"], p["bs_full"], p["out_norm_a"],
        "mixer_a_bwd" + tag, ([gw_out], ("x:w_out",)))
    out_sums = pair_sum("w_out", gw_out, received)
    g["b_spatial"] = dbs[:, ::GROUP_DIM].T
    dob, delta, g["out_norm_b"] = _attn_bwd_prep(dmixed, sv["ob"], p["out_norm_b"], "attn_bwd_prep" + tag)
    riders = dict(zip(DILATIONS, ((down_sums, ("w_down",)), (up_sums, ("w_up:0",)), (up_sums, ("w_up:1",))))) if hide else {}
    dqs, dks, dvs, edges, received = zip(*[
        _attn_bwd(*(_as_classes(t[d]) for t in (sv["q"], sv["k"], sv["v"], dob, sv["lse"], delta)),
                  f"attn_bwd_d{d}" + tag, riders.get(d))
        for d in DILATIONS])
    if hide:
        done[("w_down",)] = (down_sums, received[0])
        done[("w_up",)] = (up_sums, [jnp.concatenate([received[1][0], received[2][0]], axis=-1)])
    nat = lambda ts: [t.reshape(s, B_WIDTH) if d == 1 else t for t, d in zip(ts, DILATIONS)]
    halos = [t[0] for t in edges[0]]
    dproj = _rope_bwd(nat(dqs), nat(dks), nat(dvs), halos, tabs, dpa, "rope_bwd" + tag)
    wcol = IN_COLS // N_CHIPS
    gw_in = _matmul(
        sv["h1"], dproj, grid=(N_CHIPS,), a_spec=pl.BlockSpec((s, D_MODEL), lambda n: (0, 0)),
        b_spec=pl.BlockSpec((s, wcol), lambda n: (0, n)),
        o_spec=pl.BlockSpec((None, D_MODEL, wcol), lambda n: (n, 0, 0)),
        o_shape=(N_CHIPS, D_MODEL, wcol), o_dtype=BF16,
        dims=TN, nk=1, kaxis=None, acc_shape=None, name="w_in_grad" + tag,
        scatter=(out_sums, ("w_out",)) if hide else None)
    if hide:
        gw_in, received = gw_in
        done[("w_out",)] = (out_sums, received)
        in_sums = _chip_sums(l, dict(w_in=gw_in), pos, ("w_in",))
        dh1, received = _proj_bwd(dproj, wg["w_in"], "proj_bwd" + tag, (in_sums, ("w_in",)))
        done[("w_in",)] = (in_sums, received)
        return dx1, dh1, {}, g, scattered, done
    dh1, received = _proj_bwd(dproj, wg["w_in"], "proj_bwd" + tag, ([gw_in], ("x:w_in",)))
    sums = dict(w_in=pair_sum("w_in", gw_in, received)[0], w_up=up_sums[0], w_out=out_sums[0], w_down=down_sums[0])
    return dx1, dh1, sums, g, scattered, done


SMALL = ("pre_mix_norm", "v_norm_g", "v_norm_b", "w_spatial", "b_spatial", "out_norm_a", "out_norm_b",
         "post_mix_norm", "pre_ffn_norm", "conv_b", "post_ffn_norm")
BIG = ("w_in", "w_out", "w_up", "w_down")
DEPTH = 2


def _layer_params(l, small, conv_w_full):
    p = {n: small[n][l].reshape(1, -1) for n in SMALL if n not in ("w_spatial", "b_spatial")}
    p["w_spatial"] = small["w_spatial"][l]
    p["bs_full"] = jnp.repeat(small["b_spatial"][l].T, GROUP_DIM, axis=1)
    p["cw3"] = conv_w_full[l].reshape(3, 2, D_FF).transpose(1, 0, 2)
    p["cb3"] = small["conv_b"][l].reshape(2, 1, D_FF)
    return p


def _mesh_pos():
    return lax.axis_index("x"), lax.axis_index("y"), lax.axis_index("c")


def _other_chips(x, y):
    return [(1 - x, y), (x, 1 - y), (1 - x, 1 - y)]


def _gathered_shapes(blocks):
    return [jax.ShapeDtypeStruct((N_CHIPS, 1) + a.shape, a.dtype) for a in blocks]


def _gather_sems(nw):
    n = 2 * nw * (N_CHIPS - 1) + nw
    return [pltpu.SemaphoreType.DMA((n,)), pltpu.SemaphoreType.DMA((n,))]


def _gather_steps(ins, outs, send, recv):
    nw, nrel = len(ins), N_CHIPS - 1
    x, y, c = _mesh_pos()
    mine, sibling, chips = 2 * x + y, (x, y, 1 - c), _other_chips(x, y)

    def copy(src, dst, slot, to):
        return pltpu.make_async_remote_copy(src_ref=src, dst_ref=dst, send_sem=send.at[slot],
                                            recv_sem=recv.at[slot], device_id=to, device_id_type=MESH)

    def half_rows(t, core):
        rows = ins[t].shape[0] // 2
        return pl.ds(pl.multiple_of(core * rows, rows), rows)

    def landing(t, chip, core):
        return outs[t].at[chip, 0, half_rows(t, core), :]

    slots = [(t, r, chip) for t in range(nw) for r, chip in enumerate(chips)]
    own = [copy(ins[t], outs[t].at[mine, 0], 2 * nw * nrel + t, sibling) for t in range(nw)]
    first = [copy(ins[t].at[half_rows(t, c), :], landing(t, mine, c), t * nrel + r, (px, py, c))
             for t, r, (px, py) in slots]
    relays = [copy(landing(t, 2 * px + py, c), landing(t, 2 * px + py, c), nw * nrel + t * nrel + r, sibling)
              for t, r, (px, py) in slots]

    def start():
        for cp in own + first:
            cp.start()

    def relay():
        for (t, r, (px, py)), cp in zip(slots, relays):
            copy(landing(t, 2 * px + py, c), landing(t, 2 * px + py, c), t * nrel + r, (px, py, c)).wait_recv()
            cp.start()

    def finish():
        for t, r, (px, py) in slots:
            passed = landing(t, 2 * px + py, 1 - c)
            copy(passed, passed, nw * nrel + t * nrel + r, sibling).wait_recv()
        for cp in first + relays:
            cp.wait_send()
        for cp in own:
            cp.wait()

    return start, relay, finish


HALF = 512

GRAD_GEOM = {"w_in": ("rows", D_MODEL, IN_COLS // N_CHIPS), "w_up": ("rows", D_MODEL, 2 * D_FF // N_CHIPS),
             "w_out": ("cols", D_MODEL, D_MODEL // N_CHIPS), "w_down": ("cols", D_FF, D_FF // N_CHIPS)}


def _exchange_shape(n):
    kind, a, b = GRAD_GEOM[n]
    return (N_CHIPS, HALF, b) if kind == "rows" else (a, HALF)


def _piece_shape(n):
    name, _, part = n.partition(":")
    kind, _, b = GRAD_GEOM[name]
    if part:
        assert kind == "rows"
        return (HALF, b // 2)
    return (HALF, b) if kind == "rows" else (b, HALF)


def _half_of(ref, n, core):
    if GRAD_GEOM[n][0] == "rows":
        return ref.at[:, pl.ds(pl.multiple_of(core * HALF, HALF), HALF), :]
    return ref.at[core]


def _piece_of(ref, n, chip):
    name, _, part = n.partition(":")
    kind, _, b = GRAD_GEOM[name]
    if part:
        return ref.at[chip, :, pl.ds(int(part) * (b // 2), b // 2)]
    return ref.at[chip] if kind == "rows" else ref.at[pl.ds(pl.multiple_of(chip * b, b), b), :]


def _pair_exchange(g, names, name):
    n = len(names)

    def body(*refs):
        send, recv = refs[2 * n:]
        x, y, c = _mesh_pos()
        o = 1 - c
        cps = [pltpu.make_async_remote_copy(src_ref=_half_of(refs[t], nm, o), dst_ref=refs[n + t], send_sem=send.at[t],
                                            recv_sem=recv.at[t], device_id=(x, y, o), device_id_type=MESH)
               for t, nm in enumerate(names)]
        for cp in cps:
            cp.start()
        for cp in cps:
            cp.wait()

    return pl.pallas_call(
        body, in_specs=[ANY] * n, out_specs=[ANY] * n,
        out_shape=[jax.ShapeDtypeStruct(_exchange_shape(nm), BF16) for nm in names],
        scratch_shapes=[pltpu.SemaphoreType.DMA((n,)), pltpu.SemaphoreType.DMA((n,))],
        name=name)(*[g[nm] for nm in names])


def _pair_sum(g, recv, pos, names, name_prefix):
    def add(a, b, grid, a_spec, b_spec, name):
        def body(pos_ref, a_ref, b_ref, o_ref):
            o_ref[...] = (a_ref[...].astype(F32) + b_ref[...].astype(F32)).astype(BF16)

        return pl.pallas_call(
            body, grid_spec=pltpu.PrefetchScalarGridSpec(
                num_scalar_prefetch=1, grid=grid, in_specs=[a_spec, b_spec], out_specs=b_spec),
            out_shape=jax.ShapeDtypeStruct(b.shape, BF16), compiler_params=_cparams("parallel"), name=name)(pos, a, b)

    out = []
    for nm, r in zip(names, recv):
        kind, rows, width = GRAD_GEOM[nm]
        if kind == "rows":
            out.append(add(g[nm], r, (N_CHIPS,), pl.BlockSpec((None, HALF, width), lambda j, pos: (j, pos[2], 0)),
                           pl.BlockSpec((None, HALF, width), lambda j, pos: (j, 0, 0)), f"{name_prefix}_{nm}"))
        else:
            out.append(add(g[nm], r, (rows // D_MODEL,), pl.BlockSpec((None, D_MODEL, HALF), lambda j, pos: (pos[2], j, 0)),
                           pl.BlockSpec((D_MODEL, HALF), lambda j, pos: (j, 0)), f"{name_prefix}_{nm}"))
    return out


def _scattered_shapes(names):
    return [jax.ShapeDtypeStruct(_exchange_shape(nm[2:]) if nm.startswith("x:") else (N_CHIPS - 1,) + _piece_shape(nm),
                                 BF16) for nm in names]


def _scatter_sems(n):
    return [pltpu.SemaphoreType.DMA((n * (N_CHIPS - 1),)), pltpu.SemaphoreType.DMA((n * (N_CHIPS - 1),))]


def _scatter_steps(sums, outs, send, recv, names):
    nrel = N_CHIPS - 1
    x, y, c = _mesh_pos()
    cps = [pltpu.make_async_remote_copy(
        src_ref=_half_of(sums[t], nm[2:], 1 - c), dst_ref=outs[t], send_sem=send.at[t * nrel],
        recv_sem=recv.at[t * nrel], device_id=(x, y, 1 - c), device_id_type=MESH)
        for t, nm in enumerate(names) if nm.startswith("x:")]
    for r, (px, py) in enumerate(_other_chips(x, y)):
        for t, nm in enumerate(names):
            if nm.startswith("x:"):
                continue
            cps.append(pltpu.make_async_remote_copy(
                src_ref=_piece_of(sums[t], nm, 2 * px + py), dst_ref=outs[t].at[r], send_sem=send.at[t * nrel + r],
                recv_sem=recv.at[t * nrel + r], device_id=(px, py, c), device_id_type=MESH))

    def start():
        for cp in cps:
            cp.start()

    def finish():
        for cp in cps:
            cp.wait()

    return start, finish


def _chip_scatter(sums, names, name):
    n = len(names)

    def body(*refs):
        start, finish = _scatter_steps(refs[:n], refs[n:2 * n], *refs[2 * n:], names)
        start()
        finish()

    return pl.pallas_call(
        body, in_specs=[ANY] * n, out_specs=[ANY] * n, out_shape=_scattered_shapes(names),
        scratch_shapes=_scatter_sems(n), name=name)(*sums)


def _chip_sum(sums, recv, pos, names, name_prefix):
    def add(a, b, a_spec, shape, name):
        def body(pos_ref, a_ref, b_ref, o_ref):
            tot = a_ref[...].astype(F32)
            for r in range(N_CHIPS - 1):
                tot = tot + b_ref[r].astype(F32)
            o_ref[...] = tot

        return pl.pallas_call(
            body, grid_spec=pltpu.PrefetchScalarGridSpec(
                num_scalar_prefetch=1, grid=(1,), in_specs=[a_spec, pl.BlockSpec(b.shape, lambda i, pos: (0, 0, 0))],
                out_specs=pl.BlockSpec((None,) + shape, lambda i, pos: (pos[2], 0, 0))),
            out_shape=jax.ShapeDtypeStruct((2,) + shape, F32), compiler_params=_cparams("arbitrary"),
            name=name)(pos, a, b)

    chip = lambda pos: 2 * pos[0] + pos[1]
    out = []
    for nm, a, b in zip(names, sums, recv):
        shape = _piece_shape(nm)
        if GRAD_GEOM[nm][0] == "rows":
            spec = pl.BlockSpec((None,) + shape, lambda i, pos: (chip(pos), 0, 0))
        else:
            spec = pl.BlockSpec(shape, lambda i, pos: (chip(pos), 0))
        out.append(add(a, b, spec, shape, f"{name_prefix}_{nm}"))
    return out


def _pair_share(totals, name):
    n = len(totals)

    def body(*refs):
        ins, outs = refs[:n], refs[n:2 * n]
        send, recv = refs[2 * n:]
        x, y, c = _mesh_pos()
        o = 1 - c
        cps = [pltpu.make_async_remote_copy(src_ref=ins[t].at[c], dst_ref=outs[t].at[c], send_sem=send.at[t],
                                            recv_sem=recv.at[t], device_id=(x, y, o), device_id_type=MESH)
               for t in range(n)]
        for cp in cps:
            cp.start()
        for t in range(n):
            pltpu.make_async_remote_copy(src_ref=ins[t].at[o], dst_ref=outs[t].at[o], send_sem=send.at[t],
                                         recv_sem=recv.at[t], device_id=(x, y, o), device_id_type=MESH).wait_recv()
        for cp in cps:
            cp.wait_send()

    return pl.pallas_call(
        body, in_specs=[ANY] * n, out_specs=[ANY] * n,
        out_shape=[jax.ShapeDtypeStruct(t.shape, t.dtype) for t in totals],
        scratch_shapes=[pltpu.SemaphoreType.DMA((n,)), pltpu.SemaphoreType.DMA((n,))],
        input_output_aliases={t: t for t in range(n)}, name=name)(*totals)


def _chip_sums(l, g, pos, names):
    tag = f"l{l}_" + "_".join(names)
    recv = _pair_exchange(g, names, "pair_exchange_" + tag)
    return _pair_sum(g, recv, pos, names, "pair_sum_" + tag)


def _gradient_shards(l, sums, scattered, pos, names):
    tag = f"l{l}_" + "_".join(names)
    halves = _pair_share(_chip_sum(sums, scattered, pos, names, "chip_sum_" + tag), "pair_share_" + tag)
    out = {}
    for nm, t in zip(names, halves):
        rows, cols = _piece_shape(nm)
        out[nm] = t.reshape(2 * rows, cols) if GRAD_GEOM[nm][0] == "rows" else t
    return out


def _allreduce_small(packed, name):
    rows = packed.shape[0]
    half = rows // 2
    assert half % 8 == 0

    def body(x_ref, out_ref, sib, parts, done, landed, send_sems, recv_sems):
        x, y, c = _mesh_pos()
        sibling = (x, y, 1 - c)
        mine = pl.ds(pl.multiple_of(c * half, 8), half)
        other = pl.ds(pl.multiple_of((1 - c) * half, 8), half)

        def copy(k, src, dst, to):
            return pltpu.make_async_remote_copy(src_ref=src, dst_ref=dst, send_sem=send_sems.at[k],
                                                recv_sem=recv_sems.at[k], device_id=to, device_id_type=MESH)

        swap = copy(0, x_ref.at[other, :], sib, sibling)
        swap.start()
        swap.wait()
        parts[0] = x_ref[mine, :] + sib[...]
        sends = [copy(1 + j, parts.at[0], parts.at[1 + j], (*chip, c)) for j, chip in enumerate(_other_chips(x, y))]
        for cp in sends:
            cp.start()
        for cp in sends:
            cp.wait()
        tot = None
        for chip in range(N_CHIPS):
            rel = jnp.bitwise_xor(chip, 2 * x + y)
            slot = jnp.where(rel == 0, 0, jnp.where(rel == 2, 1, jnp.where(rel == 1, 2, 3)))
            term = parts[slot]
            tot = term if tot is None else tot + term
        out_ref[mine, :] = tot
        done[...] = tot
        back = copy(4, done, landed, sibling)
        back.start()
        back.wait()
        out_ref[other, :] = landed[...]

    vmem = pl.BlockSpec(memory_space=pltpu.VMEM)
    return pl.pallas_call(
        body, in_specs=[vmem], out_specs=vmem, out_shape=jax.ShapeDtypeStruct((rows, LANES), F32),
        scratch_shapes=[pltpu.VMEM((half, LANES), F32), pltpu.VMEM((N_CHIPS, half, LANES), F32),
                        pltpu.VMEM((half, LANES), F32), pltpu.VMEM((half, LANES), F32),
                        pltpu.SemaphoreType.DMA((5,)), pltpu.SemaphoreType.DMA((5,))],
        compiler_params=pltpu.CompilerParams(vmem_limit_bytes=VMEM_LIMIT_BYTES),
        name=name)(packed)


def _adamw_step(w, g, m, v):
    mn = ADAM_B1 * m + (1.0 - ADAM_B1) * g
    vn = ADAM_B2 * v + (1.0 - ADAM_B2) * (g * g)
    m_hat = mn / (1.0 - ADAM_B1 ** ADAM_STEP)
    v_hat = vn / (1.0 - ADAM_B2 ** ADAM_STEP)
    return -ADAM_LR * (m_hat / (jnp.sqrt(v_hat) + ADAM_EPS) + ADAM_WD * w), mn, vn


def _adamw(w, g, m, v, name):
    rows, cols = w.shape
    tr = 256 if rows % 256 == 0 else rows

    def body(w_ref, g_ref, m_ref, v_ref, d_ref, mo_ref, vo_ref):
        d_ref[...], mo_ref[...], vo_ref[...] = _adamw_step(w_ref[...], g_ref[...], m_ref[...], v_ref[...])

    spec = pl.BlockSpec((tr, cols), lambda i: (i, 0))
    return pl.pallas_call(
        body, grid=(rows // tr,), in_specs=[spec] * 4, out_specs=[spec] * 3,
        out_shape=[jax.ShapeDtypeStruct((rows, cols), F32)] * 3, compiler_params=_cparams("parallel"),
        name=name)(w, g, m, v)


def _adamw_layers(w, gs, m, v, name):
    depth, rows, cols = w.shape
    tr = 256
    nblk = rows // tr
    split = gs[0].ndim == 3

    def body(w_ref, m_ref, v_ref, *rest):
        g_refs, (g_out, d_ref, mo_ref, vo_ref) = rest[:depth], rest[depth:]
        layer = pl.program_id(0)
        load = (lambda r: jnp.concatenate([r[0], r[1]], axis=-1)) if split else (lambda r: r[...])
        gv = load(g_refs[0])
        for k in range(1, depth):
            gv = jnp.where(layer == k, load(g_refs[k]), gv)
        g_out[...] = gv
        d_ref[...], mo_ref[...], vo_ref[...] = _adamw_step(w_ref[...], gv, m_ref[...], v_ref[...])

    def g_spec(k):
        tile = lambda l, i: jnp.where(l == k, i, jnp.where(l < k, 0, nblk - 1))
        if split:
            return pl.BlockSpec((2, tr, cols // 2), lambda l, i: (0, tile(l, i), 0))
        return pl.BlockSpec((tr, cols), lambda l, i: (tile(l, i), 0))

    spec = pl.BlockSpec((None, tr, cols), lambda l, i: (l, i, 0))
    return pl.pallas_call(
        body, grid=(depth, nblk), in_specs=[spec] * 3 + [g_spec(k) for k in range(depth)], out_specs=[spec] * 4,
        out_shape=[jax.ShapeDtypeStruct(w.shape, F32)] * 4, compiler_params=_cparams("parallel", "parallel"),
        name=name)(w, m, v, *gs)


def _adamw_nd(w, g, m, v, name):
    cols = w.shape[-1] if w.shape[-1] % LANES == 0 else LANES
    outs = _adamw(*(t.reshape(-1, cols) for t in (w, g, m, v)), name)
    return tuple(t.reshape(w.shape) for t in outs)


def _pack(arrays):
    return jnp.concatenate([a.reshape(-1, LANES) for a in arrays], axis=0)


def _unpack(packed, shapes):
    out, row = [], 0
    for sh in shapes:
        n = math.prod(sh) // LANES
        out.append(packed[row:row + n].reshape(sh))
        row += n
    return out


WEIGHTS = ("pre_mix_norm", "w_in", "v_norm_g", "v_norm_b", "w_spatial", "b_spatial", "out_norm_a", "out_norm_b",
           "w_out", "post_mix_norm", "pre_ffn_norm", "w_up", "conv_w", "conv_b", "w_down", "post_ffn_norm")


def kernel(x, pre_mix_norm, w_in, v_norm_g, v_norm_b, w_spatial, b_spatial, out_norm_a, out_norm_b, w_out, post_mix_norm, pre_ffn_norm, w_up, conv_w, conv_b, w_down, post_ffn_norm, loss_target, m_pre_mix_norm, m_w_in, m_v_norm_g, m_v_norm_b, m_w_spatial, m_b_spatial, m_out_norm_a, m_out_norm_b, m_w_out, m_post_mix_norm, m_pre_ffn_norm, m_w_up, m_conv_w, m_conv_b, m_w_down, m_post_ffn_norm, v_pre_mix_norm, v_w_in, v_v_norm_g, v_v_norm_b, v_w_spatial, v_b_spatial, v_out_norm_a, v_out_norm_b, v_w_out, v_post_mix_norm, v_pre_ffn_norm, v_w_up, v_conv_w, v_conv_b, v_w_down, v_post_ffn_norm):
    w = dict(pre_mix_norm=pre_mix_norm, w_in=w_in, v_norm_g=v_norm_g, v_norm_b=v_norm_b, w_spatial=w_spatial,
             b_spatial=b_spatial, out_norm_a=out_norm_a, out_norm_b=out_norm_b, w_out=w_out,
             post_mix_norm=post_mix_norm, pre_ffn_norm=pre_ffn_norm, w_up=w_up, conv_w=conv_w, conv_b=conv_b,
             w_down=w_down, post_ffn_norm=post_ffn_norm)
    m = dict(pre_mix_norm=m_pre_mix_norm, w_in=m_w_in, v_norm_g=m_v_norm_g, v_norm_b=m_v_norm_b,
             w_spatial=m_w_spatial, b_spatial=m_b_spatial, out_norm_a=m_out_norm_a, out_norm_b=m_out_norm_b,
             w_out=m_w_out, post_mix_norm=m_post_mix_norm, pre_ffn_norm=m_pre_ffn_norm, w_up=m_w_up,
             conv_w=m_conv_w, conv_b=m_conv_b, w_down=m_w_down, post_ffn_norm=m_post_ffn_norm)
    v = dict(pre_mix_norm=v_pre_mix_norm, w_in=v_w_in, v_norm_g=v_v_norm_g, v_norm_b=v_v_norm_b,
             w_spatial=v_w_spatial, b_spatial=v_b_spatial, out_norm_a=v_out_norm_a, out_norm_b=v_out_norm_b,
             w_out=v_w_out, post_mix_norm=v_post_mix_norm, pre_ffn_norm=v_pre_ffn_norm, w_up=v_w_up,
             conv_w=v_conv_w, conv_b=v_conv_b, w_down=v_w_down, post_ffn_norm=v_post_ffn_norm)
    pos = jnp.stack([lax.axis_index("x"), lax.axis_index("y"), lax.axis_index("c")]).astype(jnp.int32)
    chip = 2 * lax.axis_index("x") + lax.axis_index("y")

    cw_cols = conv_w.shape[-1]
    blocks = [{n: w[n][l].astype(BF16) for n in BIG} for l in range(DEPTH)]
    small = {n: w[n] for n in SMALL}
    xs, target = x[0], loss_target[0]
    xin = xs
    h, (w_in0, cw_all) = _rms_cast(xin, small["pre_mix_norm"][0].reshape(1, -1), "pre_mix_l0",
                                   [blocks[0]["w_in"], conv_w.reshape(-1, LANES)])
    wg = dict(w_in=w_in0)
    conv_w_full = cw_all.reshape(N_CHIPS, DEPTH, 3, cw_cols).transpose(1, 2, 0, 3).reshape(DEPTH, 3, 2 * D_FF)

    tabs = _rope_tables(xs.shape[0])
    params = [_layer_params(l, small, conv_w_full) for l in range(DEPTH)]
    saved, wgs = [], []
    for l in range(DEPTH):
        sv, gathered, wg = _layer_forward(l, xin, h, params[l], wg, tabs,
                                          blocks[l + 1] if l + 1 < DEPTH else None,
                                          blocks[0] if l == 0 else None,
                                          params[l + 1]["pre_mix_norm"] if l + 1 < DEPTH else None)
        saved.append(sv)
        wgs.append(wg)
        if l + 1 < DEPTH:
            wg = gathered
            xin, h = sv["x2"], sv["h_next"]
    loss_part, dx, df, g_post = _loss_norm_bwd(saved[-1]["x1"], saved[-1]["f"], params[-1]["post_ffn_norm"], target,
                                               "loss")
    smalls, shards = [None] * DEPTH, [{} for _ in range(DEPTH)]
    pending = None
    for l in reversed(range(DEPTH)):
        dx1, dh1, big, smalls[l], scattered, done = _layer_backward(l, dx, df, saved[l], params[l], wgs[l], tabs, pos,
                                                                    pending[1:] if pending else None, hide=l == 0)
        smalls[l]["post_ffn_norm"] = g_post
        if l > 0:
            dx, smalls[l]["pre_mix_norm"], df, g_post = _norm_bwd_in_out(
                dx1, dh1, saved[l]["x0"], params[l]["pre_mix_norm"], saved[l - 1]["f"], params[l - 1]["post_ffn_norm"],
                f"norm_bwd_in_out_l{l}")
        else:
            dx, smalls[l]["pre_mix_norm"] = _norm_bwd_in(dx1, dh1, saved[l]["x0"], params[l]["pre_mix_norm"],
                                                         "norm_bwd_in_l0")
        if pending:
            shards[pending[0]].update(_gradient_shards(pending[0], pending[1], scattered, pos, pending[2]))
        if done:
            shards[l].update(_gradient_shards(
                l, [t for sums, _ in done.values() for t in sums], [t for _, received in done.values() for t in received],
                pos, tuple(n for names in done for n in names)))
        names = tuple(big)
        pending = (l, [big[n] for n in names], names) if names else None
    if pending:
        shards[pending[0]].update(_gradient_shards(
            pending[0], pending[1], _chip_scatter(pending[1], pending[2], f"chip_scatter_l{pending[0]}"), pos,
            pending[2]))

    small_shapes = [w[n].shape for n in SMALL]
    stacked = [jnp.stack([smalls[l][n].reshape(w[n].shape[1:]) for l in range(DEPTH)]) for n in SMALL]
    cw_grad = jnp.stack([smalls[l]["conv_w"] for l in range(DEPTH)])
    packed = _pack(stacked + [cw_grad, loss_part])
    total = _allreduce_small(packed, "allreduce_small")
    parts = _unpack(total, small_shapes + [cw_grad.shape, (8, LANES)])
    g_small = dict(zip(SMALL, parts[:len(SMALL)]))
    loss = parts[-1][0, 0]
    g_conv_w = lax.dynamic_slice(parts[-2], (0, 0, chip * cw_cols), conv_w.shape)

    grads = dict(g_small, conv_w=g_conv_w)

    dp, mp, vp = _adamw(_pack([w[n] for n in SMALL]), _pack([g_small[n] for n in SMALL]),
                        _pack([m[n] for n in SMALL]), _pack([v[n] for n in SMALL]), "adamw_small")
    delta = dict(zip(SMALL, _unpack(dp, small_shapes)))
    new_m = dict(zip(SMALL, _unpack(mp, small_shapes)))
    new_v = dict(zip(SMALL, _unpack(vp, small_shapes)))
    delta["conv_w"], new_m["conv_w"], new_v["conv_w"] = _adamw_nd(w["conv_w"], g_conv_w, m["conv_w"], v["conv_w"],
                                                                  "adamw_conv_w")
    for n in BIG:
        grads[n], delta[n], new_m[n], new_v[n] = _adamw_layers(w[n], [shards[l][n] for l in range(DEPTH)], m[n],
                                                               v[n], "adamw_" + n)

    return (loss, dx[None], *[grads[n] for n in WEIGHTS], *[delta[n] for n in WEIGHTS],
            *[new_m[n] for n in WEIGHTS], *[new_v[n] for n in WEIGHTS])
```

```python
import functools
import math

import jax
import jax.numpy as jnp
import numpy as np
from jax import lax
from jax.experimental import pallas as pl
from jax.experimental.pallas import tpu as pltpu

F32 = jnp.float32
BF16 = jnp.bfloat16
MESH = pl.DeviceIdType.MESH

D_MODEL = 1024
A_WIDTH = 512
A_GROUPS = 4
GROUP_DIM = 128
CHUNK = 128
B_WIDTH = 512
HEAD_DIM = 64
ROT_DIM = 16
ROPE_THETA = 500000.0
DILATIONS = (1, 4, 16)
BAND = 128
IN_COLS = 2560
D_FF = 4096
EPS = 1e-6
NEG_INF = -1e30
N_CHIPS = 4
LANES = 128

ADAM_LR = 0.001
ADAM_B1 = 0.9
ADAM_B2 = 0.999
ADAM_EPS = 1e-08
ADAM_WD = 0.01
ADAM_STEP = 10

VMEM_LIMIT_BYTES = 56 * 1024 * 1024
RSQRT2 = 0.7071067811865476
INV_SQRT_2PI = 0.3989422804014327
GELU_C = 0.7978845608028654
GELU_A = 0.044715

ANY = pl.BlockSpec(memory_space=pl.ANY)
NN = ((1,), (0,))
NT = ((1,), (1,))
TN = ((0,), (0,))


def _cparams(*sem):
    return pltpu.CompilerParams(dimension_semantics=sem, vmem_limit_bytes=VMEM_LIMIT_BYTES)


def _dot(a, b, dims):
    return lax.dot_general(a, b, (dims, ((), ())), preferred_element_type=F32)


def _rsq_mean(a):
    return lax.rsqrt(jnp.mean(a * a, axis=-1, keepdims=True) + EPS)


def _rms_bwd(a, r, g, dz):
    t = dz * g
    da = r * t - a * (r * r * r) * jnp.mean(t * a, axis=-1, keepdims=True)
    return da, dz * a * r


def _colsum(a):
    return jnp.sum(a, axis=0, keepdims=True)


def _gelu_tanh(x):
    u = x * x
    t = jnp.tanh(x * (GELU_C + (GELU_C * GELU_A) * u))
    hx = 0.5 * x
    act = hx + hx * t
    grad = 0.5 + 0.5 * t + (hx - hx * t * t) * (GELU_C + (3.0 * GELU_C * GELU_A) * u)
    return act, grad


def _grid_edges(grid):
    ids = [pl.program_id(ax) for ax in range(len(grid))]
    first = functools.reduce(jnp.logical_and, [i == 0 for i in ids])
    last = functools.reduce(jnp.logical_and, [i == n - 1 for i, n in zip(ids, grid)])
    return first, last


def _matmul(a, b, *, grid, a_spec, b_spec, o_spec, o_shape, o_dtype, dims, nk, kaxis, acc_shape, name, b_2d=None,
            halves=False, scatter=None, gather=None):
    assert scatter is None or gather is None
    ns = len(scatter[0]) if scatter else len(gather) if gather else 0

    def body(*refs):
        a_ref, b_ref = refs[:2]
        o_ref = refs[2 + ns]
        scratch = refs[3 + 2 * ns:]
        if ns:
            first, last = _grid_edges(grid)
            if scatter:
                start, finish = _scatter_steps(refs[2:2 + ns], refs[3 + ns:3 + 2 * ns], scratch[-2], scratch[-1],
                                               scatter[1])
            else:
                start, relay, last_wait = _gather_steps(refs[2:2 + ns], refs[3 + ns:3 + 2 * ns], scratch[-2],
                                                        scratch[-1])

                def finish():
                    relay()
                    last_wait()
            pl.when(first)(start)
        def store(val):
            if halves:
                half = val.shape[1] // 2
                o_ref[0] = val[:, :half].astype(o_dtype)
                o_ref[1] = val[:, half:].astype(o_dtype)
            else:
                o_ref[...] = val.astype(o_dtype)

        bv = b_ref[...] if b_2d is None else b_ref[...].reshape(b_2d)
        part = _dot(a_ref[...], bv, dims)
        if nk == 1:
            store(part)
        else:
            acc = scratch[0]
            k = pl.program_id(kaxis)

            @pl.when(k == 0)
            def _():
                acc[...] = part

            @pl.when(k > 0)
            def _():
                acc[...] += part

            @pl.when(k == nk - 1)
            def _():
                store(acc[...])

        if ns:
            pl.when(last)(finish)

    sem = tuple("arbitrary" if (ns or (nk > 1 and ax == kaxis)) else "parallel" for ax in range(len(grid)))
    riding = list(scatter[0]) if scatter else list(gather or [])
    rider_shapes = _scattered_shapes(scatter[1]) if scatter else _gathered_shapes(riding)
    rider_sems = _scatter_sems(ns) if scatter else _gather_sems(ns) if gather else []
    res = pl.pallas_call(
        body, grid=grid, in_specs=[a_spec, b_spec] + [ANY] * ns, out_specs=[o_spec] + [ANY] * ns,
        out_shape=[jax.ShapeDtypeStruct(o_shape, o_dtype)] + rider_shapes,
        scratch_shapes=([pltpu.VMEM(acc_shape, F32)] if nk > 1 else []) + rider_sems,
        compiler_params=_cparams(*sem), name=name)(a, b, *riding)
    return (res[0], list(res[1:])) if ns else res[0]


def _mix_out_norm(mixed, w_out, x0, g_post, g_next, name, gather=None):
    s, d = x0.shape
    tm = 512
    ng = 0 if gather is None else len(gather)

    def body(a_ref, w_ref, x_ref, gp_ref, gn_ref, *rest):
        y_ref, x1_ref, h_ref = rest[ng:ng + 3]
        if ng:
            start, relay, finish = _gather_steps(rest[:ng], rest[ng + 3:2 * ng + 3], *rest[2 * ng + 3:])
            first, last = _grid_edges((s // tm,))
            pl.when(first)(start)
        y = _dot(a_ref[...], w_ref[...].reshape(d, d), NN)
        y_ref[...] = y
        x1 = x_ref[...] + y * _rsq_mean(y) * gp_ref[...]
        x1_ref[...] = x1
        h_ref[...] = (x1 * _rsq_mean(x1) * gn_ref[...]).astype(BF16)

        if ng:
            @pl.when(last)
            def _():
                relay()
                finish()

    row = pl.BlockSpec((tm, d), lambda i: (i, 0))
    vec = pl.BlockSpec((1, d), lambda i: (0, 0))
    res = pl.pallas_call(
        body, grid=(s // tm,),
        in_specs=[row, pl.BlockSpec((N_CHIPS, None, d // N_CHIPS, d), lambda i: (0, 0, 0, 0)), row, vec, vec]
        + [ANY] * ng,
        out_specs=[row, row, row] + [ANY] * ng,
        out_shape=[jax.ShapeDtypeStruct((s, d), F32), jax.ShapeDtypeStruct((s, d), F32),
                   jax.ShapeDtypeStruct((s, d), BF16)] + _gathered_shapes(gather or []),
        scratch_shapes=_gather_sems(ng) if ng else [],
        compiler_params=_cparams("arbitrary" if ng else "parallel"), name=name)(mixed, w_out, x0, g_post, g_next,
                                                                              *(gather or []))
    return res[:3], list(res[3:])


def _proj_bwd(dproj, w_in, name, scatter=None):
    s = dproj.shape[0]
    wcol = IN_COLS // N_CHIPS
    ns = 0 if scatter is None else len(scatter[0])

    def body(*refs):
        a_ref, w_ref = refs[:2]
        o_ref = refs[2 + ns]
        if ns:
            start, finish = _scatter_steps(refs[2:2 + ns], refs[3 + ns:3 + 2 * ns], *refs[3 + 2 * ns:], scatter[1])
            first, last = _grid_edges((s // TMM,))
            pl.when(first)(start)
        acc = _dot(a_ref[:, :wcol], w_ref[0], NT)
        for j in range(1, N_CHIPS):
            acc = acc + _dot(a_ref[:, j * wcol:(j + 1) * wcol], w_ref[j], NT)
        o_ref[...] = acc
        if ns:
            pl.when(last)(finish)

    res = pl.pallas_call(
        body, grid=(s // TMM,),
        in_specs=[pl.BlockSpec((TMM, IN_COLS), lambda i: (i, 0)),
                  pl.BlockSpec((N_CHIPS, None, D_MODEL, wcol), lambda i: (0, 0, 0, 0))] + [ANY] * ns,
        out_specs=[pl.BlockSpec((TMM, D_MODEL), lambda i: (i, 0))] + [ANY] * ns,
        out_shape=[jax.ShapeDtypeStruct((s, D_MODEL), F32)] + (_scattered_shapes(scatter[1]) if ns else []),
        scratch_shapes=_scatter_sems(ns) if ns else [],
        compiler_params=_cparams("arbitrary" if ns else "parallel"), name=name)(dproj, w_in,
                                                                              *(scatter[0] if ns else []))
    return res[0], list(res[1:])


TM = 1024
TMM = 1024


TR = 1024


def _row_spec(width, col=0):
    return pl.BlockSpec((TR, width), lambda i, col=col: (i, col))


def _vec_spec(width):
    return pl.BlockSpec((1, width), lambda i: (0, 0))


def _rms_cast(x, g, name, gather=None):
    s, d = x.shape
    ng = 0 if gather is None else len(gather)

    def body(x_ref, g_ref, *rest):
        if ng:
            start, relay, finish = _gather_steps(rest[:ng], rest[ng + 1:2 * ng + 1], *rest[2 * ng + 1:])
            first, last = _grid_edges((s // TR,))
            pl.when(first)(start)
        a = x_ref[...]
        rest[ng][...] = (a * _rsq_mean(a) * g_ref[...]).astype(BF16)

        if ng:
            @pl.when(last)
            def _():
                relay()
                finish()

    res = pl.pallas_call(
        body, grid=(s // TR,), in_specs=[_row_spec(d), _vec_spec(d)] + [ANY] * ng,
        out_specs=[_row_spec(d)] + [ANY] * ng,
        out_shape=[jax.ShapeDtypeStruct((s, d), BF16)] + _gathered_shapes(gather or []),
        scratch_shapes=_gather_sems(ng) if ng else [],
        compiler_params=_cparams("arbitrary" if ng else "parallel"), name=name)(x, g, *(gather or []))
    return res[0], list(res[1:])


def _acc_init(refs):
    @pl.when(pl.program_id(0) == 0)
    def _():
        for r in refs:
            r[...] = jnp.zeros_like(r)


def _loss_norm_bwd(x1, f, g_post, target, name):
    s, d = x1.shape

    def body(x_ref, f_ref, gp_ref, t_ref, loss_ref, dx_ref, df_ref, dg_ref):
        _acc_init([loss_ref, dg_ref])
        fv = f_ref[...]
        r = _rsq_mean(fv)
        err = x_ref[...] + fv * r * gp_ref[...] - t_ref[...]
        dx = err * (1.0 / d)
        dx_ref[...] = dx
        part = 0.5 * jnp.sum(jnp.mean(err * err, axis=-1, keepdims=True), axis=0, keepdims=True)
        loss_ref[...] += jnp.broadcast_to(part, loss_ref.shape)
        da, dgt = _rms_bwd(fv, r, gp_ref[...], dx)
        df_ref[...] = da.astype(BF16)
        dg_ref[...] += _colsum(dgt)

    return pl.pallas_call(
        body, grid=(s // TR,), in_specs=[_row_spec(d), _row_spec(d), _vec_spec(d), _row_spec(d)],
        out_specs=[pl.BlockSpec((8, LANES), lambda i: (0, 0)), _row_spec(d), _row_spec(d), _vec_spec(d)],
        out_shape=[jax.ShapeDtypeStruct((8, LANES), F32), jax.ShapeDtypeStruct((s, d), F32),
                   jax.ShapeDtypeStruct((s, d), BF16), jax.ShapeDtypeStruct((1, d), F32)],
        compiler_params=_cparams("arbitrary"), name=name)(x1, f, g_post, target)


def _norm_bwd_mid(dx2, dh2, x1, g_pf, y1, g_pm, name):
    s, d = dx2.shape

    def body(dx2_ref, dh_ref, x1_ref, gpf_ref, y1_ref, gpm_ref, dx1_ref, dy1_ref, dgpf_ref, dgpm_ref):
        _acc_init([dgpf_ref, dgpm_ref])
        x1 = x1_ref[...]
        da, dgt = _rms_bwd(x1, _rsq_mean(x1), gpf_ref[...], dh_ref[...])
        dx1 = dx2_ref[...] + da
        dx1_ref[...] = dx1
        dgpf_ref[...] += _colsum(dgt)
        y1 = y1_ref[...]
        dy, dgt2 = _rms_bwd(y1, _rsq_mean(y1), gpm_ref[...], dx1)
        dy1_ref[...] = dy.astype(BF16)
        dgpm_ref[...] += _colsum(dgt2)

    return pl.pallas_call(
        body, grid=(s // TR,),
        in_specs=[_row_spec(d), _row_spec(d), _row_spec(d), _vec_spec(d), _row_spec(d), _vec_spec(d)],
        out_specs=[_row_spec(d), _row_spec(d), _vec_spec(d), _vec_spec(d)],
        out_shape=[jax.ShapeDtypeStruct((s, d), F32), jax.ShapeDtypeStruct((s, d), BF16),
                   jax.ShapeDtypeStruct((1, d), F32), jax.ShapeDtypeStruct((1, d), F32)],
        compiler_params=_cparams("arbitrary"), name=name)(dx2, dh2, x1, g_pf, y1, g_pm)


def _norm_bwd_in_out(dx1, dh1, x0, g1, f_below, g_post_below, name):
    s, d = dx1.shape

    def body(dx1_ref, dh_ref, x0_ref, g_ref, f_ref, gp_ref, dx0_ref, dg_ref, df_ref, dgp_ref):
        _acc_init([dg_ref, dgp_ref])
        x0 = x0_ref[...]
        da, dgt = _rms_bwd(x0, _rsq_mean(x0), g_ref[...], dh_ref[...])
        dx0 = dx1_ref[...] + da
        dx0_ref[...] = dx0
        dg_ref[...] += _colsum(dgt)
        fv = f_ref[...]
        db, dgt2 = _rms_bwd(fv, _rsq_mean(fv), gp_ref[...], dx0)
        df_ref[...] = db.astype(BF16)
        dgp_ref[...] += _colsum(dgt2)

    return pl.pallas_call(
        body, grid=(s // TR,),
        in_specs=[_row_spec(d), _row_spec(d), _row_spec(d), _vec_spec(d), _row_spec(d), _vec_spec(d)],
        out_specs=[_row_spec(d), _vec_spec(d), _row_spec(d), _vec_spec(d)],
        out_shape=[jax.ShapeDtypeStruct((s, d), F32), jax.ShapeDtypeStruct((1, d), F32),
                   jax.ShapeDtypeStruct((s, d), BF16), jax.ShapeDtypeStruct((1, d), F32)],
        compiler_params=_cparams("arbitrary"), name=name)(dx1, dh1, x0, g1, f_below, g_post_below)


def _norm_bwd_in(dx1, dh1, x0, g1, name):
    s, d = dx1.shape

    def body(dx1_ref, dh_ref, x0_ref, g_ref, dx0_ref, dg_ref):
        _acc_init([dg_ref])
        x0 = x0_ref[...]
        da, dgt = _rms_bwd(x0, _rsq_mean(x0), g_ref[...], dh_ref[...])
        dx0_ref[...] = dx1_ref[...] + da
        dg_ref[...] += _colsum(dgt)

    return pl.pallas_call(
        body, grid=(s // TR,), in_specs=[_row_spec(d), _row_spec(d), _row_spec(d), _vec_spec(d)],
        out_specs=[_row_spec(d), _vec_spec(d)],
        out_shape=[jax.ShapeDtypeStruct((s, d), F32), jax.ShapeDtypeStruct((1, d), F32)],
        compiler_params=_cparams("arbitrary"), name=name)(dx1, dh1, x0, g1)


def _tril_mask():
    row = lax.broadcasted_iota(jnp.int32, (CHUNK, CHUNK), 0)
    col = lax.broadcasted_iota(jnp.int32, (CHUNK, CHUNK), 1)
    return row >= col


def _gating_forward(pa, gv, bv, wt, bsf):
    er = lax.erf(pa * RSQRT2)
    za = 0.5 * pa * (1.0 + er)
    u = za[:, :A_WIDTH]
    va = za[:, A_WIDTH:]
    xc = va - jnp.mean(va, axis=-1, keepdims=True)
    rs = lax.rsqrt(jnp.mean(xc * xc, axis=-1, keepdims=True) + EPS)
    vn = xc * rs
    vlb = (vn * gv + bv).astype(BF16)
    sg = jnp.concatenate(
        [_dot(wt[g], vlb[:, g * GROUP_DIM:(g + 1) * GROUP_DIM], NN) for g in range(A_GROUPS)], axis=1) + bsf
    return er, u, rs, vn, vlb, sg


def _masked_ws(ws_ref):
    mask = _tril_mask()
    return [jnp.where(mask, ws_ref[g], 0.0).astype(BF16) for g in range(A_GROUPS)]


def _mixer_a_fwd(proj, gv, bv, ws, bsf, ga, name, gather=None):
    s = proj.shape[0]
    ng = 0 if gather is None else len(gather)

    def body(p_ref, gv_ref, bv_ref, ws_ref, bs_ref, ga_ref, *rest):
        o_ref = rest[ng]
        if ng:
            start, relay, finish = _gather_steps(rest[:ng], rest[ng + 1:2 * ng + 1], *rest[2 * ng + 1:])
            first, last = _grid_edges((s // TR,))
            pl.when(first)(start)
        wt = _masked_ws(ws_ref)
        for ch in range(TR // CHUNK):
            rows = slice(ch * CHUNK, (ch + 1) * CHUNK)
            _, u, _, _, _, sg = _gating_forward(p_ref[rows, :].astype(F32), gv_ref[...], bv_ref[...], wt, bs_ref[...])
            oa = u * sg
            o_ref[rows, :] = (oa * _rsq_mean(oa) * ga_ref[...]).astype(BF16)

        if ng:
            @pl.when(last)
            def _():
                relay()
                finish()

    res = pl.pallas_call(
        body, grid=(s // TR,),
        in_specs=[_row_spec(2 * A_WIDTH), _vec_spec(A_WIDTH), _vec_spec(A_WIDTH),
                  pl.BlockSpec((A_GROUPS, CHUNK, CHUNK), lambda i: (0, 0, 0)),
                  pl.BlockSpec((CHUNK, A_WIDTH), lambda i: (0, 0)), _vec_spec(A_WIDTH)] + [ANY] * ng,
        out_specs=[_row_spec(A_WIDTH)] + [ANY] * ng,
        out_shape=[jax.ShapeDtypeStruct((s, A_WIDTH + B_WIDTH), BF16)] + _gathered_shapes(gather or []),
        scratch_shapes=_gather_sems(ng) if ng else [],
        compiler_params=_cparams("arbitrary" if ng else "parallel"), name=name)(proj, gv, bv, ws, bsf, ga,
                                                                              *(gather or []))
    return res[0], list(res[1:])


def _mixer_a_bwd(proj, dmixed, gv, bv, ws, bsf, ga, name, scatter=None):
    s = proj.shape[0]
    nsteps = s // TR
    ns = 0 if scatter is None else len(scatter[0])

    def body(*refs):
        p_ref, dm_ref, gv_ref, bv_ref, ws_ref, bs_ref, ga_ref = refs[:7]
        dp_ref, dga_ref, dgv_ref, dbv_ref, dbs_ref, dws_ref = refs[7 + ns:13 + ns]
        if ns:
            start, finish = _scatter_steps(refs[7:7 + ns], refs[13 + ns:13 + 2 * ns], *refs[13 + 2 * ns:], scatter[1])
            first, last = _grid_edges((nsteps,))
            pl.when(first)(start)
        _acc_init([dga_ref, dgv_ref, dbv_ref, dbs_ref, dws_ref])
        mask = _tril_mask()
        wt = _masked_ws(ws_ref)
        gvv = gv_ref[...]
        gav = ga_ref[...]
        for ch in range(TR // CHUNK):
            rows = slice(ch * CHUNK, (ch + 1) * CHUNK)
            pa = p_ref[rows, :].astype(F32)
            er, u, rs, vn, vlb, sg = _gating_forward(pa, gvv, bv_ref[...], wt, bs_ref[...])
            oa = u * sg
            doa, dgt = _rms_bwd(oa, _rsq_mean(oa), gav, dm_ref[rows, :])
            dga_ref[...] += _colsum(dgt)
            du = doa * sg
            dsg = doa * u
            dbs_ref[...] += dsg
            dsgb = dsg.astype(BF16)
            dvl = []
            for g in range(A_GROUPS):
                cols = slice(g * GROUP_DIM, (g + 1) * GROUP_DIM)
                dws_ref[g] += jnp.where(mask, _dot(dsgb[:, cols], vlb[:, cols], NT), 0.0)
                dvl.append(_dot(wt[g], dsgb[:, cols], TN))
            dvl = jnp.concatenate(dvl, axis=1)
            dgv_ref[...] += _colsum(dvl * vn)
            dbv_ref[...] += _colsum(dvl)
            dvn = dvl * gvv
            dva = rs * (dvn - jnp.mean(dvn, axis=-1, keepdims=True)
                        - vn * jnp.mean(dvn * vn, axis=-1, keepdims=True))
            gp = 0.5 * (1.0 + er) + pa * jnp.exp(-0.5 * pa * pa) * INV_SQRT_2PI
            dp_ref[rows, :] = (jnp.concatenate([du, dva], axis=1) * gp).astype(BF16)

        @pl.when(pl.program_id(0) == nsteps - 1)
        def _():
            for g in range(A_GROUPS):
                cols = slice(g * GROUP_DIM, (g + 1) * GROUP_DIM)
                tot = jnp.sum(dbs_ref[:, cols], axis=1, keepdims=True)
                dbs_ref[:, cols] = jnp.broadcast_to(tot, (CHUNK, GROUP_DIM))

        if ns:
            pl.when(last)(finish)

    full = lambda *shape: pl.BlockSpec(shape, lambda i: (0,) * len(shape))
    res = pl.pallas_call(
        body, grid=(nsteps,),
        in_specs=[_row_spec(2 * A_WIDTH), _row_spec(A_WIDTH), _vec_spec(A_WIDTH), _vec_spec(A_WIDTH),
                  full(A_GROUPS, CHUNK, CHUNK), full(CHUNK, A_WIDTH), _vec_spec(A_WIDTH)] + [ANY] * ns,
        out_specs=[_row_spec(2 * A_WIDTH), _vec_spec(A_WIDTH), _vec_spec(A_WIDTH), _vec_spec(A_WIDTH),
                   full(CHUNK, A_WIDTH), full(A_GROUPS, CHUNK, CHUNK)] + [ANY] * ns,
        out_shape=[jax.ShapeDtypeStruct((s, IN_COLS), BF16), jax.ShapeDtypeStruct((1, A_WIDTH), F32),
                   jax.ShapeDtypeStruct((1, A_WIDTH), F32), jax.ShapeDtypeStruct((1, A_WIDTH), F32),
                   jax.ShapeDtypeStruct((CHUNK, A_WIDTH), F32),
                   jax.ShapeDtypeStruct((A_GROUPS, CHUNK, CHUNK), F32)]
        + (_scattered_shapes(scatter[1]) if ns else []),
        scratch_shapes=_scatter_sems(ns) if ns else [],
        compiler_params=_cparams("arbitrary"), name=name)(proj, dmixed, gv, bv, ws, bsf, ga,
                                                          *(scatter[0] if ns else []))
    return res[:6] + (list(res[6:]),)


def _rope_tables(s):
    half = ROT_DIM // 2
    lane = jnp.arange(LANES) % HEAD_DIM
    inv = ROPE_THETA ** (-(2 * (lane % half)).astype(F32) / ROT_DIM)
    ang = jnp.arange(s, dtype=F32)[:, None] * inv[None, :]
    cos, sin = jnp.cos(ang), jnp.sin(ang)
    c = jnp.where(lane < ROT_DIM, cos, 1.0)
    s1 = jnp.where(lane < half, -sin, 0.0)
    s2 = jnp.where((lane >= half) & (lane < ROT_DIM), sin, 0.0)
    return c, s1, s2


def _lane_blocks(width):
    return [slice(b * LANES, (b + 1) * LANES) for b in range(width // LANES)]


CLASS_DILS = tuple(d for d in DILATIONS if d > 1)


def _class_shape(s, dil, dtype):
    return jax.ShapeDtypeStruct((dil, s // dil, B_WIDTH), dtype)


def _class_spec(dil):
    return pl.BlockSpec((dil, TR // dil, B_WIDTH), lambda i, *_: (0, i, 0))


NBLK = B_WIDTH // LANES
STAGE = pltpu.VMEM((NBLK, TR, LANES), F32)


def _stage_put(stage, value):
    for b, sl in enumerate(_lane_blocks(B_WIDTH)):
        stage[b] = value[:, sl]


def _stage_get(stage):
    return jnp.concatenate([stage[b] for b in range(NBLK)], axis=1)


def _store_classes(stage, dst_ref, dil):
    for b, sl in enumerate(_lane_blocks(B_WIDTH)):
        for r in range(dil):
            dst_ref[r, :, sl] = stage[b, pl.ds(r, TR // dil, stride=dil), :].astype(dst_ref.dtype)


def _load_classes(src_ref, stage, dil):
    for b, sl in enumerate(_lane_blocks(B_WIDTH)):
        for r in range(dil):
            stage[b, pl.ds(r, TR // dil, stride=dil), :] = src_ref[r, :, sl].astype(F32)
    return _stage_get(stage)


def _rope_fwd(proj, tabs, name, gather=None):
    s = proj.shape[0]
    half = ROT_DIM // 2
    scale = HEAD_DIM ** -0.5
    nlay = 1 + len(CLASS_DILS)
    ng = 0 if gather is None else len(gather)

    def body(q_ref, k_ref, v_ref, c_ref, s1_ref, s2_ref, *rest):
        outs, stage = rest[ng:ng + 3 * nlay], rest[2 * ng + 3 * nlay]
        if ng:
            start, relay, finish = _gather_steps(rest[:ng], rest[ng + 3 * nlay:2 * ng + 3 * nlay],
                                                 *rest[2 * ng + 3 * nlay + 1:])
            first, last = _grid_edges((s // TR,))
            pl.when(first)(start)
        c, s1, s2 = c_ref[...], s1_ref[...], s2_ref[...]
        for which, (src, mul) in enumerate(((q_ref, scale), (k_ref, 1.0), (v_ref, None))):
            if mul is None:
                _stage_put(stage, src[...].astype(F32))
            else:
                for b, sl in enumerate(_lane_blocks(B_WIDTH)):
                    a = src[:, sl].astype(F32)
                    r = a * c + pltpu.roll(a, LANES - half, 1) * s1 + pltpu.roll(a, half, 1) * s2
                    stage[b] = r * mul
            dst = outs[which * nlay:(which + 1) * nlay]
            dst[0][...] = _stage_get(stage).astype(BF16)
            for ref, d in zip(dst[1:], CLASS_DILS):
                _store_classes(stage, ref, d)

        if ng:
            @pl.when(last)
            def _():
                relay()
                finish()

    tab = pl.BlockSpec((TR, LANES), lambda i: (i, 0))
    lay_specs = [_row_spec(B_WIDTH)] + [_class_spec(d) for d in CLASS_DILS]
    lay_shapes = [jax.ShapeDtypeStruct((s, B_WIDTH), BF16)] + [_class_shape(s, d, BF16) for d in CLASS_DILS]
    outs = pl.pallas_call(
        body, grid=(s // TR,),
        in_specs=[_row_spec(B_WIDTH, 2), _row_spec(B_WIDTH, 3), _row_spec(B_WIDTH, 4), tab, tab, tab] + [ANY] * ng,
        out_specs=lay_specs * 3 + [ANY] * ng, out_shape=lay_shapes * 3 + _gathered_shapes(gather or []),
        scratch_shapes=[STAGE] + (_gather_sems(ng) if ng else []),
        compiler_params=_cparams("arbitrary" if ng else "parallel"), name=name)(proj, proj, proj, *tabs,
                                                                              *(gather or []))
    q, k, v = (dict(zip(DILATIONS, outs[w * nlay:(w + 1) * nlay])) for w in range(3))
    return q, k, v, list(outs[3 * nlay:])


def _as_classes(t):
    return t if t.ndim == 3 else t[None]


def _head_masks():
    lane = lax.broadcasted_iota(jnp.int32, (1, LANES), 1)
    return lane < HEAD_DIM, lane >= HEAD_DIM


def _stack_heads(t):
    lo, hi = _head_masks()
    zero = jnp.zeros_like(t)
    return jnp.concatenate([jnp.where(lo, t, zero), jnp.where(hi, t, zero)], axis=0)


MAX_SEGMENT_BLOCKS = 8


def _segment_masks(j):
    qi = lax.broadcasted_iota(jnp.int32, (BAND, 2 * BAND), 0)
    kj = lax.broadcasted_iota(jnp.int32, (BAND, 2 * BAND), 1)
    both = (kj >= qi) & (kj <= qi + BAND)
    own = kj[:, :BAND] <= qi[:, :BAND]
    head = both & ((kj >= BAND) | (j > 0))
    return tuple(jnp.concatenate([m, m], axis=0) for m in (own, both, head))


def _block_rows(g):
    return pl.ds(pl.multiple_of(g * BAND, BAND), BAND)


def _key_rows(g):
    return pl.ds(pl.multiple_of((g - 1) * BAND, BAND), 2 * BAND)


def _segments(n):
    nb = n // BAND
    seg = min(nb, MAX_SEGMENT_BLOCKS)
    return seg, nb // seg


def _segment_specs(seg):
    main = pl.BlockSpec((None, seg * BAND, B_WIDTH), lambda r, j: (r, j, 0))
    halo = pl.BlockSpec((None, BAND, B_WIDTH), lambda r, j: (r, jnp.maximum(j * seg - 1, 0), 0))
    return main, halo


def _attn_fwd(q, k, v, name, gather=None):
    dil, n, _ = q.shape
    seg, nseg = _segments(n)
    nh = 2 if nseg > 1 else 0
    ng = 0 if gather is None else len(gather)

    def body(*refs):
        q_ref, k_ref, v_ref = refs[:3]
        halos = refs[3:3 + nh]
        o_ref, l_ref = refs[3 + nh + ng:5 + nh + ng]
        if ng:
            start, relay, finish = _gather_steps(refs[3 + nh:3 + nh + ng], refs[5 + nh + ng:5 + nh + 2 * ng],
                                                 *refs[5 + nh + 2 * ng:])
            first, last = _grid_edges((dil, nseg))
            pl.when(first)(start)
        own, both, head = _segment_masks(pl.program_id(1))
        lo, _ = _head_masks()

        def block(rows, keys_of, valid):
            for sl in _lane_blocks(B_WIDTH):
                kk, vv = keys_of(sl)
                sc = jnp.where(valid, _dot(_stack_heads(q_ref[rows, sl]), kk, NT), NEG_INF)
                mx = jnp.max(sc, axis=1, keepdims=True)
                p = jnp.exp(sc - mx)
                den = jnp.sum(p, axis=1, keepdims=True)
                out = _dot(p.astype(BF16), vv, NN) / den
                lse = mx + jnp.log(den)
                o_ref[rows, sl] = jnp.where(lo, out[:BAND], out[BAND:]).astype(BF16)
                l_ref[rows, sl] = jnp.where(lo, lse[:BAND], lse[BAND:])

        if nh:
            block(_block_rows(0), lambda sl: (jnp.concatenate([halos[0][:, sl], k_ref[0:BAND, sl]], axis=0),
                                              jnp.concatenate([halos[1][:, sl], v_ref[0:BAND, sl]], axis=0)), head)
        else:
            block(_block_rows(0), lambda sl: (k_ref[0:BAND, sl], v_ref[0:BAND, sl]), own)

        @pl.loop(1, seg)
        def _(g):
            block(_block_rows(g), lambda sl: (k_ref[_key_rows(g), sl], v_ref[_key_rows(g), sl]), both)

        if ng:
            @pl.when(last)
            def _():
                relay()
                finish()

    main, halo = _segment_specs(seg)
    res = pl.pallas_call(
        body, grid=(dil, nseg), in_specs=[main] * 3 + [halo] * nh + [ANY] * ng, out_specs=[main, main] + [ANY] * ng,
        out_shape=[jax.ShapeDtypeStruct((dil, n, B_WIDTH), BF16), jax.ShapeDtypeStruct((dil, n, B_WIDTH), F32)]
        + _gathered_shapes(gather or []),
        scratch_shapes=_gather_sems(ng) if ng else [],
        compiler_params=_cparams(*(["arbitrary"] * 2 if ng else ["parallel"] * 2)), name=name)(
            q, k, v, *([k, v] if nh else []), *(gather or []))
    return res[0], res[1], list(res[2:])


def _attn_bwd(q, k, v, do, lse, delta, name, scatter=None):
    dil, n, _ = q.shape
    seg, nseg = _segments(n)
    nh = 2 if nseg > 1 else 0
    ns = 0 if scatter is None else len(scatter[0])

    def body(*refs):
        q_ref, k_ref, v_ref, do_ref, lse_ref, dl_ref = refs[:6]
        halos = refs[6:6 + nh]
        dq_ref, dk_ref, dv_ref = refs[6 + nh + ns:9 + nh + ns]
        halo_out = refs[9 + nh + ns:9 + 2 * nh + ns]
        ck_ref, cv_ref = refs[9 + 2 * nh + 2 * ns:11 + 2 * nh + 2 * ns]
        if ns:
            start, finish = _scatter_steps(refs[6 + nh:6 + nh + ns], refs[9 + 2 * nh + ns:9 + 2 * nh + 2 * ns],
                                           *refs[11 + 2 * nh + 2 * ns:], scatter[1])
            first, last = _grid_edges((dil, nseg))
            pl.when(first)(start)
        own, both, head = _segment_masks(pl.program_id(1))
        lo, _ = _head_masks()
        lane = lax.broadcasted_iota(jnp.int32, (1, LANES), 1)

        def per_head(t):
            return jnp.concatenate(
                [jnp.sum(jnp.where(lane == first, t, 0.0), axis=1, keepdims=True) for first in (0, HEAD_DIM)], axis=0)

        def grads(rows, kk, vv, valid, sl):
            q2 = _stack_heads(q_ref[rows, sl])
            do2 = _stack_heads(do_ref[rows, sl])
            p = jnp.where(valid, jnp.exp(_dot(q2, kk, NT) - per_head(lse_ref[rows, sl])), 0.0)
            ds = (p * (_dot(do2, vv, NT) - per_head(dl_ref[rows, sl]))).astype(BF16)
            dq = _dot(ds, kk, NN)
            dq_ref[rows, sl] = jnp.where(lo, dq[:BAND], dq[BAND:]).astype(BF16)
            return _dot(ds, q2, TN), _dot(p.astype(BF16), do2, TN)

        for sl in _lane_blocks(B_WIDTH):
            if nh:
                dkk, dvv = grads(_block_rows(0), jnp.concatenate([halos[0][:, sl], k_ref[0:BAND, sl]], axis=0),
                                 jnp.concatenate([halos[1][:, sl], v_ref[0:BAND, sl]], axis=0), head, sl)
                halo_out[0][:, sl], halo_out[1][:, sl] = dkk[:BAND], dvv[:BAND]
                ck_ref[:, sl], cv_ref[:, sl] = dkk[BAND:], dvv[BAND:]
            else:
                ck_ref[:, sl], cv_ref[:, sl] = grads(_block_rows(0), k_ref[0:BAND, sl], v_ref[0:BAND, sl], own, sl)

        @pl.loop(1, seg)
        def _(g):
            before = _block_rows(g - 1)
            for sl in _lane_blocks(B_WIDTH):
                dkk, dvv = grads(_block_rows(g), k_ref[_key_rows(g), sl], v_ref[_key_rows(g), sl], both, sl)
                dk_ref[before, sl] = (ck_ref[:, sl] + dkk[:BAND]).astype(BF16)
                dv_ref[before, sl] = (cv_ref[:, sl] + dvv[:BAND]).astype(BF16)
                ck_ref[:, sl] = dkk[BAND:]
                cv_ref[:, sl] = dvv[BAND:]

        final = pl.ds((seg - 1) * BAND, BAND)
        dk_ref[final, :] = ck_ref[...].astype(BF16)
        dv_ref[final, :] = cv_ref[...].astype(BF16)

        if ns:
            pl.when(last)(finish)

    main, halo = _segment_specs(seg)
    shape = jax.ShapeDtypeStruct((dil, n, B_WIDTH), BF16)
    halo_shape = jax.ShapeDtypeStruct((dil, nseg, BAND, B_WIDTH), F32)
    halo_spec = pl.BlockSpec((None, None, BAND, B_WIDTH), lambda r, j: (r, j, 0, 0))
    res = pl.pallas_call(
        body, grid=(dil, nseg), in_specs=[main] * 6 + [halo] * nh + [ANY] * ns,
        out_specs=[main] * 3 + [halo_spec] * nh + [ANY] * ns,
        out_shape=[shape] * 3 + [halo_shape] * nh + (_scattered_shapes(scatter[1]) if ns else []),
        scratch_shapes=[pltpu.VMEM((BAND, B_WIDTH), F32)] * 2 + (_scatter_sems(ns) if ns else []),
        compiler_params=_cparams(*(["arbitrary"] * 2 if ns else ["parallel"] * 2)), name=name)(
            q, k, v, do, lse, delta, *([k, v] if nh else []), *(scatter[0] if ns else []))
    return res[0], res[1], res[2], (tuple(res[3:3 + nh]) if nh else None), list(res[3 + nh:])


def _attn_combine(outs, lses, gb, mixed, name, gather=None):
    s = mixed.shape[0]
    npat = len(DILATIONS)
    w = B_WIDTH
    ng = 0 if gather is None else len(gather)

    def body(*refs):
        o_refs, l_refs = refs[:npat], refs[npat:2 * npat]
        g_ref = refs[2 * npat]
        ob_ref = refs[2 * npat + 2 + ng]
        lse_refs = refs[2 * npat + 3 + ng:3 * npat + 3 + ng]
        mb_ref = refs[3 * npat + 3 + ng]
        stage = refs[3 * npat + 4 + 2 * ng]
        if ng:
            start, relay, finish = _gather_steps(refs[2 * npat + 2:2 * npat + 2 + ng],
                                                 refs[3 * npat + 4 + ng:3 * npat + 4 + 2 * ng],
                                                 *refs[3 * npat + 5 + 2 * ng:])
            first, last = _grid_edges((s // TR,))
            pl.when(first)(start)
        os_ = [o_refs[0][...].astype(F32)] + [_load_classes(r, stage, d) for r, d in zip(o_refs[1:], CLASS_DILS)]
        ls = [l_refs[0][...]] + [_load_classes(r, stage, d) for r, d in zip(l_refs[1:], CLASS_DILS)]
        mx = functools.reduce(jnp.maximum, ls)
        ws = [jnp.exp(l - mx) for l in ls]
        tot = functools.reduce(lambda a, b: a + b, ws)
        ob = functools.reduce(lambda a, b: a + b, [wt / tot * o for wt, o in zip(ws, os_)])
        ob_ref[...] = ob
        lse = mx + jnp.log(tot)
        _stage_put(stage, lse)
        lse_refs[0][...] = lse
        for ref, d in zip(lse_refs[1:], CLASS_DILS):
            _store_classes(stage, ref, d)
        mb_ref[...] = (ob * _rsq_mean(ob) * g_ref[...]).astype(BF16)

        if ng:
            @pl.when(last)
            def _():
                relay()
                finish()

    lay_specs = [_row_spec(w)] + [_class_spec(d) for d in CLASS_DILS]
    res = pl.pallas_call(
        body, grid=(s // TR,), in_specs=lay_specs * 2 + [_vec_spec(w), ANY] + [ANY] * ng,
        out_specs=[_row_spec(w)] + lay_specs + [_row_spec(w, 1)] + [ANY] * ng,
        out_shape=[jax.ShapeDtypeStruct((s, w), F32), jax.ShapeDtypeStruct((s, w), F32)]
        + [_class_shape(s, d, F32) for d in CLASS_DILS] + [jax.ShapeDtypeStruct(mixed.shape, mixed.dtype)]
        + _gathered_shapes(gather or []),
        scratch_shapes=[STAGE] + (_gather_sems(ng) if ng else []), input_output_aliases={2 * npat + 1: npat + 1},
        compiler_params=_cparams("arbitrary" if ng else "parallel"), name=name)(*outs, *lses, gb, mixed,
                                                                              *(gather or []))
    return res[0], dict(zip(DILATIONS, res[1:npat + 1])), res[npat + 1], list(res[npat + 2:])


def _attn_bwd_prep(dmixed, ob, gb, name):
    s = ob.shape[0]
    w = B_WIDTH
    nlay = len(DILATIONS)

    def body(dm_ref, ob_ref, g_ref, *rest):
        do_refs, dl_refs = rest[:nlay], rest[nlay:2 * nlay]
        dg_ref, stage = rest[2 * nlay:]
        _acc_init([dg_ref])
        ob = ob_ref[...]
        dob, dgt = _rms_bwd(ob, _rsq_mean(ob), g_ref[...], dm_ref[...])
        dg_ref[...] += _colsum(dgt)
        _stage_put(stage, dob)
        do_refs[0][...] = dob.astype(BF16)
        for ref, d in zip(do_refs[1:], CLASS_DILS):
            _store_classes(stage, ref, d)
        lo, hi = _head_masks()
        t = dob * ob
        for b, sl in enumerate(_lane_blocks(w)):
            tb = t[:, sl]
            s0 = jnp.sum(jnp.where(lo, tb, 0.0), axis=1, keepdims=True)
            s1 = jnp.sum(jnp.where(hi, tb, 0.0), axis=1, keepdims=True)
            stage[b] = jnp.where(lo, s0, s1)
        dl_refs[0][...] = _stage_get(stage)
        for ref, d in zip(dl_refs[1:], CLASS_DILS):
            _store_classes(stage, ref, d)

    lay_specs = [_row_spec(w)] + [_class_spec(d) for d in CLASS_DILS]
    shapes = lambda dt: [jax.ShapeDtypeStruct((s, w), dt)] + [_class_shape(s, d, dt) for d in CLASS_DILS]
    res = pl.pallas_call(
        body, grid=(s // TR,), in_specs=[_row_spec(w, 1), _row_spec(w), _vec_spec(w)],
        out_specs=lay_specs * 2 + [_vec_spec(w)],
        out_shape=shapes(BF16) + shapes(F32) + [jax.ShapeDtypeStruct((1, w), F32)],
        scratch_shapes=[STAGE],
        compiler_params=_cparams("arbitrary"), name=name)(dmixed, ob, gb)
    return dict(zip(DILATIONS, res[:nlay])), dict(zip(DILATIONS, res[nlay:2 * nlay])), res[2 * nlay]


def _rope_bwd(dqs, dks, dvs, halos, tabs, dproj, name):
    s = dproj.shape[0]
    half = ROT_DIM // 2
    scale = HEAD_DIM ** -0.5
    npat = len(DILATIONS)
    w = B_WIDTH
    nseg = halos[0].shape[0]
    per = s // nseg // TR

    def body(*refs):
        groups = [refs[g * npat:(g + 1) * npat] for g in range(3)]
        halo_refs = (None,) + tuple(refs[3 * npat:3 * npat + 2])
        c_ref, s1_ref, s2_ref, _, o_ref, stage = refs[3 * npat + 2:]
        i = pl.program_id(0)
        at_edge = ((i + 1) % per == 0) & ((i + 1) // per < nseg)

        def total(rs, halo_ref=None):
            acc = rs[0][...].astype(F32)
            if halo_ref is not None:
                edge = jnp.concatenate([jnp.zeros((TR - BAND, w), F32), halo_ref[...]], axis=0)
                acc = acc + jnp.where(at_edge, edge, 0.0)
            for ref, d in zip(rs[1:], CLASS_DILS):
                acc = acc + _load_classes(ref, stage, d)
            return acc

        def unrope(g):
            c, s1, s2 = c_ref[...], s1_ref[...], s2_ref[...]
            for sl in _lane_blocks(w):
                gb = g[:, sl]
                o = gb * c + pltpu.roll(gb * s1, half, 1) + pltpu.roll(gb * s2, LANES - half, 1)
                o_ref[:, sl] = o.astype(BF16)

        which = pl.program_id(1)

        @pl.when(which == 0)
        def _():
            unrope(total(groups[0]) * scale)

        @pl.when(which == 1)
        def _():
            unrope(total(groups[1], halo_refs[1]))

        @pl.when(which == 2)
        def _():
            o_ref[...] = total(groups[2], halo_refs[2]).astype(BF16)

    tab = pl.BlockSpec((TR, LANES), lambda i, j: (i, 0))
    nat = pl.BlockSpec((TR, w), lambda i, j: (i, 0))
    lay_specs = [nat] + [_class_spec(d) for d in CLASS_DILS]
    edge_spec = pl.BlockSpec((None, BAND, w), lambda i, j: (jnp.minimum((i + 1) // per, nseg - 1), 0, 0))
    first_col = 2 * A_WIDTH // w
    return pl.pallas_call(
        body, grid=(s // TR, 3), in_specs=lay_specs * 3 + [edge_spec] * 2 + [tab] * 3 + [ANY],
        out_specs=pl.BlockSpec((TR, w), lambda i, j: (i, first_col + j)),
        out_shape=jax.ShapeDtypeStruct(dproj.shape, dproj.dtype), scratch_shapes=[STAGE],
        input_output_aliases={3 * npat + 5: 0},
        compiler_params=_cparams("parallel", "arbitrary"), name=name)(*dqs, *dks, *dvs, *halos, *tabs, dproj)


TK = 512
HALO = 16
FFN_ROWS = 256
FFN_CHUNKS = tuple(slice(r, r + FFN_ROWS) for r in range(0, TM, FFN_ROWS))


def _row_of(v, r):
    rows = lax.broadcasted_iota(jnp.int32, (v.shape[0], 1), 0)
    return jnp.sum(jnp.where(rows == r, v, 0.0), axis=0, keepdims=True)


def _taps_before(x, halo):
    row = lax.broadcasted_iota(jnp.int32, (x.shape[0], 1), 0)
    m1 = jnp.where(row == 0, _row_of(halo, HALO - 1), pltpu.roll(x, 1, 0))
    m2 = jnp.where(row == 0, _row_of(halo, HALO - 2), jnp.where(row == 1, _row_of(halo, HALO - 1), pltpu.roll(x, 2, 0)))
    return m2, m1, x


def _taps_after(x, halo):
    rows = x.shape[0]
    row = lax.broadcasted_iota(jnp.int32, (rows, 1), 0)
    p1 = jnp.where(row == rows - 1, _row_of(halo, 0), pltpu.roll(x, rows - 1, 0))
    p2 = jnp.where(row == rows - 2, _row_of(halo, 0), jnp.where(row == rows - 1, _row_of(halo, 1), pltpu.roll(x, rows - 2, 0)))
    return p1, p2


def _conv_value(taps, cw_ref, cb_ref, h):
    return cb_ref[h] + cw_ref[h, 0:1, :] * taps[0] + cw_ref[h, 1:2, :] * taps[1] + cw_ref[h, 2:3, :] * taps[2]


def _ffn_weight_specs(ncol):
    per_up = (2 * D_FF // N_CHIPS) // TK
    per_dn = (D_FF // N_CHIPS) // TK
    wg = pl.BlockSpec((None, None, D_MODEL, TK), lambda i, j: (j // per_up, 0, 0, j % per_up))
    wv = pl.BlockSpec((None, None, D_MODEL, TK), lambda i, j: ((j + ncol) // per_up, 0, 0, (j + ncol) % per_up))
    wd = pl.BlockSpec((None, None, TK, D_MODEL), lambda i, j: (j // per_dn, 0, j % per_dn, 0))
    cw = pl.BlockSpec((2, 3, TK), lambda i, j: (0, 0, j))
    cb = pl.BlockSpec((2, 1, TK), lambda i, j: (0, 0, j))
    return wg, wv, wd, cw, cb


def _ffn_forward(h2, w_up, w_down, cw3, cb3, name, gather=None, post=None):
    s = h2.shape[0]
    nm, ncol = s // TM, D_FF // TK
    ng = 0 if gather is None else len(gather)
    npost = 0 if post is None else 3
    nout = 4 + (2 if post else 0)

    def body(*refs):
        h_ref, wg_ref, wv_ref, wd_ref, cw_ref, cb_ref = refs[:6]
        post_in = refs[6:6 + npost]
        g_in = refs[6 + npost:6 + npost + ng]
        outs = refs[6 + npost + ng:6 + npost + ng + nout]
        y_ref, up_ref, cv_ref, f_ref = outs[:4]
        g_out = refs[6 + npost + ng + nout:6 + npost + 2 * ng + nout]
        carry = refs[6 + npost + 2 * ng + nout]
        i, j = pl.program_id(0), pl.program_id(1)
        if ng:
            start, relay, finish = _gather_steps(g_in, g_out, *refs[7 + npost + 2 * ng + nout:])
            pl.when((i == 0) & (j == 0))(start)
            pl.when((i == nm - 1) & (j == 0))(relay)

        @pl.when((i == 0) & (j == 0))
        def _():
            carry[...] = jnp.zeros_like(carry)

        @pl.when(j == 0)
        def _():
            f_ref[...] = jnp.zeros_like(f_ref)

        ups = []
        for rs in FFN_CHUNKS:
            hc = h_ref[rs, :]
            ups.append([_dot(hc, w_ref[...], NN).astype(BF16) for w_ref in (wg_ref, wv_ref)])
            for hh in range(2):
                up_ref[hh, rs, :] = ups[-1][hh]
        before = [carry[j, hh] for hh in range(2)]
        for rs, up in zip(FFN_CHUNKS, ups):
            conv = []
            for hh in range(2):
                x = up[hh].astype(F32)
                conv.append(_conv_value(_taps_before(x, before[hh]), cw_ref, cb_ref, hh))
                cv_ref[hh, rs, :] = conv[hh].astype(BF16)
                before[hh] = x[x.shape[0] - HALO:, :]
            y = (_gelu_tanh(conv[0])[0] * conv[1]).astype(BF16)
            y_ref[rs, :] = y
            f_ref[rs, :] += _dot(y, wd_ref[...], NN)
        for hh in range(2):
            carry[j, hh] = before[hh]

        @pl.when(j == ncol - 1)
        def _():
            if post:
                f = f_ref[...]
                x1_ref, gp_ref, gn_ref = post_in
                x2 = x1_ref[...] + f * _rsq_mean(f) * gp_ref[...]
                outs[4][...] = x2
                outs[5][...] = (x2 * _rsq_mean(x2) * gn_ref[...]).astype(BF16)

        if ng:
            pl.when((i == nm - 1) & (j == ncol - 1))(finish)

    wg, wv, wd, cw, cb = _ffn_weight_specs(ncol)
    row = pl.BlockSpec((TM, D_MODEL), lambda i, j: (i, 0))
    vec = pl.BlockSpec((1, D_MODEL), lambda i, j: (0, 0))
    res = pl.pallas_call(
        body, grid=(nm, ncol),
        in_specs=[row, wg, wv, wd, cw, cb] + ([row, vec, vec] if post else []) + [ANY] * ng,
        out_specs=[pl.BlockSpec((TM, TK), lambda i, j: (i, j)), pl.BlockSpec((2, TM, TK), lambda i, j: (0, i, j)),
                   pl.BlockSpec((2, TM, TK), lambda i, j: (0, i, j)), row] + ([row, row] if post else [])
        + [ANY] * ng,
        out_shape=[jax.ShapeDtypeStruct((s, D_FF), BF16), jax.ShapeDtypeStruct((2, s, D_FF), BF16),
                   jax.ShapeDtypeStruct((2, s, D_FF), BF16), jax.ShapeDtypeStruct((s, D_MODEL), F32)]
        + ([jax.ShapeDtypeStruct((s, D_MODEL), F32), jax.ShapeDtypeStruct((s, D_MODEL), BF16)] if post else [])
        + _gathered_shapes(gather or []),
        scratch_shapes=[pltpu.VMEM((ncol, 2, HALO, TK), F32)] + (_gather_sems(ng) if ng else []),
        compiler_params=_cparams("arbitrary", "arbitrary"), name=name)(h2, w_up, w_up, w_down, cw3, cb3,
                                                                      *(post or []), *(gather or []))
    return res[:nout], list(res[nout:])


def _ffn_backward(df, w_up, w_down, up3, cv3, cw3, name, scatter=None):
    s = df.shape[0]
    nm, ncol = s // TM, D_FF // TK
    ns = 0 if scatter is None else len(scatter[0])

    def body(*refs):
        df_ref, wg_ref, wv_ref, wd_ref, cw_ref, up_ref, cv_ref = refs[:7]
        s_in = refs[7:7 + ns]
        dup_ref, dh_ref, sums_ref = refs[7 + ns:10 + ns]
        s_out = refs[10 + ns:10 + 2 * ns]
        carry = refs[10 + 2 * ns]
        i, j = pl.program_id(0), pl.program_id(1)
        if ns:
            start, finish = _scatter_steps(s_in, s_out, *refs[11 + 2 * ns:], scatter[1])
            pl.when((i == 0) & (j == 0))(start)

        @pl.when((i == 0) & (j == 0))
        def _():
            carry[...] = jnp.zeros_like(carry)
            sums_ref[...] = jnp.zeros_like(sums_ref)

        @pl.when(j == 0)
        def _():
            dh_ref[...] = jnp.zeros_like(dh_ref)

        chunks = FFN_CHUNKS[::-1]
        dys = [_dot(df_ref[rs, :], wd_ref[...], NT) for rs in chunks]
        row = lax.broadcasted_iota(jnp.int32, (8, 1), 0)
        after = [carry[j, hh] for hh in range(2)]
        upd = [jnp.zeros((8, TK), F32) for _ in range(2)]
        for rs, dy in zip(chunks, dys):
            act, grad = _gelu_tanh(cv_ref[0, rs, :].astype(F32))
            dcs = (dy * cv_ref[1, rs, :].astype(F32) * grad, dy * act)
            part = dh_ref[rs, :]
            for hh, w_ref in ((0, wg_ref), (1, wv_ref)):
                dc = dcs[hh]
                x = up_ref[hh, rs, :].astype(F32)
                after1, after2 = _taps_after(dc, after[hh])
                for ridx, sm in enumerate((_colsum(after2 * x), _colsum(after1 * x), _colsum(dc * x), _colsum(dc))):
                    upd[hh] = upd[hh] + jnp.where(row == ridx, sm, 0.0)
                dup = (cw_ref[hh, 2:3, :] * dc + cw_ref[hh, 1:2, :] * after1 + cw_ref[hh, 0:1, :] * after2).astype(BF16)
                after[hh] = dc[:HALO, :]
                dup_ref[hh, rs, :] = dup
                part = part + _dot(dup, w_ref[...], NT)
            dh_ref[rs, :] = part
        for hh in range(2):
            sums_ref[j, hh] += upd[hh]
            carry[j, hh] = after[hh]

        if ns:
            pl.when((i == nm - 1) & (j == ncol - 1))(finish)

    wg, wv, wd, cw, _ = _ffn_weight_specs(ncol)
    rev = lambda i: nm - 1 - i
    res = pl.pallas_call(
        body, grid=(nm, ncol),
        in_specs=[pl.BlockSpec((TM, D_MODEL), lambda i, j: (rev(i), 0)), wg, wv, wd, cw,
                  pl.BlockSpec((2, TM, TK), lambda i, j: (0, rev(i), j)),
                  pl.BlockSpec((2, TM, TK), lambda i, j: (0, rev(i), j))] + [ANY] * ns,
        out_specs=[pl.BlockSpec((2, TM, TK), lambda i, j: (0, rev(i), j)),
                   pl.BlockSpec((TM, D_MODEL), lambda i, j: (rev(i), 0)),
                   pl.BlockSpec((ncol, 2, 8, TK), lambda i, j: (0, 0, 0, 0))] + [ANY] * ns,
        out_shape=[jax.ShapeDtypeStruct((2, s, D_FF), BF16), jax.ShapeDtypeStruct((s, D_MODEL), F32),
                   jax.ShapeDtypeStruct((ncol, 2, 8, TK), F32)] + (_scattered_shapes(scatter[1]) if ns else []),
        scratch_shapes=[pltpu.VMEM((ncol, 2, HALO, TK), F32)] + (_scatter_sems(ns) if ns else []),
        compiler_params=_cparams("arbitrary", "arbitrary"), name=name)(df, w_up, w_up, w_down, cw3, up3, cv3,
                                                                      *(scatter[0] if ns else []))
    return res[:3], list(res[3:])


def _wspec(rows, cols, index_map):
    return pl.BlockSpec((None, None, rows, cols), index_map)


def _layer_forward(l, x0, h1, p, wg, tabs, gather=None, late=None, g_next=None):
    s = x0.shape[0]
    nm = s // TMM
    tag = f"_l{l}"
    riders = dict.fromkeys(DILATIONS)
    proj_rider = rope_rider = combine_rider = None
    if late is not None:
        cols = lambda t, parts: [t[:, i * t.shape[1] // parts:(i + 1) * t.shape[1] // parts] for i in range(parts)]
        (down_a, down_b), up_q = cols(late["w_down"], 2), cols(late["w_up"], 4)
        proj_rider, rope_rider, combine_rider = [late["w_out"], down_a], [up_q[2]], [up_q[3]]
        riders = dict(zip(DILATIONS, ([down_b], [up_q[0]], [up_q[1]])))
    proj = _matmul(
        h1, wg["w_in"], grid=(nm, N_CHIPS), a_spec=pl.BlockSpec((TMM, D_MODEL), lambda i, j: (i, 0)),
        b_spec=_wspec(D_MODEL, IN_COLS // N_CHIPS, lambda i, j: (j, 0, 0, 0)),
        o_spec=pl.BlockSpec((TMM, IN_COLS // N_CHIPS), lambda i, j: (i, j)), o_shape=(s, IN_COLS), o_dtype=BF16,
        dims=NN, nk=1, kaxis=None, acc_shape=None, name="proj" + tag, gather=proj_rider)
    if late is not None:
        proj, (w_out_all4, down_a) = proj
    ma, next_out = _mixer_a_fwd(proj, p["v_norm_g"], p["v_norm_b"], p["w_spatial"], p["bs_full"], p["out_norm_a"],
                                "mixer_a_fwd" + tag, [gather["w_out"]] if gather else None)
    q, k, v, rope_landed = _rope_fwd(proj, tabs, "rope_fwd" + tag, rope_rider)
    outs, lses, landed = zip(*[
        _attn_fwd(_as_classes(q[d]), _as_classes(k[d]), _as_classes(v[d]), f"attn_fwd_d{d}" + tag, riders[d])
        for d in DILATIONS])
    outs = [o.reshape(s, B_WIDTH) if d == 1 else o for o, d in zip(outs, DILATIONS)]
    lses = [t.reshape(s, B_WIDTH) if d == 1 else t for t, d in zip(lses, DILATIONS)]
    ob, lse, mixed, combine_landed = _attn_combine(outs, lses, p["out_norm_b"], ma, "attn_combine" + tag,
                                                   combine_rider)
    if late is not None:
        wg = dict(wg, w_out=w_out_all4, w_down=jnp.concatenate([down_a, landed[0][0]], axis=-1),
                  w_up=jnp.concatenate([landed[1][0], landed[2][0], rope_landed[0], combine_landed[0]], axis=-1))
    (y1, x1, h2), next_in = _mix_out_norm(mixed, wg["w_out"], x0, p["post_mix_norm"], p["pre_ffn_norm"],
                                          "mix_out" + tag, [gather["w_in"]] if gather else None)
    post = None if g_next is None else (x1, p["post_ffn_norm"], g_next)
    (y, up3, cv3, f, *after), next_ffn = _ffn_forward(h2, wg["w_up"], wg["w_down"], p["cw3"], p["cb3"], "ffn_fwd" + tag,
                                                      [gather["w_up"], gather["w_down"]] if gather else None, post)
    gathered = dict(w_in=next_in[0], w_out=next_out[0], w_up=next_ffn[0], w_down=next_ffn[1]) if gather else None
    saved = dict(x0=x0, h1=h1, proj=proj, q=q, k=k, v=v, ob=ob, lse=lse, mixed=mixed, y1=y1, x1=x1, h2=h2,
                 up3=up3, cv3=cv3, y=y, f=f)
    if after:
        saved.update(x2=after[0], h_next=after[1])
    return saved, gathered, wg


def _layer_backward(l, dx2, df, sv, p, wg, tabs, pos, scatter=None, hide=False):
    s = dx2.shape[0]
    nm = s // TMM
    tag = f"_l{l}"
    g = {}
    (dup3, dh2, conv_sums), scattered = _ffn_backward(df, wg["w_up"], wg["w_down"], sv["up3"], sv["cv3"], p["cw3"],
                                                      "ffn_bwd" + tag, scatter)
    sums = conv_sums.transpose(1, 2, 0, 3).reshape(2, 8, D_FF)
    g["conv_w"] = jnp.concatenate([sums[0, :3], sums[1, :3]], axis=1)
    g["conv_b"] = jnp.concatenate([sums[0, 3:4], sums[1, 3:4]], axis=1)
    tn = 1024
    done = {}
    gw_down = _matmul(
        sv["y"], df, grid=(D_FF // tn,), a_spec=pl.BlockSpec((s, tn), lambda k: (0, k)),
        b_spec=pl.BlockSpec((s, D_MODEL), lambda k: (0, 0)),
        o_spec=pl.BlockSpec((2, tn, D_MODEL // 2), lambda k: (0, k, 0)),
        o_shape=(2, D_FF, D_MODEL // 2), o_dtype=BF16,
        dims=TN, nk=1, kaxis=None, acc_shape=None, name="w_down_grad" + tag, halves=True)
    pair_sum = lambda n, grad, recv: _pair_sum({n: grad}, recv, pos, (n,), f"pair_sum_l{l}")
    gw_up, received = _matmul(
        sv["h2"], dup3, grid=(2 * D_FF // tn,), a_spec=pl.BlockSpec((s, D_MODEL), lambda n: (0, 0)),
        b_spec=pl.BlockSpec((None, s, tn), lambda n: (n // (D_FF // tn), 0, n % (D_FF // tn))),
        o_spec=pl.BlockSpec((None, D_MODEL, tn), lambda n: (n // 2, 0, n % 2)),
        o_shape=(N_CHIPS, D_MODEL, 2 * D_FF // N_CHIPS), o_dtype=BF16,
        dims=TN, nk=1, kaxis=None, acc_shape=None, name="w_up_grad" + tag, scatter=([gw_down], ("x:w_down",)))
    down_sums = pair_sum("w_down", gw_down, received)
    dx1, dy1, g["pre_ffn_norm"], g["post_mix_norm"] = _norm_bwd_mid(
        dx2, dh2, sv["x1"], p["pre_ffn_norm"], sv["y1"], p["post_mix_norm"], "norm_bwd_mid" + tag)
    w_out_all = pl.BlockSpec((N_CHIPS, None, D_MODEL // N_CHIPS, D_MODEL), lambda i: (0, 0, 0, 0))
    dmixed, received = _matmul(
        dy1, wg["w_out"], grid=(nm,), a_spec=pl.BlockSpec((TMM, D_MODEL), lambda i: (i, 0)), b_spec=w_out_all,
        o_spec=pl.BlockSpec((TMM, D_MODEL), lambda i: (i, 0)), o_shape=(s, D_MODEL), o_dtype=F32,
        dims=NT, nk=1, kaxis=None, acc_shape=None, name="mix_out_bwd" + tag, b_2d=(D_MODEL, D_MODEL),
        scatter=([gw_up], ("x:w_up",)))
    up_sums = pair_sum("w_up", gw_up, received)
    gw_out = _matmul(
        sv["mixed"], dy1, grid=(1,), a_spec=pl.BlockSpec((s, D_MODEL), lambda m: (0, 0)),
        b_spec=pl.BlockSpec((s, D_MODEL), lambda m: (0, 0)),
        o_spec=pl.BlockSpec((2, D_MODEL, D_MODEL // 2), lambda m: (0, 0, 0)),
        o_shape=(2, D_MODEL, D_MODEL // 2), o_dtype=BF16,
        dims=TN, nk=1, kaxis=None, acc_shape=None, name="w_out_grad" + tag, halves=True)
    dpa, g["out_norm_a"], g["v_norm_g"], g["v_norm_b"], dbs, g["w_spatial"], received = _mixer_a_bwd(
        sv["proj"], dmixed, p["v_norm_g"], p["v_norm_b"], p["w_spatial"], p["bs_full"], p["out_norm_a"],
        "mixer_a_bwd" + tag, ([gw_out], ("x:w_out",)))
    out_sums = pair_sum("w_out", gw_out, received)
    g["b_spatial"] = dbs[:, ::GROUP_DIM].T
    dob, delta, g["out_norm_b"] = _attn_bwd_prep(dmixed, sv["ob"], p["out_norm_b"], "attn_bwd_prep" + tag)
    riders = dict(zip(DILATIONS, ((down_sums, ("w_down",)), (up_sums, ("w_up:0",)), (up_sums, ("w_up:1",))))) if hide else {}
    dqs, dks, dvs, edges, received = zip(*[
        _attn_bwd(*(_as_classes(t[d]) for t in (sv["q"], sv["k"], sv["v"], dob, sv["lse"], delta)),
                  f"attn_bwd_d{d}" + tag, riders.get(d))
        for d in DILATIONS])
    if hide:
        done[("w_down",)] = (down_sums, received[0])
        done[("w_up",)] = (up_sums, [jnp.concatenate([received[1][0], received[2][0]], axis=-1)])
    nat = lambda ts: [t.reshape(s, B_WIDTH) if d == 1 else t for t, d in zip(ts, DILATIONS)]
    halos = [t[0] for t in edges[0]]
    dproj = _rope_bwd(nat(dqs), nat(dks), nat(dvs), halos, tabs, dpa, "rope_bwd" + tag)
    wcol = IN_COLS // N_CHIPS
    gw_in = _matmul(
        sv["h1"], dproj, grid=(N_CHIPS,), a_spec=pl.BlockSpec((s, D_MODEL), lambda n: (0, 0)),
        b_spec=pl.BlockSpec((s, wcol), lambda n: (0, n)),
        o_spec=pl.BlockSpec((None, D_MODEL, wcol), lambda n: (n, 0, 0)),
        o_shape=(N_CHIPS, D_MODEL, wcol), o_dtype=BF16,
        dims=TN, nk=1, kaxis=None, acc_shape=None, name="w_in_grad" + tag,
        scatter=(out_sums, ("w_out",)) if hide else None)
    if hide:
        gw_in, received = gw_in
        done[("w_out",)] = (out_sums, received)
        in_sums = _chip_sums(l, dict(w_in=gw_in), pos, ("w_in",))
        dh1, received = _proj_bwd(dproj, wg["w_in"], "proj_bwd" + tag, (in_sums, ("w_in",)))
        done[("w_in",)] = (in_sums, received)
        return dx1, dh1, {}, g, scattered, done
    dh1, received = _proj_bwd(dproj, wg["w_in"], "proj_bwd" + tag, ([gw_in], ("x:w_in",)))
    sums = dict(w_in=pair_sum("w_in", gw_in, received)[0], w_up=up_sums[0], w_out=out_sums[0], w_down=down_sums[0])
    return dx1, dh1, sums, g, scattered, done


SMALL = ("pre_mix_norm", "v_norm_g", "v_norm_b", "w_spatial", "b_spatial", "out_norm_a", "out_norm_b",
         "post_mix_norm", "pre_ffn_norm", "conv_b", "post_ffn_norm")
BIG = ("w_in", "w_out", "w_up", "w_down")
DEPTH = 2


def _layer_params(l, small, conv_w_full):
    p = {n: small[n][l].reshape(1, -1) for n in SMALL if n not in ("w_spatial", "b_spatial")}
    p["w_spatial"] = small["w_spatial"][l]
    p["bs_full"] = jnp.repeat(small["b_spatial"][l].T, GROUP_DIM, axis=1)
    p["cw3"] = conv_w_full[l].reshape(3, 2, D_FF).transpose(1, 0, 2)
    p["cb3"] = small["conv_b"][l].reshape(2, 1, D_FF)
    return p


def _mesh_pos():
    return lax.axis_index("x"), lax.axis_index("y"), lax.axis_index("c")


def _other_chips(x, y):
    return [(1 - x, y), (x, 1 - y), (1 - x, 1 - y)]


def _gathered_shapes(blocks):
    return [jax.ShapeDtypeStruct((N_CHIPS, 1) + a.shape, a.dtype) for a in blocks]


def _gather_sems(nw):
    n = 2 * nw * (N_CHIPS - 1) + nw
    return [pltpu.SemaphoreType.DMA((n,)), pltpu.SemaphoreType.DMA((n,))]


def _gather_steps(ins, outs, send, recv):
    nw, nrel = len(ins), N_CHIPS - 1
    x, y, c = _mesh_pos()
    mine, sibling, chips = 2 * x + y, (x, y, 1 - c), _other_chips(x, y)

    def copy(src, dst, slot, to):
        return pltpu.make_async_remote_copy(src_ref=src, dst_ref=dst, send_sem=send.at[slot],
                                            recv_sem=recv.at[slot], device_id=to, device_id_type=MESH)

    def half_rows(t, core):
        rows = ins[t].shape[0] // 2
        return pl.ds(pl.multiple_of(core * rows, rows), rows)

    def landing(t, chip, core):
        return outs[t].at[chip, 0, half_rows(t, core), :]

    slots = [(t, r, chip) for t in range(nw) for r, chip in enumerate(chips)]
    own = [copy(ins[t], outs[t].at[mine, 0], 2 * nw * nrel + t, sibling) for t in range(nw)]
    first = [copy(ins[t].at[half_rows(t, c), :], landing(t, mine, c), t * nrel + r, (px, py, c))
             for t, r, (px, py) in slots]
    relays = [copy(landing(t, 2 * px + py, c), landing(t, 2 * px + py, c), nw * nrel + t * nrel + r, sibling)
              for t, r, (px, py) in slots]

    def start():
        for cp in own + first:
            cp.start()

    def relay():
        for (t, r, (px, py)), cp in zip(slots, relays):
            copy(landing(t, 2 * px + py, c), landing(t, 2 * px + py, c), t * nrel + r, (px, py, c)).wait_recv()
            cp.start()

    def finish():
        for t, r, (px, py) in slots:
            passed = landing(t, 2 * px + py, 1 - c)
            copy(passed, passed, nw * nrel + t * nrel + r, sibling).wait_recv()
        for cp in first + relays:
            cp.wait_send()
        for cp in own:
            cp.wait()

    return start, relay, finish


HALF = 512

GRAD_GEOM = {"w_in": ("rows", D_MODEL, IN_COLS // N_CHIPS), "w_up": ("rows", D_MODEL, 2 * D_FF // N_CHIPS),
             "w_out": ("cols", D_MODEL, D_MODEL // N_CHIPS), "w_down": ("cols", D_FF, D_FF // N_CHIPS)}


def _exchange_shape(n):
    kind, a, b = GRAD_GEOM[n]
    return (N_CHIPS, HALF, b) if kind == "rows" else (a, HALF)


def _piece_shape(n):
    name, _, part = n.partition(":")
    kind, _, b = GRAD_GEOM[name]
    if part:
        assert kind == "rows"
        return (HALF, b // 2)
    return (HALF, b) if kind == "rows" else (b, HALF)


def _half_of(ref, n, core):
    if GRAD_GEOM[n][0] == "rows":
        return ref.at[:, pl.ds(pl.multiple_of(core * HALF, HALF), HALF), :]
    return ref.at[core]


def _piece_of(ref, n, chip):
    name, _, part = n.partition(":")
    kind, _, b = GRAD_GEOM[name]
    if part:
        return ref.at[chip, :, pl.ds(int(part) * (b // 2), b // 2)]
    return ref.at[chip] if kind == "rows" else ref.at[pl.ds(pl.multiple_of(chip * b, b), b), :]


def _pair_exchange(g, names, name):
    n = len(names)

    def body(*refs):
        send, recv = refs[2 * n:]
        x, y, c = _mesh_pos()
        o = 1 - c
        cps = [pltpu.make_async_remote_copy(src_ref=_half_of(refs[t], nm, o), dst_ref=refs[n + t], send_sem=send.at[t],
                                            recv_sem=recv.at[t], device_id=(x, y, o), device_id_type=MESH)
               for t, nm in enumerate(names)]
        for cp in cps:
            cp.start()
        for cp in cps:
            cp.wait()

    return pl.pallas_call(
        body, in_specs=[ANY] * n, out_specs=[ANY] * n,
        out_shape=[jax.ShapeDtypeStruct(_exchange_shape(nm), BF16) for nm in names],
        scratch_shapes=[pltpu.SemaphoreType.DMA((n,)), pltpu.SemaphoreType.DMA((n,))],
        name=name)(*[g[nm] for nm in names])


def _pair_sum(g, recv, pos, names, name_prefix):
    def add(a, b, grid, a_spec, b_spec, name):
        def body(pos_ref, a_ref, b_ref, o_ref):
            o_ref[...] = (a_ref[...].astype(F32) + b_ref[...].astype(F32)).astype(BF16)

        return pl.pallas_call(
            body, grid_spec=pltpu.PrefetchScalarGridSpec(
                num_scalar_prefetch=1, grid=grid, in_specs=[a_spec, b_spec], out_specs=b_spec),
            out_shape=jax.ShapeDtypeStruct(b.shape, BF16), compiler_params=_cparams("parallel"), name=name)(pos, a, b)

    out = []
    for nm, r in zip(names, recv):
        kind, rows, width = GRAD_GEOM[nm]
        if kind == "rows":
            out.append(add(g[nm], r, (N_CHIPS,), pl.BlockSpec((None, HALF, width), lambda j, pos: (j, pos[2], 0)),
                           pl.BlockSpec((None, HALF, width), lambda j, pos: (j, 0, 0)), f"{name_prefix}_{nm}"))
        else:
            out.append(add(g[nm], r, (rows // D_MODEL,), pl.BlockSpec((None, D_MODEL, HALF), lambda j, pos: (pos[2], j, 0)),
                           pl.BlockSpec((D_MODEL, HALF), lambda j, pos: (j, 0)), f"{name_prefix}_{nm}"))
    return out


def _scattered_shapes(names):
    return [jax.ShapeDtypeStruct(_exchange_shape(nm[2:]) if nm.startswith("x:") else (N_CHIPS - 1,) + _piece_shape(nm),
                                 BF16) for nm in names]


def _scatter_sems(n):
    return [pltpu.SemaphoreType.DMA((n * (N_CHIPS - 1),)), pltpu.SemaphoreType.DMA((n * (N_CHIPS - 1),))]


def _scatter_steps(sums, outs, send, recv, names):
    nrel = N_CHIPS - 1
    x, y, c = _mesh_pos()
    cps = [pltpu.make_async_remote_copy(
        src_ref=_half_of(sums[t], nm[2:], 1 - c), dst_ref=outs[t], send_sem=send.at[t * nrel],
        recv_sem=recv.at[t * nrel], device_id=(x, y, 1 - c), device_id_type=MESH)
        for t, nm in enumerate(names) if nm.startswith("x:")]
    for r, (px, py) in enumerate(_other_chips(x, y)):
        for t, nm in enumerate(names):
            if nm.startswith("x:"):
                continue
            cps.append(pltpu.make_async_remote_copy(
                src_ref=_piece_of(sums[t], nm, 2 * px + py), dst_ref=outs[t].at[r], send_sem=send.at[t * nrel + r],
                recv_sem=recv.at[t * nrel + r], device_id=(px, py, c), device_id_type=MESH))

    def start():
        for cp in cps:
            cp.start()

    def finish():
        for cp in cps:
            cp.wait()

    return start, finish


def _chip_scatter(sums, names, name):
    n = len(names)

    def body(*refs):
        start, finish = _scatter_steps(refs[:n], refs[n:2 * n], *refs[2 * n:], names)
        start()
        finish()

    return pl.pallas_call(
        body, in_specs=[ANY] * n, out_specs=[ANY] * n, out_shape=_scattered_shapes(names),
        scratch_shapes=_scatter_sems(n), name=name)(*sums)


def _chip_sum(sums, recv, pos, names, name_prefix):
    def add(a, b, a_spec, shape, name):
        def body(pos_ref, a_ref, b_ref, o_ref):
            tot = a_ref[...].astype(F32)
            for r in range(N_CHIPS - 1):
                tot = tot + b_ref[r].astype(F32)
            o_ref[...] = tot

        return pl.pallas_call(
            body, grid_spec=pltpu.PrefetchScalarGridSpec(
                num_scalar_prefetch=1, grid=(1,), in_specs=[a_spec, pl.BlockSpec(b.shape, lambda i, pos: (0, 0, 0))],
                out_specs=pl.BlockSpec((None,) + shape, lambda i, pos: (pos[2], 0, 0))),
            out_shape=jax.ShapeDtypeStruct((2,) + shape, F32), compiler_params=_cparams("arbitrary"),
            name=name)(pos, a, b)

    chip = lambda pos: 2 * pos[0] + pos[1]
    out = []
    for nm, a, b in zip(names, sums, recv):
        shape = _piece_shape(nm)
        if GRAD_GEOM[nm][0] == "rows":
            spec = pl.BlockSpec((None,) + shape, lambda i, pos: (chip(pos), 0, 0))
        else:
            spec = pl.BlockSpec(shape, lambda i, pos: (chip(pos), 0))
        out.append(add(a, b, spec, shape, f"{name_prefix}_{nm}"))
    return out


def _pair_share(totals, name):
    n = len(totals)

    def body(*refs):
        ins, outs = refs[:n], refs[n:2 * n]
        send, recv = refs[2 * n:]
        x, y, c = _mesh_pos()
        o = 1 - c
        cps = [pltpu.make_async_remote_copy(src_ref=ins[t].at[c], dst_ref=outs[t].at[c], send_sem=send.at[t],
                                            recv_sem=recv.at[t], device_id=(x, y, o), device_id_type=MESH)
               for t in range(n)]
        for cp in cps:
            cp.start()
        for t in range(n):
            pltpu.make_async_remote_copy(src_ref=ins[t].at[o], dst_ref=outs[t].at[o], send_sem=send.at[t],
                                         recv_sem=recv.at[t], device_id=(x, y, o), device_id_type=MESH).wait_recv()
        for cp in cps:
            cp.wait_send()

    return pl.pallas_call(
        body, in_specs=[ANY] * n, out_specs=[ANY] * n,
        out_shape=[jax.ShapeDtypeStruct(t.shape, t.dtype) for t in totals],
        scratch_shapes=[pltpu.SemaphoreType.DMA((n,)), pltpu.SemaphoreType.DMA((n,))],
        input_output_aliases={t: t for t in range(n)}, name=name)(*totals)


def _chip_sums(l, g, pos, names):
    tag = f"l{l}_" + "_".join(names)
    recv = _pair_exchange(g, names, "pair_exchange_" + tag)
    return _pair_sum(g, recv, pos, names, "pair_sum_" + tag)


def _gradient_shards(l, sums, scattered, pos, names):
    tag = f"l{l}_" + "_".join(names)
    halves = _pair_share(_chip_sum(sums, scattered, pos, names, "chip_sum_" + tag), "pair_share_" + tag)
    out = {}
    for nm, t in zip(names, halves):
        rows, cols = _piece_shape(nm)
        out[nm] = t.reshape(2 * rows, cols) if GRAD_GEOM[nm][0] == "rows" else t
    return out


def _allreduce_small(packed, name):
    rows = packed.shape[0]
    half = rows // 2
    assert half % 8 == 0

    def body(x_ref, out_ref, sib, parts, done, landed, send_sems, recv_sems):
        x, y, c = _mesh_pos()
        sibling = (x, y, 1 - c)
        mine = pl.ds(pl.multiple_of(c * half, 8), half)
        other = pl.ds(pl.multiple_of((1 - c) * half, 8), half)

        def copy(k, src, dst, to):
            return pltpu.make_async_remote_copy(src_ref=src, dst_ref=dst, send_sem=send_sems.at[k],
                                                recv_sem=recv_sems.at[k], device_id=to, device_id_type=MESH)

        swap = copy(0, x_ref.at[other, :], sib, sibling)
        swap.start()
        swap.wait()
        parts[0] = x_ref[mine, :] + sib[...]
        sends = [copy(1 + j, parts.at[0], parts.at[1 + j], (*chip, c)) for j, chip in enumerate(_other_chips(x, y))]
        for cp in sends:
            cp.start()
        for cp in sends:
            cp.wait()
        tot = None
        for chip in range(N_CHIPS):
            rel = jnp.bitwise_xor(chip, 2 * x + y)
            slot = jnp.where(rel == 0, 0, jnp.where(rel == 2, 1, jnp.where(rel == 1, 2, 3)))
            term = parts[slot]
            tot = term if tot is None else tot + term
        out_ref[mine, :] = tot
        done[...] = tot
        back = copy(4, done, landed, sibling)
        back.start()
        back.wait()
        out_ref[other, :] = landed[...]

    vmem = pl.BlockSpec(memory_space=pltpu.VMEM)
    return pl.pallas_call(
        body, in_specs=[vmem], out_specs=vmem, out_shape=jax.ShapeDtypeStruct((rows, LANES), F32),
        scratch_shapes=[pltpu.VMEM((half, LANES), F32), pltpu.VMEM((N_CHIPS, half, LANES), F32),
                        pltpu.VMEM((half, LANES), F32), pltpu.VMEM((half, LANES), F32),
                        pltpu.SemaphoreType.DMA((5,)), pltpu.SemaphoreType.DMA((5,))],
        compiler_params=pltpu.CompilerParams(vmem_limit_bytes=VMEM_LIMIT_BYTES),
        name=name)(packed)


def _adamw_step(w, g, m, v):
    mn = ADAM_B1 * m + (1.0 - ADAM_B1) * g
    vn = ADAM_B2 * v + (1.0 - ADAM_B2) * (g * g)
    m_hat = mn / (1.0 - ADAM_B1 ** ADAM_STEP)
    v_hat = vn / (1.0 - ADAM_B2 ** ADAM_STEP)
    return -ADAM_LR * (m_hat / (jnp.sqrt(v_hat) + ADAM_EPS) + ADAM_WD * w), mn, vn


def _adamw(w, g, m, v, name):
    rows, cols = w.shape
    tr = 256 if rows % 256 == 0 else rows

    def body(w_ref, g_ref, m_ref, v_ref, d_ref, mo_ref, vo_ref):
        d_ref[...], mo_ref[...], vo_ref[...] = _adamw_step(w_ref[...], g_ref[...], m_ref[...], v_ref[...])

    spec = pl.BlockSpec((tr, cols), lambda i: (i, 0))
    return pl.pallas_call(
        body, grid=(rows // tr,), in_specs=[spec] * 4, out_specs=[spec] * 3,
        out_shape=[jax.ShapeDtypeStruct((rows, cols), F32)] * 3, compiler_params=_cparams("parallel"),
        name=name)(w, g, m, v)


def _adamw_layers(w, gs, m, v, name):
    depth, rows, cols = w.shape
    tr = 256
    nblk = rows // tr
    split = gs[0].ndim == 3

    def body(w_ref, m_ref, v_ref, *rest):
        g_refs, (g_out, d_ref, mo_ref, vo_ref) = rest[:depth], rest[depth:]
        layer = pl.program_id(0)
        load = (lambda r: jnp.concatenate([r[0], r[1]], axis=-1)) if split else (lambda r: r[...])
        gv = load(g_refs[0])
        for k in range(1, depth):
            gv = jnp.where(layer == k, load(g_refs[k]), gv)
        g_out[...] = gv
        d_ref[...], mo_ref[...], vo_ref[...] = _adamw_step(w_ref[...], gv, m_ref[...], v_ref[...])

    def g_spec(k):
        tile = lambda l, i: jnp.where(l == k, i, jnp.where(l < k, 0, nblk - 1))
        if split:
            return pl.BlockSpec((2, tr, cols // 2), lambda l, i: (0, tile(l, i), 0))
        return pl.BlockSpec((tr, cols), lambda l, i: (tile(l, i), 0))

    spec = pl.BlockSpec((None, tr, cols), lambda l, i: (l, i, 0))
    return pl.pallas_call(
        body, grid=(depth, nblk), in_specs=[spec] * 3 + [g_spec(k) for k in range(depth)], out_specs=[spec] * 4,
        out_shape=[jax.ShapeDtypeStruct(w.shape, F32)] * 4, compiler_params=_cparams("parallel", "parallel"),
        name=name)(w, m, v, *gs)


def _adamw_nd(w, g, m, v, name):
    cols = w.shape[-1] if w.shape[-1] % LANES == 0 else LANES
    outs = _adamw(*(t.reshape(-1, cols) for t in (w, g, m, v)), name)
    return tuple(t.reshape(w.shape) for t in outs)


def _pack(arrays):
    return jnp.concatenate([a.reshape(-1, LANES) for a in arrays], axis=0)


def _unpack(packed, shapes):
    out, row = [], 0
    for sh in shapes:
        n = math.prod(sh) // LANES
        out.append(packed[row:row + n].reshape(sh))
        row += n
    return out


WEIGHTS = ("pre_mix_norm", "w_in", "v_norm_g", "v_norm_b", "w_spatial", "b_spatial", "out_norm_a", "out_norm_b",
           "w_out", "post_mix_norm", "pre_ffn_norm", "w_up", "conv_w", "conv_b", "w_down", "post_ffn_norm")


def kernel(x, pre_mix_norm, w_in, v_norm_g, v_norm_b, w_spatial, b_spatial, out_norm_a, out_norm_b, w_out, post_mix_norm, pre_ffn_norm, w_up, conv_w, conv_b, w_down, post_ffn_norm, loss_target, m_pre_mix_norm, m_w_in, m_v_norm_g, m_v_norm_b, m_w_spatial, m_b_spatial, m_out_norm_a, m_out_norm_b, m_w_out, m_post_mix_norm, m_pre_ffn_norm, m_w_up, m_conv_w, m_conv_b, m_w_down, m_post_ffn_norm, v_pre_mix_norm, v_w_in, v_v_norm_g, v_v_norm_b, v_w_spatial, v_b_spatial, v_out_norm_a, v_out_norm_b, v_w_out, v_post_mix_norm, v_pre_ffn_norm, v_w_up, v_conv_w, v_conv_b, v_w_down, v_post_ffn_norm):
    w = dict(pre_mix_norm=pre_mix_norm, w_in=w_in, v_norm_g=v_norm_g, v_norm_b=v_norm_b, w_spatial=w_spatial,
             b_spatial=b_spatial, out_norm_a=out_norm_a, out_norm_b=out_norm_b, w_out=w_out,
             post_mix_norm=post_mix_norm, pre_ffn_norm=pre_ffn_norm, w_up=w_up, conv_w=conv_w, conv_b=conv_b,
             w_down=w_down, post_ffn_norm=post_ffn_norm)
    m = dict(pre_mix_norm=m_pre_mix_norm, w_in=m_w_in, v_norm_g=m_v_norm_g, v_norm_b=m_v_norm_b,
             w_spatial=m_w_spatial, b_spatial=m_b_spatial, out_norm_a=m_out_norm_a, out_norm_b=m_out_norm_b,
             w_out=m_w_out, post_mix_norm=m_post_mix_norm, pre_ffn_norm=m_pre_ffn_norm, w_up=m_w_up,
             conv_w=m_conv_w, conv_b=m_conv_b, w_down=m_w_down, post_ffn_norm=m_post_ffn_norm)
    v = dict(pre_mix_norm=v_pre_mix_norm, w_in=v_w_in, v_norm_g=v_v_norm_g, v_norm_b=v_v_norm_b,
             w_spatial=v_w_spatial, b_spatial=v_b_spatial, out_norm_a=v_out_norm_a, out_norm_b=v_out_norm_b,
             w_out=v_w_out, post_mix_norm=v_post_mix_norm, pre_ffn_norm=v_pre_ffn_norm, w_up=v_w_up,
             conv_w=v_conv_w, conv_b=v_conv_b, w_down=v_w_down, post_ffn_norm=v_post_ffn_norm)
    pos = jnp.stack([lax.axis_index("x"), lax.axis_index("y"), lax.axis_index("c")]).astype(jnp.int32)
    chip = 2 * lax.axis_index("x") + lax.axis_index("y")

    cw_cols = conv_w.shape[-1]
    blocks = [{n: w[n][l].astype(BF16) for n in BIG} for l in range(DEPTH)]
    small = {n: w[n] for n in SMALL}
    xs, target = x[0], loss_target[0]
    xin = xs
    h, (w_in0, cw_all) = _rms_cast(xin, small["pre_mix_norm"][0].reshape(1, -1), "pre_mix_l0",
                                   [blocks[0]["w_in"], conv_w.reshape(-1, LANES)])
    wg = dict(w_in=w_in0)
    conv_w_full = cw_all.reshape(N_CHIPS, DEPTH, 3, cw_cols).transpose(1, 2, 0, 3).reshape(DEPTH, 3, 2 * D_FF)

    tabs = _rope_tables(xs.shape[0])
    params = [_layer_params(l, small, conv_w_full) for l in range(DEPTH)]
    saved, wgs = [], []
    for l in range(DEPTH):
        sv, gathered, wg = _layer_forward(l, xin, h, params[l], wg, tabs,
                                          blocks[l + 1] if l + 1 < DEPTH else None,
                                          blocks[0] if l == 0 else None,
                                          params[l + 1]["pre_mix_norm"] if l + 1 < DEPTH else None)
        saved.append(sv)
        wgs.append(wg)
        if l + 1 < DEPTH:
            wg = gathered
            xin, h = sv["x2"], sv["h_next"]
    loss_part, dx, df, g_post = _loss_norm_bwd(saved[-1]["x1"], saved[-1]["f"], params[-1]["post_ffn_norm"], target,
                                               "loss")
    smalls, shards = [None] * DEPTH, [{} for _ in range(DEPTH)]
    pending = None
    for l in reversed(range(DEPTH)):
        dx1, dh1, big, smalls[l], scattered, done = _layer_backward(l, dx, df, saved[l], params[l], wgs[l], tabs, pos,
                                                                    pending[1:] if pending else None, hide=l == 0)
        smalls[l]["post_ffn_norm"] = g_post
        if l > 0:
            dx, smalls[l]["pre_mix_norm"], df, g_post = _norm_bwd_in_out(
                dx1, dh1, saved[l]["x0"], params[l]["pre_mix_norm"], saved[l - 1]["f"], params[l - 1]["post_ffn_norm"],
                f"norm_bwd_in_out_l{l}")
        else:
            dx, smalls[l]["pre_mix_norm"] = _norm_bwd_in(dx1, dh1, saved[l]["x0"], params[l]["pre_mix_norm"],
                                                         "norm_bwd_in_l0")
        if pending:
            shards[pending[0]].update(_gradient_shards(pending[0], pending[1], scattered, pos, pending[2]))
        if done:
            shards[l].update(_gradient_shards(
                l, [t for sums, _ in done.values() for t in sums], [t for _, received in done.values() for t in received],
                pos, tuple(n for names in done for n in names)))
        names = tuple(big)
        pending = (l, [big[n] for n in names], names) if names else None
    if pending:
        shards[pending[0]].update(_gradient_shards(
            pending[0], pending[1], _chip_scatter(pending[1], pending[2], f"chip_scatter_l{pending[0]}"), pos,
            pending[2]))

    small_shapes = [w[n].shape for n in SMALL]
    stacked = [jnp.stack([smalls[l][n].reshape(w[n].shape[1:]) for l in range(DEPTH)]) for n in SMALL]
    cw_grad = jnp.stack([smalls[l]["conv_w"] for l in range(DEPTH)])
    packed = _pack(stacked + [cw_grad, loss_part])
    total = _allreduce_small(packed, "allreduce_small")
    parts = _unpack(total, small_shapes + [cw_grad.shape, (8, LANES)])
    g_small = dict(zip(SMALL, parts[:len(SMALL)]))
    loss = parts[-1][0, 0]
    g_conv_w = lax.dynamic_slice(parts[-2], (0, 0, chip * cw_cols), conv_w.shape)

    grads = dict(g_small, conv_w=g_conv_w)

    dp, mp, vp = _adamw(_pack([w[n] for n in SMALL]), _pack([g_small[n] for n in SMALL]),
                        _pack([m[n] for n in SMALL]), _pack([v[n] for n in SMALL]), "adamw_small")
    delta = dict(zip(SMALL, _unpack(dp, small_shapes)))
    new_m = dict(zip(SMALL, _unpack(mp, small_shapes)))
    new_v = dict(zip(SMALL, _unpack(vp, small_shapes)))
    delta["conv_w"], new_m["conv_w"], new_v["conv_w"] = _adamw_nd(w["conv_w"], g_conv_w, m["conv_w"], v["conv_w"],
                                                                  "adamw_conv_w")
    for n in BIG:
        grads[n], delta[n], new_m[n], new_v[n] = _adamw_layers(w[n], [shards[l][n] for l in range(DEPTH)], m[n],
                                                               v[n], "adamw_" + n)

    return (loss, dx[None], *[grads[n] for n in WEIGHTS], *[delta[n] for n in WEIGHTS],
            *[new_m[n] for n in WEIGHTS], *[new_v[n] for n in WEIGHTS])
```

```python
import functools
import math

import jax
import jax.numpy as jnp
import numpy as np
from jax import lax
from jax.experimental import pallas as pl
from jax.experimental.pallas import tpu as pltpu

F32 = jnp.float32
BF16 = jnp.bfloat16
MESH = pl.DeviceIdType.MESH

D_MODEL = 1024
A_WIDTH = 512
A_GROUPS = 4
GROUP_DIM = 128
CHUNK = 128
B_WIDTH = 512
HEAD_DIM = 64
ROT_DIM = 16
ROPE_THETA = 500000.0
DILATIONS = (1, 4, 16)
BAND = 128
IN_COLS = 2560
D_FF = 4096
EPS = 1e-6
NEG_INF = -1e30
N_CHIPS = 4
LANES = 128

ADAM_LR = 0.001
ADAM_B1 = 0.9
ADAM_B2 = 0.999
ADAM_EPS = 1e-08
ADAM_WD = 0.01
ADAM_STEP = 10

VMEM_LIMIT_BYTES = 56 * 1024 * 1024
RSQRT2 = 0.7071067811865476
INV_SQRT_2PI = 0.3989422804014327
GELU_C = 0.7978845608028654
GELU_A = 0.044715

ANY = pl.BlockSpec(memory_space=pl.ANY)
NN = ((1,), (0,))
NT = ((1,), (1,))
TN = ((0,), (0,))


def _cparams(*sem):
    return pltpu.CompilerParams(dimension_semantics=sem, vmem_limit_bytes=VMEM_LIMIT_BYTES)


def _dot(a, b, dims):
    return lax.dot_general(a, b, (dims, ((), ())), preferred_element_type=F32)


def _rsq_mean(a):
    return lax.rsqrt(jnp.mean(a * a, axis=-1, keepdims=True) + EPS)


def _rms_bwd(a, r, g, dz):
    t = dz * g
    da = r * t - a * (r * r * r) * jnp.mean(t * a, axis=-1, keepdims=True)
    return da, dz * a * r


def _colsum(a):
    return jnp.sum(a, axis=0, keepdims=True)


def _gelu_tanh(x):
    u = x * x
    t = jnp.tanh(x * (GELU_C + (GELU_C * GELU_A) * u))
    hx = 0.5 * x
    act = hx + hx * t
    grad = 0.5 + 0.5 * t + (hx - hx * t * t) * (GELU_C + (3.0 * GELU_C * GELU_A) * u)
    return act, grad


def _grid_edges(grid):
    ids = [pl.program_id(ax) for ax in range(len(grid))]
    first = functools.reduce(jnp.logical_and, [i == 0 for i in ids])
    last = functools.reduce(jnp.logical_and, [i == n - 1 for i, n in zip(ids, grid)])
    return first, last


def _matmul(a, b, *, grid, a_spec, b_spec, o_spec, o_shape, o_dtype, dims, nk, kaxis, acc_shape, name, b_2d=None,
            halves=False, scatter=None, gather=None):
    assert scatter is None or gather is None
    ns = len(scatter[0]) if scatter else len(gather) if gather else 0

    def body(*refs):
        a_ref, b_ref = refs[:2]
        o_ref = refs[2 + ns]
        scratch = refs[3 + 2 * ns:]
        if ns:
            first, last = _grid_edges(grid)
            if scatter:
                start, finish = _scatter_steps(refs[2:2 + ns], refs[3 + ns:3 + 2 * ns], scratch[-2], scratch[-1],
                                               scatter[1])
            else:
                start, relay, last_wait = _gather_steps(refs[2:2 + ns], refs[3 + ns:3 + 2 * ns], scratch[-2],
                                                        scratch[-1])

                def finish():
                    relay()
                    last_wait()
            pl.when(first)(start)
        def store(val):
            if halves:
                half = val.shape[1] // 2
                o_ref[0] = val[:, :half].astype(o_dtype)
                o_ref[1] = val[:, half:].astype(o_dtype)
            else:
                o_ref[...] = val.astype(o_dtype)

        bv = b_ref[...] if b_2d is None else b_ref[...].reshape(b_2d)
        part = _dot(a_ref[...], bv, dims)
        if nk == 1:
            store(part)
        else:
            acc = scratch[0]
            k = pl.program_id(kaxis)

            @pl.when(k == 0)
            def _():
                acc[...] = part

            @pl.when(k > 0)
            def _():
                acc[...] += part

            @pl.when(k == nk - 1)
            def _():
                store(acc[...])

        if ns:
            pl.when(last)(finish)

    sem = tuple("arbitrary" if (ns or (nk > 1 and ax == kaxis)) else "parallel" for ax in range(len(grid)))
    riding = list(scatter[0]) if scatter else list(gather or [])
    rider_shapes = _scattered_shapes(scatter[1]) if scatter else _gathered_shapes(riding)
    rider_sems = _scatter_sems(ns) if scatter else _gather_sems(ns) if gather else []
    res = pl.pallas_call(
        body, grid=grid, in_specs=[a_spec, b_spec] + [ANY] * ns, out_specs=[o_spec] + [ANY] * ns,
        out_shape=[jax.ShapeDtypeStruct(o_shape, o_dtype)] + rider_shapes,
        scratch_shapes=([pltpu.VMEM(acc_shape, F32)] if nk > 1 else []) + rider_sems,
        compiler_params=_cparams(*sem), name=name)(a, b, *riding)
    return (res[0], list(res[1:])) if ns else res[0]


def _mix_out_norm(mixed, w_out, x0, g_post, g_next, name, gather=None):
    s, d = x0.shape
    tm = 512
    ng = 0 if gather is None else len(gather)

    def body(a_ref, w_ref, x_ref, gp_ref, gn_ref, *rest):
        y_ref, x1_ref, h_ref = rest[ng:ng + 3]
        if ng:
            start, relay, finish = _gather_steps(rest[:ng], rest[ng + 3:2 * ng + 3], *rest[2 * ng + 3:])
            first, last = _grid_edges((s // tm,))
            pl.when(first)(start)
        y = _dot(a_ref[...], w_ref[...].reshape(d, d), NN)
        y_ref[...] = y
        x1 = x_ref[...] + y * _rsq_mean(y) * gp_ref[...]
        x1_ref[...] = x1
        h_ref[...] = (x1 * _rsq_mean(x1) * gn_ref[...]).astype(BF16)

        if ng:
            @pl.when(last)
            def _():
                relay()
                finish()

    row = pl.BlockSpec((tm, d), lambda i: (i, 0))
    vec = pl.BlockSpec((1, d), lambda i: (0, 0))
    res = pl.pallas_call(
        body, grid=(s // tm,),
        in_specs=[row, pl.BlockSpec((N_CHIPS, None, d // N_CHIPS, d), lambda i: (0, 0, 0, 0)), row, vec, vec]
        + [ANY] * ng,
        out_specs=[row, row, row] + [ANY] * ng,
        out_shape=[jax.ShapeDtypeStruct((s, d), F32), jax.ShapeDtypeStruct((s, d), F32),
                   jax.ShapeDtypeStruct((s, d), BF16)] + _gathered_shapes(gather or []),
        scratch_shapes=_gather_sems(ng) if ng else [],
        compiler_params=_cparams("arbitrary" if ng else "parallel"), name=name)(mixed, w_out, x0, g_post, g_next,
                                                                              *(gather or []))
    return res[:3], list(res[3:])


def _proj_bwd(dproj, w_in, name, scatter=None):
    s = dproj.shape[0]
    wcol = IN_COLS // N_CHIPS
    ns = 0 if scatter is None else len(scatter[0])

    def body(*refs):
        a_ref, w_ref = refs[:2]
        o_ref = refs[2 + ns]
        if ns:
            start, finish = _scatter_steps(refs[2:2 + ns], refs[3 + ns:3 + 2 * ns], *refs[3 + 2 * ns:], scatter[1])
            first, last = _grid_edges((s // TMM,))
            pl.when(first)(start)
        acc = _dot(a_ref[:, :wcol], w_ref[0], NT)
        for j in range(1, N_CHIPS):
            acc = acc + _dot(a_ref[:, j * wcol:(j + 1) * wcol], w_ref[j], NT)
        o_ref[...] = acc
        if ns:
            pl.when(last)(finish)

    res = pl.pallas_call(
        body, grid=(s // TMM,),
        in_specs=[pl.BlockSpec((TMM, IN_COLS), lambda i: (i, 0)),
                  pl.BlockSpec((N_CHIPS, None, D_MODEL, wcol), lambda i: (0, 0, 0, 0))] + [ANY] * ns,
        out_specs=[pl.BlockSpec((TMM, D_MODEL), lambda i: (i, 0))] + [ANY] * ns,
        out_shape=[jax.ShapeDtypeStruct((s, D_MODEL), F32)] + (_scattered_shapes(scatter[1]) if ns else []),
        scratch_shapes=_scatter_sems(ns) if ns else [],
        compiler_params=_cparams("arbitrary" if ns else "parallel"), name=name)(dproj, w_in,
                                                                              *(scatter[0] if ns else []))
    return res[0], list(res[1:])


TM = 1024
TMM = 1024


TR = 512


def _row_spec(width, col=0):
    return pl.BlockSpec((TR, width), lambda i, col=col: (i, col))


def _vec_spec(width):
    return pl.BlockSpec((1, width), lambda i: (0, 0))


def _rms_cast(x, g, name, gather=None):
    s, d = x.shape
    ng = 0 if gather is None else len(gather)

    def body(x_ref, g_ref, *rest):
        if ng:
            start, relay, finish = _gather_steps(rest[:ng], rest[ng + 1:2 * ng + 1], *rest[2 * ng + 1:])
            first, last = _grid_edges((s // TR,))
            pl.when(first)(start)
        a = x_ref[...]
        rest[ng][...] = (a * _rsq_mean(a) * g_ref[...]).astype(BF16)

        if ng:
            @pl.when(last)
            def _():
                relay()
                finish()

    res = pl.pallas_call(
        body, grid=(s // TR,), in_specs=[_row_spec(d), _vec_spec(d)] + [ANY] * ng,
        out_specs=[_row_spec(d)] + [ANY] * ng,
        out_shape=[jax.ShapeDtypeStruct((s, d), BF16)] + _gathered_shapes(gather or []),
        scratch_shapes=_gather_sems(ng) if ng else [],
        compiler_params=_cparams("arbitrary" if ng else "parallel"), name=name)(x, g, *(gather or []))
    return res[0], list(res[1:])


def _acc_init(refs):
    @pl.when(pl.program_id(0) == 0)
    def _():
        for r in refs:
            r[...] = jnp.zeros_like(r)


def _loss_norm_bwd(x1, f, g_post, target, name):
    s, d = x1.shape

    def body(x_ref, f_ref, gp_ref, t_ref, loss_ref, dx_ref, df_ref, dg_ref):
        _acc_init([loss_ref, dg_ref])
        fv = f_ref[...]
        r = _rsq_mean(fv)
        err = x_ref[...] + fv * r * gp_ref[...] - t_ref[...]
        dx = err * (1.0 / d)
        dx_ref[...] = dx
        part = 0.5 * jnp.sum(jnp.mean(err * err, axis=-1, keepdims=True), axis=0, keepdims=True)
        loss_ref[...] += jnp.broadcast_to(part, loss_ref.shape)
        da, dgt = _rms_bwd(fv, r, gp_ref[...], dx)
        df_ref[...] = da.astype(BF16)
        dg_ref[...] += _colsum(dgt)

    return pl.pallas_call(
        body, grid=(s // TR,), in_specs=[_row_spec(d), _row_spec(d), _vec_spec(d), _row_spec(d)],
        out_specs=[pl.BlockSpec((8, LANES), lambda i: (0, 0)), _row_spec(d), _row_spec(d), _vec_spec(d)],
        out_shape=[jax.ShapeDtypeStruct((8, LANES), F32), jax.ShapeDtypeStruct((s, d), F32),
                   jax.ShapeDtypeStruct((s, d), BF16), jax.ShapeDtypeStruct((1, d), F32)],
        compiler_params=_cparams("arbitrary"), name=name)(x1, f, g_post, target)


def _norm_bwd_mid(dx2, dh2, x1, g_pf, y1, g_pm, name):
    s, d = dx2.shape

    def body(dx2_ref, dh_ref, x1_ref, gpf_ref, y1_ref, gpm_ref, dx1_ref, dy1_ref, dgpf_ref, dgpm_ref):
        _acc_init([dgpf_ref, dgpm_ref])
        x1 = x1_ref[...]
        da, dgt = _rms_bwd(x1, _rsq_mean(x1), gpf_ref[...], dh_ref[...])
        dx1 = dx2_ref[...] + da
        dx1_ref[...] = dx1
        dgpf_ref[...] += _colsum(dgt)
        y1 = y1_ref[...]
        dy, dgt2 = _rms_bwd(y1, _rsq_mean(y1), gpm_ref[...], dx1)
        dy1_ref[...] = dy.astype(BF16)
        dgpm_ref[...] += _colsum(dgt2)

    return pl.pallas_call(
        body, grid=(s // TR,),
        in_specs=[_row_spec(d), _row_spec(d), _row_spec(d), _vec_spec(d), _row_spec(d), _vec_spec(d)],
        out_specs=[_row_spec(d), _row_spec(d), _vec_spec(d), _vec_spec(d)],
        out_shape=[jax.ShapeDtypeStruct((s, d), F32), jax.ShapeDtypeStruct((s, d), BF16),
                   jax.ShapeDtypeStruct((1, d), F32), jax.ShapeDtypeStruct((1, d), F32)],
        compiler_params=_cparams("arbitrary"), name=name)(dx2, dh2, x1, g_pf, y1, g_pm)


def _norm_bwd_in_out(dx1, dh1, x0, g1, f_below, g_post_below, name):
    s, d = dx1.shape

    def body(dx1_ref, dh_ref, x0_ref, g_ref, f_ref, gp_ref, dx0_ref, dg_ref, df_ref, dgp_ref):
        _acc_init([dg_ref, dgp_ref])
        x0 = x0_ref[...]
        da, dgt = _rms_bwd(x0, _rsq_mean(x0), g_ref[...], dh_ref[...])
        dx0 = dx1_ref[...] + da
        dx0_ref[...] = dx0
        dg_ref[...] += _colsum(dgt)
        fv = f_ref[...]
        db, dgt2 = _rms_bwd(fv, _rsq_mean(fv), gp_ref[...], dx0)
        df_ref[...] = db.astype(BF16)
        dgp_ref[...] += _colsum(dgt2)

    return pl.pallas_call(
        body, grid=(s // TR,),
        in_specs=[_row_spec(d), _row_spec(d), _row_spec(d), _vec_spec(d), _row_spec(d), _vec_spec(d)],
        out_specs=[_row_spec(d), _vec_spec(d), _row_spec(d), _vec_spec(d)],
        out_shape=[jax.ShapeDtypeStruct((s, d), F32), jax.ShapeDtypeStruct((1, d), F32),
                   jax.ShapeDtypeStruct((s, d), BF16), jax.ShapeDtypeStruct((1, d), F32)],
        compiler_params=_cparams("arbitrary"), name=name)(dx1, dh1, x0, g1, f_below, g_post_below)


def _norm_bwd_in(dx1, dh1, x0, g1, name):
    s, d = dx1.shape

    def body(dx1_ref, dh_ref, x0_ref, g_ref, dx0_ref, dg_ref):
        _acc_init([dg_ref])
        x0 = x0_ref[...]
        da, dgt = _rms_bwd(x0, _rsq_mean(x0), g_ref[...], dh_ref[...])
        dx0_ref[...] = dx1_ref[...] + da
        dg_ref[...] += _colsum(dgt)

    return pl.pallas_call(
        body, grid=(s // TR,), in_specs=[_row_spec(d), _row_spec(d), _row_spec(d), _vec_spec(d)],
        out_specs=[_row_spec(d), _vec_spec(d)],
        out_shape=[jax.ShapeDtypeStruct((s, d), F32), jax.ShapeDtypeStruct((1, d), F32)],
        compiler_params=_cparams("arbitrary"), name=name)(dx1, dh1, x0, g1)


def _tril_mask():
    row = lax.broadcasted_iota(jnp.int32, (CHUNK, CHUNK), 0)
    col = lax.broadcasted_iota(jnp.int32, (CHUNK, CHUNK), 1)
    return row >= col


def _gating_forward(pa, gv, bv, wt, bsf):
    er = lax.erf(pa * RSQRT2)
    za = 0.5 * pa * (1.0 + er)
    u = za[:, :A_WIDTH]
    va = za[:, A_WIDTH:]
    xc = va - jnp.mean(va, axis=-1, keepdims=True)
    rs = lax.rsqrt(jnp.mean(xc * xc, axis=-1, keepdims=True) + EPS)
    vn = xc * rs
    vlb = (vn * gv + bv).astype(BF16)
    sg = jnp.concatenate(
        [_dot(wt[g], vlb[:, g * GROUP_DIM:(g + 1) * GROUP_DIM], NN) for g in range(A_GROUPS)], axis=1) + bsf
    return er, u, rs, vn, vlb, sg


def _masked_ws(ws_ref):
    mask = _tril_mask()
    return [jnp.where(mask, ws_ref[g], 0.0).astype(BF16) for g in range(A_GROUPS)]


def _mixer_a_fwd(proj, gv, bv, ws, bsf, ga, name):
    s = proj.shape[0]

    def body(p_ref, gv_ref, bv_ref, ws_ref, bs_ref, ga_ref, o_ref):
        wt = _masked_ws(ws_ref)
        for ch in range(TR // CHUNK):
            rows = slice(ch * CHUNK, (ch + 1) * CHUNK)
            _, u, _, _, _, sg = _gating_forward(p_ref[rows, :].astype(F32), gv_ref[...], bv_ref[...], wt, bs_ref[...])
            oa = u * sg
            o_ref[rows, :] = (oa * _rsq_mean(oa) * ga_ref[...]).astype(BF16)

    return pl.pallas_call(
        body, grid=(s // TR,),
        in_specs=[_row_spec(2 * A_WIDTH), _vec_spec(A_WIDTH), _vec_spec(A_WIDTH),
                  pl.BlockSpec((A_GROUPS, CHUNK, CHUNK), lambda i: (0, 0, 0)),
                  pl.BlockSpec((CHUNK, A_WIDTH), lambda i: (0, 0)), _vec_spec(A_WIDTH)],
        out_specs=_row_spec(A_WIDTH), out_shape=jax.ShapeDtypeStruct((s, A_WIDTH + B_WIDTH), BF16),
        compiler_params=_cparams("parallel"), name=name)(proj, gv, bv, ws, bsf, ga)


def _mixer_a_bwd(proj, dmixed, gv, bv, ws, bsf, ga, name, scatter=None):
    s = proj.shape[0]
    nsteps = s // TR
    ns = 0 if scatter is None else len(scatter[0])

    def body(*refs):
        p_ref, dm_ref, gv_ref, bv_ref, ws_ref, bs_ref, ga_ref = refs[:7]
        dp_ref, dga_ref, dgv_ref, dbv_ref, dbs_ref, dws_ref = refs[7 + ns:13 + ns]
        if ns:
            start, finish = _scatter_steps(refs[7:7 + ns], refs[13 + ns:13 + 2 * ns], *refs[13 + 2 * ns:], scatter[1])
            first, last = _grid_edges((nsteps,))
            pl.when(first)(start)
        _acc_init([dga_ref, dgv_ref, dbv_ref, dbs_ref, dws_ref])
        mask = _tril_mask()
        wt = _masked_ws(ws_ref)
        gvv = gv_ref[...]
        gav = ga_ref[...]
        for ch in range(TR // CHUNK):
            rows = slice(ch * CHUNK, (ch + 1) * CHUNK)
            pa = p_ref[rows, :].astype(F32)
            er, u, rs, vn, vlb, sg = _gating_forward(pa, gvv, bv_ref[...], wt, bs_ref[...])
            oa = u * sg
            doa, dgt = _rms_bwd(oa, _rsq_mean(oa), gav, dm_ref[rows, :])
            dga_ref[...] += _colsum(dgt)
            du = doa * sg
            dsg = doa * u
            dbs_ref[...] += dsg
            dsgb = dsg.astype(BF16)
            dvl = []
            for g in range(A_GROUPS):
                cols = slice(g * GROUP_DIM, (g + 1) * GROUP_DIM)
                dws_ref[g] += jnp.where(mask, _dot(dsgb[:, cols], vlb[:, cols], NT), 0.0)
                dvl.append(_dot(wt[g], dsgb[:, cols], TN))
            dvl = jnp.concatenate(dvl, axis=1)
            dgv_ref[...] += _colsum(dvl * vn)
            dbv_ref[...] += _colsum(dvl)
            dvn = dvl * gvv
            dva = rs * (dvn - jnp.mean(dvn, axis=-1, keepdims=True)
                        - vn * jnp.mean(dvn * vn, axis=-1, keepdims=True))
            gp = 0.5 * (1.0 + er) + pa * jnp.exp(-0.5 * pa * pa) * INV_SQRT_2PI
            dp_ref[rows, :] = (jnp.concatenate([du, dva], axis=1) * gp).astype(BF16)

        @pl.when(pl.program_id(0) == nsteps - 1)
        def _():
            for g in range(A_GROUPS):
                cols = slice(g * GROUP_DIM, (g + 1) * GROUP_DIM)
                tot = jnp.sum(dbs_ref[:, cols], axis=1, keepdims=True)
                dbs_ref[:, cols] = jnp.broadcast_to(tot, (CHUNK, GROUP_DIM))

        if ns:
            pl.when(last)(finish)

    full = lambda *shape: pl.BlockSpec(shape, lambda i: (0,) * len(shape))
    res = pl.pallas_call(
        body, grid=(nsteps,),
        in_specs=[_row_spec(2 * A_WIDTH), _row_spec(A_WIDTH), _vec_spec(A_WIDTH), _vec_spec(A_WIDTH),
                  full(A_GROUPS, CHUNK, CHUNK), full(CHUNK, A_WIDTH), _vec_spec(A_WIDTH)] + [ANY] * ns,
        out_specs=[_row_spec(2 * A_WIDTH), _vec_spec(A_WIDTH), _vec_spec(A_WIDTH), _vec_spec(A_WIDTH),
                   full(CHUNK, A_WIDTH), full(A_GROUPS, CHUNK, CHUNK)] + [ANY] * ns,
        out_shape=[jax.ShapeDtypeStruct((s, IN_COLS), BF16), jax.ShapeDtypeStruct((1, A_WIDTH), F32),
                   jax.ShapeDtypeStruct((1, A_WIDTH), F32), jax.ShapeDtypeStruct((1, A_WIDTH), F32),
                   jax.ShapeDtypeStruct((CHUNK, A_WIDTH), F32),
                   jax.ShapeDtypeStruct((A_GROUPS, CHUNK, CHUNK), F32)]
        + (_scattered_shapes(scatter[1]) if ns else []),
        scratch_shapes=_scatter_sems(ns) if ns else [],
        compiler_params=_cparams("arbitrary"), name=name)(proj, dmixed, gv, bv, ws, bsf, ga,
                                                          *(scatter[0] if ns else []))
    return res[:6] + (list(res[6:]),)


def _rope_tables(s):
    half = ROT_DIM // 2
    lane = jnp.arange(LANES) % HEAD_DIM
    inv = ROPE_THETA ** (-(2 * (lane % half)).astype(F32) / ROT_DIM)
    ang = jnp.arange(s, dtype=F32)[:, None] * inv[None, :]
    cos, sin = jnp.cos(ang), jnp.sin(ang)
    c = jnp.where(lane < ROT_DIM, cos, 1.0)
    s1 = jnp.where(lane < half, -sin, 0.0)
    s2 = jnp.where((lane >= half) & (lane < ROT_DIM), sin, 0.0)
    return c, s1, s2


def _lane_blocks(width):
    return [slice(b * LANES, (b + 1) * LANES) for b in range(width // LANES)]


CLASS_DILS = tuple(d for d in DILATIONS if d > 1)


def _class_shape(s, dil, dtype):
    return jax.ShapeDtypeStruct((dil, s // dil, B_WIDTH), dtype)


def _class_spec(dil):
    return pl.BlockSpec((dil, TR // dil, B_WIDTH), lambda i, *_: (0, i, 0))


NBLK = B_WIDTH // LANES
STAGE = pltpu.VMEM((NBLK, TR, LANES), F32)


def _stage_put(stage, value):
    for b, sl in enumerate(_lane_blocks(B_WIDTH)):
        stage[b] = value[:, sl]


def _stage_get(stage):
    return jnp.concatenate([stage[b] for b in range(NBLK)], axis=1)


def _store_classes(stage, dst_ref, dil):
    for b, sl in enumerate(_lane_blocks(B_WIDTH)):
        for r in range(dil):
            dst_ref[r, :, sl] = stage[b, pl.ds(r, TR // dil, stride=dil), :].astype(dst_ref.dtype)


def _load_classes(src_ref, stage, dil):
    for b, sl in enumerate(_lane_blocks(B_WIDTH)):
        for r in range(dil):
            stage[b, pl.ds(r, TR // dil, stride=dil), :] = src_ref[r, :, sl].astype(F32)
    return _stage_get(stage)


def _rope_fwd(proj, tabs, name, gather=None):
    s = proj.shape[0]
    half = ROT_DIM // 2
    scale = HEAD_DIM ** -0.5
    nlay = 1 + len(CLASS_DILS)
    ng = 0 if gather is None else len(gather)

    def body(q_ref, k_ref, v_ref, c_ref, s1_ref, s2_ref, *rest):
        outs, stage = rest[ng:ng + 3 * nlay], rest[2 * ng + 3 * nlay]
        if ng:
            start, relay, finish = _gather_steps(rest[:ng], rest[ng + 3 * nlay:2 * ng + 3 * nlay],
                                                 *rest[2 * ng + 3 * nlay + 1:])
            first, last = _grid_edges((s // TR,))
            pl.when(first)(start)
        c, s1, s2 = c_ref[...], s1_ref[...], s2_ref[...]
        for which, (src, mul) in enumerate(((q_ref, scale), (k_ref, 1.0), (v_ref, None))):
            if mul is None:
                _stage_put(stage, src[...].astype(F32))
            else:
                for b, sl in enumerate(_lane_blocks(B_WIDTH)):
                    a = src[:, sl].astype(F32)
                    r = a * c + pltpu.roll(a, LANES - half, 1) * s1 + pltpu.roll(a, half, 1) * s2
                    stage[b] = r * mul
            dst = outs[which * nlay:(which + 1) * nlay]
            dst[0][...] = _stage_get(stage).astype(BF16)
            for ref, d in zip(dst[1:], CLASS_DILS):
                _store_classes(stage, ref, d)

        if ng:
            @pl.when(last)
            def _():
                relay()
                finish()

    tab = pl.BlockSpec((TR, LANES), lambda i: (i, 0))
    lay_specs = [_row_spec(B_WIDTH)] + [_class_spec(d) for d in CLASS_DILS]
    lay_shapes = [jax.ShapeDtypeStruct((s, B_WIDTH), BF16)] + [_class_shape(s, d, BF16) for d in CLASS_DILS]
    outs = pl.pallas_call(
        body, grid=(s // TR,),
        in_specs=[_row_spec(B_WIDTH, 2), _row_spec(B_WIDTH, 3), _row_spec(B_WIDTH, 4), tab, tab, tab] + [ANY] * ng,
        out_specs=lay_specs * 3 + [ANY] * ng, out_shape=lay_shapes * 3 + _gathered_shapes(gather or []),
        scratch_shapes=[STAGE] + (_gather_sems(ng) if ng else []),
        compiler_params=_cparams("arbitrary" if ng else "parallel"), name=name)(proj, proj, proj, *tabs,
                                                                              *(gather or []))
    q, k, v = (dict(zip(DILATIONS, outs[w * nlay:(w + 1) * nlay])) for w in range(3))
    return q, k, v, list(outs[3 * nlay:])


def _as_classes(t):
    return t if t.ndim == 3 else t[None]


def _head_masks():
    lane = lax.broadcasted_iota(jnp.int32, (1, LANES), 1)
    return lane < HEAD_DIM, lane >= HEAD_DIM


def _stack_heads(t):
    lo, hi = _head_masks()
    zero = jnp.zeros_like(t)
    return jnp.concatenate([jnp.where(lo, t, zero), jnp.where(hi, t, zero)], axis=0)


MAX_SEGMENT_BLOCKS = 8


def _segment_masks(j):
    qi = lax.broadcasted_iota(jnp.int32, (BAND, 2 * BAND), 0)
    kj = lax.broadcasted_iota(jnp.int32, (BAND, 2 * BAND), 1)
    both = (kj >= qi) & (kj <= qi + BAND)
    own = kj[:, :BAND] <= qi[:, :BAND]
    head = both & ((kj >= BAND) | (j > 0))
    return tuple(jnp.concatenate([m, m], axis=0) for m in (own, both, head))


def _block_rows(g):
    return pl.ds(pl.multiple_of(g * BAND, BAND), BAND)


def _key_rows(g):
    return pl.ds(pl.multiple_of((g - 1) * BAND, BAND), 2 * BAND)


def _segments(n):
    nb = n // BAND
    seg = min(nb, MAX_SEGMENT_BLOCKS)
    return seg, nb // seg


def _segment_specs(seg):
    main = pl.BlockSpec((None, seg * BAND, B_WIDTH), lambda r, j: (r, j, 0))
    halo = pl.BlockSpec((None, BAND, B_WIDTH), lambda r, j: (r, jnp.maximum(j * seg - 1, 0), 0))
    return main, halo


def _attn_fwd(q, k, v, name, gather=None):
    dil, n, _ = q.shape
    seg, nseg = _segments(n)
    nh = 2 if nseg > 1 else 0
    ng = 0 if gather is None else len(gather)

    def body(*refs):
        q_ref, k_ref, v_ref = refs[:3]
        halos = refs[3:3 + nh]
        o_ref, l_ref = refs[3 + nh + ng:5 + nh + ng]
        if ng:
            start, relay, finish = _gather_steps(refs[3 + nh:3 + nh + ng], refs[5 + nh + ng:5 + nh + 2 * ng],
                                                 *refs[5 + nh + 2 * ng:])
            first, last = _grid_edges((dil, nseg))
            pl.when(first)(start)
        own, both, head = _segment_masks(pl.program_id(1))
        lo, _ = _head_masks()

        def block(rows, keys_of, valid):
            for sl in _lane_blocks(B_WIDTH):
                kk, vv = keys_of(sl)
                sc = jnp.where(valid, _dot(_stack_heads(q_ref[rows, sl]), kk, NT), NEG_INF)
                mx = jnp.max(sc, axis=1, keepdims=True)
                p = jnp.exp(sc - mx)
                den = jnp.sum(p, axis=1, keepdims=True)
                out = _dot(p.astype(BF16), vv, NN) / den
                lse = mx + jnp.log(den)
                o_ref[rows, sl] = jnp.where(lo, out[:BAND], out[BAND:]).astype(BF16)
                l_ref[rows, sl] = jnp.where(lo, lse[:BAND], lse[BAND:])

        if nh:
            block(_block_rows(0), lambda sl: (jnp.concatenate([halos[0][:, sl], k_ref[0:BAND, sl]], axis=0),
                                              jnp.concatenate([halos[1][:, sl], v_ref[0:BAND, sl]], axis=0)), head)
        else:
            block(_block_rows(0), lambda sl: (k_ref[0:BAND, sl], v_ref[0:BAND, sl]), own)

        @pl.loop(1, seg)
        def _(g):
            block(_block_rows(g), lambda sl: (k_ref[_key_rows(g), sl], v_ref[_key_rows(g), sl]), both)

        if ng:
            @pl.when(last)
            def _():
                relay()
                finish()

    main, halo = _segment_specs(seg)
    res = pl.pallas_call(
        body, grid=(dil, nseg), in_specs=[main] * 3 + [halo] * nh + [ANY] * ng, out_specs=[main, main] + [ANY] * ng,
        out_shape=[jax.ShapeDtypeStruct((dil, n, B_WIDTH), BF16), jax.ShapeDtypeStruct((dil, n, B_WIDTH), F32)]
        + _gathered_shapes(gather or []),
        scratch_shapes=_gather_sems(ng) if ng else [],
        compiler_params=_cparams(*(["arbitrary"] * 2 if ng else ["parallel"] * 2)), name=name)(
            q, k, v, *([k, v] if nh else []), *(gather or []))
    return res[0], res[1], list(res[2:])


def _attn_bwd(q, k, v, do, lse, delta, name, scatter=None):
    dil, n, _ = q.shape
    seg, nseg = _segments(n)
    nh = 2 if nseg > 1 else 0
    ns = 0 if scatter is None else len(scatter[0])

    def body(*refs):
        q_ref, k_ref, v_ref, do_ref, lse_ref, dl_ref = refs[:6]
        halos = refs[6:6 + nh]
        dq_ref, dk_ref, dv_ref = refs[6 + nh + ns:9 + nh + ns]
        halo_out = refs[9 + nh + ns:9 + 2 * nh + ns]
        ck_ref, cv_ref = refs[9 + 2 * nh + 2 * ns:11 + 2 * nh + 2 * ns]
        if ns:
            start, finish = _scatter_steps(refs[6 + nh:6 + nh + ns], refs[9 + 2 * nh + ns:9 + 2 * nh + 2 * ns],
                                           *refs[11 + 2 * nh + 2 * ns:], scatter[1])
            first, last = _grid_edges((dil, nseg))
            pl.when(first)(start)
        own, both, head = _segment_masks(pl.program_id(1))
        lo, _ = _head_masks()
        lane = lax.broadcasted_iota(jnp.int32, (1, LANES), 1)

        def per_head(t):
            return jnp.concatenate(
                [jnp.sum(jnp.where(lane == first, t, 0.0), axis=1, keepdims=True) for first in (0, HEAD_DIM)], axis=0)

        def grads(rows, kk, vv, valid, sl):
            q2 = _stack_heads(q_ref[rows, sl])
            do2 = _stack_heads(do_ref[rows, sl])
            p = jnp.where(valid, jnp.exp(_dot(q2, kk, NT) - per_head(lse_ref[rows, sl])), 0.0)
            ds = (p * (_dot(do2, vv, NT) - per_head(dl_ref[rows, sl]))).astype(BF16)
            dq = _dot(ds, kk, NN)
            dq_ref[rows, sl] = jnp.where(lo, dq[:BAND], dq[BAND:]).astype(BF16)
            return _dot(ds, q2, TN), _dot(p.astype(BF16), do2, TN)

        for sl in _lane_blocks(B_WIDTH):
            if nh:
                dkk, dvv = grads(_block_rows(0), jnp.concatenate([halos[0][:, sl], k_ref[0:BAND, sl]], axis=0),
                                 jnp.concatenate([halos[1][:, sl], v_ref[0:BAND, sl]], axis=0), head, sl)
                halo_out[0][:, sl], halo_out[1][:, sl] = dkk[:BAND], dvv[:BAND]
                ck_ref[:, sl], cv_ref[:, sl] = dkk[BAND:], dvv[BAND:]
            else:
                ck_ref[:, sl], cv_ref[:, sl] = grads(_block_rows(0), k_ref[0:BAND, sl], v_ref[0:BAND, sl], own, sl)

        @pl.loop(1, seg)
        def _(g):
            before = _block_rows(g - 1)
            for sl in _lane_blocks(B_WIDTH):
                dkk, dvv = grads(_block_rows(g), k_ref[_key_rows(g), sl], v_ref[_key_rows(g), sl], both, sl)
                dk_ref[before, sl] = (ck_ref[:, sl] + dkk[:BAND]).astype(BF16)
                dv_ref[before, sl] = (cv_ref[:, sl] + dvv[:BAND]).astype(BF16)
                ck_ref[:, sl] = dkk[BAND:]
                cv_ref[:, sl] = dvv[BAND:]

        final = pl.ds((seg - 1) * BAND, BAND)
        dk_ref[final, :] = ck_ref[...].astype(BF16)
        dv_ref[final, :] = cv_ref[...].astype(BF16)

        if ns:
            pl.when(last)(finish)

    main, halo = _segment_specs(seg)
    shape = jax.ShapeDtypeStruct((dil, n, B_WIDTH), BF16)
    halo_shape = jax.ShapeDtypeStruct((dil, nseg, BAND, B_WIDTH), F32)
    halo_spec = pl.BlockSpec((None, None, BAND, B_WIDTH), lambda r, j: (r, j, 0, 0))
    res = pl.pallas_call(
        body, grid=(dil, nseg), in_specs=[main] * 6 + [halo] * nh + [ANY] * ns,
        out_specs=[main] * 3 + [halo_spec] * nh + [ANY] * ns,
        out_shape=[shape] * 3 + [halo_shape] * nh + (_scattered_shapes(scatter[1]) if ns else []),
        scratch_shapes=[pltpu.VMEM((BAND, B_WIDTH), F32)] * 2 + (_scatter_sems(ns) if ns else []),
        compiler_params=_cparams(*(["arbitrary"] * 2 if ns else ["parallel"] * 2)), name=name)(
            q, k, v, do, lse, delta, *([k, v] if nh else []), *(scatter[0] if ns else []))
    return res[0], res[1], res[2], (tuple(res[3:3 + nh]) if nh else None), list(res[3 + nh:])


def _attn_combine(outs, lses, gb, mixed, name, gather=None):
    s = mixed.shape[0]
    npat = len(DILATIONS)
    w = B_WIDTH
    ng = 0 if gather is None else len(gather)

    def body(*refs):
        o_refs, l_refs = refs[:npat], refs[npat:2 * npat]
        g_ref = refs[2 * npat]
        ob_ref = refs[2 * npat + 2 + ng]
        lse_refs = refs[2 * npat + 3 + ng:3 * npat + 3 + ng]
        mb_ref = refs[3 * npat + 3 + ng]
        stage = refs[3 * npat + 4 + 2 * ng]
        if ng:
            start, relay, finish = _gather_steps(refs[2 * npat + 2:2 * npat + 2 + ng],
                                                 refs[3 * npat + 4 + ng:3 * npat + 4 + 2 * ng],
                                                 *refs[3 * npat + 5 + 2 * ng:])
            first, last = _grid_edges((s // TR,))
            pl.when(first)(start)
        os_ = [o_refs[0][...].astype(F32)] + [_load_classes(r, stage, d) for r, d in zip(o_refs[1:], CLASS_DILS)]
        ls = [l_refs[0][...]] + [_load_classes(r, stage, d) for r, d in zip(l_refs[1:], CLASS_DILS)]
        mx = functools.reduce(jnp.maximum, ls)
        ws = [jnp.exp(l - mx) for l in ls]
        tot = functools.reduce(lambda a, b: a + b, ws)
        ob = functools.reduce(lambda a, b: a + b, [wt / tot * o for wt, o in zip(ws, os_)])
        ob_ref[...] = ob
        lse = mx + jnp.log(tot)
        _stage_put(stage, lse)
        lse_refs[0][...] = lse
        for ref, d in zip(lse_refs[1:], CLASS_DILS):
            _store_classes(stage, ref, d)
        mb_ref[...] = (ob * _rsq_mean(ob) * g_ref[...]).astype(BF16)

        if ng:
            @pl.when(last)
            def _():
                relay()
                finish()

    lay_specs = [_row_spec(w)] + [_class_spec(d) for d in CLASS_DILS]
    res = pl.pallas_call(
        body, grid=(s // TR,), in_specs=lay_specs * 2 + [_vec_spec(w), ANY] + [ANY] * ng,
        out_specs=[_row_spec(w)] + lay_specs + [_row_spec(w, 1)] + [ANY] * ng,
        out_shape=[jax.ShapeDtypeStruct((s, w), F32), jax.ShapeDtypeStruct((s, w), F32)]
        + [_class_shape(s, d, F32) for d in CLASS_DILS] + [jax.ShapeDtypeStruct(mixed.shape, mixed.dtype)]
        + _gathered_shapes(gather or []),
        scratch_shapes=[STAGE] + (_gather_sems(ng) if ng else []), input_output_aliases={2 * npat + 1: npat + 1},
        compiler_params=_cparams("arbitrary" if ng else "parallel"), name=name)(*outs, *lses, gb, mixed,
                                                                              *(gather or []))
    return res[0], dict(zip(DILATIONS, res[1:npat + 1])), res[npat + 1], list(res[npat + 2:])


def _attn_bwd_prep(dmixed, ob, gb, name):
    s = ob.shape[0]
    w = B_WIDTH
    nlay = len(DILATIONS)

    def body(dm_ref, ob_ref, g_ref, *rest):
        do_refs, dl_refs = rest[:nlay], rest[nlay:2 * nlay]
        dg_ref, stage = rest[2 * nlay:]
        _acc_init([dg_ref])
        ob = ob_ref[...]
        dob, dgt = _rms_bwd(ob, _rsq_mean(ob), g_ref[...], dm_ref[...])
        dg_ref[...] += _colsum(dgt)
        _stage_put(stage, dob)
        do_refs[0][...] = dob.astype(BF16)
        for ref, d in zip(do_refs[1:], CLASS_DILS):
            _store_classes(stage, ref, d)
        lo, hi = _head_masks()
        t = dob * ob
        for b, sl in enumerate(_lane_blocks(w)):
            tb = t[:, sl]
            s0 = jnp.sum(jnp.where(lo, tb, 0.0), axis=1, keepdims=True)
            s1 = jnp.sum(jnp.where(hi, tb, 0.0), axis=1, keepdims=True)
            stage[b] = jnp.where(lo, s0, s1)
        dl_refs[0][...] = _stage_get(stage)
        for ref, d in zip(dl_refs[1:], CLASS_DILS):
            _store_classes(stage, ref, d)

    lay_specs = [_row_spec(w)] + [_class_spec(d) for d in CLASS_DILS]
    shapes = lambda dt: [jax.ShapeDtypeStruct((s, w), dt)] + [_class_shape(s, d, dt) for d in CLASS_DILS]
    res = pl.pallas_call(
        body, grid=(s // TR,), in_specs=[_row_spec(w, 1), _row_spec(w), _vec_spec(w)],
        out_specs=lay_specs * 2 + [_vec_spec(w)],
        out_shape=shapes(BF16) + shapes(F32) + [jax.ShapeDtypeStruct((1, w), F32)],
        scratch_shapes=[STAGE],
        compiler_params=_cparams("arbitrary"), name=name)(dmixed, ob, gb)
    return dict(zip(DILATIONS, res[:nlay])), dict(zip(DILATIONS, res[nlay:2 * nlay])), res[2 * nlay]


def _rope_bwd(dqs, dks, dvs, halos, tabs, dproj, name):
    s = dproj.shape[0]
    half = ROT_DIM // 2
    scale = HEAD_DIM ** -0.5
    npat = len(DILATIONS)
    w = B_WIDTH
    nseg = halos[0].shape[0]
    per = s // nseg // TR

    def body(*refs):
        groups = [refs[g * npat:(g + 1) * npat] for g in range(3)]
        halo_refs = (None,) + tuple(refs[3 * npat:3 * npat + 2])
        c_ref, s1_ref, s2_ref, _, o_ref, stage = refs[3 * npat + 2:]
        i = pl.program_id(0)
        at_edge = ((i + 1) % per == 0) & ((i + 1) // per < nseg)

        def total(rs, halo_ref=None):
            acc = rs[0][...].astype(F32)
            if halo_ref is not None:
                edge = jnp.concatenate([jnp.zeros((TR - BAND, w), F32), halo_ref[...]], axis=0)
                acc = acc + jnp.where(at_edge, edge, 0.0)
            for ref, d in zip(rs[1:], CLASS_DILS):
                acc = acc + _load_classes(ref, stage, d)
            return acc

        def unrope(g):
            c, s1, s2 = c_ref[...], s1_ref[...], s2_ref[...]
            for sl in _lane_blocks(w):
                gb = g[:, sl]
                o = gb * c + pltpu.roll(gb * s1, half, 1) + pltpu.roll(gb * s2, LANES - half, 1)
                o_ref[:, sl] = o.astype(BF16)

        which = pl.program_id(1)

        @pl.when(which == 0)
        def _():
            unrope(total(groups[0]) * scale)

        @pl.when(which == 1)
        def _():
            unrope(total(groups[1], halo_refs[1]))

        @pl.when(which == 2)
        def _():
            o_ref[...] = total(groups[2], halo_refs[2]).astype(BF16)

    tab = pl.BlockSpec((TR, LANES), lambda i, j: (i, 0))
    nat = pl.BlockSpec((TR, w), lambda i, j: (i, 0))
    lay_specs = [nat] + [_class_spec(d) for d in CLASS_DILS]
    edge_spec = pl.BlockSpec((None, BAND, w), lambda i, j: (jnp.minimum((i + 1) // per, nseg - 1), 0, 0))
    first_col = 2 * A_WIDTH // w
    return pl.pallas_call(
        body, grid=(s // TR, 3), in_specs=lay_specs * 3 + [edge_spec] * 2 + [tab] * 3 + [ANY],
        out_specs=pl.BlockSpec((TR, w), lambda i, j: (i, first_col + j)),
        out_shape=jax.ShapeDtypeStruct(dproj.shape, dproj.dtype), scratch_shapes=[STAGE],
        input_output_aliases={3 * npat + 5: 0},
        compiler_params=_cparams("parallel", "arbitrary"), name=name)(*dqs, *dks, *dvs, *halos, *tabs, dproj)


TK = 512
HALO = 16
FFN_ROWS = 256
FFN_CHUNKS = tuple(slice(r, r + FFN_ROWS) for r in range(0, TM, FFN_ROWS))


def _row_of(v, r):
    rows = lax.broadcasted_iota(jnp.int32, (v.shape[0], 1), 0)
    return jnp.sum(jnp.where(rows == r, v, 0.0), axis=0, keepdims=True)


def _taps_before(x, halo):
    row = lax.broadcasted_iota(jnp.int32, (x.shape[0], 1), 0)
    m1 = jnp.where(row == 0, _row_of(halo, HALO - 1), pltpu.roll(x, 1, 0))
    m2 = jnp.where(row == 0, _row_of(halo, HALO - 2), jnp.where(row == 1, _row_of(halo, HALO - 1), pltpu.roll(x, 2, 0)))
    return m2, m1, x


def _taps_after(x, halo):
    rows = x.shape[0]
    row = lax.broadcasted_iota(jnp.int32, (rows, 1), 0)
    p1 = jnp.where(row == rows - 1, _row_of(halo, 0), pltpu.roll(x, rows - 1, 0))
    p2 = jnp.where(row == rows - 2, _row_of(halo, 0), jnp.where(row == rows - 1, _row_of(halo, 1), pltpu.roll(x, rows - 2, 0)))
    return p1, p2


def _conv_value(taps, cw_ref, cb_ref, h):
    return cb_ref[h] + cw_ref[h, 0:1, :] * taps[0] + cw_ref[h, 1:2, :] * taps[1] + cw_ref[h, 2:3, :] * taps[2]


def _ffn_weight_specs(ncol):
    per_up = (2 * D_FF // N_CHIPS) // TK
    per_dn = (D_FF // N_CHIPS) // TK
    wg = pl.BlockSpec((None, None, D_MODEL, TK), lambda i, j: (j // per_up, 0, 0, j % per_up))
    wv = pl.BlockSpec((None, None, D_MODEL, TK), lambda i, j: ((j + ncol) // per_up, 0, 0, (j + ncol) % per_up))
    wd = pl.BlockSpec((None, None, TK, D_MODEL), lambda i, j: (j // per_dn, 0, j % per_dn, 0))
    cw = pl.BlockSpec((2, 3, TK), lambda i, j: (0, 0, j))
    cb = pl.BlockSpec((2, 1, TK), lambda i, j: (0, 0, j))
    return wg, wv, wd, cw, cb


def _ffn_forward(h2, w_up, w_down, cw3, cb3, name, gather=None, post=None):
    s = h2.shape[0]
    nm, ncol = s // TM, D_FF // TK
    ng = 0 if gather is None else len(gather)
    npost = 0 if post is None else 3
    nout = 4 + (2 if post else 0)

    def body(*refs):
        h_ref, wg_ref, wv_ref, wd_ref, cw_ref, cb_ref = refs[:6]
        post_in = refs[6:6 + npost]
        g_in = refs[6 + npost:6 + npost + ng]
        outs = refs[6 + npost + ng:6 + npost + ng + nout]
        y_ref, up_ref, cv_ref, f_ref = outs[:4]
        g_out = refs[6 + npost + ng + nout:6 + npost + 2 * ng + nout]
        carry = refs[6 + npost + 2 * ng + nout]
        i, j = pl.program_id(0), pl.program_id(1)
        if ng:
            start, relay, finish = _gather_steps(g_in, g_out, *refs[7 + npost + 2 * ng + nout:])
            pl.when((i == 0) & (j == 0))(start)
            pl.when((i == nm - 1) & (j == 0))(relay)

        @pl.when((i == 0) & (j == 0))
        def _():
            carry[...] = jnp.zeros_like(carry)

        @pl.when(j == 0)
        def _():
            f_ref[...] = jnp.zeros_like(f_ref)

        ups = []
        for rs in FFN_CHUNKS:
            hc = h_ref[rs, :]
            ups.append([_dot(hc, w_ref[...], NN).astype(BF16) for w_ref in (wg_ref, wv_ref)])
            for hh in range(2):
                up_ref[hh, rs, :] = ups[-1][hh]
        before = [carry[j, hh] for hh in range(2)]
        for rs, up in zip(FFN_CHUNKS, ups):
            conv = []
            for hh in range(2):
                x = up[hh].astype(F32)
                conv.append(_conv_value(_taps_before(x, before[hh]), cw_ref, cb_ref, hh))
                cv_ref[hh, rs, :] = conv[hh].astype(BF16)
                before[hh] = x[x.shape[0] - HALO:, :]
            y = (_gelu_tanh(conv[0])[0] * conv[1]).astype(BF16)
            y_ref[rs, :] = y
            f_ref[rs, :] += _dot(y, wd_ref[...], NN)
        for hh in range(2):
            carry[j, hh] = before[hh]

        @pl.when(j == ncol - 1)
        def _():
            if post:
                f = f_ref[...]
                x1_ref, gp_ref, gn_ref = post_in
                x2 = x1_ref[...] + f * _rsq_mean(f) * gp_ref[...]
                outs[4][...] = x2
                outs[5][...] = (x2 * _rsq_mean(x2) * gn_ref[...]).astype(BF16)

        if ng:
            pl.when((i == nm - 1) & (j == ncol - 1))(finish)

    wg, wv, wd, cw, cb = _ffn_weight_specs(ncol)
    row = pl.BlockSpec((TM, D_MODEL), lambda i, j: (i, 0))
    vec = pl.BlockSpec((1, D_MODEL), lambda i, j: (0, 0))
    res = pl.pallas_call(
        body, grid=(nm, ncol),
        in_specs=[row, wg, wv, wd, cw, cb] + ([row, vec, vec] if post else []) + [ANY] * ng,
        out_specs=[pl.BlockSpec((TM, TK), lambda i, j: (i, j)), pl.BlockSpec((2, TM, TK), lambda i, j: (0, i, j)),
                   pl.BlockSpec((2, TM, TK), lambda i, j: (0, i, j)), row] + ([row, row] if post else [])
        + [ANY] * ng,
        out_shape=[jax.ShapeDtypeStruct((s, D_FF), BF16), jax.ShapeDtypeStruct((2, s, D_FF), BF16),
                   jax.ShapeDtypeStruct((2, s, D_FF), BF16), jax.ShapeDtypeStruct((s, D_MODEL), F32)]
        + ([jax.ShapeDtypeStruct((s, D_MODEL), F32), jax.ShapeDtypeStruct((s, D_MODEL), BF16)] if post else [])
        + _gathered_shapes(gather or []),
        scratch_shapes=[pltpu.VMEM((ncol, 2, HALO, TK), F32)] + (_gather_sems(ng) if ng else []),
        compiler_params=_cparams("arbitrary", "arbitrary"), name=name)(h2, w_up, w_up, w_down, cw3, cb3,
                                                                      *(post or []), *(gather or []))
    return res[:nout], list(res[nout:])


def _ffn_backward(df, w_up, w_down, up3, cv3, cw3, name, scatter=None):
    s = df.shape[0]
    nm, ncol = s // TM, D_FF // TK
    ns = 0 if scatter is None else len(scatter[0])

    def body(*refs):
        df_ref, wg_ref, wv_ref, wd_ref, cw_ref, up_ref, cv_ref = refs[:7]
        s_in = refs[7:7 + ns]
        dup_ref, dh_ref, sums_ref = refs[7 + ns:10 + ns]
        s_out = refs[10 + ns:10 + 2 * ns]
        carry = refs[10 + 2 * ns]
        i, j = pl.program_id(0), pl.program_id(1)
        if ns:
            start, finish = _scatter_steps(s_in, s_out, *refs[11 + 2 * ns:], scatter[1])
            pl.when((i == 0) & (j == 0))(start)

        @pl.when((i == 0) & (j == 0))
        def _():
            carry[...] = jnp.zeros_like(carry)
            sums_ref[...] = jnp.zeros_like(sums_ref)

        @pl.when(j == 0)
        def _():
            dh_ref[...] = jnp.zeros_like(dh_ref)

        chunks = FFN_CHUNKS[::-1]
        dys = [_dot(df_ref[rs, :], wd_ref[...], NT) for rs in chunks]
        row = lax.broadcasted_iota(jnp.int32, (8, 1), 0)
        after = [carry[j, hh] for hh in range(2)]
        upd = [jnp.zeros((8, TK), F32) for _ in range(2)]
        for rs, dy in zip(chunks, dys):
            act, grad = _gelu_tanh(cv_ref[0, rs, :].astype(F32))
            dcs = (dy * cv_ref[1, rs, :].astype(F32) * grad, dy * act)
            part = dh_ref[rs, :]
            for hh, w_ref in ((0, wg_ref), (1, wv_ref)):
                dc = dcs[hh]
                x = up_ref[hh, rs, :].astype(F32)
                after1, after2 = _taps_after(dc, after[hh])
                for ridx, sm in enumerate((_colsum(after2 * x), _colsum(after1 * x), _colsum(dc * x), _colsum(dc))):
                    upd[hh] = upd[hh] + jnp.where(row == ridx, sm, 0.0)
                dup = (cw_ref[hh, 2:3, :] * dc + cw_ref[hh, 1:2, :] * after1 + cw_ref[hh, 0:1, :] * after2).astype(BF16)
                after[hh] = dc[:HALO, :]
                dup_ref[hh, rs, :] = dup
                part = part + _dot(dup, w_ref[...], NT)
            dh_ref[rs, :] = part
        for hh in range(2):
            sums_ref[j, hh] += upd[hh]
            carry[j, hh] = after[hh]

        if ns:
            pl.when((i == nm - 1) & (j == ncol - 1))(finish)

    wg, wv, wd, cw, _ = _ffn_weight_specs(ncol)
    rev = lambda i: nm - 1 - i
    res = pl.pallas_call(
        body, grid=(nm, ncol),
        in_specs=[pl.BlockSpec((TM, D_MODEL), lambda i, j: (rev(i), 0)), wg, wv, wd, cw,
                  pl.BlockSpec((2, TM, TK), lambda i, j: (0, rev(i), j)),
                  pl.BlockSpec((2, TM, TK), lambda i, j: (0, rev(i), j))] + [ANY] * ns,
        out_specs=[pl.BlockSpec((2, TM, TK), lambda i, j: (0, rev(i), j)),
                   pl.BlockSpec((TM, D_MODEL), lambda i, j: (rev(i), 0)),
                   pl.BlockSpec((ncol, 2, 8, TK), lambda i, j: (0, 0, 0, 0))] + [ANY] * ns,
        out_shape=[jax.ShapeDtypeStruct((2, s, D_FF), BF16), jax.ShapeDtypeStruct((s, D_MODEL), F32),
                   jax.ShapeDtypeStruct((ncol, 2, 8, TK), F32)] + (_scattered_shapes(scatter[1]) if ns else []),
        scratch_shapes=[pltpu.VMEM((ncol, 2, HALO, TK), F32)] + (_scatter_sems(ns) if ns else []),
        compiler_params=_cparams("arbitrary", "arbitrary"), name=name)(df, w_up, w_up, w_down, cw3, up3, cv3,
                                                                      *(scatter[0] if ns else []))
    return res[:3], list(res[3:])


def _wspec(rows, cols, index_map):
    return pl.BlockSpec((None, None, rows, cols), index_map)


def _layer_forward(l, x0, h1, p, wg, tabs, gather=None, late=None, g_next=None):
    s = x0.shape[0]
    nm = s // TMM
    tag = f"_l{l}"
    riders = dict.fromkeys(DILATIONS)
    proj_rider = rope_rider = combine_rider = None
    if late is not None:
        cols = lambda t, parts: [t[:, i * t.shape[1] // parts:(i + 1) * t.shape[1] // parts] for i in range(parts)]
        (down_a, down_b), up_q = cols(late["w_down"], 2), cols(late["w_up"], 4)
        proj_rider, rope_rider, combine_rider = [late["w_out"], down_a], [up_q[2]], [up_q[3]]
        riders = dict(zip(DILATIONS, ([down_b], [up_q[0]], [up_q[1]])))
    proj = _matmul(
        h1, wg["w_in"], grid=(nm, N_CHIPS), a_spec=pl.BlockSpec((TMM, D_MODEL), lambda i, j: (i, 0)),
        b_spec=_wspec(D_MODEL, IN_COLS // N_CHIPS, lambda i, j: (j, 0, 0, 0)),
        o_spec=pl.BlockSpec((TMM, IN_COLS // N_CHIPS), lambda i, j: (i, j)), o_shape=(s, IN_COLS), o_dtype=BF16,
        dims=NN, nk=1, kaxis=None, acc_shape=None, name="proj" + tag, gather=proj_rider)
    if late is not None:
        proj, (w_out_all4, down_a) = proj
    ma = _mixer_a_fwd(proj, p["v_norm_g"], p["v_norm_b"], p["w_spatial"], p["bs_full"], p["out_norm_a"],
                      "mixer_a_fwd" + tag)
    q, k, v, rope_landed = _rope_fwd(proj, tabs, "rope_fwd" + tag, rope_rider)
    outs, lses, landed = zip(*[
        _attn_fwd(_as_classes(q[d]), _as_classes(k[d]), _as_classes(v[d]), f"attn_fwd_d{d}" + tag, riders[d])
        for d in DILATIONS])
    outs = [o.reshape(s, B_WIDTH) if d == 1 else o for o, d in zip(outs, DILATIONS)]
    lses = [t.reshape(s, B_WIDTH) if d == 1 else t for t, d in zip(lses, DILATIONS)]
    ob, lse, mixed, combine_landed = _attn_combine(outs, lses, p["out_norm_b"], ma, "attn_combine" + tag,
                                                   combine_rider)
    if late is not None:
        wg = dict(wg, w_out=w_out_all4, w_down=jnp.concatenate([down_a, landed[0][0]], axis=-1),
                  w_up=jnp.concatenate([landed[1][0], landed[2][0], rope_landed[0], combine_landed[0]], axis=-1))
    (y1, x1, h2), next_in = _mix_out_norm(mixed, wg["w_out"], x0, p["post_mix_norm"], p["pre_ffn_norm"],
                                          "mix_out" + tag, [gather["w_in"]] if gather else None)
    post = None if g_next is None else (x1, p["post_ffn_norm"], g_next)
    (y, up3, cv3, f, *after), next_ffn = _ffn_forward(h2, wg["w_up"], wg["w_down"], p["cw3"], p["cb3"], "ffn_fwd" + tag,
                                                      [gather[n] for n in ("w_up", "w_down", "w_out")] if gather else None,
                                                      post)
    gathered = dict(zip(("w_up", "w_down", "w_out"), next_ffn), w_in=next_in[0]) if gather else None
    saved = dict(x0=x0, h1=h1, proj=proj, q=q, k=k, v=v, ob=ob, lse=lse, mixed=mixed, y1=y1, x1=x1, h2=h2,
                 up3=up3, cv3=cv3, y=y, f=f)
    if after:
        saved.update(x2=after[0], h_next=after[1])
    return saved, gathered, wg


def _layer_backward(l, dx2, df, sv, p, wg, tabs, pos, scatter=None, hide=False):
    s = dx2.shape[0]
    nm = s // TMM
    tag = f"_l{l}"
    g = {}
    (dup3, dh2, conv_sums), scattered = _ffn_backward(df, wg["w_up"], wg["w_down"], sv["up3"], sv["cv3"], p["cw3"],
                                                      "ffn_bwd" + tag, scatter)
    sums = conv_sums.transpose(1, 2, 0, 3).reshape(2, 8, D_FF)
    g["conv_w"] = jnp.concatenate([sums[0, :3], sums[1, :3]], axis=1)
    g["conv_b"] = jnp.concatenate([sums[0, 3:4], sums[1, 3:4]], axis=1)
    tn = 1024
    done = {}
    gw_down = _matmul(
        sv["y"], df, grid=(D_FF // tn,), a_spec=pl.BlockSpec((s, tn), lambda k: (0, k)),
        b_spec=pl.BlockSpec((s, D_MODEL), lambda k: (0, 0)),
        o_spec=pl.BlockSpec((2, tn, D_MODEL // 2), lambda k: (0, k, 0)),
        o_shape=(2, D_FF, D_MODEL // 2), o_dtype=BF16,
        dims=TN, nk=1, kaxis=None, acc_shape=None, name="w_down_grad" + tag, halves=True)
    pair_sum = lambda n, grad, recv: _pair_sum({n: grad}, recv, pos, (n,), f"pair_sum_l{l}")
    gw_up, received = _matmul(
        sv["h2"], dup3, grid=(2 * D_FF // tn,), a_spec=pl.BlockSpec((s, D_MODEL), lambda n: (0, 0)),
        b_spec=pl.BlockSpec((None, s, tn), lambda n: (n // (D_FF // tn), 0, n % (D_FF // tn))),
        o_spec=pl.BlockSpec((None, D_MODEL, tn), lambda n: (n // 2, 0, n % 2)),
        o_shape=(N_CHIPS, D_MODEL, 2 * D_FF // N_CHIPS), o_dtype=BF16,
        dims=TN, nk=1, kaxis=None, acc_shape=None, name="w_up_grad" + tag, scatter=([gw_down], ("x:w_down",)))
    down_sums = pair_sum("w_down", gw_down, received)
    dx1, dy1, g["pre_ffn_norm"], g["post_mix_norm"] = _norm_bwd_mid(
        dx2, dh2, sv["x1"], p["pre_ffn_norm"], sv["y1"], p["post_mix_norm"], "norm_bwd_mid" + tag)
    w_out_all = pl.BlockSpec((N_CHIPS, None, D_MODEL // N_CHIPS, D_MODEL), lambda i: (0, 0, 0, 0))
    dmixed, received = _matmul(
        dy1, wg["w_out"], grid=(nm,), a_spec=pl.BlockSpec((TMM, D_MODEL), lambda i: (i, 0)), b_spec=w_out_all,
        o_spec=pl.BlockSpec((TMM, D_MODEL), lambda i: (i, 0)), o_shape=(s, D_MODEL), o_dtype=F32,
        dims=NT, nk=1, kaxis=None, acc_shape=None, name="mix_out_bwd" + tag, b_2d=(D_MODEL, D_MODEL),
        scatter=([gw_up], ("x:w_up",)))
    up_sums = pair_sum("w_up", gw_up, received)
    gw_out = _matmul(
        sv["mixed"], dy1, grid=(1,), a_spec=pl.BlockSpec((s, D_MODEL), lambda m: (0, 0)),
        b_spec=pl.BlockSpec((s, D_MODEL), lambda m: (0, 0)),
        o_spec=pl.BlockSpec((2, D_MODEL, D_MODEL // 2), lambda m: (0, 0, 0)),
        o_shape=(2, D_MODEL, D_MODEL // 2), o_dtype=BF16,
        dims=TN, nk=1, kaxis=None, acc_shape=None, name="w_out_grad" + tag, halves=True)
    dpa, g["out_norm_a"], g["v_norm_g"], g["v_norm_b"], dbs, g["w_spatial"], received = _mixer_a_bwd(
        sv["proj"], dmixed, p["v_norm_g"], p["v_norm_b"], p["w_spatial"], p["bs_full"], p["out_norm_a"],
        "mixer_a_bwd" + tag, ([gw_out], ("x:w_out",)))
    out_sums = pair_sum("w_out", gw_out, received)
    g["b_spatial"] = dbs[:, ::GROUP_DIM].T
    dob, delta, g["out_norm_b"] = _attn_bwd_prep(dmixed, sv["ob"], p["out_norm_b"], "attn_bwd_prep" + tag)
    riders = dict(zip(DILATIONS, ((down_sums, ("w_down",)), (up_sums, ("w_up:0",)), (up_sums, ("w_up:1",))))) if hide else {}
    dqs, dks, dvs, edges, received = zip(*[
        _attn_bwd(*(_as_classes(t[d]) for t in (sv["q"], sv["k"], sv["v"], dob, sv["lse"], delta)),
                  f"attn_bwd_d{d}" + tag, riders.get(d))
        for d in DILATIONS])
    if hide:
        done[("w_down",)] = (down_sums, received[0])
        done[("w_up",)] = (up_sums, [jnp.concatenate([received[1][0], received[2][0]], axis=-1)])
    nat = lambda ts: [t.reshape(s, B_WIDTH) if d == 1 else t for t, d in zip(ts, DILATIONS)]
    halos = [t[0] for t in edges[0]]
    dproj = _rope_bwd(nat(dqs), nat(dks), nat(dvs), halos, tabs, dpa, "rope_bwd" + tag)
    wcol = IN_COLS // N_CHIPS
    gw_in = _matmul(
        sv["h1"], dproj, grid=(N_CHIPS,), a_spec=pl.BlockSpec((s, D_MODEL), lambda n: (0, 0)),
        b_spec=pl.BlockSpec((s, wcol), lambda n: (0, n)),
        o_spec=pl.BlockSpec((None, D_MODEL, wcol), lambda n: (n, 0, 0)),
        o_shape=(N_CHIPS, D_MODEL, wcol), o_dtype=BF16,
        dims=TN, nk=1, kaxis=None, acc_shape=None, name="w_in_grad" + tag,
        scatter=(out_sums, ("w_out",)) if hide else None)
    if hide:
        gw_in, received = gw_in
        done[("w_out",)] = (out_sums, received)
        in_sums = _chip_sums(l, dict(w_in=gw_in), pos, ("w_in",))
        dh1, received = _proj_bwd(dproj, wg["w_in"], "proj_bwd" + tag, (in_sums, ("w_in",)))
        done[("w_in",)] = (in_sums, received)
        return dx1, dh1, {}, g, scattered, done
    dh1, received = _proj_bwd(dproj, wg["w_in"], "proj_bwd" + tag, ([gw_in], ("x:w_in",)))
    sums = dict(w_in=pair_sum("w_in", gw_in, received)[0], w_up=up_sums[0], w_out=out_sums[0], w_down=down_sums[0])
    return dx1, dh1, sums, g, scattered, done


SMALL = ("pre_mix_norm", "v_norm_g", "v_norm_b", "w_spatial", "b_spatial", "out_norm_a", "out_norm_b",
         "post_mix_norm", "pre_ffn_norm", "conv_b", "post_ffn_norm")
BIG = ("w_in", "w_out", "w_up", "w_down")
DEPTH = 2


def _layer_params(l, small, conv_w_full):
    p = {n: small[n][l].reshape(1, -1) for n in SMALL if n not in ("w_spatial", "b_spatial")}
    p["w_spatial"] = small["w_spatial"][l]
    p["bs_full"] = jnp.repeat(small["b_spatial"][l].T, GROUP_DIM, axis=1)
    p["cw3"] = conv_w_full[l].reshape(3, 2, D_FF).transpose(1, 0, 2)
    p["cb3"] = small["conv_b"][l].reshape(2, 1, D_FF)
    return p


def _mesh_pos():
    return lax.axis_index("x"), lax.axis_index("y"), lax.axis_index("c")


def _other_chips(x, y):
    return [(1 - x, y), (x, 1 - y), (1 - x, 1 - y)]


def _gathered_shapes(blocks):
    return [jax.ShapeDtypeStruct((N_CHIPS, 1) + a.shape, a.dtype) for a in blocks]


def _gather_sems(nw):
    n = 2 * nw * (N_CHIPS - 1) + nw
    return [pltpu.SemaphoreType.DMA((n,)), pltpu.SemaphoreType.DMA((n,))]


def _gather_steps(ins, outs, send, recv):
    nw, nrel = len(ins), N_CHIPS - 1
    x, y, c = _mesh_pos()
    mine, sibling, chips = 2 * x + y, (x, y, 1 - c), _other_chips(x, y)

    def copy(src, dst, slot, to):
        return pltpu.make_async_remote_copy(src_ref=src, dst_ref=dst, send_sem=send.at[slot],
                                            recv_sem=recv.at[slot], device_id=to, device_id_type=MESH)

    def half_rows(t, core):
        rows = ins[t].shape[0] // 2
        return pl.ds(pl.multiple_of(core * rows, rows), rows)

    def landing(t, chip, core):
        return outs[t].at[chip, 0, half_rows(t, core), :]

    slots = [(t, r, chip) for t in range(nw) for r, chip in enumerate(chips)]
    own = [copy(ins[t], outs[t].at[mine, 0], 2 * nw * nrel + t, sibling) for t in range(nw)]
    first = [copy(ins[t].at[half_rows(t, c), :], landing(t, mine, c), t * nrel + r, (px, py, c))
             for t, r, (px, py) in slots]
    relays = [copy(landing(t, 2 * px + py, c), landing(t, 2 * px + py, c), nw * nrel + t * nrel + r, sibling)
              for t, r, (px, py) in slots]

    def start():
        for cp in own + first:
            cp.start()

    def relay():
        for (t, r, (px, py)), cp in zip(slots, relays):
            copy(landing(t, 2 * px + py, c), landing(t, 2 * px + py, c), t * nrel + r, (px, py, c)).wait_recv()
            cp.start()

    def finish():
        for t, r, (px, py) in slots:
            passed = landing(t, 2 * px + py, 1 - c)
            copy(passed, passed, nw * nrel + t * nrel + r, sibling).wait_recv()
        for cp in first + relays:
            cp.wait_send()
        for cp in own:
            cp.wait()

    return start, relay, finish


HALF = 512

GRAD_GEOM = {"w_in": ("rows", D_MODEL, IN_COLS // N_CHIPS), "w_up": ("rows", D_MODEL, 2 * D_FF // N_CHIPS),
             "w_out": ("cols", D_MODEL, D_MODEL // N_CHIPS), "w_down": ("cols", D_FF, D_FF // N_CHIPS)}


def _exchange_shape(n):
    kind, a, b = GRAD_GEOM[n]
    return (N_CHIPS, HALF, b) if kind == "rows" else (a, HALF)


def _piece_shape(n):
    name, _, part = n.partition(":")
    kind, _, b = GRAD_GEOM[name]
    if part:
        assert kind == "rows"
        return (HALF, b // 2)
    return (HALF, b) if kind == "rows" else (b, HALF)


def _half_of(ref, n, core):
    if GRAD_GEOM[n][0] == "rows":
        return ref.at[:, pl.ds(pl.multiple_of(core * HALF, HALF), HALF), :]
    return ref.at[core]


def _piece_of(ref, n, chip):
    name, _, part = n.partition(":")
    kind, _, b = GRAD_GEOM[name]
    if part:
        return ref.at[chip, :, pl.ds(int(part) * (b // 2), b // 2)]
    return ref.at[chip] if kind == "rows" else ref.at[pl.ds(pl.multiple_of(chip * b, b), b), :]


def _pair_exchange(g, names, name):
    n = len(names)

    def body(*refs):
        send, recv = refs[2 * n:]
        x, y, c = _mesh_pos()
        o = 1 - c
        cps = [pltpu.make_async_remote_copy(src_ref=_half_of(refs[t], nm, o), dst_ref=refs[n + t], send_sem=send.at[t],
                                            recv_sem=recv.at[t], device_id=(x, y, o), device_id_type=MESH)
               for t, nm in enumerate(names)]
        for cp in cps:
            cp.start()
        for cp in cps:
            cp.wait()

    return pl.pallas_call(
        body, in_specs=[ANY] * n, out_specs=[ANY] * n,
        out_shape=[jax.ShapeDtypeStruct(_exchange_shape(nm), BF16) for nm in names],
        scratch_shapes=[pltpu.SemaphoreType.DMA((n,)), pltpu.SemaphoreType.DMA((n,))],
        name=name)(*[g[nm] for nm in names])


def _pair_sum(g, recv, pos, names, name_prefix):
    def add(a, b, grid, a_spec, b_spec, name):
        def body(pos_ref, a_ref, b_ref, o_ref):
            o_ref[...] = (a_ref[...].astype(F32) + b_ref[...].astype(F32)).astype(BF16)

        return pl.pallas_call(
            body, grid_spec=pltpu.PrefetchScalarGridSpec(
                num_scalar_prefetch=1, grid=grid, in_specs=[a_spec, b_spec], out_specs=b_spec),
            out_shape=jax.ShapeDtypeStruct(b.shape, BF16), compiler_params=_cparams("parallel"), name=name)(pos, a, b)

    out = []
    for nm, r in zip(names, recv):
        kind, rows, width = GRAD_GEOM[nm]
        if kind == "rows":
            out.append(add(g[nm], r, (N_CHIPS,), pl.BlockSpec((None, HALF, width), lambda j, pos: (j, pos[2], 0)),
                           pl.BlockSpec((None, HALF, width), lambda j, pos: (j, 0, 0)), f"{name_prefix}_{nm}"))
        else:
            out.append(add(g[nm], r, (rows // D_MODEL,), pl.BlockSpec((None, D_MODEL, HALF), lambda j, pos: (pos[2], j, 0)),
                           pl.BlockSpec((D_MODEL, HALF), lambda j, pos: (j, 0)), f"{name_prefix}_{nm}"))
    return out


def _scattered_shapes(names):
    return [jax.ShapeDtypeStruct(_exchange_shape(nm[2:]) if nm.startswith("x:") else (N_CHIPS - 1,) + _piece_shape(nm),
                                 BF16) for nm in names]


def _scatter_sems(n):
    return [pltpu.SemaphoreType.DMA((n * (N_CHIPS - 1),)), pltpu.SemaphoreType.DMA((n * (N_CHIPS - 1),))]


def _scatter_steps(sums, outs, send, recv, names):
    nrel = N_CHIPS - 1
    x, y, c = _mesh_pos()
    cps = [pltpu.make_async_remote_copy(
        src_ref=_half_of(sums[t], nm[2:], 1 - c), dst_ref=outs[t], send_sem=send.at[t * nrel],
        recv_sem=recv.at[t * nrel], device_id=(x, y, 1 - c), device_id_type=MESH)
        for t, nm in enumerate(names) if nm.startswith("x:")]
    for r, (px, py) in enumerate(_other_chips(x, y)):
        for t, nm in enumerate(names):
            if nm.startswith("x:"):
                continue
            cps.append(pltpu.make_async_remote_copy(
                src_ref=_piece_of(sums[t], nm, 2 * px + py), dst_ref=outs[t].at[r], send_sem=send.at[t * nrel + r],
                recv_sem=recv.at[t * nrel + r], device_id=(px, py, c), device_id_type=MESH))

    def start():
        for cp in cps:
            cp.start()

    def finish():
        for cp in cps:
            cp.wait()

    return start, finish


def _chip_scatter(sums, names, name):
    n = len(names)

    def body(*refs):
        start, finish = _scatter_steps(refs[:n], refs[n:2 * n], *refs[2 * n:], names)
        start()
        finish()

    return pl.pallas_call(
        body, in_specs=[ANY] * n, out_specs=[ANY] * n, out_shape=_scattered_shapes(names),
        scratch_shapes=_scatter_sems(n), name=name)(*sums)


def _chip_sum(sums, recv, pos, names, name_prefix):
    def add(a, b, a_spec, shape, name):
        def body(pos_ref, a_ref, b_ref, o_ref):
            tot = a_ref[...].astype(F32)
            for r in range(N_CHIPS - 1):
                tot = tot + b_ref[r].astype(F32)
            o_ref[...] = tot

        return pl.pallas_call(
            body, grid_spec=pltpu.PrefetchScalarGridSpec(
                num_scalar_prefetch=1, grid=(1,), in_specs=[a_spec, pl.BlockSpec(b.shape, lambda i, pos: (0, 0, 0))],
                out_specs=pl.BlockSpec((None,) + shape, lambda i, pos: (pos[2], 0, 0))),
            out_shape=jax.ShapeDtypeStruct((2,) + shape, F32), compiler_params=_cparams("arbitrary"),
            name=name)(pos, a, b)

    chip = lambda pos: 2 * pos[0] + pos[1]
    out = []
    for nm, a, b in zip(names, sums, recv):
        shape = _piece_shape(nm)
        if GRAD_GEOM[nm][0] == "rows":
            spec = pl.BlockSpec((None,) + shape, lambda i, pos: (chip(pos), 0, 0))
        else:
            spec = pl.BlockSpec(shape, lambda i, pos: (chip(pos), 0))
        out.append(add(a, b, spec, shape, f"{name_prefix}_{nm}"))
    return out


def _pair_share(totals, name):
    n = len(totals)

    def body(*refs):
        ins, outs = refs[:n], refs[n:2 * n]
        send, recv = refs[2 * n:]
        x, y, c = _mesh_pos()
        o = 1 - c
        cps = [pltpu.make_async_remote_copy(src_ref=ins[t].at[c], dst_ref=outs[t].at[c], send_sem=send.at[t],
                                            recv_sem=recv.at[t], device_id=(x, y, o), device_id_type=MESH)
               for t in range(n)]
        for cp in cps:
            cp.start()
        for t in range(n):
            pltpu.make_async_remote_copy(src_ref=ins[t].at[o], dst_ref=outs[t].at[o], send_sem=send.at[t],
                                         recv_sem=recv.at[t], device_id=(x, y, o), device_id_type=MESH).wait_recv()
        for cp in cps:
            cp.wait_send()

    return pl.pallas_call(
        body, in_specs=[ANY] * n, out_specs=[ANY] * n,
        out_shape=[jax.ShapeDtypeStruct(t.shape, t.dtype) for t in totals],
        scratch_shapes=[pltpu.SemaphoreType.DMA((n,)), pltpu.SemaphoreType.DMA((n,))],
        input_output_aliases={t: t for t in range(n)}, name=name)(*totals)


def _chip_sums(l, g, pos, names):
    tag = f"l{l}_" + "_".join(names)
    recv = _pair_exchange(g, names, "pair_exchange_" + tag)
    return _pair_sum(g, recv, pos, names, "pair_sum_" + tag)


def _gradient_shards(l, sums, scattered, pos, names):
    tag = f"l{l}_" + "_".join(names)
    halves = _pair_share(_chip_sum(sums, scattered, pos, names, "chip_sum_" + tag), "pair_share_" + tag)
    out = {}
    for nm, t in zip(names, halves):
        rows, cols = _piece_shape(nm)
        out[nm] = t.reshape(2 * rows, cols) if GRAD_GEOM[nm][0] == "rows" else t
    return out


def _allreduce_small(packed, name):
    rows = packed.shape[0]
    half = rows // 2
    assert half % 8 == 0

    def body(x_ref, out_ref, sib, parts, done, landed, send_sems, recv_sems):
        x, y, c = _mesh_pos()
        sibling = (x, y, 1 - c)
        mine = pl.ds(pl.multiple_of(c * half, 8), half)
        other = pl.ds(pl.multiple_of((1 - c) * half, 8), half)

        def copy(k, src, dst, to):
            return pltpu.make_async_remote_copy(src_ref=src, dst_ref=dst, send_sem=send_sems.at[k],
                                                recv_sem=recv_sems.at[k], device_id=to, device_id_type=MESH)

        swap = copy(0, x_ref.at[other, :], sib, sibling)
        swap.start()
        swap.wait()
        parts[0] = x_ref[mine, :] + sib[...]
        sends = [copy(1 + j, parts.at[0], parts.at[1 + j], (*chip, c)) for j, chip in enumerate(_other_chips(x, y))]
        for cp in sends:
            cp.start()
        for cp in sends:
            cp.wait()
        tot = None
        for chip in range(N_CHIPS):
            rel = jnp.bitwise_xor(chip, 2 * x + y)
            slot = jnp.where(rel == 0, 0, jnp.where(rel == 2, 1, jnp.where(rel == 1, 2, 3)))
            term = parts[slot]
            tot = term if tot is None else tot + term
        out_ref[mine, :] = tot
        done[...] = tot
        back = copy(4, done, landed, sibling)
        back.start()
        back.wait()
        out_ref[other, :] = landed[...]

    vmem = pl.BlockSpec(memory_space=pltpu.VMEM)
    return pl.pallas_call(
        body, in_specs=[vmem], out_specs=vmem, out_shape=jax.ShapeDtypeStruct((rows, LANES), F32),
        scratch_shapes=[pltpu.VMEM((half, LANES), F32), pltpu.VMEM((N_CHIPS, half, LANES), F32),
                        pltpu.VMEM((half, LANES), F32), pltpu.VMEM((half, LANES), F32),
                        pltpu.SemaphoreType.DMA((5,)), pltpu.SemaphoreType.DMA((5,))],
        compiler_params=pltpu.CompilerParams(vmem_limit_bytes=VMEM_LIMIT_BYTES),
        name=name)(packed)


def _adamw_step(w, g, m, v):
    mn = ADAM_B1 * m + (1.0 - ADAM_B1) * g
    vn = ADAM_B2 * v + (1.0 - ADAM_B2) * (g * g)
    m_hat = mn / (1.0 - ADAM_B1 ** ADAM_STEP)
    v_hat = vn / (1.0 - ADAM_B2 ** ADAM_STEP)
    return -ADAM_LR * (m_hat / (jnp.sqrt(v_hat) + ADAM_EPS) + ADAM_WD * w), mn, vn


def _adamw(w, g, m, v, name):
    rows, cols = w.shape
    tr = 256 if rows % 256 == 0 else rows

    def body(w_ref, g_ref, m_ref, v_ref, d_ref, mo_ref, vo_ref):
        d_ref[...], mo_ref[...], vo_ref[...] = _adamw_step(w_ref[...], g_ref[...], m_ref[...], v_ref[...])

    spec = pl.BlockSpec((tr, cols), lambda i: (i, 0))
    return pl.pallas_call(
        body, grid=(rows // tr,), in_specs=[spec] * 4, out_specs=[spec] * 3,
        out_shape=[jax.ShapeDtypeStruct((rows, cols), F32)] * 3, compiler_params=_cparams("parallel"),
        name=name)(w, g, m, v)


def _adamw_layers(w, gs, m, v, name):
    depth, rows, cols = w.shape
    tr = 256
    nblk = rows // tr
    split = gs[0].ndim == 3

    def body(w_ref, m_ref, v_ref, *rest):
        g_refs, (g_out, d_ref, mo_ref, vo_ref) = rest[:depth], rest[depth:]
        layer = pl.program_id(0)
        load = (lambda r: jnp.concatenate([r[0], r[1]], axis=-1)) if split else (lambda r: r[...])
        gv = load(g_refs[0])
        for k in range(1, depth):
            gv = jnp.where(layer == k, load(g_refs[k]), gv)
        g_out[...] = gv
        d_ref[...], mo_ref[...], vo_ref[...] = _adamw_step(w_ref[...], gv, m_ref[...], v_ref[...])

    def g_spec(k):
        tile = lambda l, i: jnp.where(l == k, i, jnp.where(l < k, 0, nblk - 1))
        if split:
            return pl.BlockSpec((2, tr, cols // 2), lambda l, i: (0, tile(l, i), 0))
        return pl.BlockSpec((tr, cols), lambda l, i: (tile(l, i), 0))

    spec = pl.BlockSpec((None, tr, cols), lambda l, i: (l, i, 0))
    return pl.pallas_call(
        body, grid=(depth, nblk), in_specs=[spec] * 3 + [g_spec(k) for k in range(depth)], out_specs=[spec] * 4,
        out_shape=[jax.ShapeDtypeStruct(w.shape, F32)] * 4, compiler_params=_cparams("parallel", "parallel"),
        name=name)(w, m, v, *gs)


def _adamw_nd(w, g, m, v, name):
    cols = w.shape[-1] if w.shape[-1] % LANES == 0 else LANES
    outs = _adamw(*(t.reshape(-1, cols) for t in (w, g, m, v)), name)
    return tuple(t.reshape(w.shape) for t in outs)


def _pack(arrays):
    return jnp.concatenate([a.reshape(-1, LANES) for a in arrays], axis=0)


def _unpack(packed, shapes):
    out, row = [], 0
    for sh in shapes:
        n = math.prod(sh) // LANES
        out.append(packed[row:row + n].reshape(sh))
        row += n
    return out


WEIGHTS = ("pre_mix_norm", "w_in", "v_norm_g", "v_norm_b", "w_spatial", "b_spatial", "out_norm_a", "out_norm_b",
           "w_out", "post_mix_norm", "pre_ffn_norm", "w_up", "conv_w", "conv_b", "w_down", "post_ffn_norm")


def kernel(x, pre_mix_norm, w_in, v_norm_g, v_norm_b, w_spatial, b_spatial, out_norm_a, out_norm_b, w_out, post_mix_norm, pre_ffn_norm, w_up, conv_w, conv_b, w_down, post_ffn_norm, loss_target, m_pre_mix_norm, m_w_in, m_v_norm_g, m_v_norm_b, m_w_spatial, m_b_spatial, m_out_norm_a, m_out_norm_b, m_w_out, m_post_mix_norm, m_pre_ffn_norm, m_w_up, m_conv_w, m_conv_b, m_w_down, m_post_ffn_norm, v_pre_mix_norm, v_w_in, v_v_norm_g, v_v_norm_b, v_w_spatial, v_b_spatial, v_out_norm_a, v_out_norm_b, v_w_out, v_post_mix_norm, v_pre_ffn_norm, v_w_up, v_conv_w, v_conv_b, v_w_down, v_post_ffn_norm):
    w = dict(pre_mix_norm=pre_mix_norm, w_in=w_in, v_norm_g=v_norm_g, v_norm_b=v_norm_b, w_spatial=w_spatial,
             b_spatial=b_spatial, out_norm_a=out_norm_a, out_norm_b=out_norm_b, w_out=w_out,
             post_mix_norm=post_mix_norm, pre_ffn_norm=pre_ffn_norm, w_up=w_up, conv_w=conv_w, conv_b=conv_b,
             w_down=w_down, post_ffn_norm=post_ffn_norm)
    m = dict(pre_mix_norm=m_pre_mix_norm, w_in=m_w_in, v_norm_g=m_v_norm_g, v_norm_b=m_v_norm_b,
             w_spatial=m_w_spatial, b_spatial=m_b_spatial, out_norm_a=m_out_norm_a, out_norm_b=m_out_norm_b,
             w_out=m_w_out, post_mix_norm=m_post_mix_norm, pre_ffn_norm=m_pre_ffn_norm, w_up=m_w_up,
             conv_w=m_conv_w, conv_b=m_conv_b, w_down=m_w_down, post_ffn_norm=m_post_ffn_norm)
    v = dict(pre_mix_norm=v_pre_mix_norm, w_in=v_w_in, v_norm_g=v_v_norm_g, v_norm_b=v_v_norm_b,
             w_spatial=v_w_spatial, b_spatial=v_b_spatial, out_norm_a=v_out_norm_a, out_norm_b=v_out_norm_b,
             w_out=v_w_out, post_mix_norm=v_post_mix_norm, pre_ffn_norm=v_pre_ffn_norm, w_up=v_w_up,
             conv_w=v_conv_w, conv_b=v_conv_b, w_down=v_w_down, post_ffn_norm=v_post_ffn_norm)
    pos = jnp.stack([lax.axis_index("x"), lax.axis_index("y"), lax.axis_index("c")]).astype(jnp.int32)
    chip = 2 * lax.axis_index("x") + lax.axis_index("y")

    cw_cols = conv_w.shape[-1]
    blocks = [{n: w[n][l].astype(BF16) for n in BIG} for l in range(DEPTH)]
    small = {n: w[n] for n in SMALL}
    xs, target = x[0], loss_target[0]
    xin = xs
    h, (w_in0, cw_all) = _rms_cast(xin, small["pre_mix_norm"][0].reshape(1, -1), "pre_mix_l0",
                                   [blocks[0]["w_in"], conv_w.reshape(-1, LANES)])
    wg = dict(w_in=w_in0)
    conv_w_full = cw_all.reshape(N_CHIPS, DEPTH, 3, cw_cols).transpose(1, 2, 0, 3).reshape(DEPTH, 3, 2 * D_FF)

    tabs = _rope_tables(xs.shape[0])
    params = [_layer_params(l, small, conv_w_full) for l in range(DEPTH)]
    saved, wgs = [], []
    for l in range(DEPTH):
        sv, gathered, wg = _layer_forward(l, xin, h, params[l], wg, tabs,
                                          blocks[l + 1] if l + 1 < DEPTH else None,
                                          blocks[0] if l == 0 else None,
                                          params[l + 1]["pre_mix_norm"] if l + 1 < DEPTH else None)
        saved.append(sv)
        wgs.append(wg)
        if l + 1 < DEPTH:
            wg = gathered
            xin, h = sv["x2"], sv["h_next"]
    loss_part, dx, df, g_post = _loss_norm_bwd(saved[-1]["x1"], saved[-1]["f"], params[-1]["post_ffn_norm"], target,
                                               "loss")
    smalls, shards = [None] * DEPTH, [{} for _ in range(DEPTH)]
    pending = None
    for l in reversed(range(DEPTH)):
        dx1, dh1, big, smalls[l], scattered, done = _layer_backward(l, dx, df, saved[l], params[l], wgs[l], tabs, pos,
                                                                    pending[1:] if pending else None, hide=l == 0)
        smalls[l]["post_ffn_norm"] = g_post
        if l > 0:
            dx, smalls[l]["pre_mix_norm"], df, g_post = _norm_bwd_in_out(
                dx1, dh1, saved[l]["x0"], params[l]["pre_mix_norm"], saved[l - 1]["f"], params[l - 1]["post_ffn_norm"],
                f"norm_bwd_in_out_l{l}")
        else:
            dx, smalls[l]["pre_mix_norm"] = _norm_bwd_in(dx1, dh1, saved[l]["x0"], params[l]["pre_mix_norm"],
                                                         "norm_bwd_in_l0")
        if pending:
            shards[pending[0]].update(_gradient_shards(pending[0], pending[1], scattered, pos, pending[2]))
        if done:
            shards[l].update(_gradient_shards(
                l, [t for sums, _ in done.values() for t in sums], [t for _, received in done.values() for t in received],
                pos, tuple(n for names in done for n in names)))
        names = tuple(big)
        pending = (l, [big[n] for n in names], names) if names else None
    if pending:
        shards[pending[0]].update(_gradient_shards(
            pending[0], pending[1], _chip_scatter(pending[1], pending[2], f"chip_scatter_l{pending[0]}"), pos,
            pending[2]))

    small_shapes = [w[n].shape for n in SMALL]
    stacked = [jnp.stack([smalls[l][n].reshape(w[n].shape[1:]) for l in range(DEPTH)]) for n in SMALL]
    cw_grad = jnp.stack([smalls[l]["conv_w"] for l in range(DEPTH)])
    packed = _pack(stacked + [cw_grad, loss_part])
    total = _allreduce_small(packed, "allreduce_small")
    parts = _unpack(total, small_shapes + [cw_grad.shape, (8, LANES)])
    g_small = dict(zip(SMALL, parts[:len(SMALL)]))
    loss = parts[-1][0, 0]
    g_conv_w = lax.dynamic_slice(parts[-2], (0, 0, chip * cw_cols), conv_w.shape)

    grads = dict(g_small, conv_w=g_conv_w)

    dp, mp, vp = _adamw(_pack([w[n] for n in SMALL]), _pack([g_small[n] for n in SMALL]),
                        _pack([m[n] for n in SMALL]), _pack([v[n] for n in SMALL]), "adamw_small")
    delta = dict(zip(SMALL, _unpack(dp, small_shapes)))
    new_m = dict(zip(SMALL, _unpack(mp, small_shapes)))
    new_v = dict(zip(SMALL, _unpack(vp, small_shapes)))
    delta["conv_w"], new_m["conv_w"], new_v["conv_w"] = _adamw_nd(w["conv_w"], g_conv_w, m["conv_w"], v["conv_w"],
                                                                  "adamw_conv_w")
    for n in BIG:
        grads[n], delta[n], new_m[n], new_v[n] = _adamw_layers(w[n], [shards[l][n] for l in range(DEPTH)], m[n],
                                                               v[n], "adamw_" + n)

    return (loss, dx[None], *[grads[n] for n in WEIGHTS], *[delta[n] for n in WEIGHTS],
            *[new_m[n] for n in WEIGHTS], *[new_v[n] for n in WEIGHTS])
```

```python
import functools
import math

import jax
import jax.numpy as jnp
import numpy as np
from jax import lax
from jax.experimental import pallas as pl
from jax.experimental.pallas import tpu as pltpu

F32 = jnp.float32
BF16 = jnp.bfloat16
MESH = pl.DeviceIdType.MESH

D_MODEL = 1024
A_WIDTH = 512
A_GROUPS = 4
GROUP_DIM = 128
CHUNK = 128
B_WIDTH = 512
HEAD_DIM = 64
ROT_DIM = 16
ROPE_THETA = 500000.0
DILATIONS = (1, 4, 16)
BAND = 128
IN_COLS = 2560
D_FF = 4096
EPS = 1e-6
NEG_INF = -1e30
N_CHIPS = 4
LANES = 128

ADAM_LR = 0.001
ADAM_B1 = 0.9
ADAM_B2 = 0.999
ADAM_EPS = 1e-08
ADAM_WD = 0.01
ADAM_STEP = 10

VMEM_LIMIT_BYTES = 56 * 1024 * 1024
RSQRT2 = 0.7071067811865476
INV_SQRT_2PI = 0.3989422804014327
GELU_C = 0.7978845608028654
GELU_A = 0.044715

ANY = pl.BlockSpec(memory_space=pl.ANY)
NN = ((1,), (0,))
NT = ((1,), (1,))
TN = ((0,), (0,))


def _cparams(*sem):
    return pltpu.CompilerParams(dimension_semantics=sem, vmem_limit_bytes=VMEM_LIMIT_BYTES)


def _dot(a, b, dims):
    return lax.dot_general(a, b, (dims, ((), ())), preferred_element_type=F32)


def _rsq_mean(a):
    return lax.rsqrt(jnp.mean(a * a, axis=-1, keepdims=True) + EPS)


def _rms_bwd(a, r, g, dz):
    t = dz * g
    da = r * t - a * (r * r * r) * jnp.mean(t * a, axis=-1, keepdims=True)
    return da, dz * a * r


def _colsum(a):
    return jnp.sum(a, axis=0, keepdims=True)


def _gelu_tanh(x):
    u = x * x
    t = jnp.tanh(x * (GELU_C + (GELU_C * GELU_A) * u))
    hx = 0.5 * x
    act = hx + hx * t
    grad = 0.5 + 0.5 * t + (hx - hx * t * t) * (GELU_C + (3.0 * GELU_C * GELU_A) * u)
    return act, grad


def _grid_edges(grid):
    ids = [pl.program_id(ax) for ax in range(len(grid))]
    first = functools.reduce(jnp.logical_and, [i == 0 for i in ids])
    last = functools.reduce(jnp.logical_and, [i == n - 1 for i, n in zip(ids, grid)])
    return first, last


def _matmul(a, b, *, grid, a_spec, b_spec, o_spec, o_shape, o_dtype, dims, nk, kaxis, acc_shape, name, b_2d=None,
            halves=False, scatter=None, gather=None):
    assert scatter is None or gather is None
    ns = len(scatter[0]) if scatter else len(gather) if gather else 0

    def body(*refs):
        a_ref, b_ref = refs[:2]
        o_ref = refs[2 + ns]
        scratch = refs[3 + 2 * ns:]
        if ns:
            first, last = _grid_edges(grid)
            if scatter:
                start, finish = _scatter_steps(refs[2:2 + ns], refs[3 + ns:3 + 2 * ns], scratch[-2], scratch[-1],
                                               scatter[1])
            else:
                start, relay, last_wait = _gather_steps(refs[2:2 + ns], refs[3 + ns:3 + 2 * ns], scratch[-2],
                                                        scratch[-1])

                def finish():
                    relay()
                    last_wait()
            pl.when(first)(start)
        def store(val):
            if halves:
                half = val.shape[1] // 2
                o_ref[0] = val[:, :half].astype(o_dtype)
                o_ref[1] = val[:, half:].astype(o_dtype)
            else:
                o_ref[...] = val.astype(o_dtype)

        bv = b_ref[...] if b_2d is None else b_ref[...].reshape(b_2d)
        part = _dot(a_ref[...], bv, dims)
        if nk == 1:
            store(part)
        else:
            acc = scratch[0]
            k = pl.program_id(kaxis)

            @pl.when(k == 0)
            def _():
                acc[...] = part

            @pl.when(k > 0)
            def _():
                acc[...] += part

            @pl.when(k == nk - 1)
            def _():
                store(acc[...])

        if ns:
            pl.when(last)(finish)

    sem = tuple("arbitrary" if (ns or (nk > 1 and ax == kaxis)) else "parallel" for ax in range(len(grid)))
    riding = list(scatter[0]) if scatter else list(gather or [])
    rider_shapes = _scattered_shapes(scatter[1]) if scatter else _gathered_shapes(riding)
    rider_sems = _scatter_sems(ns) if scatter else _gather_sems(ns) if gather else []
    res = pl.pallas_call(
        body, grid=grid, in_specs=[a_spec, b_spec] + [ANY] * ns, out_specs=[o_spec] + [ANY] * ns,
        out_shape=[jax.ShapeDtypeStruct(o_shape, o_dtype)] + rider_shapes,
        scratch_shapes=([pltpu.VMEM(acc_shape, F32)] if nk > 1 else []) + rider_sems,
        compiler_params=_cparams(*sem), name=name)(a, b, *riding)
    return (res[0], list(res[1:])) if ns else res[0]


def _mix_out_norm(mixed, w_out, x0, g_post, g_next, name, gather=None):
    s, d = x0.shape
    tm = 512
    ng = 0 if gather is None else len(gather)

    def body(a_ref, w_ref, x_ref, gp_ref, gn_ref, *rest):
        y_ref, x1_ref, h_ref = rest[ng:ng + 3]
        if ng:
            start, relay, finish = _gather_steps(rest[:ng], rest[ng + 3:2 * ng + 3], *rest[2 * ng + 3:])
            first, last = _grid_edges((s // tm,))
            pl.when(first)(start)
        y = _dot(a_ref[...], w_ref[...].reshape(d, d), NN)
        y_ref[...] = y
        x1 = x_ref[...] + y * _rsq_mean(y) * gp_ref[...]
        x1_ref[...] = x1
        h_ref[...] = (x1 * _rsq_mean(x1) * gn_ref[...]).astype(BF16)

        if ng:
            @pl.when(last)
            def _():
                relay()
                finish()

    row = pl.BlockSpec((tm, d), lambda i: (i, 0))
    vec = pl.BlockSpec((1, d), lambda i: (0, 0))
    res = pl.pallas_call(
        body, grid=(s // tm,),
        in_specs=[row, pl.BlockSpec((N_CHIPS, None, d // N_CHIPS, d), lambda i: (0, 0, 0, 0)), row, vec, vec]
        + [ANY] * ng,
        out_specs=[row, row, row] + [ANY] * ng,
        out_shape=[jax.ShapeDtypeStruct((s, d), F32), jax.ShapeDtypeStruct((s, d), F32),
                   jax.ShapeDtypeStruct((s, d), BF16)] + _gathered_shapes(gather or []),
        scratch_shapes=_gather_sems(ng) if ng else [],
        compiler_params=_cparams("arbitrary" if ng else "parallel"), name=name)(mixed, w_out, x0, g_post, g_next,
                                                                              *(gather or []))
    return res[:3], list(res[3:])


def _proj_bwd(dproj, w_in, name, scatter=None):
    s = dproj.shape[0]
    wcol = IN_COLS // N_CHIPS
    ns = 0 if scatter is None else len(scatter[0])

    def body(*refs):
        a_ref, w_ref = refs[:2]
        o_ref = refs[2 + ns]
        if ns:
            start, finish = _scatter_steps(refs[2:2 + ns], refs[3 + ns:3 + 2 * ns], *refs[3 + 2 * ns:], scatter[1])
            first, last = _grid_edges((s // TMM,))
            pl.when(first)(start)
        acc = _dot(a_ref[:, :wcol], w_ref[0], NT)
        for j in range(1, N_CHIPS):
            acc = acc + _dot(a_ref[:, j * wcol:(j + 1) * wcol], w_ref[j], NT)
        o_ref[...] = acc
        if ns:
            pl.when(last)(finish)

    res = pl.pallas_call(
        body, grid=(s // TMM,),
        in_specs=[pl.BlockSpec((TMM, IN_COLS), lambda i: (i, 0)),
                  pl.BlockSpec((N_CHIPS, None, D_MODEL, wcol), lambda i: (0, 0, 0, 0))] + [ANY] * ns,
        out_specs=[pl.BlockSpec((TMM, D_MODEL), lambda i: (i, 0))] + [ANY] * ns,
        out_shape=[jax.ShapeDtypeStruct((s, D_MODEL), F32)] + (_scattered_shapes(scatter[1]) if ns else []),
        scratch_shapes=_scatter_sems(ns) if ns else [],
        compiler_params=_cparams("arbitrary" if ns else "parallel"), name=name)(dproj, w_in,
                                                                              *(scatter[0] if ns else []))
    return res[0], list(res[1:])


TM = 1024
TMM = 1024


TR = 512


def _row_spec(width, col=0):
    return pl.BlockSpec((TR, width), lambda i, col=col: (i, col))


def _vec_spec(width):
    return pl.BlockSpec((1, width), lambda i: (0, 0))


def _rms_cast(x, g, name, gather=None):
    s, d = x.shape
    ng = 0 if gather is None else len(gather)

    def body(x_ref, g_ref, *rest):
        if ng:
            start, relay, finish = _gather_steps(rest[:ng], rest[ng + 1:2 * ng + 1], *rest[2 * ng + 1:])
            first, last = _grid_edges((s // TR,))
            pl.when(first)(start)
        a = x_ref[...]
        rest[ng][...] = (a * _rsq_mean(a) * g_ref[...]).astype(BF16)

        if ng:
            @pl.when(last)
            def _():
                relay()
                finish()

    res = pl.pallas_call(
        body, grid=(s // TR,), in_specs=[_row_spec(d), _vec_spec(d)] + [ANY] * ng,
        out_specs=[_row_spec(d)] + [ANY] * ng,
        out_shape=[jax.ShapeDtypeStruct((s, d), BF16)] + _gathered_shapes(gather or []),
        scratch_shapes=_gather_sems(ng) if ng else [],
        compiler_params=_cparams("arbitrary" if ng else "parallel"), name=name)(x, g, *(gather or []))
    return res[0], list(res[1:])


def _acc_init(refs):
    @pl.when(pl.program_id(0) == 0)
    def _():
        for r in refs:
            r[...] = jnp.zeros_like(r)


def _loss_norm_bwd(x1, f, g_post, target, name):
    s, d = x1.shape

    def body(x_ref, f_ref, gp_ref, t_ref, loss_ref, dx_ref, df_ref, dg_ref):
        _acc_init([loss_ref, dg_ref])
        fv = f_ref[...]
        r = _rsq_mean(fv)
        err = x_ref[...] + fv * r * gp_ref[...] - t_ref[...]
        dx = err * (1.0 / d)
        dx_ref[...] = dx
        part = 0.5 * jnp.sum(jnp.mean(err * err, axis=-1, keepdims=True), axis=0, keepdims=True)
        loss_ref[...] += jnp.broadcast_to(part, loss_ref.shape)
        da, dgt = _rms_bwd(fv, r, gp_ref[...], dx)
        df_ref[...] = da.astype(BF16)
        dg_ref[...] += _colsum(dgt)

    return pl.pallas_call(
        body, grid=(s // TR,), in_specs=[_row_spec(d), _row_spec(d), _vec_spec(d), _row_spec(d)],
        out_specs=[pl.BlockSpec((8, LANES), lambda i: (0, 0)), _row_spec(d), _row_spec(d), _vec_spec(d)],
        out_shape=[jax.ShapeDtypeStruct((8, LANES), F32), jax.ShapeDtypeStruct((s, d), F32),
                   jax.ShapeDtypeStruct((s, d), BF16), jax.ShapeDtypeStruct((1, d), F32)],
        compiler_params=_cparams("arbitrary"), name=name)(x1, f, g_post, target)


def _norm_bwd_mid(dx2, dh2, x1, g_pf, y1, g_pm, name):
    s, d = dx2.shape

    def body(dx2_ref, dh_ref, x1_ref, gpf_ref, y1_ref, gpm_ref, dx1_ref, dy1_ref, dgpf_ref, dgpm_ref):
        _acc_init([dgpf_ref, dgpm_ref])
        x1 = x1_ref[...]
        da, dgt = _rms_bwd(x1, _rsq_mean(x1), gpf_ref[...], dh_ref[...])
        dx1 = dx2_ref[...] + da
        dx1_ref[...] = dx1
        dgpf_ref[...] += _colsum(dgt)
        y1 = y1_ref[...]
        dy, dgt2 = _rms_bwd(y1, _rsq_mean(y1), gpm_ref[...], dx1)
        dy1_ref[...] = dy.astype(BF16)
        dgpm_ref[...] += _colsum(dgt2)

    return pl.pallas_call(
        body, grid=(s // TR,),
        in_specs=[_row_spec(d), _row_spec(d), _row_spec(d), _vec_spec(d), _row_spec(d), _vec_spec(d)],
        out_specs=[_row_spec(d), _row_spec(d), _vec_spec(d), _vec_spec(d)],
        out_shape=[jax.ShapeDtypeStruct((s, d), F32), jax.ShapeDtypeStruct((s, d), BF16),
                   jax.ShapeDtypeStruct((1, d), F32), jax.ShapeDtypeStruct((1, d), F32)],
        compiler_params=_cparams("arbitrary"), name=name)(dx2, dh2, x1, g_pf, y1, g_pm)


def _norm_bwd_in_out(dx1, dh1, x0, g1, f_below, g_post_below, name):
    s, d = dx1.shape

    def body(dx1_ref, dh_ref, x0_ref, g_ref, f_ref, gp_ref, dx0_ref, dg_ref, df_ref, dgp_ref):
        _acc_init([dg_ref, dgp_ref])
        x0 = x0_ref[...]
        da, dgt = _rms_bwd(x0, _rsq_mean(x0), g_ref[...], dh_ref[...])
        dx0 = dx1_ref[...] + da
        dx0_ref[...] = dx0
        dg_ref[...] += _colsum(dgt)
        fv = f_ref[...]
        db, dgt2 = _rms_bwd(fv, _rsq_mean(fv), gp_ref[...], dx0)
        df_ref[...] = db.astype(BF16)
        dgp_ref[...] += _colsum(dgt2)

    return pl.pallas_call(
        body, grid=(s // TR,),
        in_specs=[_row_spec(d), _row_spec(d), _row_spec(d), _vec_spec(d), _row_spec(d), _vec_spec(d)],
        out_specs=[_row_spec(d), _vec_spec(d), _row_spec(d), _vec_spec(d)],
        out_shape=[jax.ShapeDtypeStruct((s, d), F32), jax.ShapeDtypeStruct((1, d), F32),
                   jax.ShapeDtypeStruct((s, d), BF16), jax.ShapeDtypeStruct((1, d), F32)],
        compiler_params=_cparams("arbitrary"), name=name)(dx1, dh1, x0, g1, f_below, g_post_below)


def _norm_bwd_in(dx1, dh1, x0, g1, name):
    s, d = dx1.shape

    def body(dx1_ref, dh_ref, x0_ref, g_ref, dx0_ref, dg_ref):
        _acc_init([dg_ref])
        x0 = x0_ref[...]
        da, dgt = _rms_bwd(x0, _rsq_mean(x0), g_ref[...], dh_ref[...])
        dx0_ref[...] = dx1_ref[...] + da
        dg_ref[...] += _colsum(dgt)

    return pl.pallas_call(
        body, grid=(s // TR,), in_specs=[_row_spec(d), _row_spec(d), _row_spec(d), _vec_spec(d)],
        out_specs=[_row_spec(d), _vec_spec(d)],
        out_shape=[jax.ShapeDtypeStruct((s, d), F32), jax.ShapeDtypeStruct((1, d), F32)],
        compiler_params=_cparams("arbitrary"), name=name)(dx1, dh1, x0, g1)


def _tril_mask():
    row = lax.broadcasted_iota(jnp.int32, (CHUNK, CHUNK), 0)
    col = lax.broadcasted_iota(jnp.int32, (CHUNK, CHUNK), 1)
    return row >= col


def _gating_forward(pa, gv, bv, wt, bsf):
    er = lax.erf(pa * RSQRT2)
    za = 0.5 * pa * (1.0 + er)
    u = za[:, :A_WIDTH]
    va = za[:, A_WIDTH:]
    xc = va - jnp.mean(va, axis=-1, keepdims=True)
    rs = lax.rsqrt(jnp.mean(xc * xc, axis=-1, keepdims=True) + EPS)
    vn = xc * rs
    vlb = (vn * gv + bv).astype(BF16)
    sg = jnp.concatenate(
        [_dot(wt[g], vlb[:, g * GROUP_DIM:(g + 1) * GROUP_DIM], NN) for g in range(A_GROUPS)], axis=1) + bsf
    return er, u, rs, vn, vlb, sg


def _masked_ws(ws_ref):
    mask = _tril_mask()
    return [jnp.where(mask, ws_ref[g], 0.0).astype(BF16) for g in range(A_GROUPS)]


def _mixer_a_fwd(proj, gv, bv, ws, bsf, ga, name):
    s = proj.shape[0]

    def body(p_ref, gv_ref, bv_ref, ws_ref, bs_ref, ga_ref, o_ref):
        wt = _masked_ws(ws_ref)
        for ch in range(TR // CHUNK):
            rows = slice(ch * CHUNK, (ch + 1) * CHUNK)
            _, u, _, _, _, sg = _gating_forward(p_ref[rows, :].astype(F32), gv_ref[...], bv_ref[...], wt, bs_ref[...])
            oa = u * sg
            o_ref[rows, :] = (oa * _rsq_mean(oa) * ga_ref[...]).astype(BF16)

    return pl.pallas_call(
        body, grid=(s // TR,),
        in_specs=[_row_spec(2 * A_WIDTH), _vec_spec(A_WIDTH), _vec_spec(A_WIDTH),
                  pl.BlockSpec((A_GROUPS, CHUNK, CHUNK), lambda i: (0, 0, 0)),
                  pl.BlockSpec((CHUNK, A_WIDTH), lambda i: (0, 0)), _vec_spec(A_WIDTH)],
        out_specs=_row_spec(A_WIDTH), out_shape=jax.ShapeDtypeStruct((s, A_WIDTH + B_WIDTH), BF16),
        compiler_params=_cparams("parallel"), name=name)(proj, gv, bv, ws, bsf, ga)


def _mixer_a_bwd(proj, dmixed, gv, bv, ws, bsf, ga, name, scatter=None):
    s = proj.shape[0]
    nsteps = s // TR
    ns = 0 if scatter is None else len(scatter[0])

    def body(*refs):
        p_ref, dm_ref, gv_ref, bv_ref, ws_ref, bs_ref, ga_ref = refs[:7]
        dp_ref, dga_ref, dgv_ref, dbv_ref, dbs_ref, dws_ref = refs[7 + ns:13 + ns]
        if ns:
            start, finish = _scatter_steps(refs[7:7 + ns], refs[13 + ns:13 + 2 * ns], *refs[13 + 2 * ns:], scatter[1])
            first, last = _grid_edges((nsteps,))
            pl.when(first)(start)
        _acc_init([dga_ref, dgv_ref, dbv_ref, dbs_ref, dws_ref])
        mask = _tril_mask()
        wt = _masked_ws(ws_ref)
        gvv = gv_ref[...]
        gav = ga_ref[...]
        for ch in range(TR // CHUNK):
            rows = slice(ch * CHUNK, (ch + 1) * CHUNK)
            pa = p_ref[rows, :].astype(F32)
            er, u, rs, vn, vlb, sg = _gating_forward(pa, gvv, bv_ref[...], wt, bs_ref[...])
            oa = u * sg
            doa, dgt = _rms_bwd(oa, _rsq_mean(oa), gav, dm_ref[rows, :])
            dga_ref[...] += _colsum(dgt)
            du = doa * sg
            dsg = doa * u
            dbs_ref[...] += dsg
            dsgb = dsg.astype(BF16)
            dvl = []
            for g in range(A_GROUPS):
                cols = slice(g * GROUP_DIM, (g + 1) * GROUP_DIM)
                dws_ref[g] += jnp.where(mask, _dot(dsgb[:, cols], vlb[:, cols], NT), 0.0)
                dvl.append(_dot(wt[g], dsgb[:, cols], TN))
            dvl = jnp.concatenate(dvl, axis=1)
            dgv_ref[...] += _colsum(dvl * vn)
            dbv_ref[...] += _colsum(dvl)
            dvn = dvl * gvv
            dva = rs * (dvn - jnp.mean(dvn, axis=-1, keepdims=True)
                        - vn * jnp.mean(dvn * vn, axis=-1, keepdims=True))
            gp = 0.5 * (1.0 + er) + pa * jnp.exp(-0.5 * pa * pa) * INV_SQRT_2PI
            dp_ref[rows, :] = (jnp.concatenate([du, dva], axis=1) * gp).astype(BF16)

        @pl.when(pl.program_id(0) == nsteps - 1)
        def _():
            for g in range(A_GROUPS):
                cols = slice(g * GROUP_DIM, (g + 1) * GROUP_DIM)
                tot = jnp.sum(dbs_ref[:, cols], axis=1, keepdims=True)
                dbs_ref[:, cols] = jnp.broadcast_to(tot, (CHUNK, GROUP_DIM))

        if ns:
            pl.when(last)(finish)

    full = lambda *shape: pl.BlockSpec(shape, lambda i: (0,) * len(shape))
    res = pl.pallas_call(
        body, grid=(nsteps,),
        in_specs=[_row_spec(2 * A_WIDTH), _row_spec(A_WIDTH), _vec_spec(A_WIDTH), _vec_spec(A_WIDTH),
                  full(A_GROUPS, CHUNK, CHUNK), full(CHUNK, A_WIDTH), _vec_spec(A_WIDTH)] + [ANY] * ns,
        out_specs=[_row_spec(2 * A_WIDTH), _vec_spec(A_WIDTH), _vec_spec(A_WIDTH), _vec_spec(A_WIDTH),
                   full(CHUNK, A_WIDTH), full(A_GROUPS, CHUNK, CHUNK)] + [ANY] * ns,
        out_shape=[jax.ShapeDtypeStruct((s, IN_COLS), BF16), jax.ShapeDtypeStruct((1, A_WIDTH), F32),
                   jax.ShapeDtypeStruct((1, A_WIDTH), F32), jax.ShapeDtypeStruct((1, A_WIDTH), F32),
                   jax.ShapeDtypeStruct((CHUNK, A_WIDTH), F32),
                   jax.ShapeDtypeStruct((A_GROUPS, CHUNK, CHUNK), F32)]
        + (_scattered_shapes(scatter[1]) if ns else []),
        scratch_shapes=_scatter_sems(ns) if ns else [],
        compiler_params=_cparams("arbitrary"), name=name)(proj, dmixed, gv, bv, ws, bsf, ga,
                                                          *(scatter[0] if ns else []))
    return res[:6] + (list(res[6:]),)


def _rope_tables(s):
    half = ROT_DIM // 2
    lane = jnp.arange(LANES) % HEAD_DIM
    inv = ROPE_THETA ** (-(2 * (lane % half)).astype(F32) / ROT_DIM)
    ang = jnp.arange(s, dtype=F32)[:, None] * inv[None, :]
    cos, sin = jnp.cos(ang), jnp.sin(ang)
    c = jnp.where(lane < ROT_DIM, cos, 1.0)
    s1 = jnp.where(lane < half, -sin, 0.0)
    s2 = jnp.where((lane >= half) & (lane < ROT_DIM), sin, 0.0)
    return c, s1, s2


def _lane_blocks(width):
    return [slice(b * LANES, (b + 1) * LANES) for b in range(width // LANES)]


CLASS_DILS = tuple(d for d in DILATIONS if d > 1)


def _class_shape(s, dil, dtype):
    return jax.ShapeDtypeStruct((dil, s // dil, B_WIDTH), dtype)


def _class_spec(dil):
    return pl.BlockSpec((dil, TR // dil, B_WIDTH), lambda i, *_: (0, i, 0))


NBLK = B_WIDTH // LANES
STAGE = pltpu.VMEM((NBLK, TR, LANES), F32)


def _stage_put(stage, value):
    for b, sl in enumerate(_lane_blocks(B_WIDTH)):
        stage[b] = value[:, sl]


def _stage_get(stage):
    return jnp.concatenate([stage[b] for b in range(NBLK)], axis=1)


def _store_classes(stage, dst_ref, dil):
    for b, sl in enumerate(_lane_blocks(B_WIDTH)):
        for r in range(dil):
            dst_ref[r, :, sl] = stage[b, pl.ds(r, TR // dil, stride=dil), :].astype(dst_ref.dtype)


def _load_classes(src_ref, stage, dil):
    for b, sl in enumerate(_lane_blocks(B_WIDTH)):
        for r in range(dil):
            stage[b, pl.ds(r, TR // dil, stride=dil), :] = src_ref[r, :, sl].astype(F32)
    return _stage_get(stage)


def _rope_fwd(proj, tabs, name, gather=None):
    s = proj.shape[0]
    half = ROT_DIM // 2
    scale = HEAD_DIM ** -0.5
    nlay = 1 + len(CLASS_DILS)
    ng = 0 if gather is None else len(gather)

    def body(q_ref, k_ref, v_ref, c_ref, s1_ref, s2_ref, *rest):
        outs, stage = rest[ng:ng + 3 * nlay], rest[2 * ng + 3 * nlay]
        if ng:
            start, relay, finish = _gather_steps(rest[:ng], rest[ng + 3 * nlay:2 * ng + 3 * nlay],
                                                 *rest[2 * ng + 3 * nlay + 1:])
            first, last = _grid_edges((s // TR,))
            pl.when(first)(start)
        c, s1, s2 = c_ref[...], s1_ref[...], s2_ref[...]
        for which, (src, mul) in enumerate(((q_ref, scale), (k_ref, 1.0), (v_ref, None))):
            if mul is None:
                _stage_put(stage, src[...].astype(F32))
            else:
                for b, sl in enumerate(_lane_blocks(B_WIDTH)):
                    a = src[:, sl].astype(F32)
                    r = a * c + pltpu.roll(a, LANES - half, 1) * s1 + pltpu.roll(a, half, 1) * s2
                    stage[b] = r * mul
            dst = outs[which * nlay:(which + 1) * nlay]
            dst[0][...] = _stage_get(stage).astype(BF16)
            for ref, d in zip(dst[1:], CLASS_DILS):
                _store_classes(stage, ref, d)

        if ng:
            @pl.when(last)
            def _():
                relay()
                finish()

    tab = pl.BlockSpec((TR, LANES), lambda i: (i, 0))
    lay_specs = [_row_spec(B_WIDTH)] + [_class_spec(d) for d in CLASS_DILS]
    lay_shapes = [jax.ShapeDtypeStruct((s, B_WIDTH), BF16)] + [_class_shape(s, d, BF16) for d in CLASS_DILS]
    outs = pl.pallas_call(
        body, grid=(s // TR,),
        in_specs=[_row_spec(B_WIDTH, 2), _row_spec(B_WIDTH, 3), _row_spec(B_WIDTH, 4), tab, tab, tab] + [ANY] * ng,
        out_specs=lay_specs * 3 + [ANY] * ng, out_shape=lay_shapes * 3 + _gathered_shapes(gather or []),
        scratch_shapes=[STAGE] + (_gather_sems(ng) if ng else []),
        compiler_params=_cparams("arbitrary" if ng else "parallel"), name=name)(proj, proj, proj, *tabs,
                                                                              *(gather or []))
    q, k, v = (dict(zip(DILATIONS, outs[w * nlay:(w + 1) * nlay])) for w in range(3))
    return q, k, v, list(outs[3 * nlay:])


def _as_classes(t):
    return t if t.ndim == 3 else t[None]


def _head_masks():
    lane = lax.broadcasted_iota(jnp.int32, (1, LANES), 1)
    return lane < HEAD_DIM, lane >= HEAD_DIM


def _stack_heads(t):
    lo, hi = _head_masks()
    zero = jnp.zeros_like(t)
    return jnp.concatenate([jnp.where(lo, t, zero), jnp.where(hi, t, zero)], axis=0)


MAX_SEGMENT_BLOCKS = 8


def _segment_masks(j):
    qi = lax.broadcasted_iota(jnp.int32, (BAND, 2 * BAND), 0)
    kj = lax.broadcasted_iota(jnp.int32, (BAND, 2 * BAND), 1)
    both = (kj >= qi) & (kj <= qi + BAND)
    own = kj[:, :BAND] <= qi[:, :BAND]
    head = both & ((kj >= BAND) | (j > 0))
    return tuple(jnp.concatenate([m, m], axis=0) for m in (own, both, head))


def _block_rows(g):
    return pl.ds(pl.multiple_of(g * BAND, BAND), BAND)


def _key_rows(g):
    return pl.ds(pl.multiple_of((g - 1) * BAND, BAND), 2 * BAND)


def _segments(n):
    nb = n // BAND
    seg = min(nb, MAX_SEGMENT_BLOCKS)
    return seg, nb // seg


def _segment_specs(seg):
    main = pl.BlockSpec((None, seg * BAND, B_WIDTH), lambda r, j: (r, j, 0))
    halo = pl.BlockSpec((None, BAND, B_WIDTH), lambda r, j: (r, jnp.maximum(j * seg - 1, 0), 0))
    return main, halo


def _attn_fwd(q, k, v, name, gather=None):
    dil, n, _ = q.shape
    seg, nseg = _segments(n)
    nh = 2 if nseg > 1 else 0
    ng = 0 if gather is None else len(gather)

    def body(*refs):
        q_ref, k_ref, v_ref = refs[:3]
        halos = refs[3:3 + nh]
        o_ref, l_ref = refs[3 + nh + ng:5 + nh + ng]
        if ng:
            start, relay, finish = _gather_steps(refs[3 + nh:3 + nh + ng], refs[5 + nh + ng:5 + nh + 2 * ng],
                                                 *refs[5 + nh + 2 * ng:])
            first, last = _grid_edges((dil, nseg))
            pl.when(first)(start)
        own, both, head = _segment_masks(pl.program_id(1))
        lo, _ = _head_masks()

        def block(rows, keys_of, valid):
            for sl in _lane_blocks(B_WIDTH):
                kk, vv = keys_of(sl)
                sc = jnp.where(valid, _dot(_stack_heads(q_ref[rows, sl]), kk, NT), NEG_INF)
                mx = jnp.max(sc, axis=1, keepdims=True)
                p = jnp.exp(sc - mx)
                den = jnp.sum(p, axis=1, keepdims=True)
                out = _dot(p.astype(BF16), vv, NN) / den
                lse = mx + jnp.log(den)
                o_ref[rows, sl] = jnp.where(lo, out[:BAND], out[BAND:]).astype(BF16)
                l_ref[rows, sl] = jnp.where(lo, lse[:BAND], lse[BAND:])

        if nh:
            block(_block_rows(0), lambda sl: (jnp.concatenate([halos[0][:, sl], k_ref[0:BAND, sl]], axis=0),
                                              jnp.concatenate([halos[1][:, sl], v_ref[0:BAND, sl]], axis=0)), head)
        else:
            block(_block_rows(0), lambda sl: (k_ref[0:BAND, sl], v_ref[0:BAND, sl]), own)

        @pl.loop(1, seg)
        def _(g):
            block(_block_rows(g), lambda sl: (k_ref[_key_rows(g), sl], v_ref[_key_rows(g), sl]), both)

        if ng:
            @pl.when(last)
            def _():
                relay()
                finish()

    main, halo = _segment_specs(seg)
    res = pl.pallas_call(
        body, grid=(dil, nseg), in_specs=[main] * 3 + [halo] * nh + [ANY] * ng, out_specs=[main, main] + [ANY] * ng,
        out_shape=[jax.ShapeDtypeStruct((dil, n, B_WIDTH), BF16), jax.ShapeDtypeStruct((dil, n, B_WIDTH), F32)]
        + _gathered_shapes(gather or []),
        scratch_shapes=_gather_sems(ng) if ng else [],
        compiler_params=_cparams(*(["arbitrary"] * 2 if ng else ["parallel"] * 2)), name=name)(
            q, k, v, *([k, v] if nh else []), *(gather or []))
    return res[0], res[1], list(res[2:])


def _attn_bwd(q, k, v, do, lse, delta, name, scatter=None):
    dil, n, _ = q.shape
    seg, nseg = _segments(n)
    nh = 2 if nseg > 1 else 0
    ns = 0 if scatter is None else len(scatter[0])

    def body(*refs):
        q_ref, k_ref, v_ref, do_ref, lse_ref, dl_ref = refs[:6]
        halos = refs[6:6 + nh]
        dq_ref, dk_ref, dv_ref = refs[6 + nh + ns:9 + nh + ns]
        halo_out = refs[9 + nh + ns:9 + 2 * nh + ns]
        ck_ref, cv_ref = refs[9 + 2 * nh + 2 * ns:11 + 2 * nh + 2 * ns]
        if ns:
            start, finish = _scatter_steps(refs[6 + nh:6 + nh + ns], refs[9 + 2 * nh + ns:9 + 2 * nh + 2 * ns],
                                           *refs[11 + 2 * nh + 2 * ns:], scatter[1])
            first, last = _grid_edges((dil, nseg))
            pl.when(first)(start)
        own, both, head = _segment_masks(pl.program_id(1))
        lo, _ = _head_masks()
        lane = lax.broadcasted_iota(jnp.int32, (1, LANES), 1)

        def per_head(t):
            return jnp.concatenate(
                [jnp.sum(jnp.where(lane == first, t, 0.0), axis=1, keepdims=True) for first in (0, HEAD_DIM)], axis=0)

        def grads(rows, kk, vv, valid, sl):
            q2 = _stack_heads(q_ref[rows, sl])
            do2 = _stack_heads(do_ref[rows, sl])
            p = jnp.where(valid, jnp.exp(_dot(q2, kk, NT) - per_head(lse_ref[rows, sl])), 0.0)
            ds = (p * (_dot(do2, vv, NT) - per_head(dl_ref[rows, sl]))).astype(BF16)
            dq = _dot(ds, kk, NN)
            dq_ref[rows, sl] = jnp.where(lo, dq[:BAND], dq[BAND:]).astype(BF16)
            return _dot(ds, q2, TN), _dot(p.astype(BF16), do2, TN)

        for sl in _lane_blocks(B_WIDTH):
            if nh:
                dkk, dvv = grads(_block_rows(0), jnp.concatenate([halos[0][:, sl], k_ref[0:BAND, sl]], axis=0),
                                 jnp.concatenate([halos[1][:, sl], v_ref[0:BAND, sl]], axis=0), head, sl)
                halo_out[0][:, sl], halo_out[1][:, sl] = dkk[:BAND], dvv[:BAND]
                ck_ref[:, sl], cv_ref[:, sl] = dkk[BAND:], dvv[BAND:]
            else:
                ck_ref[:, sl], cv_ref[:, sl] = grads(_block_rows(0), k_ref[0:BAND, sl], v_ref[0:BAND, sl], own, sl)

        @pl.loop(1, seg)
        def _(g):
            before = _block_rows(g - 1)
            for sl in _lane_blocks(B_WIDTH):
                dkk, dvv = grads(_block_rows(g), k_ref[_key_rows(g), sl], v_ref[_key_rows(g), sl], both, sl)
                dk_ref[before, sl] = (ck_ref[:, sl] + dkk[:BAND]).astype(BF16)
                dv_ref[before, sl] = (cv_ref[:, sl] + dvv[:BAND]).astype(BF16)
                ck_ref[:, sl] = dkk[BAND:]
                cv_ref[:, sl] = dvv[BAND:]

        final = pl.ds((seg - 1) * BAND, BAND)
        dk_ref[final, :] = ck_ref[...].astype(BF16)
        dv_ref[final, :] = cv_ref[...].astype(BF16)

        if ns:
            pl.when(last)(finish)

    main, halo = _segment_specs(seg)
    shape = jax.ShapeDtypeStruct((dil, n, B_WIDTH), BF16)
    halo_shape = jax.ShapeDtypeStruct((dil, nseg, BAND, B_WIDTH), F32)
    halo_spec = pl.BlockSpec((None, None, BAND, B_WIDTH), lambda r, j: (r, j, 0, 0))
    res = pl.pallas_call(
        body, grid=(dil, nseg), in_specs=[main] * 6 + [halo] * nh + [ANY] * ns,
        out_specs=[main] * 3 + [halo_spec] * nh + [ANY] * ns,
        out_shape=[shape] * 3 + [halo_shape] * nh + (_scattered_shapes(scatter[1]) if ns else []),
        scratch_shapes=[pltpu.VMEM((BAND, B_WIDTH), F32)] * 2 + (_scatter_sems(ns) if ns else []),
        compiler_params=_cparams(*(["arbitrary"] * 2 if ns else ["parallel"] * 2)), name=name)(
            q, k, v, do, lse, delta, *([k, v] if nh else []), *(scatter[0] if ns else []))
    return res[0], res[1], res[2], (tuple(res[3:3 + nh]) if nh else None), list(res[3 + nh:])


def _attn_combine(outs, lses, gb, mixed, name, gather=None):
    s = mixed.shape[0]
    npat = len(DILATIONS)
    w = B_WIDTH
    ng = 0 if gather is None else len(gather)

    def body(*refs):
        o_refs, l_refs = refs[:npat], refs[npat:2 * npat]
        g_ref = refs[2 * npat]
        ob_ref = refs[2 * npat + 2 + ng]
        lse_refs = refs[2 * npat + 3 + ng:3 * npat + 3 + ng]
        mb_ref = refs[3 * npat + 3 + ng]
        stage = refs[3 * npat + 4 + 2 * ng]
        if ng:
            start, relay, finish = _gather_steps(refs[2 * npat + 2:2 * npat + 2 + ng],
                                                 refs[3 * npat + 4 + ng:3 * npat + 4 + 2 * ng],
                                                 *refs[3 * npat + 5 + 2 * ng:])
            first, last = _grid_edges((s // TR,))
            pl.when(first)(start)
        os_ = [o_refs[0][...].astype(F32)] + [_load_classes(r, stage, d) for r, d in zip(o_refs[1:], CLASS_DILS)]
        ls = [l_refs[0][...]] + [_load_classes(r, stage, d) for r, d in zip(l_refs[1:], CLASS_DILS)]
        mx = functools.reduce(jnp.maximum, ls)
        ws = [jnp.exp(l - mx) for l in ls]
        tot = functools.reduce(lambda a, b: a + b, ws)
        ob = functools.reduce(lambda a, b: a + b, [wt / tot * o for wt, o in zip(ws, os_)])
        ob_ref[...] = ob
        lse = mx + jnp.log(tot)
        _stage_put(stage, lse)
        lse_refs[0][...] = lse
        for ref, d in zip(lse_refs[1:], CLASS_DILS):
            _store_classes(stage, ref, d)
        mb_ref[...] = (ob * _rsq_mean(ob) * g_ref[...]).astype(BF16)

        if ng:
            @pl.when(last)
            def _():
                relay()
                finish()

    lay_specs = [_row_spec(w)] + [_class_spec(d) for d in CLASS_DILS]
    res = pl.pallas_call(
        body, grid=(s // TR,), in_specs=lay_specs * 2 + [_vec_spec(w), ANY] + [ANY] * ng,
        out_specs=[_row_spec(w)] + lay_specs + [_row_spec(w, 1)] + [ANY] * ng,
        out_shape=[jax.ShapeDtypeStruct((s, w), F32), jax.ShapeDtypeStruct((s, w), F32)]
        + [_class_shape(s, d, F32) for d in CLASS_DILS] + [jax.ShapeDtypeStruct(mixed.shape, mixed.dtype)]
        + _gathered_shapes(gather or []),
        scratch_shapes=[STAGE] + (_gather_sems(ng) if ng else []), input_output_aliases={2 * npat + 1: npat + 1},
        compiler_params=_cparams("arbitrary" if ng else "parallel"), name=name)(*outs, *lses, gb, mixed,
                                                                              *(gather or []))
    return res[0], dict(zip(DILATIONS, res[1:npat + 1])), res[npat + 1], list(res[npat + 2:])


def _attn_bwd_prep(dmixed, ob, gb, name):
    s = ob.shape[0]
    w = B_WIDTH
    nlay = len(DILATIONS)

    def body(dm_ref, ob_ref, g_ref, *rest):
        do_refs, dl_refs = rest[:nlay], rest[nlay:2 * nlay]
        dg_ref, stage = rest[2 * nlay:]
        _acc_init([dg_ref])
        ob = ob_ref[...]
        dob, dgt = _rms_bwd(ob, _rsq_mean(ob), g_ref[...], dm_ref[...])
        dg_ref[...] += _colsum(dgt)
        _stage_put(stage, dob)
        do_refs[0][...] = dob.astype(BF16)
        for ref, d in zip(do_refs[1:], CLASS_DILS):
            _store_classes(stage, ref, d)
        lo, hi = _head_masks()
        t = dob * ob
        for b, sl in enumerate(_lane_blocks(w)):
            tb = t[:, sl]
            s0 = jnp.sum(jnp.where(lo, tb, 0.0), axis=1, keepdims=True)
            s1 = jnp.sum(jnp.where(hi, tb, 0.0), axis=1, keepdims=True)
            stage[b] = jnp.where(lo, s0, s1)
        dl_refs[0][...] = _stage_get(stage)
        for ref, d in zip(dl_refs[1:], CLASS_DILS):
            _store_classes(stage, ref, d)

    lay_specs = [_row_spec(w)] + [_class_spec(d) for d in CLASS_DILS]
    shapes = lambda dt: [jax.ShapeDtypeStruct((s, w), dt)] + [_class_shape(s, d, dt) for d in CLASS_DILS]
    res = pl.pallas_call(
        body, grid=(s // TR,), in_specs=[_row_spec(w, 1), _row_spec(w), _vec_spec(w)],
        out_specs=lay_specs * 2 + [_vec_spec(w)],
        out_shape=shapes(BF16) + shapes(F32) + [jax.ShapeDtypeStruct((1, w), F32)],
        scratch_shapes=[STAGE],
        compiler_params=_cparams("arbitrary"), name=name)(dmixed, ob, gb)
    return dict(zip(DILATIONS, res[:nlay])), dict(zip(DILATIONS, res[nlay:2 * nlay])), res[2 * nlay]


def _rope_bwd(dqs, dks, dvs, halos, tabs, dproj, name):
    s = dproj.shape[0]
    half = ROT_DIM // 2
    scale = HEAD_DIM ** -0.5
    npat = len(DILATIONS)
    w = B_WIDTH
    nseg = halos[0].shape[0]
    per = s // nseg // TR

    def body(*refs):
        groups = [refs[g * npat:(g + 1) * npat] for g in range(3)]
        halo_refs = (None,) + tuple(refs[3 * npat:3 * npat + 2])
        c_ref, s1_ref, s2_ref, _, o_ref, stage = refs[3 * npat + 2:]
        i = pl.program_id(0)
        at_edge = ((i + 1) % per == 0) & ((i + 1) // per < nseg)

        def total(rs, halo_ref=None):
            acc = rs[0][...].astype(F32)
            if halo_ref is not None:
                edge = jnp.concatenate([jnp.zeros((TR - BAND, w), F32), halo_ref[...]], axis=0)
                acc = acc + jnp.where(at_edge, edge, 0.0)
            for ref, d in zip(rs[1:], CLASS_DILS):
                acc = acc + _load_classes(ref, stage, d)
            return acc

        def unrope(g):
            c, s1, s2 = c_ref[...], s1_ref[...], s2_ref[...]
            for sl in _lane_blocks(w):
                gb = g[:, sl]
                o = gb * c + pltpu.roll(gb * s1, half, 1) + pltpu.roll(gb * s2, LANES - half, 1)
                o_ref[:, sl] = o.astype(BF16)

        which = pl.program_id(1)

        @pl.when(which == 0)
        def _():
            unrope(total(groups[0]) * scale)

        @pl.when(which == 1)
        def _():
            unrope(total(groups[1], halo_refs[1]))

        @pl.when(which == 2)
        def _():
            o_ref[...] = total(groups[2], halo_refs[2]).astype(BF16)

    tab = pl.BlockSpec((TR, LANES), lambda i, j: (i, 0))
    nat = pl.BlockSpec((TR, w), lambda i, j: (i, 0))
    lay_specs = [nat] + [_class_spec(d) for d in CLASS_DILS]
    edge_spec = pl.BlockSpec((None, BAND, w), lambda i, j: (jnp.minimum((i + 1) // per, nseg - 1), 0, 0))
    first_col = 2 * A_WIDTH // w
    return pl.pallas_call(
        body, grid=(s // TR, 3), in_specs=lay_specs * 3 + [edge_spec] * 2 + [tab] * 3 + [ANY],
        out_specs=pl.BlockSpec((TR, w), lambda i, j: (i, first_col + j)),
        out_shape=jax.ShapeDtypeStruct(dproj.shape, dproj.dtype), scratch_shapes=[STAGE],
        input_output_aliases={3 * npat + 5: 0},
        compiler_params=_cparams("parallel", "arbitrary"), name=name)(*dqs, *dks, *dvs, *halos, *tabs, dproj)


TK = 512
HALO = 16
FFN_ROWS = 256
FFN_CHUNKS = tuple(slice(r, r + FFN_ROWS) for r in range(0, TM, FFN_ROWS))


def _row_of(v, r):
    rows = lax.broadcasted_iota(jnp.int32, (v.shape[0], 1), 0)
    return jnp.sum(jnp.where(rows == r, v, 0.0), axis=0, keepdims=True)


def _taps_before(x, halo):
    row = lax.broadcasted_iota(jnp.int32, (x.shape[0], 1), 0)
    m1 = jnp.where(row == 0, _row_of(halo, HALO - 1), pltpu.roll(x, 1, 0))
    m2 = jnp.where(row == 0, _row_of(halo, HALO - 2), jnp.where(row == 1, _row_of(halo, HALO - 1), pltpu.roll(x, 2, 0)))
    return m2, m1, x


def _taps_after(x, halo):
    rows = x.shape[0]
    row = lax.broadcasted_iota(jnp.int32, (rows, 1), 0)
    p1 = jnp.where(row == rows - 1, _row_of(halo, 0), pltpu.roll(x, rows - 1, 0))
    p2 = jnp.where(row == rows - 2, _row_of(halo, 0), jnp.where(row == rows - 1, _row_of(halo, 1), pltpu.roll(x, rows - 2, 0)))
    return p1, p2


def _conv_value(taps, cw_ref, cb_ref, h):
    return cb_ref[h] + cw_ref[h, 0:1, :] * taps[0] + cw_ref[h, 1:2, :] * taps[1] + cw_ref[h, 2:3, :] * taps[2]


def _ffn_weight_specs(ncol):
    per_up = (2 * D_FF // N_CHIPS) // TK
    per_dn = (D_FF // N_CHIPS) // TK
    wg = pl.BlockSpec((None, None, D_MODEL, TK), lambda i, j: (j // per_up, 0, 0, j % per_up))
    wv = pl.BlockSpec((None, None, D_MODEL, TK), lambda i, j: ((j + ncol) // per_up, 0, 0, (j + ncol) % per_up))
    wd = pl.BlockSpec((None, None, TK, D_MODEL), lambda i, j: (j // per_dn, 0, j % per_dn, 0))
    cw = pl.BlockSpec((2, 3, TK), lambda i, j: (0, 0, j))
    cb = pl.BlockSpec((2, 1, TK), lambda i, j: (0, 0, j))
    return wg, wv, wd, cw, cb


def _ffn_forward(h2, w_up, w_down, cw3, cb3, name, gather=None, post=None):
    s = h2.shape[0]
    nm, ncol = s // TM, D_FF // TK
    ng = 0 if gather is None else len(gather)
    npost = 0 if post is None else 3
    nout = 4 + (2 if post else 0)

    def body(*refs):
        h_ref, wg_ref, wv_ref, wd_ref, cw_ref, cb_ref = refs[:6]
        post_in = refs[6:6 + npost]
        g_in = refs[6 + npost:6 + npost + ng]
        outs = refs[6 + npost + ng:6 + npost + ng + nout]
        y_ref, up_ref, cv_ref, f_ref = outs[:4]
        g_out = refs[6 + npost + ng + nout:6 + npost + 2 * ng + nout]
        carry = refs[6 + npost + 2 * ng + nout]
        i, j = pl.program_id(0), pl.program_id(1)
        if ng:
            start, relay, finish = _gather_steps(g_in, g_out, *refs[7 + npost + 2 * ng + nout:])
            pl.when((i == 0) & (j == 0))(start)
            pl.when((i == nm - 1) & (j == ncol // 2))(relay)

        @pl.when((i == 0) & (j == 0))
        def _():
            carry[...] = jnp.zeros_like(carry)

        @pl.when(j == 0)
        def _():
            f_ref[...] = jnp.zeros_like(f_ref)

        ups = []
        for rs in FFN_CHUNKS:
            hc = h_ref[rs, :]
            ups.append([_dot(hc, w_ref[...], NN).astype(BF16) for w_ref in (wg_ref, wv_ref)])
            for hh in range(2):
                up_ref[hh, rs, :] = ups[-1][hh]
        before = [carry[j, hh] for hh in range(2)]
        for rs, up in zip(FFN_CHUNKS, ups):
            conv = []
            for hh in range(2):
                x = up[hh].astype(F32)
                conv.append(_conv_value(_taps_before(x, before[hh]), cw_ref, cb_ref, hh))
                cv_ref[hh, rs, :] = conv[hh].astype(BF16)
                before[hh] = x[x.shape[0] - HALO:, :]
            y = (_gelu_tanh(conv[0])[0] * conv[1]).astype(BF16)
            y_ref[rs, :] = y
            f_ref[rs, :] += _dot(y, wd_ref[...], NN)
        for hh in range(2):
            carry[j, hh] = before[hh]

        @pl.when(j == ncol - 1)
        def _():
            if post:
                f = f_ref[...]
                x1_ref, gp_ref, gn_ref = post_in
                x2 = x1_ref[...] + f * _rsq_mean(f) * gp_ref[...]
                outs[4][...] = x2
                outs[5][...] = (x2 * _rsq_mean(x2) * gn_ref[...]).astype(BF16)

        if ng:
            pl.when((i == nm - 1) & (j == ncol - 1))(finish)

    wg, wv, wd, cw, cb = _ffn_weight_specs(ncol)
    row = pl.BlockSpec((TM, D_MODEL), lambda i, j: (i, 0))
    vec = pl.BlockSpec((1, D_MODEL), lambda i, j: (0, 0))
    res = pl.pallas_call(
        body, grid=(nm, ncol),
        in_specs=[row, wg, wv, wd, cw, cb] + ([row, vec, vec] if post else []) + [ANY] * ng,
        out_specs=[pl.BlockSpec((TM, TK), lambda i, j: (i, j)), pl.BlockSpec((2, TM, TK), lambda i, j: (0, i, j)),
                   pl.BlockSpec((2, TM, TK), lambda i, j: (0, i, j)), row] + ([row, row] if post else [])
        + [ANY] * ng,
        out_shape=[jax.ShapeDtypeStruct((s, D_FF), BF16), jax.ShapeDtypeStruct((2, s, D_FF), BF16),
                   jax.ShapeDtypeStruct((2, s, D_FF), BF16), jax.ShapeDtypeStruct((s, D_MODEL), F32)]
        + ([jax.ShapeDtypeStruct((s, D_MODEL), F32), jax.ShapeDtypeStruct((s, D_MODEL), BF16)] if post else [])
        + _gathered_shapes(gather or []),
        scratch_shapes=[pltpu.VMEM((ncol, 2, HALO, TK), F32)] + (_gather_sems(ng) if ng else []),
        compiler_params=_cparams("arbitrary", "arbitrary"), name=name)(h2, w_up, w_up, w_down, cw3, cb3,
                                                                      *(post or []), *(gather or []))
    return res[:nout], list(res[nout:])


def _ffn_backward(df, w_up, w_down, up3, cv3, cw3, name, scatter=None):
    s = df.shape[0]
    nm, ncol = s // TM, D_FF // TK
    ns = 0 if scatter is None else len(scatter[0])

    def body(*refs):
        df_ref, wg_ref, wv_ref, wd_ref, cw_ref, up_ref, cv_ref = refs[:7]
        s_in = refs[7:7 + ns]
        dup_ref, dh_ref, sums_ref = refs[7 + ns:10 + ns]
        s_out = refs[10 + ns:10 + 2 * ns]
        carry = refs[10 + 2 * ns]
        i, j = pl.program_id(0), pl.program_id(1)
        if ns:
            start, finish = _scatter_steps(s_in, s_out, *refs[11 + 2 * ns:], scatter[1])
            pl.when((i == 0) & (j == 0))(start)

        @pl.when((i == 0) & (j == 0))
        def _():
            carry[...] = jnp.zeros_like(carry)
            sums_ref[...] = jnp.zeros_like(sums_ref)

        @pl.when(j == 0)
        def _():
            dh_ref[...] = jnp.zeros_like(dh_ref)

        chunks = FFN_CHUNKS[::-1]
        dys = [_dot(df_ref[rs, :], wd_ref[...], NT) for rs in chunks]
        row = lax.broadcasted_iota(jnp.int32, (8, 1), 0)
        after = [carry[j, hh] for hh in range(2)]
        upd = [jnp.zeros((8, TK), F32) for _ in range(2)]
        for rs, dy in zip(chunks, dys):
            act, grad = _gelu_tanh(cv_ref[0, rs, :].astype(F32))
            dcs = (dy * cv_ref[1, rs, :].astype(F32) * grad, dy * act)
            part = dh_ref[rs, :]
            for hh, w_ref in ((0, wg_ref), (1, wv_ref)):
                dc = dcs[hh]
                x = up_ref[hh, rs, :].astype(F32)
                after1, after2 = _taps_after(dc, after[hh])
                for ridx, sm in enumerate((_colsum(after2 * x), _colsum(after1 * x), _colsum(dc * x), _colsum(dc))):
                    upd[hh] = upd[hh] + jnp.where(row == ridx, sm, 0.0)
                dup = (cw_ref[hh, 2:3, :] * dc + cw_ref[hh, 1:2, :] * after1 + cw_ref[hh, 0:1, :] * after2).astype(BF16)
                after[hh] = dc[:HALO, :]
                dup_ref[hh, rs, :] = dup
                part = part + _dot(dup, w_ref[...], NT)
            dh_ref[rs, :] = part
        for hh in range(2):
            sums_ref[j, hh] += upd[hh]
            carry[j, hh] = after[hh]

        if ns:
            pl.when((i == nm - 1) & (j == ncol - 1))(finish)

    wg, wv, wd, cw, _ = _ffn_weight_specs(ncol)
    rev = lambda i: nm - 1 - i
    res = pl.pallas_call(
        body, grid=(nm, ncol),
        in_specs=[pl.BlockSpec((TM, D_MODEL), lambda i, j: (rev(i), 0)), wg, wv, wd, cw,
                  pl.BlockSpec((2, TM, TK), lambda i, j: (0, rev(i), j)),
                  pl.BlockSpec((2, TM, TK), lambda i, j: (0, rev(i), j))] + [ANY] * ns,
        out_specs=[pl.BlockSpec((2, TM, TK), lambda i, j: (0, rev(i), j)),
                   pl.BlockSpec((TM, D_MODEL), lambda i, j: (rev(i), 0)),
                   pl.BlockSpec((ncol, 2, 8, TK), lambda i, j: (0, 0, 0, 0))] + [ANY] * ns,
        out_shape=[jax.ShapeDtypeStruct((2, s, D_FF), BF16), jax.ShapeDtypeStruct((s, D_MODEL), F32),
                   jax.ShapeDtypeStruct((ncol, 2, 8, TK), F32)] + (_scattered_shapes(scatter[1]) if ns else []),
        scratch_shapes=[pltpu.VMEM((ncol, 2, HALO, TK), F32)] + (_scatter_sems(ns) if ns else []),
        compiler_params=_cparams("arbitrary", "arbitrary"), name=name)(df, w_up, w_up, w_down, cw3, up3, cv3,
                                                                      *(scatter[0] if ns else []))
    return res[:3], list(res[3:])


def _wspec(rows, cols, index_map):
    return pl.BlockSpec((None, None, rows, cols), index_map)


def _layer_forward(l, x0, h1, p, wg, tabs, gather=None, late=None, g_next=None):
    s = x0.shape[0]
    nm = s // TMM
    tag = f"_l{l}"
    riders = dict.fromkeys(DILATIONS)
    proj_rider = rope_rider = combine_rider = None
    if late is not None:
        cols = lambda t, parts: [t[:, i * t.shape[1] // parts:(i + 1) * t.shape[1] // parts] for i in range(parts)]
        (down_a, down_b), up_q = cols(late["w_down"], 2), cols(late["w_up"], 4)
        proj_rider, rope_rider, combine_rider = [late["w_out"], down_a], [up_q[2]], [up_q[3]]
        riders = dict(zip(DILATIONS, ([down_b], [up_q[0]], [up_q[1]])))
    proj = _matmul(
        h1, wg["w_in"], grid=(nm, N_CHIPS), a_spec=pl.BlockSpec((TMM, D_MODEL), lambda i, j: (i, 0)),
        b_spec=_wspec(D_MODEL, IN_COLS // N_CHIPS, lambda i, j: (j, 0, 0, 0)),
        o_spec=pl.BlockSpec((TMM, IN_COLS // N_CHIPS), lambda i, j: (i, j)), o_shape=(s, IN_COLS), o_dtype=BF16,
        dims=NN, nk=1, kaxis=None, acc_shape=None, name="proj" + tag, gather=proj_rider)
    if late is not None:
        proj, (w_out_all4, down_a) = proj
    ma = _mixer_a_fwd(proj, p["v_norm_g"], p["v_norm_b"], p["w_spatial"], p["bs_full"], p["out_norm_a"],
                      "mixer_a_fwd" + tag)
    q, k, v, rope_landed = _rope_fwd(proj, tabs, "rope_fwd" + tag, rope_rider)
    outs, lses, landed = zip(*[
        _attn_fwd(_as_classes(q[d]), _as_classes(k[d]), _as_classes(v[d]), f"attn_fwd_d{d}" + tag, riders[d])
        for d in DILATIONS])
    outs = [o.reshape(s, B_WIDTH) if d == 1 else o for o, d in zip(outs, DILATIONS)]
    lses = [t.reshape(s, B_WIDTH) if d == 1 else t for t, d in zip(lses, DILATIONS)]
    ob, lse, mixed, combine_landed = _attn_combine(outs, lses, p["out_norm_b"], ma, "attn_combine" + tag,
                                                   combine_rider)
    if late is not None:
        wg = dict(wg, w_out=w_out_all4, w_down=jnp.concatenate([down_a, landed[0][0]], axis=-1),
                  w_up=jnp.concatenate([landed[1][0], landed[2][0], rope_landed[0], combine_landed[0]], axis=-1))
    (y1, x1, h2), next_in = _mix_out_norm(mixed, wg["w_out"], x0, p["post_mix_norm"], p["pre_ffn_norm"],
                                          "mix_out" + tag, [gather["w_in"]] if gather else None)
    post = None if g_next is None else (x1, p["post_ffn_norm"], g_next)
    (y, up3, cv3, f, *after), next_ffn = _ffn_forward(h2, wg["w_up"], wg["w_down"], p["cw3"], p["cb3"], "ffn_fwd" + tag,
                                                      [gather[n] for n in ("w_up", "w_down", "w_out")] if gather else None,
                                                      post)
    gathered = dict(zip(("w_up", "w_down", "w_out"), next_ffn), w_in=next_in[0]) if gather else None
    saved = dict(x0=x0, h1=h1, proj=proj, q=q, k=k, v=v, ob=ob, lse=lse, mixed=mixed, y1=y1, x1=x1, h2=h2,
                 up3=up3, cv3=cv3, y=y, f=f)
    if after:
        saved.update(x2=after[0], h_next=after[1])
    return saved, gathered, wg


def _layer_backward(l, dx2, df, sv, p, wg, tabs, pos, scatter=None, hide=False):
    s = dx2.shape[0]
    nm = s // TMM
    tag = f"_l{l}"
    g = {}
    (dup3, dh2, conv_sums), scattered = _ffn_backward(df, wg["w_up"], wg["w_down"], sv["up3"], sv["cv3"], p["cw3"],
                                                      "ffn_bwd" + tag, scatter)
    sums = conv_sums.transpose(1, 2, 0, 3).reshape(2, 8, D_FF)
    g["conv_w"] = jnp.concatenate([sums[0, :3], sums[1, :3]], axis=1)
    g["conv_b"] = jnp.concatenate([sums[0, 3:4], sums[1, 3:4]], axis=1)
    tn = 1024
    done = {}
    gw_down = _matmul(
        sv["y"], df, grid=(D_FF // tn,), a_spec=pl.BlockSpec((s, tn), lambda k: (0, k)),
        b_spec=pl.BlockSpec((s, D_MODEL), lambda k: (0, 0)),
        o_spec=pl.BlockSpec((2, tn, D_MODEL // 2), lambda k: (0, k, 0)),
        o_shape=(2, D_FF, D_MODEL // 2), o_dtype=BF16,
        dims=TN, nk=1, kaxis=None, acc_shape=None, name="w_down_grad" + tag, halves=True)
    pair_sum = lambda n, grad, recv: _pair_sum({n: grad}, recv, pos, (n,), f"pair_sum_l{l}")
    gw_up, received = _matmul(
        sv["h2"], dup3, grid=(2 * D_FF // tn,), a_spec=pl.BlockSpec((s, D_MODEL), lambda n: (0, 0)),
        b_spec=pl.BlockSpec((None, s, tn), lambda n: (n // (D_FF // tn), 0, n % (D_FF // tn))),
        o_spec=pl.BlockSpec((None, D_MODEL, tn), lambda n: (n // 2, 0, n % 2)),
        o_shape=(N_CHIPS, D_MODEL, 2 * D_FF // N_CHIPS), o_dtype=BF16,
        dims=TN, nk=1, kaxis=None, acc_shape=None, name="w_up_grad" + tag, scatter=([gw_down], ("x:w_down",)))
    down_sums = pair_sum("w_down", gw_down, received)
    dx1, dy1, g["pre_ffn_norm"], g["post_mix_norm"] = _norm_bwd_mid(
        dx2, dh2, sv["x1"], p["pre_ffn_norm"], sv["y1"], p["post_mix_norm"], "norm_bwd_mid" + tag)
    w_out_all = pl.BlockSpec((N_CHIPS, None, D_MODEL // N_CHIPS, D_MODEL), lambda i: (0, 0, 0, 0))
    dmixed, received = _matmul(
        dy1, wg["w_out"], grid=(nm,), a_spec=pl.BlockSpec((TMM, D_MODEL), lambda i: (i, 0)), b_spec=w_out_all,
        o_spec=pl.BlockSpec((TMM, D_MODEL), lambda i: (i, 0)), o_shape=(s, D_MODEL), o_dtype=F32,
        dims=NT, nk=1, kaxis=None, acc_shape=None, name="mix_out_bwd" + tag, b_2d=(D_MODEL, D_MODEL),
        scatter=([gw_up], ("x:w_up",)))
    up_sums = pair_sum("w_up", gw_up, received)
    gw_out = _matmul(
        sv["mixed"], dy1, grid=(1,), a_spec=pl.BlockSpec((s, D_MODEL), lambda m: (0, 0)),
        b_spec=pl.BlockSpec((s, D_MODEL), lambda m: (0, 0)),
        o_spec=pl.BlockSpec((2, D_MODEL, D_MODEL // 2), lambda m: (0, 0, 0)),
        o_shape=(2, D_MODEL, D_MODEL // 2), o_dtype=BF16,
        dims=TN, nk=1, kaxis=None, acc_shape=None, name="w_out_grad" + tag, halves=True)
    dpa, g["out_norm_a"], g["v_norm_g"], g["v_norm_b"], dbs, g["w_spatial"], received = _mixer_a_bwd(
        sv["proj"], dmixed, p["v_norm_g"], p["v_norm_b"], p["w_spatial"], p["bs_full"], p["out_norm_a"],
        "mixer_a_bwd" + tag, ([gw_out], ("x:w_out",)))
    out_sums = pair_sum("w_out", gw_out, received)
    g["b_spatial"] = dbs[:, ::GROUP_DIM].T
    dob, delta, g["out_norm_b"] = _attn_bwd_prep(dmixed, sv["ob"], p["out_norm_b"], "attn_bwd_prep" + tag)
    riders = dict(zip(DILATIONS, ((down_sums, ("w_down",)), (up_sums, ("w_up:0",)), (up_sums, ("w_up:1",))))) if hide else {}
    dqs, dks, dvs, edges, received = zip(*[
        _attn_bwd(*(_as_classes(t[d]) for t in (sv["q"], sv["k"], sv["v"], dob, sv["lse"], delta)),
                  f"attn_bwd_d{d}" + tag, riders.get(d))
        for d in DILATIONS])
    if hide:
        done[("w_down",)] = (down_sums, received[0])
        done[("w_up",)] = (up_sums, [jnp.concatenate([received[1][0], received[2][0]], axis=-1)])
    nat = lambda ts: [t.reshape(s, B_WIDTH) if d == 1 else t for t, d in zip(ts, DILATIONS)]
    halos = [t[0] for t in edges[0]]
    dproj = _rope_bwd(nat(dqs), nat(dks), nat(dvs), halos, tabs, dpa, "rope_bwd" + tag)
    wcol = IN_COLS // N_CHIPS
    gw_in = _matmul(
        sv["h1"], dproj, grid=(N_CHIPS,), a_spec=pl.BlockSpec((s, D_MODEL), lambda n: (0, 0)),
        b_spec=pl.BlockSpec((s, wcol), lambda n: (0, n)),
        o_spec=pl.BlockSpec((None, D_MODEL, wcol), lambda n: (n, 0, 0)),
        o_shape=(N_CHIPS, D_MODEL, wcol), o_dtype=BF16,
        dims=TN, nk=1, kaxis=None, acc_shape=None, name="w_in_grad" + tag,
        scatter=(out_sums, ("w_out",)) if hide else None)
    if hide:
        gw_in, received = gw_in
        done[("w_out",)] = (out_sums, received)
        in_sums = _chip_sums(l, dict(w_in=gw_in), pos, ("w_in",))
        dh1, received = _proj_bwd(dproj, wg["w_in"], "proj_bwd" + tag, (in_sums, ("w_in",)))
        done[("w_in",)] = (in_sums, received)
        return dx1, dh1, {}, g, scattered, done
    dh1, received = _proj_bwd(dproj, wg["w_in"], "proj_bwd" + tag, ([gw_in], ("x:w_in",)))
    sums = dict(w_in=pair_sum("w_in", gw_in, received)[0], w_up=up_sums[0], w_out=out_sums[0], w_down=down_sums[0])
    return dx1, dh1, sums, g, scattered, done


SMALL = ("pre_mix_norm", "v_norm_g", "v_norm_b", "w_spatial", "b_spatial", "out_norm_a", "out_norm_b",
         "post_mix_norm", "pre_ffn_norm", "conv_b", "post_ffn_norm")
BIG = ("w_in", "w_out", "w_up", "w_down")
DEPTH = 2


def _layer_params(l, small, conv_w_full):
    p = {n: small[n][l].reshape(1, -1) for n in SMALL if n not in ("w_spatial", "b_spatial")}
    p["w_spatial"] = small["w_spatial"][l]
    p["bs_full"] = jnp.repeat(small["b_spatial"][l].T, GROUP_DIM, axis=1)
    p["cw3"] = conv_w_full[l].reshape(3, 2, D_FF).transpose(1, 0, 2)
    p["cb3"] = small["conv_b"][l].reshape(2, 1, D_FF)
    return p


def _mesh_pos():
    return lax.axis_index("x"), lax.axis_index("y"), lax.axis_index("c")


def _other_chips(x, y):
    return [(1 - x, y), (x, 1 - y), (1 - x, 1 - y)]


def _gathered_shapes(blocks):
    return [jax.ShapeDtypeStruct((N_CHIPS, 1) + a.shape, a.dtype) for a in blocks]


def _gather_sems(nw):
    n = 2 * nw * (N_CHIPS - 1) + nw
    return [pltpu.SemaphoreType.DMA((n,)), pltpu.SemaphoreType.DMA((n,))]


def _gather_steps(ins, outs, send, recv):
    nw, nrel = len(ins), N_CHIPS - 1
    x, y, c = _mesh_pos()
    mine, sibling, chips = 2 * x + y, (x, y, 1 - c), _other_chips(x, y)

    def copy(src, dst, slot, to):
        return pltpu.make_async_remote_copy(src_ref=src, dst_ref=dst, send_sem=send.at[slot],
                                            recv_sem=recv.at[slot], device_id=to, device_id_type=MESH)

    def half_rows(t, core):
        rows = ins[t].shape[0] // 2
        return pl.ds(pl.multiple_of(core * rows, rows), rows)

    def landing(t, chip, core):
        return outs[t].at[chip, 0, half_rows(t, core), :]

    slots = [(t, r, chip) for t in range(nw) for r, chip in enumerate(chips)]
    own = [copy(ins[t], outs[t].at[mine, 0], 2 * nw * nrel + t, sibling) for t in range(nw)]
    first = [copy(ins[t].at[half_rows(t, c), :], landing(t, mine, c), t * nrel + r, (px, py, c))
             for t, r, (px, py) in slots]
    relays = [copy(landing(t, 2 * px + py, c), landing(t, 2 * px + py, c), nw * nrel + t * nrel + r, sibling)
              for t, r, (px, py) in slots]

    def start():
        for cp in own + first:
            cp.start()

    def relay():
        for (t, r, (px, py)), cp in zip(slots, relays):
            copy(landing(t, 2 * px + py, c), landing(t, 2 * px + py, c), t * nrel + r, (px, py, c)).wait_recv()
            cp.start()

    def finish():
        for t, r, (px, py) in slots:
            passed = landing(t, 2 * px + py, 1 - c)
            copy(passed, passed, nw * nrel + t * nrel + r, sibling).wait_recv()
        for cp in first + relays:
            cp.wait_send()
        for cp in own:
            cp.wait()

    return start, relay, finish


HALF = 512

GRAD_GEOM = {"w_in": ("rows", D_MODEL, IN_COLS // N_CHIPS), "w_up": ("rows", D_MODEL, 2 * D_FF // N_CHIPS),
             "w_out": ("cols", D_MODEL, D_MODEL // N_CHIPS), "w_down": ("cols", D_FF, D_FF // N_CHIPS)}


def _exchange_shape(n):
    kind, a, b = GRAD_GEOM[n]
    return (N_CHIPS, HALF, b) if kind == "rows" else (a, HALF)


def _piece_shape(n):
    name, _, part = n.partition(":")
    kind, _, b = GRAD_GEOM[name]
    if part:
        assert kind == "rows"
        return (HALF, b // 2)
    return (HALF, b) if kind == "rows" else (b, HALF)


def _half_of(ref, n, core):
    if GRAD_GEOM[n][0] == "rows":
        return ref.at[:, pl.ds(pl.multiple_of(core * HALF, HALF), HALF), :]
    return ref.at[core]


def _piece_of(ref, n, chip):
    name, _, part = n.partition(":")
    kind, _, b = GRAD_GEOM[name]
    if part:
        return ref.at[chip, :, pl.ds(int(part) * (b // 2), b // 2)]
    return ref.at[chip] if kind == "rows" else ref.at[pl.ds(pl.multiple_of(chip * b, b), b), :]


def _pair_exchange(g, names, name):
    n = len(names)

    def body(*refs):
        send, recv = refs[2 * n:]
        x, y, c = _mesh_pos()
        o = 1 - c
        cps = [pltpu.make_async_remote_copy(src_ref=_half_of(refs[t], nm, o), dst_ref=refs[n + t], send_sem=send.at[t],
                                            recv_sem=recv.at[t], device_id=(x, y, o), device_id_type=MESH)
               for t, nm in enumerate(names)]
        for cp in cps:
            cp.start()
        for cp in cps:
            cp.wait()

    return pl.pallas_call(
        body, in_specs=[ANY] * n, out_specs=[ANY] * n,
        out_shape=[jax.ShapeDtypeStruct(_exchange_shape(nm), BF16) for nm in names],
        scratch_shapes=[pltpu.SemaphoreType.DMA((n,)), pltpu.SemaphoreType.DMA((n,))],
        name=name)(*[g[nm] for nm in names])


def _pair_sum(g, recv, pos, names, name_prefix):
    def add(a, b, grid, a_spec, b_spec, name):
        def body(pos_ref, a_ref, b_ref, o_ref):
            o_ref[...] = (a_ref[...].astype(F32) + b_ref[...].astype(F32)).astype(BF16)

        return pl.pallas_call(
            body, grid_spec=pltpu.PrefetchScalarGridSpec(
                num_scalar_prefetch=1, grid=grid, in_specs=[a_spec, b_spec], out_specs=b_spec),
            out_shape=jax.ShapeDtypeStruct(b.shape, BF16), compiler_params=_cparams("parallel"), name=name)(pos, a, b)

    out = []
    for nm, r in zip(names, recv):
        kind, rows, width = GRAD_GEOM[nm]
        if kind == "rows":
            out.append(add(g[nm], r, (N_CHIPS,), pl.BlockSpec((None, HALF, width), lambda j, pos: (j, pos[2], 0)),
                           pl.BlockSpec((None, HALF, width), lambda j, pos: (j, 0, 0)), f"{name_prefix}_{nm}"))
        else:
            out.append(add(g[nm], r, (rows // D_MODEL,), pl.BlockSpec((None, D_MODEL, HALF), lambda j, pos: (pos[2], j, 0)),
                           pl.BlockSpec((D_MODEL, HALF), lambda j, pos: (j, 0)), f"{name_prefix}_{nm}"))
    return out


def _scattered_shapes(names):
    return [jax.ShapeDtypeStruct(_exchange_shape(nm[2:]) if nm.startswith("x:") else (N_CHIPS - 1,) + _piece_shape(nm),
                                 BF16) for nm in names]


def _scatter_sems(n):
    return [pltpu.SemaphoreType.DMA((n * (N_CHIPS - 1),)), pltpu.SemaphoreType.DMA((n * (N_CHIPS - 1),))]


def _scatter_steps(sums, outs, send, recv, names):
    nrel = N_CHIPS - 1
    x, y, c = _mesh_pos()
    cps = [pltpu.make_async_remote_copy(
        src_ref=_half_of(sums[t], nm[2:], 1 - c), dst_ref=outs[t], send_sem=send.at[t * nrel],
        recv_sem=recv.at[t * nrel], device_id=(x, y, 1 - c), device_id_type=MESH)
        for t, nm in enumerate(names) if nm.startswith("x:")]
    for r, (px, py) in enumerate(_other_chips(x, y)):
        for t, nm in enumerate(names):
            if nm.startswith("x:"):
                continue
            cps.append(pltpu.make_async_remote_copy(
                src_ref=_piece_of(sums[t], nm, 2 * px + py), dst_ref=outs[t].at[r], send_sem=send.at[t * nrel + r],
                recv_sem=recv.at[t * nrel + r], device_id=(px, py, c), device_id_type=MESH))

    def start():
        for cp in cps:
            cp.start()

    def finish():
        for cp in cps:
            cp.wait()

    return start, finish


def _chip_scatter(sums, names, name):
    n = len(names)

    def body(*refs):
        start, finish = _scatter_steps(refs[:n], refs[n:2 * n], *refs[2 * n:], names)
        start()
        finish()

    return pl.pallas_call(
        body, in_specs=[ANY] * n, out_specs=[ANY] * n, out_shape=_scattered_shapes(names),
        scratch_shapes=_scatter_sems(n), name=name)(*sums)


def _chip_sum(sums, recv, pos, names, name_prefix):
    def add(a, b, a_spec, shape, name):
        def body(pos_ref, a_ref, b_ref, o_ref):
            tot = a_ref[...].astype(F32)
            for r in range(N_CHIPS - 1):
                tot = tot + b_ref[r].astype(F32)
            o_ref[...] = tot

        return pl.pallas_call(
            body, grid_spec=pltpu.PrefetchScalarGridSpec(
                num_scalar_prefetch=1, grid=(1,), in_specs=[a_spec, pl.BlockSpec(b.shape, lambda i, pos: (0, 0, 0))],
                out_specs=pl.BlockSpec((None,) + shape, lambda i, pos: (pos[2], 0, 0))),
            out_shape=jax.ShapeDtypeStruct((2,) + shape, F32), compiler_params=_cparams("arbitrary"),
            name=name)(pos, a, b)

    chip = lambda pos: 2 * pos[0] + pos[1]
    out = []
    for nm, a, b in zip(names, sums, recv):
        shape = _piece_shape(nm)
        if GRAD_GEOM[nm][0] == "rows":
            spec = pl.BlockSpec((None,) + shape, lambda i, pos: (chip(pos), 0, 0))
        else:
            spec = pl.BlockSpec(shape, lambda i, pos: (chip(pos), 0))
        out.append(add(a, b, spec, shape, f"{name_prefix}_{nm}"))
    return out


def _pair_share(totals, name):
    n = len(totals)

    def body(*refs):
        ins, outs = refs[:n], refs[n:2 * n]
        send, recv = refs[2 * n:]
        x, y, c = _mesh_pos()
        o = 1 - c
        cps = [pltpu.make_async_remote_copy(src_ref=ins[t].at[c], dst_ref=outs[t].at[c], send_sem=send.at[t],
                                            recv_sem=recv.at[t], device_id=(x, y, o), device_id_type=MESH)
               for t in range(n)]
        for cp in cps:
            cp.start()
        for t in range(n):
            pltpu.make_async_remote_copy(src_ref=ins[t].at[o], dst_ref=outs[t].at[o], send_sem=send.at[t],
                                         recv_sem=recv.at[t], device_id=(x, y, o), device_id_type=MESH).wait_recv()
        for cp in cps:
            cp.wait_send()

    return pl.pallas_call(
        body, in_specs=[ANY] * n, out_specs=[ANY] * n,
        out_shape=[jax.ShapeDtypeStruct(t.shape, t.dtype) for t in totals],
        scratch_shapes=[pltpu.SemaphoreType.DMA((n,)), pltpu.SemaphoreType.DMA((n,))],
        input_output_aliases={t: t for t in range(n)}, name=name)(*totals)


def _chip_sums(l, g, pos, names):
    tag = f"l{l}_" + "_".join(names)
    recv = _pair_exchange(g, names, "pair_exchange_" + tag)
    return _pair_sum(g, recv, pos, names, "pair_sum_" + tag)


def _gradient_shards(l, sums, scattered, pos, names):
    tag = f"l{l}_" + "_".join(names)
    halves = _pair_share(_chip_sum(sums, scattered, pos, names, "chip_sum_" + tag), "pair_share_" + tag)
    out = {}
    for nm, t in zip(names, halves):
        rows, cols = _piece_shape(nm)
        out[nm] = t.reshape(2 * rows, cols) if GRAD_GEOM[nm][0] == "rows" else t
    return out


def _allreduce_small(packed, name):
    rows = packed.shape[0]
    half = rows // 2
    assert half % 8 == 0

    def body(x_ref, out_ref, sib, parts, done, landed, send_sems, recv_sems):
        x, y, c = _mesh_pos()
        sibling = (x, y, 1 - c)
        mine = pl.ds(pl.multiple_of(c * half, 8), half)
        other = pl.ds(pl.multiple_of((1 - c) * half, 8), half)

        def copy(k, src, dst, to):
            return pltpu.make_async_remote_copy(src_ref=src, dst_ref=dst, send_sem=send_sems.at[k],
                                                recv_sem=recv_sems.at[k], device_id=to, device_id_type=MESH)

        swap = copy(0, x_ref.at[other, :], sib, sibling)
        swap.start()
        swap.wait()
        parts[0] = x_ref[mine, :] + sib[...]
        sends = [copy(1 + j, parts.at[0], parts.at[1 + j], (*chip, c)) for j, chip in enumerate(_other_chips(x, y))]
        for cp in sends:
            cp.start()
        for cp in sends:
            cp.wait()
        tot = None
        for chip in range(N_CHIPS):
            rel = jnp.bitwise_xor(chip, 2 * x + y)
            slot = jnp.where(rel == 0, 0, jnp.where(rel == 2, 1, jnp.where(rel == 1, 2, 3)))
            term = parts[slot]
            tot = term if tot is None else tot + term
        out_ref[mine, :] = tot
        done[...] = tot
        back = copy(4, done, landed, sibling)
        back.start()
        back.wait()
        out_ref[other, :] = landed[...]

    vmem = pl.BlockSpec(memory_space=pltpu.VMEM)
    return pl.pallas_call(
        body, in_specs=[vmem], out_specs=vmem, out_shape=jax.ShapeDtypeStruct((rows, LANES), F32),
        scratch_shapes=[pltpu.VMEM((half, LANES), F32), pltpu.VMEM((N_CHIPS, half, LANES), F32),
                        pltpu.VMEM((half, LANES), F32), pltpu.VMEM((half, LANES), F32),
                        pltpu.SemaphoreType.DMA((5,)), pltpu.SemaphoreType.DMA((5,))],
        compiler_params=pltpu.CompilerParams(vmem_limit_bytes=VMEM_LIMIT_BYTES),
        name=name)(packed)


def _adamw_step(w, g, m, v):
    mn = ADAM_B1 * m + (1.0 - ADAM_B1) * g
    vn = ADAM_B2 * v + (1.0 - ADAM_B2) * (g * g)
    m_hat = mn / (1.0 - ADAM_B1 ** ADAM_STEP)
    v_hat = vn / (1.0 - ADAM_B2 ** ADAM_STEP)
    return -ADAM_LR * (m_hat / (jnp.sqrt(v_hat) + ADAM_EPS) + ADAM_WD * w), mn, vn


def _adamw(w, g, m, v, name):
    rows, cols = w.shape
    tr = 256 if rows % 256 == 0 else rows

    def body(w_ref, g_ref, m_ref, v_ref, d_ref, mo_ref, vo_ref):
        d_ref[...], mo_ref[...], vo_ref[...] = _adamw_step(w_ref[...], g_ref[...], m_ref[...], v_ref[...])

    spec = pl.BlockSpec((tr, cols), lambda i: (i, 0))
    return pl.pallas_call(
        body, grid=(rows // tr,), in_specs=[spec] * 4, out_specs=[spec] * 3,
        out_shape=[jax.ShapeDtypeStruct((rows, cols), F32)] * 3, compiler_params=_cparams("parallel"),
        name=name)(w, g, m, v)


def _adamw_layers(w, gs, m, v, name):
    depth, rows, cols = w.shape
    tr = 256
    nblk = rows // tr
    split = gs[0].ndim == 3

    def body(w_ref, m_ref, v_ref, *rest):
        g_refs, (g_out, d_ref, mo_ref, vo_ref) = rest[:depth], rest[depth:]
        layer = pl.program_id(0)
        load = (lambda r: jnp.concatenate([r[0], r[1]], axis=-1)) if split else (lambda r: r[...])
        gv = load(g_refs[0])
        for k in range(1, depth):
            gv = jnp.where(layer == k, load(g_refs[k]), gv)
        g_out[...] = gv
        d_ref[...], mo_ref[...], vo_ref[...] = _adamw_step(w_ref[...], gv, m_ref[...], v_ref[...])

    def g_spec(k):
        tile = lambda l, i: jnp.where(l == k, i, jnp.where(l < k, 0, nblk - 1))
        if split:
            return pl.BlockSpec((2, tr, cols // 2), lambda l, i: (0, tile(l, i), 0))
        return pl.BlockSpec((tr, cols), lambda l, i: (tile(l, i), 0))

    spec = pl.BlockSpec((None, tr, cols), lambda l, i: (l, i, 0))
    return pl.pallas_call(
        body, grid=(depth, nblk), in_specs=[spec] * 3 + [g_spec(k) for k in range(depth)], out_specs=[spec] * 4,
        out_shape=[jax.ShapeDtypeStruct(w.shape, F32)] * 4, compiler_params=_cparams("parallel", "parallel"),
        name=name)(w, m, v, *gs)


def _adamw_nd(w, g, m, v, name):
    cols = w.shape[-1] if w.shape[-1] % LANES == 0 else LANES
    outs = _adamw(*(t.reshape(-1, cols) for t in (w, g, m, v)), name)
    return tuple(t.reshape(w.shape) for t in outs)


def _pack(arrays):
    return jnp.concatenate([a.reshape(-1, LANES) for a in arrays], axis=0)


def _unpack(packed, shapes):
    out, row = [], 0
    for sh in shapes:
        n = math.prod(sh) // LANES
        out.append(packed[row:row + n].reshape(sh))
        row += n
    return out


WEIGHTS = ("pre_mix_norm", "w_in", "v_norm_g", "v_norm_b", "w_spatial", "b_spatial", "out_norm_a", "out_norm_b",
           "w_out", "post_mix_norm", "pre_ffn_norm", "w_up", "conv_w", "conv_b", "w_down", "post_ffn_norm")


def kernel(x, pre_mix_norm, w_in, v_norm_g, v_norm_b, w_spatial, b_spatial, out_norm_a, out_norm_b, w_out, post_mix_norm, pre_ffn_norm, w_up, conv_w, conv_b, w_down, post_ffn_norm, loss_target, m_pre_mix_norm, m_w_in, m_v_norm_g, m_v_norm_b, m_w_spatial, m_b_spatial, m_out_norm_a, m_out_norm_b, m_w_out, m_post_mix_norm, m_pre_ffn_norm, m_w_up, m_conv_w, m_conv_b, m_w_down, m_post_ffn_norm, v_pre_mix_norm, v_w_in, v_v_norm_g, v_v_norm_b, v_w_spatial, v_b_spatial, v_out_norm_a, v_out_norm_b, v_w_out, v_post_mix_norm, v_pre_ffn_norm, v_w_up, v_conv_w, v_conv_b, v_w_down, v_post_ffn_norm):
    w = dict(pre_mix_norm=pre_mix_norm, w_in=w_in, v_norm_g=v_norm_g, v_norm_b=v_norm_b, w_spatial=w_spatial,
             b_spatial=b_spatial, out_norm_a=out_norm_a, out_norm_b=out_norm_b, w_out=w_out,
             post_mix_norm=post_mix_norm, pre_ffn_norm=pre_ffn_norm, w_up=w_up, conv_w=conv_w, conv_b=conv_b,
             w_down=w_down, post_ffn_norm=post_ffn_norm)
    m = dict(pre_mix_norm=m_pre_mix_norm, w_in=m_w_in, v_norm_g=m_v_norm_g, v_norm_b=m_v_norm_b,
             w_spatial=m_w_spatial, b_spatial=m_b_spatial, out_norm_a=m_out_norm_a, out_norm_b=m_out_norm_b,
             w_out=m_w_out, post_mix_norm=m_post_mix_norm, pre_ffn_norm=m_pre_ffn_norm, w_up=m_w_up,
             conv_w=m_conv_w, conv_b=m_conv_b, w_down=m_w_down, post_ffn_norm=m_post_ffn_norm)
    v = dict(pre_mix_norm=v_pre_mix_norm, w_in=v_w_in, v_norm_g=v_v_norm_g, v_norm_b=v_v_norm_b,
             w_spatial=v_w_spatial, b_spatial=v_b_spatial, out_norm_a=v_out_norm_a, out_norm_b=v_out_norm_b,
             w_out=v_w_out, post_mix_norm=v_post_mix_norm, pre_ffn_norm=v_pre_ffn_norm, w_up=v_w_up,
             conv_w=v_conv_w, conv_b=v_conv_b, w_down=v_w_down, post_ffn_norm=v_post_ffn_norm)
    pos = jnp.stack([lax.axis_index("x"), lax.axis_index("y"), lax.axis_index("c")]).astype(jnp.int32)
    chip = 2 * lax.axis_index("x") + lax.axis_index("y")

    cw_cols = conv_w.shape[-1]
    blocks = [{n: w[n][l].astype(BF16) for n in BIG} for l in range(DEPTH)]
    small = {n: w[n] for n in SMALL}
    xs, target = x[0], loss_target[0]
    xin = xs
    h, (w_in0, cw_all) = _rms_cast(xin, small["pre_mix_norm"][0].reshape(1, -1), "pre_mix_l0",
                                   [blocks[0]["w_in"], conv_w.reshape(-1, LANES)])
    wg = dict(w_in=w_in0)
    conv_w_full = cw_all.reshape(N_CHIPS, DEPTH, 3, cw_cols).transpose(1, 2, 0, 3).reshape(DEPTH, 3, 2 * D_FF)

    tabs = _rope_tables(xs.shape[0])
    params = [_layer_params(l, small, conv_w_full) for l in range(DEPTH)]
    saved, wgs = [], []
    for l in range(DEPTH):
        sv, gathered, wg = _layer_forward(l, xin, h, params[l], wg, tabs,
                                          blocks[l + 1] if l + 1 < DEPTH else None,
                                          blocks[0] if l == 0 else None,
                                          params[l + 1]["pre_mix_norm"] if l + 1 < DEPTH else None)
        saved.append(sv)
        wgs.append(wg)
        if l + 1 < DEPTH:
            wg = gathered
            xin, h = sv["x2"], sv["h_next"]
    loss_part, dx, df, g_post = _loss_norm_bwd(saved[-1]["x1"], saved[-1]["f"], params[-1]["post_ffn_norm"], target,
                                               "loss")
    smalls, shards = [None] * DEPTH, [{} for _ in range(DEPTH)]
    pending = None
    for l in reversed(range(DEPTH)):
        dx1, dh1, big, smalls[l], scattered, done = _layer_backward(l, dx, df, saved[l], params[l], wgs[l], tabs, pos,
                                                                    pending[1:] if pending else None, hide=l == 0)
        smalls[l]["post_ffn_norm"] = g_post
        if l > 0:
            dx, smalls[l]["pre_mix_norm"], df, g_post = _norm_bwd_in_out(
                dx1, dh1, saved[l]["x0"], params[l]["pre_mix_norm"], saved[l - 1]["f"], params[l - 1]["post_ffn_norm"],
                f"norm_bwd_in_out_l{l}")
        else:
            dx, smalls[l]["pre_mix_norm"] = _norm_bwd_in(dx1, dh1, saved[l]["x0"], params[l]["pre_mix_norm"],
                                                         "norm_bwd_in_l0")
        if pending:
            shards[pending[0]].update(_gradient_shards(pending[0], pending[1], scattered, pos, pending[2]))
        if done:
            shards[l].update(_gradient_shards(
                l, [t for sums, _ in done.values() for t in sums], [t for _, received in done.values() for t in received],
                pos, tuple(n for names in done for n in names)))
        names = tuple(big)
        pending = (l, [big[n] for n in names], names) if names else None
    if pending:
        shards[pending[0]].update(_gradient_shards(
            pending[0], pending[1], _chip_scatter(pending[1], pending[2], f"chip_scatter_l{pending[0]}"), pos,
            pending[2]))

    small_shapes = [w[n].shape for n in SMALL]
    stacked = [jnp.stack([smalls[l][n].reshape(w[n].shape[1:]) for l in range(DEPTH)]) for n in SMALL]
    cw_grad = jnp.stack([smalls[l]["conv_w"] for l in range(DEPTH)])
    packed = _pack(stacked + [cw_grad, loss_part])
    total = _allreduce_small(packed, "allreduce_small")
    parts = _unpack(total, small_shapes + [cw_grad.shape, (8, LANES)])
    g_small = dict(zip(SMALL, parts[:len(SMALL)]))
    loss = parts[-1][0, 0]
    g_conv_w = lax.dynamic_slice(parts[-2], (0, 0, chip * cw_cols), conv_w.shape)

    grads = dict(g_small, conv_w=g_conv_w)

    dp, mp, vp = _adamw(_pack([w[n] for n in SMALL]), _pack([g_small[n] for n in SMALL]),
                        _pack([m[n] for n in SMALL]), _pack([v[n] for n in SMALL]), "adamw_small")
    delta = dict(zip(SMALL, _unpack(dp, small_shapes)))
    new_m = dict(zip(SMALL, _unpack(mp, small_shapes)))
    new_v = dict(zip(SMALL, _unpack(vp, small_shapes)))
    delta["conv_w"], new_m["conv_w"], new_v["conv_w"] = _adamw_nd(w["conv_w"], g_conv_w, m["conv_w"], v["conv_w"],
                                                                  "adamw_conv_w")
    for n in BIG:
        grads[n], delta[n], new_m[n], new_v[n] = _adamw_layers(w[n], [shards[l][n] for l in range(DEPTH)], m[n],
                                                               v[n], "adamw_" + n)

    return (loss, dx[None], *[grads[n] for n in WEIGHTS], *[delta[n] for n in WEIGHTS],
            *[new_m[n] for n in WEIGHTS], *[new_v[n] for n in WEIGHTS])
```
